```python
import math
import jax, jax.numpy as jnp
from jax import lax
import numpy as np

D_MODEL = 1024
BATCH = 16
SEQ = 2048
DEPTH = 1

HEAD_DIM = 64
HEADS_PER_GROUP = 4
ATTN_GROUPS = ((128, 1), (512, 4), (2048, 16))
N_ATTN_HEADS = HEADS_PER_GROUP * len(ATTN_GROUPS)
ATTN_OUT_W = HEADS_PER_GROUP * HEAD_DIM
ROPE_DIM = HEAD_DIM // 4
ROPE_THETA = 500000.0
BLOCK = 128
SSM_CH_PER_GROUP = 16
SSM_GROUPS = 32
SSM_W = SSM_CH_PER_GROUP * SSM_GROUPS
SSM_STATE = 64
D_FF = -(-8 * D_MODEL // (3 * 256)) * 256
QKV_W = 3 * N_ATTN_HEADS * HEAD_DIM
GATE_W = 2 * D_MODEL
IN_W = QKV_W + SSM_W + GATE_W
RMS_EPS = 1e-6
NEG_INF = -1e30

kernel_name = "hybrid_dilated_attn_s5_gated_block"


def rmsnorm(x, g):
    x32 = x.astype(jnp.float32)
    y = x32 * lax.rsqrt(jnp.mean(x32 * x32, axis=-1, keepdims=True) + RMS_EPS)
    return (y * g.astype(jnp.float32)).astype(x.dtype)


def partial_rope(t, pos):
    half = ROPE_DIM // 2
    inv = jnp.power(jnp.float32(ROPE_THETA), -jnp.arange(half, dtype=jnp.float32) * 2.0 / ROPE_DIM)
    ang = pos[:, None] * inv[None, :]
    cos = jnp.cos(ang)[None, :, None, :]
    sin = jnp.sin(ang)[None, :, None, :]
    tr = t[..., :ROPE_DIM].astype(jnp.float32)
    t1, t2 = tr[..., :half], tr[..., half:]
    rot = jnp.concatenate([t1 * cos - t2 * sin, t2 * cos + t1 * sin], axis=-1)
    return jnp.concatenate([rot.astype(t.dtype), t[..., ROPE_DIM:]], axis=-1)


def dilated_group_attention(q, k, v, window, dilation):
    B, S, H, E = q.shape
    span = window // dilation
    L = S // dilation
    nb = -(-L // BLOCK)
    Lp = nb * BLOCK

    def to_blocks(t):
        t = t.reshape(B, L, dilation, H, E).transpose(0, 2, 1, 3, 4)
        t = jnp.pad(t, ((0, 0), (0, 0), (0, Lp - L), (0, 0), (0, 0)))
        return t.reshape(B, dilation, nb, BLOCK, H, E)

    def with_prev(t):
        prev = jnp.pad(t[:, :, :-1], ((0, 0), (0, 0), (1, 0), (0, 0), (0, 0), (0, 0)))
        return jnp.concatenate([prev, t], axis=3)

    qb = to_blocks(q)
    kk = with_prev(to_blocks(k))
    vv = with_prev(to_blocks(v)).astype(jnp.float32)
    s = jnp.einsum('bdnqhe,bdnkhe->bdnhqk', qb, kk).astype(jnp.float32) * (HEAD_DIM ** -0.5)
    qi = jnp.arange(BLOCK)[:, None]
    ki = jnp.arange(2 * BLOCK)[None, :]
    dist = qi + BLOCK - ki
    band = (dist >= 0) & (dist <= span)
    blk = jnp.arange(nb)[:, None, None]
    valid = band[None] & ((blk > 0) | (ki >= BLOCK)[None])
    s = jnp.where(valid[None, None, :, None], s, NEG_INF)
    m = jnp.max(s, axis=-1, keepdims=True)
    p = jnp.exp(s - m)
    l = jnp.sum(p, axis=-1, keepdims=True)
    o = jnp.einsum('bdnhqk,bdnkhe->bdnhqe', p, vv) / l
    lse = (m + jnp.log(l))[..., 0]
    o = o.transpose(0, 1, 2, 4, 3, 5).reshape(B, dilation, Lp, H, E)[:, :, :L]
    o = o.transpose(0, 2, 1, 3, 4).reshape(B, S, H, E)
    lse = lse.transpose(0, 1, 2, 4, 3).reshape(B, dilation, Lp, H)[:, :, :L]
    lse = lse.transpose(0, 2, 1, 3).reshape(B, S, H)
    return o, lse


def s5_branch(u, a_re, a_im, log_dt, b_re, b_im, c_re, c_im, d_skip, w_glu):
    B, S, _ = u.shape
    u32 = u.astype(jnp.float32).reshape(B, S, SSM_GROUPS, SSM_CH_PER_GROUP)
    lr, li = a_re.astype(jnp.float32), a_im.astype(jnp.float32)
    dt = jnp.exp(log_dt.astype(jnp.float32))[:, None]
    mag = jnp.exp(lr * dt)
    ab_re, ab_im = mag * jnp.cos(li * dt), mag * jnp.sin(li * dt)
    den = lr * lr + li * li
    nr, ni = ab_re - 1.0, ab_im
    f_re = (nr * lr + ni * li) / den
    f_im = (ni * lr - nr * li) / den
    br, bi = b_re.astype(jnp.float32), b_im.astype(jnp.float32)
    bb_re = f_re[..., None] * br - f_im[..., None] * bi
    bb_im = f_re[..., None] * bi + f_im[..., None] * br
    bu_re = jnp.einsum('bsgc,gnc->bsgn', u32, bb_re)
    bu_im = jnp.einsum('bsgc,gnc->bsgn', u32, bb_im)
    a_r = jnp.broadcast_to(ab_re, bu_re.shape)
    a_i = jnp.broadcast_to(ab_im, bu_im.shape)

    def combine(e1, e2):
        a1r, a1i, b1r, b1i = e1
        a2r, a2i, b2r, b2i = e2
        return (a2r * a1r - a2i * a1i,
                a2r * a1i + a2i * a1r,
                a2r * b1r - a2i * b1i + b2r,
                a2r * b1i + a2i * b1r + b2i)

    _, _, xr, xi = lax.associative_scan(combine, (a_r, a_i, bu_re, bu_im), axis=1)
    y = (jnp.einsum('bsgn,gcn->bsgc', xr, c_re.astype(jnp.float32))
         - jnp.einsum('bsgn,gcn->bsgc', xi, c_im.astype(jnp.float32))
         + d_skip.astype(jnp.float32) * u32)
    y = jax.nn.gelu(y.reshape(B, S, SSM_W)).astype(u.dtype)
    z = y @ w_glu
    za, zb = z[..., :D_MODEL], z[..., D_MODEL:]
    return za * jax.nn.sigmoid(zb)


def _fwd_setup_inputs(seed: int = 0) -> dict:
    key = jax.random.key(seed)
    ks = jax.random.split(key, 24)
    f32 = jnp.float32
    nrm = lambda k, shape, scale: jax.random.normal(k, shape, f32) * scale
    x = jax.random.normal(ks[0], (BATCH, SEQ, D_MODEL), f32)
    norm_mix_g = 1.0 + nrm(ks[1], (DEPTH, D_MODEL), 0.05)
    w_in = nrm(ks[2], (DEPTH, D_MODEL, IN_W), D_MODEL ** -0.5)
    n_idx = jnp.arange(SSM_STATE, dtype=f32)
    ssm_a_re = -0.5 * jnp.exp(nrm(ks[3], (DEPTH, SSM_GROUPS, SSM_STATE), 0.05))
    ssm_a_im = math.pi * n_idx + nrm(ks[4], (DEPTH, SSM_GROUPS, SSM_STATE), 0.01)
    ssm_log_dt = jax.random.uniform(ks[5], (DEPTH, SSM_GROUPS), f32, math.log(1e-3), math.log(1e-1))
    ssm_b_re = nrm(ks[6], (DEPTH, SSM_GROUPS, SSM_STATE, SSM_CH_PER_GROUP), (2.0 * SSM_CH_PER_GROUP) ** -0.5)
    ssm_b_im = nrm(ks[7], (DEPTH, SSM_GROUPS, SSM_STATE, SSM_CH_PER_GROUP), (2.0 * SSM_CH_PER_GROUP) ** -0.5)
    ssm_c_re = nrm(ks[8], (DEPTH, SSM_GROUPS, SSM_CH_PER_GROUP, SSM_STATE), SSM_STATE ** -0.5)
    ssm_c_im = nrm(ks[9], (DEPTH, SSM_GROUPS, SSM_CH_PER_GROUP, SSM_STATE), SSM_STATE ** -0.5)
    ssm_d = nrm(ks[10], (DEPTH, SSM_GROUPS, SSM_CH_PER_GROUP), 1.0)
    w_glu = nrm(ks[11], (DEPTH, SSM_W, 2 * D_MODEL), SSM_W ** -0.5)
    w_attn_out = nrm(ks[12], (DEPTH, ATTN_OUT_W, D_MODEL), ATTN_OUT_W ** -0.5)
    w_out = nrm(ks[13], (DEPTH, D_MODEL, D_MODEL), D_MODEL ** -0.5)
    norm_ffn_g = 1.0 + nrm(ks[14], (DEPTH, D_MODEL), 0.05)
    w_ffn_gate = nrm(ks[15], (DEPTH, D_MODEL, D_FF), D_MODEL ** -0.5)
    w_ffn_up = nrm(ks[16], (DEPTH, D_MODEL, D_FF), D_MODEL ** -0.5)
    w_ffn_down = nrm(ks[17], (DEPTH, D_FF, D_MODEL), D_FF ** -0.5)
    norm_final_g = 1.0 + nrm(ks[18], (D_MODEL,), 0.05)
    return {"x": x, "norm_mix_g": norm_mix_g, "w_in": w_in,
            "ssm_a_re": ssm_a_re, "ssm_a_im": ssm_a_im, "ssm_log_dt": ssm_log_dt,
            "ssm_b_re": ssm_b_re, "ssm_b_im": ssm_b_im, "ssm_c_re": ssm_c_re,
            "ssm_c_im": ssm_c_im, "ssm_d": ssm_d, "w_glu": w_glu,
            "w_attn_out": w_attn_out, "w_out": w_out, "norm_ffn_g": norm_ffn_g,
            "w_ffn_gate": w_ffn_gate, "w_ffn_up": w_ffn_up, "w_ffn_down": w_ffn_down,
            "norm_final_g": norm_final_g}


def _fwd_reference(x, norm_mix_g, w_in, ssm_a_re, ssm_a_im, ssm_log_dt, ssm_b_re, ssm_b_im,
              ssm_c_re, ssm_c_im, ssm_d, w_glu, w_attn_out, w_out, norm_ffn_g,
              w_ffn_gate, w_ffn_up, w_ffn_down, norm_final_g):
    B, S, D = x.shape
    pos = jnp.arange(S, dtype=jnp.float32)
    for layer in range(DEPTH):
        h = rmsnorm(x, norm_mix_g[layer])
        proj = h @ w_in[layer]
        qkv = proj[..., :QKV_W].reshape(B, S, 3, N_ATTN_HEADS, HEAD_DIM)
        u = proj[..., QKV_W:QKV_W + SSM_W]
        gate = jax.nn.sigmoid(proj[..., QKV_W + SSM_W:].astype(jnp.float32)).reshape(B, S, 2, D)
        q = partial_rope(qkv[:, :, 0], pos)
        k = partial_rope(qkv[:, :, 1], pos)
        v = qkv[:, :, 2]
        outs, lses = [], []
        for gi, (window, dilation) in enumerate(ATTN_GROUPS):
            sl = slice(gi * HEADS_PER_GROUP, (gi + 1) * HEADS_PER_GROUP)
            o_g, lse_g = dilated_group_attention(q[:, :, sl], k[:, :, sl], v[:, :, sl], window, dilation)
            outs.append(o_g)
            lses.append(lse_g)
        outs = jnp.stack(outs, axis=0)
        alpha = jax.nn.softmax(jnp.stack(lses, axis=0), axis=0)
        attn = jnp.sum(alpha[..., None] * outs, axis=0).reshape(B, S, ATTN_OUT_W).astype(x.dtype)
        attn_d = attn @ w_attn_out[layer]
        ssm_out = s5_branch(u, ssm_a_re[layer], ssm_a_im[layer], ssm_log_dt[layer],
                            ssm_b_re[layer], ssm_b_im[layer], ssm_c_re[layer], ssm_c_im[layer],
                            ssm_d[layer], w_glu[layer])
        merged = (gate[:, :, 0] * attn_d.astype(jnp.float32)
                  + gate[:, :, 1] * ssm_out.astype(jnp.float32)).astype(x.dtype)
        x = x + merged @ w_out[layer]
        h2 = rmsnorm(x, norm_ffn_g[layer])
        ff = (jax.nn.silu(h2 @ w_ffn_gate[layer]) * (h2 @ w_ffn_up[layer])) @ w_ffn_down[layer]
        x = x + ff
    return rmsnorm(x, norm_final_g)


import jax as _jax
import jax.numpy as _jnp

TWIN_FORMAT = 'train_step'
FWD_PARAMS = ['x', 'norm_mix_g', 'w_in', 'ssm_a_re', 'ssm_a_im', 'ssm_log_dt', 'ssm_b_re', 'ssm_b_im', 'ssm_c_re', 'ssm_c_im', 'ssm_d', 'w_glu', 'w_attn_out', 'w_out', 'norm_ffn_g', 'w_ffn_gate', 'w_ffn_up', 'w_ffn_down', 'norm_final_g']
TWIN_WEIGHTS = ['norm_mix_g', 'w_in', 'ssm_a_re', 'ssm_a_im', 'ssm_log_dt', 'ssm_b_re', 'ssm_b_im', 'ssm_c_re', 'ssm_c_im', 'ssm_d', 'w_glu', 'w_attn_out', 'w_out', 'norm_ffn_g', 'w_ffn_gate', 'w_ffn_up', 'w_ffn_down', 'norm_final_g']
TWIN_DIFF_INPUT = 'x'
TWIN_INPUTS = ['x', 'norm_mix_g', 'w_in', 'ssm_a_re', 'ssm_a_im', 'ssm_log_dt', 'ssm_b_re', 'ssm_b_im', 'ssm_c_re', 'ssm_c_im', 'ssm_d', 'w_glu', 'w_attn_out', 'w_out', 'norm_ffn_g', 'w_ffn_gate', 'w_ffn_up', 'w_ffn_down', 'norm_final_g', 'loss_target', 'm_norm_mix_g', 'm_w_in', 'm_ssm_a_re', 'm_ssm_a_im', 'm_ssm_log_dt', 'm_ssm_b_re', 'm_ssm_b_im', 'm_ssm_c_re', 'm_ssm_c_im', 'm_ssm_d', 'm_w_glu', 'm_w_attn_out', 'm_w_out', 'm_norm_ffn_g', 'm_w_ffn_gate', 'm_w_ffn_up', 'm_w_ffn_down', 'm_norm_final_g', 'v_norm_mix_g', 'v_w_in', 'v_ssm_a_re', 'v_ssm_a_im', 'v_ssm_log_dt', 'v_ssm_b_re', 'v_ssm_b_im', 'v_ssm_c_re', 'v_ssm_c_im', 'v_ssm_d', 'v_w_glu', 'v_w_attn_out', 'v_w_out', 'v_norm_ffn_g', 'v_w_ffn_gate', 'v_w_ffn_up', 'v_w_ffn_down', 'v_norm_final_g']
TWIN_OUTPUTS = ['loss', 'grad_x', 'grad_norm_mix_g', 'grad_w_in', 'grad_ssm_a_re', 'grad_ssm_a_im', 'grad_ssm_log_dt', 'grad_ssm_b_re', 'grad_ssm_b_im', 'grad_ssm_c_re', 'grad_ssm_c_im', 'grad_ssm_d', 'grad_w_glu', 'grad_w_attn_out', 'grad_w_out', 'grad_norm_ffn_g', 'grad_w_ffn_gate', 'grad_w_ffn_up', 'grad_w_ffn_down', 'grad_norm_final_g', 'delta_norm_mix_g', 'delta_w_in', 'delta_ssm_a_re', 'delta_ssm_a_im', 'delta_ssm_log_dt', 'delta_ssm_b_re', 'delta_ssm_b_im', 'delta_ssm_c_re', 'delta_ssm_c_im', 'delta_ssm_d', 'delta_w_glu', 'delta_w_attn_out', 'delta_w_out', 'delta_norm_ffn_g', 'delta_w_ffn_gate', 'delta_w_ffn_up', 'delta_w_ffn_down', 'delta_norm_final_g', 'new_m_norm_mix_g', 'new_m_w_in', 'new_m_ssm_a_re', 'new_m_ssm_a_im', 'new_m_ssm_log_dt', 'new_m_ssm_b_re', 'new_m_ssm_b_im', 'new_m_ssm_c_re', 'new_m_ssm_c_im', 'new_m_ssm_d', 'new_m_w_glu', 'new_m_w_attn_out', 'new_m_w_out', 'new_m_norm_ffn_g', 'new_m_w_ffn_gate', 'new_m_w_ffn_up', 'new_m_w_ffn_down', 'new_m_norm_final_g', 'new_v_norm_mix_g', 'new_v_w_in', 'new_v_ssm_a_re', 'new_v_ssm_a_im', 'new_v_ssm_log_dt', 'new_v_ssm_b_re', 'new_v_ssm_b_im', 'new_v_ssm_c_re', 'new_v_ssm_c_im', 'new_v_ssm_d', 'new_v_w_glu', 'new_v_w_attn_out', 'new_v_w_out', 'new_v_norm_ffn_g', 'new_v_w_ffn_gate', 'new_v_w_ffn_up', 'new_v_w_ffn_down', 'new_v_norm_final_g']
TWIN_LEAF_KINDS = {'loss': 'loss', 'grad_x': 'grad_x', 'grad_norm_mix_g': 'grad_w', 'grad_w_in': 'grad_w', 'grad_ssm_a_re': 'grad_w', 'grad_ssm_a_im': 'grad_w', 'grad_ssm_log_dt': 'grad_w', 'grad_ssm_b_re': 'grad_w', 'grad_ssm_b_im': 'grad_w', 'grad_ssm_c_re': 'grad_w', 'grad_ssm_c_im': 'grad_w', 'grad_ssm_d': 'grad_w', 'grad_w_glu': 'grad_w', 'grad_w_attn_out': 'grad_w', 'grad_w_out': 'grad_w', 'grad_norm_ffn_g': 'grad_w', 'grad_w_ffn_gate': 'grad_w', 'grad_w_ffn_up': 'grad_w', 'grad_w_ffn_down': 'grad_w', 'grad_norm_final_g': 'grad_w', 'delta_norm_mix_g': 'delta_w', 'delta_w_in': 'delta_w', 'delta_ssm_a_re': 'delta_w', 'delta_ssm_a_im': 'delta_w', 'delta_ssm_log_dt': 'delta_w', 'delta_ssm_b_re': 'delta_w', 'delta_ssm_b_im': 'delta_w', 'delta_ssm_c_re': 'delta_w', 'delta_ssm_c_im': 'delta_w', 'delta_ssm_d': 'delta_w', 'delta_w_glu': 'delta_w', 'delta_w_attn_out': 'delta_w', 'delta_w_out': 'delta_w', 'delta_norm_ffn_g': 'delta_w', 'delta_w_ffn_gate': 'delta_w', 'delta_w_ffn_up': 'delta_w', 'delta_w_ffn_down': 'delta_w', 'delta_norm_final_g': 'delta_w', 'new_m_norm_mix_g': 'new_m', 'new_m_w_in': 'new_m', 'new_m_ssm_a_re': 'new_m', 'new_m_ssm_a_im': 'new_m', 'new_m_ssm_log_dt': 'new_m', 'new_m_ssm_b_re': 'new_m', 'new_m_ssm_b_im': 'new_m', 'new_m_ssm_c_re': 'new_m', 'new_m_ssm_c_im': 'new_m', 'new_m_ssm_d': 'new_m', 'new_m_w_glu': 'new_m', 'new_m_w_attn_out': 'new_m', 'new_m_w_out': 'new_m', 'new_m_norm_ffn_g': 'new_m', 'new_m_w_ffn_gate': 'new_m', 'new_m_w_ffn_up': 'new_m', 'new_m_w_ffn_down': 'new_m', 'new_m_norm_final_g': 'new_m', 'new_v_norm_mix_g': 'new_v', 'new_v_w_in': 'new_v', 'new_v_ssm_a_re': 'new_v', 'new_v_ssm_a_im': 'new_v', 'new_v_ssm_log_dt': 'new_v', 'new_v_ssm_b_re': 'new_v', 'new_v_ssm_b_im': 'new_v', 'new_v_ssm_c_re': 'new_v', 'new_v_ssm_c_im': 'new_v', 'new_v_ssm_d': 'new_v', 'new_v_w_glu': 'new_v', 'new_v_w_attn_out': 'new_v', 'new_v_w_out': 'new_v', 'new_v_norm_ffn_g': 'new_v', 'new_v_w_ffn_gate': 'new_v', 'new_v_w_ffn_up': 'new_v', 'new_v_w_ffn_down': 'new_v', 'new_v_norm_final_g': 'new_v'}


def _forward(args):
    return _fwd_reference(*[args[k] for k in FWD_PARAMS])


def _output_shape():
    out = _jax.eval_shape(lambda: _forward(_fwd_setup_inputs(0)))
    return out.shape, out.dtype

N_MICROBATCH = 1
ADAM_LR = 0.001
ADAM_B1 = 0.9
ADAM_B2 = 0.999
ADAM_EPS = 1e-08
ADAM_WD = 0.01
ADAM_STEP = 10
PER_EXAMPLE_BATCH_AXIS = {'x': 0, 'loss_target': 0}
SHARED_INPUTS = []
_WEIGHT_DTYPES = {'norm_mix_g': _jnp.float32, 'w_in': _jnp.float32, 'ssm_a_re': _jnp.float32, 'ssm_a_im': _jnp.float32, 'ssm_log_dt': _jnp.float32, 'ssm_b_re': _jnp.float32, 'ssm_b_im': _jnp.float32, 'ssm_c_re': _jnp.float32, 'ssm_c_im': _jnp.float32, 'ssm_d': _jnp.float32, 'w_glu': _jnp.float32, 'w_attn_out': _jnp.float32, 'w_out': _jnp.float32, 'norm_ffn_g': _jnp.float32, 'w_ffn_gate': _jnp.float32, 'w_ffn_up': _jnp.float32, 'w_ffn_down': _jnp.float32, 'norm_final_g': _jnp.float32}
MOMENT_SCALE = {'norm_mix_g': 5.308834e-02, 'w_in': 2.510978e-02, 'ssm_a_re': 3.973967e-03, 'ssm_a_im': 4.312852e-03, 'ssm_log_dt': 1.348447e+00, 'ssm_b_re': 2.745941e-03, 'ssm_b_im': 2.814421e-03, 'ssm_c_re': 3.969562e-03, 'ssm_c_im': 3.871806e-03, 'ssm_d': 6.034637e-02, 'w_glu': 2.974573e-02, 'w_attn_out': 1.993047e-02, 'w_out': 4.365970e-02, 'norm_ffn_g': 1.277630e-01, 'w_ffn_gate': 5.625481e-02, 'w_ffn_up': 5.463860e-02, 'w_ffn_down': 9.092008e-02, 'norm_final_g': 3.207409e+01}


def _to_microbatches(a, axis):
    t = _jnp.moveaxis(a, axis, 0)
    t = t.reshape((N_MICROBATCH, t.shape[0] // N_MICROBATCH) + t.shape[1:])
    return _jnp.moveaxis(t, 1, axis + 1)


def setup_inputs(seed: int = 0) -> dict:
    inp = _fwd_setup_inputs(seed)
    key = _jax.random.fold_in(_jax.random.key(seed), 7919)
    shape, _ = _output_shape()
    out = dict(inp)
    out["loss_target"] = _jax.random.normal(_jax.random.fold_in(key, 0), shape, _jnp.float32)
    for i, name in enumerate(TWIN_WEIGHTS):
        w = inp[name].astype(_jnp.float32)
        if MOMENT_SCALE is None:
            s = _jnp.sqrt(_jnp.mean(_jnp.square(w)) + 1e-30)
        else:
            s = MOMENT_SCALE[name]
        km, kv = _jax.random.split(_jax.random.fold_in(key, i + 1))
        out[name] = w
        out["m_" + name] = s * _jax.random.normal(km, w.shape, _jnp.float32)
        out["v_" + name] = (s * s) * _jax.random.uniform(kv, w.shape, _jnp.float32, 0.5, 1.5)
    if N_MICROBATCH > 1:
        for name, axis in PER_EXAMPLE_BATCH_AXIS.items():
            out[name] = _to_microbatches(out[name], axis)
    return {'x': out['x'], 'norm_mix_g': out['norm_mix_g'], 'w_in': out['w_in'], 'ssm_a_re': out['ssm_a_re'], 'ssm_a_im': out['ssm_a_im'], 'ssm_log_dt': out['ssm_log_dt'], 'ssm_b_re': out['ssm_b_re'], 'ssm_b_im': out['ssm_b_im'], 'ssm_c_re': out['ssm_c_re'], 'ssm_c_im': out['ssm_c_im'], 'ssm_d': out['ssm_d'], 'w_glu': out['w_glu'], 'w_attn_out': out['w_attn_out'], 'w_out': out['w_out'], 'norm_ffn_g': out['norm_ffn_g'], 'w_ffn_gate': out['w_ffn_gate'], 'w_ffn_up': out['w_ffn_up'], 'w_ffn_down': out['w_ffn_down'], 'norm_final_g': out['norm_final_g'], 'loss_target': out['loss_target'], 'm_norm_mix_g': out['m_norm_mix_g'], 'm_w_in': out['m_w_in'], 'm_ssm_a_re': out['m_ssm_a_re'], 'm_ssm_a_im': out['m_ssm_a_im'], 'm_ssm_log_dt': out['m_ssm_log_dt'], 'm_ssm_b_re': out['m_ssm_b_re'], 'm_ssm_b_im': out['m_ssm_b_im'], 'm_ssm_c_re': out['m_ssm_c_re'], 'm_ssm_c_im': out['m_ssm_c_im'], 'm_ssm_d': out['m_ssm_d'], 'm_w_glu': out['m_w_glu'], 'm_w_attn_out': out['m_w_attn_out'], 'm_w_out': out['m_w_out'], 'm_norm_ffn_g': out['m_norm_ffn_g'], 'm_w_ffn_gate': out['m_w_ffn_gate'], 'm_w_ffn_up': out['m_w_ffn_up'], 'm_w_ffn_down': out['m_w_ffn_down'], 'm_norm_final_g': out['m_norm_final_g'], 'v_norm_mix_g': out['v_norm_mix_g'], 'v_w_in': out['v_w_in'], 'v_ssm_a_re': out['v_ssm_a_re'], 'v_ssm_a_im': out['v_ssm_a_im'], 'v_ssm_log_dt': out['v_ssm_log_dt'], 'v_ssm_b_re': out['v_ssm_b_re'], 'v_ssm_b_im': out['v_ssm_b_im'], 'v_ssm_c_re': out['v_ssm_c_re'], 'v_ssm_c_im': out['v_ssm_c_im'], 'v_ssm_d': out['v_ssm_d'], 'v_w_glu': out['v_w_glu'], 'v_w_attn_out': out['v_w_attn_out'], 'v_w_out': out['v_w_out'], 'v_norm_ffn_g': out['v_norm_ffn_g'], 'v_w_ffn_gate': out['v_w_ffn_gate'], 'v_w_ffn_up': out['v_w_ffn_up'], 'v_w_ffn_down': out['v_w_ffn_down'], 'v_norm_final_g': out['v_norm_final_g']}


def _loss(weights, diff, rest, loss_target):
    with _jax.named_scope("forward"):
        args = {**rest, TWIN_DIFF_INPUT: diff, **{k: w.astype(_WEIGHT_DTYPES[k]) for k, w in weights.items()}}
        y = _forward(args)
    with _jax.named_scope("loss_head"):
        err = _jnp.square(y.astype(_jnp.float32) - loss_target)
        return 0.5 * _jnp.sum(_jnp.mean(err, axis=-1)) if err.ndim else 0.5 * err


def _adamw(w, g, m, v):
    m = ADAM_B1 * m + (1.0 - ADAM_B1) * g
    v = ADAM_B2 * v + (1.0 - ADAM_B2) * _jnp.square(g)
    m_hat = m / (1.0 - ADAM_B1 ** ADAM_STEP)
    v_hat = v / (1.0 - ADAM_B2 ** ADAM_STEP)
    delta = -ADAM_LR * (m_hat / (_jnp.sqrt(v_hat) + ADAM_EPS) + ADAM_WD * w)
    return delta, m, v


def reference(x, norm_mix_g, w_in, ssm_a_re, ssm_a_im, ssm_log_dt, ssm_b_re, ssm_b_im, ssm_c_re, ssm_c_im, ssm_d, w_glu, w_attn_out, w_out, norm_ffn_g, w_ffn_gate, w_ffn_up, w_ffn_down, norm_final_g, loss_target, m_norm_mix_g, m_w_in, m_ssm_a_re, m_ssm_a_im, m_ssm_log_dt, m_ssm_b_re, m_ssm_b_im, m_ssm_c_re, m_ssm_c_im, m_ssm_d, m_w_glu, m_w_attn_out, m_w_out, m_norm_ffn_g, m_w_ffn_gate, m_w_ffn_up, m_w_ffn_down, m_norm_final_g, v_norm_mix_g, v_w_in, v_ssm_a_re, v_ssm_a_im, v_ssm_log_dt, v_ssm_b_re, v_ssm_b_im, v_ssm_c_re, v_ssm_c_im, v_ssm_d, v_w_glu, v_w_attn_out, v_w_out, v_norm_ffn_g, v_w_ffn_gate, v_w_ffn_up, v_w_ffn_down, v_norm_final_g):
    given = dict(x=x, norm_mix_g=norm_mix_g, w_in=w_in, ssm_a_re=ssm_a_re, ssm_a_im=ssm_a_im, ssm_log_dt=ssm_log_dt, ssm_b_re=ssm_b_re, ssm_b_im=ssm_b_im, ssm_c_re=ssm_c_re, ssm_c_im=ssm_c_im, ssm_d=ssm_d, w_glu=w_glu, w_attn_out=w_attn_out, w_out=w_out, norm_ffn_g=norm_ffn_g, w_ffn_gate=w_ffn_gate, w_ffn_up=w_ffn_up, w_ffn_down=w_ffn_down, norm_final_g=norm_final_g, loss_target=loss_target, m_norm_mix_g=m_norm_mix_g, m_w_in=m_w_in, m_ssm_a_re=m_ssm_a_re, m_ssm_a_im=m_ssm_a_im, m_ssm_log_dt=m_ssm_log_dt, m_ssm_b_re=m_ssm_b_re, m_ssm_b_im=m_ssm_b_im, m_ssm_c_re=m_ssm_c_re, m_ssm_c_im=m_ssm_c_im, m_ssm_d=m_ssm_d, m_w_glu=m_w_glu, m_w_attn_out=m_w_attn_out, m_w_out=m_w_out, m_norm_ffn_g=m_norm_ffn_g, m_w_ffn_gate=m_w_ffn_gate, m_w_ffn_up=m_w_ffn_up, m_w_ffn_down=m_w_ffn_down, m_norm_final_g=m_norm_final_g, v_norm_mix_g=v_norm_mix_g, v_w_in=v_w_in, v_ssm_a_re=v_ssm_a_re, v_ssm_a_im=v_ssm_a_im, v_ssm_log_dt=v_ssm_log_dt, v_ssm_b_re=v_ssm_b_re, v_ssm_b_im=v_ssm_b_im, v_ssm_c_re=v_ssm_c_re, v_ssm_c_im=v_ssm_c_im, v_ssm_d=v_ssm_d, v_w_glu=v_w_glu, v_w_attn_out=v_w_attn_out, v_w_out=v_w_out, v_norm_ffn_g=v_norm_ffn_g, v_w_ffn_gate=v_w_ffn_gate, v_w_ffn_up=v_w_ffn_up, v_w_ffn_down=v_w_ffn_down, v_norm_final_g=v_norm_final_g)
    weights = {n: given[n] for n in TWIN_WEIGHTS}
    shared = {n: given[n] for n in SHARED_INPUTS}
    per_example = {n: given[n] for n in ['x']}
    grad_fn = _jax.value_and_grad(_loss, argnums=(0, 1))

    def one_microbatch(ex, loss_target):
        ex = dict(ex)
        diff = ex.pop(TWIN_DIFF_INPUT)
        return grad_fn(weights, diff, {**shared, **ex}, loss_target)

    if N_MICROBATCH == 1:
        loss, (grad_w, grad_x) = one_microbatch(per_example, given["loss_target"])
    else:
        def body(carry, xs):
            loss_sum, grad_sum = carry
            l_k, (gw_k, gx_k) = one_microbatch(xs[0], xs[1])
            with _jax.named_scope("update"):
                return (loss_sum + l_k, _jax.tree.map(_jnp.add, grad_sum, gw_k)), gx_k

        init = (_jnp.zeros((), _jnp.float32), _jax.tree.map(_jnp.zeros_like, weights))
        (loss, grad_w), grad_x = _jax.lax.scan(body, init, (per_example, given["loss_target"]))
    with _jax.named_scope("update"):
        delta_w, new_m, new_v = {}, {}, {}
        for n in TWIN_WEIGHTS:
            delta_w[n], new_m[n], new_v[n] = _adamw(weights[n], grad_w[n], given["m_" + n], given["v_" + n])
    return (loss, grad_x, *[grad_w[n] for n in TWIN_WEIGHTS], *[delta_w[n] for n in TWIN_WEIGHTS],
            *[new_m[n] for n in TWIN_WEIGHTS], *[new_v[n] for n in TWIN_WEIGHTS])
```

```python
import functools
import math

import jax
import jax.numpy as jnp
from jax import lax
from jax.experimental import pallas as pl
from jax.experimental.pallas import tpu as pltpu

F32 = jnp.float32
BF16 = jnp.bfloat16
MESH = pl.DeviceIdType.MESH

D_MODEL = 1024
SEQ = 2048
LOCAL_BATCH = 2
TOKENS = LOCAL_BATCH * SEQ
HEAD_DIM = 64
HEADS_PER_GROUP = 4
GROUP_W = HEADS_PER_GROUP * HEAD_DIM
DILATIONS = (1, 4, 16)
ATTN_BLOCK = 128
ROPE_DIM = 16
ROPE_THETA = 500000.0
QKV_W = 3 * 3 * GROUP_W
SSM_W = 512
SSM_STATE_W = 2048
SSM_LANE_BLOCKS = 4
GATE_W = 2 * D_MODEL
D_FF = 2816
RMS_EPS = 1e-6
NEG_INF = -1e30
ADAM_LR, ADAM_B1, ADAM_B2, ADAM_EPS, ADAM_WD, ADAM_STEP = 0.001, 0.9, 0.999, 1e-08, 0.01, 10
N_CHIPS = 4
N_DEV = 8

VMEM_LIMIT = 56 * 1024 * 1024
LANES = 128


def _params(sem=None):
    return pltpu.CompilerParams(dimension_semantics=sem, vmem_limit_bytes=VMEM_LIMIT)


def _pick(n, cap, align=LANES):
    best = None
    for d in range(align, min(n, cap) + 1, align):
        if n % d == 0:
            best = d
    return n if best is None or n <= cap else best


_DIMS = {"nn": (((1,), (0,)), ((), ())), "nt": (((1,), (1,)), ((), ())), "tn": (((0,), (0,)), ((), ()))}


def matmul(a, b, mode, out_dtype, name, add=None):
    if mode == "nn":
        (m, k), n = a.shape, b.shape[1]
    elif mode == "nt":
        (m, k), n = a.shape, b.shape[0]
    else:
        (k, m), n = a.shape, b.shape[1]
    tn = _pick(n, 1408)
    tk = _pick(k, 2816) if mode != "tn" else _pick(k, 1024)
    tm = _pick(m, 1024)
    out_bytes = jnp.dtype(out_dtype).itemsize

    def need(tm_):
        return 2 * 2 * (tm_ * tk + tk * tn) + tm_ * tn * (4 + 2 * out_bytes + (8 if add is not None else 0))

    while need(tm) > 40 * 1024 * 1024 and tm % 256 == 0:
        tm //= 2
    nk = k // tk
    a_spec = {"nn": pl.BlockSpec((tm, tk), lambda i, j, kk: (i, kk)),
              "nt": pl.BlockSpec((tm, tk), lambda i, j, kk: (i, kk)),
              "tn": pl.BlockSpec((tk, tm), lambda i, j, kk: (kk, i))}[mode]
    b_spec = {"nn": pl.BlockSpec((tk, tn), lambda i, j, kk: (kk, j)),
              "nt": pl.BlockSpec((tn, tk), lambda i, j, kk: (j, kk)),
              "tn": pl.BlockSpec((tk, tn), lambda i, j, kk: (kk, j))}[mode]
    o_spec = pl.BlockSpec((tm, tn), lambda i, j, kk: (i, j))
    dims = _DIMS[mode]

    def body(a_ref, b_ref, *rest):
        if add is not None:
            add_ref, o_ref, acc_ref = rest
        else:
            o_ref, acc_ref = rest
        part = lax.dot_general(a_ref[...], b_ref[...], dims, preferred_element_type=F32)
        if nk == 1:
            res = part if add is None else part + add_ref[...]
            o_ref[...] = res.astype(out_dtype)
            return
        kk = pl.program_id(2)

        @pl.when(kk == 0)
        def _():
            acc_ref[...] = part

        @pl.when(kk > 0)
        def _():
            acc_ref[...] += part

        @pl.when(kk == nk - 1)
        def _():
            res = acc_ref[...] if add is None else acc_ref[...] + add_ref[...]
            o_ref[...] = res.astype(out_dtype)

    in_specs = [a_spec, b_spec] + ([o_spec] if add is not None else [])
    args = (a, b) + ((add,) if add is not None else ())
    return pl.pallas_call(
        body, name=name, grid=(m // tm, n // tn, nk), in_specs=in_specs, out_specs=o_spec,
        out_shape=jax.ShapeDtypeStruct((m, n), out_dtype),
        scratch_shapes=[pltpu.VMEM((tm, tn) if nk > 1 else (8, LANES), F32)],
        compiler_params=_params(("parallel", "parallel", "arbitrary")),
    )(*args)


def rowwise(fn, ins, outs, name, accs=(), tm=256, rows=TOKENS):
    in_specs, args = [], []
    for item in ins:
        arr, width, blk = item if isinstance(item, tuple) else (item, None, 0)
        if arr.shape[0] == 1:
            in_specs.append(pl.BlockSpec(arr.shape, lambda i: (0, 0)))
        elif width is None:
            in_specs.append(pl.BlockSpec((tm, arr.shape[1]), lambda i: (i, 0)))
        else:
            in_specs.append(pl.BlockSpec((tm, width), functools.partial(lambda i, blk_: (i, blk_), blk_=blk)))
        args.append(arr)
    out_specs = [pl.BlockSpec((tm, c), lambda i: (i, 0)) for c, _ in outs]
    out_specs += [pl.BlockSpec((1, c), lambda i: (0, 0)) for c in accs]
    out_shape = [jax.ShapeDtypeStruct((rows, c), dt) for c, dt in outs]
    out_shape += [jax.ShapeDtypeStruct((1, c), F32) for c in accs]
    n_in, n_out = len(ins), len(outs)

    def body(*refs):
        res = fn(*[r[...] for r in refs[:n_in]])
        for r, v in zip(refs[n_in:n_in + n_out], res[:n_out]):
            r[...] = v.astype(r.dtype)
        first = pl.program_id(0) == 0
        for r, v in zip(refs[n_in + n_out:], res[n_out:]):
            @pl.when(first)
            def _(r=r, v=v):
                r[...] = v

            @pl.when(jnp.logical_not(first))
            def _(r=r, v=v):
                r[...] += v

    res = pl.pallas_call(
        body, name=name, grid=(rows // tm,), in_specs=in_specs, out_specs=out_specs, out_shape=out_shape,
        compiler_params=_params(("arbitrary",)),
    )(*args)
    return res


def _rms(x, g):
    return x * lax.rsqrt(jnp.mean(x * x, axis=-1, keepdims=True) + RMS_EPS) * g


def _colsum(v):
    return jnp.sum(v, axis=0, keepdims=True)


def _rope_tables():
    half = ROPE_DIM // 2
    inv = jnp.power(jnp.float32(ROPE_THETA), -jnp.arange(half, dtype=F32) * 2.0 / ROPE_DIM)
    ang = jnp.arange(SEQ, dtype=F32)[:, None] * inv[None, :]
    cos, sin = jnp.cos(ang), jnp.sin(ang)
    zeros = jnp.zeros((SEQ, HEAD_DIM - ROPE_DIM), F32)
    zh = jnp.zeros((SEQ, half), F32)
    c = jnp.concatenate([cos, cos, zeros + 1.0], axis=1)
    sa = jnp.concatenate([-sin, zh, zeros], axis=1)
    sb = jnp.concatenate([zh, sin, zeros], axis=1)
    return [jnp.tile(t, (LOCAL_BATCH, HEADS_PER_GROUP)) for t in (c, sa, sb)]


def _rope_fwd(x, c, sa, sb):
    return x * c + pltpu.roll(x, GROUP_W - 8, 1) * sa + pltpu.roll(x, 8, 1) * sb


def _rope_bwd(dy, c, sa, sb):
    return dy * c + pltpu.roll(dy * sb, GROUP_W - 8, 1) + pltpu.roll(dy * sa, 8, 1)


def _qkv_prep_fn(qkv, c, sa, sb):
    scale = HEAD_DIM ** -0.5
    parts = []
    for g in range(3):
        parts.append(_rope_fwd(qkv[:, g * GROUP_W:(g + 1) * GROUP_W], c, sa, sb) * scale)
    for g in range(3, 6):
        parts.append(_rope_fwd(qkv[:, g * GROUP_W:(g + 1) * GROUP_W], c, sa, sb))
    parts.append(qkv[:, 6 * GROUP_W:])
    return (jnp.concatenate(parts, axis=1),)


def _dqkv_fn(*vals):
    grads, (c, sa, sb) = vals[:9], vals[9:]
    scale = HEAD_DIM ** -0.5
    dq = [_rope_bwd(grads[3 * g] * scale, c, sa, sb) for g in range(3)]
    dk = [_rope_bwd(grads[3 * g + 1], c, sa, sb) for g in range(3)]
    dv = [grads[3 * g + 2] for g in range(3)]
    return (jnp.concatenate(dq + dk + dv, axis=1),)


def _dot(a, b, mode):
    return lax.dot_general(a, b, _DIMS[mode], preferred_element_type=F32)


def _band_masks():
    row = lax.broadcasted_iota(jnp.int32, (ATTN_BLOCK, ATTN_BLOCK), 0)
    col = lax.broadcasted_iota(jnp.int32, (ATTN_BLOCK, ATTN_BLOCK), 1)
    return col <= row, col >= row


def _dilated_view(arr, dil, width):
    return arr.reshape(LOCAL_BATCH, SEQ // dil, dil * width)


def attn_fwd(qkv_b, gi):
    dil = DILATIONS[gi]
    nb = SEQ // dil // ATTN_BLOCK
    blocks = QKV_W // GROUP_W
    view = _dilated_view(qkv_b, dil, QKV_W)

    def spec(off, prev):
        def index(b, r, n):
            return (b, jnp.maximum(n - 1, 0) if prev else n, r * blocks + off + gi)
        return pl.BlockSpec((None, ATTN_BLOCK, GROUP_W), index)

    o_spec = pl.BlockSpec((None, ATTN_BLOCK, GROUP_W), lambda b, r, n: (b, n, r))

    def body(q_ref, kc_ref, kp_ref, vc_ref, vp_ref, o_ref, lse_ref):
        has_prev = pl.program_id(2) > 0
        cur_mask, prev_mask = _band_masks()
        prev_mask = jnp.logical_and(prev_mask, has_prev)
        for h in range(HEADS_PER_GROUP):
            sl = slice(h * HEAD_DIM, (h + 1) * HEAD_DIM)
            q = q_ref[:, sl]
            sc = jnp.where(cur_mask, _dot(q, kc_ref[:, sl], "nt"), NEG_INF)
            sp = jnp.where(prev_mask, _dot(q, kp_ref[:, sl], "nt"), NEG_INF)
            m = jnp.maximum(jnp.max(sc, axis=-1, keepdims=True), jnp.max(sp, axis=-1, keepdims=True))
            pc, pp = jnp.exp(sc - m), jnp.exp(sp - m)
            l = jnp.sum(pc, axis=-1, keepdims=True) + jnp.sum(pp, axis=-1, keepdims=True)
            o = _dot(pc.astype(BF16), vc_ref[:, sl], "nn") + _dot(pp.astype(BF16), vp_ref[:, sl], "nn")
            o_ref[:, sl] = o / l
            lse_ref[:, sl] = jnp.broadcast_to(m + jnp.log(l), (ATTN_BLOCK, HEAD_DIM))

    out = jax.ShapeDtypeStruct((LOCAL_BATCH, SEQ // dil, dil * GROUP_W), F32)
    o, lse = pl.pallas_call(
        body, name=f"attn_fwd_{gi}", grid=(LOCAL_BATCH, dil, nb),
        in_specs=[spec(0, False), spec(3, False), spec(3, True), spec(6, False), spec(6, True)],
        out_specs=[o_spec, o_spec], out_shape=[out, out],
        compiler_params=_params(("parallel", "parallel", "arbitrary")),
    )(view, view, view, view, view)
    return o.reshape(TOKENS, GROUP_W), lse.reshape(TOKENS, GROUP_W)


def _attn_combine_fn(o0, o1, o2, l0, l1, l2):
    m = jnp.maximum(jnp.maximum(l0, l1), l2)
    e0, e1, e2 = jnp.exp(l0 - m), jnp.exp(l1 - m), jnp.exp(l2 - m)
    tot = e0 + e1 + e2
    attn = (e0 * o0 + e1 * o1 + e2 * o2) / tot
    return attn, attn, m + jnp.log(tot)


def attn_bwd(qkv_b, dattn, attn, lse, gi):
    dil = DILATIONS[gi]
    nb = SEQ // dil // ATTN_BLOCK
    blocks = QKV_W // GROUP_W
    view = _dilated_view(qkv_b, dil, QKV_W)
    tok_views = [_dilated_view(t, dil, GROUP_W) for t in (dattn, attn, lse)]

    def spec(off, shift):
        def index(b, r, n):
            return (b, jnp.clip(n + shift, 0, nb - 1), r * blocks + off + gi)
        return pl.BlockSpec((None, ATTN_BLOCK, GROUP_W), index)

    def tok_spec(shift):
        return pl.BlockSpec((None, ATTN_BLOCK, GROUP_W), lambda b, r, n: (b, jnp.clip(n + shift, 0, nb - 1), r))

    o_spec = pl.BlockSpec((None, ATTN_BLOCK, GROUP_W), lambda b, r, n: (b, n, r))

    def body(q0_ref, q1_ref, kp_ref, kc_ref, vp_ref, vc_ref, do0_ref, do1_ref, out0_ref, out1_ref,
             lse0_ref, lse1_ref, dq_ref, dk_ref, dv_ref):
        n = pl.program_id(2)
        cur_mask, prev_mask = _band_masks()
        has_prev = jnp.logical_and(prev_mask, n > 0)
        has_next = jnp.logical_and(prev_mask, n < nb - 1)

        def pair(q, k, v, do, lse_col, delta, mask):
            p = jnp.where(mask, jnp.exp(_dot(q, k, "nt") - lse_col), 0.0)
            ds = p * (_dot(do, v, "nt") - delta)
            return p.astype(BF16), ds.astype(BF16)

        for h in range(HEADS_PER_GROUP):
            sl = slice(h * HEAD_DIM, (h + 1) * HEAD_DIM)
            q0, q1, kp, kc, vp, vc = (r[:, sl] for r in (q0_ref, q1_ref, kp_ref, kc_ref, vp_ref, vc_ref))
            do0f, do1f = do0_ref[:, sl], do1_ref[:, sl]
            delta0 = jnp.sum(do0f * out0_ref[:, sl], axis=-1, keepdims=True)
            delta1 = jnp.sum(do1f * out1_ref[:, sl], axis=-1, keepdims=True)
            do0, do1 = do0f.astype(BF16), do1f.astype(BF16)
            lse0 = lse0_ref[:, h * HEAD_DIM:h * HEAD_DIM + 1]
            lse1 = lse1_ref[:, h * HEAD_DIM:h * HEAD_DIM + 1]
            _, ds_a = pair(q0, kp, vp, do0, lse0, delta0, has_prev)
            p_b, ds_b = pair(q0, kc, vc, do0, lse0, delta0, cur_mask)
            p_c, ds_c = pair(q1, kc, vc, do1, lse1, delta1, has_next)
            dq_ref[:, sl] = _dot(ds_a, kp, "nn") + _dot(ds_b, kc, "nn")
            dk_ref[:, sl] = _dot(ds_b, q0, "tn") + _dot(ds_c, q1, "tn")
            dv_ref[:, sl] = _dot(p_b, do0, "tn") + _dot(p_c, do1, "tn")

    out = jax.ShapeDtypeStruct((LOCAL_BATCH, SEQ // dil, dil * GROUP_W), F32)
    res = pl.pallas_call(
        body, name=f"attn_bwd_{gi}", grid=(LOCAL_BATCH, dil, nb),
        in_specs=[spec(0, 0), spec(0, 1), spec(3, -1), spec(3, 0), spec(6, -1), spec(6, 0),
                  tok_spec(0), tok_spec(1), tok_spec(0), tok_spec(1), tok_spec(0), tok_spec(1)],
        out_specs=[o_spec] * 3, out_shape=[out] * 3,
        compiler_params=_params(("parallel", "parallel", "arbitrary")),
    )(view, view, view, view, view, view, tok_views[0], tok_views[0], tok_views[1], tok_views[1],
      tok_views[2], tok_views[2])
    return [t.reshape(TOKENS, GROUP_W) for t in res]


def _discretize(lr, li, log_dt, br, bi):
    dt = jnp.exp(log_dt)
    mag = jnp.exp(lr * dt)
    ab_re, ab_im = mag * jnp.cos(li * dt), mag * jnp.sin(li * dt)
    den = lr * lr + li * li
    nr, ni = ab_re - 1.0, ab_im
    f_re = (nr * lr + ni * li) / den
    f_im = (ni * lr - nr * li) / den
    return ab_re, ab_im, f_re[None] * br - f_im[None] * bi, f_re[None] * bi + f_im[None] * br


def ssm_prep(lr, li, log_dt, br, bi):
    def body(lr_ref, li_ref, dt_ref, br_ref, bi_ref, *outs):
        for o, v in zip(outs, _discretize(lr_ref[...], li_ref[...], dt_ref[...], br_ref[...], bi_ref[...])):
            o[...] = v
    shapes = [lr, li, br, bi]
    return pl.pallas_call(body, name="ssm_prep",
                          out_shape=[jax.ShapeDtypeStruct(s.shape, F32) for s in shapes])(lr, li, log_dt, br, bi)


def ssm_prep_bwd(lr, li, log_dt, br, bi, g_ab_re, g_ab_im, g_bb_re, g_bb_im):
    def body(lr_ref, li_ref, dt_ref, br_ref, bi_ref, g0, g1, g2, g3, *outs):
        _, vjp = jax.vjp(_discretize, lr_ref[...], li_ref[...], dt_ref[...], br_ref[...], bi_ref[...])
        for o, v in zip(outs, vjp((g0[...], g1[...], g2[...], g3[...]))):
            o[...] = v
    shapes = [lr, li, log_dt, br, bi]
    return pl.pallas_call(body, name="ssm_prep_bwd",
                          out_shape=[jax.ShapeDtypeStruct(s.shape, F32) for s in shapes])(
        lr, li, log_dt, br, bi, g_ab_re, g_ab_im, g_bb_re, g_bb_im)


def _block_diag(t):
    per = SSM_STATE_W // SSM_LANE_BLOCKS // 64
    g = t.transpose(1, 0, 2).reshape(SSM_LANE_BLOCKS, per, 16, 64)
    eye = jnp.eye(per, dtype=t.dtype)
    return jnp.einsum("jgcn,gh->jgchn", g, eye).reshape(SSM_LANE_BLOCKS, per * 16, per * 64)


def _block_diag_t(m):
    per = SSM_STATE_W // SSM_LANE_BLOCKS // 64
    m5 = m.reshape(SSM_LANE_BLOCKS, per, 16, per, 64)
    d = jnp.einsum("jgchn,gh->jgcn", m5, jnp.eye(per, dtype=m.dtype))
    return d.reshape(SSM_LANE_BLOCKS * per, 16, 64).transpose(1, 0, 2)


def _cmul(ar, ai, br, bi):
    return ar * br - ai * bi, ar * bi + ai * br


def _power_tables(ar, ai, reverse):
    width = ar.shape[1]
    row = lax.broadcasted_iota(jnp.int32, (8, width), 0)
    pows = [(ar, ai)]
    for _ in range(7):
        pows.append(_cmul(pows[-1][0], pows[-1][1], ar, ai))
    steps = []
    for k in (1, 2, 4):
        keep = (row >= k) if not reverse else (row < 8 - k)
        steps.append((jnp.where(keep, pows[k - 1][0], 0.0), jnp.where(keep, pows[k - 1][1], 0.0)))
    cr = jnp.zeros((8, width), F32)
    ci = jnp.zeros((8, width), F32)
    for i in range(8):
        pr, pi = pows[i] if not reverse else pows[7 - i]
        cr = jnp.where(row == i, pr, cr)
        ci = jnp.where(row == i, pi, ci)
    return steps, (cr, ci)


SCAN_CHUNK = 512
STATE_BLOCK = SSM_STATE_W // SSM_LANE_BLOCKS
CHAN_BLOCK = SSM_W // SSM_LANE_BLOCKS


def ssm_fwd(u, ab_re, ab_im, bb_re, bb_im, cb_re, cb_im, d_skip):
    nt = SEQ // SCAN_CHUNK
    u3 = u.reshape(LOCAL_BATCH, SEQ, SSM_W)
    chan = pl.BlockSpec((None, SCAN_CHUNK, CHAN_BLOCK), lambda b, j, t: (b, t, j))
    state = pl.BlockSpec((None, SCAN_CHUNK, STATE_BLOCK), lambda b, j, t: (b, t, j))
    mat = pl.BlockSpec((None, CHAN_BLOCK, STATE_BLOCK), lambda b, j, t: (j, 0, 0))
    lane = pl.BlockSpec((1, STATE_BLOCK), lambda b, j, t: (0, j))
    dsp = pl.BlockSpec((1, CHAN_BLOCK), lambda b, j, t: (0, j))

    def body(u_ref, ar_ref, ai_ref, bbr_ref, bbi_ref, cbr_ref, cbi_ref, d_ref, y_ref, yg_ref, xr_ref, xi_ref,
             car_r, car_i):
        @pl.when(pl.program_id(2) == 0)
        def _():
            car_r[...] = jnp.zeros_like(car_r)
            car_i[...] = jnp.zeros_like(car_i)

        steps, (pr, pi) = _power_tables(ar_ref[...], ai_ref[...], reverse=False)
        uf = u_ref[...]
        ub = uf.astype(BF16)
        xr_ref[...] = _dot(ub, bbr_ref[...], "nn")
        xi_ref[...] = _dot(ub, bbi_ref[...], "nn")

        def tile(i, carry):
            cr, ci = carry
            sl = pl.ds(pl.multiple_of(i * 8, 8), 8)
            br, bi = xr_ref[sl, :], xi_ref[sl, :]
            for k, (sr, si) in zip((1, 2, 4), steps):
                tr, ti = _cmul(sr, si, pltpu.roll(br, k, 0), pltpu.roll(bi, k, 0))
                br, bi = br + tr, bi + ti
            tr, ti = _cmul(pr, pi, cr, ci)
            br, bi = br + tr, bi + ti
            xr_ref[sl, :] = br
            xi_ref[sl, :] = bi
            return br[7:8, :], bi[7:8, :]

        cr, ci = lax.fori_loop(0, SCAN_CHUNK // 8, tile, (car_r[0:1, :], car_i[0:1, :]))
        car_r[0:1, :] = cr
        car_i[0:1, :] = ci
        y = (_dot(xr_ref[...].astype(BF16), cbr_ref[...], "nt") - _dot(xi_ref[...].astype(BF16), cbi_ref[...], "nt")
             + d_ref[...] * uf)
        y_ref[...] = y
        yg_ref[...] = jax.nn.gelu(y).astype(BF16)

    return pl.pallas_call(
        body, name="ssm_fwd", grid=(LOCAL_BATCH, SSM_LANE_BLOCKS, nt),
        in_specs=[chan, lane, lane, mat, mat, mat, mat, dsp],
        out_specs=[chan, chan, state, state],
        out_shape=[jax.ShapeDtypeStruct((LOCAL_BATCH, SEQ, SSM_W), F32),
                   jax.ShapeDtypeStruct((LOCAL_BATCH, SEQ, SSM_W), BF16),
                   jax.ShapeDtypeStruct((LOCAL_BATCH, SEQ, SSM_STATE_W), F32),
                   jax.ShapeDtypeStruct((LOCAL_BATCH, SEQ, SSM_STATE_W), F32)],
        scratch_shapes=[pltpu.VMEM((8, STATE_BLOCK), F32), pltpu.VMEM((8, STATE_BLOCK), F32)],
        compiler_params=_params(("parallel", "parallel", "arbitrary")),
    )(u3, ab_re, ab_im, bb_re, bb_im, cb_re, cb_im, d_skip)


def ssm_bwd(dyg, y, u, xr, xi, ab_re, ab_im, bb_re, bb_im, cb_re, cb_im, d_skip):
    nt = SEQ // SCAN_CHUNK
    ntile = SCAN_CHUNK // 8

    def rev(t):
        return nt - 1 - t

    chan = pl.BlockSpec((None, SCAN_CHUNK, CHAN_BLOCK), lambda j, b, t: (b, rev(t), j))
    state = pl.BlockSpec((None, SCAN_CHUNK, STATE_BLOCK), lambda j, b, t: (b, rev(t), j))
    before = pl.BlockSpec((None, 8, STATE_BLOCK), lambda j, b, t: (b, jnp.maximum(rev(t) * ntile - 1, 0), j))
    mat = pl.BlockSpec((None, CHAN_BLOCK, STATE_BLOCK), lambda j, b, t: (j, 0, 0))
    lane = pl.BlockSpec((1, STATE_BLOCK), lambda j, b, t: (0, j))
    lane8 = pl.BlockSpec((8, STATE_BLOCK), lambda j, b, t: (0, j))
    dsp = pl.BlockSpec((1, CHAN_BLOCK), lambda j, b, t: (0, j))

    def body(dyg_ref, y_ref, u_ref, xr_ref, xi_ref, xrb_ref, xib_ref, ar_ref, ai_ref, bbr_ref, bbi_ref, cbr_ref,
             cbi_ref, d_ref, du_ref, dcbr_ref, dcbi_ref, dbbr_ref, dbbi_ref, dd_ref, dar_ref, dai_ref,
             lam_r, lam_i, car_r, car_i):
        b, t = pl.program_id(1), pl.program_id(2)
        first = jnp.logical_and(b == 0, t == 0)

        @pl.when(t == 0)
        def _():
            car_r[...] = jnp.zeros_like(car_r)
            car_i[...] = jnp.zeros_like(car_i)

        @pl.when(first)
        def _():
            for r in (dcbr_ref, dcbi_ref, dbbr_ref, dbbi_ref, dd_ref, dar_ref, dai_ref):
                r[...] = jnp.zeros_like(r)

        steps, (pr, pi) = _power_tables(ar_ref[...], -ai_ref[...], reverse=True)
        uf = u_ref[...]
        _, gelu_vjp = jax.vjp(jax.nn.gelu, y_ref[...])
        dy = gelu_vjp(dyg_ref[...])[0]
        dyb = dy.astype(BF16)
        dd_ref[...] += _colsum(dy * uf)
        lam_r[...] = _dot(dyb, cbr_ref[...], "nn")
        lam_i[...] = -_dot(dyb, cbi_ref[...], "nn")
        dcbr_ref[...] += _dot(dyb, xr_ref[...].astype(BF16), "tn")
        dcbi_ref[...] -= _dot(dyb, xi_ref[...].astype(BF16), "tn")
        row0 = lax.broadcasted_iota(jnp.int32, (8, STATE_BLOCK), 0) == 0
        has_before = rev(t) > 0
        xrb = jnp.where(has_before, xrb_ref[...], 0.0)
        xib = jnp.where(has_before, xib_ref[...], 0.0)

        def tile(s, carry):
            cr, ci, acc_r, acc_i = carry
            i = ntile - 1 - s
            sl = pl.ds(pl.multiple_of(i * 8, 8), 8)
            gr, gi = lam_r[sl, :], lam_i[sl, :]
            for k, (sr, si) in zip((1, 2, 4), steps):
                tr, ti = _cmul(sr, si, pltpu.roll(gr, 8 - k, 0), pltpu.roll(gi, 8 - k, 0))
                gr, gi = gr + tr, gi + ti
            tr, ti = _cmul(pr, pi, cr, ci)
            gr, gi = gr + tr, gi + ti
            lam_r[sl, :] = gr
            lam_i[sl, :] = gi
            sp = pl.ds(pl.multiple_of(jnp.maximum(i - 1, 0) * 8, 8), 8)
            pvr = jnp.where(i > 0, xr_ref[sp, :], xrb)
            pvi = jnp.where(i > 0, xi_ref[sp, :], xib)
            xsr = jnp.where(row0, pltpu.roll(pvr, 1, 0), pltpu.roll(xr_ref[sl, :], 1, 0))
            xsi = jnp.where(row0, pltpu.roll(pvi, 1, 0), pltpu.roll(xi_ref[sl, :], 1, 0))
            acc_r = acc_r + xsr * gr + xsi * gi
            acc_i = acc_i + xsr * gi - xsi * gr
            return gr[0:1, :], gi[0:1, :], acc_r, acc_i

        zero = jnp.zeros((8, STATE_BLOCK), F32)
        cr, ci, acc_r, acc_i = lax.fori_loop(0, ntile, tile, (car_r[0:1, :], car_i[0:1, :], zero, zero))
        car_r[0:1, :] = cr
        car_i[0:1, :] = ci
        dar_ref[...] += acc_r
        dai_ref[...] += acc_i
        lrb, lib = lam_r[...].astype(BF16), lam_i[...].astype(BF16)
        du = _dot(lrb, bbr_ref[...], "nt") + _dot(lib, bbi_ref[...], "nt") + d_ref[...] * dy
        du_ref[...] = du.astype(BF16)
        ub = uf.astype(BF16)
        dbbr_ref[...] += _dot(ub, lrb, "tn")
        dbbi_ref[...] += _dot(ub, lib, "tn")

    mat_shape = jax.ShapeDtypeStruct((SSM_LANE_BLOCKS, CHAN_BLOCK, STATE_BLOCK), F32)
    return pl.pallas_call(
        body, name="ssm_bwd", grid=(SSM_LANE_BLOCKS, LOCAL_BATCH, nt),
        in_specs=[chan, chan, chan, state, state, before, before, lane, lane, mat, mat, mat, mat, dsp],
        out_specs=[chan, mat, mat, mat, mat, dsp, lane8, lane8],
        out_shape=[jax.ShapeDtypeStruct((LOCAL_BATCH, SEQ, SSM_W), BF16), mat_shape, mat_shape, mat_shape, mat_shape,
                   jax.ShapeDtypeStruct((1, SSM_W), F32), jax.ShapeDtypeStruct((8, SSM_STATE_W), F32),
                   jax.ShapeDtypeStruct((8, SSM_STATE_W), F32)],
        scratch_shapes=[pltpu.VMEM((SCAN_CHUNK, STATE_BLOCK), F32), pltpu.VMEM((SCAN_CHUNK, STATE_BLOCK), F32),
                        pltpu.VMEM((8, STATE_BLOCK), F32), pltpu.VMEM((8, STATE_BLOCK), F32)],
        compiler_params=_params(("parallel", "arbitrary", "arbitrary")),
    )(dyg, y, u, xr, xi, xr, xi, ab_re, ab_im, bb_re, bb_im, cb_re, cb_im, d_skip)


def _merge_fn(g0, g1, attn_d, za, zb):
    return jax.nn.sigmoid(g0) * attn_d + jax.nn.sigmoid(g1) * (za * jax.nn.sigmoid(zb))


def _swiglu_fn(a, b):
    return jax.nn.silu(a) * b


def local_step(x, target, w, small):
    g_mix, g_ffn, g_final = small["norm_mix_g"], small["norm_ffn_g"], small["norm_final_g"]
    tables = _rope_tables()

    (h,) = rowwise(lambda xv, g: (_rms(xv, g),), [x, g_mix], [(D_MODEL, BF16)], "norm_mix")
    qkv = matmul(h, w["qkv"], "nn", F32, "proj_qkv")
    u = matmul(h, w["u"], "nn", F32, "proj_u")
    gl = matmul(h, w["gate"], "nn", F32, "proj_gate")
    (qkv_b,) = rowwise(_qkv_prep_fn, [qkv] + tables, [(QKV_W, BF16)], "qkv_prep")
    os_, lses = zip(*[attn_fwd(qkv_b, gi) for gi in range(3)])
    attn_b, attn, lse = rowwise(_attn_combine_fn, list(os_) + list(lses),
                                [(GROUP_W, BF16), (GROUP_W, F32), (GROUP_W, F32)], "attn_combine", tm=512)
    attn_d = matmul(attn_b, w["attn_out"], "nn", F32, "attn_out")

    br_t = small["ssm_b_re"].transpose(2, 0, 1)
    bi_t = small["ssm_b_im"].transpose(2, 0, 1)
    log_dt = small["ssm_log_dt"].reshape(32, 1)
    ab_re, ab_im, bb_re_t, bb_im_t = ssm_prep(small["ssm_a_re"], small["ssm_a_im"], log_dt, br_t, bi_t)
    ab = [ab_re.reshape(1, SSM_STATE_W), ab_im.reshape(1, SSM_STATE_W)]
    bb = [_block_diag(bb_re_t).astype(BF16), _block_diag(bb_im_t).astype(BF16)]
    cb = [_block_diag(small["ssm_c_re"].transpose(1, 0, 2)).astype(BF16),
          _block_diag(small["ssm_c_im"].transpose(1, 0, 2)).astype(BF16)]
    d_skip = small["ssm_d"].reshape(1, SSM_W)
    y, yg, xr, xi = ssm_fwd(u, *ab, *bb, *cb, d_skip)
    yg2 = yg.reshape(TOKENS, SSM_W)
    z = matmul(yg2, w["glu"], "nn", F32, "glu")
    gate_ins = [(gl, D_MODEL, 0), (gl, D_MODEL, 1), attn_d, (z, D_MODEL, 0), (z, D_MODEL, 1)]
    (merged,) = rowwise(lambda *v: (_merge_fn(*v),), gate_ins, [(D_MODEL, BF16)], "merge")
    x1 = matmul(merged, w["out"], "nn", F32, "out_proj", add=x)
    (h2,) = rowwise(lambda xv, g: (_rms(xv, g),), [x1, g_ffn], [(D_MODEL, BF16)], "norm_ffn")
    a = matmul(h2, w["ffn_gate"], "nn", F32, "ffn_gate")
    b = matmul(h2, w["ffn_up"], "nn", F32, "ffn_up")
    (act,) = rowwise(lambda av, bv: (_swiglu_fn(av, bv),), [a, b], [(D_FF, BF16)], "swiglu")
    x2 = matmul(act, w["ffn_down"], "nn", F32, "ffn_down", add=x1)

    def final_fn(xv, g, tgt):
        yv, vjp = jax.vjp(_rms, xv, g)
        err = yv - tgt
        dx, dg = vjp(err * (1.0 / D_MODEL))
        loss = 0.5 * jnp.sum(jnp.mean(err * err, axis=-1, keepdims=True), axis=0, keepdims=True)
        return dx, dx, dg, jnp.broadcast_to(loss, (1, LANES))

    dx2, dx2_b, dg_final, loss = rowwise(final_fn, [x2, g_final, target], [(D_MODEL, F32), (D_MODEL, BF16)],
                                         "final_norm_loss", accs=(D_MODEL, LANES))
    gw = {}
    gw["ffn_down"] = matmul(act, dx2_b, "tn", F32, "d_ffn_down")
    dact = matmul(dx2_b, w["ffn_down"], "nt", F32, "d_act")

    def swiglu_bwd(av, bv, dv):
        _, vjp = jax.vjp(_swiglu_fn, av, bv)
        return vjp(dv)

    da_b, db_b = rowwise(swiglu_bwd, [a, b, dact], [(D_FF, BF16), (D_FF, BF16)], "swiglu_bwd")
    gw["ffn_gate"] = matmul(h2, da_b, "tn", F32, "d_ffn_gate")
    gw["ffn_up"] = matmul(h2, db_b, "tn", F32, "d_ffn_up")
    dh2 = matmul(da_b, w["ffn_gate"], "nt", F32, "d_h2_gate")
    dh2 = matmul(db_b, w["ffn_up"], "nt", F32, "d_h2_up", add=dh2)

    def norm_bwd(xv, g, dh, skip):
        _, vjp = jax.vjp(_rms, xv, g)
        dx, dg = vjp(dh)
        dx = dx + skip
        return dx, dx, dg

    dx1, dx1_b, dg_ffn = rowwise(norm_bwd, [x1, g_ffn, dh2, dx2], [(D_MODEL, F32), (D_MODEL, BF16)],
                                 "norm_ffn_bwd", accs=(D_MODEL,))
    gw["out"] = matmul(merged, dx1_b, "tn", F32, "d_out")
    dmerged = matmul(dx1_b, w["out"], "nt", F32, "d_merged")

    def merge_bwd(g0, g1, ad, za, zb, dm):
        _, vjp = jax.vjp(_merge_fn, g0, g1, ad, za, zb)
        dg0, dg1, dad, dza, dzb = vjp(dm)
        return jnp.concatenate([dg0, dg1], axis=1), dad, jnp.concatenate([dza, dzb], axis=1)

    dgl_b, dattn_d_b, dz_b = rowwise(merge_bwd, gate_ins + [dmerged],
                                     [(GATE_W, BF16), (D_MODEL, BF16), (GATE_W, BF16)], "merge_bwd")
    gw["attn_out"] = matmul(attn_b, dattn_d_b, "tn", F32, "d_attn_out")
    dattn = matmul(dattn_d_b, w["attn_out"], "nt", F32, "d_attn")
    dqkv_parts = []
    for gi in range(3):
        dqkv_parts += attn_bwd(qkv_b, dattn, attn, lse, gi)
    (dqkv_b,) = rowwise(_dqkv_fn, dqkv_parts + tables, [(QKV_W, BF16)], "dqkv")
    gw["glu"] = matmul(yg2, dz_b, "tn", F32, "d_glu")
    dyg = matmul(dz_b, w["glu"], "nt", F32, "d_yg").reshape(LOCAL_BATCH, SEQ, SSM_W)
    u3 = u.reshape(LOCAL_BATCH, SEQ, SSM_W)
    du_b, dcb_re, dcb_im, dbb_re, dbb_im, dd, da_re8, da_im8 = ssm_bwd(dyg, y, u3, xr, xi, *ab, *bb, *cb, d_skip)
    du_b = du_b.reshape(TOKENS, SSM_W)
    gw["qkv"] = matmul(h, dqkv_b, "tn", F32, "d_w_qkv")
    gw["u"] = matmul(h, du_b, "tn", F32, "d_w_u")
    gw["gate"] = matmul(h, dgl_b, "tn", F32, "d_w_gate")
    dh = matmul(dqkv_b, w["qkv"], "nt", F32, "d_h_qkv")
    dh = matmul(du_b, w["u"], "nt", F32, "d_h_u", add=dh)
    dh = matmul(dgl_b, w["gate"], "nt", F32, "d_h_gate", add=dh)
    grad_x, _, dg_mix = rowwise(norm_bwd, [x, g_mix, dh, dx1], [(D_MODEL, F32), (D_MODEL, BF16)],
                                "norm_mix_bwd", accs=(D_MODEL,))
    g_ab_re = jnp.sum(da_re8, axis=0).reshape(32, 64)
    g_ab_im = jnp.sum(da_im8, axis=0).reshape(32, 64)
    d_lr, d_li, d_ldt, d_br_t, d_bi_t = ssm_prep_bwd(
        small["ssm_a_re"], small["ssm_a_im"], log_dt, br_t, bi_t,
        g_ab_re, g_ab_im, _block_diag_t(dbb_re), _block_diag_t(dbb_im))
    gs = {
        "norm_mix_g": dg_mix, "ssm_a_re": d_lr, "ssm_a_im": d_li, "ssm_log_dt": d_ldt.reshape(1, 32),
        "ssm_b_re": d_br_t.transpose(1, 2, 0), "ssm_b_im": d_bi_t.transpose(1, 2, 0),
        "ssm_c_re": _block_diag_t(dcb_re).transpose(1, 0, 2), "ssm_c_im": _block_diag_t(dcb_im).transpose(1, 0, 2),
        "ssm_d": dd.reshape(32, 16), "norm_ffn_g": dg_ffn, "norm_final_g": dg_final,
    }
    return loss, grad_x, gw, gs


ANY = pl.BlockSpec(memory_space=pl.ANY)
PACK_W = 1024
BIG = ("w_in", "w_glu", "w_attn_out", "w_out", "w_ffn_gate", "w_ffn_up", "w_ffn_down")
ROW_SHARDED = ("w_out", "w_ffn_down")
SMALL = ("norm_mix_g", "ssm_a_re", "ssm_a_im", "ssm_log_dt", "ssm_b_re", "ssm_b_im", "ssm_c_re", "ssm_c_im",
         "ssm_d", "norm_ffn_g", "norm_final_g")
WEIGHTS = ("norm_mix_g", "w_in", "ssm_a_re", "ssm_a_im", "ssm_log_dt", "ssm_b_re", "ssm_b_im", "ssm_c_re",
           "ssm_c_im", "ssm_d", "w_glu", "w_attn_out", "w_out", "norm_ffn_g", "w_ffn_gate", "w_ffn_up",
           "w_ffn_down", "norm_final_g")
SMALL_ROWS = 1088


def _position():
    return lax.axis_index("x"), lax.axis_index("y"), lax.axis_index("c")


def _remote(src, dst, send_sem, recv_sem, device):
    return pltpu.make_async_remote_copy(src_ref=src, dst_ref=dst, send_sem=send_sem, recv_sem=recv_sem,
                                        device_id=device, device_id_type=MESH)


def allgather_weights(packed):
    rows = packed.shape[0]
    half = rows // 2

    def body(src_ref, out_ref, send_sems, recv_sems, local_sem):
        x, y, c = _position()
        me = 2 * x + y
        sibling = (x, y, 1 - c)
        chips = [(1 - x, y), (x, 1 - y), (1 - x, 1 - y)]

        def part(chip, which):
            return out_ref.at[chip, pl.ds(which * half, half), :]

        local = pltpu.make_async_copy(src_ref, out_ref.at[me], local_sem)
        local.start()
        sends = [_remote(src_ref.at[pl.ds(c * half, half), :], part(me, c), send_sems.at[j], recv_sems.at[j],
                         (px, py, c)) for j, (px, py) in enumerate(chips)]
        for cp in sends:
            cp.start()
        passed = []
        for j, (px, py) in enumerate(chips):
            got = part(2 * px + py, c)
            _remote(got, got, send_sems.at[j], recv_sems.at[j], (px, py, c)).wait_recv()
            cp = _remote(got, got, send_sems.at[3 + j], recv_sems.at[3 + j], sibling)
            cp.start()
            passed.append(cp)
        for j, (px, py) in enumerate(chips):
            got = part(2 * px + py, 1 - c)
            _remote(got, got, send_sems.at[3 + j], recv_sems.at[3 + j], sibling).wait_recv()
        for cp in sends + passed:
            cp.wait_send()
        local.wait()

    return pl.pallas_call(
        body, name="allgather_weights", in_specs=[ANY], out_specs=ANY,
        out_shape=jax.ShapeDtypeStruct((N_CHIPS, rows, PACK_W), packed.dtype),
        scratch_shapes=[pltpu.SemaphoreType.DMA((6,)), pltpu.SemaphoreType.DMA((6,)), pltpu.SemaphoreType.DMA],
    )(packed)


def swap_halves(g):
    rows = g.shape[1]
    half = rows // 2

    def body(g_ref, got_ref, send_sem, recv_sem):
        x, y, c = _position()
        cp = _remote(g_ref.at[:, pl.ds((1 - c) * half, half), :], got_ref, send_sem, recv_sem, (x, y, 1 - c))
        cp.start()
        cp.wait()

    return pl.pallas_call(
        body, name="swap_halves", in_specs=[ANY], out_specs=ANY,
        out_shape=jax.ShapeDtypeStruct((N_CHIPS, half, PACK_W), g.dtype),
        scratch_shapes=[pltpu.SemaphoreType.DMA, pltpu.SemaphoreType.DMA],
    )(g)


def add_halves(g, got, core):
    half = got.shape[1]
    tr = half // 2
    mine = pl.BlockSpec((None, tr, PACK_W), lambda k, i, c_ref: (k, c_ref[0] * 2 + i, 0))
    other = pl.BlockSpec((None, tr, PACK_W), lambda k, i, c_ref: (k, i, 0))

    def body(c_ref, g_ref, got_ref, o_ref):
        o_ref[...] = (g_ref[...] + got_ref[...]).astype(BF16)

    return pl.pallas_call(
        body, name="add_halves",
        grid_spec=pltpu.PrefetchScalarGridSpec(num_scalar_prefetch=1, grid=(N_CHIPS, 2), in_specs=[mine, other],
                                               out_specs=other),
        out_shape=jax.ShapeDtypeStruct(got.shape, BF16),
        compiler_params=_params(("parallel", "parallel")),
    )(core, g, got)


def exchange_chips(p):
    def body(p_ref, out_ref, send_sems, recv_sems, local_sem):
        x, y, c = _position()
        me = 2 * x + y
        chips = [(1 - x, y), (x, 1 - y), (1 - x, 1 - y)]
        local = pltpu.make_async_copy(p_ref.at[me], out_ref.at[me], local_sem)
        local.start()
        sends = [_remote(p_ref.at[2 * px + py], out_ref.at[me], send_sems.at[j], recv_sems.at[j], (px, py, c))
                 for j, (px, py) in enumerate(chips)]
        for cp in sends:
            cp.start()
        for j, (px, py) in enumerate(chips):
            got = out_ref.at[2 * px + py]
            _remote(got, got, send_sems.at[j], recv_sems.at[j], (px, py, c)).wait_recv()
        for cp in sends:
            cp.wait_send()
        local.wait()

    return pl.pallas_call(
        body, name="exchange_chips", in_specs=[ANY], out_specs=ANY, out_shape=jax.ShapeDtypeStruct(p.shape, p.dtype),
        scratch_shapes=[pltpu.SemaphoreType.DMA((3,)), pltpu.SemaphoreType.DMA((3,)), pltpu.SemaphoreType.DMA],
    )(p)


def sum_chips(parts):
    half = parts.shape[1]
    tr = half // 2
    specs = [pl.BlockSpec((None, tr, PACK_W), functools.partial(lambda i, k_: (k_, i, 0), k_=k)) for k in range(N_CHIPS)]

    def body(p0, p1, p2, p3, o_ref):
        o_ref[...] = ((p0[...].astype(F32) + p1[...].astype(F32)) + p2[...].astype(F32)) + p3[...].astype(F32)

    return pl.pallas_call(
        body, name="sum_chips", grid=(2,), in_specs=specs, out_specs=pl.BlockSpec((tr, PACK_W), lambda i: (i, 0)),
        out_shape=jax.ShapeDtypeStruct((half, PACK_W), F32), compiler_params=_params(("parallel",)),
    )(parts, parts, parts, parts)


def join_halves(mine):
    half = mine.shape[0]

    def body(m_ref, out_ref, send_sem, recv_sem, local_sem):
        x, y, c = _position()
        local = pltpu.make_async_copy(m_ref, out_ref.at[pl.ds(c * half, half), :], local_sem)
        local.start()
        cp = _remote(m_ref, out_ref.at[pl.ds(c * half, half), :], send_sem, recv_sem, (x, y, 1 - c))
        cp.start()
        got = out_ref.at[pl.ds((1 - c) * half, half), :]
        _remote(got, got, send_sem, recv_sem, (x, y, 1 - c)).wait_recv()
        cp.wait_send()
        local.wait()

    return pl.pallas_call(
        body, name="join_halves", in_specs=[ANY], out_specs=ANY,
        out_shape=jax.ShapeDtypeStruct((2 * half, PACK_W), mine.dtype),
        scratch_shapes=[pltpu.SemaphoreType.DMA, pltpu.SemaphoreType.DMA, pltpu.SemaphoreType.DMA],
    )(mine)


def allgather_small(pack):
    def body(src_ref, out_ref, send_sems, recv_sems, local_sem):
        x, y, c = _position()
        me = 4 * x + 2 * y + c
        local = pltpu.make_async_copy(src_ref, out_ref.at[me], local_sem)
        local.start()
        flips = [(fx, fy, fc) for fx in (0, 1) for fy in (0, 1) for fc in (0, 1)][1:]
        peers = [(1 - x if fx else x, 1 - y if fy else y, 1 - c if fc else c) for fx, fy, fc in flips]
        sends = [_remote(src_ref, out_ref.at[me], send_sems.at[j], recv_sems.at[j], peer)
                 for j, peer in enumerate(peers)]
        for cp in sends:
            cp.start()
        for j, (px, py, pc) in enumerate(peers):
            got = out_ref.at[4 * px + 2 * py + pc]
            _remote(got, got, send_sems.at[j], recv_sems.at[j], (px, py, pc)).wait_recv()
        for cp in sends:
            cp.wait_send()
        local.wait()

    return pl.pallas_call(
        body, name="allgather_small", in_specs=[ANY], out_specs=ANY,
        out_shape=jax.ShapeDtypeStruct((N_DEV,) + pack.shape, pack.dtype),
        scratch_shapes=[pltpu.SemaphoreType.DMA((7,)), pltpu.SemaphoreType.DMA((7,)), pltpu.SemaphoreType.DMA],
    )(pack)


def _adam_fn(w, g, m, v):
    m = ADAM_B1 * m + (1.0 - ADAM_B1) * g
    v = ADAM_B2 * v + (1.0 - ADAM_B2) * jnp.square(g)
    m_hat = m / (1.0 - ADAM_B1 ** ADAM_STEP)
    v_hat = v / (1.0 - ADAM_B2 ** ADAM_STEP)
    return -ADAM_LR * (m_hat / (jnp.sqrt(v_hat) + ADAM_EPS) + ADAM_WD * w), m, v


def adam_big(name, w, g, m, v):
    rows, cols = w.shape
    tm = 256 if rows % 256 == 0 else rows // 2
    return rowwise(_adam_fn, [w, g, m, v], [(cols, F32)] * 3, "adam_" + name, tm=tm, rows=rows)


def adam_small(gathered, w, m, v):
    def body(g_ref, w_ref, m_ref, v_ref, go_ref, d_ref, mo_ref, vo_ref):
        g = g_ref[0]
        for k in range(1, N_DEV):
            g = g + g_ref[k]
        go_ref[...] = g
        d_ref[...], mo_ref[...], vo_ref[...] = _adam_fn(w_ref[...], g, m_ref[...], v_ref[...])

    return pl.pallas_call(body, name="adam_small", out_shape=[jax.ShapeDtypeStruct(w.shape, F32)] * 4,
                          compiler_params=_params())(gathered, w, m, v)


def _pack_small(vals, last=None):
    flat = [vals[n].reshape(-1) for n in SMALL]
    if last is not None:
        flat.append(last.reshape(-1))
    flat = jnp.concatenate(flat)
    return jnp.pad(flat, (0, SMALL_ROWS * LANES - flat.shape[0])).reshape(SMALL_ROWS, LANES)


def _unpack_small(pack, shapes):
    flat, out, off = pack.reshape(-1), {}, 0
    for n in SMALL:
        size = math.prod(shapes[n])
        out[n] = flat[off:off + size].reshape(shapes[n])
        off += size
    return out, flat[off]


def kernel(x, norm_mix_g, w_in, ssm_a_re, ssm_a_im, ssm_log_dt, ssm_b_re, ssm_b_im, ssm_c_re, ssm_c_im, ssm_d, w_glu, w_attn_out, w_out, norm_ffn_g, w_ffn_gate, w_ffn_up, w_ffn_down, norm_final_g, loss_target, m_norm_mix_g, m_w_in, m_ssm_a_re, m_ssm_a_im, m_ssm_log_dt, m_ssm_b_re, m_ssm_b_im, m_ssm_c_re, m_ssm_c_im, m_ssm_d, m_w_glu, m_w_attn_out, m_w_out, m_norm_ffn_g, m_w_ffn_gate, m_w_ffn_up, m_w_ffn_down, m_norm_final_g, v_norm_mix_g, v_w_in, v_ssm_a_re, v_ssm_a_im, v_ssm_log_dt, v_ssm_b_re, v_ssm_b_im, v_ssm_c_re, v_ssm_c_im, v_ssm_d, v_w_glu, v_w_attn_out, v_w_out, v_norm_ffn_g, v_w_ffn_gate, v_w_ffn_up, v_w_ffn_down, v_norm_final_g):
    given = dict(locals())
    shard = {n: given[n][0] for n in BIG}
    shapes = {n: given[n].shape for n in WEIGHTS}

    packed = jnp.concatenate([shard[n].astype(BF16).reshape(-1) for n in BIG]).reshape(-1, PACK_W)
    gathered = allgather_weights(packed)
    full, off = {}, 0
    for n in BIG:
        r, cdim = shard[n].shape
        size = r * cdim // PACK_W
        blocks = gathered[:, off:off + size, :].reshape(N_CHIPS, r, cdim)
        full[n] = jnp.concatenate(list(blocks), axis=0 if n in ROW_SHARDED else 1)
        off += size
    w = {"qkv": full["w_in"][:, :QKV_W], "u": full["w_in"][:, QKV_W:QKV_W + SSM_W],
         "gate": full["w_in"][:, QKV_W + SSM_W:], "glu": full["w_glu"], "attn_out": full["w_attn_out"],
         "out": full["w_out"], "ffn_gate": full["w_ffn_gate"], "ffn_up": full["w_ffn_up"],
         "ffn_down": full["w_ffn_down"]}
    small = {n: given[n] for n in SMALL}
    small_2d = dict(small)
    for n in ("ssm_a_re", "ssm_a_im", "ssm_b_re", "ssm_b_im", "ssm_c_re", "ssm_c_im", "ssm_d"):
        small_2d[n] = small[n][0]
    small_2d["norm_final_g"] = norm_final_g.reshape(1, D_MODEL)

    loss, grad_x, gw, gs = local_step(x.reshape(TOKENS, D_MODEL), loss_target.reshape(TOKENS, D_MODEL), w, small_2d)

    share = _pack_small({n: gs[n] for n in SMALL}, last=loss)
    everyone = allgather_small(share)
    packs = [_pack_small({n: given[p + n] for n in SMALL}) for p in ("", "m_", "v_")]
    small_out = [_unpack_small(t, shapes) for t in adam_small(everyone, *packs)]
    total_loss = small_out[0][1][()]

    gfull = {"w_in": jnp.concatenate([gw["qkv"], gw["u"], gw["gate"]], axis=1), "w_glu": gw["glu"],
             "w_attn_out": gw["attn_out"], "w_out": gw["out"], "w_ffn_gate": gw["ffn_gate"],
             "w_ffn_up": gw["ffn_up"], "w_ffn_down": gw["ffn_down"]}
    slots = []
    for k in range(N_CHIPS):
        pieces = []
        for n in BIG:
            r, cdim = shard[n].shape
            piece = gfull[n][k * r:(k + 1) * r] if n in ROW_SHARDED else gfull[n][:, k * cdim:(k + 1) * cdim]
            pieces.append(piece.reshape(-1))
        slots.append(jnp.concatenate(pieces).reshape(-1, PACK_W))
    gpack = jnp.stack(slots)
    core = lax.axis_index("c").astype(jnp.int32).reshape(1)
    chip_sum = add_halves(gpack, swap_halves(gpack), core)
    reduced = join_halves(sum_chips(exchange_chips(chip_sum)))
    big_out, off = {}, 0
    for n in BIG:
        r, cdim = shard[n].shape
        size = r * cdim // PACK_W
        g = reduced[off:off + size].reshape(r, cdim)
        off += size
        delta, new_m, new_v = adam_big(n, shard[n], g, given["m_" + n][0], given["v_" + n][0])
        big_out[n] = [t[None] for t in (g, delta, new_m, new_v)]

    outs = [total_loss, grad_x.reshape(LOCAL_BATCH, SEQ, D_MODEL)]
    for kind in range(4):
        for n in WEIGHTS:
            outs.append(big_out[n][kind] if n in BIG else small_out[kind][0][n])
    return tuple(outs)
```

```python
import functools
import math

import jax
import jax.numpy as jnp
from jax import lax
from jax.experimental import pallas as pl
from jax.experimental.pallas import tpu as pltpu

F32 = jnp.float32
BF16 = jnp.bfloat16
MESH = pl.DeviceIdType.MESH

D_MODEL = 1024
SEQ = 2048
LOCAL_BATCH = 2
TOKENS = LOCAL_BATCH * SEQ
HEAD_DIM = 64
HEADS_PER_GROUP = 4
GROUP_W = HEADS_PER_GROUP * HEAD_DIM
N_GROUPS = 3
DILATIONS = (1, 4, 16)
ATTN_BLOCK = 128
ROPE_DIM = 16
ROPE_THETA = 500000.0
QKV_W = 3 * N_GROUPS * GROUP_W
SSM_W = 512
SSM_STATE_W = 2048
SSM_LANE_BLOCKS = 4
GATE_W = 2 * D_MODEL
D_FF = 2816
RMS_EPS = 1e-6
NEG_INF = -1e30
ADAM_LR, ADAM_B1, ADAM_B2, ADAM_EPS, ADAM_WD, ADAM_STEP = 0.001, 0.9, 0.999, 1e-08, 0.01, 10
N_CHIPS = 4
N_DEV = 8

VMEM_LIMIT = 56 * 1024 * 1024
LANES = 128


def _params(sem=None):
    return pltpu.CompilerParams(dimension_semantics=sem, vmem_limit_bytes=VMEM_LIMIT)


def _pick(n, cap, align=LANES):
    best = None
    for d in range(align, min(n, cap) + 1, align):
        if n % d == 0:
            best = d
    return n if best is None or n <= cap else best


_DIMS = {"nn": (((1,), (0,)), ((), ())), "nt": (((1,), (1,)), ((), ())), "tn": (((0,), (0,)), ((), ()))}


def _dot(a, b, mode):
    return lax.dot_general(a, b, _DIMS[mode], preferred_element_type=F32)


def matmul(a, b, mode, out_dtype, name, add=None):
    if mode == "nn":
        (m, k), n = a.shape, b.shape[1]
    elif mode == "nt":
        (m, k), n = a.shape, b.shape[0]
    else:
        (k, m), n = a.shape, b.shape[1]
    tn = _pick(n, 1408)
    tk = _pick(k, 2816) if mode != "tn" else _pick(k, 1024)
    tm = _pick(m, 1024)
    out_bytes = jnp.dtype(out_dtype).itemsize

    def need(tm_):
        return 2 * 2 * (tm_ * tk + tk * tn) + tm_ * tn * (4 + 2 * out_bytes + (8 if add is not None else 0))

    while need(tm) > 40 * 1024 * 1024 and tm % 256 == 0:
        tm //= 2
    nk = k // tk
    a_spec = {"nn": pl.BlockSpec((tm, tk), lambda i, j, kk: (i, kk)),
              "nt": pl.BlockSpec((tm, tk), lambda i, j, kk: (i, kk)),
              "tn": pl.BlockSpec((tk, tm), lambda i, j, kk: (kk, i))}[mode]
    b_spec = {"nn": pl.BlockSpec((tk, tn), lambda i, j, kk: (kk, j)),
              "nt": pl.BlockSpec((tn, tk), lambda i, j, kk: (j, kk)),
              "tn": pl.BlockSpec((tk, tn), lambda i, j, kk: (kk, j))}[mode]
    o_spec = pl.BlockSpec((tm, tn), lambda i, j, kk: (i, j))

    def body(a_ref, b_ref, *rest):
        if add is not None:
            add_ref, o_ref, acc_ref = rest
        else:
            o_ref, acc_ref = rest
        part = _dot(a_ref[...], b_ref[...], mode)
        if nk == 1:
            res = part if add is None else part + add_ref[...]
            o_ref[...] = res.astype(out_dtype)
            return
        kk = pl.program_id(2)

        @pl.when(kk == 0)
        def _():
            acc_ref[...] = part

        @pl.when(kk > 0)
        def _():
            acc_ref[...] += part

        @pl.when(kk == nk - 1)
        def _():
            res = acc_ref[...] if add is None else acc_ref[...] + add_ref[...]
            o_ref[...] = res.astype(out_dtype)

    in_specs = [a_spec, b_spec] + ([o_spec] if add is not None else [])
    args = (a, b) + ((add,) if add is not None else ())
    return pl.pallas_call(
        body, name=name, grid=(m // tm, n // tn, nk), in_specs=in_specs, out_specs=o_spec,
        out_shape=jax.ShapeDtypeStruct((m, n), out_dtype),
        scratch_shapes=[pltpu.VMEM((tm, tn) if nk > 1 else (8, LANES), F32)],
        compiler_params=_params(("parallel", "parallel", "arbitrary")),
    )(*args)


def rowwise(fn, ins, outs, name, accs=(), tm=256, rows=TOKENS):
    in_specs, args = [], []
    for item in ins:
        arr, width, blk = item if isinstance(item, tuple) else (item, None, 0)
        if arr.ndim == 3:
            for k in range(arr.shape[0]):
                in_specs.append(pl.BlockSpec((None, tm, arr.shape[2]), functools.partial(lambda i, k_: (k_, i, 0), k_=k)))
                args.append(arr)
            continue
        if arr.shape[0] == 1:
            in_specs.append(pl.BlockSpec(arr.shape, lambda i: (0, 0)))
        elif width is None:
            in_specs.append(pl.BlockSpec((tm, arr.shape[1]), lambda i: (i, 0)))
        else:
            in_specs.append(pl.BlockSpec((tm, width), functools.partial(lambda i, blk_: (i, blk_), blk_=blk)))
        args.append(arr)
    out_specs = [pl.BlockSpec((tm, c), lambda i: (i, 0)) for c, _ in outs]
    out_specs += [pl.BlockSpec((1, c), lambda i: (0, 0)) for c in accs]
    out_shape = [jax.ShapeDtypeStruct((rows, c), dt) for c, dt in outs]
    out_shape += [jax.ShapeDtypeStruct((1, c), F32) for c in accs]
    n_in, n_out = len(args), len(outs)

    def body(*refs):
        res = fn(*[r[...] for r in refs[:n_in]])
        for r, v in zip(refs[n_in:n_in + n_out], res[:n_out]):
            r[...] = v.astype(r.dtype)
        first = pl.program_id(0) == 0
        for r, v in zip(refs[n_in + n_out:], res[n_out:]):
            @pl.when(first)
            def _(r=r, v=v):
                r[...] = v

            @pl.when(jnp.logical_not(first))
            def _(r=r, v=v):
                r[...] += v

    return pl.pallas_call(
        body, name=name, grid=(rows // tm,), in_specs=in_specs, out_specs=out_specs, out_shape=out_shape,
        compiler_params=_params(("arbitrary",)),
    )(*args)


def _rms(x, g):
    return x * lax.rsqrt(jnp.mean(x * x, axis=-1, keepdims=True) + RMS_EPS) * g


def _colsum(v):
    return jnp.sum(v, axis=0, keepdims=True)


PAIR_W = 2 * HEAD_DIM
N_PAIRS = HEADS_PER_GROUP // 2


def _qkv_order(w, back=False):
    rows = w.shape[0]
    dims = (N_PAIRS, N_GROUPS, 3) if back else (3, N_GROUPS, N_PAIRS)
    return w.reshape((rows,) + dims + (PAIR_W,)).transpose(0, 3, 2, 1, 4).reshape(rows, QKV_W)


def _rope_tables():
    half = ROPE_DIM // 2
    inv = jnp.power(jnp.float32(ROPE_THETA), -jnp.arange(half, dtype=F32) * 2.0 / ROPE_DIM)
    ang = jnp.arange(SEQ, dtype=F32)[:, None] * inv[None, :]
    cos, sin = jnp.cos(ang), jnp.sin(ang)
    zeros = jnp.zeros((SEQ, HEAD_DIM - ROPE_DIM), F32)
    zh = jnp.zeros((SEQ, half), F32)
    c = jnp.concatenate([cos, cos, zeros + 1.0], axis=1)
    sa = jnp.concatenate([-sin, zh, zeros], axis=1)
    sb = jnp.concatenate([zh, sin, zeros], axis=1)
    return [jnp.tile(t, (1, 2)) for t in (c, sa, sb)]


def _rope_fwd(x, c, sa, sb):
    return x * c + pltpu.roll(x, PAIR_W - 8, 1) * sa + pltpu.roll(x, 8, 1) * sb


def _rope_bwd(dy, c, sa, sb):
    return dy * c + pltpu.roll(dy * sb, PAIR_W - 8, 1) + pltpu.roll(dy * sa, 8, 1)


def _band_masks():
    row = lax.broadcasted_iota(jnp.int32, (ATTN_BLOCK, ATTN_BLOCK), 0)
    col = lax.broadcasted_iota(jnp.int32, (ATTN_BLOCK, ATTN_BLOCK), 1)
    return col <= row, col >= row


def _per_head(fn):
    return jnp.concatenate([fn(slice(h * HEAD_DIM, (h + 1) * HEAD_DIM)) for h in range(2)], axis=1)


def _slab_spec(kind):
    return pl.BlockSpec((None, SEQ, PAIR_W), lambda b, p, g: (b, 0, p * 3 * N_GROUPS + g * 3 + kind))


_TABLE_SPEC = pl.BlockSpec((SEQ, PAIR_W), lambda b, p, g: (0, 0))
_PAIR_SPEC = pl.BlockSpec((None, SEQ, PAIR_W), lambda b, p, g: (b, 0, p))


def _block_rows(dil, r, n):
    return pl.ds(n * (ATTN_BLOCK * dil) + r, ATTN_BLOCK, stride=dil)


def attn_fwd(qkv, tables):
    scale = HEAD_DIM ** -0.5

    def body(q_ref, k_ref, v_ref, c_ref, sa_ref, sb_ref, attn_b_ref, attn_ref, lse_ref, qs, ks, o0, o1, o2, l0, l1, l2):
        g = pl.program_id(2)
        c, sa, sb = c_ref[...], sa_ref[...], sb_ref[...]
        qs[...] = _rope_fwd(q_ref[...], c, sa, sb) * scale
        ks[...] = _rope_fwd(k_ref[...], c, sa, sb)
        cur_mask, prev_mask = _band_masks()

        def run(dil, o_slab, l_slab):
            nb = SEQ // dil // ATTN_BLOCK

            def block(idx, carry):
                r, n = lax.div(idx, nb), lax.rem(idx, nb)
                cur, prev = _block_rows(dil, r, n), _block_rows(dil, r, jnp.maximum(n - 1, 0))
                q = qs[cur, :].astype(BF16)
                kc, kp = ks[cur, :].astype(BF16), ks[prev, :].astype(BF16)
                vc, vp = v_ref[cur, :].astype(BF16), v_ref[prev, :].astype(BF16)
                pmask = jnp.logical_and(prev_mask, n > 0)
                outs, lses = [], []
                for h in range(2):
                    sl = slice(h * HEAD_DIM, (h + 1) * HEAD_DIM)
                    sc = jnp.where(cur_mask, _dot(q[:, sl], kc[:, sl], "nt"), NEG_INF)
                    sp = jnp.where(pmask, _dot(q[:, sl], kp[:, sl], "nt"), NEG_INF)
                    m = jnp.maximum(jnp.max(sc, axis=-1, keepdims=True), jnp.max(sp, axis=-1, keepdims=True))
                    pc, pp = jnp.exp(sc - m), jnp.exp(sp - m)
                    l = jnp.sum(pc, axis=-1, keepdims=True) + jnp.sum(pp, axis=-1, keepdims=True)
                    o = _dot(pc.astype(BF16), vc[:, sl], "nn") + _dot(pp.astype(BF16), vp[:, sl], "nn")
                    outs.append(o / l)
                    lses.append(jnp.broadcast_to(m + jnp.log(l), (ATTN_BLOCK, HEAD_DIM)))
                o_slab[cur, :] = jnp.concatenate(outs, axis=1)
                l_slab[cur, :] = jnp.concatenate(lses, axis=1)
                return carry

            lax.fori_loop(0, SEQ // ATTN_BLOCK, block, 0)

        for gi, (o_slab, l_slab) in enumerate(((o0, l0), (o1, l1), (o2, l2))):
            @pl.when(g == gi)
            def _(gi=gi, o_slab=o_slab, l_slab=l_slab):
                run(DILATIONS[gi], o_slab, l_slab)

        @pl.when(g == N_GROUPS - 1)
        def _():
            a, b, cc = l0[...], l1[...], l2[...]
            m = jnp.maximum(jnp.maximum(a, b), cc)
            e0, e1, e2 = jnp.exp(a - m), jnp.exp(b - m), jnp.exp(cc - m)
            tot = e0 + e1 + e2
            attn = (e0 * o0[...] + e1 * o1[...] + e2 * o2[...]) / tot
            attn_ref[...] = attn
            attn_b_ref[...] = attn.astype(BF16)
            lse_ref[...] = m + jnp.log(tot)

    shape = (LOCAL_BATCH, SEQ, GROUP_W)
    slab = pltpu.VMEM((SEQ, PAIR_W), F32)
    return pl.pallas_call(
        body, name="attn_fwd", grid=(LOCAL_BATCH, N_PAIRS, N_GROUPS),
        in_specs=[_slab_spec(0), _slab_spec(1), _slab_spec(2), _TABLE_SPEC, _TABLE_SPEC, _TABLE_SPEC],
        out_specs=[_PAIR_SPEC] * 3,
        out_shape=[jax.ShapeDtypeStruct(shape, BF16), jax.ShapeDtypeStruct(shape, F32), jax.ShapeDtypeStruct(shape, F32)],
        scratch_shapes=[slab] * 8,
        compiler_params=_params(("parallel", "parallel", "arbitrary")),
    )(qkv, qkv, qkv, *tables)


def attn_bwd(qkv, tables, dattn, attn, lse):
    scale = HEAD_DIM ** -0.5

    def body(q_ref, k_ref, v_ref, c_ref, sa_ref, sb_ref, do_ref, out_ref, lse_ref, dqkv_ref, qs, ks, dl, dq_s, dk_s, dv_s):
        g = pl.program_id(2)
        c, sa, sb = c_ref[...], sa_ref[...], sb_ref[...]
        qs[...] = _rope_fwd(q_ref[...], c, sa, sb) * scale
        ks[...] = _rope_fwd(k_ref[...], c, sa, sb)
        prod = do_ref[...] * out_ref[...]
        dl[...] = _per_head(lambda sl: jnp.broadcast_to(jnp.sum(prod[:, sl], axis=-1, keepdims=True), (SEQ, HEAD_DIM)))
        cur_mask, prev_mask = _band_masks()

        def run(dil):
            nb = SEQ // dil // ATTN_BLOCK

            def block(idx, carry):
                r, n = lax.div(idx, nb), lax.rem(idx, nb)
                cur = _block_rows(dil, r, n)
                prev = _block_rows(dil, r, jnp.maximum(n - 1, 0))
                nxt = _block_rows(dil, r, jnp.minimum(n + 1, nb - 1))
                q0, q1 = qs[cur, :].astype(BF16), qs[nxt, :].astype(BF16)
                kp, kc = ks[prev, :].astype(BF16), ks[cur, :].astype(BF16)
                vp, vc = v_ref[prev, :].astype(BF16), v_ref[cur, :].astype(BF16)
                do0, do1 = do_ref[cur, :].astype(BF16), do_ref[nxt, :].astype(BF16)
                lse0, lse1, dl0, dl1 = lse_ref[cur, :], lse_ref[nxt, :], dl[cur, :], dl[nxt, :]
                has_prev = jnp.logical_and(prev_mask, n > 0)
                has_next = jnp.logical_and(prev_mask, n < nb - 1)

                def pair(q, k, v, do, lse_col, delta, mask):
                    p = jnp.where(mask, jnp.exp(_dot(q, k, "nt") - lse_col), 0.0)
                    ds = p * (_dot(do, v, "nt") - delta)
                    return p.astype(BF16), ds.astype(BF16)

                dqs, dks, dvs = [], [], []
                for h in range(2):
                    sl = slice(h * HEAD_DIM, (h + 1) * HEAD_DIM)
                    one = slice(h * HEAD_DIM, h * HEAD_DIM + 1)
                    _, ds_a = pair(q0[:, sl], kp[:, sl], vp[:, sl], do0[:, sl], lse0[:, one], dl0[:, one], has_prev)
                    p_b, ds_b = pair(q0[:, sl], kc[:, sl], vc[:, sl], do0[:, sl], lse0[:, one], dl0[:, one], cur_mask)
                    p_c, ds_c = pair(q1[:, sl], kc[:, sl], vc[:, sl], do1[:, sl], lse1[:, one], dl1[:, one], has_next)
                    dqs.append(_dot(ds_a, kp[:, sl], "nn") + _dot(ds_b, kc[:, sl], "nn"))
                    dks.append(_dot(ds_b, q0[:, sl], "tn") + _dot(ds_c, q1[:, sl], "tn"))
                    dvs.append(_dot(p_b, do0[:, sl], "tn") + _dot(p_c, do1[:, sl], "tn"))
                dq_s[cur, :] = jnp.concatenate(dqs, axis=1)
                dk_s[cur, :] = jnp.concatenate(dks, axis=1)
                dv_s[cur, :] = jnp.concatenate(dvs, axis=1)
                return carry

            lax.fori_loop(0, SEQ // ATTN_BLOCK, block, 0)

        for gi in range(N_GROUPS):
            @pl.when(g == gi)
            def _(gi=gi):
                run(DILATIONS[gi])

        dqkv_ref[:, 0:PAIR_W] = _rope_bwd(dq_s[...] * scale, c, sa, sb).astype(BF16)
        dqkv_ref[:, PAIR_W:2 * PAIR_W] = _rope_bwd(dk_s[...], c, sa, sb).astype(BF16)
        dqkv_ref[:, 2 * PAIR_W:] = dv_s[...].astype(BF16)

    slab = pltpu.VMEM((SEQ, PAIR_W), F32)
    return pl.pallas_call(
        body, name="attn_bwd", grid=(LOCAL_BATCH, N_PAIRS, N_GROUPS),
        in_specs=[_slab_spec(0), _slab_spec(1), _slab_spec(2), _TABLE_SPEC, _TABLE_SPEC, _TABLE_SPEC,
                  _PAIR_SPEC, _PAIR_SPEC, _PAIR_SPEC],
        out_specs=pl.BlockSpec((None, SEQ, 3 * PAIR_W), lambda b, p, g: (b, 0, p * N_GROUPS + g)),
        out_shape=jax.ShapeDtypeStruct((LOCAL_BATCH, SEQ, QKV_W), BF16),
        scratch_shapes=[slab] * 6,
        compiler_params=_params(("parallel", "parallel", "arbitrary")),
    )(qkv, qkv, qkv, *tables, dattn, attn, lse)


def _discretize(lr, li, log_dt, br, bi):
    dt = jnp.exp(log_dt)
    mag = jnp.exp(lr * dt)
    ab_re, ab_im = mag * jnp.cos(li * dt), mag * jnp.sin(li * dt)
    den = lr * lr + li * li
    nr, ni = ab_re - 1.0, ab_im
    f_re = (nr * lr + ni * li) / den
    f_im = (ni * lr - nr * li) / den
    return ab_re, ab_im, f_re[None] * br - f_im[None] * bi, f_re[None] * bi + f_im[None] * br


def ssm_prep(lr, li, log_dt, br, bi):
    def body(lr_ref, li_ref, dt_ref, br_ref, bi_ref, *outs):
        for o, v in zip(outs, _discretize(lr_ref[...], li_ref[...], dt_ref[...], br_ref[...], bi_ref[...])):
            o[...] = v
    shapes = [lr, li, br, bi]
    return pl.pallas_call(body, name="ssm_prep",
                          out_shape=[jax.ShapeDtypeStruct(s.shape, F32) for s in shapes])(lr, li, log_dt, br, bi)


def ssm_prep_bwd(lr, li, log_dt, br, bi, g_ab_re, g_ab_im, g_bb_re, g_bb_im):
    def body(lr_ref, li_ref, dt_ref, br_ref, bi_ref, g0, g1, g2, g3, *outs):
        _, vjp = jax.vjp(_discretize, lr_ref[...], li_ref[...], dt_ref[...], br_ref[...], bi_ref[...])
        for o, v in zip(outs, vjp((g0[...], g1[...], g2[...], g3[...]))):
            o[...] = v
    shapes = [lr, li, log_dt, br, bi]
    return pl.pallas_call(body, name="ssm_prep_bwd",
                          out_shape=[jax.ShapeDtypeStruct(s.shape, F32) for s in shapes])(
        lr, li, log_dt, br, bi, g_ab_re, g_ab_im, g_bb_re, g_bb_im)


def _block_diag(t):
    per = SSM_STATE_W // SSM_LANE_BLOCKS // 64
    g = t.transpose(1, 0, 2).reshape(SSM_LANE_BLOCKS, per, 16, 64)
    eye = jnp.eye(per, dtype=t.dtype)
    return jnp.einsum("jgcn,gh->jgchn", g, eye).reshape(SSM_LANE_BLOCKS, per * 16, per * 64)


def _block_diag_t(m):
    per = SSM_STATE_W // SSM_LANE_BLOCKS // 64
    m5 = m.reshape(SSM_LANE_BLOCKS, per, 16, per, 64)
    d = jnp.einsum("jgchn,gh->jgcn", m5, jnp.eye(per, dtype=m.dtype))
    return d.reshape(SSM_LANE_BLOCKS * per, 16, 64).transpose(1, 0, 2)


def _cmul(ar, ai, br, bi):
    return ar * br - ai * bi, ar * bi + ai * br


def _power_tables(ar, ai, reverse):
    width = ar.shape[1]
    row = lax.broadcasted_iota(jnp.int32, (8, width), 0)
    pows = [(ar, ai)]
    for _ in range(7):
        pows.append(_cmul(pows[-1][0], pows[-1][1], ar, ai))
    steps = []
    for k in (1, 2, 4):
        keep = (row >= k) if not reverse else (row < 8 - k)
        steps.append((jnp.where(keep, pows[k - 1][0], 0.0), jnp.where(keep, pows[k - 1][1], 0.0)))
    cr = jnp.zeros((8, width), F32)
    ci = jnp.zeros((8, width), F32)
    for i in range(8):
        pr, pi = pows[i] if not reverse else pows[7 - i]
        cr = jnp.where(row == i, pr, cr)
        ci = jnp.where(row == i, pi, ci)
    return steps, (cr, ci)


SCAN_CHUNK = 512
STATE_BLOCK = SSM_STATE_W // SSM_LANE_BLOCKS
CHAN_BLOCK = SSM_W // SSM_LANE_BLOCKS


def ssm_fwd(u, ab_re, ab_im, bb_re, bb_im, cb_re, cb_im, d_skip):
    nt = SEQ // SCAN_CHUNK
    chan = pl.BlockSpec((None, SCAN_CHUNK, CHAN_BLOCK), lambda b, j, t: (b, t, j))
    state = pl.BlockSpec((None, SCAN_CHUNK, STATE_BLOCK), lambda b, j, t: (b, t, j))
    mat = pl.BlockSpec((None, CHAN_BLOCK, STATE_BLOCK), lambda b, j, t: (j, 0, 0))
    lane = pl.BlockSpec((1, STATE_BLOCK), lambda b, j, t: (0, j))
    dsp = pl.BlockSpec((1, CHAN_BLOCK), lambda b, j, t: (0, j))

    def body(u_ref, ar_ref, ai_ref, bbr_ref, bbi_ref, cbr_ref, cbi_ref, d_ref, y_ref, yg_ref, xr_ref, xi_ref,
             car_r, car_i):
        @pl.when(pl.program_id(2) == 0)
        def _():
            car_r[...] = jnp.zeros_like(car_r)
            car_i[...] = jnp.zeros_like(car_i)

        steps, (pr, pi) = _power_tables(ar_ref[...], ai_ref[...], reverse=False)
        uf = u_ref[...]
        ub = uf.astype(BF16)
        xr_ref[...] = _dot(ub, bbr_ref[...], "nn")
        xi_ref[...] = _dot(ub, bbi_ref[...], "nn")

        def tile(i, carry):
            cr, ci = carry
            sl = pl.ds(pl.multiple_of(i * 8, 8), 8)
            br, bi = xr_ref[sl, :], xi_ref[sl, :]
            for k, (sr, si) in zip((1, 2, 4), steps):
                tr, ti = _cmul(sr, si, pltpu.roll(br, k, 0), pltpu.roll(bi, k, 0))
                br, bi = br + tr, bi + ti
            tr, ti = _cmul(pr, pi, cr, ci)
            br, bi = br + tr, bi + ti
            xr_ref[sl, :] = br
            xi_ref[sl, :] = bi
            return br[7:8, :], bi[7:8, :]

        cr, ci = lax.fori_loop(0, SCAN_CHUNK // 8, tile, (car_r[0:1, :], car_i[0:1, :]))
        car_r[0:1, :] = cr
        car_i[0:1, :] = ci
        y = (_dot(xr_ref[...].astype(BF16), cbr_ref[...], "nt") - _dot(xi_ref[...].astype(BF16), cbi_ref[...], "nt")
             + d_ref[...] * uf)
        y_ref[...] = y
        yg_ref[...] = jax.nn.gelu(y).astype(BF16)

    return pl.pallas_call(
        body, name="ssm_fwd", grid=(LOCAL_BATCH, SSM_LANE_BLOCKS, nt),
        in_specs=[chan, lane, lane, mat, mat, mat, mat, dsp],
        out_specs=[chan, chan, state, state],
        out_shape=[jax.ShapeDtypeStruct((LOCAL_BATCH, SEQ, SSM_W), F32),
                   jax.ShapeDtypeStruct((LOCAL_BATCH, SEQ, SSM_W), BF16),
                   jax.ShapeDtypeStruct((LOCAL_BATCH, SEQ, SSM_STATE_W), F32),
                   jax.ShapeDtypeStruct((LOCAL_BATCH, SEQ, SSM_STATE_W), F32)],
        scratch_shapes=[pltpu.VMEM((8, STATE_BLOCK), F32), pltpu.VMEM((8, STATE_BLOCK), F32)],
        compiler_params=_params(("parallel", "parallel", "arbitrary")),
    )(u, ab_re, ab_im, bb_re, bb_im, cb_re, cb_im, d_skip)


def ssm_bwd(dyg, y, u, xr, xi, ab_re, ab_im, bb_re, bb_im, cb_re, cb_im, d_skip):
    nt = SEQ // SCAN_CHUNK
    ntile = SCAN_CHUNK // 8

    def rev(t):
        return nt - 1 - t

    chan = pl.BlockSpec((None, SCAN_CHUNK, CHAN_BLOCK), lambda j, b, t: (b, rev(t), j))
    state = pl.BlockSpec((None, SCAN_CHUNK, STATE_BLOCK), lambda j, b, t: (b, rev(t), j))
    before = pl.BlockSpec((None, 8, STATE_BLOCK), lambda j, b, t: (b, jnp.maximum(rev(t) * ntile - 1, 0), j))
    mat = pl.BlockSpec((None, CHAN_BLOCK, STATE_BLOCK), lambda j, b, t: (j, 0, 0))
    lane = pl.BlockSpec((1, STATE_BLOCK), lambda j, b, t: (0, j))
    lane8 = pl.BlockSpec((8, STATE_BLOCK), lambda j, b, t: (0, j))
    dsp = pl.BlockSpec((1, CHAN_BLOCK), lambda j, b, t: (0, j))

    def body(dyg_ref, y_ref, u_ref, xr_ref, xi_ref, xrb_ref, xib_ref, ar_ref, ai_ref, bbr_ref, bbi_ref, cbr_ref,
             cbi_ref, d_ref, du_ref, dcbr_ref, dcbi_ref, dbbr_ref, dbbi_ref, dd_ref, dar_ref, dai_ref,
             lam_r, lam_i, car_r, car_i):
        b, t = pl.program_id(1), pl.program_id(2)
        first = jnp.logical_and(b == 0, t == 0)

        @pl.when(t == 0)
        def _():
            car_r[...] = jnp.zeros_like(car_r)
            car_i[...] = jnp.zeros_like(car_i)

        @pl.when(first)
        def _():
            for r in (dcbr_ref, dcbi_ref, dbbr_ref, dbbi_ref, dd_ref, dar_ref, dai_ref):
                r[...] = jnp.zeros_like(r)

        steps, (pr, pi) = _power_tables(ar_ref[...], -ai_ref[...], reverse=True)
        uf = u_ref[...]
        _, gelu_vjp = jax.vjp(jax.nn.gelu, y_ref[...])
        dy = gelu_vjp(dyg_ref[...])[0]
        dyb = dy.astype(BF16)
        dd_ref[...] += _colsum(dy * uf)
        lam_r[...] = _dot(dyb, cbr_ref[...], "nn")
        lam_i[...] = -_dot(dyb, cbi_ref[...], "nn")
        dcbr_ref[...] += _dot(dyb, xr_ref[...].astype(BF16), "tn")
        dcbi_ref[...] -= _dot(dyb, xi_ref[...].astype(BF16), "tn")
        row0 = lax.broadcasted_iota(jnp.int32, (8, STATE_BLOCK), 0) == 0
        has_before = rev(t) > 0
        xrb = jnp.where(has_before, xrb_ref[...], 0.0)
        xib = jnp.where(has_before, xib_ref[...], 0.0)

        def tile(s, carry):
            cr, ci, acc_r, acc_i = carry
            i = ntile - 1 - s
            sl = pl.ds(pl.multiple_of(i * 8, 8), 8)
            gr, gi = lam_r[sl, :], lam_i[sl, :]
            for k, (sr, si) in zip((1, 2, 4), steps):
                tr, ti = _cmul(sr, si, pltpu.roll(gr, 8 - k, 0), pltpu.roll(gi, 8 - k, 0))
                gr, gi = gr + tr, gi + ti
            tr, ti = _cmul(pr, pi, cr, ci)
            gr, gi = gr + tr, gi + ti
            lam_r[sl, :] = gr
            lam_i[sl, :] = gi
            sp = pl.ds(pl.multiple_of(jnp.maximum(i - 1, 0) * 8, 8), 8)
            pvr = jnp.where(i > 0, xr_ref[sp, :], xrb)
            pvi = jnp.where(i > 0, xi_ref[sp, :], xib)
            xsr = jnp.where(row0, pltpu.roll(pvr, 1, 0), pltpu.roll(xr_ref[sl, :], 1, 0))
            xsi = jnp.where(row0, pltpu.roll(pvi, 1, 0), pltpu.roll(xi_ref[sl, :], 1, 0))
            acc_r = acc_r + xsr * gr + xsi * gi
            acc_i = acc_i + xsr * gi - xsi * gr
            return gr[0:1, :], gi[0:1, :], acc_r, acc_i

        zero = jnp.zeros((8, STATE_BLOCK), F32)
        cr, ci, acc_r, acc_i = lax.fori_loop(0, ntile, tile, (car_r[0:1, :], car_i[0:1, :], zero, zero))
        car_r[0:1, :] = cr
        car_i[0:1, :] = ci
        dar_ref[...] += acc_r
        dai_ref[...] += acc_i
        lrb, lib = lam_r[...].astype(BF16), lam_i[...].astype(BF16)
        du = _dot(lrb, bbr_ref[...], "nt") + _dot(lib, bbi_ref[...], "nt") + d_ref[...] * dy
        du_ref[...] = du.astype(BF16)
        ub = uf.astype(BF16)
        dbbr_ref[...] += _dot(ub, lrb, "tn")
        dbbi_ref[...] += _dot(ub, lib, "tn")

    mat_shape = jax.ShapeDtypeStruct((SSM_LANE_BLOCKS, CHAN_BLOCK, STATE_BLOCK), F32)
    return pl.pallas_call(
        body, name="ssm_bwd", grid=(SSM_LANE_BLOCKS, LOCAL_BATCH, nt),
        in_specs=[chan, chan, chan, state, state, before, before, lane, lane, mat, mat, mat, mat, dsp],
        out_specs=[chan, mat, mat, mat, mat, dsp, lane8, lane8],
        out_shape=[jax.ShapeDtypeStruct((LOCAL_BATCH, SEQ, SSM_W), BF16), mat_shape, mat_shape, mat_shape, mat_shape,
                   jax.ShapeDtypeStruct((1, SSM_W), F32), jax.ShapeDtypeStruct((8, SSM_STATE_W), F32),
                   jax.ShapeDtypeStruct((8, SSM_STATE_W), F32)],
        scratch_shapes=[pltpu.VMEM((SCAN_CHUNK, STATE_BLOCK), F32), pltpu.VMEM((SCAN_CHUNK, STATE_BLOCK), F32),
                        pltpu.VMEM((8, STATE_BLOCK), F32), pltpu.VMEM((8, STATE_BLOCK), F32)],
        compiler_params=_params(("parallel", "arbitrary", "arbitrary")),
    )(dyg, y, u, xr, xi, xr, xi, ab_re, ab_im, bb_re, bb_im, cb_re, cb_im, d_skip)


def _merge_fn(g0, g1, attn_d, za, zb):
    return jax.nn.sigmoid(g0) * attn_d + jax.nn.sigmoid(g1) * (za * jax.nn.sigmoid(zb))


def _swiglu_fn(a, b):
    return jax.nn.silu(a) * b


def local_step(x, target, w, small):
    g_mix, g_ffn, g_final = small["norm_mix_g"], small["norm_ffn_g"], small["norm_final_g"]
    tables = _rope_tables()
    seqs = lambda t: t.reshape(LOCAL_BATCH, SEQ, t.shape[-1])
    toks = lambda t: t.reshape(TOKENS, t.shape[-1])

    (h,) = rowwise(lambda xv, g: (_rms(xv, g),), [x, g_mix], [(D_MODEL, BF16)], "norm_mix")
    qkv = seqs(matmul(h, w["qkv"], "nn", F32, "proj_qkv"))
    u = seqs(matmul(h, w["u"], "nn", F32, "proj_u"))
    gl = matmul(h, w["gate"], "nn", F32, "proj_gate")
    attn_b, attn, lse = attn_fwd(qkv, tables)
    attn_b = toks(attn_b)
    attn_d = matmul(attn_b, w["attn_out"], "nn", F32, "attn_out")

    br_t = small["ssm_b_re"].transpose(2, 0, 1)
    bi_t = small["ssm_b_im"].transpose(2, 0, 1)
    log_dt = small["ssm_log_dt"].reshape(32, 1)
    ab_re, ab_im, bb_re_t, bb_im_t = ssm_prep(small["ssm_a_re"], small["ssm_a_im"], log_dt, br_t, bi_t)
    ab = [ab_re.reshape(1, SSM_STATE_W), ab_im.reshape(1, SSM_STATE_W)]
    bb = [_block_diag(bb_re_t).astype(BF16), _block_diag(bb_im_t).astype(BF16)]
    cb = [_block_diag(small["ssm_c_re"].transpose(1, 0, 2)).astype(BF16),
          _block_diag(small["ssm_c_im"].transpose(1, 0, 2)).astype(BF16)]
    d_skip = small["ssm_d"].reshape(1, SSM_W)
    y, yg, xr, xi = ssm_fwd(u, *ab, *bb, *cb, d_skip)
    yg2 = toks(yg)
    z = matmul(yg2, w["glu"], "nn", F32, "glu")
    gate_ins = [(gl, D_MODEL, 0), (gl, D_MODEL, 1), attn_d, (z, D_MODEL, 0), (z, D_MODEL, 1)]
    (merged,) = rowwise(lambda *v: (_merge_fn(*v),), gate_ins, [(D_MODEL, BF16)], "merge")
    x1 = matmul(merged, w["out"], "nn", F32, "out_proj", add=x)
    (h2,) = rowwise(lambda xv, g: (_rms(xv, g),), [x1, g_ffn], [(D_MODEL, BF16)], "norm_ffn")
    a = matmul(h2, w["ffn_gate"], "nn", F32, "ffn_gate")
    b = matmul(h2, w["ffn_up"], "nn", F32, "ffn_up")
    (act,) = rowwise(lambda av, bv: (_swiglu_fn(av, bv),), [a, b], [(D_FF, BF16)], "swiglu")
    x2 = matmul(act, w["ffn_down"], "nn", F32, "ffn_down", add=x1)

    def final_fn(xv, g, tgt):
        yv, vjp = jax.vjp(_rms, xv, g)
        err = yv - tgt
        dx, dg = vjp(err * (1.0 / D_MODEL))
        loss = 0.5 * jnp.sum(jnp.mean(err * err, axis=-1, keepdims=True), axis=0, keepdims=True)
        return dx, dx, dg, jnp.broadcast_to(loss, (1, LANES))

    dx2, dx2_b, dg_final, loss = rowwise(final_fn, [x2, g_final, target], [(D_MODEL, F32), (D_MODEL, BF16)],
                                         "final_norm_loss", accs=(D_MODEL, LANES))
    gw = {}
    gw["ffn_down"] = matmul(act, dx2_b, "tn", F32, "d_ffn_down")
    dact = matmul(dx2_b, w["ffn_down"], "nt", F32, "d_act")

    def swiglu_bwd(av, bv, dv):
        _, vjp = jax.vjp(_swiglu_fn, av, bv)
        return vjp(dv)

    da_b, db_b = rowwise(swiglu_bwd, [a, b, dact], [(D_FF, BF16), (D_FF, BF16)], "swiglu_bwd")
    gw["ffn_gate"] = matmul(h2, da_b, "tn", F32, "d_ffn_gate")
    gw["ffn_up"] = matmul(h2, db_b, "tn", F32, "d_ffn_up")
    dh2 = matmul(da_b, w["ffn_gate"], "nt", F32, "d_h2_gate")
    dh2 = matmul(db_b, w["ffn_up"], "nt", F32, "d_h2_up", add=dh2)

    def norm_bwd(xv, g, dh, skip):
        _, vjp = jax.vjp(_rms, xv, g)
        dx, dg = vjp(dh)
        dx = dx + skip
        return dx, dx, dg

    dx1, dx1_b, dg_ffn = rowwise(norm_bwd, [x1, g_ffn, dh2, dx2], [(D_MODEL, F32), (D_MODEL, BF16)],
                                 "norm_ffn_bwd", accs=(D_MODEL,))
    gw["out"] = matmul(merged, dx1_b, "tn", F32, "d_out")
    dmerged = matmul(dx1_b, w["out"], "nt", F32, "d_merged")

    def merge_bwd(g0, g1, ad, za, zb, dm):
        _, vjp = jax.vjp(_merge_fn, g0, g1, ad, za, zb)
        dg0, dg1, dad, dza, dzb = vjp(dm)
        return jnp.concatenate([dg0, dg1], axis=1), dad, jnp.concatenate([dza, dzb], axis=1)

    dgl_b, dattn_d_b, dz_b = rowwise(merge_bwd, gate_ins + [dmerged],
                                     [(GATE_W, BF16), (D_MODEL, BF16), (GATE_W, BF16)], "merge_bwd")
    gw["attn_out"] = matmul(attn_b, dattn_d_b, "tn", F32, "d_attn_out")
    dattn = seqs(matmul(dattn_d_b, w["attn_out"], "nt", F32, "d_attn"))
    dqkv_b = toks(attn_bwd(qkv, tables, dattn, attn, lse))
    gw["glu"] = matmul(yg2, dz_b, "tn", F32, "d_glu")
    dyg = seqs(matmul(dz_b, w["glu"], "nt", F32, "d_yg"))
    du_b, dcb_re, dcb_im, dbb_re, dbb_im, dd, da_re8, da_im8 = ssm_bwd(dyg, y, u, xr, xi, *ab, *bb, *cb, d_skip)
    du_b = toks(du_b)
    gw["qkv"] = matmul(h, dqkv_b, "tn", F32, "d_w_qkv")
    gw["u"] = matmul(h, du_b, "tn", F32, "d_w_u")
    gw["gate"] = matmul(h, dgl_b, "tn", F32, "d_w_gate")
    dh = matmul(dqkv_b, w["qkv"], "nt", F32, "d_h_qkv")
    dh = matmul(du_b, w["u"], "nt", F32, "d_h_u", add=dh)
    dh = matmul(dgl_b, w["gate"], "nt", F32, "d_h_gate", add=dh)

    def norm_bwd_last(xv, g, dhv, skip):
        _, vjp = jax.vjp(_rms, xv, g)
        dx, dg = vjp(dhv)
        return dx + skip, dg

    grad_x, dg_mix = rowwise(norm_bwd_last, [x, g_mix, dh, dx1], [(D_MODEL, F32)], "norm_mix_bwd", accs=(D_MODEL,))
    g_ab_re = jnp.sum(da_re8, axis=0).reshape(32, 64)
    g_ab_im = jnp.sum(da_im8, axis=0).reshape(32, 64)
    d_lr, d_li, d_ldt, d_br_t, d_bi_t = ssm_prep_bwd(
        small["ssm_a_re"], small["ssm_a_im"], log_dt, br_t, bi_t,
        g_ab_re, g_ab_im, _block_diag_t(dbb_re), _block_diag_t(dbb_im))
    gs = {
        "norm_mix_g": dg_mix, "ssm_a_re": d_lr, "ssm_a_im": d_li, "ssm_log_dt": d_ldt.reshape(1, 32),
        "ssm_b_re": d_br_t.transpose(1, 2, 0), "ssm_b_im": d_bi_t.transpose(1, 2, 0),
        "ssm_c_re": _block_diag_t(dcb_re).transpose(1, 0, 2), "ssm_c_im": _block_diag_t(dcb_im).transpose(1, 0, 2),
        "ssm_d": dd.reshape(32, 16), "norm_ffn_g": dg_ffn, "norm_final_g": dg_final,
    }
    return loss, grad_x, gw, gs


ANY = pl.BlockSpec(memory_space=pl.ANY)
BIG = ("w_in", "w_glu", "w_attn_out", "w_out", "w_ffn_gate", "w_ffn_up", "w_ffn_down")
ROW_SHARDED = ("w_out", "w_ffn_down")
SMALL = ("norm_mix_g", "ssm_a_re", "ssm_a_im", "ssm_log_dt", "ssm_b_re", "ssm_b_im", "ssm_c_re", "ssm_c_im",
         "ssm_d", "norm_ffn_g", "norm_final_g")
WEIGHTS = ("norm_mix_g", "w_in", "ssm_a_re", "ssm_a_im", "ssm_log_dt", "ssm_b_re", "ssm_b_im", "ssm_c_re",
           "ssm_c_im", "ssm_d", "w_glu", "w_attn_out", "w_out", "norm_ffn_g", "w_ffn_gate", "w_ffn_up",
           "w_ffn_down", "norm_final_g")
SMALL_ROWS = 1088
N_BIG = len(BIG)


def _position():
    return lax.axis_index("x"), lax.axis_index("y"), lax.axis_index("c")


def _other_chips(x, y):
    return [(1 - x, y), (x, 1 - y), (1 - x, 1 - y)]


def _remote(src, dst, send_sem, recv_sem, device):
    return pltpu.make_async_remote_copy(src_ref=src, dst_ref=dst, send_sem=send_sem, recv_sem=recv_sem,
                                        device_id=device, device_id_type=MESH)


def _half(ref, chip, which):
    rows = ref.shape[1] // 2
    return ref.at[chip, pl.ds(which * rows, rows), :]


def allgather_weights(shards):
    def body(*refs):
        srcs, outs = refs[:N_BIG], refs[N_BIG:2 * N_BIG]
        send_sems, recv_sems, local_sems = refs[2 * N_BIG:]
        x, y, c = _position()
        me = 2 * x + y
        sibling = (x, y, 1 - c)
        chips = _other_chips(x, y)
        locals_ = [pltpu.make_async_copy(s, o.at[me], local_sems.at[i]) for i, (s, o) in enumerate(zip(srcs, outs))]
        for cp in locals_:
            cp.start()
        sends = []
        for i, (s, o) in enumerate(zip(srcs, outs)):
            rows = s.shape[0] // 2
            for j, (px, py) in enumerate(chips):
                sends.append(_remote(s.at[pl.ds(c * rows, rows), :], _half(o, me, c), send_sems.at[i, j],
                                     recv_sems.at[i, j], (px, py, c)))
        for cp in sends:
            cp.start()
        passed = []
        for j, (px, py) in enumerate(chips):
            for i, o in enumerate(outs):
                got = _half(o, 2 * px + py, c)
                _remote(got, got, send_sems.at[i, j], recv_sems.at[i, j], (px, py, c)).wait_recv()
                cp = _remote(got, got, send_sems.at[i, 3 + j], recv_sems.at[i, 3 + j], sibling)
                cp.start()
                passed.append(cp)
        for j, (px, py) in enumerate(chips):
            for i, o in enumerate(outs):
                got = _half(o, 2 * px + py, 1 - c)
                _remote(got, got, send_sems.at[i, 3 + j], recv_sems.at[i, 3 + j], sibling).wait_recv()
        for cp in sends + passed:
            cp.wait_send()
        for cp in locals_:
            cp.wait()

    return pl.pallas_call(
        body, name="allgather_weights", in_specs=[ANY] * N_BIG, out_specs=[ANY] * N_BIG,
        out_shape=[jax.ShapeDtypeStruct((N_CHIPS,) + s.shape, s.dtype) for s in shards],
        scratch_shapes=[pltpu.SemaphoreType.DMA((N_BIG, 6)), pltpu.SemaphoreType.DMA((N_BIG, 6)),
                        pltpu.SemaphoreType.DMA((N_BIG,))],
    )(*shards)


def swap_halves(grads):
    def body(*refs):
        srcs, gots = refs[:N_BIG], refs[N_BIG:2 * N_BIG]
        send_sems, recv_sems = refs[2 * N_BIG:]
        x, y, c = _position()
        copies = []
        for i, (s, o) in enumerate(zip(srcs, gots)):
            rows = s.shape[1] // 2
            copies.append(_remote(s.at[:, pl.ds((1 - c) * rows, rows), :], o, send_sems.at[i], recv_sems.at[i],
                                  (x, y, 1 - c)))
        for cp in copies:
            cp.start()
        for cp in copies:
            cp.wait()

    return pl.pallas_call(
        body, name="swap_halves", in_specs=[ANY] * N_BIG, out_specs=[ANY] * N_BIG,
        out_shape=[jax.ShapeDtypeStruct((N_CHIPS, g.shape[1] // 2, g.shape[2]), g.dtype) for g in grads],
        scratch_shapes=[pltpu.SemaphoreType.DMA((N_BIG,)), pltpu.SemaphoreType.DMA((N_BIG,))],
    )(*grads)


def add_halves(name, g, got, core):
    _, half, cols = got.shape
    mine = pl.BlockSpec((None, half, cols), lambda k, c_ref: (k, c_ref[0], 0))
    other = pl.BlockSpec((None, half, cols), lambda k, c_ref: (k, 0, 0))

    def body(c_ref, g_ref, got_ref, o_ref):
        o_ref[...] = (g_ref[...] + got_ref[...]).astype(BF16)

    return pl.pallas_call(
        body, name="add_halves_" + name,
        grid_spec=pltpu.PrefetchScalarGridSpec(num_scalar_prefetch=1, grid=(N_CHIPS,), in_specs=[mine, other],
                                               out_specs=other),
        out_shape=jax.ShapeDtypeStruct(got.shape, BF16),
        compiler_params=_params(("parallel",)),
    )(core, g, got)


def exchange_chips(parts):
    def body(*refs):
        srcs, outs = refs[:N_BIG], refs[N_BIG:2 * N_BIG]
        send_sems, recv_sems, local_sems = refs[2 * N_BIG:]
        x, y, c = _position()
        me = 2 * x + y
        sibling = (x, y, 1 - c)
        chips = _other_chips(x, y)
        locals_, sends = [], []
        for i, (s, o) in enumerate(zip(srcs, outs)):
            locals_.append(pltpu.make_async_copy(s.at[me], _half(o, me, c), local_sems.at[i]))
            for j, (px, py) in enumerate(chips):
                sends.append(_remote(s.at[2 * px + py], _half(o, me, c), send_sems.at[i, j], recv_sems.at[i, j],
                                     (px, py, c)))
            sends.append(_remote(s.at[me], _half(o, me, c), send_sems.at[i, 3], recv_sems.at[i, 3], sibling))
        for cp in locals_ + sends:
            cp.start()
        for j, (px, py) in enumerate(chips):
            for i, o in enumerate(outs):
                got = _half(o, 2 * px + py, c)
                _remote(got, got, send_sems.at[i, j], recv_sems.at[i, j], (px, py, c)).wait_recv()
                cp = _remote(got, got, send_sems.at[i, 4 + j], recv_sems.at[i, 4 + j], sibling)
                cp.start()
                sends.append(cp)
        for i, o in enumerate(outs):
            got = _half(o, me, 1 - c)
            _remote(got, got, send_sems.at[i, 3], recv_sems.at[i, 3], sibling).wait_recv()
            for j, (px, py) in enumerate(chips):
                got = _half(o, 2 * px + py, 1 - c)
                _remote(got, got, send_sems.at[i, 4 + j], recv_sems.at[i, 4 + j], sibling).wait_recv()
        for cp in sends:
            cp.wait_send()
        for cp in locals_:
            cp.wait()

    return pl.pallas_call(
        body, name="exchange_chips", in_specs=[ANY] * N_BIG, out_specs=[ANY] * N_BIG,
        out_shape=[jax.ShapeDtypeStruct((N_CHIPS, 2 * p.shape[1], p.shape[2]), p.dtype) for p in parts],
        scratch_shapes=[pltpu.SemaphoreType.DMA((N_BIG, 7)), pltpu.SemaphoreType.DMA((N_BIG, 7)),
                        pltpu.SemaphoreType.DMA((N_BIG,))],
    )(*parts)


def allgather_small(pack):
    def body(src_ref, out_ref, send_sems, recv_sems, local_sem):
        x, y, c = _position()
        me = 4 * x + 2 * y + c
        local = pltpu.make_async_copy(src_ref, out_ref.at[me], local_sem)
        local.start()
        flips = [(fx, fy, fc) for fx in (0, 1) for fy in (0, 1) for fc in (0, 1)][1:]
        peers = [(1 - x if fx else x, 1 - y if fy else y, 1 - c if fc else c) for fx, fy, fc in flips]
        sends = [_remote(src_ref, out_ref.at[me], send_sems.at[j], recv_sems.at[j], peer)
                 for j, peer in enumerate(peers)]
        for cp in sends:
            cp.start()
        for j, (px, py, pc) in enumerate(peers):
            got = out_ref.at[4 * px + 2 * py + pc]
            _remote(got, got, send_sems.at[j], recv_sems.at[j], (px, py, pc)).wait_recv()
        for cp in sends:
            cp.wait_send()
        local.wait()

    return pl.pallas_call(
        body, name="allgather_small", in_specs=[ANY], out_specs=ANY,
        out_shape=jax.ShapeDtypeStruct((N_DEV,) + pack.shape, pack.dtype),
        scratch_shapes=[pltpu.SemaphoreType.DMA((7,)), pltpu.SemaphoreType.DMA((7,)), pltpu.SemaphoreType.DMA],
    )(pack)


def _adam_fn(w, g, m, v):
    m = ADAM_B1 * m + (1.0 - ADAM_B1) * g
    v = ADAM_B2 * v + (1.0 - ADAM_B2) * jnp.square(g)
    m_hat = m / (1.0 - ADAM_B1 ** ADAM_STEP)
    v_hat = v / (1.0 - ADAM_B2 ** ADAM_STEP)
    return -ADAM_LR * (m_hat / (jnp.sqrt(v_hat) + ADAM_EPS) + ADAM_WD * w), m, v


def adam_big(name, parts, w, m, v):
    rows, cols = w.shape
    tm = 256 if rows % 256 == 0 else rows // 2

    def fn(p0, p1, p2, p3, wv, mv, vv):
        g = ((p0.astype(F32) + p1.astype(F32)) + p2.astype(F32)) + p3.astype(F32)
        return (g,) + _adam_fn(wv, g, mv, vv)

    return rowwise(fn, [parts, w, m, v], [(cols, F32)] * 4, "adam_" + name, tm=tm, rows=rows)


def adam_small(gathered, w, m, v):
    def body(g_ref, w_ref, m_ref, v_ref, go_ref, d_ref, mo_ref, vo_ref):
        g = g_ref[0]
        for k in range(1, N_DEV):
            g = g + g_ref[k]
        go_ref[...] = g
        d_ref[...], mo_ref[...], vo_ref[...] = _adam_fn(w_ref[...], g, m_ref[...], v_ref[...])

    return pl.pallas_call(body, name="adam_small", out_shape=[jax.ShapeDtypeStruct(w.shape, F32)] * 4,
                          compiler_params=_params())(gathered, w, m, v)


def _pack_small(vals, last=None):
    flat = [vals[n].reshape(-1) for n in SMALL]
    if last is not None:
        flat.append(last.reshape(-1))
    flat = jnp.concatenate(flat)
    return jnp.pad(flat, (0, SMALL_ROWS * LANES - flat.shape[0])).reshape(SMALL_ROWS, LANES)


def _unpack_small(pack, shapes):
    flat, out, off = pack.reshape(-1), {}, 0
    for n in SMALL:
        size = math.prod(shapes[n])
        out[n] = flat[off:off + size].reshape(shapes[n])
        off += size
    return out, flat[off]


def _to_slots(name, g, shard_shape):
    rows, cols = shard_shape
    if name in ROW_SHARDED:
        return g.reshape(N_CHIPS, rows, cols)
    return g.reshape(rows, N_CHIPS, cols).transpose(1, 0, 2)


def _from_slots(name, s):
    _, rows, cols = s.shape
    if name in ROW_SHARDED:
        return s.reshape(N_CHIPS * rows, cols)
    return s.transpose(1, 0, 2).reshape(rows, N_CHIPS * cols)


def kernel(x, norm_mix_g, w_in, ssm_a_re, ssm_a_im, ssm_log_dt, ssm_b_re, ssm_b_im, ssm_c_re, ssm_c_im, ssm_d, w_glu, w_attn_out, w_out, norm_ffn_g, w_ffn_gate, w_ffn_up, w_ffn_down, norm_final_g, loss_target, m_norm_mix_g, m_w_in, m_ssm_a_re, m_ssm_a_im, m_ssm_log_dt, m_ssm_b_re, m_ssm_b_im, m_ssm_c_re, m_ssm_c_im, m_ssm_d, m_w_glu, m_w_attn_out, m_w_out, m_norm_ffn_g, m_w_ffn_gate, m_w_ffn_up, m_w_ffn_down, m_norm_final_g, v_norm_mix_g, v_w_in, v_ssm_a_re, v_ssm_a_im, v_ssm_log_dt, v_ssm_b_re, v_ssm_b_im, v_ssm_c_re, v_ssm_c_im, v_ssm_d, v_w_glu, v_w_attn_out, v_w_out, v_norm_ffn_g, v_w_ffn_gate, v_w_ffn_up, v_w_ffn_down, v_norm_final_g):
    given = dict(locals())
    shard = {n: given[n][0] for n in BIG}
    shapes = {n: given[n].shape for n in WEIGHTS}

    gathered = allgather_weights([shard[n].astype(BF16) for n in BIG])
    full = {n: _from_slots(n, s) for n, s in zip(BIG, gathered)}
    w = {"qkv": _qkv_order(full["w_in"][:, :QKV_W]), "u": full["w_in"][:, QKV_W:QKV_W + SSM_W],
         "gate": full["w_in"][:, QKV_W + SSM_W:], "glu": full["w_glu"], "attn_out": full["w_attn_out"],
         "out": full["w_out"], "ffn_gate": full["w_ffn_gate"], "ffn_up": full["w_ffn_up"],
         "ffn_down": full["w_ffn_down"]}
    small = {n: given[n] for n in SMALL}
    small_2d = dict(small)
    for n in ("ssm_a_re", "ssm_a_im", "ssm_b_re", "ssm_b_im", "ssm_c_re", "ssm_c_im", "ssm_d"):
        small_2d[n] = small[n][0]
    small_2d["norm_final_g"] = norm_final_g.reshape(1, D_MODEL)

    loss, grad_x, gw, gs = local_step(x.reshape(TOKENS, D_MODEL), loss_target.reshape(TOKENS, D_MODEL), w, small_2d)

    share = _pack_small({n: gs[n] for n in SMALL}, last=loss)
    everyone = allgather_small(share)
    packs = [_pack_small({n: given[p + n] for n in SMALL}) for p in ("", "m_", "v_")]
    small_out = [_unpack_small(t, shapes) for t in adam_small(everyone, *packs)]
    total_loss = small_out[0][1][()]

    gfull = {"w_in": jnp.concatenate([_qkv_order(gw["qkv"], back=True), gw["u"], gw["gate"]], axis=1), "w_glu": gw["glu"],
             "w_attn_out": gw["attn_out"], "w_out": gw["out"], "w_ffn_gate": gw["ffn_gate"],
             "w_ffn_up": gw["ffn_up"], "w_ffn_down": gw["ffn_down"]}
    slots = [_to_slots(n, gfull[n], shard[n].shape) for n in BIG]
    core = lax.axis_index("c").astype(jnp.int32).reshape(1)
    chip_sums = [add_halves(n, g, got, core) for n, g, got in zip(BIG, slots, swap_halves(slots))]
    big_out = {}
    for n, parts in zip(BIG, exchange_chips(chip_sums)):
        big_out[n] = [t[None] for t in adam_big(n, parts, shard[n], given["m_" + n][0], given["v_" + n][0])]

    outs = [total_loss, grad_x.reshape(LOCAL_BATCH, SEQ, D_MODEL)]
    for kind in range(4):
        for n in WEIGHTS:
            outs.append(big_out[n][kind] if n in BIG else small_out[kind][0][n])
    return tuple(outs)
```

```python
import functools
import math

import jax
import jax.numpy as jnp
from jax import lax
from jax.experimental import pallas as pl
from jax.experimental.pallas import tpu as pltpu

F32 = jnp.float32
BF16 = jnp.bfloat16
MESH = pl.DeviceIdType.MESH

D_MODEL = 1024
SEQ = 2048
LOCAL_BATCH = 2
TOKENS = LOCAL_BATCH * SEQ
HEAD_DIM = 64
HEADS_PER_GROUP = 4
GROUP_W = HEADS_PER_GROUP * HEAD_DIM
N_GROUPS = 3
DILATIONS = (1, 4, 16)
ATTN_BLOCK = 128
ROPE_DIM = 16
ROPE_THETA = 500000.0
QKV_W = 3 * N_GROUPS * GROUP_W
SSM_W = 512
SSM_STATE_W = 2048
SSM_LANE_BLOCKS = 4
GATE_W = 2 * D_MODEL
D_FF = 2816
RMS_EPS = 1e-6
NEG_INF = -1e30
ADAM_LR, ADAM_B1, ADAM_B2, ADAM_EPS, ADAM_WD, ADAM_STEP = 0.001, 0.9, 0.999, 1e-08, 0.01, 10
N_CHIPS = 4
N_DEV = 8

VMEM_LIMIT = 56 * 1024 * 1024
LANES = 128


def _params(sem=None):
    return pltpu.CompilerParams(dimension_semantics=sem, vmem_limit_bytes=VMEM_LIMIT)


def _pick(n, cap, align=LANES):
    best = None
    for d in range(align, min(n, cap) + 1, align):
        if n % d == 0:
            best = d
    return n if best is None or n <= cap else best


_DIMS = {"nn": (((1,), (0,)), ((), ())), "nt": (((1,), (1,)), ((), ())), "tn": (((0,), (0,)), ((), ()))}


def _dot(a, b, mode):
    return lax.dot_general(a, b, _DIMS[mode], preferred_element_type=F32)


def matmul(a, b, mode, out_dtype, name, add=None):
    if mode == "nn":
        (m, k), n = a.shape, b.shape[1]
    elif mode == "nt":
        (m, k), n = a.shape, b.shape[0]
    else:
        (k, m), n = a.shape, b.shape[1]
    tn = _pick(n, 1408)
    tk = _pick(k, 2816) if mode != "tn" else _pick(k, 1024)
    tm = _pick(m, 1024)
    out_bytes = jnp.dtype(out_dtype).itemsize

    def need(tm_):
        return 2 * 2 * (tm_ * tk + tk * tn) + tm_ * tn * (4 + 2 * out_bytes + (8 if add is not None else 0))

    while need(tm) > 40 * 1024 * 1024 and tm % 256 == 0:
        tm //= 2
    nk = k // tk
    a_spec = {"nn": pl.BlockSpec((tm, tk), lambda i, j, kk: (i, kk)),
              "nt": pl.BlockSpec((tm, tk), lambda i, j, kk: (i, kk)),
              "tn": pl.BlockSpec((tk, tm), lambda i, j, kk: (kk, i))}[mode]
    b_spec = {"nn": pl.BlockSpec((tk, tn), lambda i, j, kk: (kk, j)),
              "nt": pl.BlockSpec((tn, tk), lambda i, j, kk: (j, kk)),
              "tn": pl.BlockSpec((tk, tn), lambda i, j, kk: (kk, j))}[mode]
    o_spec = pl.BlockSpec((tm, tn), lambda i, j, kk: (i, j))

    def body(a_ref, b_ref, *rest):
        if add is not None:
            add_ref, o_ref, acc_ref = rest
        else:
            o_ref, acc_ref = rest
        part = _dot(a_ref[...], b_ref[...], mode)
        if nk == 1:
            res = part if add is None else part + add_ref[...]
            o_ref[...] = res.astype(out_dtype)
            return
        kk = pl.program_id(2)

        @pl.when(kk == 0)
        def _():
            acc_ref[...] = part

        @pl.when(kk > 0)
        def _():
            acc_ref[...] += part

        @pl.when(kk == nk - 1)
        def _():
            res = acc_ref[...] if add is None else acc_ref[...] + add_ref[...]
            o_ref[...] = res.astype(out_dtype)

    in_specs = [a_spec, b_spec] + ([o_spec] if add is not None else [])
    args = (a, b) + ((add,) if add is not None else ())
    return pl.pallas_call(
        body, name=name, grid=(m // tm, n // tn, nk), in_specs=in_specs, out_specs=o_spec,
        out_shape=jax.ShapeDtypeStruct((m, n), out_dtype),
        scratch_shapes=[pltpu.VMEM((tm, tn) if nk > 1 else (8, LANES), F32)],
        compiler_params=_params(("parallel", "parallel", "arbitrary")),
    )(*args)


def rowwise(fn, ins, outs, name, accs=(), tm=256, rows=TOKENS, comm=None):
    in_specs, args = [], []
    for item in ins:
        arr, width, blk = item if isinstance(item, tuple) else (item, None, 0)
        if arr.ndim == 3:
            for k in range(arr.shape[0]):
                in_specs.append(pl.BlockSpec((None, tm, arr.shape[2]), functools.partial(lambda i, k_: (k_, i, 0), k_=k)))
                args.append(arr)
            continue
        if arr.shape[0] == 1:
            in_specs.append(pl.BlockSpec(arr.shape, lambda i: (0, 0)))
        elif width is None:
            in_specs.append(pl.BlockSpec((tm, arr.shape[1]), lambda i: (i, 0)))
        else:
            in_specs.append(pl.BlockSpec((tm, width), functools.partial(lambda i, blk_: (i, blk_), blk_=blk)))
        args.append(arr)
    out_specs = [pl.BlockSpec((tm, c), lambda i: (i, 0)) for c, _ in outs]
    out_specs += [pl.BlockSpec((1, c), lambda i: (0, 0)) for c in accs]
    out_shape = [jax.ShapeDtypeStruct((rows, c), dt) for c, dt in outs]
    out_shape += [jax.ShapeDtypeStruct((1, c), F32) for c in accs]
    n_in, n_out = len(args), len(outs)
    c_ins, c_outs, c_sems = _comm_operands(comm)

    def body(*refs):
        refs, c_refs = _comm_refs(comm, refs, n_in, n_out + len(accs))
        step = pl.program_id(0)
        _comm_begin(comm, c_refs, step, rows // tm)
        res = fn(*[r[...] for r in refs[:n_in]])
        for r, v in zip(refs[n_in:n_in + n_out], res[:n_out]):
            r[...] = v.astype(r.dtype)
        first = step == 0
        for r, v in zip(refs[n_in + n_out:], res[n_out:]):
            @pl.when(first)
            def _(r=r, v=v):
                r[...] = v

            @pl.when(jnp.logical_not(first))
            def _(r=r, v=v):
                r[...] += v
        _comm_end(comm, c_refs, step, rows // tm)

    return pl.pallas_call(
        body, name=name, grid=(rows // tm,), in_specs=in_specs + [ANY] * len(c_ins),
        out_specs=out_specs + [ANY] * len(c_outs), out_shape=out_shape + c_outs, scratch_shapes=c_sems,
        compiler_params=_params(("arbitrary",)),
    )(*args, *c_ins)


def _rms(x, g):
    return x * lax.rsqrt(jnp.mean(x * x, axis=-1, keepdims=True) + RMS_EPS) * g


def _colsum(v):
    return jnp.sum(v, axis=0, keepdims=True)


PAIR_W = 2 * HEAD_DIM
N_PAIRS = HEADS_PER_GROUP // 2


def _qkv_order(w, back=False):
    rows = w.shape[0]
    dims = (N_PAIRS, N_GROUPS, 3) if back else (3, N_GROUPS, N_PAIRS)
    return w.reshape((rows,) + dims + (PAIR_W,)).transpose(0, 3, 2, 1, 4).reshape(rows, QKV_W)


def _rope_tables():
    half = ROPE_DIM // 2
    inv = jnp.power(jnp.float32(ROPE_THETA), -jnp.arange(half, dtype=F32) * 2.0 / ROPE_DIM)
    ang = jnp.arange(SEQ, dtype=F32)[:, None] * inv[None, :]
    cos, sin = jnp.cos(ang), jnp.sin(ang)
    zeros = jnp.zeros((SEQ, HEAD_DIM - ROPE_DIM), F32)
    zh = jnp.zeros((SEQ, half), F32)
    c = jnp.concatenate([cos, cos, zeros + 1.0], axis=1)
    sa = jnp.concatenate([-sin, zh, zeros], axis=1)
    sb = jnp.concatenate([zh, sin, zeros], axis=1)
    return [jnp.tile(t, (1, 2)) for t in (c, sa, sb)]


def _rope_fwd(x, c, sa, sb):
    return x * c + pltpu.roll(x, PAIR_W - 8, 1) * sa + pltpu.roll(x, 8, 1) * sb


def _rope_bwd(dy, c, sa, sb):
    return dy * c + pltpu.roll(dy * sb, PAIR_W - 8, 1) + pltpu.roll(dy * sa, 8, 1)


def _band_masks():
    row = lax.broadcasted_iota(jnp.int32, (ATTN_BLOCK, ATTN_BLOCK), 0)
    col = lax.broadcasted_iota(jnp.int32, (ATTN_BLOCK, ATTN_BLOCK), 1)
    return col <= row, col >= row


def _per_head(fn):
    return jnp.concatenate([fn(slice(h * HEAD_DIM, (h + 1) * HEAD_DIM)) for h in range(2)], axis=1)


def _slab_spec(kind):
    return pl.BlockSpec((None, SEQ, PAIR_W), lambda b, p, g: (b, 0, p * 3 * N_GROUPS + g * 3 + kind))


_TABLE_SPEC = pl.BlockSpec((SEQ, PAIR_W), lambda b, p, g: (0, 0))
_PAIR_SPEC = pl.BlockSpec((None, SEQ, PAIR_W), lambda b, p, g: (b, 0, p))


def _block_rows(dil, r, n):
    return pl.ds(n * (ATTN_BLOCK * dil) + r, ATTN_BLOCK, stride=dil)


def attn_fwd(qkv, tables, comm=None):
    scale = HEAD_DIM ** -0.5

    def body(q_ref, k_ref, v_ref, c_ref, sa_ref, sb_ref, attn_b_ref, attn_ref, lse_ref, qs, ks, o0, o1, o2, l0, l1, l2):
        g = pl.program_id(2)
        c, sa, sb = c_ref[...], sa_ref[...], sb_ref[...]
        qs[...] = _rope_fwd(q_ref[...], c, sa, sb) * scale
        ks[...] = _rope_fwd(k_ref[...], c, sa, sb)
        cur_mask, prev_mask = _band_masks()
        first_head = lax.broadcasted_iota(jnp.int32, (ATTN_BLOCK, PAIR_W), 1) < HEAD_DIM

        def run(dil, o_slab, l_slab):
            nb = SEQ // dil // ATTN_BLOCK

            def block(idx, carry):
                r, n = lax.div(idx, nb), lax.rem(idx, nb)
                cur, prev = _block_rows(dil, r, n), _block_rows(dil, r, jnp.maximum(n - 1, 0))
                q = qs[cur, :].astype(BF16)
                kc, kp = ks[cur, :].astype(BF16), ks[prev, :].astype(BF16)
                vc, vp = v_ref[cur, :].astype(BF16), v_ref[prev, :].astype(BF16)
                pmask = jnp.logical_and(prev_mask, n > 0)
                res, ms = [], []
                for h in range(2):
                    mine = first_head if h == 0 else jnp.logical_not(first_head)
                    qh = jnp.where(mine, q, 0)
                    sc = jnp.where(cur_mask, _dot(qh, kc, "nt"), NEG_INF)
                    sp = jnp.where(pmask, _dot(qh, kp, "nt"), NEG_INF)
                    m = jnp.max(jnp.maximum(sc, sp), axis=-1, keepdims=True)
                    pc, pp = jnp.exp(sc - m).astype(BF16), jnp.exp(sp - m).astype(BF16)
                    res.append(_dot(pc, jnp.where(mine, vc, 1), "nn") + _dot(pp, jnp.where(mine, vp, 1), "nn"))
                    ms.append(m)
                num = jnp.where(first_head, res[0], res[1])
                den = pltpu.roll(jnp.where(first_head, res[1], res[0]), HEAD_DIM, 1)
                o_slab[cur, :] = num / den
                l_slab[cur, :] = jnp.where(first_head, ms[0], ms[1]) + jnp.log(den)
                return carry

            lax.fori_loop(0, SEQ // ATTN_BLOCK, block, 0, unroll=2)

        for gi, (o_slab, l_slab) in enumerate(((o0, l0), (o1, l1), (o2, l2))):
            @pl.when(g == gi)
            def _(gi=gi, o_slab=o_slab, l_slab=l_slab):
                run(DILATIONS[gi], o_slab, l_slab)

        @pl.when(g == N_GROUPS - 1)
        def _():
            a, b, cc = l0[...], l1[...], l2[...]
            m = jnp.maximum(jnp.maximum(a, b), cc)
            e0, e1, e2 = jnp.exp(a - m), jnp.exp(b - m), jnp.exp(cc - m)
            tot = e0 + e1 + e2
            attn = (e0 * o0[...] + e1 * o1[...] + e2 * o2[...]) / tot
            attn_ref[...] = attn
            attn_b_ref[...] = attn.astype(BF16)
            lse_ref[...] = m + jnp.log(tot)

    shape = (LOCAL_BATCH, SEQ, GROUP_W)
    slab = pltpu.VMEM((SEQ, PAIR_W), F32)
    return hosted_call(
        body, comm, "attn_fwd", (LOCAL_BATCH, N_PAIRS, N_GROUPS),
        [_slab_spec(0), _slab_spec(1), _slab_spec(2), _TABLE_SPEC, _TABLE_SPEC, _TABLE_SPEC], [_PAIR_SPEC] * 3,
        [jax.ShapeDtypeStruct(shape, BF16), jax.ShapeDtypeStruct(shape, F32), jax.ShapeDtypeStruct(shape, F32)],
        [slab] * 8, (qkv, qkv, qkv, *tables), ("parallel", "parallel", "arbitrary"))


def attn_bwd(qkv, tables, dattn, attn, lse, comm=None):
    scale = HEAD_DIM ** -0.5

    def body(q_ref, k_ref, v_ref, c_ref, sa_ref, sb_ref, do_ref, out_ref, lse_ref, dqkv_ref, qs, ks, dl, dq_s, dk_s, dv_s):
        g = pl.program_id(2)
        c, sa, sb = c_ref[...], sa_ref[...], sb_ref[...]
        qs[...] = _rope_fwd(q_ref[...], c, sa, sb) * scale
        ks[...] = _rope_fwd(k_ref[...], c, sa, sb)
        prod = do_ref[...] * out_ref[...]
        dl[...] = _per_head(lambda sl: jnp.broadcast_to(jnp.sum(prod[:, sl], axis=-1, keepdims=True), (SEQ, HEAD_DIM)))
        cur_mask, prev_mask = _band_masks()
        first_head = lax.broadcasted_iota(jnp.int32, (ATTN_BLOCK, PAIR_W), 1) < HEAD_DIM

        def run(dil):
            nb = SEQ // dil // ATTN_BLOCK

            def block(idx, carry):
                r, n = lax.div(idx, nb), lax.rem(idx, nb)
                cur = _block_rows(dil, r, n)
                prev = _block_rows(dil, r, jnp.maximum(n - 1, 0))
                nxt = _block_rows(dil, r, jnp.minimum(n + 1, nb - 1))
                q0, q1 = qs[cur, :].astype(BF16), qs[nxt, :].astype(BF16)
                kp, kc = ks[prev, :].astype(BF16), ks[cur, :].astype(BF16)
                vp, vc = v_ref[prev, :].astype(BF16), v_ref[cur, :].astype(BF16)
                do0, do1 = do_ref[cur, :].astype(BF16), do_ref[nxt, :].astype(BF16)
                lse0, lse1, dl0, dl1 = lse_ref[cur, :], lse_ref[nxt, :], dl[cur, :], dl[nxt, :]
                has_prev = jnp.logical_and(prev_mask, n > 0)
                has_next = jnp.logical_and(prev_mask, n < nb - 1)

                def pair(q, k, v, do, lse_col, delta, mask):
                    p = jnp.where(mask, jnp.exp(_dot(q, k, "nt") - lse_col), 0.0)
                    ds = p * (_dot(do, v, "nt") - delta)
                    return p.astype(BF16), ds.astype(BF16)

                dqs, dks, dvs = [], [], []
                for h in range(2):
                    mine = first_head if h == 0 else jnp.logical_not(first_head)
                    one = slice(h * HEAD_DIM, h * HEAD_DIM + 1)
                    q0h, q1h = jnp.where(mine, q0, 0), jnp.where(mine, q1, 0)
                    do0h, do1h = jnp.where(mine, do0, 0), jnp.where(mine, do1, 0)
                    _, ds_a = pair(q0h, kp, vp, do0h, lse0[:, one], dl0[:, one], has_prev)
                    p_b, ds_b = pair(q0h, kc, vc, do0h, lse0[:, one], dl0[:, one], cur_mask)
                    p_c, ds_c = pair(q1h, kc, vc, do1h, lse1[:, one], dl1[:, one], has_next)
                    dqs.append(_dot(ds_a, kp, "nn") + _dot(ds_b, kc, "nn"))
                    dks.append(_dot(ds_b, q0h, "tn") + _dot(ds_c, q1h, "tn"))
                    dvs.append(_dot(p_b, do0h, "tn") + _dot(p_c, do1h, "tn"))
                dq_s[cur, :] = jnp.where(first_head, dqs[0], dqs[1])
                dk_s[cur, :] = dks[0] + dks[1]
                dv_s[cur, :] = dvs[0] + dvs[1]
                return carry

            lax.fori_loop(0, SEQ // ATTN_BLOCK, block, 0, unroll=2)

        for gi in range(N_GROUPS):
            @pl.when(g == gi)
            def _(gi=gi):
                run(DILATIONS[gi])

        dqkv_ref[:, 0:PAIR_W] = _rope_bwd(dq_s[...] * scale, c, sa, sb).astype(BF16)
        dqkv_ref[:, PAIR_W:2 * PAIR_W] = _rope_bwd(dk_s[...], c, sa, sb).astype(BF16)
        dqkv_ref[:, 2 * PAIR_W:] = dv_s[...].astype(BF16)

    slab = pltpu.VMEM((SEQ, PAIR_W), F32)
    return hosted_call(
        body, comm, "attn_bwd", (LOCAL_BATCH, N_PAIRS, N_GROUPS),
        [_slab_spec(0), _slab_spec(1), _slab_spec(2), _TABLE_SPEC, _TABLE_SPEC, _TABLE_SPEC,
         _PAIR_SPEC, _PAIR_SPEC, _PAIR_SPEC],
        [pl.BlockSpec((None, SEQ, 3 * PAIR_W), lambda b, p, g: (b, 0, p * N_GROUPS + g))],
        [jax.ShapeDtypeStruct((LOCAL_BATCH, SEQ, QKV_W), BF16)],
        [slab] * 6, (qkv, qkv, qkv, *tables, dattn, attn, lse), ("parallel", "parallel", "arbitrary"))


def _discretize(lr, li, log_dt, br, bi):
    dt = jnp.exp(log_dt)
    mag = jnp.exp(lr * dt)
    ab_re, ab_im = mag * jnp.cos(li * dt), mag * jnp.sin(li * dt)
    den = lr * lr + li * li
    nr, ni = ab_re - 1.0, ab_im
    f_re = (nr * lr + ni * li) / den
    f_im = (ni * lr - nr * li) / den
    return ab_re, ab_im, f_re[None] * br - f_im[None] * bi, f_re[None] * bi + f_im[None] * br


def ssm_prep(lr, li, log_dt, br, bi):
    def body(lr_ref, li_ref, dt_ref, br_ref, bi_ref, *outs):
        for o, v in zip(outs, _discretize(lr_ref[...], li_ref[...], dt_ref[...], br_ref[...], bi_ref[...])):
            o[...] = v
    shapes = [lr, li, br, bi]
    return pl.pallas_call(body, name="ssm_prep",
                          out_shape=[jax.ShapeDtypeStruct(s.shape, F32) for s in shapes])(lr, li, log_dt, br, bi)


def ssm_prep_bwd(lr, li, log_dt, br, bi, g_ab_re, g_ab_im, g_bb_re, g_bb_im):
    def body(lr_ref, li_ref, dt_ref, br_ref, bi_ref, g0, g1, g2, g3, *outs):
        _, vjp = jax.vjp(_discretize, lr_ref[...], li_ref[...], dt_ref[...], br_ref[...], bi_ref[...])
        for o, v in zip(outs, vjp((g0[...], g1[...], g2[...], g3[...]))):
            o[...] = v
    shapes = [lr, li, log_dt, br, bi]
    return pl.pallas_call(body, name="ssm_prep_bwd",
                          out_shape=[jax.ShapeDtypeStruct(s.shape, F32) for s in shapes])(
        lr, li, log_dt, br, bi, g_ab_re, g_ab_im, g_bb_re, g_bb_im)


def _block_diag(t):
    per = SSM_STATE_W // SSM_LANE_BLOCKS // 64
    g = t.transpose(1, 0, 2).reshape(SSM_LANE_BLOCKS, per, 16, 64)
    eye = jnp.eye(per, dtype=t.dtype)
    return jnp.einsum("jgcn,gh->jgchn", g, eye).reshape(SSM_LANE_BLOCKS, per * 16, per * 64)


def _block_diag_t(m):
    per = SSM_STATE_W // SSM_LANE_BLOCKS // 64
    m5 = m.reshape(SSM_LANE_BLOCKS, per, 16, per, 64)
    d = jnp.einsum("jgchn,gh->jgcn", m5, jnp.eye(per, dtype=m.dtype))
    return d.reshape(SSM_LANE_BLOCKS * per, 16, 64).transpose(1, 0, 2)


def _cmul(ar, ai, br, bi):
    return ar * br - ai * bi, ar * bi + ai * br


def _power_tables(ar, ai, reverse):
    width = ar.shape[1]
    row = lax.broadcasted_iota(jnp.int32, (8, width), 0)
    pows = [(ar, ai)]
    for _ in range(7):
        pows.append(_cmul(pows[-1][0], pows[-1][1], ar, ai))
    steps = []
    for k in (1, 2, 4):
        keep = (row >= k) if not reverse else (row < 8 - k)
        steps.append((jnp.where(keep, pows[k - 1][0], 0.0), jnp.where(keep, pows[k - 1][1], 0.0)))
    cr = jnp.zeros((8, width), F32)
    ci = jnp.zeros((8, width), F32)
    for i in range(8):
        pr, pi = pows[i] if not reverse else pows[7 - i]
        cr = jnp.where(row == i, pr, cr)
        ci = jnp.where(row == i, pi, ci)
    return steps, (cr, ci)


SCAN_CHUNK = 512
STATE_BLOCK = SSM_STATE_W // SSM_LANE_BLOCKS
CHAN_BLOCK = SSM_W // SSM_LANE_BLOCKS


def ssm_fwd(u, ab_re, ab_im, bb_re, bb_im, cb_re, cb_im, d_skip, comm=None):
    nt = SEQ // SCAN_CHUNK
    chan = pl.BlockSpec((None, SCAN_CHUNK, CHAN_BLOCK), lambda b, j, t: (b, t, j))
    state = pl.BlockSpec((None, SCAN_CHUNK, STATE_BLOCK), lambda b, j, t: (b, t, j))
    mat = pl.BlockSpec((None, CHAN_BLOCK, STATE_BLOCK), lambda b, j, t: (j, 0, 0))
    lane = pl.BlockSpec((1, STATE_BLOCK), lambda b, j, t: (0, j))
    dsp = pl.BlockSpec((1, CHAN_BLOCK), lambda b, j, t: (0, j))

    def body(u_ref, ar_ref, ai_ref, bbr_ref, bbi_ref, cbr_ref, cbi_ref, d_ref, y_ref, yg_ref, xr_ref, xi_ref,
             car_r, car_i):
        @pl.when(pl.program_id(2) == 0)
        def _():
            car_r[...] = jnp.zeros_like(car_r)
            car_i[...] = jnp.zeros_like(car_i)

        steps, (pr, pi) = _power_tables(ar_ref[...], ai_ref[...], reverse=False)
        uf = u_ref[...]
        ub = uf.astype(BF16)
        xr_ref[...] = _dot(ub, bbr_ref[...], "nn")
        xi_ref[...] = _dot(ub, bbi_ref[...], "nn")

        def tile(i, carry):
            cr, ci = carry
            sl = pl.ds(pl.multiple_of(i * 8, 8), 8)
            br, bi = xr_ref[sl, :], xi_ref[sl, :]
            for k, (sr, si) in zip((1, 2, 4), steps):
                tr, ti = _cmul(sr, si, pltpu.roll(br, k, 0), pltpu.roll(bi, k, 0))
                br, bi = br + tr, bi + ti
            tr, ti = _cmul(pr, pi, cr, ci)
            br, bi = br + tr, bi + ti
            xr_ref[sl, :] = br
            xi_ref[sl, :] = bi
            return br[7:8, :], bi[7:8, :]

        cr, ci = lax.fori_loop(0, SCAN_CHUNK // 8, tile, (car_r[0:1, :], car_i[0:1, :]), unroll=4)
        car_r[0:1, :] = cr
        car_i[0:1, :] = ci
        y = (_dot(xr_ref[...].astype(BF16), cbr_ref[...], "nt") - _dot(xi_ref[...].astype(BF16), cbi_ref[...], "nt")
             + d_ref[...] * uf)
        y_ref[...] = y
        yg_ref[...] = jax.nn.gelu(y).astype(BF16)

    return hosted_call(
        body, comm, "ssm_fwd", (LOCAL_BATCH, SSM_LANE_BLOCKS, nt),
        [chan, lane, lane, mat, mat, mat, mat, dsp], [chan, chan, state, state],
        [jax.ShapeDtypeStruct((LOCAL_BATCH, SEQ, SSM_W), F32), jax.ShapeDtypeStruct((LOCAL_BATCH, SEQ, SSM_W), BF16),
         jax.ShapeDtypeStruct((LOCAL_BATCH, SEQ, SSM_STATE_W), F32),
         jax.ShapeDtypeStruct((LOCAL_BATCH, SEQ, SSM_STATE_W), F32)],
        [pltpu.VMEM((8, STATE_BLOCK), F32), pltpu.VMEM((8, STATE_BLOCK), F32)],
        (u, ab_re, ab_im, bb_re, bb_im, cb_re, cb_im, d_skip), ("parallel", "parallel", "arbitrary"))


def ssm_bwd(dyg, y, u, xr, xi, ab_re, ab_im, bb_re, bb_im, cb_re, cb_im, d_skip, comm=None):
    nt = SEQ // SCAN_CHUNK
    ntile = SCAN_CHUNK // 8

    def rev(t):
        return nt - 1 - t

    chan = pl.BlockSpec((None, SCAN_CHUNK, CHAN_BLOCK), lambda j, b, t: (b, rev(t), j))
    state = pl.BlockSpec((None, SCAN_CHUNK, STATE_BLOCK), lambda j, b, t: (b, rev(t), j))
    before = pl.BlockSpec((None, 8, STATE_BLOCK), lambda j, b, t: (b, jnp.maximum(rev(t) * ntile - 1, 0), j))
    mat = pl.BlockSpec((None, CHAN_BLOCK, STATE_BLOCK), lambda j, b, t: (j, 0, 0))
    lane = pl.BlockSpec((1, STATE_BLOCK), lambda j, b, t: (0, j))
    lane8 = pl.BlockSpec((8, STATE_BLOCK), lambda j, b, t: (0, j))
    dsp = pl.BlockSpec((1, CHAN_BLOCK), lambda j, b, t: (0, j))

    def body(dyg_ref, y_ref, u_ref, xr_ref, xi_ref, xrb_ref, xib_ref, ar_ref, ai_ref, bbr_ref, bbi_ref, cbr_ref,
             cbi_ref, d_ref, du_ref, dcbr_ref, dcbi_ref, dbbr_ref, dbbi_ref, dd_ref, dar_ref, dai_ref,
             lam_r, lam_i, car_r, car_i):
        b, t = pl.program_id(1), pl.program_id(2)
        first = jnp.logical_and(b == 0, t == 0)

        @pl.when(t == 0)
        def _():
            car_r[...] = jnp.zeros_like(car_r)
            car_i[...] = jnp.zeros_like(car_i)

        @pl.when(first)
        def _():
            for r in (dcbr_ref, dcbi_ref, dbbr_ref, dbbi_ref, dd_ref, dar_ref, dai_ref):
                r[...] = jnp.zeros_like(r)

        steps, (pr, pi) = _power_tables(ar_ref[...], -ai_ref[...], reverse=True)
        uf = u_ref[...]
        _, gelu_vjp = jax.vjp(jax.nn.gelu, y_ref[...])
        dy = gelu_vjp(dyg_ref[...])[0]
        dyb = dy.astype(BF16)
        dd_ref[...] += _colsum(dy * uf)
        lam_r[...] = _dot(dyb, cbr_ref[...], "nn")
        lam_i[...] = -_dot(dyb, cbi_ref[...], "nn")
        dcbr_ref[...] += _dot(dyb, xr_ref[...].astype(BF16), "tn")
        dcbi_ref[...] -= _dot(dyb, xi_ref[...].astype(BF16), "tn")
        row0 = lax.broadcasted_iota(jnp.int32, (8, STATE_BLOCK), 0) == 0
        has_before = rev(t) > 0
        xrb = jnp.where(has_before, xrb_ref[...], 0.0)
        xib = jnp.where(has_before, xib_ref[...], 0.0)

        def tile(s, carry):
            cr, ci, acc_r, acc_i = carry
            i = ntile - 1 - s
            sl = pl.ds(pl.multiple_of(i * 8, 8), 8)
            gr, gi = lam_r[sl, :], lam_i[sl, :]
            for k, (sr, si) in zip((1, 2, 4), steps):
                tr, ti = _cmul(sr, si, pltpu.roll(gr, 8 - k, 0), pltpu.roll(gi, 8 - k, 0))
                gr, gi = gr + tr, gi + ti
            tr, ti = _cmul(pr, pi, cr, ci)
            gr, gi = gr + tr, gi + ti
            lam_r[sl, :] = gr
            lam_i[sl, :] = gi
            sp = pl.ds(pl.multiple_of(jnp.maximum(i - 1, 0) * 8, 8), 8)
            pvr = jnp.where(i > 0, xr_ref[sp, :], xrb)
            pvi = jnp.where(i > 0, xi_ref[sp, :], xib)
            xsr = jnp.where(row0, pltpu.roll(pvr, 1, 0), pltpu.roll(xr_ref[sl, :], 1, 0))
            xsi = jnp.where(row0, pltpu.roll(pvi, 1, 0), pltpu.roll(xi_ref[sl, :], 1, 0))
            acc_r = acc_r + xsr * gr + xsi * gi
            acc_i = acc_i + xsr * gi - xsi * gr
            return gr[0:1, :], gi[0:1, :], acc_r, acc_i

        zero = jnp.zeros((8, STATE_BLOCK), F32)
        cr, ci, acc_r, acc_i = lax.fori_loop(0, ntile, tile, (car_r[0:1, :], car_i[0:1, :], zero, zero), unroll=2)
        car_r[0:1, :] = cr
        car_i[0:1, :] = ci
        dar_ref[...] += acc_r
        dai_ref[...] += acc_i
        lrb, lib = lam_r[...].astype(BF16), lam_i[...].astype(BF16)
        du = _dot(lrb, bbr_ref[...], "nt") + _dot(lib, bbi_ref[...], "nt") + d_ref[...] * dy
        du_ref[...] = du.astype(BF16)
        ub = uf.astype(BF16)
        dbbr_ref[...] += _dot(ub, lrb, "tn")
        dbbi_ref[...] += _dot(ub, lib, "tn")

    mat_shape = jax.ShapeDtypeStruct((SSM_LANE_BLOCKS, CHAN_BLOCK, STATE_BLOCK), F32)
    return hosted_call(
        body, comm, "ssm_bwd", (SSM_LANE_BLOCKS, LOCAL_BATCH, nt),
        [chan, chan, chan, state, state, before, before, lane, lane, mat, mat, mat, mat, dsp],
        [chan, mat, mat, mat, mat, dsp, lane8, lane8],
        [jax.ShapeDtypeStruct((LOCAL_BATCH, SEQ, SSM_W), BF16), mat_shape, mat_shape, mat_shape, mat_shape,
         jax.ShapeDtypeStruct((1, SSM_W), F32), jax.ShapeDtypeStruct((8, SSM_STATE_W), F32),
         jax.ShapeDtypeStruct((8, SSM_STATE_W), F32)],
        [pltpu.VMEM((SCAN_CHUNK, STATE_BLOCK), F32), pltpu.VMEM((SCAN_CHUNK, STATE_BLOCK), F32),
         pltpu.VMEM((8, STATE_BLOCK), F32), pltpu.VMEM((8, STATE_BLOCK), F32)],
        (dyg, y, u, xr, xi, xr, xi, ab_re, ab_im, bb_re, bb_im, cb_re, cb_im, d_skip),
        ("parallel", "arbitrary", "arbitrary"))


def _merge_fn(g0, g1, attn_d, za, zb):
    return jax.nn.sigmoid(g0) * attn_d + jax.nn.sigmoid(g1) * (za * jax.nn.sigmoid(zb))


def _swiglu_fn(a, b):
    return jax.nn.silu(a) * b


def _reduce_start(names, gw, shard_shapes):
    return swap_comm([_to_slots(n, gw[n], shard_shapes[n]) for n in names])


def _reduce_chip(names, swap, got, core):
    return exchange_comm([add_halves(n, g, r, core) for n, g, r in zip(names, swap.ins, got)])


def local_step(x, target, shards, small, core):
    g_mix, g_ffn, g_final = small["norm_mix_g"], small["norm_ffn_g"], small["norm_final_g"]
    tables = _rope_tables()
    seqs = lambda t: t.reshape(LOCAL_BATCH, SEQ, t.shape[-1])
    toks = lambda t: t.reshape(TOKENS, t.shape[-1])
    shard_shapes = {n: s.shape for n, s in shards.items()}
    w = {}

    def gather(names):
        return gather_comm([shards[n] for n in names])

    def arrived(names, slots):
        for n, s in zip(names, slots):
            w[n] = _from_slots(n, s)

    arrived(["w_in"], run_comm(gather(["w_in"]), "gather_w_in"))
    w_qkv, w_u, w_gate = _qkv_order(w["w_in"][:, :QKV_W]), w["w_in"][:, QKV_W:QKV_W + SSM_W], w["w_in"][:, QKV_W + SSM_W:]
    (h,) = rowwise(lambda xv, g: (_rms(xv, g),), [x, g_mix], [(D_MODEL, BF16)], "norm_mix")
    qkv = seqs(matmul(h, w_qkv, "nn", F32, "proj_qkv"))
    u = seqs(matmul(h, w_u, "nn", F32, "proj_u"))
    gl = matmul(h, w_gate, "nn", F32, "proj_gate")
    first = ["w_attn_out", "w_ffn_gate", "w_ffn_up"]
    attn_b, attn, lse, *slots = attn_fwd(qkv, tables, comm=gather(first))
    arrived(first, slots)
    attn_b = toks(attn_b)
    attn_d = matmul(attn_b, w["w_attn_out"], "nn", F32, "attn_out")

    br_t = small["ssm_b_re"].transpose(2, 0, 1)
    bi_t = small["ssm_b_im"].transpose(2, 0, 1)
    log_dt = small["ssm_log_dt"].reshape(32, 1)
    ab_re, ab_im, bb_re_t, bb_im_t = ssm_prep(small["ssm_a_re"], small["ssm_a_im"], log_dt, br_t, bi_t)
    ab = [ab_re.reshape(1, SSM_STATE_W), ab_im.reshape(1, SSM_STATE_W)]
    bb = [_block_diag(bb_re_t).astype(BF16), _block_diag(bb_im_t).astype(BF16)]
    cb = [_block_diag(small["ssm_c_re"].transpose(1, 0, 2)).astype(BF16),
          _block_diag(small["ssm_c_im"].transpose(1, 0, 2)).astype(BF16)]
    d_skip = small["ssm_d"].reshape(1, SSM_W)
    second = ["w_glu", "w_out", "w_ffn_down"]
    y, yg, xr, xi, *slots = ssm_fwd(u, *ab, *bb, *cb, d_skip, comm=gather(second))
    arrived(second, slots)
    yg2 = toks(yg)
    z = matmul(yg2, w["w_glu"], "nn", F32, "glu")
    gate_ins = [(gl, D_MODEL, 0), (gl, D_MODEL, 1), attn_d, (z, D_MODEL, 0), (z, D_MODEL, 1)]
    (merged,) = rowwise(lambda *v: (_merge_fn(*v),), gate_ins, [(D_MODEL, BF16)], "merge")
    x1 = matmul(merged, w["w_out"], "nn", F32, "out_proj", add=x)
    (h2,) = rowwise(lambda xv, g: (_rms(xv, g),), [x1, g_ffn], [(D_MODEL, BF16)], "norm_ffn")
    a = matmul(h2, w["w_ffn_gate"], "nn", F32, "ffn_gate")
    b = matmul(h2, w["w_ffn_up"], "nn", F32, "ffn_up")
    (act,) = rowwise(lambda av, bv: (_swiglu_fn(av, bv),), [a, b], [(D_FF, BF16)], "swiglu")
    x2 = matmul(act, w["w_ffn_down"], "nn", F32, "ffn_down", add=x1)

    def final_fn(xv, g, tgt):
        yv, vjp = jax.vjp(_rms, xv, g)
        err = yv - tgt
        dx, dg = vjp(err * (1.0 / D_MODEL))
        loss = 0.5 * jnp.sum(jnp.mean(err * err, axis=-1, keepdims=True), axis=0, keepdims=True)
        return dx, dx, dg, jnp.broadcast_to(loss, (1, LANES))

    dx2, dx2_b, dg_final, loss = rowwise(final_fn, [x2, g_final, target], [(D_MODEL, F32), (D_MODEL, BF16)],
                                         "final_norm_loss", accs=(D_MODEL, LANES))
    gw, parts = {}, {}
    gw["w_ffn_down"] = matmul(act, dx2_b, "tn", F32, "d_ffn_down")
    dact = matmul(dx2_b, w["w_ffn_down"], "nt", F32, "d_act")

    def swiglu_bwd(av, bv, dv):
        _, vjp = jax.vjp(_swiglu_fn, av, bv)
        return vjp(dv)

    da_b, db_b = rowwise(swiglu_bwd, [a, b, dact], [(D_FF, BF16), (D_FF, BF16)], "swiglu_bwd")
    gw["w_ffn_gate"] = matmul(h2, da_b, "tn", F32, "d_ffn_gate")
    gw["w_ffn_up"] = matmul(h2, db_b, "tn", F32, "d_ffn_up")
    dh2 = matmul(da_b, w["w_ffn_gate"], "nt", F32, "d_h2_gate")
    dh2 = matmul(db_b, w["w_ffn_up"], "nt", F32, "d_h2_up", add=dh2)

    def norm_bwd(xv, g, dh, skip):
        _, vjp = jax.vjp(_rms, xv, g)
        dx, dg = vjp(dh)
        dx = dx + skip
        return dx, dx, dg

    dx1, dx1_b, dg_ffn = rowwise(norm_bwd, [x1, g_ffn, dh2, dx2], [(D_MODEL, F32), (D_MODEL, BF16)],
                                 "norm_ffn_bwd", accs=(D_MODEL,))
    gw["w_out"] = matmul(merged, dx1_b, "tn", F32, "d_out")
    dmerged = matmul(dx1_b, w["w_out"], "nt", F32, "d_merged")

    def merge_bwd(g0, g1, ad, za, zb, dm):
        _, vjp = jax.vjp(_merge_fn, g0, g1, ad, za, zb)
        dg0, dg1, dad, dza, dzb = vjp(dm)
        return jnp.concatenate([dg0, dg1], axis=1), dad, jnp.concatenate([dza, dzb], axis=1)

    ffn = ["w_ffn_down", "w_ffn_gate", "w_ffn_up"]
    swap = _reduce_start(ffn, gw, shard_shapes)
    dgl_b, dattn_d_b, dz_b, *got = rowwise(merge_bwd, gate_ins + [dmerged],
                                           [(GATE_W, BF16), (D_MODEL, BF16), (GATE_W, BF16)], "merge_bwd", comm=swap)
    ffn_exchange = _reduce_chip(ffn, swap, got, core)
    gw["w_attn_out"] = matmul(attn_b, dattn_d_b, "tn", F32, "d_attn_out")
    dattn = seqs(matmul(dattn_d_b, w["w_attn_out"], "nt", F32, "d_attn"))
    gw["w_glu"] = matmul(yg2, dz_b, "tn", F32, "d_glu")
    dyg = seqs(matmul(dz_b, w["w_glu"], "nt", F32, "d_yg"))
    mixer = ["w_out", "w_attn_out", "w_glu"]
    swap = _reduce_start(mixer, gw, shard_shapes)
    dqkv_b, *rest = attn_bwd(qkv, tables, dattn, attn, lse, comm=join_comms([ffn_exchange, swap]))
    for n, p in zip(ffn, rest[:len(ffn)]):
        parts[n] = p
    mixer_exchange = _reduce_chip(mixer, swap, rest[len(ffn):], core)
    dqkv_b = toks(dqkv_b)
    du_b, dcb_re, dcb_im, dbb_re, dbb_im, dd, da_re8, da_im8, *rest = ssm_bwd(
        dyg, y, u, xr, xi, *ab, *bb, *cb, d_skip, comm=mixer_exchange)
    for n, p in zip(mixer, rest):
        parts[n] = p
    du_b = toks(du_b)
    d_qkv = matmul(h, dqkv_b, "tn", F32, "d_w_qkv")
    d_u = matmul(h, du_b, "tn", F32, "d_w_u")
    d_gate = matmul(h, dgl_b, "tn", F32, "d_w_gate")
    gw["w_in"] = jnp.concatenate([_qkv_order(d_qkv, back=True), d_u, d_gate], axis=1)
    swap = _reduce_start(["w_in"], gw, shard_shapes)
    (parts["w_in"],) = run_comm(_reduce_chip(["w_in"], swap, run_comm(swap, "swap_w_in"), core), "exchange_w_in")
    dh = matmul(dqkv_b, w_qkv, "nt", F32, "d_h_qkv")
    dh = matmul(du_b, w_u, "nt", F32, "d_h_u", add=dh)
    dh = matmul(dgl_b, w_gate, "nt", F32, "d_h_gate", add=dh)

    def norm_bwd_last(xv, g, dhv, skip):
        _, vjp = jax.vjp(_rms, xv, g)
        dx, dg = vjp(dhv)
        return dx + skip, dg

    grad_x, dg_mix = rowwise(norm_bwd_last, [x, g_mix, dh, dx1], [(D_MODEL, F32)], "norm_mix_bwd", accs=(D_MODEL,))
    g_ab_re = jnp.sum(da_re8, axis=0).reshape(32, 64)
    g_ab_im = jnp.sum(da_im8, axis=0).reshape(32, 64)
    d_lr, d_li, d_ldt, d_br_t, d_bi_t = ssm_prep_bwd(
        small["ssm_a_re"], small["ssm_a_im"], log_dt, br_t, bi_t,
        g_ab_re, g_ab_im, _block_diag_t(dbb_re), _block_diag_t(dbb_im))
    gs = {
        "norm_mix_g": dg_mix, "ssm_a_re": d_lr, "ssm_a_im": d_li, "ssm_log_dt": d_ldt.reshape(1, 32),
        "ssm_b_re": d_br_t.transpose(1, 2, 0), "ssm_b_im": d_bi_t.transpose(1, 2, 0),
        "ssm_c_re": _block_diag_t(dcb_re).transpose(1, 0, 2), "ssm_c_im": _block_diag_t(dcb_im).transpose(1, 0, 2),
        "ssm_d": dd.reshape(32, 16), "norm_ffn_g": dg_ffn, "norm_final_g": dg_final,
    }
    return loss, grad_x, parts, gs


ANY = pl.BlockSpec(memory_space=pl.ANY)
BIG = ("w_in", "w_glu", "w_attn_out", "w_out", "w_ffn_gate", "w_ffn_up", "w_ffn_down")
ROW_SHARDED = ("w_out", "w_ffn_down")
SMALL = ("norm_mix_g", "ssm_a_re", "ssm_a_im", "ssm_log_dt", "ssm_b_re", "ssm_b_im", "ssm_c_re", "ssm_c_im",
         "ssm_d", "norm_ffn_g", "norm_final_g")
WEIGHTS = ("norm_mix_g", "w_in", "ssm_a_re", "ssm_a_im", "ssm_log_dt", "ssm_b_re", "ssm_b_im", "ssm_c_re",
           "ssm_c_im", "ssm_d", "w_glu", "w_attn_out", "w_out", "norm_ffn_g", "w_ffn_gate", "w_ffn_up",
           "w_ffn_down", "norm_final_g")
SMALL_ROWS = 1088
N_BIG = len(BIG)


def _position():
    return lax.axis_index("x"), lax.axis_index("y"), lax.axis_index("c")


def _other_chips(x, y):
    return [(1 - x, y), (x, 1 - y), (1 - x, 1 - y)]


def _remote(src, dst, send_sem, recv_sem, device):
    return pltpu.make_async_remote_copy(src_ref=src, dst_ref=dst, send_sem=send_sem, recv_sem=recv_sem,
                                        device_id=device, device_id_type=MESH)


def _half(ref, chip, which):
    rows = ref.shape[1] // 2
    return ref.at[chip, pl.ds(which * rows, rows), :]


class Comm:
    def __init__(self, ins, out_shapes, sems, first, mid, last):
        self.ins, self.out_shapes, self.sems = list(ins), list(out_shapes), list(sems)
        self.first, self.mid, self.last = first, mid, last


def join_comms(comms):
    def cut(refs_by_kind):
        offs, parts = [0, 0, 0], []
        for cm in comms:
            sizes = (len(cm.ins), len(cm.out_shapes), len(cm.sems))
            parts.append(tuple(refs_by_kind[k][offs[k]:offs[k] + sizes[k]] for k in range(3)))
            offs = [o + s for o, s in zip(offs, sizes)]
        return parts

    def phase(which):
        def run(ins, outs, sems):
            for cm, part in zip(comms, cut((ins, outs, sems))):
                fn = getattr(cm, which)
                if fn is not None:
                    fn(*part)
        return run

    return Comm(sum((cm.ins for cm in comms), []), sum((cm.out_shapes for cm in comms), []),
                sum((cm.sems for cm in comms), []), phase("first"), phase("mid"), phase("last"))


def _comm_operands(comm):
    if comm is None:
        return [], [], []
    return comm.ins, comm.out_shapes, comm.sems


def _comm_begin(comm, refs, step, n_steps):
    if comm is None:
        return
    pl.when(step == 0)(lambda: comm.first(*refs))
    if comm.mid is not None:
        pl.when(step == (n_steps * 5) // 8)(lambda: comm.mid(*refs))


def _comm_end(comm, refs, step, n_steps):
    if comm is not None:
        pl.when(step == n_steps - 1)(lambda: comm.last(*refs))


def _comm_refs(comm, refs, n_in, n_out):
    if comm is None:
        return list(refs), None
    ci, co, cs = len(comm.ins), len(comm.out_shapes), len(comm.sems)
    o0 = n_in + ci
    s0 = o0 + n_out + co
    host = list(refs[:n_in]) + list(refs[o0:o0 + n_out]) + list(refs[s0:len(refs) - cs])
    return host, (list(refs[n_in:o0]), list(refs[o0 + n_out:s0]), list(refs[len(refs) - cs:]))


def run_comm(comm, name):
    n_in, n_out = len(comm.ins), len(comm.out_shapes)

    def body(*refs):
        parts = (list(refs[:n_in]), list(refs[n_in:n_in + n_out]), list(refs[n_in + n_out:]))
        comm.first(*parts)
        if comm.mid is not None:
            comm.mid(*parts)
        comm.last(*parts)

    return pl.pallas_call(body, name=name, in_specs=[ANY] * n_in, out_specs=[ANY] * n_out,
                          out_shape=comm.out_shapes, scratch_shapes=comm.sems)(*comm.ins)


def hosted_call(work, comm, name, grid, in_specs, out_specs, out_shape, scratch_shapes, args, semantics):
    c_ins, c_outs, c_sems = _comm_operands(comm)
    n_steps = math.prod(grid)

    def body(*refs):
        host, c_refs = _comm_refs(comm, refs, len(in_specs), len(out_specs))
        step = 0
        for axis, size in enumerate(grid):
            step = step * size + pl.program_id(axis)
        _comm_begin(comm, c_refs, step, n_steps)
        work(*host)
        _comm_end(comm, c_refs, step, n_steps)

    return pl.pallas_call(
        body, name=name, grid=grid, in_specs=list(in_specs) + [ANY] * len(c_ins),
        out_specs=list(out_specs) + [ANY] * len(c_outs), out_shape=list(out_shape) + c_outs,
        scratch_shapes=list(scratch_shapes) + c_sems,
        compiler_params=_params(semantics if comm is None else ("arbitrary",) * len(grid)),
    )(*args, *c_ins)


def gather_comm(shards):
    n = len(shards)

    def copies(srcs, outs, sems):
        send_sems, recv_sems, local_sems = sems
        x, y, c = _position()
        me = 2 * x + y
        sibling = (x, y, 1 - c)
        chips = _other_chips(x, y)
        locals_ = [pltpu.make_async_copy(s, o.at[me], local_sems.at[i]) for i, (s, o) in enumerate(zip(srcs, outs))]
        sends, arrived, passed, from_sibling = [], [], [], []
        for j, (px, py) in enumerate(chips):
            for i, (s, o) in enumerate(zip(srcs, outs)):
                rows = s.shape[0] // 2
                sends.append(_remote(s.at[pl.ds(c * rows, rows), :], _half(o, me, c), send_sems.at[i, j],
                                     recv_sems.at[i, j], (px, py, c)))
                got = _half(o, 2 * px + py, c)
                arrived.append(_remote(got, got, send_sems.at[i, j], recv_sems.at[i, j], (px, py, c)))
                passed.append(_remote(got, got, send_sems.at[i, 3 + j], recv_sems.at[i, 3 + j], sibling))
                other = _half(o, 2 * px + py, 1 - c)
                from_sibling.append(_remote(other, other, send_sems.at[i, 3 + j], recv_sems.at[i, 3 + j], sibling))
        return locals_, sends, arrived, passed, from_sibling

    def first(srcs, outs, sems):
        locals_, sends, _, _, _ = copies(srcs, outs, sems)
        for cp in locals_ + sends:
            cp.start()

    def mid(srcs, outs, sems):
        _, _, arrived, passed, _ = copies(srcs, outs, sems)
        for got, cp in zip(arrived, passed):
            got.wait_recv()
            cp.start()

    def last(srcs, outs, sems):
        locals_, sends, _, passed, from_sibling = copies(srcs, outs, sems)
        for cp in from_sibling:
            cp.wait_recv()
        for cp in sends + passed:
            cp.wait_send()
        for cp in locals_:
            cp.wait()

    return Comm(shards, [jax.ShapeDtypeStruct((N_CHIPS,) + s.shape, s.dtype) for s in shards],
                [pltpu.SemaphoreType.DMA((n, 6)), pltpu.SemaphoreType.DMA((n, 6)), pltpu.SemaphoreType.DMA((n,))],
                first, mid, last)


def swap_comm(grads):
    n = len(grads)

    def copies(srcs, gots, sems):
        send_sems, recv_sems = sems
        x, y, c = _position()
        out = []
        for i, (s, o) in enumerate(zip(srcs, gots)):
            rows = s.shape[1] // 2
            out.append(_remote(s.at[:, pl.ds((1 - c) * rows, rows), :], o, send_sems.at[i], recv_sems.at[i],
                               (x, y, 1 - c)))
        return out

    def first(srcs, gots, sems):
        for cp in copies(srcs, gots, sems):
            cp.start()

    def last(srcs, gots, sems):
        for cp in copies(srcs, gots, sems):
            cp.wait()

    return Comm(grads, [jax.ShapeDtypeStruct((N_CHIPS, g.shape[1] // 2, g.shape[2]), g.dtype) for g in grads],
                [pltpu.SemaphoreType.DMA((n,)), pltpu.SemaphoreType.DMA((n,))], first, None, last)


def add_halves(name, g, got, core):
    _, half, cols = got.shape
    mine = pl.BlockSpec((None, half, cols), lambda k, c_ref: (k, c_ref[0], 0))
    other = pl.BlockSpec((None, half, cols), lambda k, c_ref: (k, 0, 0))

    def body(c_ref, g_ref, got_ref, o_ref):
        o_ref[...] = (g_ref[...] + got_ref[...]).astype(BF16)

    return pl.pallas_call(
        body, name="add_halves_" + name,
        grid_spec=pltpu.PrefetchScalarGridSpec(num_scalar_prefetch=1, grid=(N_CHIPS,), in_specs=[mine, other],
                                               out_specs=other),
        out_shape=jax.ShapeDtypeStruct(got.shape, BF16),
        compiler_params=_params(("parallel",)),
    )(core, g, got)


def exchange_comm(parts):
    n = len(parts)

    def copies(srcs, outs, sems):
        send_sems, recv_sems, local_sems = sems
        x, y, c = _position()
        me = 2 * x + y
        sibling = (x, y, 1 - c)
        chips = _other_chips(x, y)
        locals_, sends, arrived, passed, from_sibling = [], [], [], [], []
        for i, (s, o) in enumerate(zip(srcs, outs)):
            locals_.append(pltpu.make_async_copy(s.at[me], _half(o, me, c), local_sems.at[i]))
            sends.append(_remote(s.at[me], _half(o, me, c), send_sems.at[i, 3], recv_sems.at[i, 3], sibling))
            other = _half(o, me, 1 - c)
            from_sibling.append(_remote(other, other, send_sems.at[i, 3], recv_sems.at[i, 3], sibling))
        for j, (px, py) in enumerate(chips):
            for i, (s, o) in enumerate(zip(srcs, outs)):
                sends.append(_remote(s.at[2 * px + py], _half(o, me, c), send_sems.at[i, j], recv_sems.at[i, j],
                                     (px, py, c)))
                got = _half(o, 2 * px + py, c)
                arrived.append(_remote(got, got, send_sems.at[i, j], recv_sems.at[i, j], (px, py, c)))
                passed.append(_remote(got, got, send_sems.at[i, 4 + j], recv_sems.at[i, 4 + j], sibling))
                other = _half(o, 2 * px + py, 1 - c)
                from_sibling.append(_remote(other, other, send_sems.at[i, 4 + j], recv_sems.at[i, 4 + j], sibling))
        return locals_, sends, arrived, passed, from_sibling

    def first(srcs, outs, sems):
        locals_, sends, _, _, _ = copies(srcs, outs, sems)
        for cp in locals_ + sends:
            cp.start()

    def mid(srcs, outs, sems):
        _, _, arrived, passed, _ = copies(srcs, outs, sems)
        for got, cp in zip(arrived, passed):
            got.wait_recv()
            cp.start()

    def last(srcs, outs, sems):
        locals_, sends, _, passed, from_sibling = copies(srcs, outs, sems)
        for cp in from_sibling:
            cp.wait_recv()
        for cp in sends + passed:
            cp.wait_send()
        for cp in locals_:
            cp.wait()

    return Comm(parts, [jax.ShapeDtypeStruct((N_CHIPS, 2 * p.shape[1], p.shape[2]), p.dtype) for p in parts],
                [pltpu.SemaphoreType.DMA((n, 7)), pltpu.SemaphoreType.DMA((n, 7)), pltpu.SemaphoreType.DMA((n,))],
                first, mid, last)


def allgather_small(pack):
    def body(src_ref, out_ref, send_sems, recv_sems, local_sem):
        x, y, c = _position()
        me = 4 * x + 2 * y + c
        local = pltpu.make_async_copy(src_ref, out_ref.at[me], local_sem)
        local.start()
        flips = [(fx, fy, fc) for fx in (0, 1) for fy in (0, 1) for fc in (0, 1)][1:]
        peers = [(1 - x if fx else x, 1 - y if fy else y, 1 - c if fc else c) for fx, fy, fc in flips]
        sends = [_remote(src_ref, out_ref.at[me], send_sems.at[j], recv_sems.at[j], peer)
                 for j, peer in enumerate(peers)]
        for cp in sends:
            cp.start()
        for j, (px, py, pc) in enumerate(peers):
            got = out_ref.at[4 * px + 2 * py + pc]
            _remote(got, got, send_sems.at[j], recv_sems.at[j], (px, py, pc)).wait_recv()
        for cp in sends:
            cp.wait_send()
        local.wait()

    return pl.pallas_call(
        body, name="allgather_small", in_specs=[ANY], out_specs=ANY,
        out_shape=jax.ShapeDtypeStruct((N_DEV,) + pack.shape, pack.dtype),
        scratch_shapes=[pltpu.SemaphoreType.DMA((7,)), pltpu.SemaphoreType.DMA((7,)), pltpu.SemaphoreType.DMA],
    )(pack)


def _adam_fn(w, g, m, v):
    m = ADAM_B1 * m + (1.0 - ADAM_B1) * g
    v = ADAM_B2 * v + (1.0 - ADAM_B2) * jnp.square(g)
    m_hat = m / (1.0 - ADAM_B1 ** ADAM_STEP)
    v_hat = v / (1.0 - ADAM_B2 ** ADAM_STEP)
    return -ADAM_LR * (m_hat / (jnp.sqrt(v_hat) + ADAM_EPS) + ADAM_WD * w), m, v


def adam_big(name, parts, w, m, v):
    rows, cols = w.shape
    tm = 256 if rows % 256 == 0 else rows // 2

    def fn(p0, p1, p2, p3, wv, mv, vv):
        g = ((p0.astype(F32) + p1.astype(F32)) + p2.astype(F32)) + p3.astype(F32)
        return (g,) + _adam_fn(wv, g, mv, vv)

    return rowwise(fn, [parts, w, m, v], [(cols, F32)] * 4, "adam_" + name, tm=tm, rows=rows)


def adam_small(gathered, w, m, v):
    def body(g_ref, w_ref, m_ref, v_ref, go_ref, d_ref, mo_ref, vo_ref):
        g = g_ref[0]
        for k in range(1, N_DEV):
            g = g + g_ref[k]
        go_ref[...] = g
        d_ref[...], mo_ref[...], vo_ref[...] = _adam_fn(w_ref[...], g, m_ref[...], v_ref[...])

    return pl.pallas_call(body, name="adam_small", out_shape=[jax.ShapeDtypeStruct(w.shape, F32)] * 4,
                          compiler_params=_params())(gathered, w, m, v)


def _pack_small(vals, last=None):
    flat = [vals[n].reshape(-1) for n in SMALL]
    if last is not None:
        flat.append(last.reshape(-1))
    flat = jnp.concatenate(flat)
    return jnp.pad(flat, (0, SMALL_ROWS * LANES - flat.shape[0])).reshape(SMALL_ROWS, LANES)


def _unpack_small(pack, shapes):
    flat, out, off = pack.reshape(-1), {}, 0
    for n in SMALL:
        size = math.prod(shapes[n])
        out[n] = flat[off:off + size].reshape(shapes[n])
        off += size
    return out, flat[off]


def _to_slots(name, g, shard_shape):
    rows, cols = shard_shape
    if name in ROW_SHARDED:
        return g.reshape(N_CHIPS, rows, cols)
    return g.reshape(rows, N_CHIPS, cols).transpose(1, 0, 2)


def _from_slots(name, s):
    _, rows, cols = s.shape
    if name in ROW_SHARDED:
        return s.reshape(N_CHIPS * rows, cols)
    return s.transpose(1, 0, 2).reshape(rows, N_CHIPS * cols)


def kernel(x, norm_mix_g, w_in, ssm_a_re, ssm_a_im, ssm_log_dt, ssm_b_re, ssm_b_im, ssm_c_re, ssm_c_im, ssm_d, w_glu, w_attn_out, w_out, norm_ffn_g, w_ffn_gate, w_ffn_up, w_ffn_down, norm_final_g, loss_target, m_norm_mix_g, m_w_in, m_ssm_a_re, m_ssm_a_im, m_ssm_log_dt, m_ssm_b_re, m_ssm_b_im, m_ssm_c_re, m_ssm_c_im, m_ssm_d, m_w_glu, m_w_attn_out, m_w_out, m_norm_ffn_g, m_w_ffn_gate, m_w_ffn_up, m_w_ffn_down, m_norm_final_g, v_norm_mix_g, v_w_in, v_ssm_a_re, v_ssm_a_im, v_ssm_log_dt, v_ssm_b_re, v_ssm_b_im, v_ssm_c_re, v_ssm_c_im, v_ssm_d, v_w_glu, v_w_attn_out, v_w_out, v_norm_ffn_g, v_w_ffn_gate, v_w_ffn_up, v_w_ffn_down, v_norm_final_g):
    given = dict(locals())
    shard = {n: given[n][0] for n in BIG}
    shapes = {n: given[n].shape for n in WEIGHTS}

    small = {n: given[n] for n in SMALL}
    small_2d = dict(small)
    for n in ("ssm_a_re", "ssm_a_im", "ssm_b_re", "ssm_b_im", "ssm_c_re", "ssm_c_im", "ssm_d"):
        small_2d[n] = small[n][0]
    small_2d["norm_final_g"] = norm_final_g.reshape(1, D_MODEL)

    core = lax.axis_index("c").astype(jnp.int32).reshape(1)
    loss, grad_x, parts, gs = local_step(x.reshape(TOKENS, D_MODEL), loss_target.reshape(TOKENS, D_MODEL),
                                         {n: shard[n].astype(BF16) for n in BIG}, small_2d, core)

    share = _pack_small({n: gs[n] for n in SMALL}, last=loss)
    everyone = allgather_small(share)
    packs = [_pack_small({n: given[p + n] for n in SMALL}) for p in ("", "m_", "v_")]
    small_out = [_unpack_small(t, shapes) for t in adam_small(everyone, *packs)]
    total_loss = small_out[0][1][()]

    big_out = {}
    for n in BIG:
        big_out[n] = [t[None] for t in adam_big(n, parts[n], shard[n], given["m_" + n][0], given["v_" + n][0])]

    outs = [total_loss, grad_x.reshape(LOCAL_BATCH, SEQ, D_MODEL)]
    for kind in range(4):
        for n in WEIGHTS:
            outs.append(big_out[n][kind] if n in BIG else small_out[kind][0][n])
    return tuple(outs)
```

```python
import functools
import math

import jax
import jax.numpy as jnp
from jax import lax
from jax.experimental import pallas as pl
from jax.experimental.pallas import tpu as pltpu

F32 = jnp.float32
BF16 = jnp.bfloat16
MESH = pl.DeviceIdType.MESH

D_MODEL = 1024
SEQ = 2048
LOCAL_BATCH = 2
TOKENS = LOCAL_BATCH * SEQ
HEAD_DIM = 64
HEADS_PER_GROUP = 4
GROUP_W = HEADS_PER_GROUP * HEAD_DIM
N_GROUPS = 3
DILATIONS = (1, 4, 16)
ATTN_BLOCK = 128
ROPE_DIM = 16
ROPE_THETA = 500000.0
QKV_W = 3 * N_GROUPS * GROUP_W
SSM_W = 512
SSM_STATE_W = 2048
SSM_LANE_BLOCKS = 4
GATE_W = 2 * D_MODEL
D_FF = 2816
RMS_EPS = 1e-6
NEG_INF = -1e30
ADAM_LR, ADAM_B1, ADAM_B2, ADAM_EPS, ADAM_WD, ADAM_STEP = 0.001, 0.9, 0.999, 1e-08, 0.01, 10
N_CHIPS = 4
N_DEV = 8

VMEM_LIMIT = 56 * 1024 * 1024
LANES = 128


def _params(sem=None):
    return pltpu.CompilerParams(dimension_semantics=sem, vmem_limit_bytes=VMEM_LIMIT)


def _pick(n, cap, align=LANES):
    best = None
    for d in range(align, min(n, cap) + 1, align):
        if n % d == 0:
            best = d
    return n if best is None or n <= cap else best


_DIMS = {"nn": (((1,), (0,)), ((), ())), "nt": (((1,), (1,)), ((), ())), "tn": (((0,), (0,)), ((), ()))}


def _dot(a, b, mode):
    return lax.dot_general(a, b, _DIMS[mode], preferred_element_type=F32)


def matmul(a, b, mode, out_dtype, name, add=None):
    if mode == "nn":
        (m, k), n = a.shape, b.shape[1]
    elif mode == "nt":
        (m, k), n = a.shape, b.shape[0]
    else:
        (k, m), n = a.shape, b.shape[1]
    tn = _pick(n, 1408)
    tk = _pick(k, 2816) if mode != "tn" else _pick(k, 1024)
    tm = _pick(m, 1408)
    out_bytes = jnp.dtype(out_dtype).itemsize

    def need(tm_):
        return 2 * 2 * (tm_ * tk + tk * tn) + tm_ * tn * (4 + 2 * out_bytes + (8 if add is not None else 0))

    while need(tm) > 40 * 1024 * 1024 and tm % 256 == 0:
        tm //= 2
    nk = k // tk
    a_spec = {"nn": pl.BlockSpec((tm, tk), lambda i, j, kk: (i, kk)),
              "nt": pl.BlockSpec((tm, tk), lambda i, j, kk: (i, kk)),
              "tn": pl.BlockSpec((tk, tm), lambda i, j, kk: (kk, i))}[mode]
    b_spec = {"nn": pl.BlockSpec((tk, tn), lambda i, j, kk: (kk, j)),
              "nt": pl.BlockSpec((tn, tk), lambda i, j, kk: (j, kk)),
              "tn": pl.BlockSpec((tk, tn), lambda i, j, kk: (kk, j))}[mode]
    o_spec = pl.BlockSpec((tm, tn), lambda i, j, kk: (i, j))

    def body(a_ref, b_ref, *rest):
        if add is not None:
            add_ref, o_ref, acc_ref = rest
        else:
            o_ref, acc_ref = rest
        part = _dot(a_ref[...], b_ref[...], mode)
        if nk == 1:
            res = part if add is None else part + add_ref[...]
            o_ref[...] = res.astype(out_dtype)
            return
        kk = pl.program_id(2)

        @pl.when(kk == 0)
        def _():
            acc_ref[...] = part

        @pl.when(kk > 0)
        def _():
            acc_ref[...] += part

        @pl.when(kk == nk - 1)
        def _():
            res = acc_ref[...] if add is None else acc_ref[...] + add_ref[...]
            o_ref[...] = res.astype(out_dtype)

    in_specs = [a_spec, b_spec] + ([o_spec] if add is not None else [])
    args = (a, b) + ((add,) if add is not None else ())
    return pl.pallas_call(
        body, name=name, grid=(m // tm, n // tn, nk), in_specs=in_specs, out_specs=o_spec,
        out_shape=jax.ShapeDtypeStruct((m, n), out_dtype),
        scratch_shapes=[pltpu.VMEM((tm, tn) if nk > 1 else (8, LANES), F32)],
        compiler_params=_params(("parallel", "parallel", "arbitrary")),
    )(*args)


FFN_TM, FFN_TN = 512, 1408


def ffn_in(h2, wg_t, wu_t):
    def body(h_ref, wg_ref, wu_ref, a_ref, b_ref, act_ref):
        hv = h_ref[...]
        a, b = _dot(hv, wg_ref[...], "nt"), _dot(hv, wu_ref[...], "nt")
        a_ref[...] = a.astype(BF16)
        b_ref[...] = b.astype(BF16)
        act_ref[...] = _swiglu_fn(a, b).astype(BF16)

    rows = pl.BlockSpec((FFN_TM, D_MODEL), lambda i, j: (i, 0))
    wts = pl.BlockSpec((FFN_TN, D_MODEL), lambda i, j: (j, 0))
    out = pl.BlockSpec((FFN_TM, FFN_TN), lambda i, j: (i, j))
    return pl.pallas_call(
        body, name="ffn_in", grid=(TOKENS // FFN_TM, D_FF // FFN_TN), in_specs=[rows, wts, wts],
        out_specs=[out] * 3, out_shape=[jax.ShapeDtypeStruct((TOKENS, D_FF), BF16)] * 3,
        compiler_params=_params(("parallel", "parallel")),
    )(h2, wg_t, wu_t)


def ffn_in_bwd(dx2_b, wd, a, b):
    def body(dx_ref, wd_ref, a_ref, b_ref, da_ref, db_ref):
        dact = _dot(dx_ref[...], wd_ref[...], "nt")
        _, vjp = jax.vjp(_swiglu_fn, a_ref[...].astype(F32), b_ref[...].astype(F32))
        da, db = vjp(dact)
        da_ref[...] = da.astype(BF16)
        db_ref[...] = db.astype(BF16)

    rows = pl.BlockSpec((FFN_TM, D_MODEL), lambda i, j: (i, 0))
    wts = pl.BlockSpec((FFN_TN, D_MODEL), lambda i, j: (j, 0))
    out = pl.BlockSpec((FFN_TM, FFN_TN), lambda i, j: (i, j))
    return pl.pallas_call(
        body, name="ffn_in_bwd", grid=(TOKENS // FFN_TM, D_FF // FFN_TN), in_specs=[rows, wts, out, out],
        out_specs=[out] * 2, out_shape=[jax.ShapeDtypeStruct((TOKENS, D_FF), BF16)] * 2,
        compiler_params=_params(("parallel", "parallel")),
    )(dx2_b, wd, a, b)


def rowwise(fn, ins, outs, name, accs=(), tm=256, rows=TOKENS, comm=None):
    in_specs, args = [], []
    for item in ins:
        arr, width, blk = item if isinstance(item, tuple) else (item, None, 0)
        if arr.ndim == 3:
            for k in range(arr.shape[0]):
                in_specs.append(pl.BlockSpec((None, tm, arr.shape[2]), functools.partial(lambda i, k_: (k_, i, 0), k_=k)))
                args.append(arr)
            continue
        if arr.shape[0] == 1:
            in_specs.append(pl.BlockSpec(arr.shape, lambda i: (0, 0)))
        elif width is None:
            in_specs.append(pl.BlockSpec((tm, arr.shape[1]), lambda i: (i, 0)))
        else:
            in_specs.append(pl.BlockSpec((tm, width), functools.partial(lambda i, blk_: (i, blk_), blk_=blk)))
        args.append(arr)
    out_specs = [pl.BlockSpec((tm, c), lambda i: (i, 0)) for c, _ in outs]
    out_specs += [pl.BlockSpec((1, c), lambda i: (0, 0)) for c in accs]
    out_shape = [jax.ShapeDtypeStruct((rows, c), dt) for c, dt in outs]
    out_shape += [jax.ShapeDtypeStruct((1, c), F32) for c in accs]
    n_in, n_out = len(args), len(outs)
    c_ins, c_outs, c_sems = _comm_operands(comm)

    def body(*refs):
        refs, c_refs = _comm_refs(comm, refs, n_in, n_out + len(accs))
        step = pl.program_id(0)
        _comm_begin(comm, c_refs, step, rows // tm)
        res = fn(*[r[...] for r in refs[:n_in]])
        for r, v in zip(refs[n_in:n_in + n_out], res[:n_out]):
            r[...] = v.astype(r.dtype)
        first = step == 0
        for r, v in zip(refs[n_in + n_out:], res[n_out:]):
            @pl.when(first)
            def _(r=r, v=v):
                r[...] = v

            @pl.when(jnp.logical_not(first))
            def _(r=r, v=v):
                r[...] += v
        _comm_end(comm, c_refs, step, rows // tm)

    return pl.pallas_call(
        body, name=name, grid=(rows // tm,), in_specs=in_specs + [ANY] * len(c_ins),
        out_specs=out_specs + [ANY] * len(c_outs), out_shape=out_shape + c_outs, scratch_shapes=c_sems,
        compiler_params=_params(("arbitrary",)),
    )(*args, *c_ins)


def _rms(x, g):
    return x * lax.rsqrt(jnp.mean(x * x, axis=-1, keepdims=True) + RMS_EPS) * g


def _colsum(v):
    return jnp.sum(v, axis=0, keepdims=True)


PAIR_W = 2 * HEAD_DIM
N_PAIRS = HEADS_PER_GROUP // 2


def _qkv_order(w_t, back=False):
    dims = (N_PAIRS, N_GROUPS, 3) if back else (3, N_GROUPS, N_PAIRS)
    return w_t.reshape(dims + (PAIR_W, w_t.shape[1])).transpose(2, 1, 0, 3, 4).reshape(QKV_W, w_t.shape[1])


def _rope_tables():
    half = ROPE_DIM // 2
    inv = jnp.power(jnp.float32(ROPE_THETA), -jnp.arange(half, dtype=F32) * 2.0 / ROPE_DIM)
    ang = jnp.arange(SEQ, dtype=F32)[:, None] * inv[None, :]
    cos, sin = jnp.cos(ang), jnp.sin(ang)
    zeros = jnp.zeros((SEQ, HEAD_DIM - ROPE_DIM), F32)
    zh = jnp.zeros((SEQ, half), F32)
    c = jnp.concatenate([cos, cos, zeros + 1.0], axis=1)
    sa = jnp.concatenate([-sin, zh, zeros], axis=1)
    sb = jnp.concatenate([zh, sin, zeros], axis=1)
    return [jnp.tile(t, (1, 2)) for t in (c, sa, sb)]


def _rope_fwd(x, c, sa, sb):
    return x * c + pltpu.roll(x, PAIR_W - 8, 1) * sa + pltpu.roll(x, 8, 1) * sb


def _rope_bwd(dy, c, sa, sb):
    return dy * c + pltpu.roll(dy * sb, PAIR_W - 8, 1) + pltpu.roll(dy * sa, 8, 1)


def _band_masks():
    row = lax.broadcasted_iota(jnp.int32, (ATTN_BLOCK, ATTN_BLOCK), 0)
    col = lax.broadcasted_iota(jnp.int32, (ATTN_BLOCK, ATTN_BLOCK), 1)
    return col <= row, col >= row


def _per_head(fn):
    return jnp.concatenate([fn(slice(h * HEAD_DIM, (h + 1) * HEAD_DIM)) for h in range(2)], axis=1)


def _slab_spec(kind):
    return pl.BlockSpec((None, SEQ, PAIR_W), lambda b, p, g: (b, 0, p * 3 * N_GROUPS + g * 3 + kind))


_TABLE_SPEC = pl.BlockSpec((SEQ, PAIR_W), lambda b, p, g: (0, 0))
_PAIR_SPEC = pl.BlockSpec((None, SEQ, PAIR_W), lambda b, p, g: (b, 0, p))


def _block_rows(dil, r, n):
    return pl.ds(n * (ATTN_BLOCK * dil) + r, ATTN_BLOCK, stride=dil)


def attn_fwd(qkv, tables, comm=None):
    scale = HEAD_DIM ** -0.5

    def body(q_ref, k_ref, v_ref, c_ref, sa_ref, sb_ref, attn_b_ref, attn_ref, lse_ref, qs, ks, o0, o1, o2, l0, l1, l2):
        g = pl.program_id(2)
        c, sa, sb = c_ref[...], sa_ref[...], sb_ref[...]
        qs[...] = _rope_fwd(q_ref[...], c, sa, sb) * scale
        ks[...] = _rope_fwd(k_ref[...], c, sa, sb)
        cur_mask, prev_mask = _band_masks()
        first_head = lax.broadcasted_iota(jnp.int32, (ATTN_BLOCK, PAIR_W), 1) < HEAD_DIM

        def run(dil, o_slab, l_slab):
            nb = SEQ // dil // ATTN_BLOCK

            def block(idx, carry):
                r, n = lax.div(idx, nb), lax.rem(idx, nb)
                cur, prev = _block_rows(dil, r, n), _block_rows(dil, r, jnp.maximum(n - 1, 0))
                q = qs[cur, :].astype(BF16)
                kc, kp = ks[cur, :].astype(BF16), ks[prev, :].astype(BF16)
                vc, vp = v_ref[cur, :].astype(BF16), v_ref[prev, :].astype(BF16)
                pmask = jnp.logical_and(prev_mask, n > 0)
                res, ms = [], []
                for h in range(2):
                    mine = first_head if h == 0 else jnp.logical_not(first_head)
                    qh = jnp.where(mine, q, 0)
                    sc = jnp.where(cur_mask, _dot(qh, kc, "nt"), NEG_INF)
                    sp = jnp.where(pmask, _dot(qh, kp, "nt"), NEG_INF)
                    m = jnp.max(jnp.maximum(sc, sp), axis=-1, keepdims=True)
                    pc, pp = jnp.exp(sc - m).astype(BF16), jnp.exp(sp - m).astype(BF16)
                    res.append(_dot(pc, jnp.where(mine, vc, 1), "nn") + _dot(pp, jnp.where(mine, vp, 1), "nn"))
                    ms.append(m)
                num = jnp.where(first_head, res[0], res[1])
                den = pltpu.roll(jnp.where(first_head, res[1], res[0]), HEAD_DIM, 1)
                o_slab[cur, :] = num / den
                l_slab[cur, :] = jnp.where(first_head, ms[0], ms[1]) + jnp.log(den)
                return carry

            lax.fori_loop(0, SEQ // ATTN_BLOCK, block, 0, unroll=2)

        for gi, (o_slab, l_slab) in enumerate(((o0, l0), (o1, l1), (o2, l2))):
            @pl.when(g == gi)
            def _(gi=gi, o_slab=o_slab, l_slab=l_slab):
                run(DILATIONS[gi], o_slab, l_slab)

        @pl.when(g == N_GROUPS - 1)
        def _():
            a, b, cc = l0[...], l1[...], l2[...]
            m = jnp.maximum(jnp.maximum(a, b), cc)
            e0, e1, e2 = jnp.exp(a - m), jnp.exp(b - m), jnp.exp(cc - m)
            tot = e0 + e1 + e2
            attn = (e0 * o0[...] + e1 * o1[...] + e2 * o2[...]) / tot
            attn_ref[...] = attn
            attn_b_ref[...] = attn.astype(BF16)
            lse_ref[...] = m + jnp.log(tot)

    shape = (LOCAL_BATCH, SEQ, GROUP_W)
    slab = pltpu.VMEM((SEQ, PAIR_W), F32)
    return hosted_call(
        body, comm, "attn_fwd", (LOCAL_BATCH, N_PAIRS, N_GROUPS),
        [_slab_spec(0), _slab_spec(1), _slab_spec(2), _TABLE_SPEC, _TABLE_SPEC, _TABLE_SPEC], [_PAIR_SPEC] * 3,
        [jax.ShapeDtypeStruct(shape, BF16), jax.ShapeDtypeStruct(shape, F32), jax.ShapeDtypeStruct(shape, F32)],
        [slab] * 8, (qkv, qkv, qkv, *tables), ("parallel", "parallel", "arbitrary"))


def attn_bwd(qkv, tables, dattn, attn, lse, comm=None):
    scale = HEAD_DIM ** -0.5

    def body(q_ref, k_ref, v_ref, c_ref, sa_ref, sb_ref, do_ref, out_ref, lse_ref, dqkv_ref, qs, ks, dl, dq_s, dk_s, dv_s):
        g = pl.program_id(2)
        c, sa, sb = c_ref[...], sa_ref[...], sb_ref[...]
        qs[...] = _rope_fwd(q_ref[...], c, sa, sb) * scale
        ks[...] = _rope_fwd(k_ref[...], c, sa, sb)
        prod = do_ref[...] * out_ref[...]
        dl[...] = _per_head(lambda sl: jnp.broadcast_to(jnp.sum(prod[:, sl], axis=-1, keepdims=True), (SEQ, HEAD_DIM)))
        cur_mask, prev_mask = _band_masks()
        first_head = lax.broadcasted_iota(jnp.int32, (ATTN_BLOCK, PAIR_W), 1) < HEAD_DIM

        def run(dil):
            nb = SEQ // dil // ATTN_BLOCK

            def block(idx, carry):
                r, n = lax.div(idx, nb), lax.rem(idx, nb)
                cur = _block_rows(dil, r, n)
                prev = _block_rows(dil, r, jnp.maximum(n - 1, 0))
                nxt = _block_rows(dil, r, jnp.minimum(n + 1, nb - 1))
                q0, q1 = qs[cur, :].astype(BF16), qs[nxt, :].astype(BF16)
                kp, kc = ks[prev, :].astype(BF16), ks[cur, :].astype(BF16)
                vp, vc = v_ref[prev, :].astype(BF16), v_ref[cur, :].astype(BF16)
                do0, do1 = do_ref[cur, :].astype(BF16), do_ref[nxt, :].astype(BF16)
                lse0, lse1, dl0, dl1 = lse_ref[cur, :], lse_ref[nxt, :], dl[cur, :], dl[nxt, :]
                has_prev = jnp.logical_and(prev_mask, n > 0)
                has_next = jnp.logical_and(prev_mask, n < nb - 1)

                def pair(q, k, v, do, lse_col, delta, mask):
                    p = jnp.where(mask, jnp.exp(_dot(q, k, "nt") - lse_col), 0.0)
                    ds = p * (_dot(do, v, "nt") - delta)
                    return p.astype(BF16), ds.astype(BF16)

                dqs, dks, dvs = [], [], []
                for h in range(2):
                    mine = first_head if h == 0 else jnp.logical_not(first_head)
                    one = slice(h * HEAD_DIM, h * HEAD_DIM + 1)
                    q0h, q1h = jnp.where(mine, q0, 0), jnp.where(mine, q1, 0)
                    do0h, do1h = jnp.where(mine, do0, 0), jnp.where(mine, do1, 0)
                    _, ds_a = pair(q0h, kp, vp, do0h, lse0[:, one], dl0[:, one], has_prev)
                    p_b, ds_b = pair(q0h, kc, vc, do0h, lse0[:, one], dl0[:, one], cur_mask)
                    p_c, ds_c = pair(q1h, kc, vc, do1h, lse1[:, one], dl1[:, one], has_next)
                    dqs.append(_dot(ds_a, kp, "nn") + _dot(ds_b, kc, "nn"))
                    dks.append(_dot(ds_b, q0h, "tn") + _dot(ds_c, q1h, "tn"))
                    dvs.append(_dot(p_b, do0h, "tn") + _dot(p_c, do1h, "tn"))
                dq_s[cur, :] = jnp.where(first_head, dqs[0], dqs[1])
                dk_s[cur, :] = dks[0] + dks[1]
                dv_s[cur, :] = dvs[0] + dvs[1]
                return carry

            lax.fori_loop(0, SEQ // ATTN_BLOCK, block, 0, unroll=2)

        for gi in range(N_GROUPS):
            @pl.when(g == gi)
            def _(gi=gi):
                run(DILATIONS[gi])

        dqkv_ref[:, 0:PAIR_W] = _rope_bwd(dq_s[...] * scale, c, sa, sb).astype(BF16)
        dqkv_ref[:, PAIR_W:2 * PAIR_W] = _rope_bwd(dk_s[...], c, sa, sb).astype(BF16)
        dqkv_ref[:, 2 * PAIR_W:] = dv_s[...].astype(BF16)

    slab = pltpu.VMEM((SEQ, PAIR_W), F32)
    return hosted_call(
        body, comm, "attn_bwd", (LOCAL_BATCH, N_PAIRS, N_GROUPS),
        [_slab_spec(0), _slab_spec(1), _slab_spec(2), _TABLE_SPEC, _TABLE_SPEC, _TABLE_SPEC,
         _PAIR_SPEC, _PAIR_SPEC, _PAIR_SPEC],
        [pl.BlockSpec((None, SEQ, 3 * PAIR_W), lambda b, p, g: (b, 0, p * N_GROUPS + g))],
        [jax.ShapeDtypeStruct((LOCAL_BATCH, SEQ, QKV_W), BF16)],
        [slab] * 6, (qkv, qkv, qkv, *tables, dattn, attn, lse), ("parallel", "parallel", "arbitrary"))


def _discretize(lr, li, log_dt, br, bi):
    dt = jnp.exp(log_dt)
    mag = jnp.exp(lr * dt)
    ab_re, ab_im = mag * jnp.cos(li * dt), mag * jnp.sin(li * dt)
    den = lr * lr + li * li
    nr, ni = ab_re - 1.0, ab_im
    f_re = (nr * lr + ni * li) / den
    f_im = (ni * lr - nr * li) / den
    return ab_re, ab_im, f_re[None] * br - f_im[None] * bi, f_re[None] * bi + f_im[None] * br


def ssm_prep(lr, li, log_dt, br, bi):
    def body(lr_ref, li_ref, dt_ref, br_ref, bi_ref, *outs):
        for o, v in zip(outs, _discretize(lr_ref[...], li_ref[...], dt_ref[...], br_ref[...], bi_ref[...])):
            o[...] = v
    shapes = [lr, li, br, bi]
    return pl.pallas_call(body, name="ssm_prep",
                          out_shape=[jax.ShapeDtypeStruct(s.shape, F32) for s in shapes])(lr, li, log_dt, br, bi)


def ssm_prep_bwd(lr, li, log_dt, br, bi, g_ab_re, g_ab_im, g_bb_re, g_bb_im):
    def body(lr_ref, li_ref, dt_ref, br_ref, bi_ref, g0, g1, g2, g3, *outs):
        _, vjp = jax.vjp(_discretize, lr_ref[...], li_ref[...], dt_ref[...], br_ref[...], bi_ref[...])
        for o, v in zip(outs, vjp((g0[...], g1[...], g2[...], g3[...]))):
            o[...] = v
    shapes = [lr, li, log_dt, br, bi]
    return pl.pallas_call(body, name="ssm_prep_bwd",
                          out_shape=[jax.ShapeDtypeStruct(s.shape, F32) for s in shapes])(
        lr, li, log_dt, br, bi, g_ab_re, g_ab_im, g_bb_re, g_bb_im)


def _block_diag(t):
    per = SSM_STATE_W // SSM_LANE_BLOCKS // 64
    g = t.transpose(1, 0, 2).reshape(SSM_LANE_BLOCKS, per, 16, 64)
    eye = jnp.eye(per, dtype=t.dtype)
    return jnp.einsum("jgcn,gh->jgchn", g, eye).reshape(SSM_LANE_BLOCKS, per * 16, per * 64)


def _block_diag_t(m):
    per = SSM_STATE_W // SSM_LANE_BLOCKS // 64
    m5 = m.reshape(SSM_LANE_BLOCKS, per, 16, per, 64)
    d = jnp.einsum("jgchn,gh->jgcn", m5, jnp.eye(per, dtype=m.dtype))
    return d.reshape(SSM_LANE_BLOCKS * per, 16, 64).transpose(1, 0, 2)


def _cmul(ar, ai, br, bi):
    return ar * br - ai * bi, ar * bi + ai * br


def _power_tables(ar, ai, reverse):
    width = ar.shape[1]
    row = lax.broadcasted_iota(jnp.int32, (8, width), 0)
    pows = [(ar, ai)]
    for _ in range(7):
        pows.append(_cmul(pows[-1][0], pows[-1][1], ar, ai))
    steps = []
    for k in (1, 2, 4):
        keep = (row >= k) if not reverse else (row < 8 - k)
        steps.append((jnp.where(keep, pows[k - 1][0], 0.0), jnp.where(keep, pows[k - 1][1], 0.0)))
    cr = jnp.zeros((8, width), F32)
    ci = jnp.zeros((8, width), F32)
    for i in range(8):
        pr, pi = pows[i] if not reverse else pows[7 - i]
        cr = jnp.where(row == i, pr, cr)
        ci = jnp.where(row == i, pi, ci)
    return steps, (cr, ci)


SCAN_CHUNK = 512
STATE_BLOCK = SSM_STATE_W // SSM_LANE_BLOCKS
CHAN_BLOCK = SSM_W // SSM_LANE_BLOCKS


def ssm_fwd(u, ab_re, ab_im, bb_re, bb_im, cb_re, cb_im, d_skip, comm=None):
    nt = SEQ // SCAN_CHUNK
    chan = pl.BlockSpec((None, SCAN_CHUNK, CHAN_BLOCK), lambda b, j, t: (b, t, j))
    state = pl.BlockSpec((None, SCAN_CHUNK, STATE_BLOCK), lambda b, j, t: (b, t, j))
    mat = pl.BlockSpec((None, CHAN_BLOCK, STATE_BLOCK), lambda b, j, t: (j, 0, 0))
    lane = pl.BlockSpec((1, STATE_BLOCK), lambda b, j, t: (0, j))
    dsp = pl.BlockSpec((1, CHAN_BLOCK), lambda b, j, t: (0, j))

    def body(u_ref, ar_ref, ai_ref, bbr_ref, bbi_ref, cbr_ref, cbi_ref, d_ref, y_ref, yg_ref, xr_ref, xi_ref,
             car_r, car_i):
        @pl.when(pl.program_id(2) == 0)
        def _():
            car_r[...] = jnp.zeros_like(car_r)
            car_i[...] = jnp.zeros_like(car_i)

        steps, (pr, pi) = _power_tables(ar_ref[...], ai_ref[...], reverse=False)
        uf = u_ref[...]
        ub = uf.astype(BF16)
        xr_ref[...] = _dot(ub, bbr_ref[...], "nn")
        xi_ref[...] = _dot(ub, bbi_ref[...], "nn")

        def tile(i, carry):
            cr, ci = carry
            sl = pl.ds(pl.multiple_of(i * 8, 8), 8)
            br, bi = xr_ref[sl, :], xi_ref[sl, :]
            for k, (sr, si) in zip((1, 2, 4), steps):
                tr, ti = _cmul(sr, si, pltpu.roll(br, k, 0), pltpu.roll(bi, k, 0))
                br, bi = br + tr, bi + ti
            tr, ti = _cmul(pr, pi, cr, ci)
            br, bi = br + tr, bi + ti
            xr_ref[sl, :] = br
            xi_ref[sl, :] = bi
            return br[7:8, :], bi[7:8, :]

        cr, ci = lax.fori_loop(0, SCAN_CHUNK // 8, tile, (car_r[0:1, :], car_i[0:1, :]), unroll=4)
        car_r[0:1, :] = cr
        car_i[0:1, :] = ci
        y = (_dot(xr_ref[...].astype(BF16), cbr_ref[...], "nt") - _dot(xi_ref[...].astype(BF16), cbi_ref[...], "nt")
             + d_ref[...] * uf)
        y_ref[...] = y
        yg_ref[...] = jax.nn.gelu(y).astype(BF16)

    return hosted_call(
        body, comm, "ssm_fwd", (LOCAL_BATCH, SSM_LANE_BLOCKS, nt),
        [chan, lane, lane, mat, mat, mat, mat, dsp], [chan, chan, state, state],
        [jax.ShapeDtypeStruct((LOCAL_BATCH, SEQ, SSM_W), F32), jax.ShapeDtypeStruct((LOCAL_BATCH, SEQ, SSM_W), BF16),
         jax.ShapeDtypeStruct((LOCAL_BATCH, SEQ, SSM_STATE_W), F32),
         jax.ShapeDtypeStruct((LOCAL_BATCH, SEQ, SSM_STATE_W), F32)],
        [pltpu.VMEM((8, STATE_BLOCK), F32), pltpu.VMEM((8, STATE_BLOCK), F32)],
        (u, ab_re, ab_im, bb_re, bb_im, cb_re, cb_im, d_skip), ("parallel", "parallel", "arbitrary"))


def ssm_bwd(dyg, y, u, xr, xi, ab_re, ab_im, bb_re, bb_im, cb_re, cb_im, d_skip, comm=None):
    nt = SEQ // SCAN_CHUNK
    ntile = SCAN_CHUNK // 8

    def rev(t):
        return nt - 1 - t

    chan = pl.BlockSpec((None, SCAN_CHUNK, CHAN_BLOCK), lambda j, b, t: (b, rev(t), j))
    state = pl.BlockSpec((None, SCAN_CHUNK, STATE_BLOCK), lambda j, b, t: (b, rev(t), j))
    before = pl.BlockSpec((None, 8, STATE_BLOCK), lambda j, b, t: (b, jnp.maximum(rev(t) * ntile - 1, 0), j))
    mat = pl.BlockSpec((None, CHAN_BLOCK, STATE_BLOCK), lambda j, b, t: (j, 0, 0))
    lane = pl.BlockSpec((1, STATE_BLOCK), lambda j, b, t: (0, j))
    lane8 = pl.BlockSpec((8, STATE_BLOCK), lambda j, b, t: (0, j))
    dsp = pl.BlockSpec((1, CHAN_BLOCK), lambda j, b, t: (0, j))

    def body(dyg_ref, y_ref, u_ref, xr_ref, xi_ref, xrb_ref, xib_ref, ar_ref, ai_ref, bbr_ref, bbi_ref, cbr_ref,
             cbi_ref, d_ref, du_ref, dcbr_ref, dcbi_ref, dbbr_ref, dbbi_ref, dd_ref, dar_ref, dai_ref,
             lam_r, lam_i, car_r, car_i):
        b, t = pl.program_id(1), pl.program_id(2)
        first = jnp.logical_and(b == 0, t == 0)

        @pl.when(t == 0)
        def _():
            car_r[...] = jnp.zeros_like(car_r)
            car_i[...] = jnp.zeros_like(car_i)

        @pl.when(first)
        def _():
            for r in (dcbr_ref, dcbi_ref, dbbr_ref, dbbi_ref, dd_ref, dar_ref, dai_ref):
                r[...] = jnp.zeros_like(r)

        steps, (pr, pi) = _power_tables(ar_ref[...], -ai_ref[...], reverse=True)
        uf = u_ref[...]
        _, gelu_vjp = jax.vjp(jax.nn.gelu, y_ref[...])
        dy = gelu_vjp(dyg_ref[...])[0]
        dyb = dy.astype(BF16)
        dd_ref[...] += _colsum(dy * uf)
        lam_r[...] = _dot(dyb, cbr_ref[...], "nn")
        lam_i[...] = -_dot(dyb, cbi_ref[...], "nn")
        dcbr_ref[...] += _dot(dyb, xr_ref[...].astype(BF16), "tn")
        dcbi_ref[...] -= _dot(dyb, xi_ref[...].astype(BF16), "tn")
        row0 = lax.broadcasted_iota(jnp.int32, (8, STATE_BLOCK), 0) == 0
        has_before = rev(t) > 0
        xrb = jnp.where(has_before, xrb_ref[...], 0.0)
        xib = jnp.where(has_before, xib_ref[...], 0.0)

        def tile(s, carry):
            cr, ci, acc_r, acc_i = carry
            i = ntile - 1 - s
            sl = pl.ds(pl.multiple_of(i * 8, 8), 8)
            gr, gi = lam_r[sl, :], lam_i[sl, :]
            for k, (sr, si) in zip((1, 2, 4), steps):
                tr, ti = _cmul(sr, si, pltpu.roll(gr, 8 - k, 0), pltpu.roll(gi, 8 - k, 0))
                gr, gi = gr + tr, gi + ti
            tr, ti = _cmul(pr, pi, cr, ci)
            gr, gi = gr + tr, gi + ti
            lam_r[sl, :] = gr
            lam_i[sl, :] = gi
            sp = pl.ds(pl.multiple_of(jnp.maximum(i - 1, 0) * 8, 8), 8)
            pvr = jnp.where(i > 0, xr_ref[sp, :], xrb)
            pvi = jnp.where(i > 0, xi_ref[sp, :], xib)
            xsr = jnp.where(row0, pltpu.roll(pvr, 1, 0), pltpu.roll(xr_ref[sl, :], 1, 0))
            xsi = jnp.where(row0, pltpu.roll(pvi, 1, 0), pltpu.roll(xi_ref[sl, :], 1, 0))
            acc_r = acc_r + xsr * gr + xsi * gi
            acc_i = acc_i + xsr * gi - xsi * gr
            return gr[0:1, :], gi[0:1, :], acc_r, acc_i

        zero = jnp.zeros((8, STATE_BLOCK), F32)
        cr, ci, acc_r, acc_i = lax.fori_loop(0, ntile, tile, (car_r[0:1, :], car_i[0:1, :], zero, zero), unroll=2)
        car_r[0:1, :] = cr
        car_i[0:1, :] = ci
        dar_ref[...] += acc_r
        dai_ref[...] += acc_i
        lrb, lib = lam_r[...].astype(BF16), lam_i[...].astype(BF16)
        du = _dot(lrb, bbr_ref[...], "nt") + _dot(lib, bbi_ref[...], "nt") + d_ref[...] * dy
        du_ref[...] = du.astype(BF16)
        ub = uf.astype(BF16)
        dbbr_ref[...] += _dot(ub, lrb, "tn")
        dbbi_ref[...] += _dot(ub, lib, "tn")

    mat_shape = jax.ShapeDtypeStruct((SSM_LANE_BLOCKS, CHAN_BLOCK, STATE_BLOCK), F32)
    return hosted_call(
        body, comm, "ssm_bwd", (SSM_LANE_BLOCKS, LOCAL_BATCH, nt),
        [chan, chan, chan, state, state, before, before, lane, lane, mat, mat, mat, mat, dsp],
        [chan, mat, mat, mat, mat, dsp, lane8, lane8],
        [jax.ShapeDtypeStruct((LOCAL_BATCH, SEQ, SSM_W), BF16), mat_shape, mat_shape, mat_shape, mat_shape,
         jax.ShapeDtypeStruct((1, SSM_W), F32), jax.ShapeDtypeStruct((8, SSM_STATE_W), F32),
         jax.ShapeDtypeStruct((8, SSM_STATE_W), F32)],
        [pltpu.VMEM((SCAN_CHUNK, STATE_BLOCK), F32), pltpu.VMEM((SCAN_CHUNK, STATE_BLOCK), F32),
         pltpu.VMEM((8, STATE_BLOCK), F32), pltpu.VMEM((8, STATE_BLOCK), F32)],
        (dyg, y, u, xr, xi, xr, xi, ab_re, ab_im, bb_re, bb_im, cb_re, cb_im, d_skip),
        ("parallel", "arbitrary", "arbitrary"))


def _merge_fn(g0, g1, attn_d, za, zb):
    return jax.nn.sigmoid(g0) * attn_d + jax.nn.sigmoid(g1) * (za * jax.nn.sigmoid(zb))


def _swiglu_fn(a, b):
    return jax.nn.silu(a) * b


def _reduce_start(names, gw, shard_shapes):
    return swap_comm([_to_slots(n, gw[n], shard_shapes[n]) for n in names])


def _reduce_chip(names, swap, got, core):
    return exchange_comm([add_halves(n, g, r, core) for n, g, r in zip(names, swap.ins, got)])


def local_step(x, target, shards, small, core):
    g_mix, g_ffn, g_final = small["norm_mix_g"], small["norm_ffn_g"], small["norm_final_g"]
    tables = _rope_tables()
    seqs = lambda t: t.reshape(LOCAL_BATCH, SEQ, t.shape[-1])
    toks = lambda t: t.reshape(TOKENS, t.shape[-1])
    shard_shapes = {n: s.shape for n, s in shards.items()}
    w = {}

    def gather(names):
        return gather_comm([shards[n] for n in names])

    def arrived(names, slots):
        for n, s in zip(names, slots):
            w[n] = _from_slots(n, s)

    h, *slots = rowwise(lambda xv, g: (_rms(xv, g),), [x, g_mix], [(D_MODEL, BF16)], "norm_mix", comm=gather(["w_in"]))
    arrived(["w_in"], slots)
    w_qkv, w_u, w_gate = _qkv_order(w["w_in"][:QKV_W]), w["w_in"][QKV_W:QKV_W + SSM_W], w["w_in"][QKV_W + SSM_W:]
    qkv = seqs(matmul(h, w_qkv, "nt", F32, "proj_qkv"))
    u = seqs(matmul(h, w_u, "nt", F32, "proj_u"))
    gl = matmul(h, w_gate, "nt", F32, "proj_gate")
    first = ["w_attn_out", "w_ffn_gate", "w_ffn_up"]
    attn_b, attn, lse, *slots = attn_fwd(qkv, tables, comm=gather(first))
    arrived(first, slots)
    attn_b = toks(attn_b)
    attn_d = matmul(attn_b, w["w_attn_out"], "nn", F32, "attn_out")

    br_t = small["ssm_b_re"].transpose(2, 0, 1)
    bi_t = small["ssm_b_im"].transpose(2, 0, 1)
    log_dt = small["ssm_log_dt"].reshape(32, 1)
    ab_re, ab_im, bb_re_t, bb_im_t = ssm_prep(small["ssm_a_re"], small["ssm_a_im"], log_dt, br_t, bi_t)
    ab = [ab_re.reshape(1, SSM_STATE_W), ab_im.reshape(1, SSM_STATE_W)]
    bb = [_block_diag(bb_re_t).astype(BF16), _block_diag(bb_im_t).astype(BF16)]
    cb = [_block_diag(small["ssm_c_re"].transpose(1, 0, 2)).astype(BF16),
          _block_diag(small["ssm_c_im"].transpose(1, 0, 2)).astype(BF16)]
    d_skip = small["ssm_d"].reshape(1, SSM_W)
    second = ["w_glu", "w_out", "w_ffn_down"]
    y, yg, xr, xi, *slots = ssm_fwd(u, *ab, *bb, *cb, d_skip, comm=gather(second))
    arrived(second, slots)
    yg2 = toks(yg)
    z = matmul(yg2, w["w_glu"], "nn", F32, "glu")
    gate_ins = [(gl, D_MODEL, 0), (gl, D_MODEL, 1), attn_d, (z, D_MODEL, 0), (z, D_MODEL, 1)]
    (merged,) = rowwise(lambda *v: (_merge_fn(*v),), gate_ins, [(D_MODEL, BF16)], "merge")
    x1 = matmul(merged, w["w_out"], "nn", F32, "out_proj", add=x)
    (h2,) = rowwise(lambda xv, g: (_rms(xv, g),), [x1, g_ffn], [(D_MODEL, BF16)], "norm_ffn")
    a, b, act = ffn_in(h2, w["w_ffn_gate"], w["w_ffn_up"])
    x2 = matmul(act, w["w_ffn_down"], "nn", F32, "ffn_down", add=x1)

    def final_fn(xv, g, tgt):
        yv, vjp = jax.vjp(_rms, xv, g)
        err = yv - tgt
        dx, dg = vjp(err * (1.0 / D_MODEL))
        loss = 0.5 * jnp.sum(jnp.mean(err * err, axis=-1, keepdims=True), axis=0, keepdims=True)
        return dx, dx, dg, jnp.broadcast_to(loss, (1, LANES))

    dx2, dx2_b, dg_final, loss = rowwise(final_fn, [x2, g_final, target], [(D_MODEL, F32), (D_MODEL, BF16)],
                                         "final_norm_loss", accs=(D_MODEL, LANES))
    gw, parts = {}, {}
    gw["w_ffn_down"] = matmul(act, dx2_b, "tn", F32, "d_ffn_down")
    da_b, db_b = ffn_in_bwd(dx2_b, w["w_ffn_down"], a, b)
    gw["w_ffn_gate"] = matmul(da_b, h2, "tn", F32, "d_ffn_gate")
    gw["w_ffn_up"] = matmul(db_b, h2, "tn", F32, "d_ffn_up")
    dh2 = matmul(da_b, w["w_ffn_gate"], "nn", F32, "d_h2_gate")
    dh2 = matmul(db_b, w["w_ffn_up"], "nn", F32, "d_h2_up", add=dh2)

    def norm_bwd(xv, g, dh, skip):
        _, vjp = jax.vjp(_rms, xv, g)
        dx, dg = vjp(dh)
        dx = dx + skip
        return dx, dx, dg

    dx1, dx1_b, dg_ffn = rowwise(norm_bwd, [x1, g_ffn, dh2, dx2], [(D_MODEL, F32), (D_MODEL, BF16)],
                                 "norm_ffn_bwd", accs=(D_MODEL,))
    gw["w_out"] = matmul(merged, dx1_b, "tn", F32, "d_out")
    dmerged = matmul(dx1_b, w["w_out"], "nt", F32, "d_merged")

    def merge_bwd(g0, g1, ad, za, zb, dm):
        _, vjp = jax.vjp(_merge_fn, g0, g1, ad, za, zb)
        dg0, dg1, dad, dza, dzb = vjp(dm)
        return jnp.concatenate([dg0, dg1], axis=1), dad, jnp.concatenate([dza, dzb], axis=1)

    ffn = ["w_ffn_down", "w_ffn_gate", "w_ffn_up"]
    swap = _reduce_start(ffn, gw, shard_shapes)
    dgl_b, dattn_d_b, dz_b, *got = rowwise(merge_bwd, gate_ins + [dmerged],
                                           [(GATE_W, BF16), (D_MODEL, BF16), (GATE_W, BF16)], "merge_bwd", comm=swap)
    ffn_exchange = _reduce_chip(ffn, swap, got, core)
    gw["w_attn_out"] = matmul(attn_b, dattn_d_b, "tn", F32, "d_attn_out")
    dattn = seqs(matmul(dattn_d_b, w["w_attn_out"], "nt", F32, "d_attn"))
    gw["w_glu"] = matmul(yg2, dz_b, "tn", F32, "d_glu")
    dyg = seqs(matmul(dz_b, w["w_glu"], "nt", F32, "d_yg"))
    mixer = ["w_out", "w_attn_out", "w_glu"]
    swap = _reduce_start(mixer, gw, shard_shapes)
    dqkv_b, *rest = attn_bwd(qkv, tables, dattn, attn, lse, comm=join_comms([ffn_exchange, swap]))
    for n, p in zip(ffn, rest[:len(ffn)]):
        parts[n] = p
    mixer_exchange = _reduce_chip(mixer, swap, rest[len(ffn):], core)
    dqkv_b = toks(dqkv_b)
    du_b, dcb_re, dcb_im, dbb_re, dbb_im, dd, da_re8, da_im8, *rest = ssm_bwd(
        dyg, y, u, xr, xi, *ab, *bb, *cb, d_skip, comm=mixer_exchange)
    for n, p in zip(mixer, rest):
        parts[n] = p
    du_b = toks(du_b)
    d_qkv = matmul(dqkv_b, h, "tn", F32, "d_w_qkv")
    d_u = matmul(du_b, h, "tn", F32, "d_w_u")
    d_gate = matmul(dgl_b, h, "tn", F32, "d_w_gate")
    gw["w_in"] = jnp.concatenate([_qkv_order(d_qkv, back=True), d_u, d_gate], axis=0)
    swap = _reduce_start(["w_in"], gw, shard_shapes)
    (parts["w_in"],) = run_comm(_reduce_chip(["w_in"], swap, run_comm(swap, "swap_w_in"), core), "exchange_w_in")
    dh = matmul(dqkv_b, w_qkv, "nn", F32, "d_h_qkv")
    dh = matmul(du_b, w_u, "nn", F32, "d_h_u", add=dh)
    dh = matmul(dgl_b, w_gate, "nn", F32, "d_h_gate", add=dh)

    def norm_bwd_last(xv, g, dhv, skip):
        _, vjp = jax.vjp(_rms, xv, g)
        dx, dg = vjp(dhv)
        return dx + skip, dg

    grad_x, dg_mix = rowwise(norm_bwd_last, [x, g_mix, dh, dx1], [(D_MODEL, F32)], "norm_mix_bwd", accs=(D_MODEL,))
    g_ab_re = jnp.sum(da_re8, axis=0).reshape(32, 64)
    g_ab_im = jnp.sum(da_im8, axis=0).reshape(32, 64)
    d_lr, d_li, d_ldt, d_br_t, d_bi_t = ssm_prep_bwd(
        small["ssm_a_re"], small["ssm_a_im"], log_dt, br_t, bi_t,
        g_ab_re, g_ab_im, _block_diag_t(dbb_re), _block_diag_t(dbb_im))
    gs = {
        "norm_mix_g": dg_mix, "ssm_a_re": d_lr, "ssm_a_im": d_li, "ssm_log_dt": d_ldt.reshape(1, 32),
        "ssm_b_re": d_br_t.transpose(1, 2, 0), "ssm_b_im": d_bi_t.transpose(1, 2, 0),
        "ssm_c_re": _block_diag_t(dcb_re).transpose(1, 0, 2), "ssm_c_im": _block_diag_t(dcb_im).transpose(1, 0, 2),
        "ssm_d": dd.reshape(32, 16), "norm_ffn_g": dg_ffn, "norm_final_g": dg_final,
    }
    return loss, grad_x, parts, gs


ANY = pl.BlockSpec(memory_space=pl.ANY)
BIG = ("w_in", "w_glu", "w_attn_out", "w_out", "w_ffn_gate", "w_ffn_up", "w_ffn_down")
TRANSPOSED = ("w_in", "w_ffn_gate", "w_ffn_up")
ROW_SHARDED = TRANSPOSED + ("w_out", "w_ffn_down")
SMALL = ("norm_mix_g", "ssm_a_re", "ssm_a_im", "ssm_log_dt", "ssm_b_re", "ssm_b_im", "ssm_c_re", "ssm_c_im",
         "ssm_d", "norm_ffn_g", "norm_final_g")
WEIGHTS = ("norm_mix_g", "w_in", "ssm_a_re", "ssm_a_im", "ssm_log_dt", "ssm_b_re", "ssm_b_im", "ssm_c_re",
           "ssm_c_im", "ssm_d", "w_glu", "w_attn_out", "w_out", "norm_ffn_g", "w_ffn_gate", "w_ffn_up",
           "w_ffn_down", "norm_final_g")
SMALL_ROWS = 1088
N_BIG = len(BIG)


def _position():
    return lax.axis_index("x"), lax.axis_index("y"), lax.axis_index("c")


def _other_chips(x, y):
    return [(1 - x, y), (x, 1 - y), (1 - x, 1 - y)]


def _remote(src, dst, send_sem, recv_sem, device):
    return pltpu.make_async_remote_copy(src_ref=src, dst_ref=dst, send_sem=send_sem, recv_sem=recv_sem,
                                        device_id=device, device_id_type=MESH)


_later = functools.partial


def _two_level_phases(copies):
    def first(*refs):
        locals_, sends, _, _, _ = copies(*refs)
        for cp in locals_ + sends:
            cp().start()

    def mid(*refs):
        _, _, arrived, passed, _ = copies(*refs)
        for got, cp in zip(arrived, passed):
            got().wait_recv()
            cp().start()

    def last(*refs):
        locals_, sends, _, passed, from_sibling = copies(*refs)
        for cp in from_sibling:
            cp().wait_recv()
        for cp in sends + passed:
            cp().wait_send()
        for cp in locals_:
            cp().wait()

    return first, mid, last


def _half(ref, chip, which):
    rows = ref.shape[1] // 2
    return ref.at[chip, pl.ds(which * rows, rows), :]


class Comm:
    def __init__(self, ins, out_shapes, sems, first, mid, last):
        self.ins, self.out_shapes, self.sems = list(ins), list(out_shapes), list(sems)
        self.first, self.mid, self.last = first, mid, last


def join_comms(comms):
    def cut(refs_by_kind):
        offs, parts = [0, 0, 0], []
        for cm in comms:
            sizes = (len(cm.ins), len(cm.out_shapes), len(cm.sems))
            parts.append(tuple(refs_by_kind[k][offs[k]:offs[k] + sizes[k]] for k in range(3)))
            offs = [o + s for o, s in zip(offs, sizes)]
        return parts

    def phase(which):
        def run(ins, outs, sems):
            for cm, part in zip(comms, cut((ins, outs, sems))):
                fn = getattr(cm, which)
                if fn is not None:
                    fn(*part)
        return run

    return Comm(sum((cm.ins for cm in comms), []), sum((cm.out_shapes for cm in comms), []),
                sum((cm.sems for cm in comms), []), phase("first"), phase("mid"), phase("last"))


def _comm_operands(comm):
    if comm is None:
        return [], [], []
    return comm.ins, comm.out_shapes, comm.sems


def _comm_begin(comm, refs, step, n_steps):
    if comm is None:
        return
    pl.when(step == 0)(lambda: comm.first(*refs))
    if comm.mid is not None:
        pl.when(step == (n_steps * 5) // 8)(lambda: comm.mid(*refs))


def _comm_end(comm, refs, step, n_steps):
    if comm is not None:
        pl.when(step == n_steps - 1)(lambda: comm.last(*refs))


def _comm_refs(comm, refs, n_in, n_out):
    if comm is None:
        return list(refs), None
    ci, co, cs = len(comm.ins), len(comm.out_shapes), len(comm.sems)
    o0 = n_in + ci
    s0 = o0 + n_out + co
    host = list(refs[:n_in]) + list(refs[o0:o0 + n_out]) + list(refs[s0:len(refs) - cs])
    return host, (list(refs[n_in:o0]), list(refs[o0 + n_out:s0]), list(refs[len(refs) - cs:]))


def run_comm(comm, name):
    n_in, n_out = len(comm.ins), len(comm.out_shapes)

    def body(*refs):
        parts = (list(refs[:n_in]), list(refs[n_in:n_in + n_out]), list(refs[n_in + n_out:]))
        comm.first(*parts)
        if comm.mid is not None:
            comm.mid(*parts)
        comm.last(*parts)

    return pl.pallas_call(body, name=name, in_specs=[ANY] * n_in, out_specs=[ANY] * n_out,
                          out_shape=comm.out_shapes, scratch_shapes=comm.sems)(*comm.ins)


def hosted_call(work, comm, name, grid, in_specs, out_specs, out_shape, scratch_shapes, args, semantics):
    c_ins, c_outs, c_sems = _comm_operands(comm)
    n_steps = math.prod(grid)

    def body(*refs):
        host, c_refs = _comm_refs(comm, refs, len(in_specs), len(out_specs))
        step = 0
        for axis, size in enumerate(grid):
            step = step * size + pl.program_id(axis)
        _comm_begin(comm, c_refs, step, n_steps)
        work(*host)
        _comm_end(comm, c_refs, step, n_steps)

    return pl.pallas_call(
        body, name=name, grid=grid, in_specs=list(in_specs) + [ANY] * len(c_ins),
        out_specs=list(out_specs) + [ANY] * len(c_outs), out_shape=list(out_shape) + c_outs,
        scratch_shapes=list(scratch_shapes) + c_sems,
        compiler_params=_params(semantics if comm is None else ("arbitrary",) * len(grid)),
    )(*args, *c_ins)


def gather_comm(shards):
    n = len(shards)

    def copies(srcs, outs, sems):
        send_sems, recv_sems, local_sems = sems
        x, y, c = _position()
        me = 2 * x + y
        sibling = (x, y, 1 - c)
        chips = _other_chips(x, y)
        locals_ = [_later(pltpu.make_async_copy, s, o.at[me], local_sems.at[i])
                   for i, (s, o) in enumerate(zip(srcs, outs))]
        sends, arrived, passed, from_sibling = [], [], [], []
        for j, (px, py) in enumerate(chips):
            for i, (s, o) in enumerate(zip(srcs, outs)):
                rows = s.shape[0] // 2
                sends.append(_later(_remote, s.at[pl.ds(c * rows, rows), :], _half(o, me, c), send_sems.at[i, j],
                                    recv_sems.at[i, j], (px, py, c)))
                got = _half(o, 2 * px + py, c)
                arrived.append(_later(_remote, got, got, send_sems.at[i, j], recv_sems.at[i, j], (px, py, c)))
                passed.append(_later(_remote, got, got, send_sems.at[i, 3 + j], recv_sems.at[i, 3 + j], sibling))
                other = _half(o, 2 * px + py, 1 - c)
                from_sibling.append(_later(_remote, other, other, send_sems.at[i, 3 + j], recv_sems.at[i, 3 + j],
                                           sibling))
        return locals_, sends, arrived, passed, from_sibling

    return Comm(shards, [jax.ShapeDtypeStruct((N_CHIPS,) + s.shape, s.dtype) for s in shards],
                [pltpu.SemaphoreType.DMA((n, 6)), pltpu.SemaphoreType.DMA((n, 6)), pltpu.SemaphoreType.DMA((n,))],
                *_two_level_phases(copies))


def swap_comm(grads):
    n = len(grads)

    def copies(srcs, gots, sems):
        send_sems, recv_sems = sems
        x, y, c = _position()
        out = []
        for i, (s, o) in enumerate(zip(srcs, gots)):
            rows = s.shape[1] // 2
            out.append(_remote(s.at[:, pl.ds((1 - c) * rows, rows), :], o, send_sems.at[i], recv_sems.at[i],
                               (x, y, 1 - c)))
        return out

    def first(srcs, gots, sems):
        for cp in copies(srcs, gots, sems):
            cp.start()

    def last(srcs, gots, sems):
        for cp in copies(srcs, gots, sems):
            cp.wait()

    return Comm(grads, [jax.ShapeDtypeStruct((N_CHIPS, g.shape[1] // 2, g.shape[2]), g.dtype) for g in grads],
                [pltpu.SemaphoreType.DMA((n,)), pltpu.SemaphoreType.DMA((n,))], first, None, last)


def add_halves(name, g, got, core):
    _, half, cols = got.shape
    mine = pl.BlockSpec((None, half, cols), lambda k, c_ref: (k, c_ref[0], 0))
    other = pl.BlockSpec((None, half, cols), lambda k, c_ref: (k, 0, 0))

    def body(c_ref, g_ref, got_ref, o_ref):
        o_ref[...] = (g_ref[...] + got_ref[...]).astype(BF16)

    return pl.pallas_call(
        body, name="add_halves_" + name,
        grid_spec=pltpu.PrefetchScalarGridSpec(num_scalar_prefetch=1, grid=(N_CHIPS,), in_specs=[mine, other],
                                               out_specs=other),
        out_shape=jax.ShapeDtypeStruct(got.shape, BF16),
        compiler_params=_params(("parallel",)),
    )(core, g, got)


def exchange_comm(parts):
    n = len(parts)

    def copies(srcs, outs, sems):
        send_sems, recv_sems, local_sems = sems
        x, y, c = _position()
        me = 2 * x + y
        sibling = (x, y, 1 - c)
        chips = _other_chips(x, y)
        locals_, sends, arrived, passed, from_sibling = [], [], [], [], []
        for i, (s, o) in enumerate(zip(srcs, outs)):
            locals_.append(_later(pltpu.make_async_copy, s.at[me], _half(o, me, c), local_sems.at[i]))
            sends.append(_later(_remote, s.at[me], _half(o, me, c), send_sems.at[i, 3], recv_sems.at[i, 3], sibling))
            other = _half(o, me, 1 - c)
            from_sibling.append(_later(_remote, other, other, send_sems.at[i, 3], recv_sems.at[i, 3], sibling))
        for j, (px, py) in enumerate(chips):
            for i, (s, o) in enumerate(zip(srcs, outs)):
                sends.append(_later(_remote, s.at[2 * px + py], _half(o, me, c), send_sems.at[i, j],
                                    recv_sems.at[i, j], (px, py, c)))
                got = _half(o, 2 * px + py, c)
                arrived.append(_later(_remote, got, got, send_sems.at[i, j], recv_sems.at[i, j], (px, py, c)))
                passed.append(_later(_remote, got, got, send_sems.at[i, 4 + j], recv_sems.at[i, 4 + j], sibling))
                other = _half(o, 2 * px + py, 1 - c)
                from_sibling.append(_later(_remote, other, other, send_sems.at[i, 4 + j], recv_sems.at[i, 4 + j],
                                           sibling))
        return locals_, sends, arrived, passed, from_sibling

    return Comm(parts, [jax.ShapeDtypeStruct((N_CHIPS, 2 * p.shape[1], p.shape[2]), p.dtype) for p in parts],
                [pltpu.SemaphoreType.DMA((n, 7)), pltpu.SemaphoreType.DMA((n, 7)), pltpu.SemaphoreType.DMA((n,))],
                *_two_level_phases(copies))


def allgather_small(pack):
    def body(src_ref, out_ref, send_sems, recv_sems, local_sem):
        x, y, c = _position()
        me = 4 * x + 2 * y + c
        local = pltpu.make_async_copy(src_ref, out_ref.at[me], local_sem)
        local.start()
        flips = [(fx, fy, fc) for fx in (0, 1) for fy in (0, 1) for fc in (0, 1)][1:]
        peers = [(1 - x if fx else x, 1 - y if fy else y, 1 - c if fc else c) for fx, fy, fc in flips]
        sends = [_remote(src_ref, out_ref.at[me], send_sems.at[j], recv_sems.at[j], peer)
                 for j, peer in enumerate(peers)]
        for cp in sends:
            cp.start()
        for j, (px, py, pc) in enumerate(peers):
            got = out_ref.at[4 * px + 2 * py + pc]
            _remote(got, got, send_sems.at[j], recv_sems.at[j], (px, py, pc)).wait_recv()
        for cp in sends:
            cp.wait_send()
        local.wait()

    return pl.pallas_call(
        body, name="allgather_small", in_specs=[ANY], out_specs=ANY,
        out_shape=jax.ShapeDtypeStruct((N_DEV,) + pack.shape, pack.dtype),
        scratch_shapes=[pltpu.SemaphoreType.DMA((7,)), pltpu.SemaphoreType.DMA((7,)), pltpu.SemaphoreType.DMA],
    )(pack)


def _adam_fn(w, g, m, v):
    m = ADAM_B1 * m + (1.0 - ADAM_B1) * g
    v = ADAM_B2 * v + (1.0 - ADAM_B2) * jnp.square(g)
    m_hat = m / (1.0 - ADAM_B1 ** ADAM_STEP)
    v_hat = v / (1.0 - ADAM_B2 ** ADAM_STEP)
    return -ADAM_LR * (m_hat / (jnp.sqrt(v_hat) + ADAM_EPS) + ADAM_WD * w), m, v


def adam_big(name, parts, w, m, v):
    rows, cols = w.shape
    tm = _pick(rows, 384, 16)

    def fn(p0, p1, p2, p3, wv, mv, vv):
        g = ((p0.astype(F32) + p1.astype(F32)) + p2.astype(F32)) + p3.astype(F32)
        return (g,) + _adam_fn(wv, g, mv, vv)

    return rowwise(fn, [parts, w, m, v], [(cols, F32)] * 4, "adam_" + name, tm=tm, rows=rows)


def adam_small(gathered, w, m, v):
    def body(g_ref, w_ref, m_ref, v_ref, go_ref, d_ref, mo_ref, vo_ref):
        g = g_ref[0]
        for k in range(1, N_DEV):
            g = g + g_ref[k]
        go_ref[...] = g
        d_ref[...], mo_ref[...], vo_ref[...] = _adam_fn(w_ref[...], g, m_ref[...], v_ref[...])

    return pl.pallas_call(body, name="adam_small", out_shape=[jax.ShapeDtypeStruct(w.shape, F32)] * 4,
                          compiler_params=_params())(gathered, w, m, v)


def _pack_small(vals, last=None):
    flat = [vals[n].reshape(-1) for n in SMALL]
    if last is not None:
        flat.append(last.reshape(-1))
    flat = jnp.concatenate(flat)
    return jnp.pad(flat, (0, SMALL_ROWS * LANES - flat.shape[0])).reshape(SMALL_ROWS, LANES)


def _unpack_small(pack, shapes):
    flat, out, off = pack.reshape(-1), {}, 0
    for n in SMALL:
        size = math.prod(shapes[n])
        out[n] = flat[off:off + size].reshape(shapes[n])
        off += size
    return out, flat[off]


def _to_slots(name, g, shard_shape):
    rows, cols = shard_shape
    if name in ROW_SHARDED:
        return g.reshape(N_CHIPS, rows, cols)
    return g.reshape(rows, N_CHIPS, cols).transpose(1, 0, 2)


def _from_slots(name, s):
    _, rows, cols = s.shape
    if name in ROW_SHARDED:
        return s.reshape(N_CHIPS * rows, cols)
    return s.transpose(1, 0, 2).reshape(rows, N_CHIPS * cols)


def kernel(x, norm_mix_g, w_in, ssm_a_re, ssm_a_im, ssm_log_dt, ssm_b_re, ssm_b_im, ssm_c_re, ssm_c_im, ssm_d, w_glu, w_attn_out, w_out, norm_ffn_g, w_ffn_gate, w_ffn_up, w_ffn_down, norm_final_g, loss_target, m_norm_mix_g, m_w_in, m_ssm_a_re, m_ssm_a_im, m_ssm_log_dt, m_ssm_b_re, m_ssm_b_im, m_ssm_c_re, m_ssm_c_im, m_ssm_d, m_w_glu, m_w_attn_out, m_w_out, m_norm_ffn_g, m_w_ffn_gate, m_w_ffn_up, m_w_ffn_down, m_norm_final_g, v_norm_mix_g, v_w_in, v_ssm_a_re, v_ssm_a_im, v_ssm_log_dt, v_ssm_b_re, v_ssm_b_im, v_ssm_c_re, v_ssm_c_im, v_ssm_d, v_w_glu, v_w_attn_out, v_w_out, v_norm_ffn_g, v_w_ffn_gate, v_w_ffn_up, v_w_ffn_down, v_norm_final_g):
    given = dict(locals())
    def local(name, prefix=""):
        t = given[prefix + name][0]
        return t.T if name in TRANSPOSED else t

    shard = {n: local(n) for n in BIG}
    shapes = {n: given[n].shape for n in WEIGHTS}

    small = {n: given[n] for n in SMALL}
    small_2d = dict(small)
    for n in ("ssm_a_re", "ssm_a_im", "ssm_b_re", "ssm_b_im", "ssm_c_re", "ssm_c_im", "ssm_d"):
        small_2d[n] = small[n][0]
    small_2d["norm_final_g"] = norm_final_g.reshape(1, D_MODEL)

    core = lax.axis_index("c").astype(jnp.int32).reshape(1)
    loss, grad_x, parts, gs = local_step(x.reshape(TOKENS, D_MODEL), loss_target.reshape(TOKENS, D_MODEL),
                                         {n: shard[n].astype(BF16) for n in BIG}, small_2d, core)

    share = _pack_small({n: gs[n] for n in SMALL}, last=loss)
    everyone = allgather_small(share)
    packs = [_pack_small({n: given[p + n] for n in SMALL}) for p in ("", "m_", "v_")]
    small_out = [_unpack_small(t, shapes) for t in adam_small(everyone, *packs)]
    total_loss = small_out[0][1][()]

    big_out = {}
    for n in BIG:
        res = adam_big(n, parts[n], shard[n], local(n, "m_"), local(n, "v_"))
        big_out[n] = [(t.T if n in TRANSPOSED else t)[None] for t in res]

    outs = [total_loss, grad_x.reshape(LOCAL_BATCH, SEQ, D_MODEL)]
    for kind in range(4):
        for n in WEIGHTS:
            outs.append(big_out[n][kind] if n in BIG else small_out[kind][0][n])
    return tuple(outs)
```

```python
import functools
import math

import jax
import jax.numpy as jnp
from jax import lax
from jax.experimental import pallas as pl
from jax.experimental.pallas import tpu as pltpu

F32 = jnp.float32
BF16 = jnp.bfloat16
MESH = pl.DeviceIdType.MESH

D_MODEL = 1024
SEQ = 2048
LOCAL_BATCH = 2
TOKENS = LOCAL_BATCH * SEQ
HEAD_DIM = 64
HEADS_PER_GROUP = 4
GROUP_W = HEADS_PER_GROUP * HEAD_DIM
N_GROUPS = 3
DILATIONS = (1, 4, 16)
ATTN_BLOCK = 128
ROPE_DIM = 16
ROPE_THETA = 500000.0
QKV_W = 3 * N_GROUPS * GROUP_W
SSM_W = 512
SSM_STATE_W = 2048
SSM_LANE_BLOCKS = 4
GATE_W = 2 * D_MODEL
D_FF = 2816
RMS_EPS = 1e-6
NEG_INF = -1e30
ADAM_LR, ADAM_B1, ADAM_B2, ADAM_EPS, ADAM_WD, ADAM_STEP = 0.001, 0.9, 0.999, 1e-08, 0.01, 10
N_CHIPS = 4
N_DEV = 8

VMEM_LIMIT = 56 * 1024 * 1024
LANES = 128


def _params(sem=None):
    return pltpu.CompilerParams(dimension_semantics=sem, vmem_limit_bytes=VMEM_LIMIT)


def _pick(n, cap, align=LANES):
    best = None
    for d in range(align, min(n, cap) + 1, align):
        if n % d == 0:
            best = d
    return n if best is None or n <= cap else best


_DIMS = {"nn": (((1,), (0,)), ((), ())), "nt": (((1,), (1,)), ((), ())), "tn": (((0,), (0,)), ((), ()))}


def _dot(a, b, mode):
    return lax.dot_general(a, b, _DIMS[mode], preferred_element_type=F32)


def matmul(a, b, mode, out_dtype, name, add=None, comm=None):
    if mode == "nn":
        (m, k), n = a.shape, b.shape[1]
    elif mode == "nt":
        (m, k), n = a.shape, b.shape[0]
    else:
        (k, m), n = a.shape, b.shape[1]
    tn = _pick(n, 1408)
    tk = _pick(k, 2816) if mode != "tn" else _pick(k, 1024)
    tm = _pick(m, 1408)
    out_bytes = jnp.dtype(out_dtype).itemsize

    def need(tm_):
        return 2 * 2 * (tm_ * tk + tk * tn) + tm_ * tn * (4 + 2 * out_bytes + (8 if add is not None else 0))

    while need(tm) > 40 * 1024 * 1024 and tm % 256 == 0:
        tm //= 2
    nk = k // tk
    a_spec = {"nn": pl.BlockSpec((tm, tk), lambda i, j, kk: (i, kk)),
              "nt": pl.BlockSpec((tm, tk), lambda i, j, kk: (i, kk)),
              "tn": pl.BlockSpec((tk, tm), lambda i, j, kk: (kk, i))}[mode]
    b_spec = {"nn": pl.BlockSpec((tk, tn), lambda i, j, kk: (kk, j)),
              "nt": pl.BlockSpec((tn, tk), lambda i, j, kk: (j, kk)),
              "tn": pl.BlockSpec((tk, tn), lambda i, j, kk: (kk, j))}[mode]
    o_spec = pl.BlockSpec((tm, tn), lambda i, j, kk: (i, j))

    def body(a_ref, b_ref, *rest):
        if add is not None:
            add_ref, o_ref, acc_ref = rest
        else:
            o_ref, acc_ref = rest
        part = _dot(a_ref[...], b_ref[...], mode)
        if nk == 1:
            res = part if add is None else part + add_ref[...]
            o_ref[...] = res.astype(out_dtype)
            return
        kk = pl.program_id(2)

        @pl.when(kk == 0)
        def _():
            acc_ref[...] = part

        @pl.when(kk > 0)
        def _():
            acc_ref[...] += part

        @pl.when(kk == nk - 1)
        def _():
            res = acc_ref[...] if add is None else acc_ref[...] + add_ref[...]
            o_ref[...] = res.astype(out_dtype)

    in_specs = [a_spec, b_spec] + ([o_spec] if add is not None else [])
    args = (a, b) + ((add,) if add is not None else ())
    res = hosted_call(
        body, comm, name, (m // tm, n // tn, nk), in_specs, [o_spec], [jax.ShapeDtypeStruct((m, n), out_dtype)],
        [pltpu.VMEM((tm, tn) if nk > 1 else (8, LANES), F32)], args, ("parallel", "parallel", "arbitrary"))
    return res[0] if comm is None else res


FFN_TM, FFN_TN = 512, 1408


def ffn_in(h2, wg_t, wu_t):
    def body(h_ref, wg_ref, wu_ref, a_ref, b_ref, act_ref):
        hv = h_ref[...]
        a, b = _dot(hv, wg_ref[...], "nt"), _dot(hv, wu_ref[...], "nt")
        a_ref[...] = a.astype(BF16)
        b_ref[...] = b.astype(BF16)
        act_ref[...] = _swiglu_fn(a, b).astype(BF16)

    rows = pl.BlockSpec((FFN_TM, D_MODEL), lambda i, j: (i, 0))
    wts = pl.BlockSpec((FFN_TN, D_MODEL), lambda i, j: (j, 0))
    out = pl.BlockSpec((FFN_TM, FFN_TN), lambda i, j: (i, j))
    return pl.pallas_call(
        body, name="ffn_in", grid=(TOKENS // FFN_TM, D_FF // FFN_TN), in_specs=[rows, wts, wts],
        out_specs=[out] * 3, out_shape=[jax.ShapeDtypeStruct((TOKENS, D_FF), BF16)] * 3,
        compiler_params=_params(("parallel", "parallel")),
    )(h2, wg_t, wu_t)


def ffn_in_bwd(dx2_b, wd, a, b):
    def body(dx_ref, wd_ref, a_ref, b_ref, da_ref, db_ref):
        dact = _dot(dx_ref[...], wd_ref[...], "nt")
        _, vjp = jax.vjp(_swiglu_fn, a_ref[...].astype(F32), b_ref[...].astype(F32))
        da, db = vjp(dact)
        da_ref[...] = da.astype(BF16)
        db_ref[...] = db.astype(BF16)

    rows = pl.BlockSpec((FFN_TM, D_MODEL), lambda i, j: (i, 0))
    wts = pl.BlockSpec((FFN_TN, D_MODEL), lambda i, j: (j, 0))
    out = pl.BlockSpec((FFN_TM, FFN_TN), lambda i, j: (i, j))
    return pl.pallas_call(
        body, name="ffn_in_bwd", grid=(TOKENS // FFN_TM, D_FF // FFN_TN), in_specs=[rows, wts, out, out],
        out_specs=[out] * 2, out_shape=[jax.ShapeDtypeStruct((TOKENS, D_FF), BF16)] * 2,
        compiler_params=_params(("parallel", "parallel")),
    )(dx2_b, wd, a, b)


def rowwise(fn, ins, outs, name, accs=(), tm=256, rows=TOKENS, comm=None):
    in_specs, args = [], []
    for item in ins:
        arr, width, blk = item if isinstance(item, tuple) else (item, None, 0)
        if arr.ndim == 3:
            for k in range(arr.shape[0]):
                in_specs.append(pl.BlockSpec((None, tm, arr.shape[2]), functools.partial(lambda i, k_: (k_, i, 0), k_=k)))
                args.append(arr)
            continue
        if arr.shape[0] == 1:
            in_specs.append(pl.BlockSpec(arr.shape, lambda i: (0, 0)))
        elif width is None:
            in_specs.append(pl.BlockSpec((tm, arr.shape[1]), lambda i: (i, 0)))
        else:
            in_specs.append(pl.BlockSpec((tm, width), functools.partial(lambda i, blk_: (i, blk_), blk_=blk)))
        args.append(arr)
    out_specs = [pl.BlockSpec((tm, c), lambda i: (i, 0)) for c, _ in outs]
    out_specs += [pl.BlockSpec((1, c), lambda i: (0, 0)) for c in accs]
    out_shape = [jax.ShapeDtypeStruct((rows, c), dt) for c, dt in outs]
    out_shape += [jax.ShapeDtypeStruct((1, c), F32) for c in accs]
    n_in, n_out = len(args), len(outs)
    c_ins, c_outs, c_sems = _comm_operands(comm)

    def body(*refs):
        refs, c_refs = _comm_refs(comm, refs, n_in, n_out + len(accs))
        step = pl.program_id(0)
        _comm_begin(comm, c_refs, step, rows // tm)
        res = fn(*[r[...] for r in refs[:n_in]])
        for r, v in zip(refs[n_in:n_in + n_out], res[:n_out]):
            r[...] = v.astype(r.dtype)
        first = step == 0
        for r, v in zip(refs[n_in + n_out:], res[n_out:]):
            @pl.when(first)
            def _(r=r, v=v):
                r[...] = v

            @pl.when(jnp.logical_not(first))
            def _(r=r, v=v):
                r[...] += v
        _comm_end(comm, c_refs, step, rows // tm)

    return pl.pallas_call(
        body, name=name, grid=(rows // tm,), in_specs=in_specs + [ANY] * len(c_ins),
        out_specs=out_specs + [ANY] * len(c_outs), out_shape=out_shape + c_outs, scratch_shapes=c_sems,
        compiler_params=_params(("arbitrary",)),
    )(*args, *c_ins)


def _rms(x, g):
    return x * lax.rsqrt(jnp.mean(x * x, axis=-1, keepdims=True) + RMS_EPS) * g


def _colsum(v):
    return jnp.sum(v, axis=0, keepdims=True)


PAIR_W = 2 * HEAD_DIM
N_PAIRS = HEADS_PER_GROUP // 2


def _qkv_order(w_t, back=False):
    dims = (N_PAIRS, N_GROUPS, 3) if back else (3, N_GROUPS, N_PAIRS)
    return w_t.reshape(dims + (PAIR_W, w_t.shape[1])).transpose(2, 1, 0, 3, 4).reshape(QKV_W, w_t.shape[1])


def _rope_tables():
    half = ROPE_DIM // 2
    inv = jnp.power(jnp.float32(ROPE_THETA), -jnp.arange(half, dtype=F32) * 2.0 / ROPE_DIM)
    ang = jnp.arange(SEQ, dtype=F32)[:, None] * inv[None, :]
    cos, sin = jnp.cos(ang), jnp.sin(ang)
    zeros = jnp.zeros((SEQ, HEAD_DIM - ROPE_DIM), F32)
    zh = jnp.zeros((SEQ, half), F32)
    c = jnp.concatenate([cos, cos, zeros + 1.0], axis=1)
    sa = jnp.concatenate([-sin, zh, zeros], axis=1)
    sb = jnp.concatenate([zh, sin, zeros], axis=1)
    return [jnp.tile(t, (1, 2)) for t in (c, sa, sb)]


def _rope_fwd(x, c, sa, sb):
    return x * c + pltpu.roll(x, PAIR_W - 8, 1) * sa + pltpu.roll(x, 8, 1) * sb


def _rope_bwd(dy, c, sa, sb):
    return dy * c + pltpu.roll(dy * sb, PAIR_W - 8, 1) + pltpu.roll(dy * sa, 8, 1)


def _band_masks():
    row = lax.broadcasted_iota(jnp.int32, (ATTN_BLOCK, ATTN_BLOCK), 0)
    col = lax.broadcasted_iota(jnp.int32, (ATTN_BLOCK, ATTN_BLOCK), 1)
    return col <= row, col >= row


def _per_head(fn):
    return jnp.concatenate([fn(slice(h * HEAD_DIM, (h + 1) * HEAD_DIM)) for h in range(2)], axis=1)


def _slab_spec(kind):
    return pl.BlockSpec((None, SEQ, PAIR_W), lambda b, p, g: (b, 0, p * 3 * N_GROUPS + g * 3 + kind))


_TABLE_SPEC = pl.BlockSpec((SEQ, PAIR_W), lambda b, p, g: (0, 0))
_PAIR_SPEC = pl.BlockSpec((None, SEQ, PAIR_W), lambda b, p, g: (b, 0, p))


def _block_rows(dil, r, n):
    return pl.ds(n * (ATTN_BLOCK * dil) + r, ATTN_BLOCK, stride=dil)


def attn_fwd(qkv, tables, comm=None):
    scale = HEAD_DIM ** -0.5

    def body(q_ref, k_ref, v_ref, c_ref, sa_ref, sb_ref, attn_b_ref, attn_ref, lse_ref, qs, ks, o0, o1, o2, l0, l1, l2):
        g = pl.program_id(2)
        c, sa, sb = c_ref[...], sa_ref[...], sb_ref[...]
        qs[...] = _rope_fwd(q_ref[...], c, sa, sb) * scale
        ks[...] = _rope_fwd(k_ref[...], c, sa, sb)
        cur_mask, prev_mask = _band_masks()
        first_head = lax.broadcasted_iota(jnp.int32, (ATTN_BLOCK, PAIR_W), 1) < HEAD_DIM

        def run(dil, o_slab, l_slab):
            nb = SEQ // dil // ATTN_BLOCK

            def block(idx, carry):
                r, n = lax.div(idx, nb), lax.rem(idx, nb)
                cur, prev = _block_rows(dil, r, n), _block_rows(dil, r, jnp.maximum(n - 1, 0))
                q = qs[cur, :].astype(BF16)
                kc, kp = ks[cur, :].astype(BF16), ks[prev, :].astype(BF16)
                vc, vp = v_ref[cur, :].astype(BF16), v_ref[prev, :].astype(BF16)
                pmask = jnp.logical_and(prev_mask, n > 0)
                res, ms = [], []
                for h in range(2):
                    mine = first_head if h == 0 else jnp.logical_not(first_head)
                    qh = jnp.where(mine, q, 0)
                    sc = jnp.where(cur_mask, _dot(qh, kc, "nt"), NEG_INF)
                    sp = jnp.where(pmask, _dot(qh, kp, "nt"), NEG_INF)
                    m = jnp.max(jnp.maximum(sc, sp), axis=-1, keepdims=True)
                    pc, pp = jnp.exp(sc - m).astype(BF16), jnp.exp(sp - m).astype(BF16)
                    res.append(_dot(pc, jnp.where(mine, vc, 1), "nn") + _dot(pp, jnp.where(mine, vp, 1), "nn"))
                    ms.append(m)
                num = jnp.where(first_head, res[0], res[1])
                den = pltpu.roll(jnp.where(first_head, res[1], res[0]), HEAD_DIM, 1)
                o_slab[cur, :] = num / den
                l_slab[cur, :] = jnp.where(first_head, ms[0], ms[1]) + jnp.log(den)
                return carry

            lax.fori_loop(0, SEQ // ATTN_BLOCK, block, 0, unroll=2)

        for gi, (o_slab, l_slab) in enumerate(((o0, l0), (o1, l1), (o2, l2))):
            @pl.when(g == gi)
            def _(gi=gi, o_slab=o_slab, l_slab=l_slab):
                run(DILATIONS[gi], o_slab, l_slab)

        @pl.when(g == N_GROUPS - 1)
        def _():
            a, b, cc = l0[...], l1[...], l2[...]
            m = jnp.maximum(jnp.maximum(a, b), cc)
            e0, e1, e2 = jnp.exp(a - m), jnp.exp(b - m), jnp.exp(cc - m)
            tot = e0 + e1 + e2
            attn = (e0 * o0[...] + e1 * o1[...] + e2 * o2[...]) / tot
            attn_ref[...] = attn
            attn_b_ref[...] = attn.astype(BF16)
            lse_ref[...] = m + jnp.log(tot)

    shape = (LOCAL_BATCH, SEQ, GROUP_W)
    slab = pltpu.VMEM((SEQ, PAIR_W), F32)
    return hosted_call(
        body, comm, "attn_fwd", (LOCAL_BATCH, N_PAIRS, N_GROUPS),
        [_slab_spec(0), _slab_spec(1), _slab_spec(2), _TABLE_SPEC, _TABLE_SPEC, _TABLE_SPEC], [_PAIR_SPEC] * 3,
        [jax.ShapeDtypeStruct(shape, BF16), jax.ShapeDtypeStruct(shape, F32), jax.ShapeDtypeStruct(shape, F32)],
        [slab] * 8, (qkv, qkv, qkv, *tables), ("parallel", "parallel", "arbitrary"))


def attn_bwd(qkv, tables, dattn, attn, lse, comm=None):
    scale = HEAD_DIM ** -0.5

    def body(q_ref, k_ref, v_ref, c_ref, sa_ref, sb_ref, do_ref, out_ref, lse_ref, dqkv_ref, qs, ks, dl, dq_s, dk_s, dv_s):
        g = pl.program_id(2)
        c, sa, sb = c_ref[...], sa_ref[...], sb_ref[...]
        qs[...] = _rope_fwd(q_ref[...], c, sa, sb) * scale
        ks[...] = _rope_fwd(k_ref[...], c, sa, sb)
        prod = do_ref[...] * out_ref[...]
        dl[...] = _per_head(lambda sl: jnp.broadcast_to(jnp.sum(prod[:, sl], axis=-1, keepdims=True), (SEQ, HEAD_DIM)))
        cur_mask, prev_mask = _band_masks()
        first_head = lax.broadcasted_iota(jnp.int32, (ATTN_BLOCK, PAIR_W), 1) < HEAD_DIM

        def run(dil):
            nb = SEQ // dil // ATTN_BLOCK

            def block(idx, carry):
                r, n = lax.div(idx, nb), lax.rem(idx, nb)
                cur = _block_rows(dil, r, n)
                prev = _block_rows(dil, r, jnp.maximum(n - 1, 0))
                nxt = _block_rows(dil, r, jnp.minimum(n + 1, nb - 1))
                q0, q1 = qs[cur, :].astype(BF16), qs[nxt, :].astype(BF16)
                kp, kc = ks[prev, :].astype(BF16), ks[cur, :].astype(BF16)
                vp, vc = v_ref[prev, :].astype(BF16), v_ref[cur, :].astype(BF16)
                do0, do1 = do_ref[cur, :].astype(BF16), do_ref[nxt, :].astype(BF16)
                lse0, lse1, dl0, dl1 = lse_ref[cur, :], lse_ref[nxt, :], dl[cur, :], dl[nxt, :]
                has_prev = jnp.logical_and(prev_mask, n > 0)
                has_next = jnp.logical_and(prev_mask, n < nb - 1)

                def pair(q, k, v, do, lse_col, delta, mask):
                    p = jnp.where(mask, jnp.exp(_dot(q, k, "nt") - lse_col), 0.0)
                    ds = p * (_dot(do, v, "nt") - delta)
                    return p.astype(BF16), ds.astype(BF16)

                dqs, dks, dvs = [], [], []
                for h in range(2):
                    mine = first_head if h == 0 else jnp.logical_not(first_head)
                    one = slice(h * HEAD_DIM, h * HEAD_DIM + 1)
                    q0h, q1h = jnp.where(mine, q0, 0), jnp.where(mine, q1, 0)
                    do0h, do1h = jnp.where(mine, do0, 0), jnp.where(mine, do1, 0)
                    _, ds_a = pair(q0h, kp, vp, do0h, lse0[:, one], dl0[:, one], has_prev)
                    p_b, ds_b = pair(q0h, kc, vc, do0h, lse0[:, one], dl0[:, one], cur_mask)
                    p_c, ds_c = pair(q1h, kc, vc, do1h, lse1[:, one], dl1[:, one], has_next)
                    dqs.append(_dot(ds_a, kp, "nn") + _dot(ds_b, kc, "nn"))
                    dks.append(_dot(ds_b, q0h, "tn") + _dot(ds_c, q1h, "tn"))
                    dvs.append(_dot(p_b, do0h, "tn") + _dot(p_c, do1h, "tn"))
                dq_s[cur, :] = jnp.where(first_head, dqs[0], dqs[1])
                dk_s[cur, :] = dks[0] + dks[1]
                dv_s[cur, :] = dvs[0] + dvs[1]
                return carry

            lax.fori_loop(0, SEQ // ATTN_BLOCK, block, 0, unroll=2)

        for gi in range(N_GROUPS):
            @pl.when(g == gi)
            def _(gi=gi):
                run(DILATIONS[gi])

        dqkv_ref[:, 0:PAIR_W] = _rope_bwd(dq_s[...] * scale, c, sa, sb).astype(BF16)
        dqkv_ref[:, PAIR_W:2 * PAIR_W] = _rope_bwd(dk_s[...], c, sa, sb).astype(BF16)
        dqkv_ref[:, 2 * PAIR_W:] = dv_s[...].astype(BF16)

    slab = pltpu.VMEM((SEQ, PAIR_W), F32)
    return hosted_call(
        body, comm, "attn_bwd", (LOCAL_BATCH, N_PAIRS, N_GROUPS),
        [_slab_spec(0), _slab_spec(1), _slab_spec(2), _TABLE_SPEC, _TABLE_SPEC, _TABLE_SPEC,
         _PAIR_SPEC, _PAIR_SPEC, _PAIR_SPEC],
        [pl.BlockSpec((None, SEQ, 3 * PAIR_W), lambda b, p, g: (b, 0, p * N_GROUPS + g))],
        [jax.ShapeDtypeStruct((LOCAL_BATCH, SEQ, QKV_W), BF16)],
        [slab] * 6, (qkv, qkv, qkv, *tables, dattn, attn, lse), ("parallel", "parallel", "arbitrary"))


def _discretize(lr, li, log_dt, br, bi):
    dt = jnp.exp(log_dt)
    mag = jnp.exp(lr * dt)
    ab_re, ab_im = mag * jnp.cos(li * dt), mag * jnp.sin(li * dt)
    den = lr * lr + li * li
    nr, ni = ab_re - 1.0, ab_im
    f_re = (nr * lr + ni * li) / den
    f_im = (ni * lr - nr * li) / den
    return ab_re, ab_im, f_re[None] * br - f_im[None] * bi, f_re[None] * bi + f_im[None] * br


def ssm_prep(lr, li, log_dt, br, bi):
    def body(lr_ref, li_ref, dt_ref, br_ref, bi_ref, *outs):
        for o, v in zip(outs, _discretize(lr_ref[...], li_ref[...], dt_ref[...], br_ref[...], bi_ref[...])):
            o[...] = v
    shapes = [lr, li, br, bi]
    return pl.pallas_call(body, name="ssm_prep",
                          out_shape=[jax.ShapeDtypeStruct(s.shape, F32) for s in shapes])(lr, li, log_dt, br, bi)


def ssm_prep_bwd(lr, li, log_dt, br, bi, g_ab_re, g_ab_im, g_bb_re, g_bb_im):
    def body(lr_ref, li_ref, dt_ref, br_ref, bi_ref, g0, g1, g2, g3, *outs):
        _, vjp = jax.vjp(_discretize, lr_ref[...], li_ref[...], dt_ref[...], br_ref[...], bi_ref[...])
        for o, v in zip(outs, vjp((g0[...], g1[...], g2[...], g3[...]))):
            o[...] = v
    shapes = [lr, li, log_dt, br, bi]
    return pl.pallas_call(body, name="ssm_prep_bwd",
                          out_shape=[jax.ShapeDtypeStruct(s.shape, F32) for s in shapes])(
        lr, li, log_dt, br, bi, g_ab_re, g_ab_im, g_bb_re, g_bb_im)


def _block_diag(t):
    per = SSM_STATE_W // SSM_LANE_BLOCKS // 64
    g = t.transpose(1, 0, 2).reshape(SSM_LANE_BLOCKS, per, 16, 64)
    eye = jnp.eye(per, dtype=t.dtype)
    return jnp.einsum("jgcn,gh->jgchn", g, eye).reshape(SSM_LANE_BLOCKS, per * 16, per * 64)


def _block_diag_t(m):
    per = SSM_STATE_W // SSM_LANE_BLOCKS // 64
    m5 = m.reshape(SSM_LANE_BLOCKS, per, 16, per, 64)
    d = jnp.einsum("jgchn,gh->jgcn", m5, jnp.eye(per, dtype=m.dtype))
    return d.reshape(SSM_LANE_BLOCKS * per, 16, 64).transpose(1, 0, 2)


def _cmul(ar, ai, br, bi):
    return ar * br - ai * bi, ar * bi + ai * br


def _power_tables(ar, ai, reverse):
    width = ar.shape[1]
    row = lax.broadcasted_iota(jnp.int32, (8, width), 0)
    pows = [(ar, ai)]
    for _ in range(7):
        pows.append(_cmul(pows[-1][0], pows[-1][1], ar, ai))
    steps = []
    for k in (1, 2, 4):
        keep = (row >= k) if not reverse else (row < 8 - k)
        steps.append((jnp.where(keep, pows[k - 1][0], 0.0), jnp.where(keep, pows[k - 1][1], 0.0)))
    cr = jnp.zeros((8, width), F32)
    ci = jnp.zeros((8, width), F32)
    for i in range(8):
        pr, pi = pows[i] if not reverse else pows[7 - i]
        cr = jnp.where(row == i, pr, cr)
        ci = jnp.where(row == i, pi, ci)
    return steps, (cr, ci)


SCAN_CHUNK = 512
STATE_BLOCK = SSM_STATE_W // SSM_LANE_BLOCKS
CHAN_BLOCK = SSM_W // SSM_LANE_BLOCKS


def ssm_fwd(u, ab_re, ab_im, bb_re, bb_im, cb_re, cb_im, d_skip, comm=None):
    nt = SEQ // SCAN_CHUNK
    chan = pl.BlockSpec((None, SCAN_CHUNK, CHAN_BLOCK), lambda b, j, t: (b, t, j))
    state = pl.BlockSpec((None, SCAN_CHUNK, STATE_BLOCK), lambda b, j, t: (b, t, j))
    mat = pl.BlockSpec((None, CHAN_BLOCK, STATE_BLOCK), lambda b, j, t: (j, 0, 0))
    lane = pl.BlockSpec((1, STATE_BLOCK), lambda b, j, t: (0, j))
    dsp = pl.BlockSpec((1, CHAN_BLOCK), lambda b, j, t: (0, j))

    def body(u_ref, ar_ref, ai_ref, bbr_ref, bbi_ref, cbr_ref, cbi_ref, d_ref, y_ref, yg_ref, xr_ref, xi_ref,
             car_r, car_i):
        @pl.when(pl.program_id(2) == 0)
        def _():
            car_r[...] = jnp.zeros_like(car_r)
            car_i[...] = jnp.zeros_like(car_i)

        steps, (pr, pi) = _power_tables(ar_ref[...], ai_ref[...], reverse=False)
        uf = u_ref[...]
        ub = uf.astype(BF16)
        xr_ref[...] = _dot(ub, bbr_ref[...], "nn")
        xi_ref[...] = _dot(ub, bbi_ref[...], "nn")

        def tile(i, carry):
            cr, ci = carry
            sl = pl.ds(pl.multiple_of(i * 8, 8), 8)
            br, bi = xr_ref[sl, :], xi_ref[sl, :]
            for k, (sr, si) in zip((1, 2, 4), steps):
                tr, ti = _cmul(sr, si, pltpu.roll(br, k, 0), pltpu.roll(bi, k, 0))
                br, bi = br + tr, bi + ti
            tr, ti = _cmul(pr, pi, cr, ci)
            br, bi = br + tr, bi + ti
            xr_ref[sl, :] = br
            xi_ref[sl, :] = bi
            return br[7:8, :], bi[7:8, :]

        cr, ci = lax.fori_loop(0, SCAN_CHUNK // 8, tile, (car_r[0:1, :], car_i[0:1, :]), unroll=4)
        car_r[0:1, :] = cr
        car_i[0:1, :] = ci
        y = (_dot(xr_ref[...].astype(BF16), cbr_ref[...], "nt") - _dot(xi_ref[...].astype(BF16), cbi_ref[...], "nt")
             + d_ref[...] * uf)
        y_ref[...] = y
        yg_ref[...] = jax.nn.gelu(y).astype(BF16)

    return hosted_call(
        body, comm, "ssm_fwd", (LOCAL_BATCH, SSM_LANE_BLOCKS, nt),
        [chan, lane, lane, mat, mat, mat, mat, dsp], [chan, chan, state, state],
        [jax.ShapeDtypeStruct((LOCAL_BATCH, SEQ, SSM_W), F32), jax.ShapeDtypeStruct((LOCAL_BATCH, SEQ, SSM_W), BF16),
         jax.ShapeDtypeStruct((LOCAL_BATCH, SEQ, SSM_STATE_W), F32),
         jax.ShapeDtypeStruct((LOCAL_BATCH, SEQ, SSM_STATE_W), F32)],
        [pltpu.VMEM((8, STATE_BLOCK), F32), pltpu.VMEM((8, STATE_BLOCK), F32)],
        (u, ab_re, ab_im, bb_re, bb_im, cb_re, cb_im, d_skip), ("parallel", "parallel", "arbitrary"))


def ssm_bwd(dyg, y, u, xr, xi, ab_re, ab_im, bb_re, bb_im, cb_re, cb_im, d_skip, comm=None):
    nt = SEQ // SCAN_CHUNK
    ntile = SCAN_CHUNK // 8

    def rev(t):
        return nt - 1 - t

    chan = pl.BlockSpec((None, SCAN_CHUNK, CHAN_BLOCK), lambda j, b, t: (b, rev(t), j))
    state = pl.BlockSpec((None, SCAN_CHUNK, STATE_BLOCK), lambda j, b, t: (b, rev(t), j))
    before = pl.BlockSpec((None, 8, STATE_BLOCK), lambda j, b, t: (b, jnp.maximum(rev(t) * ntile - 1, 0), j))
    mat = pl.BlockSpec((None, CHAN_BLOCK, STATE_BLOCK), lambda j, b, t: (j, 0, 0))
    lane = pl.BlockSpec((1, STATE_BLOCK), lambda j, b, t: (0, j))
    lane8 = pl.BlockSpec((8, STATE_BLOCK), lambda j, b, t: (0, j))
    dsp = pl.BlockSpec((1, CHAN_BLOCK), lambda j, b, t: (0, j))

    def body(dyg_ref, y_ref, u_ref, xr_ref, xi_ref, xrb_ref, xib_ref, ar_ref, ai_ref, bbr_ref, bbi_ref, cbr_ref,
             cbi_ref, d_ref, du_ref, dcbr_ref, dcbi_ref, dbbr_ref, dbbi_ref, dd_ref, dar_ref, dai_ref,
             lam_r, lam_i, car_r, car_i):
        b, t = pl.program_id(1), pl.program_id(2)
        first = jnp.logical_and(b == 0, t == 0)

        @pl.when(t == 0)
        def _():
            car_r[...] = jnp.zeros_like(car_r)
            car_i[...] = jnp.zeros_like(car_i)

        @pl.when(first)
        def _():
            for r in (dcbr_ref, dcbi_ref, dbbr_ref, dbbi_ref, dd_ref, dar_ref, dai_ref):
                r[...] = jnp.zeros_like(r)

        steps, (pr, pi) = _power_tables(ar_ref[...], -ai_ref[...], reverse=True)
        uf = u_ref[...]
        _, gelu_vjp = jax.vjp(jax.nn.gelu, y_ref[...])
        dy = gelu_vjp(dyg_ref[...])[0]
        dyb = dy.astype(BF16)
        dd_ref[...] += _colsum(dy * uf)
        lam_r[...] = _dot(dyb, cbr_ref[...], "nn")
        lam_i[...] = -_dot(dyb, cbi_ref[...], "nn")
        dcbr_ref[...] += _dot(dyb, xr_ref[...].astype(BF16), "tn")
        dcbi_ref[...] -= _dot(dyb, xi_ref[...].astype(BF16), "tn")
        row0 = lax.broadcasted_iota(jnp.int32, (8, STATE_BLOCK), 0) == 0
        has_before = rev(t) > 0
        xrb = jnp.where(has_before, xrb_ref[...], 0.0)
        xib = jnp.where(has_before, xib_ref[...], 0.0)

        def tile(s, carry):
            cr, ci, acc_r, acc_i = carry
            i = ntile - 1 - s
            sl = pl.ds(pl.multiple_of(i * 8, 8), 8)
            gr, gi = lam_r[sl, :], lam_i[sl, :]
            for k, (sr, si) in zip((1, 2, 4), steps):
                tr, ti = _cmul(sr, si, pltpu.roll(gr, 8 - k, 0), pltpu.roll(gi, 8 - k, 0))
                gr, gi = gr + tr, gi + ti
            tr, ti = _cmul(pr, pi, cr, ci)
            gr, gi = gr + tr, gi + ti
            lam_r[sl, :] = gr
            lam_i[sl, :] = gi
            sp = pl.ds(pl.multiple_of(jnp.maximum(i - 1, 0) * 8, 8), 8)
            pvr = jnp.where(i > 0, xr_ref[sp, :], xrb)
            pvi = jnp.where(i > 0, xi_ref[sp, :], xib)
            xsr = jnp.where(row0, pltpu.roll(pvr, 1, 0), pltpu.roll(xr_ref[sl, :], 1, 0))
            xsi = jnp.where(row0, pltpu.roll(pvi, 1, 0), pltpu.roll(xi_ref[sl, :], 1, 0))
            acc_r = acc_r + xsr * gr + xsi * gi
            acc_i = acc_i + xsr * gi - xsi * gr
            return gr[0:1, :], gi[0:1, :], acc_r, acc_i

        zero = jnp.zeros((8, STATE_BLOCK), F32)
        cr, ci, acc_r, acc_i = lax.fori_loop(0, ntile, tile, (car_r[0:1, :], car_i[0:1, :], zero, zero), unroll=2)
        car_r[0:1, :] = cr
        car_i[0:1, :] = ci
        dar_ref[...] += acc_r
        dai_ref[...] += acc_i
        lrb, lib = lam_r[...].astype(BF16), lam_i[...].astype(BF16)
        du = _dot(lrb, bbr_ref[...], "nt") + _dot(lib, bbi_ref[...], "nt") + d_ref[...] * dy
        du_ref[...] = du.astype(BF16)
        ub = uf.astype(BF16)
        dbbr_ref[...] += _dot(ub, lrb, "tn")
        dbbi_ref[...] += _dot(ub, lib, "tn")

    mat_shape = jax.ShapeDtypeStruct((SSM_LANE_BLOCKS, CHAN_BLOCK, STATE_BLOCK), F32)
    return hosted_call(
        body, comm, "ssm_bwd", (SSM_LANE_BLOCKS, LOCAL_BATCH, nt),
        [chan, chan, chan, state, state, before, before, lane, lane, mat, mat, mat, mat, dsp],
        [chan, mat, mat, mat, mat, dsp, lane8, lane8],
        [jax.ShapeDtypeStruct((LOCAL_BATCH, SEQ, SSM_W), BF16), mat_shape, mat_shape, mat_shape, mat_shape,
         jax.ShapeDtypeStruct((1, SSM_W), F32), jax.ShapeDtypeStruct((8, SSM_STATE_W), F32),
         jax.ShapeDtypeStruct((8, SSM_STATE_W), F32)],
        [pltpu.VMEM((SCAN_CHUNK, STATE_BLOCK), F32), pltpu.VMEM((SCAN_CHUNK, STATE_BLOCK), F32),
         pltpu.VMEM((8, STATE_BLOCK), F32), pltpu.VMEM((8, STATE_BLOCK), F32)],
        (dyg, y, u, xr, xi, xr, xi, ab_re, ab_im, bb_re, bb_im, cb_re, cb_im, d_skip),
        ("parallel", "arbitrary", "arbitrary"))


def _merge_fn(g0, g1, attn_d, za, zb):
    return jax.nn.sigmoid(g0) * attn_d + jax.nn.sigmoid(g1) * (za * jax.nn.sigmoid(zb))


def _swiglu_fn(a, b):
    return jax.nn.silu(a) * b


def _reduce_start(names, gw, shard_shapes):
    return swap_comm([_to_slots(n, gw[n], shard_shapes[n]) for n in names])


def _reduce_chip(names, swap, got, core):
    return exchange_comm([add_halves(n, g, r, core) for n, g, r in zip(names, swap.ins, got)])


def local_step(x, target, shards, small, core):
    g_mix, g_ffn, g_final = small["norm_mix_g"], small["norm_ffn_g"], small["norm_final_g"]
    tables = _rope_tables()
    seqs = lambda t: t.reshape(LOCAL_BATCH, SEQ, t.shape[-1])
    toks = lambda t: t.reshape(TOKENS, t.shape[-1])
    shard_shapes = {n: s.shape for n, s in shards.items()}
    w = {}

    def gather(names):
        return gather_comm([shards[n] for n in names])

    def arrived(names, slots):
        for n, s in zip(names, slots):
            w[n] = _from_slots(n, s)

    h, *slots = rowwise(lambda xv, g: (_rms(xv, g),), [x, g_mix], [(D_MODEL, BF16)], "norm_mix", comm=gather(["w_in"]))
    arrived(["w_in"], slots)
    w_qkv, w_u, w_gate = _qkv_order(w["w_in"][:QKV_W]), w["w_in"][QKV_W:QKV_W + SSM_W], w["w_in"][QKV_W + SSM_W:]
    qkv = seqs(matmul(h, w_qkv, "nt", F32, "proj_qkv"))
    u = seqs(matmul(h, w_u, "nt", F32, "proj_u"))
    gl = matmul(h, w_gate, "nt", F32, "proj_gate")
    first = ["w_attn_out", "w_ffn_gate", "w_ffn_up"]
    attn_b, attn, lse, *slots = attn_fwd(qkv, tables, comm=gather(first))
    arrived(first, slots)
    attn_b = toks(attn_b)
    attn_d = matmul(attn_b, w["w_attn_out"], "nn", F32, "attn_out")

    br_t = small["ssm_b_re"].transpose(2, 0, 1)
    bi_t = small["ssm_b_im"].transpose(2, 0, 1)
    log_dt = small["ssm_log_dt"].reshape(32, 1)
    ab_re, ab_im, bb_re_t, bb_im_t = ssm_prep(small["ssm_a_re"], small["ssm_a_im"], log_dt, br_t, bi_t)
    ab = [ab_re.reshape(1, SSM_STATE_W), ab_im.reshape(1, SSM_STATE_W)]
    bb = [_block_diag(bb_re_t).astype(BF16), _block_diag(bb_im_t).astype(BF16)]
    cb = [_block_diag(small["ssm_c_re"].transpose(1, 0, 2)).astype(BF16),
          _block_diag(small["ssm_c_im"].transpose(1, 0, 2)).astype(BF16)]
    d_skip = small["ssm_d"].reshape(1, SSM_W)
    second = ["w_glu", "w_out", "w_ffn_down"]
    y, yg, xr, xi, *slots = ssm_fwd(u, *ab, *bb, *cb, d_skip, comm=gather(second))
    arrived(second, slots)
    yg2 = toks(yg)
    z = matmul(yg2, w["w_glu"], "nn", F32, "glu")
    gate_ins = [(gl, D_MODEL, 0), (gl, D_MODEL, 1), attn_d, (z, D_MODEL, 0), (z, D_MODEL, 1)]
    (merged,) = rowwise(lambda *v: (_merge_fn(*v),), gate_ins, [(D_MODEL, BF16)], "merge")
    x1 = matmul(merged, w["w_out"], "nn", F32, "out_proj", add=x)
    (h2,) = rowwise(lambda xv, g: (_rms(xv, g),), [x1, g_ffn], [(D_MODEL, BF16)], "norm_ffn")
    a, b, act = ffn_in(h2, w["w_ffn_gate"], w["w_ffn_up"])
    x2 = matmul(act, w["w_ffn_down"], "nn", F32, "ffn_down", add=x1)

    def final_fn(xv, g, tgt):
        yv, vjp = jax.vjp(_rms, xv, g)
        err = yv - tgt
        dx, dg = vjp(err * (1.0 / D_MODEL))
        loss = 0.5 * jnp.sum(jnp.mean(err * err, axis=-1, keepdims=True), axis=0, keepdims=True)
        return dx, dx, dg, jnp.broadcast_to(loss, (1, LANES))

    dx2, dx2_b, dg_final, loss = rowwise(final_fn, [x2, g_final, target], [(D_MODEL, F32), (D_MODEL, BF16)],
                                         "final_norm_loss", accs=(D_MODEL, LANES))
    gw, parts = {}, {}
    gw["w_ffn_down"] = matmul(act, dx2_b, "tn", F32, "d_ffn_down")
    da_b, db_b = ffn_in_bwd(dx2_b, w["w_ffn_down"], a, b)
    gw["w_ffn_gate"] = matmul(da_b, h2, "tn", F32, "d_ffn_gate")
    gw["w_ffn_up"] = matmul(db_b, h2, "tn", F32, "d_ffn_up")
    dh2 = matmul(da_b, w["w_ffn_gate"], "nn", F32, "d_h2_gate")
    dh2 = matmul(db_b, w["w_ffn_up"], "nn", F32, "d_h2_up", add=dh2)

    def norm_bwd(xv, g, dh, skip):
        _, vjp = jax.vjp(_rms, xv, g)
        dx, dg = vjp(dh)
        dx = dx + skip
        return dx, dx, dg

    dx1, dx1_b, dg_ffn = rowwise(norm_bwd, [x1, g_ffn, dh2, dx2], [(D_MODEL, F32), (D_MODEL, BF16)],
                                 "norm_ffn_bwd", accs=(D_MODEL,))
    gw["w_out"] = matmul(merged, dx1_b, "tn", F32, "d_out")
    dmerged = matmul(dx1_b, w["w_out"], "nt", F32, "d_merged")

    def merge_bwd(g0, g1, ad, za, zb, dm):
        _, vjp = jax.vjp(_merge_fn, g0, g1, ad, za, zb)
        dg0, dg1, dad, dza, dzb = vjp(dm)
        return jnp.concatenate([dg0, dg1], axis=1), dad, jnp.concatenate([dza, dzb], axis=1)

    ffn = ["w_ffn_down", "w_ffn_gate", "w_ffn_up"]
    swap = _reduce_start(ffn, gw, shard_shapes)
    dgl_b, dattn_d_b, dz_b, *got = rowwise(merge_bwd, gate_ins + [dmerged],
                                           [(GATE_W, BF16), (D_MODEL, BF16), (GATE_W, BF16)], "merge_bwd", comm=swap)
    ffn_exchange = _reduce_chip(ffn, swap, got, core)
    gw["w_attn_out"] = matmul(attn_b, dattn_d_b, "tn", F32, "d_attn_out")
    dattn = seqs(matmul(dattn_d_b, w["w_attn_out"], "nt", F32, "d_attn"))
    gw["w_glu"] = matmul(yg2, dz_b, "tn", F32, "d_glu")
    dyg = seqs(matmul(dz_b, w["w_glu"], "nt", F32, "d_yg"))
    mixer = ["w_out", "w_attn_out", "w_glu"]
    swap = _reduce_start(mixer, gw, shard_shapes)
    du_b, dcb_re, dcb_im, dbb_re, dbb_im, dd, da_re8, da_im8, *rest = ssm_bwd(
        dyg, y, u, xr, xi, *ab, *bb, *cb, d_skip, comm=join_comms([ffn_exchange, swap]))
    for n, p in zip(ffn, rest[:len(ffn)]):
        parts[n] = p
    mixer_exchange = _reduce_chip(mixer, swap, rest[len(ffn):], core)
    du_b = toks(du_b)
    g_ab_re = jnp.sum(da_re8, axis=0).reshape(32, 64)
    g_ab_im = jnp.sum(da_im8, axis=0).reshape(32, 64)
    d_lr, d_li, d_ldt, d_br_t, d_bi_t = ssm_prep_bwd(
        small["ssm_a_re"], small["ssm_a_im"], log_dt, br_t, bi_t,
        g_ab_re, g_ab_im, _block_diag_t(dbb_re), _block_diag_t(dbb_im))
    gs = {
        "ssm_a_re": d_lr, "ssm_a_im": d_li, "ssm_log_dt": d_ldt.reshape(1, 32),
        "ssm_b_re": d_br_t.transpose(1, 2, 0), "ssm_b_im": d_bi_t.transpose(1, 2, 0),
        "ssm_c_re": _block_diag_t(dcb_re).transpose(1, 0, 2), "ssm_c_im": _block_diag_t(dcb_im).transpose(1, 0, 2),
        "ssm_d": dd.reshape(32, 16),
    }
    ssm_gather = small_comm(_pack_small(SSM_SMALL, gs, SSM_ROWS))
    dqkv_b, *rest = attn_bwd(qkv, tables, dattn, attn, lse, comm=join_comms([mixer_exchange, ssm_gather]))
    for n, p in zip(mixer, rest):
        parts[n] = p
    ssm_shares = rest[len(mixer)]
    dqkv_b = toks(dqkv_b)
    d_qkv = matmul(dqkv_b, h, "tn", F32, "d_w_qkv")
    d_u = matmul(du_b, h, "tn", F32, "d_w_u")
    d_gate = matmul(dgl_b, h, "tn", F32, "d_w_gate")
    gw["w_in"] = jnp.concatenate([_qkv_order(d_qkv, back=True), d_u, d_gate], axis=0)
    swap = _reduce_start(["w_in"], gw, shard_shapes)
    dh, *got = matmul(dqkv_b, w_qkv, "nn", F32, "d_h_qkv", comm=swap)
    w_in_exchange = _reduce_chip(["w_in"], swap, got, core)
    dh = matmul(du_b, w_u, "nn", F32, "d_h_u", add=dh)
    dh, parts["w_in"] = matmul(dgl_b, w_gate, "nn", F32, "d_h_gate", add=dh, comm=w_in_exchange)

    def norm_bwd_last(xv, g, dhv, skip):
        _, vjp = jax.vjp(_rms, xv, g)
        dx, dg = vjp(dhv)
        return dx + skip, dg

    grad_x, dg_mix = rowwise(norm_bwd_last, [x, g_mix, dh, dx1], [(D_MODEL, F32)], "norm_mix_bwd", accs=(D_MODEL,))
    gs_norm = {"norm_mix_g": dg_mix, "norm_ffn_g": dg_ffn, "norm_final_g": dg_final}
    return loss, grad_x, parts, ssm_shares, gs_norm


ANY = pl.BlockSpec(memory_space=pl.ANY)
BIG = ("w_in", "w_glu", "w_attn_out", "w_out", "w_ffn_gate", "w_ffn_up", "w_ffn_down")
TRANSPOSED = ("w_in", "w_ffn_gate", "w_ffn_up")
ROW_SHARDED = TRANSPOSED + ("w_out", "w_ffn_down")
SMALL = ("norm_mix_g", "ssm_a_re", "ssm_a_im", "ssm_log_dt", "ssm_b_re", "ssm_b_im", "ssm_c_re", "ssm_c_im",
         "ssm_d", "norm_ffn_g", "norm_final_g")
WEIGHTS = ("norm_mix_g", "w_in", "ssm_a_re", "ssm_a_im", "ssm_log_dt", "ssm_b_re", "ssm_b_im", "ssm_c_re",
           "ssm_c_im", "ssm_d", "w_glu", "w_attn_out", "w_out", "norm_ffn_g", "w_ffn_gate", "w_ffn_up",
           "w_ffn_down", "norm_final_g")
SSM_SMALL = SMALL[1:9]
NORM_SMALL = (SMALL[0],) + SMALL[9:]
SSM_ROWS, NORM_ROWS = 1064, 32
N_BIG = len(BIG)


def _position():
    return lax.axis_index("x"), lax.axis_index("y"), lax.axis_index("c")


def _other_chips(x, y):
    return [(1 - x, y), (x, 1 - y), (1 - x, 1 - y)]


def _remote(src, dst, send_sem, recv_sem, device):
    return pltpu.make_async_remote_copy(src_ref=src, dst_ref=dst, send_sem=send_sem, recv_sem=recv_sem,
                                        device_id=device, device_id_type=MESH)


_later = functools.partial


def _two_level_phases(copies):
    def first(*refs):
        locals_, sends, _, _, _ = copies(*refs)
        for cp in locals_ + sends:
            cp().start()

    def mid(*refs):
        _, _, arrived, passed, _ = copies(*refs)
        for got, cp in zip(arrived, passed):
            got().wait_recv()
            cp().start()

    def last(*refs):
        locals_, sends, _, passed, from_sibling = copies(*refs)
        for cp in from_sibling:
            cp().wait_recv()
        for cp in sends + passed:
            cp().wait_send()
        for cp in locals_:
            cp().wait()

    return first, mid, last


def _half(ref, chip, which):
    rows = ref.shape[1] // 2
    return ref.at[chip, pl.ds(which * rows, rows), :]


class Comm:
    def __init__(self, ins, out_shapes, sems, first, mid, last):
        self.ins, self.out_shapes, self.sems = list(ins), list(out_shapes), list(sems)
        self.first, self.mid, self.last = first, mid, last


def join_comms(comms):
    def cut(refs_by_kind):
        offs, parts = [0, 0, 0], []
        for cm in comms:
            sizes = (len(cm.ins), len(cm.out_shapes), len(cm.sems))
            parts.append(tuple(refs_by_kind[k][offs[k]:offs[k] + sizes[k]] for k in range(3)))
            offs = [o + s for o, s in zip(offs, sizes)]
        return parts

    def phase(which):
        def run(ins, outs, sems):
            for cm, part in zip(comms, cut((ins, outs, sems))):
                fn = getattr(cm, which)
                if fn is not None:
                    fn(*part)
        return run

    return Comm(sum((cm.ins for cm in comms), []), sum((cm.out_shapes for cm in comms), []),
                sum((cm.sems for cm in comms), []), phase("first"), phase("mid"), phase("last"))


def _comm_operands(comm):
    if comm is None:
        return [], [], []
    return comm.ins, comm.out_shapes, comm.sems


def _comm_begin(comm, refs, step, n_steps):
    if comm is None:
        return
    pl.when(step == 0)(lambda: comm.first(*refs))
    if comm.mid is not None:
        pl.when(step == (n_steps * 5) // 8)(lambda: comm.mid(*refs))


def _comm_end(comm, refs, step, n_steps):
    if comm is not None:
        pl.when(step == n_steps - 1)(lambda: comm.last(*refs))


def _comm_refs(comm, refs, n_in, n_out):
    if comm is None:
        return list(refs), None
    ci, co, cs = len(comm.ins), len(comm.out_shapes), len(comm.sems)
    o0 = n_in + ci
    s0 = o0 + n_out + co
    host = list(refs[:n_in]) + list(refs[o0:o0 + n_out]) + list(refs[s0:len(refs) - cs])
    return host, (list(refs[n_in:o0]), list(refs[o0 + n_out:s0]), list(refs[len(refs) - cs:]))


def run_comm(comm, name):
    n_in, n_out = len(comm.ins), len(comm.out_shapes)

    def body(*refs):
        parts = (list(refs[:n_in]), list(refs[n_in:n_in + n_out]), list(refs[n_in + n_out:]))
        comm.first(*parts)
        if comm.mid is not None:
            comm.mid(*parts)
        comm.last(*parts)

    return pl.pallas_call(body, name=name, in_specs=[ANY] * n_in, out_specs=[ANY] * n_out,
                          out_shape=comm.out_shapes, scratch_shapes=comm.sems)(*comm.ins)


def hosted_call(work, comm, name, grid, in_specs, out_specs, out_shape, scratch_shapes, args, semantics):
    c_ins, c_outs, c_sems = _comm_operands(comm)
    n_steps = math.prod(grid)

    def body(*refs):
        host, c_refs = _comm_refs(comm, refs, len(in_specs), len(out_specs))
        step = 0
        for axis, size in enumerate(grid):
            step = step * size + pl.program_id(axis)
        _comm_begin(comm, c_refs, step, n_steps)
        work(*host)
        _comm_end(comm, c_refs, step, n_steps)

    return pl.pallas_call(
        body, name=name, grid=grid, in_specs=list(in_specs) + [ANY] * len(c_ins),
        out_specs=list(out_specs) + [ANY] * len(c_outs), out_shape=list(out_shape) + c_outs,
        scratch_shapes=list(scratch_shapes) + c_sems,
        compiler_params=_params(semantics if comm is None else ("arbitrary",) * len(grid)),
    )(*args, *c_ins)


def gather_comm(shards):
    n = len(shards)

    def copies(srcs, outs, sems):
        send_sems, recv_sems, local_sems = sems
        x, y, c = _position()
        me = 2 * x + y
        sibling = (x, y, 1 - c)
        chips = _other_chips(x, y)
        locals_ = [_later(pltpu.make_async_copy, s, o.at[me], local_sems.at[i])
                   for i, (s, o) in enumerate(zip(srcs, outs))]
        sends, arrived, passed, from_sibling = [], [], [], []
        for j, (px, py) in enumerate(chips):
            for i, (s, o) in enumerate(zip(srcs, outs)):
                rows = s.shape[0] // 2
                sends.append(_later(_remote, s.at[pl.ds(c * rows, rows), :], _half(o, me, c), send_sems.at[i, j],
                                    recv_sems.at[i, j], (px, py, c)))
                got = _half(o, 2 * px + py, c)
                arrived.append(_later(_remote, got, got, send_sems.at[i, j], recv_sems.at[i, j], (px, py, c)))
                passed.append(_later(_remote, got, got, send_sems.at[i, 3 + j], recv_sems.at[i, 3 + j], sibling))
                other = _half(o, 2 * px + py, 1 - c)
                from_sibling.append(_later(_remote, other, other, send_sems.at[i, 3 + j], recv_sems.at[i, 3 + j],
                                           sibling))
        return locals_, sends, arrived, passed, from_sibling

    return Comm(shards, [jax.ShapeDtypeStruct((N_CHIPS,) + s.shape, s.dtype) for s in shards],
                [pltpu.SemaphoreType.DMA((n, 6)), pltpu.SemaphoreType.DMA((n, 6)), pltpu.SemaphoreType.DMA((n,))],
                *_two_level_phases(copies))


def swap_comm(grads):
    n = len(grads)

    def copies(srcs, gots, sems):
        send_sems, recv_sems = sems
        x, y, c = _position()
        out = []
        for i, (s, o) in enumerate(zip(srcs, gots)):
            rows = s.shape[1] // 2
            out.append(_remote(s.at[:, pl.ds((1 - c) * rows, rows), :], o, send_sems.at[i], recv_sems.at[i],
                               (x, y, 1 - c)))
        return out

    def first(srcs, gots, sems):
        for cp in copies(srcs, gots, sems):
            cp.start()

    def last(srcs, gots, sems):
        for cp in copies(srcs, gots, sems):
            cp.wait()

    return Comm(grads, [jax.ShapeDtypeStruct((N_CHIPS, g.shape[1] // 2, g.shape[2]), g.dtype) for g in grads],
                [pltpu.SemaphoreType.DMA((n,)), pltpu.SemaphoreType.DMA((n,))], first, None, last)


def add_halves(name, g, got, core):
    _, half, cols = got.shape
    mine = pl.BlockSpec((None, half, cols), lambda k, c_ref: (k, c_ref[0], 0))
    other = pl.BlockSpec((None, half, cols), lambda k, c_ref: (k, 0, 0))

    def body(c_ref, g_ref, got_ref, o_ref):
        o_ref[...] = (g_ref[...] + got_ref[...]).astype(BF16)

    return pl.pallas_call(
        body, name="add_halves_" + name,
        grid_spec=pltpu.PrefetchScalarGridSpec(num_scalar_prefetch=1, grid=(N_CHIPS,), in_specs=[mine, other],
                                               out_specs=other),
        out_shape=jax.ShapeDtypeStruct(got.shape, BF16),
        compiler_params=_params(("parallel",)),
    )(core, g, got)


def exchange_comm(parts):
    n = len(parts)

    def copies(srcs, outs, sems):
        send_sems, recv_sems, local_sems = sems
        x, y, c = _position()
        me = 2 * x + y
        sibling = (x, y, 1 - c)
        chips = _other_chips(x, y)
        locals_, sends, arrived, passed, from_sibling = [], [], [], [], []
        for i, (s, o) in enumerate(zip(srcs, outs)):
            locals_.append(_later(pltpu.make_async_copy, s.at[me], _half(o, me, c), local_sems.at[i]))
            sends.append(_later(_remote, s.at[me], _half(o, me, c), send_sems.at[i, 3], recv_sems.at[i, 3], sibling))
            other = _half(o, me, 1 - c)
            from_sibling.append(_later(_remote, other, other, send_sems.at[i, 3], recv_sems.at[i, 3], sibling))
        for j, (px, py) in enumerate(chips):
            for i, (s, o) in enumerate(zip(srcs, outs)):
                sends.append(_later(_remote, s.at[2 * px + py], _half(o, me, c), send_sems.at[i, j],
                                    recv_sems.at[i, j], (px, py, c)))
                got = _half(o, 2 * px + py, c)
                arrived.append(_later(_remote, got, got, send_sems.at[i, j], recv_sems.at[i, j], (px, py, c)))
                passed.append(_later(_remote, got, got, send_sems.at[i, 4 + j], recv_sems.at[i, 4 + j], sibling))
                other = _half(o, 2 * px + py, 1 - c)
                from_sibling.append(_later(_remote, other, other, send_sems.at[i, 4 + j], recv_sems.at[i, 4 + j],
                                           sibling))
        return locals_, sends, arrived, passed, from_sibling

    return Comm(parts, [jax.ShapeDtypeStruct((N_CHIPS, 2 * p.shape[1], p.shape[2]), p.dtype) for p in parts],
                [pltpu.SemaphoreType.DMA((n, 7)), pltpu.SemaphoreType.DMA((n, 7)), pltpu.SemaphoreType.DMA((n,))],
                *_two_level_phases(copies))


def small_comm(pack):
    def copies(srcs, outs, sems):
        (src_ref,), (out_ref,), (send_sems, recv_sems, local_sem) = srcs, outs, sems
        x, y, c = _position()
        me = 4 * x + 2 * y + c
        flips = [(fx, fy, fc) for fx in (0, 1) for fy in (0, 1) for fc in (0, 1)][1:]
        peers = [(1 - x if fx else x, 1 - y if fy else y, 1 - c if fc else c) for fx, fy, fc in flips]
        local = _later(pltpu.make_async_copy, src_ref, out_ref.at[me], local_sem)
        sends = [_later(_remote, src_ref, out_ref.at[me], send_sems.at[j], recv_sems.at[j], peer)
                 for j, peer in enumerate(peers)]
        arrived = []
        for j, (px, py, pc) in enumerate(peers):
            got = out_ref.at[4 * px + 2 * py + pc]
            arrived.append(_later(_remote, got, got, send_sems.at[j], recv_sems.at[j], (px, py, pc)))
        return local, sends, arrived

    def first(*refs):
        local, sends, _ = copies(*refs)
        for cp in [local] + sends:
            cp().start()

    def last(*refs):
        local, sends, arrived = copies(*refs)
        for cp in arrived:
            cp().wait_recv()
        for cp in sends:
            cp().wait_send()
        local().wait()

    return Comm([pack], [jax.ShapeDtypeStruct((N_DEV,) + pack.shape, pack.dtype)],
                [pltpu.SemaphoreType.DMA((7,)), pltpu.SemaphoreType.DMA((7,)), pltpu.SemaphoreType.DMA],
                first, None, last)


def _adam_fn(w, g, m, v):
    m = ADAM_B1 * m + (1.0 - ADAM_B1) * g
    v = ADAM_B2 * v + (1.0 - ADAM_B2) * jnp.square(g)
    m_hat = m / (1.0 - ADAM_B1 ** ADAM_STEP)
    v_hat = v / (1.0 - ADAM_B2 ** ADAM_STEP)
    return -ADAM_LR * (m_hat / (jnp.sqrt(v_hat) + ADAM_EPS) + ADAM_WD * w), m, v


def adam_big(name, parts, w, m, v):
    rows, cols = w.shape
    tm = _pick(rows, 384, 16)

    def fn(p0, p1, p2, p3, wv, mv, vv):
        g = ((p0.astype(F32) + p1.astype(F32)) + p2.astype(F32)) + p3.astype(F32)
        return (g,) + _adam_fn(wv, g, mv, vv)

    return rowwise(fn, [parts, w, m, v], [(cols, F32)] * 4, "adam_" + name, tm=tm, rows=rows)


def adam_small(name, gathered, w, m, v):
    def body(g_ref, w_ref, m_ref, v_ref, go_ref, d_ref, mo_ref, vo_ref):
        g = g_ref[0]
        for k in range(1, N_DEV):
            g = g + g_ref[k]
        go_ref[...] = g
        d_ref[...], mo_ref[...], vo_ref[...] = _adam_fn(w_ref[...], g, m_ref[...], v_ref[...])

    return pl.pallas_call(body, name=name, out_shape=[jax.ShapeDtypeStruct(w.shape, F32)] * 4,
                          compiler_params=_params())(gathered, w, m, v)


def _pack_small(names, vals, rows, last=None):
    flat = [vals[n].reshape(-1) for n in names]
    if last is not None:
        flat.append(last.reshape(-1))
    flat = jnp.concatenate(flat)
    return jnp.pad(flat, (0, rows * LANES - flat.shape[0])).reshape(rows, LANES)


def _unpack_small(names, pack, shapes):
    flat, out, off = pack.reshape(-1), {}, 0
    for n in names:
        size = math.prod(shapes[n])
        out[n] = flat[off:off + size].reshape(shapes[n])
        off += size
    return out, flat[off]


def _to_slots(name, g, shard_shape):
    rows, cols = shard_shape
    if name in ROW_SHARDED:
        return g.reshape(N_CHIPS, rows, cols)
    return g.reshape(rows, N_CHIPS, cols).transpose(1, 0, 2)


def _from_slots(name, s):
    _, rows, cols = s.shape
    if name in ROW_SHARDED:
        return s.reshape(N_CHIPS * rows, cols)
    return s.transpose(1, 0, 2).reshape(rows, N_CHIPS * cols)


def kernel(x, norm_mix_g, w_in, ssm_a_re, ssm_a_im, ssm_log_dt, ssm_b_re, ssm_b_im, ssm_c_re, ssm_c_im, ssm_d, w_glu, w_attn_out, w_out, norm_ffn_g, w_ffn_gate, w_ffn_up, w_ffn_down, norm_final_g, loss_target, m_norm_mix_g, m_w_in, m_ssm_a_re, m_ssm_a_im, m_ssm_log_dt, m_ssm_b_re, m_ssm_b_im, m_ssm_c_re, m_ssm_c_im, m_ssm_d, m_w_glu, m_w_attn_out, m_w_out, m_norm_ffn_g, m_w_ffn_gate, m_w_ffn_up, m_w_ffn_down, m_norm_final_g, v_norm_mix_g, v_w_in, v_ssm_a_re, v_ssm_a_im, v_ssm_log_dt, v_ssm_b_re, v_ssm_b_im, v_ssm_c_re, v_ssm_c_im, v_ssm_d, v_w_glu, v_w_attn_out, v_w_out, v_norm_ffn_g, v_w_ffn_gate, v_w_ffn_up, v_w_ffn_down, v_norm_final_g):
    given = dict(locals())
    def local(name, prefix=""):
        t = given[prefix + name][0]
        return t.T if name in TRANSPOSED else t

    shard = {n: local(n) for n in BIG}
    shapes = {n: given[n].shape for n in WEIGHTS}

    small = {n: given[n] for n in SMALL}
    small_2d = dict(small)
    for n in ("ssm_a_re", "ssm_a_im", "ssm_b_re", "ssm_b_im", "ssm_c_re", "ssm_c_im", "ssm_d"):
        small_2d[n] = small[n][0]
    small_2d["norm_final_g"] = norm_final_g.reshape(1, D_MODEL)

    core = lax.axis_index("c").astype(jnp.int32).reshape(1)
    loss, grad_x, parts, ssm_shares, gs_norm = local_step(
        x.reshape(TOKENS, D_MODEL), loss_target.reshape(TOKENS, D_MODEL),
        {n: shard[n].astype(BF16) for n in BIG}, small_2d, core)

    (norm_shares,) = run_comm(small_comm(_pack_small(NORM_SMALL, gs_norm, NORM_ROWS, last=loss)), "gather_norm_grads")
    small_out = [{} for _ in range(4)]
    for names, rows, shares in ((SSM_SMALL, SSM_ROWS, ssm_shares), (NORM_SMALL, NORM_ROWS, norm_shares)):
        packs = [_pack_small(names, {n: given[p + n] for n in names}, rows) for p in ("", "m_", "v_")]
        for kind, t in enumerate(adam_small("adam_" + names[0], shares, *packs)):
            vals, after = _unpack_small(names, t, shapes)
            small_out[kind].update(vals)
            if kind == 0:
                total_loss = after

    big_out = {}
    for n in BIG:
        res = adam_big(n, parts[n], shard[n], local(n, "m_"), local(n, "v_"))
        big_out[n] = [(t.T if n in TRANSPOSED else t)[None] for t in res]

    outs = [total_loss, grad_x.reshape(LOCAL_BATCH, SEQ, D_MODEL)]
    for kind in range(4):
        for n in WEIGHTS:
            outs.append(big_out[n][kind] if n in BIG else small_out[kind][n])
    return tuple(outs)
```

```python
import functools
import math

import jax
import jax.numpy as jnp
from jax import lax
from jax.experimental import pallas as pl
from jax.experimental.pallas import tpu as pltpu

F32 = jnp.float32
BF16 = jnp.bfloat16
MESH = pl.DeviceIdType.MESH

D_MODEL = 1024
SEQ = 2048
LOCAL_BATCH = 2
TOKENS = LOCAL_BATCH * SEQ
HEAD_DIM = 64
HEADS_PER_GROUP = 4
GROUP_W = HEADS_PER_GROUP * HEAD_DIM
N_GROUPS = 3
DILATIONS = (1, 4, 16)
ATTN_BLOCK = 128
ROPE_DIM = 16
ROPE_THETA = 500000.0
QKV_W = 3 * N_GROUPS * GROUP_W
SSM_W = 512
SSM_STATE_W = 2048
SSM_LANE_BLOCKS = 4
GATE_W = 2 * D_MODEL
D_FF = 2816
RMS_EPS = 1e-6
NEG_INF = -1e30
ADAM_LR, ADAM_B1, ADAM_B2, ADAM_EPS, ADAM_WD, ADAM_STEP = 0.001, 0.9, 0.999, 1e-08, 0.01, 10
N_CHIPS = 4
N_DEV = 8

VMEM_LIMIT = 56 * 1024 * 1024
LANES = 128


def _params(sem=None):
    return pltpu.CompilerParams(dimension_semantics=sem, vmem_limit_bytes=VMEM_LIMIT)


def _pick(n, cap, align=LANES):
    best = None
    for d in range(align, min(n, cap) + 1, align):
        if n % d == 0:
            best = d
    return n if best is None or n <= cap else best


_DIMS = {"nn": (((1,), (0,)), ((), ())), "nt": (((1,), (1,)), ((), ())), "tn": (((0,), (0,)), ((), ()))}


def _dot(a, b, mode):
    return lax.dot_general(a, b, _DIMS[mode], preferred_element_type=F32)


def matmul(a, b, mode, out_dtype, name, add=None, comm=None):
    if mode == "nn":
        (m, k), n = a.shape, b.shape[1]
    elif mode == "nt":
        (m, k), n = a.shape, b.shape[0]
    else:
        (k, m), n = a.shape, b.shape[1]
    tn = _pick(n, 1408)
    tk = _pick(k, 2816) if mode != "tn" else _pick(k, 1024)
    tm = _pick(m, 1408)
    out_bytes = jnp.dtype(out_dtype).itemsize

    def need(tm_):
        return 2 * 2 * (tm_ * tk + tk * tn) + tm_ * tn * (4 + 2 * out_bytes + (8 if add is not None else 0))

    while need(tm) > 40 * 1024 * 1024 and tm % 256 == 0:
        tm //= 2
    nk = k // tk
    a_spec = {"nn": pl.BlockSpec((tm, tk), lambda i, j, kk: (i, kk)),
              "nt": pl.BlockSpec((tm, tk), lambda i, j, kk: (i, kk)),
              "tn": pl.BlockSpec((tk, tm), lambda i, j, kk: (kk, i))}[mode]
    b_spec = {"nn": pl.BlockSpec((tk, tn), lambda i, j, kk: (kk, j)),
              "nt": pl.BlockSpec((tn, tk), lambda i, j, kk: (j, kk)),
              "tn": pl.BlockSpec((tk, tn), lambda i, j, kk: (kk, j))}[mode]
    o_spec = pl.BlockSpec((tm, tn), lambda i, j, kk: (i, j))

    def body(a_ref, b_ref, *rest):
        if add is not None:
            add_ref, o_ref, acc_ref = rest
        else:
            o_ref, acc_ref = rest
        part = _dot(a_ref[...], b_ref[...], mode)
        if nk == 1:
            res = part if add is None else part + add_ref[...]
            o_ref[...] = res.astype(out_dtype)
            return
        kk = pl.program_id(2)

        @pl.when(kk == 0)
        def _():
            acc_ref[...] = part

        @pl.when(kk > 0)
        def _():
            acc_ref[...] += part

        @pl.when(kk == nk - 1)
        def _():
            res = acc_ref[...] if add is None else acc_ref[...] + add_ref[...]
            o_ref[...] = res.astype(out_dtype)

    in_specs = [a_spec, b_spec] + ([o_spec] if add is not None else [])
    args = (a, b) + ((add,) if add is not None else ())
    res = hosted_call(
        body, comm, name, (m // tm, n // tn, nk), in_specs, [o_spec], [jax.ShapeDtypeStruct((m, n), out_dtype)],
        [pltpu.VMEM((tm, tn) if nk > 1 else (8, LANES), F32)], args, ("parallel", "parallel", "arbitrary"))
    return res[0] if comm is None else res


FFN_TM, FFN_TN = 512, 1408


def ffn_in(h2, wg_t, wu_t):
    def body(h_ref, wg_ref, wu_ref, a_ref, b_ref, act_ref):
        hv = h_ref[...]
        a, b = _dot(hv, wg_ref[...], "nt"), _dot(hv, wu_ref[...], "nt")
        a_ref[...] = a.astype(BF16)
        b_ref[...] = b.astype(BF16)
        act_ref[...] = _swiglu_fn(a, b).astype(BF16)

    rows = pl.BlockSpec((FFN_TM, D_MODEL), lambda i, j: (i, 0))
    wts = pl.BlockSpec((FFN_TN, D_MODEL), lambda i, j: (j, 0))
    out = pl.BlockSpec((FFN_TM, FFN_TN), lambda i, j: (i, j))
    return pl.pallas_call(
        body, name="ffn_in", grid=(TOKENS // FFN_TM, D_FF // FFN_TN), in_specs=[rows, wts, wts],
        out_specs=[out] * 3, out_shape=[jax.ShapeDtypeStruct((TOKENS, D_FF), BF16)] * 3,
        compiler_params=_params(("parallel", "parallel")),
    )(h2, wg_t, wu_t)


def ffn_in_bwd(dx2_b, wd, a, b):
    def body(dx_ref, wd_ref, a_ref, b_ref, da_ref, db_ref):
        dact = _dot(dx_ref[...], wd_ref[...], "nt")
        _, vjp = jax.vjp(_swiglu_fn, a_ref[...].astype(F32), b_ref[...].astype(F32))
        da, db = vjp(dact)
        da_ref[...] = da.astype(BF16)
        db_ref[...] = db.astype(BF16)

    rows = pl.BlockSpec((FFN_TM, D_MODEL), lambda i, j: (i, 0))
    wts = pl.BlockSpec((FFN_TN, D_MODEL), lambda i, j: (j, 0))
    out = pl.BlockSpec((FFN_TM, FFN_TN), lambda i, j: (i, j))
    return pl.pallas_call(
        body, name="ffn_in_bwd", grid=(TOKENS // FFN_TM, D_FF // FFN_TN), in_specs=[rows, wts, out, out],
        out_specs=[out] * 2, out_shape=[jax.ShapeDtypeStruct((TOKENS, D_FF), BF16)] * 2,
        compiler_params=_params(("parallel", "parallel")),
    )(dx2_b, wd, a, b)


def mix_in_bwd(grads, weights, partial, x, g, skip, comm=None):
    n = len(grads)
    tm = 512

    def body(*refs):
        a_refs, b_refs = refs[:n], refs[n:2 * n]
        part_ref, x_ref, g_ref, skip_ref, gx_ref, dg_ref = refs[2 * n:]
        dh = part_ref[...]
        for a_ref, b_ref in zip(a_refs, b_refs):
            dh = dh + _dot(a_ref[...], b_ref[...], "nn")
        _, vjp = jax.vjp(_rms, x_ref[...], g_ref[...])
        dx, dg = vjp(dh)
        gx_ref[...] = dx + skip_ref[...]
        first = pl.program_id(0) == 0

        @pl.when(first)
        def _():
            dg_ref[...] = dg

        @pl.when(jnp.logical_not(first))
        def _():
            dg_ref[...] += dg

    rows = pl.BlockSpec((tm, D_MODEL), lambda i: (i, 0))
    gain = pl.BlockSpec((1, D_MODEL), lambda i: (0, 0))
    in_specs = [pl.BlockSpec((tm, a.shape[1]), lambda i: (i, 0)) for a in grads]
    in_specs += [pl.BlockSpec(b.shape, lambda i: (0, 0)) for b in weights]
    return hosted_call(
        body, comm, "mix_in_bwd", (TOKENS // tm,), in_specs + [rows, rows, gain, rows], [rows, gain],
        [jax.ShapeDtypeStruct((TOKENS, D_MODEL), F32), jax.ShapeDtypeStruct((1, D_MODEL), F32)], [],
        (*grads, *weights, partial, x, g, skip), ("arbitrary",))


def rowwise(fn, ins, outs, name, accs=(), tm=256, rows=TOKENS, comm=None):
    in_specs, args = [], []
    for item in ins:
        arr, width, blk = item if isinstance(item, tuple) else (item, None, 0)
        if arr.ndim == 3:
            for k in range(arr.shape[0]):
                in_specs.append(pl.BlockSpec((None, tm, arr.shape[2]), functools.partial(lambda i, k_: (k_, i, 0), k_=k)))
                args.append(arr)
            continue
        if arr.shape[0] == 1:
            in_specs.append(pl.BlockSpec(arr.shape, lambda i: (0, 0)))
        elif width is None:
            in_specs.append(pl.BlockSpec((tm, arr.shape[1]), lambda i: (i, 0)))
        else:
            in_specs.append(pl.BlockSpec((tm, width), functools.partial(lambda i, blk_: (i, blk_), blk_=blk)))
        args.append(arr)
    out_specs = [pl.BlockSpec((tm, c), lambda i: (i, 0)) for c, _ in outs]
    out_specs += [pl.BlockSpec((1, c), lambda i: (0, 0)) for c in accs]
    out_shape = [jax.ShapeDtypeStruct((rows, c), dt) for c, dt in outs]
    out_shape += [jax.ShapeDtypeStruct((1, c), F32) for c in accs]
    n_in, n_out = len(args), len(outs)
    c_ins, c_outs, c_sems = _comm_operands(comm)

    def body(*refs):
        refs, c_refs = _comm_refs(comm, refs, n_in, n_out + len(accs))
        step = pl.program_id(0)
        _comm_begin(comm, c_refs, step, rows // tm)
        res = fn(*[r[...] for r in refs[:n_in]])
        for r, v in zip(refs[n_in:n_in + n_out], res[:n_out]):
            r[...] = v.astype(r.dtype)
        first = step == 0
        for r, v in zip(refs[n_in + n_out:], res[n_out:]):
            @pl.when(first)
            def _(r=r, v=v):
                r[...] = v

            @pl.when(jnp.logical_not(first))
            def _(r=r, v=v):
                r[...] += v
        _comm_end(comm, c_refs, step, rows // tm)

    return pl.pallas_call(
        body, name=name, grid=(rows // tm,), in_specs=in_specs + [ANY] * len(c_ins),
        out_specs=out_specs + [ANY] * len(c_outs), out_shape=out_shape + c_outs, scratch_shapes=c_sems,
        compiler_params=_params(("arbitrary",)),
    )(*args, *c_ins)


def _rms(x, g):
    return x * lax.rsqrt(jnp.mean(x * x, axis=-1, keepdims=True) + RMS_EPS) * g


def _colsum(v):
    return jnp.sum(v, axis=0, keepdims=True)


PAIR_W = 2 * HEAD_DIM
N_PAIRS = HEADS_PER_GROUP // 2


def _qkv_order(w_t, back=False):
    dims = (N_PAIRS, N_GROUPS, 3) if back else (3, N_GROUPS, N_PAIRS)
    return w_t.reshape(dims + (PAIR_W, w_t.shape[1])).transpose(2, 1, 0, 3, 4).reshape(QKV_W, w_t.shape[1])


def _rope_tables():
    half = ROPE_DIM // 2
    inv = jnp.power(jnp.float32(ROPE_THETA), -jnp.arange(half, dtype=F32) * 2.0 / ROPE_DIM)
    ang = jnp.arange(SEQ, dtype=F32)[:, None] * inv[None, :]
    cos, sin = jnp.cos(ang), jnp.sin(ang)
    zeros = jnp.zeros((SEQ, HEAD_DIM - ROPE_DIM), F32)
    zh = jnp.zeros((SEQ, half), F32)
    c = jnp.concatenate([cos, cos, zeros + 1.0], axis=1)
    sa = jnp.concatenate([-sin, zh, zeros], axis=1)
    sb = jnp.concatenate([zh, sin, zeros], axis=1)
    return [jnp.tile(t, (1, 2)) for t in (c, sa, sb)]


def _rope_fwd(x, c, sa, sb):
    return x * c + pltpu.roll(x, PAIR_W - 8, 1) * sa + pltpu.roll(x, 8, 1) * sb


def _rope_bwd(dy, c, sa, sb):
    return dy * c + pltpu.roll(dy * sb, PAIR_W - 8, 1) + pltpu.roll(dy * sa, 8, 1)


def _band_masks():
    row = lax.broadcasted_iota(jnp.int32, (ATTN_BLOCK, ATTN_BLOCK), 0)
    col = lax.broadcasted_iota(jnp.int32, (ATTN_BLOCK, ATTN_BLOCK), 1)
    return col <= row, col >= row


def _stack_rows(t):
    return jnp.concatenate([t, t], axis=0)


def _stack_heads(t, first_head):
    return jnp.concatenate([jnp.where(first_head, t, 0), jnp.where(first_head, 0, t)], axis=0)


def _per_head(fn):
    return jnp.concatenate([fn(slice(h * HEAD_DIM, (h + 1) * HEAD_DIM)) for h in range(2)], axis=1)


def _slab_spec(kind):
    return pl.BlockSpec((None, SEQ, PAIR_W), lambda b, p, g: (b, 0, p * 3 * N_GROUPS + g * 3 + kind))


_TABLE_SPEC = pl.BlockSpec((SEQ, PAIR_W), lambda b, p, g: (0, 0))
_PAIR_SPEC = pl.BlockSpec((None, SEQ, PAIR_W), lambda b, p, g: (b, 0, p))


def _block_rows(dil, r, n):
    return pl.ds(n * (ATTN_BLOCK * dil) + r, ATTN_BLOCK, stride=dil)


def attn_fwd(qkv, tables, comm=None):
    scale = HEAD_DIM ** -0.5

    def body(q_ref, k_ref, v_ref, c_ref, sa_ref, sb_ref, attn_b_ref, attn_ref, lse_ref, qs, ks, o0, o1, o2, l0, l1, l2):
        g = pl.program_id(2)
        c, sa, sb = c_ref[...], sa_ref[...], sb_ref[...]
        qs[...] = _rope_fwd(q_ref[...], c, sa, sb) * scale
        ks[...] = _rope_fwd(k_ref[...], c, sa, sb)
        cur_mask, prev_mask = _band_masks()
        first_head = lax.broadcasted_iota(jnp.int32, (ATTN_BLOCK, PAIR_W), 1) < HEAD_DIM

        def run(dil, o_slab, l_slab):
            nb = SEQ // dil // ATTN_BLOCK

            def block(idx, carry):
                r, n = lax.div(idx, nb), lax.rem(idx, nb)
                cur, prev = _block_rows(dil, r, n), _block_rows(dil, r, jnp.maximum(n - 1, 0))
                q = qs[cur, :].astype(BF16)
                kc, kp = ks[cur, :].astype(BF16), ks[prev, :].astype(BF16)
                vc, vp = v_ref[cur, :].astype(BF16), v_ref[prev, :].astype(BF16)
                q2 = _stack_heads(q, first_head)
                mask = _stack_rows(jnp.concatenate([jnp.logical_and(prev_mask, n > 0), cur_mask], axis=1))
                s2 = jnp.where(mask, _dot(q2, jnp.concatenate([kp, kc], axis=0), "nt"), NEG_INF)
                m = jnp.max(s2, axis=-1, keepdims=True)
                vcat, two = jnp.concatenate([vp, vc], axis=0), _stack_rows(first_head)
                vext = jnp.concatenate([jnp.where(two, vcat, 1), jnp.where(two, 1, vcat)], axis=1)
                r2 = _dot(jnp.exp(s2 - m).astype(BF16), vext, "nn")
                r0, r1 = r2[:ATTN_BLOCK, :PAIR_W], r2[ATTN_BLOCK:, PAIR_W:]
                num = jnp.where(first_head, r0, r1)
                den = pltpu.roll(jnp.where(first_head, r1, r0), HEAD_DIM, 1)
                o_slab[cur, :] = num / den
                l_slab[cur, :] = jnp.where(first_head, m[:ATTN_BLOCK], m[ATTN_BLOCK:]) + jnp.log(den)
                return carry

            lax.fori_loop(0, SEQ // ATTN_BLOCK, block, 0, unroll=4)

        for gi, (o_slab, l_slab) in enumerate(((o0, l0), (o1, l1), (o2, l2))):
            @pl.when(g == gi)
            def _(gi=gi, o_slab=o_slab, l_slab=l_slab):
                run(DILATIONS[gi], o_slab, l_slab)

        @pl.when(g == N_GROUPS - 1)
        def _():
            a, b, cc = l0[...], l1[...], l2[...]
            m = jnp.maximum(jnp.maximum(a, b), cc)
            e0, e1, e2 = jnp.exp(a - m), jnp.exp(b - m), jnp.exp(cc - m)
            tot = e0 + e1 + e2
            attn = (e0 * o0[...] + e1 * o1[...] + e2 * o2[...]) / tot
            attn_ref[...] = attn
            attn_b_ref[...] = attn.astype(BF16)
            lse_ref[...] = m + jnp.log(tot)

    shape = (LOCAL_BATCH, SEQ, GROUP_W)
    slab = pltpu.VMEM((SEQ, PAIR_W), F32)
    return hosted_call(
        body, comm, "attn_fwd", (LOCAL_BATCH, N_PAIRS, N_GROUPS),
        [_slab_spec(0), _slab_spec(1), _slab_spec(2), _TABLE_SPEC, _TABLE_SPEC, _TABLE_SPEC], [_PAIR_SPEC] * 3,
        [jax.ShapeDtypeStruct(shape, BF16), jax.ShapeDtypeStruct(shape, F32), jax.ShapeDtypeStruct(shape, F32)],
        [slab] * 8, (qkv, qkv, qkv, *tables), ("parallel", "parallel", "arbitrary"))


def attn_bwd(qkv, tables, dattn, attn, lse, comm=None):
    scale = HEAD_DIM ** -0.5

    def body(q_ref, k_ref, v_ref, c_ref, sa_ref, sb_ref, do_ref, out_ref, lse_ref, dqkv_ref, qs, ks, dl, dq_s, dk_s, dv_s):
        g = pl.program_id(2)
        c, sa, sb = c_ref[...], sa_ref[...], sb_ref[...]
        qs[...] = _rope_fwd(q_ref[...], c, sa, sb) * scale
        ks[...] = _rope_fwd(k_ref[...], c, sa, sb)
        @pl.when(g == 0)
        def _():
            prod = do_ref[...] * out_ref[...]
            dl[...] = _per_head(
                lambda sl: jnp.broadcast_to(jnp.sum(prod[:, sl], axis=-1, keepdims=True), (SEQ, HEAD_DIM)))

        cur_mask, prev_mask = _band_masks()
        first_head = lax.broadcasted_iota(jnp.int32, (ATTN_BLOCK, PAIR_W), 1) < HEAD_DIM

        def run(dil):
            nb = SEQ // dil // ATTN_BLOCK

            def block(idx, carry):
                r, n = lax.div(idx, nb), lax.rem(idx, nb)
                cur = _block_rows(dil, r, n)
                prev = _block_rows(dil, r, jnp.maximum(n - 1, 0))
                nxt = _block_rows(dil, r, jnp.minimum(n + 1, nb - 1))
                q0, q1 = qs[cur, :].astype(BF16), qs[nxt, :].astype(BF16)
                kp, kc = ks[prev, :].astype(BF16), ks[cur, :].astype(BF16)
                vp, vc = v_ref[prev, :].astype(BF16), v_ref[cur, :].astype(BF16)
                do0, do1 = do_ref[cur, :].astype(BF16), do_ref[nxt, :].astype(BF16)
                lse0, lse1, dl0, dl1 = lse_ref[cur, :], lse_ref[nxt, :], dl[cur, :], dl[nxt, :]
                has_prev = jnp.logical_and(prev_mask, n > 0)
                has_next = jnp.logical_and(prev_mask, n < nb - 1)

                def per_row(t):
                    return jnp.concatenate([t[:, 0:1], t[:, HEAD_DIM:HEAD_DIM + 1]], axis=0)

                q20, q21 = _stack_heads(q0, first_head), _stack_heads(q1, first_head)
                do20, do21 = _stack_heads(do0, first_head), _stack_heads(do1, first_head)
                kcat, vcat = jnp.concatenate([kp, kc], axis=0), jnp.concatenate([vp, vc], axis=0)
                mask0 = _stack_rows(jnp.concatenate([has_prev, cur_mask], axis=1))
                p0 = jnp.where(mask0, jnp.exp(_dot(q20, kcat, "nt") - per_row(lse0)), 0.0)
                ds0 = (p0 * (_dot(do20, vcat, "nt") - per_row(dl0))).astype(BF16)
                p1 = jnp.where(_stack_rows(has_next), jnp.exp(_dot(q21, kc, "nt") - per_row(lse1)), 0.0)
                ds1 = (p1 * (_dot(do21, vc, "nt") - per_row(dl1))).astype(BF16)
                dq2 = _dot(ds0, kcat, "nn")
                dq_s[cur, :] = jnp.where(first_head, dq2[:ATTN_BLOCK], dq2[ATTN_BLOCK:])
                ds_cur = jnp.concatenate([ds0[:, ATTN_BLOCK:], ds1], axis=0)
                p_cur = jnp.concatenate([p0[:, ATTN_BLOCK:], p1], axis=0).astype(BF16)
                dk_s[cur, :] = _dot(ds_cur, jnp.concatenate([q20, q21], axis=0), "tn")
                dv_s[cur, :] = _dot(p_cur, jnp.concatenate([do20, do21], axis=0), "tn")
                return carry

            lax.fori_loop(0, SEQ // ATTN_BLOCK, block, 0, unroll=2)

        for gi in range(N_GROUPS):
            @pl.when(g == gi)
            def _(gi=gi):
                run(DILATIONS[gi])

        dqkv_ref[:, 0:PAIR_W] = _rope_bwd(dq_s[...] * scale, c, sa, sb).astype(BF16)
        dqkv_ref[:, PAIR_W:2 * PAIR_W] = _rope_bwd(dk_s[...], c, sa, sb).astype(BF16)
        dqkv_ref[:, 2 * PAIR_W:] = dv_s[...].astype(BF16)

    slab = pltpu.VMEM((SEQ, PAIR_W), F32)
    return hosted_call(
        body, comm, "attn_bwd", (LOCAL_BATCH, N_PAIRS, N_GROUPS),
        [_slab_spec(0), _slab_spec(1), _slab_spec(2), _TABLE_SPEC, _TABLE_SPEC, _TABLE_SPEC,
         _PAIR_SPEC, _PAIR_SPEC, _PAIR_SPEC],
        [pl.BlockSpec((None, SEQ, 3 * PAIR_W), lambda b, p, g: (b, 0, p * N_GROUPS + g))],
        [jax.ShapeDtypeStruct((LOCAL_BATCH, SEQ, QKV_W), BF16)],
        [slab] * 6, (qkv, qkv, qkv, *tables, dattn, attn, lse), ("parallel", "parallel", "arbitrary"))


def _discretize(lr, li, log_dt, br, bi):
    dt = jnp.exp(log_dt)
    mag = jnp.exp(lr * dt)
    ab_re, ab_im = mag * jnp.cos(li * dt), mag * jnp.sin(li * dt)
    den = lr * lr + li * li
    nr, ni = ab_re - 1.0, ab_im
    f_re = (nr * lr + ni * li) / den
    f_im = (ni * lr - nr * li) / den
    return ab_re, ab_im, f_re[None] * br - f_im[None] * bi, f_re[None] * bi + f_im[None] * br


def ssm_prep(lr, li, log_dt, br, bi):
    def body(lr_ref, li_ref, dt_ref, br_ref, bi_ref, *outs):
        for o, v in zip(outs, _discretize(lr_ref[...], li_ref[...], dt_ref[...], br_ref[...], bi_ref[...])):
            o[...] = v
    shapes = [lr, li, br, bi]
    return pl.pallas_call(body, name="ssm_prep",
                          out_shape=[jax.ShapeDtypeStruct(s.shape, F32) for s in shapes])(lr, li, log_dt, br, bi)


def ssm_prep_bwd(lr, li, log_dt, br, bi, g_ab_re, g_ab_im, g_bb_re, g_bb_im):
    def body(lr_ref, li_ref, dt_ref, br_ref, bi_ref, g0, g1, g2, g3, *outs):
        _, vjp = jax.vjp(_discretize, lr_ref[...], li_ref[...], dt_ref[...], br_ref[...], bi_ref[...])
        for o, v in zip(outs, vjp((g0[...], g1[...], g2[...], g3[...]))):
            o[...] = v
    shapes = [lr, li, log_dt, br, bi]
    return pl.pallas_call(body, name="ssm_prep_bwd",
                          out_shape=[jax.ShapeDtypeStruct(s.shape, F32) for s in shapes])(
        lr, li, log_dt, br, bi, g_ab_re, g_ab_im, g_bb_re, g_bb_im)


def _block_diag(t):
    per = SSM_STATE_W // SSM_LANE_BLOCKS // 64
    g = t.transpose(1, 0, 2).reshape(SSM_LANE_BLOCKS, per, 16, 64)
    eye = jnp.eye(per, dtype=t.dtype)
    return jnp.einsum("jgcn,gh->jgchn", g, eye).reshape(SSM_LANE_BLOCKS, per * 16, per * 64)


def _block_diag_t(m):
    per = SSM_STATE_W // SSM_LANE_BLOCKS // 64
    m5 = m.reshape(SSM_LANE_BLOCKS, per, 16, per, 64)
    d = jnp.einsum("jgchn,gh->jgcn", m5, jnp.eye(per, dtype=m.dtype))
    return d.reshape(SSM_LANE_BLOCKS * per, 16, 64).transpose(1, 0, 2)


def _cmul(ar, ai, br, bi):
    return ar * br - ai * bi, ar * bi + ai * br


def _power_tables(ar, ai, reverse):
    width = ar.shape[1]
    row = lax.broadcasted_iota(jnp.int32, (8, width), 0)
    pows = [(ar, ai)]
    for _ in range(7):
        pows.append(_cmul(pows[-1][0], pows[-1][1], ar, ai))
    steps = []
    for k in (1, 2, 4):
        keep = (row >= k) if not reverse else (row < 8 - k)
        steps.append((jnp.where(keep, pows[k - 1][0], 0.0), jnp.where(keep, pows[k - 1][1], 0.0)))
    cr = jnp.zeros((8, width), F32)
    ci = jnp.zeros((8, width), F32)
    for i in range(8):
        pr, pi = pows[i] if not reverse else pows[7 - i]
        cr = jnp.where(row == i, pr, cr)
        ci = jnp.where(row == i, pi, ci)
    return steps, (cr, ci)


SCAN_CHUNK = 512
STATE_BLOCK = SSM_STATE_W // SSM_LANE_BLOCKS
CHAN_BLOCK = SSM_W // SSM_LANE_BLOCKS


def ssm_fwd(u, ab_re, ab_im, bb_re, bb_im, cb_re, cb_im, d_skip, comm=None):
    nt = SEQ // SCAN_CHUNK
    chan = pl.BlockSpec((None, SCAN_CHUNK, CHAN_BLOCK), lambda b, j, t: (b, t, j))
    state = pl.BlockSpec((None, SCAN_CHUNK, STATE_BLOCK), lambda b, j, t: (b, t, j))
    mat = pl.BlockSpec((None, CHAN_BLOCK, STATE_BLOCK), lambda b, j, t: (j, 0, 0))
    lane = pl.BlockSpec((1, STATE_BLOCK), lambda b, j, t: (0, j))
    dsp = pl.BlockSpec((1, CHAN_BLOCK), lambda b, j, t: (0, j))

    def body(u_ref, ar_ref, ai_ref, bbr_ref, bbi_ref, cbr_ref, cbi_ref, d_ref, y_ref, yg_ref, xr_ref, xi_ref,
             car_r, car_i):
        @pl.when(pl.program_id(2) == 0)
        def _():
            car_r[...] = jnp.zeros_like(car_r)
            car_i[...] = jnp.zeros_like(car_i)

        steps, (pr, pi) = _power_tables(ar_ref[...], ai_ref[...], reverse=False)
        uf = u_ref[...]
        ub = uf.astype(BF16)
        xr_ref[...] = _dot(ub, bbr_ref[...], "nn")
        xi_ref[...] = _dot(ub, bbi_ref[...], "nn")

        def tile(i, carry):
            cr, ci = carry
            sl = pl.ds(pl.multiple_of(i * 8, 8), 8)
            br, bi = xr_ref[sl, :], xi_ref[sl, :]
            for k, (sr, si) in zip((1, 2, 4), steps):
                tr, ti = _cmul(sr, si, pltpu.roll(br, k, 0), pltpu.roll(bi, k, 0))
                br, bi = br + tr, bi + ti
            tr, ti = _cmul(pr, pi, cr, ci)
            br, bi = br + tr, bi + ti
            xr_ref[sl, :] = br
            xi_ref[sl, :] = bi
            return br[7:8, :], bi[7:8, :]

        cr, ci = lax.fori_loop(0, SCAN_CHUNK // 8, tile, (car_r[0:1, :], car_i[0:1, :]), unroll=4)
        car_r[0:1, :] = cr
        car_i[0:1, :] = ci
        y = (_dot(xr_ref[...].astype(BF16), cbr_ref[...], "nt") - _dot(xi_ref[...].astype(BF16), cbi_ref[...], "nt")
             + d_ref[...] * uf)
        y_ref[...] = y
        yg_ref[...] = jax.nn.gelu(y).astype(BF16)

    return hosted_call(
        body, comm, "ssm_fwd", (LOCAL_BATCH, SSM_LANE_BLOCKS, nt),
        [chan, lane, lane, mat, mat, mat, mat, dsp], [chan, chan, state, state],
        [jax.ShapeDtypeStruct((LOCAL_BATCH, SEQ, SSM_W), F32), jax.ShapeDtypeStruct((LOCAL_BATCH, SEQ, SSM_W), BF16),
         jax.ShapeDtypeStruct((LOCAL_BATCH, SEQ, SSM_STATE_W), F32),
         jax.ShapeDtypeStruct((LOCAL_BATCH, SEQ, SSM_STATE_W), F32)],
        [pltpu.VMEM((8, STATE_BLOCK), F32), pltpu.VMEM((8, STATE_BLOCK), F32)],
        (u, ab_re, ab_im, bb_re, bb_im, cb_re, cb_im, d_skip), ("parallel", "parallel", "arbitrary"))


def ssm_bwd(dyg, y, u, xr, xi, ab_re, ab_im, bb_re, bb_im, cb_re, cb_im, d_skip, comm=None):
    nt = SEQ // SCAN_CHUNK
    ntile = SCAN_CHUNK // 8

    def rev(t):
        return nt - 1 - t

    chan = pl.BlockSpec((None, SCAN_CHUNK, CHAN_BLOCK), lambda j, b, t: (b, rev(t), j))
    state = pl.BlockSpec((None, SCAN_CHUNK, STATE_BLOCK), lambda j, b, t: (b, rev(t), j))
    before = pl.BlockSpec((None, 8, STATE_BLOCK), lambda j, b, t: (b, jnp.maximum(rev(t) * ntile - 1, 0), j))
    mat = pl.BlockSpec((None, CHAN_BLOCK, STATE_BLOCK), lambda j, b, t: (j, 0, 0))
    lane = pl.BlockSpec((1, STATE_BLOCK), lambda j, b, t: (0, j))
    lane8 = pl.BlockSpec((8, STATE_BLOCK), lambda j, b, t: (0, j))
    dsp = pl.BlockSpec((1, CHAN_BLOCK), lambda j, b, t: (0, j))

    def body(dyg_ref, y_ref, u_ref, xr_ref, xi_ref, xrb_ref, xib_ref, ar_ref, ai_ref, bbr_ref, bbi_ref, cbr_ref,
             cbi_ref, d_ref, du_ref, dcbr_ref, dcbi_ref, dbbr_ref, dbbi_ref, dd_ref, dar_ref, dai_ref,
             lam_r, lam_i, car_r, car_i):
        b, t = pl.program_id(1), pl.program_id(2)
        first = jnp.logical_and(b == 0, t == 0)

        @pl.when(t == 0)
        def _():
            car_r[...] = jnp.zeros_like(car_r)
            car_i[...] = jnp.zeros_like(car_i)

        @pl.when(first)
        def _():
            for r in (dcbr_ref, dcbi_ref, dbbr_ref, dbbi_ref, dd_ref, dar_ref, dai_ref):
                r[...] = jnp.zeros_like(r)

        steps, (pr, pi) = _power_tables(ar_ref[...], -ai_ref[...], reverse=True)
        uf = u_ref[...]
        _, gelu_vjp = jax.vjp(jax.nn.gelu, y_ref[...])
        dy = gelu_vjp(dyg_ref[...])[0]
        dyb = dy.astype(BF16)
        dd_ref[...] += _colsum(dy * uf)
        lam_r[...] = _dot(dyb, cbr_ref[...], "nn")
        lam_i[...] = -_dot(dyb, cbi_ref[...], "nn")
        dcbr_ref[...] += _dot(dyb, xr_ref[...].astype(BF16), "tn")
        dcbi_ref[...] -= _dot(dyb, xi_ref[...].astype(BF16), "tn")
        row0 = lax.broadcasted_iota(jnp.int32, (8, STATE_BLOCK), 0) == 0
        has_before = rev(t) > 0
        xrb = jnp.where(has_before, xrb_ref[...], 0.0)
        xib = jnp.where(has_before, xib_ref[...], 0.0)

        def tile(s, carry):
            cr, ci, acc_r, acc_i = carry
            i = ntile - 1 - s
            sl = pl.ds(pl.multiple_of(i * 8, 8), 8)
            gr, gi = lam_r[sl, :], lam_i[sl, :]
            for k, (sr, si) in zip((1, 2, 4), steps):
                tr, ti = _cmul(sr, si, pltpu.roll(gr, 8 - k, 0), pltpu.roll(gi, 8 - k, 0))
                gr, gi = gr + tr, gi + ti
            tr, ti = _cmul(pr, pi, cr, ci)
            gr, gi = gr + tr, gi + ti
            lam_r[sl, :] = gr
            lam_i[sl, :] = gi
            sp = pl.ds(pl.multiple_of(jnp.maximum(i - 1, 0) * 8, 8), 8)
            pvr = jnp.where(i > 0, xr_ref[sp, :], xrb)
            pvi = jnp.where(i > 0, xi_ref[sp, :], xib)
            xsr = jnp.where(row0, pltpu.roll(pvr, 1, 0), pltpu.roll(xr_ref[sl, :], 1, 0))
            xsi = jnp.where(row0, pltpu.roll(pvi, 1, 0), pltpu.roll(xi_ref[sl, :], 1, 0))
            acc_r = acc_r + xsr * gr + xsi * gi
            acc_i = acc_i + xsr * gi - xsi * gr
            return gr[0:1, :], gi[0:1, :], acc_r, acc_i

        zero = jnp.zeros((8, STATE_BLOCK), F32)
        cr, ci, acc_r, acc_i = lax.fori_loop(0, ntile, tile, (car_r[0:1, :], car_i[0:1, :], zero, zero), unroll=2)
        car_r[0:1, :] = cr
        car_i[0:1, :] = ci
        dar_ref[...] += acc_r
        dai_ref[...] += acc_i
        lrb, lib = lam_r[...].astype(BF16), lam_i[...].astype(BF16)
        du = _dot(lrb, bbr_ref[...], "nt") + _dot(lib, bbi_ref[...], "nt") + d_ref[...] * dy
        du_ref[...] = du.astype(BF16)
        ub = uf.astype(BF16)
        dbbr_ref[...] += _dot(ub, lrb, "tn")
        dbbi_ref[...] += _dot(ub, lib, "tn")

    mat_shape = jax.ShapeDtypeStruct((SSM_LANE_BLOCKS, CHAN_BLOCK, STATE_BLOCK), F32)
    return hosted_call(
        body, comm, "ssm_bwd", (SSM_LANE_BLOCKS, LOCAL_BATCH, nt),
        [chan, chan, chan, state, state, before, before, lane, lane, mat, mat, mat, mat, dsp],
        [chan, mat, mat, mat, mat, dsp, lane8, lane8],
        [jax.ShapeDtypeStruct((LOCAL_BATCH, SEQ, SSM_W), BF16), mat_shape, mat_shape, mat_shape, mat_shape,
         jax.ShapeDtypeStruct((1, SSM_W), F32), jax.ShapeDtypeStruct((8, SSM_STATE_W), F32),
         jax.ShapeDtypeStruct((8, SSM_STATE_W), F32)],
        [pltpu.VMEM((SCAN_CHUNK, STATE_BLOCK), F32), pltpu.VMEM((SCAN_CHUNK, STATE_BLOCK), F32),
         pltpu.VMEM((8, STATE_BLOCK), F32), pltpu.VMEM((8, STATE_BLOCK), F32)],
        (dyg, y, u, xr, xi, xr, xi, ab_re, ab_im, bb_re, bb_im, cb_re, cb_im, d_skip),
        ("parallel", "arbitrary", "arbitrary"))


def _merge_fn(g0, g1, attn_d, za, zb):
    return jax.nn.sigmoid(g0) * attn_d + jax.nn.sigmoid(g1) * (za * jax.nn.sigmoid(zb))


def _swiglu_fn(a, b):
    return jax.nn.silu(a) * b


def _reduce_start(names, gw, shard_shapes):
    return swap_comm([_to_slots(n, gw[n], shard_shapes[n]) for n in names])


def _reduce_chip(names, swap, got, core):
    return exchange_comm([add_halves(n, g, r, core) for n, g, r in zip(names, swap.ins, got)])


def local_step(x, target, shards, small, core):
    g_mix, g_ffn, g_final = small["norm_mix_g"], small["norm_ffn_g"], small["norm_final_g"]
    tables = _rope_tables()
    seqs = lambda t: t.reshape(LOCAL_BATCH, SEQ, t.shape[-1])
    toks = lambda t: t.reshape(TOKENS, t.shape[-1])
    shard_shapes = {n: s.shape for n, s in shards.items()}
    w = {}

    def gather(names):
        return gather_comm([shards[n] for n in names])

    def arrived(names, slots):
        for n, s in zip(names, slots):
            w[n] = _from_slots(n, s)

    h, *slots = rowwise(lambda xv, g: (_rms(xv, g),), [x, g_mix], [(D_MODEL, BF16)], "norm_mix", comm=gather(["w_in"]))
    arrived(["w_in"], slots)
    w_qkv, w_u, w_gate = _qkv_order(w["w_in"][:QKV_W]), w["w_in"][QKV_W:QKV_W + SSM_W], w["w_in"][QKV_W + SSM_W:]
    qkv = seqs(matmul(h, w_qkv, "nt", F32, "proj_qkv"))
    u = seqs(matmul(h, w_u, "nt", F32, "proj_u"))
    gl = matmul(h, w_gate, "nt", F32, "proj_gate")
    first = ["w_attn_out", "w_ffn_gate", "w_ffn_up"]
    attn_b, attn, lse, *slots = attn_fwd(qkv, tables, comm=gather(first))
    arrived(first, slots)
    attn_b = toks(attn_b)
    attn_d = matmul(attn_b, w["w_attn_out"], "nn", F32, "attn_out")

    br_t = small["ssm_b_re"].transpose(2, 0, 1)
    bi_t = small["ssm_b_im"].transpose(2, 0, 1)
    log_dt = small["ssm_log_dt"].reshape(32, 1)
    ab_re, ab_im, bb_re_t, bb_im_t = ssm_prep(small["ssm_a_re"], small["ssm_a_im"], log_dt, br_t, bi_t)
    ab = [ab_re.reshape(1, SSM_STATE_W), ab_im.reshape(1, SSM_STATE_W)]
    bb = [_block_diag(bb_re_t).astype(BF16), _block_diag(bb_im_t).astype(BF16)]
    cb = [_block_diag(small["ssm_c_re"].transpose(1, 0, 2)).astype(BF16),
          _block_diag(small["ssm_c_im"].transpose(1, 0, 2)).astype(BF16)]
    d_skip = small["ssm_d"].reshape(1, SSM_W)
    second = ["w_glu", "w_out", "w_ffn_down"]
    y, yg, xr, xi, *slots = ssm_fwd(u, *ab, *bb, *cb, d_skip, comm=gather(second))
    arrived(second, slots)
    yg2 = toks(yg)
    z = matmul(yg2, w["w_glu"], "nn", F32, "glu")
    gate_ins = [(gl, D_MODEL, 0), (gl, D_MODEL, 1), attn_d, (z, D_MODEL, 0), (z, D_MODEL, 1)]
    (merged,) = rowwise(lambda *v: (_merge_fn(*v),), gate_ins, [(D_MODEL, BF16)], "merge")
    x1 = matmul(merged, w["w_out"], "nn", F32, "out_proj", add=x)
    (h2,) = rowwise(lambda xv, g: (_rms(xv, g),), [x1, g_ffn], [(D_MODEL, BF16)], "norm_ffn")
    a, b, act = ffn_in(h2, w["w_ffn_gate"], w["w_ffn_up"])
    x2 = matmul(act, w["w_ffn_down"], "nn", F32, "ffn_down", add=x1)

    def final_fn(xv, g, tgt):
        yv, vjp = jax.vjp(_rms, xv, g)
        err = yv - tgt
        dx, dg = vjp(err * (1.0 / D_MODEL))
        loss = 0.5 * jnp.sum(jnp.mean(err * err, axis=-1, keepdims=True), axis=0, keepdims=True)
        return dx, dx, dg, jnp.broadcast_to(loss, (1, LANES))

    dx2, dx2_b, dg_final, loss = rowwise(final_fn, [x2, g_final, target], [(D_MODEL, F32), (D_MODEL, BF16)],
                                         "final_norm_loss", accs=(D_MODEL, LANES))
    gw, parts = {}, {}
    gw["w_ffn_down"] = matmul(act, dx2_b, "tn", F32, "d_ffn_down")
    da_b, db_b = ffn_in_bwd(dx2_b, w["w_ffn_down"], a, b)
    gw["w_ffn_gate"] = matmul(da_b, h2, "tn", F32, "d_ffn_gate")
    gw["w_ffn_up"] = matmul(db_b, h2, "tn", F32, "d_ffn_up")
    dh2 = matmul(da_b, w["w_ffn_gate"], "nn", F32, "d_h2_gate")
    dh2 = matmul(db_b, w["w_ffn_up"], "nn", F32, "d_h2_up", add=dh2)

    def norm_bwd(xv, g, dh, skip):
        _, vjp = jax.vjp(_rms, xv, g)
        dx, dg = vjp(dh)
        dx = dx + skip
        return dx, dx, dg

    dx1, dx1_b, dg_ffn = rowwise(norm_bwd, [x1, g_ffn, dh2, dx2], [(D_MODEL, F32), (D_MODEL, BF16)],
                                 "norm_ffn_bwd", accs=(D_MODEL,))
    gw["w_out"] = matmul(merged, dx1_b, "tn", F32, "d_out")
    dmerged = matmul(dx1_b, w["w_out"], "nt", F32, "d_merged")

    def merge_bwd(g0, g1, ad, za, zb, dm):
        _, vjp = jax.vjp(_merge_fn, g0, g1, ad, za, zb)
        dg0, dg1, dad, dza, dzb = vjp(dm)
        return jnp.concatenate([dg0, dg1], axis=1), dad, jnp.concatenate([dza, dzb], axis=1)

    ffn = ["w_ffn_down", "w_ffn_gate", "w_ffn_up"]
    swap = _reduce_start(ffn, gw, shard_shapes)
    dgl_b, dattn_d_b, dz_b, *got = rowwise(merge_bwd, gate_ins + [dmerged],
                                           [(GATE_W, BF16), (D_MODEL, BF16), (GATE_W, BF16)], "merge_bwd", comm=swap)
    ffn_exchange = _reduce_chip(ffn, swap, got, core)
    gw["w_attn_out"] = matmul(attn_b, dattn_d_b, "tn", F32, "d_attn_out")
    dattn = seqs(matmul(dattn_d_b, w["w_attn_out"], "nt", F32, "d_attn"))
    gw["w_glu"] = matmul(yg2, dz_b, "tn", F32, "d_glu")
    dyg = seqs(matmul(dz_b, w["w_glu"], "nt", F32, "d_yg"))
    mixer = ["w_out", "w_attn_out", "w_glu"]
    swap = _reduce_start(mixer, gw, shard_shapes)
    du_b, dcb_re, dcb_im, dbb_re, dbb_im, dd, da_re8, da_im8, *rest = ssm_bwd(
        dyg, y, u, xr, xi, *ab, *bb, *cb, d_skip, comm=join_comms([ffn_exchange, swap]))
    for n, p in zip(ffn, rest[:len(ffn)]):
        parts[n] = p
    mixer_exchange = _reduce_chip(mixer, swap, rest[len(ffn):], core)
    du_b = toks(du_b)
    g_ab_re = jnp.sum(da_re8, axis=0).reshape(32, 64)
    g_ab_im = jnp.sum(da_im8, axis=0).reshape(32, 64)
    d_lr, d_li, d_ldt, d_br_t, d_bi_t = ssm_prep_bwd(
        small["ssm_a_re"], small["ssm_a_im"], log_dt, br_t, bi_t,
        g_ab_re, g_ab_im, _block_diag_t(dbb_re), _block_diag_t(dbb_im))
    gs = {
        "ssm_a_re": d_lr, "ssm_a_im": d_li, "ssm_log_dt": d_ldt.reshape(1, 32),
        "ssm_b_re": d_br_t.transpose(1, 2, 0), "ssm_b_im": d_bi_t.transpose(1, 2, 0),
        "ssm_c_re": _block_diag_t(dcb_re).transpose(1, 0, 2), "ssm_c_im": _block_diag_t(dcb_im).transpose(1, 0, 2),
        "ssm_d": dd.reshape(32, 16),
    }
    ssm_gather = small_comm(_pack_small(SSM_SMALL, gs, SSM_ROWS))
    dqkv_b, *rest = attn_bwd(qkv, tables, dattn, attn, lse, comm=join_comms([mixer_exchange, ssm_gather]))
    for n, p in zip(mixer, rest):
        parts[n] = p
    ssm_shares = rest[len(mixer)]
    dqkv_b = toks(dqkv_b)
    d_qkv = matmul(dqkv_b, h, "tn", F32, "d_w_qkv")
    d_u = matmul(du_b, h, "tn", F32, "d_w_u")
    d_gate = matmul(dgl_b, h, "tn", F32, "d_w_gate")
    gw["w_in"] = jnp.concatenate([_qkv_order(d_qkv, back=True), d_u, d_gate], axis=0)
    swap = _reduce_start(["w_in"], gw, shard_shapes)
    dh, *got = matmul(dqkv_b, w_qkv, "nn", F32, "d_h_qkv", comm=swap)
    w_in_exchange = _reduce_chip(["w_in"], swap, got, core)
    grad_x, dg_mix, parts["w_in"] = mix_in_bwd([du_b, dgl_b], [w_u, w_gate], dh, x, g_mix, dx1, comm=w_in_exchange)
    gs_norm = {"norm_mix_g": dg_mix, "norm_ffn_g": dg_ffn, "norm_final_g": dg_final}
    return loss, grad_x, parts, ssm_shares, gs_norm


ANY = pl.BlockSpec(memory_space=pl.ANY)
BIG = ("w_in", "w_glu", "w_attn_out", "w_out", "w_ffn_gate", "w_ffn_up", "w_ffn_down")
TRANSPOSED = ("w_in", "w_ffn_gate", "w_ffn_up")
ROW_SHARDED = TRANSPOSED + ("w_out", "w_ffn_down")
SMALL = ("norm_mix_g", "ssm_a_re", "ssm_a_im", "ssm_log_dt", "ssm_b_re", "ssm_b_im", "ssm_c_re", "ssm_c_im",
         "ssm_d", "norm_ffn_g", "norm_final_g")
WEIGHTS = ("norm_mix_g", "w_in", "ssm_a_re", "ssm_a_im", "ssm_log_dt", "ssm_b_re", "ssm_b_im", "ssm_c_re",
           "ssm_c_im", "ssm_d", "w_glu", "w_attn_out", "w_out", "norm_ffn_g", "w_ffn_gate", "w_ffn_up",
           "w_ffn_down", "norm_final_g")
SSM_SMALL = SMALL[1:9]
NORM_SMALL = (SMALL[0],) + SMALL[9:]
SSM_ROWS, NORM_ROWS = 1064, 32
N_BIG = len(BIG)


def _position():
    return lax.axis_index("x"), lax.axis_index("y"), lax.axis_index("c")


def _other_chips(x, y):
    return [(1 - x, y), (x, 1 - y), (1 - x, 1 - y)]


def _remote(src, dst, send_sem, recv_sem, device):
    return pltpu.make_async_remote_copy(src_ref=src, dst_ref=dst, send_sem=send_sem, recv_sem=recv_sem,
                                        device_id=device, device_id_type=MESH)


_later = functools.partial


def _two_level_phases(copies):
    def first(*refs):
        locals_, sends, _, _, _ = copies(*refs)
        for cp in locals_ + sends:
            cp().start()

    def mid(*refs):
        _, _, arrived, passed, _ = copies(*refs)
        for got, cp in zip(arrived, passed):
            got().wait_recv()
            cp().start()

    def last(*refs):
        locals_, sends, _, passed, from_sibling = copies(*refs)
        for cp in from_sibling:
            cp().wait_recv()
        for cp in sends + passed:
            cp().wait_send()
        for cp in locals_:
            cp().wait()

    return first, mid, last


def _half(ref, chip, which):
    rows = ref.shape[1] // 2
    return ref.at[chip, pl.ds(which * rows, rows), :]


class Comm:
    def __init__(self, ins, out_shapes, sems, first, mid, last):
        self.ins, self.out_shapes, self.sems = list(ins), list(out_shapes), list(sems)
        self.first, self.mid, self.last = first, mid, last


def join_comms(comms):
    def cut(refs_by_kind):
        offs, parts = [0, 0, 0], []
        for cm in comms:
            sizes = (len(cm.ins), len(cm.out_shapes), len(cm.sems))
            parts.append(tuple(refs_by_kind[k][offs[k]:offs[k] + sizes[k]] for k in range(3)))
            offs = [o + s for o, s in zip(offs, sizes)]
        return parts

    def phase(which):
        def run(ins, outs, sems):
            for cm, part in zip(comms, cut((ins, outs, sems))):
                fn = getattr(cm, which)
                if fn is not None:
                    fn(*part)
        return run

    return Comm(sum((cm.ins for cm in comms), []), sum((cm.out_shapes for cm in comms), []),
                sum((cm.sems for cm in comms), []), phase("first"), phase("mid"), phase("last"))


def _comm_operands(comm):
    if comm is None:
        return [], [], []
    return comm.ins, comm.out_shapes, comm.sems


def _comm_begin(comm, refs, step, n_steps):
    if comm is None:
        return
    pl.when(step == 0)(lambda: comm.first(*refs))
    if comm.mid is not None:
        pl.when(step == (n_steps * 3) // 4)(lambda: comm.mid(*refs))


def _comm_end(comm, refs, step, n_steps):
    if comm is not None:
        pl.when(step == n_steps - 1)(lambda: comm.last(*refs))


def _comm_refs(comm, refs, n_in, n_out):
    if comm is None:
        return list(refs), None
    ci, co, cs = len(comm.ins), len(comm.out_shapes), len(comm.sems)
    o0 = n_in + ci
    s0 = o0 + n_out + co
    host = list(refs[:n_in]) + list(refs[o0:o0 + n_out]) + list(refs[s0:len(refs) - cs])
    return host, (list(refs[n_in:o0]), list(refs[o0 + n_out:s0]), list(refs[len(refs) - cs:]))


def run_comm(comm, name):
    n_in, n_out = len(comm.ins), len(comm.out_shapes)

    def body(*refs):
        parts = (list(refs[:n_in]), list(refs[n_in:n_in + n_out]), list(refs[n_in + n_out:]))
        comm.first(*parts)
        if comm.mid is not None:
            comm.mid(*parts)
        comm.last(*parts)

    return pl.pallas_call(body, name=name, in_specs=[ANY] * n_in, out_specs=[ANY] * n_out,
                          out_shape=comm.out_shapes, scratch_shapes=comm.sems)(*comm.ins)


def hosted_call(work, comm, name, grid, in_specs, out_specs, out_shape, scratch_shapes, args, semantics):
    c_ins, c_outs, c_sems = _comm_operands(comm)
    n_steps = math.prod(grid)

    def body(*refs):
        host, c_refs = _comm_refs(comm, refs, len(in_specs), len(out_specs))
        step = 0
        for axis, size in enumerate(grid):
            step = step * size + pl.program_id(axis)
        _comm_begin(comm, c_refs, step, n_steps)
        work(*host)
        _comm_end(comm, c_refs, step, n_steps)

    return pl.pallas_call(
        body, name=name, grid=grid, in_specs=list(in_specs) + [ANY] * len(c_ins),
        out_specs=list(out_specs) + [ANY] * len(c_outs), out_shape=list(out_shape) + c_outs,
        scratch_shapes=list(scratch_shapes) + c_sems,
        compiler_params=_params(semantics if comm is None else ("arbitrary",) * len(grid)),
    )(*args, *c_ins)


def gather_comm(shards):
    n = len(shards)

    def copies(srcs, outs, sems):
        send_sems, recv_sems, local_sems = sems
        x, y, c = _position()
        me = 2 * x + y
        sibling = (x, y, 1 - c)
        chips = _other_chips(x, y)
        locals_ = [_later(pltpu.make_async_copy, s, o.at[me], local_sems.at[i])
                   for i, (s, o) in enumerate(zip(srcs, outs))]
        sends, arrived, passed, from_sibling = [], [], [], []
        for j, (px, py) in enumerate(chips):
            for i, (s, o) in enumerate(zip(srcs, outs)):
                rows = s.shape[0] // 2
                sends.append(_later(_remote, s.at[pl.ds(c * rows, rows), :], _half(o, me, c), send_sems.at[i, j],
                                    recv_sems.at[i, j], (px, py, c)))
                got = _half(o, 2 * px + py, c)
                arrived.append(_later(_remote, got, got, send_sems.at[i, j], recv_sems.at[i, j], (px, py, c)))
                passed.append(_later(_remote, got, got, send_sems.at[i, 3 + j], recv_sems.at[i, 3 + j], sibling))
                other = _half(o, 2 * px + py, 1 - c)
                from_sibling.append(_later(_remote, other, other, send_sems.at[i, 3 + j], recv_sems.at[i, 3 + j],
                                           sibling))
        return locals_, sends, arrived, passed, from_sibling

    return Comm(shards, [jax.ShapeDtypeStruct((N_CHIPS,) + s.shape, s.dtype) for s in shards],
                [pltpu.SemaphoreType.DMA((n, 6)), pltpu.SemaphoreType.DMA((n, 6)), pltpu.SemaphoreType.DMA((n,))],
                *_two_level_phases(copies))


def swap_comm(grads):
    n = len(grads)

    def copies(srcs, gots, sems):
        send_sems, recv_sems = sems
        x, y, c = _position()
        out = []
        for i, (s, o) in enumerate(zip(srcs, gots)):
            rows = s.shape[1] // 2
            out.append(_remote(s.at[:, pl.ds((1 - c) * rows, rows), :], o, send_sems.at[i], recv_sems.at[i],
                               (x, y, 1 - c)))
        return out

    def first(srcs, gots, sems):
        for cp in copies(srcs, gots, sems):
            cp.start()

    def last(srcs, gots, sems):
        for cp in copies(srcs, gots, sems):
            cp.wait()

    return Comm(grads, [jax.ShapeDtypeStruct((N_CHIPS, g.shape[1] // 2, g.shape[2]), g.dtype) for g in grads],
                [pltpu.SemaphoreType.DMA((n,)), pltpu.SemaphoreType.DMA((n,))], first, None, last)


def add_halves(name, g, got, core):
    _, half, cols = got.shape
    mine = pl.BlockSpec((None, half, cols), lambda k, c_ref: (k, c_ref[0], 0))
    other = pl.BlockSpec((None, half, cols), lambda k, c_ref: (k, 0, 0))

    def body(c_ref, g_ref, got_ref, o_ref):
        o_ref[...] = (g_ref[...] + got_ref[...]).astype(BF16)

    return pl.pallas_call(
        body, name="add_halves_" + name,
        grid_spec=pltpu.PrefetchScalarGridSpec(num_scalar_prefetch=1, grid=(N_CHIPS,), in_specs=[mine, other],
                                               out_specs=other),
        out_shape=jax.ShapeDtypeStruct(got.shape, BF16),
        compiler_params=_params(("parallel",)),
    )(core, g, got)


def exchange_comm(parts):
    n = len(parts)

    def copies(srcs, outs, sems):
        send_sems, recv_sems, local_sems = sems
        x, y, c = _position()
        me = 2 * x + y
        sibling = (x, y, 1 - c)
        chips = _other_chips(x, y)
        locals_, sends, arrived, passed, from_sibling = [], [], [], [], []
        for i, (s, o) in enumerate(zip(srcs, outs)):
            locals_.append(_later(pltpu.make_async_copy, s.at[me], _half(o, me, c), local_sems.at[i]))
            sends.append(_later(_remote, s.at[me], _half(o, me, c), send_sems.at[i, 3], recv_sems.at[i, 3], sibling))
            other = _half(o, me, 1 - c)
            from_sibling.append(_later(_remote, other, other, send_sems.at[i, 3], recv_sems.at[i, 3], sibling))
        for j, (px, py) in enumerate(chips):
            for i, (s, o) in enumerate(zip(srcs, outs)):
                sends.append(_later(_remote, s.at[2 * px + py], _half(o, me, c), send_sems.at[i, j],
                                    recv_sems.at[i, j], (px, py, c)))
                got = _half(o, 2 * px + py, c)
                arrived.append(_later(_remote, got, got, send_sems.at[i, j], recv_sems.at[i, j], (px, py, c)))
                passed.append(_later(_remote, got, got, send_sems.at[i, 4 + j], recv_sems.at[i, 4 + j], sibling))
                other = _half(o, 2 * px + py, 1 - c)
                from_sibling.append(_later(_remote, other, other, send_sems.at[i, 4 + j], recv_sems.at[i, 4 + j],
                                           sibling))
        return locals_, sends, arrived, passed, from_sibling

    return Comm(parts, [jax.ShapeDtypeStruct((N_CHIPS, 2 * p.shape[1], p.shape[2]), p.dtype) for p in parts],
                [pltpu.SemaphoreType.DMA((n, 7)), pltpu.SemaphoreType.DMA((n, 7)), pltpu.SemaphoreType.DMA((n,))],
                *_two_level_phases(copies))


def small_comm(pack):
    def copies(srcs, outs, sems):
        (src_ref,), (out_ref,), (send_sems, recv_sems, local_sem) = srcs, outs, sems
        x, y, c = _position()
        me = 4 * x + 2 * y + c
        flips = [(fx, fy, fc) for fx in (0, 1) for fy in (0, 1) for fc in (0, 1)][1:]
        peers = [(1 - x if fx else x, 1 - y if fy else y, 1 - c if fc else c) for fx, fy, fc in flips]
        local = _later(pltpu.make_async_copy, src_ref, out_ref.at[me], local_sem)
        sends = [_later(_remote, src_ref, out_ref.at[me], send_sems.at[j], recv_sems.at[j], peer)
                 for j, peer in enumerate(peers)]
        arrived = []
        for j, (px, py, pc) in enumerate(peers):
            got = out_ref.at[4 * px + 2 * py + pc]
            arrived.append(_later(_remote, got, got, send_sems.at[j], recv_sems.at[j], (px, py, pc)))
        return local, sends, arrived

    def first(*refs):
        local, sends, _ = copies(*refs)
        for cp in [local] + sends:
            cp().start()

    def last(*refs):
        local, sends, arrived = copies(*refs)
        for cp in arrived:
            cp().wait_recv()
        for cp in sends:
            cp().wait_send()
        local().wait()

    return Comm([pack], [jax.ShapeDtypeStruct((N_DEV,) + pack.shape, pack.dtype)],
                [pltpu.SemaphoreType.DMA((7,)), pltpu.SemaphoreType.DMA((7,)), pltpu.SemaphoreType.DMA],
                first, None, last)


def _adam_fn(w, g, m, v):
    m = ADAM_B1 * m + (1.0 - ADAM_B1) * g
    v = ADAM_B2 * v + (1.0 - ADAM_B2) * jnp.square(g)
    m_hat = m / (1.0 - ADAM_B1 ** ADAM_STEP)
    v_hat = v / (1.0 - ADAM_B2 ** ADAM_STEP)
    return -ADAM_LR * (m_hat / (jnp.sqrt(v_hat) + ADAM_EPS) + ADAM_WD * w), m, v


def adam_big(name, parts, w, m, v):
    rows, cols = w.shape
    tm = _pick(rows, 384, 16)

    def fn(p0, p1, p2, p3, wv, mv, vv):
        g = ((p0.astype(F32) + p1.astype(F32)) + p2.astype(F32)) + p3.astype(F32)
        return (g,) + _adam_fn(wv, g, mv, vv)

    return rowwise(fn, [parts, w, m, v], [(cols, F32)] * 4, "adam_" + name, tm=tm, rows=rows)


def adam_small(name, gathered, w, m, v):
    def body(g_ref, w_ref, m_ref, v_ref, go_ref, d_ref, mo_ref, vo_ref):
        g = g_ref[0]
        for k in range(1, N_DEV):
            g = g + g_ref[k]
        go_ref[...] = g
        d_ref[...], mo_ref[...], vo_ref[...] = _adam_fn(w_ref[...], g, m_ref[...], v_ref[...])

    return pl.pallas_call(body, name=name, out_shape=[jax.ShapeDtypeStruct(w.shape, F32)] * 4,
                          compiler_params=_params())(gathered, w, m, v)


def _pack_small(names, vals, rows, last=None):
    flat = [vals[n].reshape(-1) for n in names]
    if last is not None:
        flat.append(last.reshape(-1))
    flat = jnp.concatenate(flat)
    return jnp.pad(flat, (0, rows * LANES - flat.shape[0])).reshape(rows, LANES)


def _unpack_small(names, pack, shapes):
    flat, out, off = pack.reshape(-1), {}, 0
    for n in names:
        size = math.prod(shapes[n])
        out[n] = flat[off:off + size].reshape(shapes[n])
        off += size
    return out, flat[off]


def _to_slots(name, g, shard_shape):
    rows, cols = shard_shape
    if name in ROW_SHARDED:
        return g.reshape(N_CHIPS, rows, cols)
    return g.reshape(rows, N_CHIPS, cols).transpose(1, 0, 2)


def _from_slots(name, s):
    _, rows, cols = s.shape
    if name in ROW_SHARDED:
        return s.reshape(N_CHIPS * rows, cols)
    return s.transpose(1, 0, 2).reshape(rows, N_CHIPS * cols)


def kernel(x, norm_mix_g, w_in, ssm_a_re, ssm_a_im, ssm_log_dt, ssm_b_re, ssm_b_im, ssm_c_re, ssm_c_im, ssm_d, w_glu, w_attn_out, w_out, norm_ffn_g, w_ffn_gate, w_ffn_up, w_ffn_down, norm_final_g, loss_target, m_norm_mix_g, m_w_in, m_ssm_a_re, m_ssm_a_im, m_ssm_log_dt, m_ssm_b_re, m_ssm_b_im, m_ssm_c_re, m_ssm_c_im, m_ssm_d, m_w_glu, m_w_attn_out, m_w_out, m_norm_ffn_g, m_w_ffn_gate, m_w_ffn_up, m_w_ffn_down, m_norm_final_g, v_norm_mix_g, v_w_in, v_ssm_a_re, v_ssm_a_im, v_ssm_log_dt, v_ssm_b_re, v_ssm_b_im, v_ssm_c_re, v_ssm_c_im, v_ssm_d, v_w_glu, v_w_attn_out, v_w_out, v_norm_ffn_g, v_w_ffn_gate, v_w_ffn_up, v_w_ffn_down, v_norm_final_g):
    given = dict(locals())
    def local(name, prefix=""):
        t = given[prefix + name][0]
        return t.T if name in TRANSPOSED else t

    shard = {n: local(n) for n in BIG}
    shapes = {n: given[n].shape for n in WEIGHTS}

    small = {n: given[n] for n in SMALL}
    small_2d = dict(small)
    for n in ("ssm_a_re", "ssm_a_im", "ssm_b_re", "ssm_b_im", "ssm_c_re", "ssm_c_im", "ssm_d"):
        small_2d[n] = small[n][0]
    small_2d["norm_final_g"] = norm_final_g.reshape(1, D_MODEL)

    core = lax.axis_index("c").astype(jnp.int32).reshape(1)
    loss, grad_x, parts, ssm_shares, gs_norm = local_step(
        x.reshape(TOKENS, D_MODEL), loss_target.reshape(TOKENS, D_MODEL),
        {n: shard[n].astype(BF16) for n in BIG}, small_2d, core)

    (norm_shares,) = run_comm(small_comm(_pack_small(NORM_SMALL, gs_norm, NORM_ROWS, last=loss)), "gather_norm_grads")
    small_out = [{} for _ in range(4)]
    for names, rows, shares in ((SSM_SMALL, SSM_ROWS, ssm_shares), (NORM_SMALL, NORM_ROWS, norm_shares)):
        packs = [_pack_small(names, {n: given[p + n] for n in names}, rows) for p in ("", "m_", "v_")]
        for kind, t in enumerate(adam_small("adam_" + names[0], shares, *packs)):
            vals, after = _unpack_small(names, t, shapes)
            small_out[kind].update(vals)
            if kind == 0:
                total_loss = after

    big_out = {}
    for n in BIG:
        res = adam_big(n, parts[n], shard[n], local(n, "m_"), local(n, "v_"))
        big_out[n] = [(t.T if n in TRANSPOSED else t)[None] for t in res]

    outs = [total_loss, grad_x.reshape(LOCAL_BATCH, SEQ, D_MODEL)]
    for kind in range(4):
        for n in WEIGHTS:
            outs.append(big_out[n][kind] if n in BIG else small_out[kind][n])
    return tuple(outs)
```

```python
import functools
import math

import jax
import jax.numpy as jnp
from jax import lax
from jax.experimental import pallas as pl
from jax.experimental.pallas import tpu as pltpu

F32 = jnp.float32
BF16 = jnp.bfloat16
MESH = pl.DeviceIdType.MESH

D_MODEL = 1024
SEQ = 2048
LOCAL_BATCH = 2
TOKENS = LOCAL_BATCH * SEQ
HEAD_DIM = 64
HEADS_PER_GROUP = 4
GROUP_W = HEADS_PER_GROUP * HEAD_DIM
N_GROUPS = 3
DILATIONS = (1, 4, 16)
ATTN_BLOCK = 128
ROPE_DIM = 16
ROPE_THETA = 500000.0
QKV_W = 3 * N_GROUPS * GROUP_W
SSM_W = 512
SSM_STATE_W = 2048
SSM_LANE_BLOCKS = 4
GATE_W = 2 * D_MODEL
D_FF = 2816
RMS_EPS = 1e-6
NEG_INF = -1e30
ADAM_LR, ADAM_B1, ADAM_B2, ADAM_EPS, ADAM_WD, ADAM_STEP = 0.001, 0.9, 0.999, 1e-08, 0.01, 10
N_CHIPS = 4
N_DEV = 8

VMEM_LIMIT = 56 * 1024 * 1024
LANES = 128


def _params(sem=None):
    return pltpu.CompilerParams(dimension_semantics=sem, vmem_limit_bytes=VMEM_LIMIT)


def _pick(n, cap, align=LANES):
    best = None
    for d in range(align, min(n, cap) + 1, align):
        if n % d == 0:
            best = d
    return n if best is None or n <= cap else best


_DIMS = {"nn": (((1,), (0,)), ((), ())), "nt": (((1,), (1,)), ((), ())), "tn": (((0,), (0,)), ((), ()))}


def _dot(a, b, mode):
    return lax.dot_general(a, b, _DIMS[mode], preferred_element_type=F32)


def matmul(a, b, mode, out_dtype, name, add=None, comm=None):
    if mode == "nn":
        (m, k), n = a.shape, b.shape[1]
    elif mode == "nt":
        (m, k), n = a.shape, b.shape[0]
    else:
        (k, m), n = a.shape, b.shape[1]
    tn = _pick(n, 1408)
    tk = _pick(k, 2816) if mode != "tn" else _pick(k, 1024)
    tm = _pick(m, 1408)
    out_bytes = jnp.dtype(out_dtype).itemsize

    def need(tm_):
        return 2 * 2 * (tm_ * tk + tk * tn) + tm_ * tn * (4 + 2 * out_bytes + (8 if add is not None else 0))

    while need(tm) > 40 * 1024 * 1024 and tm % 256 == 0:
        tm //= 2
    nk = k // tk
    a_spec = {"nn": pl.BlockSpec((tm, tk), lambda i, j, kk: (i, kk)),
              "nt": pl.BlockSpec((tm, tk), lambda i, j, kk: (i, kk)),
              "tn": pl.BlockSpec((tk, tm), lambda i, j, kk: (kk, i))}[mode]
    b_spec = {"nn": pl.BlockSpec((tk, tn), lambda i, j, kk: (kk, j)),
              "nt": pl.BlockSpec((tn, tk), lambda i, j, kk: (j, kk)),
              "tn": pl.BlockSpec((tk, tn), lambda i, j, kk: (kk, j))}[mode]
    o_spec = pl.BlockSpec((tm, tn), lambda i, j, kk: (i, j))

    def body(a_ref, b_ref, *rest):
        if add is not None:
            add_ref, o_ref, acc_ref = rest
        else:
            o_ref, acc_ref = rest
        part = _dot(a_ref[...], b_ref[...], mode)
        if nk == 1:
            res = part if add is None else part + add_ref[...]
            o_ref[...] = res.astype(out_dtype)
            return
        kk = pl.program_id(2)

        @pl.when(kk == 0)
        def _():
            acc_ref[...] = part

        @pl.when(kk > 0)
        def _():
            acc_ref[...] += part

        @pl.when(kk == nk - 1)
        def _():
            res = acc_ref[...] if add is None else acc_ref[...] + add_ref[...]
            o_ref[...] = res.astype(out_dtype)

    in_specs = [a_spec, b_spec] + ([o_spec] if add is not None else [])
    args = (a, b) + ((add,) if add is not None else ())
    res = hosted_call(
        body, comm, name, (m // tm, n // tn, nk), in_specs, [o_spec], [jax.ShapeDtypeStruct((m, n), out_dtype)],
        [pltpu.VMEM((tm, tn) if nk > 1 else (8, LANES), F32)], args, ("parallel", "parallel", "arbitrary"))
    return res[0] if comm is None else res


FFN_TM, FFN_TN = 512, 1408


def ffn_in(h2, wg_t, wu_t):
    def body(h_ref, wg_ref, wu_ref, a_ref, b_ref, act_ref):
        hv = h_ref[...]
        a, b = _dot(hv, wg_ref[...], "nt"), _dot(hv, wu_ref[...], "nt")
        a_ref[...] = a.astype(BF16)
        b_ref[...] = b.astype(BF16)
        act_ref[...] = _swiglu_fn(a, b).astype(BF16)

    rows = pl.BlockSpec((FFN_TM, D_MODEL), lambda i, j: (i, 0))
    wts = pl.BlockSpec((FFN_TN, D_MODEL), lambda i, j: (j, 0))
    out = pl.BlockSpec((FFN_TM, FFN_TN), lambda i, j: (i, j))
    return pl.pallas_call(
        body, name="ffn_in", grid=(TOKENS // FFN_TM, D_FF // FFN_TN), in_specs=[rows, wts, wts],
        out_specs=[out] * 3, out_shape=[jax.ShapeDtypeStruct((TOKENS, D_FF), BF16)] * 3,
        compiler_params=_params(("parallel", "parallel")),
    )(h2, wg_t, wu_t)


def ffn_in_bwd(dx2_b, wd, a, b):
    def body(dx_ref, wd_ref, a_ref, b_ref, da_ref, db_ref):
        dact = _dot(dx_ref[...], wd_ref[...], "nt")
        _, vjp = jax.vjp(_swiglu_fn, a_ref[...].astype(F32), b_ref[...].astype(F32))
        da, db = vjp(dact)
        da_ref[...] = da.astype(BF16)
        db_ref[...] = db.astype(BF16)

    rows = pl.BlockSpec((FFN_TM, D_MODEL), lambda i, j: (i, 0))
    wts = pl.BlockSpec((FFN_TN, D_MODEL), lambda i, j: (j, 0))
    out = pl.BlockSpec((FFN_TM, FFN_TN), lambda i, j: (i, j))
    return pl.pallas_call(
        body, name="ffn_in_bwd", grid=(TOKENS // FFN_TM, D_FF // FFN_TN), in_specs=[rows, wts, out, out],
        out_specs=[out] * 2, out_shape=[jax.ShapeDtypeStruct((TOKENS, D_FF), BF16)] * 2,
        compiler_params=_params(("parallel", "parallel")),
    )(dx2_b, wd, a, b)


def mix_in_bwd(grads, weights, partial, x, g, skip, comm=None):
    n = len(grads)
    tm = 512

    def body(*refs):
        a_refs, b_refs = refs[:n], refs[n:2 * n]
        part_ref, x_ref, g_ref, skip_ref, gx_ref, dg_ref = refs[2 * n:]
        dh = part_ref[...]
        for a_ref, b_ref in zip(a_refs, b_refs):
            dh = dh + _dot(a_ref[...], b_ref[...], "nn")
        _, vjp = jax.vjp(_rms, x_ref[...], g_ref[...])
        dx, dg = vjp(dh)
        gx_ref[...] = dx + skip_ref[...]
        first = pl.program_id(0) == 0

        @pl.when(first)
        def _():
            dg_ref[...] = dg

        @pl.when(jnp.logical_not(first))
        def _():
            dg_ref[...] += dg

    rows = pl.BlockSpec((tm, D_MODEL), lambda i: (i, 0))
    gain = pl.BlockSpec((1, D_MODEL), lambda i: (0, 0))
    in_specs = [pl.BlockSpec((tm, a.shape[1]), lambda i: (i, 0)) for a in grads]
    in_specs += [pl.BlockSpec(b.shape, lambda i: (0, 0)) for b in weights]
    return hosted_call(
        body, comm, "mix_in_bwd", (TOKENS // tm,), in_specs + [rows, rows, gain, rows], [rows, gain],
        [jax.ShapeDtypeStruct((TOKENS, D_MODEL), F32), jax.ShapeDtypeStruct((1, D_MODEL), F32)], [],
        (*grads, *weights, partial, x, g, skip), ("arbitrary",))


def rowwise(fn, ins, outs, name, accs=(), tm=256, rows=TOKENS, comm=None):
    in_specs, args = [], []
    for item in ins:
        arr, width, blk = item if isinstance(item, tuple) else (item, None, 0)
        if arr.ndim == 3:
            for k in range(arr.shape[0]):
                in_specs.append(pl.BlockSpec((None, tm, arr.shape[2]), functools.partial(lambda i, k_: (k_, i, 0), k_=k)))
                args.append(arr)
            continue
        if arr.shape[0] == 1:
            in_specs.append(pl.BlockSpec(arr.shape, lambda i: (0, 0)))
        elif width is None:
            in_specs.append(pl.BlockSpec((tm, arr.shape[1]), lambda i: (i, 0)))
        else:
            in_specs.append(pl.BlockSpec((tm, width), functools.partial(lambda i, blk_: (i, blk_), blk_=blk)))
        args.append(arr)
    out_specs = [pl.BlockSpec((tm, c), lambda i: (i, 0)) for c, _ in outs]
    out_specs += [pl.BlockSpec((1, c), lambda i: (0, 0)) for c in accs]
    out_shape = [jax.ShapeDtypeStruct((rows, c), dt) for c, dt in outs]
    out_shape += [jax.ShapeDtypeStruct((1, c), F32) for c in accs]
    n_in, n_out = len(args), len(outs)
    c_ins, c_outs, c_sems = _comm_operands(comm)

    def body(*refs):
        refs, c_refs = _comm_refs(comm, refs, n_in, n_out + len(accs))
        step = pl.program_id(0)
        _comm_begin(comm, c_refs, step, rows // tm)
        res = fn(*[r[...] for r in refs[:n_in]])
        for r, v in zip(refs[n_in:n_in + n_out], res[:n_out]):
            r[...] = v.astype(r.dtype)
        first = step == 0
        for r, v in zip(refs[n_in + n_out:], res[n_out:]):
            @pl.when(first)
            def _(r=r, v=v):
                r[...] = v

            @pl.when(jnp.logical_not(first))
            def _(r=r, v=v):
                r[...] += v
        _comm_end(comm, c_refs, step, rows // tm)

    return pl.pallas_call(
        body, name=name, grid=(rows // tm,), in_specs=in_specs + [ANY] * len(c_ins),
        out_specs=out_specs + [ANY] * len(c_outs), out_shape=out_shape + c_outs, scratch_shapes=c_sems,
        compiler_params=_params(("arbitrary",)),
    )(*args, *c_ins)


def _rms(x, g):
    return x * lax.rsqrt(jnp.mean(x * x, axis=-1, keepdims=True) + RMS_EPS) * g


def _colsum(v):
    return jnp.sum(v, axis=0, keepdims=True)


PAIR_W = 2 * HEAD_DIM
N_PAIRS = HEADS_PER_GROUP // 2


def _qkv_order(w_t, back=False):
    dims = (N_PAIRS, N_GROUPS, 3) if back else (3, N_GROUPS, N_PAIRS)
    return w_t.reshape(dims + (PAIR_W, w_t.shape[1])).transpose(2, 1, 0, 3, 4).reshape(QKV_W, w_t.shape[1])


def _rope_tables():
    half = ROPE_DIM // 2
    inv = jnp.power(jnp.float32(ROPE_THETA), -jnp.arange(half, dtype=F32) * 2.0 / ROPE_DIM)
    ang = jnp.arange(SEQ, dtype=F32)[:, None] * inv[None, :]
    cos, sin = jnp.cos(ang), jnp.sin(ang)
    zeros = jnp.zeros((SEQ, HEAD_DIM - ROPE_DIM), F32)
    zh = jnp.zeros((SEQ, half), F32)
    c = jnp.concatenate([cos, cos, zeros + 1.0], axis=1)
    sa = jnp.concatenate([-sin, zh, zeros], axis=1)
    sb = jnp.concatenate([zh, sin, zeros], axis=1)
    return [jnp.tile(t, (1, 2)) for t in (c, sa, sb)]


def _rope_fwd(x, c, sa, sb):
    return x * c + pltpu.roll(x, PAIR_W - 8, 1) * sa + pltpu.roll(x, 8, 1) * sb


def _rope_bwd(dy, c, sa, sb):
    return dy * c + pltpu.roll(dy * sb, PAIR_W - 8, 1) + pltpu.roll(dy * sa, 8, 1)


def _band_masks():
    row = lax.broadcasted_iota(jnp.int32, (ATTN_BLOCK, ATTN_BLOCK), 0)
    col = lax.broadcasted_iota(jnp.int32, (ATTN_BLOCK, ATTN_BLOCK), 1)
    return col <= row, col >= row


def _stack_rows(t):
    return jnp.concatenate([t, t], axis=0)


def _stack_heads(t, first_head):
    return jnp.concatenate([jnp.where(first_head, t, 0), jnp.where(first_head, 0, t)], axis=0)


def _per_head(fn):
    return jnp.concatenate([fn(slice(h * HEAD_DIM, (h + 1) * HEAD_DIM)) for h in range(2)], axis=1)


def _slab_spec(kind):
    return pl.BlockSpec((None, SEQ, PAIR_W), lambda b, p, g: (b, 0, p * 3 * N_GROUPS + g * 3 + kind))


_TABLE_SPEC = pl.BlockSpec((SEQ, PAIR_W), lambda b, p, g: (0, 0))
_PAIR_SPEC = pl.BlockSpec((None, SEQ, PAIR_W), lambda b, p, g: (b, 0, p))


def _block_rows(dil, r, n):
    return pl.ds(n * (ATTN_BLOCK * dil) + r, ATTN_BLOCK, stride=dil)


def attn_fwd(qkv, tables, comm=None):
    scale = HEAD_DIM ** -0.5

    def body(q_ref, k_ref, v_ref, c_ref, sa_ref, sb_ref, attn_b_ref, attn_ref, lse_ref, qs, ks, o0, o1, o2, l0, l1, l2):
        g = pl.program_id(2)
        c, sa, sb = c_ref[...], sa_ref[...], sb_ref[...]
        qs[...] = _rope_fwd(q_ref[...], c, sa, sb) * scale
        ks[...] = _rope_fwd(k_ref[...], c, sa, sb)
        cur_mask, prev_mask = _band_masks()
        first_head = lax.broadcasted_iota(jnp.int32, (ATTN_BLOCK, PAIR_W), 1) < HEAD_DIM

        def run(dil, o_slab, l_slab):
            nb = SEQ // dil // ATTN_BLOCK

            def block(idx, carry):
                r, n = lax.div(idx, nb), lax.rem(idx, nb)
                cur, prev = _block_rows(dil, r, n), _block_rows(dil, r, jnp.maximum(n - 1, 0))
                q = qs[cur, :].astype(BF16)
                kc, kp = ks[cur, :].astype(BF16), ks[prev, :].astype(BF16)
                vc, vp = v_ref[cur, :].astype(BF16), v_ref[prev, :].astype(BF16)
                q2 = _stack_heads(q, first_head)
                mask = _stack_rows(jnp.concatenate([jnp.logical_and(prev_mask, n > 0), cur_mask], axis=1))
                s2 = jnp.where(mask, _dot(q2, jnp.concatenate([kp, kc], axis=0), "nt"), NEG_INF)
                m = jnp.max(s2, axis=-1, keepdims=True)
                vcat, two = jnp.concatenate([vp, vc], axis=0), _stack_rows(first_head)
                vext = jnp.concatenate([jnp.where(two, vcat, 1), jnp.where(two, 1, vcat)], axis=1)
                r2 = _dot(jnp.exp(s2 - m).astype(BF16), vext, "nn")
                r0, r1 = r2[:ATTN_BLOCK, :PAIR_W], r2[ATTN_BLOCK:, PAIR_W:]
                num = jnp.where(first_head, r0, r1)
                den = pltpu.roll(jnp.where(first_head, r1, r0), HEAD_DIM, 1)
                o_slab[cur, :] = num / den
                l_slab[cur, :] = jnp.where(first_head, m[:ATTN_BLOCK], m[ATTN_BLOCK:]) + jnp.log(den)
                return carry

            lax.fori_loop(0, SEQ // ATTN_BLOCK, block, 0, unroll=4)

        for gi, (o_slab, l_slab) in enumerate(((o0, l0), (o1, l1), (o2, l2))):
            @pl.when(g == gi)
            def _(gi=gi, o_slab=o_slab, l_slab=l_slab):
                run(DILATIONS[gi], o_slab, l_slab)

        @pl.when(g == N_GROUPS - 1)
        def _():
            a, b, cc = l0[...], l1[...], l2[...]
            m = jnp.maximum(jnp.maximum(a, b), cc)
            e0, e1, e2 = jnp.exp(a - m), jnp.exp(b - m), jnp.exp(cc - m)
            tot = e0 + e1 + e2
            attn = (e0 * o0[...] + e1 * o1[...] + e2 * o2[...]) / tot
            attn_ref[...] = attn
            attn_b_ref[...] = attn.astype(BF16)
            lse_ref[...] = m + jnp.log(tot)

    shape = (LOCAL_BATCH, SEQ, GROUP_W)
    slab = pltpu.VMEM((SEQ, PAIR_W), F32)
    return hosted_call(
        body, comm, "attn_fwd", (LOCAL_BATCH, N_PAIRS, N_GROUPS),
        [_slab_spec(0), _slab_spec(1), _slab_spec(2), _TABLE_SPEC, _TABLE_SPEC, _TABLE_SPEC], [_PAIR_SPEC] * 3,
        [jax.ShapeDtypeStruct(shape, BF16), jax.ShapeDtypeStruct(shape, F32), jax.ShapeDtypeStruct(shape, F32)],
        [slab] * 8, (qkv, qkv, qkv, *tables), ("parallel", "parallel", "arbitrary"))


def attn_bwd(qkv, tables, dattn, attn, lse, comm=None):
    scale = HEAD_DIM ** -0.5

    def body(q_ref, k_ref, v_ref, c_ref, sa_ref, sb_ref, do_ref, out_ref, lse_ref, dqkv_ref, qs, ks, dl, dq_s, dk_s, dv_s):
        g = pl.program_id(2)
        c, sa, sb = c_ref[...], sa_ref[...], sb_ref[...]
        qs[...] = _rope_fwd(q_ref[...], c, sa, sb) * scale
        ks[...] = _rope_fwd(k_ref[...], c, sa, sb)
        @pl.when(g == 0)
        def _():
            prod = do_ref[...] * out_ref[...]
            dl[...] = _per_head(
                lambda sl: jnp.broadcast_to(jnp.sum(prod[:, sl], axis=-1, keepdims=True), (SEQ, HEAD_DIM)))

        cur_mask, prev_mask = _band_masks()
        first_head = lax.broadcasted_iota(jnp.int32, (ATTN_BLOCK, PAIR_W), 1) < HEAD_DIM

        def run(dil):
            nb = SEQ // dil // ATTN_BLOCK

            def block(idx, carry):
                r, n = lax.div(idx, nb), lax.rem(idx, nb)
                cur = _block_rows(dil, r, n)
                prev = _block_rows(dil, r, jnp.maximum(n - 1, 0))
                nxt = _block_rows(dil, r, jnp.minimum(n + 1, nb - 1))
                q0, q1 = qs[cur, :].astype(BF16), qs[nxt, :].astype(BF16)
                kp, kc = ks[prev, :].astype(BF16), ks[cur, :].astype(BF16)
                vp, vc = v_ref[prev, :].astype(BF16), v_ref[cur, :].astype(BF16)
                do0, do1 = do_ref[cur, :].astype(BF16), do_ref[nxt, :].astype(BF16)
                lse0, lse1, dl0, dl1 = lse_ref[cur, :], lse_ref[nxt, :], dl[cur, :], dl[nxt, :]
                has_prev = jnp.logical_and(prev_mask, n > 0)
                has_next = jnp.logical_and(prev_mask, n < nb - 1)

                def per_row(t):
                    return jnp.concatenate([t[:, 0:1], t[:, HEAD_DIM:HEAD_DIM + 1]], axis=0)

                q20, q21 = _stack_heads(q0, first_head), _stack_heads(q1, first_head)
                do20, do21 = _stack_heads(do0, first_head), _stack_heads(do1, first_head)
                kcat, vcat = jnp.concatenate([kp, kc], axis=0), jnp.concatenate([vp, vc], axis=0)
                mask0 = _stack_rows(jnp.concatenate([has_prev, cur_mask], axis=1))
                p0 = jnp.where(mask0, jnp.exp(_dot(q20, kcat, "nt") - per_row(lse0)), 0.0)
                ds0 = (p0 * (_dot(do20, vcat, "nt") - per_row(dl0))).astype(BF16)
                p1 = jnp.where(_stack_rows(has_next), jnp.exp(_dot(q21, kc, "nt") - per_row(lse1)), 0.0)
                ds1 = (p1 * (_dot(do21, vc, "nt") - per_row(dl1))).astype(BF16)
                dq2 = _dot(ds0, kcat, "nn")
                dq_s[cur, :] = jnp.where(first_head, dq2[:ATTN_BLOCK], dq2[ATTN_BLOCK:])
                ds_cur = jnp.concatenate([ds0[:, ATTN_BLOCK:], ds1], axis=0)
                p_cur = jnp.concatenate([p0[:, ATTN_BLOCK:], p1], axis=0).astype(BF16)
                dk_s[cur, :] = _dot(ds_cur, jnp.concatenate([q20, q21], axis=0), "tn")
                dv_s[cur, :] = _dot(p_cur, jnp.concatenate([do20, do21], axis=0), "tn")
                return carry

            lax.fori_loop(0, SEQ // ATTN_BLOCK, block, 0, unroll=2)

        for gi in range(N_GROUPS):
            @pl.when(g == gi)
            def _(gi=gi):
                run(DILATIONS[gi])

        dqkv_ref[:, 0:PAIR_W] = _rope_bwd(dq_s[...] * scale, c, sa, sb).astype(BF16)
        dqkv_ref[:, PAIR_W:2 * PAIR_W] = _rope_bwd(dk_s[...], c, sa, sb).astype(BF16)
        dqkv_ref[:, 2 * PAIR_W:] = dv_s[...].astype(BF16)

    slab = pltpu.VMEM((SEQ, PAIR_W), F32)
    return hosted_call(
        body, comm, "attn_bwd", (LOCAL_BATCH, N_PAIRS, N_GROUPS),
        [_slab_spec(0), _slab_spec(1), _slab_spec(2), _TABLE_SPEC, _TABLE_SPEC, _TABLE_SPEC,
         _PAIR_SPEC, _PAIR_SPEC, _PAIR_SPEC],
        [pl.BlockSpec((None, SEQ, 3 * PAIR_W), lambda b, p, g: (b, 0, p * N_GROUPS + g))],
        [jax.ShapeDtypeStruct((LOCAL_BATCH, SEQ, QKV_W), BF16)],
        [slab] * 6, (qkv, qkv, qkv, *tables, dattn, attn, lse), ("parallel", "parallel", "arbitrary"))


def _discretize(lr, li, log_dt, br, bi):
    dt = jnp.exp(log_dt)
    mag = jnp.exp(lr * dt)
    ab_re, ab_im = mag * jnp.cos(li * dt), mag * jnp.sin(li * dt)
    den = lr * lr + li * li
    nr, ni = ab_re - 1.0, ab_im
    f_re = (nr * lr + ni * li) / den
    f_im = (ni * lr - nr * li) / den
    return ab_re, ab_im, f_re[None] * br - f_im[None] * bi, f_re[None] * bi + f_im[None] * br


def ssm_prep(lr, li, log_dt, br, bi):
    def body(lr_ref, li_ref, dt_ref, br_ref, bi_ref, *outs):
        for o, v in zip(outs, _discretize(lr_ref[...], li_ref[...], dt_ref[...], br_ref[...], bi_ref[...])):
            o[...] = v
    shapes = [lr, li, br, bi]
    return pl.pallas_call(body, name="ssm_prep",
                          out_shape=[jax.ShapeDtypeStruct(s.shape, F32) for s in shapes])(lr, li, log_dt, br, bi)


def ssm_prep_bwd(lr, li, log_dt, br, bi, g_ab_re, g_ab_im, g_bb_re, g_bb_im):
    def body(lr_ref, li_ref, dt_ref, br_ref, bi_ref, g0, g1, g2, g3, *outs):
        _, vjp = jax.vjp(_discretize, lr_ref[...], li_ref[...], dt_ref[...], br_ref[...], bi_ref[...])
        for o, v in zip(outs, vjp((g0[...], g1[...], g2[...], g3[...]))):
            o[...] = v
    shapes = [lr, li, log_dt, br, bi]
    return pl.pallas_call(body, name="ssm_prep_bwd",
                          out_shape=[jax.ShapeDtypeStruct(s.shape, F32) for s in shapes])(
        lr, li, log_dt, br, bi, g_ab_re, g_ab_im, g_bb_re, g_bb_im)


def _block_diag(t):
    per = SSM_STATE_W // SSM_LANE_BLOCKS // 64
    g = t.transpose(1, 0, 2).reshape(SSM_LANE_BLOCKS, per, 16, 64)
    eye = jnp.eye(per, dtype=t.dtype)
    return jnp.einsum("jgcn,gh->jgchn", g, eye).reshape(SSM_LANE_BLOCKS, per * 16, per * 64)


def _block_diag_t(m):
    per = SSM_STATE_W // SSM_LANE_BLOCKS // 64
    m5 = m.reshape(SSM_LANE_BLOCKS, per, 16, per, 64)
    d = jnp.einsum("jgchn,gh->jgcn", m5, jnp.eye(per, dtype=m.dtype))
    return d.reshape(SSM_LANE_BLOCKS * per, 16, 64).transpose(1, 0, 2)


def _cmul(ar, ai, br, bi):
    return ar * br - ai * bi, ar * bi + ai * br


def _power_tables(ar, ai, reverse):
    width = ar.shape[1]
    row = lax.broadcasted_iota(jnp.int32, (8, width), 0)
    pows = [(ar, ai)]
    for _ in range(7):
        pows.append(_cmul(pows[-1][0], pows[-1][1], ar, ai))
    steps = []
    for k in (1, 2, 4):
        keep = (row >= k) if not reverse else (row < 8 - k)
        steps.append((jnp.where(keep, pows[k - 1][0], 0.0), jnp.where(keep, pows[k - 1][1], 0.0)))
    cr = jnp.zeros((8, width), F32)
    ci = jnp.zeros((8, width), F32)
    for i in range(8):
        pr, pi = pows[i] if not reverse else pows[7 - i]
        cr = jnp.where(row == i, pr, cr)
        ci = jnp.where(row == i, pi, ci)
    return steps, (cr, ci)


SCAN_CHUNK = 512
STATE_BLOCK = SSM_STATE_W // SSM_LANE_BLOCKS
CHAN_BLOCK = SSM_W // SSM_LANE_BLOCKS


def ssm_fwd(u, ab_re, ab_im, bb_re, bb_im, cb_re, cb_im, d_skip, comm=None):
    nt = SEQ // SCAN_CHUNK
    chan = pl.BlockSpec((None, SCAN_CHUNK, CHAN_BLOCK), lambda b, j, t: (b, t, j))
    state = pl.BlockSpec((None, SCAN_CHUNK, STATE_BLOCK), lambda b, j, t: (b, t, j))
    mat = pl.BlockSpec((None, CHAN_BLOCK, STATE_BLOCK), lambda b, j, t: (j, 0, 0))
    lane = pl.BlockSpec((1, STATE_BLOCK), lambda b, j, t: (0, j))
    dsp = pl.BlockSpec((1, CHAN_BLOCK), lambda b, j, t: (0, j))

    def body(u_ref, ar_ref, ai_ref, bbr_ref, bbi_ref, cbr_ref, cbi_ref, d_ref, y_ref, yg_ref, xr_ref, xi_ref,
             car_r, car_i):
        @pl.when(pl.program_id(2) == 0)
        def _():
            car_r[...] = jnp.zeros_like(car_r)
            car_i[...] = jnp.zeros_like(car_i)

        steps, (pr, pi) = _power_tables(ar_ref[...], ai_ref[...], reverse=False)
        uf = u_ref[...]
        ub = uf.astype(BF16)
        xr_ref[...] = _dot(ub, bbr_ref[...], "nn")
        xi_ref[...] = _dot(ub, bbi_ref[...], "nn")

        def tile(i, carry):
            cr, ci = carry
            sl = pl.ds(pl.multiple_of(i * 8, 8), 8)
            br, bi = xr_ref[sl, :], xi_ref[sl, :]
            for k, (sr, si) in zip((1, 2, 4), steps):
                tr, ti = _cmul(sr, si, pltpu.roll(br, k, 0), pltpu.roll(bi, k, 0))
                br, bi = br + tr, bi + ti
            tr, ti = _cmul(pr, pi, cr, ci)
            br, bi = br + tr, bi + ti
            xr_ref[sl, :] = br
            xi_ref[sl, :] = bi
            return br[7:8, :], bi[7:8, :]

        cr, ci = lax.fori_loop(0, SCAN_CHUNK // 8, tile, (car_r[0:1, :], car_i[0:1, :]), unroll=4)
        car_r[0:1, :] = cr
        car_i[0:1, :] = ci
        y = (_dot(xr_ref[...].astype(BF16), cbr_ref[...], "nt") - _dot(xi_ref[...].astype(BF16), cbi_ref[...], "nt")
             + d_ref[...] * uf)
        y_ref[...] = y
        yg_ref[...] = jax.nn.gelu(y).astype(BF16)

    return hosted_call(
        body, comm, "ssm_fwd", (LOCAL_BATCH, SSM_LANE_BLOCKS, nt),
        [chan, lane, lane, mat, mat, mat, mat, dsp], [chan, chan, state, state],
        [jax.ShapeDtypeStruct((LOCAL_BATCH, SEQ, SSM_W), F32), jax.ShapeDtypeStruct((LOCAL_BATCH, SEQ, SSM_W), BF16),
         jax.ShapeDtypeStruct((LOCAL_BATCH, SEQ, SSM_STATE_W), F32),
         jax.ShapeDtypeStruct((LOCAL_BATCH, SEQ, SSM_STATE_W), F32)],
        [pltpu.VMEM((8, STATE_BLOCK), F32), pltpu.VMEM((8, STATE_BLOCK), F32)],
        (u, ab_re, ab_im, bb_re, bb_im, cb_re, cb_im, d_skip), ("parallel", "parallel", "arbitrary"))


def ssm_bwd(dyg, y, u, xr, xi, ab_re, ab_im, bb_re, bb_im, cb_re, cb_im, d_skip, comm=None):
    nt = SEQ // SCAN_CHUNK
    ntile = SCAN_CHUNK // 8

    def rev(t):
        return nt - 1 - t

    chan = pl.BlockSpec((None, SCAN_CHUNK, CHAN_BLOCK), lambda j, b, t: (b, rev(t), j))
    state = pl.BlockSpec((None, SCAN_CHUNK, STATE_BLOCK), lambda j, b, t: (b, rev(t), j))
    before = pl.BlockSpec((None, 8, STATE_BLOCK), lambda j, b, t: (b, jnp.maximum(rev(t) * ntile - 1, 0), j))
    mat = pl.BlockSpec((None, CHAN_BLOCK, STATE_BLOCK), lambda j, b, t: (j, 0, 0))
    lane = pl.BlockSpec((1, STATE_BLOCK), lambda j, b, t: (0, j))
    lane8 = pl.BlockSpec((8, STATE_BLOCK), lambda j, b, t: (0, j))
    dsp = pl.BlockSpec((1, CHAN_BLOCK), lambda j, b, t: (0, j))

    def body(dyg_ref, y_ref, u_ref, xr_ref, xi_ref, xrb_ref, xib_ref, ar_ref, ai_ref, bbr_ref, bbi_ref, cbr_ref,
             cbi_ref, d_ref, du_ref, dcbr_ref, dcbi_ref, dbbr_ref, dbbi_ref, dd_ref, dar_ref, dai_ref,
             lam_r, lam_i, car_r, car_i):
        b, t = pl.program_id(1), pl.program_id(2)
        first = jnp.logical_and(b == 0, t == 0)

        @pl.when(t == 0)
        def _():
            car_r[...] = jnp.zeros_like(car_r)
            car_i[...] = jnp.zeros_like(car_i)

        @pl.when(first)
        def _():
            for r in (dcbr_ref, dcbi_ref, dbbr_ref, dbbi_ref, dd_ref, dar_ref, dai_ref):
                r[...] = jnp.zeros_like(r)

        steps, (pr, pi) = _power_tables(ar_ref[...], -ai_ref[...], reverse=True)
        uf = u_ref[...]
        _, gelu_vjp = jax.vjp(jax.nn.gelu, y_ref[...])
        dy = gelu_vjp(dyg_ref[...])[0]
        dyb = dy.astype(BF16)
        dd_ref[...] += _colsum(dy * uf)
        lam_r[...] = _dot(dyb, cbr_ref[...], "nn")
        lam_i[...] = -_dot(dyb, cbi_ref[...], "nn")
        dcbr_ref[...] += _dot(dyb, xr_ref[...].astype(BF16), "tn")
        dcbi_ref[...] -= _dot(dyb, xi_ref[...].astype(BF16), "tn")
        row0 = lax.broadcasted_iota(jnp.int32, (8, STATE_BLOCK), 0) == 0
        has_before = rev(t) > 0
        xrb = jnp.where(has_before, xrb_ref[...], 0.0)
        xib = jnp.where(has_before, xib_ref[...], 0.0)

        def tile(s, carry):
            cr, ci, acc_r, acc_i = carry
            i = ntile - 1 - s
            sl = pl.ds(pl.multiple_of(i * 8, 8), 8)
            gr, gi = lam_r[sl, :], lam_i[sl, :]
            for k, (sr, si) in zip((1, 2, 4), steps):
                tr, ti = _cmul(sr, si, pltpu.roll(gr, 8 - k, 0), pltpu.roll(gi, 8 - k, 0))
                gr, gi = gr + tr, gi + ti
            tr, ti = _cmul(pr, pi, cr, ci)
            gr, gi = gr + tr, gi + ti
            lam_r[sl, :] = gr
            lam_i[sl, :] = gi
            sp = pl.ds(pl.multiple_of(jnp.maximum(i - 1, 0) * 8, 8), 8)
            pvr = jnp.where(i > 0, xr_ref[sp, :], xrb)
            pvi = jnp.where(i > 0, xi_ref[sp, :], xib)
            xsr = jnp.where(row0, pltpu.roll(pvr, 1, 0), pltpu.roll(xr_ref[sl, :], 1, 0))
            xsi = jnp.where(row0, pltpu.roll(pvi, 1, 0), pltpu.roll(xi_ref[sl, :], 1, 0))
            acc_r = acc_r + xsr * gr + xsi * gi
            acc_i = acc_i + xsr * gi - xsi * gr
            return gr[0:1, :], gi[0:1, :], acc_r, acc_i

        zero = jnp.zeros((8, STATE_BLOCK), F32)
        cr, ci, acc_r, acc_i = lax.fori_loop(0, ntile, tile, (car_r[0:1, :], car_i[0:1, :], zero, zero), unroll=2)
        car_r[0:1, :] = cr
        car_i[0:1, :] = ci
        dar_ref[...] += acc_r
        dai_ref[...] += acc_i
        lrb, lib = lam_r[...].astype(BF16), lam_i[...].astype(BF16)
        du = _dot(lrb, bbr_ref[...], "nt") + _dot(lib, bbi_ref[...], "nt") + d_ref[...] * dy
        du_ref[...] = du.astype(BF16)
        ub = uf.astype(BF16)
        dbbr_ref[...] += _dot(ub, lrb, "tn")
        dbbi_ref[...] += _dot(ub, lib, "tn")

    mat_shape = jax.ShapeDtypeStruct((SSM_LANE_BLOCKS, CHAN_BLOCK, STATE_BLOCK), F32)
    return hosted_call(
        body, comm, "ssm_bwd", (SSM_LANE_BLOCKS, LOCAL_BATCH, nt),
        [chan, chan, chan, state, state, before, before, lane, lane, mat, mat, mat, mat, dsp],
        [chan, mat, mat, mat, mat, dsp, lane8, lane8],
        [jax.ShapeDtypeStruct((LOCAL_BATCH, SEQ, SSM_W), BF16), mat_shape, mat_shape, mat_shape, mat_shape,
         jax.ShapeDtypeStruct((1, SSM_W), F32), jax.ShapeDtypeStruct((8, SSM_STATE_W), F32),
         jax.ShapeDtypeStruct((8, SSM_STATE_W), F32)],
        [pltpu.VMEM((SCAN_CHUNK, STATE_BLOCK), F32), pltpu.VMEM((SCAN_CHUNK, STATE_BLOCK), F32),
         pltpu.VMEM((8, STATE_BLOCK), F32), pltpu.VMEM((8, STATE_BLOCK), F32)],
        (dyg, y, u, xr, xi, xr, xi, ab_re, ab_im, bb_re, bb_im, cb_re, cb_im, d_skip),
        ("parallel", "arbitrary", "arbitrary"))


def _merge_fn(g0, g1, attn_d, za, zb):
    return jax.nn.sigmoid(g0) * attn_d + jax.nn.sigmoid(g1) * (za * jax.nn.sigmoid(zb))


def _swiglu_fn(a, b):
    return jax.nn.silu(a) * b


def _reduce_start(names, gw, shard_shapes):
    return swap_comm([_to_slots(n, gw[n], shard_shapes[n]) for n in names])


def _reduce_chip(names, swap, got, core):
    return exchange_comm([add_halves(n, g, r, core) for n, g, r in zip(names, swap.ins, got)])


def local_step(x, target, shards, small, core):
    g_mix, g_ffn, g_final = small["norm_mix_g"], small["norm_ffn_g"], small["norm_final_g"]
    tables = _rope_tables()
    seqs = lambda t: t.reshape(LOCAL_BATCH, SEQ, t.shape[-1])
    toks = lambda t: t.reshape(TOKENS, t.shape[-1])
    shard_shapes = {n: s.shape for n, s in shards.items()}
    w = {}

    def gather(names):
        return gather_comm([shards[n] for n in names])

    def arrived(names, slots):
        for n, s in zip(names, slots):
            w[n] = _from_slots(n, s)

    h, *slots = rowwise(lambda xv, g: (_rms(xv, g),), [x, g_mix], [(D_MODEL, BF16)], "norm_mix", comm=gather(["w_in"]))
    arrived(["w_in"], slots)
    w_qkv, w_u, w_gate = _qkv_order(w["w_in"][:QKV_W]), w["w_in"][QKV_W:QKV_W + SSM_W], w["w_in"][QKV_W + SSM_W:]
    mixer_out = ["w_attn_out", "w_glu", "w_out"]
    qkv, *slots = matmul(h, w_qkv, "nt", F32, "proj_qkv", comm=gather(mixer_out))
    arrived(mixer_out, slots)
    qkv = seqs(qkv)
    u = seqs(matmul(h, w_u, "nt", F32, "proj_u"))
    gl = matmul(h, w_gate, "nt", F32, "proj_gate")
    attn_b, attn, lse, *slots = attn_fwd(qkv, tables, comm=gather(["w_ffn_gate"]))
    arrived(["w_ffn_gate"], slots)
    attn_b = toks(attn_b)
    attn_d = matmul(attn_b, w["w_attn_out"], "nn", F32, "attn_out")

    br_t = small["ssm_b_re"].transpose(2, 0, 1)
    bi_t = small["ssm_b_im"].transpose(2, 0, 1)
    log_dt = small["ssm_log_dt"].reshape(32, 1)
    ab_re, ab_im, bb_re_t, bb_im_t = ssm_prep(small["ssm_a_re"], small["ssm_a_im"], log_dt, br_t, bi_t)
    ab = [ab_re.reshape(1, SSM_STATE_W), ab_im.reshape(1, SSM_STATE_W)]
    bb = [_block_diag(bb_re_t).astype(BF16), _block_diag(bb_im_t).astype(BF16)]
    cb = [_block_diag(small["ssm_c_re"].transpose(1, 0, 2)).astype(BF16),
          _block_diag(small["ssm_c_im"].transpose(1, 0, 2)).astype(BF16)]
    d_skip = small["ssm_d"].reshape(1, SSM_W)
    ffn_rest = ["w_ffn_up", "w_ffn_down"]
    y, yg, xr, xi, *slots = ssm_fwd(u, *ab, *bb, *cb, d_skip, comm=gather(ffn_rest))
    arrived(ffn_rest, slots)
    yg2 = toks(yg)
    z = matmul(yg2, w["w_glu"], "nn", F32, "glu")
    gate_ins = [(gl, D_MODEL, 0), (gl, D_MODEL, 1), attn_d, (z, D_MODEL, 0), (z, D_MODEL, 1)]
    (merged,) = rowwise(lambda *v: (_merge_fn(*v),), gate_ins, [(D_MODEL, BF16)], "merge")
    x1 = matmul(merged, w["w_out"], "nn", F32, "out_proj", add=x)
    (h2,) = rowwise(lambda xv, g: (_rms(xv, g),), [x1, g_ffn], [(D_MODEL, BF16)], "norm_ffn")
    a, b, act = ffn_in(h2, w["w_ffn_gate"], w["w_ffn_up"])
    x2 = matmul(act, w["w_ffn_down"], "nn", F32, "ffn_down", add=x1)

    def final_fn(xv, g, tgt):
        yv, vjp = jax.vjp(_rms, xv, g)
        err = yv - tgt
        dx, dg = vjp(err * (1.0 / D_MODEL))
        loss = 0.5 * jnp.sum(jnp.mean(err * err, axis=-1, keepdims=True), axis=0, keepdims=True)
        return dx, dx, dg, jnp.broadcast_to(loss, (1, LANES))

    dx2, dx2_b, dg_final, loss = rowwise(final_fn, [x2, g_final, target], [(D_MODEL, F32), (D_MODEL, BF16)],
                                         "final_norm_loss", accs=(D_MODEL, LANES))
    gw, parts = {}, {}
    gw["w_ffn_down"] = matmul(act, dx2_b, "tn", F32, "d_ffn_down")
    da_b, db_b = ffn_in_bwd(dx2_b, w["w_ffn_down"], a, b)
    gw["w_ffn_gate"] = matmul(da_b, h2, "tn", F32, "d_ffn_gate")
    gw["w_ffn_up"] = matmul(db_b, h2, "tn", F32, "d_ffn_up")
    ffn = ["w_ffn_down", "w_ffn_gate", "w_ffn_up"]
    swap = _reduce_start(ffn[:2], gw, shard_shapes)
    dh2, *got = matmul(da_b, w["w_ffn_gate"], "nn", F32, "d_h2_gate", comm=swap)
    ffn_exchange = [_reduce_chip(ffn[:2], swap, got, core)]
    swap = _reduce_start(ffn[2:], gw, shard_shapes)
    dh2, *got = matmul(db_b, w["w_ffn_up"], "nn", F32, "d_h2_up", add=dh2, comm=swap)
    ffn_exchange = join_comms(ffn_exchange + [_reduce_chip(ffn[2:], swap, got, core)])

    def norm_bwd(xv, g, dh, skip):
        _, vjp = jax.vjp(_rms, xv, g)
        dx, dg = vjp(dh)
        dx = dx + skip
        return dx, dx, dg

    dx1, dx1_b, dg_ffn = rowwise(norm_bwd, [x1, g_ffn, dh2, dx2], [(D_MODEL, F32), (D_MODEL, BF16)],
                                 "norm_ffn_bwd", accs=(D_MODEL,))
    gw["w_out"] = matmul(merged, dx1_b, "tn", F32, "d_out")
    dmerged = matmul(dx1_b, w["w_out"], "nt", F32, "d_merged")

    def merge_bwd(g0, g1, ad, za, zb, dm):
        _, vjp = jax.vjp(_merge_fn, g0, g1, ad, za, zb)
        dg0, dg1, dad, dza, dzb = vjp(dm)
        return jnp.concatenate([dg0, dg1], axis=1), dad, jnp.concatenate([dza, dzb], axis=1)

    dgl_b, dattn_d_b, dz_b = rowwise(merge_bwd, gate_ins + [dmerged],
                                     [(GATE_W, BF16), (D_MODEL, BF16), (GATE_W, BF16)], "merge_bwd")
    gw["w_attn_out"] = matmul(attn_b, dattn_d_b, "tn", F32, "d_attn_out")
    dattn = seqs(matmul(dattn_d_b, w["w_attn_out"], "nt", F32, "d_attn"))
    gw["w_glu"] = matmul(yg2, dz_b, "tn", F32, "d_glu")
    dyg = seqs(matmul(dz_b, w["w_glu"], "nt", F32, "d_yg"))
    mixer = ["w_out", "w_attn_out", "w_glu"]
    swap = _reduce_start(mixer, gw, shard_shapes)
    du_b, dcb_re, dcb_im, dbb_re, dbb_im, dd, da_re8, da_im8, *rest = ssm_bwd(
        dyg, y, u, xr, xi, *ab, *bb, *cb, d_skip, comm=join_comms([ffn_exchange, swap]))
    for n, p in zip(ffn, rest[:len(ffn)]):
        parts[n] = p
    mixer_exchange = _reduce_chip(mixer, swap, rest[len(ffn):], core)
    du_b = toks(du_b)
    g_ab_re = jnp.sum(da_re8, axis=0).reshape(32, 64)
    g_ab_im = jnp.sum(da_im8, axis=0).reshape(32, 64)
    d_lr, d_li, d_ldt, d_br_t, d_bi_t = ssm_prep_bwd(
        small["ssm_a_re"], small["ssm_a_im"], log_dt, br_t, bi_t,
        g_ab_re, g_ab_im, _block_diag_t(dbb_re), _block_diag_t(dbb_im))
    gs = {
        "ssm_a_re": d_lr, "ssm_a_im": d_li, "ssm_log_dt": d_ldt.reshape(1, 32),
        "ssm_b_re": d_br_t.transpose(1, 2, 0), "ssm_b_im": d_bi_t.transpose(1, 2, 0),
        "ssm_c_re": _block_diag_t(dcb_re).transpose(1, 0, 2), "ssm_c_im": _block_diag_t(dcb_im).transpose(1, 0, 2),
        "ssm_d": dd.reshape(32, 16),
    }
    ssm_gather = small_comm(_pack_small(SSM_SMALL, gs, SSM_ROWS))
    dqkv_b, *rest = attn_bwd(qkv, tables, dattn, attn, lse, comm=join_comms([mixer_exchange, ssm_gather]))
    for n, p in zip(mixer, rest):
        parts[n] = p
    ssm_shares = rest[len(mixer)]
    dqkv_b = toks(dqkv_b)
    d_qkv = matmul(dqkv_b, h, "tn", F32, "d_w_qkv")
    d_u = matmul(du_b, h, "tn", F32, "d_w_u")
    d_gate = matmul(dgl_b, h, "tn", F32, "d_w_gate")
    gw["w_in"] = jnp.concatenate([_qkv_order(d_qkv, back=True), d_u, d_gate], axis=0)
    swap = _reduce_start(["w_in"], gw, shard_shapes)
    dh, *got = matmul(dqkv_b, w_qkv, "nn", F32, "d_h_qkv", comm=swap)
    w_in_exchange = _reduce_chip(["w_in"], swap, got, core)
    grad_x, dg_mix, parts["w_in"] = mix_in_bwd([du_b, dgl_b], [w_u, w_gate], dh, x, g_mix, dx1, comm=w_in_exchange)
    gs_norm = {"norm_mix_g": dg_mix, "norm_ffn_g": dg_ffn, "norm_final_g": dg_final}
    return loss, grad_x, parts, ssm_shares, gs_norm


ANY = pl.BlockSpec(memory_space=pl.ANY)
BIG = ("w_in", "w_glu", "w_attn_out", "w_out", "w_ffn_gate", "w_ffn_up", "w_ffn_down")
TRANSPOSED = ("w_in", "w_ffn_gate", "w_ffn_up")
ROW_SHARDED = TRANSPOSED + ("w_out", "w_ffn_down")
SMALL = ("norm_mix_g", "ssm_a_re", "ssm_a_im", "ssm_log_dt", "ssm_b_re", "ssm_b_im", "ssm_c_re", "ssm_c_im",
         "ssm_d", "norm_ffn_g", "norm_final_g")
WEIGHTS = ("norm_mix_g", "w_in", "ssm_a_re", "ssm_a_im", "ssm_log_dt", "ssm_b_re", "ssm_b_im", "ssm_c_re",
           "ssm_c_im", "ssm_d", "w_glu", "w_attn_out", "w_out", "norm_ffn_g", "w_ffn_gate", "w_ffn_up",
           "w_ffn_down", "norm_final_g")
SSM_SMALL = SMALL[1:9]
NORM_SMALL = (SMALL[0],) + SMALL[9:]
SSM_ROWS, NORM_ROWS = 1064, 32
N_BIG = len(BIG)


def _position():
    return lax.axis_index("x"), lax.axis_index("y"), lax.axis_index("c")


def _other_chips(x, y):
    return [(1 - x, y), (x, 1 - y), (1 - x, 1 - y)]


def _remote(src, dst, send_sem, recv_sem, device):
    return pltpu.make_async_remote_copy(src_ref=src, dst_ref=dst, send_sem=send_sem, recv_sem=recv_sem,
                                        device_id=device, device_id_type=MESH)


_later = functools.partial


def _two_level_phases(copies):
    def first(*refs):
        locals_, sends, _, _, _ = copies(*refs)
        for cp in locals_ + sends:
            cp().start()

    def mid(*refs):
        _, _, arrived, passed, _ = copies(*refs)
        for got, cp in zip(arrived, passed):
            got().wait_recv()
            cp().start()

    def last(*refs):
        locals_, sends, _, passed, from_sibling = copies(*refs)
        for cp in from_sibling:
            cp().wait_recv()
        for cp in sends + passed:
            cp().wait_send()
        for cp in locals_:
            cp().wait()

    return first, mid, last


def _half(ref, chip, which):
    rows = ref.shape[1] // 2
    return ref.at[chip, pl.ds(which * rows, rows), :]


class Comm:
    def __init__(self, ins, out_shapes, sems, first, mid, last):
        self.ins, self.out_shapes, self.sems = list(ins), list(out_shapes), list(sems)
        self.first, self.mid, self.last = first, mid, last


def join_comms(comms):
    def cut(refs_by_kind):
        offs, parts = [0, 0, 0], []
        for cm in comms:
            sizes = (len(cm.ins), len(cm.out_shapes), len(cm.sems))
            parts.append(tuple(refs_by_kind[k][offs[k]:offs[k] + sizes[k]] for k in range(3)))
            offs = [o + s for o, s in zip(offs, sizes)]
        return parts

    def phase(which):
        def run(ins, outs, sems):
            for cm, part in zip(comms, cut((ins, outs, sems))):
                fn = getattr(cm, which)
                if fn is not None:
                    fn(*part)
        return run

    return Comm(sum((cm.ins for cm in comms), []), sum((cm.out_shapes for cm in comms), []),
                sum((cm.sems for cm in comms), []), phase("first"), phase("mid"), phase("last"))


def _comm_operands(comm):
    if comm is None:
        return [], [], []
    return comm.ins, comm.out_shapes, comm.sems


def _comm_begin(comm, refs, step, n_steps):
    if comm is None:
        return
    pl.when(step == 0)(lambda: comm.first(*refs))
    if comm.mid is not None:
        pl.when(step == (n_steps * 3) // 4)(lambda: comm.mid(*refs))


def _comm_end(comm, refs, step, n_steps):
    if comm is not None:
        pl.when(step == n_steps - 1)(lambda: comm.last(*refs))


def _comm_refs(comm, refs, n_in, n_out):
    if comm is None:
        return list(refs), None
    ci, co, cs = len(comm.ins), len(comm.out_shapes), len(comm.sems)
    o0 = n_in + ci
    s0 = o0 + n_out + co
    host = list(refs[:n_in]) + list(refs[o0:o0 + n_out]) + list(refs[s0:len(refs) - cs])
    return host, (list(refs[n_in:o0]), list(refs[o0 + n_out:s0]), list(refs[len(refs) - cs:]))


def run_comm(comm, name):
    n_in, n_out = len(comm.ins), len(comm.out_shapes)

    def body(*refs):
        parts = (list(refs[:n_in]), list(refs[n_in:n_in + n_out]), list(refs[n_in + n_out:]))
        comm.first(*parts)
        if comm.mid is not None:
            comm.mid(*parts)
        comm.last(*parts)

    return pl.pallas_call(body, name=name, in_specs=[ANY] * n_in, out_specs=[ANY] * n_out,
                          out_shape=comm.out_shapes, scratch_shapes=comm.sems)(*comm.ins)


def hosted_call(work, comm, name, grid, in_specs, out_specs, out_shape, scratch_shapes, args, semantics):
    c_ins, c_outs, c_sems = _comm_operands(comm)
    n_steps = math.prod(grid)

    def body(*refs):
        host, c_refs = _comm_refs(comm, refs, len(in_specs), len(out_specs))
        step = 0
        for axis, size in enumerate(grid):
            step = step * size + pl.program_id(axis)
        _comm_begin(comm, c_refs, step, n_steps)
        work(*host)
        _comm_end(comm, c_refs, step, n_steps)

    return pl.pallas_call(
        body, name=name, grid=grid, in_specs=list(in_specs) + [ANY] * len(c_ins),
        out_specs=list(out_specs) + [ANY] * len(c_outs), out_shape=list(out_shape) + c_outs,
        scratch_shapes=list(scratch_shapes) + c_sems,
        compiler_params=_params(semantics if comm is None else ("arbitrary",) * len(grid)),
    )(*args, *c_ins)


def gather_comm(shards):
    n = len(shards)

    def copies(srcs, outs, sems):
        send_sems, recv_sems, local_sems = sems
        x, y, c = _position()
        me = 2 * x + y
        sibling = (x, y, 1 - c)
        chips = _other_chips(x, y)
        locals_ = [_later(pltpu.make_async_copy, s, o.at[me], local_sems.at[i])
                   for i, (s, o) in enumerate(zip(srcs, outs))]
        sends, arrived, passed, from_sibling = [], [], [], []
        for j, (px, py) in enumerate(chips):
            for i, (s, o) in enumerate(zip(srcs, outs)):
                rows = s.shape[0] // 2
                sends.append(_later(_remote, s.at[pl.ds(c * rows, rows), :], _half(o, me, c), send_sems.at[i, j],
                                    recv_sems.at[i, j], (px, py, c)))
                got = _half(o, 2 * px + py, c)
                arrived.append(_later(_remote, got, got, send_sems.at[i, j], recv_sems.at[i, j], (px, py, c)))
                passed.append(_later(_remote, got, got, send_sems.at[i, 3 + j], recv_sems.at[i, 3 + j], sibling))
                other = _half(o, 2 * px + py, 1 - c)
                from_sibling.append(_later(_remote, other, other, send_sems.at[i, 3 + j], recv_sems.at[i, 3 + j],
                                           sibling))
        return locals_, sends, arrived, passed, from_sibling

    return Comm(shards, [jax.ShapeDtypeStruct((N_CHIPS,) + s.shape, s.dtype) for s in shards],
                [pltpu.SemaphoreType.DMA((n, 6)), pltpu.SemaphoreType.DMA((n, 6)), pltpu.SemaphoreType.DMA((n,))],
                *_two_level_phases(copies))


def swap_comm(grads):
    n = len(grads)

    def copies(srcs, gots, sems):
        send_sems, recv_sems = sems
        x, y, c = _position()
        out = []
        for i, (s, o) in enumerate(zip(srcs, gots)):
            rows = s.shape[1] // 2
            out.append(_remote(s.at[:, pl.ds((1 - c) * rows, rows), :], o, send_sems.at[i], recv_sems.at[i],
                               (x, y, 1 - c)))
        return out

    def first(srcs, gots, sems):
        for cp in copies(srcs, gots, sems):
            cp.start()

    def last(srcs, gots, sems):
        for cp in copies(srcs, gots, sems):
            cp.wait()

    return Comm(grads, [jax.ShapeDtypeStruct((N_CHIPS, g.shape[1] // 2, g.shape[2]), g.dtype) for g in grads],
                [pltpu.SemaphoreType.DMA((n,)), pltpu.SemaphoreType.DMA((n,))], first, None, last)


def add_halves(name, g, got, core):
    _, half, cols = got.shape
    mine = pl.BlockSpec((None, half, cols), lambda k, c_ref: (k, c_ref[0], 0))
    other = pl.BlockSpec((None, half, cols), lambda k, c_ref: (k, 0, 0))

    def body(c_ref, g_ref, got_ref, o_ref):
        o_ref[...] = (g_ref[...] + got_ref[...]).astype(BF16)

    return pl.pallas_call(
        body, name="add_halves_" + name,
        grid_spec=pltpu.PrefetchScalarGridSpec(num_scalar_prefetch=1, grid=(N_CHIPS,), in_specs=[mine, other],
                                               out_specs=other),
        out_shape=jax.ShapeDtypeStruct(got.shape, BF16),
        compiler_params=_params(("parallel",)),
    )(core, g, got)


def exchange_comm(parts):
    n = len(parts)

    def copies(srcs, outs, sems):
        send_sems, recv_sems, local_sems = sems
        x, y, c = _position()
        me = 2 * x + y
        sibling = (x, y, 1 - c)
        chips = _other_chips(x, y)
        locals_, sends, arrived, passed, from_sibling = [], [], [], [], []
        for i, (s, o) in enumerate(zip(srcs, outs)):
            locals_.append(_later(pltpu.make_async_copy, s.at[me], _half(o, me, c), local_sems.at[i]))
            sends.append(_later(_remote, s.at[me], _half(o, me, c), send_sems.at[i, 3], recv_sems.at[i, 3], sibling))
            other = _half(o, me, 1 - c)
            from_sibling.append(_later(_remote, other, other, send_sems.at[i, 3], recv_sems.at[i, 3], sibling))
        for j, (px, py) in enumerate(chips):
            for i, (s, o) in enumerate(zip(srcs, outs)):
                sends.append(_later(_remote, s.at[2 * px + py], _half(o, me, c), send_sems.at[i, j],
                                    recv_sems.at[i, j], (px, py, c)))
                got = _half(o, 2 * px + py, c)
                arrived.append(_later(_remote, got, got, send_sems.at[i, j], recv_sems.at[i, j], (px, py, c)))
                passed.append(_later(_remote, got, got, send_sems.at[i, 4 + j], recv_sems.at[i, 4 + j], sibling))
                other = _half(o, 2 * px + py, 1 - c)
                from_sibling.append(_later(_remote, other, other, send_sems.at[i, 4 + j], recv_sems.at[i, 4 + j],
                                           sibling))
        return locals_, sends, arrived, passed, from_sibling

    return Comm(parts, [jax.ShapeDtypeStruct((N_CHIPS, 2 * p.shape[1], p.shape[2]), p.dtype) for p in parts],
                [pltpu.SemaphoreType.DMA((n, 7)), pltpu.SemaphoreType.DMA((n, 7)), pltpu.SemaphoreType.DMA((n,))],
                *_two_level_phases(copies))


def small_comm(pack):
    def copies(srcs, outs, sems):
        (src_ref,), (out_ref,), (send_sems, recv_sems, local_sem) = srcs, outs, sems
        x, y, c = _position()
        me = 4 * x + 2 * y + c
        flips = [(fx, fy, fc) for fx in (0, 1) for fy in (0, 1) for fc in (0, 1)][1:]
        peers = [(1 - x if fx else x, 1 - y if fy else y, 1 - c if fc else c) for fx, fy, fc in flips]
        local = _later(pltpu.make_async_copy, src_ref, out_ref.at[me], local_sem)
        sends = [_later(_remote, src_ref, out_ref.at[me], send_sems.at[j], recv_sems.at[j], peer)
                 for j, peer in enumerate(peers)]
        arrived = []
        for j, (px, py, pc) in enumerate(peers):
            got = out_ref.at[4 * px + 2 * py + pc]
            arrived.append(_later(_remote, got, got, send_sems.at[j], recv_sems.at[j], (px, py, pc)))
        return local, sends, arrived

    def first(*refs):
        local, sends, _ = copies(*refs)
        for cp in [local] + sends:
            cp().start()

    def last(*refs):
        local, sends, arrived = copies(*refs)
        for cp in arrived:
            cp().wait_recv()
        for cp in sends:
            cp().wait_send()
        local().wait()

    return Comm([pack], [jax.ShapeDtypeStruct((N_DEV,) + pack.shape, pack.dtype)],
                [pltpu.SemaphoreType.DMA((7,)), pltpu.SemaphoreType.DMA((7,)), pltpu.SemaphoreType.DMA],
                first, None, last)


def _adam_fn(w, g, m, v):
    m = ADAM_B1 * m + (1.0 - ADAM_B1) * g
    v = ADAM_B2 * v + (1.0 - ADAM_B2) * jnp.square(g)
    m_hat = m / (1.0 - ADAM_B1 ** ADAM_STEP)
    v_hat = v / (1.0 - ADAM_B2 ** ADAM_STEP)
    return -ADAM_LR * (m_hat / (jnp.sqrt(v_hat) + ADAM_EPS) + ADAM_WD * w), m, v


def adam_big(name, parts, w, m, v):
    rows, cols = w.shape
    tm = _pick(rows, 384, 16)

    def fn(p0, p1, p2, p3, wv, mv, vv):
        g = ((p0.astype(F32) + p1.astype(F32)) + p2.astype(F32)) + p3.astype(F32)
        return (g,) + _adam_fn(wv, g, mv, vv)

    return rowwise(fn, [parts, w, m, v], [(cols, F32)] * 4, "adam_" + name, tm=tm, rows=rows)


def adam_small(name, gathered, w, m, v):
    def body(g_ref, w_ref, m_ref, v_ref, go_ref, d_ref, mo_ref, vo_ref):
        g = g_ref[0]
        for k in range(1, N_DEV):
            g = g + g_ref[k]
        go_ref[...] = g
        d_ref[...], mo_ref[...], vo_ref[...] = _adam_fn(w_ref[...], g, m_ref[...], v_ref[...])

    return pl.pallas_call(body, name=name, out_shape=[jax.ShapeDtypeStruct(w.shape, F32)] * 4,
                          compiler_params=_params())(gathered, w, m, v)


def _pack_small(names, vals, rows, last=None):
    flat = [vals[n].reshape(-1) for n in names]
    if last is not None:
        flat.append(last.reshape(-1))
    flat = jnp.concatenate(flat)
    return jnp.pad(flat, (0, rows * LANES - flat.shape[0])).reshape(rows, LANES)


def _unpack_small(names, pack, shapes):
    flat, out, off = pack.reshape(-1), {}, 0
    for n in names:
        size = math.prod(shapes[n])
        out[n] = flat[off:off + size].reshape(shapes[n])
        off += size
    return out, flat[off]


def _to_slots(name, g, shard_shape):
    rows, cols = shard_shape
    if name in ROW_SHARDED:
        return g.reshape(N_CHIPS, rows, cols)
    return g.reshape(rows, N_CHIPS, cols).transpose(1, 0, 2)


def _from_slots(name, s):
    _, rows, cols = s.shape
    if name in ROW_SHARDED:
        return s.reshape(N_CHIPS * rows, cols)
    return s.transpose(1, 0, 2).reshape(rows, N_CHIPS * cols)


def kernel(x, norm_mix_g, w_in, ssm_a_re, ssm_a_im, ssm_log_dt, ssm_b_re, ssm_b_im, ssm_c_re, ssm_c_im, ssm_d, w_glu, w_attn_out, w_out, norm_ffn_g, w_ffn_gate, w_ffn_up, w_ffn_down, norm_final_g, loss_target, m_norm_mix_g, m_w_in, m_ssm_a_re, m_ssm_a_im, m_ssm_log_dt, m_ssm_b_re, m_ssm_b_im, m_ssm_c_re, m_ssm_c_im, m_ssm_d, m_w_glu, m_w_attn_out, m_w_out, m_norm_ffn_g, m_w_ffn_gate, m_w_ffn_up, m_w_ffn_down, m_norm_final_g, v_norm_mix_g, v_w_in, v_ssm_a_re, v_ssm_a_im, v_ssm_log_dt, v_ssm_b_re, v_ssm_b_im, v_ssm_c_re, v_ssm_c_im, v_ssm_d, v_w_glu, v_w_attn_out, v_w_out, v_norm_ffn_g, v_w_ffn_gate, v_w_ffn_up, v_w_ffn_down, v_norm_final_g):
    given = dict(locals())
    def local(name, prefix=""):
        t = given[prefix + name][0]
        return t.T if name in TRANSPOSED else t

    shard = {n: local(n) for n in BIG}
    shapes = {n: given[n].shape for n in WEIGHTS}

    small = {n: given[n] for n in SMALL}
    small_2d = dict(small)
    for n in ("ssm_a_re", "ssm_a_im", "ssm_b_re", "ssm_b_im", "ssm_c_re", "ssm_c_im", "ssm_d"):
        small_2d[n] = small[n][0]
    small_2d["norm_final_g"] = norm_final_g.reshape(1, D_MODEL)

    core = lax.axis_index("c").astype(jnp.int32).reshape(1)
    loss, grad_x, parts, ssm_shares, gs_norm = local_step(
        x.reshape(TOKENS, D_MODEL), loss_target.reshape(TOKENS, D_MODEL),
        {n: shard[n].astype(BF16) for n in BIG}, small_2d, core)

    (norm_shares,) = run_comm(small_comm(_pack_small(NORM_SMALL, gs_norm, NORM_ROWS, last=loss)), "gather_norm_grads")
    small_out = [{} for _ in range(4)]
    for names, rows, shares in ((SSM_SMALL, SSM_ROWS, ssm_shares), (NORM_SMALL, NORM_ROWS, norm_shares)):
        packs = [_pack_small(names, {n: given[p + n] for n in names}, rows) for p in ("", "m_", "v_")]
        for kind, t in enumerate(adam_small("adam_" + names[0], shares, *packs)):
            vals, after = _unpack_small(names, t, shapes)
            small_out[kind].update(vals)
            if kind == 0:
                total_loss = after

    big_out = {}
    for n in BIG:
        res = adam_big(n, parts[n], shard[n], local(n, "m_"), local(n, "v_"))
        big_out[n] = [(t.T if n in TRANSPOSED else t)[None] for t in res]

    outs = [total_loss, grad_x.reshape(LOCAL_BATCH, SEQ, D_MODEL)]
    for kind in range(4):
        for n in WEIGHTS:
            outs.append(big_out[n][kind] if n in BIG else small_out[kind][n])
    return tuple(outs)
```

```python
import functools
import math

import jax
import jax.numpy as jnp
from jax import lax
from jax.experimental import pallas as pl
from jax.experimental.pallas import tpu as pltpu

F32 = jnp.float32
BF16 = jnp.bfloat16
MESH = pl.DeviceIdType.MESH

D_MODEL = 1024
SEQ = 2048
LOCAL_BATCH = 2
TOKENS = LOCAL_BATCH * SEQ
HEAD_DIM = 64
HEADS_PER_GROUP = 4
GROUP_W = HEADS_PER_GROUP * HEAD_DIM
N_GROUPS = 3
DILATIONS = (1, 4, 16)
ATTN_BLOCK = 128
ROPE_DIM = 16
ROPE_THETA = 500000.0
QKV_W = 3 * N_GROUPS * GROUP_W
SSM_W = 512
SSM_STATE_W = 2048
SSM_LANE_BLOCKS = 4
GATE_W = 2 * D_MODEL
D_FF = 2816
RMS_EPS = 1e-6
NEG_INF = -1e30
ADAM_LR, ADAM_B1, ADAM_B2, ADAM_EPS, ADAM_WD, ADAM_STEP = 0.001, 0.9, 0.999, 1e-08, 0.01, 10
N_CHIPS = 4
N_DEV = 8

VMEM_LIMIT = 56 * 1024 * 1024
LANES = 128


def _params(sem=None):
    return pltpu.CompilerParams(dimension_semantics=sem, vmem_limit_bytes=VMEM_LIMIT)


def _pick(n, cap, align=LANES):
    best = None
    for d in range(align, min(n, cap) + 1, align):
        if n % d == 0:
            best = d
    return n if best is None or n <= cap else best


_DIMS = {"nn": (((1,), (0,)), ((), ())), "nt": (((1,), (1,)), ((), ())), "tn": (((0,), (0,)), ((), ()))}


def _dot(a, b, mode):
    return lax.dot_general(a, b, _DIMS[mode], preferred_element_type=F32)


def matmul(a, b, mode, out_dtype, name, add=None, comm=None):
    if mode == "nn":
        (m, k), n = a.shape, b.shape[1]
    elif mode == "nt":
        (m, k), n = a.shape, b.shape[0]
    else:
        (k, m), n = a.shape, b.shape[1]
    tn = _pick(n, 1408)
    tk = _pick(k, 2816) if mode != "tn" else _pick(k, 1024)
    tm = _pick(m, 1408)
    out_bytes = jnp.dtype(out_dtype).itemsize

    def need(tm_):
        return 2 * 2 * (tm_ * tk + tk * tn) + tm_ * tn * (4 + 2 * out_bytes + (8 if add is not None else 0))

    while need(tm) > 40 * 1024 * 1024 and tm % 256 == 0:
        tm //= 2
    nk = k // tk
    a_spec = {"nn": pl.BlockSpec((tm, tk), lambda i, j, kk: (i, kk)),
              "nt": pl.BlockSpec((tm, tk), lambda i, j, kk: (i, kk)),
              "tn": pl.BlockSpec((tk, tm), lambda i, j, kk: (kk, i))}[mode]
    b_spec = {"nn": pl.BlockSpec((tk, tn), lambda i, j, kk: (kk, j)),
              "nt": pl.BlockSpec((tn, tk), lambda i, j, kk: (j, kk)),
              "tn": pl.BlockSpec((tk, tn), lambda i, j, kk: (kk, j))}[mode]
    o_spec = pl.BlockSpec((tm, tn), lambda i, j, kk: (i, j))

    def body(a_ref, b_ref, *rest):
        if add is not None:
            add_ref, o_ref, acc_ref = rest
        else:
            o_ref, acc_ref = rest
        part = _dot(a_ref[...], b_ref[...], mode)
        if nk == 1:
            res = part if add is None else part + add_ref[...]
            o_ref[...] = res.astype(out_dtype)
            return
        kk = pl.program_id(2)

        @pl.when(kk == 0)
        def _():
            acc_ref[...] = part

        @pl.when(kk > 0)
        def _():
            acc_ref[...] += part

        @pl.when(kk == nk - 1)
        def _():
            res = acc_ref[...] if add is None else acc_ref[...] + add_ref[...]
            o_ref[...] = res.astype(out_dtype)

    in_specs = [a_spec, b_spec] + ([o_spec] if add is not None else [])
    args = (a, b) + ((add,) if add is not None else ())
    res = hosted_call(
        body, comm, name, (m // tm, n // tn, nk), in_specs, [o_spec], [jax.ShapeDtypeStruct((m, n), out_dtype)],
        [pltpu.VMEM((tm, tn) if nk > 1 else (8, LANES), F32)], args, ("parallel", "parallel", "arbitrary"))
    return res[0] if comm is None else res


def matmul_rows(a, b, name, fn, extra, outs, accs=(), add=None, comm=None, tm=512):
    (m, k), n = a.shape, b.shape[1]
    n_fixed = 2 + (add is not None)
    row_spec = lambda cols: pl.BlockSpec((tm, cols), lambda i: (i, 0))
    in_specs = [row_spec(k), pl.BlockSpec((k, n), lambda i: (0, 0))] + ([row_spec(n)] if add is not None else [])
    in_specs += [pl.BlockSpec(e.shape, lambda i: (0, 0)) if e.shape[0] == 1 else row_spec(e.shape[1]) for e in extra]
    out_specs = [row_spec(c) for c, _ in outs] + [pl.BlockSpec((1, c), lambda i: (0, 0)) for c in accs]
    out_shape = [jax.ShapeDtypeStruct((m, c), dt) for c, dt in outs] + [jax.ShapeDtypeStruct((1, c), F32) for c in accs]

    def body(*refs):
        rows = _dot(refs[0][...], refs[1][...], "nn")
        if add is not None:
            rows = rows + refs[2][...]
        n_in = n_fixed + len(extra)
        res = fn(rows, *[r[...] for r in refs[n_fixed:n_in]])
        for r, v in zip(refs[n_in:n_in + len(outs)], res[:len(outs)]):
            r[...] = v.astype(r.dtype)
        first = pl.program_id(0) == 0
        for r, v in zip(refs[n_in + len(outs):], res[len(outs):]):
            @pl.when(first)
            def _(r=r, v=v):
                r[...] = v

            @pl.when(jnp.logical_not(first))
            def _(r=r, v=v):
                r[...] += v

    args = (a, b) + ((add,) if add is not None else ()) + tuple(extra)
    return hosted_call(body, comm, name, (m // tm,), in_specs, out_specs, out_shape, [], args, ("arbitrary",))


FFN_TM, FFN_TN = 512, 1408


def ffn_in(h2, wg_t, wu_t):
    def body(h_ref, wg_ref, wu_ref, a_ref, b_ref, act_ref):
        hv = h_ref[...]
        a, b = _dot(hv, wg_ref[...], "nt"), _dot(hv, wu_ref[...], "nt")
        a_ref[...] = a.astype(BF16)
        b_ref[...] = b.astype(BF16)
        act_ref[...] = _swiglu_fn(a, b).astype(BF16)

    rows = pl.BlockSpec((FFN_TM, D_MODEL), lambda i, j: (i, 0))
    wts = pl.BlockSpec((FFN_TN, D_MODEL), lambda i, j: (j, 0))
    out = pl.BlockSpec((FFN_TM, FFN_TN), lambda i, j: (i, j))
    return pl.pallas_call(
        body, name="ffn_in", grid=(TOKENS // FFN_TM, D_FF // FFN_TN), in_specs=[rows, wts, wts],
        out_specs=[out] * 3, out_shape=[jax.ShapeDtypeStruct((TOKENS, D_FF), BF16)] * 3,
        compiler_params=_params(("parallel", "parallel")),
    )(h2, wg_t, wu_t)


def ffn_in_bwd(dx2_b, wd, a, b):
    def body(dx_ref, wd_ref, a_ref, b_ref, da_ref, db_ref):
        dact = _dot(dx_ref[...], wd_ref[...], "nt")
        _, vjp = jax.vjp(_swiglu_fn, a_ref[...].astype(F32), b_ref[...].astype(F32))
        da, db = vjp(dact)
        da_ref[...] = da.astype(BF16)
        db_ref[...] = db.astype(BF16)

    rows = pl.BlockSpec((FFN_TM, D_MODEL), lambda i, j: (i, 0))
    wts = pl.BlockSpec((FFN_TN, D_MODEL), lambda i, j: (j, 0))
    out = pl.BlockSpec((FFN_TM, FFN_TN), lambda i, j: (i, j))
    return pl.pallas_call(
        body, name="ffn_in_bwd", grid=(TOKENS // FFN_TM, D_FF // FFN_TN), in_specs=[rows, wts, out, out],
        out_specs=[out] * 2, out_shape=[jax.ShapeDtypeStruct((TOKENS, D_FF), BF16)] * 2,
        compiler_params=_params(("parallel", "parallel")),
    )(dx2_b, wd, a, b)


def mix_in_bwd(grads, weights, partial, x, g, skip, comm=None):
    n = len(grads)
    tm = 512

    def body(*refs):
        a_refs, b_refs = refs[:n], refs[n:2 * n]
        part_ref, x_ref, g_ref, skip_ref, gx_ref, dg_ref = refs[2 * n:]
        dh = part_ref[...]
        for a_ref, b_ref in zip(a_refs, b_refs):
            dh = dh + _dot(a_ref[...], b_ref[...], "nn")
        _, vjp = jax.vjp(_rms, x_ref[...], g_ref[...])
        dx, dg = vjp(dh)
        gx_ref[...] = dx + skip_ref[...]
        first = pl.program_id(0) == 0

        @pl.when(first)
        def _():
            dg_ref[...] = dg

        @pl.when(jnp.logical_not(first))
        def _():
            dg_ref[...] += dg

    rows = pl.BlockSpec((tm, D_MODEL), lambda i: (i, 0))
    gain = pl.BlockSpec((1, D_MODEL), lambda i: (0, 0))
    in_specs = [pl.BlockSpec((tm, a.shape[1]), lambda i: (i, 0)) for a in grads]
    in_specs += [pl.BlockSpec(b.shape, lambda i: (0, 0)) for b in weights]
    return hosted_call(
        body, comm, "mix_in_bwd", (TOKENS // tm,), in_specs + [rows, rows, gain, rows], [rows, gain],
        [jax.ShapeDtypeStruct((TOKENS, D_MODEL), F32), jax.ShapeDtypeStruct((1, D_MODEL), F32)], [],
        (*grads, *weights, partial, x, g, skip), ("arbitrary",))


def rowwise(fn, ins, outs, name, accs=(), tm=256, rows=TOKENS, comm=None):
    in_specs, args = [], []
    for item in ins:
        arr, width, blk = item if isinstance(item, tuple) else (item, None, 0)
        if arr.ndim == 3:
            for k in range(arr.shape[0]):
                in_specs.append(pl.BlockSpec((None, tm, arr.shape[2]), functools.partial(lambda i, k_: (k_, i, 0), k_=k)))
                args.append(arr)
            continue
        if arr.shape[0] == 1:
            in_specs.append(pl.BlockSpec(arr.shape, lambda i: (0, 0)))
        elif width is None:
            in_specs.append(pl.BlockSpec((tm, arr.shape[1]), lambda i: (i, 0)))
        else:
            in_specs.append(pl.BlockSpec((tm, width), functools.partial(lambda i, blk_: (i, blk_), blk_=blk)))
        args.append(arr)
    out_specs = [pl.BlockSpec((tm, c), lambda i: (i, 0)) for c, _ in outs]
    out_specs += [pl.BlockSpec((1, c), lambda i: (0, 0)) for c in accs]
    out_shape = [jax.ShapeDtypeStruct((rows, c), dt) for c, dt in outs]
    out_shape += [jax.ShapeDtypeStruct((1, c), F32) for c in accs]
    n_in, n_out = len(args), len(outs)
    c_ins, c_outs, c_sems = _comm_operands(comm)

    def body(*refs):
        refs, c_refs = _comm_refs(comm, refs, n_in, n_out + len(accs))
        step = pl.program_id(0)
        _comm_begin(comm, c_refs, step, rows // tm)
        res = fn(*[r[...] for r in refs[:n_in]])
        for r, v in zip(refs[n_in:n_in + n_out], res[:n_out]):
            r[...] = v.astype(r.dtype)
        first = step == 0
        for r, v in zip(refs[n_in + n_out:], res[n_out:]):
            @pl.when(first)
            def _(r=r, v=v):
                r[...] = v

            @pl.when(jnp.logical_not(first))
            def _(r=r, v=v):
                r[...] += v
        _comm_end(comm, c_refs, step, rows // tm)

    return pl.pallas_call(
        body, name=name, grid=(rows // tm,), in_specs=in_specs + [ANY] * len(c_ins),
        out_specs=out_specs + [ANY] * len(c_outs), out_shape=out_shape + c_outs, scratch_shapes=c_sems,
        compiler_params=_params(("arbitrary",)),
    )(*args, *c_ins)


def _rms(x, g):
    return x * lax.rsqrt(jnp.mean(x * x, axis=-1, keepdims=True) + RMS_EPS) * g


def _colsum(v):
    return jnp.sum(v, axis=0, keepdims=True)


PAIR_W = 2 * HEAD_DIM
N_PAIRS = HEADS_PER_GROUP // 2


def _qkv_order(w_t, back=False):
    dims = (N_PAIRS, N_GROUPS, 3) if back else (3, N_GROUPS, N_PAIRS)
    return w_t.reshape(dims + (PAIR_W, w_t.shape[1])).transpose(2, 1, 0, 3, 4).reshape(QKV_W, w_t.shape[1])


def _rope_tables():
    half = ROPE_DIM // 2
    inv = jnp.power(jnp.float32(ROPE_THETA), -jnp.arange(half, dtype=F32) * 2.0 / ROPE_DIM)
    ang = jnp.arange(SEQ, dtype=F32)[:, None] * inv[None, :]
    cos, sin = jnp.cos(ang), jnp.sin(ang)
    zeros = jnp.zeros((SEQ, HEAD_DIM - ROPE_DIM), F32)
    zh = jnp.zeros((SEQ, half), F32)
    c = jnp.concatenate([cos, cos, zeros + 1.0], axis=1)
    sa = jnp.concatenate([-sin, zh, zeros], axis=1)
    sb = jnp.concatenate([zh, sin, zeros], axis=1)
    return [jnp.tile(t, (1, 2)) for t in (c, sa, sb)]


def _rope_fwd(x, c, sa, sb):
    return x * c + pltpu.roll(x, PAIR_W - 8, 1) * sa + pltpu.roll(x, 8, 1) * sb


def _rope_bwd(dy, c, sa, sb):
    return dy * c + pltpu.roll(dy * sb, PAIR_W - 8, 1) + pltpu.roll(dy * sa, 8, 1)


def _band_masks():
    row = lax.broadcasted_iota(jnp.int32, (ATTN_BLOCK, ATTN_BLOCK), 0)
    col = lax.broadcasted_iota(jnp.int32, (ATTN_BLOCK, ATTN_BLOCK), 1)
    return col <= row, col >= row


def _stack_rows(t):
    return jnp.concatenate([t, t], axis=0)


def _stack_heads(t, first_head):
    return jnp.concatenate([jnp.where(first_head, t, 0), jnp.where(first_head, 0, t)], axis=0)


def _per_head(fn):
    return jnp.concatenate([fn(slice(h * HEAD_DIM, (h + 1) * HEAD_DIM)) for h in range(2)], axis=1)


def _slab_spec(kind):
    return pl.BlockSpec((None, SEQ, PAIR_W), lambda b, p, g: (b, 0, p * 3 * N_GROUPS + g * 3 + kind))


_TABLE_SPEC = pl.BlockSpec((SEQ, PAIR_W), lambda b, p, g: (0, 0))
_PAIR_SPEC = pl.BlockSpec((None, SEQ, PAIR_W), lambda b, p, g: (b, 0, p))


def _block_rows(dil, r, n):
    return pl.ds(n * (ATTN_BLOCK * dil) + r, ATTN_BLOCK, stride=dil)


def attn_fwd(qkv, tables, comm=None):
    scale = HEAD_DIM ** -0.5

    def body(q_ref, k_ref, v_ref, c_ref, sa_ref, sb_ref, attn_b_ref, attn_ref, lse_ref, qs, ks, o0, o1, o2, l0, l1, l2):
        g = pl.program_id(2)
        c, sa, sb = c_ref[...], sa_ref[...], sb_ref[...]
        qs[...] = _rope_fwd(q_ref[...], c, sa, sb) * scale
        ks[...] = _rope_fwd(k_ref[...], c, sa, sb)
        cur_mask, prev_mask = _band_masks()
        first_head = lax.broadcasted_iota(jnp.int32, (ATTN_BLOCK, PAIR_W), 1) < HEAD_DIM

        def run(dil, o_slab, l_slab):
            nb = SEQ // dil // ATTN_BLOCK

            def block(idx, carry):
                r, n = lax.div(idx, nb), lax.rem(idx, nb)
                cur, prev = _block_rows(dil, r, n), _block_rows(dil, r, jnp.maximum(n - 1, 0))
                q = qs[cur, :].astype(BF16)
                kc, kp = ks[cur, :].astype(BF16), ks[prev, :].astype(BF16)
                vc, vp = v_ref[cur, :].astype(BF16), v_ref[prev, :].astype(BF16)
                q2 = _stack_heads(q, first_head)
                mask = _stack_rows(jnp.concatenate([jnp.logical_and(prev_mask, n > 0), cur_mask], axis=1))
                s2 = jnp.where(mask, _dot(q2, jnp.concatenate([kp, kc], axis=0), "nt"), NEG_INF)
                m = jnp.max(s2, axis=-1, keepdims=True)
                vcat, two = jnp.concatenate([vp, vc], axis=0), _stack_rows(first_head)
                vext = jnp.concatenate([jnp.where(two, vcat, 1), jnp.where(two, 1, vcat)], axis=1)
                r2 = _dot(jnp.exp(s2 - m).astype(BF16), vext, "nn")
                r0, r1 = r2[:ATTN_BLOCK, :PAIR_W], r2[ATTN_BLOCK:, PAIR_W:]
                num = jnp.where(first_head, r0, r1)
                den = pltpu.roll(jnp.where(first_head, r1, r0), HEAD_DIM, 1)
                o_slab[cur, :] = num / den
                l_slab[cur, :] = jnp.where(first_head, m[:ATTN_BLOCK], m[ATTN_BLOCK:]) + jnp.log(den)
                return carry

            lax.fori_loop(0, SEQ // ATTN_BLOCK, block, 0, unroll=4)

        for gi, (o_slab, l_slab) in enumerate(((o0, l0), (o1, l1), (o2, l2))):
            @pl.when(g == gi)
            def _(gi=gi, o_slab=o_slab, l_slab=l_slab):
                run(DILATIONS[gi], o_slab, l_slab)

        @pl.when(g == N_GROUPS - 1)
        def _():
            a, b, cc = l0[...], l1[...], l2[...]
            m = jnp.maximum(jnp.maximum(a, b), cc)
            e0, e1, e2 = jnp.exp(a - m), jnp.exp(b - m), jnp.exp(cc - m)
            tot = e0 + e1 + e2
            attn = (e0 * o0[...] + e1 * o1[...] + e2 * o2[...]) / tot
            attn_ref[...] = attn
            attn_b_ref[...] = attn.astype(BF16)
            lse_ref[...] = m + jnp.log(tot)

    shape = (LOCAL_BATCH, SEQ, GROUP_W)
    slab = pltpu.VMEM((SEQ, PAIR_W), F32)
    return hosted_call(
        body, comm, "attn_fwd", (LOCAL_BATCH, N_PAIRS, N_GROUPS),
        [_slab_spec(0), _slab_spec(1), _slab_spec(2), _TABLE_SPEC, _TABLE_SPEC, _TABLE_SPEC], [_PAIR_SPEC] * 3,
        [jax.ShapeDtypeStruct(shape, BF16), jax.ShapeDtypeStruct(shape, F32), jax.ShapeDtypeStruct(shape, F32)],
        [slab] * 8, (qkv, qkv, qkv, *tables), ("parallel", "parallel", "arbitrary"))


def attn_bwd(qkv, tables, dattn, attn, lse, comm=None):
    scale = HEAD_DIM ** -0.5

    def body(q_ref, k_ref, v_ref, c_ref, sa_ref, sb_ref, do_ref, out_ref, lse_ref, dqkv_ref, qs, ks, dl, dq_s, dk_s, dv_s):
        g = pl.program_id(2)
        c, sa, sb = c_ref[...], sa_ref[...], sb_ref[...]
        qs[...] = _rope_fwd(q_ref[...], c, sa, sb) * scale
        ks[...] = _rope_fwd(k_ref[...], c, sa, sb)
        @pl.when(g == 0)
        def _():
            prod = do_ref[...] * out_ref[...]
            dl[...] = _per_head(
                lambda sl: jnp.broadcast_to(jnp.sum(prod[:, sl], axis=-1, keepdims=True), (SEQ, HEAD_DIM)))

        cur_mask, prev_mask = _band_masks()
        first_head = lax.broadcasted_iota(jnp.int32, (ATTN_BLOCK, PAIR_W), 1) < HEAD_DIM

        def run(dil):
            nb = SEQ // dil // ATTN_BLOCK

            def block(idx, carry):
                r, n = lax.div(idx, nb), lax.rem(idx, nb)
                cur = _block_rows(dil, r, n)
                prev = _block_rows(dil, r, jnp.maximum(n - 1, 0))
                nxt = _block_rows(dil, r, jnp.minimum(n + 1, nb - 1))
                q0, q1 = qs[cur, :].astype(BF16), qs[nxt, :].astype(BF16)
                kp, kc = ks[prev, :].astype(BF16), ks[cur, :].astype(BF16)
                vp, vc = v_ref[prev, :].astype(BF16), v_ref[cur, :].astype(BF16)
                do0, do1 = do_ref[cur, :].astype(BF16), do_ref[nxt, :].astype(BF16)
                lse0, lse1, dl0, dl1 = lse_ref[cur, :], lse_ref[nxt, :], dl[cur, :], dl[nxt, :]
                has_prev = jnp.logical_and(prev_mask, n > 0)
                has_next = jnp.logical_and(prev_mask, n < nb - 1)

                def per_row(t):
                    return jnp.concatenate([t[:, 0:1], t[:, HEAD_DIM:HEAD_DIM + 1]], axis=0)

                q20, q21 = _stack_heads(q0, first_head), _stack_heads(q1, first_head)
                do20, do21 = _stack_heads(do0, first_head), _stack_heads(do1, first_head)
                kcat, vcat = jnp.concatenate([kp, kc], axis=0), jnp.concatenate([vp, vc], axis=0)
                mask0 = _stack_rows(jnp.concatenate([has_prev, cur_mask], axis=1))
                p0 = jnp.where(mask0, jnp.exp(_dot(q20, kcat, "nt") - per_row(lse0)), 0.0)
                ds0 = (p0 * (_dot(do20, vcat, "nt") - per_row(dl0))).astype(BF16)
                p1 = jnp.where(_stack_rows(has_next), jnp.exp(_dot(q21, kc, "nt") - per_row(lse1)), 0.0)
                ds1 = (p1 * (_dot(do21, vc, "nt") - per_row(dl1))).astype(BF16)
                dq2 = _dot(ds0, kcat, "nn")
                dq_s[cur, :] = jnp.where(first_head, dq2[:ATTN_BLOCK], dq2[ATTN_BLOCK:])
                ds_cur = jnp.concatenate([ds0[:, ATTN_BLOCK:], ds1], axis=0)
                p_cur = jnp.concatenate([p0[:, ATTN_BLOCK:], p1], axis=0).astype(BF16)
                dk_s[cur, :] = _dot(ds_cur, jnp.concatenate([q20, q21], axis=0), "tn")
                dv_s[cur, :] = _dot(p_cur, jnp.concatenate([do20, do21], axis=0), "tn")
                return carry

            lax.fori_loop(0, SEQ // ATTN_BLOCK, block, 0, unroll=2)

        for gi in range(N_GROUPS):
            @pl.when(g == gi)
            def _(gi=gi):
                run(DILATIONS[gi])

        dqkv_ref[:, 0:PAIR_W] = _rope_bwd(dq_s[...] * scale, c, sa, sb).astype(BF16)
        dqkv_ref[:, PAIR_W:2 * PAIR_W] = _rope_bwd(dk_s[...], c, sa, sb).astype(BF16)
        dqkv_ref[:, 2 * PAIR_W:] = dv_s[...].astype(BF16)

    slab = pltpu.VMEM((SEQ, PAIR_W), F32)
    return hosted_call(
        body, comm, "attn_bwd", (LOCAL_BATCH, N_PAIRS, N_GROUPS),
        [_slab_spec(0), _slab_spec(1), _slab_spec(2), _TABLE_SPEC, _TABLE_SPEC, _TABLE_SPEC,
         _PAIR_SPEC, _PAIR_SPEC, _PAIR_SPEC],
        [pl.BlockSpec((None, SEQ, 3 * PAIR_W), lambda b, p, g: (b, 0, p * N_GROUPS + g))],
        [jax.ShapeDtypeStruct((LOCAL_BATCH, SEQ, QKV_W), BF16)],
        [slab] * 6, (qkv, qkv, qkv, *tables, dattn, attn, lse), ("parallel", "parallel", "arbitrary"))


def _discretize(lr, li, log_dt, br, bi):
    dt = jnp.exp(log_dt)
    mag = jnp.exp(lr * dt)
    ab_re, ab_im = mag * jnp.cos(li * dt), mag * jnp.sin(li * dt)
    den = lr * lr + li * li
    nr, ni = ab_re - 1.0, ab_im
    f_re = (nr * lr + ni * li) / den
    f_im = (ni * lr - nr * li) / den
    return ab_re, ab_im, f_re[None] * br - f_im[None] * bi, f_re[None] * bi + f_im[None] * br


def ssm_prep(lr, li, log_dt, br, bi):
    def body(lr_ref, li_ref, dt_ref, br_ref, bi_ref, *outs):
        for o, v in zip(outs, _discretize(lr_ref[...], li_ref[...], dt_ref[...], br_ref[...], bi_ref[...])):
            o[...] = v
    shapes = [lr, li, br, bi]
    return pl.pallas_call(body, name="ssm_prep",
                          out_shape=[jax.ShapeDtypeStruct(s.shape, F32) for s in shapes])(lr, li, log_dt, br, bi)


def ssm_prep_bwd(lr, li, log_dt, br, bi, g_ab_re, g_ab_im, g_bb_re, g_bb_im):
    def body(lr_ref, li_ref, dt_ref, br_ref, bi_ref, g0, g1, g2, g3, *outs):
        _, vjp = jax.vjp(_discretize, lr_ref[...], li_ref[...], dt_ref[...], br_ref[...], bi_ref[...])
        for o, v in zip(outs, vjp((g0[...], g1[...], g2[...], g3[...]))):
            o[...] = v
    shapes = [lr, li, log_dt, br, bi]
    return pl.pallas_call(body, name="ssm_prep_bwd",
                          out_shape=[jax.ShapeDtypeStruct(s.shape, F32) for s in shapes])(
        lr, li, log_dt, br, bi, g_ab_re, g_ab_im, g_bb_re, g_bb_im)


def _block_diag(t):
    per = SSM_STATE_W // SSM_LANE_BLOCKS // 64
    g = t.transpose(1, 0, 2).reshape(SSM_LANE_BLOCKS, per, 16, 64)
    eye = jnp.eye(per, dtype=t.dtype)
    return jnp.einsum("jgcn,gh->jgchn", g, eye).reshape(SSM_LANE_BLOCKS, per * 16, per * 64)


def _block_diag_t(m):
    per = SSM_STATE_W // SSM_LANE_BLOCKS // 64
    m5 = m.reshape(SSM_LANE_BLOCKS, per, 16, per, 64)
    d = jnp.einsum("jgchn,gh->jgcn", m5, jnp.eye(per, dtype=m.dtype))
    return d.reshape(SSM_LANE_BLOCKS * per, 16, 64).transpose(1, 0, 2)


def _cmul(ar, ai, br, bi):
    return ar * br - ai * bi, ar * bi + ai * br


def _power_tables(ar, ai, reverse):
    width = ar.shape[1]
    row = lax.broadcasted_iota(jnp.int32, (8, width), 0)
    pows = [(ar, ai)]
    for _ in range(7):
        pows.append(_cmul(pows[-1][0], pows[-1][1], ar, ai))
    steps = []
    for k in (1, 2, 4):
        keep = (row >= k) if not reverse else (row < 8 - k)
        steps.append((jnp.where(keep, pows[k - 1][0], 0.0), jnp.where(keep, pows[k - 1][1], 0.0)))
    cr = jnp.zeros((8, width), F32)
    ci = jnp.zeros((8, width), F32)
    for i in range(8):
        pr, pi = pows[i] if not reverse else pows[7 - i]
        cr = jnp.where(row == i, pr, cr)
        ci = jnp.where(row == i, pi, ci)
    return steps, (cr, ci)


SCAN_CHUNK = 512
STATE_BLOCK = SSM_STATE_W // SSM_LANE_BLOCKS
CHAN_BLOCK = SSM_W // SSM_LANE_BLOCKS


def ssm_fwd(u, ab_re, ab_im, bb_re, bb_im, cb_re, cb_im, d_skip, comm=None):
    nt = SEQ // SCAN_CHUNK
    chan = pl.BlockSpec((None, SCAN_CHUNK, CHAN_BLOCK), lambda b, j, t: (b, t, j))
    state = pl.BlockSpec((None, SCAN_CHUNK, STATE_BLOCK), lambda b, j, t: (b, t, j))
    mat = pl.BlockSpec((None, CHAN_BLOCK, STATE_BLOCK), lambda b, j, t: (j, 0, 0))
    lane = pl.BlockSpec((1, STATE_BLOCK), lambda b, j, t: (0, j))
    dsp = pl.BlockSpec((1, CHAN_BLOCK), lambda b, j, t: (0, j))

    def body(u_ref, ar_ref, ai_ref, bbr_ref, bbi_ref, cbr_ref, cbi_ref, d_ref, y_ref, yg_ref, xr_ref, xi_ref,
             car_r, car_i):
        @pl.when(pl.program_id(2) == 0)
        def _():
            car_r[...] = jnp.zeros_like(car_r)
            car_i[...] = jnp.zeros_like(car_i)

        steps, (pr, pi) = _power_tables(ar_ref[...], ai_ref[...], reverse=False)
        uf = u_ref[...]
        ub = uf.astype(BF16)
        xr_ref[...] = _dot(ub, bbr_ref[...], "nn")
        xi_ref[...] = _dot(ub, bbi_ref[...], "nn")

        def tile(i, carry):
            cr, ci = carry
            sl = pl.ds(pl.multiple_of(i * 8, 8), 8)
            br, bi = xr_ref[sl, :], xi_ref[sl, :]
            for k, (sr, si) in zip((1, 2, 4), steps):
                tr, ti = _cmul(sr, si, pltpu.roll(br, k, 0), pltpu.roll(bi, k, 0))
                br, bi = br + tr, bi + ti
            tr, ti = _cmul(pr, pi, cr, ci)
            br, bi = br + tr, bi + ti
            xr_ref[sl, :] = br
            xi_ref[sl, :] = bi
            return br[7:8, :], bi[7:8, :]

        cr, ci = lax.fori_loop(0, SCAN_CHUNK // 8, tile, (car_r[0:1, :], car_i[0:1, :]), unroll=4)
        car_r[0:1, :] = cr
        car_i[0:1, :] = ci
        y = (_dot(xr_ref[...].astype(BF16), cbr_ref[...], "nt") - _dot(xi_ref[...].astype(BF16), cbi_ref[...], "nt")
             + d_ref[...] * uf)
        y_ref[...] = y
        yg_ref[...] = jax.nn.gelu(y).astype(BF16)

    return hosted_call(
        body, comm, "ssm_fwd", (LOCAL_BATCH, SSM_LANE_BLOCKS, nt),
        [chan, lane, lane, mat, mat, mat, mat, dsp], [chan, chan, state, state],
        [jax.ShapeDtypeStruct((LOCAL_BATCH, SEQ, SSM_W), F32), jax.ShapeDtypeStruct((LOCAL_BATCH, SEQ, SSM_W), BF16),
         jax.ShapeDtypeStruct((LOCAL_BATCH, SEQ, SSM_STATE_W), F32),
         jax.ShapeDtypeStruct((LOCAL_BATCH, SEQ, SSM_STATE_W), F32)],
        [pltpu.VMEM((8, STATE_BLOCK), F32), pltpu.VMEM((8, STATE_BLOCK), F32)],
        (u, ab_re, ab_im, bb_re, bb_im, cb_re, cb_im, d_skip), ("parallel", "parallel", "arbitrary"))


def ssm_bwd(dyg, y, u, xr, xi, ab_re, ab_im, bb_re, bb_im, cb_re, cb_im, d_skip, comm=None):
    nt = SEQ // SCAN_CHUNK
    ntile = SCAN_CHUNK // 8

    def rev(t):
        return nt - 1 - t

    chan = pl.BlockSpec((None, SCAN_CHUNK, CHAN_BLOCK), lambda j, b, t: (b, rev(t), j))
    state = pl.BlockSpec((None, SCAN_CHUNK, STATE_BLOCK), lambda j, b, t: (b, rev(t), j))
    before = pl.BlockSpec((None, 8, STATE_BLOCK), lambda j, b, t: (b, jnp.maximum(rev(t) * ntile - 1, 0), j))
    mat = pl.BlockSpec((None, CHAN_BLOCK, STATE_BLOCK), lambda j, b, t: (j, 0, 0))
    lane = pl.BlockSpec((1, STATE_BLOCK), lambda j, b, t: (0, j))
    lane8 = pl.BlockSpec((8, STATE_BLOCK), lambda j, b, t: (0, j))
    dsp = pl.BlockSpec((1, CHAN_BLOCK), lambda j, b, t: (0, j))

    def body(dyg_ref, y_ref, u_ref, xr_ref, xi_ref, xrb_ref, xib_ref, ar_ref, ai_ref, bbr_ref, bbi_ref, cbr_ref,
             cbi_ref, d_ref, du_ref, dcbr_ref, dcbi_ref, dbbr_ref, dbbi_ref, dd_ref, dar_ref, dai_ref,
             lam_r, lam_i, car_r, car_i):
        b, t = pl.program_id(1), pl.program_id(2)
        first = jnp.logical_and(b == 0, t == 0)

        @pl.when(t == 0)
        def _():
            car_r[...] = jnp.zeros_like(car_r)
            car_i[...] = jnp.zeros_like(car_i)

        @pl.when(first)
        def _():
            for r in (dcbr_ref, dcbi_ref, dbbr_ref, dbbi_ref, dd_ref, dar_ref, dai_ref):
                r[...] = jnp.zeros_like(r)

        steps, (pr, pi) = _power_tables(ar_ref[...], -ai_ref[...], reverse=True)
        uf = u_ref[...]
        _, gelu_vjp = jax.vjp(jax.nn.gelu, y_ref[...])
        dy = gelu_vjp(dyg_ref[...])[0]
        dyb = dy.astype(BF16)
        dd_ref[...] += _colsum(dy * uf)
        lam_r[...] = _dot(dyb, cbr_ref[...], "nn")
        lam_i[...] = -_dot(dyb, cbi_ref[...], "nn")
        dcbr_ref[...] += _dot(dyb, xr_ref[...].astype(BF16), "tn")
        dcbi_ref[...] -= _dot(dyb, xi_ref[...].astype(BF16), "tn")
        row0 = lax.broadcasted_iota(jnp.int32, (8, STATE_BLOCK), 0) == 0
        has_before = rev(t) > 0
        xrb = jnp.where(has_before, xrb_ref[...], 0.0)
        xib = jnp.where(has_before, xib_ref[...], 0.0)

        def tile(s, carry):
            cr, ci, acc_r, acc_i = carry
            i = ntile - 1 - s
            sl = pl.ds(pl.multiple_of(i * 8, 8), 8)
            gr, gi = lam_r[sl, :], lam_i[sl, :]
            for k, (sr, si) in zip((1, 2, 4), steps):
                tr, ti = _cmul(sr, si, pltpu.roll(gr, 8 - k, 0), pltpu.roll(gi, 8 - k, 0))
                gr, gi = gr + tr, gi + ti
            tr, ti = _cmul(pr, pi, cr, ci)
            gr, gi = gr + tr, gi + ti
            lam_r[sl, :] = gr
            lam_i[sl, :] = gi
            sp = pl.ds(pl.multiple_of(jnp.maximum(i - 1, 0) * 8, 8), 8)
            pvr = jnp.where(i > 0, xr_ref[sp, :], xrb)
            pvi = jnp.where(i > 0, xi_ref[sp, :], xib)
            xsr = jnp.where(row0, pltpu.roll(pvr, 1, 0), pltpu.roll(xr_ref[sl, :], 1, 0))
            xsi = jnp.where(row0, pltpu.roll(pvi, 1, 0), pltpu.roll(xi_ref[sl, :], 1, 0))
            acc_r = acc_r + xsr * gr + xsi * gi
            acc_i = acc_i + xsr * gi - xsi * gr
            return gr[0:1, :], gi[0:1, :], acc_r, acc_i

        zero = jnp.zeros((8, STATE_BLOCK), F32)
        cr, ci, acc_r, acc_i = lax.fori_loop(0, ntile, tile, (car_r[0:1, :], car_i[0:1, :], zero, zero), unroll=2)
        car_r[0:1, :] = cr
        car_i[0:1, :] = ci
        dar_ref[...] += acc_r
        dai_ref[...] += acc_i
        lrb, lib = lam_r[...].astype(BF16), lam_i[...].astype(BF16)
        du = _dot(lrb, bbr_ref[...], "nt") + _dot(lib, bbi_ref[...], "nt") + d_ref[...] * dy
        du_ref[...] = du.astype(BF16)
        ub = uf.astype(BF16)
        dbbr_ref[...] += _dot(ub, lrb, "tn")
        dbbi_ref[...] += _dot(ub, lib, "tn")

    mat_shape = jax.ShapeDtypeStruct((SSM_LANE_BLOCKS, CHAN_BLOCK, STATE_BLOCK), F32)
    return hosted_call(
        body, comm, "ssm_bwd", (SSM_LANE_BLOCKS, LOCAL_BATCH, nt),
        [chan, chan, chan, state, state, before, before, lane, lane, mat, mat, mat, mat, dsp],
        [chan, mat, mat, mat, mat, dsp, lane8, lane8],
        [jax.ShapeDtypeStruct((LOCAL_BATCH, SEQ, SSM_W), BF16), mat_shape, mat_shape, mat_shape, mat_shape,
         jax.ShapeDtypeStruct((1, SSM_W), F32), jax.ShapeDtypeStruct((8, SSM_STATE_W), F32),
         jax.ShapeDtypeStruct((8, SSM_STATE_W), F32)],
        [pltpu.VMEM((SCAN_CHUNK, STATE_BLOCK), F32), pltpu.VMEM((SCAN_CHUNK, STATE_BLOCK), F32),
         pltpu.VMEM((8, STATE_BLOCK), F32), pltpu.VMEM((8, STATE_BLOCK), F32)],
        (dyg, y, u, xr, xi, xr, xi, ab_re, ab_im, bb_re, bb_im, cb_re, cb_im, d_skip),
        ("parallel", "arbitrary", "arbitrary"))


def _merge_fn(g0, g1, attn_d, za, zb):
    return jax.nn.sigmoid(g0) * attn_d + jax.nn.sigmoid(g1) * (za * jax.nn.sigmoid(zb))


def _swiglu_fn(a, b):
    return jax.nn.silu(a) * b


def _reduce_start(names, gw, shard_shapes):
    return swap_comm([_to_slots(n, gw[n], shard_shapes[n]) for n in names])


def _reduce_chip(names, swap, got, core):
    return exchange_comm([add_halves(n, g, r, core) for n, g, r in zip(names, swap.ins, got)])


def local_step(x, target, shards, small, core):
    g_mix, g_ffn, g_final = small["norm_mix_g"], small["norm_ffn_g"], small["norm_final_g"]
    tables = _rope_tables()
    seqs = lambda t: t.reshape(LOCAL_BATCH, SEQ, t.shape[-1])
    toks = lambda t: t.reshape(TOKENS, t.shape[-1])
    shard_shapes = {n: s.shape for n, s in shards.items()}
    w = {}

    def gather(names):
        return gather_comm([shards[n] for n in names])

    def arrived(names, slots):
        for n, s in zip(names, slots):
            w[n] = _from_slots(n, s)

    h, *slots = rowwise(lambda xv, g: (_rms(xv, g),), [x, g_mix], [(D_MODEL, BF16)], "norm_mix", comm=gather(["w_in"]))
    arrived(["w_in"], slots)
    w_qkv, w_u, w_gate = _qkv_order(w["w_in"][:QKV_W]), w["w_in"][QKV_W:QKV_W + SSM_W], w["w_in"][QKV_W + SSM_W:]
    mixer_out = ["w_attn_out", "w_glu", "w_out"]
    qkv, *slots = matmul(h, w_qkv, "nt", F32, "proj_qkv", comm=gather(mixer_out))
    arrived(mixer_out, slots)
    qkv = seqs(qkv)
    u = seqs(matmul(h, w_u, "nt", F32, "proj_u"))
    gl = matmul(h, w_gate, "nt", F32, "proj_gate")
    attn_b, attn, lse, *slots = attn_fwd(qkv, tables, comm=gather(["w_ffn_gate"]))
    arrived(["w_ffn_gate"], slots)
    attn_b = toks(attn_b)
    attn_d = matmul(attn_b, w["w_attn_out"], "nn", F32, "attn_out")

    br_t = small["ssm_b_re"].transpose(2, 0, 1)
    bi_t = small["ssm_b_im"].transpose(2, 0, 1)
    log_dt = small["ssm_log_dt"].reshape(32, 1)
    ab_re, ab_im, bb_re_t, bb_im_t = ssm_prep(small["ssm_a_re"], small["ssm_a_im"], log_dt, br_t, bi_t)
    ab = [ab_re.reshape(1, SSM_STATE_W), ab_im.reshape(1, SSM_STATE_W)]
    bb = [_block_diag(bb_re_t).astype(BF16), _block_diag(bb_im_t).astype(BF16)]
    cb = [_block_diag(small["ssm_c_re"].transpose(1, 0, 2)).astype(BF16),
          _block_diag(small["ssm_c_im"].transpose(1, 0, 2)).astype(BF16)]
    d_skip = small["ssm_d"].reshape(1, SSM_W)
    ffn_rest = ["w_ffn_up", "w_ffn_down"]
    y, yg, xr, xi, *slots = ssm_fwd(u, *ab, *bb, *cb, d_skip, comm=gather(ffn_rest))
    arrived(ffn_rest, slots)
    yg2 = toks(yg)
    z = matmul(yg2, w["w_glu"], "nn", F32, "glu")
    gate_ins = [(gl, D_MODEL, 0), (gl, D_MODEL, 1), attn_d, (z, D_MODEL, 0), (z, D_MODEL, 1)]
    (merged,) = rowwise(lambda *v: (_merge_fn(*v),), gate_ins, [(D_MODEL, BF16)], "merge")
    x1, h2 = matmul_rows(merged, w["w_out"], "out_proj", lambda rows, g: (rows, _rms(rows, g)), [g_ffn],
                         [(D_MODEL, F32), (D_MODEL, BF16)], add=x)
    a, b, act = ffn_in(h2, w["w_ffn_gate"], w["w_ffn_up"])

    def final_fn(xv, g, tgt):
        yv, vjp = jax.vjp(_rms, xv, g)
        err = yv - tgt
        dx, dg = vjp(err * (1.0 / D_MODEL))
        loss = 0.5 * jnp.sum(jnp.mean(err * err, axis=-1, keepdims=True), axis=0, keepdims=True)
        return dx, dx, dg, jnp.broadcast_to(loss, (1, LANES))

    dx2, dx2_b, dg_final, loss = matmul_rows(act, w["w_ffn_down"], "ffn_down_loss", final_fn, [g_final, target],
                                             [(D_MODEL, F32), (D_MODEL, BF16)], accs=(D_MODEL, LANES), add=x1)
    gw, parts = {}, {}
    gw["w_ffn_down"] = matmul(act, dx2_b, "tn", F32, "d_ffn_down")
    da_b, db_b = ffn_in_bwd(dx2_b, w["w_ffn_down"], a, b)
    gw["w_ffn_gate"] = matmul(da_b, h2, "tn", F32, "d_ffn_gate")
    gw["w_ffn_up"] = matmul(db_b, h2, "tn", F32, "d_ffn_up")
    ffn = ["w_ffn_down", "w_ffn_gate", "w_ffn_up"]
    swap = _reduce_start(ffn[:2], gw, shard_shapes)
    dh2, *got = matmul(da_b, w["w_ffn_gate"], "nn", F32, "d_h2_gate", comm=swap)
    ffn_exchange = [_reduce_chip(ffn[:2], swap, got, core)]
    swap = _reduce_start(ffn[2:], gw, shard_shapes)

    def norm_bwd(dh, xv, g, skip):
        _, vjp = jax.vjp(_rms, xv, g)
        dx, dg = vjp(dh)
        dx = dx + skip
        return dx, dx, dg

    dx1, dx1_b, dg_ffn, *got = matmul_rows(db_b, w["w_ffn_up"], "d_h2_up_norm", norm_bwd, [x1, g_ffn, dx2],
                                           [(D_MODEL, F32), (D_MODEL, BF16)], accs=(D_MODEL,), add=dh2, comm=swap)
    ffn_exchange = join_comms(ffn_exchange + [_reduce_chip(ffn[2:], swap, got, core)])
    gw["w_out"] = matmul(merged, dx1_b, "tn", F32, "d_out")
    dmerged = matmul(dx1_b, w["w_out"], "nt", F32, "d_merged")

    def merge_bwd(g0, g1, ad, za, zb, dm):
        _, vjp = jax.vjp(_merge_fn, g0, g1, ad, za, zb)
        dg0, dg1, dad, dza, dzb = vjp(dm)
        return jnp.concatenate([dg0, dg1], axis=1), dad, jnp.concatenate([dza, dzb], axis=1)

    dgl_b, dattn_d_b, dz_b = rowwise(merge_bwd, gate_ins + [dmerged],
                                     [(GATE_W, BF16), (D_MODEL, BF16), (GATE_W, BF16)], "merge_bwd")
    gw["w_attn_out"] = matmul(attn_b, dattn_d_b, "tn", F32, "d_attn_out")
    dattn = seqs(matmul(dattn_d_b, w["w_attn_out"], "nt", F32, "d_attn"))
    gw["w_glu"] = matmul(yg2, dz_b, "tn", F32, "d_glu")
    dyg = seqs(matmul(dz_b, w["w_glu"], "nt", F32, "d_yg"))
    mixer = ["w_out", "w_attn_out", "w_glu"]
    swap = _reduce_start(mixer, gw, shard_shapes)
    du_b, dcb_re, dcb_im, dbb_re, dbb_im, dd, da_re8, da_im8, *rest = ssm_bwd(
        dyg, y, u, xr, xi, *ab, *bb, *cb, d_skip, comm=join_comms([ffn_exchange, swap]))
    for n, p in zip(ffn, rest[:len(ffn)]):
        parts[n] = p
    mixer_exchange = _reduce_chip(mixer, swap, rest[len(ffn):], core)
    du_b = toks(du_b)
    g_ab_re = jnp.sum(da_re8, axis=0).reshape(32, 64)
    g_ab_im = jnp.sum(da_im8, axis=0).reshape(32, 64)
    d_lr, d_li, d_ldt, d_br_t, d_bi_t = ssm_prep_bwd(
        small["ssm_a_re"], small["ssm_a_im"], log_dt, br_t, bi_t,
        g_ab_re, g_ab_im, _block_diag_t(dbb_re), _block_diag_t(dbb_im))
    gs = {
        "ssm_a_re": d_lr, "ssm_a_im": d_li, "ssm_log_dt": d_ldt.reshape(1, 32),
        "ssm_b_re": d_br_t.transpose(1, 2, 0), "ssm_b_im": d_bi_t.transpose(1, 2, 0),
        "ssm_c_re": _block_diag_t(dcb_re).transpose(1, 0, 2), "ssm_c_im": _block_diag_t(dcb_im).transpose(1, 0, 2),
        "ssm_d": dd.reshape(32, 16),
    }
    ssm_gather = small_comm(_pack_small(SSM_SMALL, gs, SSM_ROWS))
    dqkv_b, *rest = attn_bwd(qkv, tables, dattn, attn, lse, comm=join_comms([mixer_exchange, ssm_gather]))
    for n, p in zip(mixer, rest):
        parts[n] = p
    ssm_shares = rest[len(mixer)]
    dqkv_b = toks(dqkv_b)
    d_qkv = matmul(dqkv_b, h, "tn", F32, "d_w_qkv")
    d_u = matmul(du_b, h, "tn", F32, "d_w_u")
    d_gate = matmul(dgl_b, h, "tn", F32, "d_w_gate")
    gw["w_in"] = jnp.concatenate([_qkv_order(d_qkv, back=True), d_u, d_gate], axis=0)
    swap = _reduce_start(["w_in"], gw, shard_shapes)
    dh, *got = matmul(dqkv_b, w_qkv, "nn", F32, "d_h_qkv", comm=swap)
    w_in_exchange = _reduce_chip(["w_in"], swap, got, core)
    grad_x, dg_mix, parts["w_in"] = mix_in_bwd([du_b, dgl_b], [w_u, w_gate], dh, x, g_mix, dx1, comm=w_in_exchange)
    gs_norm = {"norm_mix_g": dg_mix, "norm_ffn_g": dg_ffn, "norm_final_g": dg_final}
    return loss, grad_x, parts, ssm_shares, gs_norm


ANY = pl.BlockSpec(memory_space=pl.ANY)
BIG = ("w_in", "w_glu", "w_attn_out", "w_out", "w_ffn_gate", "w_ffn_up", "w_ffn_down")
TRANSPOSED = ("w_in", "w_ffn_gate", "w_ffn_up")
ROW_SHARDED = TRANSPOSED + ("w_out", "w_ffn_down")
SMALL = ("norm_mix_g", "ssm_a_re", "ssm_a_im", "ssm_log_dt", "ssm_b_re", "ssm_b_im", "ssm_c_re", "ssm_c_im",
         "ssm_d", "norm_ffn_g", "norm_final_g")
WEIGHTS = ("norm_mix_g", "w_in", "ssm_a_re", "ssm_a_im", "ssm_log_dt", "ssm_b_re", "ssm_b_im", "ssm_c_re",
           "ssm_c_im", "ssm_d", "w_glu", "w_attn_out", "w_out", "norm_ffn_g", "w_ffn_gate", "w_ffn_up",
           "w_ffn_down", "norm_final_g")
SSM_SMALL = SMALL[1:9]
NORM_SMALL = (SMALL[0],) + SMALL[9:]
SSM_ROWS, NORM_ROWS = 1064, 32
N_BIG = len(BIG)


def _position():
    return lax.axis_index("x"), lax.axis_index("y"), lax.axis_index("c")


def _other_chips(x, y):
    return [(1 - x, y), (x, 1 - y), (1 - x, 1 - y)]


def _remote(src, dst, send_sem, recv_sem, device):
    return pltpu.make_async_remote_copy(src_ref=src, dst_ref=dst, send_sem=send_sem, recv_sem=recv_sem,
                                        device_id=device, device_id_type=MESH)


_later = functools.partial


def _two_level_phases(copies):
    def first(*refs):
        locals_, sends, _, _, _ = copies(*refs)
        for cp in locals_ + sends:
            cp().start()

    def mid(*refs):
        _, _, arrived, passed, _ = copies(*refs)
        for got, cp in zip(arrived, passed):
            got().wait_recv()
            cp().start()

    def last(*refs):
        locals_, sends, _, passed, from_sibling = copies(*refs)
        for cp in from_sibling:
            cp().wait_recv()
        for cp in sends + passed:
            cp().wait_send()
        for cp in locals_:
            cp().wait()

    return first, mid, last


def _half(ref, chip, which):
    rows = ref.shape[1] // 2
    return ref.at[chip, pl.ds(which * rows, rows), :]


class Comm:
    def __init__(self, ins, out_shapes, sems, first, mid, last):
        self.ins, self.out_shapes, self.sems = list(ins), list(out_shapes), list(sems)
        self.first, self.mid, self.last = first, mid, last


def join_comms(comms):
    def cut(refs_by_kind):
        offs, parts = [0, 0, 0], []
        for cm in comms:
            sizes = (len(cm.ins), len(cm.out_shapes), len(cm.sems))
            parts.append(tuple(refs_by_kind[k][offs[k]:offs[k] + sizes[k]] for k in range(3)))
            offs = [o + s for o, s in zip(offs, sizes)]
        return parts

    def phase(which):
        def run(ins, outs, sems):
            for cm, part in zip(comms, cut((ins, outs, sems))):
                fn = getattr(cm, which)
                if fn is not None:
                    fn(*part)
        return run

    return Comm(sum((cm.ins for cm in comms), []), sum((cm.out_shapes for cm in comms), []),
                sum((cm.sems for cm in comms), []), phase("first"), phase("mid"), phase("last"))


def _comm_operands(comm):
    if comm is None:
        return [], [], []
    return comm.ins, comm.out_shapes, comm.sems


def _comm_begin(comm, refs, step, n_steps):
    if comm is None:
        return
    pl.when(step == 0)(lambda: comm.first(*refs))
    if comm.mid is not None:
        pl.when(step == (n_steps * 3) // 4)(lambda: comm.mid(*refs))


def _comm_end(comm, refs, step, n_steps):
    if comm is not None:
        pl.when(step == n_steps - 1)(lambda: comm.last(*refs))


def _comm_refs(comm, refs, n_in, n_out):
    if comm is None:
        return list(refs), None
    ci, co, cs = len(comm.ins), len(comm.out_shapes), len(comm.sems)
    o0 = n_in + ci
    s0 = o0 + n_out + co
    host = list(refs[:n_in]) + list(refs[o0:o0 + n_out]) + list(refs[s0:len(refs) - cs])
    return host, (list(refs[n_in:o0]), list(refs[o0 + n_out:s0]), list(refs[len(refs) - cs:]))


def run_comm(comm, name):
    n_in, n_out = len(comm.ins), len(comm.out_shapes)

    def body(*refs):
        parts = (list(refs[:n_in]), list(refs[n_in:n_in + n_out]), list(refs[n_in + n_out:]))
        comm.first(*parts)
        if comm.mid is not None:
            comm.mid(*parts)
        comm.last(*parts)

    return pl.pallas_call(body, name=name, in_specs=[ANY] * n_in, out_specs=[ANY] * n_out,
                          out_shape=comm.out_shapes, scratch_shapes=comm.sems)(*comm.ins)


def hosted_call(work, comm, name, grid, in_specs, out_specs, out_shape, scratch_shapes, args, semantics):
    c_ins, c_outs, c_sems = _comm_operands(comm)
    n_steps = math.prod(grid)

    def body(*refs):
        host, c_refs = _comm_refs(comm, refs, len(in_specs), len(out_specs))
        step = 0
        for axis, size in enumerate(grid):
            step = step * size + pl.program_id(axis)
        _comm_begin(comm, c_refs, step, n_steps)
        work(*host)
        _comm_end(comm, c_refs, step, n_steps)

    return pl.pallas_call(
        body, name=name, grid=grid, in_specs=list(in_specs) + [ANY] * len(c_ins),
        out_specs=list(out_specs) + [ANY] * len(c_outs), out_shape=list(out_shape) + c_outs,
        scratch_shapes=list(scratch_shapes) + c_sems,
        compiler_params=_params(semantics if comm is None else ("arbitrary",) * len(grid)),
    )(*args, *c_ins)


def gather_comm(shards):
    n = len(shards)

    def copies(srcs, outs, sems):
        send_sems, recv_sems, local_sems = sems
        x, y, c = _position()
        me = 2 * x + y
        sibling = (x, y, 1 - c)
        chips = _other_chips(x, y)
        locals_ = [_later(pltpu.make_async_copy, s, o.at[me], local_sems.at[i])
                   for i, (s, o) in enumerate(zip(srcs, outs))]
        sends, arrived, passed, from_sibling = [], [], [], []
        for j, (px, py) in enumerate(chips):
            for i, (s, o) in enumerate(zip(srcs, outs)):
                rows = s.shape[0] // 2
                sends.append(_later(_remote, s.at[pl.ds(c * rows, rows), :], _half(o, me, c), send_sems.at[i, j],
                                    recv_sems.at[i, j], (px, py, c)))
                got = _half(o, 2 * px + py, c)
                arrived.append(_later(_remote, got, got, send_sems.at[i, j], recv_sems.at[i, j], (px, py, c)))
                passed.append(_later(_remote, got, got, send_sems.at[i, 3 + j], recv_sems.at[i, 3 + j], sibling))
                other = _half(o, 2 * px + py, 1 - c)
                from_sibling.append(_later(_remote, other, other, send_sems.at[i, 3 + j], recv_sems.at[i, 3 + j],
                                           sibling))
        return locals_, sends, arrived, passed, from_sibling

    return Comm(shards, [jax.ShapeDtypeStruct((N_CHIPS,) + s.shape, s.dtype) for s in shards],
                [pltpu.SemaphoreType.DMA((n, 6)), pltpu.SemaphoreType.DMA((n, 6)), pltpu.SemaphoreType.DMA((n,))],
                *_two_level_phases(copies))


def swap_comm(grads):
    n = len(grads)

    def copies(srcs, gots, sems):
        send_sems, recv_sems = sems
        x, y, c = _position()
        out = []
        for i, (s, o) in enumerate(zip(srcs, gots)):
            rows = s.shape[1] // 2
            out.append(_remote(s.at[:, pl.ds((1 - c) * rows, rows), :], o, send_sems.at[i], recv_sems.at[i],
                               (x, y, 1 - c)))
        return out

    def first(srcs, gots, sems):
        for cp in copies(srcs, gots, sems):
            cp.start()

    def last(srcs, gots, sems):
        for cp in copies(srcs, gots, sems):
            cp.wait()

    return Comm(grads, [jax.ShapeDtypeStruct((N_CHIPS, g.shape[1] // 2, g.shape[2]), g.dtype) for g in grads],
                [pltpu.SemaphoreType.DMA((n,)), pltpu.SemaphoreType.DMA((n,))], first, None, last)


def add_halves(name, g, got, core):
    _, half, cols = got.shape
    mine = pl.BlockSpec((None, half, cols), lambda k, c_ref: (k, c_ref[0], 0))
    other = pl.BlockSpec((None, half, cols), lambda k, c_ref: (k, 0, 0))

    def body(c_ref, g_ref, got_ref, o_ref):
        o_ref[...] = (g_ref[...] + got_ref[...]).astype(BF16)

    return pl.pallas_call(
        body, name="add_halves_" + name,
        grid_spec=pltpu.PrefetchScalarGridSpec(num_scalar_prefetch=1, grid=(N_CHIPS,), in_specs=[mine, other],
                                               out_specs=other),
        out_shape=jax.ShapeDtypeStruct(got.shape, BF16),
        compiler_params=_params(("parallel",)),
    )(core, g, got)


def exchange_comm(parts):
    n = len(parts)

    def copies(srcs, outs, sems):
        send_sems, recv_sems, local_sems = sems
        x, y, c = _position()
        me = 2 * x + y
        sibling = (x, y, 1 - c)
        chips = _other_chips(x, y)
        locals_, sends, arrived, passed, from_sibling = [], [], [], [], []
        for i, (s, o) in enumerate(zip(srcs, outs)):
            locals_.append(_later(pltpu.make_async_copy, s.at[me], _half(o, me, c), local_sems.at[i]))
            sends.append(_later(_remote, s.at[me], _half(o, me, c), send_sems.at[i, 3], recv_sems.at[i, 3], sibling))
            other = _half(o, me, 1 - c)
            from_sibling.append(_later(_remote, other, other, send_sems.at[i, 3], recv_sems.at[i, 3], sibling))
        for j, (px, py) in enumerate(chips):
            for i, (s, o) in enumerate(zip(srcs, outs)):
                sends.append(_later(_remote, s.at[2 * px + py], _half(o, me, c), send_sems.at[i, j],
                                    recv_sems.at[i, j], (px, py, c)))
                got = _half(o, 2 * px + py, c)
                arrived.append(_later(_remote, got, got, send_sems.at[i, j], recv_sems.at[i, j], (px, py, c)))
                passed.append(_later(_remote, got, got, send_sems.at[i, 4 + j], recv_sems.at[i, 4 + j], sibling))
                other = _half(o, 2 * px + py, 1 - c)
                from_sibling.append(_later(_remote, other, other, send_sems.at[i, 4 + j], recv_sems.at[i, 4 + j],
                                           sibling))
        return locals_, sends, arrived, passed, from_sibling

    return Comm(parts, [jax.ShapeDtypeStruct((N_CHIPS, 2 * p.shape[1], p.shape[2]), p.dtype) for p in parts],
                [pltpu.SemaphoreType.DMA((n, 7)), pltpu.SemaphoreType.DMA((n, 7)), pltpu.SemaphoreType.DMA((n,))],
                *_two_level_phases(copies))


def small_comm(pack):
    def copies(srcs, outs, sems):
        (src_ref,), (out_ref,), (send_sems, recv_sems, local_sem) = srcs, outs, sems
        x, y, c = _position()
        me = 4 * x + 2 * y + c
        flips = [(fx, fy, fc) for fx in (0, 1) for fy in (0, 1) for fc in (0, 1)][1:]
        peers = [(1 - x if fx else x, 1 - y if fy else y, 1 - c if fc else c) for fx, fy, fc in flips]
        local = _later(pltpu.make_async_copy, src_ref, out_ref.at[me], local_sem)
        sends = [_later(_remote, src_ref, out_ref.at[me], send_sems.at[j], recv_sems.at[j], peer)
                 for j, peer in enumerate(peers)]
        arrived = []
        for j, (px, py, pc) in enumerate(peers):
            got = out_ref.at[4 * px + 2 * py + pc]
            arrived.append(_later(_remote, got, got, send_sems.at[j], recv_sems.at[j], (px, py, pc)))
        return local, sends, arrived

    def first(*refs):
        local, sends, _ = copies(*refs)
        for cp in [local] + sends:
            cp().start()

    def last(*refs):
        local, sends, arrived = copies(*refs)
        for cp in arrived:
            cp().wait_recv()
        for cp in sends:
            cp().wait_send()
        local().wait()

    return Comm([pack], [jax.ShapeDtypeStruct((N_DEV,) + pack.shape, pack.dtype)],
                [pltpu.SemaphoreType.DMA((7,)), pltpu.SemaphoreType.DMA((7,)), pltpu.SemaphoreType.DMA],
                first, None, last)


def _adam_fn(w, g, m, v):
    m = ADAM_B1 * m + (1.0 - ADAM_B1) * g
    v = ADAM_B2 * v + (1.0 - ADAM_B2) * jnp.square(g)
    m_hat = m / (1.0 - ADAM_B1 ** ADAM_STEP)
    v_hat = v / (1.0 - ADAM_B2 ** ADAM_STEP)
    return -ADAM_LR * (m_hat / (jnp.sqrt(v_hat) + ADAM_EPS) + ADAM_WD * w), m, v


def adam_big(name, parts, w, m, v):
    rows, cols = w.shape
    tm = _pick(rows, 384, 16)

    def fn(p0, p1, p2, p3, wv, mv, vv):
        g = ((p0.astype(F32) + p1.astype(F32)) + p2.astype(F32)) + p3.astype(F32)
        return (g,) + _adam_fn(wv, g, mv, vv)

    return rowwise(fn, [parts, w, m, v], [(cols, F32)] * 4, "adam_" + name, tm=tm, rows=rows)


def adam_small(name, gathered, w, m, v):
    def body(g_ref, w_ref, m_ref, v_ref, go_ref, d_ref, mo_ref, vo_ref):
        g = g_ref[0]
        for k in range(1, N_DEV):
            g = g + g_ref[k]
        go_ref[...] = g
        d_ref[...], mo_ref[...], vo_ref[...] = _adam_fn(w_ref[...], g, m_ref[...], v_ref[...])

    return pl.pallas_call(body, name=name, out_shape=[jax.ShapeDtypeStruct(w.shape, F32)] * 4,
                          compiler_params=_params())(gathered, w, m, v)


def _pack_small(names, vals, rows, last=None):
    flat = [vals[n].reshape(-1) for n in names]
    if last is not None:
        flat.append(last.reshape(-1))
    flat = jnp.concatenate(flat)
    return jnp.pad(flat, (0, rows * LANES - flat.shape[0])).reshape(rows, LANES)


def _unpack_small(names, pack, shapes):
    flat, out, off = pack.reshape(-1), {}, 0
    for n in names:
        size = math.prod(shapes[n])
        out[n] = flat[off:off + size].reshape(shapes[n])
        off += size
    return out, flat[off]


def _to_slots(name, g, shard_shape):
    rows, cols = shard_shape
    if name in ROW_SHARDED:
        return g.reshape(N_CHIPS, rows, cols)
    return g.reshape(rows, N_CHIPS, cols).transpose(1, 0, 2)


def _from_slots(name, s):
    _, rows, cols = s.shape
    if name in ROW_SHARDED:
        return s.reshape(N_CHIPS * rows, cols)
    return s.transpose(1, 0, 2).reshape(rows, N_CHIPS * cols)


def kernel(x, norm_mix_g, w_in, ssm_a_re, ssm_a_im, ssm_log_dt, ssm_b_re, ssm_b_im, ssm_c_re, ssm_c_im, ssm_d, w_glu, w_attn_out, w_out, norm_ffn_g, w_ffn_gate, w_ffn_up, w_ffn_down, norm_final_g, loss_target, m_norm_mix_g, m_w_in, m_ssm_a_re, m_ssm_a_im, m_ssm_log_dt, m_ssm_b_re, m_ssm_b_im, m_ssm_c_re, m_ssm_c_im, m_ssm_d, m_w_glu, m_w_attn_out, m_w_out, m_norm_ffn_g, m_w_ffn_gate, m_w_ffn_up, m_w_ffn_down, m_norm_final_g, v_norm_mix_g, v_w_in, v_ssm_a_re, v_ssm_a_im, v_ssm_log_dt, v_ssm_b_re, v_ssm_b_im, v_ssm_c_re, v_ssm_c_im, v_ssm_d, v_w_glu, v_w_attn_out, v_w_out, v_norm_ffn_g, v_w_ffn_gate, v_w_ffn_up, v_w_ffn_down, v_norm_final_g):
    given = dict(locals())
    def local(name, prefix=""):
        t = given[prefix + name][0]
        return t.T if name in TRANSPOSED else t

    shard = {n: local(n) for n in BIG}
    shapes = {n: given[n].shape for n in WEIGHTS}

    small = {n: given[n] for n in SMALL}
    small_2d = dict(small)
    for n in ("ssm_a_re", "ssm_a_im", "ssm_b_re", "ssm_b_im", "ssm_c_re", "ssm_c_im", "ssm_d"):
        small_2d[n] = small[n][0]
    small_2d["norm_final_g"] = norm_final_g.reshape(1, D_MODEL)

    core = lax.axis_index("c").astype(jnp.int32).reshape(1)
    loss, grad_x, parts, ssm_shares, gs_norm = local_step(
        x.reshape(TOKENS, D_MODEL), loss_target.reshape(TOKENS, D_MODEL),
        {n: shard[n].astype(BF16) for n in BIG}, small_2d, core)

    (norm_shares,) = run_comm(small_comm(_pack_small(NORM_SMALL, gs_norm, NORM_ROWS, last=loss)), "gather_norm_grads")
    small_out = [{} for _ in range(4)]
    for names, rows, shares in ((SSM_SMALL, SSM_ROWS, ssm_shares), (NORM_SMALL, NORM_ROWS, norm_shares)):
        packs = [_pack_small(names, {n: given[p + n] for n in names}, rows) for p in ("", "m_", "v_")]
        for kind, t in enumerate(adam_small("adam_" + names[0], shares, *packs)):
            vals, after = _unpack_small(names, t, shapes)
            small_out[kind].update(vals)
            if kind == 0:
                total_loss = after

    big_out = {}
    for n in BIG:
        res = adam_big(n, parts[n], shard[n], local(n, "m_"), local(n, "v_"))
        big_out[n] = [(t.T if n in TRANSPOSED else t)[None] for t in res]

    outs = [total_loss, grad_x.reshape(LOCAL_BATCH, SEQ, D_MODEL)]
    for kind in range(4):
        for n in WEIGHTS:
            outs.append(big_out[n][kind] if n in BIG else small_out[kind][n])
    return tuple(outs)
```

```python
import functools
import math

import jax
import jax.numpy as jnp
from jax import lax
from jax.experimental import pallas as pl
from jax.experimental.pallas import tpu as pltpu

F32 = jnp.float32
BF16 = jnp.bfloat16
MESH = pl.DeviceIdType.MESH

D_MODEL = 1024
SEQ = 2048
LOCAL_BATCH = 2
TOKENS = LOCAL_BATCH * SEQ
HEAD_DIM = 64
HEADS_PER_GROUP = 4
GROUP_W = HEADS_PER_GROUP * HEAD_DIM
N_GROUPS = 3
DILATIONS = (1, 4, 16)
ATTN_BLOCK = 128
ROPE_DIM = 16
ROPE_THETA = 500000.0
QKV_W = 3 * N_GROUPS * GROUP_W
SSM_W = 512
SSM_STATE_W = 2048
SSM_LANE_BLOCKS = 4
GATE_W = 2 * D_MODEL
D_FF = 2816
RMS_EPS = 1e-6
NEG_INF = -1e30
ADAM_LR, ADAM_B1, ADAM_B2, ADAM_EPS, ADAM_WD, ADAM_STEP = 0.001, 0.9, 0.999, 1e-08, 0.01, 10
N_CHIPS = 4
N_DEV = 8

VMEM_LIMIT = 56 * 1024 * 1024
LANES = 128


def _params(sem=None):
    return pltpu.CompilerParams(dimension_semantics=sem, vmem_limit_bytes=VMEM_LIMIT)


def _pick(n, cap, align=LANES):
    best = None
    for d in range(align, min(n, cap) + 1, align):
        if n % d == 0:
            best = d
    return n if best is None or n <= cap else best


_DIMS = {"nn": (((1,), (0,)), ((), ())), "nt": (((1,), (1,)), ((), ())), "tn": (((0,), (0,)), ((), ()))}


def _dot(a, b, mode):
    return lax.dot_general(a, b, _DIMS[mode], preferred_element_type=F32)


def matmul(a, b, mode, out_dtype, name, add=None, comm=None):
    if mode == "nn":
        (m, k), n = a.shape, b.shape[1]
    elif mode == "nt":
        (m, k), n = a.shape, b.shape[0]
    else:
        (k, m), n = a.shape, b.shape[1]
    tn = _pick(n, 1408)
    tk = _pick(k, 2816) if mode != "tn" else _pick(k, 1024)
    tm = _pick(m, 1408)
    out_bytes = jnp.dtype(out_dtype).itemsize

    def need(tm_):
        return 2 * 2 * (tm_ * tk + tk * tn) + tm_ * tn * (4 + 2 * out_bytes + (8 if add is not None else 0))

    while need(tm) > 40 * 1024 * 1024 and tm % 256 == 0:
        tm //= 2
    nk = k // tk
    a_spec = {"nn": pl.BlockSpec((tm, tk), lambda i, j, kk: (i, kk)),
              "nt": pl.BlockSpec((tm, tk), lambda i, j, kk: (i, kk)),
              "tn": pl.BlockSpec((tk, tm), lambda i, j, kk: (kk, i))}[mode]
    b_spec = {"nn": pl.BlockSpec((tk, tn), lambda i, j, kk: (kk, j)),
              "nt": pl.BlockSpec((tn, tk), lambda i, j, kk: (j, kk)),
              "tn": pl.BlockSpec((tk, tn), lambda i, j, kk: (kk, j))}[mode]
    o_spec = pl.BlockSpec((tm, tn), lambda i, j, kk: (i, j))

    def body(a_ref, b_ref, *rest):
        if add is not None:
            add_ref, o_ref, acc_ref = rest
        else:
            o_ref, acc_ref = rest
        part = _dot(a_ref[...], b_ref[...], mode)
        if nk == 1:
            res = part if add is None else part + add_ref[...]
            o_ref[...] = res.astype(out_dtype)
            return
        kk = pl.program_id(2)

        @pl.when(kk == 0)
        def _():
            acc_ref[...] = part

        @pl.when(kk > 0)
        def _():
            acc_ref[...] += part

        @pl.when(kk == nk - 1)
        def _():
            res = acc_ref[...] if add is None else acc_ref[...] + add_ref[...]
            o_ref[...] = res.astype(out_dtype)

    in_specs = [a_spec, b_spec] + ([o_spec] if add is not None else [])
    args = (a, b) + ((add,) if add is not None else ())
    res = hosted_call(
        body, comm, name, (m // tm, n // tn, nk), in_specs, [o_spec], [jax.ShapeDtypeStruct((m, n), out_dtype)],
        [pltpu.VMEM((tm, tn) if nk > 1 else (8, LANES), F32)], args, ("parallel", "parallel", "arbitrary"))
    return res[0] if comm is None else res


def matmul_rows(a, b, name, fn, extra, outs, accs=(), add=None, comm=None, tm=512):
    (m, k), n = a.shape, b.shape[1]
    n_fixed = 2 + (add is not None)
    row_spec = lambda cols: pl.BlockSpec((tm, cols), lambda i: (i, 0))
    in_specs = [row_spec(k), pl.BlockSpec((k, n), lambda i: (0, 0))] + ([row_spec(n)] if add is not None else [])
    in_specs += [pl.BlockSpec(e.shape, lambda i: (0, 0)) if e.shape[0] == 1 else row_spec(e.shape[1]) for e in extra]
    out_specs = [row_spec(c) for c, _ in outs] + [pl.BlockSpec((1, c), lambda i: (0, 0)) for c in accs]
    out_shape = [jax.ShapeDtypeStruct((m, c), dt) for c, dt in outs] + [jax.ShapeDtypeStruct((1, c), F32) for c in accs]

    def body(*refs):
        rows = _dot(refs[0][...], refs[1][...], "nn")
        if add is not None:
            rows = rows + refs[2][...]
        n_in = n_fixed + len(extra)
        res = fn(rows, *[r[...] for r in refs[n_fixed:n_in]])
        for r, v in zip(refs[n_in:n_in + len(outs)], res[:len(outs)]):
            r[...] = v.astype(r.dtype)
        first = pl.program_id(0) == 0
        for r, v in zip(refs[n_in + len(outs):], res[len(outs):]):
            @pl.when(first)
            def _(r=r, v=v):
                r[...] = v

            @pl.when(jnp.logical_not(first))
            def _(r=r, v=v):
                r[...] += v

    args = (a, b) + ((add,) if add is not None else ()) + tuple(extra)
    return hosted_call(body, comm, name, (m // tm,), in_specs, out_specs, out_shape, [], args, ("arbitrary",))


FFN_TM, FFN_TN = 512, 1408


def ffn_in(h2, wg_t, wu_t):
    def body(h_ref, wg_ref, wu_ref, a_ref, b_ref, act_ref):
        hv = h_ref[...]
        a, b = _dot(hv, wg_ref[...], "nt"), _dot(hv, wu_ref[...], "nt")
        a_ref[...] = a.astype(BF16)
        b_ref[...] = b.astype(BF16)
        act_ref[...] = _swiglu_fn(a, b).astype(BF16)

    rows = pl.BlockSpec((FFN_TM, D_MODEL), lambda i, j: (i, 0))
    wts = pl.BlockSpec((FFN_TN, D_MODEL), lambda i, j: (j, 0))
    out = pl.BlockSpec((FFN_TM, FFN_TN), lambda i, j: (i, j))
    return pl.pallas_call(
        body, name="ffn_in", grid=(TOKENS // FFN_TM, D_FF // FFN_TN), in_specs=[rows, wts, wts],
        out_specs=[out] * 3, out_shape=[jax.ShapeDtypeStruct((TOKENS, D_FF), BF16)] * 3,
        compiler_params=_params(("parallel", "parallel")),
    )(h2, wg_t, wu_t)


def ffn_in_bwd(dx2_b, wd, a, b):
    def body(dx_ref, wd_ref, a_ref, b_ref, da_ref, db_ref):
        dact = _dot(dx_ref[...], wd_ref[...], "nt")
        _, vjp = jax.vjp(_swiglu_fn, a_ref[...].astype(F32), b_ref[...].astype(F32))
        da, db = vjp(dact)
        da_ref[...] = da.astype(BF16)
        db_ref[...] = db.astype(BF16)

    rows = pl.BlockSpec((FFN_TM, D_MODEL), lambda i, j: (i, 0))
    wts = pl.BlockSpec((FFN_TN, D_MODEL), lambda i, j: (j, 0))
    out = pl.BlockSpec((FFN_TM, FFN_TN), lambda i, j: (i, j))
    return pl.pallas_call(
        body, name="ffn_in_bwd", grid=(TOKENS // FFN_TM, D_FF // FFN_TN), in_specs=[rows, wts, out, out],
        out_specs=[out] * 2, out_shape=[jax.ShapeDtypeStruct((TOKENS, D_FF), BF16)] * 2,
        compiler_params=_params(("parallel", "parallel")),
    )(dx2_b, wd, a, b)


def mix_in_bwd(grads, weights, partial, x, g, skip, comm=None):
    n = len(grads)
    tm = 512

    def body(*refs):
        a_refs, b_refs = refs[:n], refs[n:2 * n]
        part_ref, x_ref, g_ref, skip_ref, gx_ref, dg_ref = refs[2 * n:]
        dh = part_ref[...]
        for a_ref, b_ref in zip(a_refs, b_refs):
            dh = dh + _dot(a_ref[...], b_ref[...], "nn")
        _, vjp = jax.vjp(_rms, x_ref[...], g_ref[...])
        dx, dg = vjp(dh)
        gx_ref[...] = dx + skip_ref[...]
        first = pl.program_id(0) == 0

        @pl.when(first)
        def _():
            dg_ref[...] = dg

        @pl.when(jnp.logical_not(first))
        def _():
            dg_ref[...] += dg

    rows = pl.BlockSpec((tm, D_MODEL), lambda i: (i, 0))
    gain = pl.BlockSpec((1, D_MODEL), lambda i: (0, 0))
    in_specs = [pl.BlockSpec((tm, a.shape[1]), lambda i: (i, 0)) for a in grads]
    in_specs += [pl.BlockSpec(b.shape, lambda i: (0, 0)) for b in weights]
    return hosted_call(
        body, comm, "mix_in_bwd", (TOKENS // tm,), in_specs + [rows, rows, gain, rows], [rows, gain],
        [jax.ShapeDtypeStruct((TOKENS, D_MODEL), F32), jax.ShapeDtypeStruct((1, D_MODEL), F32)], [],
        (*grads, *weights, partial, x, g, skip), ("arbitrary",))


def rowwise(fn, ins, outs, name, accs=(), tm=256, rows=TOKENS, comm=None):
    in_specs, args = [], []
    for item in ins:
        arr, width, blk = item if isinstance(item, tuple) else (item, None, 0)
        if arr.ndim == 3:
            for k in range(arr.shape[0]):
                in_specs.append(pl.BlockSpec((None, tm, arr.shape[2]), functools.partial(lambda i, k_: (k_, i, 0), k_=k)))
                args.append(arr)
            continue
        if arr.shape[0] == 1:
            in_specs.append(pl.BlockSpec(arr.shape, lambda i: (0, 0)))
        elif width is None:
            in_specs.append(pl.BlockSpec((tm, arr.shape[1]), lambda i: (i, 0)))
        else:
            in_specs.append(pl.BlockSpec((tm, width), functools.partial(lambda i, blk_: (i, blk_), blk_=blk)))
        args.append(arr)
    out_specs = [pl.BlockSpec((tm, c), lambda i: (i, 0)) for c, _ in outs]
    out_specs += [pl.BlockSpec((1, c), lambda i: (0, 0)) for c in accs]
    out_shape = [jax.ShapeDtypeStruct((rows, c), dt) for c, dt in outs]
    out_shape += [jax.ShapeDtypeStruct((1, c), F32) for c in accs]
    n_in, n_out = len(args), len(outs)
    c_ins, c_outs, c_sems = _comm_operands(comm)

    def body(*refs):
        refs, c_refs = _comm_refs(comm, refs, n_in, n_out + len(accs))
        step = pl.program_id(0)
        _comm_begin(comm, c_refs, step, rows // tm)
        res = fn(*[r[...] for r in refs[:n_in]])
        for r, v in zip(refs[n_in:n_in + n_out], res[:n_out]):
            r[...] = v.astype(r.dtype)
        first = step == 0
        for r, v in zip(refs[n_in + n_out:], res[n_out:]):
            @pl.when(first)
            def _(r=r, v=v):
                r[...] = v

            @pl.when(jnp.logical_not(first))
            def _(r=r, v=v):
                r[...] += v
        _comm_end(comm, c_refs, step, rows // tm)

    return pl.pallas_call(
        body, name=name, grid=(rows // tm,), in_specs=in_specs + [ANY] * len(c_ins),
        out_specs=out_specs + [ANY] * len(c_outs), out_shape=out_shape + c_outs, scratch_shapes=c_sems,
        compiler_params=_params(("arbitrary",)),
    )(*args, *c_ins)


def _rms(x, g):
    return x * lax.rsqrt(jnp.mean(x * x, axis=-1, keepdims=True) + RMS_EPS) * g


def _colsum(v):
    return jnp.sum(v, axis=0, keepdims=True)


PAIR_W = 2 * HEAD_DIM
N_PAIRS = HEADS_PER_GROUP // 2


def _qkv_order(w_t, back=False):
    dims = (N_PAIRS, N_GROUPS, 3) if back else (3, N_GROUPS, N_PAIRS)
    return w_t.reshape(dims + (PAIR_W, w_t.shape[1])).transpose(2, 1, 0, 3, 4).reshape(QKV_W, w_t.shape[1])


def _rope_tables():
    half = ROPE_DIM // 2
    inv = jnp.power(jnp.float32(ROPE_THETA), -jnp.arange(half, dtype=F32) * 2.0 / ROPE_DIM)
    ang = jnp.arange(SEQ, dtype=F32)[:, None] * inv[None, :]
    cos, sin = jnp.cos(ang), jnp.sin(ang)
    zeros = jnp.zeros((SEQ, HEAD_DIM - ROPE_DIM), F32)
    zh = jnp.zeros((SEQ, half), F32)
    c = jnp.concatenate([cos, cos, zeros + 1.0], axis=1)
    sa = jnp.concatenate([-sin, zh, zeros], axis=1)
    sb = jnp.concatenate([zh, sin, zeros], axis=1)
    return [jnp.tile(t, (1, 2)) for t in (c, sa, sb)]


def _rope_fwd(x, c, sa, sb):
    return x * c + pltpu.roll(x, PAIR_W - 8, 1) * sa + pltpu.roll(x, 8, 1) * sb


def _rope_bwd(dy, c, sa, sb):
    return dy * c + pltpu.roll(dy * sb, PAIR_W - 8, 1) + pltpu.roll(dy * sa, 8, 1)


def _band_masks():
    row = lax.broadcasted_iota(jnp.int32, (ATTN_BLOCK, ATTN_BLOCK), 0)
    col = lax.broadcasted_iota(jnp.int32, (ATTN_BLOCK, ATTN_BLOCK), 1)
    return col <= row, col >= row


def _stack_rows(t):
    return jnp.concatenate([t, t], axis=0)


def _stack_heads(t, first_head):
    return jnp.concatenate([jnp.where(first_head, t, 0), jnp.where(first_head, 0, t)], axis=0)


def _per_head(fn):
    return jnp.concatenate([fn(slice(h * HEAD_DIM, (h + 1) * HEAD_DIM)) for h in range(2)], axis=1)


def _slab_spec(kind):
    return pl.BlockSpec((None, SEQ, PAIR_W), lambda b, p, g: (b, 0, p * 3 * N_GROUPS + g * 3 + kind))


_TABLE_SPEC = pl.BlockSpec((SEQ, PAIR_W), lambda b, p, g: (0, 0))
_PAIR_SPEC = pl.BlockSpec((None, SEQ, PAIR_W), lambda b, p, g: (b, 0, p))


def _block_rows(dil, r, n):
    return pl.ds(n * (ATTN_BLOCK * dil) + r, ATTN_BLOCK, stride=dil)


def attn_fwd(qkv, tables, comm=None):
    scale = HEAD_DIM ** -0.5

    def body(q_ref, k_ref, v_ref, c_ref, sa_ref, sb_ref, attn_b_ref, attn_ref, lse_ref, qs, ks, o0, o1, o2, l0, l1, l2):
        g = pl.program_id(2)
        c, sa, sb = c_ref[...], sa_ref[...], sb_ref[...]
        qs[...] = _rope_fwd(q_ref[...], c, sa, sb) * scale
        ks[...] = _rope_fwd(k_ref[...], c, sa, sb)
        cur_mask, prev_mask = _band_masks()
        first_head = lax.broadcasted_iota(jnp.int32, (ATTN_BLOCK, PAIR_W), 1) < HEAD_DIM

        def run(dil, o_slab, l_slab):
            nb = SEQ // dil // ATTN_BLOCK

            def block(idx, carry):
                r, n = lax.div(idx, nb), lax.rem(idx, nb)
                cur, prev = _block_rows(dil, r, n), _block_rows(dil, r, jnp.maximum(n - 1, 0))
                q = qs[cur, :].astype(BF16)
                kc, kp = ks[cur, :].astype(BF16), ks[prev, :].astype(BF16)
                vc, vp = v_ref[cur, :].astype(BF16), v_ref[prev, :].astype(BF16)
                q2 = _stack_heads(q, first_head)
                mask = _stack_rows(jnp.concatenate([jnp.logical_and(prev_mask, n > 0), cur_mask], axis=1))
                s2 = jnp.where(mask, _dot(q2, jnp.concatenate([kp, kc], axis=0), "nt"), NEG_INF)
                m = jnp.max(s2, axis=-1, keepdims=True)
                vcat, two = jnp.concatenate([vp, vc], axis=0), _stack_rows(first_head)
                vext = jnp.concatenate([jnp.where(two, vcat, 1), jnp.where(two, 1, vcat)], axis=1)
                r2 = _dot(jnp.exp(s2 - m).astype(BF16), vext, "nn")
                r0, r1 = r2[:ATTN_BLOCK, :PAIR_W], r2[ATTN_BLOCK:, PAIR_W:]
                num = jnp.where(first_head, r0, r1)
                den = pltpu.roll(jnp.where(first_head, r1, r0), HEAD_DIM, 1)
                o_slab[cur, :] = num / den
                l_slab[cur, :] = jnp.where(first_head, m[:ATTN_BLOCK], m[ATTN_BLOCK:]) + jnp.log(den)
                return carry

            lax.fori_loop(0, SEQ // ATTN_BLOCK, block, 0, unroll=4)

        for gi, (o_slab, l_slab) in enumerate(((o0, l0), (o1, l1), (o2, l2))):
            @pl.when(g == gi)
            def _(gi=gi, o_slab=o_slab, l_slab=l_slab):
                run(DILATIONS[gi], o_slab, l_slab)

        @pl.when(g == N_GROUPS - 1)
        def _():
            a, b, cc = l0[...], l1[...], l2[...]
            m = jnp.maximum(jnp.maximum(a, b), cc)
            e0, e1, e2 = jnp.exp(a - m), jnp.exp(b - m), jnp.exp(cc - m)
            tot = e0 + e1 + e2
            attn = (e0 * o0[...] + e1 * o1[...] + e2 * o2[...]) / tot
            attn_ref[...] = attn
            attn_b_ref[...] = attn.astype(BF16)
            lse_ref[...] = m + jnp.log(tot)

    shape = (LOCAL_BATCH, SEQ, GROUP_W)
    slab = pltpu.VMEM((SEQ, PAIR_W), F32)
    return hosted_call(
        body, comm, "attn_fwd", (LOCAL_BATCH, N_PAIRS, N_GROUPS),
        [_slab_spec(0), _slab_spec(1), _slab_spec(2), _TABLE_SPEC, _TABLE_SPEC, _TABLE_SPEC], [_PAIR_SPEC] * 3,
        [jax.ShapeDtypeStruct(shape, BF16), jax.ShapeDtypeStruct(shape, F32), jax.ShapeDtypeStruct(shape, F32)],
        [slab] * 8, (qkv, qkv, qkv, *tables), ("parallel", "parallel", "arbitrary"))


def attn_bwd(qkv, tables, dattn, attn, lse, comm=None):
    scale = HEAD_DIM ** -0.5

    def body(q_ref, k_ref, v_ref, c_ref, sa_ref, sb_ref, do_ref, out_ref, lse_ref, dqkv_ref, qs, ks, dl, dq_s, dk_s, dv_s):
        g = pl.program_id(2)
        c, sa, sb = c_ref[...], sa_ref[...], sb_ref[...]
        qs[...] = _rope_fwd(q_ref[...], c, sa, sb) * scale
        ks[...] = _rope_fwd(k_ref[...], c, sa, sb)
        @pl.when(g == 0)
        def _():
            prod = do_ref[...] * out_ref[...]
            dl[...] = _per_head(
                lambda sl: jnp.broadcast_to(jnp.sum(prod[:, sl], axis=-1, keepdims=True), (SEQ, HEAD_DIM)))

        cur_mask, prev_mask = _band_masks()
        first_head = lax.broadcasted_iota(jnp.int32, (ATTN_BLOCK, PAIR_W), 1) < HEAD_DIM

        def run(dil):
            nb = SEQ // dil // ATTN_BLOCK

            def block(idx, carry):
                r, n = lax.div(idx, nb), lax.rem(idx, nb)
                cur = _block_rows(dil, r, n)
                prev = _block_rows(dil, r, jnp.maximum(n - 1, 0))
                nxt = _block_rows(dil, r, jnp.minimum(n + 1, nb - 1))
                q0, q1 = qs[cur, :].astype(BF16), qs[nxt, :].astype(BF16)
                kp, kc = ks[prev, :].astype(BF16), ks[cur, :].astype(BF16)
                vp, vc = v_ref[prev, :].astype(BF16), v_ref[cur, :].astype(BF16)
                do0, do1 = do_ref[cur, :].astype(BF16), do_ref[nxt, :].astype(BF16)
                lse0, lse1, dl0, dl1 = lse_ref[cur, :], lse_ref[nxt, :], dl[cur, :], dl[nxt, :]
                has_prev = jnp.logical_and(prev_mask, n > 0)
                has_next = jnp.logical_and(prev_mask, n < nb - 1)

                def per_row(t):
                    return jnp.concatenate([t[:, 0:1], t[:, HEAD_DIM:HEAD_DIM + 1]], axis=0)

                q20, q21 = _stack_heads(q0, first_head), _stack_heads(q1, first_head)
                do20, do21 = _stack_heads(do0, first_head), _stack_heads(do1, first_head)
                kcat, vcat = jnp.concatenate([kp, kc], axis=0), jnp.concatenate([vp, vc], axis=0)
                mask0 = _stack_rows(jnp.concatenate([has_prev, cur_mask], axis=1))
                p0 = jnp.where(mask0, jnp.exp(_dot(q20, kcat, "nt") - per_row(lse0)), 0.0)
                ds0 = (p0 * (_dot(do20, vcat, "nt") - per_row(dl0))).astype(BF16)
                p1 = jnp.where(_stack_rows(has_next), jnp.exp(_dot(q21, kc, "nt") - per_row(lse1)), 0.0)
                ds1 = (p1 * (_dot(do21, vc, "nt") - per_row(dl1))).astype(BF16)
                dq2 = _dot(ds0, kcat, "nn")
                dq_s[cur, :] = jnp.where(first_head, dq2[:ATTN_BLOCK], dq2[ATTN_BLOCK:])
                ds_cur = jnp.concatenate([ds0[:, ATTN_BLOCK:], ds1], axis=0)
                p_cur = jnp.concatenate([p0[:, ATTN_BLOCK:], p1], axis=0).astype(BF16)
                dk_s[cur, :] = _dot(ds_cur, jnp.concatenate([q20, q21], axis=0), "tn")
                dv_s[cur, :] = _dot(p_cur, jnp.concatenate([do20, do21], axis=0), "tn")
                return carry

            lax.fori_loop(0, SEQ // ATTN_BLOCK, block, 0, unroll=2)

        for gi in range(N_GROUPS):
            @pl.when(g == gi)
            def _(gi=gi):
                run(DILATIONS[gi])

        dqkv_ref[:, 0:PAIR_W] = _rope_bwd(dq_s[...] * scale, c, sa, sb).astype(BF16)
        dqkv_ref[:, PAIR_W:2 * PAIR_W] = _rope_bwd(dk_s[...], c, sa, sb).astype(BF16)
        dqkv_ref[:, 2 * PAIR_W:] = dv_s[...].astype(BF16)

    slab = pltpu.VMEM((SEQ, PAIR_W), F32)
    return hosted_call(
        body, comm, "attn_bwd", (LOCAL_BATCH, N_PAIRS, N_GROUPS),
        [_slab_spec(0), _slab_spec(1), _slab_spec(2), _TABLE_SPEC, _TABLE_SPEC, _TABLE_SPEC,
         _PAIR_SPEC, _PAIR_SPEC, _PAIR_SPEC],
        [pl.BlockSpec((None, SEQ, 3 * PAIR_W), lambda b, p, g: (b, 0, p * N_GROUPS + g))],
        [jax.ShapeDtypeStruct((LOCAL_BATCH, SEQ, QKV_W), BF16)],
        [slab] * 6, (qkv, qkv, qkv, *tables, dattn, attn, lse), ("parallel", "parallel", "arbitrary"))


def _discretize(lr, li, log_dt, br, bi):
    dt = jnp.exp(log_dt)
    mag = jnp.exp(lr * dt)
    ab_re, ab_im = mag * jnp.cos(li * dt), mag * jnp.sin(li * dt)
    den = lr * lr + li * li
    nr, ni = ab_re - 1.0, ab_im
    f_re = (nr * lr + ni * li) / den
    f_im = (ni * lr - nr * li) / den
    return ab_re, ab_im, f_re[None] * br - f_im[None] * bi, f_re[None] * bi + f_im[None] * br


def ssm_prep(lr, li, log_dt, br, bi):
    def body(lr_ref, li_ref, dt_ref, br_ref, bi_ref, *outs):
        for o, v in zip(outs, _discretize(lr_ref[...], li_ref[...], dt_ref[...], br_ref[...], bi_ref[...])):
            o[...] = v
    shapes = [lr, li, br, bi]
    return pl.pallas_call(body, name="ssm_prep",
                          out_shape=[jax.ShapeDtypeStruct(s.shape, F32) for s in shapes])(lr, li, log_dt, br, bi)


def ssm_prep_bwd(lr, li, log_dt, br, bi, g_ab_re, g_ab_im, g_bb_re, g_bb_im):
    def body(lr_ref, li_ref, dt_ref, br_ref, bi_ref, g0, g1, g2, g3, *outs):
        _, vjp = jax.vjp(_discretize, lr_ref[...], li_ref[...], dt_ref[...], br_ref[...], bi_ref[...])
        for o, v in zip(outs, vjp((g0[...], g1[...], g2[...], g3[...]))):
            o[...] = v
    shapes = [lr, li, log_dt, br, bi]
    return pl.pallas_call(body, name="ssm_prep_bwd",
                          out_shape=[jax.ShapeDtypeStruct(s.shape, F32) for s in shapes])(
        lr, li, log_dt, br, bi, g_ab_re, g_ab_im, g_bb_re, g_bb_im)


def _block_diag(t):
    per = SSM_STATE_W // SSM_LANE_BLOCKS // 64
    g = t.transpose(1, 0, 2).reshape(SSM_LANE_BLOCKS, per, 16, 64)
    eye = jnp.eye(per, dtype=t.dtype)
    return jnp.einsum("jgcn,gh->jgchn", g, eye).reshape(SSM_LANE_BLOCKS, per * 16, per * 64)


def _block_diag_t(m):
    per = SSM_STATE_W // SSM_LANE_BLOCKS // 64
    m5 = m.reshape(SSM_LANE_BLOCKS, per, 16, per, 64)
    d = jnp.einsum("jgchn,gh->jgcn", m5, jnp.eye(per, dtype=m.dtype))
    return d.reshape(SSM_LANE_BLOCKS * per, 16, 64).transpose(1, 0, 2)


def _cmul(ar, ai, br, bi):
    return ar * br - ai * bi, ar * bi + ai * br


def _power_tables(ar, ai, reverse):
    width = ar.shape[1]
    row = lax.broadcasted_iota(jnp.int32, (8, width), 0)
    pows = [(ar, ai)]
    for _ in range(7):
        pows.append(_cmul(pows[-1][0], pows[-1][1], ar, ai))
    steps = []
    for k in (1, 2, 4):
        keep = (row >= k) if not reverse else (row < 8 - k)
        steps.append((jnp.where(keep, pows[k - 1][0], 0.0), jnp.where(keep, pows[k - 1][1], 0.0)))
    cr = jnp.zeros((8, width), F32)
    ci = jnp.zeros((8, width), F32)
    for i in range(8):
        pr, pi = pows[i] if not reverse else pows[7 - i]
        cr = jnp.where(row == i, pr, cr)
        ci = jnp.where(row == i, pi, ci)
    return steps, (cr, ci)


SCAN_CHUNK = 512
STATE_BLOCK = SSM_STATE_W // SSM_LANE_BLOCKS
CHAN_BLOCK = SSM_W // SSM_LANE_BLOCKS


def ssm_fwd(u, ab_re, ab_im, bb_re, bb_im, cb_re, cb_im, d_skip, comm=None):
    nt = SEQ // SCAN_CHUNK
    chan = pl.BlockSpec((None, SCAN_CHUNK, CHAN_BLOCK), lambda b, j, t: (b, t, j))
    state = pl.BlockSpec((None, SCAN_CHUNK, STATE_BLOCK), lambda b, j, t: (b, t, j))
    mat = pl.BlockSpec((None, CHAN_BLOCK, STATE_BLOCK), lambda b, j, t: (j, 0, 0))
    lane = pl.BlockSpec((1, STATE_BLOCK), lambda b, j, t: (0, j))
    dsp = pl.BlockSpec((1, CHAN_BLOCK), lambda b, j, t: (0, j))

    def body(u_ref, ar_ref, ai_ref, bbr_ref, bbi_ref, cbr_ref, cbi_ref, d_ref, y_ref, yg_ref, xr_ref, xi_ref,
             car_r, car_i):
        @pl.when(pl.program_id(2) == 0)
        def _():
            car_r[...] = jnp.zeros_like(car_r)
            car_i[...] = jnp.zeros_like(car_i)

        steps, (pr, pi) = _power_tables(ar_ref[...], ai_ref[...], reverse=False)
        uf = u_ref[...]
        ub = uf.astype(BF16)
        xr_ref[...] = _dot(ub, bbr_ref[...], "nn")
        xi_ref[...] = _dot(ub, bbi_ref[...], "nn")

        def tile(i, carry):
            cr, ci = carry
            sl = pl.ds(pl.multiple_of(i * 8, 8), 8)
            br, bi = xr_ref[sl, :], xi_ref[sl, :]
            for k, (sr, si) in zip((1, 2, 4), steps):
                tr, ti = _cmul(sr, si, pltpu.roll(br, k, 0), pltpu.roll(bi, k, 0))
                br, bi = br + tr, bi + ti
            tr, ti = _cmul(pr, pi, cr, ci)
            br, bi = br + tr, bi + ti
            xr_ref[sl, :] = br
            xi_ref[sl, :] = bi
            return br[7:8, :], bi[7:8, :]

        cr, ci = lax.fori_loop(0, SCAN_CHUNK // 8, tile, (car_r[0:1, :], car_i[0:1, :]), unroll=4)
        car_r[0:1, :] = cr
        car_i[0:1, :] = ci
        y = (_dot(xr_ref[...].astype(BF16), cbr_ref[...], "nt") - _dot(xi_ref[...].astype(BF16), cbi_ref[...], "nt")
             + d_ref[...] * uf)
        y_ref[...] = y
        yg_ref[...] = jax.nn.gelu(y).astype(BF16)

    return hosted_call(
        body, comm, "ssm_fwd", (LOCAL_BATCH, SSM_LANE_BLOCKS, nt),
        [chan, lane, lane, mat, mat, mat, mat, dsp], [chan, chan, state, state],
        [jax.ShapeDtypeStruct((LOCAL_BATCH, SEQ, SSM_W), F32), jax.ShapeDtypeStruct((LOCAL_BATCH, SEQ, SSM_W), BF16),
         jax.ShapeDtypeStruct((LOCAL_BATCH, SEQ, SSM_STATE_W), F32),
         jax.ShapeDtypeStruct((LOCAL_BATCH, SEQ, SSM_STATE_W), F32)],
        [pltpu.VMEM((8, STATE_BLOCK), F32), pltpu.VMEM((8, STATE_BLOCK), F32)],
        (u, ab_re, ab_im, bb_re, bb_im, cb_re, cb_im, d_skip), ("parallel", "parallel", "arbitrary"))


def ssm_bwd(dyg, y, u, xr, xi, ab_re, ab_im, bb_re, bb_im, cb_re, cb_im, d_skip, comm=None):
    nt = SEQ // SCAN_CHUNK
    ntile = SCAN_CHUNK // 8

    def rev(t):
        return nt - 1 - t

    chan = pl.BlockSpec((None, SCAN_CHUNK, CHAN_BLOCK), lambda j, b, t: (b, rev(t), j))
    state = pl.BlockSpec((None, SCAN_CHUNK, STATE_BLOCK), lambda j, b, t: (b, rev(t), j))
    before = pl.BlockSpec((None, 8, STATE_BLOCK), lambda j, b, t: (b, jnp.maximum(rev(t) * ntile - 1, 0), j))
    mat = pl.BlockSpec((None, CHAN_BLOCK, STATE_BLOCK), lambda j, b, t: (j, 0, 0))
    lane = pl.BlockSpec((1, STATE_BLOCK), lambda j, b, t: (0, j))
    lane8 = pl.BlockSpec((8, STATE_BLOCK), lambda j, b, t: (0, j))
    dsp = pl.BlockSpec((1, CHAN_BLOCK), lambda j, b, t: (0, j))

    def body(dyg_ref, y_ref, u_ref, xr_ref, xi_ref, xrb_ref, xib_ref, ar_ref, ai_ref, bbr_ref, bbi_ref, cbr_ref,
             cbi_ref, d_ref, du_ref, dcbr_ref, dcbi_ref, dbbr_ref, dbbi_ref, dd_ref, dar_ref, dai_ref,
             lam_r, lam_i, car_r, car_i):
        b, t = pl.program_id(1), pl.program_id(2)
        first = jnp.logical_and(b == 0, t == 0)

        @pl.when(t == 0)
        def _():
            car_r[...] = jnp.zeros_like(car_r)
            car_i[...] = jnp.zeros_like(car_i)

        @pl.when(first)
        def _():
            for r in (dcbr_ref, dcbi_ref, dbbr_ref, dbbi_ref, dd_ref, dar_ref, dai_ref):
                r[...] = jnp.zeros_like(r)

        steps, (pr, pi) = _power_tables(ar_ref[...], -ai_ref[...], reverse=True)
        uf = u_ref[...]
        _, gelu_vjp = jax.vjp(jax.nn.gelu, y_ref[...])
        dy = gelu_vjp(dyg_ref[...])[0]
        dyb = dy.astype(BF16)
        dd_ref[...] += _colsum(dy * uf)
        lam_r[...] = _dot(dyb, cbr_ref[...], "nn")
        lam_i[...] = -_dot(dyb, cbi_ref[...], "nn")
        dcbr_ref[...] += _dot(dyb, xr_ref[...].astype(BF16), "tn")
        dcbi_ref[...] -= _dot(dyb, xi_ref[...].astype(BF16), "tn")
        row0 = lax.broadcasted_iota(jnp.int32, (8, STATE_BLOCK), 0) == 0
        has_before = rev(t) > 0
        xrb = jnp.where(has_before, xrb_ref[...], 0.0)
        xib = jnp.where(has_before, xib_ref[...], 0.0)

        def tile(s, carry):
            cr, ci, acc_r, acc_i = carry
            i = ntile - 1 - s
            sl = pl.ds(pl.multiple_of(i * 8, 8), 8)
            gr, gi = lam_r[sl, :], lam_i[sl, :]
            for k, (sr, si) in zip((1, 2, 4), steps):
                tr, ti = _cmul(sr, si, pltpu.roll(gr, 8 - k, 0), pltpu.roll(gi, 8 - k, 0))
                gr, gi = gr + tr, gi + ti
            tr, ti = _cmul(pr, pi, cr, ci)
            gr, gi = gr + tr, gi + ti
            lam_r[sl, :] = gr
            lam_i[sl, :] = gi
            sp = pl.ds(pl.multiple_of(jnp.maximum(i - 1, 0) * 8, 8), 8)
            pvr = jnp.where(i > 0, xr_ref[sp, :], xrb)
            pvi = jnp.where(i > 0, xi_ref[sp, :], xib)
            xsr = jnp.where(row0, pltpu.roll(pvr, 1, 0), pltpu.roll(xr_ref[sl, :], 1, 0))
            xsi = jnp.where(row0, pltpu.roll(pvi, 1, 0), pltpu.roll(xi_ref[sl, :], 1, 0))
            acc_r = acc_r + xsr * gr + xsi * gi
            acc_i = acc_i + xsr * gi - xsi * gr
            return gr[0:1, :], gi[0:1, :], acc_r, acc_i

        zero = jnp.zeros((8, STATE_BLOCK), F32)
        cr, ci, acc_r, acc_i = lax.fori_loop(0, ntile, tile, (car_r[0:1, :], car_i[0:1, :], zero, zero), unroll=2)
        car_r[0:1, :] = cr
        car_i[0:1, :] = ci
        dar_ref[...] += acc_r
        dai_ref[...] += acc_i
        lrb, lib = lam_r[...].astype(BF16), lam_i[...].astype(BF16)
        du = _dot(lrb, bbr_ref[...], "nt") + _dot(lib, bbi_ref[...], "nt") + d_ref[...] * dy
        du_ref[...] = du.astype(BF16)
        ub = uf.astype(BF16)
        dbbr_ref[...] += _dot(ub, lrb, "tn")
        dbbi_ref[...] += _dot(ub, lib, "tn")

    mat_shape = jax.ShapeDtypeStruct((SSM_LANE_BLOCKS, CHAN_BLOCK, STATE_BLOCK), F32)
    return hosted_call(
        body, comm, "ssm_bwd", (SSM_LANE_BLOCKS, LOCAL_BATCH, nt),
        [chan, chan, chan, state, state, before, before, lane, lane, mat, mat, mat, mat, dsp],
        [chan, mat, mat, mat, mat, dsp, lane8, lane8],
        [jax.ShapeDtypeStruct((LOCAL_BATCH, SEQ, SSM_W), BF16), mat_shape, mat_shape, mat_shape, mat_shape,
         jax.ShapeDtypeStruct((1, SSM_W), F32), jax.ShapeDtypeStruct((8, SSM_STATE_W), F32),
         jax.ShapeDtypeStruct((8, SSM_STATE_W), F32)],
        [pltpu.VMEM((SCAN_CHUNK, STATE_BLOCK), F32), pltpu.VMEM((SCAN_CHUNK, STATE_BLOCK), F32),
         pltpu.VMEM((8, STATE_BLOCK), F32), pltpu.VMEM((8, STATE_BLOCK), F32)],
        (dyg, y, u, xr, xi, xr, xi, ab_re, ab_im, bb_re, bb_im, cb_re, cb_im, d_skip),
        ("parallel", "arbitrary", "arbitrary"))


def _merge_fn(g0, g1, attn_d, za, zb):
    return jax.nn.sigmoid(g0) * attn_d + jax.nn.sigmoid(g1) * (za * jax.nn.sigmoid(zb))


def _swiglu_fn(a, b):
    return jax.nn.silu(a) * b


def _reduce_start(names, gw, shard_shapes):
    return swap_comm([_to_slots(n, gw[n], shard_shapes[n]) for n in names])


def _reduce_chip(names, swap, got, core):
    return exchange_comm([add_halves(n, g, r, core) for n, g, r in zip(names, swap.ins, got)])


def local_step(x, target, shards, small, core):
    g_mix, g_ffn, g_final = small["norm_mix_g"], small["norm_ffn_g"], small["norm_final_g"]
    tables = _rope_tables()
    seqs = lambda t: t.reshape(LOCAL_BATCH, SEQ, t.shape[-1])
    toks = lambda t: t.reshape(TOKENS, t.shape[-1])
    shard_shapes = {n: s.shape for n, s in shards.items()}
    w = {}

    def gather(names):
        return gather_comm([shards[n] for n in names])

    def arrived(names, slots):
        for n, s in zip(names, slots):
            w[n] = _from_slots(n, s)

    h, *slots = rowwise(lambda xv, g: (_rms(xv, g),), [x, g_mix], [(D_MODEL, BF16)], "norm_mix", comm=gather(["w_in"]))
    arrived(["w_in"], slots)
    w_qkv, w_u, w_gate = _qkv_order(w["w_in"][:QKV_W]), w["w_in"][QKV_W:QKV_W + SSM_W], w["w_in"][QKV_W + SSM_W:]
    qkv, *slots = matmul(h, w_qkv, "nt", F32, "proj_qkv", comm=gather(["w_attn_out", "w_glu"]))
    arrived(["w_attn_out", "w_glu"], slots)
    qkv = seqs(qkv)
    u = seqs(matmul(h, w_u, "nt", F32, "proj_u"))
    gl, *slots = matmul(h, w_gate, "nt", F32, "proj_gate", comm=gather(["w_out"]))
    arrived(["w_out"], slots)
    attn_b, attn, lse, *slots = attn_fwd(qkv, tables, comm=gather(["w_ffn_gate"]))
    arrived(["w_ffn_gate"], slots)
    attn_b = toks(attn_b)
    attn_d = matmul(attn_b, w["w_attn_out"], "nn", F32, "attn_out")

    br_t = small["ssm_b_re"].transpose(2, 0, 1)
    bi_t = small["ssm_b_im"].transpose(2, 0, 1)
    log_dt = small["ssm_log_dt"].reshape(32, 1)
    ab_re, ab_im, bb_re_t, bb_im_t = ssm_prep(small["ssm_a_re"], small["ssm_a_im"], log_dt, br_t, bi_t)
    ab = [ab_re.reshape(1, SSM_STATE_W), ab_im.reshape(1, SSM_STATE_W)]
    bb = [_block_diag(bb_re_t).astype(BF16), _block_diag(bb_im_t).astype(BF16)]
    cb = [_block_diag(small["ssm_c_re"].transpose(1, 0, 2)).astype(BF16),
          _block_diag(small["ssm_c_im"].transpose(1, 0, 2)).astype(BF16)]
    d_skip = small["ssm_d"].reshape(1, SSM_W)
    ffn_rest = ["w_ffn_up", "w_ffn_down"]
    y, yg, xr, xi, *slots = ssm_fwd(u, *ab, *bb, *cb, d_skip, comm=gather(ffn_rest))
    arrived(ffn_rest, slots)
    yg2 = toks(yg)
    z = matmul(yg2, w["w_glu"], "nn", F32, "glu")
    gate_ins = [(gl, D_MODEL, 0), (gl, D_MODEL, 1), attn_d, (z, D_MODEL, 0), (z, D_MODEL, 1)]
    (merged,) = rowwise(lambda *v: (_merge_fn(*v),), gate_ins, [(D_MODEL, BF16)], "merge")
    x1, h2 = matmul_rows(merged, w["w_out"], "out_proj", lambda rows, g: (rows, _rms(rows, g)), [g_ffn],
                         [(D_MODEL, F32), (D_MODEL, BF16)], add=x)
    a, b, act = ffn_in(h2, w["w_ffn_gate"], w["w_ffn_up"])

    def final_fn(xv, g, tgt):
        yv, vjp = jax.vjp(_rms, xv, g)
        err = yv - tgt
        dx, dg = vjp(err * (1.0 / D_MODEL))
        loss = 0.5 * jnp.sum(jnp.mean(err * err, axis=-1, keepdims=True), axis=0, keepdims=True)
        return dx, dx, dg, jnp.broadcast_to(loss, (1, LANES))

    dx2, dx2_b, dg_final, loss = matmul_rows(act, w["w_ffn_down"], "ffn_down_loss", final_fn, [g_final, target],
                                             [(D_MODEL, F32), (D_MODEL, BF16)], accs=(D_MODEL, LANES), add=x1)
    gw, parts = {}, {}
    gw["w_ffn_down"] = matmul(act, dx2_b, "tn", F32, "d_ffn_down")
    da_b, db_b = ffn_in_bwd(dx2_b, w["w_ffn_down"], a, b)
    gw["w_ffn_gate"] = matmul(da_b, h2, "tn", F32, "d_ffn_gate")
    gw["w_ffn_up"] = matmul(db_b, h2, "tn", F32, "d_ffn_up")
    ffn = ["w_ffn_down", "w_ffn_gate", "w_ffn_up"]
    swap = _reduce_start(ffn[:2], gw, shard_shapes)
    dh2, *got = matmul(da_b, w["w_ffn_gate"], "nn", F32, "d_h2_gate", comm=swap)
    ffn_exchange = [_reduce_chip(ffn[:2], swap, got, core)]
    swap = _reduce_start(ffn[2:], gw, shard_shapes)

    def norm_bwd(dh, xv, g, skip):
        _, vjp = jax.vjp(_rms, xv, g)
        dx, dg = vjp(dh)
        dx = dx + skip
        return dx, dx, dg

    dx1, dx1_b, dg_ffn, *got = matmul_rows(db_b, w["w_ffn_up"], "d_h2_up_norm", norm_bwd, [x1, g_ffn, dx2],
                                           [(D_MODEL, F32), (D_MODEL, BF16)], accs=(D_MODEL,), add=dh2, comm=swap)
    ffn_up_exchange = _reduce_chip(ffn[2:], swap, got, core)
    gw["w_out"] = matmul(merged, dx1_b, "tn", F32, "d_out")
    dmerged = matmul(dx1_b, w["w_out"], "nt", F32, "d_merged")

    def merge_bwd(g0, g1, ad, za, zb, dm):
        _, vjp = jax.vjp(_merge_fn, g0, g1, ad, za, zb)
        dg0, dg1, dad, dza, dzb = vjp(dm)
        return jnp.concatenate([dg0, dg1], axis=1), dad, jnp.concatenate([dza, dzb], axis=1)

    dgl_b, dattn_d_b, dz_b, parts["w_ffn_up"] = rowwise(
        merge_bwd, gate_ins + [dmerged], [(GATE_W, BF16), (D_MODEL, BF16), (GATE_W, BF16)], "merge_bwd",
        comm=ffn_up_exchange)
    gw["w_attn_out"] = matmul(attn_b, dattn_d_b, "tn", F32, "d_attn_out")
    dattn = seqs(matmul(dattn_d_b, w["w_attn_out"], "nt", F32, "d_attn"))
    gw["w_glu"] = matmul(yg2, dz_b, "tn", F32, "d_glu")
    dyg = seqs(matmul(dz_b, w["w_glu"], "nt", F32, "d_yg"))
    mixer = ["w_out", "w_attn_out", "w_glu"]
    swap = _reduce_start(mixer, gw, shard_shapes)
    du_b, dcb_re, dcb_im, dbb_re, dbb_im, dd, da_re8, da_im8, *rest = ssm_bwd(
        dyg, y, u, xr, xi, *ab, *bb, *cb, d_skip, comm=join_comms(ffn_exchange + [swap]))
    for n, p in zip(ffn[:2], rest[:2]):
        parts[n] = p
    mixer_exchange = _reduce_chip(mixer, swap, rest[2:], core)
    du_b = toks(du_b)
    g_ab_re = jnp.sum(da_re8, axis=0).reshape(32, 64)
    g_ab_im = jnp.sum(da_im8, axis=0).reshape(32, 64)
    d_lr, d_li, d_ldt, d_br_t, d_bi_t = ssm_prep_bwd(
        small["ssm_a_re"], small["ssm_a_im"], log_dt, br_t, bi_t,
        g_ab_re, g_ab_im, _block_diag_t(dbb_re), _block_diag_t(dbb_im))
    gs = {
        "ssm_a_re": d_lr, "ssm_a_im": d_li, "ssm_log_dt": d_ldt.reshape(1, 32),
        "ssm_b_re": d_br_t.transpose(1, 2, 0), "ssm_b_im": d_bi_t.transpose(1, 2, 0),
        "ssm_c_re": _block_diag_t(dcb_re).transpose(1, 0, 2), "ssm_c_im": _block_diag_t(dcb_im).transpose(1, 0, 2),
        "ssm_d": dd.reshape(32, 16),
    }
    ssm_gather = small_comm(_pack_small(SSM_SMALL, gs, SSM_ROWS))
    dqkv_b, *rest = attn_bwd(qkv, tables, dattn, attn, lse, comm=join_comms([mixer_exchange, ssm_gather]))
    for n, p in zip(mixer, rest):
        parts[n] = p
    ssm_shares = rest[len(mixer)]
    dqkv_b = toks(dqkv_b)
    d_qkv = matmul(dqkv_b, h, "tn", F32, "d_w_qkv")
    d_u = matmul(du_b, h, "tn", F32, "d_w_u")
    d_gate = matmul(dgl_b, h, "tn", F32, "d_w_gate")
    gw["w_in"] = jnp.concatenate([_qkv_order(d_qkv, back=True), d_u, d_gate], axis=0)
    swap = _reduce_start(["w_in"], gw, shard_shapes)
    dh, *got = matmul(dqkv_b, w_qkv, "nn", F32, "d_h_qkv", comm=swap)
    w_in_exchange = _reduce_chip(["w_in"], swap, got, core)
    grad_x, dg_mix, parts["w_in"] = mix_in_bwd([du_b, dgl_b], [w_u, w_gate], dh, x, g_mix, dx1, comm=w_in_exchange)
    gs_norm = {"norm_mix_g": dg_mix, "norm_ffn_g": dg_ffn, "norm_final_g": dg_final}
    return loss, grad_x, parts, ssm_shares, gs_norm


ANY = pl.BlockSpec(memory_space=pl.ANY)
BIG = ("w_in", "w_glu", "w_attn_out", "w_out", "w_ffn_gate", "w_ffn_up", "w_ffn_down")
TRANSPOSED = ("w_in", "w_ffn_gate", "w_ffn_up")
ROW_SHARDED = TRANSPOSED + ("w_out", "w_ffn_down")
SMALL = ("norm_mix_g", "ssm_a_re", "ssm_a_im", "ssm_log_dt", "ssm_b_re", "ssm_b_im", "ssm_c_re", "ssm_c_im",
         "ssm_d", "norm_ffn_g", "norm_final_g")
WEIGHTS = ("norm_mix_g", "w_in", "ssm_a_re", "ssm_a_im", "ssm_log_dt", "ssm_b_re", "ssm_b_im", "ssm_c_re",
           "ssm_c_im", "ssm_d", "w_glu", "w_attn_out", "w_out", "norm_ffn_g", "w_ffn_gate", "w_ffn_up",
           "w_ffn_down", "norm_final_g")
SSM_SMALL = SMALL[1:9]
NORM_SMALL = (SMALL[0],) + SMALL[9:]
SSM_ROWS, NORM_ROWS = 1064, 32
N_BIG = len(BIG)


def _position():
    return lax.axis_index("x"), lax.axis_index("y"), lax.axis_index("c")


def _other_chips(x, y):
    return [(1 - x, y), (x, 1 - y), (1 - x, 1 - y)]


def _remote(src, dst, send_sem, recv_sem, device):
    return pltpu.make_async_remote_copy(src_ref=src, dst_ref=dst, send_sem=send_sem, recv_sem=recv_sem,
                                        device_id=device, device_id_type=MESH)


_later = functools.partial


def _two_level_phases(copies):
    def first(*refs):
        locals_, sends, _, _, _ = copies(*refs)
        for cp in locals_ + sends:
            cp().start()

    def mid(*refs):
        _, _, arrived, passed, _ = copies(*refs)
        for got, cp in zip(arrived, passed):
            got().wait_recv()
            cp().start()

    def last(*refs):
        locals_, sends, _, passed, from_sibling = copies(*refs)
        for cp in from_sibling:
            cp().wait_recv()
        for cp in sends + passed:
            cp().wait_send()
        for cp in locals_:
            cp().wait()

    return first, mid, last


def _half(ref, chip, which):
    rows = ref.shape[1] // 2
    return ref.at[chip, pl.ds(which * rows, rows), :]


class Comm:
    def __init__(self, ins, out_shapes, sems, first, mid, last):
        self.ins, self.out_shapes, self.sems = list(ins), list(out_shapes), list(sems)
        self.first, self.mid, self.last = first, mid, last


def join_comms(comms):
    def cut(refs_by_kind):
        offs, parts = [0, 0, 0], []
        for cm in comms:
            sizes = (len(cm.ins), len(cm.out_shapes), len(cm.sems))
            parts.append(tuple(refs_by_kind[k][offs[k]:offs[k] + sizes[k]] for k in range(3)))
            offs = [o + s for o, s in zip(offs, sizes)]
        return parts

    def phase(which):
        def run(ins, outs, sems):
            for cm, part in zip(comms, cut((ins, outs, sems))):
                fn = getattr(cm, which)
                if fn is not None:
                    fn(*part)
        return run

    return Comm(sum((cm.ins for cm in comms), []), sum((cm.out_shapes for cm in comms), []),
                sum((cm.sems for cm in comms), []), phase("first"), phase("mid"), phase("last"))


def _comm_operands(comm):
    if comm is None:
        return [], [], []
    return comm.ins, comm.out_shapes, comm.sems


def _comm_begin(comm, refs, step, n_steps):
    if comm is None:
        return
    pl.when(step == 0)(lambda: comm.first(*refs))
    if comm.mid is not None:
        pl.when(step == (n_steps * 3) // 4)(lambda: comm.mid(*refs))


def _comm_end(comm, refs, step, n_steps):
    if comm is not None:
        pl.when(step == n_steps - 1)(lambda: comm.last(*refs))


def _comm_refs(comm, refs, n_in, n_out):
    if comm is None:
        return list(refs), None
    ci, co, cs = len(comm.ins), len(comm.out_shapes), len(comm.sems)
    o0 = n_in + ci
    s0 = o0 + n_out + co
    host = list(refs[:n_in]) + list(refs[o0:o0 + n_out]) + list(refs[s0:len(refs) - cs])
    return host, (list(refs[n_in:o0]), list(refs[o0 + n_out:s0]), list(refs[len(refs) - cs:]))


def run_comm(comm, name):
    n_in, n_out = len(comm.ins), len(comm.out_shapes)

    def body(*refs):
        parts = (list(refs[:n_in]), list(refs[n_in:n_in + n_out]), list(refs[n_in + n_out:]))
        comm.first(*parts)
        if comm.mid is not None:
            comm.mid(*parts)
        comm.last(*parts)

    return pl.pallas_call(body, name=name, in_specs=[ANY] * n_in, out_specs=[ANY] * n_out,
                          out_shape=comm.out_shapes, scratch_shapes=comm.sems)(*comm.ins)


def hosted_call(work, comm, name, grid, in_specs, out_specs, out_shape, scratch_shapes, args, semantics):
    c_ins, c_outs, c_sems = _comm_operands(comm)
    n_steps = math.prod(grid)

    def body(*refs):
        host, c_refs = _comm_refs(comm, refs, len(in_specs), len(out_specs))
        step = 0
        for axis, size in enumerate(grid):
            step = step * size + pl.program_id(axis)
        _comm_begin(comm, c_refs, step, n_steps)
        work(*host)
        _comm_end(comm, c_refs, step, n_steps)

    return pl.pallas_call(
        body, name=name, grid=grid, in_specs=list(in_specs) + [ANY] * len(c_ins),
        out_specs=list(out_specs) + [ANY] * len(c_outs), out_shape=list(out_shape) + c_outs,
        scratch_shapes=list(scratch_shapes) + c_sems,
        compiler_params=_params(semantics if comm is None else ("arbitrary",) * len(grid)),
    )(*args, *c_ins)


def gather_comm(shards):
    n = len(shards)

    def copies(srcs, outs, sems):
        send_sems, recv_sems, local_sems = sems
        x, y, c = _position()
        me = 2 * x + y
        sibling = (x, y, 1 - c)
        chips = _other_chips(x, y)
        locals_ = [_later(pltpu.make_async_copy, s, o.at[me], local_sems.at[i])
                   for i, (s, o) in enumerate(zip(srcs, outs))]
        sends, arrived, passed, from_sibling = [], [], [], []
        for j, (px, py) in enumerate(chips):
            for i, (s, o) in enumerate(zip(srcs, outs)):
                rows = s.shape[0] // 2
                sends.append(_later(_remote, s.at[pl.ds(c * rows, rows), :], _half(o, me, c), send_sems.at[i, j],
                                    recv_sems.at[i, j], (px, py, c)))
                got = _half(o, 2 * px + py, c)
                arrived.append(_later(_remote, got, got, send_sems.at[i, j], recv_sems.at[i, j], (px, py, c)))
                passed.append(_later(_remote, got, got, send_sems.at[i, 3 + j], recv_sems.at[i, 3 + j], sibling))
                other = _half(o, 2 * px + py, 1 - c)
                from_sibling.append(_later(_remote, other, other, send_sems.at[i, 3 + j], recv_sems.at[i, 3 + j],
                                           sibling))
        return locals_, sends, arrived, passed, from_sibling

    return Comm(shards, [jax.ShapeDtypeStruct((N_CHIPS,) + s.shape, s.dtype) for s in shards],
                [pltpu.SemaphoreType.DMA((n, 6)), pltpu.SemaphoreType.DMA((n, 6)), pltpu.SemaphoreType.DMA((n,))],
                *_two_level_phases(copies))


def swap_comm(grads):
    n = len(grads)

    def copies(srcs, gots, sems):
        send_sems, recv_sems = sems
        x, y, c = _position()
        out = []
        for i, (s, o) in enumerate(zip(srcs, gots)):
            rows = s.shape[1] // 2
            out.append(_remote(s.at[:, pl.ds((1 - c) * rows, rows), :], o, send_sems.at[i], recv_sems.at[i],
                               (x, y, 1 - c)))
        return out

    def first(srcs, gots, sems):
        for cp in copies(srcs, gots, sems):
            cp.start()

    def last(srcs, gots, sems):
        for cp in copies(srcs, gots, sems):
            cp.wait()

    return Comm(grads, [jax.ShapeDtypeStruct((N_CHIPS, g.shape[1] // 2, g.shape[2]), g.dtype) for g in grads],
                [pltpu.SemaphoreType.DMA((n,)), pltpu.SemaphoreType.DMA((n,))], first, None, last)


def add_halves(name, g, got, core):
    _, half, cols = got.shape
    mine = pl.BlockSpec((None, half, cols), lambda k, c_ref: (k, c_ref[0], 0))
    other = pl.BlockSpec((None, half, cols), lambda k, c_ref: (k, 0, 0))

    def body(c_ref, g_ref, got_ref, o_ref):
        o_ref[...] = (g_ref[...] + got_ref[...]).astype(BF16)

    return pl.pallas_call(
        body, name="add_halves_" + name,
        grid_spec=pltpu.PrefetchScalarGridSpec(num_scalar_prefetch=1, grid=(N_CHIPS,), in_specs=[mine, other],
                                               out_specs=other),
        out_shape=jax.ShapeDtypeStruct(got.shape, BF16),
        compiler_params=_params(("parallel",)),
    )(core, g, got)


def exchange_comm(parts):
    n = len(parts)

    def copies(srcs, outs, sems):
        send_sems, recv_sems, local_sems = sems
        x, y, c = _position()
        me = 2 * x + y
        sibling = (x, y, 1 - c)
        chips = _other_chips(x, y)
        locals_, sends, arrived, passed, from_sibling = [], [], [], [], []
        for i, (s, o) in enumerate(zip(srcs, outs)):
            locals_.append(_later(pltpu.make_async_copy, s.at[me], _half(o, me, c), local_sems.at[i]))
            sends.append(_later(_remote, s.at[me], _half(o, me, c), send_sems.at[i, 3], recv_sems.at[i, 3], sibling))
            other = _half(o, me, 1 - c)
            from_sibling.append(_later(_remote, other, other, send_sems.at[i, 3], recv_sems.at[i, 3], sibling))
        for j, (px, py) in enumerate(chips):
            for i, (s, o) in enumerate(zip(srcs, outs)):
                sends.append(_later(_remote, s.at[2 * px + py], _half(o, me, c), send_sems.at[i, j],
                                    recv_sems.at[i, j], (px, py, c)))
                got = _half(o, 2 * px + py, c)
                arrived.append(_later(_remote, got, got, send_sems.at[i, j], recv_sems.at[i, j], (px, py, c)))
                passed.append(_later(_remote, got, got, send_sems.at[i, 4 + j], recv_sems.at[i, 4 + j], sibling))
                other = _half(o, 2 * px + py, 1 - c)
                from_sibling.append(_later(_remote, other, other, send_sems.at[i, 4 + j], recv_sems.at[i, 4 + j],
                                           sibling))
        return locals_, sends, arrived, passed, from_sibling

    return Comm(parts, [jax.ShapeDtypeStruct((N_CHIPS, 2 * p.shape[1], p.shape[2]), p.dtype) for p in parts],
                [pltpu.SemaphoreType.DMA((n, 7)), pltpu.SemaphoreType.DMA((n, 7)), pltpu.SemaphoreType.DMA((n,))],
                *_two_level_phases(copies))


def small_comm(pack):
    def copies(srcs, outs, sems):
        (src_ref,), (out_ref,), (send_sems, recv_sems, local_sem) = srcs, outs, sems
        x, y, c = _position()
        me = 4 * x + 2 * y + c
        flips = [(fx, fy, fc) for fx in (0, 1) for fy in (0, 1) for fc in (0, 1)][1:]
        peers = [(1 - x if fx else x, 1 - y if fy else y, 1 - c if fc else c) for fx, fy, fc in flips]
        local = _later(pltpu.make_async_copy, src_ref, out_ref.at[me], local_sem)
        sends = [_later(_remote, src_ref, out_ref.at[me], send_sems.at[j], recv_sems.at[j], peer)
                 for j, peer in enumerate(peers)]
        arrived = []
        for j, (px, py, pc) in enumerate(peers):
            got = out_ref.at[4 * px + 2 * py + pc]
            arrived.append(_later(_remote, got, got, send_sems.at[j], recv_sems.at[j], (px, py, pc)))
        return local, sends, arrived

    def first(*refs):
        local, sends, _ = copies(*refs)
        for cp in [local] + sends:
            cp().start()

    def last(*refs):
        local, sends, arrived = copies(*refs)
        for cp in arrived:
            cp().wait_recv()
        for cp in sends:
            cp().wait_send()
        local().wait()

    return Comm([pack], [jax.ShapeDtypeStruct((N_DEV,) + pack.shape, pack.dtype)],
                [pltpu.SemaphoreType.DMA((7,)), pltpu.SemaphoreType.DMA((7,)), pltpu.SemaphoreType.DMA],
                first, None, last)


def _adam_fn(w, g, m, v):
    m = ADAM_B1 * m + (1.0 - ADAM_B1) * g
    v = ADAM_B2 * v + (1.0 - ADAM_B2) * jnp.square(g)
    m_hat = m / (1.0 - ADAM_B1 ** ADAM_STEP)
    v_hat = v / (1.0 - ADAM_B2 ** ADAM_STEP)
    return -ADAM_LR * (m_hat / (jnp.sqrt(v_hat) + ADAM_EPS) + ADAM_WD * w), m, v


def adam_big(name, parts, w, m, v):
    rows, cols = w.shape
    tm = _pick(rows, 384, 16)

    def fn(p0, p1, p2, p3, wv, mv, vv):
        g = ((p0.astype(F32) + p1.astype(F32)) + p2.astype(F32)) + p3.astype(F32)
        return (g,) + _adam_fn(wv, g, mv, vv)

    return rowwise(fn, [parts, w, m, v], [(cols, F32)] * 4, "adam_" + name, tm=tm, rows=rows)


def adam_small(name, gathered, w, m, v):
    def body(g_ref, w_ref, m_ref, v_ref, go_ref, d_ref, mo_ref, vo_ref):
        g = g_ref[0]
        for k in range(1, N_DEV):
            g = g + g_ref[k]
        go_ref[...] = g
        d_ref[...], mo_ref[...], vo_ref[...] = _adam_fn(w_ref[...], g, m_ref[...], v_ref[...])

    return pl.pallas_call(body, name=name, out_shape=[jax.ShapeDtypeStruct(w.shape, F32)] * 4,
                          compiler_params=_params())(gathered, w, m, v)


def _pack_small(names, vals, rows, last=None):
    flat = [vals[n].reshape(-1) for n in names]
    if last is not None:
        flat.append(last.reshape(-1))
    flat = jnp.concatenate(flat)
    return jnp.pad(flat, (0, rows * LANES - flat.shape[0])).reshape(rows, LANES)


def _unpack_small(names, pack, shapes):
    flat, out, off = pack.reshape(-1), {}, 0
    for n in names:
        size = math.prod(shapes[n])
        out[n] = flat[off:off + size].reshape(shapes[n])
        off += size
    return out, flat[off]


def _to_slots(name, g, shard_shape):
    rows, cols = shard_shape
    if name in ROW_SHARDED:
        return g.reshape(N_CHIPS, rows, cols)
    return g.reshape(rows, N_CHIPS, cols).transpose(1, 0, 2)


def _from_slots(name, s):
    _, rows, cols = s.shape
    if name in ROW_SHARDED:
        return s.reshape(N_CHIPS * rows, cols)
    return s.transpose(1, 0, 2).reshape(rows, N_CHIPS * cols)


def kernel(x, norm_mix_g, w_in, ssm_a_re, ssm_a_im, ssm_log_dt, ssm_b_re, ssm_b_im, ssm_c_re, ssm_c_im, ssm_d, w_glu, w_attn_out, w_out, norm_ffn_g, w_ffn_gate, w_ffn_up, w_ffn_down, norm_final_g, loss_target, m_norm_mix_g, m_w_in, m_ssm_a_re, m_ssm_a_im, m_ssm_log_dt, m_ssm_b_re, m_ssm_b_im, m_ssm_c_re, m_ssm_c_im, m_ssm_d, m_w_glu, m_w_attn_out, m_w_out, m_norm_ffn_g, m_w_ffn_gate, m_w_ffn_up, m_w_ffn_down, m_norm_final_g, v_norm_mix_g, v_w_in, v_ssm_a_re, v_ssm_a_im, v_ssm_log_dt, v_ssm_b_re, v_ssm_b_im, v_ssm_c_re, v_ssm_c_im, v_ssm_d, v_w_glu, v_w_attn_out, v_w_out, v_norm_ffn_g, v_w_ffn_gate, v_w_ffn_up, v_w_ffn_down, v_norm_final_g):
    given = dict(locals())
    def local(name, prefix=""):
        t = given[prefix + name][0]
        return t.T if name in TRANSPOSED else t

    shard = {n: local(n) for n in BIG}
    shapes = {n: given[n].shape for n in WEIGHTS}

    small = {n: given[n] for n in SMALL}
    small_2d = dict(small)
    for n in ("ssm_a_re", "ssm_a_im", "ssm_b_re", "ssm_b_im", "ssm_c_re", "ssm_c_im", "ssm_d"):
        small_2d[n] = small[n][0]
    small_2d["norm_final_g"] = norm_final_g.reshape(1, D_MODEL)

    core = lax.axis_index("c").astype(jnp.int32).reshape(1)
    loss, grad_x, parts, ssm_shares, gs_norm = local_step(
        x.reshape(TOKENS, D_MODEL), loss_target.reshape(TOKENS, D_MODEL),
        {n: shard[n].astype(BF16) for n in BIG}, small_2d, core)

    (norm_shares,) = run_comm(small_comm(_pack_small(NORM_SMALL, gs_norm, NORM_ROWS, last=loss)), "gather_norm_grads")
    small_out = [{} for _ in range(4)]
    for names, rows, shares in ((SSM_SMALL, SSM_ROWS, ssm_shares), (NORM_SMALL, NORM_ROWS, norm_shares)):
        packs = [_pack_small(names, {n: given[p + n] for n in names}, rows) for p in ("", "m_", "v_")]
        for kind, t in enumerate(adam_small("adam_" + names[0], shares, *packs)):
            vals, after = _unpack_small(names, t, shapes)
            small_out[kind].update(vals)
            if kind == 0:
                total_loss = after

    big_out = {}
    for n in BIG:
        res = adam_big(n, parts[n], shard[n], local(n, "m_"), local(n, "v_"))
        big_out[n] = [(t.T if n in TRANSPOSED else t)[None] for t in res]

    outs = [total_loss, grad_x.reshape(LOCAL_BATCH, SEQ, D_MODEL)]
    for kind in range(4):
        for n in WEIGHTS:
            outs.append(big_out[n][kind] if n in BIG else small_out[kind][n])
    return tuple(outs)
```

```python
import functools
import math

import jax
import jax.numpy as jnp
from jax import lax
from jax.experimental import pallas as pl
from jax.experimental.pallas import tpu as pltpu

F32 = jnp.float32
BF16 = jnp.bfloat16
MESH = pl.DeviceIdType.MESH

D_MODEL = 1024
SEQ = 2048
LOCAL_BATCH = 2
TOKENS = LOCAL_BATCH * SEQ
HEAD_DIM = 64
HEADS_PER_GROUP = 4
GROUP_W = HEADS_PER_GROUP * HEAD_DIM
N_GROUPS = 3
DILATIONS = (1, 4, 16)
ATTN_BLOCK = 128
ROPE_DIM = 16
ROPE_THETA = 500000.0
QKV_W = 3 * N_GROUPS * GROUP_W
SSM_W = 512
SSM_STATE_W = 2048
SSM_LANE_BLOCKS = 4
GATE_W = 2 * D_MODEL
D_FF = 2816
RMS_EPS = 1e-6
NEG_INF = -1e30
ADAM_LR, ADAM_B1, ADAM_B2, ADAM_EPS, ADAM_WD, ADAM_STEP = 0.001, 0.9, 0.999, 1e-08, 0.01, 10
N_CHIPS = 4
N_DEV = 8

VMEM_LIMIT = 56 * 1024 * 1024
LANES = 128


def _params(sem=None):
    return pltpu.CompilerParams(dimension_semantics=sem, vmem_limit_bytes=VMEM_LIMIT)


def _pick(n, cap, align=LANES):
    best = None
    for d in range(align, min(n, cap) + 1, align):
        if n % d == 0:
            best = d
    return n if best is None or n <= cap else best


_DIMS = {"nn": (((1,), (0,)), ((), ())), "nt": (((1,), (1,)), ((), ())), "tn": (((0,), (0,)), ((), ()))}


def _dot(a, b, mode):
    return lax.dot_general(a, b, _DIMS[mode], preferred_element_type=F32)


def matmul(a, b, mode, out_dtype, name, add=None, comm=None):
    if mode == "nn":
        (m, k), n = a.shape, b.shape[1]
    elif mode == "nt":
        (m, k), n = a.shape, b.shape[0]
    else:
        (k, m), n = a.shape, b.shape[1]
    tn = _pick(n, 1408 if mode != "tn" else 512)
    tk = _pick(k, 2816) if mode != "tn" else k
    tm = _pick(m, 1408)
    out_bytes = jnp.dtype(out_dtype).itemsize

    def need(tm_):
        return 2 * 2 * (tm_ * tk + tk * tn) + tm_ * tn * (4 + 2 * out_bytes + (8 if add is not None else 0))

    while need(tm) > 40 * 1024 * 1024 and tm % 256 == 0:
        tm //= 2
    nk = k // tk
    a_spec = {"nn": pl.BlockSpec((tm, tk), lambda i, j, kk: (i, kk)),
              "nt": pl.BlockSpec((tm, tk), lambda i, j, kk: (i, kk)),
              "tn": pl.BlockSpec((tk, tm), lambda i, j, kk: (kk, i))}[mode]
    b_spec = {"nn": pl.BlockSpec((tk, tn), lambda i, j, kk: (kk, j)),
              "nt": pl.BlockSpec((tn, tk), lambda i, j, kk: (j, kk)),
              "tn": pl.BlockSpec((tk, tn), lambda i, j, kk: (kk, j))}[mode]
    o_spec = pl.BlockSpec((tm, tn), lambda i, j, kk: (i, j))

    def body(a_ref, b_ref, *rest):
        if add is not None:
            add_ref, o_ref, acc_ref = rest
        else:
            o_ref, acc_ref = rest
        part = _dot(a_ref[...], b_ref[...], mode)
        if nk == 1:
            res = part if add is None else part + add_ref[...]
            o_ref[...] = res.astype(out_dtype)
            return
        kk = pl.program_id(2)

        @pl.when(kk == 0)
        def _():
            acc_ref[...] = part

        @pl.when(kk > 0)
        def _():
            acc_ref[...] += part

        @pl.when(kk == nk - 1)
        def _():
            res = acc_ref[...] if add is None else acc_ref[...] + add_ref[...]
            o_ref[...] = res.astype(out_dtype)

    in_specs = [a_spec, b_spec] + ([o_spec] if add is not None else [])
    args = (a, b) + ((add,) if add is not None else ())
    res = hosted_call(
        body, comm, name, (m // tm, n // tn, nk), in_specs, [o_spec], [jax.ShapeDtypeStruct((m, n), out_dtype)],
        [pltpu.VMEM((tm, tn) if nk > 1 else (8, LANES), F32)], args, ("parallel", "parallel", "arbitrary"))
    return res[0] if comm is None else res


def matmul_rows(a, b, name, fn, extra, outs, accs=(), add=None, comm=None, tm=512):
    (m, k), n = a.shape, b.shape[1]
    n_fixed = 2 + (add is not None)
    row_spec = lambda cols: pl.BlockSpec((tm, cols), lambda i: (i, 0))
    in_specs = [row_spec(k), pl.BlockSpec((k, n), lambda i: (0, 0))] + ([row_spec(n)] if add is not None else [])
    in_specs += [pl.BlockSpec(e.shape, lambda i: (0, 0)) if e.shape[0] == 1 else row_spec(e.shape[1]) for e in extra]
    out_specs = [row_spec(c) for c, _ in outs] + [pl.BlockSpec((1, c), lambda i: (0, 0)) for c in accs]
    out_shape = [jax.ShapeDtypeStruct((m, c), dt) for c, dt in outs] + [jax.ShapeDtypeStruct((1, c), F32) for c in accs]

    def body(*refs):
        rows = _dot(refs[0][...], refs[1][...], "nn")
        if add is not None:
            rows = rows + refs[2][...]
        n_in = n_fixed + len(extra)
        res = fn(rows, *[r[...] for r in refs[n_fixed:n_in]])
        for r, v in zip(refs[n_in:n_in + len(outs)], res[:len(outs)]):
            r[...] = v.astype(r.dtype)
        first = pl.program_id(0) == 0
        for r, v in zip(refs[n_in + len(outs):], res[len(outs):]):
            @pl.when(first)
            def _(r=r, v=v):
                r[...] = v

            @pl.when(jnp.logical_not(first))
            def _(r=r, v=v):
                r[...] += v

    args = (a, b) + ((add,) if add is not None else ()) + tuple(extra)
    return hosted_call(body, comm, name, (m // tm,), in_specs, out_specs, out_shape, [], args, ("arbitrary",))


FFN_TM, FFN_TN = 512, 1408


def ffn_in(h2, wg_t, wu_t):
    def body(h_ref, wg_ref, wu_ref, a_ref, b_ref, act_ref):
        hv = h_ref[...]
        a, b = _dot(hv, wg_ref[...], "nt"), _dot(hv, wu_ref[...], "nt")
        a_ref[...] = a.astype(BF16)
        b_ref[...] = b.astype(BF16)
        act_ref[...] = _swiglu_fn(a, b).astype(BF16)

    rows = pl.BlockSpec((FFN_TM, D_MODEL), lambda i, j: (i, 0))
    wts = pl.BlockSpec((FFN_TN, D_MODEL), lambda i, j: (j, 0))
    out = pl.BlockSpec((FFN_TM, FFN_TN), lambda i, j: (i, j))
    return pl.pallas_call(
        body, name="ffn_in", grid=(TOKENS // FFN_TM, D_FF // FFN_TN), in_specs=[rows, wts, wts],
        out_specs=[out] * 3, out_shape=[jax.ShapeDtypeStruct((TOKENS, D_FF), BF16)] * 3,
        compiler_params=_params(("parallel", "parallel")),
    )(h2, wg_t, wu_t)


def ffn_in_bwd(dx2_b, wd, a, b):
    def body(dx_ref, wd_ref, a_ref, b_ref, da_ref, db_ref):
        dx = dx_ref[...]
        for lo in range(0, FFN_TN, 512):
            cols = slice(lo, min(lo + 512, FFN_TN))
            dact = _dot(dx, wd_ref[cols, :], "nt")
            _, vjp = jax.vjp(_swiglu_fn, a_ref[:, cols].astype(F32), b_ref[:, cols].astype(F32))
            da, db = vjp(dact)
            da_ref[:, cols] = da.astype(BF16)
            db_ref[:, cols] = db.astype(BF16)

    rows = pl.BlockSpec((FFN_TM, D_MODEL), lambda i, j: (i, 0))
    wts = pl.BlockSpec((FFN_TN, D_MODEL), lambda i, j: (j, 0))
    out = pl.BlockSpec((FFN_TM, FFN_TN), lambda i, j: (i, j))
    return pl.pallas_call(
        body, name="ffn_in_bwd", grid=(TOKENS // FFN_TM, D_FF // FFN_TN), in_specs=[rows, wts, out, out],
        out_specs=[out] * 2, out_shape=[jax.ShapeDtypeStruct((TOKENS, D_FF), BF16)] * 2,
        compiler_params=_params(("parallel", "parallel")),
    )(dx2_b, wd, a, b)


def mix_in_bwd(grads, weights, partial, x, g, skip, comm=None):
    n = len(grads)
    tm = 512

    def body(*refs):
        a_refs, b_refs = refs[:n], refs[n:2 * n]
        part_ref, x_ref, g_ref, skip_ref, gx_ref, dg_ref = refs[2 * n:]
        dh = part_ref[...]
        for a_ref, b_ref in zip(a_refs, b_refs):
            dh = dh + _dot(a_ref[...], b_ref[...], "nn")
        _, vjp = jax.vjp(_rms, x_ref[...], g_ref[...])
        dx, dg = vjp(dh)
        gx_ref[...] = dx + skip_ref[...]
        first = pl.program_id(0) == 0

        @pl.when(first)
        def _():
            dg_ref[...] = dg

        @pl.when(jnp.logical_not(first))
        def _():
            dg_ref[...] += dg

    rows = pl.BlockSpec((tm, D_MODEL), lambda i: (i, 0))
    gain = pl.BlockSpec((1, D_MODEL), lambda i: (0, 0))
    in_specs = [pl.BlockSpec((tm, a.shape[1]), lambda i: (i, 0)) for a in grads]
    in_specs += [pl.BlockSpec(b.shape, lambda i: (0, 0)) for b in weights]
    return hosted_call(
        body, comm, "mix_in_bwd", (TOKENS // tm,), in_specs + [rows, rows, gain, rows], [rows, gain],
        [jax.ShapeDtypeStruct((TOKENS, D_MODEL), F32), jax.ShapeDtypeStruct((1, D_MODEL), F32)], [],
        (*grads, *weights, partial, x, g, skip), ("arbitrary",))


def rowwise(fn, ins, outs, name, accs=(), tm=256, rows=TOKENS, comm=None):
    in_specs, args = [], []
    for item in ins:
        arr, width, blk = item if isinstance(item, tuple) else (item, None, 0)
        if arr.ndim == 3:
            for k in range(arr.shape[0]):
                in_specs.append(pl.BlockSpec((None, tm, arr.shape[2]), functools.partial(lambda i, k_: (k_, i, 0), k_=k)))
                args.append(arr)
            continue
        if arr.shape[0] == 1:
            in_specs.append(pl.BlockSpec(arr.shape, lambda i: (0, 0)))
        elif width is None:
            in_specs.append(pl.BlockSpec((tm, arr.shape[1]), lambda i: (i, 0)))
        else:
            in_specs.append(pl.BlockSpec((tm, width), functools.partial(lambda i, blk_: (i, blk_), blk_=blk)))
        args.append(arr)
    out_specs = [pl.BlockSpec((tm, c), lambda i: (i, 0)) for c, _ in outs]
    out_specs += [pl.BlockSpec((1, c), lambda i: (0, 0)) for c in accs]
    out_shape = [jax.ShapeDtypeStruct((rows, c), dt) for c, dt in outs]
    out_shape += [jax.ShapeDtypeStruct((1, c), F32) for c in accs]
    n_in, n_out = len(args), len(outs)
    c_ins, c_outs, c_sems = _comm_operands(comm)

    def body(*refs):
        refs, c_refs = _comm_refs(comm, refs, n_in, n_out + len(accs))
        step = pl.program_id(0)
        _comm_begin(comm, c_refs, step, rows // tm)
        res = fn(*[r[...] for r in refs[:n_in]])
        for r, v in zip(refs[n_in:n_in + n_out], res[:n_out]):
            r[...] = v.astype(r.dtype)
        first = step == 0
        for r, v in zip(refs[n_in + n_out:], res[n_out:]):
            @pl.when(first)
            def _(r=r, v=v):
                r[...] = v

            @pl.when(jnp.logical_not(first))
            def _(r=r, v=v):
                r[...] += v
        _comm_end(comm, c_refs, step, rows // tm)

    return pl.pallas_call(
        body, name=name, grid=(rows // tm,), in_specs=in_specs + [ANY] * len(c_ins),
        out_specs=out_specs + [ANY] * len(c_outs), out_shape=out_shape + c_outs, scratch_shapes=c_sems,
        compiler_params=_params(("arbitrary",)),
    )(*args, *c_ins)


def _rms(x, g):
    return x * lax.rsqrt(jnp.mean(x * x, axis=-1, keepdims=True) + RMS_EPS) * g


def _colsum(v):
    return jnp.sum(v, axis=0, keepdims=True)


PAIR_W = 2 * HEAD_DIM
N_PAIRS = HEADS_PER_GROUP // 2


def _qkv_order(w_t, back=False):
    dims = (N_PAIRS, N_GROUPS, 3) if back else (3, N_GROUPS, N_PAIRS)
    return w_t.reshape(dims + (PAIR_W, w_t.shape[1])).transpose(2, 1, 0, 3, 4).reshape(QKV_W, w_t.shape[1])


def _rope_tables():
    half = ROPE_DIM // 2
    inv = jnp.power(jnp.float32(ROPE_THETA), -jnp.arange(half, dtype=F32) * 2.0 / ROPE_DIM)
    ang = jnp.arange(SEQ, dtype=F32)[:, None] * inv[None, :]
    cos, sin = jnp.cos(ang), jnp.sin(ang)
    zeros = jnp.zeros((SEQ, HEAD_DIM - ROPE_DIM), F32)
    zh = jnp.zeros((SEQ, half), F32)
    c = jnp.concatenate([cos, cos, zeros + 1.0], axis=1)
    sa = jnp.concatenate([-sin, zh, zeros], axis=1)
    sb = jnp.concatenate([zh, sin, zeros], axis=1)
    return [jnp.tile(t, (1, 2)) for t in (c, sa, sb)]


def _rope_fwd(x, c, sa, sb):
    return x * c + pltpu.roll(x, PAIR_W - 8, 1) * sa + pltpu.roll(x, 8, 1) * sb


def _rope_bwd(dy, c, sa, sb):
    return dy * c + pltpu.roll(dy * sb, PAIR_W - 8, 1) + pltpu.roll(dy * sa, 8, 1)


def _band_masks():
    row = lax.broadcasted_iota(jnp.int32, (ATTN_BLOCK, ATTN_BLOCK), 0)
    col = lax.broadcasted_iota(jnp.int32, (ATTN_BLOCK, ATTN_BLOCK), 1)
    return col <= row, col >= row


def _stack_rows(t):
    return jnp.concatenate([t, t], axis=0)


def _stack_heads(t, first_head):
    return jnp.concatenate([jnp.where(first_head, t, 0), jnp.where(first_head, 0, t)], axis=0)


def _per_head(fn):
    return jnp.concatenate([fn(slice(h * HEAD_DIM, (h + 1) * HEAD_DIM)) for h in range(2)], axis=1)


def _slab_spec(kind):
    return pl.BlockSpec((None, SEQ, PAIR_W), lambda b, p, g: (b, 0, p * 3 * N_GROUPS + g * 3 + kind))


_TABLE_SPEC = pl.BlockSpec((SEQ, PAIR_W), lambda b, p, g: (0, 0))
_PAIR_SPEC = pl.BlockSpec((None, SEQ, PAIR_W), lambda b, p, g: (b, 0, p))


def _block_rows(dil, r, n):
    return pl.ds(n * (ATTN_BLOCK * dil) + r, ATTN_BLOCK, stride=dil)


def attn_fwd(qkv, tables, comm=None):
    scale = HEAD_DIM ** -0.5

    def body(q_ref, k_ref, v_ref, c_ref, sa_ref, sb_ref, attn_b_ref, attn_ref, lse_ref, qs, ks, o0, o1, o2, l0, l1, l2):
        g = pl.program_id(2)
        c, sa, sb = c_ref[...], sa_ref[...], sb_ref[...]
        qs[...] = _rope_fwd(q_ref[...], c, sa, sb) * scale
        ks[...] = _rope_fwd(k_ref[...], c, sa, sb)
        cur_mask, prev_mask = _band_masks()
        first_head = lax.broadcasted_iota(jnp.int32, (ATTN_BLOCK, PAIR_W), 1) < HEAD_DIM

        def run(dil, o_slab, l_slab):
            nb = SEQ // dil // ATTN_BLOCK

            def block(idx, carry):
                r, n = lax.div(idx, nb), lax.rem(idx, nb)
                cur, prev = _block_rows(dil, r, n), _block_rows(dil, r, jnp.maximum(n - 1, 0))
                q = qs[cur, :].astype(BF16)
                kc, kp = ks[cur, :].astype(BF16), ks[prev, :].astype(BF16)
                vc, vp = v_ref[cur, :].astype(BF16), v_ref[prev, :].astype(BF16)
                q2 = _stack_heads(q, first_head)
                mask = _stack_rows(jnp.concatenate([jnp.logical_and(prev_mask, n > 0), cur_mask], axis=1))
                s2 = jnp.where(mask, _dot(q2, jnp.concatenate([kp, kc], axis=0), "nt"), NEG_INF)
                m = jnp.max(s2, axis=-1, keepdims=True)
                vcat, two = jnp.concatenate([vp, vc], axis=0), _stack_rows(first_head)
                vext = jnp.concatenate([jnp.where(two, vcat, 1), jnp.where(two, 1, vcat)], axis=1)
                r2 = _dot(jnp.exp(s2 - m).astype(BF16), vext, "nn")
                r0, r1 = r2[:ATTN_BLOCK, :PAIR_W], r2[ATTN_BLOCK:, PAIR_W:]
                num = jnp.where(first_head, r0, r1)
                den = pltpu.roll(jnp.where(first_head, r1, r0), HEAD_DIM, 1)
                o_slab[cur, :] = num / den
                l_slab[cur, :] = jnp.where(first_head, m[:ATTN_BLOCK], m[ATTN_BLOCK:]) + jnp.log(den)
                return carry

            lax.fori_loop(0, SEQ // ATTN_BLOCK, block, 0, unroll=4)

        for gi, (o_slab, l_slab) in enumerate(((o0, l0), (o1, l1), (o2, l2))):
            @pl.when(g == gi)
            def _(gi=gi, o_slab=o_slab, l_slab=l_slab):
                run(DILATIONS[gi], o_slab, l_slab)

        @pl.when(g == N_GROUPS - 1)
        def _():
            a, b, cc = l0[...], l1[...], l2[...]
            m = jnp.maximum(jnp.maximum(a, b), cc)
            e0, e1, e2 = jnp.exp(a - m), jnp.exp(b - m), jnp.exp(cc - m)
            tot = e0 + e1 + e2
            attn = (e0 * o0[...] + e1 * o1[...] + e2 * o2[...]) / tot
            attn_ref[...] = attn
            attn_b_ref[...] = attn.astype(BF16)
            lse_ref[...] = m + jnp.log(tot)

    shape = (LOCAL_BATCH, SEQ, GROUP_W)
    slab = pltpu.VMEM((SEQ, PAIR_W), F32)
    return hosted_call(
        body, comm, "attn_fwd", (LOCAL_BATCH, N_PAIRS, N_GROUPS),
        [_slab_spec(0), _slab_spec(1), _slab_spec(2), _TABLE_SPEC, _TABLE_SPEC, _TABLE_SPEC], [_PAIR_SPEC] * 3,
        [jax.ShapeDtypeStruct(shape, BF16), jax.ShapeDtypeStruct(shape, F32), jax.ShapeDtypeStruct(shape, F32)],
        [slab] * 8, (qkv, qkv, qkv, *tables), ("parallel", "parallel", "arbitrary"))


def attn_bwd(qkv, tables, dattn, attn, lse, comm=None):
    scale = HEAD_DIM ** -0.5

    def body(q_ref, k_ref, v_ref, c_ref, sa_ref, sb_ref, do_ref, out_ref, lse_ref, dqkv_ref, qs, ks, dl, dq_s, dk_s, dv_s):
        g = pl.program_id(2)
        c, sa, sb = c_ref[...], sa_ref[...], sb_ref[...]
        qs[...] = _rope_fwd(q_ref[...], c, sa, sb) * scale
        ks[...] = _rope_fwd(k_ref[...], c, sa, sb)
        @pl.when(g == 0)
        def _():
            prod = do_ref[...] * out_ref[...]
            dl[...] = _per_head(
                lambda sl: jnp.broadcast_to(jnp.sum(prod[:, sl], axis=-1, keepdims=True), (SEQ, HEAD_DIM)))

        cur_mask, prev_mask = _band_masks()
        first_head = lax.broadcasted_iota(jnp.int32, (ATTN_BLOCK, PAIR_W), 1) < HEAD_DIM

        def run(dil):
            nb = SEQ // dil // ATTN_BLOCK

            def block(idx, carry):
                r, n = lax.div(idx, nb), lax.rem(idx, nb)
                cur = _block_rows(dil, r, n)
                prev = _block_rows(dil, r, jnp.maximum(n - 1, 0))
                nxt = _block_rows(dil, r, jnp.minimum(n + 1, nb - 1))
                q0, q1 = qs[cur, :].astype(BF16), qs[nxt, :].astype(BF16)
                kp, kc = ks[prev, :].astype(BF16), ks[cur, :].astype(BF16)
                vp, vc = v_ref[prev, :].astype(BF16), v_ref[cur, :].astype(BF16)
                do0, do1 = do_ref[cur, :].astype(BF16), do_ref[nxt, :].astype(BF16)
                lse0, lse1, dl0, dl1 = lse_ref[cur, :], lse_ref[nxt, :], dl[cur, :], dl[nxt, :]
                has_prev = jnp.logical_and(prev_mask, n > 0)
                has_next = jnp.logical_and(prev_mask, n < nb - 1)

                def per_row(t):
                    return jnp.concatenate([t[:, 0:1], t[:, HEAD_DIM:HEAD_DIM + 1]], axis=0)

                q20, q21 = _stack_heads(q0, first_head), _stack_heads(q1, first_head)
                do20, do21 = _stack_heads(do0, first_head), _stack_heads(do1, first_head)
                kcat, vcat = jnp.concatenate([kp, kc], axis=0), jnp.concatenate([vp, vc], axis=0)
                mask0 = _stack_rows(jnp.concatenate([has_prev, cur_mask], axis=1))
                p0 = jnp.where(mask0, jnp.exp(_dot(q20, kcat, "nt") - per_row(lse0)), 0.0)
                ds0 = (p0 * (_dot(do20, vcat, "nt") - per_row(dl0))).astype(BF16)
                p1 = jnp.where(_stack_rows(has_next), jnp.exp(_dot(q21, kc, "nt") - per_row(lse1)), 0.0)
                ds1 = (p1 * (_dot(do21, vc, "nt") - per_row(dl1))).astype(BF16)
                dq2 = _dot(ds0, kcat, "nn")
                dq_s[cur, :] = jnp.where(first_head, dq2[:ATTN_BLOCK], dq2[ATTN_BLOCK:])
                ds_cur = jnp.concatenate([ds0[:, ATTN_BLOCK:], ds1], axis=0)
                p_cur = jnp.concatenate([p0[:, ATTN_BLOCK:], p1], axis=0).astype(BF16)
                dk_s[cur, :] = _dot(ds_cur, jnp.concatenate([q20, q21], axis=0), "tn")
                dv_s[cur, :] = _dot(p_cur, jnp.concatenate([do20, do21], axis=0), "tn")
                return carry

            lax.fori_loop(0, SEQ // ATTN_BLOCK, block, 0, unroll=2)

        for gi in range(N_GROUPS):
            @pl.when(g == gi)
            def _(gi=gi):
                run(DILATIONS[gi])

        dqkv_ref[:, 0:PAIR_W] = _rope_bwd(dq_s[...] * scale, c, sa, sb).astype(BF16)
        dqkv_ref[:, PAIR_W:2 * PAIR_W] = _rope_bwd(dk_s[...], c, sa, sb).astype(BF16)
        dqkv_ref[:, 2 * PAIR_W:] = dv_s[...].astype(BF16)

    slab = pltpu.VMEM((SEQ, PAIR_W), F32)
    return hosted_call(
        body, comm, "attn_bwd", (LOCAL_BATCH, N_PAIRS, N_GROUPS),
        [_slab_spec(0), _slab_spec(1), _slab_spec(2), _TABLE_SPEC, _TABLE_SPEC, _TABLE_SPEC,
         _PAIR_SPEC, _PAIR_SPEC, _PAIR_SPEC],
        [pl.BlockSpec((None, SEQ, 3 * PAIR_W), lambda b, p, g: (b, 0, p * N_GROUPS + g))],
        [jax.ShapeDtypeStruct((LOCAL_BATCH, SEQ, QKV_W), BF16)],
        [slab] * 6, (qkv, qkv, qkv, *tables, dattn, attn, lse), ("parallel", "parallel", "arbitrary"))


def _discretize(lr, li, log_dt, br, bi):
    dt = jnp.exp(log_dt)
    mag = jnp.exp(lr * dt)
    ab_re, ab_im = mag * jnp.cos(li * dt), mag * jnp.sin(li * dt)
    den = lr * lr + li * li
    nr, ni = ab_re - 1.0, ab_im
    f_re = (nr * lr + ni * li) / den
    f_im = (ni * lr - nr * li) / den
    return ab_re, ab_im, f_re[None] * br - f_im[None] * bi, f_re[None] * bi + f_im[None] * br


def ssm_prep(lr, li, log_dt, br, bi):
    def body(lr_ref, li_ref, dt_ref, br_ref, bi_ref, *outs):
        for o, v in zip(outs, _discretize(lr_ref[...], li_ref[...], dt_ref[...], br_ref[...], bi_ref[...])):
            o[...] = v
    shapes = [lr, li, br, bi]
    return pl.pallas_call(body, name="ssm_prep",
                          out_shape=[jax.ShapeDtypeStruct(s.shape, F32) for s in shapes])(lr, li, log_dt, br, bi)


def ssm_prep_bwd(lr, li, log_dt, br, bi, g_ab_re, g_ab_im, g_bb_re, g_bb_im):
    def body(lr_ref, li_ref, dt_ref, br_ref, bi_ref, g0, g1, g2, g3, *outs):
        _, vjp = jax.vjp(_discretize, lr_ref[...], li_ref[...], dt_ref[...], br_ref[...], bi_ref[...])
        for o, v in zip(outs, vjp((g0[...], g1[...], g2[...], g3[...]))):
            o[...] = v
    shapes = [lr, li, log_dt, br, bi]
    return pl.pallas_call(body, name="ssm_prep_bwd",
                          out_shape=[jax.ShapeDtypeStruct(s.shape, F32) for s in shapes])(
        lr, li, log_dt, br, bi, g_ab_re, g_ab_im, g_bb_re, g_bb_im)


def _block_diag(t):
    per = SSM_STATE_W // SSM_LANE_BLOCKS // 64
    g = t.transpose(1, 0, 2).reshape(SSM_LANE_BLOCKS, per, 16, 64)
    eye = jnp.eye(per, dtype=t.dtype)
    return jnp.einsum("jgcn,gh->jgchn", g, eye).reshape(SSM_LANE_BLOCKS, per * 16, per * 64)


def _block_diag_t(m):
    per = SSM_STATE_W // SSM_LANE_BLOCKS // 64
    m5 = m.reshape(SSM_LANE_BLOCKS, per, 16, per, 64)
    d = jnp.einsum("jgchn,gh->jgcn", m5, jnp.eye(per, dtype=m.dtype))
    return d.reshape(SSM_LANE_BLOCKS * per, 16, 64).transpose(1, 0, 2)


def _cmul(ar, ai, br, bi):
    return ar * br - ai * bi, ar * bi + ai * br


def _power_tables(ar, ai, reverse):
    width = ar.shape[1]
    row = lax.broadcasted_iota(jnp.int32, (8, width), 0)
    pows = [(ar, ai)]
    for _ in range(7):
        pows.append(_cmul(pows[-1][0], pows[-1][1], ar, ai))
    steps = []
    for k in (1, 2, 4):
        keep = (row >= k) if not reverse else (row < 8 - k)
        steps.append((jnp.where(keep, pows[k - 1][0], 0.0), jnp.where(keep, pows[k - 1][1], 0.0)))
    cr = jnp.zeros((8, width), F32)
    ci = jnp.zeros((8, width), F32)
    for i in range(8):
        pr, pi = pows[i] if not reverse else pows[7 - i]
        cr = jnp.where(row == i, pr, cr)
        ci = jnp.where(row == i, pi, ci)
    return steps, (cr, ci)


SCAN_CHUNK = 512
STATE_BLOCK = SSM_STATE_W // SSM_LANE_BLOCKS
CHAN_BLOCK = SSM_W // SSM_LANE_BLOCKS


def ssm_fwd(u, ab_re, ab_im, bb_re, bb_im, cb_re, cb_im, d_skip, comm=None):
    nt = SEQ // SCAN_CHUNK
    chan = pl.BlockSpec((None, SCAN_CHUNK, CHAN_BLOCK), lambda b, j, t: (b, t, j))
    state = pl.BlockSpec((None, SCAN_CHUNK, STATE_BLOCK), lambda b, j, t: (b, t, j))
    mat = pl.BlockSpec((None, CHAN_BLOCK, STATE_BLOCK), lambda b, j, t: (j, 0, 0))
    lane = pl.BlockSpec((1, STATE_BLOCK), lambda b, j, t: (0, j))
    dsp = pl.BlockSpec((1, CHAN_BLOCK), lambda b, j, t: (0, j))

    def body(u_ref, ar_ref, ai_ref, bbr_ref, bbi_ref, cbr_ref, cbi_ref, d_ref, y_ref, yg_ref, xr_ref, xi_ref,
             car_r, car_i):
        @pl.when(pl.program_id(2) == 0)
        def _():
            car_r[...] = jnp.zeros_like(car_r)
            car_i[...] = jnp.zeros_like(car_i)

        steps, (pr, pi) = _power_tables(ar_ref[...], ai_ref[...], reverse=False)
        uf = u_ref[...]
        ub = uf.astype(BF16)
        xr_ref[...] = _dot(ub, bbr_ref[...], "nn")
        xi_ref[...] = _dot(ub, bbi_ref[...], "nn")

        def tile(i, carry):
            cr, ci = carry
            sl = pl.ds(pl.multiple_of(i * 8, 8), 8)
            br, bi = xr_ref[sl, :], xi_ref[sl, :]
            for k, (sr, si) in zip((1, 2, 4), steps):
                tr, ti = _cmul(sr, si, pltpu.roll(br, k, 0), pltpu.roll(bi, k, 0))
                br, bi = br + tr, bi + ti
            tr, ti = _cmul(pr, pi, cr, ci)
            br, bi = br + tr, bi + ti
            xr_ref[sl, :] = br
            xi_ref[sl, :] = bi
            return br[7:8, :], bi[7:8, :]

        cr, ci = lax.fori_loop(0, SCAN_CHUNK // 8, tile, (car_r[0:1, :], car_i[0:1, :]), unroll=4)
        car_r[0:1, :] = cr
        car_i[0:1, :] = ci
        y = (_dot(xr_ref[...].astype(BF16), cbr_ref[...], "nt") - _dot(xi_ref[...].astype(BF16), cbi_ref[...], "nt")
             + d_ref[...] * uf)
        y_ref[...] = y
        yg_ref[...] = jax.nn.gelu(y).astype(BF16)

    return hosted_call(
        body, comm, "ssm_fwd", (LOCAL_BATCH, SSM_LANE_BLOCKS, nt),
        [chan, lane, lane, mat, mat, mat, mat, dsp], [chan, chan, state, state],
        [jax.ShapeDtypeStruct((LOCAL_BATCH, SEQ, SSM_W), F32), jax.ShapeDtypeStruct((LOCAL_BATCH, SEQ, SSM_W), BF16),
         jax.ShapeDtypeStruct((LOCAL_BATCH, SEQ, SSM_STATE_W), F32),
         jax.ShapeDtypeStruct((LOCAL_BATCH, SEQ, SSM_STATE_W), F32)],
        [pltpu.VMEM((8, STATE_BLOCK), F32), pltpu.VMEM((8, STATE_BLOCK), F32)],
        (u, ab_re, ab_im, bb_re, bb_im, cb_re, cb_im, d_skip), ("parallel", "parallel", "arbitrary"))


def ssm_bwd(dyg, y, u, xr, xi, ab_re, ab_im, bb_re, bb_im, cb_re, cb_im, d_skip, comm=None):
    nt = SEQ // SCAN_CHUNK
    ntile = SCAN_CHUNK // 8

    def rev(t):
        return nt - 1 - t

    chan = pl.BlockSpec((None, SCAN_CHUNK, CHAN_BLOCK), lambda j, b, t: (b, rev(t), j))
    state = pl.BlockSpec((None, SCAN_CHUNK, STATE_BLOCK), lambda j, b, t: (b, rev(t), j))
    before = pl.BlockSpec((None, 8, STATE_BLOCK), lambda j, b, t: (b, jnp.maximum(rev(t) * ntile - 1, 0), j))
    mat = pl.BlockSpec((None, CHAN_BLOCK, STATE_BLOCK), lambda j, b, t: (j, 0, 0))
    lane = pl.BlockSpec((1, STATE_BLOCK), lambda j, b, t: (0, j))
    lane8 = pl.BlockSpec((8, STATE_BLOCK), lambda j, b, t: (0, j))
    dsp = pl.BlockSpec((1, CHAN_BLOCK), lambda j, b, t: (0, j))

    def body(dyg_ref, y_ref, u_ref, xr_ref, xi_ref, xrb_ref, xib_ref, ar_ref, ai_ref, bbr_ref, bbi_ref, cbr_ref,
             cbi_ref, d_ref, du_ref, dcbr_ref, dcbi_ref, dbbr_ref, dbbi_ref, dd_ref, dar_ref, dai_ref,
             lam_r, lam_i, car_r, car_i):
        b, t = pl.program_id(1), pl.program_id(2)
        first = jnp.logical_and(b == 0, t == 0)

        @pl.when(t == 0)
        def _():
            car_r[...] = jnp.zeros_like(car_r)
            car_i[...] = jnp.zeros_like(car_i)

        @pl.when(first)
        def _():
            for r in (dcbr_ref, dcbi_ref, dbbr_ref, dbbi_ref, dd_ref, dar_ref, dai_ref):
                r[...] = jnp.zeros_like(r)

        steps, (pr, pi) = _power_tables(ar_ref[...], -ai_ref[...], reverse=True)
        uf = u_ref[...]
        _, gelu_vjp = jax.vjp(jax.nn.gelu, y_ref[...])
        dy = gelu_vjp(dyg_ref[...])[0]
        dyb = dy.astype(BF16)
        dd_ref[...] += _colsum(dy * uf)
        lam_r[...] = _dot(dyb, cbr_ref[...], "nn")
        lam_i[...] = -_dot(dyb, cbi_ref[...], "nn")
        dcbr_ref[...] += _dot(dyb, xr_ref[...].astype(BF16), "tn")
        dcbi_ref[...] -= _dot(dyb, xi_ref[...].astype(BF16), "tn")
        row0 = lax.broadcasted_iota(jnp.int32, (8, STATE_BLOCK), 0) == 0
        has_before = rev(t) > 0
        xrb = jnp.where(has_before, xrb_ref[...], 0.0)
        xib = jnp.where(has_before, xib_ref[...], 0.0)

        def tile(s, carry):
            cr, ci, acc_r, acc_i = carry
            i = ntile - 1 - s
            sl = pl.ds(pl.multiple_of(i * 8, 8), 8)
            gr, gi = lam_r[sl, :], lam_i[sl, :]
            for k, (sr, si) in zip((1, 2, 4), steps):
                tr, ti = _cmul(sr, si, pltpu.roll(gr, 8 - k, 0), pltpu.roll(gi, 8 - k, 0))
                gr, gi = gr + tr, gi + ti
            tr, ti = _cmul(pr, pi, cr, ci)
            gr, gi = gr + tr, gi + ti
            lam_r[sl, :] = gr
            lam_i[sl, :] = gi
            sp = pl.ds(pl.multiple_of(jnp.maximum(i - 1, 0) * 8, 8), 8)
            pvr = jnp.where(i > 0, xr_ref[sp, :], xrb)
            pvi = jnp.where(i > 0, xi_ref[sp, :], xib)
            xsr = jnp.where(row0, pltpu.roll(pvr, 1, 0), pltpu.roll(xr_ref[sl, :], 1, 0))
            xsi = jnp.where(row0, pltpu.roll(pvi, 1, 0), pltpu.roll(xi_ref[sl, :], 1, 0))
            acc_r = acc_r + xsr * gr + xsi * gi
            acc_i = acc_i + xsr * gi - xsi * gr
            return gr[0:1, :], gi[0:1, :], acc_r, acc_i

        zero = jnp.zeros((8, STATE_BLOCK), F32)
        cr, ci, acc_r, acc_i = lax.fori_loop(0, ntile, tile, (car_r[0:1, :], car_i[0:1, :], zero, zero), unroll=2)
        car_r[0:1, :] = cr
        car_i[0:1, :] = ci
        dar_ref[...] += acc_r
        dai_ref[...] += acc_i
        lrb, lib = lam_r[...].astype(BF16), lam_i[...].astype(BF16)
        du = _dot(lrb, bbr_ref[...], "nt") + _dot(lib, bbi_ref[...], "nt") + d_ref[...] * dy
        du_ref[...] = du.astype(BF16)
        ub = uf.astype(BF16)
        dbbr_ref[...] += _dot(ub, lrb, "tn")
        dbbi_ref[...] += _dot(ub, lib, "tn")

    mat_shape = jax.ShapeDtypeStruct((SSM_LANE_BLOCKS, CHAN_BLOCK, STATE_BLOCK), F32)
    return hosted_call(
        body, comm, "ssm_bwd", (SSM_LANE_BLOCKS, LOCAL_BATCH, nt),
        [chan, chan, chan, state, state, before, before, lane, lane, mat, mat, mat, mat, dsp],
        [chan, mat, mat, mat, mat, dsp, lane8, lane8],
        [jax.ShapeDtypeStruct((LOCAL_BATCH, SEQ, SSM_W), BF16), mat_shape, mat_shape, mat_shape, mat_shape,
         jax.ShapeDtypeStruct((1, SSM_W), F32), jax.ShapeDtypeStruct((8, SSM_STATE_W), F32),
         jax.ShapeDtypeStruct((8, SSM_STATE_W), F32)],
        [pltpu.VMEM((SCAN_CHUNK, STATE_BLOCK), F32), pltpu.VMEM((SCAN_CHUNK, STATE_BLOCK), F32),
         pltpu.VMEM((8, STATE_BLOCK), F32), pltpu.VMEM((8, STATE_BLOCK), F32)],
        (dyg, y, u, xr, xi, xr, xi, ab_re, ab_im, bb_re, bb_im, cb_re, cb_im, d_skip),
        ("parallel", "arbitrary", "arbitrary"))


def _merge_fn(g0, g1, attn_d, za, zb):
    return jax.nn.sigmoid(g0) * attn_d + jax.nn.sigmoid(g1) * (za * jax.nn.sigmoid(zb))


def _swiglu_fn(a, b):
    return jax.nn.silu(a) * b


def _reduce_start(names, gw, shard_shapes):
    return swap_comm([_to_slots(n, gw[n], shard_shapes[n]) for n in names])


def _reduce_chip(names, swap, got, core):
    return exchange_comm([add_halves(n, g, r, core) for n, g, r in zip(names, swap.ins, got)])


def local_step(x, target, shards, small, core):
    g_mix, g_ffn, g_final = small["norm_mix_g"], small["norm_ffn_g"], small["norm_final_g"]
    tables = _rope_tables()
    seqs = lambda t: t.reshape(LOCAL_BATCH, SEQ, t.shape[-1])
    toks = lambda t: t.reshape(TOKENS, t.shape[-1])
    shard_shapes = {n: s.shape for n, s in shards.items()}
    w = {}

    def gather(names):
        return gather_comm([shards[n] for n in names])

    def arrived(names, slots):
        for n, s in zip(names, slots):
            w[n] = _from_slots(n, s)

    h, *slots = rowwise(lambda xv, g: (_rms(xv, g),), [x, g_mix], [(D_MODEL, BF16)], "norm_mix", comm=gather(["w_in"]))
    arrived(["w_in"], slots)
    w_qkv, w_u, w_gate = _qkv_order(w["w_in"][:QKV_W]), w["w_in"][QKV_W:QKV_W + SSM_W], w["w_in"][QKV_W + SSM_W:]
    qkv, *slots = matmul(h, w_qkv, "nt", F32, "proj_qkv", comm=gather(["w_attn_out", "w_glu"]))
    arrived(["w_attn_out", "w_glu"], slots)
    qkv = seqs(qkv)
    u = seqs(matmul(h, w_u, "nt", F32, "proj_u"))
    gl, *slots = matmul(h, w_gate, "nt", F32, "proj_gate", comm=gather(["w_out"]))
    arrived(["w_out"], slots)
    attn_b, attn, lse, *slots = attn_fwd(qkv, tables, comm=gather(["w_ffn_gate"]))
    arrived(["w_ffn_gate"], slots)
    attn_b = toks(attn_b)
    attn_d = matmul(attn_b, w["w_attn_out"], "nn", F32, "attn_out")

    br_t = small["ssm_b_re"].transpose(2, 0, 1)
    bi_t = small["ssm_b_im"].transpose(2, 0, 1)
    log_dt = small["ssm_log_dt"].reshape(32, 1)
    ab_re, ab_im, bb_re_t, bb_im_t = ssm_prep(small["ssm_a_re"], small["ssm_a_im"], log_dt, br_t, bi_t)
    ab = [ab_re.reshape(1, SSM_STATE_W), ab_im.reshape(1, SSM_STATE_W)]
    bb = [_block_diag(bb_re_t).astype(BF16), _block_diag(bb_im_t).astype(BF16)]
    cb = [_block_diag(small["ssm_c_re"].transpose(1, 0, 2)).astype(BF16),
          _block_diag(small["ssm_c_im"].transpose(1, 0, 2)).astype(BF16)]
    d_skip = small["ssm_d"].reshape(1, SSM_W)
    ffn_rest = ["w_ffn_up", "w_ffn_down"]
    y, yg, xr, xi, *slots = ssm_fwd(u, *ab, *bb, *cb, d_skip, comm=gather(ffn_rest))
    arrived(ffn_rest, slots)
    yg2 = toks(yg)
    z = matmul(yg2, w["w_glu"], "nn", F32, "glu")
    gate_ins = [(gl, D_MODEL, 0), (gl, D_MODEL, 1), attn_d, (z, D_MODEL, 0), (z, D_MODEL, 1)]
    (merged,) = rowwise(lambda *v: (_merge_fn(*v),), gate_ins, [(D_MODEL, BF16)], "merge")
    x1, h2 = matmul_rows(merged, w["w_out"], "out_proj", lambda rows, g: (rows, _rms(rows, g)), [g_ffn],
                         [(D_MODEL, F32), (D_MODEL, BF16)], add=x)
    a, b, act = ffn_in(h2, w["w_ffn_gate"], w["w_ffn_up"])

    def final_fn(xv, g, tgt):
        yv, vjp = jax.vjp(_rms, xv, g)
        err = yv - tgt
        dx, dg = vjp(err * (1.0 / D_MODEL))
        loss = 0.5 * jnp.sum(jnp.mean(err * err, axis=-1, keepdims=True), axis=0, keepdims=True)
        return dx, dx, dg, jnp.broadcast_to(loss, (1, LANES))

    dx2, dx2_b, dg_final, loss = matmul_rows(act, w["w_ffn_down"], "ffn_down_loss", final_fn, [g_final, target],
                                             [(D_MODEL, F32), (D_MODEL, BF16)], accs=(D_MODEL, LANES), add=x1)
    gw, parts = {}, {}
    gw["w_ffn_down"] = matmul(act, dx2_b, "tn", F32, "d_ffn_down")
    da_b, db_b = ffn_in_bwd(dx2_b, w["w_ffn_down"], a, b)
    gw["w_ffn_gate"] = matmul(da_b, h2, "tn", F32, "d_ffn_gate")
    gw["w_ffn_up"] = matmul(db_b, h2, "tn", F32, "d_ffn_up")
    ffn = ["w_ffn_down", "w_ffn_gate", "w_ffn_up"]
    swap = _reduce_start(ffn[:2], gw, shard_shapes)
    dh2, *got = matmul(da_b, w["w_ffn_gate"], "nn", F32, "d_h2_gate", comm=swap)
    ffn_exchange = [_reduce_chip(ffn[:2], swap, got, core)]
    swap = _reduce_start(ffn[2:], gw, shard_shapes)

    def norm_bwd(dh, xv, g, skip):
        _, vjp = jax.vjp(_rms, xv, g)
        dx, dg = vjp(dh)
        dx = dx + skip
        return dx, dx, dg

    dx1, dx1_b, dg_ffn, *got = matmul_rows(db_b, w["w_ffn_up"], "d_h2_up_norm", norm_bwd, [x1, g_ffn, dx2],
                                           [(D_MODEL, F32), (D_MODEL, BF16)], accs=(D_MODEL,), add=dh2, comm=swap)
    ffn_up_exchange = _reduce_chip(ffn[2:], swap, got, core)
    gw["w_out"] = matmul(merged, dx1_b, "tn", F32, "d_out")
    dmerged = matmul(dx1_b, w["w_out"], "nt", F32, "d_merged")

    def merge_bwd(g0, g1, ad, za, zb, dm):
        _, vjp = jax.vjp(_merge_fn, g0, g1, ad, za, zb)
        dg0, dg1, dad, dza, dzb = vjp(dm)
        return jnp.concatenate([dg0, dg1], axis=1), dad, jnp.concatenate([dza, dzb], axis=1)

    dgl_b, dattn_d_b, dz_b, parts["w_ffn_up"] = rowwise(
        merge_bwd, gate_ins + [dmerged], [(GATE_W, BF16), (D_MODEL, BF16), (GATE_W, BF16)], "merge_bwd",
        comm=ffn_up_exchange)
    gw["w_attn_out"] = matmul(attn_b, dattn_d_b, "tn", F32, "d_attn_out")
    dattn = seqs(matmul(dattn_d_b, w["w_attn_out"], "nt", F32, "d_attn"))
    gw["w_glu"] = matmul(yg2, dz_b, "tn", F32, "d_glu")
    dyg = seqs(matmul(dz_b, w["w_glu"], "nt", F32, "d_yg"))
    mixer = ["w_out", "w_attn_out", "w_glu"]
    swap = _reduce_start(mixer, gw, shard_shapes)
    du_b, dcb_re, dcb_im, dbb_re, dbb_im, dd, da_re8, da_im8, *rest = ssm_bwd(
        dyg, y, u, xr, xi, *ab, *bb, *cb, d_skip, comm=join_comms(ffn_exchange + [swap]))
    for n, p in zip(ffn[:2], rest[:2]):
        parts[n] = p
    mixer_exchange = _reduce_chip(mixer, swap, rest[2:], core)
    du_b = toks(du_b)
    g_ab_re = jnp.sum(da_re8, axis=0).reshape(32, 64)
    g_ab_im = jnp.sum(da_im8, axis=0).reshape(32, 64)
    d_lr, d_li, d_ldt, d_br_t, d_bi_t = ssm_prep_bwd(
        small["ssm_a_re"], small["ssm_a_im"], log_dt, br_t, bi_t,
        g_ab_re, g_ab_im, _block_diag_t(dbb_re), _block_diag_t(dbb_im))
    gs = {
        "ssm_a_re": d_lr, "ssm_a_im": d_li, "ssm_log_dt": d_ldt.reshape(1, 32),
        "ssm_b_re": d_br_t.transpose(1, 2, 0), "ssm_b_im": d_bi_t.transpose(1, 2, 0),
        "ssm_c_re": _block_diag_t(dcb_re).transpose(1, 0, 2), "ssm_c_im": _block_diag_t(dcb_im).transpose(1, 0, 2),
        "ssm_d": dd.reshape(32, 16),
    }
    ssm_gather = small_comm(_pack_small(SSM_SMALL, gs, SSM_ROWS))
    dqkv_b, *rest = attn_bwd(qkv, tables, dattn, attn, lse, comm=join_comms([mixer_exchange, ssm_gather]))
    for n, p in zip(mixer, rest):
        parts[n] = p
    ssm_shares = rest[len(mixer)]
    dqkv_b = toks(dqkv_b)
    d_qkv = matmul(dqkv_b, h, "tn", F32, "d_w_qkv")
    d_u = matmul(du_b, h, "tn", F32, "d_w_u")
    d_gate = matmul(dgl_b, h, "tn", F32, "d_w_gate")
    gw["w_in"] = jnp.concatenate([_qkv_order(d_qkv, back=True), d_u, d_gate], axis=0)
    swap = _reduce_start(["w_in"], gw, shard_shapes)
    dh, *got = matmul(dqkv_b, w_qkv, "nn", F32, "d_h_qkv", comm=swap)
    w_in_exchange = _reduce_chip(["w_in"], swap, got, core)
    grad_x, dg_mix, parts["w_in"] = mix_in_bwd([du_b, dgl_b], [w_u, w_gate], dh, x, g_mix, dx1, comm=w_in_exchange)
    gs_norm = {"norm_mix_g": dg_mix, "norm_ffn_g": dg_ffn, "norm_final_g": dg_final}
    return loss, grad_x, parts, ssm_shares, gs_norm


ANY = pl.BlockSpec(memory_space=pl.ANY)
BIG = ("w_in", "w_glu", "w_attn_out", "w_out", "w_ffn_gate", "w_ffn_up", "w_ffn_down")
TRANSPOSED = ("w_in", "w_ffn_gate", "w_ffn_up")
ROW_SHARDED = TRANSPOSED + ("w_out", "w_ffn_down")
SMALL = ("norm_mix_g", "ssm_a_re", "ssm_a_im", "ssm_log_dt", "ssm_b_re", "ssm_b_im", "ssm_c_re", "ssm_c_im",
         "ssm_d", "norm_ffn_g", "norm_final_g")
WEIGHTS = ("norm_mix_g", "w_in", "ssm_a_re", "ssm_a_im", "ssm_log_dt", "ssm_b_re", "ssm_b_im", "ssm_c_re",
           "ssm_c_im", "ssm_d", "w_glu", "w_attn_out", "w_out", "norm_ffn_g", "w_ffn_gate", "w_ffn_up",
           "w_ffn_down", "norm_final_g")
SSM_SMALL = SMALL[1:9]
NORM_SMALL = (SMALL[0],) + SMALL[9:]
SSM_ROWS, NORM_ROWS = 1064, 32
N_BIG = len(BIG)


def _position():
    return lax.axis_index("x"), lax.axis_index("y"), lax.axis_index("c")


def _other_chips(x, y):
    return [(1 - x, y), (x, 1 - y), (1 - x, 1 - y)]


def _remote(src, dst, send_sem, recv_sem, device):
    return pltpu.make_async_remote_copy(src_ref=src, dst_ref=dst, send_sem=send_sem, recv_sem=recv_sem,
                                        device_id=device, device_id_type=MESH)


_later = functools.partial


def _two_level_phases(copies):
    def first(*refs):
        locals_, sends, _, _, _ = copies(*refs)
        for cp in locals_ + sends:
            cp().start()

    def mid(*refs):
        _, _, arrived, passed, _ = copies(*refs)
        for got, cp in zip(arrived, passed):
            got().wait_recv()
            cp().start()

    def last(*refs):
        locals_, sends, _, passed, from_sibling = copies(*refs)
        for cp in from_sibling:
            cp().wait_recv()
        for cp in sends + passed:
            cp().wait_send()
        for cp in locals_:
            cp().wait()

    return first, mid, last


def _half(ref, chip, which):
    rows = ref.shape[1] // 2
    return ref.at[chip, pl.ds(which * rows, rows), :]


class Comm:
    def __init__(self, ins, out_shapes, sems, first, mid, last):
        self.ins, self.out_shapes, self.sems = list(ins), list(out_shapes), list(sems)
        self.first, self.mid, self.last = first, mid, last


def join_comms(comms):
    def cut(refs_by_kind):
        offs, parts = [0, 0, 0], []
        for cm in comms:
            sizes = (len(cm.ins), len(cm.out_shapes), len(cm.sems))
            parts.append(tuple(refs_by_kind[k][offs[k]:offs[k] + sizes[k]] for k in range(3)))
            offs = [o + s for o, s in zip(offs, sizes)]
        return parts

    def phase(which):
        def run(ins, outs, sems):
            for cm, part in zip(comms, cut((ins, outs, sems))):
                fn = getattr(cm, which)
                if fn is not None:
                    fn(*part)
        return run

    return Comm(sum((cm.ins for cm in comms), []), sum((cm.out_shapes for cm in comms), []),
                sum((cm.sems for cm in comms), []), phase("first"), phase("mid"), phase("last"))


def _comm_operands(comm):
    if comm is None:
        return [], [], []
    return comm.ins, comm.out_shapes, comm.sems


def _comm_begin(comm, refs, step, n_steps):
    if comm is None:
        return
    pl.when(step == 0)(lambda: comm.first(*refs))
    if comm.mid is not None:
        pl.when(step == (n_steps * 3) // 4)(lambda: comm.mid(*refs))


def _comm_end(comm, refs, step, n_steps):
    if comm is not None:
        pl.when(step == n_steps - 1)(lambda: comm.last(*refs))


def _comm_refs(comm, refs, n_in, n_out):
    if comm is None:
        return list(refs), None
    ci, co, cs = len(comm.ins), len(comm.out_shapes), len(comm.sems)
    o0 = n_in + ci
    s0 = o0 + n_out + co
    host = list(refs[:n_in]) + list(refs[o0:o0 + n_out]) + list(refs[s0:len(refs) - cs])
    return host, (list(refs[n_in:o0]), list(refs[o0 + n_out:s0]), list(refs[len(refs) - cs:]))


def run_comm(comm, name):
    n_in, n_out = len(comm.ins), len(comm.out_shapes)

    def body(*refs):
        parts = (list(refs[:n_in]), list(refs[n_in:n_in + n_out]), list(refs[n_in + n_out:]))
        comm.first(*parts)
        if comm.mid is not None:
            comm.mid(*parts)
        comm.last(*parts)

    return pl.pallas_call(body, name=name, in_specs=[ANY] * n_in, out_specs=[ANY] * n_out,
                          out_shape=comm.out_shapes, scratch_shapes=comm.sems)(*comm.ins)


def hosted_call(work, comm, name, grid, in_specs, out_specs, out_shape, scratch_shapes, args, semantics):
    c_ins, c_outs, c_sems = _comm_operands(comm)
    n_steps = math.prod(grid)

    def body(*refs):
        host, c_refs = _comm_refs(comm, refs, len(in_specs), len(out_specs))
        step = 0
        for axis, size in enumerate(grid):
            step = step * size + pl.program_id(axis)
        _comm_begin(comm, c_refs, step, n_steps)
        work(*host)
        _comm_end(comm, c_refs, step, n_steps)

    return pl.pallas_call(
        body, name=name, grid=grid, in_specs=list(in_specs) + [ANY] * len(c_ins),
        out_specs=list(out_specs) + [ANY] * len(c_outs), out_shape=list(out_shape) + c_outs,
        scratch_shapes=list(scratch_shapes) + c_sems,
        compiler_params=_params(semantics if comm is None else ("arbitrary",) * len(grid)),
    )(*args, *c_ins)


def gather_comm(shards):
    n = len(shards)

    def copies(srcs, outs, sems):
        send_sems, recv_sems, local_sems = sems
        x, y, c = _position()
        me = 2 * x + y
        sibling = (x, y, 1 - c)
        chips = _other_chips(x, y)
        locals_ = [_later(pltpu.make_async_copy, s, o.at[me], local_sems.at[i])
                   for i, (s, o) in enumerate(zip(srcs, outs))]
        sends, arrived, passed, from_sibling = [], [], [], []
        for j, (px, py) in enumerate(chips):
            for i, (s, o) in enumerate(zip(srcs, outs)):
                rows = s.shape[0] // 2
                sends.append(_later(_remote, s.at[pl.ds(c * rows, rows), :], _half(o, me, c), send_sems.at[i, j],
                                    recv_sems.at[i, j], (px, py, c)))
                got = _half(o, 2 * px + py, c)
                arrived.append(_later(_remote, got, got, send_sems.at[i, j], recv_sems.at[i, j], (px, py, c)))
                passed.append(_later(_remote, got, got, send_sems.at[i, 3 + j], recv_sems.at[i, 3 + j], sibling))
                other = _half(o, 2 * px + py, 1 - c)
                from_sibling.append(_later(_remote, other, other, send_sems.at[i, 3 + j], recv_sems.at[i, 3 + j],
                                           sibling))
        return locals_, sends, arrived, passed, from_sibling

    return Comm(shards, [jax.ShapeDtypeStruct((N_CHIPS,) + s.shape, s.dtype) for s in shards],
                [pltpu.SemaphoreType.DMA((n, 6)), pltpu.SemaphoreType.DMA((n, 6)), pltpu.SemaphoreType.DMA((n,))],
                *_two_level_phases(copies))


def swap_comm(grads):
    n = len(grads)

    def copies(srcs, gots, sems):
        send_sems, recv_sems = sems
        x, y, c = _position()
        out = []
        for i, (s, o) in enumerate(zip(srcs, gots)):
            rows = s.shape[1] // 2
            out.append(_remote(s.at[:, pl.ds((1 - c) * rows, rows), :], o, send_sems.at[i], recv_sems.at[i],
                               (x, y, 1 - c)))
        return out

    def first(srcs, gots, sems):
        for cp in copies(srcs, gots, sems):
            cp.start()

    def last(srcs, gots, sems):
        for cp in copies(srcs, gots, sems):
            cp.wait()

    return Comm(grads, [jax.ShapeDtypeStruct((N_CHIPS, g.shape[1] // 2, g.shape[2]), g.dtype) for g in grads],
                [pltpu.SemaphoreType.DMA((n,)), pltpu.SemaphoreType.DMA((n,))], first, None, last)


def add_halves(name, g, got, core):
    _, half, cols = got.shape
    mine = pl.BlockSpec((None, half, cols), lambda k, c_ref: (k, c_ref[0], 0))
    other = pl.BlockSpec((None, half, cols), lambda k, c_ref: (k, 0, 0))

    def body(c_ref, g_ref, got_ref, o_ref):
        o_ref[...] = (g_ref[...] + got_ref[...]).astype(BF16)

    return pl.pallas_call(
        body, name="add_halves_" + name,
        grid_spec=pltpu.PrefetchScalarGridSpec(num_scalar_prefetch=1, grid=(N_CHIPS,), in_specs=[mine, other],
                                               out_specs=other),
        out_shape=jax.ShapeDtypeStruct(got.shape, BF16),
        compiler_params=_params(("parallel",)),
    )(core, g, got)


def exchange_comm(parts):
    n = len(parts)

    def copies(srcs, outs, sems):
        send_sems, recv_sems, local_sems = sems
        x, y, c = _position()
        me = 2 * x + y
        sibling = (x, y, 1 - c)
        chips = _other_chips(x, y)
        locals_, sends, arrived, passed, from_sibling = [], [], [], [], []
        for i, (s, o) in enumerate(zip(srcs, outs)):
            locals_.append(_later(pltpu.make_async_copy, s.at[me], _half(o, me, c), local_sems.at[i]))
            sends.append(_later(_remote, s.at[me], _half(o, me, c), send_sems.at[i, 3], recv_sems.at[i, 3], sibling))
            other = _half(o, me, 1 - c)
            from_sibling.append(_later(_remote, other, other, send_sems.at[i, 3], recv_sems.at[i, 3], sibling))
        for j, (px, py) in enumerate(chips):
            for i, (s, o) in enumerate(zip(srcs, outs)):
                sends.append(_later(_remote, s.at[2 * px + py], _half(o, me, c), send_sems.at[i, j],
                                    recv_sems.at[i, j], (px, py, c)))
                got = _half(o, 2 * px + py, c)
                arrived.append(_later(_remote, got, got, send_sems.at[i, j], recv_sems.at[i, j], (px, py, c)))
                passed.append(_later(_remote, got, got, send_sems.at[i, 4 + j], recv_sems.at[i, 4 + j], sibling))
                other = _half(o, 2 * px + py, 1 - c)
                from_sibling.append(_later(_remote, other, other, send_sems.at[i, 4 + j], recv_sems.at[i, 4 + j],
                                           sibling))
        return locals_, sends, arrived, passed, from_sibling

    return Comm(parts, [jax.ShapeDtypeStruct((N_CHIPS, 2 * p.shape[1], p.shape[2]), p.dtype) for p in parts],
                [pltpu.SemaphoreType.DMA((n, 7)), pltpu.SemaphoreType.DMA((n, 7)), pltpu.SemaphoreType.DMA((n,))],
                *_two_level_phases(copies))


def small_comm(pack):
    def copies(srcs, outs, sems):
        (src_ref,), (out_ref,), (send_sems, recv_sems, local_sem) = srcs, outs, sems
        x, y, c = _position()
        me = 4 * x + 2 * y + c
        flips = [(fx, fy, fc) for fx in (0, 1) for fy in (0, 1) for fc in (0, 1)][1:]
        peers = [(1 - x if fx else x, 1 - y if fy else y, 1 - c if fc else c) for fx, fy, fc in flips]
        local = _later(pltpu.make_async_copy, src_ref, out_ref.at[me], local_sem)
        sends = [_later(_remote, src_ref, out_ref.at[me], send_sems.at[j], recv_sems.at[j], peer)
                 for j, peer in enumerate(peers)]
        arrived = []
        for j, (px, py, pc) in enumerate(peers):
            got = out_ref.at[4 * px + 2 * py + pc]
            arrived.append(_later(_remote, got, got, send_sems.at[j], recv_sems.at[j], (px, py, pc)))
        return local, sends, arrived

    def first(*refs):
        local, sends, _ = copies(*refs)
        for cp in [local] + sends:
            cp().start()

    def last(*refs):
        local, sends, arrived = copies(*refs)
        for cp in arrived:
            cp().wait_recv()
        for cp in sends:
            cp().wait_send()
        local().wait()

    return Comm([pack], [jax.ShapeDtypeStruct((N_DEV,) + pack.shape, pack.dtype)],
                [pltpu.SemaphoreType.DMA((7,)), pltpu.SemaphoreType.DMA((7,)), pltpu.SemaphoreType.DMA],
                first, None, last)


def _adam_fn(w, g, m, v):
    m = ADAM_B1 * m + (1.0 - ADAM_B1) * g
    v = ADAM_B2 * v + (1.0 - ADAM_B2) * jnp.square(g)
    m_hat = m / (1.0 - ADAM_B1 ** ADAM_STEP)
    v_hat = v / (1.0 - ADAM_B2 ** ADAM_STEP)
    return -ADAM_LR * (m_hat / (jnp.sqrt(v_hat) + ADAM_EPS) + ADAM_WD * w), m, v


def adam_big(name, parts, w, m, v):
    rows, cols = w.shape
    tm = _pick(rows, 384, 16)

    def fn(p0, p1, p2, p3, wv, mv, vv):
        g = ((p0.astype(F32) + p1.astype(F32)) + p2.astype(F32)) + p3.astype(F32)
        return (g,) + _adam_fn(wv, g, mv, vv)

    return rowwise(fn, [parts, w, m, v], [(cols, F32)] * 4, "adam_" + name, tm=tm, rows=rows)


def adam_small(name, gathered, w, m, v):
    def body(g_ref, w_ref, m_ref, v_ref, go_ref, d_ref, mo_ref, vo_ref):
        g = g_ref[0]
        for k in range(1, N_DEV):
            g = g + g_ref[k]
        go_ref[...] = g
        d_ref[...], mo_ref[...], vo_ref[...] = _adam_fn(w_ref[...], g, m_ref[...], v_ref[...])

    return pl.pallas_call(body, name=name, out_shape=[jax.ShapeDtypeStruct(w.shape, F32)] * 4,
                          compiler_params=_params())(gathered, w, m, v)


def _pack_small(names, vals, rows, last=None):
    flat = [vals[n].reshape(-1) for n in names]
    if last is not None:
        flat.append(last.reshape(-1))
    flat = jnp.concatenate(flat)
    return jnp.pad(flat, (0, rows * LANES - flat.shape[0])).reshape(rows, LANES)


def _unpack_small(names, pack, shapes):
    flat, out, off = pack.reshape(-1), {}, 0
    for n in names:
        size = math.prod(shapes[n])
        out[n] = flat[off:off + size].reshape(shapes[n])
        off += size
    return out, flat[off]


def _to_slots(name, g, shard_shape):
    rows, cols = shard_shape
    if name in ROW_SHARDED:
        return g.reshape(N_CHIPS, rows, cols)
    return g.reshape(rows, N_CHIPS, cols).transpose(1, 0, 2)


def _from_slots(name, s):
    _, rows, cols = s.shape
    if name in ROW_SHARDED:
        return s.reshape(N_CHIPS * rows, cols)
    return s.transpose(1, 0, 2).reshape(rows, N_CHIPS * cols)


def kernel(x, norm_mix_g, w_in, ssm_a_re, ssm_a_im, ssm_log_dt, ssm_b_re, ssm_b_im, ssm_c_re, ssm_c_im, ssm_d, w_glu, w_attn_out, w_out, norm_ffn_g, w_ffn_gate, w_ffn_up, w_ffn_down, norm_final_g, loss_target, m_norm_mix_g, m_w_in, m_ssm_a_re, m_ssm_a_im, m_ssm_log_dt, m_ssm_b_re, m_ssm_b_im, m_ssm_c_re, m_ssm_c_im, m_ssm_d, m_w_glu, m_w_attn_out, m_w_out, m_norm_ffn_g, m_w_ffn_gate, m_w_ffn_up, m_w_ffn_down, m_norm_final_g, v_norm_mix_g, v_w_in, v_ssm_a_re, v_ssm_a_im, v_ssm_log_dt, v_ssm_b_re, v_ssm_b_im, v_ssm_c_re, v_ssm_c_im, v_ssm_d, v_w_glu, v_w_attn_out, v_w_out, v_norm_ffn_g, v_w_ffn_gate, v_w_ffn_up, v_w_ffn_down, v_norm_final_g):
    given = dict(locals())
    def local(name, prefix=""):
        t = given[prefix + name][0]
        return t.T if name in TRANSPOSED else t

    shard = {n: local(n) for n in BIG}
    shapes = {n: given[n].shape for n in WEIGHTS}

    small = {n: given[n] for n in SMALL}
    small_2d = dict(small)
    for n in ("ssm_a_re", "ssm_a_im", "ssm_b_re", "ssm_b_im", "ssm_c_re", "ssm_c_im", "ssm_d"):
        small_2d[n] = small[n][0]
    small_2d["norm_final_g"] = norm_final_g.reshape(1, D_MODEL)

    core = lax.axis_index("c").astype(jnp.int32).reshape(1)
    loss, grad_x, parts, ssm_shares, gs_norm = local_step(
        x.reshape(TOKENS, D_MODEL), loss_target.reshape(TOKENS, D_MODEL),
        {n: shard[n].astype(BF16) for n in BIG}, small_2d, core)

    (norm_shares,) = run_comm(small_comm(_pack_small(NORM_SMALL, gs_norm, NORM_ROWS, last=loss)), "gather_norm_grads")
    small_out = [{} for _ in range(4)]
    for names, rows, shares in ((SSM_SMALL, SSM_ROWS, ssm_shares), (NORM_SMALL, NORM_ROWS, norm_shares)):
        packs = [_pack_small(names, {n: given[p + n] for n in names}, rows) for p in ("", "m_", "v_")]
        for kind, t in enumerate(adam_small("adam_" + names[0], shares, *packs)):
            vals, after = _unpack_small(names, t, shapes)
            small_out[kind].update(vals)
            if kind == 0:
                total_loss = after

    big_out = {}
    for n in BIG:
        res = adam_big(n, parts[n], shard[n], local(n, "m_"), local(n, "v_"))
        big_out[n] = [(t.T if n in TRANSPOSED else t)[None] for t in res]

    outs = [total_loss, grad_x.reshape(LOCAL_BATCH, SEQ, D_MODEL)]
    for kind in range(4):
        for n in WEIGHTS:
            outs.append(big_out[n][kind] if n in BIG else small_out[kind][n])
    return tuple(outs)
```

```python
import functools
import math

import jax
import jax.numpy as jnp
from jax import lax
from jax.experimental import pallas as pl
from jax.experimental.pallas import tpu as pltpu

F32 = jnp.float32
BF16 = jnp.bfloat16
MESH = pl.DeviceIdType.MESH

D_MODEL = 1024
SEQ = 2048
LOCAL_BATCH = 2
TOKENS = LOCAL_BATCH * SEQ
HEAD_DIM = 64
HEADS_PER_GROUP = 4
GROUP_W = HEADS_PER_GROUP * HEAD_DIM
N_GROUPS = 3
DILATIONS = (1, 4, 16)
ATTN_BLOCK = 128
ROPE_DIM = 16
ROPE_THETA = 500000.0
QKV_W = 3 * N_GROUPS * GROUP_W
SSM_W = 512
SSM_STATE_W = 2048
SSM_LANE_BLOCKS = 4
GATE_W = 2 * D_MODEL
D_FF = 2816
RMS_EPS = 1e-6
NEG_INF = -1e30
ADAM_LR, ADAM_B1, ADAM_B2, ADAM_EPS, ADAM_WD, ADAM_STEP = 0.001, 0.9, 0.999, 1e-08, 0.01, 10
N_CHIPS = 4
N_DEV = 8

VMEM_LIMIT = 56 * 1024 * 1024
LANES = 128


def _params(sem=None):
    return pltpu.CompilerParams(dimension_semantics=sem, vmem_limit_bytes=VMEM_LIMIT)


def _pick(n, cap, align=LANES):
    best = None
    for d in range(align, min(n, cap) + 1, align):
        if n % d == 0:
            best = d
    return n if best is None or n <= cap else best


_DIMS = {"nn": (((1,), (0,)), ((), ())), "nt": (((1,), (1,)), ((), ())), "tn": (((0,), (0,)), ((), ()))}


def _dot(a, b, mode):
    return lax.dot_general(a, b, _DIMS[mode], preferred_element_type=F32)


def matmul(a, b, mode, out_dtype, name, add=None, comm=None):
    if mode == "nn":
        (m, k), n = a.shape, b.shape[1]
    elif mode == "nt":
        (m, k), n = a.shape, b.shape[0]
    else:
        (k, m), n = a.shape, b.shape[1]
    tn = _pick(n, 1408 if mode != "tn" else 512)
    tk = _pick(k, 2816) if mode != "tn" else k
    tm = _pick(m, 1408)
    out_bytes = jnp.dtype(out_dtype).itemsize

    def need(tm_):
        return 2 * 2 * (tm_ * tk + tk * tn) + tm_ * tn * (4 + 2 * out_bytes + (8 if add is not None else 0))

    while need(tm) > 40 * 1024 * 1024 and tm % 256 == 0:
        tm //= 2
    nk = k // tk
    a_spec = {"nn": pl.BlockSpec((tm, tk), lambda i, j, kk: (i, kk)),
              "nt": pl.BlockSpec((tm, tk), lambda i, j, kk: (i, kk)),
              "tn": pl.BlockSpec((tk, tm), lambda i, j, kk: (kk, i))}[mode]
    b_spec = {"nn": pl.BlockSpec((tk, tn), lambda i, j, kk: (kk, j)),
              "nt": pl.BlockSpec((tn, tk), lambda i, j, kk: (j, kk)),
              "tn": pl.BlockSpec((tk, tn), lambda i, j, kk: (kk, j))}[mode]
    o_spec = pl.BlockSpec((tm, tn), lambda i, j, kk: (i, j))

    def body(a_ref, b_ref, *rest):
        if add is not None:
            add_ref, o_ref, acc_ref = rest
        else:
            o_ref, acc_ref = rest
        part = _dot(a_ref[...], b_ref[...], mode)
        if nk == 1:
            res = part if add is None else part + add_ref[...]
            o_ref[...] = res.astype(out_dtype)
            return
        kk = pl.program_id(2)

        @pl.when(kk == 0)
        def _():
            acc_ref[...] = part

        @pl.when(kk > 0)
        def _():
            acc_ref[...] += part

        @pl.when(kk == nk - 1)
        def _():
            res = acc_ref[...] if add is None else acc_ref[...] + add_ref[...]
            o_ref[...] = res.astype(out_dtype)

    in_specs = [a_spec, b_spec] + ([o_spec] if add is not None else [])
    args = (a, b) + ((add,) if add is not None else ())
    res = hosted_call(
        body, comm, name, (m // tm, n // tn, nk), in_specs, [o_spec], [jax.ShapeDtypeStruct((m, n), out_dtype)],
        [pltpu.VMEM((tm, tn) if nk > 1 else (8, LANES), F32)], args, ("parallel", "parallel", "arbitrary"))
    return res[0] if comm is None else res


def matmul_rows(a, b, name, fn, extra, outs, accs=(), add=None, comm=None, tm=512):
    (m, k), n = a.shape, b.shape[1]
    n_fixed = 2 + (add is not None)
    row_spec = lambda cols: pl.BlockSpec((tm, cols), lambda i: (i, 0))
    in_specs = [row_spec(k), pl.BlockSpec((k, n), lambda i: (0, 0))] + ([row_spec(n)] if add is not None else [])
    in_specs += [pl.BlockSpec(e.shape, lambda i: (0, 0)) if e.shape[0] == 1 else row_spec(e.shape[1]) for e in extra]
    out_specs = [row_spec(c) for c, _ in outs] + [pl.BlockSpec((1, c), lambda i: (0, 0)) for c in accs]
    out_shape = [jax.ShapeDtypeStruct((m, c), dt) for c, dt in outs] + [jax.ShapeDtypeStruct((1, c), F32) for c in accs]

    def body(*refs):
        rows = _dot(refs[0][...], refs[1][...], "nn")
        if add is not None:
            rows = rows + refs[2][...]
        n_in = n_fixed + len(extra)
        res = fn(rows, *[r[...] for r in refs[n_fixed:n_in]])
        for r, v in zip(refs[n_in:n_in + len(outs)], res[:len(outs)]):
            r[...] = v.astype(r.dtype)
        first = pl.program_id(0) == 0
        for r, v in zip(refs[n_in + len(outs):], res[len(outs):]):
            @pl.when(first)
            def _(r=r, v=v):
                r[...] = v

            @pl.when(jnp.logical_not(first))
            def _(r=r, v=v):
                r[...] += v

    args = (a, b) + ((add,) if add is not None else ()) + tuple(extra)
    return hosted_call(body, comm, name, (m // tm,), in_specs, out_specs, out_shape, [], args, ("arbitrary",))


FFN_TM, FFN_TN = 512, 1408


def ffn_in(h2, wg_t, wu_t):
    def body(h_ref, wg_ref, wu_ref, a_ref, b_ref, act_ref):
        hv = h_ref[...]
        a, b = _dot(hv, wg_ref[...], "nt"), _dot(hv, wu_ref[...], "nt")
        a_ref[...] = a.astype(BF16)
        b_ref[...] = b.astype(BF16)
        act_ref[...] = _swiglu_fn(a, b).astype(BF16)

    rows = pl.BlockSpec((FFN_TM, D_MODEL), lambda i, j: (i, 0))
    wts = pl.BlockSpec((FFN_TN, D_MODEL), lambda i, j: (j, 0))
    out = pl.BlockSpec((FFN_TM, FFN_TN), lambda i, j: (i, j))
    return pl.pallas_call(
        body, name="ffn_in", grid=(TOKENS // FFN_TM, D_FF // FFN_TN), in_specs=[rows, wts, wts],
        out_specs=[out] * 3, out_shape=[jax.ShapeDtypeStruct((TOKENS, D_FF), BF16)] * 3,
        compiler_params=_params(("parallel", "parallel")),
    )(h2, wg_t, wu_t)


def ffn_in_bwd(dx2_b, wd, a, b):
    def body(dx_ref, wd_ref, a_ref, b_ref, da_ref, db_ref):
        dx = dx_ref[...]
        for lo in range(0, FFN_TN, 512):
            cols = slice(lo, min(lo + 512, FFN_TN))
            dact = _dot(dx, wd_ref[cols, :], "nt")
            _, vjp = jax.vjp(_swiglu_fn, a_ref[:, cols].astype(F32), b_ref[:, cols].astype(F32))
            da, db = vjp(dact)
            da_ref[:, cols] = da.astype(BF16)
            db_ref[:, cols] = db.astype(BF16)

    rows = pl.BlockSpec((FFN_TM, D_MODEL), lambda i, j: (i, 0))
    wts = pl.BlockSpec((FFN_TN, D_MODEL), lambda i, j: (j, 0))
    out = pl.BlockSpec((FFN_TM, FFN_TN), lambda i, j: (i, j))
    return pl.pallas_call(
        body, name="ffn_in_bwd", grid=(TOKENS // FFN_TM, D_FF // FFN_TN), in_specs=[rows, wts, out, out],
        out_specs=[out] * 2, out_shape=[jax.ShapeDtypeStruct((TOKENS, D_FF), BF16)] * 2,
        compiler_params=_params(("parallel", "parallel")),
    )(dx2_b, wd, a, b)


def mix_in_bwd(grads, weights, partial, x, g, skip, comm=None):
    n = len(grads)
    tm = 512

    def body(*refs):
        a_refs, b_refs = refs[:n], refs[n:2 * n]
        part_ref, x_ref, g_ref, skip_ref, gx_ref, dg_ref = refs[2 * n:]
        dh = part_ref[...]
        for a_ref, b_ref in zip(a_refs, b_refs):
            dh = dh + _dot(a_ref[...], b_ref[...], "nn")
        _, vjp = jax.vjp(_rms, x_ref[...], g_ref[...])
        dx, dg = vjp(dh)
        gx_ref[...] = dx + skip_ref[...]
        first = pl.program_id(0) == 0

        @pl.when(first)
        def _():
            dg_ref[...] = dg

        @pl.when(jnp.logical_not(first))
        def _():
            dg_ref[...] += dg

    rows = pl.BlockSpec((tm, D_MODEL), lambda i: (i, 0))
    gain = pl.BlockSpec((1, D_MODEL), lambda i: (0, 0))
    in_specs = [pl.BlockSpec((tm, a.shape[1]), lambda i: (i, 0)) for a in grads]
    in_specs += [pl.BlockSpec(b.shape, lambda i: (0, 0)) for b in weights]
    return hosted_call(
        body, comm, "mix_in_bwd", (TOKENS // tm,), in_specs + [rows, rows, gain, rows], [rows, gain],
        [jax.ShapeDtypeStruct((TOKENS, D_MODEL), F32), jax.ShapeDtypeStruct((1, D_MODEL), F32)], [],
        (*grads, *weights, partial, x, g, skip), ("arbitrary",))


def rowwise(fn, ins, outs, name, accs=(), tm=256, rows=TOKENS, comm=None):
    in_specs, args = [], []
    for item in ins:
        arr, width, blk = item if isinstance(item, tuple) else (item, None, 0)
        if arr.ndim == 3:
            for k in range(arr.shape[0]):
                in_specs.append(pl.BlockSpec((None, tm, arr.shape[2]), functools.partial(lambda i, k_: (k_, i, 0), k_=k)))
                args.append(arr)
            continue
        if arr.shape[0] == 1:
            in_specs.append(pl.BlockSpec(arr.shape, lambda i: (0, 0)))
        elif width is None:
            in_specs.append(pl.BlockSpec((tm, arr.shape[1]), lambda i: (i, 0)))
        else:
            in_specs.append(pl.BlockSpec((tm, width), functools.partial(lambda i, blk_: (i, blk_), blk_=blk)))
        args.append(arr)
    out_specs = [pl.BlockSpec((tm, c), lambda i: (i, 0)) for c, _ in outs]
    out_specs += [pl.BlockSpec((1, c), lambda i: (0, 0)) for c in accs]
    out_shape = [jax.ShapeDtypeStruct((rows, c), dt) for c, dt in outs]
    out_shape += [jax.ShapeDtypeStruct((1, c), F32) for c in accs]
    n_in, n_out = len(args), len(outs)
    c_ins, c_outs, c_sems = _comm_operands(comm)

    def body(*refs):
        refs, c_refs = _comm_refs(comm, refs, n_in, n_out + len(accs))
        step = pl.program_id(0)
        _comm_begin(comm, c_refs, step, rows // tm)
        res = fn(*[r[...] for r in refs[:n_in]])
        for r, v in zip(refs[n_in:n_in + n_out], res[:n_out]):
            r[...] = v.astype(r.dtype)
        first = step == 0
        for r, v in zip(refs[n_in + n_out:], res[n_out:]):
            @pl.when(first)
            def _(r=r, v=v):
                r[...] = v

            @pl.when(jnp.logical_not(first))
            def _(r=r, v=v):
                r[...] += v
        _comm_end(comm, c_refs, step, rows // tm)

    return pl.pallas_call(
        body, name=name, grid=(rows // tm,), in_specs=in_specs + [ANY] * len(c_ins),
        out_specs=out_specs + [ANY] * len(c_outs), out_shape=out_shape + c_outs, scratch_shapes=c_sems,
        compiler_params=_params(("arbitrary",)),
    )(*args, *c_ins)


def _rms(x, g):
    return x * lax.rsqrt(jnp.mean(x * x, axis=-1, keepdims=True) + RMS_EPS) * g


def _colsum(v):
    return jnp.sum(v, axis=0, keepdims=True)


PAIR_W = 2 * HEAD_DIM
N_PAIRS = HEADS_PER_GROUP // 2


def _qkv_order(w_t, back=False):
    dims = (N_PAIRS, N_GROUPS, 3) if back else (3, N_GROUPS, N_PAIRS)
    return w_t.reshape(dims + (PAIR_W, w_t.shape[1])).transpose(2, 1, 0, 3, 4).reshape(QKV_W, w_t.shape[1])


def _rope_tables():
    half = ROPE_DIM // 2
    inv = jnp.power(jnp.float32(ROPE_THETA), -jnp.arange(half, dtype=F32) * 2.0 / ROPE_DIM)
    ang = jnp.arange(SEQ, dtype=F32)[:, None] * inv[None, :]
    cos, sin = jnp.cos(ang), jnp.sin(ang)
    zeros = jnp.zeros((SEQ, HEAD_DIM - ROPE_DIM), F32)
    zh = jnp.zeros((SEQ, half), F32)
    c = jnp.concatenate([cos, cos, zeros + 1.0], axis=1)
    sa = jnp.concatenate([-sin, zh, zeros], axis=1)
    sb = jnp.concatenate([zh, sin, zeros], axis=1)
    return [jnp.tile(t, (1, 2)) for t in (c, sa, sb)]


def _rope_fwd(x, c, sa, sb):
    return x * c + pltpu.roll(x, PAIR_W - 8, 1) * sa + pltpu.roll(x, 8, 1) * sb


def _rope_bwd(dy, c, sa, sb):
    return dy * c + pltpu.roll(dy * sb, PAIR_W - 8, 1) + pltpu.roll(dy * sa, 8, 1)


def _band_masks():
    row = lax.broadcasted_iota(jnp.int32, (ATTN_BLOCK, ATTN_BLOCK), 0)
    col = lax.broadcasted_iota(jnp.int32, (ATTN_BLOCK, ATTN_BLOCK), 1)
    return col <= row, col >= row


def _stack_rows(t):
    return jnp.concatenate([t, t], axis=0)


def _stack_heads(t, first_head):
    return jnp.concatenate([jnp.where(first_head, t, 0), jnp.where(first_head, 0, t)], axis=0)


def _per_head(fn):
    return jnp.concatenate([fn(slice(h * HEAD_DIM, (h + 1) * HEAD_DIM)) for h in range(2)], axis=1)


def _slab_spec(kind):
    return pl.BlockSpec((None, SEQ, PAIR_W), lambda b, p, g: (b, 0, p * 3 * N_GROUPS + g * 3 + kind))


_TABLE_SPEC = pl.BlockSpec((SEQ, PAIR_W), lambda b, p, g: (0, 0))
_PAIR_SPEC = pl.BlockSpec((None, SEQ, PAIR_W), lambda b, p, g: (b, 0, p))


def _block_rows(dil, r, n):
    return pl.ds(n * (ATTN_BLOCK * dil) + r, ATTN_BLOCK, stride=dil)


def proj_qkv(h, w_qkv_t, tables, comm=None):
    tm = 1024
    pair_w = QKV_W // N_PAIRS
    scale = HEAD_DIM ** -0.5

    def body(h_ref, w_ref, c_ref, sa_ref, sb_ref, o_ref):
        rows = _dot(h_ref[...], w_ref[...], "nt")
        c, sa, sb = c_ref[...], sa_ref[...], sb_ref[...]
        for blk in range(pair_w // PAIR_W):
            cols = slice(blk * PAIR_W, (blk + 1) * PAIR_W)
            x = rows[:, cols]
            if blk % 3 == 0:
                x = _rope_fwd(x, c, sa, sb) * scale
            elif blk % 3 == 1:
                x = _rope_fwd(x, c, sa, sb)
            o_ref[:, cols] = x

    table = pl.BlockSpec((tm, PAIR_W), lambda i, j, : (i % (SEQ // tm), 0))
    res = hosted_call(
        body, comm, "proj_qkv", (TOKENS // tm, N_PAIRS),
        [pl.BlockSpec((tm, D_MODEL), lambda i, j: (i, 0)), pl.BlockSpec((pair_w, D_MODEL), lambda i, j: (j, 0)),
         table, table, table],
        [pl.BlockSpec((tm, pair_w), lambda i, j: (i, j))], [jax.ShapeDtypeStruct((TOKENS, QKV_W), F32)], [],
        (h, w_qkv_t, *tables), ("parallel", "parallel"))
    return res[0] if comm is None else res


def attn_fwd(qkv, comm=None):
    def body(qs, ks, v_ref, attn_b_ref, attn_ref, lse_ref, o0, o1, o2, l0, l1, l2):
        g = pl.program_id(2)
        cur_mask, prev_mask = _band_masks()
        first_head = lax.broadcasted_iota(jnp.int32, (ATTN_BLOCK, PAIR_W), 1) < HEAD_DIM

        def run(dil, o_slab, l_slab):
            nb = SEQ // dil // ATTN_BLOCK

            def block(idx, carry):
                r, n = lax.div(idx, nb), lax.rem(idx, nb)
                cur, prev = _block_rows(dil, r, n), _block_rows(dil, r, jnp.maximum(n - 1, 0))
                q = qs[cur, :].astype(BF16)
                kc, kp = ks[cur, :].astype(BF16), ks[prev, :].astype(BF16)
                vc, vp = v_ref[cur, :].astype(BF16), v_ref[prev, :].astype(BF16)
                q2 = _stack_heads(q, first_head)
                mask = _stack_rows(jnp.concatenate([jnp.logical_and(prev_mask, n > 0), cur_mask], axis=1))
                s2 = jnp.where(mask, _dot(q2, jnp.concatenate([kp, kc], axis=0), "nt"), NEG_INF)
                m = jnp.max(s2, axis=-1, keepdims=True)
                vcat, two = jnp.concatenate([vp, vc], axis=0), _stack_rows(first_head)
                vext = jnp.concatenate([jnp.where(two, vcat, 1), jnp.where(two, 1, vcat)], axis=1)
                r2 = _dot(jnp.exp(s2 - m).astype(BF16), vext, "nn")
                r0, r1 = r2[:ATTN_BLOCK, :PAIR_W], r2[ATTN_BLOCK:, PAIR_W:]
                num = jnp.where(first_head, r0, r1)
                den = pltpu.roll(jnp.where(first_head, r1, r0), HEAD_DIM, 1)
                o_slab[cur, :] = num / den
                l_slab[cur, :] = jnp.where(first_head, m[:ATTN_BLOCK], m[ATTN_BLOCK:]) + jnp.log(den)
                return carry

            lax.fori_loop(0, SEQ // ATTN_BLOCK, block, 0, unroll=4)

        for gi, (o_slab, l_slab) in enumerate(((o0, l0), (o1, l1), (o2, l2))):
            @pl.when(g == gi)
            def _(gi=gi, o_slab=o_slab, l_slab=l_slab):
                run(DILATIONS[gi], o_slab, l_slab)

        @pl.when(g == N_GROUPS - 1)
        def _():
            a, b, cc = l0[...], l1[...], l2[...]
            m = jnp.maximum(jnp.maximum(a, b), cc)
            e0, e1, e2 = jnp.exp(a - m), jnp.exp(b - m), jnp.exp(cc - m)
            tot = e0 + e1 + e2
            attn = (e0 * o0[...] + e1 * o1[...] + e2 * o2[...]) / tot
            attn_ref[...] = attn
            attn_b_ref[...] = attn.astype(BF16)
            lse_ref[...] = m + jnp.log(tot)

    shape = (LOCAL_BATCH, SEQ, GROUP_W)
    slab = pltpu.VMEM((SEQ, PAIR_W), F32)
    return hosted_call(
        body, comm, "attn_fwd", (LOCAL_BATCH, N_PAIRS, N_GROUPS),
        [_slab_spec(0), _slab_spec(1), _slab_spec(2)], [_PAIR_SPEC] * 3,
        [jax.ShapeDtypeStruct(shape, BF16), jax.ShapeDtypeStruct(shape, F32), jax.ShapeDtypeStruct(shape, F32)],
        [slab] * 6, (qkv, qkv, qkv), ("parallel", "parallel", "arbitrary"))


def attn_bwd(qkv, tables, dattn, attn, lse, comm=None):
    scale = HEAD_DIM ** -0.5

    def body(qs, ks, v_ref, c_ref, sa_ref, sb_ref, do_ref, out_ref, lse_ref, dqkv_ref, dl, dq_s, dk_s, dv_s):
        g = pl.program_id(2)
        c, sa, sb = c_ref[...], sa_ref[...], sb_ref[...]

        @pl.when(g == 0)
        def _():
            prod = do_ref[...] * out_ref[...]
            dl[...] = _per_head(
                lambda sl: jnp.broadcast_to(jnp.sum(prod[:, sl], axis=-1, keepdims=True), (SEQ, HEAD_DIM)))

        cur_mask, prev_mask = _band_masks()
        first_head = lax.broadcasted_iota(jnp.int32, (ATTN_BLOCK, PAIR_W), 1) < HEAD_DIM

        def run(dil):
            nb = SEQ // dil // ATTN_BLOCK

            def block(idx, carry):
                r, n = lax.div(idx, nb), lax.rem(idx, nb)
                cur = _block_rows(dil, r, n)
                prev = _block_rows(dil, r, jnp.maximum(n - 1, 0))
                nxt = _block_rows(dil, r, jnp.minimum(n + 1, nb - 1))
                q0, q1 = qs[cur, :].astype(BF16), qs[nxt, :].astype(BF16)
                kp, kc = ks[prev, :].astype(BF16), ks[cur, :].astype(BF16)
                vp, vc = v_ref[prev, :].astype(BF16), v_ref[cur, :].astype(BF16)
                do0, do1 = do_ref[cur, :].astype(BF16), do_ref[nxt, :].astype(BF16)
                lse0, lse1, dl0, dl1 = lse_ref[cur, :], lse_ref[nxt, :], dl[cur, :], dl[nxt, :]
                has_prev = jnp.logical_and(prev_mask, n > 0)
                has_next = jnp.logical_and(prev_mask, n < nb - 1)

                def per_row(t):
                    return jnp.concatenate([t[:, 0:1], t[:, HEAD_DIM:HEAD_DIM + 1]], axis=0)

                q20, q21 = _stack_heads(q0, first_head), _stack_heads(q1, first_head)
                do20, do21 = _stack_heads(do0, first_head), _stack_heads(do1, first_head)
                kcat, vcat = jnp.concatenate([kp, kc], axis=0), jnp.concatenate([vp, vc], axis=0)
                mask0 = _stack_rows(jnp.concatenate([has_prev, cur_mask], axis=1))
                p0 = jnp.where(mask0, jnp.exp(_dot(q20, kcat, "nt") - per_row(lse0)), 0.0)
                ds0 = (p0 * (_dot(do20, vcat, "nt") - per_row(dl0))).astype(BF16)
                p1 = jnp.where(_stack_rows(has_next), jnp.exp(_dot(q21, kc, "nt") - per_row(lse1)), 0.0)
                ds1 = (p1 * (_dot(do21, vc, "nt") - per_row(dl1))).astype(BF16)
                dq2 = _dot(ds0, kcat, "nn")
                dq_s[cur, :] = jnp.where(first_head, dq2[:ATTN_BLOCK], dq2[ATTN_BLOCK:])
                ds_cur = jnp.concatenate([ds0[:, ATTN_BLOCK:], ds1], axis=0)
                p_cur = jnp.concatenate([p0[:, ATTN_BLOCK:], p1], axis=0).astype(BF16)
                dk_s[cur, :] = _dot(ds_cur, jnp.concatenate([q20, q21], axis=0), "tn")
                dv_s[cur, :] = _dot(p_cur, jnp.concatenate([do20, do21], axis=0), "tn")
                return carry

            lax.fori_loop(0, SEQ // ATTN_BLOCK, block, 0, unroll=2)

        for gi in range(N_GROUPS):
            @pl.when(g == gi)
            def _(gi=gi):
                run(DILATIONS[gi])

        dqkv_ref[:, 0:PAIR_W] = _rope_bwd(dq_s[...] * scale, c, sa, sb).astype(BF16)
        dqkv_ref[:, PAIR_W:2 * PAIR_W] = _rope_bwd(dk_s[...], c, sa, sb).astype(BF16)
        dqkv_ref[:, 2 * PAIR_W:] = dv_s[...].astype(BF16)

    slab = pltpu.VMEM((SEQ, PAIR_W), F32)
    return hosted_call(
        body, comm, "attn_bwd", (LOCAL_BATCH, N_PAIRS, N_GROUPS),
        [_slab_spec(0), _slab_spec(1), _slab_spec(2), _TABLE_SPEC, _TABLE_SPEC, _TABLE_SPEC,
         _PAIR_SPEC, _PAIR_SPEC, _PAIR_SPEC],
        [pl.BlockSpec((None, SEQ, 3 * PAIR_W), lambda b, p, g: (b, 0, p * N_GROUPS + g))],
        [jax.ShapeDtypeStruct((LOCAL_BATCH, SEQ, QKV_W), BF16)],
        [slab] * 4, (qkv, qkv, qkv, *tables, dattn, attn, lse), ("parallel", "parallel", "arbitrary"))


def _discretize(lr, li, log_dt, br, bi):
    dt = jnp.exp(log_dt)
    mag = jnp.exp(lr * dt)
    ab_re, ab_im = mag * jnp.cos(li * dt), mag * jnp.sin(li * dt)
    den = lr * lr + li * li
    nr, ni = ab_re - 1.0, ab_im
    f_re = (nr * lr + ni * li) / den
    f_im = (ni * lr - nr * li) / den
    return ab_re, ab_im, f_re[None] * br - f_im[None] * bi, f_re[None] * bi + f_im[None] * br


def ssm_prep(lr, li, log_dt, br, bi):
    def body(lr_ref, li_ref, dt_ref, br_ref, bi_ref, *outs):
        for o, v in zip(outs, _discretize(lr_ref[...], li_ref[...], dt_ref[...], br_ref[...], bi_ref[...])):
            o[...] = v
    shapes = [lr, li, br, bi]
    return pl.pallas_call(body, name="ssm_prep",
                          out_shape=[jax.ShapeDtypeStruct(s.shape, F32) for s in shapes])(lr, li, log_dt, br, bi)


def ssm_prep_bwd(lr, li, log_dt, br, bi, g_ab_re, g_ab_im, g_bb_re, g_bb_im):
    def body(lr_ref, li_ref, dt_ref, br_ref, bi_ref, g0, g1, g2, g3, *outs):
        _, vjp = jax.vjp(_discretize, lr_ref[...], li_ref[...], dt_ref[...], br_ref[...], bi_ref[...])
        for o, v in zip(outs, vjp((g0[...], g1[...], g2[...], g3[...]))):
            o[...] = v
    shapes = [lr, li, log_dt, br, bi]
    return pl.pallas_call(body, name="ssm_prep_bwd",
                          out_shape=[jax.ShapeDtypeStruct(s.shape, F32) for s in shapes])(
        lr, li, log_dt, br, bi, g_ab_re, g_ab_im, g_bb_re, g_bb_im)


def _block_diag(t):
    per = SSM_STATE_W // SSM_LANE_BLOCKS // 64
    g = t.transpose(1, 0, 2).reshape(SSM_LANE_BLOCKS, per, 16, 64)
    eye = jnp.eye(per, dtype=t.dtype)
    return jnp.einsum("jgcn,gh->jgchn", g, eye).reshape(SSM_LANE_BLOCKS, per * 16, per * 64)


def _block_diag_t(m):
    per = SSM_STATE_W // SSM_LANE_BLOCKS // 64
    m5 = m.reshape(SSM_LANE_BLOCKS, per, 16, per, 64)
    d = jnp.einsum("jgchn,gh->jgcn", m5, jnp.eye(per, dtype=m.dtype))
    return d.reshape(SSM_LANE_BLOCKS * per, 16, 64).transpose(1, 0, 2)


def _cmul(ar, ai, br, bi):
    return ar * br - ai * bi, ar * bi + ai * br


def _power_tables(ar, ai, reverse):
    width = ar.shape[1]
    row = lax.broadcasted_iota(jnp.int32, (8, width), 0)
    pows = [(ar, ai)]
    for _ in range(7):
        pows.append(_cmul(pows[-1][0], pows[-1][1], ar, ai))
    steps = []
    for k in (1, 2, 4):
        keep = (row >= k) if not reverse else (row < 8 - k)
        steps.append((jnp.where(keep, pows[k - 1][0], 0.0), jnp.where(keep, pows[k - 1][1], 0.0)))
    cr = jnp.zeros((8, width), F32)
    ci = jnp.zeros((8, width), F32)
    for i in range(8):
        pr, pi = pows[i] if not reverse else pows[7 - i]
        cr = jnp.where(row == i, pr, cr)
        ci = jnp.where(row == i, pi, ci)
    return steps, (cr, ci)


SCAN_CHUNK = 2048
STATE_BLOCK = SSM_STATE_W // SSM_LANE_BLOCKS
CHAN_BLOCK = SSM_W // SSM_LANE_BLOCKS


def ssm_fwd(u, ab_re, ab_im, bb_re, bb_im, cb_re, cb_im, d_skip, comm=None):
    nt = SEQ // SCAN_CHUNK
    chan = pl.BlockSpec((None, SCAN_CHUNK, CHAN_BLOCK), lambda b, j, t: (b, t, j))
    state = pl.BlockSpec((None, SCAN_CHUNK, STATE_BLOCK), lambda b, j, t: (b, t, j))
    mat = pl.BlockSpec((None, CHAN_BLOCK, STATE_BLOCK), lambda b, j, t: (j, 0, 0))
    lane = pl.BlockSpec((1, STATE_BLOCK), lambda b, j, t: (0, j))
    dsp = pl.BlockSpec((1, CHAN_BLOCK), lambda b, j, t: (0, j))

    def body(u_ref, ar_ref, ai_ref, bbr_ref, bbi_ref, cbr_ref, cbi_ref, d_ref, y_ref, yg_ref, xr_ref, xi_ref,
             car_r, car_i):
        @pl.when(pl.program_id(2) == 0)
        def _():
            car_r[...] = jnp.zeros_like(car_r)
            car_i[...] = jnp.zeros_like(car_i)

        steps, (pr, pi) = _power_tables(ar_ref[...], ai_ref[...], reverse=False)
        uf = u_ref[...]
        ub = uf.astype(BF16)
        xr_ref[...] = _dot(ub, bbr_ref[...], "nn")
        xi_ref[...] = _dot(ub, bbi_ref[...], "nn")

        def tile(i, carry):
            cr, ci = carry
            sl = pl.ds(pl.multiple_of(i * 8, 8), 8)
            br, bi = xr_ref[sl, :], xi_ref[sl, :]
            for k, (sr, si) in zip((1, 2, 4), steps):
                tr, ti = _cmul(sr, si, pltpu.roll(br, k, 0), pltpu.roll(bi, k, 0))
                br, bi = br + tr, bi + ti
            tr, ti = _cmul(pr, pi, cr, ci)
            br, bi = br + tr, bi + ti
            xr_ref[sl, :] = br
            xi_ref[sl, :] = bi
            return br[7:8, :], bi[7:8, :]

        cr, ci = lax.fori_loop(0, SCAN_CHUNK // 8, tile, (car_r[0:1, :], car_i[0:1, :]), unroll=4)
        car_r[0:1, :] = cr
        car_i[0:1, :] = ci
        y = (_dot(xr_ref[...].astype(BF16), cbr_ref[...], "nt") - _dot(xi_ref[...].astype(BF16), cbi_ref[...], "nt")
             + d_ref[...] * uf)
        y_ref[...] = y
        yg_ref[...] = jax.nn.gelu(y).astype(BF16)

    return hosted_call(
        body, comm, "ssm_fwd", (LOCAL_BATCH, SSM_LANE_BLOCKS, nt),
        [chan, lane, lane, mat, mat, mat, mat, dsp], [chan, chan, state, state],
        [jax.ShapeDtypeStruct((LOCAL_BATCH, SEQ, SSM_W), F32), jax.ShapeDtypeStruct((LOCAL_BATCH, SEQ, SSM_W), BF16),
         jax.ShapeDtypeStruct((LOCAL_BATCH, SEQ, SSM_STATE_W), F32),
         jax.ShapeDtypeStruct((LOCAL_BATCH, SEQ, SSM_STATE_W), F32)],
        [pltpu.VMEM((8, STATE_BLOCK), F32), pltpu.VMEM((8, STATE_BLOCK), F32)],
        (u, ab_re, ab_im, bb_re, bb_im, cb_re, cb_im, d_skip), ("parallel", "parallel", "arbitrary"))


def ssm_bwd(dyg, y, u, xr, xi, ab_re, ab_im, bb_re, bb_im, cb_re, cb_im, d_skip, comm=None):
    nt = SEQ // SCAN_CHUNK
    ntile = SCAN_CHUNK // 8

    def rev(t):
        return nt - 1 - t

    chan = pl.BlockSpec((None, SCAN_CHUNK, CHAN_BLOCK), lambda j, b, t: (b, rev(t), j))
    state = pl.BlockSpec((None, SCAN_CHUNK, STATE_BLOCK), lambda j, b, t: (b, rev(t), j))
    before = pl.BlockSpec((None, 8, STATE_BLOCK), lambda j, b, t: (b, jnp.maximum(rev(t) * ntile - 1, 0), j))
    mat = pl.BlockSpec((None, CHAN_BLOCK, STATE_BLOCK), lambda j, b, t: (j, 0, 0))
    lane = pl.BlockSpec((1, STATE_BLOCK), lambda j, b, t: (0, j))
    lane8 = pl.BlockSpec((8, STATE_BLOCK), lambda j, b, t: (0, j))
    dsp = pl.BlockSpec((1, CHAN_BLOCK), lambda j, b, t: (0, j))

    def body(dyg_ref, y_ref, u_ref, xr_ref, xi_ref, xrb_ref, xib_ref, ar_ref, ai_ref, bbr_ref, bbi_ref, cbr_ref,
             cbi_ref, d_ref, du_ref, dcbr_ref, dcbi_ref, dbbr_ref, dbbi_ref, dd_ref, dar_ref, dai_ref,
             lam_r, lam_i, car_r, car_i):
        b, t = pl.program_id(1), pl.program_id(2)
        first = jnp.logical_and(b == 0, t == 0)

        @pl.when(t == 0)
        def _():
            car_r[...] = jnp.zeros_like(car_r)
            car_i[...] = jnp.zeros_like(car_i)

        @pl.when(first)
        def _():
            for r in (dcbr_ref, dcbi_ref, dbbr_ref, dbbi_ref, dd_ref, dar_ref, dai_ref):
                r[...] = jnp.zeros_like(r)

        steps, (pr, pi) = _power_tables(ar_ref[...], -ai_ref[...], reverse=True)
        uf = u_ref[...]
        _, gelu_vjp = jax.vjp(jax.nn.gelu, y_ref[...])
        dy = gelu_vjp(dyg_ref[...])[0]
        dyb = dy.astype(BF16)
        dd_ref[...] += _colsum(dy * uf)
        lam_r[...] = _dot(dyb, cbr_ref[...], "nn")
        lam_i[...] = -_dot(dyb, cbi_ref[...], "nn")
        dcbr_ref[...] += _dot(dyb, xr_ref[...].astype(BF16), "tn")
        dcbi_ref[...] -= _dot(dyb, xi_ref[...].astype(BF16), "tn")
        row0 = lax.broadcasted_iota(jnp.int32, (8, STATE_BLOCK), 0) == 0
        has_before = rev(t) > 0
        xrb = jnp.where(has_before, xrb_ref[...], 0.0)
        xib = jnp.where(has_before, xib_ref[...], 0.0)

        def tile(s, carry):
            cr, ci, acc_r, acc_i = carry
            i = ntile - 1 - s
            sl = pl.ds(pl.multiple_of(i * 8, 8), 8)
            gr, gi = lam_r[sl, :], lam_i[sl, :]
            for k, (sr, si) in zip((1, 2, 4), steps):
                tr, ti = _cmul(sr, si, pltpu.roll(gr, 8 - k, 0), pltpu.roll(gi, 8 - k, 0))
                gr, gi = gr + tr, gi + ti
            tr, ti = _cmul(pr, pi, cr, ci)
            gr, gi = gr + tr, gi + ti
            lam_r[sl, :] = gr
            lam_i[sl, :] = gi
            sp = pl.ds(pl.multiple_of(jnp.maximum(i - 1, 0) * 8, 8), 8)
            pvr = jnp.where(i > 0, xr_ref[sp, :], xrb)
            pvi = jnp.where(i > 0, xi_ref[sp, :], xib)
            xsr = jnp.where(row0, pltpu.roll(pvr, 1, 0), pltpu.roll(xr_ref[sl, :], 1, 0))
            xsi = jnp.where(row0, pltpu.roll(pvi, 1, 0), pltpu.roll(xi_ref[sl, :], 1, 0))
            acc_r = acc_r + xsr * gr + xsi * gi
            acc_i = acc_i + xsr * gi - xsi * gr
            return gr[0:1, :], gi[0:1, :], acc_r, acc_i

        zero = jnp.zeros((8, STATE_BLOCK), F32)
        cr, ci, acc_r, acc_i = lax.fori_loop(0, ntile, tile, (car_r[0:1, :], car_i[0:1, :], zero, zero), unroll=2)
        car_r[0:1, :] = cr
        car_i[0:1, :] = ci
        dar_ref[...] += acc_r
        dai_ref[...] += acc_i
        lrb, lib = lam_r[...].astype(BF16), lam_i[...].astype(BF16)
        du = _dot(lrb, bbr_ref[...], "nt") + _dot(lib, bbi_ref[...], "nt") + d_ref[...] * dy
        du_ref[...] = du.astype(BF16)
        ub = uf.astype(BF16)
        dbbr_ref[...] += _dot(ub, lrb, "tn")
        dbbi_ref[...] += _dot(ub, lib, "tn")

    mat_shape = jax.ShapeDtypeStruct((SSM_LANE_BLOCKS, CHAN_BLOCK, STATE_BLOCK), F32)
    return hosted_call(
        body, comm, "ssm_bwd", (SSM_LANE_BLOCKS, LOCAL_BATCH, nt),
        [chan, chan, chan, state, state, before, before, lane, lane, mat, mat, mat, mat, dsp],
        [chan, mat, mat, mat, mat, dsp, lane8, lane8],
        [jax.ShapeDtypeStruct((LOCAL_BATCH, SEQ, SSM_W), BF16), mat_shape, mat_shape, mat_shape, mat_shape,
         jax.ShapeDtypeStruct((1, SSM_W), F32), jax.ShapeDtypeStruct((8, SSM_STATE_W), F32),
         jax.ShapeDtypeStruct((8, SSM_STATE_W), F32)],
        [pltpu.VMEM((SCAN_CHUNK, STATE_BLOCK), F32), pltpu.VMEM((SCAN_CHUNK, STATE_BLOCK), F32),
         pltpu.VMEM((8, STATE_BLOCK), F32), pltpu.VMEM((8, STATE_BLOCK), F32)],
        (dyg, y, u, xr, xi, xr, xi, ab_re, ab_im, bb_re, bb_im, cb_re, cb_im, d_skip),
        ("parallel", "arbitrary", "arbitrary"))


def _merge_fn(g0, g1, attn_d, za, zb):
    return jax.nn.sigmoid(g0) * attn_d + jax.nn.sigmoid(g1) * (za * jax.nn.sigmoid(zb))


def _swiglu_fn(a, b):
    return jax.nn.silu(a) * b


def _reduce_start(names, gw, shard_shapes):
    return swap_comm([_to_slots(n, gw[n], shard_shapes[n]) for n in names])


def _reduce_chip(names, swap, got, core):
    return exchange_comm([add_halves(n, g, r, core) for n, g, r in zip(names, swap.ins, got)])


def local_step(x, target, shards, small, core):
    g_mix, g_ffn, g_final = small["norm_mix_g"], small["norm_ffn_g"], small["norm_final_g"]
    tables = _rope_tables()
    seqs = lambda t: t.reshape(LOCAL_BATCH, SEQ, t.shape[-1])
    toks = lambda t: t.reshape(TOKENS, t.shape[-1])
    shard_shapes = {n: s.shape for n, s in shards.items()}
    w = {}

    def gather(names):
        return gather_comm([shards[n] for n in names])

    def arrived(names, slots):
        for n, s in zip(names, slots):
            w[n] = _from_slots(n, s)

    h, *slots = rowwise(lambda xv, g: (_rms(xv, g),), [x, g_mix], [(D_MODEL, BF16)], "norm_mix", comm=gather(["w_in"]))
    arrived(["w_in"], slots)
    w_qkv, w_u, w_gate = _qkv_order(w["w_in"][:QKV_W]), w["w_in"][QKV_W:QKV_W + SSM_W], w["w_in"][QKV_W + SSM_W:]
    qkv, *slots = proj_qkv(h, w_qkv, tables, comm=gather(["w_attn_out", "w_glu"]))
    arrived(["w_attn_out", "w_glu"], slots)
    qkv = seqs(qkv)
    u = seqs(matmul(h, w_u, "nt", F32, "proj_u"))
    gl, *slots = matmul(h, w_gate, "nt", BF16, "proj_gate", comm=gather(["w_out"]))
    arrived(["w_out"], slots)
    attn_b, attn, lse, *slots = attn_fwd(qkv, comm=gather(["w_ffn_gate"]))
    arrived(["w_ffn_gate"], slots)
    attn_b = toks(attn_b)
    attn_d = matmul(attn_b, w["w_attn_out"], "nn", F32, "attn_out")

    br_t = small["ssm_b_re"].transpose(2, 0, 1)
    bi_t = small["ssm_b_im"].transpose(2, 0, 1)
    log_dt = small["ssm_log_dt"].reshape(32, 1)
    ab_re, ab_im, bb_re_t, bb_im_t = ssm_prep(small["ssm_a_re"], small["ssm_a_im"], log_dt, br_t, bi_t)
    ab = [ab_re.reshape(1, SSM_STATE_W), ab_im.reshape(1, SSM_STATE_W)]
    bb = [_block_diag(bb_re_t).astype(BF16), _block_diag(bb_im_t).astype(BF16)]
    cb = [_block_diag(small["ssm_c_re"].transpose(1, 0, 2)).astype(BF16),
          _block_diag(small["ssm_c_im"].transpose(1, 0, 2)).astype(BF16)]
    d_skip = small["ssm_d"].reshape(1, SSM_W)
    ffn_rest = ["w_ffn_up", "w_ffn_down"]
    y, yg, xr, xi, *slots = ssm_fwd(u, *ab, *bb, *cb, d_skip, comm=gather(ffn_rest))
    arrived(ffn_rest, slots)
    yg2 = toks(yg)
    z = matmul(yg2, w["w_glu"], "nn", BF16, "glu")
    gate_ins = [(gl, D_MODEL, 0), (gl, D_MODEL, 1), attn_d, (z, D_MODEL, 0), (z, D_MODEL, 1)]
    (merged,) = rowwise(lambda *v: (_merge_fn(*[t.astype(F32) for t in v]),), gate_ins, [(D_MODEL, BF16)], "merge")
    x1, h2 = matmul_rows(merged, w["w_out"], "out_proj", lambda rows, g: (rows, _rms(rows, g)), [g_ffn],
                         [(D_MODEL, F32), (D_MODEL, BF16)], add=x)
    a, b, act = ffn_in(h2, w["w_ffn_gate"], w["w_ffn_up"])

    def final_fn(xv, g, tgt):
        yv, vjp = jax.vjp(_rms, xv, g)
        err = yv - tgt
        dx, dg = vjp(err * (1.0 / D_MODEL))
        loss = 0.5 * jnp.sum(jnp.mean(err * err, axis=-1, keepdims=True), axis=0, keepdims=True)
        return dx, dx, dg, jnp.broadcast_to(loss, (1, LANES))

    dx2, dx2_b, dg_final, loss = matmul_rows(act, w["w_ffn_down"], "ffn_down_loss", final_fn, [g_final, target],
                                             [(D_MODEL, F32), (D_MODEL, BF16)], accs=(D_MODEL, LANES), add=x1)
    gw, parts = {}, {}
    gw["w_ffn_down"] = matmul(act, dx2_b, "tn", F32, "d_ffn_down")
    da_b, db_b = ffn_in_bwd(dx2_b, w["w_ffn_down"], a, b)
    gw["w_ffn_gate"] = matmul(da_b, h2, "tn", F32, "d_ffn_gate")
    gw["w_ffn_up"] = matmul(db_b, h2, "tn", F32, "d_ffn_up")
    ffn = ["w_ffn_down", "w_ffn_gate", "w_ffn_up"]
    swap = _reduce_start(ffn[:2], gw, shard_shapes)
    dh2, *got = matmul(da_b, w["w_ffn_gate"], "nn", F32, "d_h2_gate", comm=swap)
    ffn_exchange = [_reduce_chip(ffn[:2], swap, got, core)]
    swap = _reduce_start(ffn[2:], gw, shard_shapes)

    def norm_bwd(dh, xv, g, skip):
        _, vjp = jax.vjp(_rms, xv, g)
        dx, dg = vjp(dh)
        dx = dx + skip
        return dx, dx, dg

    dx1, dx1_b, dg_ffn, *got = matmul_rows(db_b, w["w_ffn_up"], "d_h2_up_norm", norm_bwd, [x1, g_ffn, dx2],
                                           [(D_MODEL, F32), (D_MODEL, BF16)], accs=(D_MODEL,), add=dh2, comm=swap)
    ffn_up_exchange = _reduce_chip(ffn[2:], swap, got, core)
    gw["w_out"] = matmul(merged, dx1_b, "tn", F32, "d_out")
    dmerged = matmul(dx1_b, w["w_out"], "nt", F32, "d_merged")

    def merge_bwd(g0, g1, ad, za, zb, dm):
        _, vjp = jax.vjp(_merge_fn, *[t.astype(F32) for t in (g0, g1, ad, za, zb)])
        dg0, dg1, dad, dza, dzb = vjp(dm)
        return jnp.concatenate([dg0, dg1], axis=1), dad, jnp.concatenate([dza, dzb], axis=1)

    dgl_b, dattn_d_b, dz_b, parts["w_ffn_up"] = rowwise(
        merge_bwd, gate_ins + [dmerged], [(GATE_W, BF16), (D_MODEL, BF16), (GATE_W, BF16)], "merge_bwd",
        comm=ffn_up_exchange)
    gw["w_attn_out"] = matmul(attn_b, dattn_d_b, "tn", F32, "d_attn_out")
    dattn = seqs(matmul(dattn_d_b, w["w_attn_out"], "nt", F32, "d_attn"))
    gw["w_glu"] = matmul(yg2, dz_b, "tn", F32, "d_glu")
    dyg = seqs(matmul(dz_b, w["w_glu"], "nt", F32, "d_yg"))
    mixer = ["w_out", "w_attn_out", "w_glu"]
    swap = _reduce_start(mixer, gw, shard_shapes)
    du_b, dcb_re, dcb_im, dbb_re, dbb_im, dd, da_re8, da_im8, *rest = ssm_bwd(
        dyg, y, u, xr, xi, *ab, *bb, *cb, d_skip, comm=join_comms(ffn_exchange + [swap]))
    for n, p in zip(ffn[:2], rest[:2]):
        parts[n] = p
    mixer_exchange = _reduce_chip(mixer, swap, rest[2:], core)
    du_b = toks(du_b)
    g_ab_re = jnp.sum(da_re8, axis=0).reshape(32, 64)
    g_ab_im = jnp.sum(da_im8, axis=0).reshape(32, 64)
    d_lr, d_li, d_ldt, d_br_t, d_bi_t = ssm_prep_bwd(
        small["ssm_a_re"], small["ssm_a_im"], log_dt, br_t, bi_t,
        g_ab_re, g_ab_im, _block_diag_t(dbb_re), _block_diag_t(dbb_im))
    gs = {
        "ssm_a_re": d_lr, "ssm_a_im": d_li, "ssm_log_dt": d_ldt.reshape(1, 32),
        "ssm_b_re": d_br_t.transpose(1, 2, 0), "ssm_b_im": d_bi_t.transpose(1, 2, 0),
        "ssm_c_re": _block_diag_t(dcb_re).transpose(1, 0, 2), "ssm_c_im": _block_diag_t(dcb_im).transpose(1, 0, 2),
        "ssm_d": dd.reshape(32, 16),
    }
    ssm_gather = small_comm(_pack_small(SSM_SMALL, gs, SSM_ROWS))
    dqkv_b, *rest = attn_bwd(qkv, tables, dattn, attn, lse, comm=join_comms([mixer_exchange, ssm_gather]))
    for n, p in zip(mixer, rest):
        parts[n] = p
    ssm_shares = rest[len(mixer)]
    dqkv_b = toks(dqkv_b)
    d_qkv = matmul(dqkv_b, h, "tn", F32, "d_w_qkv")
    d_u = matmul(du_b, h, "tn", F32, "d_w_u")
    d_gate = matmul(dgl_b, h, "tn", F32, "d_w_gate")
    gw["w_in"] = jnp.concatenate([_qkv_order(d_qkv, back=True), d_u, d_gate], axis=0)
    swap = _reduce_start(["w_in"], gw, shard_shapes)
    dh, *got = matmul(dqkv_b, w_qkv, "nn", F32, "d_h_qkv", comm=swap)
    w_in_exchange = _reduce_chip(["w_in"], swap, got, core)
    grad_x, dg_mix, parts["w_in"] = mix_in_bwd([du_b, dgl_b], [w_u, w_gate], dh, x, g_mix, dx1, comm=w_in_exchange)
    gs_norm = {"norm_mix_g": dg_mix, "norm_ffn_g": dg_ffn, "norm_final_g": dg_final}
    return loss, grad_x, parts, ssm_shares, gs_norm


ANY = pl.BlockSpec(memory_space=pl.ANY)
BIG = ("w_in", "w_glu", "w_attn_out", "w_out", "w_ffn_gate", "w_ffn_up", "w_ffn_down")
TRANSPOSED = ("w_in", "w_ffn_gate", "w_ffn_up")
ROW_SHARDED = TRANSPOSED + ("w_out", "w_ffn_down")
SMALL = ("norm_mix_g", "ssm_a_re", "ssm_a_im", "ssm_log_dt", "ssm_b_re", "ssm_b_im", "ssm_c_re", "ssm_c_im",
         "ssm_d", "norm_ffn_g", "norm_final_g")
WEIGHTS = ("norm_mix_g", "w_in", "ssm_a_re", "ssm_a_im", "ssm_log_dt", "ssm_b_re", "ssm_b_im", "ssm_c_re",
           "ssm_c_im", "ssm_d", "w_glu", "w_attn_out", "w_out", "norm_ffn_g", "w_ffn_gate", "w_ffn_up",
           "w_ffn_down", "norm_final_g")
SSM_SMALL = SMALL[1:9]
NORM_SMALL = (SMALL[0],) + SMALL[9:]
SSM_ROWS, NORM_ROWS = 1064, 32
N_BIG = len(BIG)


def _position():
    return lax.axis_index("x"), lax.axis_index("y"), lax.axis_index("c")


def _other_chips(x, y):
    return [(1 - x, y), (x, 1 - y), (1 - x, 1 - y)]


def _remote(src, dst, send_sem, recv_sem, device):
    return pltpu.make_async_remote_copy(src_ref=src, dst_ref=dst, send_sem=send_sem, recv_sem=recv_sem,
                                        device_id=device, device_id_type=MESH)


_later = functools.partial


def _two_level_phases(copies):
    def first(*refs):
        locals_, sends, _, _, _ = copies(*refs)
        for cp in locals_ + sends:
            cp().start()

    def mid(*refs):
        _, _, arrived, passed, _ = copies(*refs)
        for got, cp in zip(arrived, passed):
            got().wait_recv()
            cp().start()

    def last(*refs):
        locals_, sends, _, passed, from_sibling = copies(*refs)
        for cp in from_sibling:
            cp().wait_recv()
        for cp in sends + passed:
            cp().wait_send()
        for cp in locals_:
            cp().wait()

    return first, mid, last


def _half(ref, chip, which):
    rows = ref.shape[1] // 2
    return ref.at[chip, pl.ds(which * rows, rows), :]


class Comm:
    def __init__(self, ins, out_shapes, sems, first, mid, last):
        self.ins, self.out_shapes, self.sems = list(ins), list(out_shapes), list(sems)
        self.first, self.mid, self.last = first, mid, last


def join_comms(comms):
    def cut(refs_by_kind):
        offs, parts = [0, 0, 0], []
        for cm in comms:
            sizes = (len(cm.ins), len(cm.out_shapes), len(cm.sems))
            parts.append(tuple(refs_by_kind[k][offs[k]:offs[k] + sizes[k]] for k in range(3)))
            offs = [o + s for o, s in zip(offs, sizes)]
        return parts

    def phase(which):
        def run(ins, outs, sems):
            for cm, part in zip(comms, cut((ins, outs, sems))):
                fn = getattr(cm, which)
                if fn is not None:
                    fn(*part)
        return run

    return Comm(sum((cm.ins for cm in comms), []), sum((cm.out_shapes for cm in comms), []),
                sum((cm.sems for cm in comms), []), phase("first"), phase("mid"), phase("last"))


def _comm_operands(comm):
    if comm is None:
        return [], [], []
    return comm.ins, comm.out_shapes, comm.sems


def _comm_begin(comm, refs, step, n_steps):
    if comm is None:
        return
    pl.when(step == 0)(lambda: comm.first(*refs))
    if comm.mid is not None:
        pl.when(step == (n_steps * 3) // 4)(lambda: comm.mid(*refs))


def _comm_end(comm, refs, step, n_steps):
    if comm is not None:
        pl.when(step == n_steps - 1)(lambda: comm.last(*refs))


def _comm_refs(comm, refs, n_in, n_out):
    if comm is None:
        return list(refs), None
    ci, co, cs = len(comm.ins), len(comm.out_shapes), len(comm.sems)
    o0 = n_in + ci
    s0 = o0 + n_out + co
    host = list(refs[:n_in]) + list(refs[o0:o0 + n_out]) + list(refs[s0:len(refs) - cs])
    return host, (list(refs[n_in:o0]), list(refs[o0 + n_out:s0]), list(refs[len(refs) - cs:]))


def run_comm(comm, name):
    n_in, n_out = len(comm.ins), len(comm.out_shapes)

    def body(*refs):
        parts = (list(refs[:n_in]), list(refs[n_in:n_in + n_out]), list(refs[n_in + n_out:]))
        comm.first(*parts)
        if comm.mid is not None:
            comm.mid(*parts)
        comm.last(*parts)

    return pl.pallas_call(body, name=name, in_specs=[ANY] * n_in, out_specs=[ANY] * n_out,
                          out_shape=comm.out_shapes, scratch_shapes=comm.sems)(*comm.ins)


def hosted_call(work, comm, name, grid, in_specs, out_specs, out_shape, scratch_shapes, args, semantics):
    c_ins, c_outs, c_sems = _comm_operands(comm)
    n_steps = math.prod(grid)

    def body(*refs):
        host, c_refs = _comm_refs(comm, refs, len(in_specs), len(out_specs))
        step = 0
        for axis, size in enumerate(grid):
            step = step * size + pl.program_id(axis)
        _comm_begin(comm, c_refs, step, n_steps)
        work(*host)
        _comm_end(comm, c_refs, step, n_steps)

    return pl.pallas_call(
        body, name=name, grid=grid, in_specs=list(in_specs) + [ANY] * len(c_ins),
        out_specs=list(out_specs) + [ANY] * len(c_outs), out_shape=list(out_shape) + c_outs,
        scratch_shapes=list(scratch_shapes) + c_sems,
        compiler_params=_params(semantics if comm is None else ("arbitrary",) * len(grid)),
    )(*args, *c_ins)


def gather_comm(shards):
    n = len(shards)

    def copies(srcs, outs, sems):
        send_sems, recv_sems, local_sems = sems
        x, y, c = _position()
        me = 2 * x + y
        sibling = (x, y, 1 - c)
        chips = _other_chips(x, y)
        locals_ = [_later(pltpu.make_async_copy, s, o.at[me], local_sems.at[i])
                   for i, (s, o) in enumerate(zip(srcs, outs))]
        sends, arrived, passed, from_sibling = [], [], [], []
        for j, (px, py) in enumerate(chips):
            for i, (s, o) in enumerate(zip(srcs, outs)):
                rows = s.shape[0] // 2
                sends.append(_later(_remote, s.at[pl.ds(c * rows, rows), :], _half(o, me, c), send_sems.at[i, j],
                                    recv_sems.at[i, j], (px, py, c)))
                got = _half(o, 2 * px + py, c)
                arrived.append(_later(_remote, got, got, send_sems.at[i, j], recv_sems.at[i, j], (px, py, c)))
                passed.append(_later(_remote, got, got, send_sems.at[i, 3 + j], recv_sems.at[i, 3 + j], sibling))
                other = _half(o, 2 * px + py, 1 - c)
                from_sibling.append(_later(_remote, other, other, send_sems.at[i, 3 + j], recv_sems.at[i, 3 + j],
                                           sibling))
        return locals_, sends, arrived, passed, from_sibling

    return Comm(shards, [jax.ShapeDtypeStruct((N_CHIPS,) + s.shape, s.dtype) for s in shards],
                [pltpu.SemaphoreType.DMA((n, 6)), pltpu.SemaphoreType.DMA((n, 6)), pltpu.SemaphoreType.DMA((n,))],
                *_two_level_phases(copies))


def swap_comm(grads):
    n = len(grads)

    def copies(srcs, gots, sems):
        send_sems, recv_sems = sems
        x, y, c = _position()
        out = []
        for i, (s, o) in enumerate(zip(srcs, gots)):
            rows = s.shape[1] // 2
            out.append(_remote(s.at[:, pl.ds((1 - c) * rows, rows), :], o, send_sems.at[i], recv_sems.at[i],
                               (x, y, 1 - c)))
        return out

    def first(srcs, gots, sems):
        for cp in copies(srcs, gots, sems):
            cp.start()

    def last(srcs, gots, sems):
        for cp in copies(srcs, gots, sems):
            cp.wait()

    return Comm(grads, [jax.ShapeDtypeStruct((N_CHIPS, g.shape[1] // 2, g.shape[2]), g.dtype) for g in grads],
                [pltpu.SemaphoreType.DMA((n,)), pltpu.SemaphoreType.DMA((n,))], first, None, last)


def add_halves(name, g, got, core):
    _, half, cols = got.shape
    mine = pl.BlockSpec((None, half, cols), lambda k, c_ref: (k, c_ref[0], 0))
    other = pl.BlockSpec((None, half, cols), lambda k, c_ref: (k, 0, 0))

    def body(c_ref, g_ref, got_ref, o_ref):
        o_ref[...] = (g_ref[...] + got_ref[...]).astype(BF16)

    return pl.pallas_call(
        body, name="add_halves_" + name,
        grid_spec=pltpu.PrefetchScalarGridSpec(num_scalar_prefetch=1, grid=(N_CHIPS,), in_specs=[mine, other],
                                               out_specs=other),
        out_shape=jax.ShapeDtypeStruct(got.shape, BF16),
        compiler_params=_params(("parallel",)),
    )(core, g, got)


def exchange_comm(parts):
    n = len(parts)

    def copies(srcs, outs, sems):
        send_sems, recv_sems, local_sems = sems
        x, y, c = _position()
        me = 2 * x + y
        sibling = (x, y, 1 - c)
        chips = _other_chips(x, y)
        locals_, sends, arrived, passed, from_sibling = [], [], [], [], []
        for i, (s, o) in enumerate(zip(srcs, outs)):
            locals_.append(_later(pltpu.make_async_copy, s.at[me], _half(o, me, c), local_sems.at[i]))
            sends.append(_later(_remote, s.at[me], _half(o, me, c), send_sems.at[i, 3], recv_sems.at[i, 3], sibling))
            other = _half(o, me, 1 - c)
            from_sibling.append(_later(_remote, other, other, send_sems.at[i, 3], recv_sems.at[i, 3], sibling))
        for j, (px, py) in enumerate(chips):
            for i, (s, o) in enumerate(zip(srcs, outs)):
                sends.append(_later(_remote, s.at[2 * px + py], _half(o, me, c), send_sems.at[i, j],
                                    recv_sems.at[i, j], (px, py, c)))
                got = _half(o, 2 * px + py, c)
                arrived.append(_later(_remote, got, got, send_sems.at[i, j], recv_sems.at[i, j], (px, py, c)))
                passed.append(_later(_remote, got, got, send_sems.at[i, 4 + j], recv_sems.at[i, 4 + j], sibling))
                other = _half(o, 2 * px + py, 1 - c)
                from_sibling.append(_later(_remote, other, other, send_sems.at[i, 4 + j], recv_sems.at[i, 4 + j],
                                           sibling))
        return locals_, sends, arrived, passed, from_sibling

    return Comm(parts, [jax.ShapeDtypeStruct((N_CHIPS, 2 * p.shape[1], p.shape[2]), p.dtype) for p in parts],
                [pltpu.SemaphoreType.DMA((n, 7)), pltpu.SemaphoreType.DMA((n, 7)), pltpu.SemaphoreType.DMA((n,))],
                *_two_level_phases(copies))


def small_comm(pack):
    def copies(srcs, outs, sems):
        (src_ref,), (out_ref,), (send_sems, recv_sems, local_sem) = srcs, outs, sems
        x, y, c = _position()
        me = 4 * x + 2 * y + c
        flips = [(fx, fy, fc) for fx in (0, 1) for fy in (0, 1) for fc in (0, 1)][1:]
        peers = [(1 - x if fx else x, 1 - y if fy else y, 1 - c if fc else c) for fx, fy, fc in flips]
        local = _later(pltpu.make_async_copy, src_ref, out_ref.at[me], local_sem)
        sends = [_later(_remote, src_ref, out_ref.at[me], send_sems.at[j], recv_sems.at[j], peer)
                 for j, peer in enumerate(peers)]
        arrived = []
        for j, (px, py, pc) in enumerate(peers):
            got = out_ref.at[4 * px + 2 * py + pc]
            arrived.append(_later(_remote, got, got, send_sems.at[j], recv_sems.at[j], (px, py, pc)))
        return local, sends, arrived

    def first(*refs):
        local, sends, _ = copies(*refs)
        for cp in [local] + sends:
            cp().start()

    def last(*refs):
        local, sends, arrived = copies(*refs)
        for cp in arrived:
            cp().wait_recv()
        for cp in sends:
            cp().wait_send()
        local().wait()

    return Comm([pack], [jax.ShapeDtypeStruct((N_DEV,) + pack.shape, pack.dtype)],
                [pltpu.SemaphoreType.DMA((7,)), pltpu.SemaphoreType.DMA((7,)), pltpu.SemaphoreType.DMA],
                first, None, last)


def _adam_fn(w, g, m, v):
    m = ADAM_B1 * m + (1.0 - ADAM_B1) * g
    v = ADAM_B2 * v + (1.0 - ADAM_B2) * jnp.square(g)
    m_hat = m / (1.0 - ADAM_B1 ** ADAM_STEP)
    v_hat = v / (1.0 - ADAM_B2 ** ADAM_STEP)
    return -ADAM_LR * (m_hat / (jnp.sqrt(v_hat) + ADAM_EPS) + ADAM_WD * w), m, v


def adam_big(name, parts, w, m, v):
    rows, cols = w.shape
    tm = _pick(rows, 384, 16)

    def fn(p0, p1, p2, p3, wv, mv, vv):
        g = ((p0.astype(F32) + p1.astype(F32)) + p2.astype(F32)) + p3.astype(F32)
        return (g,) + _adam_fn(wv, g, mv, vv)

    return rowwise(fn, [parts, w, m, v], [(cols, F32)] * 4, "adam_" + name, tm=tm, rows=rows)


def adam_small(name, gathered, w, m, v):
    def body(g_ref, w_ref, m_ref, v_ref, go_ref, d_ref, mo_ref, vo_ref):
        g = g_ref[0]
        for k in range(1, N_DEV):
            g = g + g_ref[k]
        go_ref[...] = g
        d_ref[...], mo_ref[...], vo_ref[...] = _adam_fn(w_ref[...], g, m_ref[...], v_ref[...])

    return pl.pallas_call(body, name=name, out_shape=[jax.ShapeDtypeStruct(w.shape, F32)] * 4,
                          compiler_params=_params())(gathered, w, m, v)


def _pack_small(names, vals, rows, last=None):
    flat = [vals[n].reshape(-1) for n in names]
    if last is not None:
        flat.append(last.reshape(-1))
    flat = jnp.concatenate(flat)
    return jnp.pad(flat, (0, rows * LANES - flat.shape[0])).reshape(rows, LANES)


def _unpack_small(names, pack, shapes):
    flat, out, off = pack.reshape(-1), {}, 0
    for n in names:
        size = math.prod(shapes[n])
        out[n] = flat[off:off + size].reshape(shapes[n])
        off += size
    return out, flat[off]


def _to_slots(name, g, shard_shape):
    rows, cols = shard_shape
    if name in ROW_SHARDED:
        return g.reshape(N_CHIPS, rows, cols)
    return g.reshape(rows, N_CHIPS, cols).transpose(1, 0, 2)


def _from_slots(name, s):
    _, rows, cols = s.shape
    if name in ROW_SHARDED:
        return s.reshape(N_CHIPS * rows, cols)
    return s.transpose(1, 0, 2).reshape(rows, N_CHIPS * cols)


def kernel(x, norm_mix_g, w_in, ssm_a_re, ssm_a_im, ssm_log_dt, ssm_b_re, ssm_b_im, ssm_c_re, ssm_c_im, ssm_d, w_glu, w_attn_out, w_out, norm_ffn_g, w_ffn_gate, w_ffn_up, w_ffn_down, norm_final_g, loss_target, m_norm_mix_g, m_w_in, m_ssm_a_re, m_ssm_a_im, m_ssm_log_dt, m_ssm_b_re, m_ssm_b_im, m_ssm_c_re, m_ssm_c_im, m_ssm_d, m_w_glu, m_w_attn_out, m_w_out, m_norm_ffn_g, m_w_ffn_gate, m_w_ffn_up, m_w_ffn_down, m_norm_final_g, v_norm_mix_g, v_w_in, v_ssm_a_re, v_ssm_a_im, v_ssm_log_dt, v_ssm_b_re, v_ssm_b_im, v_ssm_c_re, v_ssm_c_im, v_ssm_d, v_w_glu, v_w_attn_out, v_w_out, v_norm_ffn_g, v_w_ffn_gate, v_w_ffn_up, v_w_ffn_down, v_norm_final_g):
    given = dict(locals())
    def local(name, prefix=""):
        t = given[prefix + name][0]
        return t.T if name in TRANSPOSED else t

    shard = {n: local(n) for n in BIG}
    shapes = {n: given[n].shape for n in WEIGHTS}

    small = {n: given[n] for n in SMALL}
    small_2d = dict(small)
    for n in ("ssm_a_re", "ssm_a_im", "ssm_b_re", "ssm_b_im", "ssm_c_re", "ssm_c_im", "ssm_d"):
        small_2d[n] = small[n][0]
    small_2d["norm_final_g"] = norm_final_g.reshape(1, D_MODEL)

    core = lax.axis_index("c").astype(jnp.int32).reshape(1)
    loss, grad_x, parts, ssm_shares, gs_norm = local_step(
        x.reshape(TOKENS, D_MODEL), loss_target.reshape(TOKENS, D_MODEL),
        {n: shard[n].astype(BF16) for n in BIG}, small_2d, core)

    (norm_shares,) = run_comm(small_comm(_pack_small(NORM_SMALL, gs_norm, NORM_ROWS, last=loss)), "gather_norm_grads")
    small_out = [{} for _ in range(4)]
    for names, rows, shares in ((SSM_SMALL, SSM_ROWS, ssm_shares), (NORM_SMALL, NORM_ROWS, norm_shares)):
        packs = [_pack_small(names, {n: given[p + n] for n in names}, rows) for p in ("", "m_", "v_")]
        for kind, t in enumerate(adam_small("adam_" + names[0], shares, *packs)):
            vals, after = _unpack_small(names, t, shapes)
            small_out[kind].update(vals)
            if kind == 0:
                total_loss = after

    big_out = {}
    for n in BIG:
        res = adam_big(n, parts[n], shard[n], local(n, "m_"), local(n, "v_"))
        big_out[n] = [(t.T if n in TRANSPOSED else t)[None] for t in res]

    outs = [total_loss, grad_x.reshape(LOCAL_BATCH, SEQ, D_MODEL)]
    for kind in range(4):
        for n in WEIGHTS:
            outs.append(big_out[n][kind] if n in BIG else small_out[kind][n])
    return tuple(outs)
```

```python
import functools
import math

import jax
import jax.numpy as jnp
import numpy as np
from jax import lax
from jax.experimental import pallas as pl
from jax.experimental.pallas import tpu as pltpu

F32 = jnp.float32
BF16 = jnp.bfloat16
MESH = pl.DeviceIdType.MESH

D_MODEL = 1024
SEQ = 2048
LOCAL_BATCH = 2
TOKENS = LOCAL_BATCH * SEQ
HEAD_DIM = 64
HEADS_PER_GROUP = 4
GROUP_W = HEADS_PER_GROUP * HEAD_DIM
N_GROUPS = 3
DILATIONS = (1, 4, 16)
ATTN_BLOCK = 128
ROPE_DIM = 16
ROPE_THETA = 500000.0
QKV_W = 3 * N_GROUPS * GROUP_W
SSM_W = 512
SSM_STATE_W = 2048
SSM_LANE_BLOCKS = 4
GATE_W = 2 * D_MODEL
D_FF = 2816
RMS_EPS = 1e-6
NEG_INF = -1e30
ADAM_LR, ADAM_B1, ADAM_B2, ADAM_EPS, ADAM_WD, ADAM_STEP = 0.001, 0.9, 0.999, 1e-08, 0.01, 10
N_CHIPS = 4
N_DEV = 8

VMEM_LIMIT = 56 * 1024 * 1024
LANES = 128


def _params(sem=None):
    return pltpu.CompilerParams(dimension_semantics=sem, vmem_limit_bytes=VMEM_LIMIT)


def _pick(n, cap, align=LANES):
    best = None
    for d in range(align, min(n, cap) + 1, align):
        if n % d == 0:
            best = d
    return n if best is None or n <= cap else best


_DIMS = {"nn": (((1,), (0,)), ((), ())), "nt": (((1,), (1,)), ((), ())), "tn": (((0,), (0,)), ((), ()))}


def _dot(a, b, mode):
    return lax.dot_general(a, b, _DIMS[mode], preferred_element_type=F32)


def matmul(a, b, mode, out_dtype, name, add=None, comm=None):
    if mode == "nn":
        (m, k), n = a.shape, b.shape[1]
    elif mode == "nt":
        (m, k), n = a.shape, b.shape[0]
    else:
        (k, m), n = a.shape, b.shape[1]
    tn = _pick(n, 1408 if mode != "tn" else 512)
    tk = _pick(k, 2816) if mode != "tn" else k
    tm = _pick(m, 1408)
    out_bytes = jnp.dtype(out_dtype).itemsize

    def need(tm_):
        return 2 * 2 * (tm_ * tk + tk * tn) + tm_ * tn * (4 + 2 * out_bytes + (8 if add is not None else 0))

    while need(tm) > 40 * 1024 * 1024 and tm % 256 == 0:
        tm //= 2
    nk = k // tk
    a_spec = {"nn": pl.BlockSpec((tm, tk), lambda i, j, kk: (i, kk)),
              "nt": pl.BlockSpec((tm, tk), lambda i, j, kk: (i, kk)),
              "tn": pl.BlockSpec((tk, tm), lambda i, j, kk: (kk, i))}[mode]
    b_spec = {"nn": pl.BlockSpec((tk, tn), lambda i, j, kk: (kk, j)),
              "nt": pl.BlockSpec((tn, tk), lambda i, j, kk: (j, kk)),
              "tn": pl.BlockSpec((tk, tn), lambda i, j, kk: (kk, j))}[mode]
    o_spec = pl.BlockSpec((tm, tn), lambda i, j, kk: (i, j))

    def body(a_ref, b_ref, *rest):
        if add is not None:
            add_ref, o_ref, acc_ref = rest
        else:
            o_ref, acc_ref = rest
        part = _dot(a_ref[...], b_ref[...], mode)
        if nk == 1:
            res = part if add is None else part + add_ref[...]
            o_ref[...] = res.astype(out_dtype)
            return
        kk = pl.program_id(2)

        @pl.when(kk == 0)
        def _():
            acc_ref[...] = part

        @pl.when(kk > 0)
        def _():
            acc_ref[...] += part

        @pl.when(kk == nk - 1)
        def _():
            res = acc_ref[...] if add is None else acc_ref[...] + add_ref[...]
            o_ref[...] = res.astype(out_dtype)

    in_specs = [a_spec, b_spec] + ([o_spec] if add is not None else [])
    args = (a, b) + ((add,) if add is not None else ())
    res = hosted_call(
        body, comm, name, (m // tm, n // tn, nk), in_specs, [o_spec], [jax.ShapeDtypeStruct((m, n), out_dtype)],
        [pltpu.VMEM((tm, tn) if nk > 1 else (8, LANES), F32)], args, ("parallel", "parallel", "arbitrary"))
    return res[0] if comm is None else res


def matmul_rows(a, b, name, fn, extra, outs, accs=(), add=None, comm=None, tm=512):
    (m, k), n = a.shape, b.shape[1]
    n_fixed = 2 + (add is not None)
    row_spec = lambda cols: pl.BlockSpec((tm, cols), lambda i: (i, 0))
    in_specs = [row_spec(k), pl.BlockSpec((k, n), lambda i: (0, 0))] + ([row_spec(n)] if add is not None else [])
    in_specs += [pl.BlockSpec(e.shape, lambda i: (0, 0)) if e.shape[0] == 1 else row_spec(e.shape[1]) for e in extra]
    out_specs = [row_spec(c) for c, _ in outs] + [pl.BlockSpec((1, c), lambda i: (0, 0)) for c in accs]
    out_shape = [jax.ShapeDtypeStruct((m, c), dt) for c, dt in outs] + [jax.ShapeDtypeStruct((1, c), F32) for c in accs]

    def body(*refs):
        rows = _dot(refs[0][...], refs[1][...], "nn")
        if add is not None:
            rows = rows + refs[2][...]
        n_in = n_fixed + len(extra)
        res = fn(rows, *[r[...] for r in refs[n_fixed:n_in]])
        for r, v in zip(refs[n_in:n_in + len(outs)], res[:len(outs)]):
            r[...] = v.astype(r.dtype)
        first = pl.program_id(0) == 0
        for r, v in zip(refs[n_in + len(outs):], res[len(outs):]):
            @pl.when(first)
            def _(r=r, v=v):
                r[...] = v

            @pl.when(jnp.logical_not(first))
            def _(r=r, v=v):
                r[...] += v

    args = (a, b) + ((add,) if add is not None else ()) + tuple(extra)
    return hosted_call(body, comm, name, (m // tm,), in_specs, out_specs, out_shape, [], args, ("arbitrary",))


FFN_TM, FFN_TN = 512, 1408


def ffn_in(h2, wg_t, wu_t, comm=None):
    def body(h_ref, wg_ref, wu_ref, a_ref, b_ref, act_ref):
        hv = h_ref[...]
        a, b = _dot(hv, wg_ref[...], "nt"), _dot(hv, wu_ref[...], "nt")
        a_ref[...] = a.astype(BF16)
        b_ref[...] = b.astype(BF16)
        act_ref[...] = _swiglu_fn(a, b).astype(BF16)

    rows = pl.BlockSpec((FFN_TM, D_MODEL), lambda i, j: (i, 0))
    wts = pl.BlockSpec((FFN_TN, D_MODEL), lambda i, j: (j, 0))
    out = pl.BlockSpec((FFN_TM, FFN_TN), lambda i, j: (i, j))
    return hosted_call(body, comm, "ffn_in", (TOKENS // FFN_TM, D_FF // FFN_TN), [rows, wts, wts], [out] * 3,
                       [jax.ShapeDtypeStruct((TOKENS, D_FF), BF16)] * 3, [], (h2, wg_t, wu_t),
                       ("parallel", "parallel"))


def ffn_in_bwd(dx2_b, wd, a, b):
    def body(dx_ref, wd_ref, a_ref, b_ref, da_ref, db_ref):
        dx = dx_ref[...]
        for lo in range(0, FFN_TN, 512):
            cols = slice(lo, min(lo + 512, FFN_TN))
            dact = _dot(dx, wd_ref[cols, :], "nt")
            _, vjp = jax.vjp(_swiglu_fn, a_ref[:, cols].astype(F32), b_ref[:, cols].astype(F32))
            da, db = vjp(dact)
            da_ref[:, cols] = da.astype(BF16)
            db_ref[:, cols] = db.astype(BF16)

    rows = pl.BlockSpec((FFN_TM, D_MODEL), lambda i, j: (i, 0))
    wts = pl.BlockSpec((FFN_TN, D_MODEL), lambda i, j: (j, 0))
    out = pl.BlockSpec((FFN_TM, FFN_TN), lambda i, j: (i, j))
    return pl.pallas_call(
        body, name="ffn_in_bwd", grid=(TOKENS // FFN_TM, D_FF // FFN_TN), in_specs=[rows, wts, out, out],
        out_specs=[out] * 2, out_shape=[jax.ShapeDtypeStruct((TOKENS, D_FF), BF16)] * 2,
        compiler_params=_params(("parallel", "parallel")),
    )(dx2_b, wd, a, b)


def mix_in_bwd(grads, weights, partial, x, g, skip, comm=None):
    n = len(grads)
    tm = 512

    def body(*refs):
        a_refs, b_refs = refs[:n], refs[n:2 * n]
        part_ref, x_ref, g_ref, skip_ref, gx_ref, dg_ref = refs[2 * n:]
        dh = part_ref[...]
        for a_ref, b_ref in zip(a_refs, b_refs):
            dh = dh + _dot(a_ref[...], b_ref[...], "nn")
        _, vjp = jax.vjp(_rms, x_ref[...], g_ref[...])
        dx, dg = vjp(dh)
        gx_ref[...] = dx + skip_ref[...]
        first = pl.program_id(0) == 0

        @pl.when(first)
        def _():
            dg_ref[...] = dg

        @pl.when(jnp.logical_not(first))
        def _():
            dg_ref[...] += dg

    rows = pl.BlockSpec((tm, D_MODEL), lambda i: (i, 0))
    gain = pl.BlockSpec((1, D_MODEL), lambda i: (0, 0))
    in_specs = [pl.BlockSpec((tm, a.shape[1]), lambda i: (i, 0)) for a in grads]
    in_specs += [pl.BlockSpec(b.shape, lambda i: (0, 0)) for b in weights]
    return hosted_call(
        body, comm, "mix_in_bwd", (TOKENS // tm,), in_specs + [rows, rows, gain, rows], [rows, gain],
        [jax.ShapeDtypeStruct((TOKENS, D_MODEL), F32), jax.ShapeDtypeStruct((1, D_MODEL), F32)], [],
        (*grads, *weights, partial, x, g, skip), ("arbitrary",))


def rowwise(fn, ins, outs, name, accs=(), tm=256, rows=TOKENS, comm=None):
    in_specs, args = [], []
    for item in ins:
        arr, width, blk = item if isinstance(item, tuple) else (item, None, 0)
        if arr.ndim == 3:
            for k in range(arr.shape[0]):
                in_specs.append(pl.BlockSpec((None, tm, arr.shape[2]), functools.partial(lambda i, k_: (k_, i, 0), k_=k)))
                args.append(arr)
            continue
        if arr.shape[0] == 1:
            in_specs.append(pl.BlockSpec(arr.shape, lambda i: (0, 0)))
        elif width is None:
            in_specs.append(pl.BlockSpec((tm, arr.shape[1]), lambda i: (i, 0)))
        else:
            in_specs.append(pl.BlockSpec((tm, width), functools.partial(lambda i, blk_: (i, blk_), blk_=blk)))
        args.append(arr)
    out_specs = [pl.BlockSpec((tm, c), lambda i: (i, 0)) for c, _ in outs]
    out_specs += [pl.BlockSpec((1, c), lambda i: (0, 0)) for c in accs]
    out_shape = [jax.ShapeDtypeStruct((rows, c), dt) for c, dt in outs]
    out_shape += [jax.ShapeDtypeStruct((1, c), F32) for c in accs]
    n_in, n_out = len(args), len(outs)
    c_ins, c_outs, c_sems = _comm_operands(comm)

    def body(*refs):
        refs, c_refs = _comm_refs(comm, refs, n_in, n_out + len(accs))
        step = pl.program_id(0)
        _comm_begin(comm, c_refs, step, rows // tm)
        res = fn(*[r[...] for r in refs[:n_in]])
        for r, v in zip(refs[n_in:n_in + n_out], res[:n_out]):
            r[...] = v.astype(r.dtype)
        first = step == 0
        for r, v in zip(refs[n_in + n_out:], res[n_out:]):
            @pl.when(first)
            def _(r=r, v=v):
                r[...] = v

            @pl.when(jnp.logical_not(first))
            def _(r=r, v=v):
                r[...] += v
        _comm_end(comm, c_refs, step, rows // tm)

    return pl.pallas_call(
        body, name=name, grid=(rows // tm,), in_specs=in_specs + [ANY] * len(c_ins),
        out_specs=out_specs + [ANY] * len(c_outs), out_shape=out_shape + c_outs, scratch_shapes=c_sems,
        compiler_params=_params(("arbitrary",)),
    )(*args, *c_ins)


def _rms(x, g):
    return x * lax.rsqrt(jnp.mean(x * x, axis=-1, keepdims=True) + RMS_EPS) * g


def _colsum(v):
    return jnp.sum(v, axis=0, keepdims=True)


PAIR_W = 2 * HEAD_DIM
N_PAIRS = HEADS_PER_GROUP // 2


def _qkv_order(w_t, back=False):
    dims = (N_PAIRS, N_GROUPS, 3) if back else (3, N_GROUPS, N_PAIRS)
    return w_t.reshape(dims + (PAIR_W, w_t.shape[1])).transpose(2, 1, 0, 3, 4).reshape(QKV_W, w_t.shape[1])


def _rope_tables():
    half = ROPE_DIM // 2
    inv = np.power(np.float32(ROPE_THETA), -np.arange(half, dtype=np.float32) * np.float32(2.0 / ROPE_DIM))
    ang = (np.arange(SEQ, dtype=np.float32)[:, None] * inv[None, :]).astype(np.float32)
    cos, sin = np.cos(ang), np.sin(ang)
    zeros = np.zeros((SEQ, HEAD_DIM - ROPE_DIM), np.float32)
    zh = np.zeros((SEQ, half), np.float32)
    c = np.concatenate([cos, cos, zeros + 1.0], axis=1)
    sa = np.concatenate([-sin, zh, zeros], axis=1)
    sb = np.concatenate([zh, sin, zeros], axis=1)
    return [jnp.asarray(np.tile(t, (1, 2)), F32) for t in (c, sa, sb)]


def _rope_fwd(x, c, sa, sb):
    return x * c + pltpu.roll(x, PAIR_W - 8, 1) * sa + pltpu.roll(x, 8, 1) * sb


def _rope_bwd(dy, c, sa, sb):
    return dy * c + pltpu.roll(dy * sb, PAIR_W - 8, 1) + pltpu.roll(dy * sa, 8, 1)


def _band_masks():
    row = lax.broadcasted_iota(jnp.int32, (ATTN_BLOCK, ATTN_BLOCK), 0)
    col = lax.broadcasted_iota(jnp.int32, (ATTN_BLOCK, ATTN_BLOCK), 1)
    return col <= row, col >= row


def _stack_rows(t):
    return jnp.concatenate([t, t], axis=0)


def _stack_heads(t, first_head):
    return jnp.concatenate([jnp.where(first_head, t, 0), jnp.where(first_head, 0, t)], axis=0)


def _per_head(fn):
    return jnp.concatenate([fn(slice(h * HEAD_DIM, (h + 1) * HEAD_DIM)) for h in range(2)], axis=1)


def _slab_spec(kind):
    return pl.BlockSpec((None, SEQ, PAIR_W), lambda b, p, g: (b, 0, p * 3 * N_GROUPS + g * 3 + kind))


_TABLE_SPEC = pl.BlockSpec((SEQ, PAIR_W), lambda b, p, g: (0, 0))
_PAIR_SPEC = pl.BlockSpec((None, SEQ, PAIR_W), lambda b, p, g: (b, 0, p))


def _block_rows(dil, r, n):
    return pl.ds(n * (ATTN_BLOCK * dil) + r, ATTN_BLOCK, stride=dil)


def proj_qkv(h, w_qkv_t, tables, comm=None):
    tm = 1024
    pair_w = QKV_W // N_PAIRS
    scale = HEAD_DIM ** -0.5

    def body(h_ref, w_ref, c_ref, sa_ref, sb_ref, o_ref):
        rows = _dot(h_ref[...], w_ref[...], "nt")
        c, sa, sb = c_ref[...], sa_ref[...], sb_ref[...]
        for blk in range(pair_w // PAIR_W):
            cols = slice(blk * PAIR_W, (blk + 1) * PAIR_W)
            x = rows[:, cols]
            if blk % 3 == 0:
                x = _rope_fwd(x, c, sa, sb) * scale
            elif blk % 3 == 1:
                x = _rope_fwd(x, c, sa, sb)
            o_ref[:, cols] = x

    table = pl.BlockSpec((tm, PAIR_W), lambda i, j, : (i % (SEQ // tm), 0))
    res = hosted_call(
        body, comm, "proj_qkv", (TOKENS // tm, N_PAIRS),
        [pl.BlockSpec((tm, D_MODEL), lambda i, j: (i, 0)), pl.BlockSpec((pair_w, D_MODEL), lambda i, j: (j, 0)),
         table, table, table],
        [pl.BlockSpec((tm, pair_w), lambda i, j: (i, j))], [jax.ShapeDtypeStruct((TOKENS, QKV_W), F32)], [],
        (h, w_qkv_t, *tables), ("parallel", "parallel"))
    return res[0] if comm is None else res


def attn_fwd(qkv, comm=None):
    def body(qs, ks, v_ref, attn_b_ref, attn_ref, lse_ref, o0, o1, o2, l0, l1, l2):
        g = pl.program_id(2)
        cur_mask, prev_mask = _band_masks()
        first_head = lax.broadcasted_iota(jnp.int32, (ATTN_BLOCK, PAIR_W), 1) < HEAD_DIM

        def run(dil, o_slab, l_slab):
            nb = SEQ // dil // ATTN_BLOCK

            def block(idx, carry):
                r, n = lax.div(idx, nb), lax.rem(idx, nb)
                cur, prev = _block_rows(dil, r, n), _block_rows(dil, r, jnp.maximum(n - 1, 0))
                q = qs[cur, :].astype(BF16)
                kc, kp = ks[cur, :].astype(BF16), ks[prev, :].astype(BF16)
                vc, vp = v_ref[cur, :].astype(BF16), v_ref[prev, :].astype(BF16)
                q2 = _stack_heads(q, first_head)
                mask = _stack_rows(jnp.concatenate([jnp.logical_and(prev_mask, n > 0), cur_mask], axis=1))
                s2 = jnp.where(mask, _dot(q2, jnp.concatenate([kp, kc], axis=0), "nt"), NEG_INF)
                m = jnp.max(s2, axis=-1, keepdims=True)
                vcat, two = jnp.concatenate([vp, vc], axis=0), _stack_rows(first_head)
                vext = jnp.concatenate([jnp.where(two, vcat, 1), jnp.where(two, 1, vcat)], axis=1)
                r2 = _dot(jnp.exp(s2 - m).astype(BF16), vext, "nn")
                r0, r1 = r2[:ATTN_BLOCK, :PAIR_W], r2[ATTN_BLOCK:, PAIR_W:]
                num = jnp.where(first_head, r0, r1)
                den = pltpu.roll(jnp.where(first_head, r1, r0), HEAD_DIM, 1)
                o_slab[cur, :] = num / den
                l_slab[cur, :] = jnp.where(first_head, m[:ATTN_BLOCK], m[ATTN_BLOCK:]) + jnp.log(den)
                return carry

            lax.fori_loop(0, SEQ // ATTN_BLOCK, block, 0, unroll=4)

        for gi, (o_slab, l_slab) in enumerate(((o0, l0), (o1, l1), (o2, l2))):
            @pl.when(g == gi)
            def _(gi=gi, o_slab=o_slab, l_slab=l_slab):
                run(DILATIONS[gi], o_slab, l_slab)

        @pl.when(g == N_GROUPS - 1)
        def _():
            a, b, cc = l0[...], l1[...], l2[...]
            m = jnp.maximum(jnp.maximum(a, b), cc)
            e0, e1, e2 = jnp.exp(a - m), jnp.exp(b - m), jnp.exp(cc - m)
            tot = e0 + e1 + e2
            attn = (e0 * o0[...] + e1 * o1[...] + e2 * o2[...]) / tot
            attn_ref[...] = attn
            attn_b_ref[...] = attn.astype(BF16)
            lse_ref[...] = m + jnp.log(tot)

    shape = (LOCAL_BATCH, SEQ, GROUP_W)
    slab = pltpu.VMEM((SEQ, PAIR_W), F32)
    return hosted_call(
        body, comm, "attn_fwd", (LOCAL_BATCH, N_PAIRS, N_GROUPS),
        [_slab_spec(0), _slab_spec(1), _slab_spec(2)], [_PAIR_SPEC] * 3,
        [jax.ShapeDtypeStruct(shape, BF16), jax.ShapeDtypeStruct(shape, F32), jax.ShapeDtypeStruct(shape, F32)],
        [slab] * 6, (qkv, qkv, qkv), ("parallel", "parallel", "arbitrary"))


def attn_bwd(qkv, tables, dattn, attn, lse, comm=None):
    scale = HEAD_DIM ** -0.5

    def body(qs, ks, v_ref, c_ref, sa_ref, sb_ref, do_ref, out_ref, lse_ref, dqkv_ref, dl, dq_s, dk_s, dv_s):
        g = pl.program_id(2)
        c, sa, sb = c_ref[...], sa_ref[...], sb_ref[...]

        @pl.when(g == 0)
        def _():
            prod = do_ref[...] * out_ref[...]
            dl[...] = _per_head(
                lambda sl: jnp.broadcast_to(jnp.sum(prod[:, sl], axis=-1, keepdims=True), (SEQ, HEAD_DIM)))

        cur_mask, prev_mask = _band_masks()
        first_head = lax.broadcasted_iota(jnp.int32, (ATTN_BLOCK, PAIR_W), 1) < HEAD_DIM

        def run(dil):
            nb = SEQ // dil // ATTN_BLOCK

            def block(idx, carry):
                r, n = lax.div(idx, nb), lax.rem(idx, nb)
                cur = _block_rows(dil, r, n)
                prev = _block_rows(dil, r, jnp.maximum(n - 1, 0))
                nxt = _block_rows(dil, r, jnp.minimum(n + 1, nb - 1))
                q0, q1 = qs[cur, :].astype(BF16), qs[nxt, :].astype(BF16)
                kp, kc = ks[prev, :].astype(BF16), ks[cur, :].astype(BF16)
                vp, vc = v_ref[prev, :].astype(BF16), v_ref[cur, :].astype(BF16)
                do0, do1 = do_ref[cur, :].astype(BF16), do_ref[nxt, :].astype(BF16)
                lse0, lse1, dl0, dl1 = lse_ref[cur, :], lse_ref[nxt, :], dl[cur, :], dl[nxt, :]
                has_prev = jnp.logical_and(prev_mask, n > 0)
                has_next = jnp.logical_and(prev_mask, n < nb - 1)

                def per_row(t):
                    return jnp.concatenate([t[:, 0:1], t[:, HEAD_DIM:HEAD_DIM + 1]], axis=0)

                q20, q21 = _stack_heads(q0, first_head), _stack_heads(q1, first_head)
                do20, do21 = _stack_heads(do0, first_head), _stack_heads(do1, first_head)
                kcat, vcat = jnp.concatenate([kp, kc], axis=0), jnp.concatenate([vp, vc], axis=0)
                mask0 = _stack_rows(jnp.concatenate([has_prev, cur_mask], axis=1))
                p0 = jnp.where(mask0, jnp.exp(_dot(q20, kcat, "nt") - per_row(lse0)), 0.0)
                ds0 = (p0 * (_dot(do20, vcat, "nt") - per_row(dl0))).astype(BF16)
                p1 = jnp.where(_stack_rows(has_next), jnp.exp(_dot(q21, kc, "nt") - per_row(lse1)), 0.0)
                ds1 = (p1 * (_dot(do21, vc, "nt") - per_row(dl1))).astype(BF16)
                dq2 = _dot(ds0, kcat, "nn")
                dq_s[cur, :] = jnp.where(first_head, dq2[:ATTN_BLOCK], dq2[ATTN_BLOCK:])
                ds_cur = jnp.concatenate([ds0[:, ATTN_BLOCK:], ds1], axis=0)
                p_cur = jnp.concatenate([p0[:, ATTN_BLOCK:], p1], axis=0).astype(BF16)
                dk_s[cur, :] = _dot(ds_cur, jnp.concatenate([q20, q21], axis=0), "tn")
                dv_s[cur, :] = _dot(p_cur, jnp.concatenate([do20, do21], axis=0), "tn")
                return carry

            lax.fori_loop(0, SEQ // ATTN_BLOCK, block, 0, unroll=2)

        for gi in range(N_GROUPS):
            @pl.when(g == gi)
            def _(gi=gi):
                run(DILATIONS[gi])

        dqkv_ref[:, 0:PAIR_W] = _rope_bwd(dq_s[...] * scale, c, sa, sb).astype(BF16)
        dqkv_ref[:, PAIR_W:2 * PAIR_W] = _rope_bwd(dk_s[...], c, sa, sb).astype(BF16)
        dqkv_ref[:, 2 * PAIR_W:] = dv_s[...].astype(BF16)

    slab = pltpu.VMEM((SEQ, PAIR_W), F32)
    return hosted_call(
        body, comm, "attn_bwd", (LOCAL_BATCH, N_PAIRS, N_GROUPS),
        [_slab_spec(0), _slab_spec(1), _slab_spec(2), _TABLE_SPEC, _TABLE_SPEC, _TABLE_SPEC,
         _PAIR_SPEC, _PAIR_SPEC, _PAIR_SPEC],
        [pl.BlockSpec((None, SEQ, 3 * PAIR_W), lambda b, p, g: (b, 0, p * N_GROUPS + g))],
        [jax.ShapeDtypeStruct((LOCAL_BATCH, SEQ, QKV_W), BF16)],
        [slab] * 4, (qkv, qkv, qkv, *tables, dattn, attn, lse), ("parallel", "parallel", "arbitrary"))


def _discretize(lr, li, log_dt, br, bi):
    dt = jnp.exp(log_dt)
    mag = jnp.exp(lr * dt)
    ab_re, ab_im = mag * jnp.cos(li * dt), mag * jnp.sin(li * dt)
    den = lr * lr + li * li
    nr, ni = ab_re - 1.0, ab_im
    f_re = (nr * lr + ni * li) / den
    f_im = (ni * lr - nr * li) / den
    return ab_re, ab_im, f_re[None] * br - f_im[None] * bi, f_re[None] * bi + f_im[None] * br


def ssm_prep(lr, li, log_dt, br, bi):
    def body(lr_ref, li_ref, dt_ref, br_ref, bi_ref, *outs):
        for o, v in zip(outs, _discretize(lr_ref[...], li_ref[...], dt_ref[...], br_ref[...], bi_ref[...])):
            o[...] = v
    shapes = [lr, li, br, bi]
    return pl.pallas_call(body, name="ssm_prep",
                          out_shape=[jax.ShapeDtypeStruct(s.shape, F32) for s in shapes])(lr, li, log_dt, br, bi)


def ssm_prep_bwd(lr, li, log_dt, br, bi, g_ab_re, g_ab_im, g_bb_re, g_bb_im):
    def body(lr_ref, li_ref, dt_ref, br_ref, bi_ref, g0, g1, g2, g3, *outs):
        _, vjp = jax.vjp(_discretize, lr_ref[...], li_ref[...], dt_ref[...], br_ref[...], bi_ref[...])
        for o, v in zip(outs, vjp((g0[...], g1[...], g2[...], g3[...]))):
            o[...] = v
    shapes = [lr, li, log_dt, br, bi]
    return pl.pallas_call(body, name="ssm_prep_bwd",
                          out_shape=[jax.ShapeDtypeStruct(s.shape, F32) for s in shapes])(
        lr, li, log_dt, br, bi, g_ab_re, g_ab_im, g_bb_re, g_bb_im)


def _block_diag(t):
    per = SSM_STATE_W // SSM_LANE_BLOCKS // 64
    g = t.transpose(1, 0, 2).reshape(SSM_LANE_BLOCKS, per, 16, 64)
    eye = jnp.eye(per, dtype=t.dtype)
    return jnp.einsum("jgcn,gh->jgchn", g, eye).reshape(SSM_LANE_BLOCKS, per * 16, per * 64)


def _block_diag_t(m):
    per = SSM_STATE_W // SSM_LANE_BLOCKS // 64
    m5 = m.reshape(SSM_LANE_BLOCKS, per, 16, per, 64)
    d = jnp.einsum("jgchn,gh->jgcn", m5, jnp.eye(per, dtype=m.dtype))
    return d.reshape(SSM_LANE_BLOCKS * per, 16, 64).transpose(1, 0, 2)


def _cmul(ar, ai, br, bi):
    return ar * br - ai * bi, ar * bi + ai * br


def _power_tables(ar, ai, reverse):
    width = ar.shape[1]
    row = lax.broadcasted_iota(jnp.int32, (8, width), 0)
    pows = [(ar, ai)]
    for _ in range(7):
        pows.append(_cmul(pows[-1][0], pows[-1][1], ar, ai))
    steps = []
    for k in (1, 2, 4):
        keep = (row >= k) if not reverse else (row < 8 - k)
        steps.append((jnp.where(keep, pows[k - 1][0], 0.0), jnp.where(keep, pows[k - 1][1], 0.0)))
    cr = jnp.zeros((8, width), F32)
    ci = jnp.zeros((8, width), F32)
    for i in range(8):
        pr, pi = pows[i] if not reverse else pows[7 - i]
        cr = jnp.where(row == i, pr, cr)
        ci = jnp.where(row == i, pi, ci)
    return steps, (cr, ci)


SCAN_CHUNK = 2048
STATE_BLOCK = SSM_STATE_W // SSM_LANE_BLOCKS
CHAN_BLOCK = SSM_W // SSM_LANE_BLOCKS


def ssm_fwd(u, ab_re, ab_im, bb_re, bb_im, cb_re, cb_im, d_skip, comm=None):
    nt = SEQ // SCAN_CHUNK
    chan = pl.BlockSpec((None, SCAN_CHUNK, CHAN_BLOCK), lambda b, j, t: (b, t, j))
    state = pl.BlockSpec((None, SCAN_CHUNK, STATE_BLOCK), lambda b, j, t: (b, t, j))
    mat = pl.BlockSpec((None, CHAN_BLOCK, STATE_BLOCK), lambda b, j, t: (j, 0, 0))
    lane = pl.BlockSpec((1, STATE_BLOCK), lambda b, j, t: (0, j))
    dsp = pl.BlockSpec((1, CHAN_BLOCK), lambda b, j, t: (0, j))

    def body(u_ref, ar_ref, ai_ref, bbr_ref, bbi_ref, cbr_ref, cbi_ref, d_ref, y_ref, yg_ref, xr_ref, xi_ref,
             car_r, car_i):
        @pl.when(pl.program_id(2) == 0)
        def _():
            car_r[...] = jnp.zeros_like(car_r)
            car_i[...] = jnp.zeros_like(car_i)

        steps, (pr, pi) = _power_tables(ar_ref[...], ai_ref[...], reverse=False)
        uf = u_ref[...]
        ub = uf.astype(BF16)
        xr_ref[...] = _dot(ub, bbr_ref[...], "nn")
        xi_ref[...] = _dot(ub, bbi_ref[...], "nn")

        def tile(i, carry):
            cr, ci = carry
            sl = pl.ds(pl.multiple_of(i * 8, 8), 8)
            br, bi = xr_ref[sl, :], xi_ref[sl, :]
            for k, (sr, si) in zip((1, 2, 4), steps):
                tr, ti = _cmul(sr, si, pltpu.roll(br, k, 0), pltpu.roll(bi, k, 0))
                br, bi = br + tr, bi + ti
            tr, ti = _cmul(pr, pi, cr, ci)
            br, bi = br + tr, bi + ti
            xr_ref[sl, :] = br
            xi_ref[sl, :] = bi
            return br[7:8, :], bi[7:8, :]

        cr, ci = lax.fori_loop(0, SCAN_CHUNK // 8, tile, (car_r[0:1, :], car_i[0:1, :]), unroll=4)
        car_r[0:1, :] = cr
        car_i[0:1, :] = ci
        y = (_dot(xr_ref[...].astype(BF16), cbr_ref[...], "nt") - _dot(xi_ref[...].astype(BF16), cbi_ref[...], "nt")
             + d_ref[...] * uf)
        y_ref[...] = y
        yg_ref[...] = jax.nn.gelu(y).astype(BF16)

    return hosted_call(
        body, comm, "ssm_fwd", (LOCAL_BATCH, SSM_LANE_BLOCKS, nt),
        [chan, lane, lane, mat, mat, mat, mat, dsp], [chan, chan, state, state],
        [jax.ShapeDtypeStruct((LOCAL_BATCH, SEQ, SSM_W), F32), jax.ShapeDtypeStruct((LOCAL_BATCH, SEQ, SSM_W), BF16),
         jax.ShapeDtypeStruct((LOCAL_BATCH, SEQ, SSM_STATE_W), F32),
         jax.ShapeDtypeStruct((LOCAL_BATCH, SEQ, SSM_STATE_W), F32)],
        [pltpu.VMEM((8, STATE_BLOCK), F32), pltpu.VMEM((8, STATE_BLOCK), F32)],
        (u, ab_re, ab_im, bb_re, bb_im, cb_re, cb_im, d_skip), ("parallel", "parallel", "arbitrary"))


def ssm_bwd(dyg, y, u, xr, xi, ab_re, ab_im, bb_re, bb_im, cb_re, cb_im, d_skip, comm=None):
    nt = SEQ // SCAN_CHUNK
    ntile = SCAN_CHUNK // 8

    def rev(t):
        return nt - 1 - t

    chan = pl.BlockSpec((None, SCAN_CHUNK, CHAN_BLOCK), lambda j, b, t: (b, rev(t), j))
    state = pl.BlockSpec((None, SCAN_CHUNK, STATE_BLOCK), lambda j, b, t: (b, rev(t), j))
    before = pl.BlockSpec((None, 8, STATE_BLOCK), lambda j, b, t: (b, jnp.maximum(rev(t) * ntile - 1, 0), j))
    mat = pl.BlockSpec((None, CHAN_BLOCK, STATE_BLOCK), lambda j, b, t: (j, 0, 0))
    lane = pl.BlockSpec((1, STATE_BLOCK), lambda j, b, t: (0, j))
    lane8 = pl.BlockSpec((8, STATE_BLOCK), lambda j, b, t: (0, j))
    dsp = pl.BlockSpec((1, CHAN_BLOCK), lambda j, b, t: (0, j))

    def body(dyg_ref, y_ref, u_ref, xr_ref, xi_ref, xrb_ref, xib_ref, ar_ref, ai_ref, bbr_ref, bbi_ref, cbr_ref,
             cbi_ref, d_ref, du_ref, dcbr_ref, dcbi_ref, dbbr_ref, dbbi_ref, dd_ref, dar_ref, dai_ref,
             lam_r, lam_i, car_r, car_i):
        b, t = pl.program_id(1), pl.program_id(2)
        first = jnp.logical_and(b == 0, t == 0)

        @pl.when(t == 0)
        def _():
            car_r[...] = jnp.zeros_like(car_r)
            car_i[...] = jnp.zeros_like(car_i)

        @pl.when(first)
        def _():
            for r in (dcbr_ref, dcbi_ref, dbbr_ref, dbbi_ref, dd_ref, dar_ref, dai_ref):
                r[...] = jnp.zeros_like(r)

        steps, (pr, pi) = _power_tables(ar_ref[...], -ai_ref[...], reverse=True)
        uf = u_ref[...]
        _, gelu_vjp = jax.vjp(jax.nn.gelu, y_ref[...])
        dy = gelu_vjp(dyg_ref[...])[0]
        dyb = dy.astype(BF16)
        dd_ref[...] += _colsum(dy * uf)
        lam_r[...] = _dot(dyb, cbr_ref[...], "nn")
        lam_i[...] = -_dot(dyb, cbi_ref[...], "nn")
        dcbr_ref[...] += _dot(dyb, xr_ref[...].astype(BF16), "tn")
        dcbi_ref[...] -= _dot(dyb, xi_ref[...].astype(BF16), "tn")
        row0 = lax.broadcasted_iota(jnp.int32, (8, STATE_BLOCK), 0) == 0
        has_before = rev(t) > 0
        xrb = jnp.where(has_before, xrb_ref[...], 0.0)
        xib = jnp.where(has_before, xib_ref[...], 0.0)

        def tile(s, carry):
            cr, ci, acc_r, acc_i = carry
            i = ntile - 1 - s
            sl = pl.ds(pl.multiple_of(i * 8, 8), 8)
            gr, gi = lam_r[sl, :], lam_i[sl, :]
            for k, (sr, si) in zip((1, 2, 4), steps):
                tr, ti = _cmul(sr, si, pltpu.roll(gr, 8 - k, 0), pltpu.roll(gi, 8 - k, 0))
                gr, gi = gr + tr, gi + ti
            tr, ti = _cmul(pr, pi, cr, ci)
            gr, gi = gr + tr, gi + ti
            lam_r[sl, :] = gr
            lam_i[sl, :] = gi
            sp = pl.ds(pl.multiple_of(jnp.maximum(i - 1, 0) * 8, 8), 8)
            pvr = jnp.where(i > 0, xr_ref[sp, :], xrb)
            pvi = jnp.where(i > 0, xi_ref[sp, :], xib)
            xsr = jnp.where(row0, pltpu.roll(pvr, 1, 0), pltpu.roll(xr_ref[sl, :], 1, 0))
            xsi = jnp.where(row0, pltpu.roll(pvi, 1, 0), pltpu.roll(xi_ref[sl, :], 1, 0))
            acc_r = acc_r + xsr * gr + xsi * gi
            acc_i = acc_i + xsr * gi - xsi * gr
            return gr[0:1, :], gi[0:1, :], acc_r, acc_i

        zero = jnp.zeros((8, STATE_BLOCK), F32)
        cr, ci, acc_r, acc_i = lax.fori_loop(0, ntile, tile, (car_r[0:1, :], car_i[0:1, :], zero, zero), unroll=2)
        car_r[0:1, :] = cr
        car_i[0:1, :] = ci
        dar_ref[...] += acc_r
        dai_ref[...] += acc_i
        lrb, lib = lam_r[...].astype(BF16), lam_i[...].astype(BF16)
        du = _dot(lrb, bbr_ref[...], "nt") + _dot(lib, bbi_ref[...], "nt") + d_ref[...] * dy
        du_ref[...] = du.astype(BF16)
        ub = uf.astype(BF16)
        dbbr_ref[...] += _dot(ub, lrb, "tn")
        dbbi_ref[...] += _dot(ub, lib, "tn")

    mat_shape = jax.ShapeDtypeStruct((SSM_LANE_BLOCKS, CHAN_BLOCK, STATE_BLOCK), F32)
    return hosted_call(
        body, comm, "ssm_bwd", (SSM_LANE_BLOCKS, LOCAL_BATCH, nt),
        [chan, chan, chan, state, state, before, before, lane, lane, mat, mat, mat, mat, dsp],
        [chan, mat, mat, mat, mat, dsp, lane8, lane8],
        [jax.ShapeDtypeStruct((LOCAL_BATCH, SEQ, SSM_W), BF16), mat_shape, mat_shape, mat_shape, mat_shape,
         jax.ShapeDtypeStruct((1, SSM_W), F32), jax.ShapeDtypeStruct((8, SSM_STATE_W), F32),
         jax.ShapeDtypeStruct((8, SSM_STATE_W), F32)],
        [pltpu.VMEM((SCAN_CHUNK, STATE_BLOCK), F32), pltpu.VMEM((SCAN_CHUNK, STATE_BLOCK), F32),
         pltpu.VMEM((8, STATE_BLOCK), F32), pltpu.VMEM((8, STATE_BLOCK), F32)],
        (dyg, y, u, xr, xi, xr, xi, ab_re, ab_im, bb_re, bb_im, cb_re, cb_im, d_skip),
        ("parallel", "arbitrary", "arbitrary"))


def _merge_fn(g0, g1, attn_d, za, zb):
    return jax.nn.sigmoid(g0) * attn_d + jax.nn.sigmoid(g1) * (za * jax.nn.sigmoid(zb))


def _swiglu_fn(a, b):
    return jax.nn.silu(a) * b


def _reduce_start(names, gw, shard_shapes):
    return swap_comm([_to_slots(n, gw[n], shard_shapes[n]) for n in names])


def _reduce_chip(names, swap, got, core):
    return exchange_comm([add_halves(n, g, r, core) for n, g, r in zip(names, swap.ins, got)])


def local_step(x, target, shards, small, core):
    g_mix, g_ffn, g_final = small["norm_mix_g"], small["norm_ffn_g"], small["norm_final_g"]
    tables = _rope_tables()
    seqs = lambda t: t.reshape(LOCAL_BATCH, SEQ, t.shape[-1])
    toks = lambda t: t.reshape(TOKENS, t.shape[-1])
    shard_shapes = {n: s.shape for n, s in shards.items()}
    w = {}

    def gather(names):
        return gather_comm([shards[n] for n in names])

    def arrived(names, slots):
        for n, s in zip(names, slots):
            w[n] = _from_slots(n, s)

    h, *slots = rowwise(lambda xv, g: (_rms(xv, g),), [x, g_mix], [(D_MODEL, BF16)], "norm_mix", comm=gather(["w_in"]))
    arrived(["w_in"], slots)
    w_qkv, w_u, w_gate = _qkv_order(w["w_in"][:QKV_W]), w["w_in"][QKV_W:QKV_W + SSM_W], w["w_in"][QKV_W + SSM_W:]
    qkv, *slots = proj_qkv(h, w_qkv, tables, comm=gather(["w_attn_out", "w_glu"]))
    arrived(["w_attn_out", "w_glu"], slots)
    qkv = seqs(qkv)
    u = seqs(matmul(h, w_u, "nt", F32, "proj_u"))
    gl, *slots = matmul(h, w_gate, "nt", BF16, "proj_gate", comm=gather(["w_out"]))
    arrived(["w_out"], slots)
    attn_b, attn, lse, *slots = attn_fwd(qkv, comm=gather(["w_ffn_gate"]))
    arrived(["w_ffn_gate"], slots)
    attn_b = toks(attn_b)
    attn_d = matmul(attn_b, w["w_attn_out"], "nn", F32, "attn_out")

    br_t = small["ssm_b_re"].transpose(2, 0, 1)
    bi_t = small["ssm_b_im"].transpose(2, 0, 1)
    log_dt = small["ssm_log_dt"].reshape(32, 1)
    ab_re, ab_im, bb_re_t, bb_im_t = ssm_prep(small["ssm_a_re"], small["ssm_a_im"], log_dt, br_t, bi_t)
    ab = [ab_re.reshape(1, SSM_STATE_W), ab_im.reshape(1, SSM_STATE_W)]
    bb = [_block_diag(bb_re_t).astype(BF16), _block_diag(bb_im_t).astype(BF16)]
    cb = [_block_diag(small["ssm_c_re"].transpose(1, 0, 2)).astype(BF16),
          _block_diag(small["ssm_c_im"].transpose(1, 0, 2)).astype(BF16)]
    d_skip = small["ssm_d"].reshape(1, SSM_W)
    y, yg, xr, xi, *slots = ssm_fwd(u, *ab, *bb, *cb, d_skip, comm=gather(["w_ffn_up"]))
    arrived(["w_ffn_up"], slots)
    yg2 = toks(yg)
    z = matmul(yg2, w["w_glu"], "nn", BF16, "glu")
    gate_ins = [(gl, D_MODEL, 0), (gl, D_MODEL, 1), attn_d, (z, D_MODEL, 0), (z, D_MODEL, 1)]
    (merged,) = rowwise(lambda *v: (_merge_fn(*[t.astype(F32) for t in v]),), gate_ins, [(D_MODEL, BF16)], "merge")
    x1, h2 = matmul_rows(merged, w["w_out"], "out_proj", lambda rows, g: (rows, _rms(rows, g)), [g_ffn],
                         [(D_MODEL, F32), (D_MODEL, BF16)], add=x)
    a, b, act, *slots = ffn_in(h2, w["w_ffn_gate"], w["w_ffn_up"], comm=gather(["w_ffn_down"]))
    arrived(["w_ffn_down"], slots)

    def final_fn(xv, g, tgt):
        yv, vjp = jax.vjp(_rms, xv, g)
        err = yv - tgt
        dx, dg = vjp(err * (1.0 / D_MODEL))
        loss = 0.5 * jnp.sum(jnp.mean(err * err, axis=-1, keepdims=True), axis=0, keepdims=True)
        return dx, dx, dg, jnp.broadcast_to(loss, (1, LANES))

    dx2, dx2_b, dg_final, loss = matmul_rows(act, w["w_ffn_down"], "ffn_down_loss", final_fn, [g_final, target],
                                             [(D_MODEL, F32), (D_MODEL, BF16)], accs=(D_MODEL, LANES), add=x1)
    gw, parts = {}, {}
    gw["w_ffn_down"] = matmul(act, dx2_b, "tn", F32, "d_ffn_down")
    da_b, db_b = ffn_in_bwd(dx2_b, w["w_ffn_down"], a, b)
    gw["w_ffn_gate"] = matmul(da_b, h2, "tn", F32, "d_ffn_gate")
    gw["w_ffn_up"] = matmul(db_b, h2, "tn", F32, "d_ffn_up")
    ffn = ["w_ffn_down", "w_ffn_gate", "w_ffn_up"]
    swap = _reduce_start(ffn[:2], gw, shard_shapes)
    dh2, *got = matmul(da_b, w["w_ffn_gate"], "nn", F32, "d_h2_gate", comm=swap)
    ffn_exchange = [_reduce_chip(ffn[:2], swap, got, core)]
    swap = _reduce_start(ffn[2:], gw, shard_shapes)

    def norm_bwd(dh, xv, g, skip):
        _, vjp = jax.vjp(_rms, xv, g)
        dx, dg = vjp(dh)
        dx = dx + skip
        return dx, dx, dg

    dx1, dx1_b, dg_ffn, *got = matmul_rows(db_b, w["w_ffn_up"], "d_h2_up_norm", norm_bwd, [x1, g_ffn, dx2],
                                           [(D_MODEL, F32), (D_MODEL, BF16)], accs=(D_MODEL,), add=dh2, comm=swap)
    ffn_up_exchange = _reduce_chip(ffn[2:], swap, got, core)
    gw["w_out"] = matmul(merged, dx1_b, "tn", F32, "d_out")
    dmerged = matmul(dx1_b, w["w_out"], "nt", F32, "d_merged")

    def merge_bwd(g0, g1, ad, za, zb, dm):
        _, vjp = jax.vjp(_merge_fn, *[t.astype(F32) for t in (g0, g1, ad, za, zb)])
        dg0, dg1, dad, dza, dzb = vjp(dm)
        return jnp.concatenate([dg0, dg1], axis=1), dad, jnp.concatenate([dza, dzb], axis=1)

    dgl_b, dattn_d_b, dz_b, parts["w_ffn_up"] = rowwise(
        merge_bwd, gate_ins + [dmerged], [(GATE_W, BF16), (D_MODEL, BF16), (GATE_W, BF16)], "merge_bwd",
        comm=ffn_up_exchange)
    gw["w_attn_out"] = matmul(attn_b, dattn_d_b, "tn", F32, "d_attn_out")
    dattn = seqs(matmul(dattn_d_b, w["w_attn_out"], "nt", F32, "d_attn"))
    gw["w_glu"] = matmul(yg2, dz_b, "tn", F32, "d_glu")
    dyg = seqs(matmul(dz_b, w["w_glu"], "nt", F32, "d_yg"))
    mixer = ["w_out", "w_attn_out", "w_glu"]
    swap = _reduce_start(mixer, gw, shard_shapes)
    du_b, dcb_re, dcb_im, dbb_re, dbb_im, dd, da_re8, da_im8, *rest = ssm_bwd(
        dyg, y, u, xr, xi, *ab, *bb, *cb, d_skip, comm=join_comms(ffn_exchange + [swap]))
    for n, p in zip(ffn[:2], rest[:2]):
        parts[n] = p
    mixer_exchange = _reduce_chip(mixer, swap, rest[2:], core)
    du_b = toks(du_b)
    g_ab_re = jnp.sum(da_re8, axis=0).reshape(32, 64)
    g_ab_im = jnp.sum(da_im8, axis=0).reshape(32, 64)
    d_lr, d_li, d_ldt, d_br_t, d_bi_t = ssm_prep_bwd(
        small["ssm_a_re"], small["ssm_a_im"], log_dt, br_t, bi_t,
        g_ab_re, g_ab_im, _block_diag_t(dbb_re), _block_diag_t(dbb_im))
    gs = {
        "ssm_a_re": d_lr, "ssm_a_im": d_li, "ssm_log_dt": d_ldt.reshape(1, 32),
        "ssm_b_re": d_br_t.transpose(1, 2, 0), "ssm_b_im": d_bi_t.transpose(1, 2, 0),
        "ssm_c_re": _block_diag_t(dcb_re).transpose(1, 0, 2), "ssm_c_im": _block_diag_t(dcb_im).transpose(1, 0, 2),
        "ssm_d": dd.reshape(32, 16),
    }
    ssm_gather = small_comm(_pack_small(SSM_SMALL, gs, SSM_ROWS))
    dqkv_b, *rest = attn_bwd(qkv, tables, dattn, attn, lse, comm=join_comms([mixer_exchange, ssm_gather]))
    for n, p in zip(mixer, rest):
        parts[n] = p
    ssm_shares = rest[len(mixer)]
    dqkv_b = toks(dqkv_b)
    d_qkv = matmul(dqkv_b, h, "tn", F32, "d_w_qkv")
    d_u = matmul(du_b, h, "tn", F32, "d_w_u")
    d_gate = matmul(dgl_b, h, "tn", F32, "d_w_gate")
    gw["w_in"] = jnp.concatenate([_qkv_order(d_qkv, back=True), d_u, d_gate], axis=0)
    swap = _reduce_start(["w_in"], gw, shard_shapes)
    dh, *got = matmul(dqkv_b, w_qkv, "nn", F32, "d_h_qkv", comm=swap)
    w_in_exchange = _reduce_chip(["w_in"], swap, got, core)
    grad_x, dg_mix, parts["w_in"] = mix_in_bwd([du_b, dgl_b], [w_u, w_gate], dh, x, g_mix, dx1, comm=w_in_exchange)
    gs_norm = {"norm_mix_g": dg_mix, "norm_ffn_g": dg_ffn, "norm_final_g": dg_final}
    return loss, grad_x, parts, ssm_shares, gs_norm


ANY = pl.BlockSpec(memory_space=pl.ANY)
BIG = ("w_in", "w_glu", "w_attn_out", "w_out", "w_ffn_gate", "w_ffn_up", "w_ffn_down")
TRANSPOSED = ("w_in", "w_ffn_gate", "w_ffn_up")
ROW_SHARDED = TRANSPOSED + ("w_out", "w_ffn_down")
SMALL = ("norm_mix_g", "ssm_a_re", "ssm_a_im", "ssm_log_dt", "ssm_b_re", "ssm_b_im", "ssm_c_re", "ssm_c_im",
         "ssm_d", "norm_ffn_g", "norm_final_g")
WEIGHTS = ("norm_mix_g", "w_in", "ssm_a_re", "ssm_a_im", "ssm_log_dt", "ssm_b_re", "ssm_b_im", "ssm_c_re",
           "ssm_c_im", "ssm_d", "w_glu", "w_attn_out", "w_out", "norm_ffn_g", "w_ffn_gate", "w_ffn_up",
           "w_ffn_down", "norm_final_g")
SSM_SMALL = SMALL[1:9]
NORM_SMALL = (SMALL[0],) + SMALL[9:]
SSM_ROWS, NORM_ROWS = 1064, 32
N_BIG = len(BIG)


def _position():
    return lax.axis_index("x"), lax.axis_index("y"), lax.axis_index("c")


def _other_chips(x, y):
    return [(1 - x, y), (x, 1 - y), (1 - x, 1 - y)]


def _remote(src, dst, send_sem, recv_sem, device):
    return pltpu.make_async_remote_copy(src_ref=src, dst_ref=dst, send_sem=send_sem, recv_sem=recv_sem,
                                        device_id=device, device_id_type=MESH)


_later = functools.partial


def _two_level_phases(copies):
    def first(*refs):
        locals_, sends, _, _, _ = copies(*refs)
        for cp in locals_ + sends:
            cp().start()

    def mid(*refs):
        _, _, arrived, passed, _ = copies(*refs)
        for got, cp in zip(arrived, passed):
            got().wait_recv()
            cp().start()

    def last(*refs):
        locals_, sends, _, passed, from_sibling = copies(*refs)
        for cp in from_sibling:
            cp().wait_recv()
        for cp in sends + passed:
            cp().wait_send()
        for cp in locals_:
            cp().wait()

    return first, mid, last


def _half(ref, chip, which):
    rows = ref.shape[1] // 2
    return ref.at[chip, pl.ds(which * rows, rows), :]


class Comm:
    def __init__(self, ins, out_shapes, sems, first, mid, last):
        self.ins, self.out_shapes, self.sems = list(ins), list(out_shapes), list(sems)
        self.first, self.mid, self.last = first, mid, last


def join_comms(comms):
    def cut(refs_by_kind):
        offs, parts = [0, 0, 0], []
        for cm in comms:
            sizes = (len(cm.ins), len(cm.out_shapes), len(cm.sems))
            parts.append(tuple(refs_by_kind[k][offs[k]:offs[k] + sizes[k]] for k in range(3)))
            offs = [o + s for o, s in zip(offs, sizes)]
        return parts

    def phase(which):
        def run(ins, outs, sems):
            for cm, part in zip(comms, cut((ins, outs, sems))):
                fn = getattr(cm, which)
                if fn is not None:
                    fn(*part)
        return run

    return Comm(sum((cm.ins for cm in comms), []), sum((cm.out_shapes for cm in comms), []),
                sum((cm.sems for cm in comms), []), phase("first"), phase("mid"), phase("last"))


def _comm_operands(comm):
    if comm is None:
        return [], [], []
    return comm.ins, comm.out_shapes, comm.sems


def _comm_begin(comm, refs, step, n_steps):
    if comm is None:
        return
    pl.when(step == 0)(lambda: comm.first(*refs))
    if comm.mid is not None:
        pl.when(step == (n_steps * 3) // 4)(lambda: comm.mid(*refs))


def _comm_end(comm, refs, step, n_steps):
    if comm is not None:
        pl.when(step == n_steps - 1)(lambda: comm.last(*refs))


def _comm_refs(comm, refs, n_in, n_out):
    if comm is None:
        return list(refs), None
    ci, co, cs = len(comm.ins), len(comm.out_shapes), len(comm.sems)
    o0 = n_in + ci
    s0 = o0 + n_out + co
    host = list(refs[:n_in]) + list(refs[o0:o0 + n_out]) + list(refs[s0:len(refs) - cs])
    return host, (list(refs[n_in:o0]), list(refs[o0 + n_out:s0]), list(refs[len(refs) - cs:]))


def run_comm(comm, name):
    n_in, n_out = len(comm.ins), len(comm.out_shapes)

    def body(*refs):
        parts = (list(refs[:n_in]), list(refs[n_in:n_in + n_out]), list(refs[n_in + n_out:]))
        comm.first(*parts)
        if comm.mid is not None:
            comm.mid(*parts)
        comm.last(*parts)

    return pl.pallas_call(body, name=name, in_specs=[ANY] * n_in, out_specs=[ANY] * n_out,
                          out_shape=comm.out_shapes, scratch_shapes=comm.sems)(*comm.ins)


def hosted_call(work, comm, name, grid, in_specs, out_specs, out_shape, scratch_shapes, args, semantics):
    c_ins, c_outs, c_sems = _comm_operands(comm)
    n_steps = math.prod(grid)

    def body(*refs):
        host, c_refs = _comm_refs(comm, refs, len(in_specs), len(out_specs))
        step = 0
        for axis, size in enumerate(grid):
            step = step * size + pl.program_id(axis)
        _comm_begin(comm, c_refs, step, n_steps)
        work(*host)
        _comm_end(comm, c_refs, step, n_steps)

    return pl.pallas_call(
        body, name=name, grid=grid, in_specs=list(in_specs) + [ANY] * len(c_ins),
        out_specs=list(out_specs) + [ANY] * len(c_outs), out_shape=list(out_shape) + c_outs,
        scratch_shapes=list(scratch_shapes) + c_sems,
        compiler_params=_params(semantics if comm is None else ("arbitrary",) * len(grid)),
    )(*args, *c_ins)


def gather_comm(shards):
    n = len(shards)

    def copies(srcs, outs, sems):
        send_sems, recv_sems, local_sems = sems
        x, y, c = _position()
        me = 2 * x + y
        sibling = (x, y, 1 - c)
        chips = _other_chips(x, y)
        locals_ = [_later(pltpu.make_async_copy, s, o.at[me], local_sems.at[i])
                   for i, (s, o) in enumerate(zip(srcs, outs))]
        sends, arrived, passed, from_sibling = [], [], [], []
        for j, (px, py) in enumerate(chips):
            for i, (s, o) in enumerate(zip(srcs, outs)):
                rows = s.shape[0] // 2
                sends.append(_later(_remote, s.at[pl.ds(c * rows, rows), :], _half(o, me, c), send_sems.at[i, j],
                                    recv_sems.at[i, j], (px, py, c)))
                got = _half(o, 2 * px + py, c)
                arrived.append(_later(_remote, got, got, send_sems.at[i, j], recv_sems.at[i, j], (px, py, c)))
                passed.append(_later(_remote, got, got, send_sems.at[i, 3 + j], recv_sems.at[i, 3 + j], sibling))
                other = _half(o, 2 * px + py, 1 - c)
                from_sibling.append(_later(_remote, other, other, send_sems.at[i, 3 + j], recv_sems.at[i, 3 + j],
                                           sibling))
        return locals_, sends, arrived, passed, from_sibling

    return Comm(shards, [jax.ShapeDtypeStruct((N_CHIPS,) + s.shape, s.dtype) for s in shards],
                [pltpu.SemaphoreType.DMA((n, 6)), pltpu.SemaphoreType.DMA((n, 6)), pltpu.SemaphoreType.DMA((n,))],
                *_two_level_phases(copies))


def swap_comm(grads):
    n = len(grads)

    def copies(srcs, gots, sems):
        send_sems, recv_sems = sems
        x, y, c = _position()
        out = []
        for i, (s, o) in enumerate(zip(srcs, gots)):
            rows = s.shape[1] // 2
            out.append(_remote(s.at[:, pl.ds((1 - c) * rows, rows), :], o, send_sems.at[i], recv_sems.at[i],
                               (x, y, 1 - c)))
        return out

    def first(srcs, gots, sems):
        for cp in copies(srcs, gots, sems):
            cp.start()

    def last(srcs, gots, sems):
        for cp in copies(srcs, gots, sems):
            cp.wait()

    return Comm(grads, [jax.ShapeDtypeStruct((N_CHIPS, g.shape[1] // 2, g.shape[2]), g.dtype) for g in grads],
                [pltpu.SemaphoreType.DMA((n,)), pltpu.SemaphoreType.DMA((n,))], first, None, last)


def add_halves(name, g, got, core):
    _, half, cols = got.shape
    mine = pl.BlockSpec((None, half, cols), lambda k, c_ref: (k, c_ref[0], 0))
    other = pl.BlockSpec((None, half, cols), lambda k, c_ref: (k, 0, 0))

    def body(c_ref, g_ref, got_ref, o_ref):
        o_ref[...] = (g_ref[...] + got_ref[...]).astype(BF16)

    return pl.pallas_call(
        body, name="add_halves_" + name,
        grid_spec=pltpu.PrefetchScalarGridSpec(num_scalar_prefetch=1, grid=(N_CHIPS,), in_specs=[mine, other],
                                               out_specs=other),
        out_shape=jax.ShapeDtypeStruct(got.shape, BF16),
        compiler_params=_params(("parallel",)),
    )(core, g, got)


def exchange_comm(parts):
    n = len(parts)

    def copies(srcs, outs, sems):
        send_sems, recv_sems, local_sems = sems
        x, y, c = _position()
        me = 2 * x + y
        sibling = (x, y, 1 - c)
        chips = _other_chips(x, y)
        locals_, sends, arrived, passed, from_sibling = [], [], [], [], []
        for i, (s, o) in enumerate(zip(srcs, outs)):
            locals_.append(_later(pltpu.make_async_copy, s.at[me], _half(o, me, c), local_sems.at[i]))
            sends.append(_later(_remote, s.at[me], _half(o, me, c), send_sems.at[i, 3], recv_sems.at[i, 3], sibling))
            other = _half(o, me, 1 - c)
            from_sibling.append(_later(_remote, other, other, send_sems.at[i, 3], recv_sems.at[i, 3], sibling))
        for j, (px, py) in enumerate(chips):
            for i, (s, o) in enumerate(zip(srcs, outs)):
                sends.append(_later(_remote, s.at[2 * px + py], _half(o, me, c), send_sems.at[i, j],
                                    recv_sems.at[i, j], (px, py, c)))
                got = _half(o, 2 * px + py, c)
                arrived.append(_later(_remote, got, got, send_sems.at[i, j], recv_sems.at[i, j], (px, py, c)))
                passed.append(_later(_remote, got, got, send_sems.at[i, 4 + j], recv_sems.at[i, 4 + j], sibling))
                other = _half(o, 2 * px + py, 1 - c)
                from_sibling.append(_later(_remote, other, other, send_sems.at[i, 4 + j], recv_sems.at[i, 4 + j],
                                           sibling))
        return locals_, sends, arrived, passed, from_sibling

    return Comm(parts, [jax.ShapeDtypeStruct((N_CHIPS, 2 * p.shape[1], p.shape[2]), p.dtype) for p in parts],
                [pltpu.SemaphoreType.DMA((n, 7)), pltpu.SemaphoreType.DMA((n, 7)), pltpu.SemaphoreType.DMA((n,))],
                *_two_level_phases(copies))


def small_comm(pack):
    def copies(srcs, outs, sems):
        (src_ref,), (out_ref,), (send_sems, recv_sems, local_sem) = srcs, outs, sems
        x, y, c = _position()
        me = 4 * x + 2 * y + c
        flips = [(fx, fy, fc) for fx in (0, 1) for fy in (0, 1) for fc in (0, 1)][1:]
        peers = [(1 - x if fx else x, 1 - y if fy else y, 1 - c if fc else c) for fx, fy, fc in flips]
        local = _later(pltpu.make_async_copy, src_ref, out_ref.at[me], local_sem)
        sends = [_later(_remote, src_ref, out_ref.at[me], send_sems.at[j], recv_sems.at[j], peer)
                 for j, peer in enumerate(peers)]
        arrived = []
        for j, (px, py, pc) in enumerate(peers):
            got = out_ref.at[4 * px + 2 * py + pc]
            arrived.append(_later(_remote, got, got, send_sems.at[j], recv_sems.at[j], (px, py, pc)))
        return local, sends, arrived

    def first(*refs):
        local, sends, _ = copies(*refs)
        for cp in [local] + sends:
            cp().start()

    def last(*refs):
        local, sends, arrived = copies(*refs)
        for cp in arrived:
            cp().wait_recv()
        for cp in sends:
            cp().wait_send()
        local().wait()

    return Comm([pack], [jax.ShapeDtypeStruct((N_DEV,) + pack.shape, pack.dtype)],
                [pltpu.SemaphoreType.DMA((7,)), pltpu.SemaphoreType.DMA((7,)), pltpu.SemaphoreType.DMA],
                first, None, last)


def _adam_fn(w, g, m, v):
    m = ADAM_B1 * m + (1.0 - ADAM_B1) * g
    v = ADAM_B2 * v + (1.0 - ADAM_B2) * jnp.square(g)
    m_hat = m / (1.0 - ADAM_B1 ** ADAM_STEP)
    v_hat = v / (1.0 - ADAM_B2 ** ADAM_STEP)
    return -ADAM_LR * (m_hat / (jnp.sqrt(v_hat) + ADAM_EPS) + ADAM_WD * w), m, v


def adam_big(name, parts, w, m, v):
    rows, cols = w.shape
    tm = _pick(rows, 384, 16)

    def fn(p0, p1, p2, p3, wv, mv, vv):
        g = ((p0.astype(F32) + p1.astype(F32)) + p2.astype(F32)) + p3.astype(F32)
        return (g,) + _adam_fn(wv, g, mv, vv)

    return rowwise(fn, [parts, w, m, v], [(cols, F32)] * 4, "adam_" + name, tm=tm, rows=rows)


def adam_small(name, gathered, w, m, v):
    def body(g_ref, w_ref, m_ref, v_ref, go_ref, d_ref, mo_ref, vo_ref):
        g = g_ref[0]
        for k in range(1, N_DEV):
            g = g + g_ref[k]
        go_ref[...] = g
        d_ref[...], mo_ref[...], vo_ref[...] = _adam_fn(w_ref[...], g, m_ref[...], v_ref[...])

    return pl.pallas_call(body, name=name, out_shape=[jax.ShapeDtypeStruct(w.shape, F32)] * 4,
                          compiler_params=_params())(gathered, w, m, v)


def _pack_small(names, vals, rows, last=None):
    flat = [vals[n].reshape(-1) for n in names]
    if last is not None:
        flat.append(last.reshape(-1))
    flat = jnp.concatenate(flat)
    return jnp.pad(flat, (0, rows * LANES - flat.shape[0])).reshape(rows, LANES)


def _unpack_small(names, pack, shapes):
    flat, out, off = pack.reshape(-1), {}, 0
    for n in names:
        size = math.prod(shapes[n])
        out[n] = flat[off:off + size].reshape(shapes[n])
        off += size
    return out, flat[off]


def _to_slots(name, g, shard_shape):
    rows, cols = shard_shape
    if name in ROW_SHARDED:
        return g.reshape(N_CHIPS, rows, cols)
    return g.reshape(rows, N_CHIPS, cols).transpose(1, 0, 2)


def _from_slots(name, s):
    _, rows, cols = s.shape
    if name in ROW_SHARDED:
        return s.reshape(N_CHIPS * rows, cols)
    return s.transpose(1, 0, 2).reshape(rows, N_CHIPS * cols)


def kernel(x, norm_mix_g, w_in, ssm_a_re, ssm_a_im, ssm_log_dt, ssm_b_re, ssm_b_im, ssm_c_re, ssm_c_im, ssm_d, w_glu, w_attn_out, w_out, norm_ffn_g, w_ffn_gate, w_ffn_up, w_ffn_down, norm_final_g, loss_target, m_norm_mix_g, m_w_in, m_ssm_a_re, m_ssm_a_im, m_ssm_log_dt, m_ssm_b_re, m_ssm_b_im, m_ssm_c_re, m_ssm_c_im, m_ssm_d, m_w_glu, m_w_attn_out, m_w_out, m_norm_ffn_g, m_w_ffn_gate, m_w_ffn_up, m_w_ffn_down, m_norm_final_g, v_norm_mix_g, v_w_in, v_ssm_a_re, v_ssm_a_im, v_ssm_log_dt, v_ssm_b_re, v_ssm_b_im, v_ssm_c_re, v_ssm_c_im, v_ssm_d, v_w_glu, v_w_attn_out, v_w_out, v_norm_ffn_g, v_w_ffn_gate, v_w_ffn_up, v_w_ffn_down, v_norm_final_g):
    given = dict(locals())
    def local(name, prefix=""):
        t = given[prefix + name][0]
        return t.T if name in TRANSPOSED else t

    shard = {n: local(n) for n in BIG}
    shapes = {n: given[n].shape for n in WEIGHTS}

    small = {n: given[n] for n in SMALL}
    small_2d = dict(small)
    for n in ("ssm_a_re", "ssm_a_im", "ssm_b_re", "ssm_b_im", "ssm_c_re", "ssm_c_im", "ssm_d"):
        small_2d[n] = small[n][0]
    small_2d["norm_final_g"] = norm_final_g.reshape(1, D_MODEL)

    core = lax.axis_index("c").astype(jnp.int32).reshape(1)
    loss, grad_x, parts, ssm_shares, gs_norm = local_step(
        x.reshape(TOKENS, D_MODEL), loss_target.reshape(TOKENS, D_MODEL),
        {n: shard[n].astype(BF16) for n in BIG}, small_2d, core)

    (norm_shares,) = run_comm(small_comm(_pack_small(NORM_SMALL, gs_norm, NORM_ROWS, last=loss)), "gather_norm_grads")
    small_out = [{} for _ in range(4)]
    for names, rows, shares in ((SSM_SMALL, SSM_ROWS, ssm_shares), (NORM_SMALL, NORM_ROWS, norm_shares)):
        packs = [_pack_small(names, {n: given[p + n] for n in names}, rows) for p in ("", "m_", "v_")]
        for kind, t in enumerate(adam_small("adam_" + names[0], shares, *packs)):
            vals, after = _unpack_small(names, t, shapes)
            small_out[kind].update(vals)
            if kind == 0:
                total_loss = after

    big_out = {}
    for n in BIG:
        res = adam_big(n, parts[n], shard[n], local(n, "m_"), local(n, "v_"))
        big_out[n] = [(t.T if n in TRANSPOSED else t)[None] for t in res]

    outs = [total_loss, grad_x.reshape(LOCAL_BATCH, SEQ, D_MODEL)]
    for kind in range(4):
        for n in WEIGHTS:
            outs.append(big_out[n][kind] if n in BIG else small_out[kind][n])
    return tuple(outs)
```

```python
import functools
import math

import jax
import jax.numpy as jnp
import numpy as np
from jax import lax
from jax.experimental import pallas as pl
from jax.experimental.pallas import tpu as pltpu

F32 = jnp.float32
BF16 = jnp.bfloat16
MESH = pl.DeviceIdType.MESH

D_MODEL = 1024
SEQ = 2048
LOCAL_BATCH = 2
TOKENS = LOCAL_BATCH * SEQ
HEAD_DIM = 64
HEADS_PER_GROUP = 4
GROUP_W = HEADS_PER_GROUP * HEAD_DIM
N_GROUPS = 3
DILATIONS = (1, 4, 16)
ATTN_BLOCK = 128
ROPE_DIM = 16
ROPE_THETA = 500000.0
QKV_W = 3 * N_GROUPS * GROUP_W
SSM_W = 512
SSM_STATE_W = 2048
SSM_LANE_BLOCKS = 4
GATE_W = 2 * D_MODEL
D_FF = 2816
RMS_EPS = 1e-6
NEG_INF = -1e30
ADAM_LR, ADAM_B1, ADAM_B2, ADAM_EPS, ADAM_WD, ADAM_STEP = 0.001, 0.9, 0.999, 1e-08, 0.01, 10
N_CHIPS = 4
N_DEV = 8

VMEM_LIMIT = 56 * 1024 * 1024
LANES = 128


def _params(sem=None):
    return pltpu.CompilerParams(dimension_semantics=sem, vmem_limit_bytes=VMEM_LIMIT)


def _pick(n, cap, align=LANES):
    best = None
    for d in range(align, min(n, cap) + 1, align):
        if n % d == 0:
            best = d
    return n if best is None or n <= cap else best


_DIMS = {"nn": (((1,), (0,)), ((), ())), "nt": (((1,), (1,)), ((), ())), "tn": (((0,), (0,)), ((), ()))}


def _dot(a, b, mode):
    return lax.dot_general(a, b, _DIMS[mode], preferred_element_type=F32)


def matmul(a, b, mode, out_dtype, name, add=None, comm=None):
    if mode == "nn":
        (m, k), n = a.shape, b.shape[1]
    elif mode == "nt":
        (m, k), n = a.shape, b.shape[0]
    else:
        (k, m), n = a.shape, b.shape[1]
    tn = _pick(n, 1408 if mode != "tn" else 512)
    tk = _pick(k, 2816) if mode != "tn" else k
    tm = _pick(m, 1408)
    out_bytes = jnp.dtype(out_dtype).itemsize

    def need(tm_):
        return 2 * 2 * (tm_ * tk + tk * tn) + tm_ * tn * (4 + 2 * out_bytes + (8 if add is not None else 0))

    while need(tm) > 40 * 1024 * 1024 and tm % 256 == 0:
        tm //= 2
    nk = k // tk
    a_spec = {"nn": pl.BlockSpec((tm, tk), lambda i, j, kk: (i, kk)),
              "nt": pl.BlockSpec((tm, tk), lambda i, j, kk: (i, kk)),
              "tn": pl.BlockSpec((tk, tm), lambda i, j, kk: (kk, i))}[mode]
    b_spec = {"nn": pl.BlockSpec((tk, tn), lambda i, j, kk: (kk, j)),
              "nt": pl.BlockSpec((tn, tk), lambda i, j, kk: (j, kk)),
              "tn": pl.BlockSpec((tk, tn), lambda i, j, kk: (kk, j))}[mode]
    o_spec = pl.BlockSpec((tm, tn), lambda i, j, kk: (i, j))

    def body(a_ref, b_ref, *rest):
        if add is not None:
            add_ref, o_ref, acc_ref = rest
        else:
            o_ref, acc_ref = rest
        part = _dot(a_ref[...], b_ref[...], mode)
        if nk == 1:
            res = part if add is None else part + add_ref[...]
            o_ref[...] = res.astype(out_dtype)
            return
        kk = pl.program_id(2)

        @pl.when(kk == 0)
        def _():
            acc_ref[...] = part

        @pl.when(kk > 0)
        def _():
            acc_ref[...] += part

        @pl.when(kk == nk - 1)
        def _():
            res = acc_ref[...] if add is None else acc_ref[...] + add_ref[...]
            o_ref[...] = res.astype(out_dtype)

    in_specs = [a_spec, b_spec] + ([o_spec] if add is not None else [])
    args = (a, b) + ((add,) if add is not None else ())
    res = hosted_call(
        body, comm, name, (m // tm, n // tn, nk), in_specs, [o_spec], [jax.ShapeDtypeStruct((m, n), out_dtype)],
        [pltpu.VMEM((tm, tn) if nk > 1 else (8, LANES), F32)], args, ("parallel", "parallel", "arbitrary"))
    return res[0] if comm is None else res


def matmul_rows(a, b, name, fn, extra, outs, accs=(), add=None, comm=None, tm=512):
    (m, k), n = a.shape, b.shape[1]
    n_fixed = 2 + (add is not None)
    row_spec = lambda cols: pl.BlockSpec((tm, cols), lambda i: (i, 0))
    in_specs = [row_spec(k), pl.BlockSpec((k, n), lambda i: (0, 0))] + ([row_spec(n)] if add is not None else [])
    in_specs += [pl.BlockSpec(e.shape, lambda i: (0, 0)) if e.shape[0] == 1 else row_spec(e.shape[1]) for e in extra]
    out_specs = [row_spec(c) for c, _ in outs] + [pl.BlockSpec((1, c), lambda i: (0, 0)) for c in accs]
    out_shape = [jax.ShapeDtypeStruct((m, c), dt) for c, dt in outs] + [jax.ShapeDtypeStruct((1, c), F32) for c in accs]

    def body(*refs):
        rows = _dot(refs[0][...], refs[1][...], "nn")
        if add is not None:
            rows = rows + refs[2][...]
        n_in = n_fixed + len(extra)
        res = fn(rows, *[r[...] for r in refs[n_fixed:n_in]])
        for r, v in zip(refs[n_in:n_in + len(outs)], res[:len(outs)]):
            r[...] = v.astype(r.dtype)
        first = pl.program_id(0) == 0
        for r, v in zip(refs[n_in + len(outs):], res[len(outs):]):
            @pl.when(first)
            def _(r=r, v=v):
                r[...] = v

            @pl.when(jnp.logical_not(first))
            def _(r=r, v=v):
                r[...] += v

    args = (a, b) + ((add,) if add is not None else ()) + tuple(extra)
    return hosted_call(body, comm, name, (m // tm,), in_specs, out_specs, out_shape, [], args, ("arbitrary",))


FFN_TM, FFN_TN = 512, 1408


def ffn_in(h2, wg_t, wu_t, comm=None):
    def body(h_ref, wg_ref, wu_ref, a_ref, b_ref, act_ref):
        hv = h_ref[...]
        a, b = _dot(hv, wg_ref[...], "nt"), _dot(hv, wu_ref[...], "nt")
        a_ref[...] = a.astype(BF16)
        b_ref[...] = b.astype(BF16)
        act_ref[...] = _swiglu_fn(a, b).astype(BF16)

    rows = pl.BlockSpec((FFN_TM, D_MODEL), lambda i, j: (i, 0))
    wts = pl.BlockSpec((FFN_TN, D_MODEL), lambda i, j: (j, 0))
    out = pl.BlockSpec((FFN_TM, FFN_TN), lambda i, j: (i, j))
    return hosted_call(body, comm, "ffn_in", (TOKENS // FFN_TM, D_FF // FFN_TN), [rows, wts, wts], [out] * 3,
                       [jax.ShapeDtypeStruct((TOKENS, D_FF), BF16)] * 3, [], (h2, wg_t, wu_t),
                       ("parallel", "parallel"))


def ffn_in_bwd(dx2_b, wd, a, b):
    def body(dx_ref, wd_ref, a_ref, b_ref, da_ref, db_ref):
        dx = dx_ref[...]
        for lo in range(0, FFN_TN, 512):
            cols = slice(lo, min(lo + 512, FFN_TN))
            dact = _dot(dx, wd_ref[cols, :], "nt")
            _, vjp = jax.vjp(_swiglu_fn, a_ref[:, cols].astype(F32), b_ref[:, cols].astype(F32))
            da, db = vjp(dact)
            da_ref[:, cols] = da.astype(BF16)
            db_ref[:, cols] = db.astype(BF16)

    rows = pl.BlockSpec((FFN_TM, D_MODEL), lambda i, j: (i, 0))
    wts = pl.BlockSpec((FFN_TN, D_MODEL), lambda i, j: (j, 0))
    out = pl.BlockSpec((FFN_TM, FFN_TN), lambda i, j: (i, j))
    return pl.pallas_call(
        body, name="ffn_in_bwd", grid=(TOKENS // FFN_TM, D_FF // FFN_TN), in_specs=[rows, wts, out, out],
        out_specs=[out] * 2, out_shape=[jax.ShapeDtypeStruct((TOKENS, D_FF), BF16)] * 2,
        compiler_params=_params(("parallel", "parallel")),
    )(dx2_b, wd, a, b)


def mix_in_bwd(grads, weights, partial, x, g, skip, comm=None):
    n = len(grads)
    tm = 512

    def body(*refs):
        a_refs, b_refs = refs[:n], refs[n:2 * n]
        part_ref, x_ref, g_ref, skip_ref, gx_ref, dg_ref = refs[2 * n:]
        dh = part_ref[...]
        for a_ref, b_ref in zip(a_refs, b_refs):
            dh = dh + _dot(a_ref[...], b_ref[...], "nn")
        _, vjp = jax.vjp(_rms, x_ref[...], g_ref[...])
        dx, dg = vjp(dh)
        gx_ref[...] = dx + skip_ref[...]
        first = pl.program_id(0) == 0

        @pl.when(first)
        def _():
            dg_ref[...] = dg

        @pl.when(jnp.logical_not(first))
        def _():
            dg_ref[...] += dg

    rows = pl.BlockSpec((tm, D_MODEL), lambda i: (i, 0))
    gain = pl.BlockSpec((1, D_MODEL), lambda i: (0, 0))
    in_specs = [pl.BlockSpec((tm, a.shape[1]), lambda i: (i, 0)) for a in grads]
    in_specs += [pl.BlockSpec(b.shape, lambda i: (0, 0)) for b in weights]
    return hosted_call(
        body, comm, "mix_in_bwd", (TOKENS // tm,), in_specs + [rows, rows, gain, rows], [rows, gain],
        [jax.ShapeDtypeStruct((TOKENS, D_MODEL), F32), jax.ShapeDtypeStruct((1, D_MODEL), F32)], [],
        (*grads, *weights, partial, x, g, skip), ("arbitrary",))


def rowwise(fn, ins, outs, name, accs=(), tm=256, rows=TOKENS, comm=None):
    in_specs, args = [], []
    for item in ins:
        arr, width, blk = item if isinstance(item, tuple) else (item, None, 0)
        if arr.ndim == 3:
            for k in range(arr.shape[0]):
                in_specs.append(pl.BlockSpec((None, tm, arr.shape[2]), functools.partial(lambda i, k_: (k_, i, 0), k_=k)))
                args.append(arr)
            continue
        if arr.shape[0] == 1:
            in_specs.append(pl.BlockSpec(arr.shape, lambda i: (0, 0)))
        elif width is None:
            in_specs.append(pl.BlockSpec((tm, arr.shape[1]), lambda i: (i, 0)))
        else:
            in_specs.append(pl.BlockSpec((tm, width), functools.partial(lambda i, blk_: (i, blk_), blk_=blk)))
        args.append(arr)
    out_specs = [pl.BlockSpec((tm, c), lambda i: (i, 0)) for c, _ in outs]
    out_specs += [pl.BlockSpec((1, c), lambda i: (0, 0)) for c in accs]
    out_shape = [jax.ShapeDtypeStruct((rows, c), dt) for c, dt in outs]
    out_shape += [jax.ShapeDtypeStruct((1, c), F32) for c in accs]
    n_in, n_out = len(args), len(outs)
    c_ins, c_outs, c_sems = _comm_operands(comm)

    def body(*refs):
        refs, c_refs = _comm_refs(comm, refs, n_in, n_out + len(accs))
        step = pl.program_id(0)
        _comm_begin(comm, c_refs, step, rows // tm)
        res = fn(*[r[...] for r in refs[:n_in]])
        for r, v in zip(refs[n_in:n_in + n_out], res[:n_out]):
            r[...] = v.astype(r.dtype)
        first = step == 0
        for r, v in zip(refs[n_in + n_out:], res[n_out:]):
            @pl.when(first)
            def _(r=r, v=v):
                r[...] = v

            @pl.when(jnp.logical_not(first))
            def _(r=r, v=v):
                r[...] += v
        _comm_end(comm, c_refs, step, rows // tm)

    return pl.pallas_call(
        body, name=name, grid=(rows // tm,), in_specs=in_specs + [ANY] * len(c_ins),
        out_specs=out_specs + [ANY] * len(c_outs), out_shape=out_shape + c_outs, scratch_shapes=c_sems,
        compiler_params=_params(("arbitrary",)),
    )(*args, *c_ins)


def _rms(x, g):
    return x * lax.rsqrt(jnp.mean(x * x, axis=-1, keepdims=True) + RMS_EPS) * g


def _colsum(v):
    return jnp.sum(v, axis=0, keepdims=True)


PAIR_W = 2 * HEAD_DIM
N_PAIRS = HEADS_PER_GROUP // 2


def _qkv_order(w_t, back=False):
    dims = (N_PAIRS, N_GROUPS, 3) if back else (3, N_GROUPS, N_PAIRS)
    return w_t.reshape(dims + (PAIR_W, w_t.shape[1])).transpose(2, 1, 0, 3, 4).reshape(QKV_W, w_t.shape[1])


def _rope_tables():
    half = ROPE_DIM // 2
    inv = np.power(np.float32(ROPE_THETA), -np.arange(half, dtype=np.float32) * np.float32(2.0 / ROPE_DIM))
    ang = (np.arange(SEQ, dtype=np.float32)[:, None] * inv[None, :]).astype(np.float32)
    cos, sin = np.cos(ang), np.sin(ang)
    zeros = np.zeros((SEQ, HEAD_DIM - ROPE_DIM), np.float32)
    zh = np.zeros((SEQ, half), np.float32)
    c = np.concatenate([cos, cos, zeros + 1.0], axis=1)
    sa = np.concatenate([-sin, zh, zeros], axis=1)
    sb = np.concatenate([zh, sin, zeros], axis=1)
    return [jnp.asarray(np.tile(t, (1, 2)), F32) for t in (c, sa, sb)]


def _rope_fwd(x, c, sa, sb):
    return x * c + pltpu.roll(x, PAIR_W - 8, 1) * sa + pltpu.roll(x, 8, 1) * sb


def _rope_bwd(dy, c, sa, sb):
    return dy * c + pltpu.roll(dy * sb, PAIR_W - 8, 1) + pltpu.roll(dy * sa, 8, 1)


def _band_masks():
    row = lax.broadcasted_iota(jnp.int32, (ATTN_BLOCK, ATTN_BLOCK), 0)
    col = lax.broadcasted_iota(jnp.int32, (ATTN_BLOCK, ATTN_BLOCK), 1)
    return col <= row, col >= row


def _stack_rows(t):
    return jnp.concatenate([t, t], axis=0)


def _stack_heads(t, first_head):
    return jnp.concatenate([jnp.where(first_head, t, 0), jnp.where(first_head, 0, t)], axis=0)


def _per_head(fn):
    return jnp.concatenate([fn(slice(h * HEAD_DIM, (h + 1) * HEAD_DIM)) for h in range(2)], axis=1)


def _slab_spec(kind):
    return pl.BlockSpec((None, SEQ, PAIR_W), lambda b, p, g: (b, 0, p * 3 * N_GROUPS + g * 3 + kind))


_TABLE_SPEC = pl.BlockSpec((SEQ, PAIR_W), lambda b, p, g: (0, 0))
_PAIR_SPEC = pl.BlockSpec((None, SEQ, PAIR_W), lambda b, p, g: (b, 0, p))


def _block_rows(dil, r, n):
    return pl.ds(n * (ATTN_BLOCK * dil) + r, ATTN_BLOCK, stride=dil)


def proj_qkv(h, w_qkv_t, tables, comm=None):
    tm = 1024
    pair_w = QKV_W // N_PAIRS
    scale = HEAD_DIM ** -0.5

    def body(h_ref, w_ref, c_ref, sa_ref, sb_ref, o_ref):
        rows = _dot(h_ref[...], w_ref[...], "nt")
        c, sa, sb = c_ref[...], sa_ref[...], sb_ref[...]
        for blk in range(pair_w // PAIR_W):
            cols = slice(blk * PAIR_W, (blk + 1) * PAIR_W)
            x = rows[:, cols]
            if blk % 3 == 0:
                x = _rope_fwd(x, c, sa, sb) * scale
            elif blk % 3 == 1:
                x = _rope_fwd(x, c, sa, sb)
            o_ref[:, cols] = x

    table = pl.BlockSpec((tm, PAIR_W), lambda i, j, : (i % (SEQ // tm), 0))
    res = hosted_call(
        body, comm, "proj_qkv", (TOKENS // tm, N_PAIRS),
        [pl.BlockSpec((tm, D_MODEL), lambda i, j: (i, 0)), pl.BlockSpec((pair_w, D_MODEL), lambda i, j: (j, 0)),
         table, table, table],
        [pl.BlockSpec((tm, pair_w), lambda i, j: (i, j))], [jax.ShapeDtypeStruct((TOKENS, QKV_W), F32)], [],
        (h, w_qkv_t, *tables), ("parallel", "parallel"))
    return res[0] if comm is None else res


def attn_fwd(qkv, comm=None):
    def body(qs, ks, v_ref, attn_b_ref, attn_ref, lse_ref, o0, o1, o2, l0, l1, l2):
        g = pl.program_id(2)
        cur_mask, prev_mask = _band_masks()
        first_head = lax.broadcasted_iota(jnp.int32, (ATTN_BLOCK, PAIR_W), 1) < HEAD_DIM

        def run(dil, o_slab, l_slab):
            nb = SEQ // dil // ATTN_BLOCK

            def block(idx, carry):
                r, n = lax.div(idx, nb), lax.rem(idx, nb)
                cur, prev = _block_rows(dil, r, n), _block_rows(dil, r, jnp.maximum(n - 1, 0))
                q = qs[cur, :].astype(BF16)
                kc, kp = ks[cur, :].astype(BF16), ks[prev, :].astype(BF16)
                vc, vp = v_ref[cur, :].astype(BF16), v_ref[prev, :].astype(BF16)
                q2 = _stack_heads(q, first_head)
                mask = _stack_rows(jnp.concatenate([jnp.logical_and(prev_mask, n > 0), cur_mask], axis=1))
                s2 = jnp.where(mask, _dot(q2, jnp.concatenate([kp, kc], axis=0), "nt"), NEG_INF)
                m = jnp.max(s2, axis=-1, keepdims=True)
                vcat, two = jnp.concatenate([vp, vc], axis=0), _stack_rows(first_head)
                vext = jnp.concatenate([jnp.where(two, vcat, 1), jnp.where(two, 1, vcat)], axis=1)
                r2 = _dot(jnp.exp(s2 - m).astype(BF16), vext, "nn")
                r0, r1 = r2[:ATTN_BLOCK, :PAIR_W], r2[ATTN_BLOCK:, PAIR_W:]
                num = jnp.where(first_head, r0, r1)
                den = pltpu.roll(jnp.where(first_head, r1, r0), HEAD_DIM, 1)
                o_slab[cur, :] = num / den
                l_slab[cur, :] = jnp.where(first_head, m[:ATTN_BLOCK], m[ATTN_BLOCK:]) + jnp.log(den)
                return carry

            lax.fori_loop(0, SEQ // ATTN_BLOCK, block, 0, unroll=4)

        for gi, (o_slab, l_slab) in enumerate(((o0, l0), (o1, l1), (o2, l2))):
            @pl.when(g == gi)
            def _(gi=gi, o_slab=o_slab, l_slab=l_slab):
                run(DILATIONS[gi], o_slab, l_slab)

        @pl.when(g == N_GROUPS - 1)
        def _():
            a, b, cc = l0[...], l1[...], l2[...]
            m = jnp.maximum(jnp.maximum(a, b), cc)
            e0, e1, e2 = jnp.exp(a - m), jnp.exp(b - m), jnp.exp(cc - m)
            tot = e0 + e1 + e2
            attn = (e0 * o0[...] + e1 * o1[...] + e2 * o2[...]) / tot
            attn_ref[...] = attn
            attn_b_ref[...] = attn.astype(BF16)
            lse_ref[...] = m + jnp.log(tot)

    shape = (LOCAL_BATCH, SEQ, GROUP_W)
    slab = pltpu.VMEM((SEQ, PAIR_W), F32)
    return hosted_call(
        body, comm, "attn_fwd", (LOCAL_BATCH, N_PAIRS, N_GROUPS),
        [_slab_spec(0), _slab_spec(1), _slab_spec(2)], [_PAIR_SPEC] * 3,
        [jax.ShapeDtypeStruct(shape, BF16), jax.ShapeDtypeStruct(shape, F32), jax.ShapeDtypeStruct(shape, F32)],
        [slab] * 6, (qkv, qkv, qkv), ("parallel", "parallel", "arbitrary"))


def attn_bwd(qkv, tables, dattn, attn, lse, comm=None):
    scale = HEAD_DIM ** -0.5

    def body(qs, ks, v_ref, c_ref, sa_ref, sb_ref, do_ref, out_ref, lse_ref, dqkv_ref, dl, dq_s, dk_s, dv_s):
        g = pl.program_id(2)
        c, sa, sb = c_ref[...], sa_ref[...], sb_ref[...]

        @pl.when(g == 0)
        def _():
            prod = do_ref[...] * out_ref[...]
            dl[...] = _per_head(
                lambda sl: jnp.broadcast_to(jnp.sum(prod[:, sl], axis=-1, keepdims=True), (SEQ, HEAD_DIM)))

        cur_mask, prev_mask = _band_masks()
        first_head = lax.broadcasted_iota(jnp.int32, (ATTN_BLOCK, PAIR_W), 1) < HEAD_DIM

        def run(dil):
            nb = SEQ // dil // ATTN_BLOCK

            def block(idx, carry):
                r, n = lax.div(idx, nb), lax.rem(idx, nb)
                cur = _block_rows(dil, r, n)
                prev = _block_rows(dil, r, jnp.maximum(n - 1, 0))
                nxt = _block_rows(dil, r, jnp.minimum(n + 1, nb - 1))
                q0, q1 = qs[cur, :].astype(BF16), qs[nxt, :].astype(BF16)
                kp, kc = ks[prev, :].astype(BF16), ks[cur, :].astype(BF16)
                vp, vc = v_ref[prev, :].astype(BF16), v_ref[cur, :].astype(BF16)
                do0, do1 = do_ref[cur, :].astype(BF16), do_ref[nxt, :].astype(BF16)
                lse0, lse1, dl0, dl1 = lse_ref[cur, :], lse_ref[nxt, :], dl[cur, :], dl[nxt, :]
                has_prev = jnp.logical_and(prev_mask, n > 0)
                has_next = jnp.logical_and(prev_mask, n < nb - 1)

                def per_row(t):
                    return jnp.concatenate([t[:, 0:1], t[:, HEAD_DIM:HEAD_DIM + 1]], axis=0)

                q20, q21 = _stack_heads(q0, first_head), _stack_heads(q1, first_head)
                do20, do21 = _stack_heads(do0, first_head), _stack_heads(do1, first_head)
                kcat, vcat = jnp.concatenate([kp, kc], axis=0), jnp.concatenate([vp, vc], axis=0)
                mask0 = _stack_rows(jnp.concatenate([has_prev, cur_mask], axis=1))
                p0 = jnp.where(mask0, jnp.exp(_dot(q20, kcat, "nt") - per_row(lse0)), 0.0)
                ds0 = (p0 * (_dot(do20, vcat, "nt") - per_row(dl0))).astype(BF16)
                p1 = jnp.where(_stack_rows(has_next), jnp.exp(_dot(q21, kc, "nt") - per_row(lse1)), 0.0)
                ds1 = (p1 * (_dot(do21, vc, "nt") - per_row(dl1))).astype(BF16)
                dq2 = _dot(ds0, kcat, "nn")
                dq_s[cur, :] = jnp.where(first_head, dq2[:ATTN_BLOCK], dq2[ATTN_BLOCK:])
                ds_cur = jnp.concatenate([ds0[:, ATTN_BLOCK:], ds1], axis=0)
                p_cur = jnp.concatenate([p0[:, ATTN_BLOCK:], p1], axis=0).astype(BF16)
                dk_s[cur, :] = _dot(ds_cur, jnp.concatenate([q20, q21], axis=0), "tn")
                dv_s[cur, :] = _dot(p_cur, jnp.concatenate([do20, do21], axis=0), "tn")
                return carry

            lax.fori_loop(0, SEQ // ATTN_BLOCK, block, 0, unroll=2)

        for gi in range(N_GROUPS):
            @pl.when(g == gi)
            def _(gi=gi):
                run(DILATIONS[gi])

        dqkv_ref[:, 0:PAIR_W] = _rope_bwd(dq_s[...] * scale, c, sa, sb).astype(BF16)
        dqkv_ref[:, PAIR_W:2 * PAIR_W] = _rope_bwd(dk_s[...], c, sa, sb).astype(BF16)
        dqkv_ref[:, 2 * PAIR_W:] = dv_s[...].astype(BF16)

    slab = pltpu.VMEM((SEQ, PAIR_W), F32)
    return hosted_call(
        body, comm, "attn_bwd", (LOCAL_BATCH, N_PAIRS, N_GROUPS),
        [_slab_spec(0), _slab_spec(1), _slab_spec(2), _TABLE_SPEC, _TABLE_SPEC, _TABLE_SPEC,
         _PAIR_SPEC, _PAIR_SPEC, _PAIR_SPEC],
        [pl.BlockSpec((None, SEQ, 3 * PAIR_W), lambda b, p, g: (b, 0, p * N_GROUPS + g))],
        [jax.ShapeDtypeStruct((LOCAL_BATCH, SEQ, QKV_W), BF16)],
        [slab] * 4, (qkv, qkv, qkv, *tables, dattn, attn, lse), ("parallel", "parallel", "arbitrary"))


def _discretize(lr, li, log_dt, br, bi):
    dt = jnp.exp(log_dt)
    mag = jnp.exp(lr * dt)
    ab_re, ab_im = mag * jnp.cos(li * dt), mag * jnp.sin(li * dt)
    den = lr * lr + li * li
    nr, ni = ab_re - 1.0, ab_im
    f_re = (nr * lr + ni * li) / den
    f_im = (ni * lr - nr * li) / den
    return ab_re, ab_im, f_re[None] * br - f_im[None] * bi, f_re[None] * bi + f_im[None] * br


def ssm_prep(lr, li, log_dt, br, bi):
    def body(lr_ref, li_ref, dt_ref, br_ref, bi_ref, *outs):
        for o, v in zip(outs, _discretize(lr_ref[...], li_ref[...], dt_ref[...], br_ref[...], bi_ref[...])):
            o[...] = v
    shapes = [lr, li, br, bi]
    return pl.pallas_call(body, name="ssm_prep",
                          out_shape=[jax.ShapeDtypeStruct(s.shape, F32) for s in shapes])(lr, li, log_dt, br, bi)


def ssm_prep_bwd(lr, li, log_dt, br, bi, g_ab_re, g_ab_im, g_bb_re, g_bb_im):
    def body(lr_ref, li_ref, dt_ref, br_ref, bi_ref, g0, g1, g2, g3, *outs):
        _, vjp = jax.vjp(_discretize, lr_ref[...], li_ref[...], dt_ref[...], br_ref[...], bi_ref[...])
        for o, v in zip(outs, vjp((g0[...], g1[...], g2[...], g3[...]))):
            o[...] = v
    shapes = [lr, li, log_dt, br, bi]
    return pl.pallas_call(body, name="ssm_prep_bwd",
                          out_shape=[jax.ShapeDtypeStruct(s.shape, F32) for s in shapes])(
        lr, li, log_dt, br, bi, g_ab_re, g_ab_im, g_bb_re, g_bb_im)


def _block_diag(t):
    per = SSM_STATE_W // SSM_LANE_BLOCKS // 64
    g = t.transpose(1, 0, 2).reshape(SSM_LANE_BLOCKS, per, 16, 64)
    eye = jnp.eye(per, dtype=t.dtype)
    return jnp.einsum("jgcn,gh->jgchn", g, eye).reshape(SSM_LANE_BLOCKS, per * 16, per * 64)


def _block_diag_t(m):
    per = SSM_STATE_W // SSM_LANE_BLOCKS // 64
    m5 = m.reshape(SSM_LANE_BLOCKS, per, 16, per, 64)
    d = jnp.einsum("jgchn,gh->jgcn", m5, jnp.eye(per, dtype=m.dtype))
    return d.reshape(SSM_LANE_BLOCKS * per, 16, 64).transpose(1, 0, 2)


def _cmul(ar, ai, br, bi):
    return ar * br - ai * bi, ar * bi + ai * br


def _power_tables(ar, ai, reverse):
    width = ar.shape[1]
    row = lax.broadcasted_iota(jnp.int32, (8, width), 0)
    pows = [(ar, ai)]
    for _ in range(7):
        pows.append(_cmul(pows[-1][0], pows[-1][1], ar, ai))
    steps = []
    for k in (1, 2, 4):
        keep = (row >= k) if not reverse else (row < 8 - k)
        steps.append((jnp.where(keep, pows[k - 1][0], 0.0), jnp.where(keep, pows[k - 1][1], 0.0)))
    cr = jnp.zeros((8, width), F32)
    ci = jnp.zeros((8, width), F32)
    for i in range(8):
        pr, pi = pows[i] if not reverse else pows[7 - i]
        cr = jnp.where(row == i, pr, cr)
        ci = jnp.where(row == i, pi, ci)
    return steps, (cr, ci)


SCAN_CHUNK = 2048
STATE_BLOCK = SSM_STATE_W // SSM_LANE_BLOCKS
CHAN_BLOCK = SSM_W // SSM_LANE_BLOCKS


def ssm_fwd(u, ab_re, ab_im, bb_re, bb_im, cb_re, cb_im, d_skip, comm=None):
    nt = SEQ // SCAN_CHUNK
    chan = pl.BlockSpec((None, SCAN_CHUNK, CHAN_BLOCK), lambda b, j, t: (b, t, j))
    state = pl.BlockSpec((None, SCAN_CHUNK, STATE_BLOCK), lambda b, j, t: (b, t, j))
    mat = pl.BlockSpec((None, CHAN_BLOCK, STATE_BLOCK), lambda b, j, t: (j, 0, 0))
    lane = pl.BlockSpec((1, STATE_BLOCK), lambda b, j, t: (0, j))
    dsp = pl.BlockSpec((1, CHAN_BLOCK), lambda b, j, t: (0, j))

    def body(u_ref, ar_ref, ai_ref, bbr_ref, bbi_ref, cbr_ref, cbi_ref, d_ref, y_ref, yg_ref, xr_ref, xi_ref,
             car_r, car_i):
        @pl.when(pl.program_id(2) == 0)
        def _():
            car_r[...] = jnp.zeros_like(car_r)
            car_i[...] = jnp.zeros_like(car_i)

        steps, (pr, pi) = _power_tables(ar_ref[...], ai_ref[...], reverse=False)
        uf = u_ref[...]
        ub = uf.astype(BF16)
        xr_ref[...] = _dot(ub, bbr_ref[...], "nn")
        xi_ref[...] = _dot(ub, bbi_ref[...], "nn")

        def tile(i, carry):
            cr, ci = carry
            sl = pl.ds(pl.multiple_of(i * 8, 8), 8)
            br, bi = xr_ref[sl, :], xi_ref[sl, :]
            for k, (sr, si) in zip((1, 2, 4), steps):
                tr, ti = _cmul(sr, si, pltpu.roll(br, k, 0), pltpu.roll(bi, k, 0))
                br, bi = br + tr, bi + ti
            tr, ti = _cmul(pr, pi, cr, ci)
            br, bi = br + tr, bi + ti
            xr_ref[sl, :] = br
            xi_ref[sl, :] = bi
            return br[7:8, :], bi[7:8, :]

        cr, ci = lax.fori_loop(0, SCAN_CHUNK // 8, tile, (car_r[0:1, :], car_i[0:1, :]), unroll=4)
        car_r[0:1, :] = cr
        car_i[0:1, :] = ci
        y = (_dot(xr_ref[...].astype(BF16), cbr_ref[...], "nt") - _dot(xi_ref[...].astype(BF16), cbi_ref[...], "nt")
             + d_ref[...] * uf)
        y_ref[...] = y
        yg_ref[...] = jax.nn.gelu(y).astype(BF16)

    return hosted_call(
        body, comm, "ssm_fwd", (LOCAL_BATCH, SSM_LANE_BLOCKS, nt),
        [chan, lane, lane, mat, mat, mat, mat, dsp], [chan, chan, state, state],
        [jax.ShapeDtypeStruct((LOCAL_BATCH, SEQ, SSM_W), F32), jax.ShapeDtypeStruct((LOCAL_BATCH, SEQ, SSM_W), BF16),
         jax.ShapeDtypeStruct((LOCAL_BATCH, SEQ, SSM_STATE_W), F32),
         jax.ShapeDtypeStruct((LOCAL_BATCH, SEQ, SSM_STATE_W), F32)],
        [pltpu.VMEM((8, STATE_BLOCK), F32), pltpu.VMEM((8, STATE_BLOCK), F32)],
        (u, ab_re, ab_im, bb_re, bb_im, cb_re, cb_im, d_skip), ("parallel", "parallel", "arbitrary"))


def ssm_bwd(dyg, y, u, xr, xi, ab_re, ab_im, bb_re, bb_im, cb_re, cb_im, d_skip, comm=None):
    nt = SEQ // SCAN_CHUNK
    ntile = SCAN_CHUNK // 8

    def rev(t):
        return nt - 1 - t

    chan = pl.BlockSpec((None, SCAN_CHUNK, CHAN_BLOCK), lambda j, b, t: (b, rev(t), j))
    state = pl.BlockSpec((None, SCAN_CHUNK, STATE_BLOCK), lambda j, b, t: (b, rev(t), j))
    before = pl.BlockSpec((None, 8, STATE_BLOCK), lambda j, b, t: (b, jnp.maximum(rev(t) * ntile - 1, 0), j))
    mat = pl.BlockSpec((None, CHAN_BLOCK, STATE_BLOCK), lambda j, b, t: (j, 0, 0))
    lane = pl.BlockSpec((1, STATE_BLOCK), lambda j, b, t: (0, j))
    lane8 = pl.BlockSpec((8, STATE_BLOCK), lambda j, b, t: (0, j))
    dsp = pl.BlockSpec((1, CHAN_BLOCK), lambda j, b, t: (0, j))

    def body(dyg_ref, y_ref, u_ref, xr_ref, xi_ref, xrb_ref, xib_ref, ar_ref, ai_ref, bbr_ref, bbi_ref, cbr_ref,
             cbi_ref, d_ref, du_ref, dcbr_ref, dcbi_ref, dbbr_ref, dbbi_ref, dd_ref, dar_ref, dai_ref,
             lam_r, lam_i, car_r, car_i):
        b, t = pl.program_id(1), pl.program_id(2)
        first = jnp.logical_and(b == 0, t == 0)

        @pl.when(t == 0)
        def _():
            car_r[...] = jnp.zeros_like(car_r)
            car_i[...] = jnp.zeros_like(car_i)

        @pl.when(first)
        def _():
            for r in (dcbr_ref, dcbi_ref, dbbr_ref, dbbi_ref, dd_ref, dar_ref, dai_ref):
                r[...] = jnp.zeros_like(r)

        steps, (pr, pi) = _power_tables(ar_ref[...], -ai_ref[...], reverse=True)
        uf = u_ref[...]
        _, gelu_vjp = jax.vjp(jax.nn.gelu, y_ref[...])
        dy = gelu_vjp(dyg_ref[...])[0]
        dyb = dy.astype(BF16)
        dd_ref[...] += _colsum(dy * uf)
        lam_r[...] = _dot(dyb, cbr_ref[...], "nn")
        lam_i[...] = -_dot(dyb, cbi_ref[...], "nn")
        dcbr_ref[...] += _dot(dyb, xr_ref[...].astype(BF16), "tn")
        dcbi_ref[...] -= _dot(dyb, xi_ref[...].astype(BF16), "tn")
        row0 = lax.broadcasted_iota(jnp.int32, (8, STATE_BLOCK), 0) == 0
        has_before = rev(t) > 0
        xrb = jnp.where(has_before, xrb_ref[...], 0.0)
        xib = jnp.where(has_before, xib_ref[...], 0.0)

        def tile(s, carry):
            cr, ci, acc_r, acc_i = carry
            i = ntile - 1 - s
            sl = pl.ds(pl.multiple_of(i * 8, 8), 8)
            gr, gi = lam_r[sl, :], lam_i[sl, :]
            for k, (sr, si) in zip((1, 2, 4), steps):
                tr, ti = _cmul(sr, si, pltpu.roll(gr, 8 - k, 0), pltpu.roll(gi, 8 - k, 0))
                gr, gi = gr + tr, gi + ti
            tr, ti = _cmul(pr, pi, cr, ci)
            gr, gi = gr + tr, gi + ti
            lam_r[sl, :] = gr
            lam_i[sl, :] = gi
            sp = pl.ds(pl.multiple_of(jnp.maximum(i - 1, 0) * 8, 8), 8)
            pvr = jnp.where(i > 0, xr_ref[sp, :], xrb)
            pvi = jnp.where(i > 0, xi_ref[sp, :], xib)
            xsr = jnp.where(row0, pltpu.roll(pvr, 1, 0), pltpu.roll(xr_ref[sl, :], 1, 0))
            xsi = jnp.where(row0, pltpu.roll(pvi, 1, 0), pltpu.roll(xi_ref[sl, :], 1, 0))
            acc_r = acc_r + xsr * gr + xsi * gi
            acc_i = acc_i + xsr * gi - xsi * gr
            return gr[0:1, :], gi[0:1, :], acc_r, acc_i

        zero = jnp.zeros((8, STATE_BLOCK), F32)
        cr, ci, acc_r, acc_i = lax.fori_loop(0, ntile, tile, (car_r[0:1, :], car_i[0:1, :], zero, zero), unroll=2)
        car_r[0:1, :] = cr
        car_i[0:1, :] = ci
        dar_ref[...] += acc_r
        dai_ref[...] += acc_i
        lrb, lib = lam_r[...].astype(BF16), lam_i[...].astype(BF16)
        du = _dot(lrb, bbr_ref[...], "nt") + _dot(lib, bbi_ref[...], "nt") + d_ref[...] * dy
        du_ref[...] = du.astype(BF16)
        ub = uf.astype(BF16)
        dbbr_ref[...] += _dot(ub, lrb, "tn")
        dbbi_ref[...] += _dot(ub, lib, "tn")

    mat_shape = jax.ShapeDtypeStruct((SSM_LANE_BLOCKS, CHAN_BLOCK, STATE_BLOCK), F32)
    return hosted_call(
        body, comm, "ssm_bwd", (SSM_LANE_BLOCKS, LOCAL_BATCH, nt),
        [chan, chan, chan, state, state, before, before, lane, lane, mat, mat, mat, mat, dsp],
        [chan, mat, mat, mat, mat, dsp, lane8, lane8],
        [jax.ShapeDtypeStruct((LOCAL_BATCH, SEQ, SSM_W), BF16), mat_shape, mat_shape, mat_shape, mat_shape,
         jax.ShapeDtypeStruct((1, SSM_W), F32), jax.ShapeDtypeStruct((8, SSM_STATE_W), F32),
         jax.ShapeDtypeStruct((8, SSM_STATE_W), F32)],
        [pltpu.VMEM((SCAN_CHUNK, STATE_BLOCK), F32), pltpu.VMEM((SCAN_CHUNK, STATE_BLOCK), F32),
         pltpu.VMEM((8, STATE_BLOCK), F32), pltpu.VMEM((8, STATE_BLOCK), F32)],
        (dyg, y, u, xr, xi, xr, xi, ab_re, ab_im, bb_re, bb_im, cb_re, cb_im, d_skip),
        ("parallel", "arbitrary", "arbitrary"))


def _merge_fn(g0, g1, attn_d, za, zb):
    return jax.nn.sigmoid(g0) * attn_d + jax.nn.sigmoid(g1) * (za * jax.nn.sigmoid(zb))


def _swiglu_fn(a, b):
    return jax.nn.silu(a) * b


def _reduce_start(names, gw, shard_shapes):
    return swap_comm([_to_slots(n, gw[n], shard_shapes[n]) for n in names])


def _reduce_chip(names, swap, got, core):
    return exchange_comm([add_halves(n, g, r, core) for n, g, r in zip(names, swap.ins, got)])


def local_step(x, target, shards, small, core):
    g_mix, g_ffn, g_final = small["norm_mix_g"], small["norm_ffn_g"], small["norm_final_g"]
    tables = _rope_tables()
    seqs = lambda t: t.reshape(LOCAL_BATCH, SEQ, t.shape[-1])
    toks = lambda t: t.reshape(TOKENS, t.shape[-1])
    shard_shapes = {n: s.shape for n, s in shards.items()}
    w = {}

    def gather(names):
        return gather_comm([shards[n] for n in names])

    def arrived(names, slots):
        for n, s in zip(names, slots):
            w[n] = _from_slots(n, s)

    h, *slots = rowwise(lambda xv, g: (_rms(xv, g),), [x, g_mix], [(D_MODEL, BF16)], "norm_mix", comm=gather(["w_in"]))
    arrived(["w_in"], slots)
    w_qkv, w_u, w_gate = _qkv_order(w["w_in"][:QKV_W]), w["w_in"][QKV_W:QKV_W + SSM_W], w["w_in"][QKV_W + SSM_W:]
    qkv, *slots = proj_qkv(h, w_qkv, tables, comm=gather(["w_attn_out", "w_glu"]))
    arrived(["w_attn_out", "w_glu"], slots)
    qkv = seqs(qkv)
    u = seqs(matmul(h, w_u, "nt", F32, "proj_u"))
    gl, *slots = matmul(h, w_gate, "nt", BF16, "proj_gate", comm=gather(["w_out"]))
    arrived(["w_out"], slots)
    attn_b, attn, lse, *slots = attn_fwd(qkv, comm=gather(["w_ffn_gate"]))
    arrived(["w_ffn_gate"], slots)
    attn_b = toks(attn_b)
    attn_d = matmul(attn_b, w["w_attn_out"], "nn", F32, "attn_out")

    br_t = small["ssm_b_re"].transpose(2, 0, 1)
    bi_t = small["ssm_b_im"].transpose(2, 0, 1)
    log_dt = small["ssm_log_dt"].reshape(32, 1)
    ab_re, ab_im, bb_re_t, bb_im_t = ssm_prep(small["ssm_a_re"], small["ssm_a_im"], log_dt, br_t, bi_t)
    ab = [ab_re.reshape(1, SSM_STATE_W), ab_im.reshape(1, SSM_STATE_W)]
    bb = [_block_diag(bb_re_t).astype(BF16), _block_diag(bb_im_t).astype(BF16)]
    cb = [_block_diag(small["ssm_c_re"].transpose(1, 0, 2)).astype(BF16),
          _block_diag(small["ssm_c_im"].transpose(1, 0, 2)).astype(BF16)]
    d_skip = small["ssm_d"].reshape(1, SSM_W)
    y, yg, xr, xi, *slots = ssm_fwd(u, *ab, *bb, *cb, d_skip, comm=gather(["w_ffn_up"]))
    arrived(["w_ffn_up"], slots)
    yg2 = toks(yg)
    z = matmul(yg2, w["w_glu"], "nn", BF16, "glu")
    gate_ins = [(gl, D_MODEL, 0), (gl, D_MODEL, 1), attn_d, (z, D_MODEL, 0), (z, D_MODEL, 1)]
    (merged,) = rowwise(lambda *v: (_merge_fn(*[t.astype(F32) for t in v]),), gate_ins, [(D_MODEL, BF16)], "merge")
    x1, h2 = matmul_rows(merged, w["w_out"], "out_proj", lambda rows, g: (rows, _rms(rows, g)), [g_ffn],
                         [(D_MODEL, F32), (D_MODEL, BF16)], add=x)
    a, b, act, *slots = ffn_in(h2, w["w_ffn_gate"], w["w_ffn_up"], comm=gather(["w_ffn_down"]))
    arrived(["w_ffn_down"], slots)

    def final_fn(xv, g, tgt):
        yv, vjp = jax.vjp(_rms, xv, g)
        err = yv - tgt
        dx, dg = vjp(err * (1.0 / D_MODEL))
        loss = 0.5 * jnp.sum(jnp.mean(err * err, axis=-1, keepdims=True), axis=0, keepdims=True)
        return dx, dx, dg, jnp.broadcast_to(loss, (1, LANES))

    dx2, dx2_b, dg_final, loss = matmul_rows(act, w["w_ffn_down"], "ffn_down_loss", final_fn, [g_final, target],
                                             [(D_MODEL, F32), (D_MODEL, BF16)], accs=(D_MODEL, LANES), add=x1)
    gw, parts = {}, {}
    gw["w_ffn_down"] = matmul(act, dx2_b, "tn", F32, "d_ffn_down")
    da_b, db_b = ffn_in_bwd(dx2_b, w["w_ffn_down"], a, b)
    gw["w_ffn_gate"] = matmul(da_b, h2, "tn", F32, "d_ffn_gate")
    gw["w_ffn_up"] = matmul(db_b, h2, "tn", F32, "d_ffn_up")
    ffn = ["w_ffn_down", "w_ffn_gate", "w_ffn_up"]
    swap = _reduce_start(ffn[:2], gw, shard_shapes)
    dh2, *got = matmul(da_b, w["w_ffn_gate"], "nn", F32, "d_h2_gate", comm=swap)
    ffn_exchange = [_reduce_chip(ffn[:2], swap, got, core)]
    swap = _reduce_start(ffn[2:], gw, shard_shapes)

    def norm_bwd(dh, xv, g, skip):
        _, vjp = jax.vjp(_rms, xv, g)
        dx, dg = vjp(dh)
        dx = dx + skip
        return dx, dx, dg

    dx1, dx1_b, dg_ffn, *got = matmul_rows(db_b, w["w_ffn_up"], "d_h2_up_norm", norm_bwd, [x1, g_ffn, dx2],
                                           [(D_MODEL, F32), (D_MODEL, BF16)], accs=(D_MODEL,), add=dh2, comm=swap)
    ffn_up_exchange = _reduce_chip(ffn[2:], swap, got, core)
    gw["w_out"] = matmul(merged, dx1_b, "tn", F32, "d_out")
    dmerged = matmul(dx1_b, w["w_out"], "nt", F32, "d_merged")

    def merge_bwd(g0, g1, ad, za, zb, dm):
        _, vjp = jax.vjp(_merge_fn, *[t.astype(F32) for t in (g0, g1, ad, za, zb)])
        dg0, dg1, dad, dza, dzb = vjp(dm)
        return jnp.concatenate([dg0, dg1], axis=1), dad, jnp.concatenate([dza, dzb], axis=1)

    dgl_b, dattn_d_b, dz_b, parts["w_ffn_up"] = rowwise(
        merge_bwd, gate_ins + [dmerged], [(GATE_W, BF16), (D_MODEL, BF16), (GATE_W, BF16)], "merge_bwd",
        comm=ffn_up_exchange)
    gw["w_attn_out"] = matmul(attn_b, dattn_d_b, "tn", F32, "d_attn_out")
    dattn = seqs(matmul(dattn_d_b, w["w_attn_out"], "nt", F32, "d_attn"))
    gw["w_glu"] = matmul(yg2, dz_b, "tn", F32, "d_glu")
    dyg = seqs(matmul(dz_b, w["w_glu"], "nt", F32, "d_yg"))
    mixer = ["w_out", "w_attn_out", "w_glu"]
    swap = _reduce_start(mixer, gw, shard_shapes)
    du_b, dcb_re, dcb_im, dbb_re, dbb_im, dd, da_re8, da_im8, *rest = ssm_bwd(
        dyg, y, u, xr, xi, *ab, *bb, *cb, d_skip, comm=join_comms(ffn_exchange + [swap]))
    for n, p in zip(ffn[:2], rest[:2]):
        parts[n] = p
    mixer_exchange = _reduce_chip(mixer, swap, rest[2:], core)
    du_b = toks(du_b)
    g_ab_re = jnp.sum(da_re8, axis=0).reshape(32, 64)
    g_ab_im = jnp.sum(da_im8, axis=0).reshape(32, 64)
    d_lr, d_li, d_ldt, d_br_t, d_bi_t = ssm_prep_bwd(
        small["ssm_a_re"], small["ssm_a_im"], log_dt, br_t, bi_t,
        g_ab_re, g_ab_im, _block_diag_t(dbb_re), _block_diag_t(dbb_im))
    as_gcn = lambda t: t.transpose(1, 0, 2).reshape(SSM_W, 64)
    gs = {
        "ssm_a_re": d_lr, "ssm_a_im": d_li, "ssm_log_dt": d_ldt.reshape(1, 32),
        "ssm_b_re": as_gcn(d_br_t), "ssm_b_im": as_gcn(d_bi_t),
        "ssm_c_re": as_gcn(_block_diag_t(dcb_re)), "ssm_c_im": as_gcn(_block_diag_t(dcb_im)),
        "ssm_d": dd.reshape(32, 16).T,
    }
    ssm_gather = small_comm([gs[n] for n in SSM_SMALL])
    dqkv_b, *rest = attn_bwd(qkv, tables, dattn, attn, lse, comm=join_comms([mixer_exchange, ssm_gather]))
    for n, p in zip(mixer, rest):
        parts[n] = p
    ssm_shares = rest[len(mixer):]
    dqkv_b = toks(dqkv_b)
    d_qkv = matmul(dqkv_b, h, "tn", F32, "d_w_qkv")
    d_u = matmul(du_b, h, "tn", F32, "d_w_u")
    d_gate = matmul(dgl_b, h, "tn", F32, "d_w_gate")
    gw["w_in"] = jnp.concatenate([_qkv_order(d_qkv, back=True), d_u, d_gate], axis=0)
    swap = _reduce_start(["w_in"], gw, shard_shapes)
    dh, *got = matmul(dqkv_b, w_qkv, "nn", F32, "d_h_qkv", comm=swap)
    w_in_exchange = _reduce_chip(["w_in"], swap, got, core)
    grad_x, dg_mix, parts["w_in"] = mix_in_bwd([du_b, dgl_b], [w_u, w_gate], dh, x, g_mix, dx1, comm=w_in_exchange)
    gs_norm = {"norm_mix_g": dg_mix, "norm_ffn_g": dg_ffn, "norm_final_g": dg_final}
    return loss, grad_x, parts, ssm_shares, gs_norm


ANY = pl.BlockSpec(memory_space=pl.ANY)
BIG = ("w_in", "w_glu", "w_attn_out", "w_out", "w_ffn_gate", "w_ffn_up", "w_ffn_down")
TRANSPOSED = ("w_in", "w_ffn_gate", "w_ffn_up")
ROW_SHARDED = TRANSPOSED + ("w_out", "w_ffn_down")
SMALL = ("norm_mix_g", "ssm_a_re", "ssm_a_im", "ssm_log_dt", "ssm_b_re", "ssm_b_im", "ssm_c_re", "ssm_c_im",
         "ssm_d", "norm_ffn_g", "norm_final_g")
WEIGHTS = ("norm_mix_g", "w_in", "ssm_a_re", "ssm_a_im", "ssm_log_dt", "ssm_b_re", "ssm_b_im", "ssm_c_re",
           "ssm_c_im", "ssm_d", "w_glu", "w_attn_out", "w_out", "norm_ffn_g", "w_ffn_gate", "w_ffn_up",
           "w_ffn_down", "norm_final_g")
SSM_SMALL = SMALL[1:9]
NORM_SMALL = (SMALL[0],) + SMALL[9:]
NORM_ROWS = 32
N_BIG = len(BIG)


def _position():
    return lax.axis_index("x"), lax.axis_index("y"), lax.axis_index("c")


def _other_chips(x, y):
    return [(1 - x, y), (x, 1 - y), (1 - x, 1 - y)]


def _remote(src, dst, send_sem, recv_sem, device):
    return pltpu.make_async_remote_copy(src_ref=src, dst_ref=dst, send_sem=send_sem, recv_sem=recv_sem,
                                        device_id=device, device_id_type=MESH)


_later = functools.partial


def _two_level_phases(copies):
    def first(*refs):
        locals_, sends, _, _, _ = copies(*refs)
        for cp in locals_ + sends:
            cp().start()

    def mid(*refs):
        _, _, arrived, passed, _ = copies(*refs)
        for got, cp in zip(arrived, passed):
            got().wait_recv()
            cp().start()

    def last(*refs):
        locals_, sends, _, passed, from_sibling = copies(*refs)
        for cp in from_sibling:
            cp().wait_recv()
        for cp in sends + passed:
            cp().wait_send()
        for cp in locals_:
            cp().wait()

    return first, mid, last


def _half(ref, chip, which):
    rows = ref.shape[1] // 2
    return ref.at[chip, pl.ds(which * rows, rows), :]


class Comm:
    def __init__(self, ins, out_shapes, sems, first, mid, last):
        self.ins, self.out_shapes, self.sems = list(ins), list(out_shapes), list(sems)
        self.first, self.mid, self.last = first, mid, last


def join_comms(comms):
    def cut(refs_by_kind):
        offs, parts = [0, 0, 0], []
        for cm in comms:
            sizes = (len(cm.ins), len(cm.out_shapes), len(cm.sems))
            parts.append(tuple(refs_by_kind[k][offs[k]:offs[k] + sizes[k]] for k in range(3)))
            offs = [o + s for o, s in zip(offs, sizes)]
        return parts

    def phase(which):
        def run(ins, outs, sems):
            for cm, part in zip(comms, cut((ins, outs, sems))):
                fn = getattr(cm, which)
                if fn is not None:
                    fn(*part)
        return run

    return Comm(sum((cm.ins for cm in comms), []), sum((cm.out_shapes for cm in comms), []),
                sum((cm.sems for cm in comms), []), phase("first"), phase("mid"), phase("last"))


def _comm_operands(comm):
    if comm is None:
        return [], [], []
    return comm.ins, comm.out_shapes, comm.sems


def _comm_begin(comm, refs, step, n_steps):
    if comm is None:
        return
    pl.when(step == 0)(lambda: comm.first(*refs))
    if comm.mid is not None:
        pl.when(step == (n_steps * 3) // 4)(lambda: comm.mid(*refs))


def _comm_end(comm, refs, step, n_steps):
    if comm is not None:
        pl.when(step == n_steps - 1)(lambda: comm.last(*refs))


def _comm_refs(comm, refs, n_in, n_out):
    if comm is None:
        return list(refs), None
    ci, co, cs = len(comm.ins), len(comm.out_shapes), len(comm.sems)
    o0 = n_in + ci
    s0 = o0 + n_out + co
    host = list(refs[:n_in]) + list(refs[o0:o0 + n_out]) + list(refs[s0:len(refs) - cs])
    return host, (list(refs[n_in:o0]), list(refs[o0 + n_out:s0]), list(refs[len(refs) - cs:]))


def run_comm(comm, name):
    n_in, n_out = len(comm.ins), len(comm.out_shapes)

    def body(*refs):
        parts = (list(refs[:n_in]), list(refs[n_in:n_in + n_out]), list(refs[n_in + n_out:]))
        comm.first(*parts)
        if comm.mid is not None:
            comm.mid(*parts)
        comm.last(*parts)

    return pl.pallas_call(body, name=name, in_specs=[ANY] * n_in, out_specs=[ANY] * n_out,
                          out_shape=comm.out_shapes, scratch_shapes=comm.sems)(*comm.ins)


def hosted_call(work, comm, name, grid, in_specs, out_specs, out_shape, scratch_shapes, args, semantics):
    c_ins, c_outs, c_sems = _comm_operands(comm)
    n_steps = math.prod(grid)

    def body(*refs):
        host, c_refs = _comm_refs(comm, refs, len(in_specs), len(out_specs))
        step = 0
        for axis, size in enumerate(grid):
            step = step * size + pl.program_id(axis)
        _comm_begin(comm, c_refs, step, n_steps)
        work(*host)
        _comm_end(comm, c_refs, step, n_steps)

    return pl.pallas_call(
        body, name=name, grid=grid, in_specs=list(in_specs) + [ANY] * len(c_ins),
        out_specs=list(out_specs) + [ANY] * len(c_outs), out_shape=list(out_shape) + c_outs,
        scratch_shapes=list(scratch_shapes) + c_sems,
        compiler_params=_params(semantics if comm is None else ("arbitrary",) * len(grid)),
    )(*args, *c_ins)


def gather_comm(shards):
    n = len(shards)

    def copies(srcs, outs, sems):
        send_sems, recv_sems, local_sems = sems
        x, y, c = _position()
        me = 2 * x + y
        sibling = (x, y, 1 - c)
        chips = _other_chips(x, y)
        locals_ = [_later(pltpu.make_async_copy, s, o.at[me], local_sems.at[i])
                   for i, (s, o) in enumerate(zip(srcs, outs))]
        sends, arrived, passed, from_sibling = [], [], [], []
        for j, (px, py) in enumerate(chips):
            for i, (s, o) in enumerate(zip(srcs, outs)):
                rows = s.shape[0] // 2
                sends.append(_later(_remote, s.at[pl.ds(c * rows, rows), :], _half(o, me, c), send_sems.at[i, j],
                                    recv_sems.at[i, j], (px, py, c)))
                got = _half(o, 2 * px + py, c)
                arrived.append(_later(_remote, got, got, send_sems.at[i, j], recv_sems.at[i, j], (px, py, c)))
                passed.append(_later(_remote, got, got, send_sems.at[i, 3 + j], recv_sems.at[i, 3 + j], sibling))
                other = _half(o, 2 * px + py, 1 - c)
                from_sibling.append(_later(_remote, other, other, send_sems.at[i, 3 + j], recv_sems.at[i, 3 + j],
                                           sibling))
        return locals_, sends, arrived, passed, from_sibling

    return Comm(shards, [jax.ShapeDtypeStruct((N_CHIPS,) + s.shape, s.dtype) for s in shards],
                [pltpu.SemaphoreType.DMA((n, 6)), pltpu.SemaphoreType.DMA((n, 6)), pltpu.SemaphoreType.DMA((n,))],
                *_two_level_phases(copies))


def swap_comm(grads):
    n = len(grads)

    def copies(srcs, gots, sems):
        send_sems, recv_sems = sems
        x, y, c = _position()
        out = []
        for i, (s, o) in enumerate(zip(srcs, gots)):
            rows = s.shape[1] // 2
            out.append(_remote(s.at[:, pl.ds((1 - c) * rows, rows), :], o, send_sems.at[i], recv_sems.at[i],
                               (x, y, 1 - c)))
        return out

    def first(srcs, gots, sems):
        for cp in copies(srcs, gots, sems):
            cp.start()

    def last(srcs, gots, sems):
        for cp in copies(srcs, gots, sems):
            cp.wait()

    return Comm(grads, [jax.ShapeDtypeStruct((N_CHIPS, g.shape[1] // 2, g.shape[2]), g.dtype) for g in grads],
                [pltpu.SemaphoreType.DMA((n,)), pltpu.SemaphoreType.DMA((n,))], first, None, last)


def add_halves(name, g, got, core):
    _, half, cols = got.shape
    mine = pl.BlockSpec((None, half, cols), lambda k, c_ref: (k, c_ref[0], 0))
    other = pl.BlockSpec((None, half, cols), lambda k, c_ref: (k, 0, 0))

    def body(c_ref, g_ref, got_ref, o_ref):
        o_ref[...] = (g_ref[...] + got_ref[...]).astype(BF16)

    return pl.pallas_call(
        body, name="add_halves_" + name,
        grid_spec=pltpu.PrefetchScalarGridSpec(num_scalar_prefetch=1, grid=(N_CHIPS,), in_specs=[mine, other],
                                               out_specs=other),
        out_shape=jax.ShapeDtypeStruct(got.shape, BF16),
        compiler_params=_params(("parallel",)),
    )(core, g, got)


def exchange_comm(parts):
    n = len(parts)

    def copies(srcs, outs, sems):
        send_sems, recv_sems, local_sems = sems
        x, y, c = _position()
        me = 2 * x + y
        sibling = (x, y, 1 - c)
        chips = _other_chips(x, y)
        locals_, sends, arrived, passed, from_sibling = [], [], [], [], []
        for i, (s, o) in enumerate(zip(srcs, outs)):
            locals_.append(_later(pltpu.make_async_copy, s.at[me], _half(o, me, c), local_sems.at[i]))
            sends.append(_later(_remote, s.at[me], _half(o, me, c), send_sems.at[i, 3], recv_sems.at[i, 3], sibling))
            other = _half(o, me, 1 - c)
            from_sibling.append(_later(_remote, other, other, send_sems.at[i, 3], recv_sems.at[i, 3], sibling))
        for j, (px, py) in enumerate(chips):
            for i, (s, o) in enumerate(zip(srcs, outs)):
                sends.append(_later(_remote, s.at[2 * px + py], _half(o, me, c), send_sems.at[i, j],
                                    recv_sems.at[i, j], (px, py, c)))
                got = _half(o, 2 * px + py, c)
                arrived.append(_later(_remote, got, got, send_sems.at[i, j], recv_sems.at[i, j], (px, py, c)))
                passed.append(_later(_remote, got, got, send_sems.at[i, 4 + j], recv_sems.at[i, 4 + j], sibling))
                other = _half(o, 2 * px + py, 1 - c)
                from_sibling.append(_later(_remote, other, other, send_sems.at[i, 4 + j], recv_sems.at[i, 4 + j],
                                           sibling))
        return locals_, sends, arrived, passed, from_sibling

    return Comm(parts, [jax.ShapeDtypeStruct((N_CHIPS, 2 * p.shape[1], p.shape[2]), p.dtype) for p in parts],
                [pltpu.SemaphoreType.DMA((n, 7)), pltpu.SemaphoreType.DMA((n, 7)), pltpu.SemaphoreType.DMA((n,))],
                *_two_level_phases(copies))


def small_comm(shares):
    n = len(shares)

    def copies(srcs, outs, sems):
        send_sems, recv_sems, local_sems = sems
        x, y, c = _position()
        me = 4 * x + 2 * y + c
        flips = [(fx, fy, fc) for fx in (0, 1) for fy in (0, 1) for fc in (0, 1)][1:]
        peers = [(1 - x if fx else x, 1 - y if fy else y, 1 - c if fc else c) for fx, fy, fc in flips]
        locals_, sends, arrived = [], [], []
        for i, (src_ref, out_ref) in enumerate(zip(srcs, outs)):
            locals_.append(_later(pltpu.make_async_copy, src_ref, out_ref.at[me], local_sems.at[i]))
            for j, (px, py, pc) in enumerate(peers):
                sends.append(_later(_remote, src_ref, out_ref.at[me], send_sems.at[i, j], recv_sems.at[i, j],
                                    (px, py, pc)))
                got = out_ref.at[4 * px + 2 * py + pc]
                arrived.append(_later(_remote, got, got, send_sems.at[i, j], recv_sems.at[i, j], (px, py, pc)))
        return locals_, sends, arrived

    def first(*refs):
        locals_, sends, _ = copies(*refs)
        for cp in locals_ + sends:
            cp().start()

    def last(*refs):
        locals_, sends, arrived = copies(*refs)
        for cp in arrived:
            cp().wait_recv()
        for cp in sends:
            cp().wait_send()
        for cp in locals_:
            cp().wait()

    return Comm(shares, [jax.ShapeDtypeStruct((N_DEV,) + s.shape, s.dtype) for s in shares],
                [pltpu.SemaphoreType.DMA((n, 7)), pltpu.SemaphoreType.DMA((n, 7)), pltpu.SemaphoreType.DMA((n,))],
                first, None, last)


def _adam_fn(w, g, m, v):
    m = ADAM_B1 * m + (1.0 - ADAM_B1) * g
    v = ADAM_B2 * v + (1.0 - ADAM_B2) * jnp.square(g)
    m_hat = m / (1.0 - ADAM_B1 ** ADAM_STEP)
    v_hat = v / (1.0 - ADAM_B2 ** ADAM_STEP)
    return -ADAM_LR * (m_hat / (jnp.sqrt(v_hat) + ADAM_EPS) + ADAM_WD * w), m, v


def adam_big(name, parts, w, m, v):
    rows, cols = w.shape
    tm = _pick(rows, 384, 16)

    def fn(p0, p1, p2, p3, wv, mv, vv):
        g = ((p0.astype(F32) + p1.astype(F32)) + p2.astype(F32)) + p3.astype(F32)
        return (g,) + _adam_fn(wv, g, mv, vv)

    return rowwise(fn, [parts, w, m, v], [(cols, F32)] * 4, "adam_" + name, tm=tm, rows=rows)


def adam_small(name, gathered, w, m, v):
    def body(g_ref, w_ref, m_ref, v_ref, go_ref, d_ref, mo_ref, vo_ref):
        g = g_ref[0]
        for k in range(1, N_DEV):
            g = g + g_ref[k]
        go_ref[...] = g
        d_ref[...], mo_ref[...], vo_ref[...] = _adam_fn(w_ref[...], g, m_ref[...], v_ref[...])

    return pl.pallas_call(body, name=name, out_shape=[jax.ShapeDtypeStruct(w.shape, F32)] * 4,
                          compiler_params=_params())(gathered, w, m, v)


def _ssm_2d(name, t):
    t = t[0] if t.ndim > 2 else t
    if name in ("ssm_b_re", "ssm_b_im"):
        return t.transpose(0, 2, 1).reshape(SSM_W, 64)
    if name in ("ssm_c_re", "ssm_c_im"):
        return t.reshape(SSM_W, 64)
    return t.T if name == "ssm_d" else t


def _ssm_back(name, t):
    if name in ("ssm_b_re", "ssm_b_im"):
        return t.reshape(32, 16, 64).transpose(0, 2, 1)[None]
    if name in ("ssm_c_re", "ssm_c_im"):
        return t.reshape(1, 32, 16, 64)
    if name == "ssm_d":
        return t.T[None]
    return t if name == "ssm_log_dt" else t[None]


def adam_ssm(shares, w, m, v):
    n = len(w)

    def body(*refs):
        ins, outs = refs[:4 * n], refs[4 * n:]
        for i in range(n):
            g_ref, w_ref, m_ref, v_ref = (ins[k * n + i] for k in range(4))
            g = g_ref[0]
            for k in range(1, N_DEV):
                g = g + g_ref[k]
            outs[4 * i][...] = g
            outs[4 * i + 1][...], outs[4 * i + 2][...], outs[4 * i + 3][...] = _adam_fn(w_ref[...], g, m_ref[...],
                                                                                      v_ref[...])

    out_shape = [jax.ShapeDtypeStruct(t.shape, F32) for t in w for _ in range(4)]
    res = pl.pallas_call(body, name="adam_ssm", out_shape=out_shape, compiler_params=_params())(*shares, *w, *m, *v)
    return [res[4 * i:4 * i + 4] for i in range(n)]


def _pack_small(names, vals, rows, last=None):
    flat = [vals[n].reshape(-1) for n in names]
    if last is not None:
        flat.append(last.reshape(-1))
    flat = jnp.concatenate(flat)
    return jnp.pad(flat, (0, rows * LANES - flat.shape[0])).reshape(rows, LANES)


def _unpack_small(names, pack, shapes):
    flat, out, off = pack.reshape(-1), {}, 0
    for n in names:
        size = math.prod(shapes[n])
        out[n] = flat[off:off + size].reshape(shapes[n])
        off += size
    return out, flat[off]


def _to_slots(name, g, shard_shape):
    rows, cols = shard_shape
    if name in ROW_SHARDED:
        return g.reshape(N_CHIPS, rows, cols)
    return g.reshape(rows, N_CHIPS, cols).transpose(1, 0, 2)


def _from_slots(name, s):
    _, rows, cols = s.shape
    if name in ROW_SHARDED:
        return s.reshape(N_CHIPS * rows, cols)
    return s.transpose(1, 0, 2).reshape(rows, N_CHIPS * cols)


def kernel(x, norm_mix_g, w_in, ssm_a_re, ssm_a_im, ssm_log_dt, ssm_b_re, ssm_b_im, ssm_c_re, ssm_c_im, ssm_d, w_glu, w_attn_out, w_out, norm_ffn_g, w_ffn_gate, w_ffn_up, w_ffn_down, norm_final_g, loss_target, m_norm_mix_g, m_w_in, m_ssm_a_re, m_ssm_a_im, m_ssm_log_dt, m_ssm_b_re, m_ssm_b_im, m_ssm_c_re, m_ssm_c_im, m_ssm_d, m_w_glu, m_w_attn_out, m_w_out, m_norm_ffn_g, m_w_ffn_gate, m_w_ffn_up, m_w_ffn_down, m_norm_final_g, v_norm_mix_g, v_w_in, v_ssm_a_re, v_ssm_a_im, v_ssm_log_dt, v_ssm_b_re, v_ssm_b_im, v_ssm_c_re, v_ssm_c_im, v_ssm_d, v_w_glu, v_w_attn_out, v_w_out, v_norm_ffn_g, v_w_ffn_gate, v_w_ffn_up, v_w_ffn_down, v_norm_final_g):
    given = dict(locals())
    def local(name, prefix=""):
        t = given[prefix + name][0]
        return t.T if name in TRANSPOSED else t

    shard = {n: local(n) for n in BIG}
    shapes = {n: given[n].shape for n in WEIGHTS}

    small = {n: given[n] for n in SMALL}
    small_2d = dict(small)
    for n in ("ssm_a_re", "ssm_a_im", "ssm_b_re", "ssm_b_im", "ssm_c_re", "ssm_c_im", "ssm_d"):
        small_2d[n] = small[n][0]
    small_2d["norm_final_g"] = norm_final_g.reshape(1, D_MODEL)

    core = lax.axis_index("c").astype(jnp.int32).reshape(1)
    loss, grad_x, parts, ssm_shares, gs_norm = local_step(
        x.reshape(TOKENS, D_MODEL), loss_target.reshape(TOKENS, D_MODEL),
        {n: shard[n].astype(BF16) for n in BIG}, small_2d, core)

    (norm_shares,) = run_comm(small_comm([_pack_small(NORM_SMALL, gs_norm, NORM_ROWS, last=loss)]),
                              "gather_norm_grads")
    small_out = [{} for _ in range(4)]
    packs = [_pack_small(NORM_SMALL, {n: given[p + n] for n in NORM_SMALL}, NORM_ROWS) for p in ("", "m_", "v_")]
    for kind, t in enumerate(adam_small("adam_norm_gains", norm_shares, *packs)):
        vals, after = _unpack_small(NORM_SMALL, t, shapes)
        small_out[kind].update(vals)
        if kind == 0:
            total_loss = after
    ssm_in = [[_ssm_2d(n, given[p + n]) for n in SSM_SMALL] for p in ("", "m_", "v_")]
    for n, res in zip(SSM_SMALL, adam_ssm(ssm_shares, *ssm_in)):
        for kind, t in enumerate(res):
            small_out[kind][n] = _ssm_back(n, t)

    big_out = {}
    for n in BIG:
        res = adam_big(n, parts[n], shard[n], local(n, "m_"), local(n, "v_"))
        big_out[n] = [(t.T if n in TRANSPOSED else t)[None] for t in res]

    outs = [total_loss, grad_x.reshape(LOCAL_BATCH, SEQ, D_MODEL)]
    for kind in range(4):
        for n in WEIGHTS:
            outs.append(big_out[n][kind] if n in BIG else small_out[kind][n])
    return tuple(outs)
```

```python
import functools
import math

import jax
import jax.numpy as jnp
import numpy as np
from jax import lax
from jax.experimental import pallas as pl
from jax.experimental.pallas import tpu as pltpu

F32 = jnp.float32
BF16 = jnp.bfloat16
MESH = pl.DeviceIdType.MESH

D_MODEL = 1024
SEQ = 2048
LOCAL_BATCH = 2
TOKENS = LOCAL_BATCH * SEQ
HEAD_DIM = 64
HEADS_PER_GROUP = 4
GROUP_W = HEADS_PER_GROUP * HEAD_DIM
N_GROUPS = 3
DILATIONS = (1, 4, 16)
ATTN_BLOCK = 128
ROPE_DIM = 16
ROPE_THETA = 500000.0
QKV_W = 3 * N_GROUPS * GROUP_W
SSM_W = 512
SSM_STATE_W = 2048
SSM_LANE_BLOCKS = 4
GATE_W = 2 * D_MODEL
D_FF = 2816
RMS_EPS = 1e-6
NEG_INF = -1e30
ADAM_LR, ADAM_B1, ADAM_B2, ADAM_EPS, ADAM_WD, ADAM_STEP = 0.001, 0.9, 0.999, 1e-08, 0.01, 10
N_CHIPS = 4
N_DEV = 8

VMEM_LIMIT = 56 * 1024 * 1024
LANES = 128


def _params(sem=None):
    return pltpu.CompilerParams(dimension_semantics=sem, vmem_limit_bytes=VMEM_LIMIT)


def _pick(n, cap, align=LANES):
    best = None
    for d in range(align, min(n, cap) + 1, align):
        if n % d == 0:
            best = d
    return n if best is None or n <= cap else best


_DIMS = {"nn": (((1,), (0,)), ((), ())), "nt": (((1,), (1,)), ((), ())), "tn": (((0,), (0,)), ((), ()))}


def _dot(a, b, mode):
    return lax.dot_general(a, b, _DIMS[mode], preferred_element_type=F32)


def matmul(a, b, mode, out_dtype, name, add=None, comm=None):
    if mode == "nn":
        (m, k), n = a.shape, b.shape[1]
    elif mode == "nt":
        (m, k), n = a.shape, b.shape[0]
    else:
        (k, m), n = a.shape, b.shape[1]
    tn = _pick(n, 1408 if mode != "tn" else 512)
    tk = _pick(k, 2816) if mode != "tn" else k
    tm = _pick(m, 1408)
    out_bytes = jnp.dtype(out_dtype).itemsize

    def need(tm_):
        return 2 * 2 * (tm_ * tk + tk * tn) + tm_ * tn * (4 + 2 * out_bytes + (8 if add is not None else 0))

    while need(tm) > 40 * 1024 * 1024 and tm % 256 == 0:
        tm //= 2
    nk = k // tk
    a_spec = {"nn": pl.BlockSpec((tm, tk), lambda i, j, kk: (i, kk)),
              "nt": pl.BlockSpec((tm, tk), lambda i, j, kk: (i, kk)),
              "tn": pl.BlockSpec((tk, tm), lambda i, j, kk: (kk, i))}[mode]
    b_spec = {"nn": pl.BlockSpec((tk, tn), lambda i, j, kk: (kk, j)),
              "nt": pl.BlockSpec((tn, tk), lambda i, j, kk: (j, kk)),
              "tn": pl.BlockSpec((tk, tn), lambda i, j, kk: (kk, j))}[mode]
    o_spec = pl.BlockSpec((tm, tn), lambda i, j, kk: (i, j))

    def body(a_ref, b_ref, *rest):
        if add is not None:
            add_ref, o_ref, acc_ref = rest
        else:
            o_ref, acc_ref = rest
        part = _dot(a_ref[...], b_ref[...], mode)
        if nk == 1:
            res = part if add is None else part + add_ref[...]
            o_ref[...] = res.astype(out_dtype)
            return
        kk = pl.program_id(2)

        @pl.when(kk == 0)
        def _():
            acc_ref[...] = part

        @pl.when(kk > 0)
        def _():
            acc_ref[...] += part

        @pl.when(kk == nk - 1)
        def _():
            res = acc_ref[...] if add is None else acc_ref[...] + add_ref[...]
            o_ref[...] = res.astype(out_dtype)

    in_specs = [a_spec, b_spec] + ([o_spec] if add is not None else [])
    args = (a, b) + ((add,) if add is not None else ())
    res = hosted_call(
        body, comm, name, (m // tm, n // tn, nk), in_specs, [o_spec], [jax.ShapeDtypeStruct((m, n), out_dtype)],
        [pltpu.VMEM((tm, tn) if nk > 1 else (8, LANES), F32)], args, ("parallel", "parallel", "arbitrary"))
    return res[0] if comm is None else res


def matmul_rows(a, b, name, fn, extra, outs, accs=(), add=None, comm=None, tm=512):
    (m, k), n = a.shape, b.shape[1]
    n_fixed = 2 + (add is not None)
    row_spec = lambda cols: pl.BlockSpec((tm, cols), lambda i: (i, 0))
    in_specs = [row_spec(k), pl.BlockSpec((k, n), lambda i: (0, 0))] + ([row_spec(n)] if add is not None else [])
    in_specs += [pl.BlockSpec(e.shape, lambda i: (0, 0)) if e.shape[0] == 1 else row_spec(e.shape[1]) for e in extra]
    out_specs = [row_spec(c) for c, _ in outs] + [pl.BlockSpec((1, c), lambda i: (0, 0)) for c in accs]
    out_shape = [jax.ShapeDtypeStruct((m, c), dt) for c, dt in outs] + [jax.ShapeDtypeStruct((1, c), F32) for c in accs]

    def body(*refs):
        rows = _dot(refs[0][...], refs[1][...], "nn")
        if add is not None:
            rows = rows + refs[2][...]
        n_in = n_fixed + len(extra)
        res = fn(rows, *[r[...] for r in refs[n_fixed:n_in]])
        for r, v in zip(refs[n_in:n_in + len(outs)], res[:len(outs)]):
            r[...] = v.astype(r.dtype)
        first = pl.program_id(0) == 0
        for r, v in zip(refs[n_in + len(outs):], res[len(outs):]):
            @pl.when(first)
            def _(r=r, v=v):
                r[...] = v

            @pl.when(jnp.logical_not(first))
            def _(r=r, v=v):
                r[...] += v

    args = (a, b) + ((add,) if add is not None else ()) + tuple(extra)
    return hosted_call(body, comm, name, (m // tm,), in_specs, out_specs, out_shape, [], args, ("arbitrary",))


FFN_TM, FFN_TN = 512, 1408


def ffn_in(h2, wg_t, wu_t, comm=None):
    def body(h_ref, wg_ref, wu_ref, a_ref, b_ref, act_ref):
        hv = h_ref[...]
        a, b = _dot(hv, wg_ref[...], "nt"), _dot(hv, wu_ref[...], "nt")
        a_ref[...] = a.astype(BF16)
        b_ref[...] = b.astype(BF16)
        act_ref[...] = _swiglu_fn(a, b).astype(BF16)

    rows = pl.BlockSpec((FFN_TM, D_MODEL), lambda i, j: (i, 0))
    wts = pl.BlockSpec((FFN_TN, D_MODEL), lambda i, j: (j, 0))
    out = pl.BlockSpec((FFN_TM, FFN_TN), lambda i, j: (i, j))
    return hosted_call(body, comm, "ffn_in", (TOKENS // FFN_TM, D_FF // FFN_TN), [rows, wts, wts], [out] * 3,
                       [jax.ShapeDtypeStruct((TOKENS, D_FF), BF16)] * 3, [], (h2, wg_t, wu_t),
                       ("parallel", "parallel"))


def ffn_in_bwd(dx2_b, wd, a, b):
    def body(dx_ref, wd_ref, a_ref, b_ref, da_ref, db_ref):
        dx = dx_ref[...]
        for lo in range(0, FFN_TN, 512):
            cols = slice(lo, min(lo + 512, FFN_TN))
            dact = _dot(dx, wd_ref[cols, :], "nt")
            _, vjp = jax.vjp(_swiglu_fn, a_ref[:, cols].astype(F32), b_ref[:, cols].astype(F32))
            da, db = vjp(dact)
            da_ref[:, cols] = da.astype(BF16)
            db_ref[:, cols] = db.astype(BF16)

    rows = pl.BlockSpec((FFN_TM, D_MODEL), lambda i, j: (i, 0))
    wts = pl.BlockSpec((FFN_TN, D_MODEL), lambda i, j: (j, 0))
    out = pl.BlockSpec((FFN_TM, FFN_TN), lambda i, j: (i, j))
    return pl.pallas_call(
        body, name="ffn_in_bwd", grid=(TOKENS // FFN_TM, D_FF // FFN_TN), in_specs=[rows, wts, out, out],
        out_specs=[out] * 2, out_shape=[jax.ShapeDtypeStruct((TOKENS, D_FF), BF16)] * 2,
        compiler_params=_params(("parallel", "parallel")),
    )(dx2_b, wd, a, b)


def mix_in_bwd(grads, weights, partial, x, g, skip, comm=None):
    n = len(grads)
    tm = 512

    def body(*refs):
        a_refs, b_refs = refs[:n], refs[n:2 * n]
        part_ref, x_ref, g_ref, skip_ref, gx_ref, dg_ref = refs[2 * n:]
        dh = part_ref[...]
        for a_ref, b_ref in zip(a_refs, b_refs):
            dh = dh + _dot(a_ref[...], b_ref[...], "nn")
        _, vjp = jax.vjp(_rms, x_ref[...], g_ref[...])
        dx, dg = vjp(dh)
        gx_ref[...] = dx + skip_ref[...]
        first = pl.program_id(0) == 0

        @pl.when(first)
        def _():
            dg_ref[...] = dg

        @pl.when(jnp.logical_not(first))
        def _():
            dg_ref[...] += dg

    rows = pl.BlockSpec((tm, D_MODEL), lambda i: (i, 0))
    gain = pl.BlockSpec((1, D_MODEL), lambda i: (0, 0))
    in_specs = [pl.BlockSpec((tm, a.shape[1]), lambda i: (i, 0)) for a in grads]
    in_specs += [pl.BlockSpec(b.shape, lambda i: (0, 0)) for b in weights]
    return hosted_call(
        body, comm, "mix_in_bwd", (TOKENS // tm,), in_specs + [rows, rows, gain, rows], [rows, gain],
        [jax.ShapeDtypeStruct((TOKENS, D_MODEL), F32), jax.ShapeDtypeStruct((1, D_MODEL), F32)], [],
        (*grads, *weights, partial, x, g, skip), ("arbitrary",))


def rowwise(fn, ins, outs, name, accs=(), tm=256, rows=TOKENS, comm=None):
    in_specs, args = [], []
    for item in ins:
        arr, width, blk = item if isinstance(item, tuple) else (item, None, 0)
        if arr.ndim == 3:
            for k in range(arr.shape[0]):
                in_specs.append(pl.BlockSpec((None, tm, arr.shape[2]), functools.partial(lambda i, k_: (k_, i, 0), k_=k)))
                args.append(arr)
            continue
        if arr.shape[0] == 1:
            in_specs.append(pl.BlockSpec(arr.shape, lambda i: (0, 0)))
        elif width is None:
            in_specs.append(pl.BlockSpec((tm, arr.shape[1]), lambda i: (i, 0)))
        else:
            in_specs.append(pl.BlockSpec((tm, width), functools.partial(lambda i, blk_: (i, blk_), blk_=blk)))
        args.append(arr)
    out_specs = [pl.BlockSpec((tm, c), lambda i: (i, 0)) for c, _ in outs]
    out_specs += [pl.BlockSpec((1, c), lambda i: (0, 0)) for c in accs]
    out_shape = [jax.ShapeDtypeStruct((rows, c), dt) for c, dt in outs]
    out_shape += [jax.ShapeDtypeStruct((1, c), F32) for c in accs]
    n_in, n_out = len(args), len(outs)
    c_ins, c_outs, c_sems = _comm_operands(comm)

    def body(*refs):
        refs, c_refs = _comm_refs(comm, refs, n_in, n_out + len(accs))
        step = pl.program_id(0)
        _comm_begin(comm, c_refs, step, rows // tm)
        res = fn(*[r[...] for r in refs[:n_in]])
        for r, v in zip(refs[n_in:n_in + n_out], res[:n_out]):
            r[...] = v.astype(r.dtype)
        first = step == 0
        for r, v in zip(refs[n_in + n_out:], res[n_out:]):
            @pl.when(first)
            def _(r=r, v=v):
                r[...] = v

            @pl.when(jnp.logical_not(first))
            def _(r=r, v=v):
                r[...] += v
        _comm_end(comm, c_refs, step, rows // tm)

    return pl.pallas_call(
        body, name=name, grid=(rows // tm,), in_specs=in_specs + [ANY] * len(c_ins),
        out_specs=out_specs + [ANY] * len(c_outs), out_shape=out_shape + c_outs, scratch_shapes=c_sems,
        compiler_params=_params(("arbitrary",)),
    )(*args, *c_ins)


def first_norm(x, g, others, comm=None):
    tm, n = 256, len(others)

    def body(x_ref, g_ref, *rest):
        srcs, h_ref, dsts = rest[:n], rest[n], rest[n + 1:]
        h_ref[...] = _rms(x_ref[...], g_ref[...]).astype(BF16)
        for k, (s, d) in enumerate(zip(srcs, dsts)):
            @pl.when(pl.program_id(0) == k)
            def _(s=s, d=d):
                d[...] = s[...].astype(BF16)

    rows = pl.BlockSpec((tm, D_MODEL), lambda i: (i, 0))
    whole = [pl.BlockSpec(a.shape, lambda i: (0, 0)) for a in others]
    return hosted_call(
        body, comm, "norm_mix", (TOKENS // tm,), [rows, pl.BlockSpec((1, D_MODEL), lambda i: (0, 0))] + whole,
        [rows] + whole, [jax.ShapeDtypeStruct((TOKENS, D_MODEL), BF16)]
        + [jax.ShapeDtypeStruct(a.shape, BF16) for a in others], [], (x, g, *others), ("arbitrary",))


def _rms(x, g):
    return x * lax.rsqrt(jnp.mean(x * x, axis=-1, keepdims=True) + RMS_EPS) * g


def _colsum(v):
    return jnp.sum(v, axis=0, keepdims=True)


PAIR_W = 2 * HEAD_DIM
N_PAIRS = HEADS_PER_GROUP // 2


def _qkv_order(w_t, back=False):
    dims = (N_PAIRS, N_GROUPS, 3) if back else (3, N_GROUPS, N_PAIRS)
    return w_t.reshape(dims + (PAIR_W, w_t.shape[1])).transpose(2, 1, 0, 3, 4).reshape(QKV_W, w_t.shape[1])


def _rope_tables():
    half = ROPE_DIM // 2
    inv = np.power(np.float32(ROPE_THETA), -np.arange(half, dtype=np.float32) * np.float32(2.0 / ROPE_DIM))
    ang = (np.arange(SEQ, dtype=np.float32)[:, None] * inv[None, :]).astype(np.float32)
    cos, sin = np.cos(ang), np.sin(ang)
    zeros = np.zeros((SEQ, HEAD_DIM - ROPE_DIM), np.float32)
    zh = np.zeros((SEQ, half), np.float32)
    c = np.concatenate([cos, cos, zeros + 1.0], axis=1)
    sa = np.concatenate([-sin, zh, zeros], axis=1)
    sb = np.concatenate([zh, sin, zeros], axis=1)
    return [jnp.asarray(np.tile(t, (1, 2)), F32) for t in (c, sa, sb)]


def _rope_fwd(x, c, sa, sb):
    return x * c + pltpu.roll(x, PAIR_W - 8, 1) * sa + pltpu.roll(x, 8, 1) * sb


def _rope_bwd(dy, c, sa, sb):
    return dy * c + pltpu.roll(dy * sb, PAIR_W - 8, 1) + pltpu.roll(dy * sa, 8, 1)


def _band_masks():
    row = lax.broadcasted_iota(jnp.int32, (ATTN_BLOCK, ATTN_BLOCK), 0)
    col = lax.broadcasted_iota(jnp.int32, (ATTN_BLOCK, ATTN_BLOCK), 1)
    return col <= row, col >= row


def _stack_rows(t):
    return jnp.concatenate([t, t], axis=0)


def _stack_heads(t, first_head):
    return jnp.concatenate([jnp.where(first_head, t, 0), jnp.where(first_head, 0, t)], axis=0)


def _per_head(fn):
    return jnp.concatenate([fn(slice(h * HEAD_DIM, (h + 1) * HEAD_DIM)) for h in range(2)], axis=1)


def _slab_spec(kind):
    return pl.BlockSpec((None, SEQ, PAIR_W), lambda b, p, g: (b, 0, p * 3 * N_GROUPS + g * 3 + kind))


_TABLE_SPEC = pl.BlockSpec((SEQ, PAIR_W), lambda b, p, g: (0, 0))
_PAIR_SPEC = pl.BlockSpec((None, SEQ, PAIR_W), lambda b, p, g: (b, 0, p))


def _block_rows(dil, r, n):
    return pl.ds(n * (ATTN_BLOCK * dil) + r, ATTN_BLOCK, stride=dil)


def proj_qkv(h, w_qkv_t, tables, comm=None):
    tm = 1024
    pair_w = QKV_W // N_PAIRS
    scale = HEAD_DIM ** -0.5

    def body(h_ref, w_ref, c_ref, sa_ref, sb_ref, o_ref):
        rows = _dot(h_ref[...], w_ref[...], "nt")
        c, sa, sb = c_ref[...], sa_ref[...], sb_ref[...]
        for blk in range(pair_w // PAIR_W):
            cols = slice(blk * PAIR_W, (blk + 1) * PAIR_W)
            x = rows[:, cols]
            if blk % 3 == 0:
                x = _rope_fwd(x, c, sa, sb) * scale
            elif blk % 3 == 1:
                x = _rope_fwd(x, c, sa, sb)
            o_ref[:, cols] = x

    table = pl.BlockSpec((tm, PAIR_W), lambda i, j, : (i % (SEQ // tm), 0))
    res = hosted_call(
        body, comm, "proj_qkv", (TOKENS // tm, N_PAIRS),
        [pl.BlockSpec((tm, D_MODEL), lambda i, j: (i, 0)), pl.BlockSpec((pair_w, D_MODEL), lambda i, j: (j, 0)),
         table, table, table],
        [pl.BlockSpec((tm, pair_w), lambda i, j: (i, j))], [jax.ShapeDtypeStruct((TOKENS, QKV_W), F32)], [],
        (h, w_qkv_t, *tables), ("parallel", "parallel"))
    return res[0] if comm is None else res


def attn_fwd(qkv, comm=None):
    def body(qs, ks, v_ref, attn_b_ref, attn_ref, lse_ref, o0, o1, o2, l0, l1, l2):
        g = pl.program_id(2)
        cur_mask, prev_mask = _band_masks()
        first_head = lax.broadcasted_iota(jnp.int32, (ATTN_BLOCK, PAIR_W), 1) < HEAD_DIM

        def run(dil, o_slab, l_slab):
            nb = SEQ // dil // ATTN_BLOCK

            def block(idx, carry):
                r, n = lax.div(idx, nb), lax.rem(idx, nb)
                cur, prev = _block_rows(dil, r, n), _block_rows(dil, r, jnp.maximum(n - 1, 0))
                q = qs[cur, :].astype(BF16)
                kc, kp = ks[cur, :].astype(BF16), ks[prev, :].astype(BF16)
                vc, vp = v_ref[cur, :].astype(BF16), v_ref[prev, :].astype(BF16)
                q2 = _stack_heads(q, first_head)
                mask = _stack_rows(jnp.concatenate([jnp.logical_and(prev_mask, n > 0), cur_mask], axis=1))
                s2 = jnp.where(mask, _dot(q2, jnp.concatenate([kp, kc], axis=0), "nt"), NEG_INF)
                m = jnp.max(s2, axis=-1, keepdims=True)
                vcat, two = jnp.concatenate([vp, vc], axis=0), _stack_rows(first_head)
                vext = jnp.concatenate([jnp.where(two, vcat, 1), jnp.where(two, 1, vcat)], axis=1)
                r2 = _dot(jnp.exp(s2 - m).astype(BF16), vext, "nn")
                r0, r1 = r2[:ATTN_BLOCK, :PAIR_W], r2[ATTN_BLOCK:, PAIR_W:]
                num = jnp.where(first_head, r0, r1)
                den = pltpu.roll(jnp.where(first_head, r1, r0), HEAD_DIM, 1)
                o_slab[cur, :] = num / den
                l_slab[cur, :] = jnp.where(first_head, m[:ATTN_BLOCK], m[ATTN_BLOCK:]) + jnp.log(den)
                return carry

            lax.fori_loop(0, SEQ // ATTN_BLOCK, block, 0, unroll=4)

        for gi, (o_slab, l_slab) in enumerate(((o0, l0), (o1, l1), (o2, l2))):
            @pl.when(g == gi)
            def _(gi=gi, o_slab=o_slab, l_slab=l_slab):
                run(DILATIONS[gi], o_slab, l_slab)

        @pl.when(g == N_GROUPS - 1)
        def _():
            a, b, cc = l0[...], l1[...], l2[...]
            m = jnp.maximum(jnp.maximum(a, b), cc)
            e0, e1, e2 = jnp.exp(a - m), jnp.exp(b - m), jnp.exp(cc - m)
            tot = e0 + e1 + e2
            attn = (e0 * o0[...] + e1 * o1[...] + e2 * o2[...]) / tot
            attn_ref[...] = attn
            attn_b_ref[...] = attn.astype(BF16)
            lse_ref[...] = m + jnp.log(tot)

    shape = (LOCAL_BATCH, SEQ, GROUP_W)
    slab = pltpu.VMEM((SEQ, PAIR_W), F32)
    return hosted_call(
        body, comm, "attn_fwd", (LOCAL_BATCH, N_PAIRS, N_GROUPS),
        [_slab_spec(0), _slab_spec(1), _slab_spec(2)], [_PAIR_SPEC] * 3,
        [jax.ShapeDtypeStruct(shape, BF16), jax.ShapeDtypeStruct(shape, F32), jax.ShapeDtypeStruct(shape, F32)],
        [slab] * 6, (qkv, qkv, qkv), ("parallel", "parallel", "arbitrary"))


def attn_bwd(qkv, tables, dattn, attn, lse, comm=None):
    scale = HEAD_DIM ** -0.5

    def body(qs, ks, v_ref, c_ref, sa_ref, sb_ref, do_ref, out_ref, lse_ref, dqkv_ref, dl, dq_s, dk_s, dv_s):
        g = pl.program_id(2)
        c, sa, sb = c_ref[...], sa_ref[...], sb_ref[...]

        @pl.when(g == 0)
        def _():
            prod = do_ref[...] * out_ref[...]
            dl[...] = _per_head(
                lambda sl: jnp.broadcast_to(jnp.sum(prod[:, sl], axis=-1, keepdims=True), (SEQ, HEAD_DIM)))

        cur_mask, prev_mask = _band_masks()
        first_head = lax.broadcasted_iota(jnp.int32, (ATTN_BLOCK, PAIR_W), 1) < HEAD_DIM

        def run(dil):
            nb = SEQ // dil // ATTN_BLOCK

            def block(idx, carry):
                r, n = lax.div(idx, nb), lax.rem(idx, nb)
                cur = _block_rows(dil, r, n)
                prev = _block_rows(dil, r, jnp.maximum(n - 1, 0))
                nxt = _block_rows(dil, r, jnp.minimum(n + 1, nb - 1))
                q0, q1 = qs[cur, :].astype(BF16), qs[nxt, :].astype(BF16)
                kp, kc = ks[prev, :].astype(BF16), ks[cur, :].astype(BF16)
                vp, vc = v_ref[prev, :].astype(BF16), v_ref[cur, :].astype(BF16)
                do0, do1 = do_ref[cur, :].astype(BF16), do_ref[nxt, :].astype(BF16)
                lse0, lse1, dl0, dl1 = lse_ref[cur, :], lse_ref[nxt, :], dl[cur, :], dl[nxt, :]
                has_prev = jnp.logical_and(prev_mask, n > 0)
                has_next = jnp.logical_and(prev_mask, n < nb - 1)

                def per_row(t):
                    return jnp.concatenate([t[:, 0:1], t[:, HEAD_DIM:HEAD_DIM + 1]], axis=0)

                q20, q21 = _stack_heads(q0, first_head), _stack_heads(q1, first_head)
                do20, do21 = _stack_heads(do0, first_head), _stack_heads(do1, first_head)
                kcat, vcat = jnp.concatenate([kp, kc], axis=0), jnp.concatenate([vp, vc], axis=0)
                mask0 = _stack_rows(jnp.concatenate([has_prev, cur_mask], axis=1))
                p0 = jnp.where(mask0, jnp.exp(_dot(q20, kcat, "nt") - per_row(lse0)), 0.0)
                ds0 = (p0 * (_dot(do20, vcat, "nt") - per_row(dl0))).astype(BF16)
                p1 = jnp.where(_stack_rows(has_next), jnp.exp(_dot(q21, kc, "nt") - per_row(lse1)), 0.0)
                ds1 = (p1 * (_dot(do21, vc, "nt") - per_row(dl1))).astype(BF16)
                dq2 = _dot(ds0, kcat, "nn")
                dq_s[cur, :] = jnp.where(first_head, dq2[:ATTN_BLOCK], dq2[ATTN_BLOCK:])
                ds_cur = jnp.concatenate([ds0[:, ATTN_BLOCK:], ds1], axis=0)
                p_cur = jnp.concatenate([p0[:, ATTN_BLOCK:], p1], axis=0).astype(BF16)
                dk_s[cur, :] = _dot(ds_cur, jnp.concatenate([q20, q21], axis=0), "tn")
                dv_s[cur, :] = _dot(p_cur, jnp.concatenate([do20, do21], axis=0), "tn")
                return carry

            lax.fori_loop(0, SEQ // ATTN_BLOCK, block, 0, unroll=2)

        for gi in range(N_GROUPS):
            @pl.when(g == gi)
            def _(gi=gi):
                run(DILATIONS[gi])

        dqkv_ref[:, 0:PAIR_W] = _rope_bwd(dq_s[...] * scale, c, sa, sb).astype(BF16)
        dqkv_ref[:, PAIR_W:2 * PAIR_W] = _rope_bwd(dk_s[...], c, sa, sb).astype(BF16)
        dqkv_ref[:, 2 * PAIR_W:] = dv_s[...].astype(BF16)

    slab = pltpu.VMEM((SEQ, PAIR_W), F32)
    return hosted_call(
        body, comm, "attn_bwd", (LOCAL_BATCH, N_PAIRS, N_GROUPS),
        [_slab_spec(0), _slab_spec(1), _slab_spec(2), _TABLE_SPEC, _TABLE_SPEC, _TABLE_SPEC,
         _PAIR_SPEC, _PAIR_SPEC, _PAIR_SPEC],
        [pl.BlockSpec((None, SEQ, 3 * PAIR_W), lambda b, p, g: (b, 0, p * N_GROUPS + g))],
        [jax.ShapeDtypeStruct((LOCAL_BATCH, SEQ, QKV_W), BF16)],
        [slab] * 4, (qkv, qkv, qkv, *tables, dattn, attn, lse), ("parallel", "parallel", "arbitrary"))


def _discretize(lr, li, log_dt, br, bi):
    dt = jnp.exp(log_dt)
    mag = jnp.exp(lr * dt)
    ab_re, ab_im = mag * jnp.cos(li * dt), mag * jnp.sin(li * dt)
    den = lr * lr + li * li
    nr, ni = ab_re - 1.0, ab_im
    f_re = (nr * lr + ni * li) / den
    f_im = (ni * lr - nr * li) / den
    return ab_re, ab_im, f_re[None] * br - f_im[None] * bi, f_re[None] * bi + f_im[None] * br


def ssm_prep(lr, li, log_dt, br, bi):
    def body(lr_ref, li_ref, dt_ref, br_ref, bi_ref, *outs):
        for o, v in zip(outs, _discretize(lr_ref[...], li_ref[...], dt_ref[...], br_ref[...], bi_ref[...])):
            o[...] = v
    shapes = [lr, li, br, bi]
    return pl.pallas_call(body, name="ssm_prep",
                          out_shape=[jax.ShapeDtypeStruct(s.shape, F32) for s in shapes])(lr, li, log_dt, br, bi)


def ssm_prep_bwd(lr, li, log_dt, br, bi, g_ab_re, g_ab_im, g_bb_re, g_bb_im):
    def body(lr_ref, li_ref, dt_ref, br_ref, bi_ref, g0, g1, g2, g3, *outs):
        _, vjp = jax.vjp(_discretize, lr_ref[...], li_ref[...], dt_ref[...], br_ref[...], bi_ref[...])
        for o, v in zip(outs, vjp((g0[...], g1[...], g2[...], g3[...]))):
            o[...] = v
    shapes = [lr, li, log_dt, br, bi]
    return pl.pallas_call(body, name="ssm_prep_bwd",
                          out_shape=[jax.ShapeDtypeStruct(s.shape, F32) for s in shapes])(
        lr, li, log_dt, br, bi, g_ab_re, g_ab_im, g_bb_re, g_bb_im)


def _block_diag(t):
    per = SSM_STATE_W // SSM_LANE_BLOCKS // 64
    g = t.transpose(1, 0, 2).reshape(SSM_LANE_BLOCKS, per, 16, 64)
    eye = jnp.eye(per, dtype=t.dtype)
    return jnp.einsum("jgcn,gh->jgchn", g, eye).reshape(SSM_LANE_BLOCKS, per * 16, per * 64)


def _block_diag_t(m):
    per = SSM_STATE_W // SSM_LANE_BLOCKS // 64
    m5 = m.reshape(SSM_LANE_BLOCKS, per, 16, per, 64)
    d = jnp.einsum("jgchn,gh->jgcn", m5, jnp.eye(per, dtype=m.dtype))
    return d.reshape(SSM_LANE_BLOCKS * per, 16, 64).transpose(1, 0, 2)


def _cmul(ar, ai, br, bi):
    return ar * br - ai * bi, ar * bi + ai * br


def _power_tables(ar, ai, reverse):
    width = ar.shape[1]
    row = lax.broadcasted_iota(jnp.int32, (8, width), 0)
    pows = [(ar, ai)]
    for _ in range(7):
        pows.append(_cmul(pows[-1][0], pows[-1][1], ar, ai))
    steps = []
    for k in (1, 2, 4):
        keep = (row >= k) if not reverse else (row < 8 - k)
        steps.append((jnp.where(keep, pows[k - 1][0], 0.0), jnp.where(keep, pows[k - 1][1], 0.0)))
    cr = jnp.zeros((8, width), F32)
    ci = jnp.zeros((8, width), F32)
    for i in range(8):
        pr, pi = pows[i] if not reverse else pows[7 - i]
        cr = jnp.where(row == i, pr, cr)
        ci = jnp.where(row == i, pi, ci)
    return steps, (cr, ci)


SCAN_CHUNK = 2048
STATE_BLOCK = SSM_STATE_W // SSM_LANE_BLOCKS
CHAN_BLOCK = SSM_W // SSM_LANE_BLOCKS


def ssm_fwd(u, ab_re, ab_im, bb_re, bb_im, cb_re, cb_im, d_skip, comm=None):
    nt = SEQ // SCAN_CHUNK
    chan = pl.BlockSpec((None, SCAN_CHUNK, CHAN_BLOCK), lambda b, j, t: (b, t, j))
    state = pl.BlockSpec((None, SCAN_CHUNK, STATE_BLOCK), lambda b, j, t: (b, t, j))
    mat = pl.BlockSpec((None, CHAN_BLOCK, STATE_BLOCK), lambda b, j, t: (j, 0, 0))
    lane = pl.BlockSpec((1, STATE_BLOCK), lambda b, j, t: (0, j))
    dsp = pl.BlockSpec((1, CHAN_BLOCK), lambda b, j, t: (0, j))

    def body(u_ref, ar_ref, ai_ref, bbr_ref, bbi_ref, cbr_ref, cbi_ref, d_ref, y_ref, yg_ref, xr_ref, xi_ref,
             car_r, car_i):
        @pl.when(pl.program_id(2) == 0)
        def _():
            car_r[...] = jnp.zeros_like(car_r)
            car_i[...] = jnp.zeros_like(car_i)

        steps, (pr, pi) = _power_tables(ar_ref[...], ai_ref[...], reverse=False)
        uf = u_ref[...]
        ub = uf.astype(BF16)
        xr_ref[...] = _dot(ub, bbr_ref[...], "nn")
        xi_ref[...] = _dot(ub, bbi_ref[...], "nn")

        def tile(i, carry):
            cr, ci = carry
            sl = pl.ds(pl.multiple_of(i * 8, 8), 8)
            br, bi = xr_ref[sl, :], xi_ref[sl, :]
            for k, (sr, si) in zip((1, 2, 4), steps):
                tr, ti = _cmul(sr, si, pltpu.roll(br, k, 0), pltpu.roll(bi, k, 0))
                br, bi = br + tr, bi + ti
            tr, ti = _cmul(pr, pi, cr, ci)
            br, bi = br + tr, bi + ti
            xr_ref[sl, :] = br
            xi_ref[sl, :] = bi
            return br[7:8, :], bi[7:8, :]

        cr, ci = lax.fori_loop(0, SCAN_CHUNK // 8, tile, (car_r[0:1, :], car_i[0:1, :]), unroll=4)
        car_r[0:1, :] = cr
        car_i[0:1, :] = ci
        y = (_dot(xr_ref[...].astype(BF16), cbr_ref[...], "nt") - _dot(xi_ref[...].astype(BF16), cbi_ref[...], "nt")
             + d_ref[...] * uf)
        y_ref[...] = y
        yg_ref[...] = jax.nn.gelu(y).astype(BF16)

    return hosted_call(
        body, comm, "ssm_fwd", (LOCAL_BATCH, SSM_LANE_BLOCKS, nt),
        [chan, lane, lane, mat, mat, mat, mat, dsp], [chan, chan, state, state],
        [jax.ShapeDtypeStruct((LOCAL_BATCH, SEQ, SSM_W), F32), jax.ShapeDtypeStruct((LOCAL_BATCH, SEQ, SSM_W), BF16),
         jax.ShapeDtypeStruct((LOCAL_BATCH, SEQ, SSM_STATE_W), F32),
         jax.ShapeDtypeStruct((LOCAL_BATCH, SEQ, SSM_STATE_W), F32)],
        [pltpu.VMEM((8, STATE_BLOCK), F32), pltpu.VMEM((8, STATE_BLOCK), F32)],
        (u, ab_re, ab_im, bb_re, bb_im, cb_re, cb_im, d_skip), ("parallel", "parallel", "arbitrary"))


def ssm_bwd(dyg, y, u, xr, xi, ab_re, ab_im, bb_re, bb_im, cb_re, cb_im, d_skip, comm=None):
    nt = SEQ // SCAN_CHUNK
    ntile = SCAN_CHUNK // 8

    def rev(t):
        return nt - 1 - t

    chan = pl.BlockSpec((None, SCAN_CHUNK, CHAN_BLOCK), lambda j, b, t: (b, rev(t), j))
    state = pl.BlockSpec((None, SCAN_CHUNK, STATE_BLOCK), lambda j, b, t: (b, rev(t), j))
    before = pl.BlockSpec((None, 8, STATE_BLOCK), lambda j, b, t: (b, jnp.maximum(rev(t) * ntile - 1, 0), j))
    mat = pl.BlockSpec((None, CHAN_BLOCK, STATE_BLOCK), lambda j, b, t: (j, 0, 0))
    lane = pl.BlockSpec((1, STATE_BLOCK), lambda j, b, t: (0, j))
    lane8 = pl.BlockSpec((8, STATE_BLOCK), lambda j, b, t: (0, j))
    dsp = pl.BlockSpec((1, CHAN_BLOCK), lambda j, b, t: (0, j))

    def body(dyg_ref, y_ref, u_ref, xr_ref, xi_ref, xrb_ref, xib_ref, ar_ref, ai_ref, bbr_ref, bbi_ref, cbr_ref,
             cbi_ref, d_ref, du_ref, dcbr_ref, dcbi_ref, dbbr_ref, dbbi_ref, dd_ref, dar_ref, dai_ref,
             lam_r, lam_i, car_r, car_i):
        b, t = pl.program_id(1), pl.program_id(2)
        first = jnp.logical_and(b == 0, t == 0)

        @pl.when(t == 0)
        def _():
            car_r[...] = jnp.zeros_like(car_r)
            car_i[...] = jnp.zeros_like(car_i)

        @pl.when(first)
        def _():
            for r in (dcbr_ref, dcbi_ref, dbbr_ref, dbbi_ref, dd_ref, dar_ref, dai_ref):
                r[...] = jnp.zeros_like(r)

        steps, (pr, pi) = _power_tables(ar_ref[...], -ai_ref[...], reverse=True)
        uf = u_ref[...]
        _, gelu_vjp = jax.vjp(jax.nn.gelu, y_ref[...])
        dy = gelu_vjp(dyg_ref[...])[0]
        dyb = dy.astype(BF16)
        dd_ref[...] += _colsum(dy * uf)
        lam_r[...] = _dot(dyb, cbr_ref[...], "nn")
        lam_i[...] = -_dot(dyb, cbi_ref[...], "nn")
        dcbr_ref[...] += _dot(dyb, xr_ref[...].astype(BF16), "tn")
        dcbi_ref[...] -= _dot(dyb, xi_ref[...].astype(BF16), "tn")
        row0 = lax.broadcasted_iota(jnp.int32, (8, STATE_BLOCK), 0) == 0
        has_before = rev(t) > 0
        xrb = jnp.where(has_before, xrb_ref[...], 0.0)
        xib = jnp.where(has_before, xib_ref[...], 0.0)

        def tile(s, carry):
            cr, ci, acc_r, acc_i = carry
            i = ntile - 1 - s
            sl = pl.ds(pl.multiple_of(i * 8, 8), 8)
            gr, gi = lam_r[sl, :], lam_i[sl, :]
            for k, (sr, si) in zip((1, 2, 4), steps):
                tr, ti = _cmul(sr, si, pltpu.roll(gr, 8 - k, 0), pltpu.roll(gi, 8 - k, 0))
                gr, gi = gr + tr, gi + ti
            tr, ti = _cmul(pr, pi, cr, ci)
            gr, gi = gr + tr, gi + ti
            lam_r[sl, :] = gr
            lam_i[sl, :] = gi
            sp = pl.ds(pl.multiple_of(jnp.maximum(i - 1, 0) * 8, 8), 8)
            pvr = jnp.where(i > 0, xr_ref[sp, :], xrb)
            pvi = jnp.where(i > 0, xi_ref[sp, :], xib)
            xsr = jnp.where(row0, pltpu.roll(pvr, 1, 0), pltpu.roll(xr_ref[sl, :], 1, 0))
            xsi = jnp.where(row0, pltpu.roll(pvi, 1, 0), pltpu.roll(xi_ref[sl, :], 1, 0))
            acc_r = acc_r + xsr * gr + xsi * gi
            acc_i = acc_i + xsr * gi - xsi * gr
            return gr[0:1, :], gi[0:1, :], acc_r, acc_i

        zero = jnp.zeros((8, STATE_BLOCK), F32)
        cr, ci, acc_r, acc_i = lax.fori_loop(0, ntile, tile, (car_r[0:1, :], car_i[0:1, :], zero, zero), unroll=2)
        car_r[0:1, :] = cr
        car_i[0:1, :] = ci
        dar_ref[...] += acc_r
        dai_ref[...] += acc_i
        lrb, lib = lam_r[...].astype(BF16), lam_i[...].astype(BF16)
        du = _dot(lrb, bbr_ref[...], "nt") + _dot(lib, bbi_ref[...], "nt") + d_ref[...] * dy
        du_ref[...] = du.astype(BF16)
        ub = uf.astype(BF16)
        dbbr_ref[...] += _dot(ub, lrb, "tn")
        dbbi_ref[...] += _dot(ub, lib, "tn")

    mat_shape = jax.ShapeDtypeStruct((SSM_LANE_BLOCKS, CHAN_BLOCK, STATE_BLOCK), F32)
    return hosted_call(
        body, comm, "ssm_bwd", (SSM_LANE_BLOCKS, LOCAL_BATCH, nt),
        [chan, chan, chan, state, state, before, before, lane, lane, mat, mat, mat, mat, dsp],
        [chan, mat, mat, mat, mat, dsp, lane8, lane8],
        [jax.ShapeDtypeStruct((LOCAL_BATCH, SEQ, SSM_W), BF16), mat_shape, mat_shape, mat_shape, mat_shape,
         jax.ShapeDtypeStruct((1, SSM_W), F32), jax.ShapeDtypeStruct((8, SSM_STATE_W), F32),
         jax.ShapeDtypeStruct((8, SSM_STATE_W), F32)],
        [pltpu.VMEM((SCAN_CHUNK, STATE_BLOCK), F32), pltpu.VMEM((SCAN_CHUNK, STATE_BLOCK), F32),
         pltpu.VMEM((8, STATE_BLOCK), F32), pltpu.VMEM((8, STATE_BLOCK), F32)],
        (dyg, y, u, xr, xi, xr, xi, ab_re, ab_im, bb_re, bb_im, cb_re, cb_im, d_skip),
        ("parallel", "arbitrary", "arbitrary"))


def _merge_fn(g0, g1, attn_d, za, zb):
    return jax.nn.sigmoid(g0) * attn_d + jax.nn.sigmoid(g1) * (za * jax.nn.sigmoid(zb))


def _swiglu_fn(a, b):
    return jax.nn.silu(a) * b


def _reduce_start(names, gw, shard_shapes):
    return swap_comm([_to_slots(n, gw[n], shard_shapes[n]) for n in names])


def _reduce_chip(names, swap, got, core):
    return exchange_comm([add_halves(n, g, r, core) for n, g, r in zip(names, swap.ins, got)])


def local_step(x, target, shards, small, core):
    g_mix, g_ffn, g_final = small["norm_mix_g"], small["norm_ffn_g"], small["norm_final_g"]
    tables = _rope_tables()
    seqs = lambda t: t.reshape(LOCAL_BATCH, SEQ, t.shape[-1])
    toks = lambda t: t.reshape(TOKENS, t.shape[-1])
    shard_shapes = {n: s.shape for n, s in shards.items()}
    w = {}

    def gather(names):
        return gather_comm([shards[n] for n in names])

    def arrived(names, slots):
        for n, s in zip(names, slots):
            w[n] = _from_slots(n, s)

    later = [n for n in BIG if n != "w_in"]
    shards = dict(shards, w_in=shards["w_in"].astype(BF16))
    h, *rest = first_norm(x, g_mix, [shards[n] for n in later], comm=gather(["w_in"]))
    shards.update(zip(later, rest[:len(later)]))
    arrived(["w_in"], rest[len(later):])
    w_qkv, w_u, w_gate = _qkv_order(w["w_in"][:QKV_W]), w["w_in"][QKV_W:QKV_W + SSM_W], w["w_in"][QKV_W + SSM_W:]
    qkv, *slots = proj_qkv(h, w_qkv, tables, comm=gather(["w_attn_out", "w_glu"]))
    arrived(["w_attn_out", "w_glu"], slots)
    qkv = seqs(qkv)
    u = seqs(matmul(h, w_u, "nt", F32, "proj_u"))
    gl, *slots = matmul(h, w_gate, "nt", BF16, "proj_gate", comm=gather(["w_out"]))
    arrived(["w_out"], slots)
    attn_b, attn, lse, *slots = attn_fwd(qkv, comm=gather(["w_ffn_gate"]))
    arrived(["w_ffn_gate"], slots)
    attn_b = toks(attn_b)
    attn_d = matmul(attn_b, w["w_attn_out"], "nn", F32, "attn_out")

    br_t = small["ssm_b_re"].transpose(2, 0, 1)
    bi_t = small["ssm_b_im"].transpose(2, 0, 1)
    log_dt = small["ssm_log_dt"].reshape(32, 1)
    ab_re, ab_im, bb_re_t, bb_im_t = ssm_prep(small["ssm_a_re"], small["ssm_a_im"], log_dt, br_t, bi_t)
    ab = [ab_re.reshape(1, SSM_STATE_W), ab_im.reshape(1, SSM_STATE_W)]
    bb = [_block_diag(bb_re_t).astype(BF16), _block_diag(bb_im_t).astype(BF16)]
    cb = [_block_diag(small["ssm_c_re"].transpose(1, 0, 2)).astype(BF16),
          _block_diag(small["ssm_c_im"].transpose(1, 0, 2)).astype(BF16)]
    d_skip = small["ssm_d"].reshape(1, SSM_W)
    y, yg, xr, xi, *slots = ssm_fwd(u, *ab, *bb, *cb, d_skip, comm=gather(["w_ffn_up"]))
    arrived(["w_ffn_up"], slots)
    yg2 = toks(yg)
    z = matmul(yg2, w["w_glu"], "nn", BF16, "glu")
    gate_ins = [(gl, D_MODEL, 0), (gl, D_MODEL, 1), attn_d, (z, D_MODEL, 0), (z, D_MODEL, 1)]
    (merged,) = rowwise(lambda *v: (_merge_fn(*[t.astype(F32) for t in v]),), gate_ins, [(D_MODEL, BF16)], "merge")
    x1, h2 = matmul_rows(merged, w["w_out"], "out_proj", lambda rows, g: (rows, _rms(rows, g)), [g_ffn],
                         [(D_MODEL, F32), (D_MODEL, BF16)], add=x)
    a, b, act, *slots = ffn_in(h2, w["w_ffn_gate"], w["w_ffn_up"], comm=gather(["w_ffn_down"]))
    arrived(["w_ffn_down"], slots)

    def final_fn(xv, g, tgt):
        yv, vjp = jax.vjp(_rms, xv, g)
        err = yv - tgt
        dx, dg = vjp(err * (1.0 / D_MODEL))
        loss = 0.5 * jnp.sum(jnp.mean(err * err, axis=-1, keepdims=True), axis=0, keepdims=True)
        return dx, dx, dg, jnp.broadcast_to(loss, (1, LANES))

    dx2, dx2_b, dg_final, loss = matmul_rows(act, w["w_ffn_down"], "ffn_down_loss", final_fn, [g_final, target],
                                             [(D_MODEL, F32), (D_MODEL, BF16)], accs=(D_MODEL, LANES), add=x1)
    gw, parts = {}, {}
    gw["w_ffn_down"] = matmul(act, dx2_b, "tn", F32, "d_ffn_down")
    da_b, db_b = ffn_in_bwd(dx2_b, w["w_ffn_down"], a, b)
    gw["w_ffn_gate"] = matmul(da_b, h2, "tn", F32, "d_ffn_gate")
    gw["w_ffn_up"] = matmul(db_b, h2, "tn", F32, "d_ffn_up")
    ffn = ["w_ffn_down", "w_ffn_gate", "w_ffn_up"]
    swap = _reduce_start(ffn[:2], gw, shard_shapes)
    dh2, *got = matmul(da_b, w["w_ffn_gate"], "nn", F32, "d_h2_gate", comm=swap)
    ffn_exchange = [_reduce_chip(ffn[:2], swap, got, core)]
    swap = _reduce_start(ffn[2:], gw, shard_shapes)

    def norm_bwd(dh, xv, g, skip):
        _, vjp = jax.vjp(_rms, xv, g)
        dx, dg = vjp(dh)
        dx = dx + skip
        return dx, dx, dg

    dx1, dx1_b, dg_ffn, *got = matmul_rows(db_b, w["w_ffn_up"], "d_h2_up_norm", norm_bwd, [x1, g_ffn, dx2],
                                           [(D_MODEL, F32), (D_MODEL, BF16)], accs=(D_MODEL,), add=dh2, comm=swap)
    ffn_up_exchange = _reduce_chip(ffn[2:], swap, got, core)
    gw["w_out"] = matmul(merged, dx1_b, "tn", F32, "d_out")
    dmerged = matmul(dx1_b, w["w_out"], "nt", F32, "d_merged")

    def merge_bwd(g0, g1, ad, za, zb, dm):
        _, vjp = jax.vjp(_merge_fn, *[t.astype(F32) for t in (g0, g1, ad, za, zb)])
        dg0, dg1, dad, dza, dzb = vjp(dm)
        return jnp.concatenate([dg0, dg1], axis=1), dad, jnp.concatenate([dza, dzb], axis=1)

    dgl_b, dattn_d_b, dz_b, parts["w_ffn_up"] = rowwise(
        merge_bwd, gate_ins + [dmerged], [(GATE_W, BF16), (D_MODEL, BF16), (GATE_W, BF16)], "merge_bwd",
        comm=ffn_up_exchange)
    gw["w_attn_out"] = matmul(attn_b, dattn_d_b, "tn", F32, "d_attn_out")
    dattn = seqs(matmul(dattn_d_b, w["w_attn_out"], "nt", F32, "d_attn"))
    gw["w_glu"] = matmul(yg2, dz_b, "tn", F32, "d_glu")
    dyg = seqs(matmul(dz_b, w["w_glu"], "nt", F32, "d_yg"))
    mixer = ["w_out", "w_attn_out", "w_glu"]
    swap = _reduce_start(mixer, gw, shard_shapes)
    du_b, dcb_re, dcb_im, dbb_re, dbb_im, dd, da_re8, da_im8, *rest = ssm_bwd(
        dyg, y, u, xr, xi, *ab, *bb, *cb, d_skip, comm=join_comms(ffn_exchange + [swap]))
    for n, p in zip(ffn[:2], rest[:2]):
        parts[n] = p
    mixer_exchange = _reduce_chip(mixer, swap, rest[2:], core)
    du_b = toks(du_b)
    g_ab_re = jnp.sum(da_re8, axis=0).reshape(32, 64)
    g_ab_im = jnp.sum(da_im8, axis=0).reshape(32, 64)
    d_lr, d_li, d_ldt, d_br_t, d_bi_t = ssm_prep_bwd(
        small["ssm_a_re"], small["ssm_a_im"], log_dt, br_t, bi_t,
        g_ab_re, g_ab_im, _block_diag_t(dbb_re), _block_diag_t(dbb_im))
    as_gcn = lambda t: t.transpose(1, 0, 2).reshape(SSM_W, 64)
    gs = {
        "ssm_a_re": d_lr, "ssm_a_im": d_li, "ssm_log_dt": d_ldt.reshape(1, 32),
        "ssm_b_re": as_gcn(d_br_t), "ssm_b_im": as_gcn(d_bi_t),
        "ssm_c_re": as_gcn(_block_diag_t(dcb_re)), "ssm_c_im": as_gcn(_block_diag_t(dcb_im)),
        "ssm_d": dd.reshape(32, 16).T,
    }
    ssm_gather = small_comm([gs[n] for n in SSM_SMALL])
    dqkv_b, *rest = attn_bwd(qkv, tables, dattn, attn, lse, comm=join_comms([mixer_exchange, ssm_gather]))
    for n, p in zip(mixer, rest):
        parts[n] = p
    ssm_shares = rest[len(mixer):]
    dqkv_b = toks(dqkv_b)
    d_qkv = matmul(dqkv_b, h, "tn", F32, "d_w_qkv")
    d_u = matmul(du_b, h, "tn", F32, "d_w_u")
    d_gate = matmul(dgl_b, h, "tn", F32, "d_w_gate")
    gw["w_in"] = jnp.concatenate([_qkv_order(d_qkv, back=True), d_u, d_gate], axis=0)
    swap = _reduce_start(["w_in"], gw, shard_shapes)
    dh, *got = matmul(dqkv_b, w_qkv, "nn", F32, "d_h_qkv", comm=swap)
    w_in_exchange = _reduce_chip(["w_in"], swap, got, core)
    grad_x, dg_mix, parts["w_in"] = mix_in_bwd([du_b, dgl_b], [w_u, w_gate], dh, x, g_mix, dx1, comm=w_in_exchange)
    gs_norm = {"norm_mix_g": dg_mix, "norm_ffn_g": dg_ffn, "norm_final_g": dg_final}
    return loss, grad_x, parts, ssm_shares, gs_norm


ANY = pl.BlockSpec(memory_space=pl.ANY)
BIG = ("w_in", "w_glu", "w_attn_out", "w_out", "w_ffn_gate", "w_ffn_up", "w_ffn_down")
TRANSPOSED = ("w_in", "w_ffn_gate", "w_ffn_up")
ROW_SHARDED = TRANSPOSED + ("w_out", "w_ffn_down")
SMALL = ("norm_mix_g", "ssm_a_re", "ssm_a_im", "ssm_log_dt", "ssm_b_re", "ssm_b_im", "ssm_c_re", "ssm_c_im",
         "ssm_d", "norm_ffn_g", "norm_final_g")
WEIGHTS = ("norm_mix_g", "w_in", "ssm_a_re", "ssm_a_im", "ssm_log_dt", "ssm_b_re", "ssm_b_im", "ssm_c_re",
           "ssm_c_im", "ssm_d", "w_glu", "w_attn_out", "w_out", "norm_ffn_g", "w_ffn_gate", "w_ffn_up",
           "w_ffn_down", "norm_final_g")
SSM_SMALL = SMALL[1:9]
NORM_SMALL = (SMALL[0],) + SMALL[9:]
NORM_ROWS = 32
N_BIG = len(BIG)


def _position():
    return lax.axis_index("x"), lax.axis_index("y"), lax.axis_index("c")


def _other_chips(x, y):
    return [(1 - x, y), (x, 1 - y), (1 - x, 1 - y)]


def _remote(src, dst, send_sem, recv_sem, device):
    return pltpu.make_async_remote_copy(src_ref=src, dst_ref=dst, send_sem=send_sem, recv_sem=recv_sem,
                                        device_id=device, device_id_type=MESH)


_later = functools.partial


def _two_level_phases(copies):
    def first(*refs):
        locals_, sends, _, _, _ = copies(*refs)
        for cp in locals_ + sends:
            cp().start()

    def mid(*refs):
        _, _, arrived, passed, _ = copies(*refs)
        for got, cp in zip(arrived, passed):
            got().wait_recv()
            cp().start()

    def last(*refs):
        locals_, sends, _, passed, from_sibling = copies(*refs)
        for cp in from_sibling:
            cp().wait_recv()
        for cp in sends + passed:
            cp().wait_send()
        for cp in locals_:
            cp().wait()

    return first, mid, last


def _half(ref, chip, which):
    rows = ref.shape[1] // 2
    return ref.at[chip, pl.ds(which * rows, rows), :]


class Comm:
    def __init__(self, ins, out_shapes, sems, first, mid, last):
        self.ins, self.out_shapes, self.sems = list(ins), list(out_shapes), list(sems)
        self.first, self.mid, self.last = first, mid, last


def join_comms(comms):
    def cut(refs_by_kind):
        offs, parts = [0, 0, 0], []
        for cm in comms:
            sizes = (len(cm.ins), len(cm.out_shapes), len(cm.sems))
            parts.append(tuple(refs_by_kind[k][offs[k]:offs[k] + sizes[k]] for k in range(3)))
            offs = [o + s for o, s in zip(offs, sizes)]
        return parts

    def phase(which):
        def run(ins, outs, sems):
            for cm, part in zip(comms, cut((ins, outs, sems))):
                fn = getattr(cm, which)
                if fn is not None:
                    fn(*part)
        return run

    return Comm(sum((cm.ins for cm in comms), []), sum((cm.out_shapes for cm in comms), []),
                sum((cm.sems for cm in comms), []), phase("first"), phase("mid"), phase("last"))


def _comm_operands(comm):
    if comm is None:
        return [], [], []
    return comm.ins, comm.out_shapes, comm.sems


def _comm_begin(comm, refs, step, n_steps):
    if comm is None:
        return
    pl.when(step == 0)(lambda: comm.first(*refs))
    if comm.mid is not None:
        pl.when(step == (n_steps * 3) // 4)(lambda: comm.mid(*refs))


def _comm_end(comm, refs, step, n_steps):
    if comm is not None:
        pl.when(step == n_steps - 1)(lambda: comm.last(*refs))


def _comm_refs(comm, refs, n_in, n_out):
    if comm is None:
        return list(refs), None
    ci, co, cs = len(comm.ins), len(comm.out_shapes), len(comm.sems)
    o0 = n_in + ci
    s0 = o0 + n_out + co
    host = list(refs[:n_in]) + list(refs[o0:o0 + n_out]) + list(refs[s0:len(refs) - cs])
    return host, (list(refs[n_in:o0]), list(refs[o0 + n_out:s0]), list(refs[len(refs) - cs:]))


def run_comm(comm, name):
    n_in, n_out = len(comm.ins), len(comm.out_shapes)

    def body(*refs):
        parts = (list(refs[:n_in]), list(refs[n_in:n_in + n_out]), list(refs[n_in + n_out:]))
        comm.first(*parts)
        if comm.mid is not None:
            comm.mid(*parts)
        comm.last(*parts)

    return pl.pallas_call(body, name=name, in_specs=[ANY] * n_in, out_specs=[ANY] * n_out,
                          out_shape=comm.out_shapes, scratch_shapes=comm.sems)(*comm.ins)


def hosted_call(work, comm, name, grid, in_specs, out_specs, out_shape, scratch_shapes, args, semantics):
    c_ins, c_outs, c_sems = _comm_operands(comm)
    n_steps = math.prod(grid)

    def body(*refs):
        host, c_refs = _comm_refs(comm, refs, len(in_specs), len(out_specs))
        step = 0
        for axis, size in enumerate(grid):
            step = step * size + pl.program_id(axis)
        _comm_begin(comm, c_refs, step, n_steps)
        work(*host)
        _comm_end(comm, c_refs, step, n_steps)

    return pl.pallas_call(
        body, name=name, grid=grid, in_specs=list(in_specs) + [ANY] * len(c_ins),
        out_specs=list(out_specs) + [ANY] * len(c_outs), out_shape=list(out_shape) + c_outs,
        scratch_shapes=list(scratch_shapes) + c_sems,
        compiler_params=_params(semantics if comm is None else ("arbitrary",) * len(grid)),
    )(*args, *c_ins)


def gather_comm(shards):
    n = len(shards)

    def copies(srcs, outs, sems):
        send_sems, recv_sems, local_sems = sems
        x, y, c = _position()
        me = 2 * x + y
        sibling = (x, y, 1 - c)
        chips = _other_chips(x, y)
        locals_ = [_later(pltpu.make_async_copy, s, o.at[me], local_sems.at[i])
                   for i, (s, o) in enumerate(zip(srcs, outs))]
        sends, arrived, passed, from_sibling = [], [], [], []
        for j, (px, py) in enumerate(chips):
            for i, (s, o) in enumerate(zip(srcs, outs)):
                rows = s.shape[0] // 2
                sends.append(_later(_remote, s.at[pl.ds(c * rows, rows), :], _half(o, me, c), send_sems.at[i, j],
                                    recv_sems.at[i, j], (px, py, c)))
                got = _half(o, 2 * px + py, c)
                arrived.append(_later(_remote, got, got, send_sems.at[i, j], recv_sems.at[i, j], (px, py, c)))
                passed.append(_later(_remote, got, got, send_sems.at[i, 3 + j], recv_sems.at[i, 3 + j], sibling))
                other = _half(o, 2 * px + py, 1 - c)
                from_sibling.append(_later(_remote, other, other, send_sems.at[i, 3 + j], recv_sems.at[i, 3 + j],
                                           sibling))
        return locals_, sends, arrived, passed, from_sibling

    return Comm(shards, [jax.ShapeDtypeStruct((N_CHIPS,) + s.shape, s.dtype) for s in shards],
                [pltpu.SemaphoreType.DMA((n, 6)), pltpu.SemaphoreType.DMA((n, 6)), pltpu.SemaphoreType.DMA((n,))],
                *_two_level_phases(copies))


def swap_comm(grads):
    n = len(grads)

    def copies(srcs, gots, sems):
        send_sems, recv_sems = sems
        x, y, c = _position()
        out = []
        for i, (s, o) in enumerate(zip(srcs, gots)):
            rows = s.shape[1] // 2
            out.append(_remote(s.at[:, pl.ds((1 - c) * rows, rows), :], o, send_sems.at[i], recv_sems.at[i],
                               (x, y, 1 - c)))
        return out

    def first(srcs, gots, sems):
        for cp in copies(srcs, gots, sems):
            cp.start()

    def last(srcs, gots, sems):
        for cp in copies(srcs, gots, sems):
            cp.wait()

    return Comm(grads, [jax.ShapeDtypeStruct((N_CHIPS, g.shape[1] // 2, g.shape[2]), g.dtype) for g in grads],
                [pltpu.SemaphoreType.DMA((n,)), pltpu.SemaphoreType.DMA((n,))], first, None, last)


def add_halves(name, g, got, core):
    _, half, cols = got.shape
    mine = pl.BlockSpec((None, half, cols), lambda k, c_ref: (k, c_ref[0], 0))
    other = pl.BlockSpec((None, half, cols), lambda k, c_ref: (k, 0, 0))

    def body(c_ref, g_ref, got_ref, o_ref):
        o_ref[...] = (g_ref[...] + got_ref[...]).astype(BF16)

    return pl.pallas_call(
        body, name="add_halves_" + name,
        grid_spec=pltpu.PrefetchScalarGridSpec(num_scalar_prefetch=1, grid=(N_CHIPS,), in_specs=[mine, other],
                                               out_specs=other),
        out_shape=jax.ShapeDtypeStruct(got.shape, BF16),
        compiler_params=_params(("parallel",)),
    )(core, g, got)


def exchange_comm(parts):
    n = len(parts)

    def copies(srcs, outs, sems):
        send_sems, recv_sems, local_sems = sems
        x, y, c = _position()
        me = 2 * x + y
        sibling = (x, y, 1 - c)
        chips = _other_chips(x, y)
        locals_, sends, arrived, passed, from_sibling = [], [], [], [], []
        for i, (s, o) in enumerate(zip(srcs, outs)):
            locals_.append(_later(pltpu.make_async_copy, s.at[me], _half(o, me, c), local_sems.at[i]))
            sends.append(_later(_remote, s.at[me], _half(o, me, c), send_sems.at[i, 3], recv_sems.at[i, 3], sibling))
            other = _half(o, me, 1 - c)
            from_sibling.append(_later(_remote, other, other, send_sems.at[i, 3], recv_sems.at[i, 3], sibling))
        for j, (px, py) in enumerate(chips):
            for i, (s, o) in enumerate(zip(srcs, outs)):
                sends.append(_later(_remote, s.at[2 * px + py], _half(o, me, c), send_sems.at[i, j],
                                    recv_sems.at[i, j], (px, py, c)))
                got = _half(o, 2 * px + py, c)
                arrived.append(_later(_remote, got, got, send_sems.at[i, j], recv_sems.at[i, j], (px, py, c)))
                passed.append(_later(_remote, got, got, send_sems.at[i, 4 + j], recv_sems.at[i, 4 + j], sibling))
                other = _half(o, 2 * px + py, 1 - c)
                from_sibling.append(_later(_remote, other, other, send_sems.at[i, 4 + j], recv_sems.at[i, 4 + j],
                                           sibling))
        return locals_, sends, arrived, passed, from_sibling

    return Comm(parts, [jax.ShapeDtypeStruct((N_CHIPS, 2 * p.shape[1], p.shape[2]), p.dtype) for p in parts],
                [pltpu.SemaphoreType.DMA((n, 7)), pltpu.SemaphoreType.DMA((n, 7)), pltpu.SemaphoreType.DMA((n,))],
                *_two_level_phases(copies))


def small_comm(shares):
    n = len(shares)

    def copies(srcs, outs, sems):
        send_sems, recv_sems, local_sems = sems
        x, y, c = _position()
        me = 4 * x + 2 * y + c
        flips = [(fx, fy, fc) for fx in (0, 1) for fy in (0, 1) for fc in (0, 1)][1:]
        peers = [(1 - x if fx else x, 1 - y if fy else y, 1 - c if fc else c) for fx, fy, fc in flips]
        locals_, sends, arrived = [], [], []
        for i, (src_ref, out_ref) in enumerate(zip(srcs, outs)):
            locals_.append(_later(pltpu.make_async_copy, src_ref, out_ref.at[me], local_sems.at[i]))
            for j, (px, py, pc) in enumerate(peers):
                sends.append(_later(_remote, src_ref, out_ref.at[me], send_sems.at[i, j], recv_sems.at[i, j],
                                    (px, py, pc)))
                got = out_ref.at[4 * px + 2 * py + pc]
                arrived.append(_later(_remote, got, got, send_sems.at[i, j], recv_sems.at[i, j], (px, py, pc)))
        return locals_, sends, arrived

    def first(*refs):
        locals_, sends, _ = copies(*refs)
        for cp in locals_ + sends:
            cp().start()

    def last(*refs):
        locals_, sends, arrived = copies(*refs)
        for cp in arrived:
            cp().wait_recv()
        for cp in sends:
            cp().wait_send()
        for cp in locals_:
            cp().wait()

    return Comm(shares, [jax.ShapeDtypeStruct((N_DEV,) + s.shape, s.dtype) for s in shares],
                [pltpu.SemaphoreType.DMA((n, 7)), pltpu.SemaphoreType.DMA((n, 7)), pltpu.SemaphoreType.DMA((n,))],
                first, None, last)


def _adam_fn(w, g, m, v):
    m = ADAM_B1 * m + (1.0 - ADAM_B1) * g
    v = ADAM_B2 * v + (1.0 - ADAM_B2) * jnp.square(g)
    m_hat = m / (1.0 - ADAM_B1 ** ADAM_STEP)
    v_hat = v / (1.0 - ADAM_B2 ** ADAM_STEP)
    return -ADAM_LR * (m_hat / (jnp.sqrt(v_hat) + ADAM_EPS) + ADAM_WD * w), m, v


def adam_big(name, parts, w, m, v):
    rows, cols = w.shape
    tm = _pick(rows, 384, 16)

    def fn(p0, p1, p2, p3, wv, mv, vv):
        g = ((p0.astype(F32) + p1.astype(F32)) + p2.astype(F32)) + p3.astype(F32)
        return (g,) + _adam_fn(wv, g, mv, vv)

    return rowwise(fn, [parts, w, m, v], [(cols, F32)] * 4, "adam_" + name, tm=tm, rows=rows)


def adam_small(name, gathered, w, m, v):
    def body(g_ref, w_ref, m_ref, v_ref, go_ref, d_ref, mo_ref, vo_ref):
        g = g_ref[0]
        for k in range(1, N_DEV):
            g = g + g_ref[k]
        go_ref[...] = g
        d_ref[...], mo_ref[...], vo_ref[...] = _adam_fn(w_ref[...], g, m_ref[...], v_ref[...])

    return pl.pallas_call(body, name=name, out_shape=[jax.ShapeDtypeStruct(w.shape, F32)] * 4,
                          compiler_params=_params())(gathered, w, m, v)


def _ssm_2d(name, t):
    t = t[0] if t.ndim > 2 else t
    if name in ("ssm_b_re", "ssm_b_im"):
        return t.transpose(0, 2, 1).reshape(SSM_W, 64)
    if name in ("ssm_c_re", "ssm_c_im"):
        return t.reshape(SSM_W, 64)
    return t.T if name == "ssm_d" else t


def _ssm_back(name, t):
    if name in ("ssm_b_re", "ssm_b_im"):
        return t.reshape(32, 16, 64).transpose(0, 2, 1)[None]
    if name in ("ssm_c_re", "ssm_c_im"):
        return t.reshape(1, 32, 16, 64)
    if name == "ssm_d":
        return t.T[None]
    return t if name == "ssm_log_dt" else t[None]


def adam_ssm(shares, w, m, v):
    n = len(w)

    def body(*refs):
        ins, outs = refs[:4 * n], refs[4 * n:]
        for i in range(n):
            g_ref, w_ref, m_ref, v_ref = (ins[k * n + i] for k in range(4))
            g = g_ref[0]
            for k in range(1, N_DEV):
                g = g + g_ref[k]
            outs[4 * i][...] = g
            outs[4 * i + 1][...], outs[4 * i + 2][...], outs[4 * i + 3][...] = _adam_fn(w_ref[...], g, m_ref[...],
                                                                                      v_ref[...])

    out_shape = [jax.ShapeDtypeStruct(t.shape, F32) for t in w for _ in range(4)]
    res = pl.pallas_call(body, name="adam_ssm", out_shape=out_shape, compiler_params=_params())(*shares, *w, *m, *v)
    return [res[4 * i:4 * i + 4] for i in range(n)]


def _pack_small(names, vals, rows, last=None):
    flat = [vals[n].reshape(-1) for n in names]
    if last is not None:
        flat.append(last.reshape(-1))
    flat = jnp.concatenate(flat)
    return jnp.pad(flat, (0, rows * LANES - flat.shape[0])).reshape(rows, LANES)


def _unpack_small(names, pack, shapes):
    flat, out, off = pack.reshape(-1), {}, 0
    for n in names:
        size = math.prod(shapes[n])
        out[n] = flat[off:off + size].reshape(shapes[n])
        off += size
    return out, flat[off]


def _to_slots(name, g, shard_shape):
    rows, cols = shard_shape
    if name in ROW_SHARDED:
        return g.reshape(N_CHIPS, rows, cols)
    return g.reshape(rows, N_CHIPS, cols).transpose(1, 0, 2)


def _from_slots(name, s):
    _, rows, cols = s.shape
    if name in ROW_SHARDED:
        return s.reshape(N_CHIPS * rows, cols)
    return s.transpose(1, 0, 2).reshape(rows, N_CHIPS * cols)


def kernel(x, norm_mix_g, w_in, ssm_a_re, ssm_a_im, ssm_log_dt, ssm_b_re, ssm_b_im, ssm_c_re, ssm_c_im, ssm_d, w_glu, w_attn_out, w_out, norm_ffn_g, w_ffn_gate, w_ffn_up, w_ffn_down, norm_final_g, loss_target, m_norm_mix_g, m_w_in, m_ssm_a_re, m_ssm_a_im, m_ssm_log_dt, m_ssm_b_re, m_ssm_b_im, m_ssm_c_re, m_ssm_c_im, m_ssm_d, m_w_glu, m_w_attn_out, m_w_out, m_norm_ffn_g, m_w_ffn_gate, m_w_ffn_up, m_w_ffn_down, m_norm_final_g, v_norm_mix_g, v_w_in, v_ssm_a_re, v_ssm_a_im, v_ssm_log_dt, v_ssm_b_re, v_ssm_b_im, v_ssm_c_re, v_ssm_c_im, v_ssm_d, v_w_glu, v_w_attn_out, v_w_out, v_norm_ffn_g, v_w_ffn_gate, v_w_ffn_up, v_w_ffn_down, v_norm_final_g):
    given = dict(locals())
    def local(name, prefix=""):
        t = given[prefix + name][0]
        return t.T if name in TRANSPOSED else t

    shard = {n: local(n) for n in BIG}
    shapes = {n: given[n].shape for n in WEIGHTS}

    small = {n: given[n] for n in SMALL}
    small_2d = dict(small)
    for n in ("ssm_a_re", "ssm_a_im", "ssm_b_re", "ssm_b_im", "ssm_c_re", "ssm_c_im", "ssm_d"):
        small_2d[n] = small[n][0]
    small_2d["norm_final_g"] = norm_final_g.reshape(1, D_MODEL)

    core = lax.axis_index("c").astype(jnp.int32).reshape(1)
    loss, grad_x, parts, ssm_shares, gs_norm = local_step(
        x.reshape(TOKENS, D_MODEL), loss_target.reshape(TOKENS, D_MODEL),
        {n: shard[n] for n in BIG}, small_2d, core)

    (norm_shares,) = run_comm(small_comm([_pack_small(NORM_SMALL, gs_norm, NORM_ROWS, last=loss)]),
                              "gather_norm_grads")
    small_out = [{} for _ in range(4)]
    packs = [_pack_small(NORM_SMALL, {n: given[p + n] for n in NORM_SMALL}, NORM_ROWS) for p in ("", "m_", "v_")]
    for kind, t in enumerate(adam_small("adam_norm_gains", norm_shares, *packs)):
        vals, after = _unpack_small(NORM_SMALL, t, shapes)
        small_out[kind].update(vals)
        if kind == 0:
            total_loss = after
    ssm_in = [[_ssm_2d(n, given[p + n]) for n in SSM_SMALL] for p in ("", "m_", "v_")]
    for n, res in zip(SSM_SMALL, adam_ssm(ssm_shares, *ssm_in)):
        for kind, t in enumerate(res):
            small_out[kind][n] = _ssm_back(n, t)

    big_out = {}
    for n in BIG:
        res = adam_big(n, parts[n], shard[n], local(n, "m_"), local(n, "v_"))
        big_out[n] = [(t.T if n in TRANSPOSED else t)[None] for t in res]

    outs = [total_loss, grad_x.reshape(LOCAL_BATCH, SEQ, D_MODEL)]
    for kind in range(4):
        for n in WEIGHTS:
            outs.append(big_out[n][kind] if n in BIG else small_out[kind][n])
    return tuple(outs)
```

```python
import functools
import math

import jax
import jax.numpy as jnp
import numpy as np
from jax import lax
from jax.experimental import pallas as pl
from jax.experimental.pallas import tpu as pltpu

F32 = jnp.float32
BF16 = jnp.bfloat16
MESH = pl.DeviceIdType.MESH

D_MODEL = 1024
SEQ = 2048
LOCAL_BATCH = 2
TOKENS = LOCAL_BATCH * SEQ
HEAD_DIM = 64
HEADS_PER_GROUP = 4
GROUP_W = HEADS_PER_GROUP * HEAD_DIM
N_GROUPS = 3
DILATIONS = (1, 4, 16)
ATTN_BLOCK = 128
ROPE_DIM = 16
ROPE_THETA = 500000.0
QKV_W = 3 * N_GROUPS * GROUP_W
SSM_W = 512
SSM_STATE_W = 2048
SSM_LANE_BLOCKS = 4
GATE_W = 2 * D_MODEL
D_FF = 2816
RMS_EPS = 1e-6
NEG_INF = -1e30
ADAM_LR, ADAM_B1, ADAM_B2, ADAM_EPS, ADAM_WD, ADAM_STEP = 0.001, 0.9, 0.999, 1e-08, 0.01, 10
N_CHIPS = 4
N_DEV = 8

VMEM_LIMIT = 56 * 1024 * 1024
LANES = 128


def _params(sem=None):
    return pltpu.CompilerParams(dimension_semantics=sem, vmem_limit_bytes=VMEM_LIMIT)


def _pick(n, cap, align=LANES):
    best = None
    for d in range(align, min(n, cap) + 1, align):
        if n % d == 0:
            best = d
    return n if best is None or n <= cap else best


_DIMS = {"nn": (((1,), (0,)), ((), ())), "nt": (((1,), (1,)), ((), ())), "tn": (((0,), (0,)), ((), ()))}


def _dot(a, b, mode):
    return lax.dot_general(a, b, _DIMS[mode], preferred_element_type=F32)


def matmul(a, b, mode, out_dtype, name, add=None, comm=None):
    if mode == "nn":
        (m, k), n = a.shape, b.shape[1]
    elif mode == "nt":
        (m, k), n = a.shape, b.shape[0]
    else:
        (k, m), n = a.shape, b.shape[1]
    tn = _pick(n, 1408 if mode != "tn" else 512)
    tk = _pick(k, 2816) if mode != "tn" else k
    tm = _pick(m, 1408)
    out_bytes = jnp.dtype(out_dtype).itemsize

    def need(tm_):
        return 2 * 2 * (tm_ * tk + tk * tn) + tm_ * tn * (4 + 2 * out_bytes + (8 if add is not None else 0))

    while need(tm) > 40 * 1024 * 1024 and tm % 256 == 0:
        tm //= 2
    nk = k // tk
    a_spec = {"nn": pl.BlockSpec((tm, tk), lambda i, j, kk: (i, kk)),
              "nt": pl.BlockSpec((tm, tk), lambda i, j, kk: (i, kk)),
              "tn": pl.BlockSpec((tk, tm), lambda i, j, kk: (kk, i))}[mode]
    b_spec = {"nn": pl.BlockSpec((tk, tn), lambda i, j, kk: (kk, j)),
              "nt": pl.BlockSpec((tn, tk), lambda i, j, kk: (j, kk)),
              "tn": pl.BlockSpec((tk, tn), lambda i, j, kk: (kk, j))}[mode]
    o_spec = pl.BlockSpec((tm, tn), lambda i, j, kk: (i, j))

    def body(a_ref, b_ref, *rest):
        if add is not None:
            add_ref, o_ref, acc_ref = rest
        else:
            o_ref, acc_ref = rest
        part = _dot(a_ref[...], b_ref[...], mode)
        if nk == 1:
            res = part if add is None else part + add_ref[...]
            o_ref[...] = res.astype(out_dtype)
            return
        kk = pl.program_id(2)

        @pl.when(kk == 0)
        def _():
            acc_ref[...] = part

        @pl.when(kk > 0)
        def _():
            acc_ref[...] += part

        @pl.when(kk == nk - 1)
        def _():
            res = acc_ref[...] if add is None else acc_ref[...] + add_ref[...]
            o_ref[...] = res.astype(out_dtype)

    in_specs = [a_spec, b_spec] + ([o_spec] if add is not None else [])
    args = (a, b) + ((add,) if add is not None else ())
    res = hosted_call(
        body, comm, name, (m // tm, n // tn, nk), in_specs, [o_spec], [jax.ShapeDtypeStruct((m, n), out_dtype)],
        [pltpu.VMEM((tm, tn) if nk > 1 else (8, LANES), F32)], args, ("parallel", "parallel", "arbitrary"))
    return res[0] if comm is None else res


def matmul_rows(a, b, name, fn, extra, outs, accs=(), add=None, comm=None, tm=512):
    (m, k), n = a.shape, b.shape[1]
    n_fixed = 2 + (add is not None)
    row_spec = lambda cols: pl.BlockSpec((tm, cols), lambda i: (i, 0))
    in_specs = [row_spec(k), pl.BlockSpec((k, n), lambda i: (0, 0))] + ([row_spec(n)] if add is not None else [])
    in_specs += [pl.BlockSpec(e.shape, lambda i: (0, 0)) if e.shape[0] == 1 else row_spec(e.shape[1]) for e in extra]
    out_specs = [row_spec(c) for c, _ in outs] + [pl.BlockSpec((1, c), lambda i: (0, 0)) for c in accs]
    out_shape = [jax.ShapeDtypeStruct((m, c), dt) for c, dt in outs] + [jax.ShapeDtypeStruct((1, c), F32) for c in accs]

    def body(*refs):
        rows = _dot(refs[0][...], refs[1][...], "nn")
        if add is not None:
            rows = rows + refs[2][...]
        n_in = n_fixed + len(extra)
        res = fn(rows, *[r[...] for r in refs[n_fixed:n_in]])
        for r, v in zip(refs[n_in:n_in + len(outs)], res[:len(outs)]):
            r[...] = v.astype(r.dtype)
        first = pl.program_id(0) == 0
        for r, v in zip(refs[n_in + len(outs):], res[len(outs):]):
            @pl.when(first)
            def _(r=r, v=v):
                r[...] = v

            @pl.when(jnp.logical_not(first))
            def _(r=r, v=v):
                r[...] += v

    args = (a, b) + ((add,) if add is not None else ()) + tuple(extra)
    return hosted_call(body, comm, name, (m // tm,), in_specs, out_specs, out_shape, [], args, ("arbitrary",))


FFN_TM, FFN_TN = 512, 1408


def ffn_in(h2, wg_t, wu_t, comm=None):
    def body(h_ref, wg_ref, wu_ref, a_ref, b_ref, act_ref):
        hv = h_ref[...]
        a, b = _dot(hv, wg_ref[...], "nt"), _dot(hv, wu_ref[...], "nt")
        a_ref[...] = a.astype(BF16)
        b_ref[...] = b.astype(BF16)
        act_ref[...] = _swiglu_fn(a, b).astype(BF16)

    rows = pl.BlockSpec((FFN_TM, D_MODEL), lambda i, j: (i, 0))
    wts = pl.BlockSpec((FFN_TN, D_MODEL), lambda i, j: (j, 0))
    out = pl.BlockSpec((FFN_TM, FFN_TN), lambda i, j: (i, j))
    return hosted_call(body, comm, "ffn_in", (TOKENS // FFN_TM, D_FF // FFN_TN), [rows, wts, wts], [out] * 3,
                       [jax.ShapeDtypeStruct((TOKENS, D_FF), BF16)] * 3, [], (h2, wg_t, wu_t),
                       ("parallel", "parallel"))


def ffn_in_bwd(dx2_b, wd, a, b):
    def body(dx_ref, wd_ref, a_ref, b_ref, da_ref, db_ref):
        dx = dx_ref[...]
        for lo in range(0, FFN_TN, 512):
            cols = slice(lo, min(lo + 512, FFN_TN))
            dact = _dot(dx, wd_ref[cols, :], "nt")
            _, vjp = jax.vjp(_swiglu_fn, a_ref[:, cols].astype(F32), b_ref[:, cols].astype(F32))
            da, db = vjp(dact)
            da_ref[:, cols] = da.astype(BF16)
            db_ref[:, cols] = db.astype(BF16)

    rows = pl.BlockSpec((FFN_TM, D_MODEL), lambda i, j: (i, 0))
    wts = pl.BlockSpec((FFN_TN, D_MODEL), lambda i, j: (j, 0))
    out = pl.BlockSpec((FFN_TM, FFN_TN), lambda i, j: (i, j))
    return pl.pallas_call(
        body, name="ffn_in_bwd", grid=(TOKENS // FFN_TM, D_FF // FFN_TN), in_specs=[rows, wts, out, out],
        out_specs=[out] * 2, out_shape=[jax.ShapeDtypeStruct((TOKENS, D_FF), BF16)] * 2,
        compiler_params=_params(("parallel", "parallel")),
    )(dx2_b, wd, a, b)


def mix_in_bwd(grads, weights, partial, x, g, skip, comm=None):
    n = len(grads)
    tm = 512

    def body(*refs):
        a_refs, b_refs = refs[:n], refs[n:2 * n]
        part_ref, x_ref, g_ref, skip_ref, gx_ref, dg_ref = refs[2 * n:]
        dh = part_ref[...]
        for a_ref, b_ref in zip(a_refs, b_refs):
            dh = dh + _dot(a_ref[...], b_ref[...], "nn")
        _, vjp = jax.vjp(_rms, x_ref[...], g_ref[...])
        dx, dg = vjp(dh)
        gx_ref[...] = dx + skip_ref[...]
        first = pl.program_id(0) == 0

        @pl.when(first)
        def _():
            dg_ref[...] = dg

        @pl.when(jnp.logical_not(first))
        def _():
            dg_ref[...] += dg

    rows = pl.BlockSpec((tm, D_MODEL), lambda i: (i, 0))
    gain = pl.BlockSpec((1, D_MODEL), lambda i: (0, 0))
    in_specs = [pl.BlockSpec((tm, a.shape[1]), lambda i: (i, 0)) for a in grads]
    in_specs += [pl.BlockSpec(b.shape, lambda i: (0, 0)) for b in weights]
    return hosted_call(
        body, comm, "mix_in_bwd", (TOKENS // tm,), in_specs + [rows, rows, gain, rows], [rows, gain],
        [jax.ShapeDtypeStruct((TOKENS, D_MODEL), F32), jax.ShapeDtypeStruct((1, D_MODEL), F32)], [],
        (*grads, *weights, partial, x, g, skip), ("arbitrary",))


def rowwise(fn, ins, outs, name, accs=(), tm=256, rows=TOKENS, comm=None):
    in_specs, args = [], []
    for item in ins:
        arr, width, blk = item if isinstance(item, tuple) else (item, None, 0)
        if arr.ndim == 3:
            for k in range(arr.shape[0]):
                in_specs.append(pl.BlockSpec((None, tm, arr.shape[2]), functools.partial(lambda i, k_: (k_, i, 0), k_=k)))
                args.append(arr)
            continue
        if arr.shape[0] == 1:
            in_specs.append(pl.BlockSpec(arr.shape, lambda i: (0, 0)))
        elif width is None:
            in_specs.append(pl.BlockSpec((tm, arr.shape[1]), lambda i: (i, 0)))
        else:
            in_specs.append(pl.BlockSpec((tm, width), functools.partial(lambda i, blk_: (i, blk_), blk_=blk)))
        args.append(arr)
    out_specs = [pl.BlockSpec((tm, c), lambda i: (i, 0)) for c, _ in outs]
    out_specs += [pl.BlockSpec((1, c), lambda i: (0, 0)) for c in accs]
    out_shape = [jax.ShapeDtypeStruct((rows, c), dt) for c, dt in outs]
    out_shape += [jax.ShapeDtypeStruct((1, c), F32) for c in accs]
    n_in, n_out = len(args), len(outs)
    c_ins, c_outs, c_sems = _comm_operands(comm)

    def body(*refs):
        refs, c_refs = _comm_refs(comm, refs, n_in, n_out + len(accs))
        step = pl.program_id(0)
        _comm_begin(comm, c_refs, step, rows // tm)
        res = fn(*[r[...] for r in refs[:n_in]])
        for r, v in zip(refs[n_in:n_in + n_out], res[:n_out]):
            r[...] = v.astype(r.dtype)
        first = step == 0
        for r, v in zip(refs[n_in + n_out:], res[n_out:]):
            @pl.when(first)
            def _(r=r, v=v):
                r[...] = v

            @pl.when(jnp.logical_not(first))
            def _(r=r, v=v):
                r[...] += v
        _comm_end(comm, c_refs, step, rows // tm)

    return pl.pallas_call(
        body, name=name, grid=(rows // tm,), in_specs=in_specs + [ANY] * len(c_ins),
        out_specs=out_specs + [ANY] * len(c_outs), out_shape=out_shape + c_outs, scratch_shapes=c_sems,
        compiler_params=_params(("arbitrary",)),
    )(*args, *c_ins)


def first_norm(x, g, others, comm=None):
    tm, n = 256, len(others)

    def body(x_ref, g_ref, *rest):
        srcs, h_ref, dsts = rest[:n], rest[n], rest[n + 1:]
        h_ref[...] = _rms(x_ref[...], g_ref[...]).astype(BF16)
        for k, (s, d) in enumerate(zip(srcs, dsts)):
            @pl.when(pl.program_id(0) == k)
            def _(s=s, d=d):
                d[...] = s[...].astype(BF16)

    rows = pl.BlockSpec((tm, D_MODEL), lambda i: (i, 0))
    whole = [pl.BlockSpec(a.shape, lambda i: (0, 0)) for a in others]
    return hosted_call(
        body, comm, "norm_mix", (TOKENS // tm,), [rows, pl.BlockSpec((1, D_MODEL), lambda i: (0, 0))] + whole,
        [rows] + whole, [jax.ShapeDtypeStruct((TOKENS, D_MODEL), BF16)]
        + [jax.ShapeDtypeStruct(a.shape, BF16) for a in others], [], (x, g, *others), ("arbitrary",))


def _rms(x, g):
    return x * lax.rsqrt(jnp.mean(x * x, axis=-1, keepdims=True) + RMS_EPS) * g


def _colsum(v):
    return jnp.sum(v, axis=0, keepdims=True)


PAIR_W = 2 * HEAD_DIM
N_PAIRS = HEADS_PER_GROUP // 2


def _qkv_order(w_t, back=False):
    dims = (N_PAIRS, N_GROUPS, 3) if back else (3, N_GROUPS, N_PAIRS)
    return w_t.reshape(dims + (PAIR_W, w_t.shape[1])).transpose(2, 1, 0, 3, 4).reshape(QKV_W, w_t.shape[1])


def _rope_tables():
    half = ROPE_DIM // 2
    inv = np.power(np.float32(ROPE_THETA), -np.arange(half, dtype=np.float32) * np.float32(2.0 / ROPE_DIM))
    ang = (np.arange(SEQ, dtype=np.float32)[:, None] * inv[None, :]).astype(np.float32)
    cos, sin = np.cos(ang), np.sin(ang)
    zeros = np.zeros((SEQ, HEAD_DIM - ROPE_DIM), np.float32)
    zh = np.zeros((SEQ, half), np.float32)
    c = np.concatenate([cos, cos, zeros + 1.0], axis=1)
    sa = np.concatenate([-sin, zh, zeros], axis=1)
    sb = np.concatenate([zh, sin, zeros], axis=1)
    return [jnp.asarray(np.tile(t, (1, 2)), F32) for t in (c, sa, sb)]


def _rope_fwd(x, c, sa, sb):
    return x * c + pltpu.roll(x, PAIR_W - 8, 1) * sa + pltpu.roll(x, 8, 1) * sb


def _rope_bwd(dy, c, sa, sb):
    return dy * c + pltpu.roll(dy * sb, PAIR_W - 8, 1) + pltpu.roll(dy * sa, 8, 1)


def _band_masks():
    row = lax.broadcasted_iota(jnp.int32, (ATTN_BLOCK, ATTN_BLOCK), 0)
    col = lax.broadcasted_iota(jnp.int32, (ATTN_BLOCK, ATTN_BLOCK), 1)
    return col <= row, col >= row


def _stack_rows(t):
    return jnp.concatenate([t, t], axis=0)


def _stack_heads(t, first_head):
    return jnp.concatenate([jnp.where(first_head, t, 0), jnp.where(first_head, 0, t)], axis=0)


def _per_head(fn):
    return jnp.concatenate([fn(slice(h * HEAD_DIM, (h + 1) * HEAD_DIM)) for h in range(2)], axis=1)


def _slab_spec(kind):
    return pl.BlockSpec((None, SEQ, PAIR_W), lambda b, p, g: (b, 0, p * 3 * N_GROUPS + g * 3 + kind))


_TABLE_SPEC = pl.BlockSpec((SEQ, PAIR_W), lambda b, p, g: (0, 0))
_PAIR_SPEC = pl.BlockSpec((None, SEQ, PAIR_W), lambda b, p, g: (b, 0, p))


def _block_rows(dil, r, n):
    return pl.ds(n * (ATTN_BLOCK * dil) + r, ATTN_BLOCK, stride=dil)


def proj_qkv(h, w_qkv_t, tables, comm=None):
    tm = 1024
    pair_w = QKV_W // N_PAIRS
    scale = HEAD_DIM ** -0.5

    def body(h_ref, w_ref, c_ref, sa_ref, sb_ref, o_ref):
        rows = _dot(h_ref[...], w_ref[...], "nt")
        c, sa, sb = c_ref[...], sa_ref[...], sb_ref[...]
        for blk in range(pair_w // PAIR_W):
            cols = slice(blk * PAIR_W, (blk + 1) * PAIR_W)
            x = rows[:, cols]
            if blk % 3 == 0:
                x = _rope_fwd(x, c, sa, sb) * scale
            elif blk % 3 == 1:
                x = _rope_fwd(x, c, sa, sb)
            o_ref[:, cols] = x

    table = pl.BlockSpec((tm, PAIR_W), lambda i, j, : (i % (SEQ // tm), 0))
    res = hosted_call(
        body, comm, "proj_qkv", (TOKENS // tm, N_PAIRS),
        [pl.BlockSpec((tm, D_MODEL), lambda i, j: (i, 0)), pl.BlockSpec((pair_w, D_MODEL), lambda i, j: (j, 0)),
         table, table, table],
        [pl.BlockSpec((tm, pair_w), lambda i, j: (i, j))], [jax.ShapeDtypeStruct((TOKENS, QKV_W), F32)], [],
        (h, w_qkv_t, *tables), ("parallel", "parallel"))
    return res[0] if comm is None else res


def attn_fwd(qkv, comm=None):
    def body(qs, ks, v_ref, attn_b_ref, attn_ref, lse_ref, o0, o1, o2, l0, l1, l2):
        g = pl.program_id(2)
        cur_mask, prev_mask = _band_masks()
        first_head = lax.broadcasted_iota(jnp.int32, (ATTN_BLOCK, PAIR_W), 1) < HEAD_DIM

        def run(dil, o_slab, l_slab):
            nb = SEQ // dil // ATTN_BLOCK

            def block(idx, carry):
                r, n = lax.div(idx, nb), lax.rem(idx, nb)
                cur, prev = _block_rows(dil, r, n), _block_rows(dil, r, jnp.maximum(n - 1, 0))
                q = qs[cur, :].astype(BF16)
                kc, kp = ks[cur, :].astype(BF16), ks[prev, :].astype(BF16)
                vc, vp = v_ref[cur, :].astype(BF16), v_ref[prev, :].astype(BF16)
                q2 = _stack_heads(q, first_head)
                mask = _stack_rows(jnp.concatenate([jnp.logical_and(prev_mask, n > 0), cur_mask], axis=1))
                s2 = jnp.where(mask, _dot(q2, jnp.concatenate([kp, kc], axis=0), "nt"), NEG_INF)
                m = jnp.max(s2, axis=-1, keepdims=True)
                vcat, two = jnp.concatenate([vp, vc], axis=0), _stack_rows(first_head)
                vext = jnp.concatenate([jnp.where(two, vcat, 1), jnp.where(two, 1, vcat)], axis=1)
                r2 = _dot(jnp.exp(s2 - m).astype(BF16), vext, "nn")
                r0, r1 = r2[:ATTN_BLOCK, :PAIR_W], r2[ATTN_BLOCK:, PAIR_W:]
                num = jnp.where(first_head, r0, r1)
                den = pltpu.roll(jnp.where(first_head, r1, r0), HEAD_DIM, 1)
                o_slab[cur, :] = num / den
                l_slab[cur, :] = jnp.where(first_head, m[:ATTN_BLOCK], m[ATTN_BLOCK:]) + jnp.log(den)
                return carry

            lax.fori_loop(0, SEQ // ATTN_BLOCK, block, 0, unroll=4)

        for gi, (o_slab, l_slab) in enumerate(((o0, l0), (o1, l1), (o2, l2))):
            @pl.when(g == gi)
            def _(gi=gi, o_slab=o_slab, l_slab=l_slab):
                run(DILATIONS[gi], o_slab, l_slab)

        @pl.when(g == N_GROUPS - 1)
        def _():
            a, b, cc = l0[...], l1[...], l2[...]
            m = jnp.maximum(jnp.maximum(a, b), cc)
            e0, e1, e2 = jnp.exp(a - m), jnp.exp(b - m), jnp.exp(cc - m)
            tot = e0 + e1 + e2
            attn = (e0 * o0[...] + e1 * o1[...] + e2 * o2[...]) / tot
            attn_ref[...] = attn
            attn_b_ref[...] = attn.astype(BF16)
            lse_ref[...] = m + jnp.log(tot)

    shape = (LOCAL_BATCH, SEQ, GROUP_W)
    slab = pltpu.VMEM((SEQ, PAIR_W), F32)
    return hosted_call(
        body, comm, "attn_fwd", (LOCAL_BATCH, N_PAIRS, N_GROUPS),
        [_slab_spec(0), _slab_spec(1), _slab_spec(2)], [_PAIR_SPEC] * 3,
        [jax.ShapeDtypeStruct(shape, BF16), jax.ShapeDtypeStruct(shape, F32), jax.ShapeDtypeStruct(shape, F32)],
        [slab] * 6, (qkv, qkv, qkv), ("parallel", "parallel", "arbitrary"))


def attn_bwd(qkv, tables, dattn, attn, lse, comm=None):
    scale = HEAD_DIM ** -0.5

    def body(qs, ks, v_ref, c_ref, sa_ref, sb_ref, do_ref, out_ref, lse_ref, dqkv_ref, dl, dq_s, dk_s, dv_s):
        g = pl.program_id(2)
        c, sa, sb = c_ref[...], sa_ref[...], sb_ref[...]

        @pl.when(g == 0)
        def _():
            prod = do_ref[...] * out_ref[...]
            dl[...] = _per_head(
                lambda sl: jnp.broadcast_to(jnp.sum(prod[:, sl], axis=-1, keepdims=True), (SEQ, HEAD_DIM)))

        cur_mask, prev_mask = _band_masks()
        first_head = lax.broadcasted_iota(jnp.int32, (ATTN_BLOCK, PAIR_W), 1) < HEAD_DIM

        def run(dil):
            nb = SEQ // dil // ATTN_BLOCK

            def block(idx, carry):
                r, n = lax.div(idx, nb), lax.rem(idx, nb)
                cur = _block_rows(dil, r, n)
                prev = _block_rows(dil, r, jnp.maximum(n - 1, 0))
                nxt = _block_rows(dil, r, jnp.minimum(n + 1, nb - 1))
                q0, q1 = qs[cur, :].astype(BF16), qs[nxt, :].astype(BF16)
                kp, kc = ks[prev, :].astype(BF16), ks[cur, :].astype(BF16)
                vp, vc = v_ref[prev, :].astype(BF16), v_ref[cur, :].astype(BF16)
                do0, do1 = do_ref[cur, :].astype(BF16), do_ref[nxt, :].astype(BF16)
                lse0, lse1, dl0, dl1 = lse_ref[cur, :], lse_ref[nxt, :], dl[cur, :], dl[nxt, :]
                has_prev = jnp.logical_and(prev_mask, n > 0)
                has_next = jnp.logical_and(prev_mask, n < nb - 1)

                def per_row(t):
                    return jnp.concatenate([t[:, 0:1], t[:, HEAD_DIM:HEAD_DIM + 1]], axis=0)

                q20, q21 = _stack_heads(q0, first_head), _stack_heads(q1, first_head)
                do20, do21 = _stack_heads(do0, first_head), _stack_heads(do1, first_head)
                kcat, vcat = jnp.concatenate([kp, kc], axis=0), jnp.concatenate([vp, vc], axis=0)
                mask0 = _stack_rows(jnp.concatenate([has_prev, cur_mask], axis=1))
                p0 = jnp.where(mask0, jnp.exp(_dot(q20, kcat, "nt") - per_row(lse0)), 0.0)
                ds0 = (p0 * (_dot(do20, vcat, "nt") - per_row(dl0))).astype(BF16)
                p1 = jnp.where(_stack_rows(has_next), jnp.exp(_dot(q21, kc, "nt") - per_row(lse1)), 0.0)
                ds1 = (p1 * (_dot(do21, vc, "nt") - per_row(dl1))).astype(BF16)
                dq2 = _dot(ds0, kcat, "nn")
                dq_s[cur, :] = jnp.where(first_head, dq2[:ATTN_BLOCK], dq2[ATTN_BLOCK:])
                ds_cur = jnp.concatenate([ds0[:, ATTN_BLOCK:], ds1], axis=0)
                p_cur = jnp.concatenate([p0[:, ATTN_BLOCK:], p1], axis=0).astype(BF16)
                dk_s[cur, :] = _dot(ds_cur, jnp.concatenate([q20, q21], axis=0), "tn")
                dv_s[cur, :] = _dot(p_cur, jnp.concatenate([do20, do21], axis=0), "tn")
                return carry

            lax.fori_loop(0, SEQ // ATTN_BLOCK, block, 0, unroll=2)

        for gi in range(N_GROUPS):
            @pl.when(g == gi)
            def _(gi=gi):
                run(DILATIONS[gi])

        dqkv_ref[:, 0:PAIR_W] = _rope_bwd(dq_s[...] * scale, c, sa, sb).astype(BF16)
        dqkv_ref[:, PAIR_W:2 * PAIR_W] = _rope_bwd(dk_s[...], c, sa, sb).astype(BF16)
        dqkv_ref[:, 2 * PAIR_W:] = dv_s[...].astype(BF16)

    slab = pltpu.VMEM((SEQ, PAIR_W), F32)
    return hosted_call(
        body, comm, "attn_bwd", (LOCAL_BATCH, N_PAIRS, N_GROUPS),
        [_slab_spec(0), _slab_spec(1), _slab_spec(2), _TABLE_SPEC, _TABLE_SPEC, _TABLE_SPEC,
         _PAIR_SPEC, _PAIR_SPEC, _PAIR_SPEC],
        [pl.BlockSpec((None, SEQ, 3 * PAIR_W), lambda b, p, g: (b, 0, p * N_GROUPS + g))],
        [jax.ShapeDtypeStruct((LOCAL_BATCH, SEQ, QKV_W), BF16)],
        [slab] * 4, (qkv, qkv, qkv, *tables, dattn, attn, lse), ("parallel", "parallel", "arbitrary"))


def _discretize(lr, li, log_dt, br, bi):
    dt = jnp.exp(log_dt)
    mag = jnp.exp(lr * dt)
    ab_re, ab_im = mag * jnp.cos(li * dt), mag * jnp.sin(li * dt)
    den = lr * lr + li * li
    nr, ni = ab_re - 1.0, ab_im
    f_re = (nr * lr + ni * li) / den
    f_im = (ni * lr - nr * li) / den
    return ab_re, ab_im, f_re[None] * br - f_im[None] * bi, f_re[None] * bi + f_im[None] * br


def ssm_prep(lr, li, log_dt, br, bi):
    def body(lr_ref, li_ref, dt_ref, br_ref, bi_ref, *outs):
        for o, v in zip(outs, _discretize(lr_ref[...], li_ref[...], dt_ref[...], br_ref[...], bi_ref[...])):
            o[...] = v
    shapes = [lr, li, br, bi]
    return pl.pallas_call(body, name="ssm_prep",
                          out_shape=[jax.ShapeDtypeStruct(s.shape, F32) for s in shapes])(lr, li, log_dt, br, bi)


def ssm_prep_bwd(lr, li, log_dt, br, bi, g_ab_re, g_ab_im, g_bb_re, g_bb_im):
    def body(lr_ref, li_ref, dt_ref, br_ref, bi_ref, g0, g1, g2, g3, *outs):
        _, vjp = jax.vjp(_discretize, lr_ref[...], li_ref[...], dt_ref[...], br_ref[...], bi_ref[...])
        for o, v in zip(outs, vjp((g0[...], g1[...], g2[...], g3[...]))):
            o[...] = v
    shapes = [lr, li, log_dt, br, bi]
    return pl.pallas_call(body, name="ssm_prep_bwd",
                          out_shape=[jax.ShapeDtypeStruct(s.shape, F32) for s in shapes])(
        lr, li, log_dt, br, bi, g_ab_re, g_ab_im, g_bb_re, g_bb_im)


def _block_diag(t):
    per = SSM_STATE_W // SSM_LANE_BLOCKS // 64
    g = t.transpose(1, 0, 2).reshape(SSM_LANE_BLOCKS, per, 16, 64)
    eye = jnp.eye(per, dtype=t.dtype)
    return jnp.einsum("jgcn,gh->jgchn", g, eye).reshape(SSM_LANE_BLOCKS, per * 16, per * 64)


def _block_diag_t(m):
    per = SSM_STATE_W // SSM_LANE_BLOCKS // 64
    m5 = m.reshape(SSM_LANE_BLOCKS, per, 16, per, 64)
    d = jnp.einsum("jgchn,gh->jgcn", m5, jnp.eye(per, dtype=m.dtype))
    return d.reshape(SSM_LANE_BLOCKS * per, 16, 64).transpose(1, 0, 2)


def _cmul(ar, ai, br, bi):
    return ar * br - ai * bi, ar * bi + ai * br


def _power_tables(ar, ai, reverse):
    width = ar.shape[1]
    row = lax.broadcasted_iota(jnp.int32, (8, width), 0)
    pows = [(ar, ai)]
    for _ in range(7):
        pows.append(_cmul(pows[-1][0], pows[-1][1], ar, ai))
    steps = []
    for k in (1, 2, 4):
        keep = (row >= k) if not reverse else (row < 8 - k)
        steps.append((jnp.where(keep, pows[k - 1][0], 0.0), jnp.where(keep, pows[k - 1][1], 0.0)))
    cr = jnp.zeros((8, width), F32)
    ci = jnp.zeros((8, width), F32)
    for i in range(8):
        pr, pi = pows[i] if not reverse else pows[7 - i]
        cr = jnp.where(row == i, pr, cr)
        ci = jnp.where(row == i, pi, ci)
    return steps, (cr, ci)


SCAN_CHUNK = 2048
STATE_BLOCK = SSM_STATE_W // SSM_LANE_BLOCKS
CHAN_BLOCK = SSM_W // SSM_LANE_BLOCKS


def ssm_fwd(u, ab_re, ab_im, bb_re, bb_im, cb_re, cb_im, d_skip, comm=None):
    nt = SEQ // SCAN_CHUNK
    chan = pl.BlockSpec((None, SCAN_CHUNK, CHAN_BLOCK), lambda b, j, t: (b, t, j))
    state = pl.BlockSpec((None, SCAN_CHUNK, STATE_BLOCK), lambda b, j, t: (b, t, j))
    mat = pl.BlockSpec((None, CHAN_BLOCK, STATE_BLOCK), lambda b, j, t: (j, 0, 0))
    lane = pl.BlockSpec((1, STATE_BLOCK), lambda b, j, t: (0, j))
    dsp = pl.BlockSpec((1, CHAN_BLOCK), lambda b, j, t: (0, j))

    def body(u_ref, ar_ref, ai_ref, bbr_ref, bbi_ref, cbr_ref, cbi_ref, d_ref, y_ref, yg_ref, xr_ref, xi_ref,
             car_r, car_i):
        @pl.when(pl.program_id(2) == 0)
        def _():
            car_r[...] = jnp.zeros_like(car_r)
            car_i[...] = jnp.zeros_like(car_i)

        steps, (pr, pi) = _power_tables(ar_ref[...], ai_ref[...], reverse=False)
        uf = u_ref[...]
        ub = uf.astype(BF16)
        xr_ref[...] = _dot(ub, bbr_ref[...], "nn")
        xi_ref[...] = _dot(ub, bbi_ref[...], "nn")

        def tile(i, carry):
            cr, ci = carry
            sl = pl.ds(pl.multiple_of(i * 8, 8), 8)
            br, bi = xr_ref[sl, :], xi_ref[sl, :]
            for k, (sr, si) in zip((1, 2, 4), steps):
                tr, ti = _cmul(sr, si, pltpu.roll(br, k, 0), pltpu.roll(bi, k, 0))
                br, bi = br + tr, bi + ti
            tr, ti = _cmul(pr, pi, cr, ci)
            br, bi = br + tr, bi + ti
            xr_ref[sl, :] = br
            xi_ref[sl, :] = bi
            return br[7:8, :], bi[7:8, :]

        cr, ci = lax.fori_loop(0, SCAN_CHUNK // 8, tile, (car_r[0:1, :], car_i[0:1, :]), unroll=4)
        car_r[0:1, :] = cr
        car_i[0:1, :] = ci
        y = (_dot(xr_ref[...].astype(BF16), cbr_ref[...], "nt") - _dot(xi_ref[...].astype(BF16), cbi_ref[...], "nt")
             + d_ref[...] * uf)
        y_ref[...] = y
        yg_ref[...] = jax.nn.gelu(y).astype(BF16)

    return hosted_call(
        body, comm, "ssm_fwd", (LOCAL_BATCH, SSM_LANE_BLOCKS, nt),
        [chan, lane, lane, mat, mat, mat, mat, dsp], [chan, chan, state, state],
        [jax.ShapeDtypeStruct((LOCAL_BATCH, SEQ, SSM_W), F32), jax.ShapeDtypeStruct((LOCAL_BATCH, SEQ, SSM_W), BF16),
         jax.ShapeDtypeStruct((LOCAL_BATCH, SEQ, SSM_STATE_W), F32),
         jax.ShapeDtypeStruct((LOCAL_BATCH, SEQ, SSM_STATE_W), F32)],
        [pltpu.VMEM((8, STATE_BLOCK), F32), pltpu.VMEM((8, STATE_BLOCK), F32)],
        (u, ab_re, ab_im, bb_re, bb_im, cb_re, cb_im, d_skip), ("parallel", "parallel", "arbitrary"))


def ssm_bwd(dyg, y, u, xr, xi, ab_re, ab_im, bb_re, bb_im, cb_re, cb_im, d_skip, comm=None):
    nt = SEQ // SCAN_CHUNK
    ntile = SCAN_CHUNK // 8

    def rev(t):
        return nt - 1 - t

    chan = pl.BlockSpec((None, SCAN_CHUNK, CHAN_BLOCK), lambda j, b, t: (b, rev(t), j))
    state = pl.BlockSpec((None, SCAN_CHUNK, STATE_BLOCK), lambda j, b, t: (b, rev(t), j))
    before = pl.BlockSpec((None, 8, STATE_BLOCK), lambda j, b, t: (b, jnp.maximum(rev(t) * ntile - 1, 0), j))
    mat = pl.BlockSpec((None, CHAN_BLOCK, STATE_BLOCK), lambda j, b, t: (j, 0, 0))
    lane = pl.BlockSpec((1, STATE_BLOCK), lambda j, b, t: (0, j))
    lane8 = pl.BlockSpec((8, STATE_BLOCK), lambda j, b, t: (0, j))
    dsp = pl.BlockSpec((1, CHAN_BLOCK), lambda j, b, t: (0, j))

    def body(dyg_ref, y_ref, u_ref, xr_ref, xi_ref, xrb_ref, xib_ref, ar_ref, ai_ref, bbr_ref, bbi_ref, cbr_ref,
             cbi_ref, d_ref, du_ref, dcbr_ref, dcbi_ref, dbbr_ref, dbbi_ref, dd_ref, dar_ref, dai_ref,
             lam_r, lam_i, car_r, car_i):
        b, t = pl.program_id(1), pl.program_id(2)
        first = jnp.logical_and(b == 0, t == 0)

        @pl.when(t == 0)
        def _():
            car_r[...] = jnp.zeros_like(car_r)
            car_i[...] = jnp.zeros_like(car_i)

        @pl.when(first)
        def _():
            for r in (dcbr_ref, dcbi_ref, dbbr_ref, dbbi_ref, dd_ref, dar_ref, dai_ref):
                r[...] = jnp.zeros_like(r)

        steps, (pr, pi) = _power_tables(ar_ref[...], -ai_ref[...], reverse=True)
        uf = u_ref[...]
        _, gelu_vjp = jax.vjp(jax.nn.gelu, y_ref[...])
        dy = gelu_vjp(dyg_ref[...])[0]
        dyb = dy.astype(BF16)
        dd_ref[...] += _colsum(dy * uf)
        lam_r[...] = _dot(dyb, cbr_ref[...], "nn")
        lam_i[...] = -_dot(dyb, cbi_ref[...], "nn")
        dcbr_ref[...] += _dot(dyb, xr_ref[...].astype(BF16), "tn")
        dcbi_ref[...] -= _dot(dyb, xi_ref[...].astype(BF16), "tn")
        row0 = lax.broadcasted_iota(jnp.int32, (8, STATE_BLOCK), 0) == 0
        has_before = rev(t) > 0
        xrb = jnp.where(has_before, xrb_ref[...], 0.0)
        xib = jnp.where(has_before, xib_ref[...], 0.0)

        def tile(s, carry):
            cr, ci, acc_r, acc_i = carry
            i = ntile - 1 - s
            sl = pl.ds(pl.multiple_of(i * 8, 8), 8)
            gr, gi = lam_r[sl, :], lam_i[sl, :]
            for k, (sr, si) in zip((1, 2, 4), steps):
                tr, ti = _cmul(sr, si, pltpu.roll(gr, 8 - k, 0), pltpu.roll(gi, 8 - k, 0))
                gr, gi = gr + tr, gi + ti
            tr, ti = _cmul(pr, pi, cr, ci)
            gr, gi = gr + tr, gi + ti
            lam_r[sl, :] = gr
            lam_i[sl, :] = gi
            sp = pl.ds(pl.multiple_of(jnp.maximum(i - 1, 0) * 8, 8), 8)
            pvr = jnp.where(i > 0, xr_ref[sp, :], xrb)
            pvi = jnp.where(i > 0, xi_ref[sp, :], xib)
            xsr = jnp.where(row0, pltpu.roll(pvr, 1, 0), pltpu.roll(xr_ref[sl, :], 1, 0))
            xsi = jnp.where(row0, pltpu.roll(pvi, 1, 0), pltpu.roll(xi_ref[sl, :], 1, 0))
            acc_r = acc_r + xsr * gr + xsi * gi
            acc_i = acc_i + xsr * gi - xsi * gr
            return gr[0:1, :], gi[0:1, :], acc_r, acc_i

        zero = jnp.zeros((8, STATE_BLOCK), F32)
        cr, ci, acc_r, acc_i = lax.fori_loop(0, ntile, tile, (car_r[0:1, :], car_i[0:1, :], zero, zero), unroll=2)
        car_r[0:1, :] = cr
        car_i[0:1, :] = ci
        dar_ref[...] += acc_r
        dai_ref[...] += acc_i
        lrb, lib = lam_r[...].astype(BF16), lam_i[...].astype(BF16)
        du = _dot(lrb, bbr_ref[...], "nt") + _dot(lib, bbi_ref[...], "nt") + d_ref[...] * dy
        du_ref[...] = du.astype(BF16)
        ub = uf.astype(BF16)
        dbbr_ref[...] += _dot(ub, lrb, "tn")
        dbbi_ref[...] += _dot(ub, lib, "tn")

    mat_shape = jax.ShapeDtypeStruct((SSM_LANE_BLOCKS, CHAN_BLOCK, STATE_BLOCK), F32)
    return hosted_call(
        body, comm, "ssm_bwd", (SSM_LANE_BLOCKS, LOCAL_BATCH, nt),
        [chan, chan, chan, state, state, before, before, lane, lane, mat, mat, mat, mat, dsp],
        [chan, mat, mat, mat, mat, dsp, lane8, lane8],
        [jax.ShapeDtypeStruct((LOCAL_BATCH, SEQ, SSM_W), BF16), mat_shape, mat_shape, mat_shape, mat_shape,
         jax.ShapeDtypeStruct((1, SSM_W), F32), jax.ShapeDtypeStruct((8, SSM_STATE_W), F32),
         jax.ShapeDtypeStruct((8, SSM_STATE_W), F32)],
        [pltpu.VMEM((SCAN_CHUNK, STATE_BLOCK), F32), pltpu.VMEM((SCAN_CHUNK, STATE_BLOCK), F32),
         pltpu.VMEM((8, STATE_BLOCK), F32), pltpu.VMEM((8, STATE_BLOCK), F32)],
        (dyg, y, u, xr, xi, xr, xi, ab_re, ab_im, bb_re, bb_im, cb_re, cb_im, d_skip),
        ("parallel", "arbitrary", "arbitrary"))


def _merge_fn(g0, g1, attn_d, za, zb):
    return jax.nn.sigmoid(g0) * attn_d + jax.nn.sigmoid(g1) * (za * jax.nn.sigmoid(zb))


def _swiglu_fn(a, b):
    return jax.nn.silu(a) * b


def _reduce_start(names, gw, shard_shapes):
    return swap_comm([_to_slots(n, gw[n], shard_shapes[n]) for n in names])


def _reduce_chip(names, swap, got, core):
    return exchange_comm([add_halves(n, g, r, core) for n, g, r in zip(names, swap.ins, got)])


def local_step(x, target, shards, small, core):
    g_mix, g_ffn, g_final = small["norm_mix_g"], small["norm_ffn_g"], small["norm_final_g"]
    tables = _rope_tables()
    seqs = lambda t: t.reshape(LOCAL_BATCH, SEQ, t.shape[-1])
    toks = lambda t: t.reshape(TOKENS, t.shape[-1])
    shard_shapes = {n: s.shape for n, s in shards.items()}
    w = {}

    def gather(names):
        return gather_comm([shards[n] for n in names])

    def arrived(names, slots):
        for n, s in zip(names, slots):
            w[n] = _from_slots(n, s)

    later = [n for n in BIG if n != "w_in"]
    sems, w_in_shard, land, token = gather_start(shards["w_in"].astype(BF16))
    zero = token[0, 0]
    h, *rest = first_norm(x, g_mix + zero, [shards[n] for n in later])
    shards = dict(shards)
    shards.update(zip(later, rest))
    br_t = small["ssm_b_re"].transpose(2, 0, 1)
    bi_t = small["ssm_b_im"].transpose(2, 0, 1)
    log_dt = small["ssm_log_dt"].reshape(32, 1)
    ab_re, ab_im, bb_re_t, bb_im_t = ssm_prep(small["ssm_a_re"] + zero, small["ssm_a_im"], log_dt, br_t, bi_t)
    ab = [ab_re.reshape(1, SSM_STATE_W), ab_im.reshape(1, SSM_STATE_W)]
    bb = [_block_diag(bb_re_t).astype(BF16), _block_diag(bb_im_t).astype(BF16)]
    cb = [_block_diag((small["ssm_c_re"] + zero).transpose(1, 0, 2)).astype(BF16),
          _block_diag((small["ssm_c_im"] + zero).transpose(1, 0, 2)).astype(BF16)]
    d_skip = small["ssm_d"].reshape(1, SSM_W)
    w_in_shard, land = gather_wait(sems, w_in_shard, land, [h] + bb + cb)
    arrived(["w_in"], run_comm(handover_comm(w_in_shard, land), "w_in_handover"))
    w_qkv, w_u, w_gate = _qkv_order(w["w_in"][:QKV_W]), w["w_in"][QKV_W:QKV_W + SSM_W], w["w_in"][QKV_W + SSM_W:]
    qkv, *slots = proj_qkv(h, w_qkv, tables, comm=gather(["w_attn_out", "w_glu"]))
    arrived(["w_attn_out", "w_glu"], slots)
    qkv = seqs(qkv)
    u = seqs(matmul(h, w_u, "nt", F32, "proj_u"))
    gl, *slots = matmul(h, w_gate, "nt", BF16, "proj_gate", comm=gather(["w_out"]))
    arrived(["w_out"], slots)
    attn_b, attn, lse, *slots = attn_fwd(qkv, comm=gather(["w_ffn_gate"]))
    arrived(["w_ffn_gate"], slots)
    attn_b = toks(attn_b)
    attn_d = matmul(attn_b, w["w_attn_out"], "nn", F32, "attn_out")
    y, yg, xr, xi, *slots = ssm_fwd(u, *ab, *bb, *cb, d_skip, comm=gather(["w_ffn_up"]))
    arrived(["w_ffn_up"], slots)
    yg2 = toks(yg)
    z = matmul(yg2, w["w_glu"], "nn", BF16, "glu")
    gate_ins = [(gl, D_MODEL, 0), (gl, D_MODEL, 1), attn_d, (z, D_MODEL, 0), (z, D_MODEL, 1)]
    (merged,) = rowwise(lambda *v: (_merge_fn(*[t.astype(F32) for t in v]),), gate_ins, [(D_MODEL, BF16)], "merge")
    x1, h2 = matmul_rows(merged, w["w_out"], "out_proj", lambda rows, g: (rows, _rms(rows, g)), [g_ffn],
                         [(D_MODEL, F32), (D_MODEL, BF16)], add=x)
    a, b, act, *slots = ffn_in(h2, w["w_ffn_gate"], w["w_ffn_up"], comm=gather(["w_ffn_down"]))
    arrived(["w_ffn_down"], slots)

    def final_fn(xv, g, tgt):
        yv, vjp = jax.vjp(_rms, xv, g)
        err = yv - tgt
        dx, dg = vjp(err * (1.0 / D_MODEL))
        loss = 0.5 * jnp.sum(jnp.mean(err * err, axis=-1, keepdims=True), axis=0, keepdims=True)
        return dx, dx, dg, jnp.broadcast_to(loss, (1, LANES))

    dx2, dx2_b, dg_final, loss = matmul_rows(act, w["w_ffn_down"], "ffn_down_loss", final_fn, [g_final, target],
                                             [(D_MODEL, F32), (D_MODEL, BF16)], accs=(D_MODEL, LANES), add=x1)
    gw, parts = {}, {}
    gw["w_ffn_down"] = matmul(act, dx2_b, "tn", F32, "d_ffn_down")
    da_b, db_b = ffn_in_bwd(dx2_b, w["w_ffn_down"], a, b)
    gw["w_ffn_gate"] = matmul(da_b, h2, "tn", F32, "d_ffn_gate")
    gw["w_ffn_up"] = matmul(db_b, h2, "tn", F32, "d_ffn_up")
    ffn = ["w_ffn_down", "w_ffn_gate", "w_ffn_up"]
    swap = _reduce_start(ffn[:2], gw, shard_shapes)
    dh2, *got = matmul(da_b, w["w_ffn_gate"], "nn", F32, "d_h2_gate", comm=swap)
    ffn_exchange = [_reduce_chip(ffn[:2], swap, got, core)]
    swap = _reduce_start(ffn[2:], gw, shard_shapes)

    def norm_bwd(dh, xv, g, skip):
        _, vjp = jax.vjp(_rms, xv, g)
        dx, dg = vjp(dh)
        dx = dx + skip
        return dx, dx, dg

    dx1, dx1_b, dg_ffn, *got = matmul_rows(db_b, w["w_ffn_up"], "d_h2_up_norm", norm_bwd, [x1, g_ffn, dx2],
                                           [(D_MODEL, F32), (D_MODEL, BF16)], accs=(D_MODEL,), add=dh2, comm=swap)
    ffn_up_exchange = _reduce_chip(ffn[2:], swap, got, core)
    gw["w_out"] = matmul(merged, dx1_b, "tn", F32, "d_out")
    dmerged = matmul(dx1_b, w["w_out"], "nt", F32, "d_merged")

    def merge_bwd(g0, g1, ad, za, zb, dm):
        _, vjp = jax.vjp(_merge_fn, *[t.astype(F32) for t in (g0, g1, ad, za, zb)])
        dg0, dg1, dad, dza, dzb = vjp(dm)
        return jnp.concatenate([dg0, dg1], axis=1), dad, jnp.concatenate([dza, dzb], axis=1)

    dgl_b, dattn_d_b, dz_b, parts["w_ffn_up"] = rowwise(
        merge_bwd, gate_ins + [dmerged], [(GATE_W, BF16), (D_MODEL, BF16), (GATE_W, BF16)], "merge_bwd",
        comm=ffn_up_exchange)
    gw["w_attn_out"] = matmul(attn_b, dattn_d_b, "tn", F32, "d_attn_out")
    dattn = seqs(matmul(dattn_d_b, w["w_attn_out"], "nt", F32, "d_attn"))
    gw["w_glu"] = matmul(yg2, dz_b, "tn", F32, "d_glu")
    dyg = seqs(matmul(dz_b, w["w_glu"], "nt", F32, "d_yg"))
    mixer = ["w_out", "w_attn_out", "w_glu"]
    swap = _reduce_start(mixer, gw, shard_shapes)
    du_b, dcb_re, dcb_im, dbb_re, dbb_im, dd, da_re8, da_im8, *rest = ssm_bwd(
        dyg, y, u, xr, xi, *ab, *bb, *cb, d_skip, comm=join_comms(ffn_exchange + [swap]))
    for n, p in zip(ffn[:2], rest[:2]):
        parts[n] = p
    mixer_exchange = _reduce_chip(mixer, swap, rest[2:], core)
    du_b = toks(du_b)
    g_ab_re = jnp.sum(da_re8, axis=0).reshape(32, 64)
    g_ab_im = jnp.sum(da_im8, axis=0).reshape(32, 64)
    d_lr, d_li, d_ldt, d_br_t, d_bi_t = ssm_prep_bwd(
        small["ssm_a_re"], small["ssm_a_im"], log_dt, br_t, bi_t,
        g_ab_re, g_ab_im, _block_diag_t(dbb_re), _block_diag_t(dbb_im))
    as_gcn = lambda t: t.transpose(1, 0, 2).reshape(SSM_W, 64)
    gs = {
        "ssm_a_re": d_lr, "ssm_a_im": d_li, "ssm_log_dt": d_ldt.reshape(1, 32),
        "ssm_b_re": as_gcn(d_br_t), "ssm_b_im": as_gcn(d_bi_t),
        "ssm_c_re": as_gcn(_block_diag_t(dcb_re)), "ssm_c_im": as_gcn(_block_diag_t(dcb_im)),
        "ssm_d": dd.reshape(32, 16).T,
    }
    ssm_gather = small_comm([gs[n] for n in SSM_SMALL])
    dqkv_b, *rest = attn_bwd(qkv, tables, dattn, attn, lse, comm=join_comms([mixer_exchange, ssm_gather]))
    for n, p in zip(mixer, rest):
        parts[n] = p
    ssm_shares = rest[len(mixer):]
    dqkv_b = toks(dqkv_b)
    d_qkv = matmul(dqkv_b, h, "tn", F32, "d_w_qkv")
    d_u = matmul(du_b, h, "tn", F32, "d_w_u")
    d_gate = matmul(dgl_b, h, "tn", F32, "d_w_gate")
    gw["w_in"] = jnp.concatenate([_qkv_order(d_qkv, back=True), d_u, d_gate], axis=0)
    swap = _reduce_start(["w_in"], gw, shard_shapes)
    dh, *got = matmul(dqkv_b, w_qkv, "nn", F32, "d_h_qkv", comm=swap)
    w_in_exchange = _reduce_chip(["w_in"], swap, got, core)
    grad_x, dg_mix, parts["w_in"] = mix_in_bwd([du_b, dgl_b], [w_u, w_gate], dh, x, g_mix, dx1, comm=w_in_exchange)
    gs_norm = {"norm_mix_g": dg_mix, "norm_ffn_g": dg_ffn, "norm_final_g": dg_final}
    return loss, grad_x, parts, ssm_shares, gs_norm


ANY = pl.BlockSpec(memory_space=pl.ANY)
BIG = ("w_in", "w_glu", "w_attn_out", "w_out", "w_ffn_gate", "w_ffn_up", "w_ffn_down")
TRANSPOSED = ("w_in", "w_ffn_gate", "w_ffn_up")
ROW_SHARDED = TRANSPOSED + ("w_out", "w_ffn_down")
SMALL = ("norm_mix_g", "ssm_a_re", "ssm_a_im", "ssm_log_dt", "ssm_b_re", "ssm_b_im", "ssm_c_re", "ssm_c_im",
         "ssm_d", "norm_ffn_g", "norm_final_g")
WEIGHTS = ("norm_mix_g", "w_in", "ssm_a_re", "ssm_a_im", "ssm_log_dt", "ssm_b_re", "ssm_b_im", "ssm_c_re",
           "ssm_c_im", "ssm_d", "w_glu", "w_attn_out", "w_out", "norm_ffn_g", "w_ffn_gate", "w_ffn_up",
           "w_ffn_down", "norm_final_g")
SSM_SMALL = SMALL[1:9]
NORM_SMALL = (SMALL[0],) + SMALL[9:]
NORM_ROWS = 32
N_BIG = len(BIG)


def _position():
    return lax.axis_index("x"), lax.axis_index("y"), lax.axis_index("c")


def _other_chips(x, y):
    return [(1 - x, y), (x, 1 - y), (1 - x, 1 - y)]


def _remote(src, dst, send_sem, recv_sem, device):
    return pltpu.make_async_remote_copy(src_ref=src, dst_ref=dst, send_sem=send_sem, recv_sem=recv_sem,
                                        device_id=device, device_id_type=MESH)


_later = functools.partial


def _two_level_phases(copies):
    def first(*refs):
        locals_, sends, _, _, _ = copies(*refs)
        for cp in locals_ + sends:
            cp().start()

    def mid(*refs):
        _, _, arrived, passed, _ = copies(*refs)
        for got, cp in zip(arrived, passed):
            got().wait_recv()
            cp().start()

    def last(*refs):
        locals_, sends, _, passed, from_sibling = copies(*refs)
        for cp in from_sibling:
            cp().wait_recv()
        for cp in sends + passed:
            cp().wait_send()
        for cp in locals_:
            cp().wait()

    return first, mid, last


def _half(ref, chip, which):
    rows = ref.shape[1] // 2
    return ref.at[chip, pl.ds(which * rows, rows), :]


class Comm:
    def __init__(self, ins, out_shapes, sems, first, mid, last):
        self.ins, self.out_shapes, self.sems = list(ins), list(out_shapes), list(sems)
        self.first, self.mid, self.last = first, mid, last


def join_comms(comms):
    def cut(refs_by_kind):
        offs, parts = [0, 0, 0], []
        for cm in comms:
            sizes = (len(cm.ins), len(cm.out_shapes), len(cm.sems))
            parts.append(tuple(refs_by_kind[k][offs[k]:offs[k] + sizes[k]] for k in range(3)))
            offs = [o + s for o, s in zip(offs, sizes)]
        return parts

    def phase(which):
        def run(ins, outs, sems):
            for cm, part in zip(comms, cut((ins, outs, sems))):
                fn = getattr(cm, which)
                if fn is not None:
                    fn(*part)
        return run

    return Comm(sum((cm.ins for cm in comms), []), sum((cm.out_shapes for cm in comms), []),
                sum((cm.sems for cm in comms), []), phase("first"), phase("mid"), phase("last"))


def _comm_operands(comm):
    if comm is None:
        return [], [], []
    return comm.ins, comm.out_shapes, comm.sems


def _comm_begin(comm, refs, step, n_steps):
    if comm is None:
        return
    pl.when(step == 0)(lambda: comm.first(*refs))
    if comm.mid is not None:
        pl.when(step == (n_steps * 3) // 4)(lambda: comm.mid(*refs))


def _comm_end(comm, refs, step, n_steps):
    if comm is not None:
        pl.when(step == n_steps - 1)(lambda: comm.last(*refs))


def _comm_refs(comm, refs, n_in, n_out):
    if comm is None:
        return list(refs), None
    ci, co, cs = len(comm.ins), len(comm.out_shapes), len(comm.sems)
    o0 = n_in + ci
    s0 = o0 + n_out + co
    host = list(refs[:n_in]) + list(refs[o0:o0 + n_out]) + list(refs[s0:len(refs) - cs])
    return host, (list(refs[n_in:o0]), list(refs[o0 + n_out:s0]), list(refs[len(refs) - cs:]))


def run_comm(comm, name):
    n_in, n_out = len(comm.ins), len(comm.out_shapes)

    def body(*refs):
        parts = (list(refs[:n_in]), list(refs[n_in:n_in + n_out]), list(refs[n_in + n_out:]))
        comm.first(*parts)
        if comm.mid is not None:
            comm.mid(*parts)
        comm.last(*parts)

    return pl.pallas_call(body, name=name, in_specs=[ANY] * n_in, out_specs=[ANY] * n_out,
                          out_shape=comm.out_shapes, scratch_shapes=comm.sems)(*comm.ins)


def hosted_call(work, comm, name, grid, in_specs, out_specs, out_shape, scratch_shapes, args, semantics):
    c_ins, c_outs, c_sems = _comm_operands(comm)
    n_steps = math.prod(grid)

    def body(*refs):
        host, c_refs = _comm_refs(comm, refs, len(in_specs), len(out_specs))
        step = 0
        for axis, size in enumerate(grid):
            step = step * size + pl.program_id(axis)
        _comm_begin(comm, c_refs, step, n_steps)
        work(*host)
        _comm_end(comm, c_refs, step, n_steps)

    return pl.pallas_call(
        body, name=name, grid=grid, in_specs=list(in_specs) + [ANY] * len(c_ins),
        out_specs=list(out_specs) + [ANY] * len(c_outs), out_shape=list(out_shape) + c_outs,
        scratch_shapes=list(scratch_shapes) + c_sems,
        compiler_params=_params(semantics if comm is None else ("arbitrary",) * len(grid)),
    )(*args, *c_ins)


def gather_comm(shards):
    n = len(shards)

    def copies(srcs, outs, sems):
        send_sems, recv_sems, local_sems = sems
        x, y, c = _position()
        me = 2 * x + y
        sibling = (x, y, 1 - c)
        chips = _other_chips(x, y)
        locals_ = [_later(pltpu.make_async_copy, s, o.at[me], local_sems.at[i])
                   for i, (s, o) in enumerate(zip(srcs, outs))]
        sends, arrived, passed, from_sibling = [], [], [], []
        for j, (px, py) in enumerate(chips):
            for i, (s, o) in enumerate(zip(srcs, outs)):
                rows = s.shape[0] // 2
                sends.append(_later(_remote, s.at[pl.ds(c * rows, rows), :], _half(o, me, c), send_sems.at[i, j],
                                    recv_sems.at[i, j], (px, py, c)))
                got = _half(o, 2 * px + py, c)
                arrived.append(_later(_remote, got, got, send_sems.at[i, j], recv_sems.at[i, j], (px, py, c)))
                passed.append(_later(_remote, got, got, send_sems.at[i, 3 + j], recv_sems.at[i, 3 + j], sibling))
                other = _half(o, 2 * px + py, 1 - c)
                from_sibling.append(_later(_remote, other, other, send_sems.at[i, 3 + j], recv_sems.at[i, 3 + j],
                                           sibling))
        return locals_, sends, arrived, passed, from_sibling

    return Comm(shards, [jax.ShapeDtypeStruct((N_CHIPS,) + s.shape, s.dtype) for s in shards],
                [pltpu.SemaphoreType.DMA((n, 6)), pltpu.SemaphoreType.DMA((n, 6)), pltpu.SemaphoreType.DMA((n,))],
                *_two_level_phases(copies))


HBM = pl.BlockSpec(memory_space=pltpu.HBM)
SEM = pl.BlockSpec(memory_space=pltpu.SEMAPHORE)
N_OTHER = N_CHIPS - 1


def _ici_halves(src_ref, land_ref, sems):
    x, y, c = _position()
    me = 2 * x + y
    rows = src_ref.shape[0] // 2
    sends, arrivals = [], []
    for j, (px, py) in enumerate(_other_chips(x, y)):
        sends.append(_later(_remote, src_ref.at[pl.ds(c * rows, rows), :], _half(land_ref, me, c), sems[j],
                            sems[N_OTHER + j], (px, py, c)))
        got = _half(land_ref, 2 * px + py, c)
        arrivals.append(_later(_remote, got, got, sems[j], sems[N_OTHER + j], (px, py, c)))
    return sends, arrivals


def gather_start(shard):
    def body(src_ref, land_ref, *rest):
        sems, token = rest[:2 * N_OTHER], rest[-1]
        for cp in _ici_halves(src_ref, land_ref, sems)[0]:
            cp().start()
        token[...] = jnp.zeros_like(token)

    sem = pltpu.SemaphoreType.DMA(())
    land = pltpu.HBM((N_CHIPS,) + shard.shape, shard.dtype)
    res = pl.pallas_call(
        body, name="w_in_gather_start", in_specs=(HBM, HBM),
        out_specs=(SEM,) * (2 * N_OTHER) + (HBM, HBM, pl.BlockSpec(memory_space=pltpu.VMEM)),
        out_shape=(sem,) * (2 * N_OTHER) + (pltpu.HBM(shard.shape, shard.dtype), land,
                                           jax.ShapeDtypeStruct((8, LANES), F32)),
        input_output_aliases={0: 2 * N_OTHER, 1: 2 * N_OTHER + 1},
        compiler_params=pltpu.CompilerParams(has_side_effects=pltpu.SideEffectType.DATAFLOW_SIDE_EFFECTING),
    )(pltpu.with_memory_space_constraint(shard, pltpu.HBM),
      pltpu.with_memory_space_constraint(lax.empty((N_CHIPS,) + shard.shape, shard.dtype), pltpu.HBM))
    return res[:2 * N_OTHER], res[2 * N_OTHER], res[2 * N_OTHER + 1], res[-1]


def gather_wait(sems, shard, land, after):
    def body(src_ref, land_ref, *rest):
        sem_refs = rest[:2 * N_OTHER]
        sends, arrivals = _ici_halves(src_ref, land_ref, sem_refs)
        for cp in sends:
            cp().wait_send()
        for cp in arrivals:
            cp().wait_recv()

    return pl.pallas_call(
        body, name="w_in_gather_wait", in_specs=(HBM, HBM) + (SEM,) * (2 * N_OTHER) + (ANY,) * len(after),
        out_specs=(HBM, HBM), out_shape=(pltpu.HBM(shard.shape, shard.dtype), pltpu.HBM(land.shape, land.dtype)),
        input_output_aliases={0: 0, 1: 1},
        compiler_params=pltpu.CompilerParams(has_side_effects=pltpu.SideEffectType.DATAFLOW_SIDE_EFFECTING),
    )(shard, land, *sems, *after)


def handover_comm(shard, land):
    def copies(srcs, outs, sems):
        (shard_ref, land_ref), (out_ref,), (send_sems, recv_sems, local_sems) = srcs, outs, sems
        x, y, c = _position()
        me = 2 * x + y
        sibling = (x, y, 1 - c)
        locals_ = [_later(pltpu.make_async_copy, shard_ref, out_ref.at[me], local_sems.at[N_OTHER])]
        sends, from_sibling = [], []
        for j, (px, py) in enumerate(_other_chips(x, y)):
            got, place = _half(land_ref, 2 * px + py, c), _half(out_ref, 2 * px + py, c)
            locals_.append(_later(pltpu.make_async_copy, got, place, local_sems.at[j]))
            sends.append(_later(_remote, got, place, send_sems.at[j], recv_sems.at[j], sibling))
            other = _half(out_ref, 2 * px + py, 1 - c)
            from_sibling.append(_later(_remote, other, other, send_sems.at[j], recv_sems.at[j], sibling))
        return locals_, sends, from_sibling

    def first(*refs):
        locals_, sends, _ = copies(*refs)
        for cp in locals_ + sends:
            cp().start()

    def last(*refs):
        locals_, sends, from_sibling = copies(*refs)
        for cp in from_sibling:
            cp().wait_recv()
        for cp in sends:
            cp().wait_send()
        for cp in locals_:
            cp().wait()

    return Comm([shard, land], [jax.ShapeDtypeStruct(land.shape, land.dtype)],
                [pltpu.SemaphoreType.DMA((N_OTHER,)), pltpu.SemaphoreType.DMA((N_OTHER,)),
                 pltpu.SemaphoreType.DMA((N_OTHER + 1,))], first, None, last)


def swap_comm(grads):
    n = len(grads)

    def copies(srcs, gots, sems):
        send_sems, recv_sems = sems
        x, y, c = _position()
        out = []
        for i, (s, o) in enumerate(zip(srcs, gots)):
            rows = s.shape[1] // 2
            out.append(_remote(s.at[:, pl.ds((1 - c) * rows, rows), :], o, send_sems.at[i], recv_sems.at[i],
                               (x, y, 1 - c)))
        return out

    def first(srcs, gots, sems):
        for cp in copies(srcs, gots, sems):
            cp.start()

    def last(srcs, gots, sems):
        for cp in copies(srcs, gots, sems):
            cp.wait()

    return Comm(grads, [jax.ShapeDtypeStruct((N_CHIPS, g.shape[1] // 2, g.shape[2]), g.dtype) for g in grads],
                [pltpu.SemaphoreType.DMA((n,)), pltpu.SemaphoreType.DMA((n,))], first, None, last)


def add_halves(name, g, got, core):
    _, half, cols = got.shape
    mine = pl.BlockSpec((None, half, cols), lambda k, c_ref: (k, c_ref[0], 0))
    other = pl.BlockSpec((None, half, cols), lambda k, c_ref: (k, 0, 0))

    def body(c_ref, g_ref, got_ref, o_ref):
        o_ref[...] = (g_ref[...] + got_ref[...]).astype(BF16)

    return pl.pallas_call(
        body, name="add_halves_" + name,
        grid_spec=pltpu.PrefetchScalarGridSpec(num_scalar_prefetch=1, grid=(N_CHIPS,), in_specs=[mine, other],
                                               out_specs=other),
        out_shape=jax.ShapeDtypeStruct(got.shape, BF16),
        compiler_params=_params(("parallel",)),
    )(core, g, got)


def exchange_comm(parts):
    n = len(parts)

    def copies(srcs, outs, sems):
        send_sems, recv_sems, local_sems = sems
        x, y, c = _position()
        me = 2 * x + y
        sibling = (x, y, 1 - c)
        chips = _other_chips(x, y)
        locals_, sends, arrived, passed, from_sibling = [], [], [], [], []
        for i, (s, o) in enumerate(zip(srcs, outs)):
            locals_.append(_later(pltpu.make_async_copy, s.at[me], _half(o, me, c), local_sems.at[i]))
            sends.append(_later(_remote, s.at[me], _half(o, me, c), send_sems.at[i, 3], recv_sems.at[i, 3], sibling))
            other = _half(o, me, 1 - c)
            from_sibling.append(_later(_remote, other, other, send_sems.at[i, 3], recv_sems.at[i, 3], sibling))
        for j, (px, py) in enumerate(chips):
            for i, (s, o) in enumerate(zip(srcs, outs)):
                sends.append(_later(_remote, s.at[2 * px + py], _half(o, me, c), send_sems.at[i, j],
                                    recv_sems.at[i, j], (px, py, c)))
                got = _half(o, 2 * px + py, c)
                arrived.append(_later(_remote, got, got, send_sems.at[i, j], recv_sems.at[i, j], (px, py, c)))
                passed.append(_later(_remote, got, got, send_sems.at[i, 4 + j], recv_sems.at[i, 4 + j], sibling))
                other = _half(o, 2 * px + py, 1 - c)
                from_sibling.append(_later(_remote, other, other, send_sems.at[i, 4 + j], recv_sems.at[i, 4 + j],
                                           sibling))
        return locals_, sends, arrived, passed, from_sibling

    return Comm(parts, [jax.ShapeDtypeStruct((N_CHIPS, 2 * p.shape[1], p.shape[2]), p.dtype) for p in parts],
                [pltpu.SemaphoreType.DMA((n, 7)), pltpu.SemaphoreType.DMA((n, 7)), pltpu.SemaphoreType.DMA((n,))],
                *_two_level_phases(copies))


def small_comm(shares):
    n = len(shares)

    def copies(srcs, outs, sems):
        send_sems, recv_sems, local_sems = sems
        x, y, c = _position()
        me = 4 * x + 2 * y + c
        flips = [(fx, fy, fc) for fx in (0, 1) for fy in (0, 1) for fc in (0, 1)][1:]
        peers = [(1 - x if fx else x, 1 - y if fy else y, 1 - c if fc else c) for fx, fy, fc in flips]
        locals_, sends, arrived = [], [], []
        for i, (src_ref, out_ref) in enumerate(zip(srcs, outs)):
            locals_.append(_later(pltpu.make_async_copy, src_ref, out_ref.at[me], local_sems.at[i]))
            for j, (px, py, pc) in enumerate(peers):
                sends.append(_later(_remote, src_ref, out_ref.at[me], send_sems.at[i, j], recv_sems.at[i, j],
                                    (px, py, pc)))
                got = out_ref.at[4 * px + 2 * py + pc]
                arrived.append(_later(_remote, got, got, send_sems.at[i, j], recv_sems.at[i, j], (px, py, pc)))
        return locals_, sends, arrived

    def first(*refs):
        locals_, sends, _ = copies(*refs)
        for cp in locals_ + sends:
            cp().start()

    def last(*refs):
        locals_, sends, arrived = copies(*refs)
        for cp in arrived:
            cp().wait_recv()
        for cp in sends:
            cp().wait_send()
        for cp in locals_:
            cp().wait()

    return Comm(shares, [jax.ShapeDtypeStruct((N_DEV,) + s.shape, s.dtype) for s in shares],
                [pltpu.SemaphoreType.DMA((n, 7)), pltpu.SemaphoreType.DMA((n, 7)), pltpu.SemaphoreType.DMA((n,))],
                first, None, last)


def _adam_fn(w, g, m, v):
    m = ADAM_B1 * m + (1.0 - ADAM_B1) * g
    v = ADAM_B2 * v + (1.0 - ADAM_B2) * jnp.square(g)
    m_hat = m / (1.0 - ADAM_B1 ** ADAM_STEP)
    v_hat = v / (1.0 - ADAM_B2 ** ADAM_STEP)
    return -ADAM_LR * (m_hat / (jnp.sqrt(v_hat) + ADAM_EPS) + ADAM_WD * w), m, v


def adam_big(name, parts, w, m, v):
    rows, cols = w.shape
    tm = _pick(rows, 384, 16)

    def fn(p0, p1, p2, p3, wv, mv, vv):
        g = ((p0.astype(F32) + p1.astype(F32)) + p2.astype(F32)) + p3.astype(F32)
        return (g,) + _adam_fn(wv, g, mv, vv)

    return rowwise(fn, [parts, w, m, v], [(cols, F32)] * 4, "adam_" + name, tm=tm, rows=rows)


def adam_small(name, gathered, w, m, v):
    def body(g_ref, w_ref, m_ref, v_ref, go_ref, d_ref, mo_ref, vo_ref):
        g = g_ref[0]
        for k in range(1, N_DEV):
            g = g + g_ref[k]
        go_ref[...] = g
        d_ref[...], mo_ref[...], vo_ref[...] = _adam_fn(w_ref[...], g, m_ref[...], v_ref[...])

    return pl.pallas_call(body, name=name, out_shape=[jax.ShapeDtypeStruct(w.shape, F32)] * 4,
                          compiler_params=_params())(gathered, w, m, v)


def _ssm_2d(name, t):
    t = t[0] if t.ndim > 2 else t
    if name in ("ssm_b_re", "ssm_b_im"):
        return t.transpose(0, 2, 1).reshape(SSM_W, 64)
    if name in ("ssm_c_re", "ssm_c_im"):
        return t.reshape(SSM_W, 64)
    return t.T if name == "ssm_d" else t


def _ssm_back(name, t):
    if name in ("ssm_b_re", "ssm_b_im"):
        return t.reshape(32, 16, 64).transpose(0, 2, 1)[None]
    if name in ("ssm_c_re", "ssm_c_im"):
        return t.reshape(1, 32, 16, 64)
    if name == "ssm_d":
        return t.T[None]
    return t if name == "ssm_log_dt" else t[None]


def adam_ssm(shares, w, m, v):
    n = len(w)

    def body(*refs):
        ins, outs = refs[:4 * n], refs[4 * n:]
        for i in range(n):
            g_ref, w_ref, m_ref, v_ref = (ins[k * n + i] for k in range(4))
            g = g_ref[0]
            for k in range(1, N_DEV):
                g = g + g_ref[k]
            outs[4 * i][...] = g
            outs[4 * i + 1][...], outs[4 * i + 2][...], outs[4 * i + 3][...] = _adam_fn(w_ref[...], g, m_ref[...],
                                                                                      v_ref[...])

    out_shape = [jax.ShapeDtypeStruct(t.shape, F32) for t in w for _ in range(4)]
    res = pl.pallas_call(body, name="adam_ssm", out_shape=out_shape, compiler_params=_params())(*shares, *w, *m, *v)
    return [res[4 * i:4 * i + 4] for i in range(n)]


def _pack_small(names, vals, rows, last=None):
    flat = [vals[n].reshape(-1) for n in names]
    if last is not None:
        flat.append(last.reshape(-1))
    flat = jnp.concatenate(flat)
    return jnp.pad(flat, (0, rows * LANES - flat.shape[0])).reshape(rows, LANES)


def _unpack_small(names, pack, shapes):
    flat, out, off = pack.reshape(-1), {}, 0
    for n in names:
        size = math.prod(shapes[n])
        out[n] = flat[off:off + size].reshape(shapes[n])
        off += size
    return out, flat[off]


def _to_slots(name, g, shard_shape):
    rows, cols = shard_shape
    if name in ROW_SHARDED:
        return g.reshape(N_CHIPS, rows, cols)
    return g.reshape(rows, N_CHIPS, cols).transpose(1, 0, 2)


def _from_slots(name, s):
    _, rows, cols = s.shape
    if name in ROW_SHARDED:
        return s.reshape(N_CHIPS * rows, cols)
    return s.transpose(1, 0, 2).reshape(rows, N_CHIPS * cols)


def kernel(x, norm_mix_g, w_in, ssm_a_re, ssm_a_im, ssm_log_dt, ssm_b_re, ssm_b_im, ssm_c_re, ssm_c_im, ssm_d, w_glu, w_attn_out, w_out, norm_ffn_g, w_ffn_gate, w_ffn_up, w_ffn_down, norm_final_g, loss_target, m_norm_mix_g, m_w_in, m_ssm_a_re, m_ssm_a_im, m_ssm_log_dt, m_ssm_b_re, m_ssm_b_im, m_ssm_c_re, m_ssm_c_im, m_ssm_d, m_w_glu, m_w_attn_out, m_w_out, m_norm_ffn_g, m_w_ffn_gate, m_w_ffn_up, m_w_ffn_down, m_norm_final_g, v_norm_mix_g, v_w_in, v_ssm_a_re, v_ssm_a_im, v_ssm_log_dt, v_ssm_b_re, v_ssm_b_im, v_ssm_c_re, v_ssm_c_im, v_ssm_d, v_w_glu, v_w_attn_out, v_w_out, v_norm_ffn_g, v_w_ffn_gate, v_w_ffn_up, v_w_ffn_down, v_norm_final_g):
    given = dict(locals())
    def local(name, prefix=""):
        t = given[prefix + name][0]
        return t.T if name in TRANSPOSED else t

    shard = {n: local(n) for n in BIG}
    shapes = {n: given[n].shape for n in WEIGHTS}

    small = {n: given[n] for n in SMALL}
    small_2d = dict(small)
    for n in ("ssm_a_re", "ssm_a_im", "ssm_b_re", "ssm_b_im", "ssm_c_re", "ssm_c_im", "ssm_d"):
        small_2d[n] = small[n][0]
    small_2d["norm_final_g"] = norm_final_g.reshape(1, D_MODEL)

    core = lax.axis_index("c").astype(jnp.int32).reshape(1)
    loss, grad_x, parts, ssm_shares, gs_norm = local_step(
        x.reshape(TOKENS, D_MODEL), loss_target.reshape(TOKENS, D_MODEL),
        {n: shard[n] for n in BIG}, small_2d, core)

    (norm_shares,) = run_comm(small_comm([_pack_small(NORM_SMALL, gs_norm, NORM_ROWS, last=loss)]),
                              "gather_norm_grads")
    small_out = [{} for _ in range(4)]
    packs = [_pack_small(NORM_SMALL, {n: given[p + n] for n in NORM_SMALL}, NORM_ROWS) for p in ("", "m_", "v_")]
    for kind, t in enumerate(adam_small("adam_norm_gains", norm_shares, *packs)):
        vals, after = _unpack_small(NORM_SMALL, t, shapes)
        small_out[kind].update(vals)
        if kind == 0:
            total_loss = after
    ssm_in = [[_ssm_2d(n, given[p + n]) for n in SSM_SMALL] for p in ("", "m_", "v_")]
    for n, res in zip(SSM_SMALL, adam_ssm(ssm_shares, *ssm_in)):
        for kind, t in enumerate(res):
            small_out[kind][n] = _ssm_back(n, t)

    big_out = {}
    for n in BIG:
        res = adam_big(n, parts[n], shard[n], local(n, "m_"), local(n, "v_"))
        big_out[n] = [(t.T if n in TRANSPOSED else t)[None] for t in res]

    outs = [total_loss, grad_x.reshape(LOCAL_BATCH, SEQ, D_MODEL)]
    for kind in range(4):
        for n in WEIGHTS:
            outs.append(big_out[n][kind] if n in BIG else small_out[kind][n])
    return tuple(outs)
```

```python
import functools
import math

import jax
import jax.numpy as jnp
import numpy as np
from jax import lax
from jax.experimental import pallas as pl
from jax.experimental.pallas import tpu as pltpu

F32 = jnp.float32
BF16 = jnp.bfloat16
MESH = pl.DeviceIdType.MESH

D_MODEL = 1024
SEQ = 2048
LOCAL_BATCH = 2
TOKENS = LOCAL_BATCH * SEQ
HEAD_DIM = 64
HEADS_PER_GROUP = 4
GROUP_W = HEADS_PER_GROUP * HEAD_DIM
N_GROUPS = 3
DILATIONS = (1, 4, 16)
ATTN_BLOCK = 128
ROPE_DIM = 16
ROPE_THETA = 500000.0
QKV_W = 3 * N_GROUPS * GROUP_W
SSM_W = 512
SSM_STATE_W = 2048
SSM_LANE_BLOCKS = 4
GATE_W = 2 * D_MODEL
D_FF = 2816
RMS_EPS = 1e-6
NEG_INF = -1e30
ADAM_LR, ADAM_B1, ADAM_B2, ADAM_EPS, ADAM_WD, ADAM_STEP = 0.001, 0.9, 0.999, 1e-08, 0.01, 10
N_CHIPS = 4
N_DEV = 8

VMEM_LIMIT = 56 * 1024 * 1024
LANES = 128


def _params(sem=None):
    return pltpu.CompilerParams(dimension_semantics=sem, vmem_limit_bytes=VMEM_LIMIT)


def _pick(n, cap, align=LANES):
    best = None
    for d in range(align, min(n, cap) + 1, align):
        if n % d == 0:
            best = d
    return n if best is None or n <= cap else best


_DIMS = {"nn": (((1,), (0,)), ((), ())), "nt": (((1,), (1,)), ((), ())), "tn": (((0,), (0,)), ((), ()))}


def _dot(a, b, mode):
    return lax.dot_general(a, b, _DIMS[mode], preferred_element_type=F32)


def matmul(a, b, mode, out_dtype, name, add=None, comm=None):
    if mode == "nn":
        (m, k), n = a.shape, b.shape[1]
    elif mode == "nt":
        (m, k), n = a.shape, b.shape[0]
    else:
        (k, m), n = a.shape, b.shape[1]
    tn = _pick(n, 1408 if mode != "tn" else 512)
    tk = _pick(k, 2816) if mode != "tn" else k
    tm = _pick(m, 1408)
    out_bytes = jnp.dtype(out_dtype).itemsize

    def need(tm_):
        return 2 * 2 * (tm_ * tk + tk * tn) + tm_ * tn * (4 + 2 * out_bytes + (8 if add is not None else 0))

    while need(tm) > 40 * 1024 * 1024 and tm % 256 == 0:
        tm //= 2
    nk = k // tk
    a_spec = {"nn": pl.BlockSpec((tm, tk), lambda i, j, kk: (i, kk)),
              "nt": pl.BlockSpec((tm, tk), lambda i, j, kk: (i, kk)),
              "tn": pl.BlockSpec((tk, tm), lambda i, j, kk: (kk, i))}[mode]
    b_spec = {"nn": pl.BlockSpec((tk, tn), lambda i, j, kk: (kk, j)),
              "nt": pl.BlockSpec((tn, tk), lambda i, j, kk: (j, kk)),
              "tn": pl.BlockSpec((tk, tn), lambda i, j, kk: (kk, j))}[mode]
    o_spec = pl.BlockSpec((tm, tn), lambda i, j, kk: (i, j))

    def body(a_ref, b_ref, *rest):
        if add is not None:
            add_ref, o_ref, acc_ref = rest
        else:
            o_ref, acc_ref = rest
        part = _dot(a_ref[...], b_ref[...], mode)
        if nk == 1:
            res = part if add is None else part + add_ref[...]
            o_ref[...] = res.astype(out_dtype)
            return
        kk = pl.program_id(2)

        @pl.when(kk == 0)
        def _():
            acc_ref[...] = part

        @pl.when(kk > 0)
        def _():
            acc_ref[...] += part

        @pl.when(kk == nk - 1)
        def _():
            res = acc_ref[...] if add is None else acc_ref[...] + add_ref[...]
            o_ref[...] = res.astype(out_dtype)

    in_specs = [a_spec, b_spec] + ([o_spec] if add is not None else [])
    args = (a, b) + ((add,) if add is not None else ())
    res = hosted_call(
        body, comm, name, (m // tm, n // tn, nk), in_specs, [o_spec], [jax.ShapeDtypeStruct((m, n), out_dtype)],
        [pltpu.VMEM((tm, tn) if nk > 1 else (8, LANES), F32)], args, ("parallel", "parallel", "arbitrary"))
    return res[0] if comm is None else res


def matmul_rows(a, b, name, fn, extra, outs, accs=(), add=None, comm=None, tm=512):
    (m, k), n = a.shape, b.shape[1]
    n_fixed = 2 + (add is not None)
    row_spec = lambda cols: pl.BlockSpec((tm, cols), lambda i: (i, 0))
    in_specs = [row_spec(k), pl.BlockSpec((k, n), lambda i: (0, 0))] + ([row_spec(n)] if add is not None else [])
    in_specs += [pl.BlockSpec(e.shape, lambda i: (0, 0)) if e.shape[0] == 1 else row_spec(e.shape[1]) for e in extra]
    out_specs = [row_spec(c) for c, _ in outs] + [pl.BlockSpec((1, c), lambda i: (0, 0)) for c in accs]
    out_shape = [jax.ShapeDtypeStruct((m, c), dt) for c, dt in outs] + [jax.ShapeDtypeStruct((1, c), F32) for c in accs]

    def body(*refs):
        rows = _dot(refs[0][...], refs[1][...], "nn")
        if add is not None:
            rows = rows + refs[2][...]
        n_in = n_fixed + len(extra)
        res = fn(rows, *[r[...] for r in refs[n_fixed:n_in]])
        for r, v in zip(refs[n_in:n_in + len(outs)], res[:len(outs)]):
            r[...] = v.astype(r.dtype)
        first = pl.program_id(0) == 0
        for r, v in zip(refs[n_in + len(outs):], res[len(outs):]):
            @pl.when(first)
            def _(r=r, v=v):
                r[...] = v

            @pl.when(jnp.logical_not(first))
            def _(r=r, v=v):
                r[...] += v

    args = (a, b) + ((add,) if add is not None else ()) + tuple(extra)
    return hosted_call(body, comm, name, (m // tm,), in_specs, out_specs, out_shape, [], args, ("arbitrary",))


FFN_TM, FFN_TN = 512, 1408


def ffn_in(h2, wg_t, wu_t, comm=None):
    def body(h_ref, wg_ref, wu_ref, a_ref, b_ref, act_ref):
        hv = h_ref[...]
        a, b = _dot(hv, wg_ref[...], "nt"), _dot(hv, wu_ref[...], "nt")
        a_ref[...] = a.astype(BF16)
        b_ref[...] = b.astype(BF16)
        act_ref[...] = _swiglu_fn(a, b).astype(BF16)

    rows = pl.BlockSpec((FFN_TM, D_MODEL), lambda i, j: (i, 0))
    wts = pl.BlockSpec((FFN_TN, D_MODEL), lambda i, j: (j, 0))
    out = pl.BlockSpec((FFN_TM, FFN_TN), lambda i, j: (i, j))
    return hosted_call(body, comm, "ffn_in", (TOKENS // FFN_TM, D_FF // FFN_TN), [rows, wts, wts], [out] * 3,
                       [jax.ShapeDtypeStruct((TOKENS, D_FF), BF16)] * 3, [], (h2, wg_t, wu_t),
                       ("parallel", "parallel"))


def ffn_in_bwd(dx2_b, wd, a, b):
    def body(dx_ref, wd_ref, a_ref, b_ref, da_ref, db_ref):
        dx = dx_ref[...]
        for lo in range(0, FFN_TN, 512):
            cols = slice(lo, min(lo + 512, FFN_TN))
            dact = _dot(dx, wd_ref[cols, :], "nt")
            _, vjp = jax.vjp(_swiglu_fn, a_ref[:, cols].astype(F32), b_ref[:, cols].astype(F32))
            da, db = vjp(dact)
            da_ref[:, cols] = da.astype(BF16)
            db_ref[:, cols] = db.astype(BF16)

    rows = pl.BlockSpec((FFN_TM, D_MODEL), lambda i, j: (i, 0))
    wts = pl.BlockSpec((FFN_TN, D_MODEL), lambda i, j: (j, 0))
    out = pl.BlockSpec((FFN_TM, FFN_TN), lambda i, j: (i, j))
    return pl.pallas_call(
        body, name="ffn_in_bwd", grid=(TOKENS // FFN_TM, D_FF // FFN_TN), in_specs=[rows, wts, out, out],
        out_specs=[out] * 2, out_shape=[jax.ShapeDtypeStruct((TOKENS, D_FF), BF16)] * 2,
        compiler_params=_params(("parallel", "parallel")),
    )(dx2_b, wd, a, b)


def mix_in_bwd(grads, weights, partial, x, g, skip, comm=None):
    n = len(grads)
    tm = 512

    def body(*refs):
        a_refs, b_refs = refs[:n], refs[n:2 * n]
        part_ref, x_ref, g_ref, skip_ref, gx_ref, dg_ref = refs[2 * n:]
        dh = part_ref[...]
        for a_ref, b_ref in zip(a_refs, b_refs):
            dh = dh + _dot(a_ref[...], b_ref[...], "nn")
        _, vjp = jax.vjp(_rms, x_ref[...], g_ref[...])
        dx, dg = vjp(dh)
        gx_ref[...] = dx + skip_ref[...]
        first = pl.program_id(0) == 0

        @pl.when(first)
        def _():
            dg_ref[...] = dg

        @pl.when(jnp.logical_not(first))
        def _():
            dg_ref[...] += dg

    rows = pl.BlockSpec((tm, D_MODEL), lambda i: (i, 0))
    gain = pl.BlockSpec((1, D_MODEL), lambda i: (0, 0))
    in_specs = [pl.BlockSpec((tm, a.shape[1]), lambda i: (i, 0)) for a in grads]
    in_specs += [pl.BlockSpec(b.shape, lambda i: (0, 0)) for b in weights]
    return hosted_call(
        body, comm, "mix_in_bwd", (TOKENS // tm,), in_specs + [rows, rows, gain, rows], [rows, gain],
        [jax.ShapeDtypeStruct((TOKENS, D_MODEL), F32), jax.ShapeDtypeStruct((1, D_MODEL), F32)], [],
        (*grads, *weights, partial, x, g, skip), ("arbitrary",))


def rowwise(fn, ins, outs, name, accs=(), tm=256, rows=TOKENS, comm=None):
    in_specs, args = [], []
    for item in ins:
        arr, width, blk = item if isinstance(item, tuple) else (item, None, 0)
        if arr.ndim == 3:
            for k in range(arr.shape[0]):
                in_specs.append(pl.BlockSpec((None, tm, arr.shape[2]), functools.partial(lambda i, k_: (k_, i, 0), k_=k)))
                args.append(arr)
            continue
        if arr.shape[0] == 1:
            in_specs.append(pl.BlockSpec(arr.shape, lambda i: (0, 0)))
        elif width is None:
            in_specs.append(pl.BlockSpec((tm, arr.shape[1]), lambda i: (i, 0)))
        else:
            in_specs.append(pl.BlockSpec((tm, width), functools.partial(lambda i, blk_: (i, blk_), blk_=blk)))
        args.append(arr)
    out_specs = [pl.BlockSpec((tm, c), lambda i: (i, 0)) for c, _ in outs]
    out_specs += [pl.BlockSpec((1, c), lambda i: (0, 0)) for c in accs]
    out_shape = [jax.ShapeDtypeStruct((rows, c), dt) for c, dt in outs]
    out_shape += [jax.ShapeDtypeStruct((1, c), F32) for c in accs]
    n_in, n_out = len(args), len(outs)
    c_ins, c_outs, c_sems = _comm_operands(comm)

    def body(*refs):
        refs, c_refs = _comm_refs(comm, refs, n_in, n_out + len(accs))
        step = pl.program_id(0)
        _comm_begin(comm, c_refs, step, rows // tm)
        res = fn(*[r[...] for r in refs[:n_in]])
        for r, v in zip(refs[n_in:n_in + n_out], res[:n_out]):
            r[...] = v.astype(r.dtype)
        first = step == 0
        for r, v in zip(refs[n_in + n_out:], res[n_out:]):
            @pl.when(first)
            def _(r=r, v=v):
                r[...] = v

            @pl.when(jnp.logical_not(first))
            def _(r=r, v=v):
                r[...] += v
        _comm_end(comm, c_refs, step, rows // tm)

    return pl.pallas_call(
        body, name=name, grid=(rows // tm,), in_specs=in_specs + [ANY] * len(c_ins),
        out_specs=out_specs + [ANY] * len(c_outs), out_shape=out_shape + c_outs, scratch_shapes=c_sems,
        compiler_params=_params(("arbitrary",)),
    )(*args, *c_ins)


def first_norm(x, g, others, comm=None):
    tm, n = 256, len(others)

    def body(x_ref, g_ref, *rest):
        srcs, h_ref, dsts = rest[:n], rest[n], rest[n + 1:]
        h_ref[...] = _rms(x_ref[...], g_ref[...]).astype(BF16)
        for k, (s, d) in enumerate(zip(srcs, dsts)):
            @pl.when(pl.program_id(0) == k)
            def _(s=s, d=d):
                d[...] = s[...].astype(BF16)

    rows = pl.BlockSpec((tm, D_MODEL), lambda i: (i, 0))
    whole = [pl.BlockSpec(a.shape, lambda i: (0, 0)) for a in others]
    return hosted_call(
        body, comm, "norm_mix", (TOKENS // tm,), [rows, pl.BlockSpec((1, D_MODEL), lambda i: (0, 0))] + whole,
        [rows] + whole, [jax.ShapeDtypeStruct((TOKENS, D_MODEL), BF16)]
        + [jax.ShapeDtypeStruct(a.shape, BF16) for a in others], [], (x, g, *others), ("arbitrary",))


def _rms(x, g):
    return x * lax.rsqrt(jnp.mean(x * x, axis=-1, keepdims=True) + RMS_EPS) * g


def _colsum(v):
    return jnp.sum(v, axis=0, keepdims=True)


PAIR_W = 2 * HEAD_DIM
N_PAIRS = HEADS_PER_GROUP // 2


def _qkv_order(w_t, back=False):
    dims = (N_PAIRS, N_GROUPS, 3) if back else (3, N_GROUPS, N_PAIRS)
    return w_t.reshape(dims + (PAIR_W, w_t.shape[1])).transpose(2, 1, 0, 3, 4).reshape(QKV_W, w_t.shape[1])


def _rope_tables():
    half = ROPE_DIM // 2
    inv = np.power(np.float32(ROPE_THETA), -np.arange(half, dtype=np.float32) * np.float32(2.0 / ROPE_DIM))
    ang = (np.arange(SEQ, dtype=np.float32)[:, None] * inv[None, :]).astype(np.float32)
    cos, sin = np.cos(ang), np.sin(ang)
    zeros = np.zeros((SEQ, HEAD_DIM - ROPE_DIM), np.float32)
    zh = np.zeros((SEQ, half), np.float32)
    c = np.concatenate([cos, cos, zeros + 1.0], axis=1)
    sa = np.concatenate([-sin, zh, zeros], axis=1)
    sb = np.concatenate([zh, sin, zeros], axis=1)
    return [jnp.asarray(np.tile(t, (1, 2)), F32) for t in (c, sa, sb)]


def _rope_fwd(x, c, sa, sb):
    return x * c + pltpu.roll(x, PAIR_W - 8, 1) * sa + pltpu.roll(x, 8, 1) * sb


def _rope_bwd(dy, c, sa, sb):
    return dy * c + pltpu.roll(dy * sb, PAIR_W - 8, 1) + pltpu.roll(dy * sa, 8, 1)


def _band_masks():
    row = lax.broadcasted_iota(jnp.int32, (ATTN_BLOCK, ATTN_BLOCK), 0)
    col = lax.broadcasted_iota(jnp.int32, (ATTN_BLOCK, ATTN_BLOCK), 1)
    return col <= row, col >= row


def _stack_rows(t):
    return jnp.concatenate([t, t], axis=0)


def _stack_heads(t, first_head):
    return jnp.concatenate([jnp.where(first_head, t, 0), jnp.where(first_head, 0, t)], axis=0)


def _per_head(fn):
    return jnp.concatenate([fn(slice(h * HEAD_DIM, (h + 1) * HEAD_DIM)) for h in range(2)], axis=1)


def _slab_spec(kind):
    return pl.BlockSpec((None, SEQ, PAIR_W), lambda b, p, g: (b, 0, p * 3 * N_GROUPS + g * 3 + kind))


_TABLE_SPEC = pl.BlockSpec((SEQ, PAIR_W), lambda b, p, g: (0, 0))
_PAIR_SPEC = pl.BlockSpec((None, SEQ, PAIR_W), lambda b, p, g: (b, 0, p))


def _block_rows(dil, r, n):
    return pl.ds(n * (ATTN_BLOCK * dil) + r, ATTN_BLOCK, stride=dil)


def proj_qkv(h, w_qkv_t, tables, comm=None):
    tm = 1024
    pair_w = QKV_W // N_PAIRS
    scale = HEAD_DIM ** -0.5

    def body(h_ref, w_ref, c_ref, sa_ref, sb_ref, o_ref):
        rows = _dot(h_ref[...], w_ref[...], "nt")
        c, sa, sb = c_ref[...], sa_ref[...], sb_ref[...]
        for blk in range(pair_w // PAIR_W):
            cols = slice(blk * PAIR_W, (blk + 1) * PAIR_W)
            x = rows[:, cols]
            if blk % 3 == 0:
                x = _rope_fwd(x, c, sa, sb) * scale
            elif blk % 3 == 1:
                x = _rope_fwd(x, c, sa, sb)
            o_ref[:, cols] = x

    table = pl.BlockSpec((tm, PAIR_W), lambda i, j, : (i % (SEQ // tm), 0))
    res = hosted_call(
        body, comm, "proj_qkv", (TOKENS // tm, N_PAIRS),
        [pl.BlockSpec((tm, D_MODEL), lambda i, j: (i, 0)), pl.BlockSpec((pair_w, D_MODEL), lambda i, j: (j, 0)),
         table, table, table],
        [pl.BlockSpec((tm, pair_w), lambda i, j: (i, j))], [jax.ShapeDtypeStruct((TOKENS, QKV_W), F32)], [],
        (h, w_qkv_t, *tables), ("parallel", "parallel"))
    return res[0] if comm is None else res


def attn_fwd(qkv, comm=None):
    def body(qs, ks, v_ref, attn_b_ref, attn_ref, lse_ref, o0, o1, o2, l0, l1, l2):
        g = pl.program_id(2)
        cur_mask, prev_mask = _band_masks()
        first_head = lax.broadcasted_iota(jnp.int32, (ATTN_BLOCK, PAIR_W), 1) < HEAD_DIM

        def run(dil, o_slab, l_slab):
            nb = SEQ // dil // ATTN_BLOCK

            def block(idx, carry):
                r, n = lax.div(idx, nb), lax.rem(idx, nb)
                cur, prev = _block_rows(dil, r, n), _block_rows(dil, r, jnp.maximum(n - 1, 0))
                q = qs[cur, :].astype(BF16)
                kc, kp = ks[cur, :].astype(BF16), ks[prev, :].astype(BF16)
                vc, vp = v_ref[cur, :].astype(BF16), v_ref[prev, :].astype(BF16)
                q2 = _stack_heads(q, first_head)
                mask = _stack_rows(jnp.concatenate([jnp.logical_and(prev_mask, n > 0), cur_mask], axis=1))
                s2 = jnp.where(mask, _dot(q2, jnp.concatenate([kp, kc], axis=0), "nt"), NEG_INF)
                m = jnp.max(s2, axis=-1, keepdims=True)
                vcat, two = jnp.concatenate([vp, vc], axis=0), _stack_rows(first_head)
                vext = jnp.concatenate([jnp.where(two, vcat, 1), jnp.where(two, 1, vcat)], axis=1)
                r2 = _dot(jnp.exp(s2 - m).astype(BF16), vext, "nn")
                r0, r1 = r2[:ATTN_BLOCK, :PAIR_W], r2[ATTN_BLOCK:, PAIR_W:]
                num = jnp.where(first_head, r0, r1)
                den = pltpu.roll(jnp.where(first_head, r1, r0), HEAD_DIM, 1)
                o_slab[cur, :] = num / den
                l_slab[cur, :] = jnp.where(first_head, m[:ATTN_BLOCK], m[ATTN_BLOCK:]) + jnp.log(den)
                return carry

            lax.fori_loop(0, SEQ // ATTN_BLOCK, block, 0, unroll=4)

        for gi, (o_slab, l_slab) in enumerate(((o0, l0), (o1, l1), (o2, l2))):
            @pl.when(g == gi)
            def _(gi=gi, o_slab=o_slab, l_slab=l_slab):
                run(DILATIONS[gi], o_slab, l_slab)

        @pl.when(g == N_GROUPS - 1)
        def _():
            a, b, cc = l0[...], l1[...], l2[...]
            m = jnp.maximum(jnp.maximum(a, b), cc)
            e0, e1, e2 = jnp.exp(a - m), jnp.exp(b - m), jnp.exp(cc - m)
            tot = e0 + e1 + e2
            attn = (e0 * o0[...] + e1 * o1[...] + e2 * o2[...]) / tot
            attn_ref[...] = attn
            attn_b_ref[...] = attn.astype(BF16)
            lse_ref[...] = m + jnp.log(tot)

    shape = (LOCAL_BATCH, SEQ, GROUP_W)
    slab = pltpu.VMEM((SEQ, PAIR_W), F32)
    return hosted_call(
        body, comm, "attn_fwd", (LOCAL_BATCH, N_PAIRS, N_GROUPS),
        [_slab_spec(0), _slab_spec(1), _slab_spec(2)], [_PAIR_SPEC] * 3,
        [jax.ShapeDtypeStruct(shape, BF16), jax.ShapeDtypeStruct(shape, F32), jax.ShapeDtypeStruct(shape, F32)],
        [slab] * 6, (qkv, qkv, qkv), ("parallel", "parallel", "arbitrary"))


def attn_bwd(qkv, tables, dattn, attn, lse, comm=None):
    scale = HEAD_DIM ** -0.5

    def body(qs, ks, v_ref, c_ref, sa_ref, sb_ref, do_ref, out_ref, lse_ref, dqkv_ref, dl, dq_s, dk_s, dv_s):
        g = pl.program_id(2)
        c, sa, sb = c_ref[...], sa_ref[...], sb_ref[...]

        @pl.when(g == 0)
        def _():
            prod = do_ref[...] * out_ref[...]
            dl[...] = _per_head(
                lambda sl: jnp.broadcast_to(jnp.sum(prod[:, sl], axis=-1, keepdims=True), (SEQ, HEAD_DIM)))

        cur_mask, prev_mask = _band_masks()
        first_head = lax.broadcasted_iota(jnp.int32, (ATTN_BLOCK, PAIR_W), 1) < HEAD_DIM

        def run(dil):
            nb = SEQ // dil // ATTN_BLOCK

            def block(idx, carry):
                r, n = lax.div(idx, nb), lax.rem(idx, nb)
                cur = _block_rows(dil, r, n)
                prev = _block_rows(dil, r, jnp.maximum(n - 1, 0))
                nxt = _block_rows(dil, r, jnp.minimum(n + 1, nb - 1))
                q0, q1 = qs[cur, :].astype(BF16), qs[nxt, :].astype(BF16)
                kp, kc = ks[prev, :].astype(BF16), ks[cur, :].astype(BF16)
                vp, vc = v_ref[prev, :].astype(BF16), v_ref[cur, :].astype(BF16)
                do0, do1 = do_ref[cur, :].astype(BF16), do_ref[nxt, :].astype(BF16)
                lse0, lse1, dl0, dl1 = lse_ref[cur, :], lse_ref[nxt, :], dl[cur, :], dl[nxt, :]
                has_prev = jnp.logical_and(prev_mask, n > 0)
                has_next = jnp.logical_and(prev_mask, n < nb - 1)

                def per_row(t):
                    return jnp.concatenate([t[:, 0:1], t[:, HEAD_DIM:HEAD_DIM + 1]], axis=0)

                q20, q21 = _stack_heads(q0, first_head), _stack_heads(q1, first_head)
                do20, do21 = _stack_heads(do0, first_head), _stack_heads(do1, first_head)
                kcat, vcat = jnp.concatenate([kp, kc], axis=0), jnp.concatenate([vp, vc], axis=0)
                mask0 = _stack_rows(jnp.concatenate([has_prev, cur_mask], axis=1))
                p0 = jnp.where(mask0, jnp.exp(_dot(q20, kcat, "nt") - per_row(lse0)), 0.0)
                ds0 = (p0 * (_dot(do20, vcat, "nt") - per_row(dl0))).astype(BF16)
                p1 = jnp.where(_stack_rows(has_next), jnp.exp(_dot(q21, kc, "nt") - per_row(lse1)), 0.0)
                ds1 = (p1 * (_dot(do21, vc, "nt") - per_row(dl1))).astype(BF16)
                dq2 = _dot(ds0, kcat, "nn")
                dq_s[cur, :] = jnp.where(first_head, dq2[:ATTN_BLOCK], dq2[ATTN_BLOCK:])
                ds_cur = jnp.concatenate([ds0[:, ATTN_BLOCK:], ds1], axis=0)
                p_cur = jnp.concatenate([p0[:, ATTN_BLOCK:], p1], axis=0).astype(BF16)
                dk_s[cur, :] = _dot(ds_cur, jnp.concatenate([q20, q21], axis=0), "tn")
                dv_s[cur, :] = _dot(p_cur, jnp.concatenate([do20, do21], axis=0), "tn")
                return carry

            lax.fori_loop(0, SEQ // ATTN_BLOCK, block, 0, unroll=2)

        for gi in range(N_GROUPS):
            @pl.when(g == gi)
            def _(gi=gi):
                run(DILATIONS[gi])

        dqkv_ref[:, 0:PAIR_W] = _rope_bwd(dq_s[...] * scale, c, sa, sb).astype(BF16)
        dqkv_ref[:, PAIR_W:2 * PAIR_W] = _rope_bwd(dk_s[...], c, sa, sb).astype(BF16)
        dqkv_ref[:, 2 * PAIR_W:] = dv_s[...].astype(BF16)

    slab = pltpu.VMEM((SEQ, PAIR_W), F32)
    return hosted_call(
        body, comm, "attn_bwd", (LOCAL_BATCH, N_PAIRS, N_GROUPS),
        [_slab_spec(0), _slab_spec(1), _slab_spec(2), _TABLE_SPEC, _TABLE_SPEC, _TABLE_SPEC,
         _PAIR_SPEC, _PAIR_SPEC, _PAIR_SPEC],
        [pl.BlockSpec((None, SEQ, 3 * PAIR_W), lambda b, p, g: (b, 0, p * N_GROUPS + g))],
        [jax.ShapeDtypeStruct((LOCAL_BATCH, SEQ, QKV_W), BF16)],
        [slab] * 4, (qkv, qkv, qkv, *tables, dattn, attn, lse), ("parallel", "parallel", "arbitrary"))


def _discretize(lr, li, log_dt, br, bi):
    dt = jnp.exp(log_dt)
    mag = jnp.exp(lr * dt)
    ab_re, ab_im = mag * jnp.cos(li * dt), mag * jnp.sin(li * dt)
    den = lr * lr + li * li
    nr, ni = ab_re - 1.0, ab_im
    f_re = (nr * lr + ni * li) / den
    f_im = (ni * lr - nr * li) / den
    return ab_re, ab_im, f_re[None] * br - f_im[None] * bi, f_re[None] * bi + f_im[None] * br


def ssm_prep(lr, li, log_dt, br, bi):
    def body(lr_ref, li_ref, dt_ref, br_ref, bi_ref, *outs):
        for o, v in zip(outs, _discretize(lr_ref[...], li_ref[...], dt_ref[...], br_ref[...], bi_ref[...])):
            o[...] = v
    shapes = [lr, li, br, bi]
    return pl.pallas_call(body, name="ssm_prep",
                          out_shape=[jax.ShapeDtypeStruct(s.shape, F32) for s in shapes])(lr, li, log_dt, br, bi)


def ssm_prep_bwd(lr, li, log_dt, br, bi, g_ab_re, g_ab_im, g_bb_re, g_bb_im):
    def body(lr_ref, li_ref, dt_ref, br_ref, bi_ref, g0, g1, g2, g3, *outs):
        _, vjp = jax.vjp(_discretize, lr_ref[...], li_ref[...], dt_ref[...], br_ref[...], bi_ref[...])
        for o, v in zip(outs, vjp((g0[...], g1[...], g2[...], g3[...]))):
            o[...] = v
    shapes = [lr, li, log_dt, br, bi]
    return pl.pallas_call(body, name="ssm_prep_bwd",
                          out_shape=[jax.ShapeDtypeStruct(s.shape, F32) for s in shapes])(
        lr, li, log_dt, br, bi, g_ab_re, g_ab_im, g_bb_re, g_bb_im)


def _block_diag(t):
    per = SSM_STATE_W // SSM_LANE_BLOCKS // 64
    g = t.transpose(1, 0, 2).reshape(SSM_LANE_BLOCKS, per, 16, 64)
    eye = jnp.eye(per, dtype=t.dtype)
    return jnp.einsum("jgcn,gh->jgchn", g, eye).reshape(SSM_LANE_BLOCKS, per * 16, per * 64)


def _block_diag_t(m):
    per = SSM_STATE_W // SSM_LANE_BLOCKS // 64
    m5 = m.reshape(SSM_LANE_BLOCKS, per, 16, per, 64)
    d = jnp.einsum("jgchn,gh->jgcn", m5, jnp.eye(per, dtype=m.dtype))
    return d.reshape(SSM_LANE_BLOCKS * per, 16, 64).transpose(1, 0, 2)


def _cmul(ar, ai, br, bi):
    return ar * br - ai * bi, ar * bi + ai * br


def _power_tables(ar, ai, reverse):
    width = ar.shape[1]
    row = lax.broadcasted_iota(jnp.int32, (8, width), 0)
    pows = [(ar, ai)]
    for _ in range(7):
        pows.append(_cmul(pows[-1][0], pows[-1][1], ar, ai))
    steps = []
    for k in (1, 2, 4):
        keep = (row >= k) if not reverse else (row < 8 - k)
        steps.append((jnp.where(keep, pows[k - 1][0], 0.0), jnp.where(keep, pows[k - 1][1], 0.0)))
    cr = jnp.zeros((8, width), F32)
    ci = jnp.zeros((8, width), F32)
    for i in range(8):
        pr, pi = pows[i] if not reverse else pows[7 - i]
        cr = jnp.where(row == i, pr, cr)
        ci = jnp.where(row == i, pi, ci)
    return steps, (cr, ci)


SCAN_CHUNK = 2048
STATE_BLOCK = SSM_STATE_W // SSM_LANE_BLOCKS
CHAN_BLOCK = SSM_W // SSM_LANE_BLOCKS


def ssm_fwd(u, ab_re, ab_im, bb_re, bb_im, cb_re, cb_im, d_skip, comm=None):
    nt = SEQ // SCAN_CHUNK
    chan = pl.BlockSpec((None, SCAN_CHUNK, CHAN_BLOCK), lambda b, j, t: (b, t, j))
    state = pl.BlockSpec((None, SCAN_CHUNK, STATE_BLOCK), lambda b, j, t: (b, t, j))
    mat = pl.BlockSpec((None, CHAN_BLOCK, STATE_BLOCK), lambda b, j, t: (j, 0, 0))
    lane = pl.BlockSpec((1, STATE_BLOCK), lambda b, j, t: (0, j))
    dsp = pl.BlockSpec((1, CHAN_BLOCK), lambda b, j, t: (0, j))

    def body(u_ref, ar_ref, ai_ref, bbr_ref, bbi_ref, cbr_ref, cbi_ref, d_ref, y_ref, yg_ref, xr_ref, xi_ref,
             car_r, car_i):
        @pl.when(pl.program_id(2) == 0)
        def _():
            car_r[...] = jnp.zeros_like(car_r)
            car_i[...] = jnp.zeros_like(car_i)

        steps, (pr, pi) = _power_tables(ar_ref[...], ai_ref[...], reverse=False)
        uf = u_ref[...]
        ub = uf.astype(BF16)
        xr_ref[...] = _dot(ub, bbr_ref[...], "nn")
        xi_ref[...] = _dot(ub, bbi_ref[...], "nn")

        def tile(i, carry):
            cr, ci = carry
            sl = pl.ds(pl.multiple_of(i * 8, 8), 8)
            br, bi = xr_ref[sl, :], xi_ref[sl, :]
            for k, (sr, si) in zip((1, 2, 4), steps):
                tr, ti = _cmul(sr, si, pltpu.roll(br, k, 0), pltpu.roll(bi, k, 0))
                br, bi = br + tr, bi + ti
            tr, ti = _cmul(pr, pi, cr, ci)
            br, bi = br + tr, bi + ti
            xr_ref[sl, :] = br
            xi_ref[sl, :] = bi
            return br[7:8, :], bi[7:8, :]

        cr, ci = lax.fori_loop(0, SCAN_CHUNK // 8, tile, (car_r[0:1, :], car_i[0:1, :]), unroll=4)
        car_r[0:1, :] = cr
        car_i[0:1, :] = ci
        y = (_dot(xr_ref[...].astype(BF16), cbr_ref[...], "nt") - _dot(xi_ref[...].astype(BF16), cbi_ref[...], "nt")
             + d_ref[...] * uf)
        y_ref[...] = y
        yg_ref[...] = jax.nn.gelu(y).astype(BF16)

    return hosted_call(
        body, comm, "ssm_fwd", (LOCAL_BATCH, SSM_LANE_BLOCKS, nt),
        [chan, lane, lane, mat, mat, mat, mat, dsp], [chan, chan, state, state],
        [jax.ShapeDtypeStruct((LOCAL_BATCH, SEQ, SSM_W), F32), jax.ShapeDtypeStruct((LOCAL_BATCH, SEQ, SSM_W), BF16),
         jax.ShapeDtypeStruct((LOCAL_BATCH, SEQ, SSM_STATE_W), F32),
         jax.ShapeDtypeStruct((LOCAL_BATCH, SEQ, SSM_STATE_W), F32)],
        [pltpu.VMEM((8, STATE_BLOCK), F32), pltpu.VMEM((8, STATE_BLOCK), F32)],
        (u, ab_re, ab_im, bb_re, bb_im, cb_re, cb_im, d_skip), ("parallel", "parallel", "arbitrary"))


def ssm_bwd(dyg, y, u, xr, xi, ab_re, ab_im, bb_re, bb_im, cb_re, cb_im, d_skip, comm=None):
    nt = SEQ // SCAN_CHUNK
    ntile = SCAN_CHUNK // 8

    def rev(t):
        return nt - 1 - t

    chan = pl.BlockSpec((None, SCAN_CHUNK, CHAN_BLOCK), lambda j, b, t: (b, rev(t), j))
    state = pl.BlockSpec((None, SCAN_CHUNK, STATE_BLOCK), lambda j, b, t: (b, rev(t), j))
    before = pl.BlockSpec((None, 8, STATE_BLOCK), lambda j, b, t: (b, jnp.maximum(rev(t) * ntile - 1, 0), j))
    mat = pl.BlockSpec((None, CHAN_BLOCK, STATE_BLOCK), lambda j, b, t: (j, 0, 0))
    lane = pl.BlockSpec((1, STATE_BLOCK), lambda j, b, t: (0, j))
    lane8 = pl.BlockSpec((8, STATE_BLOCK), lambda j, b, t: (0, j))
    dsp = pl.BlockSpec((1, CHAN_BLOCK), lambda j, b, t: (0, j))

    def body(dyg_ref, y_ref, u_ref, xr_ref, xi_ref, xrb_ref, xib_ref, ar_ref, ai_ref, bbr_ref, bbi_ref, cbr_ref,
             cbi_ref, d_ref, du_ref, dcbr_ref, dcbi_ref, dbbr_ref, dbbi_ref, dd_ref, dar_ref, dai_ref,
             lam_r, lam_i, car_r, car_i):
        b, t = pl.program_id(1), pl.program_id(2)
        first = jnp.logical_and(b == 0, t == 0)

        @pl.when(t == 0)
        def _():
            car_r[...] = jnp.zeros_like(car_r)
            car_i[...] = jnp.zeros_like(car_i)

        @pl.when(first)
        def _():
            for r in (dcbr_ref, dcbi_ref, dbbr_ref, dbbi_ref, dd_ref, dar_ref, dai_ref):
                r[...] = jnp.zeros_like(r)

        steps, (pr, pi) = _power_tables(ar_ref[...], -ai_ref[...], reverse=True)
        uf = u_ref[...]
        _, gelu_vjp = jax.vjp(jax.nn.gelu, y_ref[...])
        dy = gelu_vjp(dyg_ref[...])[0]
        dyb = dy.astype(BF16)
        dd_ref[...] += _colsum(dy * uf)
        lam_r[...] = _dot(dyb, cbr_ref[...], "nn")
        lam_i[...] = -_dot(dyb, cbi_ref[...], "nn")
        dcbr_ref[...] += _dot(dyb, xr_ref[...].astype(BF16), "tn")
        dcbi_ref[...] -= _dot(dyb, xi_ref[...].astype(BF16), "tn")
        row0 = lax.broadcasted_iota(jnp.int32, (8, STATE_BLOCK), 0) == 0
        has_before = rev(t) > 0
        xrb = jnp.where(has_before, xrb_ref[...], 0.0)
        xib = jnp.where(has_before, xib_ref[...], 0.0)

        def tile(s, carry):
            cr, ci, acc_r, acc_i = carry
            i = ntile - 1 - s
            sl = pl.ds(pl.multiple_of(i * 8, 8), 8)
            gr, gi = lam_r[sl, :], lam_i[sl, :]
            for k, (sr, si) in zip((1, 2, 4), steps):
                tr, ti = _cmul(sr, si, pltpu.roll(gr, 8 - k, 0), pltpu.roll(gi, 8 - k, 0))
                gr, gi = gr + tr, gi + ti
            tr, ti = _cmul(pr, pi, cr, ci)
            gr, gi = gr + tr, gi + ti
            lam_r[sl, :] = gr
            lam_i[sl, :] = gi
            sp = pl.ds(pl.multiple_of(jnp.maximum(i - 1, 0) * 8, 8), 8)
            pvr = jnp.where(i > 0, xr_ref[sp, :], xrb)
            pvi = jnp.where(i > 0, xi_ref[sp, :], xib)
            xsr = jnp.where(row0, pltpu.roll(pvr, 1, 0), pltpu.roll(xr_ref[sl, :], 1, 0))
            xsi = jnp.where(row0, pltpu.roll(pvi, 1, 0), pltpu.roll(xi_ref[sl, :], 1, 0))
            acc_r = acc_r + xsr * gr + xsi * gi
            acc_i = acc_i + xsr * gi - xsi * gr
            return gr[0:1, :], gi[0:1, :], acc_r, acc_i

        zero = jnp.zeros((8, STATE_BLOCK), F32)
        cr, ci, acc_r, acc_i = lax.fori_loop(0, ntile, tile, (car_r[0:1, :], car_i[0:1, :], zero, zero), unroll=2)
        car_r[0:1, :] = cr
        car_i[0:1, :] = ci
        dar_ref[...] += acc_r
        dai_ref[...] += acc_i
        lrb, lib = lam_r[...].astype(BF16), lam_i[...].astype(BF16)
        du = _dot(lrb, bbr_ref[...], "nt") + _dot(lib, bbi_ref[...], "nt") + d_ref[...] * dy
        du_ref[...] = du.astype(BF16)
        ub = uf.astype(BF16)
        dbbr_ref[...] += _dot(ub, lrb, "tn")
        dbbi_ref[...] += _dot(ub, lib, "tn")

    mat_shape = jax.ShapeDtypeStruct((SSM_LANE_BLOCKS, CHAN_BLOCK, STATE_BLOCK), F32)
    return hosted_call(
        body, comm, "ssm_bwd", (SSM_LANE_BLOCKS, LOCAL_BATCH, nt),
        [chan, chan, chan, state, state, before, before, lane, lane, mat, mat, mat, mat, dsp],
        [chan, mat, mat, mat, mat, dsp, lane8, lane8],
        [jax.ShapeDtypeStruct((LOCAL_BATCH, SEQ, SSM_W), BF16), mat_shape, mat_shape, mat_shape, mat_shape,
         jax.ShapeDtypeStruct((1, SSM_W), F32), jax.ShapeDtypeStruct((8, SSM_STATE_W), F32),
         jax.ShapeDtypeStruct((8, SSM_STATE_W), F32)],
        [pltpu.VMEM((SCAN_CHUNK, STATE_BLOCK), F32), pltpu.VMEM((SCAN_CHUNK, STATE_BLOCK), F32),
         pltpu.VMEM((8, STATE_BLOCK), F32), pltpu.VMEM((8, STATE_BLOCK), F32)],
        (dyg, y, u, xr, xi, xr, xi, ab_re, ab_im, bb_re, bb_im, cb_re, cb_im, d_skip),
        ("parallel", "arbitrary", "arbitrary"))


def _merge_fn(g0, g1, attn_d, za, zb):
    return jax.nn.sigmoid(g0) * attn_d + jax.nn.sigmoid(g1) * (za * jax.nn.sigmoid(zb))


def _swiglu_fn(a, b):
    return jax.nn.silu(a) * b


def _own_slot(slots, shard):
    me = 2 * lax.axis_index("x") + lax.axis_index("y")
    return lax.dynamic_update_slice(slots, shard[None], (me, 0, 0))


def _own_sum(parts, chip_sum):
    me = 2 * lax.axis_index("x") + lax.axis_index("y")
    rows = chip_sum.shape[1]
    return lax.dynamic_update_slice(parts, lax.dynamic_slice_in_dim(chip_sum, me, 1, 0),
                                    (me, lax.axis_index("c") * rows, 0))


def _reduce_start(names, gw, shard_shapes):
    return swap_comm([_to_slots(n, gw[n], shard_shapes[n]) for n in names])


def _reduce_chip(names, swap, got, core):
    return exchange_comm([add_halves(n, g, r, core) for n, g, r in zip(names, swap.ins, got)])


def local_step(x, target, shards, small, core):
    g_mix, g_ffn, g_final = small["norm_mix_g"], small["norm_ffn_g"], small["norm_final_g"]
    tables = _rope_tables()
    seqs = lambda t: t.reshape(LOCAL_BATCH, SEQ, t.shape[-1])
    toks = lambda t: t.reshape(TOKENS, t.shape[-1])
    shard_shapes = {n: s.shape for n, s in shards.items()}
    w = {}

    def gather(names):
        return gather_comm([shards[n] for n in names])

    def arrived(names, slots, own=None):
        for n, s in zip(names, slots):
            w[n] = _from_slots(n, _own_slot(s, shards[n] if own is None else own))

    later = [n for n in BIG if n != "w_in"]
    sems, w_in_shard, land, token = gather_start(shards["w_in"].astype(BF16))
    zero = token[0, 0]
    h, *rest = first_norm(x, g_mix + zero, [shards[n] for n in later])
    shards = dict(shards)
    shards.update(zip(later, rest))
    br_t = small["ssm_b_re"].transpose(2, 0, 1)
    bi_t = small["ssm_b_im"].transpose(2, 0, 1)
    log_dt = small["ssm_log_dt"].reshape(32, 1)
    ab_re, ab_im, bb_re_t, bb_im_t = ssm_prep(small["ssm_a_re"] + zero, small["ssm_a_im"], log_dt, br_t, bi_t)
    ab = [ab_re.reshape(1, SSM_STATE_W), ab_im.reshape(1, SSM_STATE_W)]
    bb = [_block_diag(bb_re_t).astype(BF16), _block_diag(bb_im_t).astype(BF16)]
    cb = [_block_diag((small["ssm_c_re"] + zero).transpose(1, 0, 2)).astype(BF16),
          _block_diag((small["ssm_c_im"] + zero).transpose(1, 0, 2)).astype(BF16)]
    d_skip = small["ssm_d"].reshape(1, SSM_W)
    w_in_shard, land = gather_wait(sems, w_in_shard, land, [h] + bb + cb)
    arrived(["w_in"], [handover(land)], own=w_in_shard)
    w_qkv, w_u, w_gate = _qkv_order(w["w_in"][:QKV_W]), w["w_in"][QKV_W:QKV_W + SSM_W], w["w_in"][QKV_W + SSM_W:]
    qkv, *slots = proj_qkv(h, w_qkv, tables, comm=gather(["w_attn_out", "w_glu"]))
    arrived(["w_attn_out", "w_glu"], slots)
    qkv = seqs(qkv)
    u = seqs(matmul(h, w_u, "nt", F32, "proj_u"))
    gl, *slots = matmul(h, w_gate, "nt", BF16, "proj_gate", comm=gather(["w_out"]))
    arrived(["w_out"], slots)
    attn_b, attn, lse, *slots = attn_fwd(qkv, comm=gather(["w_ffn_gate"]))
    arrived(["w_ffn_gate"], slots)
    attn_b = toks(attn_b)
    attn_d = matmul(attn_b, w["w_attn_out"], "nn", F32, "attn_out")
    y, yg, xr, xi, *slots = ssm_fwd(u, *ab, *bb, *cb, d_skip, comm=gather(["w_ffn_up"]))
    arrived(["w_ffn_up"], slots)
    yg2 = toks(yg)
    z = matmul(yg2, w["w_glu"], "nn", BF16, "glu")
    gate_ins = [(gl, D_MODEL, 0), (gl, D_MODEL, 1), attn_d, (z, D_MODEL, 0), (z, D_MODEL, 1)]
    (merged,) = rowwise(lambda *v: (_merge_fn(*[t.astype(F32) for t in v]),), gate_ins, [(D_MODEL, BF16)], "merge")
    x1, h2 = matmul_rows(merged, w["w_out"], "out_proj", lambda rows, g: (rows, _rms(rows, g)), [g_ffn],
                         [(D_MODEL, F32), (D_MODEL, BF16)], add=x)
    a, b, act, *slots = ffn_in(h2, w["w_ffn_gate"], w["w_ffn_up"], comm=gather(["w_ffn_down"]))
    arrived(["w_ffn_down"], slots)

    def final_fn(xv, g, tgt):
        yv, vjp = jax.vjp(_rms, xv, g)
        err = yv - tgt
        dx, dg = vjp(err * (1.0 / D_MODEL))
        loss = 0.5 * jnp.sum(jnp.mean(err * err, axis=-1, keepdims=True), axis=0, keepdims=True)
        return dx, dx, dg, jnp.broadcast_to(loss, (1, LANES))

    dx2, dx2_b, dg_final, loss = matmul_rows(act, w["w_ffn_down"], "ffn_down_loss", final_fn, [g_final, target],
                                             [(D_MODEL, F32), (D_MODEL, BF16)], accs=(D_MODEL, LANES), add=x1)
    gw, parts = {}, {}
    gw["w_ffn_down"] = matmul(act, dx2_b, "tn", F32, "d_ffn_down")
    da_b, db_b = ffn_in_bwd(dx2_b, w["w_ffn_down"], a, b)
    gw["w_ffn_gate"] = matmul(da_b, h2, "tn", F32, "d_ffn_gate")
    gw["w_ffn_up"] = matmul(db_b, h2, "tn", F32, "d_ffn_up")
    ffn = ["w_ffn_down", "w_ffn_gate", "w_ffn_up"]
    swap = _reduce_start(ffn[:2], gw, shard_shapes)
    dh2, *got = matmul(da_b, w["w_ffn_gate"], "nn", F32, "d_h2_gate", comm=swap)
    ffn_exchange = [_reduce_chip(ffn[:2], swap, got, core)]
    swap = _reduce_start(ffn[2:], gw, shard_shapes)

    def norm_bwd(dh, xv, g, skip):
        _, vjp = jax.vjp(_rms, xv, g)
        dx, dg = vjp(dh)
        dx = dx + skip
        return dx, dx, dg

    dx1, dx1_b, dg_ffn, *got = matmul_rows(db_b, w["w_ffn_up"], "d_h2_up_norm", norm_bwd, [x1, g_ffn, dx2],
                                           [(D_MODEL, F32), (D_MODEL, BF16)], accs=(D_MODEL,), add=dh2, comm=swap)
    ffn_up_exchange = _reduce_chip(ffn[2:], swap, got, core)
    gw["w_out"] = matmul(merged, dx1_b, "tn", F32, "d_out")
    dmerged = matmul(dx1_b, w["w_out"], "nt", F32, "d_merged")

    def merge_bwd(g0, g1, ad, za, zb, dm):
        _, vjp = jax.vjp(_merge_fn, *[t.astype(F32) for t in (g0, g1, ad, za, zb)])
        dg0, dg1, dad, dza, dzb = vjp(dm)
        return jnp.concatenate([dg0, dg1], axis=1), dad, jnp.concatenate([dza, dzb], axis=1)

    dgl_b, dattn_d_b, dz_b, got = rowwise(
        merge_bwd, gate_ins + [dmerged], [(GATE_W, BF16), (D_MODEL, BF16), (GATE_W, BF16)], "merge_bwd",
        comm=ffn_up_exchange)
    parts["w_ffn_up"] = _own_sum(got, ffn_up_exchange.ins[0])
    gw["w_attn_out"] = matmul(attn_b, dattn_d_b, "tn", F32, "d_attn_out")
    dattn = seqs(matmul(dattn_d_b, w["w_attn_out"], "nt", F32, "d_attn"))
    gw["w_glu"] = matmul(yg2, dz_b, "tn", F32, "d_glu")
    dyg = seqs(matmul(dz_b, w["w_glu"], "nt", F32, "d_yg"))
    mixer = ["w_out", "w_attn_out", "w_glu"]
    swap = _reduce_start(mixer, gw, shard_shapes)
    du_b, dcb_re, dcb_im, dbb_re, dbb_im, dd, da_re8, da_im8, *rest = ssm_bwd(
        dyg, y, u, xr, xi, *ab, *bb, *cb, d_skip, comm=join_comms(ffn_exchange + [swap]))
    for n, p, chip_sum in zip(ffn[:2], rest[:2], ffn_exchange[0].ins):
        parts[n] = _own_sum(p, chip_sum)
    mixer_exchange = _reduce_chip(mixer, swap, rest[2:], core)
    du_b = toks(du_b)
    g_ab_re = jnp.sum(da_re8, axis=0).reshape(32, 64)
    g_ab_im = jnp.sum(da_im8, axis=0).reshape(32, 64)
    d_lr, d_li, d_ldt, d_br_t, d_bi_t = ssm_prep_bwd(
        small["ssm_a_re"], small["ssm_a_im"], log_dt, br_t, bi_t,
        g_ab_re, g_ab_im, _block_diag_t(dbb_re), _block_diag_t(dbb_im))
    as_gcn = lambda t: t.transpose(1, 0, 2).reshape(SSM_W, 64)
    gs = {
        "ssm_a_re": d_lr, "ssm_a_im": d_li, "ssm_log_dt": d_ldt.reshape(1, 32),
        "ssm_b_re": as_gcn(d_br_t), "ssm_b_im": as_gcn(d_bi_t),
        "ssm_c_re": as_gcn(_block_diag_t(dcb_re)), "ssm_c_im": as_gcn(_block_diag_t(dcb_im)),
        "ssm_d": dd.reshape(32, 16).T,
    }
    ssm_gather = small_comm([gs[n] for n in SSM_SMALL])
    dqkv_b, *rest = attn_bwd(qkv, tables, dattn, attn, lse, comm=join_comms([mixer_exchange, ssm_gather]))
    for n, p, chip_sum in zip(mixer, rest, mixer_exchange.ins):
        parts[n] = _own_sum(p, chip_sum)
    ssm_shares = rest[len(mixer):]
    dqkv_b = toks(dqkv_b)
    d_qkv = matmul(dqkv_b, h, "tn", F32, "d_w_qkv")
    d_u = matmul(du_b, h, "tn", F32, "d_w_u")
    d_gate = matmul(dgl_b, h, "tn", F32, "d_w_gate")
    gw["w_in"] = jnp.concatenate([_qkv_order(d_qkv, back=True), d_u, d_gate], axis=0)
    swap = _reduce_start(["w_in"], gw, shard_shapes)
    dh, *got = matmul(dqkv_b, w_qkv, "nn", F32, "d_h_qkv", comm=swap)
    w_in_exchange = _reduce_chip(["w_in"], swap, got, core)
    grad_x, dg_mix, got = mix_in_bwd([du_b, dgl_b], [w_u, w_gate], dh, x, g_mix, dx1, comm=w_in_exchange)
    parts["w_in"] = _own_sum(got, w_in_exchange.ins[0])
    gs_norm = {"norm_mix_g": dg_mix, "norm_ffn_g": dg_ffn, "norm_final_g": dg_final}
    return loss, grad_x, parts, ssm_shares, gs_norm


ANY = pl.BlockSpec(memory_space=pl.ANY)
BIG = ("w_in", "w_glu", "w_attn_out", "w_out", "w_ffn_gate", "w_ffn_up", "w_ffn_down")
TRANSPOSED = ("w_in", "w_ffn_gate", "w_ffn_up")
ROW_SHARDED = TRANSPOSED + ("w_out", "w_ffn_down")
SMALL = ("norm_mix_g", "ssm_a_re", "ssm_a_im", "ssm_log_dt", "ssm_b_re", "ssm_b_im", "ssm_c_re", "ssm_c_im",
         "ssm_d", "norm_ffn_g", "norm_final_g")
WEIGHTS = ("norm_mix_g", "w_in", "ssm_a_re", "ssm_a_im", "ssm_log_dt", "ssm_b_re", "ssm_b_im", "ssm_c_re",
           "ssm_c_im", "ssm_d", "w_glu", "w_attn_out", "w_out", "norm_ffn_g", "w_ffn_gate", "w_ffn_up",
           "w_ffn_down", "norm_final_g")
SSM_SMALL = SMALL[1:9]
NORM_SMALL = (SMALL[0],) + SMALL[9:]
NORM_ROWS = 32
N_BIG = len(BIG)


def _position():
    return lax.axis_index("x"), lax.axis_index("y"), lax.axis_index("c")


def _other_chips(x, y):
    return [(1 - x, y), (x, 1 - y), (1 - x, 1 - y)]


def _remote(src, dst, send_sem, recv_sem, device):
    return pltpu.make_async_remote_copy(src_ref=src, dst_ref=dst, send_sem=send_sem, recv_sem=recv_sem,
                                        device_id=device, device_id_type=MESH)


_later = functools.partial


def _two_level_phases(copies):
    def first(*refs):
        locals_, sends, _, _, _ = copies(*refs)
        for cp in locals_ + sends:
            cp().start()

    def mid(*refs):
        _, _, arrived, passed, _ = copies(*refs)
        for got, cp in zip(arrived, passed):
            got().wait_recv()
            cp().start()

    def last(*refs):
        locals_, sends, _, passed, from_sibling = copies(*refs)
        for cp in from_sibling:
            cp().wait_recv()
        for cp in sends + passed:
            cp().wait_send()
        for cp in locals_:
            cp().wait()

    return first, mid, last


def _half(ref, chip, which):
    rows = ref.shape[1] // 2
    return ref.at[chip, pl.ds(which * rows, rows), :]


class Comm:
    def __init__(self, ins, out_shapes, sems, first, mid, last):
        self.ins, self.out_shapes, self.sems = list(ins), list(out_shapes), list(sems)
        self.first, self.mid, self.last = first, mid, last


def join_comms(comms):
    def cut(refs_by_kind):
        offs, parts = [0, 0, 0], []
        for cm in comms:
            sizes = (len(cm.ins), len(cm.out_shapes), len(cm.sems))
            parts.append(tuple(refs_by_kind[k][offs[k]:offs[k] + sizes[k]] for k in range(3)))
            offs = [o + s for o, s in zip(offs, sizes)]
        return parts

    def phase(which):
        def run(ins, outs, sems):
            for cm, part in zip(comms, cut((ins, outs, sems))):
                fn = getattr(cm, which)
                if fn is not None:
                    fn(*part)
        return run

    return Comm(sum((cm.ins for cm in comms), []), sum((cm.out_shapes for cm in comms), []),
                sum((cm.sems for cm in comms), []), phase("first"), phase("mid"), phase("last"))


def _comm_operands(comm):
    if comm is None:
        return [], [], []
    return comm.ins, comm.out_shapes, comm.sems


def _comm_begin(comm, refs, step, n_steps):
    if comm is None:
        return
    pl.when(step == 0)(lambda: comm.first(*refs))
    if comm.mid is not None:
        pl.when(step == (n_steps * 3) // 4)(lambda: comm.mid(*refs))


def _comm_end(comm, refs, step, n_steps):
    if comm is not None:
        pl.when(step == n_steps - 1)(lambda: comm.last(*refs))


def _comm_refs(comm, refs, n_in, n_out):
    if comm is None:
        return list(refs), None
    ci, co, cs = len(comm.ins), len(comm.out_shapes), len(comm.sems)
    o0 = n_in + ci
    s0 = o0 + n_out + co
    host = list(refs[:n_in]) + list(refs[o0:o0 + n_out]) + list(refs[s0:len(refs) - cs])
    return host, (list(refs[n_in:o0]), list(refs[o0 + n_out:s0]), list(refs[len(refs) - cs:]))


def run_comm(comm, name):
    n_in, n_out = len(comm.ins), len(comm.out_shapes)

    def body(*refs):
        parts = (list(refs[:n_in]), list(refs[n_in:n_in + n_out]), list(refs[n_in + n_out:]))
        comm.first(*parts)
        if comm.mid is not None:
            comm.mid(*parts)
        comm.last(*parts)

    return pl.pallas_call(body, name=name, in_specs=[ANY] * n_in, out_specs=[ANY] * n_out,
                          out_shape=comm.out_shapes, scratch_shapes=comm.sems)(*comm.ins)


def hosted_call(work, comm, name, grid, in_specs, out_specs, out_shape, scratch_shapes, args, semantics):
    c_ins, c_outs, c_sems = _comm_operands(comm)
    n_steps = math.prod(grid)

    def body(*refs):
        host, c_refs = _comm_refs(comm, refs, len(in_specs), len(out_specs))
        step = 0
        for axis, size in enumerate(grid):
            step = step * size + pl.program_id(axis)
        _comm_begin(comm, c_refs, step, n_steps)
        work(*host)
        _comm_end(comm, c_refs, step, n_steps)

    return pl.pallas_call(
        body, name=name, grid=grid, in_specs=list(in_specs) + [ANY] * len(c_ins),
        out_specs=list(out_specs) + [ANY] * len(c_outs), out_shape=list(out_shape) + c_outs,
        scratch_shapes=list(scratch_shapes) + c_sems,
        compiler_params=_params(semantics if comm is None else ("arbitrary",) * len(grid)),
    )(*args, *c_ins)


def gather_comm(shards):
    n = len(shards)

    def copies(srcs, outs, sems):
        send_sems, recv_sems, local_sems = sems
        x, y, c = _position()
        me = 2 * x + y
        sibling = (x, y, 1 - c)
        chips = _other_chips(x, y)
        locals_ = []
        sends, arrived, passed, from_sibling = [], [], [], []
        for j, (px, py) in enumerate(chips):
            for i, (s, o) in enumerate(zip(srcs, outs)):
                rows = s.shape[0] // 2
                sends.append(_later(_remote, s.at[pl.ds(c * rows, rows), :], _half(o, me, c), send_sems.at[i, j],
                                    recv_sems.at[i, j], (px, py, c)))
                got = _half(o, 2 * px + py, c)
                arrived.append(_later(_remote, got, got, send_sems.at[i, j], recv_sems.at[i, j], (px, py, c)))
                passed.append(_later(_remote, got, got, send_sems.at[i, 3 + j], recv_sems.at[i, 3 + j], sibling))
                other = _half(o, 2 * px + py, 1 - c)
                from_sibling.append(_later(_remote, other, other, send_sems.at[i, 3 + j], recv_sems.at[i, 3 + j],
                                           sibling))
        return locals_, sends, arrived, passed, from_sibling

    return Comm(shards, [jax.ShapeDtypeStruct((N_CHIPS,) + s.shape, s.dtype) for s in shards],
                [pltpu.SemaphoreType.DMA((n, 6)), pltpu.SemaphoreType.DMA((n, 6)), pltpu.SemaphoreType.DMA((n,))],
                *_two_level_phases(copies))


HBM = pl.BlockSpec(memory_space=pltpu.HBM)
SEM = pl.BlockSpec(memory_space=pltpu.SEMAPHORE)
N_OTHER = N_CHIPS - 1


def _ici_halves(src_ref, land_ref, sems):
    x, y, c = _position()
    me = 2 * x + y
    rows = src_ref.shape[0] // 2
    sends, arrivals = [], []
    for j, (px, py) in enumerate(_other_chips(x, y)):
        sends.append(_later(_remote, src_ref.at[pl.ds(c * rows, rows), :], _half(land_ref, me, c), sems[j],
                            sems[N_OTHER + j], (px, py, c)))
        got = _half(land_ref, 2 * px + py, c)
        arrivals.append(_later(_remote, got, got, sems[j], sems[N_OTHER + j], (px, py, c)))
    return sends, arrivals


def gather_start(shard):
    def body(src_ref, land_ref, *rest):
        sems, token = rest[:2 * N_OTHER], rest[-1]
        for cp in _ici_halves(src_ref, land_ref, sems)[0]:
            cp().start()
        token[...] = jnp.zeros_like(token)

    sem = pltpu.SemaphoreType.DMA(())
    land = pltpu.HBM((N_CHIPS,) + shard.shape, shard.dtype)
    res = pl.pallas_call(
        body, name="w_in_gather_start", in_specs=(HBM, HBM),
        out_specs=(SEM,) * (2 * N_OTHER) + (HBM, HBM, pl.BlockSpec(memory_space=pltpu.VMEM)),
        out_shape=(sem,) * (2 * N_OTHER) + (pltpu.HBM(shard.shape, shard.dtype), land,
                                           jax.ShapeDtypeStruct((8, LANES), F32)),
        input_output_aliases={0: 2 * N_OTHER, 1: 2 * N_OTHER + 1},
        compiler_params=pltpu.CompilerParams(has_side_effects=pltpu.SideEffectType.DATAFLOW_SIDE_EFFECTING),
    )(pltpu.with_memory_space_constraint(shard, pltpu.HBM),
      pltpu.with_memory_space_constraint(lax.empty((N_CHIPS,) + shard.shape, shard.dtype), pltpu.HBM))
    return res[:2 * N_OTHER], res[2 * N_OTHER], res[2 * N_OTHER + 1], res[-1]


def gather_wait(sems, shard, land, after):
    def body(src_ref, land_ref, *rest):
        sem_refs = rest[:2 * N_OTHER]
        sends, arrivals = _ici_halves(src_ref, land_ref, sem_refs)
        for cp in sends:
            cp().wait_send()
        for cp in arrivals:
            cp().wait_recv()

    return pl.pallas_call(
        body, name="w_in_gather_wait", in_specs=(HBM, HBM) + (SEM,) * (2 * N_OTHER) + (ANY,) * len(after),
        out_specs=(HBM, HBM), out_shape=(pltpu.HBM(shard.shape, shard.dtype), pltpu.HBM(land.shape, land.dtype)),
        input_output_aliases={0: 0, 1: 1},
        compiler_params=pltpu.CompilerParams(has_side_effects=pltpu.SideEffectType.DATAFLOW_SIDE_EFFECTING),
    )(shard, land, *sems, *after)


def handover(land):
    def body(land_ref, out_ref, send_sems, recv_sems):
        x, y, c = _position()
        sibling = (x, y, 1 - c)
        sends = []
        for j, (px, py) in enumerate(_other_chips(x, y)):
            got = _half(land_ref, 2 * px + py, c)
            sends.append(_remote(got, got, send_sems.at[j], recv_sems.at[j], sibling))
            sends[-1].start()
        for j, (px, py) in enumerate(_other_chips(x, y)):
            other = _half(land_ref, 2 * px + py, 1 - c)
            _remote(other, other, send_sems.at[j], recv_sems.at[j], sibling).wait_recv()
        for cp in sends:
            cp.wait_send()

    return pl.pallas_call(
        body, name="w_in_handover", in_specs=[ANY], out_specs=ANY, out_shape=jax.ShapeDtypeStruct(land.shape, land.dtype),
        input_output_aliases={0: 0},
        scratch_shapes=[pltpu.SemaphoreType.DMA((N_OTHER,)), pltpu.SemaphoreType.DMA((N_OTHER,))],
    )(land)


def swap_comm(grads):
    n = len(grads)

    def copies(srcs, gots, sems):
        send_sems, recv_sems = sems
        x, y, c = _position()
        out = []
        for i, (s, o) in enumerate(zip(srcs, gots)):
            rows = s.shape[1] // 2
            out.append(_remote(s.at[:, pl.ds((1 - c) * rows, rows), :], o, send_sems.at[i], recv_sems.at[i],
                               (x, y, 1 - c)))
        return out

    def first(srcs, gots, sems):
        for cp in copies(srcs, gots, sems):
            cp.start()

    def last(srcs, gots, sems):
        for cp in copies(srcs, gots, sems):
            cp.wait()

    return Comm(grads, [jax.ShapeDtypeStruct((N_CHIPS, g.shape[1] // 2, g.shape[2]), g.dtype) for g in grads],
                [pltpu.SemaphoreType.DMA((n,)), pltpu.SemaphoreType.DMA((n,))], first, None, last)


def add_halves(name, g, got, core):
    _, half, cols = got.shape
    mine = pl.BlockSpec((None, half, cols), lambda k, c_ref: (k, c_ref[0], 0))
    other = pl.BlockSpec((None, half, cols), lambda k, c_ref: (k, 0, 0))

    def body(c_ref, g_ref, got_ref, o_ref):
        o_ref[...] = (g_ref[...] + got_ref[...]).astype(BF16)

    return pl.pallas_call(
        body, name="add_halves_" + name,
        grid_spec=pltpu.PrefetchScalarGridSpec(num_scalar_prefetch=1, grid=(N_CHIPS,), in_specs=[mine, other],
                                               out_specs=other),
        out_shape=jax.ShapeDtypeStruct(got.shape, BF16),
        compiler_params=_params(("parallel",)),
    )(core, g, got)


def exchange_comm(parts):
    n = len(parts)

    def copies(srcs, outs, sems):
        send_sems, recv_sems, local_sems = sems
        x, y, c = _position()
        me = 2 * x + y
        sibling = (x, y, 1 - c)
        chips = _other_chips(x, y)
        locals_, sends, arrived, passed, from_sibling = [], [], [], [], []
        for i, (s, o) in enumerate(zip(srcs, outs)):
            sends.append(_later(_remote, s.at[me], _half(o, me, c), send_sems.at[i, 3], recv_sems.at[i, 3], sibling))
            other = _half(o, me, 1 - c)
            from_sibling.append(_later(_remote, other, other, send_sems.at[i, 3], recv_sems.at[i, 3], sibling))
        for j, (px, py) in enumerate(chips):
            for i, (s, o) in enumerate(zip(srcs, outs)):
                sends.append(_later(_remote, s.at[2 * px + py], _half(o, me, c), send_sems.at[i, j],
                                    recv_sems.at[i, j], (px, py, c)))
                got = _half(o, 2 * px + py, c)
                arrived.append(_later(_remote, got, got, send_sems.at[i, j], recv_sems.at[i, j], (px, py, c)))
                passed.append(_later(_remote, got, got, send_sems.at[i, 4 + j], recv_sems.at[i, 4 + j], sibling))
                other = _half(o, 2 * px + py, 1 - c)
                from_sibling.append(_later(_remote, other, other, send_sems.at[i, 4 + j], recv_sems.at[i, 4 + j],
                                           sibling))
        return locals_, sends, arrived, passed, from_sibling

    return Comm(parts, [jax.ShapeDtypeStruct((N_CHIPS, 2 * p.shape[1], p.shape[2]), p.dtype) for p in parts],
                [pltpu.SemaphoreType.DMA((n, 7)), pltpu.SemaphoreType.DMA((n, 7)), pltpu.SemaphoreType.DMA((n,))],
                *_two_level_phases(copies))


def small_comm(shares):
    n = len(shares)

    def copies(srcs, outs, sems):
        send_sems, recv_sems, local_sems = sems
        x, y, c = _position()
        me = 4 * x + 2 * y + c
        flips = [(fx, fy, fc) for fx in (0, 1) for fy in (0, 1) for fc in (0, 1)][1:]
        peers = [(1 - x if fx else x, 1 - y if fy else y, 1 - c if fc else c) for fx, fy, fc in flips]
        locals_, sends, arrived = [], [], []
        for i, (src_ref, out_ref) in enumerate(zip(srcs, outs)):
            locals_.append(_later(pltpu.make_async_copy, src_ref, out_ref.at[me], local_sems.at[i]))
            for j, (px, py, pc) in enumerate(peers):
                sends.append(_later(_remote, src_ref, out_ref.at[me], send_sems.at[i, j], recv_sems.at[i, j],
                                    (px, py, pc)))
                got = out_ref.at[4 * px + 2 * py + pc]
                arrived.append(_later(_remote, got, got, send_sems.at[i, j], recv_sems.at[i, j], (px, py, pc)))
        return locals_, sends, arrived

    def first(*refs):
        locals_, sends, _ = copies(*refs)
        for cp in locals_ + sends:
            cp().start()

    def last(*refs):
        locals_, sends, arrived = copies(*refs)
        for cp in arrived:
            cp().wait_recv()
        for cp in sends:
            cp().wait_send()
        for cp in locals_:
            cp().wait()

    return Comm(shares, [jax.ShapeDtypeStruct((N_DEV,) + s.shape, s.dtype) for s in shares],
                [pltpu.SemaphoreType.DMA((n, 7)), pltpu.SemaphoreType.DMA((n, 7)), pltpu.SemaphoreType.DMA((n,))],
                first, None, last)


def _adam_fn(w, g, m, v):
    m = ADAM_B1 * m + (1.0 - ADAM_B1) * g
    v = ADAM_B2 * v + (1.0 - ADAM_B2) * jnp.square(g)
    m_hat = m / (1.0 - ADAM_B1 ** ADAM_STEP)
    v_hat = v / (1.0 - ADAM_B2 ** ADAM_STEP)
    return -ADAM_LR * (m_hat / (jnp.sqrt(v_hat) + ADAM_EPS) + ADAM_WD * w), m, v


def adam_big(name, parts, w, m, v):
    rows, cols = w.shape
    tm = _pick(rows, 384, 16)

    def fn(p0, p1, p2, p3, wv, mv, vv):
        g = ((p0.astype(F32) + p1.astype(F32)) + p2.astype(F32)) + p3.astype(F32)
        return (g,) + _adam_fn(wv, g, mv, vv)

    return rowwise(fn, [parts, w, m, v], [(cols, F32)] * 4, "adam_" + name, tm=tm, rows=rows)


def adam_small(name, gathered, w, m, v):
    def body(g_ref, w_ref, m_ref, v_ref, go_ref, d_ref, mo_ref, vo_ref):
        g = g_ref[0]
        for k in range(1, N_DEV):
            g = g + g_ref[k]
        go_ref[...] = g
        d_ref[...], mo_ref[...], vo_ref[...] = _adam_fn(w_ref[...], g, m_ref[...], v_ref[...])

    return pl.pallas_call(body, name=name, out_shape=[jax.ShapeDtypeStruct(w.shape, F32)] * 4,
                          compiler_params=_params())(gathered, w, m, v)


def _ssm_2d(name, t):
    t = t[0] if t.ndim > 2 else t
    if name in ("ssm_b_re", "ssm_b_im"):
        return t.transpose(0, 2, 1).reshape(SSM_W, 64)
    if name in ("ssm_c_re", "ssm_c_im"):
        return t.reshape(SSM_W, 64)
    return t.T if name == "ssm_d" else t


def _ssm_back(name, t):
    if name in ("ssm_b_re", "ssm_b_im"):
        return t.reshape(32, 16, 64).transpose(0, 2, 1)[None]
    if name in ("ssm_c_re", "ssm_c_im"):
        return t.reshape(1, 32, 16, 64)
    if name == "ssm_d":
        return t.T[None]
    return t if name == "ssm_log_dt" else t[None]


def adam_ssm(shares, w, m, v):
    n = len(w)

    def body(*refs):
        ins, outs = refs[:4 * n], refs[4 * n:]
        for i in range(n):
            g_ref, w_ref, m_ref, v_ref = (ins[k * n + i] for k in range(4))
            g = g_ref[0]
            for k in range(1, N_DEV):
                g = g + g_ref[k]
            outs[4 * i][...] = g
            outs[4 * i + 1][...], outs[4 * i + 2][...], outs[4 * i + 3][...] = _adam_fn(w_ref[...], g, m_ref[...],
                                                                                      v_ref[...])

    out_shape = [jax.ShapeDtypeStruct(t.shape, F32) for t in w for _ in range(4)]
    res = pl.pallas_call(body, name="adam_ssm", out_shape=out_shape, compiler_params=_params())(*shares, *w, *m, *v)
    return [res[4 * i:4 * i + 4] for i in range(n)]


def _pack_small(names, vals, rows, last=None):
    flat = [vals[n].reshape(-1) for n in names]
    if last is not None:
        flat.append(last.reshape(-1))
    flat = jnp.concatenate(flat)
    return jnp.pad(flat, (0, rows * LANES - flat.shape[0])).reshape(rows, LANES)


def _unpack_small(names, pack, shapes):
    flat, out, off = pack.reshape(-1), {}, 0
    for n in names:
        size = math.prod(shapes[n])
        out[n] = flat[off:off + size].reshape(shapes[n])
        off += size
    return out, flat[off]


def _to_slots(name, g, shard_shape):
    rows, cols = shard_shape
    if name in ROW_SHARDED:
        return g.reshape(N_CHIPS, rows, cols)
    return g.reshape(rows, N_CHIPS, cols).transpose(1, 0, 2)


def _from_slots(name, s):
    _, rows, cols = s.shape
    if name in ROW_SHARDED:
        return s.reshape(N_CHIPS * rows, cols)
    return s.transpose(1, 0, 2).reshape(rows, N_CHIPS * cols)


def kernel(x, norm_mix_g, w_in, ssm_a_re, ssm_a_im, ssm_log_dt, ssm_b_re, ssm_b_im, ssm_c_re, ssm_c_im, ssm_d, w_glu, w_attn_out, w_out, norm_ffn_g, w_ffn_gate, w_ffn_up, w_ffn_down, norm_final_g, loss_target, m_norm_mix_g, m_w_in, m_ssm_a_re, m_ssm_a_im, m_ssm_log_dt, m_ssm_b_re, m_ssm_b_im, m_ssm_c_re, m_ssm_c_im, m_ssm_d, m_w_glu, m_w_attn_out, m_w_out, m_norm_ffn_g, m_w_ffn_gate, m_w_ffn_up, m_w_ffn_down, m_norm_final_g, v_norm_mix_g, v_w_in, v_ssm_a_re, v_ssm_a_im, v_ssm_log_dt, v_ssm_b_re, v_ssm_b_im, v_ssm_c_re, v_ssm_c_im, v_ssm_d, v_w_glu, v_w_attn_out, v_w_out, v_norm_ffn_g, v_w_ffn_gate, v_w_ffn_up, v_w_ffn_down, v_norm_final_g):
    given = dict(locals())
    def local(name, prefix=""):
        t = given[prefix + name][0]
        return t.T if name in TRANSPOSED else t

    shard = {n: local(n) for n in BIG}
    shapes = {n: given[n].shape for n in WEIGHTS}

    small = {n: given[n] for n in SMALL}
    small_2d = dict(small)
    for n in ("ssm_a_re", "ssm_a_im", "ssm_b_re", "ssm_b_im", "ssm_c_re", "ssm_c_im", "ssm_d"):
        small_2d[n] = small[n][0]
    small_2d["norm_final_g"] = norm_final_g.reshape(1, D_MODEL)

    core = lax.axis_index("c").astype(jnp.int32).reshape(1)
    loss, grad_x, parts, ssm_shares, gs_norm = local_step(
        x.reshape(TOKENS, D_MODEL), loss_target.reshape(TOKENS, D_MODEL),
        {n: shard[n] for n in BIG}, small_2d, core)

    (norm_shares,) = run_comm(small_comm([_pack_small(NORM_SMALL, gs_norm, NORM_ROWS, last=loss)]),
                              "gather_norm_grads")
    small_out = [{} for _ in range(4)]
    packs = [_pack_small(NORM_SMALL, {n: given[p + n] for n in NORM_SMALL}, NORM_ROWS) for p in ("", "m_", "v_")]
    for kind, t in enumerate(adam_small("adam_norm_gains", norm_shares, *packs)):
        vals, after = _unpack_small(NORM_SMALL, t, shapes)
        small_out[kind].update(vals)
        if kind == 0:
            total_loss = after
    ssm_in = [[_ssm_2d(n, given[p + n]) for n in SSM_SMALL] for p in ("", "m_", "v_")]
    for n, res in zip(SSM_SMALL, adam_ssm(ssm_shares, *ssm_in)):
        for kind, t in enumerate(res):
            small_out[kind][n] = _ssm_back(n, t)

    big_out = {}
    for n in BIG:
        res = adam_big(n, parts[n], shard[n], local(n, "m_"), local(n, "v_"))
        big_out[n] = [(t.T if n in TRANSPOSED else t)[None] for t in res]

    outs = [total_loss, grad_x.reshape(LOCAL_BATCH, SEQ, D_MODEL)]
    for kind in range(4):
        for n in WEIGHTS:
            outs.append(big_out[n][kind] if n in BIG else small_out[kind][n])
    return tuple(outs)
```

```python
import functools
import math

import jax
import jax.numpy as jnp
import numpy as np
from jax import lax
from jax.experimental import pallas as pl
from jax.experimental.pallas import tpu as pltpu

F32 = jnp.float32
BF16 = jnp.bfloat16
MESH = pl.DeviceIdType.MESH

D_MODEL = 1024
SEQ = 2048
LOCAL_BATCH = 2
TOKENS = LOCAL_BATCH * SEQ
HEAD_DIM = 64
HEADS_PER_GROUP = 4
GROUP_W = HEADS_PER_GROUP * HEAD_DIM
N_GROUPS = 3
DILATIONS = (1, 4, 16)
ATTN_BLOCK = 128
ROPE_DIM = 16
ROPE_THETA = 500000.0
QKV_W = 3 * N_GROUPS * GROUP_W
SSM_W = 512
SSM_STATE_W = 2048
SSM_LANE_BLOCKS = 4
GATE_W = 2 * D_MODEL
D_FF = 2816
RMS_EPS = 1e-6
NEG_INF = -1e30
ADAM_LR, ADAM_B1, ADAM_B2, ADAM_EPS, ADAM_WD, ADAM_STEP = 0.001, 0.9, 0.999, 1e-08, 0.01, 10
N_CHIPS = 4
N_DEV = 8

VMEM_LIMIT = 56 * 1024 * 1024
LANES = 128


def _params(sem=None):
    return pltpu.CompilerParams(dimension_semantics=sem, vmem_limit_bytes=VMEM_LIMIT)


def _pick(n, cap, align=LANES):
    best = None
    for d in range(align, min(n, cap) + 1, align):
        if n % d == 0:
            best = d
    return n if best is None or n <= cap else best


_DIMS = {"nn": (((1,), (0,)), ((), ())), "nt": (((1,), (1,)), ((), ())), "tn": (((0,), (0,)), ((), ()))}


def _dot(a, b, mode):
    return lax.dot_general(a, b, _DIMS[mode], preferred_element_type=F32)


def matmul(a, b, mode, out_dtype, name, add=None, comm=None):
    if mode == "nn":
        (m, k), n = a.shape, b.shape[1]
    elif mode == "nt":
        (m, k), n = a.shape, b.shape[0]
    else:
        (k, m), n = a.shape, b.shape[1]
    tn = _pick(n, 1408 if mode != "tn" else 512)
    tk = _pick(k, 2816) if mode != "tn" else k
    tm = _pick(m, 1408)
    out_bytes = jnp.dtype(out_dtype).itemsize

    def need(tm_):
        return 2 * 2 * (tm_ * tk + tk * tn) + tm_ * tn * (4 + 2 * out_bytes + (8 if add is not None else 0))

    while need(tm) > 40 * 1024 * 1024 and tm % 256 == 0:
        tm //= 2
    nk = k // tk
    a_spec = {"nn": pl.BlockSpec((tm, tk), lambda i, j, kk: (i, kk)),
              "nt": pl.BlockSpec((tm, tk), lambda i, j, kk: (i, kk)),
              "tn": pl.BlockSpec((tk, tm), lambda i, j, kk: (kk, i))}[mode]
    b_spec = {"nn": pl.BlockSpec((tk, tn), lambda i, j, kk: (kk, j)),
              "nt": pl.BlockSpec((tn, tk), lambda i, j, kk: (j, kk)),
              "tn": pl.BlockSpec((tk, tn), lambda i, j, kk: (kk, j))}[mode]
    o_spec = pl.BlockSpec((tm, tn), lambda i, j, kk: (i, j))

    def body(a_ref, b_ref, *rest):
        if add is not None:
            add_ref, o_ref, acc_ref = rest
        else:
            o_ref, acc_ref = rest
        part = _dot(a_ref[...], b_ref[...], mode)
        if nk == 1:
            res = part if add is None else part + add_ref[...]
            o_ref[...] = res.astype(out_dtype)
            return
        kk = pl.program_id(2)

        @pl.when(kk == 0)
        def _():
            acc_ref[...] = part

        @pl.when(kk > 0)
        def _():
            acc_ref[...] += part

        @pl.when(kk == nk - 1)
        def _():
            res = acc_ref[...] if add is None else acc_ref[...] + add_ref[...]
            o_ref[...] = res.astype(out_dtype)

    in_specs = [a_spec, b_spec] + ([o_spec] if add is not None else [])
    args = (a, b) + ((add,) if add is not None else ())
    res = hosted_call(
        body, comm, name, (m // tm, n // tn, nk), in_specs, [o_spec], [jax.ShapeDtypeStruct((m, n), out_dtype)],
        [pltpu.VMEM((tm, tn) if nk > 1 else (8, LANES), F32)], args, ("parallel", "parallel", "arbitrary"))
    return res[0] if comm is None else res


def matmul_rows(a, b, name, fn, extra, outs, accs=(), add=None, comm=None, tm=512):
    (m, k), n = a.shape, b.shape[1]
    n_fixed = 2 + (add is not None)
    row_spec = lambda cols: pl.BlockSpec((tm, cols), lambda i: (i, 0))
    in_specs = [row_spec(k), pl.BlockSpec((k, n), lambda i: (0, 0))] + ([row_spec(n)] if add is not None else [])
    in_specs += [pl.BlockSpec(e.shape, lambda i: (0, 0)) if e.shape[0] == 1 else row_spec(e.shape[1]) for e in extra]
    out_specs = [row_spec(c) for c, _ in outs] + [pl.BlockSpec((1, c), lambda i: (0, 0)) for c in accs]
    out_shape = [jax.ShapeDtypeStruct((m, c), dt) for c, dt in outs] + [jax.ShapeDtypeStruct((1, c), F32) for c in accs]

    def body(*refs):
        rows = _dot(refs[0][...], refs[1][...], "nn")
        if add is not None:
            rows = rows + refs[2][...]
        n_in = n_fixed + len(extra)
        res = fn(rows, *[r[...] for r in refs[n_fixed:n_in]])
        for r, v in zip(refs[n_in:n_in + len(outs)], res[:len(outs)]):
            r[...] = v.astype(r.dtype)
        first = pl.program_id(0) == 0
        for r, v in zip(refs[n_in + len(outs):], res[len(outs):]):
            @pl.when(first)
            def _(r=r, v=v):
                r[...] = v

            @pl.when(jnp.logical_not(first))
            def _(r=r, v=v):
                r[...] += v

    args = (a, b) + ((add,) if add is not None else ()) + tuple(extra)
    return hosted_call(body, comm, name, (m // tm,), in_specs, out_specs, out_shape, [], args, ("arbitrary",))


FFN_TM, FFN_TN = 512, 1408


def ffn_in(h2, wg_t, wu_t, comm=None):
    def body(h_ref, wg_ref, wu_ref, a_ref, b_ref, act_ref):
        hv = h_ref[...]
        a, b = _dot(hv, wg_ref[...], "nt"), _dot(hv, wu_ref[...], "nt")
        a_ref[...] = a.astype(BF16)
        b_ref[...] = b.astype(BF16)
        act_ref[...] = _swiglu_fn(a, b).astype(BF16)

    rows = pl.BlockSpec((FFN_TM, D_MODEL), lambda i, j: (i, 0))
    wts = pl.BlockSpec((FFN_TN, D_MODEL), lambda i, j: (j, 0))
    out = pl.BlockSpec((FFN_TM, FFN_TN), lambda i, j: (i, j))
    return hosted_call(body, comm, "ffn_in", (TOKENS // FFN_TM, D_FF // FFN_TN), [rows, wts, wts], [out] * 3,
                       [jax.ShapeDtypeStruct((TOKENS, D_FF), BF16)] * 3, [], (h2, wg_t, wu_t),
                       ("parallel", "parallel"))


def ffn_in_bwd(dx2_b, wd, a, b):
    def body(dx_ref, wd_ref, a_ref, b_ref, da_ref, db_ref):
        dx = dx_ref[...]
        for lo in range(0, FFN_TN, 512):
            cols = slice(lo, min(lo + 512, FFN_TN))
            dact = _dot(dx, wd_ref[cols, :], "nt")
            _, vjp = jax.vjp(_swiglu_fn, a_ref[:, cols].astype(F32), b_ref[:, cols].astype(F32))
            da, db = vjp(dact)
            da_ref[:, cols] = da.astype(BF16)
            db_ref[:, cols] = db.astype(BF16)

    rows = pl.BlockSpec((FFN_TM, D_MODEL), lambda i, j: (i, 0))
    wts = pl.BlockSpec((FFN_TN, D_MODEL), lambda i, j: (j, 0))
    out = pl.BlockSpec((FFN_TM, FFN_TN), lambda i, j: (i, j))
    return pl.pallas_call(
        body, name="ffn_in_bwd", grid=(TOKENS // FFN_TM, D_FF // FFN_TN), in_specs=[rows, wts, out, out],
        out_specs=[out] * 2, out_shape=[jax.ShapeDtypeStruct((TOKENS, D_FF), BF16)] * 2,
        compiler_params=_params(("parallel", "parallel")),
    )(dx2_b, wd, a, b)


def mix_in_bwd(grads, weights, partial, x, g, skip, comm=None):
    n = len(grads)
    tm = 512

    def body(*refs):
        a_refs, b_refs = refs[:n], refs[n:2 * n]
        part_ref, x_ref, g_ref, skip_ref, gx_ref, dg_ref = refs[2 * n:]
        dh = part_ref[...]
        for a_ref, b_ref in zip(a_refs, b_refs):
            dh = dh + _dot(a_ref[...], b_ref[...], "nn")
        _, vjp = jax.vjp(_rms, x_ref[...], g_ref[...])
        dx, dg = vjp(dh)
        gx_ref[...] = dx + skip_ref[...]
        first = pl.program_id(0) == 0

        @pl.when(first)
        def _():
            dg_ref[...] = dg

        @pl.when(jnp.logical_not(first))
        def _():
            dg_ref[...] += dg

    rows = pl.BlockSpec((tm, D_MODEL), lambda i: (i, 0))
    gain = pl.BlockSpec((1, D_MODEL), lambda i: (0, 0))
    in_specs = [pl.BlockSpec((tm, a.shape[1]), lambda i: (i, 0)) for a in grads]
    in_specs += [pl.BlockSpec(b.shape, lambda i: (0, 0)) for b in weights]
    return hosted_call(
        body, comm, "mix_in_bwd", (TOKENS // tm,), in_specs + [rows, rows, gain, rows], [rows, gain],
        [jax.ShapeDtypeStruct((TOKENS, D_MODEL), F32), jax.ShapeDtypeStruct((1, D_MODEL), F32)], [],
        (*grads, *weights, partial, x, g, skip), ("arbitrary",))


def rowwise(fn, ins, outs, name, accs=(), tm=256, rows=TOKENS, comm=None):
    in_specs, args = [], []
    for item in ins:
        arr, width, blk = item if isinstance(item, tuple) else (item, None, 0)
        if arr.ndim == 3:
            for k in range(arr.shape[0]):
                in_specs.append(pl.BlockSpec((None, tm, arr.shape[2]), functools.partial(lambda i, k_: (k_, i, 0), k_=k)))
                args.append(arr)
            continue
        if arr.shape[0] == 1:
            in_specs.append(pl.BlockSpec(arr.shape, lambda i: (0, 0)))
        elif width is None:
            in_specs.append(pl.BlockSpec((tm, arr.shape[1]), lambda i: (i, 0)))
        else:
            in_specs.append(pl.BlockSpec((tm, width), functools.partial(lambda i, blk_: (i, blk_), blk_=blk)))
        args.append(arr)
    out_specs = [pl.BlockSpec((tm, c), lambda i: (i, 0)) for c, _ in outs]
    out_specs += [pl.BlockSpec((1, c), lambda i: (0, 0)) for c in accs]
    out_shape = [jax.ShapeDtypeStruct((rows, c), dt) for c, dt in outs]
    out_shape += [jax.ShapeDtypeStruct((1, c), F32) for c in accs]
    n_in, n_out = len(args), len(outs)
    c_ins, c_outs, c_sems = _comm_operands(comm)

    def body(*refs):
        refs, c_refs = _comm_refs(comm, refs, n_in, n_out + len(accs))
        step = pl.program_id(0)
        _comm_begin(comm, c_refs, step, rows // tm)
        res = fn(*[r[...] for r in refs[:n_in]])
        for r, v in zip(refs[n_in:n_in + n_out], res[:n_out]):
            r[...] = v.astype(r.dtype)
        first = step == 0
        for r, v in zip(refs[n_in + n_out:], res[n_out:]):
            @pl.when(first)
            def _(r=r, v=v):
                r[...] = v

            @pl.when(jnp.logical_not(first))
            def _(r=r, v=v):
                r[...] += v
        _comm_end(comm, c_refs, step, rows // tm)

    return pl.pallas_call(
        body, name=name, grid=(rows // tm,), in_specs=in_specs + [ANY] * len(c_ins),
        out_specs=out_specs + [ANY] * len(c_outs), out_shape=out_shape + c_outs, scratch_shapes=c_sems,
        compiler_params=_params(("arbitrary",)),
    )(*args, *c_ins)


def first_norm(x, g, others, comm=None):
    tm, n = 256, len(others)

    def body(x_ref, g_ref, *rest):
        srcs, h_ref, dsts = rest[:n], rest[n], rest[n + 1:]
        h_ref[...] = _rms(x_ref[...], g_ref[...]).astype(BF16)
        for k, (s, d) in enumerate(zip(srcs, dsts)):
            @pl.when(pl.program_id(0) == k)
            def _(s=s, d=d):
                d[...] = s[...].astype(BF16)

    rows = pl.BlockSpec((tm, D_MODEL), lambda i: (i, 0))
    whole = [pl.BlockSpec(a.shape, lambda i: (0, 0)) for a in others]
    return hosted_call(
        body, comm, "norm_mix", (TOKENS // tm,), [rows, pl.BlockSpec((1, D_MODEL), lambda i: (0, 0))] + whole,
        [rows] + whole, [jax.ShapeDtypeStruct((TOKENS, D_MODEL), BF16)]
        + [jax.ShapeDtypeStruct(a.shape, BF16) for a in others], [], (x, g, *others), ("arbitrary",))


def _rms(x, g):
    return x * lax.rsqrt(jnp.mean(x * x, axis=-1, keepdims=True) + RMS_EPS) * g


def _colsum(v):
    return jnp.sum(v, axis=0, keepdims=True)


PAIR_W = 2 * HEAD_DIM
N_PAIRS = HEADS_PER_GROUP // 2


def _qkv_order(w_t, back=False):
    dims = (N_PAIRS, N_GROUPS, 3) if back else (3, N_GROUPS, N_PAIRS)
    return w_t.reshape(dims + (PAIR_W, w_t.shape[1])).transpose(2, 1, 0, 3, 4).reshape(QKV_W, w_t.shape[1])


def _rope_tables():
    half = ROPE_DIM // 2
    inv = np.power(np.float32(ROPE_THETA), -np.arange(half, dtype=np.float32) * np.float32(2.0 / ROPE_DIM))
    ang = (np.arange(SEQ, dtype=np.float32)[:, None] * inv[None, :]).astype(np.float32)
    cos, sin = np.cos(ang), np.sin(ang)
    zeros = np.zeros((SEQ, HEAD_DIM - ROPE_DIM), np.float32)
    zh = np.zeros((SEQ, half), np.float32)
    c = np.concatenate([cos, cos, zeros + 1.0], axis=1)
    sa = np.concatenate([-sin, zh, zeros], axis=1)
    sb = np.concatenate([zh, sin, zeros], axis=1)
    return [jnp.asarray(np.tile(t, (1, 2)), F32) for t in (c, sa, sb)]


def _rope_fwd(x, c, sa, sb):
    return x * c + pltpu.roll(x, PAIR_W - 8, 1) * sa + pltpu.roll(x, 8, 1) * sb


def _rope_bwd(dy, c, sa, sb):
    return dy * c + pltpu.roll(dy * sb, PAIR_W - 8, 1) + pltpu.roll(dy * sa, 8, 1)


def _band_masks():
    row = lax.broadcasted_iota(jnp.int32, (ATTN_BLOCK, ATTN_BLOCK), 0)
    col = lax.broadcasted_iota(jnp.int32, (ATTN_BLOCK, ATTN_BLOCK), 1)
    return col <= row, col >= row


def _stack_rows(t):
    return jnp.concatenate([t, t], axis=0)


def _stack_heads(t, first_head):
    return jnp.concatenate([jnp.where(first_head, t, 0), jnp.where(first_head, 0, t)], axis=0)


def _per_head(fn):
    return jnp.concatenate([fn(slice(h * HEAD_DIM, (h + 1) * HEAD_DIM)) for h in range(2)], axis=1)


def _slab_spec(kind):
    return pl.BlockSpec((None, SEQ, PAIR_W), lambda b, p, g: (b, 0, p * 3 * N_GROUPS + g * 3 + kind))


_TABLE_SPEC = pl.BlockSpec((SEQ, PAIR_W), lambda b, p, g: (0, 0))
_PAIR_SPEC = pl.BlockSpec((None, SEQ, PAIR_W), lambda b, p, g: (b, 0, p))


def _block_rows(dil, r, n):
    return pl.ds(n * (ATTN_BLOCK * dil) + r, ATTN_BLOCK, stride=dil)


def proj_qkv(h, w_qkv_t, tables, comm=None):
    tm = 1024
    pair_w = QKV_W // N_PAIRS
    scale = HEAD_DIM ** -0.5

    def body(h_ref, w_ref, c_ref, sa_ref, sb_ref, o_ref):
        rows = _dot(h_ref[...], w_ref[...], "nt")
        c, sa, sb = c_ref[...], sa_ref[...], sb_ref[...]
        for blk in range(pair_w // PAIR_W):
            cols = slice(blk * PAIR_W, (blk + 1) * PAIR_W)
            x = rows[:, cols]
            if blk % 3 == 0:
                x = _rope_fwd(x, c, sa, sb) * scale
            elif blk % 3 == 1:
                x = _rope_fwd(x, c, sa, sb)
            o_ref[:, cols] = x

    table = pl.BlockSpec((tm, PAIR_W), lambda i, j, : (i % (SEQ // tm), 0))
    res = hosted_call(
        body, comm, "proj_qkv", (TOKENS // tm, N_PAIRS),
        [pl.BlockSpec((tm, D_MODEL), lambda i, j: (i, 0)), pl.BlockSpec((pair_w, D_MODEL), lambda i, j: (j, 0)),
         table, table, table],
        [pl.BlockSpec((tm, pair_w), lambda i, j: (i, j))], [jax.ShapeDtypeStruct((TOKENS, QKV_W), F32)], [],
        (h, w_qkv_t, *tables), ("parallel", "parallel"))
    return res[0] if comm is None else res


def attn_fwd(qkv, comm=None):
    def body(qs, ks, v_ref, attn_b_ref, attn_ref, lse_ref, o0, o1, o2, l0, l1, l2):
        g = pl.program_id(2)
        cur_mask, prev_mask = _band_masks()
        first_head = lax.broadcasted_iota(jnp.int32, (ATTN_BLOCK, PAIR_W), 1) < HEAD_DIM

        def run(dil, o_slab, l_slab):
            nb = SEQ // dil // ATTN_BLOCK

            def block(idx, carry):
                r, n = lax.div(idx, nb), lax.rem(idx, nb)
                cur, prev = _block_rows(dil, r, n), _block_rows(dil, r, jnp.maximum(n - 1, 0))
                q = qs[cur, :].astype(BF16)
                kc, kp = ks[cur, :].astype(BF16), ks[prev, :].astype(BF16)
                vc, vp = v_ref[cur, :].astype(BF16), v_ref[prev, :].astype(BF16)
                q2 = _stack_heads(q, first_head)
                mask = _stack_rows(jnp.concatenate([jnp.logical_and(prev_mask, n > 0), cur_mask], axis=1))
                s2 = jnp.where(mask, _dot(q2, jnp.concatenate([kp, kc], axis=0), "nt"), NEG_INF)
                m = jnp.max(s2, axis=-1, keepdims=True)
                vcat, two = jnp.concatenate([vp, vc], axis=0), _stack_rows(first_head)
                vext = jnp.concatenate([jnp.where(two, vcat, 1), jnp.where(two, 1, vcat)], axis=1)
                r2 = _dot(jnp.exp(s2 - m).astype(BF16), vext, "nn")
                r0, r1 = r2[:ATTN_BLOCK, :PAIR_W], r2[ATTN_BLOCK:, PAIR_W:]
                num = jnp.where(first_head, r0, r1)
                den = pltpu.roll(jnp.where(first_head, r1, r0), HEAD_DIM, 1)
                o_slab[cur, :] = num / den
                l_slab[cur, :] = jnp.where(first_head, m[:ATTN_BLOCK], m[ATTN_BLOCK:]) + jnp.log(den)
                return carry

            lax.fori_loop(0, SEQ // ATTN_BLOCK, block, 0, unroll=4)

        for gi, (o_slab, l_slab) in enumerate(((o0, l0), (o1, l1), (o2, l2))):
            @pl.when(g == gi)
            def _(gi=gi, o_slab=o_slab, l_slab=l_slab):
                run(DILATIONS[gi], o_slab, l_slab)

        @pl.when(g == N_GROUPS - 1)
        def _():
            a, b, cc = l0[...], l1[...], l2[...]
            m = jnp.maximum(jnp.maximum(a, b), cc)
            e0, e1, e2 = jnp.exp(a - m), jnp.exp(b - m), jnp.exp(cc - m)
            tot = e0 + e1 + e2
            attn = (e0 * o0[...] + e1 * o1[...] + e2 * o2[...]) / tot
            attn_ref[...] = attn
            attn_b_ref[...] = attn.astype(BF16)
            lse_ref[...] = m + jnp.log(tot)

    shape = (LOCAL_BATCH, SEQ, GROUP_W)
    slab = pltpu.VMEM((SEQ, PAIR_W), F32)
    return hosted_call(
        body, comm, "attn_fwd", (LOCAL_BATCH, N_PAIRS, N_GROUPS),
        [_slab_spec(0), _slab_spec(1), _slab_spec(2)], [_PAIR_SPEC] * 3,
        [jax.ShapeDtypeStruct(shape, BF16), jax.ShapeDtypeStruct(shape, F32), jax.ShapeDtypeStruct(shape, F32)],
        [slab] * 6, (qkv, qkv, qkv), ("parallel", "parallel", "arbitrary"))


def attn_bwd(qkv, tables, dattn, attn, lse, comm=None):
    scale = HEAD_DIM ** -0.5

    def body(qs, ks, v_ref, c_ref, sa_ref, sb_ref, do_ref, out_ref, lse_ref, dqkv_ref, dl, dq_s, dk_s, dv_s):
        g = pl.program_id(2)
        c, sa, sb = c_ref[...], sa_ref[...], sb_ref[...]

        @pl.when(g == 0)
        def _():
            prod = do_ref[...] * out_ref[...]
            dl[...] = _per_head(
                lambda sl: jnp.broadcast_to(jnp.sum(prod[:, sl], axis=-1, keepdims=True), (SEQ, HEAD_DIM)))

        cur_mask, prev_mask = _band_masks()
        first_head = lax.broadcasted_iota(jnp.int32, (ATTN_BLOCK, PAIR_W), 1) < HEAD_DIM

        def run(dil):
            nb = SEQ // dil // ATTN_BLOCK

            def block(idx, carry):
                r, n = lax.div(idx, nb), lax.rem(idx, nb)
                cur = _block_rows(dil, r, n)
                prev = _block_rows(dil, r, jnp.maximum(n - 1, 0))
                nxt = _block_rows(dil, r, jnp.minimum(n + 1, nb - 1))
                q0, q1 = qs[cur, :].astype(BF16), qs[nxt, :].astype(BF16)
                kp, kc = ks[prev, :].astype(BF16), ks[cur, :].astype(BF16)
                vp, vc = v_ref[prev, :].astype(BF16), v_ref[cur, :].astype(BF16)
                do0, do1 = do_ref[cur, :].astype(BF16), do_ref[nxt, :].astype(BF16)
                lse0, lse1, dl0, dl1 = lse_ref[cur, :], lse_ref[nxt, :], dl[cur, :], dl[nxt, :]
                has_prev = jnp.logical_and(prev_mask, n > 0)
                has_next = jnp.logical_and(prev_mask, n < nb - 1)

                def per_row(t):
                    return jnp.concatenate([t[:, 0:1], t[:, HEAD_DIM:HEAD_DIM + 1]], axis=0)

                q20, q21 = _stack_heads(q0, first_head), _stack_heads(q1, first_head)
                do20, do21 = _stack_heads(do0, first_head), _stack_heads(do1, first_head)
                kcat, vcat = jnp.concatenate([kp, kc], axis=0), jnp.concatenate([vp, vc], axis=0)
                mask0 = _stack_rows(jnp.concatenate([has_prev, cur_mask], axis=1))
                p0 = jnp.where(mask0, jnp.exp(_dot(q20, kcat, "nt") - per_row(lse0)), 0.0)
                ds0 = (p0 * (_dot(do20, vcat, "nt") - per_row(dl0))).astype(BF16)
                p1 = jnp.where(_stack_rows(has_next), jnp.exp(_dot(q21, kc, "nt") - per_row(lse1)), 0.0)
                ds1 = (p1 * (_dot(do21, vc, "nt") - per_row(dl1))).astype(BF16)
                dq2 = _dot(ds0, kcat, "nn")
                dq_s[cur, :] = jnp.where(first_head, dq2[:ATTN_BLOCK], dq2[ATTN_BLOCK:])
                ds_cur = jnp.concatenate([ds0[:, ATTN_BLOCK:], ds1], axis=0)
                p_cur = jnp.concatenate([p0[:, ATTN_BLOCK:], p1], axis=0).astype(BF16)
                dk_s[cur, :] = _dot(ds_cur, jnp.concatenate([q20, q21], axis=0), "tn")
                dv_s[cur, :] = _dot(p_cur, jnp.concatenate([do20, do21], axis=0), "tn")
                return carry

            lax.fori_loop(0, SEQ // ATTN_BLOCK, block, 0, unroll=2)

        for gi in range(N_GROUPS):
            @pl.when(g == gi)
            def _(gi=gi):
                run(DILATIONS[gi])

        dqkv_ref[:, 0:PAIR_W] = _rope_bwd(dq_s[...] * scale, c, sa, sb).astype(BF16)
        dqkv_ref[:, PAIR_W:2 * PAIR_W] = _rope_bwd(dk_s[...], c, sa, sb).astype(BF16)
        dqkv_ref[:, 2 * PAIR_W:] = dv_s[...].astype(BF16)

    slab = pltpu.VMEM((SEQ, PAIR_W), F32)
    return hosted_call(
        body, comm, "attn_bwd", (LOCAL_BATCH, N_PAIRS, N_GROUPS),
        [_slab_spec(0), _slab_spec(1), _slab_spec(2), _TABLE_SPEC, _TABLE_SPEC, _TABLE_SPEC,
         _PAIR_SPEC, _PAIR_SPEC, _PAIR_SPEC],
        [pl.BlockSpec((None, SEQ, 3 * PAIR_W), lambda b, p, g: (b, 0, p * N_GROUPS + g))],
        [jax.ShapeDtypeStruct((LOCAL_BATCH, SEQ, QKV_W), BF16)],
        [slab] * 4, (qkv, qkv, qkv, *tables, dattn, attn, lse), ("parallel", "parallel", "arbitrary"))


def _discretize(lr, li, log_dt, br, bi):
    dt = jnp.exp(log_dt)
    mag = jnp.exp(lr * dt)
    ab_re, ab_im = mag * jnp.cos(li * dt), mag * jnp.sin(li * dt)
    den = lr * lr + li * li
    nr, ni = ab_re - 1.0, ab_im
    f_re = (nr * lr + ni * li) / den
    f_im = (ni * lr - nr * li) / den
    return ab_re, ab_im, f_re[None] * br - f_im[None] * bi, f_re[None] * bi + f_im[None] * br


def ssm_prep(lr, li, log_dt, br, bi):
    def body(lr_ref, li_ref, dt_ref, br_ref, bi_ref, *outs):
        for o, v in zip(outs, _discretize(lr_ref[...], li_ref[...], dt_ref[...], br_ref[...], bi_ref[...])):
            o[...] = v
    shapes = [lr, li, br, bi]
    return pl.pallas_call(body, name="ssm_prep",
                          out_shape=[jax.ShapeDtypeStruct(s.shape, F32) for s in shapes])(lr, li, log_dt, br, bi)


def ssm_prep_bwd(lr, li, log_dt, br, bi, g_ab_re, g_ab_im, g_bb_re, g_bb_im):
    def body(lr_ref, li_ref, dt_ref, br_ref, bi_ref, g0, g1, g2, g3, *outs):
        _, vjp = jax.vjp(_discretize, lr_ref[...], li_ref[...], dt_ref[...], br_ref[...], bi_ref[...])
        for o, v in zip(outs, vjp((g0[...], g1[...], g2[...], g3[...]))):
            o[...] = v
    shapes = [lr, li, log_dt, br, bi]
    return pl.pallas_call(body, name="ssm_prep_bwd",
                          out_shape=[jax.ShapeDtypeStruct(s.shape, F32) for s in shapes])(
        lr, li, log_dt, br, bi, g_ab_re, g_ab_im, g_bb_re, g_bb_im)


def _block_diag(t):
    per = SSM_STATE_W // SSM_LANE_BLOCKS // 64
    g = t.transpose(1, 0, 2).reshape(SSM_LANE_BLOCKS, per, 16, 64)
    eye = jnp.eye(per, dtype=t.dtype)
    return jnp.einsum("jgcn,gh->jgchn", g, eye).reshape(SSM_LANE_BLOCKS, per * 16, per * 64)


def _block_diag_t(m):
    per = SSM_STATE_W // SSM_LANE_BLOCKS // 64
    m5 = m.reshape(SSM_LANE_BLOCKS, per, 16, per, 64)
    d = jnp.einsum("jgchn,gh->jgcn", m5, jnp.eye(per, dtype=m.dtype))
    return d.reshape(SSM_LANE_BLOCKS * per, 16, 64).transpose(1, 0, 2)


def _cmul(ar, ai, br, bi):
    return ar * br - ai * bi, ar * bi + ai * br


def _power_tables(ar, ai, reverse):
    width = ar.shape[1]
    row = lax.broadcasted_iota(jnp.int32, (8, width), 0)
    pows = [(ar, ai)]
    for _ in range(7):
        pows.append(_cmul(pows[-1][0], pows[-1][1], ar, ai))
    steps = []
    for k in (1, 2, 4):
        keep = (row >= k) if not reverse else (row < 8 - k)
        steps.append((jnp.where(keep, pows[k - 1][0], 0.0), jnp.where(keep, pows[k - 1][1], 0.0)))
    cr = jnp.zeros((8, width), F32)
    ci = jnp.zeros((8, width), F32)
    for i in range(8):
        pr, pi = pows[i] if not reverse else pows[7 - i]
        cr = jnp.where(row == i, pr, cr)
        ci = jnp.where(row == i, pi, ci)
    return steps, (cr, ci)


SCAN_CHUNK = 2048
STATE_BLOCK = SSM_STATE_W // SSM_LANE_BLOCKS
CHAN_BLOCK = SSM_W // SSM_LANE_BLOCKS


def ssm_fwd(u, ab_re, ab_im, bb_re, bb_im, cb_re, cb_im, d_skip, comm=None):
    nt = SEQ // SCAN_CHUNK
    chan = pl.BlockSpec((None, SCAN_CHUNK, CHAN_BLOCK), lambda b, j, t: (b, t, j))
    state = pl.BlockSpec((None, SCAN_CHUNK, STATE_BLOCK), lambda b, j, t: (b, t, j))
    mat = pl.BlockSpec((None, CHAN_BLOCK, STATE_BLOCK), lambda b, j, t: (j, 0, 0))
    lane = pl.BlockSpec((1, STATE_BLOCK), lambda b, j, t: (0, j))
    dsp = pl.BlockSpec((1, CHAN_BLOCK), lambda b, j, t: (0, j))

    def body(u_ref, ar_ref, ai_ref, bbr_ref, bbi_ref, cbr_ref, cbi_ref, d_ref, y_ref, yg_ref, xr_ref, xi_ref,
             car_r, car_i):
        @pl.when(pl.program_id(2) == 0)
        def _():
            car_r[...] = jnp.zeros_like(car_r)
            car_i[...] = jnp.zeros_like(car_i)

        steps, (pr, pi) = _power_tables(ar_ref[...], ai_ref[...], reverse=False)
        uf = u_ref[...]
        ub = uf.astype(BF16)
        xr_ref[...] = _dot(ub, bbr_ref[...], "nn")
        xi_ref[...] = _dot(ub, bbi_ref[...], "nn")

        def tile(i, carry):
            cr, ci = carry
            sl = pl.ds(pl.multiple_of(i * 8, 8), 8)
            br, bi = xr_ref[sl, :], xi_ref[sl, :]
            for k, (sr, si) in zip((1, 2, 4), steps):
                tr, ti = _cmul(sr, si, pltpu.roll(br, k, 0), pltpu.roll(bi, k, 0))
                br, bi = br + tr, bi + ti
            tr, ti = _cmul(pr, pi, cr, ci)
            br, bi = br + tr, bi + ti
            xr_ref[sl, :] = br
            xi_ref[sl, :] = bi
            return br[7:8, :], bi[7:8, :]

        cr, ci = lax.fori_loop(0, SCAN_CHUNK // 8, tile, (car_r[0:1, :], car_i[0:1, :]), unroll=4)
        car_r[0:1, :] = cr
        car_i[0:1, :] = ci
        y = (_dot(xr_ref[...].astype(BF16), cbr_ref[...], "nt") - _dot(xi_ref[...].astype(BF16), cbi_ref[...], "nt")
             + d_ref[...] * uf)
        y_ref[...] = y
        yg_ref[...] = jax.nn.gelu(y).astype(BF16)

    return hosted_call(
        body, comm, "ssm_fwd", (LOCAL_BATCH, SSM_LANE_BLOCKS, nt),
        [chan, lane, lane, mat, mat, mat, mat, dsp], [chan, chan, state, state],
        [jax.ShapeDtypeStruct((LOCAL_BATCH, SEQ, SSM_W), F32), jax.ShapeDtypeStruct((LOCAL_BATCH, SEQ, SSM_W), BF16),
         jax.ShapeDtypeStruct((LOCAL_BATCH, SEQ, SSM_STATE_W), F32),
         jax.ShapeDtypeStruct((LOCAL_BATCH, SEQ, SSM_STATE_W), F32)],
        [pltpu.VMEM((8, STATE_BLOCK), F32), pltpu.VMEM((8, STATE_BLOCK), F32)],
        (u, ab_re, ab_im, bb_re, bb_im, cb_re, cb_im, d_skip), ("parallel", "parallel", "arbitrary"))


def ssm_bwd(dyg, y, u, xr, xi, ab_re, ab_im, bb_re, bb_im, cb_re, cb_im, d_skip, comm=None):
    nt = SEQ // SCAN_CHUNK
    ntile = SCAN_CHUNK // 8

    def rev(t):
        return nt - 1 - t

    chan = pl.BlockSpec((None, SCAN_CHUNK, CHAN_BLOCK), lambda j, b, t: (b, rev(t), j))
    state = pl.BlockSpec((None, SCAN_CHUNK, STATE_BLOCK), lambda j, b, t: (b, rev(t), j))
    before = pl.BlockSpec((None, 8, STATE_BLOCK), lambda j, b, t: (b, jnp.maximum(rev(t) * ntile - 1, 0), j))
    mat = pl.BlockSpec((None, CHAN_BLOCK, STATE_BLOCK), lambda j, b, t: (j, 0, 0))
    lane = pl.BlockSpec((1, STATE_BLOCK), lambda j, b, t: (0, j))
    lane8 = pl.BlockSpec((8, STATE_BLOCK), lambda j, b, t: (0, j))
    dsp = pl.BlockSpec((1, CHAN_BLOCK), lambda j, b, t: (0, j))

    def body(dyg_ref, y_ref, u_ref, xr_ref, xi_ref, xrb_ref, xib_ref, ar_ref, ai_ref, bbr_ref, bbi_ref, cbr_ref,
             cbi_ref, d_ref, du_ref, dcbr_ref, dcbi_ref, dbbr_ref, dbbi_ref, dd_ref, dar_ref, dai_ref,
             lam_r, lam_i, car_r, car_i):
        b, t = pl.program_id(1), pl.program_id(2)
        first = jnp.logical_and(b == 0, t == 0)

        @pl.when(t == 0)
        def _():
            car_r[...] = jnp.zeros_like(car_r)
            car_i[...] = jnp.zeros_like(car_i)

        @pl.when(first)
        def _():
            for r in (dcbr_ref, dcbi_ref, dbbr_ref, dbbi_ref, dd_ref, dar_ref, dai_ref):
                r[...] = jnp.zeros_like(r)

        steps, (pr, pi) = _power_tables(ar_ref[...], -ai_ref[...], reverse=True)
        uf = u_ref[...]
        _, gelu_vjp = jax.vjp(jax.nn.gelu, y_ref[...])
        dy = gelu_vjp(dyg_ref[...])[0]
        dyb = dy.astype(BF16)
        dd_ref[...] += _colsum(dy * uf)
        lam_r[...] = _dot(dyb, cbr_ref[...], "nn")
        lam_i[...] = -_dot(dyb, cbi_ref[...], "nn")
        dcbr_ref[...] += _dot(dyb, xr_ref[...].astype(BF16), "tn")
        dcbi_ref[...] -= _dot(dyb, xi_ref[...].astype(BF16), "tn")
        row0 = lax.broadcasted_iota(jnp.int32, (8, STATE_BLOCK), 0) == 0
        has_before = rev(t) > 0
        xrb = jnp.where(has_before, xrb_ref[...], 0.0)
        xib = jnp.where(has_before, xib_ref[...], 0.0)

        def tile(s, carry):
            cr, ci, acc_r, acc_i = carry
            i = ntile - 1 - s
            sl = pl.ds(pl.multiple_of(i * 8, 8), 8)
            gr, gi = lam_r[sl, :], lam_i[sl, :]
            for k, (sr, si) in zip((1, 2, 4), steps):
                tr, ti = _cmul(sr, si, pltpu.roll(gr, 8 - k, 0), pltpu.roll(gi, 8 - k, 0))
                gr, gi = gr + tr, gi + ti
            tr, ti = _cmul(pr, pi, cr, ci)
            gr, gi = gr + tr, gi + ti
            lam_r[sl, :] = gr
            lam_i[sl, :] = gi
            sp = pl.ds(pl.multiple_of(jnp.maximum(i - 1, 0) * 8, 8), 8)
            pvr = jnp.where(i > 0, xr_ref[sp, :], xrb)
            pvi = jnp.where(i > 0, xi_ref[sp, :], xib)
            xsr = jnp.where(row0, pltpu.roll(pvr, 1, 0), pltpu.roll(xr_ref[sl, :], 1, 0))
            xsi = jnp.where(row0, pltpu.roll(pvi, 1, 0), pltpu.roll(xi_ref[sl, :], 1, 0))
            acc_r = acc_r + xsr * gr + xsi * gi
            acc_i = acc_i + xsr * gi - xsi * gr
            return gr[0:1, :], gi[0:1, :], acc_r, acc_i

        zero = jnp.zeros((8, STATE_BLOCK), F32)
        cr, ci, acc_r, acc_i = lax.fori_loop(0, ntile, tile, (car_r[0:1, :], car_i[0:1, :], zero, zero), unroll=2)
        car_r[0:1, :] = cr
        car_i[0:1, :] = ci
        dar_ref[...] += acc_r
        dai_ref[...] += acc_i
        lrb, lib = lam_r[...].astype(BF16), lam_i[...].astype(BF16)
        du = _dot(lrb, bbr_ref[...], "nt") + _dot(lib, bbi_ref[...], "nt") + d_ref[...] * dy
        du_ref[...] = du.astype(BF16)
        ub = uf.astype(BF16)
        dbbr_ref[...] += _dot(ub, lrb, "tn")
        dbbi_ref[...] += _dot(ub, lib, "tn")

    mat_shape = jax.ShapeDtypeStruct((SSM_LANE_BLOCKS, CHAN_BLOCK, STATE_BLOCK), F32)
    return hosted_call(
        body, comm, "ssm_bwd", (SSM_LANE_BLOCKS, LOCAL_BATCH, nt),
        [chan, chan, chan, state, state, before, before, lane, lane, mat, mat, mat, mat, dsp],
        [chan, mat, mat, mat, mat, dsp, lane8, lane8],
        [jax.ShapeDtypeStruct((LOCAL_BATCH, SEQ, SSM_W), BF16), mat_shape, mat_shape, mat_shape, mat_shape,
         jax.ShapeDtypeStruct((1, SSM_W), F32), jax.ShapeDtypeStruct((8, SSM_STATE_W), F32),
         jax.ShapeDtypeStruct((8, SSM_STATE_W), F32)],
        [pltpu.VMEM((SCAN_CHUNK, STATE_BLOCK), F32), pltpu.VMEM((SCAN_CHUNK, STATE_BLOCK), F32),
         pltpu.VMEM((8, STATE_BLOCK), F32), pltpu.VMEM((8, STATE_BLOCK), F32)],
        (dyg, y, u, xr, xi, xr, xi, ab_re, ab_im, bb_re, bb_im, cb_re, cb_im, d_skip),
        ("parallel", "arbitrary", "arbitrary"))


def _merge_fn(g0, g1, attn_d, za, zb):
    return jax.nn.sigmoid(g0) * attn_d + jax.nn.sigmoid(g1) * (za * jax.nn.sigmoid(zb))


def _swiglu_fn(a, b):
    return jax.nn.silu(a) * b


def _own_slot(slots, shard):
    me = 2 * lax.axis_index("x") + lax.axis_index("y")
    mine = lax.broadcasted_iota(jnp.int32, (N_CHIPS, 1, 1), 0) == me
    return jnp.where(mine, shard[None], slots)


def _reduce_start(names, gw, shard_shapes):
    return swap_comm([_to_slots(n, gw[n], shard_shapes[n]) for n in names])


def _reduce_chip(names, swap, got, core):
    return exchange_comm([add_halves(n, g, r, core) for n, g, r in zip(names, swap.ins, got)])


def local_step(x, target, shards, small, core):
    g_mix, g_ffn, g_final = small["norm_mix_g"], small["norm_ffn_g"], small["norm_final_g"]
    tables = _rope_tables()
    seqs = lambda t: t.reshape(LOCAL_BATCH, SEQ, t.shape[-1])
    toks = lambda t: t.reshape(TOKENS, t.shape[-1])
    shard_shapes = {n: s.shape for n, s in shards.items()}
    w = {}

    def gather(names):
        return gather_comm([shards[n] for n in names])

    def arrived(names, slots, own=None):
        for n, s in zip(names, slots):
            w[n] = _from_slots(n, s if own is None else _own_slot(s, own))

    later = [n for n in BIG if n != "w_in"]
    sems, w_in_shard, land, token = gather_start(shards["w_in"].astype(BF16))
    zero = token[0, 0]
    h, *rest = first_norm(x, g_mix + zero, [shards[n] for n in later])
    shards = dict(shards)
    shards.update(zip(later, rest))
    br_t = small["ssm_b_re"].transpose(2, 0, 1)
    bi_t = small["ssm_b_im"].transpose(2, 0, 1)
    log_dt = small["ssm_log_dt"].reshape(32, 1)
    ab_re, ab_im, bb_re_t, bb_im_t = ssm_prep(small["ssm_a_re"] + zero, small["ssm_a_im"], log_dt, br_t, bi_t)
    ab = [ab_re.reshape(1, SSM_STATE_W), ab_im.reshape(1, SSM_STATE_W)]
    bb = [_block_diag(bb_re_t).astype(BF16), _block_diag(bb_im_t).astype(BF16)]
    cb = [_block_diag((small["ssm_c_re"] + zero).transpose(1, 0, 2)).astype(BF16),
          _block_diag((small["ssm_c_im"] + zero).transpose(1, 0, 2)).astype(BF16)]
    d_skip = small["ssm_d"].reshape(1, SSM_W)
    w_in_shard, land = gather_wait(sems, w_in_shard, land, [h] + bb + cb)
    arrived(["w_in"], [handover(land)], own=w_in_shard)
    w_qkv, w_u, w_gate = _qkv_order(w["w_in"][:QKV_W]), w["w_in"][QKV_W:QKV_W + SSM_W], w["w_in"][QKV_W + SSM_W:]
    qkv, *slots = proj_qkv(h, w_qkv, tables, comm=gather(["w_attn_out", "w_glu"]))
    arrived(["w_attn_out", "w_glu"], slots)
    qkv = seqs(qkv)
    u = seqs(matmul(h, w_u, "nt", F32, "proj_u"))
    gl, *slots = matmul(h, w_gate, "nt", BF16, "proj_gate", comm=gather(["w_out"]))
    arrived(["w_out"], slots)
    attn_b, attn, lse, *slots = attn_fwd(qkv, comm=gather(["w_ffn_gate"]))
    arrived(["w_ffn_gate"], slots)
    attn_b = toks(attn_b)
    attn_d = matmul(attn_b, w["w_attn_out"], "nn", F32, "attn_out")
    y, yg, xr, xi, *slots = ssm_fwd(u, *ab, *bb, *cb, d_skip, comm=gather(["w_ffn_up"]))
    arrived(["w_ffn_up"], slots)
    yg2 = toks(yg)
    z = matmul(yg2, w["w_glu"], "nn", BF16, "glu")
    gate_ins = [(gl, D_MODEL, 0), (gl, D_MODEL, 1), attn_d, (z, D_MODEL, 0), (z, D_MODEL, 1)]
    (merged,) = rowwise(lambda *v: (_merge_fn(*[t.astype(F32) for t in v]),), gate_ins, [(D_MODEL, BF16)], "merge")
    x1, h2 = matmul_rows(merged, w["w_out"], "out_proj", lambda rows, g: (rows, _rms(rows, g)), [g_ffn],
                         [(D_MODEL, F32), (D_MODEL, BF16)], add=x)
    a, b, act, *slots = ffn_in(h2, w["w_ffn_gate"], w["w_ffn_up"], comm=gather(["w_ffn_down"]))
    arrived(["w_ffn_down"], slots)

    def final_fn(xv, g, tgt):
        yv, vjp = jax.vjp(_rms, xv, g)
        err = yv - tgt
        dx, dg = vjp(err * (1.0 / D_MODEL))
        loss = 0.5 * jnp.sum(jnp.mean(err * err, axis=-1, keepdims=True), axis=0, keepdims=True)
        return dx, dx, dg, jnp.broadcast_to(loss, (1, LANES))

    dx2, dx2_b, dg_final, loss = matmul_rows(act, w["w_ffn_down"], "ffn_down_loss", final_fn, [g_final, target],
                                             [(D_MODEL, F32), (D_MODEL, BF16)], accs=(D_MODEL, LANES), add=x1)
    gw, parts = {}, {}
    gw["w_ffn_down"] = matmul(act, dx2_b, "tn", F32, "d_ffn_down")
    da_b, db_b = ffn_in_bwd(dx2_b, w["w_ffn_down"], a, b)
    gw["w_ffn_gate"] = matmul(da_b, h2, "tn", F32, "d_ffn_gate")
    gw["w_ffn_up"] = matmul(db_b, h2, "tn", F32, "d_ffn_up")
    ffn = ["w_ffn_down", "w_ffn_gate", "w_ffn_up"]
    swap = _reduce_start(ffn[:2], gw, shard_shapes)
    dh2, *got = matmul(da_b, w["w_ffn_gate"], "nn", F32, "d_h2_gate", comm=swap)
    ffn_exchange = [_reduce_chip(ffn[:2], swap, got, core)]
    swap = _reduce_start(ffn[2:], gw, shard_shapes)

    def norm_bwd(dh, xv, g, skip):
        _, vjp = jax.vjp(_rms, xv, g)
        dx, dg = vjp(dh)
        dx = dx + skip
        return dx, dx, dg

    dx1, dx1_b, dg_ffn, *got = matmul_rows(db_b, w["w_ffn_up"], "d_h2_up_norm", norm_bwd, [x1, g_ffn, dx2],
                                           [(D_MODEL, F32), (D_MODEL, BF16)], accs=(D_MODEL,), add=dh2, comm=swap)
    ffn_up_exchange = _reduce_chip(ffn[2:], swap, got, core)
    gw["w_out"] = matmul(merged, dx1_b, "tn", F32, "d_out")
    dmerged = matmul(dx1_b, w["w_out"], "nt", F32, "d_merged")

    def merge_bwd(g0, g1, ad, za, zb, dm):
        _, vjp = jax.vjp(_merge_fn, *[t.astype(F32) for t in (g0, g1, ad, za, zb)])
        dg0, dg1, dad, dza, dzb = vjp(dm)
        return jnp.concatenate([dg0, dg1], axis=1), dad, jnp.concatenate([dza, dzb], axis=1)

    dgl_b, dattn_d_b, dz_b, parts["w_ffn_up"] = rowwise(
        merge_bwd, gate_ins + [dmerged], [(GATE_W, BF16), (D_MODEL, BF16), (GATE_W, BF16)], "merge_bwd",
        comm=ffn_up_exchange)
    gw["w_attn_out"] = matmul(attn_b, dattn_d_b, "tn", F32, "d_attn_out")
    dattn = seqs(matmul(dattn_d_b, w["w_attn_out"], "nt", F32, "d_attn"))
    gw["w_glu"] = matmul(yg2, dz_b, "tn", F32, "d_glu")
    dyg = seqs(matmul(dz_b, w["w_glu"], "nt", F32, "d_yg"))
    mixer = ["w_out", "w_attn_out", "w_glu"]
    swap = _reduce_start(mixer, gw, shard_shapes)
    du_b, dcb_re, dcb_im, dbb_re, dbb_im, dd, da_re8, da_im8, *rest = ssm_bwd(
        dyg, y, u, xr, xi, *ab, *bb, *cb, d_skip, comm=join_comms(ffn_exchange + [swap]))
    for n, p in zip(ffn[:2], rest[:2]):
        parts[n] = p
    mixer_exchange = _reduce_chip(mixer, swap, rest[2:], core)
    du_b = toks(du_b)
    g_ab_re = jnp.sum(da_re8, axis=0).reshape(32, 64)
    g_ab_im = jnp.sum(da_im8, axis=0).reshape(32, 64)
    d_lr, d_li, d_ldt, d_br_t, d_bi_t = ssm_prep_bwd(
        small["ssm_a_re"], small["ssm_a_im"], log_dt, br_t, bi_t,
        g_ab_re, g_ab_im, _block_diag_t(dbb_re), _block_diag_t(dbb_im))
    as_gcn = lambda t: t.transpose(1, 0, 2).reshape(SSM_W, 64)
    gs = {
        "ssm_a_re": d_lr, "ssm_a_im": d_li, "ssm_log_dt": d_ldt.reshape(1, 32),
        "ssm_b_re": as_gcn(d_br_t), "ssm_b_im": as_gcn(d_bi_t),
        "ssm_c_re": as_gcn(_block_diag_t(dcb_re)), "ssm_c_im": as_gcn(_block_diag_t(dcb_im)),
        "ssm_d": dd.reshape(32, 16).T,
    }
    ssm_gather = small_comm([gs[n] for n in SSM_SMALL])
    dqkv_b, *rest = attn_bwd(qkv, tables, dattn, attn, lse, comm=join_comms([mixer_exchange, ssm_gather]))
    for n, p in zip(mixer, rest):
        parts[n] = p
    ssm_shares = rest[len(mixer):]
    dqkv_b = toks(dqkv_b)
    d_qkv = matmul(dqkv_b, h, "tn", F32, "d_w_qkv")
    d_u = matmul(du_b, h, "tn", F32, "d_w_u")
    d_gate = matmul(dgl_b, h, "tn", F32, "d_w_gate")
    gw["w_in"] = jnp.concatenate([_qkv_order(d_qkv, back=True), d_u, d_gate], axis=0)
    swap = _reduce_start(["w_in"], gw, shard_shapes)
    dh, *got = matmul(dqkv_b, w_qkv, "nn", F32, "d_h_qkv", comm=swap)
    w_in_exchange = _reduce_chip(["w_in"], swap, got, core)
    grad_x, dg_mix, parts["w_in"] = mix_in_bwd([du_b, dgl_b], [w_u, w_gate], dh, x, g_mix, dx1, comm=w_in_exchange)
    gs_norm = {"norm_mix_g": dg_mix, "norm_ffn_g": dg_ffn, "norm_final_g": dg_final}
    return loss, grad_x, parts, ssm_shares, gs_norm


ANY = pl.BlockSpec(memory_space=pl.ANY)
BIG = ("w_in", "w_glu", "w_attn_out", "w_out", "w_ffn_gate", "w_ffn_up", "w_ffn_down")
TRANSPOSED = ("w_in", "w_ffn_gate", "w_ffn_up")
ROW_SHARDED = TRANSPOSED + ("w_out", "w_ffn_down")
SMALL = ("norm_mix_g", "ssm_a_re", "ssm_a_im", "ssm_log_dt", "ssm_b_re", "ssm_b_im", "ssm_c_re", "ssm_c_im",
         "ssm_d", "norm_ffn_g", "norm_final_g")
WEIGHTS = ("norm_mix_g", "w_in", "ssm_a_re", "ssm_a_im", "ssm_log_dt", "ssm_b_re", "ssm_b_im", "ssm_c_re",
           "ssm_c_im", "ssm_d", "w_glu", "w_attn_out", "w_out", "norm_ffn_g", "w_ffn_gate", "w_ffn_up",
           "w_ffn_down", "norm_final_g")
SSM_SMALL = SMALL[1:9]
NORM_SMALL = (SMALL[0],) + SMALL[9:]
NORM_ROWS = 32
N_BIG = len(BIG)


def _position():
    return lax.axis_index("x"), lax.axis_index("y"), lax.axis_index("c")


def _other_chips(x, y):
    return [(1 - x, y), (x, 1 - y), (1 - x, 1 - y)]


def _remote(src, dst, send_sem, recv_sem, device):
    return pltpu.make_async_remote_copy(src_ref=src, dst_ref=dst, send_sem=send_sem, recv_sem=recv_sem,
                                        device_id=device, device_id_type=MESH)


_later = functools.partial


def _two_level_phases(copies):
    def first(*refs):
        locals_, sends, _, _, _ = copies(*refs)
        for cp in locals_ + sends:
            cp().start()

    def mid(*refs):
        _, _, arrived, passed, _ = copies(*refs)
        for got, cp in zip(arrived, passed):
            got().wait_recv()
            cp().start()

    def last(*refs):
        locals_, sends, _, passed, from_sibling = copies(*refs)
        for cp in from_sibling:
            cp().wait_recv()
        for cp in sends + passed:
            cp().wait_send()
        for cp in locals_:
            cp().wait()

    return first, mid, last


def _half(ref, chip, which):
    rows = ref.shape[1] // 2
    return ref.at[chip, pl.ds(which * rows, rows), :]


class Comm:
    def __init__(self, ins, out_shapes, sems, first, mid, last):
        self.ins, self.out_shapes, self.sems = list(ins), list(out_shapes), list(sems)
        self.first, self.mid, self.last = first, mid, last


def join_comms(comms):
    def cut(refs_by_kind):
        offs, parts = [0, 0, 0], []
        for cm in comms:
            sizes = (len(cm.ins), len(cm.out_shapes), len(cm.sems))
            parts.append(tuple(refs_by_kind[k][offs[k]:offs[k] + sizes[k]] for k in range(3)))
            offs = [o + s for o, s in zip(offs, sizes)]
        return parts

    def phase(which):
        def run(ins, outs, sems):
            for cm, part in zip(comms, cut((ins, outs, sems))):
                fn = getattr(cm, which)
                if fn is not None:
                    fn(*part)
        return run

    return Comm(sum((cm.ins for cm in comms), []), sum((cm.out_shapes for cm in comms), []),
                sum((cm.sems for cm in comms), []), phase("first"), phase("mid"), phase("last"))


def _comm_operands(comm):
    if comm is None:
        return [], [], []
    return comm.ins, comm.out_shapes, comm.sems


def _comm_begin(comm, refs, step, n_steps):
    if comm is None:
        return
    pl.when(step == 0)(lambda: comm.first(*refs))
    if comm.mid is not None:
        pl.when(step == (n_steps * 3) // 4)(lambda: comm.mid(*refs))


def _comm_end(comm, refs, step, n_steps):
    if comm is not None:
        pl.when(step == n_steps - 1)(lambda: comm.last(*refs))


def _comm_refs(comm, refs, n_in, n_out):
    if comm is None:
        return list(refs), None
    ci, co, cs = len(comm.ins), len(comm.out_shapes), len(comm.sems)
    o0 = n_in + ci
    s0 = o0 + n_out + co
    host = list(refs[:n_in]) + list(refs[o0:o0 + n_out]) + list(refs[s0:len(refs) - cs])
    return host, (list(refs[n_in:o0]), list(refs[o0 + n_out:s0]), list(refs[len(refs) - cs:]))


def run_comm(comm, name):
    n_in, n_out = len(comm.ins), len(comm.out_shapes)

    def body(*refs):
        parts = (list(refs[:n_in]), list(refs[n_in:n_in + n_out]), list(refs[n_in + n_out:]))
        comm.first(*parts)
        if comm.mid is not None:
            comm.mid(*parts)
        comm.last(*parts)

    return pl.pallas_call(body, name=name, in_specs=[ANY] * n_in, out_specs=[ANY] * n_out,
                          out_shape=comm.out_shapes, scratch_shapes=comm.sems)(*comm.ins)


def hosted_call(work, comm, name, grid, in_specs, out_specs, out_shape, scratch_shapes, args, semantics):
    c_ins, c_outs, c_sems = _comm_operands(comm)
    n_steps = math.prod(grid)

    def body(*refs):
        host, c_refs = _comm_refs(comm, refs, len(in_specs), len(out_specs))
        step = 0
        for axis, size in enumerate(grid):
            step = step * size + pl.program_id(axis)
        _comm_begin(comm, c_refs, step, n_steps)
        work(*host)
        _comm_end(comm, c_refs, step, n_steps)

    return pl.pallas_call(
        body, name=name, grid=grid, in_specs=list(in_specs) + [ANY] * len(c_ins),
        out_specs=list(out_specs) + [ANY] * len(c_outs), out_shape=list(out_shape) + c_outs,
        scratch_shapes=list(scratch_shapes) + c_sems,
        compiler_params=_params(semantics if comm is None else ("arbitrary",) * len(grid)),
    )(*args, *c_ins)


def gather_comm(shards):
    n = len(shards)

    def copies(srcs, outs, sems):
        send_sems, recv_sems, local_sems = sems
        x, y, c = _position()
        me = 2 * x + y
        sibling = (x, y, 1 - c)
        chips = _other_chips(x, y)
        locals_ = [_later(pltpu.make_async_copy, s, o.at[me], local_sems.at[i])
                   for i, (s, o) in enumerate(zip(srcs, outs))]
        sends, arrived, passed, from_sibling = [], [], [], []
        for j, (px, py) in enumerate(chips):
            for i, (s, o) in enumerate(zip(srcs, outs)):
                rows = s.shape[0] // 2
                sends.append(_later(_remote, s.at[pl.ds(c * rows, rows), :], _half(o, me, c), send_sems.at[i, j],
                                    recv_sems.at[i, j], (px, py, c)))
                got = _half(o, 2 * px + py, c)
                arrived.append(_later(_remote, got, got, send_sems.at[i, j], recv_sems.at[i, j], (px, py, c)))
                passed.append(_later(_remote, got, got, send_sems.at[i, 3 + j], recv_sems.at[i, 3 + j], sibling))
                other = _half(o, 2 * px + py, 1 - c)
                from_sibling.append(_later(_remote, other, other, send_sems.at[i, 3 + j], recv_sems.at[i, 3 + j],
                                           sibling))
        return locals_, sends, arrived, passed, from_sibling

    return Comm(shards, [jax.ShapeDtypeStruct((N_CHIPS,) + s.shape, s.dtype) for s in shards],
                [pltpu.SemaphoreType.DMA((n, 6)), pltpu.SemaphoreType.DMA((n, 6)), pltpu.SemaphoreType.DMA((n,))],
                *_two_level_phases(copies))


HBM = pl.BlockSpec(memory_space=pltpu.HBM)
SEM = pl.BlockSpec(memory_space=pltpu.SEMAPHORE)
N_OTHER = N_CHIPS - 1


def _ici_halves(src_ref, land_ref, sems):
    x, y, c = _position()
    me = 2 * x + y
    rows = src_ref.shape[0] // 2
    sends, arrivals = [], []
    for j, (px, py) in enumerate(_other_chips(x, y)):
        sends.append(_later(_remote, src_ref.at[pl.ds(c * rows, rows), :], _half(land_ref, me, c), sems[j],
                            sems[N_OTHER + j], (px, py, c)))
        got = _half(land_ref, 2 * px + py, c)
        arrivals.append(_later(_remote, got, got, sems[j], sems[N_OTHER + j], (px, py, c)))
    return sends, arrivals


def gather_start(shard):
    def body(src_ref, land_ref, *rest):
        sems, token = rest[:2 * N_OTHER], rest[-1]
        for cp in _ici_halves(src_ref, land_ref, sems)[0]:
            cp().start()
        token[...] = jnp.zeros_like(token)

    sem = pltpu.SemaphoreType.DMA(())
    land = pltpu.HBM((N_CHIPS,) + shard.shape, shard.dtype)
    res = pl.pallas_call(
        body, name="w_in_gather_start", in_specs=(HBM, HBM),
        out_specs=(SEM,) * (2 * N_OTHER) + (HBM, HBM, pl.BlockSpec(memory_space=pltpu.VMEM)),
        out_shape=(sem,) * (2 * N_OTHER) + (pltpu.HBM(shard.shape, shard.dtype), land,
                                           jax.ShapeDtypeStruct((8, LANES), F32)),
        input_output_aliases={0: 2 * N_OTHER, 1: 2 * N_OTHER + 1},
        compiler_params=pltpu.CompilerParams(has_side_effects=pltpu.SideEffectType.DATAFLOW_SIDE_EFFECTING),
    )(pltpu.with_memory_space_constraint(shard, pltpu.HBM),
      pltpu.with_memory_space_constraint(lax.empty((N_CHIPS,) + shard.shape, shard.dtype), pltpu.HBM))
    return res[:2 * N_OTHER], res[2 * N_OTHER], res[2 * N_OTHER + 1], res[-1]


def gather_wait(sems, shard, land, after):
    def body(src_ref, land_ref, *rest):
        sem_refs = rest[:2 * N_OTHER]
        sends, arrivals = _ici_halves(src_ref, land_ref, sem_refs)
        for cp in sends:
            cp().wait_send()
        for cp in arrivals:
            cp().wait_recv()

    return pl.pallas_call(
        body, name="w_in_gather_wait", in_specs=(HBM, HBM) + (SEM,) * (2 * N_OTHER) + (ANY,) * len(after),
        out_specs=(HBM, HBM), out_shape=(pltpu.HBM(shard.shape, shard.dtype), pltpu.HBM(land.shape, land.dtype)),
        input_output_aliases={0: 0, 1: 1},
        compiler_params=pltpu.CompilerParams(has_side_effects=pltpu.SideEffectType.DATAFLOW_SIDE_EFFECTING),
    )(shard, land, *sems, *after)


def handover(land):
    def body(land_ref, out_ref, send_sems, recv_sems):
        x, y, c = _position()
        sibling = (x, y, 1 - c)
        sends = []
        for j, (px, py) in enumerate(_other_chips(x, y)):
            got = _half(land_ref, 2 * px + py, c)
            sends.append(_remote(got, got, send_sems.at[j], recv_sems.at[j], sibling))
            sends[-1].start()
        for j, (px, py) in enumerate(_other_chips(x, y)):
            other = _half(land_ref, 2 * px + py, 1 - c)
            _remote(other, other, send_sems.at[j], recv_sems.at[j], sibling).wait_recv()
        for cp in sends:
            cp.wait_send()

    return pl.pallas_call(
        body, name="w_in_handover", in_specs=[ANY], out_specs=ANY, out_shape=jax.ShapeDtypeStruct(land.shape, land.dtype),
        input_output_aliases={0: 0},
        scratch_shapes=[pltpu.SemaphoreType.DMA((N_OTHER,)), pltpu.SemaphoreType.DMA((N_OTHER,))],
    )(land)


def swap_comm(grads):
    n = len(grads)

    def copies(srcs, gots, sems):
        send_sems, recv_sems = sems
        x, y, c = _position()
        out = []
        for i, (s, o) in enumerate(zip(srcs, gots)):
            rows = s.shape[1] // 2
            out.append(_remote(s.at[:, pl.ds((1 - c) * rows, rows), :], o, send_sems.at[i], recv_sems.at[i],
                               (x, y, 1 - c)))
        return out

    def first(srcs, gots, sems):
        for cp in copies(srcs, gots, sems):
            cp.start()

    def last(srcs, gots, sems):
        for cp in copies(srcs, gots, sems):
            cp.wait()

    return Comm(grads, [jax.ShapeDtypeStruct((N_CHIPS, g.shape[1] // 2, g.shape[2]), g.dtype) for g in grads],
                [pltpu.SemaphoreType.DMA((n,)), pltpu.SemaphoreType.DMA((n,))], first, None, last)


def add_halves(name, g, got, core):
    _, half, cols = got.shape
    mine = pl.BlockSpec((None, half, cols), lambda k, c_ref: (k, c_ref[0], 0))
    other = pl.BlockSpec((None, half, cols), lambda k, c_ref: (k, 0, 0))

    def body(c_ref, g_ref, got_ref, o_ref):
        o_ref[...] = (g_ref[...] + got_ref[...]).astype(BF16)

    return pl.pallas_call(
        body, name="add_halves_" + name,
        grid_spec=pltpu.PrefetchScalarGridSpec(num_scalar_prefetch=1, grid=(N_CHIPS,), in_specs=[mine, other],
                                               out_specs=other),
        out_shape=jax.ShapeDtypeStruct(got.shape, BF16),
        compiler_params=_params(("parallel",)),
    )(core, g, got)


def exchange_comm(parts):
    n = len(parts)

    def copies(srcs, outs, sems):
        send_sems, recv_sems, local_sems = sems
        x, y, c = _position()
        me = 2 * x + y
        sibling = (x, y, 1 - c)
        chips = _other_chips(x, y)
        locals_, sends, arrived, passed, from_sibling = [], [], [], [], []
        for i, (s, o) in enumerate(zip(srcs, outs)):
            locals_.append(_later(pltpu.make_async_copy, s.at[me], _half(o, me, c), local_sems.at[i]))
            sends.append(_later(_remote, s.at[me], _half(o, me, c), send_sems.at[i, 3], recv_sems.at[i, 3], sibling))
            other = _half(o, me, 1 - c)
            from_sibling.append(_later(_remote, other, other, send_sems.at[i, 3], recv_sems.at[i, 3], sibling))
        for j, (px, py) in enumerate(chips):
            for i, (s, o) in enumerate(zip(srcs, outs)):
                sends.append(_later(_remote, s.at[2 * px + py], _half(o, me, c), send_sems.at[i, j],
                                    recv_sems.at[i, j], (px, py, c)))
                got = _half(o, 2 * px + py, c)
                arrived.append(_later(_remote, got, got, send_sems.at[i, j], recv_sems.at[i, j], (px, py, c)))
                passed.append(_later(_remote, got, got, send_sems.at[i, 4 + j], recv_sems.at[i, 4 + j], sibling))
                other = _half(o, 2 * px + py, 1 - c)
                from_sibling.append(_later(_remote, other, other, send_sems.at[i, 4 + j], recv_sems.at[i, 4 + j],
                                           sibling))
        return locals_, sends, arrived, passed, from_sibling

    return Comm(parts, [jax.ShapeDtypeStruct((N_CHIPS, 2 * p.shape[1], p.shape[2]), p.dtype) for p in parts],
                [pltpu.SemaphoreType.DMA((n, 7)), pltpu.SemaphoreType.DMA((n, 7)), pltpu.SemaphoreType.DMA((n,))],
                *_two_level_phases(copies))


def small_comm(shares):
    n = len(shares)

    def copies(srcs, outs, sems):
        send_sems, recv_sems, local_sems = sems
        x, y, c = _position()
        me = 4 * x + 2 * y + c
        flips = [(fx, fy, fc) for fx in (0, 1) for fy in (0, 1) for fc in (0, 1)][1:]
        peers = [(1 - x if fx else x, 1 - y if fy else y, 1 - c if fc else c) for fx, fy, fc in flips]
        locals_, sends, arrived = [], [], []
        for i, (src_ref, out_ref) in enumerate(zip(srcs, outs)):
            locals_.append(_later(pltpu.make_async_copy, src_ref, out_ref.at[me], local_sems.at[i]))
            for j, (px, py, pc) in enumerate(peers):
                sends.append(_later(_remote, src_ref, out_ref.at[me], send_sems.at[i, j], recv_sems.at[i, j],
                                    (px, py, pc)))
                got = out_ref.at[4 * px + 2 * py + pc]
                arrived.append(_later(_remote, got, got, send_sems.at[i, j], recv_sems.at[i, j], (px, py, pc)))
        return locals_, sends, arrived

    def first(*refs):
        locals_, sends, _ = copies(*refs)
        for cp in locals_ + sends:
            cp().start()

    def last(*refs):
        locals_, sends, arrived = copies(*refs)
        for cp in arrived:
            cp().wait_recv()
        for cp in sends:
            cp().wait_send()
        for cp in locals_:
            cp().wait()

    return Comm(shares, [jax.ShapeDtypeStruct((N_DEV,) + s.shape, s.dtype) for s in shares],
                [pltpu.SemaphoreType.DMA((n, 7)), pltpu.SemaphoreType.DMA((n, 7)), pltpu.SemaphoreType.DMA((n,))],
                first, None, last)


def _adam_fn(w, g, m, v):
    m = ADAM_B1 * m + (1.0 - ADAM_B1) * g
    v = ADAM_B2 * v + (1.0 - ADAM_B2) * jnp.square(g)
    m_hat = m / (1.0 - ADAM_B1 ** ADAM_STEP)
    v_hat = v / (1.0 - ADAM_B2 ** ADAM_STEP)
    return -ADAM_LR * (m_hat / (jnp.sqrt(v_hat) + ADAM_EPS) + ADAM_WD * w), m, v


def adam_big(name, parts, w, m, v):
    rows, cols = w.shape
    tm = _pick(rows, 384, 16)

    def fn(p0, p1, p2, p3, wv, mv, vv):
        g = ((p0.astype(F32) + p1.astype(F32)) + p2.astype(F32)) + p3.astype(F32)
        return (g,) + _adam_fn(wv, g, mv, vv)

    return rowwise(fn, [parts, w, m, v], [(cols, F32)] * 4, "adam_" + name, tm=tm, rows=rows)


def adam_small(name, gathered, w, m, v):
    def body(g_ref, w_ref, m_ref, v_ref, go_ref, d_ref, mo_ref, vo_ref):
        g = g_ref[0]
        for k in range(1, N_DEV):
            g = g + g_ref[k]
        go_ref[...] = g
        d_ref[...], mo_ref[...], vo_ref[...] = _adam_fn(w_ref[...], g, m_ref[...], v_ref[...])

    return pl.pallas_call(body, name=name, out_shape=[jax.ShapeDtypeStruct(w.shape, F32)] * 4,
                          compiler_params=_params())(gathered, w, m, v)


def _ssm_2d(name, t):
    t = t[0] if t.ndim > 2 else t
    if name in ("ssm_b_re", "ssm_b_im"):
        return t.transpose(0, 2, 1).reshape(SSM_W, 64)
    if name in ("ssm_c_re", "ssm_c_im"):
        return t.reshape(SSM_W, 64)
    return t.T if name == "ssm_d" else t


def _ssm_back(name, t):
    if name in ("ssm_b_re", "ssm_b_im"):
        return t.reshape(32, 16, 64).transpose(0, 2, 1)[None]
    if name in ("ssm_c_re", "ssm_c_im"):
        return t.reshape(1, 32, 16, 64)
    if name == "ssm_d":
        return t.T[None]
    return t if name == "ssm_log_dt" else t[None]


def adam_ssm(shares, w, m, v):
    n = len(w)

    def body(*refs):
        ins, outs = refs[:4 * n], refs[4 * n:]
        for i in range(n):
            g_ref, w_ref, m_ref, v_ref = (ins[k * n + i] for k in range(4))
            g = g_ref[0]
            for k in range(1, N_DEV):
                g = g + g_ref[k]
            outs[4 * i][...] = g
            outs[4 * i + 1][...], outs[4 * i + 2][...], outs[4 * i + 3][...] = _adam_fn(w_ref[...], g, m_ref[...],
                                                                                      v_ref[...])

    out_shape = [jax.ShapeDtypeStruct(t.shape, F32) for t in w for _ in range(4)]
    res = pl.pallas_call(body, name="adam_ssm", out_shape=out_shape, compiler_params=_params())(*shares, *w, *m, *v)
    return [res[4 * i:4 * i + 4] for i in range(n)]


def _pack_small(names, vals, rows, last=None):
    flat = [vals[n].reshape(-1) for n in names]
    if last is not None:
        flat.append(last.reshape(-1))
    flat = jnp.concatenate(flat)
    return jnp.pad(flat, (0, rows * LANES - flat.shape[0])).reshape(rows, LANES)


def _unpack_small(names, pack, shapes):
    flat, out, off = pack.reshape(-1), {}, 0
    for n in names:
        size = math.prod(shapes[n])
        out[n] = flat[off:off + size].reshape(shapes[n])
        off += size
    return out, flat[off]


def _to_slots(name, g, shard_shape):
    rows, cols = shard_shape
    if name in ROW_SHARDED:
        return g.reshape(N_CHIPS, rows, cols)
    return g.reshape(rows, N_CHIPS, cols).transpose(1, 0, 2)


def _from_slots(name, s):
    _, rows, cols = s.shape
    if name in ROW_SHARDED:
        return s.reshape(N_CHIPS * rows, cols)
    return s.transpose(1, 0, 2).reshape(rows, N_CHIPS * cols)


def kernel(x, norm_mix_g, w_in, ssm_a_re, ssm_a_im, ssm_log_dt, ssm_b_re, ssm_b_im, ssm_c_re, ssm_c_im, ssm_d, w_glu, w_attn_out, w_out, norm_ffn_g, w_ffn_gate, w_ffn_up, w_ffn_down, norm_final_g, loss_target, m_norm_mix_g, m_w_in, m_ssm_a_re, m_ssm_a_im, m_ssm_log_dt, m_ssm_b_re, m_ssm_b_im, m_ssm_c_re, m_ssm_c_im, m_ssm_d, m_w_glu, m_w_attn_out, m_w_out, m_norm_ffn_g, m_w_ffn_gate, m_w_ffn_up, m_w_ffn_down, m_norm_final_g, v_norm_mix_g, v_w_in, v_ssm_a_re, v_ssm_a_im, v_ssm_log_dt, v_ssm_b_re, v_ssm_b_im, v_ssm_c_re, v_ssm_c_im, v_ssm_d, v_w_glu, v_w_attn_out, v_w_out, v_norm_ffn_g, v_w_ffn_gate, v_w_ffn_up, v_w_ffn_down, v_norm_final_g):
    given = dict(locals())
    def local(name, prefix=""):
        t = given[prefix + name][0]
        return t.T if name in TRANSPOSED else t

    shard = {n: local(n) for n in BIG}
    shapes = {n: given[n].shape for n in WEIGHTS}

    small = {n: given[n] for n in SMALL}
    small_2d = dict(small)
    for n in ("ssm_a_re", "ssm_a_im", "ssm_b_re", "ssm_b_im", "ssm_c_re", "ssm_c_im", "ssm_d"):
        small_2d[n] = small[n][0]
    small_2d["norm_final_g"] = norm_final_g.reshape(1, D_MODEL)

    core = lax.axis_index("c").astype(jnp.int32).reshape(1)
    loss, grad_x, parts, ssm_shares, gs_norm = local_step(
        x.reshape(TOKENS, D_MODEL), loss_target.reshape(TOKENS, D_MODEL),
        {n: shard[n] for n in BIG}, small_2d, core)

    (norm_shares,) = run_comm(small_comm([_pack_small(NORM_SMALL, gs_norm, NORM_ROWS, last=loss)]),
                              "gather_norm_grads")
    small_out = [{} for _ in range(4)]
    packs = [_pack_small(NORM_SMALL, {n: given[p + n] for n in NORM_SMALL}, NORM_ROWS) for p in ("", "m_", "v_")]
    for kind, t in enumerate(adam_small("adam_norm_gains", norm_shares, *packs)):
        vals, after = _unpack_small(NORM_SMALL, t, shapes)
        small_out[kind].update(vals)
        if kind == 0:
            total_loss = after
    ssm_in = [[_ssm_2d(n, given[p + n]) for n in SSM_SMALL] for p in ("", "m_", "v_")]
    for n, res in zip(SSM_SMALL, adam_ssm(ssm_shares, *ssm_in)):
        for kind, t in enumerate(res):
            small_out[kind][n] = _ssm_back(n, t)

    big_out = {}
    for n in BIG:
        res = adam_big(n, parts[n], shard[n], local(n, "m_"), local(n, "v_"))
        big_out[n] = [(t.T if n in TRANSPOSED else t)[None] for t in res]

    outs = [total_loss, grad_x.reshape(LOCAL_BATCH, SEQ, D_MODEL)]
    for kind in range(4):
        for n in WEIGHTS:
            outs.append(big_out[n][kind] if n in BIG else small_out[kind][n])
    return tuple(outs)
```

```python
import functools
import math

import jax
import jax.numpy as jnp
import numpy as np
from jax import lax
from jax.experimental import pallas as pl
from jax.experimental.pallas import tpu as pltpu

F32 = jnp.float32
BF16 = jnp.bfloat16
MESH = pl.DeviceIdType.MESH

D_MODEL = 1024
SEQ = 2048
LOCAL_BATCH = 2
TOKENS = LOCAL_BATCH * SEQ
HEAD_DIM = 64
HEADS_PER_GROUP = 4
GROUP_W = HEADS_PER_GROUP * HEAD_DIM
N_GROUPS = 3
DILATIONS = (1, 4, 16)
ATTN_BLOCK = 128
ROPE_DIM = 16
ROPE_THETA = 500000.0
QKV_W = 3 * N_GROUPS * GROUP_W
SSM_W = 512
SSM_STATE_W = 2048
SSM_LANE_BLOCKS = 4
GATE_W = 2 * D_MODEL
D_FF = 2816
RMS_EPS = 1e-6
NEG_INF = -1e30
ADAM_LR, ADAM_B1, ADAM_B2, ADAM_EPS, ADAM_WD, ADAM_STEP = 0.001, 0.9, 0.999, 1e-08, 0.01, 10
N_CHIPS = 4
N_DEV = 8

VMEM_LIMIT = 56 * 1024 * 1024
LANES = 128


def _params(sem=None):
    return pltpu.CompilerParams(dimension_semantics=sem, vmem_limit_bytes=VMEM_LIMIT)


def _pick(n, cap, align=LANES):
    best = None
    for d in range(align, min(n, cap) + 1, align):
        if n % d == 0:
            best = d
    return n if best is None or n <= cap else best


_DIMS = {"nn": (((1,), (0,)), ((), ())), "nt": (((1,), (1,)), ((), ())), "tn": (((0,), (0,)), ((), ()))}


def _dot(a, b, mode):
    return lax.dot_general(a, b, _DIMS[mode], preferred_element_type=F32)


def matmul(a, b, mode, out_dtype, name, add=None, comm=None):
    if mode == "nn":
        (m, k), n = a.shape, b.shape[1]
    elif mode == "nt":
        (m, k), n = a.shape, b.shape[0]
    else:
        (k, m), n = a.shape, b.shape[1]
    tn = _pick(n, 1408 if mode != "tn" else 512)
    tk = _pick(k, 2816) if mode != "tn" else k
    tm = _pick(m, 1408)
    out_bytes = jnp.dtype(out_dtype).itemsize

    def need(tm_):
        return 2 * 2 * (tm_ * tk + tk * tn) + tm_ * tn * (4 + 2 * out_bytes + (8 if add is not None else 0))

    while need(tm) > 40 * 1024 * 1024 and tm % 256 == 0:
        tm //= 2
    nk = k // tk
    a_spec = {"nn": pl.BlockSpec((tm, tk), lambda i, j, kk: (i, kk)),
              "nt": pl.BlockSpec((tm, tk), lambda i, j, kk: (i, kk)),
              "tn": pl.BlockSpec((tk, tm), lambda i, j, kk: (kk, i))}[mode]
    b_spec = {"nn": pl.BlockSpec((tk, tn), lambda i, j, kk: (kk, j)),
              "nt": pl.BlockSpec((tn, tk), lambda i, j, kk: (j, kk)),
              "tn": pl.BlockSpec((tk, tn), lambda i, j, kk: (kk, j))}[mode]
    o_spec = pl.BlockSpec((tm, tn), lambda i, j, kk: (i, j))

    def body(a_ref, b_ref, *rest):
        if add is not None:
            add_ref, o_ref, acc_ref = rest
        else:
            o_ref, acc_ref = rest
        part = _dot(a_ref[...], b_ref[...], mode)
        if nk == 1:
            res = part if add is None else part + add_ref[...]
            o_ref[...] = res.astype(out_dtype)
            return
        kk = pl.program_id(2)

        @pl.when(kk == 0)
        def _():
            acc_ref[...] = part

        @pl.when(kk > 0)
        def _():
            acc_ref[...] += part

        @pl.when(kk == nk - 1)
        def _():
            res = acc_ref[...] if add is None else acc_ref[...] + add_ref[...]
            o_ref[...] = res.astype(out_dtype)

    in_specs = [a_spec, b_spec] + ([o_spec] if add is not None else [])
    args = (a, b) + ((add,) if add is not None else ())
    res = hosted_call(
        body, comm, name, (m // tm, n // tn, nk), in_specs, [o_spec], [jax.ShapeDtypeStruct((m, n), out_dtype)],
        [pltpu.VMEM((tm, tn) if nk > 1 else (8, LANES), F32)], args, ("parallel", "parallel", "arbitrary"))
    return res[0] if comm is None else res


def matmul_rows(a, b, name, fn, extra, outs, accs=(), add=None, comm=None, tm=512):
    (m, k), n = a.shape, b.shape[1]
    n_fixed = 2 + (add is not None)
    row_spec = lambda cols: pl.BlockSpec((tm, cols), lambda i: (i, 0))
    in_specs = [row_spec(k), pl.BlockSpec((k, n), lambda i: (0, 0))] + ([row_spec(n)] if add is not None else [])
    in_specs += [pl.BlockSpec(e.shape, lambda i: (0, 0)) if e.shape[0] == 1 else row_spec(e.shape[1]) for e in extra]
    out_specs = [row_spec(c) for c, _ in outs] + [pl.BlockSpec((1, c), lambda i: (0, 0)) for c in accs]
    out_shape = [jax.ShapeDtypeStruct((m, c), dt) for c, dt in outs] + [jax.ShapeDtypeStruct((1, c), F32) for c in accs]

    def body(*refs):
        rows = _dot(refs[0][...], refs[1][...], "nn")
        if add is not None:
            rows = rows + refs[2][...]
        n_in = n_fixed + len(extra)
        res = fn(rows, *[r[...] for r in refs[n_fixed:n_in]])
        for r, v in zip(refs[n_in:n_in + len(outs)], res[:len(outs)]):
            r[...] = v.astype(r.dtype)
        first = pl.program_id(0) == 0
        for r, v in zip(refs[n_in + len(outs):], res[len(outs):]):
            @pl.when(first)
            def _(r=r, v=v):
                r[...] = v

            @pl.when(jnp.logical_not(first))
            def _(r=r, v=v):
                r[...] += v

    args = (a, b) + ((add,) if add is not None else ()) + tuple(extra)
    return hosted_call(body, comm, name, (m // tm,), in_specs, out_specs, out_shape, [], args, ("arbitrary",))


FFN_TM, FFN_TN = 512, 1408


def ffn_in(h2, wg_t, wu_t, comm=None):
    def body(h_ref, wg_ref, wu_ref, a_ref, b_ref, act_ref):
        hv = h_ref[...]
        a, b = _dot(hv, wg_ref[...], "nt"), _dot(hv, wu_ref[...], "nt")
        a_ref[...] = a.astype(BF16)
        b_ref[...] = b.astype(BF16)
        act_ref[...] = _swiglu_fn(a, b).astype(BF16)

    rows = pl.BlockSpec((FFN_TM, D_MODEL), lambda i, j: (i, 0))
    wts = pl.BlockSpec((FFN_TN, D_MODEL), lambda i, j: (j, 0))
    out = pl.BlockSpec((FFN_TM, FFN_TN), lambda i, j: (i, j))
    return hosted_call(body, comm, "ffn_in", (TOKENS // FFN_TM, D_FF // FFN_TN), [rows, wts, wts], [out] * 3,
                       [jax.ShapeDtypeStruct((TOKENS, D_FF), BF16)] * 3, [], (h2, wg_t, wu_t),
                       ("parallel", "parallel"))


def ffn_in_bwd(dx2_b, wd, a, b):
    def body(dx_ref, wd_ref, a_ref, b_ref, da_ref, db_ref):
        dx = dx_ref[...]
        for lo in range(0, FFN_TN, 512):
            cols = slice(lo, min(lo + 512, FFN_TN))
            dact = _dot(dx, wd_ref[cols, :], "nt")
            _, vjp = jax.vjp(_swiglu_fn, a_ref[:, cols].astype(F32), b_ref[:, cols].astype(F32))
            da, db = vjp(dact)
            da_ref[:, cols] = da.astype(BF16)
            db_ref[:, cols] = db.astype(BF16)

    rows = pl.BlockSpec((FFN_TM, D_MODEL), lambda i, j: (i, 0))
    wts = pl.BlockSpec((FFN_TN, D_MODEL), lambda i, j: (j, 0))
    out = pl.BlockSpec((FFN_TM, FFN_TN), lambda i, j: (i, j))
    return pl.pallas_call(
        body, name="ffn_in_bwd", grid=(TOKENS // FFN_TM, D_FF // FFN_TN), in_specs=[rows, wts, out, out],
        out_specs=[out] * 2, out_shape=[jax.ShapeDtypeStruct((TOKENS, D_FF), BF16)] * 2,
        compiler_params=_params(("parallel", "parallel")),
    )(dx2_b, wd, a, b)


def mix_in_bwd(grads, weights, partial, x, g, skip, comm=None):
    n = len(grads)
    tm = 512

    def body(*refs):
        a_refs, b_refs = refs[:n], refs[n:2 * n]
        part_ref, x_ref, g_ref, skip_ref, gx_ref, dg_ref = refs[2 * n:]
        dh = part_ref[...]
        for a_ref, b_ref in zip(a_refs, b_refs):
            dh = dh + _dot(a_ref[...], b_ref[...], "nn")
        _, vjp = jax.vjp(_rms, x_ref[...], g_ref[...])
        dx, dg = vjp(dh)
        gx_ref[...] = dx + skip_ref[...]
        first = pl.program_id(0) == 0

        @pl.when(first)
        def _():
            dg_ref[...] = dg

        @pl.when(jnp.logical_not(first))
        def _():
            dg_ref[...] += dg

    rows = pl.BlockSpec((tm, D_MODEL), lambda i: (i, 0))
    gain = pl.BlockSpec((1, D_MODEL), lambda i: (0, 0))
    in_specs = [pl.BlockSpec((tm, a.shape[1]), lambda i: (i, 0)) for a in grads]
    in_specs += [pl.BlockSpec(b.shape, lambda i: (0, 0)) for b in weights]
    return hosted_call(
        body, comm, "mix_in_bwd", (TOKENS // tm,), in_specs + [rows, rows, gain, rows], [rows, gain],
        [jax.ShapeDtypeStruct((TOKENS, D_MODEL), F32), jax.ShapeDtypeStruct((1, D_MODEL), F32)], [],
        (*grads, *weights, partial, x, g, skip), ("arbitrary",))


def rowwise(fn, ins, outs, name, accs=(), tm=256, rows=TOKENS, comm=None):
    in_specs, args = [], []
    for item in ins:
        arr, width, blk = item if isinstance(item, tuple) else (item, None, 0)
        if arr.ndim == 3:
            for k in range(arr.shape[0]):
                in_specs.append(pl.BlockSpec((None, tm, arr.shape[2]), functools.partial(lambda i, k_: (k_, i, 0), k_=k)))
                args.append(arr)
            continue
        if arr.shape[0] == 1:
            in_specs.append(pl.BlockSpec(arr.shape, lambda i: (0, 0)))
        elif width is None:
            in_specs.append(pl.BlockSpec((tm, arr.shape[1]), lambda i: (i, 0)))
        else:
            in_specs.append(pl.BlockSpec((tm, width), functools.partial(lambda i, blk_: (i, blk_), blk_=blk)))
        args.append(arr)
    out_specs = [pl.BlockSpec((tm, c), lambda i: (i, 0)) for c, _ in outs]
    out_specs += [pl.BlockSpec((1, c), lambda i: (0, 0)) for c in accs]
    out_shape = [jax.ShapeDtypeStruct((rows, c), dt) for c, dt in outs]
    out_shape += [jax.ShapeDtypeStruct((1, c), F32) for c in accs]
    n_in, n_out = len(args), len(outs)
    c_ins, c_outs, c_sems = _comm_operands(comm)

    def body(*refs):
        refs, c_refs = _comm_refs(comm, refs, n_in, n_out + len(accs))
        step = pl.program_id(0)
        _comm_begin(comm, c_refs, step, rows // tm)
        res = fn(*[r[...] for r in refs[:n_in]])
        for r, v in zip(refs[n_in:n_in + n_out], res[:n_out]):
            r[...] = v.astype(r.dtype)
        first = step == 0
        for r, v in zip(refs[n_in + n_out:], res[n_out:]):
            @pl.when(first)
            def _(r=r, v=v):
                r[...] = v

            @pl.when(jnp.logical_not(first))
            def _(r=r, v=v):
                r[...] += v
        _comm_end(comm, c_refs, step, rows // tm)

    return pl.pallas_call(
        body, name=name, grid=(rows // tm,), in_specs=in_specs + [ANY] * len(c_ins),
        out_specs=out_specs + [ANY] * len(c_outs), out_shape=out_shape + c_outs, scratch_shapes=c_sems,
        compiler_params=_params(("arbitrary",)),
    )(*args, *c_ins)


def first_norm(x, g, others, comm=None):
    tm, n = 256, len(others)

    def body(x_ref, g_ref, *rest):
        srcs, h_ref, dsts = rest[:n], rest[n], rest[n + 1:]
        h_ref[...] = _rms(x_ref[...], g_ref[...]).astype(BF16)
        for k, (s, d) in enumerate(zip(srcs, dsts)):
            @pl.when(pl.program_id(0) == k)
            def _(s=s, d=d):
                d[...] = s[...].astype(BF16)

    rows = pl.BlockSpec((tm, D_MODEL), lambda i: (i, 0))
    whole = [pl.BlockSpec(a.shape, lambda i: (0, 0)) for a in others]
    return hosted_call(
        body, comm, "norm_mix", (TOKENS // tm,), [rows, pl.BlockSpec((1, D_MODEL), lambda i: (0, 0))] + whole,
        [rows] + whole, [jax.ShapeDtypeStruct((TOKENS, D_MODEL), BF16)]
        + [jax.ShapeDtypeStruct(a.shape, BF16) for a in others], [], (x, g, *others), ("arbitrary",))


def _rms(x, g):
    return x * lax.rsqrt(jnp.mean(x * x, axis=-1, keepdims=True) + RMS_EPS) * g


def _colsum(v):
    return jnp.sum(v, axis=0, keepdims=True)


PAIR_W = 2 * HEAD_DIM
N_PAIRS = HEADS_PER_GROUP // 2


def _qkv_order(w_t, back=False):
    dims = (N_PAIRS, N_GROUPS, 3) if back else (3, N_GROUPS, N_PAIRS)
    return w_t.reshape(dims + (PAIR_W, w_t.shape[1])).transpose(2, 1, 0, 3, 4).reshape(QKV_W, w_t.shape[1])


def _rope_tables():
    half = ROPE_DIM // 2
    inv = np.power(np.float32(ROPE_THETA), -np.arange(half, dtype=np.float32) * np.float32(2.0 / ROPE_DIM))
    ang = (np.arange(SEQ, dtype=np.float32)[:, None] * inv[None, :]).astype(np.float32)
    cos, sin = np.cos(ang), np.sin(ang)
    zeros = np.zeros((SEQ, HEAD_DIM - ROPE_DIM), np.float32)
    zh = np.zeros((SEQ, half), np.float32)
    c = np.concatenate([cos, cos, zeros + 1.0], axis=1)
    sa = np.concatenate([-sin, zh, zeros], axis=1)
    sb = np.concatenate([zh, sin, zeros], axis=1)
    return [jnp.asarray(np.tile(t, (1, 2)), F32) for t in (c, sa, sb)]


def _rope_fwd(x, c, sa, sb):
    return x * c + pltpu.roll(x, PAIR_W - 8, 1) * sa + pltpu.roll(x, 8, 1) * sb


def _rope_bwd(dy, c, sa, sb):
    return dy * c + pltpu.roll(dy * sb, PAIR_W - 8, 1) + pltpu.roll(dy * sa, 8, 1)


def _band_masks():
    row = lax.broadcasted_iota(jnp.int32, (ATTN_BLOCK, ATTN_BLOCK), 0)
    col = lax.broadcasted_iota(jnp.int32, (ATTN_BLOCK, ATTN_BLOCK), 1)
    return col <= row, col >= row


def _stack_rows(t):
    return jnp.concatenate([t, t], axis=0)


def _stack_heads(t, first_head):
    return jnp.concatenate([jnp.where(first_head, t, 0), jnp.where(first_head, 0, t)], axis=0)


def _per_head(fn):
    return jnp.concatenate([fn(slice(h * HEAD_DIM, (h + 1) * HEAD_DIM)) for h in range(2)], axis=1)


def _slab_spec(kind):
    return pl.BlockSpec((None, SEQ, PAIR_W), lambda b, p, g: (b, 0, p * 3 * N_GROUPS + g * 3 + kind))


_TABLE_SPEC = pl.BlockSpec((SEQ, PAIR_W), lambda b, p, g: (0, 0))
_PAIR_SPEC = pl.BlockSpec((None, SEQ, PAIR_W), lambda b, p, g: (b, 0, p))


def _block_rows(dil, r, n):
    return pl.ds(n * (ATTN_BLOCK * dil) + r, ATTN_BLOCK, stride=dil)


def proj_qkv(h, w_qkv_t, tables, comm=None):
    tm = 1024
    pair_w = QKV_W // N_PAIRS
    scale = HEAD_DIM ** -0.5

    def body(h_ref, w_ref, c_ref, sa_ref, sb_ref, o_ref):
        rows = _dot(h_ref[...], w_ref[...], "nt")
        c, sa, sb = c_ref[...], sa_ref[...], sb_ref[...]
        for blk in range(pair_w // PAIR_W):
            cols = slice(blk * PAIR_W, (blk + 1) * PAIR_W)
            x = rows[:, cols]
            if blk % 3 == 0:
                x = _rope_fwd(x, c, sa, sb) * scale
            elif blk % 3 == 1:
                x = _rope_fwd(x, c, sa, sb)
            o_ref[:, cols] = x

    table = pl.BlockSpec((tm, PAIR_W), lambda i, j, : (i % (SEQ // tm), 0))
    res = hosted_call(
        body, comm, "proj_qkv", (TOKENS // tm, N_PAIRS),
        [pl.BlockSpec((tm, D_MODEL), lambda i, j: (i, 0)), pl.BlockSpec((pair_w, D_MODEL), lambda i, j: (j, 0)),
         table, table, table],
        [pl.BlockSpec((tm, pair_w), lambda i, j: (i, j))], [jax.ShapeDtypeStruct((TOKENS, QKV_W), F32)], [],
        (h, w_qkv_t, *tables), ("parallel", "parallel"))
    return res[0] if comm is None else res


def attn_fwd(qkv, comm=None):
    def body(qs, ks, v_ref, attn_b_ref, attn_ref, lse_ref, o0, o1, o2, l0, l1, l2):
        g = pl.program_id(2)
        cur_mask, prev_mask = _band_masks()
        first_head = lax.broadcasted_iota(jnp.int32, (ATTN_BLOCK, PAIR_W), 1) < HEAD_DIM

        def run(dil, o_slab, l_slab):
            nb = SEQ // dil // ATTN_BLOCK

            def block(idx, carry):
                r, n = lax.div(idx, nb), lax.rem(idx, nb)
                cur, prev = _block_rows(dil, r, n), _block_rows(dil, r, jnp.maximum(n - 1, 0))
                q = qs[cur, :].astype(BF16)
                kc, kp = ks[cur, :].astype(BF16), ks[prev, :].astype(BF16)
                vc, vp = v_ref[cur, :].astype(BF16), v_ref[prev, :].astype(BF16)
                q2 = _stack_heads(q, first_head)
                mask = _stack_rows(jnp.concatenate([jnp.logical_and(prev_mask, n > 0), cur_mask], axis=1))
                s2 = jnp.where(mask, _dot(q2, jnp.concatenate([kp, kc], axis=0), "nt"), NEG_INF)
                m = jnp.max(s2, axis=-1, keepdims=True)
                vcat, two = jnp.concatenate([vp, vc], axis=0), _stack_rows(first_head)
                vext = jnp.concatenate([jnp.where(two, vcat, 1), jnp.where(two, 1, vcat)], axis=1)
                r2 = _dot(jnp.exp(s2 - m).astype(BF16), vext, "nn")
                r0, r1 = r2[:ATTN_BLOCK, :PAIR_W], r2[ATTN_BLOCK:, PAIR_W:]
                num = jnp.where(first_head, r0, r1)
                den = pltpu.roll(jnp.where(first_head, r1, r0), HEAD_DIM, 1)
                o_slab[cur, :] = num / den
                l_slab[cur, :] = jnp.where(first_head, m[:ATTN_BLOCK], m[ATTN_BLOCK:]) + jnp.log(den)
                return carry

            lax.fori_loop(0, SEQ // ATTN_BLOCK, block, 0, unroll=4)

        for gi, (o_slab, l_slab) in enumerate(((o0, l0), (o1, l1), (o2, l2))):
            @pl.when(g == gi)
            def _(gi=gi, o_slab=o_slab, l_slab=l_slab):
                run(DILATIONS[gi], o_slab, l_slab)

        @pl.when(g == N_GROUPS - 1)
        def _():
            a, b, cc = l0[...], l1[...], l2[...]
            m = jnp.maximum(jnp.maximum(a, b), cc)
            e0, e1, e2 = jnp.exp(a - m), jnp.exp(b - m), jnp.exp(cc - m)
            tot = e0 + e1 + e2
            attn = (e0 * o0[...] + e1 * o1[...] + e2 * o2[...]) / tot
            attn_ref[...] = attn
            attn_b_ref[...] = attn.astype(BF16)
            lse_ref[...] = m + jnp.log(tot)

    shape = (LOCAL_BATCH, SEQ, GROUP_W)
    slab = pltpu.VMEM((SEQ, PAIR_W), F32)
    return hosted_call(
        body, comm, "attn_fwd", (LOCAL_BATCH, N_PAIRS, N_GROUPS),
        [_slab_spec(0), _slab_spec(1), _slab_spec(2)], [_PAIR_SPEC] * 3,
        [jax.ShapeDtypeStruct(shape, BF16), jax.ShapeDtypeStruct(shape, F32), jax.ShapeDtypeStruct(shape, F32)],
        [slab] * 6, (qkv, qkv, qkv), ("parallel", "parallel", "arbitrary"))


def attn_bwd(qkv, tables, dattn, attn, lse, comm=None):
    scale = HEAD_DIM ** -0.5

    def body(qs, ks, v_ref, c_ref, sa_ref, sb_ref, do_ref, out_ref, lse_ref, dqkv_ref, dl, dq_s, dk_s, dv_s):
        g = pl.program_id(2)
        c, sa, sb = c_ref[...], sa_ref[...], sb_ref[...]

        @pl.when(g == 0)
        def _():
            prod = do_ref[...] * out_ref[...]
            dl[...] = _per_head(
                lambda sl: jnp.broadcast_to(jnp.sum(prod[:, sl], axis=-1, keepdims=True), (SEQ, HEAD_DIM)))

        cur_mask, prev_mask = _band_masks()
        first_head = lax.broadcasted_iota(jnp.int32, (ATTN_BLOCK, PAIR_W), 1) < HEAD_DIM

        def run(dil):
            nb = SEQ // dil // ATTN_BLOCK

            def block(idx, carry):
                r, n = lax.div(idx, nb), lax.rem(idx, nb)
                cur = _block_rows(dil, r, n)
                prev = _block_rows(dil, r, jnp.maximum(n - 1, 0))
                nxt = _block_rows(dil, r, jnp.minimum(n + 1, nb - 1))
                q0, q1 = qs[cur, :].astype(BF16), qs[nxt, :].astype(BF16)
                kp, kc = ks[prev, :].astype(BF16), ks[cur, :].astype(BF16)
                vp, vc = v_ref[prev, :].astype(BF16), v_ref[cur, :].astype(BF16)
                do0, do1 = do_ref[cur, :].astype(BF16), do_ref[nxt, :].astype(BF16)
                lse0, lse1, dl0, dl1 = lse_ref[cur, :], lse_ref[nxt, :], dl[cur, :], dl[nxt, :]
                has_prev = jnp.logical_and(prev_mask, n > 0)
                has_next = jnp.logical_and(prev_mask, n < nb - 1)

                def per_row(t):
                    return jnp.concatenate([t[:, 0:1], t[:, HEAD_DIM:HEAD_DIM + 1]], axis=0)

                q20, q21 = _stack_heads(q0, first_head), _stack_heads(q1, first_head)
                do20, do21 = _stack_heads(do0, first_head), _stack_heads(do1, first_head)
                kcat, vcat = jnp.concatenate([kp, kc], axis=0), jnp.concatenate([vp, vc], axis=0)
                mask0 = _stack_rows(jnp.concatenate([has_prev, cur_mask], axis=1))
                p0 = jnp.where(mask0, jnp.exp(_dot(q20, kcat, "nt") - per_row(lse0)), 0.0)
                ds0 = (p0 * (_dot(do20, vcat, "nt") - per_row(dl0))).astype(BF16)
                p1 = jnp.where(_stack_rows(has_next), jnp.exp(_dot(q21, kc, "nt") - per_row(lse1)), 0.0)
                ds1 = (p1 * (_dot(do21, vc, "nt") - per_row(dl1))).astype(BF16)
                dq2 = _dot(ds0, kcat, "nn")
                dq_s[cur, :] = jnp.where(first_head, dq2[:ATTN_BLOCK], dq2[ATTN_BLOCK:])
                ds_cur = jnp.concatenate([ds0[:, ATTN_BLOCK:], ds1], axis=0)
                p_cur = jnp.concatenate([p0[:, ATTN_BLOCK:], p1], axis=0).astype(BF16)
                dk_s[cur, :] = _dot(ds_cur, jnp.concatenate([q20, q21], axis=0), "tn")
                dv_s[cur, :] = _dot(p_cur, jnp.concatenate([do20, do21], axis=0), "tn")
                return carry

            lax.fori_loop(0, SEQ // ATTN_BLOCK, block, 0, unroll=2)

        for gi in range(N_GROUPS):
            @pl.when(g == gi)
            def _(gi=gi):
                run(DILATIONS[gi])

        dqkv_ref[:, 0:PAIR_W] = _rope_bwd(dq_s[...] * scale, c, sa, sb).astype(BF16)
        dqkv_ref[:, PAIR_W:2 * PAIR_W] = _rope_bwd(dk_s[...], c, sa, sb).astype(BF16)
        dqkv_ref[:, 2 * PAIR_W:] = dv_s[...].astype(BF16)

    slab = pltpu.VMEM((SEQ, PAIR_W), F32)
    return hosted_call(
        body, comm, "attn_bwd", (LOCAL_BATCH, N_PAIRS, N_GROUPS),
        [_slab_spec(0), _slab_spec(1), _slab_spec(2), _TABLE_SPEC, _TABLE_SPEC, _TABLE_SPEC,
         _PAIR_SPEC, _PAIR_SPEC, _PAIR_SPEC],
        [pl.BlockSpec((None, SEQ, 3 * PAIR_W), lambda b, p, g: (b, 0, p * N_GROUPS + g))],
        [jax.ShapeDtypeStruct((LOCAL_BATCH, SEQ, QKV_W), BF16)],
        [slab] * 4, (qkv, qkv, qkv, *tables, dattn, attn, lse), ("parallel", "parallel", "arbitrary"))


def _discretize(lr, li, log_dt, br, bi):
    dt = jnp.exp(log_dt)
    mag = jnp.exp(lr * dt)
    ab_re, ab_im = mag * jnp.cos(li * dt), mag * jnp.sin(li * dt)
    den = lr * lr + li * li
    nr, ni = ab_re - 1.0, ab_im
    f_re = (nr * lr + ni * li) / den
    f_im = (ni * lr - nr * li) / den
    return ab_re, ab_im, f_re[None] * br - f_im[None] * bi, f_re[None] * bi + f_im[None] * br


def ssm_prep(lr, li, log_dt, br, bi):
    def body(lr_ref, li_ref, dt_ref, br_ref, bi_ref, *outs):
        for o, v in zip(outs, _discretize(lr_ref[...], li_ref[...], dt_ref[...], br_ref[...], bi_ref[...])):
            o[...] = v
    shapes = [lr, li, br, bi]
    return pl.pallas_call(body, name="ssm_prep",
                          out_shape=[jax.ShapeDtypeStruct(s.shape, F32) for s in shapes])(lr, li, log_dt, br, bi)


def ssm_prep_bwd(lr, li, log_dt, br, bi, g_ab_re, g_ab_im, g_bb_re, g_bb_im):
    def body(lr_ref, li_ref, dt_ref, br_ref, bi_ref, g0, g1, g2, g3, *outs):
        _, vjp = jax.vjp(_discretize, lr_ref[...], li_ref[...], dt_ref[...], br_ref[...], bi_ref[...])
        for o, v in zip(outs, vjp((g0[...], g1[...], g2[...], g3[...]))):
            o[...] = v
    shapes = [lr, li, log_dt, br, bi]
    return pl.pallas_call(body, name="ssm_prep_bwd",
                          out_shape=[jax.ShapeDtypeStruct(s.shape, F32) for s in shapes])(
        lr, li, log_dt, br, bi, g_ab_re, g_ab_im, g_bb_re, g_bb_im)


def _block_diag(t):
    per = SSM_STATE_W // SSM_LANE_BLOCKS // 64
    g = t.transpose(1, 0, 2).reshape(SSM_LANE_BLOCKS, per, 16, 64)
    eye = jnp.eye(per, dtype=t.dtype)
    return jnp.einsum("jgcn,gh->jgchn", g, eye).reshape(SSM_LANE_BLOCKS, per * 16, per * 64)


def _block_diag_t(m):
    per = SSM_STATE_W // SSM_LANE_BLOCKS // 64
    m5 = m.reshape(SSM_LANE_BLOCKS, per, 16, per, 64)
    d = jnp.einsum("jgchn,gh->jgcn", m5, jnp.eye(per, dtype=m.dtype))
    return d.reshape(SSM_LANE_BLOCKS * per, 16, 64).transpose(1, 0, 2)


def _cmul(ar, ai, br, bi):
    return ar * br - ai * bi, ar * bi + ai * br


def _power_tables(ar, ai, reverse):
    width = ar.shape[1]
    row = lax.broadcasted_iota(jnp.int32, (8, width), 0)
    pows = [(ar, ai)]
    for _ in range(7):
        pows.append(_cmul(pows[-1][0], pows[-1][1], ar, ai))
    steps = []
    for k in (1, 2, 4):
        keep = (row >= k) if not reverse else (row < 8 - k)
        steps.append((jnp.where(keep, pows[k - 1][0], 0.0), jnp.where(keep, pows[k - 1][1], 0.0)))
    cr = jnp.zeros((8, width), F32)
    ci = jnp.zeros((8, width), F32)
    for i in range(8):
        pr, pi = pows[i] if not reverse else pows[7 - i]
        cr = jnp.where(row == i, pr, cr)
        ci = jnp.where(row == i, pi, ci)
    return steps, (cr, ci)


SCAN_CHUNK = 2048
STATE_BLOCK = SSM_STATE_W // SSM_LANE_BLOCKS
CHAN_BLOCK = SSM_W // SSM_LANE_BLOCKS


def ssm_fwd(u, ab_re, ab_im, bb_re, bb_im, cb_re, cb_im, d_skip, comm=None):
    nt = SEQ // SCAN_CHUNK
    chan = pl.BlockSpec((None, SCAN_CHUNK, CHAN_BLOCK), lambda b, j, t: (b, t, j))
    state = pl.BlockSpec((None, SCAN_CHUNK, STATE_BLOCK), lambda b, j, t: (b, t, j))
    mat = pl.BlockSpec((None, CHAN_BLOCK, STATE_BLOCK), lambda b, j, t: (j, 0, 0))
    lane = pl.BlockSpec((1, STATE_BLOCK), lambda b, j, t: (0, j))
    dsp = pl.BlockSpec((1, CHAN_BLOCK), lambda b, j, t: (0, j))

    def body(u_ref, ar_ref, ai_ref, bbr_ref, bbi_ref, cbr_ref, cbi_ref, d_ref, y_ref, yg_ref, xr_ref, xi_ref,
             car_r, car_i):
        @pl.when(pl.program_id(2) == 0)
        def _():
            car_r[...] = jnp.zeros_like(car_r)
            car_i[...] = jnp.zeros_like(car_i)

        steps, (pr, pi) = _power_tables(ar_ref[...], ai_ref[...], reverse=False)
        uf = u_ref[...]
        ub = uf.astype(BF16)
        xr_ref[...] = _dot(ub, bbr_ref[...], "nn")
        xi_ref[...] = _dot(ub, bbi_ref[...], "nn")

        def tile(i, carry):
            cr, ci = carry
            sl = pl.ds(pl.multiple_of(i * 8, 8), 8)
            br, bi = xr_ref[sl, :], xi_ref[sl, :]
            for k, (sr, si) in zip((1, 2, 4), steps):
                tr, ti = _cmul(sr, si, pltpu.roll(br, k, 0), pltpu.roll(bi, k, 0))
                br, bi = br + tr, bi + ti
            tr, ti = _cmul(pr, pi, cr, ci)
            br, bi = br + tr, bi + ti
            xr_ref[sl, :] = br
            xi_ref[sl, :] = bi
            return br[7:8, :], bi[7:8, :]

        cr, ci = lax.fori_loop(0, SCAN_CHUNK // 8, tile, (car_r[0:1, :], car_i[0:1, :]), unroll=4)
        car_r[0:1, :] = cr
        car_i[0:1, :] = ci
        y = (_dot(xr_ref[...].astype(BF16), cbr_ref[...], "nt") - _dot(xi_ref[...].astype(BF16), cbi_ref[...], "nt")
             + d_ref[...] * uf)
        y_ref[...] = y
        yg_ref[...] = jax.nn.gelu(y).astype(BF16)

    return hosted_call(
        body, comm, "ssm_fwd", (LOCAL_BATCH, SSM_LANE_BLOCKS, nt),
        [chan, lane, lane, mat, mat, mat, mat, dsp], [chan, chan, state, state],
        [jax.ShapeDtypeStruct((LOCAL_BATCH, SEQ, SSM_W), F32), jax.ShapeDtypeStruct((LOCAL_BATCH, SEQ, SSM_W), BF16),
         jax.ShapeDtypeStruct((LOCAL_BATCH, SEQ, SSM_STATE_W), F32),
         jax.ShapeDtypeStruct((LOCAL_BATCH, SEQ, SSM_STATE_W), F32)],
        [pltpu.VMEM((8, STATE_BLOCK), F32), pltpu.VMEM((8, STATE_BLOCK), F32)],
        (u, ab_re, ab_im, bb_re, bb_im, cb_re, cb_im, d_skip), ("parallel", "parallel", "arbitrary"))


def ssm_bwd(dyg, y, u, xr, xi, ab_re, ab_im, bb_re, bb_im, cb_re, cb_im, d_skip, comm=None):
    nt = SEQ // SCAN_CHUNK
    ntile = SCAN_CHUNK // 8

    def rev(t):
        return nt - 1 - t

    chan = pl.BlockSpec((None, SCAN_CHUNK, CHAN_BLOCK), lambda j, b, t: (b, rev(t), j))
    state = pl.BlockSpec((None, SCAN_CHUNK, STATE_BLOCK), lambda j, b, t: (b, rev(t), j))
    before = pl.BlockSpec((None, 8, STATE_BLOCK), lambda j, b, t: (b, jnp.maximum(rev(t) * ntile - 1, 0), j))
    mat = pl.BlockSpec((None, CHAN_BLOCK, STATE_BLOCK), lambda j, b, t: (j, 0, 0))
    lane = pl.BlockSpec((1, STATE_BLOCK), lambda j, b, t: (0, j))
    lane8 = pl.BlockSpec((8, STATE_BLOCK), lambda j, b, t: (0, j))
    dsp = pl.BlockSpec((1, CHAN_BLOCK), lambda j, b, t: (0, j))

    def body(dyg_ref, y_ref, u_ref, xr_ref, xi_ref, xrb_ref, xib_ref, ar_ref, ai_ref, bbr_ref, bbi_ref, cbr_ref,
             cbi_ref, d_ref, du_ref, dcbr_ref, dcbi_ref, dbbr_ref, dbbi_ref, dd_ref, dar_ref, dai_ref,
             lam_r, lam_i, car_r, car_i):
        b, t = pl.program_id(1), pl.program_id(2)
        first = jnp.logical_and(b == 0, t == 0)

        @pl.when(t == 0)
        def _():
            car_r[...] = jnp.zeros_like(car_r)
            car_i[...] = jnp.zeros_like(car_i)

        @pl.when(first)
        def _():
            for r in (dcbr_ref, dcbi_ref, dbbr_ref, dbbi_ref, dd_ref, dar_ref, dai_ref):
                r[...] = jnp.zeros_like(r)

        steps, (pr, pi) = _power_tables(ar_ref[...], -ai_ref[...], reverse=True)
        uf = u_ref[...]
        _, gelu_vjp = jax.vjp(jax.nn.gelu, y_ref[...])
        dy = gelu_vjp(dyg_ref[...])[0]
        dyb = dy.astype(BF16)
        dd_ref[...] += _colsum(dy * uf)
        lam_r[...] = _dot(dyb, cbr_ref[...], "nn")
        lam_i[...] = -_dot(dyb, cbi_ref[...], "nn")
        dcbr_ref[...] += _dot(dyb, xr_ref[...].astype(BF16), "tn")
        dcbi_ref[...] -= _dot(dyb, xi_ref[...].astype(BF16), "tn")
        row0 = lax.broadcasted_iota(jnp.int32, (8, STATE_BLOCK), 0) == 0
        has_before = rev(t) > 0
        xrb = jnp.where(has_before, xrb_ref[...], 0.0)
        xib = jnp.where(has_before, xib_ref[...], 0.0)

        def tile(s, carry):
            cr, ci, acc_r, acc_i = carry
            i = ntile - 1 - s
            sl = pl.ds(pl.multiple_of(i * 8, 8), 8)
            gr, gi = lam_r[sl, :], lam_i[sl, :]
            for k, (sr, si) in zip((1, 2, 4), steps):
                tr, ti = _cmul(sr, si, pltpu.roll(gr, 8 - k, 0), pltpu.roll(gi, 8 - k, 0))
                gr, gi = gr + tr, gi + ti
            tr, ti = _cmul(pr, pi, cr, ci)
            gr, gi = gr + tr, gi + ti
            lam_r[sl, :] = gr
            lam_i[sl, :] = gi
            sp = pl.ds(pl.multiple_of(jnp.maximum(i - 1, 0) * 8, 8), 8)
            pvr = jnp.where(i > 0, xr_ref[sp, :], xrb)
            pvi = jnp.where(i > 0, xi_ref[sp, :], xib)
            xsr = jnp.where(row0, pltpu.roll(pvr, 1, 0), pltpu.roll(xr_ref[sl, :], 1, 0))
            xsi = jnp.where(row0, pltpu.roll(pvi, 1, 0), pltpu.roll(xi_ref[sl, :], 1, 0))
            acc_r = acc_r + xsr * gr + xsi * gi
            acc_i = acc_i + xsr * gi - xsi * gr
            return gr[0:1, :], gi[0:1, :], acc_r, acc_i

        zero = jnp.zeros((8, STATE_BLOCK), F32)
        cr, ci, acc_r, acc_i = lax.fori_loop(0, ntile, tile, (car_r[0:1, :], car_i[0:1, :], zero, zero), unroll=2)
        car_r[0:1, :] = cr
        car_i[0:1, :] = ci
        dar_ref[...] += acc_r
        dai_ref[...] += acc_i
        lrb, lib = lam_r[...].astype(BF16), lam_i[...].astype(BF16)
        du = _dot(lrb, bbr_ref[...], "nt") + _dot(lib, bbi_ref[...], "nt") + d_ref[...] * dy
        du_ref[...] = du.astype(BF16)
        ub = uf.astype(BF16)
        dbbr_ref[...] += _dot(ub, lrb, "tn")
        dbbi_ref[...] += _dot(ub, lib, "tn")

    mat_shape = jax.ShapeDtypeStruct((SSM_LANE_BLOCKS, CHAN_BLOCK, STATE_BLOCK), F32)
    return hosted_call(
        body, comm, "ssm_bwd", (SSM_LANE_BLOCKS, LOCAL_BATCH, nt),
        [chan, chan, chan, state, state, before, before, lane, lane, mat, mat, mat, mat, dsp],
        [chan, mat, mat, mat, mat, dsp, lane8, lane8],
        [jax.ShapeDtypeStruct((LOCAL_BATCH, SEQ, SSM_W), BF16), mat_shape, mat_shape, mat_shape, mat_shape,
         jax.ShapeDtypeStruct((1, SSM_W), F32), jax.ShapeDtypeStruct((8, SSM_STATE_W), F32),
         jax.ShapeDtypeStruct((8, SSM_STATE_W), F32)],
        [pltpu.VMEM((SCAN_CHUNK, STATE_BLOCK), F32), pltpu.VMEM((SCAN_CHUNK, STATE_BLOCK), F32),
         pltpu.VMEM((8, STATE_BLOCK), F32), pltpu.VMEM((8, STATE_BLOCK), F32)],
        (dyg, y, u, xr, xi, xr, xi, ab_re, ab_im, bb_re, bb_im, cb_re, cb_im, d_skip),
        ("parallel", "arbitrary", "arbitrary"))


def _merge_fn(g0, g1, attn_d, za, zb):
    return jax.nn.sigmoid(g0) * attn_d + jax.nn.sigmoid(g1) * (za * jax.nn.sigmoid(zb))


def _swiglu_fn(a, b):
    return jax.nn.silu(a) * b


def _own_slot(slots, shard):
    me = 2 * lax.axis_index("x") + lax.axis_index("y")
    mine = lax.broadcasted_iota(jnp.int32, (N_CHIPS, 1, 1), 0) == me
    return jnp.where(mine, shard[None], slots)


def _reduce_start(names, gw, shard_shapes):
    return swap_comm([_to_slots(n, gw[n], shard_shapes[n]) for n in names])


def _reduce_chip(names, swap, got, core):
    return exchange_comm([add_halves(n, g, r, core) for n, g, r in zip(names, swap.ins, got)])


def local_step(x, target, shards, small, core):
    g_mix, g_ffn, g_final = small["norm_mix_g"], small["norm_ffn_g"], small["norm_final_g"]
    tables = _rope_tables()
    seqs = lambda t: t.reshape(LOCAL_BATCH, SEQ, t.shape[-1])
    toks = lambda t: t.reshape(TOKENS, t.shape[-1])
    shard_shapes = {n: s.shape for n, s in shards.items()}
    w = {}

    def gather(names):
        return gather_comm([shards[n] for n in names])

    def arrived(names, slots, own=None):
        for n, s in zip(names, slots):
            w[n] = _from_slots(n, s if own is None else _own_slot(s, own))

    later = [n for n in BIG if n != "w_in"]
    sems, w_in_shard, land, token = split_start(shards["w_in"].astype(BF16), "w_in_gather_start")
    zero = token[0, 0]
    h, *rest = first_norm(x, g_mix + zero, [shards[n] for n in later])
    shards = dict(shards)
    shards.update(zip(later, rest))
    br_t = small["ssm_b_re"].transpose(2, 0, 1)
    bi_t = small["ssm_b_im"].transpose(2, 0, 1)
    log_dt = small["ssm_log_dt"].reshape(32, 1)
    ab_re, ab_im, bb_re_t, bb_im_t = ssm_prep(small["ssm_a_re"] + zero, small["ssm_a_im"], log_dt, br_t, bi_t)
    ab = [ab_re.reshape(1, SSM_STATE_W), ab_im.reshape(1, SSM_STATE_W)]
    bb = [_block_diag(bb_re_t).astype(BF16), _block_diag(bb_im_t).astype(BF16)]
    cb = [_block_diag((small["ssm_c_re"] + zero).transpose(1, 0, 2)).astype(BF16),
          _block_diag((small["ssm_c_im"] + zero).transpose(1, 0, 2)).astype(BF16)]
    d_skip = small["ssm_d"].reshape(1, SSM_W)
    w_in_shard, land = split_wait(sems, w_in_shard, land, [h] + bb + cb, "w_in_gather_wait")
    arrived(["w_in"], [handover(land, "w_in_handover")], own=w_in_shard)
    w_qkv, w_u, w_gate = _qkv_order(w["w_in"][:QKV_W]), w["w_in"][QKV_W:QKV_W + SSM_W], w["w_in"][QKV_W + SSM_W:]
    qkv, *slots = proj_qkv(h, w_qkv, tables, comm=gather(["w_attn_out", "w_glu"]))
    arrived(["w_attn_out", "w_glu"], slots)
    qkv = seqs(qkv)
    u = seqs(matmul(h, w_u, "nt", F32, "proj_u"))
    gl, *slots = matmul(h, w_gate, "nt", BF16, "proj_gate", comm=gather(["w_out"]))
    arrived(["w_out"], slots)
    attn_b, attn, lse, *slots = attn_fwd(qkv, comm=gather(["w_ffn_gate"]))
    arrived(["w_ffn_gate"], slots)
    attn_b = toks(attn_b)
    attn_d = matmul(attn_b, w["w_attn_out"], "nn", F32, "attn_out")
    y, yg, xr, xi, *slots = ssm_fwd(u, *ab, *bb, *cb, d_skip, comm=gather(["w_ffn_up"]))
    arrived(["w_ffn_up"], slots)
    yg2 = toks(yg)
    z = matmul(yg2, w["w_glu"], "nn", BF16, "glu")
    gate_ins = [(gl, D_MODEL, 0), (gl, D_MODEL, 1), attn_d, (z, D_MODEL, 0), (z, D_MODEL, 1)]
    (merged,) = rowwise(lambda *v: (_merge_fn(*[t.astype(F32) for t in v]),), gate_ins, [(D_MODEL, BF16)], "merge")
    x1, h2 = matmul_rows(merged, w["w_out"], "out_proj", lambda rows, g: (rows, _rms(rows, g)), [g_ffn],
                         [(D_MODEL, F32), (D_MODEL, BF16)], add=x)
    a, b, act, *slots = ffn_in(h2, w["w_ffn_gate"], w["w_ffn_up"], comm=gather(["w_ffn_down"]))
    arrived(["w_ffn_down"], slots)

    def final_fn(xv, g, tgt):
        yv, vjp = jax.vjp(_rms, xv, g)
        err = yv - tgt
        dx, dg = vjp(err * (1.0 / D_MODEL))
        loss = 0.5 * jnp.sum(jnp.mean(err * err, axis=-1, keepdims=True), axis=0, keepdims=True)
        return dx, dx, dg, jnp.broadcast_to(loss, (1, LANES))

    dx2, dx2_b, dg_final, loss = matmul_rows(act, w["w_ffn_down"], "ffn_down_loss", final_fn, [g_final, target],
                                             [(D_MODEL, F32), (D_MODEL, BF16)], accs=(D_MODEL, LANES), add=x1)
    gw, parts = {}, {}
    gw["w_ffn_down"] = matmul(act, dx2_b, "tn", F32, "d_ffn_down")
    da_b, db_b = ffn_in_bwd(dx2_b, w["w_ffn_down"], a, b)
    gw["w_ffn_gate"] = matmul(da_b, h2, "tn", F32, "d_ffn_gate")
    gw["w_ffn_up"] = matmul(db_b, h2, "tn", F32, "d_ffn_up")
    ffn = ["w_ffn_down", "w_ffn_gate", "w_ffn_up"]
    swap = _reduce_start(ffn[:2], gw, shard_shapes)
    dh2, *got = matmul(da_b, w["w_ffn_gate"], "nn", F32, "d_h2_gate", comm=swap)
    ffn_exchange = [_reduce_chip(ffn[:2], swap, got, core)]
    swap = _reduce_start(ffn[2:], gw, shard_shapes)

    def norm_bwd(dh, xv, g, skip):
        _, vjp = jax.vjp(_rms, xv, g)
        dx, dg = vjp(dh)
        dx = dx + skip
        return dx, dx, dg

    dx1, dx1_b, dg_ffn, *got = matmul_rows(db_b, w["w_ffn_up"], "d_h2_up_norm", norm_bwd, [x1, g_ffn, dx2],
                                           [(D_MODEL, F32), (D_MODEL, BF16)], accs=(D_MODEL,), add=dh2, comm=swap)
    ffn_up_exchange = _reduce_chip(ffn[2:], swap, got, core)
    gw["w_out"] = matmul(merged, dx1_b, "tn", F32, "d_out")
    dmerged = matmul(dx1_b, w["w_out"], "nt", F32, "d_merged")

    def merge_bwd(g0, g1, ad, za, zb, dm):
        _, vjp = jax.vjp(_merge_fn, *[t.astype(F32) for t in (g0, g1, ad, za, zb)])
        dg0, dg1, dad, dza, dzb = vjp(dm)
        return jnp.concatenate([dg0, dg1], axis=1), dad, jnp.concatenate([dza, dzb], axis=1)

    dgl_b, dattn_d_b, dz_b, parts["w_ffn_up"] = rowwise(
        merge_bwd, gate_ins + [dmerged], [(GATE_W, BF16), (D_MODEL, BF16), (GATE_W, BF16)], "merge_bwd",
        comm=ffn_up_exchange)
    gw["w_attn_out"] = matmul(attn_b, dattn_d_b, "tn", F32, "d_attn_out")
    dattn = seqs(matmul(dattn_d_b, w["w_attn_out"], "nt", F32, "d_attn"))
    gw["w_glu"] = matmul(yg2, dz_b, "tn", F32, "d_glu")
    dyg = seqs(matmul(dz_b, w["w_glu"], "nt", F32, "d_yg"))
    mixer = ["w_out", "w_attn_out", "w_glu"]
    swap = _reduce_start(mixer, gw, shard_shapes)
    du_b, dcb_re, dcb_im, dbb_re, dbb_im, dd, da_re8, da_im8, *rest = ssm_bwd(
        dyg, y, u, xr, xi, *ab, *bb, *cb, d_skip, comm=join_comms(ffn_exchange + [swap]))
    for n, p in zip(ffn[:2], rest[:2]):
        parts[n] = p
    mixer_exchange = _reduce_chip(mixer, swap, rest[2:], core)
    du_b = toks(du_b)
    g_ab_re = jnp.sum(da_re8, axis=0).reshape(32, 64)
    g_ab_im = jnp.sum(da_im8, axis=0).reshape(32, 64)
    d_lr, d_li, d_ldt, d_br_t, d_bi_t = ssm_prep_bwd(
        small["ssm_a_re"], small["ssm_a_im"], log_dt, br_t, bi_t,
        g_ab_re, g_ab_im, _block_diag_t(dbb_re), _block_diag_t(dbb_im))
    as_gcn = lambda t: t.transpose(1, 0, 2).reshape(SSM_W, 64)
    gs = {
        "ssm_a_re": d_lr, "ssm_a_im": d_li, "ssm_log_dt": d_ldt.reshape(1, 32),
        "ssm_b_re": as_gcn(d_br_t), "ssm_b_im": as_gcn(d_bi_t),
        "ssm_c_re": as_gcn(_block_diag_t(dcb_re)), "ssm_c_im": as_gcn(_block_diag_t(dcb_im)),
        "ssm_d": dd.reshape(32, 16).T,
    }
    ssm_gather = small_comm([gs[n] for n in SSM_SMALL])
    dqkv_b, *rest = attn_bwd(qkv, tables, dattn, attn, lse, comm=join_comms([mixer_exchange, ssm_gather]))
    for n, p in zip(mixer, rest):
        parts[n] = p
    ssm_shares = rest[len(mixer):]
    dqkv_b = toks(dqkv_b)
    d_qkv = matmul(dqkv_b, h, "tn", F32, "d_w_qkv")
    d_u = matmul(du_b, h, "tn", F32, "d_w_u")
    d_gate = matmul(dgl_b, h, "tn", F32, "d_w_gate")
    gw["w_in"] = jnp.concatenate([_qkv_order(d_qkv, back=True), d_u, d_gate], axis=0)
    swap = _reduce_start(["w_in"], gw, shard_shapes)
    dh, *got = matmul(dqkv_b, w_qkv, "nn", F32, "d_h_qkv", comm=swap)
    chip_sum = add_halves("w_in", swap.ins[0], got[0], core)
    sems, chip_sum, land, token = split_start(chip_sum, "w_in_reduce_start", per_chip=True)
    grad_x, dg_mix = mix_in_bwd([du_b, dgl_b], [w_u, w_gate], dh, x, g_mix + token[0, 0], dx1)
    gs_norm = {"norm_mix_g": dg_mix, "norm_ffn_g": dg_ffn, "norm_final_g": dg_final}
    return loss, grad_x, parts, ssm_shares, gs_norm, (sems, chip_sum, land)


ANY = pl.BlockSpec(memory_space=pl.ANY)
BIG = ("w_in", "w_glu", "w_attn_out", "w_out", "w_ffn_gate", "w_ffn_up", "w_ffn_down")
TRANSPOSED = ("w_in", "w_ffn_gate", "w_ffn_up")
ROW_SHARDED = TRANSPOSED + ("w_out", "w_ffn_down")
SMALL = ("norm_mix_g", "ssm_a_re", "ssm_a_im", "ssm_log_dt", "ssm_b_re", "ssm_b_im", "ssm_c_re", "ssm_c_im",
         "ssm_d", "norm_ffn_g", "norm_final_g")
WEIGHTS = ("norm_mix_g", "w_in", "ssm_a_re", "ssm_a_im", "ssm_log_dt", "ssm_b_re", "ssm_b_im", "ssm_c_re",
           "ssm_c_im", "ssm_d", "w_glu", "w_attn_out", "w_out", "norm_ffn_g", "w_ffn_gate", "w_ffn_up",
           "w_ffn_down", "norm_final_g")
SSM_SMALL = SMALL[1:9]
NORM_SMALL = (SMALL[0],) + SMALL[9:]
NORM_ROWS = 32
N_BIG = len(BIG)


def _position():
    return lax.axis_index("x"), lax.axis_index("y"), lax.axis_index("c")


def _other_chips(x, y):
    return [(1 - x, y), (x, 1 - y), (1 - x, 1 - y)]


def _remote(src, dst, send_sem, recv_sem, device):
    return pltpu.make_async_remote_copy(src_ref=src, dst_ref=dst, send_sem=send_sem, recv_sem=recv_sem,
                                        device_id=device, device_id_type=MESH)


_later = functools.partial


def _two_level_phases(copies):
    def first(*refs):
        locals_, sends, _, _, _ = copies(*refs)
        for cp in locals_ + sends:
            cp().start()

    def mid(*refs):
        _, _, arrived, passed, _ = copies(*refs)
        for got, cp in zip(arrived, passed):
            got().wait_recv()
            cp().start()

    def last(*refs):
        locals_, sends, _, passed, from_sibling = copies(*refs)
        for cp in from_sibling:
            cp().wait_recv()
        for cp in sends + passed:
            cp().wait_send()
        for cp in locals_:
            cp().wait()

    return first, mid, last


def _half(ref, chip, which):
    rows = ref.shape[1] // 2
    return ref.at[chip, pl.ds(which * rows, rows), :]


class Comm:
    def __init__(self, ins, out_shapes, sems, first, mid, last):
        self.ins, self.out_shapes, self.sems = list(ins), list(out_shapes), list(sems)
        self.first, self.mid, self.last = first, mid, last


def join_comms(comms):
    def cut(refs_by_kind):
        offs, parts = [0, 0, 0], []
        for cm in comms:
            sizes = (len(cm.ins), len(cm.out_shapes), len(cm.sems))
            parts.append(tuple(refs_by_kind[k][offs[k]:offs[k] + sizes[k]] for k in range(3)))
            offs = [o + s for o, s in zip(offs, sizes)]
        return parts

    def phase(which):
        def run(ins, outs, sems):
            for cm, part in zip(comms, cut((ins, outs, sems))):
                fn = getattr(cm, which)
                if fn is not None:
                    fn(*part)
        return run

    return Comm(sum((cm.ins for cm in comms), []), sum((cm.out_shapes for cm in comms), []),
                sum((cm.sems for cm in comms), []), phase("first"), phase("mid"), phase("last"))


def _comm_operands(comm):
    if comm is None:
        return [], [], []
    return comm.ins, comm.out_shapes, comm.sems


def _comm_begin(comm, refs, step, n_steps):
    if comm is None:
        return
    pl.when(step == 0)(lambda: comm.first(*refs))
    if comm.mid is not None:
        pl.when(step == (n_steps * 3) // 4)(lambda: comm.mid(*refs))


def _comm_end(comm, refs, step, n_steps):
    if comm is not None:
        pl.when(step == n_steps - 1)(lambda: comm.last(*refs))


def _comm_refs(comm, refs, n_in, n_out):
    if comm is None:
        return list(refs), None
    ci, co, cs = len(comm.ins), len(comm.out_shapes), len(comm.sems)
    o0 = n_in + ci
    s0 = o0 + n_out + co
    host = list(refs[:n_in]) + list(refs[o0:o0 + n_out]) + list(refs[s0:len(refs) - cs])
    return host, (list(refs[n_in:o0]), list(refs[o0 + n_out:s0]), list(refs[len(refs) - cs:]))


def run_comm(comm, name):
    n_in, n_out = len(comm.ins), len(comm.out_shapes)

    def body(*refs):
        parts = (list(refs[:n_in]), list(refs[n_in:n_in + n_out]), list(refs[n_in + n_out:]))
        comm.first(*parts)
        if comm.mid is not None:
            comm.mid(*parts)
        comm.last(*parts)

    return pl.pallas_call(body, name=name, in_specs=[ANY] * n_in, out_specs=[ANY] * n_out,
                          out_shape=comm.out_shapes, scratch_shapes=comm.sems)(*comm.ins)


def hosted_call(work, comm, name, grid, in_specs, out_specs, out_shape, scratch_shapes, args, semantics):
    c_ins, c_outs, c_sems = _comm_operands(comm)
    n_steps = math.prod(grid)

    def body(*refs):
        host, c_refs = _comm_refs(comm, refs, len(in_specs), len(out_specs))
        step = 0
        for axis, size in enumerate(grid):
            step = step * size + pl.program_id(axis)
        _comm_begin(comm, c_refs, step, n_steps)
        work(*host)
        _comm_end(comm, c_refs, step, n_steps)

    return pl.pallas_call(
        body, name=name, grid=grid, in_specs=list(in_specs) + [ANY] * len(c_ins),
        out_specs=list(out_specs) + [ANY] * len(c_outs), out_shape=list(out_shape) + c_outs,
        scratch_shapes=list(scratch_shapes) + c_sems,
        compiler_params=_params(semantics if comm is None else ("arbitrary",) * len(grid)),
    )(*args, *c_ins)


def gather_comm(shards):
    n = len(shards)

    def copies(srcs, outs, sems):
        send_sems, recv_sems, local_sems = sems
        x, y, c = _position()
        me = 2 * x + y
        sibling = (x, y, 1 - c)
        chips = _other_chips(x, y)
        locals_ = [_later(pltpu.make_async_copy, s, o.at[me], local_sems.at[i])
                   for i, (s, o) in enumerate(zip(srcs, outs))]
        sends, arrived, passed, from_sibling = [], [], [], []
        for j, (px, py) in enumerate(chips):
            for i, (s, o) in enumerate(zip(srcs, outs)):
                rows = s.shape[0] // 2
                sends.append(_later(_remote, s.at[pl.ds(c * rows, rows), :], _half(o, me, c), send_sems.at[i, j],
                                    recv_sems.at[i, j], (px, py, c)))
                got = _half(o, 2 * px + py, c)
                arrived.append(_later(_remote, got, got, send_sems.at[i, j], recv_sems.at[i, j], (px, py, c)))
                passed.append(_later(_remote, got, got, send_sems.at[i, 3 + j], recv_sems.at[i, 3 + j], sibling))
                other = _half(o, 2 * px + py, 1 - c)
                from_sibling.append(_later(_remote, other, other, send_sems.at[i, 3 + j], recv_sems.at[i, 3 + j],
                                           sibling))
        return locals_, sends, arrived, passed, from_sibling

    return Comm(shards, [jax.ShapeDtypeStruct((N_CHIPS,) + s.shape, s.dtype) for s in shards],
                [pltpu.SemaphoreType.DMA((n, 6)), pltpu.SemaphoreType.DMA((n, 6)), pltpu.SemaphoreType.DMA((n,))],
                *_two_level_phases(copies))


HBM = pl.BlockSpec(memory_space=pltpu.HBM)
SEM = pl.BlockSpec(memory_space=pltpu.SEMAPHORE)
N_OTHER = N_CHIPS - 1


def _ici_halves(src_ref, land_ref, sems, per_chip):
    x, y, c = _position()
    me = 2 * x + y
    rows = land_ref.shape[1] // 2
    sends, arrivals = [], []
    for j, (px, py) in enumerate(_other_chips(x, y)):
        piece = src_ref.at[2 * px + py] if per_chip else src_ref.at[pl.ds(c * rows, rows), :]
        sends.append(_later(_remote, piece, _half(land_ref, me, c), sems[j], sems[N_OTHER + j], (px, py, c)))
        got = _half(land_ref, 2 * px + py, c)
        arrivals.append(_later(_remote, got, got, sems[j], sems[N_OTHER + j], (px, py, c)))
    return sends, arrivals


def split_start(src, name, per_chip=False):
    def body(src_ref, land_ref, *rest):
        sems, token = rest[:2 * N_OTHER], rest[-1]
        for cp in _ici_halves(src_ref, land_ref, sems, per_chip)[0]:
            cp().start()
        token[...] = jnp.zeros_like(token)

    rows, cols = (2 * src.shape[1], src.shape[2]) if per_chip else src.shape
    sem = pltpu.SemaphoreType.DMA(())
    land = (N_CHIPS, rows, cols)
    res = pl.pallas_call(
        body, name=name, in_specs=(HBM, HBM),
        out_specs=(SEM,) * (2 * N_OTHER) + (HBM, HBM, pl.BlockSpec(memory_space=pltpu.VMEM)),
        out_shape=(sem,) * (2 * N_OTHER) + (pltpu.HBM(src.shape, src.dtype), pltpu.HBM(land, src.dtype),
                                           jax.ShapeDtypeStruct((8, LANES), F32)),
        input_output_aliases={0: 2 * N_OTHER, 1: 2 * N_OTHER + 1},
        compiler_params=pltpu.CompilerParams(has_side_effects=pltpu.SideEffectType.DATAFLOW_SIDE_EFFECTING),
    )(pltpu.with_memory_space_constraint(src, pltpu.HBM),
      pltpu.with_memory_space_constraint(lax.empty(land, src.dtype), pltpu.HBM))
    return res[:2 * N_OTHER], res[2 * N_OTHER], res[2 * N_OTHER + 1], res[-1]


def split_wait(sems, src, land, after, name, per_chip=False):
    def body(src_ref, land_ref, *rest):
        sends, arrivals = _ici_halves(src_ref, land_ref, rest[:2 * N_OTHER], per_chip)
        for cp in sends:
            cp().wait_send()
        for cp in arrivals:
            cp().wait_recv()

    return pl.pallas_call(
        body, name=name, in_specs=(HBM, HBM) + (SEM,) * (2 * N_OTHER) + (ANY,) * len(after),
        out_specs=(HBM, HBM), out_shape=(pltpu.HBM(src.shape, src.dtype), pltpu.HBM(land.shape, land.dtype)),
        input_output_aliases={0: 0, 1: 1},
        compiler_params=pltpu.CompilerParams(has_side_effects=pltpu.SideEffectType.DATAFLOW_SIDE_EFFECTING),
    )(src, land, *sems, *after)


def handover(land, name, sums=None):
    n = N_OTHER + (sums is not None)

    def body(*refs):
        land_ref, send_sems, recv_sems = refs[0], refs[-2], refs[-1]
        x, y, c = _position()
        me = 2 * x + y
        sibling = (x, y, 1 - c)
        pieces = [(_half(land_ref, 2 * px + py, c), 2 * px + py) for px, py in _other_chips(x, y)]
        if sums is not None:
            pieces.append((refs[1].at[me], me))
        sends = [_remote(piece, _half(land_ref, chip, c), send_sems.at[j], recv_sems.at[j], sibling)
                 for j, (piece, chip) in enumerate(pieces)]
        for cp in sends:
            cp.start()
        for j, (_, chip) in enumerate(pieces):
            other = _half(land_ref, chip, 1 - c)
            _remote(other, other, send_sems.at[j], recv_sems.at[j], sibling).wait_recv()
        for cp in sends:
            cp.wait_send()

    args = (land,) + ((sums,) if sums is not None else ())
    return pl.pallas_call(
        body, name=name, in_specs=[ANY] * len(args), out_specs=ANY,
        out_shape=jax.ShapeDtypeStruct(land.shape, land.dtype), input_output_aliases={0: 0},
        scratch_shapes=[pltpu.SemaphoreType.DMA((n,)), pltpu.SemaphoreType.DMA((n,))],
    )(*args)


def swap_comm(grads):
    n = len(grads)

    def copies(srcs, gots, sems):
        send_sems, recv_sems = sems
        x, y, c = _position()
        out = []
        for i, (s, o) in enumerate(zip(srcs, gots)):
            rows = s.shape[1] // 2
            out.append(_remote(s.at[:, pl.ds((1 - c) * rows, rows), :], o, send_sems.at[i], recv_sems.at[i],
                               (x, y, 1 - c)))
        return out

    def first(srcs, gots, sems):
        for cp in copies(srcs, gots, sems):
            cp.start()

    def last(srcs, gots, sems):
        for cp in copies(srcs, gots, sems):
            cp.wait()

    return Comm(grads, [jax.ShapeDtypeStruct((N_CHIPS, g.shape[1] // 2, g.shape[2]), g.dtype) for g in grads],
                [pltpu.SemaphoreType.DMA((n,)), pltpu.SemaphoreType.DMA((n,))], first, None, last)


def add_halves(name, g, got, core):
    _, half, cols = got.shape
    mine = pl.BlockSpec((None, half, cols), lambda k, c_ref: (k, c_ref[0], 0))
    other = pl.BlockSpec((None, half, cols), lambda k, c_ref: (k, 0, 0))

    def body(c_ref, g_ref, got_ref, o_ref):
        o_ref[...] = (g_ref[...] + got_ref[...]).astype(BF16)

    return pl.pallas_call(
        body, name="add_halves_" + name,
        grid_spec=pltpu.PrefetchScalarGridSpec(num_scalar_prefetch=1, grid=(N_CHIPS,), in_specs=[mine, other],
                                               out_specs=other),
        out_shape=jax.ShapeDtypeStruct(got.shape, BF16),
        compiler_params=_params(("parallel",)),
    )(core, g, got)


def exchange_comm(parts):
    n = len(parts)

    def copies(srcs, outs, sems):
        send_sems, recv_sems, local_sems = sems
        x, y, c = _position()
        me = 2 * x + y
        sibling = (x, y, 1 - c)
        chips = _other_chips(x, y)
        locals_, sends, arrived, passed, from_sibling = [], [], [], [], []
        for i, (s, o) in enumerate(zip(srcs, outs)):
            locals_.append(_later(pltpu.make_async_copy, s.at[me], _half(o, me, c), local_sems.at[i]))
            sends.append(_later(_remote, s.at[me], _half(o, me, c), send_sems.at[i, 3], recv_sems.at[i, 3], sibling))
            other = _half(o, me, 1 - c)
            from_sibling.append(_later(_remote, other, other, send_sems.at[i, 3], recv_sems.at[i, 3], sibling))
        for j, (px, py) in enumerate(chips):
            for i, (s, o) in enumerate(zip(srcs, outs)):
                sends.append(_later(_remote, s.at[2 * px + py], _half(o, me, c), send_sems.at[i, j],
                                    recv_sems.at[i, j], (px, py, c)))
                got = _half(o, 2 * px + py, c)
                arrived.append(_later(_remote, got, got, send_sems.at[i, j], recv_sems.at[i, j], (px, py, c)))
                passed.append(_later(_remote, got, got, send_sems.at[i, 4 + j], recv_sems.at[i, 4 + j], sibling))
                other = _half(o, 2 * px + py, 1 - c)
                from_sibling.append(_later(_remote, other, other, send_sems.at[i, 4 + j], recv_sems.at[i, 4 + j],
                                           sibling))
        return locals_, sends, arrived, passed, from_sibling

    return Comm(parts, [jax.ShapeDtypeStruct((N_CHIPS, 2 * p.shape[1], p.shape[2]), p.dtype) for p in parts],
                [pltpu.SemaphoreType.DMA((n, 7)), pltpu.SemaphoreType.DMA((n, 7)), pltpu.SemaphoreType.DMA((n,))],
                *_two_level_phases(copies))


def small_comm(shares):
    n = len(shares)

    def copies(srcs, outs, sems):
        send_sems, recv_sems, local_sems = sems
        x, y, c = _position()
        me = 4 * x + 2 * y + c
        flips = [(fx, fy, fc) for fx in (0, 1) for fy in (0, 1) for fc in (0, 1)][1:]
        peers = [(1 - x if fx else x, 1 - y if fy else y, 1 - c if fc else c) for fx, fy, fc in flips]
        locals_, sends, arrived = [], [], []
        for i, (src_ref, out_ref) in enumerate(zip(srcs, outs)):
            locals_.append(_later(pltpu.make_async_copy, src_ref, out_ref.at[me], local_sems.at[i]))
            for j, (px, py, pc) in enumerate(peers):
                sends.append(_later(_remote, src_ref, out_ref.at[me], send_sems.at[i, j], recv_sems.at[i, j],
                                    (px, py, pc)))
                got = out_ref.at[4 * px + 2 * py + pc]
                arrived.append(_later(_remote, got, got, send_sems.at[i, j], recv_sems.at[i, j], (px, py, pc)))
        return locals_, sends, arrived

    def first(*refs):
        locals_, sends, _ = copies(*refs)
        for cp in locals_ + sends:
            cp().start()

    def last(*refs):
        locals_, sends, arrived = copies(*refs)
        for cp in arrived:
            cp().wait_recv()
        for cp in sends:
            cp().wait_send()
        for cp in locals_:
            cp().wait()

    return Comm(shares, [jax.ShapeDtypeStruct((N_DEV,) + s.shape, s.dtype) for s in shares],
                [pltpu.SemaphoreType.DMA((n, 7)), pltpu.SemaphoreType.DMA((n, 7)), pltpu.SemaphoreType.DMA((n,))],
                first, None, last)


def _adam_fn(w, g, m, v):
    m = ADAM_B1 * m + (1.0 - ADAM_B1) * g
    v = ADAM_B2 * v + (1.0 - ADAM_B2) * jnp.square(g)
    m_hat = m / (1.0 - ADAM_B1 ** ADAM_STEP)
    v_hat = v / (1.0 - ADAM_B2 ** ADAM_STEP)
    return -ADAM_LR * (m_hat / (jnp.sqrt(v_hat) + ADAM_EPS) + ADAM_WD * w), m, v


def adam_big(name, parts, w, m, v):
    rows, cols = w.shape
    tm = _pick(rows, 384, 16)

    def fn(p0, p1, p2, p3, wv, mv, vv):
        g = ((p0.astype(F32) + p1.astype(F32)) + p2.astype(F32)) + p3.astype(F32)
        return (g,) + _adam_fn(wv, g, mv, vv)

    return rowwise(fn, [parts, w, m, v], [(cols, F32)] * 4, "adam_" + name, tm=tm, rows=rows)


def adam_small(name, gathered, w, m, v):
    def body(g_ref, w_ref, m_ref, v_ref, go_ref, d_ref, mo_ref, vo_ref):
        g = g_ref[0]
        for k in range(1, N_DEV):
            g = g + g_ref[k]
        go_ref[...] = g
        d_ref[...], mo_ref[...], vo_ref[...] = _adam_fn(w_ref[...], g, m_ref[...], v_ref[...])

    return pl.pallas_call(body, name=name, out_shape=[jax.ShapeDtypeStruct(w.shape, F32)] * 4,
                          compiler_params=_params())(gathered, w, m, v)


def _ssm_2d(name, t):
    t = t[0] if t.ndim > 2 else t
    if name in ("ssm_b_re", "ssm_b_im"):
        return t.transpose(0, 2, 1).reshape(SSM_W, 64)
    if name in ("ssm_c_re", "ssm_c_im"):
        return t.reshape(SSM_W, 64)
    return t.T if name == "ssm_d" else t


def _ssm_back(name, t):
    if name in ("ssm_b_re", "ssm_b_im"):
        return t.reshape(32, 16, 64).transpose(0, 2, 1)[None]
    if name in ("ssm_c_re", "ssm_c_im"):
        return t.reshape(1, 32, 16, 64)
    if name == "ssm_d":
        return t.T[None]
    return t if name == "ssm_log_dt" else t[None]


def adam_ssm(shares, w, m, v):
    n = len(w)

    def body(*refs):
        ins, outs = refs[:4 * n], refs[4 * n:]
        for i in range(n):
            g_ref, w_ref, m_ref, v_ref = (ins[k * n + i] for k in range(4))
            g = g_ref[0]
            for k in range(1, N_DEV):
                g = g + g_ref[k]
            outs[4 * i][...] = g
            outs[4 * i + 1][...], outs[4 * i + 2][...], outs[4 * i + 3][...] = _adam_fn(w_ref[...], g, m_ref[...],
                                                                                      v_ref[...])

    out_shape = [jax.ShapeDtypeStruct(t.shape, F32) for t in w for _ in range(4)]
    res = pl.pallas_call(body, name="adam_ssm", out_shape=out_shape, compiler_params=_params())(*shares, *w, *m, *v)
    return [res[4 * i:4 * i + 4] for i in range(n)]


def _pack_small(names, vals, rows, last=None):
    flat = [vals[n].reshape(-1) for n in names]
    if last is not None:
        flat.append(last.reshape(-1))
    flat = jnp.concatenate(flat)
    return jnp.pad(flat, (0, rows * LANES - flat.shape[0])).reshape(rows, LANES)


def _unpack_small(names, pack, shapes):
    flat, out, off = pack.reshape(-1), {}, 0
    for n in names:
        size = math.prod(shapes[n])
        out[n] = flat[off:off + size].reshape(shapes[n])
        off += size
    return out, flat[off]


def _to_slots(name, g, shard_shape):
    rows, cols = shard_shape
    if name in ROW_SHARDED:
        return g.reshape(N_CHIPS, rows, cols)
    return g.reshape(rows, N_CHIPS, cols).transpose(1, 0, 2)


def _from_slots(name, s):
    _, rows, cols = s.shape
    if name in ROW_SHARDED:
        return s.reshape(N_CHIPS * rows, cols)
    return s.transpose(1, 0, 2).reshape(rows, N_CHIPS * cols)


def kernel(x, norm_mix_g, w_in, ssm_a_re, ssm_a_im, ssm_log_dt, ssm_b_re, ssm_b_im, ssm_c_re, ssm_c_im, ssm_d, w_glu, w_attn_out, w_out, norm_ffn_g, w_ffn_gate, w_ffn_up, w_ffn_down, norm_final_g, loss_target, m_norm_mix_g, m_w_in, m_ssm_a_re, m_ssm_a_im, m_ssm_log_dt, m_ssm_b_re, m_ssm_b_im, m_ssm_c_re, m_ssm_c_im, m_ssm_d, m_w_glu, m_w_attn_out, m_w_out, m_norm_ffn_g, m_w_ffn_gate, m_w_ffn_up, m_w_ffn_down, m_norm_final_g, v_norm_mix_g, v_w_in, v_ssm_a_re, v_ssm_a_im, v_ssm_log_dt, v_ssm_b_re, v_ssm_b_im, v_ssm_c_re, v_ssm_c_im, v_ssm_d, v_w_glu, v_w_attn_out, v_w_out, v_norm_ffn_g, v_w_ffn_gate, v_w_ffn_up, v_w_ffn_down, v_norm_final_g):
    given = dict(locals())
    def local(name, prefix=""):
        t = given[prefix + name][0]
        return t.T if name in TRANSPOSED else t

    shard = {n: local(n) for n in BIG}
    shapes = {n: given[n].shape for n in WEIGHTS}

    small = {n: given[n] for n in SMALL}
    small_2d = dict(small)
    for n in ("ssm_a_re", "ssm_a_im", "ssm_b_re", "ssm_b_im", "ssm_c_re", "ssm_c_im", "ssm_d"):
        small_2d[n] = small[n][0]
    small_2d["norm_final_g"] = norm_final_g.reshape(1, D_MODEL)

    core = lax.axis_index("c").astype(jnp.int32).reshape(1)
    loss, grad_x, parts, ssm_shares, gs_norm, w_in_reduce = local_step(
        x.reshape(TOKENS, D_MODEL), loss_target.reshape(TOKENS, D_MODEL),
        {n: shard[n] for n in BIG}, small_2d, core)

    (norm_shares,) = run_comm(small_comm([_pack_small(NORM_SMALL, gs_norm, NORM_ROWS, last=loss)]),
                              "gather_norm_grads")
    small_out = [{} for _ in range(4)]
    packs = [_pack_small(NORM_SMALL, {n: given[p + n] for n in NORM_SMALL}, NORM_ROWS) for p in ("", "m_", "v_")]
    for kind, t in enumerate(adam_small("adam_norm_gains", norm_shares, *packs)):
        vals, after = _unpack_small(NORM_SMALL, t, shapes)
        small_out[kind].update(vals)
        if kind == 0:
            total_loss = after
    ssm_in = [[_ssm_2d(n, given[p + n]) for n in SSM_SMALL] for p in ("", "m_", "v_")]
    for n, res in zip(SSM_SMALL, adam_ssm(ssm_shares, *ssm_in)):
        for kind, t in enumerate(res):
            small_out[kind][n] = _ssm_back(n, t)

    big_out = {}
    for n in BIG[1:] + BIG[:1]:
        if n == "w_in":
            sems, chip_sum, land = w_in_reduce
            behind = [big_out[k][1] for k in BIG[1:]] + [small_out[1]["ssm_b_re"], small_out[1]["norm_mix_g"]]
            chip_sum, land = split_wait(sems, chip_sum, land, behind, "w_in_reduce_wait", per_chip=True)
            land = handover(land, "w_in_reduce_handover", sums=chip_sum)
            me = 2 * lax.axis_index("x") + lax.axis_index("y")
            parts[n] = lax.dynamic_update_slice(land, lax.dynamic_slice_in_dim(chip_sum, me, 1, 0),
                                                (me, lax.axis_index("c") * chip_sum.shape[1], 0))
        res = adam_big(n, parts[n], shard[n], local(n, "m_"), local(n, "v_"))
        big_out[n] = [(t.T if n in TRANSPOSED else t)[None] for t in res]

    outs = [total_loss, grad_x.reshape(LOCAL_BATCH, SEQ, D_MODEL)]
    for kind in range(4):
        for n in WEIGHTS:
            outs.append(big_out[n][kind] if n in BIG else small_out[kind][n])
    return tuple(outs)
```

```python
import functools
import math

import jax
import jax.numpy as jnp
import numpy as np
from jax import lax
from jax.experimental import pallas as pl
from jax.experimental.pallas import tpu as pltpu

F32 = jnp.float32
BF16 = jnp.bfloat16
MESH = pl.DeviceIdType.MESH

D_MODEL = 1024
SEQ = 2048
LOCAL_BATCH = 2
TOKENS = LOCAL_BATCH * SEQ
HEAD_DIM = 64
HEADS_PER_GROUP = 4
GROUP_W = HEADS_PER_GROUP * HEAD_DIM
N_GROUPS = 3
DILATIONS = (1, 4, 16)
ATTN_BLOCK = 128
ROPE_DIM = 16
ROPE_THETA = 500000.0
QKV_W = 3 * N_GROUPS * GROUP_W
SSM_W = 512
SSM_STATE_W = 2048
SSM_LANE_BLOCKS = 4
GATE_W = 2 * D_MODEL
D_FF = 2816
RMS_EPS = 1e-6
NEG_INF = -1e30
ADAM_LR, ADAM_B1, ADAM_B2, ADAM_EPS, ADAM_WD, ADAM_STEP = 0.001, 0.9, 0.999, 1e-08, 0.01, 10
N_CHIPS = 4
N_DEV = 8

VMEM_LIMIT = 56 * 1024 * 1024
LANES = 128


def _params(sem=None):
    return pltpu.CompilerParams(dimension_semantics=sem, vmem_limit_bytes=VMEM_LIMIT)


def _pick(n, cap, align=LANES):
    best = None
    for d in range(align, min(n, cap) + 1, align):
        if n % d == 0:
            best = d
    return n if best is None or n <= cap else best


_DIMS = {"nn": (((1,), (0,)), ((), ())), "nt": (((1,), (1,)), ((), ())), "tn": (((0,), (0,)), ((), ()))}


def _dot(a, b, mode):
    return lax.dot_general(a, b, _DIMS[mode], preferred_element_type=F32)


def matmul(a, b, mode, out_dtype, name, add=None, comm=None):
    if mode == "nn":
        (m, k), n = a.shape, b.shape[1]
    elif mode == "nt":
        (m, k), n = a.shape, b.shape[0]
    else:
        (k, m), n = a.shape, b.shape[1]
    tn = _pick(n, 1408 if mode != "tn" else 512)
    tk = _pick(k, 2816) if mode != "tn" else k
    tm = _pick(m, 1408)
    out_bytes = jnp.dtype(out_dtype).itemsize

    def need(tm_):
        return 2 * 2 * (tm_ * tk + tk * tn) + tm_ * tn * (4 + 2 * out_bytes + (8 if add is not None else 0))

    while need(tm) > 40 * 1024 * 1024 and tm % 256 == 0:
        tm //= 2
    nk = k // tk
    a_spec = {"nn": pl.BlockSpec((tm, tk), lambda i, j, kk: (i, kk)),
              "nt": pl.BlockSpec((tm, tk), lambda i, j, kk: (i, kk)),
              "tn": pl.BlockSpec((tk, tm), lambda i, j, kk: (kk, i))}[mode]
    b_spec = {"nn": pl.BlockSpec((tk, tn), lambda i, j, kk: (kk, j)),
              "nt": pl.BlockSpec((tn, tk), lambda i, j, kk: (j, kk)),
              "tn": pl.BlockSpec((tk, tn), lambda i, j, kk: (kk, j))}[mode]
    o_spec = pl.BlockSpec((tm, tn), lambda i, j, kk: (i, j))

    def body(a_ref, b_ref, *rest):
        if add is not None:
            add_ref, o_ref, acc_ref = rest
        else:
            o_ref, acc_ref = rest
        part = _dot(a_ref[...], b_ref[...], mode)
        if nk == 1:
            res = part if add is None else part + add_ref[...]
            o_ref[...] = res.astype(out_dtype)
            return
        kk = pl.program_id(2)

        @pl.when(kk == 0)
        def _():
            acc_ref[...] = part

        @pl.when(kk > 0)
        def _():
            acc_ref[...] += part

        @pl.when(kk == nk - 1)
        def _():
            res = acc_ref[...] if add is None else acc_ref[...] + add_ref[...]
            o_ref[...] = res.astype(out_dtype)

    in_specs = [a_spec, b_spec] + ([o_spec] if add is not None else [])
    args = (a, b) + ((add,) if add is not None else ())
    res = hosted_call(
        body, comm, name, (m // tm, n // tn, nk), in_specs, [o_spec], [jax.ShapeDtypeStruct((m, n), out_dtype)],
        [pltpu.VMEM((tm, tn) if nk > 1 else (8, LANES), F32)], args, ("parallel", "parallel", "arbitrary"))
    return res[0] if comm is None else res


def matmul_rows(a, b, name, fn, extra, outs, accs=(), add=None, comm=None, tm=512):
    (m, k), n = a.shape, b.shape[1]
    n_fixed = 2 + (add is not None)
    row_spec = lambda cols: pl.BlockSpec((tm, cols), lambda i: (i, 0))
    in_specs = [row_spec(k), pl.BlockSpec((k, n), lambda i: (0, 0))] + ([row_spec(n)] if add is not None else [])
    in_specs += [pl.BlockSpec(e.shape, lambda i: (0, 0)) if e.shape[0] == 1 else row_spec(e.shape[1]) for e in extra]
    out_specs = [row_spec(c) for c, _ in outs] + [pl.BlockSpec((1, c), lambda i: (0, 0)) for c in accs]
    out_shape = [jax.ShapeDtypeStruct((m, c), dt) for c, dt in outs] + [jax.ShapeDtypeStruct((1, c), F32) for c in accs]

    def body(*refs):
        rows = _dot(refs[0][...], refs[1][...], "nn")
        if add is not None:
            rows = rows + refs[2][...]
        n_in = n_fixed + len(extra)
        res = fn(rows, *[r[...] for r in refs[n_fixed:n_in]])
        for r, v in zip(refs[n_in:n_in + len(outs)], res[:len(outs)]):
            r[...] = v.astype(r.dtype)
        first = pl.program_id(0) == 0
        for r, v in zip(refs[n_in + len(outs):], res[len(outs):]):
            @pl.when(first)
            def _(r=r, v=v):
                r[...] = v

            @pl.when(jnp.logical_not(first))
            def _(r=r, v=v):
                r[...] += v

    args = (a, b) + ((add,) if add is not None else ()) + tuple(extra)
    return hosted_call(body, comm, name, (m // tm,), in_specs, out_specs, out_shape, [], args, ("arbitrary",))


FFN_TM, FFN_TN = 512, 1408


def ffn_in(h2, wg_t, wu_t, comm=None):
    def body(h_ref, wg_ref, wu_ref, a_ref, b_ref, act_ref):
        hv = h_ref[...]
        a, b = _dot(hv, wg_ref[...], "nt"), _dot(hv, wu_ref[...], "nt")
        a_ref[...] = a.astype(BF16)
        b_ref[...] = b.astype(BF16)
        act_ref[...] = _swiglu_fn(a, b).astype(BF16)

    rows = pl.BlockSpec((FFN_TM, D_MODEL), lambda i, j: (i, 0))
    wts = pl.BlockSpec((FFN_TN, D_MODEL), lambda i, j: (j, 0))
    out = pl.BlockSpec((FFN_TM, FFN_TN), lambda i, j: (i, j))
    return hosted_call(body, comm, "ffn_in", (TOKENS // FFN_TM, D_FF // FFN_TN), [rows, wts, wts], [out] * 3,
                       [jax.ShapeDtypeStruct((TOKENS, D_FF), BF16)] * 3, [], (h2, wg_t, wu_t),
                       ("parallel", "parallel"))


def ffn_in_bwd(dx2_b, wd, a, b):
    def body(dx_ref, wd_ref, a_ref, b_ref, da_ref, db_ref):
        dx = dx_ref[...]
        for lo in range(0, FFN_TN, 512):
            cols = slice(lo, min(lo + 512, FFN_TN))
            dact = _dot(dx, wd_ref[cols, :], "nt")
            _, vjp = jax.vjp(_swiglu_fn, a_ref[:, cols].astype(F32), b_ref[:, cols].astype(F32))
            da, db = vjp(dact)
            da_ref[:, cols] = da.astype(BF16)
            db_ref[:, cols] = db.astype(BF16)

    rows = pl.BlockSpec((FFN_TM, D_MODEL), lambda i, j: (i, 0))
    wts = pl.BlockSpec((FFN_TN, D_MODEL), lambda i, j: (j, 0))
    out = pl.BlockSpec((FFN_TM, FFN_TN), lambda i, j: (i, j))
    return pl.pallas_call(
        body, name="ffn_in_bwd", grid=(TOKENS // FFN_TM, D_FF // FFN_TN), in_specs=[rows, wts, out, out],
        out_specs=[out] * 2, out_shape=[jax.ShapeDtypeStruct((TOKENS, D_FF), BF16)] * 2,
        compiler_params=_params(("parallel", "parallel")),
    )(dx2_b, wd, a, b)


def mix_in_bwd(grads, weights, partial, x, g, skip, comm=None):
    n = len(grads)
    tm = 512

    def body(*refs):
        a_refs, b_refs = refs[:n], refs[n:2 * n]
        part_ref, x_ref, g_ref, skip_ref, gx_ref, dg_ref = refs[2 * n:]
        dh = part_ref[...]
        for a_ref, b_ref in zip(a_refs, b_refs):
            dh = dh + _dot(a_ref[...], b_ref[...], "nn")
        _, vjp = jax.vjp(_rms, x_ref[...], g_ref[...])
        dx, dg = vjp(dh)
        gx_ref[...] = dx + skip_ref[...]
        first = pl.program_id(0) == 0

        @pl.when(first)
        def _():
            dg_ref[...] = dg

        @pl.when(jnp.logical_not(first))
        def _():
            dg_ref[...] += dg

    rows = pl.BlockSpec((tm, D_MODEL), lambda i: (i, 0))
    gain = pl.BlockSpec((1, D_MODEL), lambda i: (0, 0))
    in_specs = [pl.BlockSpec((tm, a.shape[1]), lambda i: (i, 0)) for a in grads]
    in_specs += [pl.BlockSpec(b.shape, lambda i: (0, 0)) for b in weights]
    return hosted_call(
        body, comm, "mix_in_bwd", (TOKENS // tm,), in_specs + [rows, rows, gain, rows], [rows, gain],
        [jax.ShapeDtypeStruct((TOKENS, D_MODEL), F32), jax.ShapeDtypeStruct((1, D_MODEL), F32)], [],
        (*grads, *weights, partial, x, g, skip), ("arbitrary",))


def rowwise(fn, ins, outs, name, accs=(), tm=256, rows=TOKENS, comm=None):
    in_specs, args = [], []
    for item in ins:
        arr, width, blk = item if isinstance(item, tuple) else (item, None, 0)
        if arr.ndim == 3:
            for k in range(arr.shape[0]):
                in_specs.append(pl.BlockSpec((None, tm, arr.shape[2]), functools.partial(lambda i, k_: (k_, i, 0), k_=k)))
                args.append(arr)
            continue
        if arr.shape[0] == 1:
            in_specs.append(pl.BlockSpec(arr.shape, lambda i: (0, 0)))
        elif width is None:
            in_specs.append(pl.BlockSpec((tm, arr.shape[1]), lambda i: (i, 0)))
        else:
            in_specs.append(pl.BlockSpec((tm, width), functools.partial(lambda i, blk_: (i, blk_), blk_=blk)))
        args.append(arr)
    out_specs = [pl.BlockSpec((tm, c), lambda i: (i, 0)) for c, _ in outs]
    out_specs += [pl.BlockSpec((1, c), lambda i: (0, 0)) for c in accs]
    out_shape = [jax.ShapeDtypeStruct((rows, c), dt) for c, dt in outs]
    out_shape += [jax.ShapeDtypeStruct((1, c), F32) for c in accs]
    n_in, n_out = len(args), len(outs)
    c_ins, c_outs, c_sems = _comm_operands(comm)

    def body(*refs):
        refs, c_refs = _comm_refs(comm, refs, n_in, n_out + len(accs))
        step = pl.program_id(0)
        _comm_begin(comm, c_refs, step, rows // tm)
        res = fn(*[r[...] for r in refs[:n_in]])
        for r, v in zip(refs[n_in:n_in + n_out], res[:n_out]):
            r[...] = v.astype(r.dtype)
        first = step == 0
        for r, v in zip(refs[n_in + n_out:], res[n_out:]):
            @pl.when(first)
            def _(r=r, v=v):
                r[...] = v

            @pl.when(jnp.logical_not(first))
            def _(r=r, v=v):
                r[...] += v
        _comm_end(comm, c_refs, step, rows // tm)

    return pl.pallas_call(
        body, name=name, grid=(rows // tm,), in_specs=in_specs + [ANY] * len(c_ins),
        out_specs=out_specs + [ANY] * len(c_outs), out_shape=out_shape + c_outs, scratch_shapes=c_sems,
        compiler_params=_params(("arbitrary",)),
    )(*args, *c_ins)


def first_norm(x, g, others, comm=None):
    tm, n = 256, len(others)

    def body(x_ref, g_ref, *rest):
        srcs, h_ref, dsts = rest[:n], rest[n], rest[n + 1:]
        h_ref[...] = _rms(x_ref[...], g_ref[...]).astype(BF16)
        for k, (s, d) in enumerate(zip(srcs, dsts)):
            @pl.when(pl.program_id(0) == k)
            def _(s=s, d=d):
                d[...] = s[...].astype(BF16)

    rows = pl.BlockSpec((tm, D_MODEL), lambda i: (i, 0))
    whole = [pl.BlockSpec(a.shape, lambda i: (0, 0)) for a in others]
    return hosted_call(
        body, comm, "norm_mix", (TOKENS // tm,), [rows, pl.BlockSpec((1, D_MODEL), lambda i: (0, 0))] + whole,
        [rows] + whole, [jax.ShapeDtypeStruct((TOKENS, D_MODEL), BF16)]
        + [jax.ShapeDtypeStruct(a.shape, BF16) for a in others], [], (x, g, *others), ("arbitrary",))


def _rms(x, g):
    return x * lax.rsqrt(jnp.mean(x * x, axis=-1, keepdims=True) + RMS_EPS) * g


def _colsum(v):
    return jnp.sum(v, axis=0, keepdims=True)


PAIR_W = 2 * HEAD_DIM
N_PAIRS = HEADS_PER_GROUP // 2


def _qkv_order(w_t, back=False):
    dims = (N_PAIRS, N_GROUPS, 3) if back else (3, N_GROUPS, N_PAIRS)
    return w_t.reshape(dims + (PAIR_W, w_t.shape[1])).transpose(2, 1, 0, 3, 4).reshape(QKV_W, w_t.shape[1])


def _rope_tables():
    half = ROPE_DIM // 2
    inv = np.power(np.float32(ROPE_THETA), -np.arange(half, dtype=np.float32) * np.float32(2.0 / ROPE_DIM))
    ang = (np.arange(SEQ, dtype=np.float32)[:, None] * inv[None, :]).astype(np.float32)
    cos, sin = np.cos(ang), np.sin(ang)
    zeros = np.zeros((SEQ, HEAD_DIM - ROPE_DIM), np.float32)
    zh = np.zeros((SEQ, half), np.float32)
    c = np.concatenate([cos, cos, zeros + 1.0], axis=1)
    sa = np.concatenate([-sin, zh, zeros], axis=1)
    sb = np.concatenate([zh, sin, zeros], axis=1)
    return [jnp.asarray(np.tile(t, (1, 2)), F32) for t in (c, sa, sb)]


def _rope_fwd(x, c, sa, sb):
    return x * c + pltpu.roll(x, PAIR_W - 8, 1) * sa + pltpu.roll(x, 8, 1) * sb


def _rope_bwd(dy, c, sa, sb):
    return dy * c + pltpu.roll(dy * sb, PAIR_W - 8, 1) + pltpu.roll(dy * sa, 8, 1)


def _band_masks():
    row = lax.broadcasted_iota(jnp.int32, (ATTN_BLOCK, ATTN_BLOCK), 0)
    col = lax.broadcasted_iota(jnp.int32, (ATTN_BLOCK, ATTN_BLOCK), 1)
    return col <= row, col >= row


def _stack_rows(t):
    return jnp.concatenate([t, t], axis=0)


def _stack_heads(t, first_head):
    return jnp.concatenate([jnp.where(first_head, t, 0), jnp.where(first_head, 0, t)], axis=0)


def _per_head(fn):
    return jnp.concatenate([fn(slice(h * HEAD_DIM, (h + 1) * HEAD_DIM)) for h in range(2)], axis=1)


def _slab_spec(kind):
    return pl.BlockSpec((None, SEQ, PAIR_W), lambda b, p, g: (b, 0, p * 3 * N_GROUPS + g * 3 + kind))


_TABLE_SPEC = pl.BlockSpec((SEQ, PAIR_W), lambda b, p, g: (0, 0))
_PAIR_SPEC = pl.BlockSpec((None, SEQ, PAIR_W), lambda b, p, g: (b, 0, p))


def _block_rows(dil, r, n):
    return pl.ds(n * (ATTN_BLOCK * dil) + r, ATTN_BLOCK, stride=dil)


def proj_qkv(h, w_qkv_t, tables, comm=None):
    tm = 1024
    pair_w = QKV_W // N_PAIRS
    scale = HEAD_DIM ** -0.5

    def body(h_ref, w_ref, c_ref, sa_ref, sb_ref, o_ref):
        rows = _dot(h_ref[...], w_ref[...], "nt")
        c, sa, sb = c_ref[...], sa_ref[...], sb_ref[...]
        for blk in range(pair_w // PAIR_W):
            cols = slice(blk * PAIR_W, (blk + 1) * PAIR_W)
            x = rows[:, cols]
            if blk % 3 == 0:
                x = _rope_fwd(x, c, sa, sb) * scale
            elif blk % 3 == 1:
                x = _rope_fwd(x, c, sa, sb)
            o_ref[:, cols] = x

    table = pl.BlockSpec((tm, PAIR_W), lambda i, j, : (i % (SEQ // tm), 0))
    res = hosted_call(
        body, comm, "proj_qkv", (TOKENS // tm, N_PAIRS),
        [pl.BlockSpec((tm, D_MODEL), lambda i, j: (i, 0)), pl.BlockSpec((pair_w, D_MODEL), lambda i, j: (j, 0)),
         table, table, table],
        [pl.BlockSpec((tm, pair_w), lambda i, j: (i, j))], [jax.ShapeDtypeStruct((TOKENS, QKV_W), F32)], [],
        (h, w_qkv_t, *tables), ("parallel", "parallel"))
    return res[0] if comm is None else res


def attn_fwd(qkv, comm=None):
    def body(qs, ks, v_ref, attn_b_ref, attn_ref, lse_ref, o0, o1, o2, l0, l1, l2):
        g = pl.program_id(2)
        cur_mask, prev_mask = _band_masks()
        first_head = lax.broadcasted_iota(jnp.int32, (ATTN_BLOCK, PAIR_W), 1) < HEAD_DIM

        def run(dil, o_slab, l_slab):
            nb = SEQ // dil // ATTN_BLOCK

            def block(idx, carry):
                r, n = lax.div(idx, nb), lax.rem(idx, nb)
                cur, prev = _block_rows(dil, r, n), _block_rows(dil, r, jnp.maximum(n - 1, 0))
                q = qs[cur, :].astype(BF16)
                kc, kp = ks[cur, :].astype(BF16), ks[prev, :].astype(BF16)
                vc, vp = v_ref[cur, :].astype(BF16), v_ref[prev, :].astype(BF16)
                q2 = _stack_heads(q, first_head)
                mask = _stack_rows(jnp.concatenate([jnp.logical_and(prev_mask, n > 0), cur_mask], axis=1))
                s2 = jnp.where(mask, _dot(q2, jnp.concatenate([kp, kc], axis=0), "nt"), NEG_INF)
                m = jnp.max(s2, axis=-1, keepdims=True)
                vcat, two = jnp.concatenate([vp, vc], axis=0), _stack_rows(first_head)
                vext = jnp.concatenate([jnp.where(two, vcat, 1), jnp.where(two, 1, vcat)], axis=1)
                r2 = _dot(jnp.exp(s2 - m).astype(BF16), vext, "nn")
                r0, r1 = r2[:ATTN_BLOCK, :PAIR_W], r2[ATTN_BLOCK:, PAIR_W:]
                num = jnp.where(first_head, r0, r1)
                den = pltpu.roll(jnp.where(first_head, r1, r0), HEAD_DIM, 1)
                o_slab[cur, :] = num / den
                l_slab[cur, :] = jnp.where(first_head, m[:ATTN_BLOCK], m[ATTN_BLOCK:]) + jnp.log(den)
                return carry

            lax.fori_loop(0, SEQ // ATTN_BLOCK, block, 0, unroll=4)

        for gi, (o_slab, l_slab) in enumerate(((o0, l0), (o1, l1), (o2, l2))):
            @pl.when(g == gi)
            def _(gi=gi, o_slab=o_slab, l_slab=l_slab):
                run(DILATIONS[gi], o_slab, l_slab)

        @pl.when(g == N_GROUPS - 1)
        def _():
            a, b, cc = l0[...], l1[...], l2[...]
            m = jnp.maximum(jnp.maximum(a, b), cc)
            e0, e1, e2 = jnp.exp(a - m), jnp.exp(b - m), jnp.exp(cc - m)
            tot = e0 + e1 + e2
            attn = (e0 * o0[...] + e1 * o1[...] + e2 * o2[...]) / tot
            attn_ref[...] = attn
            attn_b_ref[...] = attn.astype(BF16)
            lse_ref[...] = m + jnp.log(tot)

    shape = (LOCAL_BATCH, SEQ, GROUP_W)
    slab = pltpu.VMEM((SEQ, PAIR_W), F32)
    return hosted_call(
        body, comm, "attn_fwd", (LOCAL_BATCH, N_PAIRS, N_GROUPS),
        [_slab_spec(0), _slab_spec(1), _slab_spec(2)], [_PAIR_SPEC] * 3,
        [jax.ShapeDtypeStruct(shape, BF16), jax.ShapeDtypeStruct(shape, F32), jax.ShapeDtypeStruct(shape, F32)],
        [slab] * 6, (qkv, qkv, qkv), ("parallel", "parallel", "arbitrary"))


def attn_bwd(qkv, tables, dattn, attn, lse, comm=None):
    scale = HEAD_DIM ** -0.5

    def body(qs, ks, v_ref, c_ref, sa_ref, sb_ref, do_ref, out_ref, lse_ref, dqkv_ref, dl, dq_s, dk_s, dv_s):
        g = pl.program_id(2)
        c, sa, sb = c_ref[...], sa_ref[...], sb_ref[...]

        @pl.when(g == 0)
        def _():
            prod = do_ref[...] * out_ref[...]
            dl[...] = _per_head(
                lambda sl: jnp.broadcast_to(jnp.sum(prod[:, sl], axis=-1, keepdims=True), (SEQ, HEAD_DIM)))

        cur_mask, prev_mask = _band_masks()
        first_head = lax.broadcasted_iota(jnp.int32, (ATTN_BLOCK, PAIR_W), 1) < HEAD_DIM

        def run(dil):
            nb = SEQ // dil // ATTN_BLOCK

            def block(idx, carry):
                r, n = lax.div(idx, nb), lax.rem(idx, nb)
                cur = _block_rows(dil, r, n)
                prev = _block_rows(dil, r, jnp.maximum(n - 1, 0))
                nxt = _block_rows(dil, r, jnp.minimum(n + 1, nb - 1))
                q0, q1 = qs[cur, :].astype(BF16), qs[nxt, :].astype(BF16)
                kp, kc = ks[prev, :].astype(BF16), ks[cur, :].astype(BF16)
                vp, vc = v_ref[prev, :].astype(BF16), v_ref[cur, :].astype(BF16)
                do0, do1 = do_ref[cur, :].astype(BF16), do_ref[nxt, :].astype(BF16)
                lse0, lse1, dl0, dl1 = lse_ref[cur, :], lse_ref[nxt, :], dl[cur, :], dl[nxt, :]
                has_prev = jnp.logical_and(prev_mask, n > 0)
                has_next = jnp.logical_and(prev_mask, n < nb - 1)

                def per_row(t):
                    return jnp.concatenate([t[:, 0:1], t[:, HEAD_DIM:HEAD_DIM + 1]], axis=0)

                q20, q21 = _stack_heads(q0, first_head), _stack_heads(q1, first_head)
                do20, do21 = _stack_heads(do0, first_head), _stack_heads(do1, first_head)
                kcat, vcat = jnp.concatenate([kp, kc], axis=0), jnp.concatenate([vp, vc], axis=0)
                mask0 = _stack_rows(jnp.concatenate([has_prev, cur_mask], axis=1))
                p0 = jnp.where(mask0, jnp.exp(_dot(q20, kcat, "nt") - per_row(lse0)), 0.0)
                ds0 = (p0 * (_dot(do20, vcat, "nt") - per_row(dl0))).astype(BF16)
                p1 = jnp.where(_stack_rows(has_next), jnp.exp(_dot(q21, kc, "nt") - per_row(lse1)), 0.0)
                ds1 = (p1 * (_dot(do21, vc, "nt") - per_row(dl1))).astype(BF16)
                dq2 = _dot(ds0, kcat, "nn")
                dq_s[cur, :] = jnp.where(first_head, dq2[:ATTN_BLOCK], dq2[ATTN_BLOCK:])
                ds_cur = jnp.concatenate([ds0[:, ATTN_BLOCK:], ds1], axis=0)
                p_cur = jnp.concatenate([p0[:, ATTN_BLOCK:], p1], axis=0).astype(BF16)
                dk_s[cur, :] = _dot(ds_cur, jnp.concatenate([q20, q21], axis=0), "tn")
                dv_s[cur, :] = _dot(p_cur, jnp.concatenate([do20, do21], axis=0), "tn")
                return carry

            lax.fori_loop(0, SEQ // ATTN_BLOCK, block, 0, unroll=2)

        for gi in range(N_GROUPS):
            @pl.when(g == gi)
            def _(gi=gi):
                run(DILATIONS[gi])

        dqkv_ref[:, 0:PAIR_W] = _rope_bwd(dq_s[...] * scale, c, sa, sb).astype(BF16)
        dqkv_ref[:, PAIR_W:2 * PAIR_W] = _rope_bwd(dk_s[...], c, sa, sb).astype(BF16)
        dqkv_ref[:, 2 * PAIR_W:] = dv_s[...].astype(BF16)

    slab = pltpu.VMEM((SEQ, PAIR_W), F32)
    return hosted_call(
        body, comm, "attn_bwd", (LOCAL_BATCH, N_PAIRS, N_GROUPS),
        [_slab_spec(0), _slab_spec(1), _slab_spec(2), _TABLE_SPEC, _TABLE_SPEC, _TABLE_SPEC,
         _PAIR_SPEC, _PAIR_SPEC, _PAIR_SPEC],
        [pl.BlockSpec((None, SEQ, 3 * PAIR_W), lambda b, p, g: (b, 0, p * N_GROUPS + g))],
        [jax.ShapeDtypeStruct((LOCAL_BATCH, SEQ, QKV_W), BF16)],
        [slab] * 4, (qkv, qkv, qkv, *tables, dattn, attn, lse), ("parallel", "parallel", "arbitrary"))


def _discretize(lr, li, log_dt, br, bi):
    dt = jnp.exp(log_dt)
    mag = jnp.exp(lr * dt)
    ab_re, ab_im = mag * jnp.cos(li * dt), mag * jnp.sin(li * dt)
    den = lr * lr + li * li
    nr, ni = ab_re - 1.0, ab_im
    f_re = (nr * lr + ni * li) / den
    f_im = (ni * lr - nr * li) / den
    return ab_re, ab_im, f_re[None] * br - f_im[None] * bi, f_re[None] * bi + f_im[None] * br


def ssm_prep(lr, li, log_dt, br, bi):
    def body(lr_ref, li_ref, dt_ref, br_ref, bi_ref, *outs):
        for o, v in zip(outs, _discretize(lr_ref[...], li_ref[...], dt_ref[...], br_ref[...], bi_ref[...])):
            o[...] = v
    shapes = [lr, li, br, bi]
    return pl.pallas_call(body, name="ssm_prep",
                          out_shape=[jax.ShapeDtypeStruct(s.shape, F32) for s in shapes])(lr, li, log_dt, br, bi)


def ssm_prep_bwd(lr, li, log_dt, br, bi, g_ab_re, g_ab_im, g_bb_re, g_bb_im):
    def body(lr_ref, li_ref, dt_ref, br_ref, bi_ref, g0, g1, g2, g3, *outs):
        _, vjp = jax.vjp(_discretize, lr_ref[...], li_ref[...], dt_ref[...], br_ref[...], bi_ref[...])
        for o, v in zip(outs, vjp((g0[...], g1[...], g2[...], g3[...]))):
            o[...] = v
    shapes = [lr, li, log_dt, br, bi]
    return pl.pallas_call(body, name="ssm_prep_bwd",
                          out_shape=[jax.ShapeDtypeStruct(s.shape, F32) for s in shapes])(
        lr, li, log_dt, br, bi, g_ab_re, g_ab_im, g_bb_re, g_bb_im)


def _block_diag(t):
    per = SSM_STATE_W // SSM_LANE_BLOCKS // 64
    g = t.transpose(1, 0, 2).reshape(SSM_LANE_BLOCKS, per, 16, 64)
    eye = jnp.eye(per, dtype=t.dtype)
    return jnp.einsum("jgcn,gh->jgchn", g, eye).reshape(SSM_LANE_BLOCKS, per * 16, per * 64)


def _block_diag_t(m):
    per = SSM_STATE_W // SSM_LANE_BLOCKS // 64
    m5 = m.reshape(SSM_LANE_BLOCKS, per, 16, per, 64)
    d = jnp.einsum("jgchn,gh->jgcn", m5, jnp.eye(per, dtype=m.dtype))
    return d.reshape(SSM_LANE_BLOCKS * per, 16, 64).transpose(1, 0, 2)


def _cmul(ar, ai, br, bi):
    return ar * br - ai * bi, ar * bi + ai * br


def _power_tables(ar, ai, reverse):
    width = ar.shape[1]
    row = lax.broadcasted_iota(jnp.int32, (8, width), 0)
    pows = [(ar, ai)]
    for _ in range(7):
        pows.append(_cmul(pows[-1][0], pows[-1][1], ar, ai))
    steps = []
    for k in (1, 2, 4):
        keep = (row >= k) if not reverse else (row < 8 - k)
        steps.append((jnp.where(keep, pows[k - 1][0], 0.0), jnp.where(keep, pows[k - 1][1], 0.0)))
    cr = jnp.zeros((8, width), F32)
    ci = jnp.zeros((8, width), F32)
    for i in range(8):
        pr, pi = pows[i] if not reverse else pows[7 - i]
        cr = jnp.where(row == i, pr, cr)
        ci = jnp.where(row == i, pi, ci)
    return steps, (cr, ci)


SCAN_CHUNK = 2048
STATE_BLOCK = SSM_STATE_W // SSM_LANE_BLOCKS
CHAN_BLOCK = SSM_W // SSM_LANE_BLOCKS


def ssm_fwd(u, ab_re, ab_im, bb_re, bb_im, cb_re, cb_im, d_skip, comm=None):
    nt = SEQ // SCAN_CHUNK
    chan = pl.BlockSpec((None, SCAN_CHUNK, CHAN_BLOCK), lambda b, j, t: (b, t, j))
    state = pl.BlockSpec((None, SCAN_CHUNK, STATE_BLOCK), lambda b, j, t: (b, t, j))
    mat = pl.BlockSpec((None, CHAN_BLOCK, STATE_BLOCK), lambda b, j, t: (j, 0, 0))
    lane = pl.BlockSpec((1, STATE_BLOCK), lambda b, j, t: (0, j))
    dsp = pl.BlockSpec((1, CHAN_BLOCK), lambda b, j, t: (0, j))

    def body(u_ref, ar_ref, ai_ref, bbr_ref, bbi_ref, cbr_ref, cbi_ref, d_ref, y_ref, yg_ref, xr_ref, xi_ref,
             car_r, car_i):
        @pl.when(pl.program_id(2) == 0)
        def _():
            car_r[...] = jnp.zeros_like(car_r)
            car_i[...] = jnp.zeros_like(car_i)

        steps, (pr, pi) = _power_tables(ar_ref[...], ai_ref[...], reverse=False)
        uf = u_ref[...]
        ub = uf.astype(BF16)
        xr_ref[...] = _dot(ub, bbr_ref[...], "nn")
        xi_ref[...] = _dot(ub, bbi_ref[...], "nn")

        def tile(i, carry):
            cr, ci = carry
            sl = pl.ds(pl.multiple_of(i * 8, 8), 8)
            br, bi = xr_ref[sl, :], xi_ref[sl, :]
            for k, (sr, si) in zip((1, 2, 4), steps):
                tr, ti = _cmul(sr, si, pltpu.roll(br, k, 0), pltpu.roll(bi, k, 0))
                br, bi = br + tr, bi + ti
            tr, ti = _cmul(pr, pi, cr, ci)
            br, bi = br + tr, bi + ti
            xr_ref[sl, :] = br
            xi_ref[sl, :] = bi
            return br[7:8, :], bi[7:8, :]

        cr, ci = lax.fori_loop(0, SCAN_CHUNK // 8, tile, (car_r[0:1, :], car_i[0:1, :]), unroll=4)
        car_r[0:1, :] = cr
        car_i[0:1, :] = ci
        y = (_dot(xr_ref[...].astype(BF16), cbr_ref[...], "nt") - _dot(xi_ref[...].astype(BF16), cbi_ref[...], "nt")
             + d_ref[...] * uf)
        y_ref[...] = y
        yg_ref[...] = jax.nn.gelu(y).astype(BF16)

    return hosted_call(
        body, comm, "ssm_fwd", (LOCAL_BATCH, SSM_LANE_BLOCKS, nt),
        [chan, lane, lane, mat, mat, mat, mat, dsp], [chan, chan, state, state],
        [jax.ShapeDtypeStruct((LOCAL_BATCH, SEQ, SSM_W), F32), jax.ShapeDtypeStruct((LOCAL_BATCH, SEQ, SSM_W), BF16),
         jax.ShapeDtypeStruct((LOCAL_BATCH, SEQ, SSM_STATE_W), F32),
         jax.ShapeDtypeStruct((LOCAL_BATCH, SEQ, SSM_STATE_W), F32)],
        [pltpu.VMEM((8, STATE_BLOCK), F32), pltpu.VMEM((8, STATE_BLOCK), F32)],
        (u, ab_re, ab_im, bb_re, bb_im, cb_re, cb_im, d_skip), ("parallel", "parallel", "arbitrary"))


def ssm_bwd(dyg, y, u, xr, xi, ab_re, ab_im, bb_re, bb_im, cb_re, cb_im, d_skip, comm=None):
    nt = SEQ // SCAN_CHUNK
    ntile = SCAN_CHUNK // 8

    def rev(t):
        return nt - 1 - t

    chan = pl.BlockSpec((None, SCAN_CHUNK, CHAN_BLOCK), lambda j, b, t: (b, rev(t), j))
    state = pl.BlockSpec((None, SCAN_CHUNK, STATE_BLOCK), lambda j, b, t: (b, rev(t), j))
    before = pl.BlockSpec((None, 8, STATE_BLOCK), lambda j, b, t: (b, jnp.maximum(rev(t) * ntile - 1, 0), j))
    mat = pl.BlockSpec((None, CHAN_BLOCK, STATE_BLOCK), lambda j, b, t: (j, 0, 0))
    lane = pl.BlockSpec((1, STATE_BLOCK), lambda j, b, t: (0, j))
    lane8 = pl.BlockSpec((8, STATE_BLOCK), lambda j, b, t: (0, j))
    dsp = pl.BlockSpec((1, CHAN_BLOCK), lambda j, b, t: (0, j))

    def body(dyg_ref, y_ref, u_ref, xr_ref, xi_ref, xrb_ref, xib_ref, ar_ref, ai_ref, bbr_ref, bbi_ref, cbr_ref,
             cbi_ref, d_ref, du_ref, dcbr_ref, dcbi_ref, dbbr_ref, dbbi_ref, dd_ref, dar_ref, dai_ref,
             lam_r, lam_i, car_r, car_i):
        b, t = pl.program_id(1), pl.program_id(2)
        first = jnp.logical_and(b == 0, t == 0)

        @pl.when(t == 0)
        def _():
            car_r[...] = jnp.zeros_like(car_r)
            car_i[...] = jnp.zeros_like(car_i)

        @pl.when(first)
        def _():
            for r in (dcbr_ref, dcbi_ref, dbbr_ref, dbbi_ref, dd_ref, dar_ref, dai_ref):
                r[...] = jnp.zeros_like(r)

        steps, (pr, pi) = _power_tables(ar_ref[...], -ai_ref[...], reverse=True)
        uf = u_ref[...]
        _, gelu_vjp = jax.vjp(jax.nn.gelu, y_ref[...])
        dy = gelu_vjp(dyg_ref[...])[0]
        dyb = dy.astype(BF16)
        dd_ref[...] += _colsum(dy * uf)
        lam_r[...] = _dot(dyb, cbr_ref[...], "nn")
        lam_i[...] = -_dot(dyb, cbi_ref[...], "nn")
        dcbr_ref[...] += _dot(dyb, xr_ref[...].astype(BF16), "tn")
        dcbi_ref[...] -= _dot(dyb, xi_ref[...].astype(BF16), "tn")
        row0 = lax.broadcasted_iota(jnp.int32, (8, STATE_BLOCK), 0) == 0
        has_before = rev(t) > 0
        xrb = jnp.where(has_before, xrb_ref[...], 0.0)
        xib = jnp.where(has_before, xib_ref[...], 0.0)

        def tile(s, carry):
            cr, ci, acc_r, acc_i = carry
            i = ntile - 1 - s
            sl = pl.ds(pl.multiple_of(i * 8, 8), 8)
            gr, gi = lam_r[sl, :], lam_i[sl, :]
            for k, (sr, si) in zip((1, 2, 4), steps):
                tr, ti = _cmul(sr, si, pltpu.roll(gr, 8 - k, 0), pltpu.roll(gi, 8 - k, 0))
                gr, gi = gr + tr, gi + ti
            tr, ti = _cmul(pr, pi, cr, ci)
            gr, gi = gr + tr, gi + ti
            lam_r[sl, :] = gr
            lam_i[sl, :] = gi
            sp = pl.ds(pl.multiple_of(jnp.maximum(i - 1, 0) * 8, 8), 8)
            pvr = jnp.where(i > 0, xr_ref[sp, :], xrb)
            pvi = jnp.where(i > 0, xi_ref[sp, :], xib)
            xsr = jnp.where(row0, pltpu.roll(pvr, 1, 0), pltpu.roll(xr_ref[sl, :], 1, 0))
            xsi = jnp.where(row0, pltpu.roll(pvi, 1, 0), pltpu.roll(xi_ref[sl, :], 1, 0))
            acc_r = acc_r + xsr * gr + xsi * gi
            acc_i = acc_i + xsr * gi - xsi * gr
            return gr[0:1, :], gi[0:1, :], acc_r, acc_i

        zero = jnp.zeros((8, STATE_BLOCK), F32)
        cr, ci, acc_r, acc_i = lax.fori_loop(0, ntile, tile, (car_r[0:1, :], car_i[0:1, :], zero, zero), unroll=2)
        car_r[0:1, :] = cr
        car_i[0:1, :] = ci
        dar_ref[...] += acc_r
        dai_ref[...] += acc_i
        lrb, lib = lam_r[...].astype(BF16), lam_i[...].astype(BF16)
        du = _dot(lrb, bbr_ref[...], "nt") + _dot(lib, bbi_ref[...], "nt") + d_ref[...] * dy
        du_ref[...] = du.astype(BF16)
        ub = uf.astype(BF16)
        dbbr_ref[...] += _dot(ub, lrb, "tn")
        dbbi_ref[...] += _dot(ub, lib, "tn")

    mat_shape = jax.ShapeDtypeStruct((SSM_LANE_BLOCKS, CHAN_BLOCK, STATE_BLOCK), F32)
    return hosted_call(
        body, comm, "ssm_bwd", (SSM_LANE_BLOCKS, LOCAL_BATCH, nt),
        [chan, chan, chan, state, state, before, before, lane, lane, mat, mat, mat, mat, dsp],
        [chan, mat, mat, mat, mat, dsp, lane8, lane8],
        [jax.ShapeDtypeStruct((LOCAL_BATCH, SEQ, SSM_W), BF16), mat_shape, mat_shape, mat_shape, mat_shape,
         jax.ShapeDtypeStruct((1, SSM_W), F32), jax.ShapeDtypeStruct((8, SSM_STATE_W), F32),
         jax.ShapeDtypeStruct((8, SSM_STATE_W), F32)],
        [pltpu.VMEM((SCAN_CHUNK, STATE_BLOCK), F32), pltpu.VMEM((SCAN_CHUNK, STATE_BLOCK), F32),
         pltpu.VMEM((8, STATE_BLOCK), F32), pltpu.VMEM((8, STATE_BLOCK), F32)],
        (dyg, y, u, xr, xi, xr, xi, ab_re, ab_im, bb_re, bb_im, cb_re, cb_im, d_skip),
        ("parallel", "arbitrary", "arbitrary"))


def _merge_fn(g0, g1, attn_d, za, zb):
    return jax.nn.sigmoid(g0) * attn_d + jax.nn.sigmoid(g1) * (za * jax.nn.sigmoid(zb))


def _swiglu_fn(a, b):
    return jax.nn.silu(a) * b


def _own_slot(slots, shard):
    me = 2 * lax.axis_index("x") + lax.axis_index("y")
    mine = lax.broadcasted_iota(jnp.int32, (N_CHIPS, 1, 1), 0) == me
    return jnp.where(mine, shard[None], slots)


def _reduce_start(names, gw, shard_shapes):
    return swap_comm([_to_slots(n, gw[n], shard_shapes[n]) for n in names])


def _reduce_chip(names, swap, got, core):
    return exchange_comm([add_halves(n, g, r, core) for n, g, r in zip(names, swap.ins, got)])


def local_step(x, target, shards, small, core):
    g_mix, g_ffn, g_final = small["norm_mix_g"], small["norm_ffn_g"], small["norm_final_g"]
    tables = _rope_tables()
    seqs = lambda t: t.reshape(LOCAL_BATCH, SEQ, t.shape[-1])
    toks = lambda t: t.reshape(TOKENS, t.shape[-1])
    shard_shapes = {n: s.shape for n, s in shards.items()}
    w = {}

    def gather(names):
        return gather_comm([shards[n] for n in names])

    def arrived(names, slots, own=None):
        for n, s in zip(names, slots):
            w[n] = _from_slots(n, s if own is None else _own_slot(s, own))

    later = [n for n in BIG if n != "w_in"]
    sems, w_in_shard, land, token = split_start(shards["w_in"].astype(BF16), "w_in_gather_start")
    zero = token[0, 0]
    h, *rest = first_norm(x, g_mix + zero, [shards[n] for n in later])
    shards = dict(shards)
    shards.update(zip(later, rest))
    br_t = small["ssm_b_re"].transpose(2, 0, 1)
    bi_t = small["ssm_b_im"].transpose(2, 0, 1)
    log_dt = small["ssm_log_dt"].reshape(32, 1)
    ab_re, ab_im, bb_re_t, bb_im_t = ssm_prep(small["ssm_a_re"] + zero, small["ssm_a_im"], log_dt, br_t, bi_t)
    ab = [ab_re.reshape(1, SSM_STATE_W), ab_im.reshape(1, SSM_STATE_W)]
    bb = [_block_diag(bb_re_t).astype(BF16), _block_diag(bb_im_t).astype(BF16)]
    cb = [_block_diag((small["ssm_c_re"] + zero).transpose(1, 0, 2)).astype(BF16),
          _block_diag((small["ssm_c_im"] + zero).transpose(1, 0, 2)).astype(BF16)]
    d_skip = small["ssm_d"].reshape(1, SSM_W)
    w_in_shard, land = split_wait(sems, w_in_shard, land, [h] + bb + cb, "w_in_gather_wait")
    arrived(["w_in"], [handover(land, "w_in_handover")], own=w_in_shard)
    w_qkv, w_u, w_gate = _qkv_order(w["w_in"][:QKV_W]), w["w_in"][QKV_W:QKV_W + SSM_W], w["w_in"][QKV_W + SSM_W:]
    qkv, *slots = proj_qkv(h, w_qkv, tables, comm=gather(["w_attn_out", "w_glu"]))
    arrived(["w_attn_out", "w_glu"], slots)
    qkv = seqs(qkv)
    u = seqs(matmul(h, w_u, "nt", F32, "proj_u"))
    gl, *slots = matmul(h, w_gate, "nt", BF16, "proj_gate", comm=gather(["w_out"]))
    arrived(["w_out"], slots)
    attn_b, attn, lse, *slots = attn_fwd(qkv, comm=gather(["w_ffn_gate"]))
    arrived(["w_ffn_gate"], slots)
    attn_b = toks(attn_b)
    attn_d = matmul(attn_b, w["w_attn_out"], "nn", F32, "attn_out")
    y, yg, xr, xi, *slots = ssm_fwd(u, *ab, *bb, *cb, d_skip, comm=gather(["w_ffn_up"]))
    arrived(["w_ffn_up"], slots)
    yg2 = toks(yg)
    z = matmul(yg2, w["w_glu"], "nn", BF16, "glu")
    gate_ins = [(gl, D_MODEL, 0), (gl, D_MODEL, 1), attn_d, (z, D_MODEL, 0), (z, D_MODEL, 1)]
    (merged,) = rowwise(lambda *v: (_merge_fn(*[t.astype(F32) for t in v]),), gate_ins, [(D_MODEL, BF16)], "merge")
    x1, h2 = matmul_rows(merged, w["w_out"], "out_proj", lambda rows, g: (rows, _rms(rows, g)), [g_ffn],
                         [(D_MODEL, F32), (D_MODEL, BF16)], add=x)
    a, b, act, *slots = ffn_in(h2, w["w_ffn_gate"], w["w_ffn_up"], comm=gather(["w_ffn_down"]))
    arrived(["w_ffn_down"], slots)

    def final_fn(xv, g, tgt):
        yv, vjp = jax.vjp(_rms, xv, g)
        err = yv - tgt
        dx, dg = vjp(err * (1.0 / D_MODEL))
        loss = 0.5 * jnp.sum(jnp.mean(err * err, axis=-1, keepdims=True), axis=0, keepdims=True)
        return dx, dx, dg, jnp.broadcast_to(loss, (1, LANES))

    dx2, dx2_b, dg_final, loss = matmul_rows(act, w["w_ffn_down"], "ffn_down_loss", final_fn, [g_final, target],
                                             [(D_MODEL, F32), (D_MODEL, BF16)], accs=(D_MODEL, LANES), add=x1)
    gw, parts = {}, {}
    gw["w_ffn_down"] = matmul(act, dx2_b, "tn", F32, "d_ffn_down")
    da_b, db_b = ffn_in_bwd(dx2_b, w["w_ffn_down"], a, b)
    gw["w_ffn_gate"] = matmul(da_b, h2, "tn", F32, "d_ffn_gate")
    gw["w_ffn_up"] = matmul(db_b, h2, "tn", F32, "d_ffn_up")
    ffn = ["w_ffn_down", "w_ffn_gate", "w_ffn_up"]
    swap = _reduce_start(ffn[:2], gw, shard_shapes)
    dh2, *got = matmul(da_b, w["w_ffn_gate"], "nn", F32, "d_h2_gate", comm=swap)
    ffn_exchange = [_reduce_chip(ffn[:2], swap, got, core)]
    swap = _reduce_start(ffn[2:], gw, shard_shapes)

    def norm_bwd(dh, xv, g, skip):
        _, vjp = jax.vjp(_rms, xv, g)
        dx, dg = vjp(dh)
        dx = dx + skip
        return dx, dx, dg

    dx1, dx1_b, dg_ffn, *got = matmul_rows(db_b, w["w_ffn_up"], "d_h2_up_norm", norm_bwd, [x1, g_ffn, dx2],
                                           [(D_MODEL, F32), (D_MODEL, BF16)], accs=(D_MODEL,), add=dh2, comm=swap)
    ffn_up_exchange = _reduce_chip(ffn[2:], swap, got, core)
    gw["w_out"] = matmul(merged, dx1_b, "tn", F32, "d_out")
    dmerged = matmul(dx1_b, w["w_out"], "nt", F32, "d_merged")

    def merge_bwd(g0, g1, ad, za, zb, dm):
        _, vjp = jax.vjp(_merge_fn, *[t.astype(F32) for t in (g0, g1, ad, za, zb)])
        dg0, dg1, dad, dza, dzb = vjp(dm)
        return jnp.concatenate([dg0, dg1], axis=1), dad, jnp.concatenate([dza, dzb], axis=1)

    dgl_b, dattn_d_b, dz_b, parts["w_ffn_up"] = rowwise(
        merge_bwd, gate_ins + [dmerged], [(GATE_W, BF16), (D_MODEL, BF16), (GATE_W, BF16)], "merge_bwd",
        comm=ffn_up_exchange)
    gw["w_attn_out"] = matmul(attn_b, dattn_d_b, "tn", F32, "d_attn_out")
    dattn = seqs(matmul(dattn_d_b, w["w_attn_out"], "nt", F32, "d_attn"))
    gw["w_glu"] = matmul(yg2, dz_b, "tn", F32, "d_glu")
    dyg = seqs(matmul(dz_b, w["w_glu"], "nt", F32, "d_yg"))
    mixer = ["w_out", "w_attn_out", "w_glu"]
    swap = _reduce_start(mixer, gw, shard_shapes)
    du_b, dcb_re, dcb_im, dbb_re, dbb_im, dd, da_re8, da_im8, *rest = ssm_bwd(
        dyg, y, u, xr, xi, *ab, *bb, *cb, d_skip, comm=join_comms(ffn_exchange + [swap]))
    for n, p in zip(ffn[:2], rest[:2]):
        parts[n] = p
    mixer_exchange = _reduce_chip(mixer, swap, rest[2:], core)
    du_b = toks(du_b)
    g_ab_re = jnp.sum(da_re8, axis=0).reshape(32, 64)
    g_ab_im = jnp.sum(da_im8, axis=0).reshape(32, 64)
    d_lr, d_li, d_ldt, d_br_t, d_bi_t = ssm_prep_bwd(
        small["ssm_a_re"], small["ssm_a_im"], log_dt, br_t, bi_t,
        g_ab_re, g_ab_im, _block_diag_t(dbb_re), _block_diag_t(dbb_im))
    as_gcn = lambda t: t.transpose(1, 0, 2).reshape(SSM_W, 64)
    gs = {
        "ssm_a_re": d_lr, "ssm_a_im": d_li, "ssm_log_dt": d_ldt.reshape(1, 32),
        "ssm_b_re": as_gcn(d_br_t), "ssm_b_im": as_gcn(d_bi_t),
        "ssm_c_re": as_gcn(_block_diag_t(dcb_re)), "ssm_c_im": as_gcn(_block_diag_t(dcb_im)),
        "ssm_d": dd.reshape(32, 16).T,
    }
    ssm_gather = small_comm([gs[n] for n in SSM_SMALL])
    dqkv_b, *rest = attn_bwd(qkv, tables, dattn, attn, lse, comm=join_comms([mixer_exchange, ssm_gather]))
    for n, p in zip(mixer, rest):
        parts[n] = p
    ssm_shares = rest[len(mixer):]
    dqkv_b = toks(dqkv_b)
    d_qkv = matmul(dqkv_b, h, "tn", F32, "d_w_qkv")
    d_u = matmul(du_b, h, "tn", F32, "d_w_u")
    d_gate = matmul(dgl_b, h, "tn", F32, "d_w_gate")
    gw["w_in"] = jnp.concatenate([_qkv_order(d_qkv, back=True), d_u, d_gate], axis=0)
    swap = _reduce_start(["w_in"], gw, shard_shapes)
    dh, *got = matmul(dqkv_b, w_qkv, "nn", F32, "d_h_qkv", comm=swap)
    chip_sum = add_halves("w_in", swap.ins[0], got[0], core)
    sems, chip_sum, land, token = split_start(chip_sum, "w_in_reduce_start", per_chip=True)
    grad_x, dg_mix = mix_in_bwd([du_b, dgl_b], [w_u, w_gate], dh, x, g_mix + token[0, 0], dx1)
    gs_norm = {"norm_mix_g": dg_mix, "norm_ffn_g": dg_ffn, "norm_final_g": dg_final}
    return loss, grad_x, parts, ssm_shares, gs_norm, (sems, chip_sum, land)


ANY = pl.BlockSpec(memory_space=pl.ANY)
BIG = ("w_in", "w_glu", "w_attn_out", "w_out", "w_ffn_gate", "w_ffn_up", "w_ffn_down")
TRANSPOSED = ("w_in", "w_ffn_gate", "w_ffn_up")
ROW_SHARDED = TRANSPOSED + ("w_out", "w_ffn_down")
SMALL = ("norm_mix_g", "ssm_a_re", "ssm_a_im", "ssm_log_dt", "ssm_b_re", "ssm_b_im", "ssm_c_re", "ssm_c_im",
         "ssm_d", "norm_ffn_g", "norm_final_g")
WEIGHTS = ("norm_mix_g", "w_in", "ssm_a_re", "ssm_a_im", "ssm_log_dt", "ssm_b_re", "ssm_b_im", "ssm_c_re",
           "ssm_c_im", "ssm_d", "w_glu", "w_attn_out", "w_out", "norm_ffn_g", "w_ffn_gate", "w_ffn_up",
           "w_ffn_down", "norm_final_g")
SSM_SMALL = SMALL[1:9]
NORM_SMALL = (SMALL[0],) + SMALL[9:]
NORM_ROWS = 32
N_BIG = len(BIG)


def _position():
    return lax.axis_index("x"), lax.axis_index("y"), lax.axis_index("c")


def _other_chips(x, y):
    return [(1 - x, y), (x, 1 - y), (1 - x, 1 - y)]


def _remote(src, dst, send_sem, recv_sem, device):
    return pltpu.make_async_remote_copy(src_ref=src, dst_ref=dst, send_sem=send_sem, recv_sem=recv_sem,
                                        device_id=device, device_id_type=MESH)


_later = functools.partial


def _two_level_phases(copies):
    def first(*refs):
        locals_, sends, _, _, _ = copies(*refs)
        for cp in locals_ + sends:
            cp().start()

    def mid(*refs):
        _, _, arrived, passed, _ = copies(*refs)
        for got, cp in zip(arrived, passed):
            got().wait_recv()
            cp().start()

    def last(*refs):
        locals_, sends, _, passed, from_sibling = copies(*refs)
        for cp in from_sibling:
            cp().wait_recv()
        for cp in sends + passed:
            cp().wait_send()
        for cp in locals_:
            cp().wait()

    return first, mid, last


def _half(ref, chip, which):
    rows = ref.shape[1] // 2
    return ref.at[chip, pl.ds(which * rows, rows), :]


class Comm:
    def __init__(self, ins, out_shapes, sems, first, mid, last):
        self.ins, self.out_shapes, self.sems = list(ins), list(out_shapes), list(sems)
        self.first, self.mid, self.last = first, mid, last


def join_comms(comms):
    def cut(refs_by_kind):
        offs, parts = [0, 0, 0], []
        for cm in comms:
            sizes = (len(cm.ins), len(cm.out_shapes), len(cm.sems))
            parts.append(tuple(refs_by_kind[k][offs[k]:offs[k] + sizes[k]] for k in range(3)))
            offs = [o + s for o, s in zip(offs, sizes)]
        return parts

    def phase(which):
        def run(ins, outs, sems):
            for cm, part in zip(comms, cut((ins, outs, sems))):
                fn = getattr(cm, which)
                if fn is not None:
                    fn(*part)
        return run

    return Comm(sum((cm.ins for cm in comms), []), sum((cm.out_shapes for cm in comms), []),
                sum((cm.sems for cm in comms), []), phase("first"), phase("mid"), phase("last"))


def _comm_operands(comm):
    if comm is None:
        return [], [], []
    return comm.ins, comm.out_shapes, comm.sems


def _comm_begin(comm, refs, step, n_steps):
    if comm is None:
        return
    pl.when(step == 0)(lambda: comm.first(*refs))
    if comm.mid is not None:
        pl.when(step == (n_steps * 3) // 4)(lambda: comm.mid(*refs))


def _comm_end(comm, refs, step, n_steps):
    if comm is not None:
        pl.when(step == n_steps - 1)(lambda: comm.last(*refs))


def _comm_refs(comm, refs, n_in, n_out):
    if comm is None:
        return list(refs), None
    ci, co, cs = len(comm.ins), len(comm.out_shapes), len(comm.sems)
    o0 = n_in + ci
    s0 = o0 + n_out + co
    host = list(refs[:n_in]) + list(refs[o0:o0 + n_out]) + list(refs[s0:len(refs) - cs])
    return host, (list(refs[n_in:o0]), list(refs[o0 + n_out:s0]), list(refs[len(refs) - cs:]))


def run_comm(comm, name):
    n_in, n_out = len(comm.ins), len(comm.out_shapes)

    def body(*refs):
        parts = (list(refs[:n_in]), list(refs[n_in:n_in + n_out]), list(refs[n_in + n_out:]))
        comm.first(*parts)
        if comm.mid is not None:
            comm.mid(*parts)
        comm.last(*parts)

    return pl.pallas_call(body, name=name, in_specs=[ANY] * n_in, out_specs=[ANY] * n_out,
                          out_shape=comm.out_shapes, scratch_shapes=comm.sems)(*comm.ins)


def hosted_call(work, comm, name, grid, in_specs, out_specs, out_shape, scratch_shapes, args, semantics):
    c_ins, c_outs, c_sems = _comm_operands(comm)
    n_steps = math.prod(grid)

    def body(*refs):
        host, c_refs = _comm_refs(comm, refs, len(in_specs), len(out_specs))
        step = 0
        for axis, size in enumerate(grid):
            step = step * size + pl.program_id(axis)
        _comm_begin(comm, c_refs, step, n_steps)
        work(*host)
        _comm_end(comm, c_refs, step, n_steps)

    return pl.pallas_call(
        body, name=name, grid=grid, in_specs=list(in_specs) + [ANY] * len(c_ins),
        out_specs=list(out_specs) + [ANY] * len(c_outs), out_shape=list(out_shape) + c_outs,
        scratch_shapes=list(scratch_shapes) + c_sems,
        compiler_params=_params(semantics if comm is None else ("arbitrary",) * len(grid)),
    )(*args, *c_ins)


def gather_comm(shards):
    n = len(shards)

    def copies(srcs, outs, sems):
        send_sems, recv_sems, local_sems = sems
        x, y, c = _position()
        me = 2 * x + y
        sibling = (x, y, 1 - c)
        chips = _other_chips(x, y)
        locals_ = [_later(pltpu.make_async_copy, s, o.at[me], local_sems.at[i])
                   for i, (s, o) in enumerate(zip(srcs, outs))]
        sends, arrived, passed, from_sibling = [], [], [], []
        for j, (px, py) in enumerate(chips):
            for i, (s, o) in enumerate(zip(srcs, outs)):
                rows = s.shape[0] // 2
                sends.append(_later(_remote, s.at[pl.ds(c * rows, rows), :], _half(o, me, c), send_sems.at[i, j],
                                    recv_sems.at[i, j], (px, py, c)))
                got = _half(o, 2 * px + py, c)
                arrived.append(_later(_remote, got, got, send_sems.at[i, j], recv_sems.at[i, j], (px, py, c)))
                passed.append(_later(_remote, got, got, send_sems.at[i, 3 + j], recv_sems.at[i, 3 + j], sibling))
                other = _half(o, 2 * px + py, 1 - c)
                from_sibling.append(_later(_remote, other, other, send_sems.at[i, 3 + j], recv_sems.at[i, 3 + j],
                                           sibling))
        return locals_, sends, arrived, passed, from_sibling

    return Comm(shards, [jax.ShapeDtypeStruct((N_CHIPS,) + s.shape, s.dtype) for s in shards],
                [pltpu.SemaphoreType.DMA((n, 6)), pltpu.SemaphoreType.DMA((n, 6)), pltpu.SemaphoreType.DMA((n,))],
                *_two_level_phases(copies))


HBM = pl.BlockSpec(memory_space=pltpu.HBM)
SEM = pl.BlockSpec(memory_space=pltpu.SEMAPHORE)
N_OTHER = N_CHIPS - 1


def _ici_halves(src_ref, land_ref, sems, per_chip):
    x, y, c = _position()
    me = 2 * x + y
    rows = land_ref.shape[1] // 2
    sends, arrivals = [], []
    for j, (px, py) in enumerate(_other_chips(x, y)):
        piece = src_ref.at[2 * px + py] if per_chip else src_ref.at[pl.ds(c * rows, rows), :]
        sends.append(_later(_remote, piece, _half(land_ref, me, c), sems[j], sems[N_OTHER + j], (px, py, c)))
        got = _half(land_ref, 2 * px + py, c)
        arrivals.append(_later(_remote, got, got, sems[j], sems[N_OTHER + j], (px, py, c)))
    return sends, arrivals


def split_start(src, name, per_chip=False):
    def body(src_ref, land_ref, *rest):
        sems, token = rest[:2 * N_OTHER], rest[-1]
        for cp in _ici_halves(src_ref, land_ref, sems, per_chip)[0]:
            cp().start()
        token[...] = jnp.zeros_like(token)

    rows, cols = (2 * src.shape[1], src.shape[2]) if per_chip else src.shape
    sem = pltpu.SemaphoreType.DMA(())
    land = (N_CHIPS, rows, cols)
    res = pl.pallas_call(
        body, name=name, in_specs=(HBM, HBM),
        out_specs=(SEM,) * (2 * N_OTHER) + (HBM, HBM, pl.BlockSpec(memory_space=pltpu.VMEM)),
        out_shape=(sem,) * (2 * N_OTHER) + (pltpu.HBM(src.shape, src.dtype), pltpu.HBM(land, src.dtype),
                                           jax.ShapeDtypeStruct((8, LANES), F32)),
        input_output_aliases={0: 2 * N_OTHER, 1: 2 * N_OTHER + 1},
        compiler_params=pltpu.CompilerParams(has_side_effects=pltpu.SideEffectType.DATAFLOW_SIDE_EFFECTING),
    )(pltpu.with_memory_space_constraint(src, pltpu.HBM),
      pltpu.with_memory_space_constraint(lax.empty(land, src.dtype), pltpu.HBM))
    return res[:2 * N_OTHER], res[2 * N_OTHER], res[2 * N_OTHER + 1], res[-1]


def split_wait(sems, src, land, after, name, per_chip=False):
    def body(src_ref, land_ref, *rest):
        sends, arrivals = _ici_halves(src_ref, land_ref, rest[:2 * N_OTHER], per_chip)
        for cp in sends:
            cp().wait_send()
        for cp in arrivals:
            cp().wait_recv()

    return pl.pallas_call(
        body, name=name, in_specs=(HBM, HBM) + (SEM,) * (2 * N_OTHER) + (ANY,) * len(after),
        out_specs=(HBM, HBM), out_shape=(pltpu.HBM(src.shape, src.dtype), pltpu.HBM(land.shape, land.dtype)),
        input_output_aliases={0: 0, 1: 1},
        compiler_params=pltpu.CompilerParams(has_side_effects=pltpu.SideEffectType.DATAFLOW_SIDE_EFFECTING),
    )(src, land, *sems, *after)


def handover(land, name, sums=None):
    n = N_OTHER + (sums is not None)

    def body(*refs):
        land_ref, send_sems, recv_sems = refs[0], refs[-2], refs[-1]
        x, y, c = _position()
        me = 2 * x + y
        sibling = (x, y, 1 - c)
        pieces = [(_half(land_ref, 2 * px + py, c), 2 * px + py) for px, py in _other_chips(x, y)]
        if sums is not None:
            pieces.append((refs[1].at[me], me))
        sends = [_remote(piece, _half(land_ref, chip, c), send_sems.at[j], recv_sems.at[j], sibling)
                 for j, (piece, chip) in enumerate(pieces)]
        for cp in sends:
            cp.start()
        for j, (_, chip) in enumerate(pieces):
            other = _half(land_ref, chip, 1 - c)
            _remote(other, other, send_sems.at[j], recv_sems.at[j], sibling).wait_recv()
        for cp in sends:
            cp.wait_send()

    args = (land,) + ((sums,) if sums is not None else ())
    return pl.pallas_call(
        body, name=name, in_specs=[ANY] * len(args), out_specs=ANY,
        out_shape=jax.ShapeDtypeStruct(land.shape, land.dtype), input_output_aliases={0: 0},
        scratch_shapes=[pltpu.SemaphoreType.DMA((n,)), pltpu.SemaphoreType.DMA((n,))],
    )(*args)


def swap_comm(grads):
    n = len(grads)

    def copies(srcs, gots, sems):
        send_sems, recv_sems = sems
        x, y, c = _position()
        out = []
        for i, (s, o) in enumerate(zip(srcs, gots)):
            rows = s.shape[1] // 2
            out.append(_remote(s.at[:, pl.ds((1 - c) * rows, rows), :], o, send_sems.at[i], recv_sems.at[i],
                               (x, y, 1 - c)))
        return out

    def first(srcs, gots, sems):
        for cp in copies(srcs, gots, sems):
            cp.start()

    def last(srcs, gots, sems):
        for cp in copies(srcs, gots, sems):
            cp.wait()

    return Comm(grads, [jax.ShapeDtypeStruct((N_CHIPS, g.shape[1] // 2, g.shape[2]), g.dtype) for g in grads],
                [pltpu.SemaphoreType.DMA((n,)), pltpu.SemaphoreType.DMA((n,))], first, None, last)


def add_halves(name, g, got, core):
    _, half, cols = got.shape
    mine = pl.BlockSpec((None, half, cols), lambda k, c_ref: (k, c_ref[0], 0))
    other = pl.BlockSpec((None, half, cols), lambda k, c_ref: (k, 0, 0))

    def body(c_ref, g_ref, got_ref, o_ref):
        o_ref[...] = (g_ref[...] + got_ref[...]).astype(BF16)

    return pl.pallas_call(
        body, name="add_halves_" + name,
        grid_spec=pltpu.PrefetchScalarGridSpec(num_scalar_prefetch=1, grid=(N_CHIPS,), in_specs=[mine, other],
                                               out_specs=other),
        out_shape=jax.ShapeDtypeStruct(got.shape, BF16),
        compiler_params=_params(("parallel",)),
    )(core, g, got)


def exchange_comm(parts):
    n = len(parts)

    def copies(srcs, outs, sems):
        send_sems, recv_sems, local_sems = sems
        x, y, c = _position()
        me = 2 * x + y
        sibling = (x, y, 1 - c)
        chips = _other_chips(x, y)
        locals_, sends, arrived, passed, from_sibling = [], [], [], [], []
        for i, (s, o) in enumerate(zip(srcs, outs)):
            locals_.append(_later(pltpu.make_async_copy, s.at[me], _half(o, me, c), local_sems.at[i]))
            sends.append(_later(_remote, s.at[me], _half(o, me, c), send_sems.at[i, 3], recv_sems.at[i, 3], sibling))
            other = _half(o, me, 1 - c)
            from_sibling.append(_later(_remote, other, other, send_sems.at[i, 3], recv_sems.at[i, 3], sibling))
        for j, (px, py) in enumerate(chips):
            for i, (s, o) in enumerate(zip(srcs, outs)):
                sends.append(_later(_remote, s.at[2 * px + py], _half(o, me, c), send_sems.at[i, j],
                                    recv_sems.at[i, j], (px, py, c)))
                got = _half(o, 2 * px + py, c)
                arrived.append(_later(_remote, got, got, send_sems.at[i, j], recv_sems.at[i, j], (px, py, c)))
                passed.append(_later(_remote, got, got, send_sems.at[i, 4 + j], recv_sems.at[i, 4 + j], sibling))
                other = _half(o, 2 * px + py, 1 - c)
                from_sibling.append(_later(_remote, other, other, send_sems.at[i, 4 + j], recv_sems.at[i, 4 + j],
                                           sibling))
        return locals_, sends, arrived, passed, from_sibling

    return Comm(parts, [jax.ShapeDtypeStruct((N_CHIPS, 2 * p.shape[1], p.shape[2]), p.dtype) for p in parts],
                [pltpu.SemaphoreType.DMA((n, 7)), pltpu.SemaphoreType.DMA((n, 7)), pltpu.SemaphoreType.DMA((n,))],
                *_two_level_phases(copies))


def small_comm(shares):
    n = len(shares)

    def copies(srcs, outs, sems):
        send_sems, recv_sems, local_sems = sems
        x, y, c = _position()
        me = 4 * x + 2 * y + c
        flips = [(fx, fy, fc) for fx in (0, 1) for fy in (0, 1) for fc in (0, 1)][1:]
        peers = [(1 - x if fx else x, 1 - y if fy else y, 1 - c if fc else c) for fx, fy, fc in flips]
        locals_, sends, arrived = [], [], []
        for i, (src_ref, out_ref) in enumerate(zip(srcs, outs)):
            locals_.append(_later(pltpu.make_async_copy, src_ref, out_ref.at[me], local_sems.at[i]))
            for j, (px, py, pc) in enumerate(peers):
                sends.append(_later(_remote, src_ref, out_ref.at[me], send_sems.at[i, j], recv_sems.at[i, j],
                                    (px, py, pc)))
                got = out_ref.at[4 * px + 2 * py + pc]
                arrived.append(_later(_remote, got, got, send_sems.at[i, j], recv_sems.at[i, j], (px, py, pc)))
        return locals_, sends, arrived

    def first(*refs):
        locals_, sends, _ = copies(*refs)
        for cp in locals_ + sends:
            cp().start()

    def last(*refs):
        locals_, sends, arrived = copies(*refs)
        for cp in arrived:
            cp().wait_recv()
        for cp in sends:
            cp().wait_send()
        for cp in locals_:
            cp().wait()

    return Comm(shares, [jax.ShapeDtypeStruct((N_DEV,) + s.shape, s.dtype) for s in shares],
                [pltpu.SemaphoreType.DMA((n, 7)), pltpu.SemaphoreType.DMA((n, 7)), pltpu.SemaphoreType.DMA((n,))],
                first, None, last)


def _adam_fn(w, g, m, v):
    m = ADAM_B1 * m + (1.0 - ADAM_B1) * g
    v = ADAM_B2 * v + (1.0 - ADAM_B2) * jnp.square(g)
    m_hat = m / (1.0 - ADAM_B1 ** ADAM_STEP)
    v_hat = v / (1.0 - ADAM_B2 ** ADAM_STEP)
    return -ADAM_LR * (m_hat / (jnp.sqrt(v_hat) + ADAM_EPS) + ADAM_WD * w), m, v


def adam_big(name, parts, w, m, v):
    rows, cols = w.shape
    tm = _pick(rows, 384, 16)

    def fn(p0, p1, p2, p3, wv, mv, vv):
        g = ((p0.astype(F32) + p1.astype(F32)) + p2.astype(F32)) + p3.astype(F32)
        return (g,) + _adam_fn(wv, g, mv, vv)

    return rowwise(fn, [parts, w, m, v], [(cols, F32)] * 4, "adam_" + name, tm=tm, rows=rows)


def adam_small(name, gathered, w, m, v):
    def body(g_ref, w_ref, m_ref, v_ref, go_ref, d_ref, mo_ref, vo_ref):
        g = g_ref[0]
        for k in range(1, N_DEV):
            g = g + g_ref[k]
        go_ref[...] = g
        d_ref[...], mo_ref[...], vo_ref[...] = _adam_fn(w_ref[...], g, m_ref[...], v_ref[...])

    return pl.pallas_call(body, name=name, out_shape=[jax.ShapeDtypeStruct(w.shape, F32)] * 4,
                          compiler_params=_params())(gathered, w, m, v)


def _ssm_2d(name, t):
    t = t[0] if t.ndim > 2 else t
    if name in ("ssm_b_re", "ssm_b_im"):
        return t.transpose(0, 2, 1).reshape(SSM_W, 64)
    if name in ("ssm_c_re", "ssm_c_im"):
        return t.reshape(SSM_W, 64)
    return t.T if name == "ssm_d" else t


def _ssm_back(name, t):
    if name in ("ssm_b_re", "ssm_b_im"):
        return t.reshape(32, 16, 64).transpose(0, 2, 1)[None]
    if name in ("ssm_c_re", "ssm_c_im"):
        return t.reshape(1, 32, 16, 64)
    if name == "ssm_d":
        return t.T[None]
    return t if name == "ssm_log_dt" else t[None]


def adam_ssm(shares, w, m, v):
    n = len(w)

    def body(*refs):
        ins, outs = refs[:4 * n], refs[4 * n:]
        for i in range(n):
            g_ref, w_ref, m_ref, v_ref = (ins[k * n + i] for k in range(4))
            g = g_ref[0]
            for k in range(1, N_DEV):
                g = g + g_ref[k]
            outs[4 * i][...] = g
            outs[4 * i + 1][...], outs[4 * i + 2][...], outs[4 * i + 3][...] = _adam_fn(w_ref[...], g, m_ref[...],
                                                                                      v_ref[...])

    out_shape = [jax.ShapeDtypeStruct(t.shape, F32) for t in w for _ in range(4)]
    res = pl.pallas_call(body, name="adam_ssm", out_shape=out_shape, compiler_params=_params())(*shares, *w, *m, *v)
    return [res[4 * i:4 * i + 4] for i in range(n)]


def _pack_small(names, vals, rows, last=None):
    flat = [vals[n].reshape(-1) for n in names]
    if last is not None:
        flat.append(last.reshape(-1))
    flat = jnp.concatenate(flat)
    return jnp.pad(flat, (0, rows * LANES - flat.shape[0])).reshape(rows, LANES)


def _unpack_small(names, pack, shapes):
    flat, out, off = pack.reshape(-1), {}, 0
    for n in names:
        size = math.prod(shapes[n])
        out[n] = flat[off:off + size].reshape(shapes[n])
        off += size
    return out, flat[off]


def _to_slots(name, g, shard_shape):
    rows, cols = shard_shape
    if name in ROW_SHARDED:
        return g.reshape(N_CHIPS, rows, cols)
    return g.reshape(rows, N_CHIPS, cols).transpose(1, 0, 2)


def _from_slots(name, s):
    _, rows, cols = s.shape
    if name in ROW_SHARDED:
        return s.reshape(N_CHIPS * rows, cols)
    return s.transpose(1, 0, 2).reshape(rows, N_CHIPS * cols)


def kernel(x, norm_mix_g, w_in, ssm_a_re, ssm_a_im, ssm_log_dt, ssm_b_re, ssm_b_im, ssm_c_re, ssm_c_im, ssm_d, w_glu, w_attn_out, w_out, norm_ffn_g, w_ffn_gate, w_ffn_up, w_ffn_down, norm_final_g, loss_target, m_norm_mix_g, m_w_in, m_ssm_a_re, m_ssm_a_im, m_ssm_log_dt, m_ssm_b_re, m_ssm_b_im, m_ssm_c_re, m_ssm_c_im, m_ssm_d, m_w_glu, m_w_attn_out, m_w_out, m_norm_ffn_g, m_w_ffn_gate, m_w_ffn_up, m_w_ffn_down, m_norm_final_g, v_norm_mix_g, v_w_in, v_ssm_a_re, v_ssm_a_im, v_ssm_log_dt, v_ssm_b_re, v_ssm_b_im, v_ssm_c_re, v_ssm_c_im, v_ssm_d, v_w_glu, v_w_attn_out, v_w_out, v_norm_ffn_g, v_w_ffn_gate, v_w_ffn_up, v_w_ffn_down, v_norm_final_g):
    given = dict(locals())
    def local(name, prefix=""):
        t = given[prefix + name][0]
        return t.T if name in TRANSPOSED else t

    shard = {n: local(n) for n in BIG}
    shapes = {n: given[n].shape for n in WEIGHTS}

    small = {n: given[n] for n in SMALL}
    small_2d = dict(small)
    for n in ("ssm_a_re", "ssm_a_im", "ssm_b_re", "ssm_b_im", "ssm_c_re", "ssm_c_im", "ssm_d"):
        small_2d[n] = small[n][0]
    small_2d["norm_final_g"] = norm_final_g.reshape(1, D_MODEL)

    core = lax.axis_index("c").astype(jnp.int32).reshape(1)
    loss, grad_x, parts, ssm_shares, gs_norm, w_in_reduce = local_step(
        x.reshape(TOKENS, D_MODEL), loss_target.reshape(TOKENS, D_MODEL),
        {n: shard[n] for n in BIG}, small_2d, core)

    (norm_shares,) = run_comm(small_comm([_pack_small(NORM_SMALL, gs_norm, NORM_ROWS, last=loss)]),
                              "gather_norm_grads")
    small_out = [{} for _ in range(4)]
    packs = [_pack_small(NORM_SMALL, {n: given[p + n] for n in NORM_SMALL}, NORM_ROWS) for p in ("", "m_", "v_")]
    for kind, t in enumerate(adam_small("adam_norm_gains", norm_shares, *packs)):
        vals, after = _unpack_small(NORM_SMALL, t, shapes)
        small_out[kind].update(vals)
        if kind == 0:
            total_loss = after
    ssm_in = [[_ssm_2d(n, given[p + n]) for n in SSM_SMALL] for p in ("", "m_", "v_")]
    for n, res in zip(SSM_SMALL, adam_ssm(ssm_shares, *ssm_in)):
        for kind, t in enumerate(res):
            small_out[kind][n] = _ssm_back(n, t)

    big_out, updated = {}, {}
    for n in BIG[1:] + BIG[:1]:
        if n == "w_in":
            sems, chip_sum, land = w_in_reduce
            behind = [updated[k][1] for k in BIG[1:]] + [norm_shares]
            chip_sum, land = split_wait(sems, chip_sum, land, behind, "w_in_reduce_wait", per_chip=True)
            land = handover(land, "w_in_reduce_handover", sums=chip_sum)
            me = 2 * lax.axis_index("x") + lax.axis_index("y")
            parts[n] = lax.dynamic_update_slice(land, lax.dynamic_slice_in_dim(chip_sum, me, 1, 0),
                                                (me, lax.axis_index("c") * chip_sum.shape[1], 0))
        updated[n] = adam_big(n, parts[n], shard[n], local(n, "m_"), local(n, "v_"))
        big_out[n] = [(t.T if n in TRANSPOSED else t)[None] for t in updated[n]]

    outs = [total_loss, grad_x.reshape(LOCAL_BATCH, SEQ, D_MODEL)]
    for kind in range(4):
        for n in WEIGHTS:
            outs.append(big_out[n][kind] if n in BIG else small_out[kind][n])
    return tuple(outs)
```

```python
import functools
import math

import jax
import jax.numpy as jnp
import numpy as np
from jax import lax
from jax.experimental import pallas as pl
from jax.experimental.pallas import tpu as pltpu

F32 = jnp.float32
BF16 = jnp.bfloat16
MESH = pl.DeviceIdType.MESH

D_MODEL = 1024
SEQ = 2048
LOCAL_BATCH = 2
TOKENS = LOCAL_BATCH * SEQ
HEAD_DIM = 64
HEADS_PER_GROUP = 4
GROUP_W = HEADS_PER_GROUP * HEAD_DIM
N_GROUPS = 3
DILATIONS = (1, 4, 16)
ATTN_BLOCK = 128
ROPE_DIM = 16
ROPE_THETA = 500000.0
QKV_W = 3 * N_GROUPS * GROUP_W
SSM_W = 512
SSM_STATE_W = 2048
SSM_LANE_BLOCKS = 4
GATE_W = 2 * D_MODEL
D_FF = 2816
RMS_EPS = 1e-6
NEG_INF = -1e30
ADAM_LR, ADAM_B1, ADAM_B2, ADAM_EPS, ADAM_WD, ADAM_STEP = 0.001, 0.9, 0.999, 1e-08, 0.01, 10
N_CHIPS = 4
N_DEV = 8

VMEM_LIMIT = 56 * 1024 * 1024
LANES = 128


def _params(sem=None):
    return pltpu.CompilerParams(dimension_semantics=sem, vmem_limit_bytes=VMEM_LIMIT)


def _pick(n, cap, align=LANES):
    best = None
    for d in range(align, min(n, cap) + 1, align):
        if n % d == 0:
            best = d
    return n if best is None or n <= cap else best


_DIMS = {"nn": (((1,), (0,)), ((), ())), "nt": (((1,), (1,)), ((), ())), "tn": (((0,), (0,)), ((), ()))}


def _dot(a, b, mode):
    return lax.dot_general(a, b, _DIMS[mode], preferred_element_type=F32)


def matmul(a, b, mode, out_dtype, name, add=None, comm=None):
    if mode == "nn":
        (m, k), n = a.shape, b.shape[1]
    elif mode == "nt":
        (m, k), n = a.shape, b.shape[0]
    else:
        (k, m), n = a.shape, b.shape[1]
    tn = _pick(n, 1408 if mode != "tn" else 512)
    tk = _pick(k, 2816) if mode != "tn" else k
    tm = _pick(m, 1408)
    out_bytes = jnp.dtype(out_dtype).itemsize

    def need(tm_):
        return 2 * 2 * (tm_ * tk + tk * tn) + tm_ * tn * (4 + 2 * out_bytes + (8 if add is not None else 0))

    while need(tm) > 40 * 1024 * 1024 and tm % 256 == 0:
        tm //= 2
    nk = k // tk
    a_spec = {"nn": pl.BlockSpec((tm, tk), lambda i, j, kk: (i, kk)),
              "nt": pl.BlockSpec((tm, tk), lambda i, j, kk: (i, kk)),
              "tn": pl.BlockSpec((tk, tm), lambda i, j, kk: (kk, i))}[mode]
    b_spec = {"nn": pl.BlockSpec((tk, tn), lambda i, j, kk: (kk, j)),
              "nt": pl.BlockSpec((tn, tk), lambda i, j, kk: (j, kk)),
              "tn": pl.BlockSpec((tk, tn), lambda i, j, kk: (kk, j))}[mode]
    o_spec = pl.BlockSpec((tm, tn), lambda i, j, kk: (i, j))

    def body(a_ref, b_ref, *rest):
        if add is not None:
            add_ref, o_ref, acc_ref = rest
        else:
            o_ref, acc_ref = rest
        part = _dot(a_ref[...], b_ref[...], mode)
        if nk == 1:
            res = part if add is None else part + add_ref[...]
            o_ref[...] = res.astype(out_dtype)
            return
        kk = pl.program_id(2)

        @pl.when(kk == 0)
        def _():
            acc_ref[...] = part

        @pl.when(kk > 0)
        def _():
            acc_ref[...] += part

        @pl.when(kk == nk - 1)
        def _():
            res = acc_ref[...] if add is None else acc_ref[...] + add_ref[...]
            o_ref[...] = res.astype(out_dtype)

    in_specs = [a_spec, b_spec] + ([o_spec] if add is not None else [])
    args = (a, b) + ((add,) if add is not None else ())
    res = hosted_call(
        body, comm, name, (m // tm, n // tn, nk), in_specs, [o_spec], [jax.ShapeDtypeStruct((m, n), out_dtype)],
        [pltpu.VMEM((tm, tn) if nk > 1 else (8, LANES), F32)], args, ("parallel", "parallel", "arbitrary"))
    return res[0] if comm is None else res


def matmul_rows(a, b, name, fn, extra, outs, accs=(), add=None, comm=None, tm=512):
    (m, k), n = a.shape, b.shape[1]
    n_fixed = 2 + (add is not None)
    row_spec = lambda cols: pl.BlockSpec((tm, cols), lambda i: (i, 0))
    in_specs = [row_spec(k), pl.BlockSpec((k, n), lambda i: (0, 0))] + ([row_spec(n)] if add is not None else [])
    in_specs += [pl.BlockSpec(e.shape, lambda i: (0, 0)) if e.shape[0] == 1 else row_spec(e.shape[1]) for e in extra]
    out_specs = [row_spec(c) for c, _ in outs] + [pl.BlockSpec((1, c), lambda i: (0, 0)) for c in accs]
    out_shape = [jax.ShapeDtypeStruct((m, c), dt) for c, dt in outs] + [jax.ShapeDtypeStruct((1, c), F32) for c in accs]

    def body(*refs):
        rows = _dot(refs[0][...], refs[1][...], "nn")
        if add is not None:
            rows = rows + refs[2][...]
        n_in = n_fixed + len(extra)
        res = fn(rows, *[r[...] for r in refs[n_fixed:n_in]])
        for r, v in zip(refs[n_in:n_in + len(outs)], res[:len(outs)]):
            r[...] = v.astype(r.dtype)
        first = pl.program_id(0) == 0
        for r, v in zip(refs[n_in + len(outs):], res[len(outs):]):
            @pl.when(first)
            def _(r=r, v=v):
                r[...] = v

            @pl.when(jnp.logical_not(first))
            def _(r=r, v=v):
                r[...] += v

    args = (a, b) + ((add,) if add is not None else ()) + tuple(extra)
    return hosted_call(body, comm, name, (m // tm,), in_specs, out_specs, out_shape, [], args, ("arbitrary",))


def _merge_specs(tm):
    half = lambda blk: pl.BlockSpec((tm, D_MODEL), functools.partial(lambda i, blk_: (i, blk_), blk_=blk))
    return [half(0), half(1), pl.BlockSpec((tm, D_MODEL), lambda i: (i, 0)), half(0), half(1)]


def merge_out_proj(gl, attn_d, z, w_out, x, g_ffn):
    tm = 512

    def body(g0, g1, ad, za, zb, w_ref, x_ref, g_ref, m_ref, x1_ref, h2_ref):
        merged = _merge_fn(*[t[...].astype(F32) for t in (g0, g1, ad, za, zb)]).astype(BF16)
        m_ref[...] = merged
        x1 = _dot(merged, w_ref[...], "nn") + x_ref[...]
        x1_ref[...] = x1
        h2_ref[...] = _rms(x1, g_ref[...]).astype(BF16)

    rows = pl.BlockSpec((tm, D_MODEL), lambda i: (i, 0))
    whole = pl.BlockSpec((D_MODEL, D_MODEL), lambda i: (0, 0))
    gain = pl.BlockSpec((1, D_MODEL), lambda i: (0, 0))
    tok = lambda dt: jax.ShapeDtypeStruct((TOKENS, D_MODEL), dt)
    return pl.pallas_call(
        body, name="merge_out_proj", grid=(TOKENS // tm,), in_specs=_merge_specs(tm) + [whole, rows, gain],
        out_specs=[rows] * 3, out_shape=[tok(BF16), tok(F32), tok(BF16)], compiler_params=_params(("parallel",)),
    )(gl, gl, attn_d, z, z, w_out, x, g_ffn)


def merge_bwd(dx1_b, w_out, gl, attn_d, z, comm=None):
    tm = 512

    def body(dx_ref, w_ref, g0, g1, ad, za, zb, dgl_ref, dad_ref, dz_ref):
        dm = _dot(dx_ref[...], w_ref[...], "nt")
        _, vjp = jax.vjp(_merge_fn, *[t[...].astype(F32) for t in (g0, g1, ad, za, zb)])
        dg0, dg1, dad, dza, dzb = vjp(dm)
        dgl_ref[:, :D_MODEL] = dg0.astype(BF16)
        dgl_ref[:, D_MODEL:] = dg1.astype(BF16)
        dad_ref[...] = dad.astype(BF16)
        dz_ref[:, :D_MODEL] = dza.astype(BF16)
        dz_ref[:, D_MODEL:] = dzb.astype(BF16)

    rows = pl.BlockSpec((tm, D_MODEL), lambda i: (i, 0))
    wide = pl.BlockSpec((tm, GATE_W), lambda i: (i, 0))
    whole = pl.BlockSpec((D_MODEL, D_MODEL), lambda i: (0, 0))
    return hosted_call(
        body, comm, "merge_bwd", (TOKENS // tm,), [rows, whole] + _merge_specs(tm), [wide, rows, wide],
        [jax.ShapeDtypeStruct((TOKENS, GATE_W), BF16), jax.ShapeDtypeStruct((TOKENS, D_MODEL), BF16),
         jax.ShapeDtypeStruct((TOKENS, GATE_W), BF16)], [], (dx1_b, w_out, gl, gl, attn_d, z, z), ("arbitrary",))


FFN_TM, FFN_TN = 512, 1408


def ffn_in(h2, wg_t, wu_t, comm=None):
    def body(h_ref, wg_ref, wu_ref, a_ref, b_ref, act_ref):
        hv = h_ref[...]
        a, b = _dot(hv, wg_ref[...], "nt"), _dot(hv, wu_ref[...], "nt")
        a_ref[...] = a.astype(BF16)
        b_ref[...] = b.astype(BF16)
        act_ref[...] = _swiglu_fn(a, b).astype(BF16)

    rows = pl.BlockSpec((FFN_TM, D_MODEL), lambda i, j: (i, 0))
    wts = pl.BlockSpec((FFN_TN, D_MODEL), lambda i, j: (j, 0))
    out = pl.BlockSpec((FFN_TM, FFN_TN), lambda i, j: (i, j))
    return hosted_call(body, comm, "ffn_in", (TOKENS // FFN_TM, D_FF // FFN_TN), [rows, wts, wts], [out] * 3,
                       [jax.ShapeDtypeStruct((TOKENS, D_FF), BF16)] * 3, [], (h2, wg_t, wu_t),
                       ("parallel", "parallel"))


def ffn_in_bwd(dx2_b, wd, a, b):
    def body(dx_ref, wd_ref, a_ref, b_ref, da_ref, db_ref):
        dx = dx_ref[...]
        for lo in range(0, FFN_TN, 512):
            cols = slice(lo, min(lo + 512, FFN_TN))
            dact = _dot(dx, wd_ref[cols, :], "nt")
            _, vjp = jax.vjp(_swiglu_fn, a_ref[:, cols].astype(F32), b_ref[:, cols].astype(F32))
            da, db = vjp(dact)
            da_ref[:, cols] = da.astype(BF16)
            db_ref[:, cols] = db.astype(BF16)

    rows = pl.BlockSpec((FFN_TM, D_MODEL), lambda i, j: (i, 0))
    wts = pl.BlockSpec((FFN_TN, D_MODEL), lambda i, j: (j, 0))
    out = pl.BlockSpec((FFN_TM, FFN_TN), lambda i, j: (i, j))
    return pl.pallas_call(
        body, name="ffn_in_bwd", grid=(TOKENS // FFN_TM, D_FF // FFN_TN), in_specs=[rows, wts, out, out],
        out_specs=[out] * 2, out_shape=[jax.ShapeDtypeStruct((TOKENS, D_FF), BF16)] * 2,
        compiler_params=_params(("parallel", "parallel")),
    )(dx2_b, wd, a, b)


def mix_in_bwd(grads, weights, partial, x, g, skip, comm=None):
    n = len(grads)
    tm = 512

    def body(*refs):
        a_refs, b_refs = refs[:n], refs[n:2 * n]
        part_ref, x_ref, g_ref, skip_ref, gx_ref, dg_ref = refs[2 * n:]
        dh = part_ref[...]
        for a_ref, b_ref in zip(a_refs, b_refs):
            dh = dh + _dot(a_ref[...], b_ref[...], "nn")
        _, vjp = jax.vjp(_rms, x_ref[...], g_ref[...])
        dx, dg = vjp(dh)
        gx_ref[...] = dx + skip_ref[...]
        first = pl.program_id(0) == 0

        @pl.when(first)
        def _():
            dg_ref[...] = dg

        @pl.when(jnp.logical_not(first))
        def _():
            dg_ref[...] += dg

    rows = pl.BlockSpec((tm, D_MODEL), lambda i: (i, 0))
    gain = pl.BlockSpec((1, D_MODEL), lambda i: (0, 0))
    in_specs = [pl.BlockSpec((tm, a.shape[1]), lambda i: (i, 0)) for a in grads]
    in_specs += [pl.BlockSpec(b.shape, lambda i: (0, 0)) for b in weights]
    return hosted_call(
        body, comm, "mix_in_bwd", (TOKENS // tm,), in_specs + [rows, rows, gain, rows], [rows, gain],
        [jax.ShapeDtypeStruct((TOKENS, D_MODEL), F32), jax.ShapeDtypeStruct((1, D_MODEL), F32)], [],
        (*grads, *weights, partial, x, g, skip), ("arbitrary",))


def rowwise(fn, ins, outs, name, accs=(), tm=256, rows=TOKENS, comm=None):
    in_specs, args = [], []
    for item in ins:
        arr, width, blk = item if isinstance(item, tuple) else (item, None, 0)
        if arr.ndim == 3:
            for k in range(arr.shape[0]):
                in_specs.append(pl.BlockSpec((None, tm, arr.shape[2]), functools.partial(lambda i, k_: (k_, i, 0), k_=k)))
                args.append(arr)
            continue
        if arr.shape[0] == 1:
            in_specs.append(pl.BlockSpec(arr.shape, lambda i: (0, 0)))
        elif width is None:
            in_specs.append(pl.BlockSpec((tm, arr.shape[1]), lambda i: (i, 0)))
        else:
            in_specs.append(pl.BlockSpec((tm, width), functools.partial(lambda i, blk_: (i, blk_), blk_=blk)))
        args.append(arr)
    out_specs = [pl.BlockSpec((tm, c), lambda i: (i, 0)) for c, _ in outs]
    out_specs += [pl.BlockSpec((1, c), lambda i: (0, 0)) for c in accs]
    out_shape = [jax.ShapeDtypeStruct((rows, c), dt) for c, dt in outs]
    out_shape += [jax.ShapeDtypeStruct((1, c), F32) for c in accs]
    n_in, n_out = len(args), len(outs)
    c_ins, c_outs, c_sems = _comm_operands(comm)

    def body(*refs):
        refs, c_refs = _comm_refs(comm, refs, n_in, n_out + len(accs))
        step = pl.program_id(0)
        _comm_begin(comm, c_refs, step, rows // tm)
        res = fn(*[r[...] for r in refs[:n_in]])
        for r, v in zip(refs[n_in:n_in + n_out], res[:n_out]):
            r[...] = v.astype(r.dtype)
        first = step == 0
        for r, v in zip(refs[n_in + n_out:], res[n_out:]):
            @pl.when(first)
            def _(r=r, v=v):
                r[...] = v

            @pl.when(jnp.logical_not(first))
            def _(r=r, v=v):
                r[...] += v
        _comm_end(comm, c_refs, step, rows // tm)

    return pl.pallas_call(
        body, name=name, grid=(rows // tm,), in_specs=in_specs + [ANY] * len(c_ins),
        out_specs=out_specs + [ANY] * len(c_outs), out_shape=out_shape + c_outs, scratch_shapes=c_sems,
        compiler_params=_params(("arbitrary",)),
    )(*args, *c_ins)


def first_norm(x, g, others, comm=None):
    tm, n = 256, len(others)

    def body(x_ref, g_ref, *rest):
        srcs, h_ref, dsts = rest[:n], rest[n], rest[n + 1:]
        h_ref[...] = _rms(x_ref[...], g_ref[...]).astype(BF16)
        for k, (s, d) in enumerate(zip(srcs, dsts)):
            @pl.when(pl.program_id(0) == k)
            def _(s=s, d=d):
                d[...] = s[...].astype(BF16)

    rows = pl.BlockSpec((tm, D_MODEL), lambda i: (i, 0))
    whole = [pl.BlockSpec(a.shape, lambda i: (0, 0)) for a in others]
    return hosted_call(
        body, comm, "norm_mix", (TOKENS // tm,), [rows, pl.BlockSpec((1, D_MODEL), lambda i: (0, 0))] + whole,
        [rows] + whole, [jax.ShapeDtypeStruct((TOKENS, D_MODEL), BF16)]
        + [jax.ShapeDtypeStruct(a.shape, BF16) for a in others], [], (x, g, *others), ("arbitrary",))


def _rms(x, g):
    return x * lax.rsqrt(jnp.mean(x * x, axis=-1, keepdims=True) + RMS_EPS) * g


def _colsum(v):
    return jnp.sum(v, axis=0, keepdims=True)


PAIR_W = 2 * HEAD_DIM
N_PAIRS = HEADS_PER_GROUP // 2


def _qkv_order(w_t, back=False):
    dims = (N_PAIRS, N_GROUPS, 3) if back else (3, N_GROUPS, N_PAIRS)
    return w_t.reshape(dims + (PAIR_W, w_t.shape[1])).transpose(2, 1, 0, 3, 4).reshape(QKV_W, w_t.shape[1])


def _rope_tables():
    half = ROPE_DIM // 2
    inv = np.power(np.float32(ROPE_THETA), -np.arange(half, dtype=np.float32) * np.float32(2.0 / ROPE_DIM))
    ang = (np.arange(SEQ, dtype=np.float32)[:, None] * inv[None, :]).astype(np.float32)
    cos, sin = np.cos(ang), np.sin(ang)
    zeros = np.zeros((SEQ, HEAD_DIM - ROPE_DIM), np.float32)
    zh = np.zeros((SEQ, half), np.float32)
    c = np.concatenate([cos, cos, zeros + 1.0], axis=1)
    sa = np.concatenate([-sin, zh, zeros], axis=1)
    sb = np.concatenate([zh, sin, zeros], axis=1)
    return [jnp.asarray(np.tile(t, (1, 2)), F32) for t in (c, sa, sb)]


def _rope_fwd(x, c, sa, sb):
    return x * c + pltpu.roll(x, PAIR_W - 8, 1) * sa + pltpu.roll(x, 8, 1) * sb


def _rope_bwd(dy, c, sa, sb):
    return dy * c + pltpu.roll(dy * sb, PAIR_W - 8, 1) + pltpu.roll(dy * sa, 8, 1)


def _band_masks():
    row = lax.broadcasted_iota(jnp.int32, (ATTN_BLOCK, ATTN_BLOCK), 0)
    col = lax.broadcasted_iota(jnp.int32, (ATTN_BLOCK, ATTN_BLOCK), 1)
    return col <= row, col >= row


def _stack_rows(t):
    return jnp.concatenate([t, t], axis=0)


def _stack_heads(t, first_head):
    return jnp.concatenate([jnp.where(first_head, t, 0), jnp.where(first_head, 0, t)], axis=0)


def _per_head(fn):
    return jnp.concatenate([fn(slice(h * HEAD_DIM, (h + 1) * HEAD_DIM)) for h in range(2)], axis=1)


def _slab_spec(kind):
    return pl.BlockSpec((None, SEQ, PAIR_W), lambda b, p, g: (b, 0, p * 3 * N_GROUPS + g * 3 + kind))


_TABLE_SPEC = pl.BlockSpec((SEQ, PAIR_W), lambda b, p, g: (0, 0))
_PAIR_SPEC = pl.BlockSpec((None, SEQ, PAIR_W), lambda b, p, g: (b, 0, p))


def _block_rows(dil, r, n):
    return pl.ds(n * (ATTN_BLOCK * dil) + r, ATTN_BLOCK, stride=dil)


def proj_qkv(h, w_qkv_t, tables, comm=None):
    tm = 1024
    pair_w = QKV_W // N_PAIRS
    scale = HEAD_DIM ** -0.5

    def body(h_ref, w_ref, c_ref, sa_ref, sb_ref, o_ref):
        rows = _dot(h_ref[...], w_ref[...], "nt")
        c, sa, sb = c_ref[...], sa_ref[...], sb_ref[...]
        for blk in range(pair_w // PAIR_W):
            cols = slice(blk * PAIR_W, (blk + 1) * PAIR_W)
            x = rows[:, cols]
            if blk % 3 == 0:
                x = _rope_fwd(x, c, sa, sb) * scale
            elif blk % 3 == 1:
                x = _rope_fwd(x, c, sa, sb)
            o_ref[:, cols] = x

    table = pl.BlockSpec((tm, PAIR_W), lambda i, j, : (i % (SEQ // tm), 0))
    res = hosted_call(
        body, comm, "proj_qkv", (TOKENS // tm, N_PAIRS),
        [pl.BlockSpec((tm, D_MODEL), lambda i, j: (i, 0)), pl.BlockSpec((pair_w, D_MODEL), lambda i, j: (j, 0)),
         table, table, table],
        [pl.BlockSpec((tm, pair_w), lambda i, j: (i, j))], [jax.ShapeDtypeStruct((TOKENS, QKV_W), F32)], [],
        (h, w_qkv_t, *tables), ("parallel", "parallel"))
    return res[0] if comm is None else res


def attn_fwd(qkv, comm=None):
    def body(qs, ks, v_ref, attn_b_ref, attn_ref, lse_ref, o0, o1, o2, l0, l1, l2):
        g = pl.program_id(2)
        cur_mask, prev_mask = _band_masks()
        first_head = lax.broadcasted_iota(jnp.int32, (ATTN_BLOCK, PAIR_W), 1) < HEAD_DIM

        def run(dil, o_slab, l_slab):
            nb = SEQ // dil // ATTN_BLOCK

            def block(idx, carry):
                r, n = lax.div(idx, nb), lax.rem(idx, nb)
                cur, prev = _block_rows(dil, r, n), _block_rows(dil, r, jnp.maximum(n - 1, 0))
                q = qs[cur, :].astype(BF16)
                kc, kp = ks[cur, :].astype(BF16), ks[prev, :].astype(BF16)
                vc, vp = v_ref[cur, :].astype(BF16), v_ref[prev, :].astype(BF16)
                q2 = _stack_heads(q, first_head)
                mask = _stack_rows(jnp.concatenate([jnp.logical_and(prev_mask, n > 0), cur_mask], axis=1))
                s2 = jnp.where(mask, _dot(q2, jnp.concatenate([kp, kc], axis=0), "nt"), NEG_INF)
                m = jnp.max(s2, axis=-1, keepdims=True)
                vcat, two = jnp.concatenate([vp, vc], axis=0), _stack_rows(first_head)
                vext = jnp.concatenate([jnp.where(two, vcat, 1), jnp.where(two, 1, vcat)], axis=1)
                r2 = _dot(jnp.exp(s2 - m).astype(BF16), vext, "nn")
                r0, r1 = r2[:ATTN_BLOCK, :PAIR_W], r2[ATTN_BLOCK:, PAIR_W:]
                num = jnp.where(first_head, r0, r1)
                den = pltpu.roll(jnp.where(first_head, r1, r0), HEAD_DIM, 1)
                o_slab[cur, :] = num / den
                l_slab[cur, :] = jnp.where(first_head, m[:ATTN_BLOCK], m[ATTN_BLOCK:]) + jnp.log(den)
                return carry

            lax.fori_loop(0, SEQ // ATTN_BLOCK, block, 0, unroll=4)

        for gi, (o_slab, l_slab) in enumerate(((o0, l0), (o1, l1), (o2, l2))):
            @pl.when(g == gi)
            def _(gi=gi, o_slab=o_slab, l_slab=l_slab):
                run(DILATIONS[gi], o_slab, l_slab)

        @pl.when(g == N_GROUPS - 1)
        def _():
            a, b, cc = l0[...], l1[...], l2[...]
            m = jnp.maximum(jnp.maximum(a, b), cc)
            e0, e1, e2 = jnp.exp(a - m), jnp.exp(b - m), jnp.exp(cc - m)
            tot = e0 + e1 + e2
            attn = (e0 * o0[...] + e1 * o1[...] + e2 * o2[...]) / tot
            attn_ref[...] = attn
            attn_b_ref[...] = attn.astype(BF16)
            lse_ref[...] = m + jnp.log(tot)

    shape = (LOCAL_BATCH, SEQ, GROUP_W)
    slab = pltpu.VMEM((SEQ, PAIR_W), F32)
    return hosted_call(
        body, comm, "attn_fwd", (LOCAL_BATCH, N_PAIRS, N_GROUPS),
        [_slab_spec(0), _slab_spec(1), _slab_spec(2)], [_PAIR_SPEC] * 3,
        [jax.ShapeDtypeStruct(shape, BF16), jax.ShapeDtypeStruct(shape, F32), jax.ShapeDtypeStruct(shape, F32)],
        [slab] * 6, (qkv, qkv, qkv), ("parallel", "parallel", "arbitrary"))


def attn_bwd(qkv, tables, dattn, attn, lse, comm=None):
    scale = HEAD_DIM ** -0.5

    def body(qs, ks, v_ref, c_ref, sa_ref, sb_ref, do_ref, out_ref, lse_ref, dqkv_ref, dl, dq_s, dk_s, dv_s):
        g = pl.program_id(2)
        c, sa, sb = c_ref[...], sa_ref[...], sb_ref[...]

        @pl.when(g == 0)
        def _():
            prod = do_ref[...] * out_ref[...]
            dl[...] = _per_head(
                lambda sl: jnp.broadcast_to(jnp.sum(prod[:, sl], axis=-1, keepdims=True), (SEQ, HEAD_DIM)))

        cur_mask, prev_mask = _band_masks()
        first_head = lax.broadcasted_iota(jnp.int32, (ATTN_BLOCK, PAIR_W), 1) < HEAD_DIM

        def run(dil):
            nb = SEQ // dil // ATTN_BLOCK

            def block(idx, carry):
                r, n = lax.div(idx, nb), lax.rem(idx, nb)
                cur = _block_rows(dil, r, n)
                prev = _block_rows(dil, r, jnp.maximum(n - 1, 0))
                nxt = _block_rows(dil, r, jnp.minimum(n + 1, nb - 1))
                q0, q1 = qs[cur, :].astype(BF16), qs[nxt, :].astype(BF16)
                kp, kc = ks[prev, :].astype(BF16), ks[cur, :].astype(BF16)
                vp, vc = v_ref[prev, :].astype(BF16), v_ref[cur, :].astype(BF16)
                do0, do1 = do_ref[cur, :].astype(BF16), do_ref[nxt, :].astype(BF16)
                lse0, lse1, dl0, dl1 = lse_ref[cur, :], lse_ref[nxt, :], dl[cur, :], dl[nxt, :]
                has_prev = jnp.logical_and(prev_mask, n > 0)
                has_next = jnp.logical_and(prev_mask, n < nb - 1)

                def per_row(t):
                    return jnp.concatenate([t[:, 0:1], t[:, HEAD_DIM:HEAD_DIM + 1]], axis=0)

                q20, q21 = _stack_heads(q0, first_head), _stack_heads(q1, first_head)
                do20, do21 = _stack_heads(do0, first_head), _stack_heads(do1, first_head)
                kcat, vcat = jnp.concatenate([kp, kc], axis=0), jnp.concatenate([vp, vc], axis=0)
                mask0 = _stack_rows(jnp.concatenate([has_prev, cur_mask], axis=1))
                p0 = jnp.where(mask0, jnp.exp(_dot(q20, kcat, "nt") - per_row(lse0)), 0.0)
                ds0 = (p0 * (_dot(do20, vcat, "nt") - per_row(dl0))).astype(BF16)
                p1 = jnp.where(_stack_rows(has_next), jnp.exp(_dot(q21, kc, "nt") - per_row(lse1)), 0.0)
                ds1 = (p1 * (_dot(do21, vc, "nt") - per_row(dl1))).astype(BF16)
                dq2 = _dot(ds0, kcat, "nn")
                dq_s[cur, :] = jnp.where(first_head, dq2[:ATTN_BLOCK], dq2[ATTN_BLOCK:])
                ds_cur = jnp.concatenate([ds0[:, ATTN_BLOCK:], ds1], axis=0)
                p_cur = jnp.concatenate([p0[:, ATTN_BLOCK:], p1], axis=0).astype(BF16)
                dk_s[cur, :] = _dot(ds_cur, jnp.concatenate([q20, q21], axis=0), "tn")
                dv_s[cur, :] = _dot(p_cur, jnp.concatenate([do20, do21], axis=0), "tn")
                return carry

            lax.fori_loop(0, SEQ // ATTN_BLOCK, block, 0, unroll=2)

        for gi in range(N_GROUPS):
            @pl.when(g == gi)
            def _(gi=gi):
                run(DILATIONS[gi])

        dqkv_ref[:, 0:PAIR_W] = _rope_bwd(dq_s[...] * scale, c, sa, sb).astype(BF16)
        dqkv_ref[:, PAIR_W:2 * PAIR_W] = _rope_bwd(dk_s[...], c, sa, sb).astype(BF16)
        dqkv_ref[:, 2 * PAIR_W:] = dv_s[...].astype(BF16)

    slab = pltpu.VMEM((SEQ, PAIR_W), F32)
    return hosted_call(
        body, comm, "attn_bwd", (LOCAL_BATCH, N_PAIRS, N_GROUPS),
        [_slab_spec(0), _slab_spec(1), _slab_spec(2), _TABLE_SPEC, _TABLE_SPEC, _TABLE_SPEC,
         _PAIR_SPEC, _PAIR_SPEC, _PAIR_SPEC],
        [pl.BlockSpec((None, SEQ, 3 * PAIR_W), lambda b, p, g: (b, 0, p * N_GROUPS + g))],
        [jax.ShapeDtypeStruct((LOCAL_BATCH, SEQ, QKV_W), BF16)],
        [slab] * 4, (qkv, qkv, qkv, *tables, dattn, attn, lse), ("parallel", "parallel", "arbitrary"))


def _discretize(lr, li, log_dt, br, bi):
    dt = jnp.exp(log_dt)
    mag = jnp.exp(lr * dt)
    ab_re, ab_im = mag * jnp.cos(li * dt), mag * jnp.sin(li * dt)
    den = lr * lr + li * li
    nr, ni = ab_re - 1.0, ab_im
    f_re = (nr * lr + ni * li) / den
    f_im = (ni * lr - nr * li) / den
    return ab_re, ab_im, f_re[None] * br - f_im[None] * bi, f_re[None] * bi + f_im[None] * br


def ssm_prep(lr, li, log_dt, br, bi):
    def body(lr_ref, li_ref, dt_ref, br_ref, bi_ref, *outs):
        for o, v in zip(outs, _discretize(lr_ref[...], li_ref[...], dt_ref[...], br_ref[...], bi_ref[...])):
            o[...] = v
    shapes = [lr, li, br, bi]
    return pl.pallas_call(body, name="ssm_prep",
                          out_shape=[jax.ShapeDtypeStruct(s.shape, F32) for s in shapes])(lr, li, log_dt, br, bi)


def ssm_prep_bwd(lr, li, log_dt, br, bi, g_ab_re, g_ab_im, g_bb_re, g_bb_im):
    def body(lr_ref, li_ref, dt_ref, br_ref, bi_ref, g0, g1, g2, g3, *outs):
        _, vjp = jax.vjp(_discretize, lr_ref[...], li_ref[...], dt_ref[...], br_ref[...], bi_ref[...])
        for o, v in zip(outs, vjp((g0[...], g1[...], g2[...], g3[...]))):
            o[...] = v
    shapes = [lr, li, log_dt, br, bi]
    return pl.pallas_call(body, name="ssm_prep_bwd",
                          out_shape=[jax.ShapeDtypeStruct(s.shape, F32) for s in shapes])(
        lr, li, log_dt, br, bi, g_ab_re, g_ab_im, g_bb_re, g_bb_im)


def _block_diag(t):
    per = SSM_STATE_W // SSM_LANE_BLOCKS // 64
    g = t.transpose(1, 0, 2).reshape(SSM_LANE_BLOCKS, per, 16, 64)
    eye = jnp.eye(per, dtype=t.dtype)
    return jnp.einsum("jgcn,gh->jgchn", g, eye).reshape(SSM_LANE_BLOCKS, per * 16, per * 64)


def _block_diag_t(m):
    per = SSM_STATE_W // SSM_LANE_BLOCKS // 64
    m5 = m.reshape(SSM_LANE_BLOCKS, per, 16, per, 64)
    d = jnp.einsum("jgchn,gh->jgcn", m5, jnp.eye(per, dtype=m.dtype))
    return d.reshape(SSM_LANE_BLOCKS * per, 16, 64).transpose(1, 0, 2)


def _cmul(ar, ai, br, bi):
    return ar * br - ai * bi, ar * bi + ai * br


def _power_tables(ar, ai, reverse):
    width = ar.shape[1]
    row = lax.broadcasted_iota(jnp.int32, (8, width), 0)
    pows = [(ar, ai)]
    for _ in range(7):
        pows.append(_cmul(pows[-1][0], pows[-1][1], ar, ai))
    steps = []
    for k in (1, 2, 4):
        keep = (row >= k) if not reverse else (row < 8 - k)
        steps.append((jnp.where(keep, pows[k - 1][0], 0.0), jnp.where(keep, pows[k - 1][1], 0.0)))
    cr = jnp.zeros((8, width), F32)
    ci = jnp.zeros((8, width), F32)
    for i in range(8):
        pr, pi = pows[i] if not reverse else pows[7 - i]
        cr = jnp.where(row == i, pr, cr)
        ci = jnp.where(row == i, pi, ci)
    return steps, (cr, ci)


SCAN_CHUNK = 2048
STATE_BLOCK = SSM_STATE_W // SSM_LANE_BLOCKS
CHAN_BLOCK = SSM_W // SSM_LANE_BLOCKS


def ssm_fwd(u, ab_re, ab_im, bb_re, bb_im, cb_re, cb_im, d_skip, comm=None):
    nt = SEQ // SCAN_CHUNK
    chan = pl.BlockSpec((None, SCAN_CHUNK, CHAN_BLOCK), lambda b, j, t: (b, t, j))
    state = pl.BlockSpec((None, SCAN_CHUNK, STATE_BLOCK), lambda b, j, t: (b, t, j))
    mat = pl.BlockSpec((None, CHAN_BLOCK, STATE_BLOCK), lambda b, j, t: (j, 0, 0))
    lane = pl.BlockSpec((1, STATE_BLOCK), lambda b, j, t: (0, j))
    dsp = pl.BlockSpec((1, CHAN_BLOCK), lambda b, j, t: (0, j))

    def body(u_ref, ar_ref, ai_ref, bbr_ref, bbi_ref, cbr_ref, cbi_ref, d_ref, y_ref, yg_ref, xr_ref, xi_ref,
             car_r, car_i):
        @pl.when(pl.program_id(2) == 0)
        def _():
            car_r[...] = jnp.zeros_like(car_r)
            car_i[...] = jnp.zeros_like(car_i)

        steps, (pr, pi) = _power_tables(ar_ref[...], ai_ref[...], reverse=False)
        uf = u_ref[...]
        ub = uf.astype(BF16)
        xr_ref[...] = _dot(ub, bbr_ref[...], "nn")
        xi_ref[...] = _dot(ub, bbi_ref[...], "nn")

        def tile(i, carry):
            cr, ci = carry
            sl = pl.ds(pl.multiple_of(i * 8, 8), 8)
            br, bi = xr_ref[sl, :], xi_ref[sl, :]
            for k, (sr, si) in zip((1, 2, 4), steps):
                tr, ti = _cmul(sr, si, pltpu.roll(br, k, 0), pltpu.roll(bi, k, 0))
                br, bi = br + tr, bi + ti
            tr, ti = _cmul(pr, pi, cr, ci)
            br, bi = br + tr, bi + ti
            xr_ref[sl, :] = br
            xi_ref[sl, :] = bi
            return br[7:8, :], bi[7:8, :]

        cr, ci = lax.fori_loop(0, SCAN_CHUNK // 8, tile, (car_r[0:1, :], car_i[0:1, :]), unroll=4)
        car_r[0:1, :] = cr
        car_i[0:1, :] = ci
        y = (_dot(xr_ref[...].astype(BF16), cbr_ref[...], "nt") - _dot(xi_ref[...].astype(BF16), cbi_ref[...], "nt")
             + d_ref[...] * uf)
        y_ref[...] = y
        yg_ref[...] = jax.nn.gelu(y).astype(BF16)

    return hosted_call(
        body, comm, "ssm_fwd", (LOCAL_BATCH, SSM_LANE_BLOCKS, nt),
        [chan, lane, lane, mat, mat, mat, mat, dsp], [chan, chan, state, state],
        [jax.ShapeDtypeStruct((LOCAL_BATCH, SEQ, SSM_W), F32), jax.ShapeDtypeStruct((LOCAL_BATCH, SEQ, SSM_W), BF16),
         jax.ShapeDtypeStruct((LOCAL_BATCH, SEQ, SSM_STATE_W), F32),
         jax.ShapeDtypeStruct((LOCAL_BATCH, SEQ, SSM_STATE_W), F32)],
        [pltpu.VMEM((8, STATE_BLOCK), F32), pltpu.VMEM((8, STATE_BLOCK), F32)],
        (u, ab_re, ab_im, bb_re, bb_im, cb_re, cb_im, d_skip), ("parallel", "parallel", "arbitrary"))


def ssm_bwd(dyg, y, u, xr, xi, ab_re, ab_im, bb_re, bb_im, cb_re, cb_im, d_skip, comm=None):
    nt = SEQ // SCAN_CHUNK
    ntile = SCAN_CHUNK // 8

    def rev(t):
        return nt - 1 - t

    chan = pl.BlockSpec((None, SCAN_CHUNK, CHAN_BLOCK), lambda j, b, t: (b, rev(t), j))
    state = pl.BlockSpec((None, SCAN_CHUNK, STATE_BLOCK), lambda j, b, t: (b, rev(t), j))
    before = pl.BlockSpec((None, 8, STATE_BLOCK), lambda j, b, t: (b, jnp.maximum(rev(t) * ntile - 1, 0), j))
    mat = pl.BlockSpec((None, CHAN_BLOCK, STATE_BLOCK), lambda j, b, t: (j, 0, 0))
    lane = pl.BlockSpec((1, STATE_BLOCK), lambda j, b, t: (0, j))
    lane8 = pl.BlockSpec((8, STATE_BLOCK), lambda j, b, t: (0, j))
    dsp = pl.BlockSpec((1, CHAN_BLOCK), lambda j, b, t: (0, j))

    def body(dyg_ref, y_ref, u_ref, xr_ref, xi_ref, xrb_ref, xib_ref, ar_ref, ai_ref, bbr_ref, bbi_ref, cbr_ref,
             cbi_ref, d_ref, du_ref, dcbr_ref, dcbi_ref, dbbr_ref, dbbi_ref, dd_ref, dar_ref, dai_ref,
             lam_r, lam_i, car_r, car_i):
        b, t = pl.program_id(1), pl.program_id(2)
        first = jnp.logical_and(b == 0, t == 0)

        @pl.when(t == 0)
        def _():
            car_r[...] = jnp.zeros_like(car_r)
            car_i[...] = jnp.zeros_like(car_i)

        @pl.when(first)
        def _():
            for r in (dcbr_ref, dcbi_ref, dbbr_ref, dbbi_ref, dd_ref, dar_ref, dai_ref):
                r[...] = jnp.zeros_like(r)

        steps, (pr, pi) = _power_tables(ar_ref[...], -ai_ref[...], reverse=True)
        uf = u_ref[...]
        _, gelu_vjp = jax.vjp(jax.nn.gelu, y_ref[...])
        dy = gelu_vjp(dyg_ref[...])[0]
        dyb = dy.astype(BF16)
        dd_ref[...] += _colsum(dy * uf)
        lam_r[...] = _dot(dyb, cbr_ref[...], "nn")
        lam_i[...] = -_dot(dyb, cbi_ref[...], "nn")
        dcbr_ref[...] += _dot(dyb, xr_ref[...].astype(BF16), "tn")
        dcbi_ref[...] -= _dot(dyb, xi_ref[...].astype(BF16), "tn")
        row0 = lax.broadcasted_iota(jnp.int32, (8, STATE_BLOCK), 0) == 0
        has_before = rev(t) > 0
        xrb = jnp.where(has_before, xrb_ref[...], 0.0)
        xib = jnp.where(has_before, xib_ref[...], 0.0)

        def tile(s, carry):
            cr, ci, acc_r, acc_i = carry
            i = ntile - 1 - s
            sl = pl.ds(pl.multiple_of(i * 8, 8), 8)
            gr, gi = lam_r[sl, :], lam_i[sl, :]
            for k, (sr, si) in zip((1, 2, 4), steps):
                tr, ti = _cmul(sr, si, pltpu.roll(gr, 8 - k, 0), pltpu.roll(gi, 8 - k, 0))
                gr, gi = gr + tr, gi + ti
            tr, ti = _cmul(pr, pi, cr, ci)
            gr, gi = gr + tr, gi + ti
            lam_r[sl, :] = gr
            lam_i[sl, :] = gi
            sp = pl.ds(pl.multiple_of(jnp.maximum(i - 1, 0) * 8, 8), 8)
            pvr = jnp.where(i > 0, xr_ref[sp, :], xrb)
            pvi = jnp.where(i > 0, xi_ref[sp, :], xib)
            xsr = jnp.where(row0, pltpu.roll(pvr, 1, 0), pltpu.roll(xr_ref[sl, :], 1, 0))
            xsi = jnp.where(row0, pltpu.roll(pvi, 1, 0), pltpu.roll(xi_ref[sl, :], 1, 0))
            acc_r = acc_r + xsr * gr + xsi * gi
            acc_i = acc_i + xsr * gi - xsi * gr
            return gr[0:1, :], gi[0:1, :], acc_r, acc_i

        zero = jnp.zeros((8, STATE_BLOCK), F32)
        cr, ci, acc_r, acc_i = lax.fori_loop(0, ntile, tile, (car_r[0:1, :], car_i[0:1, :], zero, zero), unroll=2)
        car_r[0:1, :] = cr
        car_i[0:1, :] = ci
        dar_ref[...] += acc_r
        dai_ref[...] += acc_i
        lrb, lib = lam_r[...].astype(BF16), lam_i[...].astype(BF16)
        du = _dot(lrb, bbr_ref[...], "nt") + _dot(lib, bbi_ref[...], "nt") + d_ref[...] * dy
        du_ref[...] = du.astype(BF16)
        ub = uf.astype(BF16)
        dbbr_ref[...] += _dot(ub, lrb, "tn")
        dbbi_ref[...] += _dot(ub, lib, "tn")

    mat_shape = jax.ShapeDtypeStruct((SSM_LANE_BLOCKS, CHAN_BLOCK, STATE_BLOCK), F32)
    return hosted_call(
        body, comm, "ssm_bwd", (SSM_LANE_BLOCKS, LOCAL_BATCH, nt),
        [chan, chan, chan, state, state, before, before, lane, lane, mat, mat, mat, mat, dsp],
        [chan, mat, mat, mat, mat, dsp, lane8, lane8],
        [jax.ShapeDtypeStruct((LOCAL_BATCH, SEQ, SSM_W), BF16), mat_shape, mat_shape, mat_shape, mat_shape,
         jax.ShapeDtypeStruct((1, SSM_W), F32), jax.ShapeDtypeStruct((8, SSM_STATE_W), F32),
         jax.ShapeDtypeStruct((8, SSM_STATE_W), F32)],
        [pltpu.VMEM((SCAN_CHUNK, STATE_BLOCK), F32), pltpu.VMEM((SCAN_CHUNK, STATE_BLOCK), F32),
         pltpu.VMEM((8, STATE_BLOCK), F32), pltpu.VMEM((8, STATE_BLOCK), F32)],
        (dyg, y, u, xr, xi, xr, xi, ab_re, ab_im, bb_re, bb_im, cb_re, cb_im, d_skip),
        ("parallel", "arbitrary", "arbitrary"))


def _merge_fn(g0, g1, attn_d, za, zb):
    return jax.nn.sigmoid(g0) * attn_d + jax.nn.sigmoid(g1) * (za * jax.nn.sigmoid(zb))


def _swiglu_fn(a, b):
    return jax.nn.silu(a) * b


def _own_slot(slots, shard):
    me = 2 * lax.axis_index("x") + lax.axis_index("y")
    mine = lax.broadcasted_iota(jnp.int32, (N_CHIPS, 1, 1), 0) == me
    return jnp.where(mine, shard[None], slots)


def _reduce_start(names, gw, shard_shapes):
    return swap_comm([_to_slots(n, gw[n], shard_shapes[n]) for n in names])


def _reduce_chip(names, swap, got, core):
    return exchange_comm([add_halves(n, g, r, core) for n, g, r in zip(names, swap.ins, got)])


def local_step(x, target, shards, small, core):
    g_mix, g_ffn, g_final = small["norm_mix_g"], small["norm_ffn_g"], small["norm_final_g"]
    tables = _rope_tables()
    seqs = lambda t: t.reshape(LOCAL_BATCH, SEQ, t.shape[-1])
    toks = lambda t: t.reshape(TOKENS, t.shape[-1])
    shard_shapes = {n: s.shape for n, s in shards.items()}
    w = {}

    def gather(names):
        return gather_comm([shards[n] for n in names])

    def arrived(names, slots, own=None):
        for n, s in zip(names, slots):
            w[n] = _from_slots(n, s if own is None else _own_slot(s, own))

    later = [n for n in BIG if n != "w_in"]
    sems, w_in_shard, land, token = split_start(shards["w_in"].astype(BF16), "w_in_gather_start")
    zero = token[0, 0]
    h, *rest = first_norm(x, g_mix + zero, [shards[n] for n in later])
    shards = dict(shards)
    shards.update(zip(later, rest))
    br_t = small["ssm_b_re"].transpose(2, 0, 1)
    bi_t = small["ssm_b_im"].transpose(2, 0, 1)
    log_dt = small["ssm_log_dt"].reshape(32, 1)
    ab_re, ab_im, bb_re_t, bb_im_t = ssm_prep(small["ssm_a_re"] + zero, small["ssm_a_im"], log_dt, br_t, bi_t)
    ab = [ab_re.reshape(1, SSM_STATE_W), ab_im.reshape(1, SSM_STATE_W)]
    bb = [_block_diag(bb_re_t).astype(BF16), _block_diag(bb_im_t).astype(BF16)]
    cb = [_block_diag((small["ssm_c_re"] + zero).transpose(1, 0, 2)).astype(BF16),
          _block_diag((small["ssm_c_im"] + zero).transpose(1, 0, 2)).astype(BF16)]
    d_skip = small["ssm_d"].reshape(1, SSM_W)
    w_in_shard, land = split_wait(sems, w_in_shard, land, [h] + bb + cb, "w_in_gather_wait")
    arrived(["w_in"], [handover(land, "w_in_handover")], own=w_in_shard)
    w_qkv, w_u, w_gate = _qkv_order(w["w_in"][:QKV_W]), w["w_in"][QKV_W:QKV_W + SSM_W], w["w_in"][QKV_W + SSM_W:]
    qkv, *slots = proj_qkv(h, w_qkv, tables, comm=gather(["w_attn_out", "w_glu"]))
    arrived(["w_attn_out", "w_glu"], slots)
    qkv = seqs(qkv)
    u = seqs(matmul(h, w_u, "nt", F32, "proj_u"))
    gl, *slots = matmul(h, w_gate, "nt", BF16, "proj_gate", comm=gather(["w_out"]))
    arrived(["w_out"], slots)
    attn_b, attn, lse, *slots = attn_fwd(qkv, comm=gather(["w_ffn_gate"]))
    arrived(["w_ffn_gate"], slots)
    attn_b = toks(attn_b)
    attn_d = matmul(attn_b, w["w_attn_out"], "nn", F32, "attn_out")
    y, yg, xr, xi, *slots = ssm_fwd(u, *ab, *bb, *cb, d_skip, comm=gather(["w_ffn_up"]))
    arrived(["w_ffn_up"], slots)
    yg2 = toks(yg)
    z = matmul(yg2, w["w_glu"], "nn", BF16, "glu")
    merged, x1, h2 = merge_out_proj(gl, attn_d, z, w["w_out"], x, g_ffn)
    a, b, act, *slots = ffn_in(h2, w["w_ffn_gate"], w["w_ffn_up"], comm=gather(["w_ffn_down"]))
    arrived(["w_ffn_down"], slots)

    def final_fn(xv, g, tgt):
        yv, vjp = jax.vjp(_rms, xv, g)
        err = yv - tgt
        dx, dg = vjp(err * (1.0 / D_MODEL))
        loss = 0.5 * jnp.sum(jnp.mean(err * err, axis=-1, keepdims=True), axis=0, keepdims=True)
        return dx, dx, dg, jnp.broadcast_to(loss, (1, LANES))

    dx2, dx2_b, dg_final, loss = matmul_rows(act, w["w_ffn_down"], "ffn_down_loss", final_fn, [g_final, target],
                                             [(D_MODEL, F32), (D_MODEL, BF16)], accs=(D_MODEL, LANES), add=x1)
    gw, parts = {}, {}
    gw["w_ffn_down"] = matmul(act, dx2_b, "tn", F32, "d_ffn_down")
    da_b, db_b = ffn_in_bwd(dx2_b, w["w_ffn_down"], a, b)
    gw["w_ffn_gate"] = matmul(da_b, h2, "tn", F32, "d_ffn_gate")
    gw["w_ffn_up"] = matmul(db_b, h2, "tn", F32, "d_ffn_up")
    ffn = ["w_ffn_down", "w_ffn_gate", "w_ffn_up"]
    swap = _reduce_start(ffn[:2], gw, shard_shapes)
    dh2, *got = matmul(da_b, w["w_ffn_gate"], "nn", F32, "d_h2_gate", comm=swap)
    ffn_exchange = [_reduce_chip(ffn[:2], swap, got, core)]
    swap = _reduce_start(ffn[2:], gw, shard_shapes)

    def norm_bwd(dh, xv, g, skip):
        _, vjp = jax.vjp(_rms, xv, g)
        dx, dg = vjp(dh)
        dx = dx + skip
        return dx, dx, dg

    dx1, dx1_b, dg_ffn, *got = matmul_rows(db_b, w["w_ffn_up"], "d_h2_up_norm", norm_bwd, [x1, g_ffn, dx2],
                                           [(D_MODEL, F32), (D_MODEL, BF16)], accs=(D_MODEL,), add=dh2, comm=swap)
    ffn_up_exchange = _reduce_chip(ffn[2:], swap, got, core)
    gw["w_out"] = matmul(merged, dx1_b, "tn", F32, "d_out")
    dgl_b, dattn_d_b, dz_b, parts["w_ffn_up"] = merge_bwd(dx1_b, w["w_out"], gl, attn_d, z, comm=ffn_up_exchange)
    gw["w_attn_out"] = matmul(attn_b, dattn_d_b, "tn", F32, "d_attn_out")
    dattn = seqs(matmul(dattn_d_b, w["w_attn_out"], "nt", F32, "d_attn"))
    gw["w_glu"] = matmul(yg2, dz_b, "tn", F32, "d_glu")
    dyg = seqs(matmul(dz_b, w["w_glu"], "nt", F32, "d_yg"))
    mixer = ["w_out", "w_attn_out", "w_glu"]
    swap = _reduce_start(mixer, gw, shard_shapes)
    du_b, dcb_re, dcb_im, dbb_re, dbb_im, dd, da_re8, da_im8, *rest = ssm_bwd(
        dyg, y, u, xr, xi, *ab, *bb, *cb, d_skip, comm=join_comms(ffn_exchange + [swap]))
    for n, p in zip(ffn[:2], rest[:2]):
        parts[n] = p
    mixer_exchange = _reduce_chip(mixer, swap, rest[2:], core)
    du_b = toks(du_b)
    g_ab_re = jnp.sum(da_re8, axis=0).reshape(32, 64)
    g_ab_im = jnp.sum(da_im8, axis=0).reshape(32, 64)
    d_lr, d_li, d_ldt, d_br_t, d_bi_t = ssm_prep_bwd(
        small["ssm_a_re"], small["ssm_a_im"], log_dt, br_t, bi_t,
        g_ab_re, g_ab_im, _block_diag_t(dbb_re), _block_diag_t(dbb_im))
    as_gcn = lambda t: t.transpose(1, 0, 2).reshape(SSM_W, 64)
    gs = {
        "ssm_a_re": d_lr, "ssm_a_im": d_li, "ssm_log_dt": d_ldt.reshape(1, 32),
        "ssm_b_re": as_gcn(d_br_t), "ssm_b_im": as_gcn(d_bi_t),
        "ssm_c_re": as_gcn(_block_diag_t(dcb_re)), "ssm_c_im": as_gcn(_block_diag_t(dcb_im)),
        "ssm_d": dd.reshape(32, 16).T,
    }
    ssm_gather = small_comm([gs[n] for n in SSM_SMALL])
    dqkv_b, *rest = attn_bwd(qkv, tables, dattn, attn, lse, comm=join_comms([mixer_exchange, ssm_gather]))
    for n, p in zip(mixer, rest):
        parts[n] = p
    ssm_shares = rest[len(mixer):]
    dqkv_b = toks(dqkv_b)
    d_qkv = matmul(dqkv_b, h, "tn", F32, "d_w_qkv")
    d_u = matmul(du_b, h, "tn", F32, "d_w_u")
    d_gate = matmul(dgl_b, h, "tn", F32, "d_w_gate")
    gw["w_in"] = jnp.concatenate([_qkv_order(d_qkv, back=True), d_u, d_gate], axis=0)
    swap = _reduce_start(["w_in"], gw, shard_shapes)
    dh, *got = matmul(dqkv_b, w_qkv, "nn", F32, "d_h_qkv", comm=swap)
    chip_sum = add_halves("w_in", swap.ins[0], got[0], core)
    sems, chip_sum, land, token = split_start(chip_sum, "w_in_reduce_start", per_chip=True)
    grad_x, dg_mix = mix_in_bwd([du_b, dgl_b], [w_u, w_gate], dh, x, g_mix + token[0, 0], dx1)
    gs_norm = {"norm_mix_g": dg_mix, "norm_ffn_g": dg_ffn, "norm_final_g": dg_final}
    return loss, grad_x, parts, ssm_shares, gs_norm, (sems, chip_sum, land)


ANY = pl.BlockSpec(memory_space=pl.ANY)
BIG = ("w_in", "w_glu", "w_attn_out", "w_out", "w_ffn_gate", "w_ffn_up", "w_ffn_down")
TRANSPOSED = ("w_in", "w_ffn_gate", "w_ffn_up")
ROW_SHARDED = TRANSPOSED + ("w_out", "w_ffn_down")
SMALL = ("norm_mix_g", "ssm_a_re", "ssm_a_im", "ssm_log_dt", "ssm_b_re", "ssm_b_im", "ssm_c_re", "ssm_c_im",
         "ssm_d", "norm_ffn_g", "norm_final_g")
WEIGHTS = ("norm_mix_g", "w_in", "ssm_a_re", "ssm_a_im", "ssm_log_dt", "ssm_b_re", "ssm_b_im", "ssm_c_re",
           "ssm_c_im", "ssm_d", "w_glu", "w_attn_out", "w_out", "norm_ffn_g", "w_ffn_gate", "w_ffn_up",
           "w_ffn_down", "norm_final_g")
SSM_SMALL = SMALL[1:9]
NORM_SMALL = (SMALL[0],) + SMALL[9:]
NORM_ROWS = 32
N_BIG = len(BIG)


def _position():
    return lax.axis_index("x"), lax.axis_index("y"), lax.axis_index("c")


def _other_chips(x, y):
    return [(1 - x, y), (x, 1 - y), (1 - x, 1 - y)]


def _remote(src, dst, send_sem, recv_sem, device):
    return pltpu.make_async_remote_copy(src_ref=src, dst_ref=dst, send_sem=send_sem, recv_sem=recv_sem,
                                        device_id=device, device_id_type=MESH)


_later = functools.partial


def _two_level_phases(copies):
    def first(*refs):
        locals_, sends, _, _, _ = copies(*refs)
        for cp in locals_ + sends:
            cp().start()

    def mid(*refs):
        _, _, arrived, passed, _ = copies(*refs)
        for got, cp in zip(arrived, passed):
            got().wait_recv()
            cp().start()

    def last(*refs):
        locals_, sends, _, passed, from_sibling = copies(*refs)
        for cp in from_sibling:
            cp().wait_recv()
        for cp in sends + passed:
            cp().wait_send()
        for cp in locals_:
            cp().wait()

    return first, mid, last


def _half(ref, chip, which):
    rows = ref.shape[1] // 2
    return ref.at[chip, pl.ds(which * rows, rows), :]


class Comm:
    def __init__(self, ins, out_shapes, sems, first, mid, last):
        self.ins, self.out_shapes, self.sems = list(ins), list(out_shapes), list(sems)
        self.first, self.mid, self.last = first, mid, last


def join_comms(comms):
    def cut(refs_by_kind):
        offs, parts = [0, 0, 0], []
        for cm in comms:
            sizes = (len(cm.ins), len(cm.out_shapes), len(cm.sems))
            parts.append(tuple(refs_by_kind[k][offs[k]:offs[k] + sizes[k]] for k in range(3)))
            offs = [o + s for o, s in zip(offs, sizes)]
        return parts

    def phase(which):
        def run(ins, outs, sems):
            for cm, part in zip(comms, cut((ins, outs, sems))):
                fn = getattr(cm, which)
                if fn is not None:
                    fn(*part)
        return run

    return Comm(sum((cm.ins for cm in comms), []), sum((cm.out_shapes for cm in comms), []),
                sum((cm.sems for cm in comms), []), phase("first"), phase("mid"), phase("last"))


def _comm_operands(comm):
    if comm is None:
        return [], [], []
    return comm.ins, comm.out_shapes, comm.sems


def _comm_begin(comm, refs, step, n_steps):
    if comm is None:
        return
    pl.when(step == 0)(lambda: comm.first(*refs))
    if comm.mid is not None:
        pl.when(step == (n_steps * 3) // 4)(lambda: comm.mid(*refs))


def _comm_end(comm, refs, step, n_steps):
    if comm is not None:
        pl.when(step == n_steps - 1)(lambda: comm.last(*refs))


def _comm_refs(comm, refs, n_in, n_out):
    if comm is None:
        return list(refs), None
    ci, co, cs = len(comm.ins), len(comm.out_shapes), len(comm.sems)
    o0 = n_in + ci
    s0 = o0 + n_out + co
    host = list(refs[:n_in]) + list(refs[o0:o0 + n_out]) + list(refs[s0:len(refs) - cs])
    return host, (list(refs[n_in:o0]), list(refs[o0 + n_out:s0]), list(refs[len(refs) - cs:]))


def run_comm(comm, name):
    n_in, n_out = len(comm.ins), len(comm.out_shapes)

    def body(*refs):
        parts = (list(refs[:n_in]), list(refs[n_in:n_in + n_out]), list(refs[n_in + n_out:]))
        comm.first(*parts)
        if comm.mid is not None:
            comm.mid(*parts)
        comm.last(*parts)

    return pl.pallas_call(body, name=name, in_specs=[ANY] * n_in, out_specs=[ANY] * n_out,
                          out_shape=comm.out_shapes, scratch_shapes=comm.sems)(*comm.ins)


def hosted_call(work, comm, name, grid, in_specs, out_specs, out_shape, scratch_shapes, args, semantics):
    c_ins, c_outs, c_sems = _comm_operands(comm)
    n_steps = math.prod(grid)

    def body(*refs):
        host, c_refs = _comm_refs(comm, refs, len(in_specs), len(out_specs))
        step = 0
        for axis, size in enumerate(grid):
            step = step * size + pl.program_id(axis)
        _comm_begin(comm, c_refs, step, n_steps)
        work(*host)
        _comm_end(comm, c_refs, step, n_steps)

    return pl.pallas_call(
        body, name=name, grid=grid, in_specs=list(in_specs) + [ANY] * len(c_ins),
        out_specs=list(out_specs) + [ANY] * len(c_outs), out_shape=list(out_shape) + c_outs,
        scratch_shapes=list(scratch_shapes) + c_sems,
        compiler_params=_params(semantics if comm is None else ("arbitrary",) * len(grid)),
    )(*args, *c_ins)


def gather_comm(shards):
    n = len(shards)

    def copies(srcs, outs, sems):
        send_sems, recv_sems, local_sems = sems
        x, y, c = _position()
        me = 2 * x + y
        sibling = (x, y, 1 - c)
        chips = _other_chips(x, y)
        locals_ = [_later(pltpu.make_async_copy, s, o.at[me], local_sems.at[i])
                   for i, (s, o) in enumerate(zip(srcs, outs))]
        sends, arrived, passed, from_sibling = [], [], [], []
        for j, (px, py) in enumerate(chips):
            for i, (s, o) in enumerate(zip(srcs, outs)):
                rows = s.shape[0] // 2
                sends.append(_later(_remote, s.at[pl.ds(c * rows, rows), :], _half(o, me, c), send_sems.at[i, j],
                                    recv_sems.at[i, j], (px, py, c)))
                got = _half(o, 2 * px + py, c)
                arrived.append(_later(_remote, got, got, send_sems.at[i, j], recv_sems.at[i, j], (px, py, c)))
                passed.append(_later(_remote, got, got, send_sems.at[i, 3 + j], recv_sems.at[i, 3 + j], sibling))
                other = _half(o, 2 * px + py, 1 - c)
                from_sibling.append(_later(_remote, other, other, send_sems.at[i, 3 + j], recv_sems.at[i, 3 + j],
                                           sibling))
        return locals_, sends, arrived, passed, from_sibling

    return Comm(shards, [jax.ShapeDtypeStruct((N_CHIPS,) + s.shape, s.dtype) for s in shards],
                [pltpu.SemaphoreType.DMA((n, 6)), pltpu.SemaphoreType.DMA((n, 6)), pltpu.SemaphoreType.DMA((n,))],
                *_two_level_phases(copies))


HBM = pl.BlockSpec(memory_space=pltpu.HBM)
SEM = pl.BlockSpec(memory_space=pltpu.SEMAPHORE)
N_OTHER = N_CHIPS - 1


def _ici_halves(src_ref, land_ref, sems, per_chip):
    x, y, c = _position()
    me = 2 * x + y
    rows = land_ref.shape[1] // 2
    sends, arrivals = [], []
    for j, (px, py) in enumerate(_other_chips(x, y)):
        piece = src_ref.at[2 * px + py] if per_chip else src_ref.at[pl.ds(c * rows, rows), :]
        sends.append(_later(_remote, piece, _half(land_ref, me, c), sems[j], sems[N_OTHER + j], (px, py, c)))
        got = _half(land_ref, 2 * px + py, c)
        arrivals.append(_later(_remote, got, got, sems[j], sems[N_OTHER + j], (px, py, c)))
    return sends, arrivals


def split_start(src, name, per_chip=False):
    def body(src_ref, land_ref, *rest):
        sems, token = rest[:2 * N_OTHER], rest[-1]
        for cp in _ici_halves(src_ref, land_ref, sems, per_chip)[0]:
            cp().start()
        token[...] = jnp.zeros_like(token)

    rows, cols = (2 * src.shape[1], src.shape[2]) if per_chip else src.shape
    sem = pltpu.SemaphoreType.DMA(())
    land = (N_CHIPS, rows, cols)
    res = pl.pallas_call(
        body, name=name, in_specs=(HBM, HBM),
        out_specs=(SEM,) * (2 * N_OTHER) + (HBM, HBM, pl.BlockSpec(memory_space=pltpu.VMEM)),
        out_shape=(sem,) * (2 * N_OTHER) + (pltpu.HBM(src.shape, src.dtype), pltpu.HBM(land, src.dtype),
                                           jax.ShapeDtypeStruct((8, LANES), F32)),
        input_output_aliases={0: 2 * N_OTHER, 1: 2 * N_OTHER + 1},
        compiler_params=pltpu.CompilerParams(has_side_effects=pltpu.SideEffectType.DATAFLOW_SIDE_EFFECTING),
    )(pltpu.with_memory_space_constraint(src, pltpu.HBM),
      pltpu.with_memory_space_constraint(lax.empty(land, src.dtype), pltpu.HBM))
    return res[:2 * N_OTHER], res[2 * N_OTHER], res[2 * N_OTHER + 1], res[-1]


def split_wait(sems, src, land, after, name, per_chip=False):
    def body(src_ref, land_ref, *rest):
        sends, arrivals = _ici_halves(src_ref, land_ref, rest[:2 * N_OTHER], per_chip)
        for cp in sends:
            cp().wait_send()
        for cp in arrivals:
            cp().wait_recv()

    return pl.pallas_call(
        body, name=name, in_specs=(HBM, HBM) + (SEM,) * (2 * N_OTHER) + (ANY,) * len(after),
        out_specs=(HBM, HBM), out_shape=(pltpu.HBM(src.shape, src.dtype), pltpu.HBM(land.shape, land.dtype)),
        input_output_aliases={0: 0, 1: 1},
        compiler_params=pltpu.CompilerParams(has_side_effects=pltpu.SideEffectType.DATAFLOW_SIDE_EFFECTING),
    )(src, land, *sems, *after)


def handover(land, name, sums=None):
    n = N_OTHER + (sums is not None)

    def body(*refs):
        land_ref, send_sems, recv_sems = refs[0], refs[-2], refs[-1]
        x, y, c = _position()
        me = 2 * x + y
        sibling = (x, y, 1 - c)
        pieces = [(_half(land_ref, 2 * px + py, c), 2 * px + py) for px, py in _other_chips(x, y)]
        if sums is not None:
            pieces.append((refs[1].at[me], me))
        sends = [_remote(piece, _half(land_ref, chip, c), send_sems.at[j], recv_sems.at[j], sibling)
                 for j, (piece, chip) in enumerate(pieces)]
        for cp in sends:
            cp.start()
        for j, (_, chip) in enumerate(pieces):
            other = _half(land_ref, chip, 1 - c)
            _remote(other, other, send_sems.at[j], recv_sems.at[j], sibling).wait_recv()
        for cp in sends:
            cp.wait_send()

    args = (land,) + ((sums,) if sums is not None else ())
    return pl.pallas_call(
        body, name=name, in_specs=[ANY] * len(args), out_specs=ANY,
        out_shape=jax.ShapeDtypeStruct(land.shape, land.dtype), input_output_aliases={0: 0},
        scratch_shapes=[pltpu.SemaphoreType.DMA((n,)), pltpu.SemaphoreType.DMA((n,))],
    )(*args)


def swap_comm(grads):
    n = len(grads)

    def copies(srcs, gots, sems):
        send_sems, recv_sems = sems
        x, y, c = _position()
        out = []
        for i, (s, o) in enumerate(zip(srcs, gots)):
            rows = s.shape[1] // 2
            out.append(_remote(s.at[:, pl.ds((1 - c) * rows, rows), :], o, send_sems.at[i], recv_sems.at[i],
                               (x, y, 1 - c)))
        return out

    def first(srcs, gots, sems):
        for cp in copies(srcs, gots, sems):
            cp.start()

    def last(srcs, gots, sems):
        for cp in copies(srcs, gots, sems):
            cp.wait()

    return Comm(grads, [jax.ShapeDtypeStruct((N_CHIPS, g.shape[1] // 2, g.shape[2]), g.dtype) for g in grads],
                [pltpu.SemaphoreType.DMA((n,)), pltpu.SemaphoreType.DMA((n,))], first, None, last)


def add_halves(name, g, got, core):
    _, half, cols = got.shape
    mine = pl.BlockSpec((None, half, cols), lambda k, c_ref: (k, c_ref[0], 0))
    other = pl.BlockSpec((None, half, cols), lambda k, c_ref: (k, 0, 0))

    def body(c_ref, g_ref, got_ref, o_ref):
        o_ref[...] = (g_ref[...] + got_ref[...]).astype(BF16)

    return pl.pallas_call(
        body, name="add_halves_" + name,
        grid_spec=pltpu.PrefetchScalarGridSpec(num_scalar_prefetch=1, grid=(N_CHIPS,), in_specs=[mine, other],
                                               out_specs=other),
        out_shape=jax.ShapeDtypeStruct(got.shape, BF16),
        compiler_params=_params(("parallel",)),
    )(core, g, got)


def exchange_comm(parts):
    n = len(parts)

    def copies(srcs, outs, sems):
        send_sems, recv_sems, local_sems = sems
        x, y, c = _position()
        me = 2 * x + y
        sibling = (x, y, 1 - c)
        chips = _other_chips(x, y)
        locals_, sends, arrived, passed, from_sibling = [], [], [], [], []
        for i, (s, o) in enumerate(zip(srcs, outs)):
            locals_.append(_later(pltpu.make_async_copy, s.at[me], _half(o, me, c), local_sems.at[i]))
            sends.append(_later(_remote, s.at[me], _half(o, me, c), send_sems.at[i, 3], recv_sems.at[i, 3], sibling))
            other = _half(o, me, 1 - c)
            from_sibling.append(_later(_remote, other, other, send_sems.at[i, 3], recv_sems.at[i, 3], sibling))
        for j, (px, py) in enumerate(chips):
            for i, (s, o) in enumerate(zip(srcs, outs)):
                sends.append(_later(_remote, s.at[2 * px + py], _half(o, me, c), send_sems.at[i, j],
                                    recv_sems.at[i, j], (px, py, c)))
                got = _half(o, 2 * px + py, c)
                arrived.append(_later(_remote, got, got, send_sems.at[i, j], recv_sems.at[i, j], (px, py, c)))
                passed.append(_later(_remote, got, got, send_sems.at[i, 4 + j], recv_sems.at[i, 4 + j], sibling))
                other = _half(o, 2 * px + py, 1 - c)
                from_sibling.append(_later(_remote, other, other, send_sems.at[i, 4 + j], recv_sems.at[i, 4 + j],
                                           sibling))
        return locals_, sends, arrived, passed, from_sibling

    return Comm(parts, [jax.ShapeDtypeStruct((N_CHIPS, 2 * p.shape[1], p.shape[2]), p.dtype) for p in parts],
                [pltpu.SemaphoreType.DMA((n, 7)), pltpu.SemaphoreType.DMA((n, 7)), pltpu.SemaphoreType.DMA((n,))],
                *_two_level_phases(copies))


def small_comm(shares):
    n = len(shares)

    def copies(srcs, outs, sems):
        send_sems, recv_sems, local_sems = sems
        x, y, c = _position()
        me = 4 * x + 2 * y + c
        flips = [(fx, fy, fc) for fx in (0, 1) for fy in (0, 1) for fc in (0, 1)][1:]
        peers = [(1 - x if fx else x, 1 - y if fy else y, 1 - c if fc else c) for fx, fy, fc in flips]
        locals_, sends, arrived = [], [], []
        for i, (src_ref, out_ref) in enumerate(zip(srcs, outs)):
            locals_.append(_later(pltpu.make_async_copy, src_ref, out_ref.at[me], local_sems.at[i]))
            for j, (px, py, pc) in enumerate(peers):
                sends.append(_later(_remote, src_ref, out_ref.at[me], send_sems.at[i, j], recv_sems.at[i, j],
                                    (px, py, pc)))
                got = out_ref.at[4 * px + 2 * py + pc]
                arrived.append(_later(_remote, got, got, send_sems.at[i, j], recv_sems.at[i, j], (px, py, pc)))
        return locals_, sends, arrived

    def first(*refs):
        locals_, sends, _ = copies(*refs)
        for cp in locals_ + sends:
            cp().start()

    def last(*refs):
        locals_, sends, arrived = copies(*refs)
        for cp in arrived:
            cp().wait_recv()
        for cp in sends:
            cp().wait_send()
        for cp in locals_:
            cp().wait()

    return Comm(shares, [jax.ShapeDtypeStruct((N_DEV,) + s.shape, s.dtype) for s in shares],
                [pltpu.SemaphoreType.DMA((n, 7)), pltpu.SemaphoreType.DMA((n, 7)), pltpu.SemaphoreType.DMA((n,))],
                first, None, last)


def _adam_fn(w, g, m, v):
    m = ADAM_B1 * m + (1.0 - ADAM_B1) * g
    v = ADAM_B2 * v + (1.0 - ADAM_B2) * jnp.square(g)
    m_hat = m / (1.0 - ADAM_B1 ** ADAM_STEP)
    v_hat = v / (1.0 - ADAM_B2 ** ADAM_STEP)
    return -ADAM_LR * (m_hat / (jnp.sqrt(v_hat) + ADAM_EPS) + ADAM_WD * w), m, v


def adam_big(name, parts, w, m, v):
    rows, cols = w.shape
    tm = _pick(rows, 384, 16)

    def fn(p0, p1, p2, p3, wv, mv, vv):
        g = ((p0.astype(F32) + p1.astype(F32)) + p2.astype(F32)) + p3.astype(F32)
        return (g,) + _adam_fn(wv, g, mv, vv)

    return rowwise(fn, [parts, w, m, v], [(cols, F32)] * 4, "adam_" + name, tm=tm, rows=rows)


def adam_small(name, gathered, w, m, v):
    def body(g_ref, w_ref, m_ref, v_ref, go_ref, d_ref, mo_ref, vo_ref):
        g = g_ref[0]
        for k in range(1, N_DEV):
            g = g + g_ref[k]
        go_ref[...] = g
        d_ref[...], mo_ref[...], vo_ref[...] = _adam_fn(w_ref[...], g, m_ref[...], v_ref[...])

    return pl.pallas_call(body, name=name, out_shape=[jax.ShapeDtypeStruct(w.shape, F32)] * 4,
                          compiler_params=_params())(gathered, w, m, v)


def _ssm_2d(name, t):
    t = t[0] if t.ndim > 2 else t
    if name in ("ssm_b_re", "ssm_b_im"):
        return t.transpose(0, 2, 1).reshape(SSM_W, 64)
    if name in ("ssm_c_re", "ssm_c_im"):
        return t.reshape(SSM_W, 64)
    return t.T if name == "ssm_d" else t


def _ssm_back(name, t):
    if name in ("ssm_b_re", "ssm_b_im"):
        return t.reshape(32, 16, 64).transpose(0, 2, 1)[None]
    if name in ("ssm_c_re", "ssm_c_im"):
        return t.reshape(1, 32, 16, 64)
    if name == "ssm_d":
        return t.T[None]
    return t if name == "ssm_log_dt" else t[None]


def adam_ssm(shares, w, m, v):
    n = len(w)

    def body(*refs):
        ins, outs = refs[:4 * n], refs[4 * n:]
        for i in range(n):
            g_ref, w_ref, m_ref, v_ref = (ins[k * n + i] for k in range(4))
            g = g_ref[0]
            for k in range(1, N_DEV):
                g = g + g_ref[k]
            outs[4 * i][...] = g
            outs[4 * i + 1][...], outs[4 * i + 2][...], outs[4 * i + 3][...] = _adam_fn(w_ref[...], g, m_ref[...],
                                                                                      v_ref[...])

    out_shape = [jax.ShapeDtypeStruct(t.shape, F32) for t in w for _ in range(4)]
    res = pl.pallas_call(body, name="adam_ssm", out_shape=out_shape, compiler_params=_params())(*shares, *w, *m, *v)
    return [res[4 * i:4 * i + 4] for i in range(n)]


def _pack_small(names, vals, rows, last=None):
    flat = [vals[n].reshape(-1) for n in names]
    if last is not None:
        flat.append(last.reshape(-1))
    flat = jnp.concatenate(flat)
    return jnp.pad(flat, (0, rows * LANES - flat.shape[0])).reshape(rows, LANES)


def _unpack_small(names, pack, shapes):
    flat, out, off = pack.reshape(-1), {}, 0
    for n in names:
        size = math.prod(shapes[n])
        out[n] = flat[off:off + size].reshape(shapes[n])
        off += size
    return out, flat[off]


def _to_slots(name, g, shard_shape):
    rows, cols = shard_shape
    if name in ROW_SHARDED:
        return g.reshape(N_CHIPS, rows, cols)
    return g.reshape(rows, N_CHIPS, cols).transpose(1, 0, 2)


def _from_slots(name, s):
    _, rows, cols = s.shape
    if name in ROW_SHARDED:
        return s.reshape(N_CHIPS * rows, cols)
    return s.transpose(1, 0, 2).reshape(rows, N_CHIPS * cols)


def kernel(x, norm_mix_g, w_in, ssm_a_re, ssm_a_im, ssm_log_dt, ssm_b_re, ssm_b_im, ssm_c_re, ssm_c_im, ssm_d, w_glu, w_attn_out, w_out, norm_ffn_g, w_ffn_gate, w_ffn_up, w_ffn_down, norm_final_g, loss_target, m_norm_mix_g, m_w_in, m_ssm_a_re, m_ssm_a_im, m_ssm_log_dt, m_ssm_b_re, m_ssm_b_im, m_ssm_c_re, m_ssm_c_im, m_ssm_d, m_w_glu, m_w_attn_out, m_w_out, m_norm_ffn_g, m_w_ffn_gate, m_w_ffn_up, m_w_ffn_down, m_norm_final_g, v_norm_mix_g, v_w_in, v_ssm_a_re, v_ssm_a_im, v_ssm_log_dt, v_ssm_b_re, v_ssm_b_im, v_ssm_c_re, v_ssm_c_im, v_ssm_d, v_w_glu, v_w_attn_out, v_w_out, v_norm_ffn_g, v_w_ffn_gate, v_w_ffn_up, v_w_ffn_down, v_norm_final_g):
    given = dict(locals())
    def local(name, prefix=""):
        t = given[prefix + name][0]
        return t.T if name in TRANSPOSED else t

    shard = {n: local(n) for n in BIG}
    shapes = {n: given[n].shape for n in WEIGHTS}

    small = {n: given[n] for n in SMALL}
    small_2d = dict(small)
    for n in ("ssm_a_re", "ssm_a_im", "ssm_b_re", "ssm_b_im", "ssm_c_re", "ssm_c_im", "ssm_d"):
        small_2d[n] = small[n][0]
    small_2d["norm_final_g"] = norm_final_g.reshape(1, D_MODEL)

    core = lax.axis_index("c").astype(jnp.int32).reshape(1)
    loss, grad_x, parts, ssm_shares, gs_norm, w_in_reduce = local_step(
        x.reshape(TOKENS, D_MODEL), loss_target.reshape(TOKENS, D_MODEL),
        {n: shard[n] for n in BIG}, small_2d, core)

    (norm_shares,) = run_comm(small_comm([_pack_small(NORM_SMALL, gs_norm, NORM_ROWS, last=loss)]),
                              "gather_norm_grads")
    small_out = [{} for _ in range(4)]
    packs = [_pack_small(NORM_SMALL, {n: given[p + n] for n in NORM_SMALL}, NORM_ROWS) for p in ("", "m_", "v_")]
    for kind, t in enumerate(adam_small("adam_norm_gains", norm_shares, *packs)):
        vals, after = _unpack_small(NORM_SMALL, t, shapes)
        small_out[kind].update(vals)
        if kind == 0:
            total_loss = after
    ssm_in = [[_ssm_2d(n, given[p + n]) for n in SSM_SMALL] for p in ("", "m_", "v_")]
    for n, res in zip(SSM_SMALL, adam_ssm(ssm_shares, *ssm_in)):
        for kind, t in enumerate(res):
            small_out[kind][n] = _ssm_back(n, t)

    big_out, updated = {}, {}
    for n in BIG[1:] + BIG[:1]:
        if n == "w_in":
            sems, chip_sum, land = w_in_reduce
            behind = [updated[k][1] for k in BIG[1:]] + [norm_shares]
            chip_sum, land = split_wait(sems, chip_sum, land, behind, "w_in_reduce_wait", per_chip=True)
            land = handover(land, "w_in_reduce_handover", sums=chip_sum)
            me = 2 * lax.axis_index("x") + lax.axis_index("y")
            parts[n] = lax.dynamic_update_slice(land, lax.dynamic_slice_in_dim(chip_sum, me, 1, 0),
                                                (me, lax.axis_index("c") * chip_sum.shape[1], 0))
        updated[n] = adam_big(n, parts[n], shard[n], local(n, "m_"), local(n, "v_"))
        big_out[n] = [(t.T if n in TRANSPOSED else t)[None] for t in updated[n]]

    outs = [total_loss, grad_x.reshape(LOCAL_BATCH, SEQ, D_MODEL)]
    for kind in range(4):
        for n in WEIGHTS:
            outs.append(big_out[n][kind] if n in BIG else small_out[kind][n])
    return tuple(outs)
```

```python
import functools
import math

import jax
import jax.numpy as jnp
import numpy as np
from jax import lax
from jax.experimental import pallas as pl
from jax.experimental.pallas import tpu as pltpu

F32 = jnp.float32
BF16 = jnp.bfloat16
MESH = pl.DeviceIdType.MESH

D_MODEL = 1024
SEQ = 2048
LOCAL_BATCH = 2
TOKENS = LOCAL_BATCH * SEQ
HEAD_DIM = 64
HEADS_PER_GROUP = 4
GROUP_W = HEADS_PER_GROUP * HEAD_DIM
N_GROUPS = 3
DILATIONS = (1, 4, 16)
ATTN_BLOCK = 128
ROPE_DIM = 16
ROPE_THETA = 500000.0
QKV_W = 3 * N_GROUPS * GROUP_W
SSM_W = 512
SSM_STATE_W = 2048
SSM_LANE_BLOCKS = 4
GATE_W = 2 * D_MODEL
D_FF = 2816
RMS_EPS = 1e-6
NEG_INF = -1e30
ADAM_LR, ADAM_B1, ADAM_B2, ADAM_EPS, ADAM_WD, ADAM_STEP = 0.001, 0.9, 0.999, 1e-08, 0.01, 10
N_CHIPS = 4
N_DEV = 8

VMEM_LIMIT = 56 * 1024 * 1024
LANES = 128


def _params(sem=None):
    return pltpu.CompilerParams(dimension_semantics=sem, vmem_limit_bytes=VMEM_LIMIT)


def _pick(n, cap, align=LANES):
    best = None
    for d in range(align, min(n, cap) + 1, align):
        if n % d == 0:
            best = d
    return n if best is None or n <= cap else best


_DIMS = {"nn": (((1,), (0,)), ((), ())), "nt": (((1,), (1,)), ((), ())), "tn": (((0,), (0,)), ((), ()))}


def _dot(a, b, mode):
    return lax.dot_general(a, b, _DIMS[mode], preferred_element_type=F32)


def matmul(a, b, mode, out_dtype, name, add=None, comm=None):
    if mode == "nn":
        (m, k), n = a.shape, b.shape[1]
    elif mode == "nt":
        (m, k), n = a.shape, b.shape[0]
    else:
        (k, m), n = a.shape, b.shape[1]
    tn = _pick(n, 1408 if mode != "tn" else 512)
    tk = _pick(k, 2816) if mode != "tn" else k
    tm = _pick(m, 1408)
    out_bytes = jnp.dtype(out_dtype).itemsize

    def need(tm_):
        return 2 * 2 * (tm_ * tk + tk * tn) + tm_ * tn * (4 + 2 * out_bytes + (8 if add is not None else 0))

    while need(tm) > 40 * 1024 * 1024 and tm % 256 == 0:
        tm //= 2
    nk = k // tk
    a_spec = {"nn": pl.BlockSpec((tm, tk), lambda i, j, kk: (i, kk)),
              "nt": pl.BlockSpec((tm, tk), lambda i, j, kk: (i, kk)),
              "tn": pl.BlockSpec((tk, tm), lambda i, j, kk: (kk, i))}[mode]
    b_spec = {"nn": pl.BlockSpec((tk, tn), lambda i, j, kk: (kk, j)),
              "nt": pl.BlockSpec((tn, tk), lambda i, j, kk: (j, kk)),
              "tn": pl.BlockSpec((tk, tn), lambda i, j, kk: (kk, j))}[mode]
    o_spec = pl.BlockSpec((tm, tn), lambda i, j, kk: (i, j))

    def body(a_ref, b_ref, *rest):
        if add is not None:
            add_ref, o_ref, acc_ref = rest
        else:
            o_ref, acc_ref = rest
        part = _dot(a_ref[...], b_ref[...], mode)
        if nk == 1:
            res = part if add is None else part + add_ref[...]
            o_ref[...] = res.astype(out_dtype)
            return
        kk = pl.program_id(2)

        @pl.when(kk == 0)
        def _():
            acc_ref[...] = part

        @pl.when(kk > 0)
        def _():
            acc_ref[...] += part

        @pl.when(kk == nk - 1)
        def _():
            res = acc_ref[...] if add is None else acc_ref[...] + add_ref[...]
            o_ref[...] = res.astype(out_dtype)

    in_specs = [a_spec, b_spec] + ([o_spec] if add is not None else [])
    args = (a, b) + ((add,) if add is not None else ())
    res = hosted_call(
        body, comm, name, (m // tm, n // tn, nk), in_specs, [o_spec], [jax.ShapeDtypeStruct((m, n), out_dtype)],
        [pltpu.VMEM((tm, tn) if nk > 1 else (8, LANES), F32)], args, ("parallel", "parallel", "arbitrary"))
    return res[0] if comm is None else res


def matmul_rows(a, b, name, fn, extra, outs, accs=(), add=None, comm=None, tm=512):
    (m, k), n = a.shape, b.shape[1]
    n_fixed = 2 + (add is not None)
    row_spec = lambda cols: pl.BlockSpec((tm, cols), lambda i: (i, 0))
    in_specs = [row_spec(k), pl.BlockSpec((k, n), lambda i: (0, 0))] + ([row_spec(n)] if add is not None else [])
    in_specs += [pl.BlockSpec(e.shape, lambda i: (0, 0)) if e.shape[0] == 1 else row_spec(e.shape[1]) for e in extra]
    out_specs = [row_spec(c) for c, _ in outs] + [pl.BlockSpec((1, c), lambda i: (0, 0)) for c in accs]
    out_shape = [jax.ShapeDtypeStruct((m, c), dt) for c, dt in outs] + [jax.ShapeDtypeStruct((1, c), F32) for c in accs]

    def body(*refs):
        rows = _dot(refs[0][...], refs[1][...], "nn")
        if add is not None:
            rows = rows + refs[2][...]
        n_in = n_fixed + len(extra)
        res = fn(rows, *[r[...] for r in refs[n_fixed:n_in]])
        for r, v in zip(refs[n_in:n_in + len(outs)], res[:len(outs)]):
            r[...] = v.astype(r.dtype)
        first = pl.program_id(0) == 0
        for r, v in zip(refs[n_in + len(outs):], res[len(outs):]):
            @pl.when(first)
            def _(r=r, v=v):
                r[...] = v

            @pl.when(jnp.logical_not(first))
            def _(r=r, v=v):
                r[...] += v

    args = (a, b) + ((add,) if add is not None else ()) + tuple(extra)
    return hosted_call(body, comm, name, (m // tm,), in_specs, out_specs, out_shape, [], args, ("arbitrary",))


def _merge_specs(tm):
    half = lambda blk: pl.BlockSpec((tm, D_MODEL), functools.partial(lambda i, blk_: (i, blk_), blk_=blk))
    return [half(0), half(1), pl.BlockSpec((tm, GROUP_W), lambda i: (i, 0)),
            pl.BlockSpec((GROUP_W, D_MODEL), lambda i: (0, 0)), half(0), half(1)]


def merge_out_proj(gl, attn_b, w_attn_out, z, w_out, x, g_ffn):
    tm = 512

    def body(g0, g1, at, wa, za, zb, w_ref, x_ref, g_ref, m_ref, x1_ref, h2_ref):
        ad = _dot(at[...], wa[...], "nn")
        merged = _merge_fn(*[t[...].astype(F32) for t in (g0, g1)], ad, *[t[...].astype(F32) for t in (za, zb)])
        merged = merged.astype(BF16)
        m_ref[...] = merged
        x1 = _dot(merged, w_ref[...], "nn") + x_ref[...]
        x1_ref[...] = x1
        h2_ref[...] = _rms(x1, g_ref[...]).astype(BF16)

    rows = pl.BlockSpec((tm, D_MODEL), lambda i: (i, 0))
    whole = pl.BlockSpec((D_MODEL, D_MODEL), lambda i: (0, 0))
    gain = pl.BlockSpec((1, D_MODEL), lambda i: (0, 0))
    tok = lambda dt: jax.ShapeDtypeStruct((TOKENS, D_MODEL), dt)
    return pl.pallas_call(
        body, name="merge_out_proj", grid=(TOKENS // tm,), in_specs=_merge_specs(tm) + [whole, rows, gain],
        out_specs=[rows] * 3, out_shape=[tok(BF16), tok(F32), tok(BF16)], compiler_params=_params(("parallel",)),
    )(gl, gl, attn_b, w_attn_out, z, z, w_out, x, g_ffn)


def merge_bwd(dx1_b, w_out, gl, attn_b, w_attn_out, z, comm=None):
    tm = 512

    def body(dx_ref, w_ref, g0, g1, at, wa, za, zb, dgl_ref, dad_ref, dz_ref, dat_ref):
        dm = _dot(dx_ref[...], w_ref[...], "nt")
        ad = _dot(at[...], wa[...], "nn")
        _, vjp = jax.vjp(_merge_fn, *[t[...].astype(F32) for t in (g0, g1)], ad,
                         *[t[...].astype(F32) for t in (za, zb)])
        dg0, dg1, dad, dza, dzb = vjp(dm)
        dat_ref[...] = _dot(dad.astype(BF16), wa[...], "nt")
        dgl_ref[:, :D_MODEL] = dg0.astype(BF16)
        dgl_ref[:, D_MODEL:] = dg1.astype(BF16)
        dad_ref[...] = dad.astype(BF16)
        dz_ref[:, :D_MODEL] = dza.astype(BF16)
        dz_ref[:, D_MODEL:] = dzb.astype(BF16)

    rows = pl.BlockSpec((tm, D_MODEL), lambda i: (i, 0))
    wide = pl.BlockSpec((tm, GATE_W), lambda i: (i, 0))
    whole = pl.BlockSpec((D_MODEL, D_MODEL), lambda i: (0, 0))
    return hosted_call(
        body, comm, "merge_bwd", (TOKENS // tm,), [rows, whole] + _merge_specs(tm),
        [wide, rows, wide, pl.BlockSpec((tm, GROUP_W), lambda i: (i, 0))],
        [jax.ShapeDtypeStruct((TOKENS, GATE_W), BF16), jax.ShapeDtypeStruct((TOKENS, D_MODEL), BF16),
         jax.ShapeDtypeStruct((TOKENS, GATE_W), BF16), jax.ShapeDtypeStruct((TOKENS, GROUP_W), F32)], [],
        (dx1_b, w_out, gl, gl, attn_b, w_attn_out, z, z), ("arbitrary",))


FFN_TM, FFN_TN = 512, 1408


def ffn_in(h2, wg_t, wu_t, comm=None):
    def body(h_ref, wg_ref, wu_ref, a_ref, b_ref, act_ref):
        hv = h_ref[...]
        a, b = _dot(hv, wg_ref[...], "nt"), _dot(hv, wu_ref[...], "nt")
        a_ref[...] = a.astype(BF16)
        b_ref[...] = b.astype(BF16)
        act_ref[...] = _swiglu_fn(a, b).astype(BF16)

    rows = pl.BlockSpec((FFN_TM, D_MODEL), lambda i, j: (i, 0))
    wts = pl.BlockSpec((FFN_TN, D_MODEL), lambda i, j: (j, 0))
    out = pl.BlockSpec((FFN_TM, FFN_TN), lambda i, j: (i, j))
    return hosted_call(body, comm, "ffn_in", (TOKENS // FFN_TM, D_FF // FFN_TN), [rows, wts, wts], [out] * 3,
                       [jax.ShapeDtypeStruct((TOKENS, D_FF), BF16)] * 3, [], (h2, wg_t, wu_t),
                       ("parallel", "parallel"))


def ffn_in_bwd(dx2_b, wd, a, b):
    def body(dx_ref, wd_ref, a_ref, b_ref, da_ref, db_ref):
        dx = dx_ref[...]
        for lo in range(0, FFN_TN, 512):
            cols = slice(lo, min(lo + 512, FFN_TN))
            dact = _dot(dx, wd_ref[cols, :], "nt")
            _, vjp = jax.vjp(_swiglu_fn, a_ref[:, cols].astype(F32), b_ref[:, cols].astype(F32))
            da, db = vjp(dact)
            da_ref[:, cols] = da.astype(BF16)
            db_ref[:, cols] = db.astype(BF16)

    rows = pl.BlockSpec((FFN_TM, D_MODEL), lambda i, j: (i, 0))
    wts = pl.BlockSpec((FFN_TN, D_MODEL), lambda i, j: (j, 0))
    out = pl.BlockSpec((FFN_TM, FFN_TN), lambda i, j: (i, j))
    return pl.pallas_call(
        body, name="ffn_in_bwd", grid=(TOKENS // FFN_TM, D_FF // FFN_TN), in_specs=[rows, wts, out, out],
        out_specs=[out] * 2, out_shape=[jax.ShapeDtypeStruct((TOKENS, D_FF), BF16)] * 2,
        compiler_params=_params(("parallel", "parallel")),
    )(dx2_b, wd, a, b)


def mix_in_bwd(grads, weights, partial, x, g, skip, comm=None):
    n = len(grads)
    tm = 512

    def body(*refs):
        a_refs, b_refs = refs[:n], refs[n:2 * n]
        part_ref, x_ref, g_ref, skip_ref, gx_ref, dg_ref = refs[2 * n:]
        dh = part_ref[...]
        for a_ref, b_ref in zip(a_refs, b_refs):
            dh = dh + _dot(a_ref[...], b_ref[...], "nn")
        _, vjp = jax.vjp(_rms, x_ref[...], g_ref[...])
        dx, dg = vjp(dh)
        gx_ref[...] = dx + skip_ref[...]
        first = pl.program_id(0) == 0

        @pl.when(first)
        def _():
            dg_ref[...] = dg

        @pl.when(jnp.logical_not(first))
        def _():
            dg_ref[...] += dg

    rows = pl.BlockSpec((tm, D_MODEL), lambda i: (i, 0))
    gain = pl.BlockSpec((1, D_MODEL), lambda i: (0, 0))
    in_specs = [pl.BlockSpec((tm, a.shape[1]), lambda i: (i, 0)) for a in grads]
    in_specs += [pl.BlockSpec(b.shape, lambda i: (0, 0)) for b in weights]
    return hosted_call(
        body, comm, "mix_in_bwd", (TOKENS // tm,), in_specs + [rows, rows, gain, rows], [rows, gain],
        [jax.ShapeDtypeStruct((TOKENS, D_MODEL), F32), jax.ShapeDtypeStruct((1, D_MODEL), F32)], [],
        (*grads, *weights, partial, x, g, skip), ("arbitrary",))


def rowwise(fn, ins, outs, name, accs=(), tm=256, rows=TOKENS, comm=None):
    in_specs, args = [], []
    for item in ins:
        arr, width, blk = item if isinstance(item, tuple) else (item, None, 0)
        if arr.ndim == 3:
            for k in range(arr.shape[0]):
                in_specs.append(pl.BlockSpec((None, tm, arr.shape[2]), functools.partial(lambda i, k_: (k_, i, 0), k_=k)))
                args.append(arr)
            continue
        if arr.shape[0] == 1:
            in_specs.append(pl.BlockSpec(arr.shape, lambda i: (0, 0)))
        elif width is None:
            in_specs.append(pl.BlockSpec((tm, arr.shape[1]), lambda i: (i, 0)))
        else:
            in_specs.append(pl.BlockSpec((tm, width), functools.partial(lambda i, blk_: (i, blk_), blk_=blk)))
        args.append(arr)
    out_specs = [pl.BlockSpec((tm, c), lambda i: (i, 0)) for c, _ in outs]
    out_specs += [pl.BlockSpec((1, c), lambda i: (0, 0)) for c in accs]
    out_shape = [jax.ShapeDtypeStruct((rows, c), dt) for c, dt in outs]
    out_shape += [jax.ShapeDtypeStruct((1, c), F32) for c in accs]
    n_in, n_out = len(args), len(outs)
    c_ins, c_outs, c_sems = _comm_operands(comm)

    def body(*refs):
        refs, c_refs = _comm_refs(comm, refs, n_in, n_out + len(accs))
        step = pl.program_id(0)
        _comm_begin(comm, c_refs, step, rows // tm)
        res = fn(*[r[...] for r in refs[:n_in]])
        for r, v in zip(refs[n_in:n_in + n_out], res[:n_out]):
            r[...] = v.astype(r.dtype)
        first = step == 0
        for r, v in zip(refs[n_in + n_out:], res[n_out:]):
            @pl.when(first)
            def _(r=r, v=v):
                r[...] = v

            @pl.when(jnp.logical_not(first))
            def _(r=r, v=v):
                r[...] += v
        _comm_end(comm, c_refs, step, rows // tm)

    return pl.pallas_call(
        body, name=name, grid=(rows // tm,), in_specs=in_specs + [ANY] * len(c_ins),
        out_specs=out_specs + [ANY] * len(c_outs), out_shape=out_shape + c_outs, scratch_shapes=c_sems,
        compiler_params=_params(("arbitrary",)),
    )(*args, *c_ins)


def first_norm(x, g, others, comm=None):
    tm, n = 256, len(others)

    def body(x_ref, g_ref, *rest):
        srcs, h_ref, dsts = rest[:n], rest[n], rest[n + 1:]
        h_ref[...] = _rms(x_ref[...], g_ref[...]).astype(BF16)
        for k, (s, d) in enumerate(zip(srcs, dsts)):
            @pl.when(pl.program_id(0) == k)
            def _(s=s, d=d):
                d[...] = s[...].astype(BF16)

    rows = pl.BlockSpec((tm, D_MODEL), lambda i: (i, 0))
    whole = [pl.BlockSpec(a.shape, lambda i: (0, 0)) for a in others]
    return hosted_call(
        body, comm, "norm_mix", (TOKENS // tm,), [rows, pl.BlockSpec((1, D_MODEL), lambda i: (0, 0))] + whole,
        [rows] + whole, [jax.ShapeDtypeStruct((TOKENS, D_MODEL), BF16)]
        + [jax.ShapeDtypeStruct(a.shape, BF16) for a in others], [], (x, g, *others), ("arbitrary",))


def _rms(x, g):
    return x * lax.rsqrt(jnp.mean(x * x, axis=-1, keepdims=True) + RMS_EPS) * g


def _colsum(v):
    return jnp.sum(v, axis=0, keepdims=True)


PAIR_W = 2 * HEAD_DIM
N_PAIRS = HEADS_PER_GROUP // 2


def _qkv_order(w_t, back=False):
    dims = (N_PAIRS, N_GROUPS, 3) if back else (3, N_GROUPS, N_PAIRS)
    return w_t.reshape(dims + (PAIR_W, w_t.shape[1])).transpose(2, 1, 0, 3, 4).reshape(QKV_W, w_t.shape[1])


def _rope_tables():
    half = ROPE_DIM // 2
    inv = np.power(np.float32(ROPE_THETA), -np.arange(half, dtype=np.float32) * np.float32(2.0 / ROPE_DIM))
    ang = (np.arange(SEQ, dtype=np.float32)[:, None] * inv[None, :]).astype(np.float32)
    cos, sin = np.cos(ang), np.sin(ang)
    zeros = np.zeros((SEQ, HEAD_DIM - ROPE_DIM), np.float32)
    zh = np.zeros((SEQ, half), np.float32)
    c = np.concatenate([cos, cos, zeros + 1.0], axis=1)
    sa = np.concatenate([-sin, zh, zeros], axis=1)
    sb = np.concatenate([zh, sin, zeros], axis=1)
    return [jnp.asarray(np.tile(t, (1, 2)), F32) for t in (c, sa, sb)]


def _rope_fwd(x, c, sa, sb):
    return x * c + pltpu.roll(x, PAIR_W - 8, 1) * sa + pltpu.roll(x, 8, 1) * sb


def _rope_bwd(dy, c, sa, sb):
    return dy * c + pltpu.roll(dy * sb, PAIR_W - 8, 1) + pltpu.roll(dy * sa, 8, 1)


def _band_masks():
    row = lax.broadcasted_iota(jnp.int32, (ATTN_BLOCK, ATTN_BLOCK), 0)
    col = lax.broadcasted_iota(jnp.int32, (ATTN_BLOCK, ATTN_BLOCK), 1)
    return col <= row, col >= row


def _stack_rows(t):
    return jnp.concatenate([t, t], axis=0)


def _stack_heads(t, first_head):
    return jnp.concatenate([jnp.where(first_head, t, 0), jnp.where(first_head, 0, t)], axis=0)


def _per_head(fn):
    return jnp.concatenate([fn(slice(h * HEAD_DIM, (h + 1) * HEAD_DIM)) for h in range(2)], axis=1)


def _slab_spec(kind):
    return pl.BlockSpec((None, SEQ, PAIR_W), lambda b, p, g: (b, 0, p * 3 * N_GROUPS + g * 3 + kind))


_TABLE_SPEC = pl.BlockSpec((SEQ, PAIR_W), lambda b, p, g: (0, 0))
_PAIR_SPEC = pl.BlockSpec((None, SEQ, PAIR_W), lambda b, p, g: (b, 0, p))


def _block_rows(dil, r, n):
    return pl.ds(n * (ATTN_BLOCK * dil) + r, ATTN_BLOCK, stride=dil)


def proj_qkv(h, w_qkv_t, tables, comm=None):
    tm = 1024
    pair_w = QKV_W // N_PAIRS
    scale = HEAD_DIM ** -0.5

    def body(h_ref, w_ref, c_ref, sa_ref, sb_ref, o_ref):
        rows = _dot(h_ref[...], w_ref[...], "nt")
        c, sa, sb = c_ref[...], sa_ref[...], sb_ref[...]
        for blk in range(pair_w // PAIR_W):
            cols = slice(blk * PAIR_W, (blk + 1) * PAIR_W)
            x = rows[:, cols]
            if blk % 3 == 0:
                x = _rope_fwd(x, c, sa, sb) * scale
            elif blk % 3 == 1:
                x = _rope_fwd(x, c, sa, sb)
            o_ref[:, cols] = x

    table = pl.BlockSpec((tm, PAIR_W), lambda i, j, : (i % (SEQ // tm), 0))
    res = hosted_call(
        body, comm, "proj_qkv", (TOKENS // tm, N_PAIRS),
        [pl.BlockSpec((tm, D_MODEL), lambda i, j: (i, 0)), pl.BlockSpec((pair_w, D_MODEL), lambda i, j: (j, 0)),
         table, table, table],
        [pl.BlockSpec((tm, pair_w), lambda i, j: (i, j))], [jax.ShapeDtypeStruct((TOKENS, QKV_W), F32)], [],
        (h, w_qkv_t, *tables), ("parallel", "parallel"))
    return res[0] if comm is None else res


def attn_fwd(qkv, comm=None):
    def body(qs, ks, v_ref, attn_b_ref, attn_ref, lse_ref, o0, o1, o2, l0, l1, l2):
        g = pl.program_id(2)
        cur_mask, prev_mask = _band_masks()
        first_head = lax.broadcasted_iota(jnp.int32, (ATTN_BLOCK, PAIR_W), 1) < HEAD_DIM

        def run(dil, o_slab, l_slab):
            nb = SEQ // dil // ATTN_BLOCK

            def block(idx, carry):
                r, n = lax.div(idx, nb), lax.rem(idx, nb)
                cur, prev = _block_rows(dil, r, n), _block_rows(dil, r, jnp.maximum(n - 1, 0))
                q = qs[cur, :].astype(BF16)
                kc, kp = ks[cur, :].astype(BF16), ks[prev, :].astype(BF16)
                vc, vp = v_ref[cur, :].astype(BF16), v_ref[prev, :].astype(BF16)
                q2 = _stack_heads(q, first_head)
                mask = _stack_rows(jnp.concatenate([jnp.logical_and(prev_mask, n > 0), cur_mask], axis=1))
                s2 = jnp.where(mask, _dot(q2, jnp.concatenate([kp, kc], axis=0), "nt"), NEG_INF)
                m = jnp.max(s2, axis=-1, keepdims=True)
                vcat, two = jnp.concatenate([vp, vc], axis=0), _stack_rows(first_head)
                vext = jnp.concatenate([jnp.where(two, vcat, 1), jnp.where(two, 1, vcat)], axis=1)
                r2 = _dot(jnp.exp(s2 - m).astype(BF16), vext, "nn")
                r0, r1 = r2[:ATTN_BLOCK, :PAIR_W], r2[ATTN_BLOCK:, PAIR_W:]
                num = jnp.where(first_head, r0, r1)
                den = pltpu.roll(jnp.where(first_head, r1, r0), HEAD_DIM, 1)
                o_slab[cur, :] = num / den
                l_slab[cur, :] = jnp.where(first_head, m[:ATTN_BLOCK], m[ATTN_BLOCK:]) + jnp.log(den)
                return carry

            lax.fori_loop(0, SEQ // ATTN_BLOCK, block, 0, unroll=4)

        for gi, (o_slab, l_slab) in enumerate(((o0, l0), (o1, l1), (o2, l2))):
            @pl.when(g == gi)
            def _(gi=gi, o_slab=o_slab, l_slab=l_slab):
                run(DILATIONS[gi], o_slab, l_slab)

        @pl.when(g == N_GROUPS - 1)
        def _():
            a, b, cc = l0[...], l1[...], l2[...]
            m = jnp.maximum(jnp.maximum(a, b), cc)
            e0, e1, e2 = jnp.exp(a - m), jnp.exp(b - m), jnp.exp(cc - m)
            tot = e0 + e1 + e2
            attn = (e0 * o0[...] + e1 * o1[...] + e2 * o2[...]) / tot
            attn_ref[...] = attn
            attn_b_ref[...] = attn.astype(BF16)
            lse_ref[...] = m + jnp.log(tot)

    shape = (LOCAL_BATCH, SEQ, GROUP_W)
    slab = pltpu.VMEM((SEQ, PAIR_W), F32)
    return hosted_call(
        body, comm, "attn_fwd", (LOCAL_BATCH, N_PAIRS, N_GROUPS),
        [_slab_spec(0), _slab_spec(1), _slab_spec(2)], [_PAIR_SPEC] * 3,
        [jax.ShapeDtypeStruct(shape, BF16), jax.ShapeDtypeStruct(shape, F32), jax.ShapeDtypeStruct(shape, F32)],
        [slab] * 6, (qkv, qkv, qkv), ("parallel", "parallel", "arbitrary"))


def attn_bwd(qkv, tables, dattn, attn, lse, comm=None):
    scale = HEAD_DIM ** -0.5

    def body(qs, ks, v_ref, c_ref, sa_ref, sb_ref, do_ref, out_ref, lse_ref, dqkv_ref, dl, dq_s, dk_s, dv_s):
        g = pl.program_id(2)
        c, sa, sb = c_ref[...], sa_ref[...], sb_ref[...]

        @pl.when(g == 0)
        def _():
            prod = do_ref[...] * out_ref[...]
            dl[...] = _per_head(
                lambda sl: jnp.broadcast_to(jnp.sum(prod[:, sl], axis=-1, keepdims=True), (SEQ, HEAD_DIM)))

        cur_mask, prev_mask = _band_masks()
        first_head = lax.broadcasted_iota(jnp.int32, (ATTN_BLOCK, PAIR_W), 1) < HEAD_DIM

        def run(dil):
            nb = SEQ // dil // ATTN_BLOCK

            def block(idx, carry):
                r, n = lax.div(idx, nb), lax.rem(idx, nb)
                cur = _block_rows(dil, r, n)
                prev = _block_rows(dil, r, jnp.maximum(n - 1, 0))
                nxt = _block_rows(dil, r, jnp.minimum(n + 1, nb - 1))
                q0, q1 = qs[cur, :].astype(BF16), qs[nxt, :].astype(BF16)
                kp, kc = ks[prev, :].astype(BF16), ks[cur, :].astype(BF16)
                vp, vc = v_ref[prev, :].astype(BF16), v_ref[cur, :].astype(BF16)
                do0, do1 = do_ref[cur, :].astype(BF16), do_ref[nxt, :].astype(BF16)
                lse0, lse1, dl0, dl1 = lse_ref[cur, :], lse_ref[nxt, :], dl[cur, :], dl[nxt, :]
                has_prev = jnp.logical_and(prev_mask, n > 0)
                has_next = jnp.logical_and(prev_mask, n < nb - 1)

                def per_row(t):
                    return jnp.concatenate([t[:, 0:1], t[:, HEAD_DIM:HEAD_DIM + 1]], axis=0)

                q20, q21 = _stack_heads(q0, first_head), _stack_heads(q1, first_head)
                do20, do21 = _stack_heads(do0, first_head), _stack_heads(do1, first_head)
                kcat, vcat = jnp.concatenate([kp, kc], axis=0), jnp.concatenate([vp, vc], axis=0)
                mask0 = _stack_rows(jnp.concatenate([has_prev, cur_mask], axis=1))
                p0 = jnp.where(mask0, jnp.exp(_dot(q20, kcat, "nt") - per_row(lse0)), 0.0)
                ds0 = (p0 * (_dot(do20, vcat, "nt") - per_row(dl0))).astype(BF16)
                p1 = jnp.where(_stack_rows(has_next), jnp.exp(_dot(q21, kc, "nt") - per_row(lse1)), 0.0)
                ds1 = (p1 * (_dot(do21, vc, "nt") - per_row(dl1))).astype(BF16)
                dq2 = _dot(ds0, kcat, "nn")
                dq_s[cur, :] = jnp.where(first_head, dq2[:ATTN_BLOCK], dq2[ATTN_BLOCK:])
                ds_cur = jnp.concatenate([ds0[:, ATTN_BLOCK:], ds1], axis=0)
                p_cur = jnp.concatenate([p0[:, ATTN_BLOCK:], p1], axis=0).astype(BF16)
                dk_s[cur, :] = _dot(ds_cur, jnp.concatenate([q20, q21], axis=0), "tn")
                dv_s[cur, :] = _dot(p_cur, jnp.concatenate([do20, do21], axis=0), "tn")
                return carry

            lax.fori_loop(0, SEQ // ATTN_BLOCK, block, 0, unroll=2)

        for gi in range(N_GROUPS):
            @pl.when(g == gi)
            def _(gi=gi):
                run(DILATIONS[gi])

        dqkv_ref[:, 0:PAIR_W] = _rope_bwd(dq_s[...] * scale, c, sa, sb).astype(BF16)
        dqkv_ref[:, PAIR_W:2 * PAIR_W] = _rope_bwd(dk_s[...], c, sa, sb).astype(BF16)
        dqkv_ref[:, 2 * PAIR_W:] = dv_s[...].astype(BF16)

    slab = pltpu.VMEM((SEQ, PAIR_W), F32)
    return hosted_call(
        body, comm, "attn_bwd", (LOCAL_BATCH, N_PAIRS, N_GROUPS),
        [_slab_spec(0), _slab_spec(1), _slab_spec(2), _TABLE_SPEC, _TABLE_SPEC, _TABLE_SPEC,
         _PAIR_SPEC, _PAIR_SPEC, _PAIR_SPEC],
        [pl.BlockSpec((None, SEQ, 3 * PAIR_W), lambda b, p, g: (b, 0, p * N_GROUPS + g))],
        [jax.ShapeDtypeStruct((LOCAL_BATCH, SEQ, QKV_W), BF16)],
        [slab] * 4, (qkv, qkv, qkv, *tables, dattn, attn, lse), ("parallel", "parallel", "arbitrary"))


def _discretize(lr, li, log_dt, br, bi):
    dt = jnp.exp(log_dt)
    mag = jnp.exp(lr * dt)
    ab_re, ab_im = mag * jnp.cos(li * dt), mag * jnp.sin(li * dt)
    den = lr * lr + li * li
    nr, ni = ab_re - 1.0, ab_im
    f_re = (nr * lr + ni * li) / den
    f_im = (ni * lr - nr * li) / den
    return ab_re, ab_im, f_re[None] * br - f_im[None] * bi, f_re[None] * bi + f_im[None] * br


def ssm_prep(lr, li, log_dt, br, bi):
    def body(lr_ref, li_ref, dt_ref, br_ref, bi_ref, *outs):
        for o, v in zip(outs, _discretize(lr_ref[...], li_ref[...], dt_ref[...], br_ref[...], bi_ref[...])):
            o[...] = v
    shapes = [lr, li, br, bi]
    return pl.pallas_call(body, name="ssm_prep",
                          out_shape=[jax.ShapeDtypeStruct(s.shape, F32) for s in shapes])(lr, li, log_dt, br, bi)


def ssm_prep_bwd(lr, li, log_dt, br, bi, g_ab_re, g_ab_im, g_bb_re, g_bb_im):
    def body(lr_ref, li_ref, dt_ref, br_ref, bi_ref, g0, g1, g2, g3, *outs):
        _, vjp = jax.vjp(_discretize, lr_ref[...], li_ref[...], dt_ref[...], br_ref[...], bi_ref[...])
        for o, v in zip(outs, vjp((g0[...], g1[...], g2[...], g3[...]))):
            o[...] = v
    shapes = [lr, li, log_dt, br, bi]
    return pl.pallas_call(body, name="ssm_prep_bwd",
                          out_shape=[jax.ShapeDtypeStruct(s.shape, F32) for s in shapes])(
        lr, li, log_dt, br, bi, g_ab_re, g_ab_im, g_bb_re, g_bb_im)


def _block_diag(t):
    per = SSM_STATE_W // SSM_LANE_BLOCKS // 64
    g = t.transpose(1, 0, 2).reshape(SSM_LANE_BLOCKS, per, 16, 64)
    eye = jnp.eye(per, dtype=t.dtype)
    return jnp.einsum("jgcn,gh->jgchn", g, eye).reshape(SSM_LANE_BLOCKS, per * 16, per * 64)


def _block_diag_t(m):
    per = SSM_STATE_W // SSM_LANE_BLOCKS // 64
    m5 = m.reshape(SSM_LANE_BLOCKS, per, 16, per, 64)
    d = jnp.einsum("jgchn,gh->jgcn", m5, jnp.eye(per, dtype=m.dtype))
    return d.reshape(SSM_LANE_BLOCKS * per, 16, 64).transpose(1, 0, 2)


def _cmul(ar, ai, br, bi):
    return ar * br - ai * bi, ar * bi + ai * br


def _power_tables(ar, ai, reverse):
    width = ar.shape[1]
    row = lax.broadcasted_iota(jnp.int32, (8, width), 0)
    pows = [(ar, ai)]
    for _ in range(7):
        pows.append(_cmul(pows[-1][0], pows[-1][1], ar, ai))
    steps = []
    for k in (1, 2, 4):
        keep = (row >= k) if not reverse else (row < 8 - k)
        steps.append((jnp.where(keep, pows[k - 1][0], 0.0), jnp.where(keep, pows[k - 1][1], 0.0)))
    cr = jnp.zeros((8, width), F32)
    ci = jnp.zeros((8, width), F32)
    for i in range(8):
        pr, pi = pows[i] if not reverse else pows[7 - i]
        cr = jnp.where(row == i, pr, cr)
        ci = jnp.where(row == i, pi, ci)
    return steps, (cr, ci)


SCAN_CHUNK = 2048
STATE_BLOCK = SSM_STATE_W // SSM_LANE_BLOCKS
CHAN_BLOCK = SSM_W // SSM_LANE_BLOCKS


def ssm_fwd(u, ab_re, ab_im, bb_re, bb_im, cb_re, cb_im, d_skip, comm=None):
    nt = SEQ // SCAN_CHUNK
    chan = pl.BlockSpec((None, SCAN_CHUNK, CHAN_BLOCK), lambda b, j, t: (b, t, j))
    state = pl.BlockSpec((None, SCAN_CHUNK, STATE_BLOCK), lambda b, j, t: (b, t, j))
    mat = pl.BlockSpec((None, CHAN_BLOCK, STATE_BLOCK), lambda b, j, t: (j, 0, 0))
    lane = pl.BlockSpec((1, STATE_BLOCK), lambda b, j, t: (0, j))
    dsp = pl.BlockSpec((1, CHAN_BLOCK), lambda b, j, t: (0, j))

    def body(u_ref, ar_ref, ai_ref, bbr_ref, bbi_ref, cbr_ref, cbi_ref, d_ref, y_ref, yg_ref, xr_ref, xi_ref,
             car_r, car_i):
        @pl.when(pl.program_id(2) == 0)
        def _():
            car_r[...] = jnp.zeros_like(car_r)
            car_i[...] = jnp.zeros_like(car_i)

        steps, (pr, pi) = _power_tables(ar_ref[...], ai_ref[...], reverse=False)
        uf = u_ref[...]
        ub = uf.astype(BF16)
        xr_ref[...] = _dot(ub, bbr_ref[...], "nn")
        xi_ref[...] = _dot(ub, bbi_ref[...], "nn")

        def tile(i, carry):
            cr, ci = carry
            sl = pl.ds(pl.multiple_of(i * 8, 8), 8)
            br, bi = xr_ref[sl, :], xi_ref[sl, :]
            for k, (sr, si) in zip((1, 2, 4), steps):
                tr, ti = _cmul(sr, si, pltpu.roll(br, k, 0), pltpu.roll(bi, k, 0))
                br, bi = br + tr, bi + ti
            tr, ti = _cmul(pr, pi, cr, ci)
            br, bi = br + tr, bi + ti
            xr_ref[sl, :] = br
            xi_ref[sl, :] = bi
            return br[7:8, :], bi[7:8, :]

        cr, ci = lax.fori_loop(0, SCAN_CHUNK // 8, tile, (car_r[0:1, :], car_i[0:1, :]), unroll=4)
        car_r[0:1, :] = cr
        car_i[0:1, :] = ci
        y = (_dot(xr_ref[...].astype(BF16), cbr_ref[...], "nt") - _dot(xi_ref[...].astype(BF16), cbi_ref[...], "nt")
             + d_ref[...] * uf)
        y_ref[...] = y
        yg_ref[...] = jax.nn.gelu(y).astype(BF16)

    return hosted_call(
        body, comm, "ssm_fwd", (LOCAL_BATCH, SSM_LANE_BLOCKS, nt),
        [chan, lane, lane, mat, mat, mat, mat, dsp], [chan, chan, state, state],
        [jax.ShapeDtypeStruct((LOCAL_BATCH, SEQ, SSM_W), F32), jax.ShapeDtypeStruct((LOCAL_BATCH, SEQ, SSM_W), BF16),
         jax.ShapeDtypeStruct((LOCAL_BATCH, SEQ, SSM_STATE_W), F32),
         jax.ShapeDtypeStruct((LOCAL_BATCH, SEQ, SSM_STATE_W), F32)],
        [pltpu.VMEM((8, STATE_BLOCK), F32), pltpu.VMEM((8, STATE_BLOCK), F32)],
        (u, ab_re, ab_im, bb_re, bb_im, cb_re, cb_im, d_skip), ("parallel", "parallel", "arbitrary"))


def ssm_bwd(dyg, y, u, xr, xi, ab_re, ab_im, bb_re, bb_im, cb_re, cb_im, d_skip, comm=None):
    nt = SEQ // SCAN_CHUNK
    ntile = SCAN_CHUNK // 8

    def rev(t):
        return nt - 1 - t

    chan = pl.BlockSpec((None, SCAN_CHUNK, CHAN_BLOCK), lambda j, b, t: (b, rev(t), j))
    state = pl.BlockSpec((None, SCAN_CHUNK, STATE_BLOCK), lambda j, b, t: (b, rev(t), j))
    before = pl.BlockSpec((None, 8, STATE_BLOCK), lambda j, b, t: (b, jnp.maximum(rev(t) * ntile - 1, 0), j))
    mat = pl.BlockSpec((None, CHAN_BLOCK, STATE_BLOCK), lambda j, b, t: (j, 0, 0))
    lane = pl.BlockSpec((1, STATE_BLOCK), lambda j, b, t: (0, j))
    lane8 = pl.BlockSpec((8, STATE_BLOCK), lambda j, b, t: (0, j))
    dsp = pl.BlockSpec((1, CHAN_BLOCK), lambda j, b, t: (0, j))

    def body(dyg_ref, y_ref, u_ref, xr_ref, xi_ref, xrb_ref, xib_ref, ar_ref, ai_ref, bbr_ref, bbi_ref, cbr_ref,
             cbi_ref, d_ref, du_ref, dcbr_ref, dcbi_ref, dbbr_ref, dbbi_ref, dd_ref, dar_ref, dai_ref,
             lam_r, lam_i, car_r, car_i):
        b, t = pl.program_id(1), pl.program_id(2)
        first = jnp.logical_and(b == 0, t == 0)

        @pl.when(t == 0)
        def _():
            car_r[...] = jnp.zeros_like(car_r)
            car_i[...] = jnp.zeros_like(car_i)

        @pl.when(first)
        def _():
            for r in (dcbr_ref, dcbi_ref, dbbr_ref, dbbi_ref, dd_ref, dar_ref, dai_ref):
                r[...] = jnp.zeros_like(r)

        steps, (pr, pi) = _power_tables(ar_ref[...], -ai_ref[...], reverse=True)
        uf = u_ref[...]
        _, gelu_vjp = jax.vjp(jax.nn.gelu, y_ref[...])
        dy = gelu_vjp(dyg_ref[...])[0]
        dyb = dy.astype(BF16)
        dd_ref[...] += _colsum(dy * uf)
        lam_r[...] = _dot(dyb, cbr_ref[...], "nn")
        lam_i[...] = -_dot(dyb, cbi_ref[...], "nn")
        dcbr_ref[...] += _dot(dyb, xr_ref[...].astype(BF16), "tn")
        dcbi_ref[...] -= _dot(dyb, xi_ref[...].astype(BF16), "tn")
        row0 = lax.broadcasted_iota(jnp.int32, (8, STATE_BLOCK), 0) == 0
        has_before = rev(t) > 0
        xrb = jnp.where(has_before, xrb_ref[...], 0.0)
        xib = jnp.where(has_before, xib_ref[...], 0.0)

        def tile(s, carry):
            cr, ci, acc_r, acc_i = carry
            i = ntile - 1 - s
            sl = pl.ds(pl.multiple_of(i * 8, 8), 8)
            gr, gi = lam_r[sl, :], lam_i[sl, :]
            for k, (sr, si) in zip((1, 2, 4), steps):
                tr, ti = _cmul(sr, si, pltpu.roll(gr, 8 - k, 0), pltpu.roll(gi, 8 - k, 0))
                gr, gi = gr + tr, gi + ti
            tr, ti = _cmul(pr, pi, cr, ci)
            gr, gi = gr + tr, gi + ti
            lam_r[sl, :] = gr
            lam_i[sl, :] = gi
            sp = pl.ds(pl.multiple_of(jnp.maximum(i - 1, 0) * 8, 8), 8)
            pvr = jnp.where(i > 0, xr_ref[sp, :], xrb)
            pvi = jnp.where(i > 0, xi_ref[sp, :], xib)
            xsr = jnp.where(row0, pltpu.roll(pvr, 1, 0), pltpu.roll(xr_ref[sl, :], 1, 0))
            xsi = jnp.where(row0, pltpu.roll(pvi, 1, 0), pltpu.roll(xi_ref[sl, :], 1, 0))
            acc_r = acc_r + xsr * gr + xsi * gi
            acc_i = acc_i + xsr * gi - xsi * gr
            return gr[0:1, :], gi[0:1, :], acc_r, acc_i

        zero = jnp.zeros((8, STATE_BLOCK), F32)
        cr, ci, acc_r, acc_i = lax.fori_loop(0, ntile, tile, (car_r[0:1, :], car_i[0:1, :], zero, zero), unroll=2)
        car_r[0:1, :] = cr
        car_i[0:1, :] = ci
        dar_ref[...] += acc_r
        dai_ref[...] += acc_i
        lrb, lib = lam_r[...].astype(BF16), lam_i[...].astype(BF16)
        du = _dot(lrb, bbr_ref[...], "nt") + _dot(lib, bbi_ref[...], "nt") + d_ref[...] * dy
        du_ref[...] = du.astype(BF16)
        ub = uf.astype(BF16)
        dbbr_ref[...] += _dot(ub, lrb, "tn")
        dbbi_ref[...] += _dot(ub, lib, "tn")

    mat_shape = jax.ShapeDtypeStruct((SSM_LANE_BLOCKS, CHAN_BLOCK, STATE_BLOCK), F32)
    return hosted_call(
        body, comm, "ssm_bwd", (SSM_LANE_BLOCKS, LOCAL_BATCH, nt),
        [chan, chan, chan, state, state, before, before, lane, lane, mat, mat, mat, mat, dsp],
        [chan, mat, mat, mat, mat, dsp, lane8, lane8],
        [jax.ShapeDtypeStruct((LOCAL_BATCH, SEQ, SSM_W), BF16), mat_shape, mat_shape, mat_shape, mat_shape,
         jax.ShapeDtypeStruct((1, SSM_W), F32), jax.ShapeDtypeStruct((8, SSM_STATE_W), F32),
         jax.ShapeDtypeStruct((8, SSM_STATE_W), F32)],
        [pltpu.VMEM((SCAN_CHUNK, STATE_BLOCK), F32), pltpu.VMEM((SCAN_CHUNK, STATE_BLOCK), F32),
         pltpu.VMEM((8, STATE_BLOCK), F32), pltpu.VMEM((8, STATE_BLOCK), F32)],
        (dyg, y, u, xr, xi, xr, xi, ab_re, ab_im, bb_re, bb_im, cb_re, cb_im, d_skip),
        ("parallel", "arbitrary", "arbitrary"))


def _merge_fn(g0, g1, attn_d, za, zb):
    return jax.nn.sigmoid(g0) * attn_d + jax.nn.sigmoid(g1) * (za * jax.nn.sigmoid(zb))


def _swiglu_fn(a, b):
    return jax.nn.silu(a) * b


def _own_slot(slots, shard):
    me = 2 * lax.axis_index("x") + lax.axis_index("y")
    mine = lax.broadcasted_iota(jnp.int32, (N_CHIPS, 1, 1), 0) == me
    return jnp.where(mine, shard[None], slots)


def _reduce_start(names, gw, shard_shapes):
    return swap_comm([_to_slots(n, gw[n], shard_shapes[n]) for n in names])


def _reduce_chip(names, swap, got, core):
    return exchange_comm([add_halves(n, g, r, core) for n, g, r in zip(names, swap.ins, got)])


def local_step(x, target, shards, small, core):
    g_mix, g_ffn, g_final = small["norm_mix_g"], small["norm_ffn_g"], small["norm_final_g"]
    tables = _rope_tables()
    seqs = lambda t: t.reshape(LOCAL_BATCH, SEQ, t.shape[-1])
    toks = lambda t: t.reshape(TOKENS, t.shape[-1])
    shard_shapes = {n: s.shape for n, s in shards.items()}
    w = {}

    def gather(names):
        return gather_comm([shards[n] for n in names])

    def arrived(names, slots, own=None):
        for n, s in zip(names, slots):
            w[n] = _from_slots(n, s if own is None else _own_slot(s, own))

    later = [n for n in BIG if n != "w_in"]
    sems, w_in_shard, land, token = split_start(shards["w_in"].astype(BF16), "w_in_gather_start")
    zero = token[0, 0]
    h, *rest = first_norm(x, g_mix + zero, [shards[n] for n in later])
    shards = dict(shards)
    shards.update(zip(later, rest))
    br_t = small["ssm_b_re"].transpose(2, 0, 1)
    bi_t = small["ssm_b_im"].transpose(2, 0, 1)
    log_dt = small["ssm_log_dt"].reshape(32, 1)
    ab_re, ab_im, bb_re_t, bb_im_t = ssm_prep(small["ssm_a_re"] + zero, small["ssm_a_im"], log_dt, br_t, bi_t)
    ab = [ab_re.reshape(1, SSM_STATE_W), ab_im.reshape(1, SSM_STATE_W)]
    bb = [_block_diag(bb_re_t).astype(BF16), _block_diag(bb_im_t).astype(BF16)]
    cb = [_block_diag((small["ssm_c_re"] + zero).transpose(1, 0, 2)).astype(BF16),
          _block_diag((small["ssm_c_im"] + zero).transpose(1, 0, 2)).astype(BF16)]
    d_skip = small["ssm_d"].reshape(1, SSM_W)
    w_in_shard, land = split_wait(sems, w_in_shard, land, [h] + bb + cb, "w_in_gather_wait")
    arrived(["w_in"], [handover(land, "w_in_handover")], own=w_in_shard)
    w_qkv, w_u, w_gate = _qkv_order(w["w_in"][:QKV_W]), w["w_in"][QKV_W:QKV_W + SSM_W], w["w_in"][QKV_W + SSM_W:]
    qkv, *slots = proj_qkv(h, w_qkv, tables, comm=gather(["w_attn_out", "w_glu"]))
    arrived(["w_attn_out", "w_glu"], slots)
    qkv = seqs(qkv)
    u = seqs(matmul(h, w_u, "nt", F32, "proj_u"))
    gl, *slots = matmul(h, w_gate, "nt", BF16, "proj_gate", comm=gather(["w_out"]))
    arrived(["w_out"], slots)
    attn_b, attn, lse, *slots = attn_fwd(qkv, comm=gather(["w_ffn_gate"]))
    arrived(["w_ffn_gate"], slots)
    attn_b = toks(attn_b)
    y, yg, xr, xi, *slots = ssm_fwd(u, *ab, *bb, *cb, d_skip, comm=gather(["w_ffn_up"]))
    arrived(["w_ffn_up"], slots)
    yg2 = toks(yg)
    z = matmul(yg2, w["w_glu"], "nn", BF16, "glu")
    merged, x1, h2 = merge_out_proj(gl, attn_b, w["w_attn_out"], z, w["w_out"], x, g_ffn)
    a, b, act, *slots = ffn_in(h2, w["w_ffn_gate"], w["w_ffn_up"], comm=gather(["w_ffn_down"]))
    arrived(["w_ffn_down"], slots)

    def final_fn(xv, g, tgt):
        yv, vjp = jax.vjp(_rms, xv, g)
        err = yv - tgt
        dx, dg = vjp(err * (1.0 / D_MODEL))
        loss = 0.5 * jnp.sum(jnp.mean(err * err, axis=-1, keepdims=True), axis=0, keepdims=True)
        return dx, dx, dg, jnp.broadcast_to(loss, (1, LANES))

    dx2, dx2_b, dg_final, loss = matmul_rows(act, w["w_ffn_down"], "ffn_down_loss", final_fn, [g_final, target],
                                             [(D_MODEL, F32), (D_MODEL, BF16)], accs=(D_MODEL, LANES), add=x1)
    gw, parts = {}, {}
    gw["w_ffn_down"] = matmul(act, dx2_b, "tn", F32, "d_ffn_down")
    da_b, db_b = ffn_in_bwd(dx2_b, w["w_ffn_down"], a, b)
    gw["w_ffn_gate"] = matmul(da_b, h2, "tn", F32, "d_ffn_gate")
    gw["w_ffn_up"] = matmul(db_b, h2, "tn", F32, "d_ffn_up")
    ffn = ["w_ffn_down", "w_ffn_gate", "w_ffn_up"]
    swap = _reduce_start(ffn[:2], gw, shard_shapes)
    dh2, *got = matmul(da_b, w["w_ffn_gate"], "nn", F32, "d_h2_gate", comm=swap)
    ffn_exchange = [_reduce_chip(ffn[:2], swap, got, core)]
    swap = _reduce_start(ffn[2:], gw, shard_shapes)

    def norm_bwd(dh, xv, g, skip):
        _, vjp = jax.vjp(_rms, xv, g)
        dx, dg = vjp(dh)
        dx = dx + skip
        return dx, dx, dg

    dx1, dx1_b, dg_ffn, *got = matmul_rows(db_b, w["w_ffn_up"], "d_h2_up_norm", norm_bwd, [x1, g_ffn, dx2],
                                           [(D_MODEL, F32), (D_MODEL, BF16)], accs=(D_MODEL,), add=dh2, comm=swap)
    ffn_up_exchange = _reduce_chip(ffn[2:], swap, got, core)
    gw["w_out"] = matmul(merged, dx1_b, "tn", F32, "d_out")
    dgl_b, dattn_d_b, dz_b, dattn, parts["w_ffn_up"] = merge_bwd(
        dx1_b, w["w_out"], gl, attn_b, w["w_attn_out"], z, comm=ffn_up_exchange)
    dattn = seqs(dattn)
    gw["w_attn_out"] = matmul(attn_b, dattn_d_b, "tn", F32, "d_attn_out")
    gw["w_glu"] = matmul(yg2, dz_b, "tn", F32, "d_glu")
    dyg = seqs(matmul(dz_b, w["w_glu"], "nt", F32, "d_yg"))
    mixer = ["w_out", "w_attn_out", "w_glu"]
    swap = _reduce_start(mixer, gw, shard_shapes)
    du_b, dcb_re, dcb_im, dbb_re, dbb_im, dd, da_re8, da_im8, *rest = ssm_bwd(
        dyg, y, u, xr, xi, *ab, *bb, *cb, d_skip, comm=join_comms(ffn_exchange + [swap]))
    for n, p in zip(ffn[:2], rest[:2]):
        parts[n] = p
    mixer_exchange = _reduce_chip(mixer, swap, rest[2:], core)
    du_b = toks(du_b)
    g_ab_re = jnp.sum(da_re8, axis=0).reshape(32, 64)
    g_ab_im = jnp.sum(da_im8, axis=0).reshape(32, 64)
    d_lr, d_li, d_ldt, d_br_t, d_bi_t = ssm_prep_bwd(
        small["ssm_a_re"], small["ssm_a_im"], log_dt, br_t, bi_t,
        g_ab_re, g_ab_im, _block_diag_t(dbb_re), _block_diag_t(dbb_im))
    as_gcn = lambda t: t.transpose(1, 0, 2).reshape(SSM_W, 64)
    gs = {
        "ssm_a_re": d_lr, "ssm_a_im": d_li, "ssm_log_dt": d_ldt.reshape(1, 32),
        "ssm_b_re": as_gcn(d_br_t), "ssm_b_im": as_gcn(d_bi_t),
        "ssm_c_re": as_gcn(_block_diag_t(dcb_re)), "ssm_c_im": as_gcn(_block_diag_t(dcb_im)),
        "ssm_d": dd.reshape(32, 16).T,
    }
    ssm_gather = small_comm([gs[n] for n in SSM_SMALL])
    dqkv_b, *rest = attn_bwd(qkv, tables, dattn, attn, lse, comm=join_comms([mixer_exchange, ssm_gather]))
    for n, p in zip(mixer, rest):
        parts[n] = p
    ssm_shares = rest[len(mixer):]
    dqkv_b = toks(dqkv_b)
    d_qkv = matmul(dqkv_b, h, "tn", F32, "d_w_qkv")
    d_u = matmul(du_b, h, "tn", F32, "d_w_u")
    d_gate = matmul(dgl_b, h, "tn", F32, "d_w_gate")
    gw["w_in"] = jnp.concatenate([_qkv_order(d_qkv, back=True), d_u, d_gate], axis=0)
    swap = _reduce_start(["w_in"], gw, shard_shapes)
    dh, *got = matmul(dqkv_b, w_qkv, "nn", F32, "d_h_qkv", comm=swap)
    chip_sum = add_halves("w_in", swap.ins[0], got[0], core)
    sems, chip_sum, land, token = split_start(chip_sum, "w_in_reduce_start", per_chip=True)
    grad_x, dg_mix = mix_in_bwd([du_b, dgl_b], [w_u, w_gate], dh, x, g_mix + token[0, 0], dx1)
    gs_norm = {"norm_mix_g": dg_mix, "norm_ffn_g": dg_ffn, "norm_final_g": dg_final}
    return loss, grad_x, parts, ssm_shares, gs_norm, (sems, chip_sum, land)


ANY = pl.BlockSpec(memory_space=pl.ANY)
BIG = ("w_in", "w_glu", "w_attn_out", "w_out", "w_ffn_gate", "w_ffn_up", "w_ffn_down")
TRANSPOSED = ("w_in", "w_ffn_gate", "w_ffn_up")
ROW_SHARDED = TRANSPOSED + ("w_out", "w_ffn_down")
SMALL = ("norm_mix_g", "ssm_a_re", "ssm_a_im", "ssm_log_dt", "ssm_b_re", "ssm_b_im", "ssm_c_re", "ssm_c_im",
         "ssm_d", "norm_ffn_g", "norm_final_g")
WEIGHTS = ("norm_mix_g", "w_in", "ssm_a_re", "ssm_a_im", "ssm_log_dt", "ssm_b_re", "ssm_b_im", "ssm_c_re",
           "ssm_c_im", "ssm_d", "w_glu", "w_attn_out", "w_out", "norm_ffn_g", "w_ffn_gate", "w_ffn_up",
           "w_ffn_down", "norm_final_g")
SSM_SMALL = SMALL[1:9]
NORM_SMALL = (SMALL[0],) + SMALL[9:]
NORM_ROWS = 32
N_BIG = len(BIG)


def _position():
    return lax.axis_index("x"), lax.axis_index("y"), lax.axis_index("c")


def _other_chips(x, y):
    return [(1 - x, y), (x, 1 - y), (1 - x, 1 - y)]


def _remote(src, dst, send_sem, recv_sem, device):
    return pltpu.make_async_remote_copy(src_ref=src, dst_ref=dst, send_sem=send_sem, recv_sem=recv_sem,
                                        device_id=device, device_id_type=MESH)


_later = functools.partial


def _two_level_phases(copies):
    def first(*refs):
        locals_, sends, _, _, _ = copies(*refs)
        for cp in locals_ + sends:
            cp().start()

    def mid(*refs):
        _, _, arrived, passed, _ = copies(*refs)
        for got, cp in zip(arrived, passed):
            got().wait_recv()
            cp().start()

    def last(*refs):
        locals_, sends, _, passed, from_sibling = copies(*refs)
        for cp in from_sibling:
            cp().wait_recv()
        for cp in sends + passed:
            cp().wait_send()
        for cp in locals_:
            cp().wait()

    return first, mid, last


def _half(ref, chip, which):
    rows = ref.shape[1] // 2
    return ref.at[chip, pl.ds(which * rows, rows), :]


class Comm:
    def __init__(self, ins, out_shapes, sems, first, mid, last):
        self.ins, self.out_shapes, self.sems = list(ins), list(out_shapes), list(sems)
        self.first, self.mid, self.last = first, mid, last


def join_comms(comms):
    def cut(refs_by_kind):
        offs, parts = [0, 0, 0], []
        for cm in comms:
            sizes = (len(cm.ins), len(cm.out_shapes), len(cm.sems))
            parts.append(tuple(refs_by_kind[k][offs[k]:offs[k] + sizes[k]] for k in range(3)))
            offs = [o + s for o, s in zip(offs, sizes)]
        return parts

    def phase(which):
        def run(ins, outs, sems):
            for cm, part in zip(comms, cut((ins, outs, sems))):
                fn = getattr(cm, which)
                if fn is not None:
                    fn(*part)
        return run

    return Comm(sum((cm.ins for cm in comms), []), sum((cm.out_shapes for cm in comms), []),
                sum((cm.sems for cm in comms), []), phase("first"), phase("mid"), phase("last"))


def _comm_operands(comm):
    if comm is None:
        return [], [], []
    return comm.ins, comm.out_shapes, comm.sems


def _comm_begin(comm, refs, step, n_steps):
    if comm is None:
        return
    pl.when(step == 0)(lambda: comm.first(*refs))
    if comm.mid is not None:
        pl.when(step == (n_steps * 3) // 4)(lambda: comm.mid(*refs))


def _comm_end(comm, refs, step, n_steps):
    if comm is not None:
        pl.when(step == n_steps - 1)(lambda: comm.last(*refs))


def _comm_refs(comm, refs, n_in, n_out):
    if comm is None:
        return list(refs), None
    ci, co, cs = len(comm.ins), len(comm.out_shapes), len(comm.sems)
    o0 = n_in + ci
    s0 = o0 + n_out + co
    host = list(refs[:n_in]) + list(refs[o0:o0 + n_out]) + list(refs[s0:len(refs) - cs])
    return host, (list(refs[n_in:o0]), list(refs[o0 + n_out:s0]), list(refs[len(refs) - cs:]))


def run_comm(comm, name):
    n_in, n_out = len(comm.ins), len(comm.out_shapes)

    def body(*refs):
        parts = (list(refs[:n_in]), list(refs[n_in:n_in + n_out]), list(refs[n_in + n_out:]))
        comm.first(*parts)
        if comm.mid is not None:
            comm.mid(*parts)
        comm.last(*parts)

    return pl.pallas_call(body, name=name, in_specs=[ANY] * n_in, out_specs=[ANY] * n_out,
                          out_shape=comm.out_shapes, scratch_shapes=comm.sems)(*comm.ins)


def hosted_call(work, comm, name, grid, in_specs, out_specs, out_shape, scratch_shapes, args, semantics):
    c_ins, c_outs, c_sems = _comm_operands(comm)
    n_steps = math.prod(grid)

    def body(*refs):
        host, c_refs = _comm_refs(comm, refs, len(in_specs), len(out_specs))
        step = 0
        for axis, size in enumerate(grid):
            step = step * size + pl.program_id(axis)
        _comm_begin(comm, c_refs, step, n_steps)
        work(*host)
        _comm_end(comm, c_refs, step, n_steps)

    return pl.pallas_call(
        body, name=name, grid=grid, in_specs=list(in_specs) + [ANY] * len(c_ins),
        out_specs=list(out_specs) + [ANY] * len(c_outs), out_shape=list(out_shape) + c_outs,
        scratch_shapes=list(scratch_shapes) + c_sems,
        compiler_params=_params(semantics if comm is None else ("arbitrary",) * len(grid)),
    )(*args, *c_ins)


def gather_comm(shards):
    n = len(shards)

    def copies(srcs, outs, sems):
        send_sems, recv_sems, local_sems = sems
        x, y, c = _position()
        me = 2 * x + y
        sibling = (x, y, 1 - c)
        chips = _other_chips(x, y)
        locals_ = [_later(pltpu.make_async_copy, s, o.at[me], local_sems.at[i])
                   for i, (s, o) in enumerate(zip(srcs, outs))]
        sends, arrived, passed, from_sibling = [], [], [], []
        for j, (px, py) in enumerate(chips):
            for i, (s, o) in enumerate(zip(srcs, outs)):
                rows = s.shape[0] // 2
                sends.append(_later(_remote, s.at[pl.ds(c * rows, rows), :], _half(o, me, c), send_sems.at[i, j],
                                    recv_sems.at[i, j], (px, py, c)))
                got = _half(o, 2 * px + py, c)
                arrived.append(_later(_remote, got, got, send_sems.at[i, j], recv_sems.at[i, j], (px, py, c)))
                passed.append(_later(_remote, got, got, send_sems.at[i, 3 + j], recv_sems.at[i, 3 + j], sibling))
                other = _half(o, 2 * px + py, 1 - c)
                from_sibling.append(_later(_remote, other, other, send_sems.at[i, 3 + j], recv_sems.at[i, 3 + j],
                                           sibling))
        return locals_, sends, arrived, passed, from_sibling

    return Comm(shards, [jax.ShapeDtypeStruct((N_CHIPS,) + s.shape, s.dtype) for s in shards],
                [pltpu.SemaphoreType.DMA((n, 6)), pltpu.SemaphoreType.DMA((n, 6)), pltpu.SemaphoreType.DMA((n,))],
                *_two_level_phases(copies))


HBM = pl.BlockSpec(memory_space=pltpu.HBM)
SEM = pl.BlockSpec(memory_space=pltpu.SEMAPHORE)
N_OTHER = N_CHIPS - 1


def _ici_halves(src_ref, land_ref, sems, per_chip):
    x, y, c = _position()
    me = 2 * x + y
    rows = land_ref.shape[1] // 2
    sends, arrivals = [], []
    for j, (px, py) in enumerate(_other_chips(x, y)):
        piece = src_ref.at[2 * px + py] if per_chip else src_ref.at[pl.ds(c * rows, rows), :]
        sends.append(_later(_remote, piece, _half(land_ref, me, c), sems[j], sems[N_OTHER + j], (px, py, c)))
        got = _half(land_ref, 2 * px + py, c)
        arrivals.append(_later(_remote, got, got, sems[j], sems[N_OTHER + j], (px, py, c)))
    return sends, arrivals


def split_start(src, name, per_chip=False):
    def body(src_ref, land_ref, *rest):
        sems, token = rest[:2 * N_OTHER], rest[-1]
        for cp in _ici_halves(src_ref, land_ref, sems, per_chip)[0]:
            cp().start()
        token[...] = jnp.zeros_like(token)

    rows, cols = (2 * src.shape[1], src.shape[2]) if per_chip else src.shape
    sem = pltpu.SemaphoreType.DMA(())
    land = (N_CHIPS, rows, cols)
    res = pl.pallas_call(
        body, name=name, in_specs=(HBM, HBM),
        out_specs=(SEM,) * (2 * N_OTHER) + (HBM, HBM, pl.BlockSpec(memory_space=pltpu.VMEM)),
        out_shape=(sem,) * (2 * N_OTHER) + (pltpu.HBM(src.shape, src.dtype), pltpu.HBM(land, src.dtype),
                                           jax.ShapeDtypeStruct((8, LANES), F32)),
        input_output_aliases={0: 2 * N_OTHER, 1: 2 * N_OTHER + 1},
        compiler_params=pltpu.CompilerParams(has_side_effects=pltpu.SideEffectType.DATAFLOW_SIDE_EFFECTING),
    )(pltpu.with_memory_space_constraint(src, pltpu.HBM),
      pltpu.with_memory_space_constraint(lax.empty(land, src.dtype), pltpu.HBM))
    return res[:2 * N_OTHER], res[2 * N_OTHER], res[2 * N_OTHER + 1], res[-1]


def split_wait(sems, src, land, after, name, per_chip=False):
    def body(src_ref, land_ref, *rest):
        sends, arrivals = _ici_halves(src_ref, land_ref, rest[:2 * N_OTHER], per_chip)
        for cp in sends:
            cp().wait_send()
        for cp in arrivals:
            cp().wait_recv()

    return pl.pallas_call(
        body, name=name, in_specs=(HBM, HBM) + (SEM,) * (2 * N_OTHER) + (ANY,) * len(after),
        out_specs=(HBM, HBM), out_shape=(pltpu.HBM(src.shape, src.dtype), pltpu.HBM(land.shape, land.dtype)),
        input_output_aliases={0: 0, 1: 1},
        compiler_params=pltpu.CompilerParams(has_side_effects=pltpu.SideEffectType.DATAFLOW_SIDE_EFFECTING),
    )(src, land, *sems, *after)


def handover(land, name, sums=None):
    n = N_OTHER + (sums is not None)

    def body(*refs):
        land_ref, send_sems, recv_sems = refs[0], refs[-2], refs[-1]
        x, y, c = _position()
        me = 2 * x + y
        sibling = (x, y, 1 - c)
        pieces = [(_half(land_ref, 2 * px + py, c), 2 * px + py) for px, py in _other_chips(x, y)]
        if sums is not None:
            pieces.append((refs[1].at[me], me))
        sends = [_remote(piece, _half(land_ref, chip, c), send_sems.at[j], recv_sems.at[j], sibling)
                 for j, (piece, chip) in enumerate(pieces)]
        for cp in sends:
            cp.start()
        for j, (_, chip) in enumerate(pieces):
            other = _half(land_ref, chip, 1 - c)
            _remote(other, other, send_sems.at[j], recv_sems.at[j], sibling).wait_recv()
        for cp in sends:
            cp.wait_send()

    args = (land,) + ((sums,) if sums is not None else ())
    return pl.pallas_call(
        body, name=name, in_specs=[ANY] * len(args), out_specs=ANY,
        out_shape=jax.ShapeDtypeStruct(land.shape, land.dtype), input_output_aliases={0: 0},
        scratch_shapes=[pltpu.SemaphoreType.DMA((n,)), pltpu.SemaphoreType.DMA((n,))],
    )(*args)


def swap_comm(grads):
    n = len(grads)

    def copies(srcs, gots, sems):
        send_sems, recv_sems = sems
        x, y, c = _position()
        out = []
        for i, (s, o) in enumerate(zip(srcs, gots)):
            rows = s.shape[1] // 2
            out.append(_remote(s.at[:, pl.ds((1 - c) * rows, rows), :], o, send_sems.at[i], recv_sems.at[i],
                               (x, y, 1 - c)))
        return out

    def first(srcs, gots, sems):
        for cp in copies(srcs, gots, sems):
            cp.start()

    def last(srcs, gots, sems):
        for cp in copies(srcs, gots, sems):
            cp.wait()

    return Comm(grads, [jax.ShapeDtypeStruct((N_CHIPS, g.shape[1] // 2, g.shape[2]), g.dtype) for g in grads],
                [pltpu.SemaphoreType.DMA((n,)), pltpu.SemaphoreType.DMA((n,))], first, None, last)


def add_halves(name, g, got, core):
    _, half, cols = got.shape
    mine = pl.BlockSpec((None, half, cols), lambda k, c_ref: (k, c_ref[0], 0))
    other = pl.BlockSpec((None, half, cols), lambda k, c_ref: (k, 0, 0))

    def body(c_ref, g_ref, got_ref, o_ref):
        o_ref[...] = (g_ref[...] + got_ref[...]).astype(BF16)

    return pl.pallas_call(
        body, name="add_halves_" + name,
        grid_spec=pltpu.PrefetchScalarGridSpec(num_scalar_prefetch=1, grid=(N_CHIPS,), in_specs=[mine, other],
                                               out_specs=other),
        out_shape=jax.ShapeDtypeStruct(got.shape, BF16),
        compiler_params=_params(("parallel",)),
    )(core, g, got)


def exchange_comm(parts):
    n = len(parts)

    def copies(srcs, outs, sems):
        send_sems, recv_sems, local_sems = sems
        x, y, c = _position()
        me = 2 * x + y
        sibling = (x, y, 1 - c)
        chips = _other_chips(x, y)
        locals_, sends, arrived, passed, from_sibling = [], [], [], [], []
        for i, (s, o) in enumerate(zip(srcs, outs)):
            locals_.append(_later(pltpu.make_async_copy, s.at[me], _half(o, me, c), local_sems.at[i]))
            sends.append(_later(_remote, s.at[me], _half(o, me, c), send_sems.at[i, 3], recv_sems.at[i, 3], sibling))
            other = _half(o, me, 1 - c)
            from_sibling.append(_later(_remote, other, other, send_sems.at[i, 3], recv_sems.at[i, 3], sibling))
        for j, (px, py) in enumerate(chips):
            for i, (s, o) in enumerate(zip(srcs, outs)):
                sends.append(_later(_remote, s.at[2 * px + py], _half(o, me, c), send_sems.at[i, j],
                                    recv_sems.at[i, j], (px, py, c)))
                got = _half(o, 2 * px + py, c)
                arrived.append(_later(_remote, got, got, send_sems.at[i, j], recv_sems.at[i, j], (px, py, c)))
                passed.append(_later(_remote, got, got, send_sems.at[i, 4 + j], recv_sems.at[i, 4 + j], sibling))
                other = _half(o, 2 * px + py, 1 - c)
                from_sibling.append(_later(_remote, other, other, send_sems.at[i, 4 + j], recv_sems.at[i, 4 + j],
                                           sibling))
        return locals_, sends, arrived, passed, from_sibling

    return Comm(parts, [jax.ShapeDtypeStruct((N_CHIPS, 2 * p.shape[1], p.shape[2]), p.dtype) for p in parts],
                [pltpu.SemaphoreType.DMA((n, 7)), pltpu.SemaphoreType.DMA((n, 7)), pltpu.SemaphoreType.DMA((n,))],
                *_two_level_phases(copies))


def small_comm(shares):
    n = len(shares)

    def copies(srcs, outs, sems):
        send_sems, recv_sems, local_sems = sems
        x, y, c = _position()
        me = 4 * x + 2 * y + c
        flips = [(fx, fy, fc) for fx in (0, 1) for fy in (0, 1) for fc in (0, 1)][1:]
        peers = [(1 - x if fx else x, 1 - y if fy else y, 1 - c if fc else c) for fx, fy, fc in flips]
        locals_, sends, arrived = [], [], []
        for i, (src_ref, out_ref) in enumerate(zip(srcs, outs)):
            locals_.append(_later(pltpu.make_async_copy, src_ref, out_ref.at[me], local_sems.at[i]))
            for j, (px, py, pc) in enumerate(peers):
                sends.append(_later(_remote, src_ref, out_ref.at[me], send_sems.at[i, j], recv_sems.at[i, j],
                                    (px, py, pc)))
                got = out_ref.at[4 * px + 2 * py + pc]
                arrived.append(_later(_remote, got, got, send_sems.at[i, j], recv_sems.at[i, j], (px, py, pc)))
        return locals_, sends, arrived

    def first(*refs):
        locals_, sends, _ = copies(*refs)
        for cp in locals_ + sends:
            cp().start()

    def last(*refs):
        locals_, sends, arrived = copies(*refs)
        for cp in arrived:
            cp().wait_recv()
        for cp in sends:
            cp().wait_send()
        for cp in locals_:
            cp().wait()

    return Comm(shares, [jax.ShapeDtypeStruct((N_DEV,) + s.shape, s.dtype) for s in shares],
                [pltpu.SemaphoreType.DMA((n, 7)), pltpu.SemaphoreType.DMA((n, 7)), pltpu.SemaphoreType.DMA((n,))],
                first, None, last)


def _adam_fn(w, g, m, v):
    m = ADAM_B1 * m + (1.0 - ADAM_B1) * g
    v = ADAM_B2 * v + (1.0 - ADAM_B2) * jnp.square(g)
    m_hat = m / (1.0 - ADAM_B1 ** ADAM_STEP)
    v_hat = v / (1.0 - ADAM_B2 ** ADAM_STEP)
    return -ADAM_LR * (m_hat / (jnp.sqrt(v_hat) + ADAM_EPS) + ADAM_WD * w), m, v


def adam_big(name, parts, w, m, v):
    rows, cols = w.shape
    tm = _pick(rows, 384, 16)

    def fn(p0, p1, p2, p3, wv, mv, vv):
        g = ((p0.astype(F32) + p1.astype(F32)) + p2.astype(F32)) + p3.astype(F32)
        return (g,) + _adam_fn(wv, g, mv, vv)

    return rowwise(fn, [parts, w, m, v], [(cols, F32)] * 4, "adam_" + name, tm=tm, rows=rows)


def adam_small(name, gathered, w, m, v):
    def body(g_ref, w_ref, m_ref, v_ref, go_ref, d_ref, mo_ref, vo_ref):
        g = g_ref[0]
        for k in range(1, N_DEV):
            g = g + g_ref[k]
        go_ref[...] = g
        d_ref[...], mo_ref[...], vo_ref[...] = _adam_fn(w_ref[...], g, m_ref[...], v_ref[...])

    return pl.pallas_call(body, name=name, out_shape=[jax.ShapeDtypeStruct(w.shape, F32)] * 4,
                          compiler_params=_params())(gathered, w, m, v)


def _ssm_2d(name, t):
    t = t[0] if t.ndim > 2 else t
    if name in ("ssm_b_re", "ssm_b_im"):
        return t.transpose(0, 2, 1).reshape(SSM_W, 64)
    if name in ("ssm_c_re", "ssm_c_im"):
        return t.reshape(SSM_W, 64)
    return t.T if name == "ssm_d" else t


def _ssm_back(name, t):
    if name in ("ssm_b_re", "ssm_b_im"):
        return t.reshape(32, 16, 64).transpose(0, 2, 1)[None]
    if name in ("ssm_c_re", "ssm_c_im"):
        return t.reshape(1, 32, 16, 64)
    if name == "ssm_d":
        return t.T[None]
    return t if name == "ssm_log_dt" else t[None]


def adam_ssm(shares, w, m, v):
    n = len(w)

    def body(*refs):
        ins, outs = refs[:4 * n], refs[4 * n:]
        for i in range(n):
            g_ref, w_ref, m_ref, v_ref = (ins[k * n + i] for k in range(4))
            g = g_ref[0]
            for k in range(1, N_DEV):
                g = g + g_ref[k]
            outs[4 * i][...] = g
            outs[4 * i + 1][...], outs[4 * i + 2][...], outs[4 * i + 3][...] = _adam_fn(w_ref[...], g, m_ref[...],
                                                                                      v_ref[...])

    out_shape = [jax.ShapeDtypeStruct(t.shape, F32) for t in w for _ in range(4)]
    res = pl.pallas_call(body, name="adam_ssm", out_shape=out_shape, compiler_params=_params())(*shares, *w, *m, *v)
    return [res[4 * i:4 * i + 4] for i in range(n)]


def _pack_small(names, vals, rows, last=None):
    flat = [vals[n].reshape(-1) for n in names]
    if last is not None:
        flat.append(last.reshape(-1))
    flat = jnp.concatenate(flat)
    return jnp.pad(flat, (0, rows * LANES - flat.shape[0])).reshape(rows, LANES)


def _unpack_small(names, pack, shapes):
    flat, out, off = pack.reshape(-1), {}, 0
    for n in names:
        size = math.prod(shapes[n])
        out[n] = flat[off:off + size].reshape(shapes[n])
        off += size
    return out, flat[off]


def _to_slots(name, g, shard_shape):
    rows, cols = shard_shape
    if name in ROW_SHARDED:
        return g.reshape(N_CHIPS, rows, cols)
    return g.reshape(rows, N_CHIPS, cols).transpose(1, 0, 2)


def _from_slots(name, s):
    _, rows, cols = s.shape
    if name in ROW_SHARDED:
        return s.reshape(N_CHIPS * rows, cols)
    return s.transpose(1, 0, 2).reshape(rows, N_CHIPS * cols)


def kernel(x, norm_mix_g, w_in, ssm_a_re, ssm_a_im, ssm_log_dt, ssm_b_re, ssm_b_im, ssm_c_re, ssm_c_im, ssm_d, w_glu, w_attn_out, w_out, norm_ffn_g, w_ffn_gate, w_ffn_up, w_ffn_down, norm_final_g, loss_target, m_norm_mix_g, m_w_in, m_ssm_a_re, m_ssm_a_im, m_ssm_log_dt, m_ssm_b_re, m_ssm_b_im, m_ssm_c_re, m_ssm_c_im, m_ssm_d, m_w_glu, m_w_attn_out, m_w_out, m_norm_ffn_g, m_w_ffn_gate, m_w_ffn_up, m_w_ffn_down, m_norm_final_g, v_norm_mix_g, v_w_in, v_ssm_a_re, v_ssm_a_im, v_ssm_log_dt, v_ssm_b_re, v_ssm_b_im, v_ssm_c_re, v_ssm_c_im, v_ssm_d, v_w_glu, v_w_attn_out, v_w_out, v_norm_ffn_g, v_w_ffn_gate, v_w_ffn_up, v_w_ffn_down, v_norm_final_g):
    given = dict(locals())
    def local(name, prefix=""):
        t = given[prefix + name][0]
        return t.T if name in TRANSPOSED else t

    shard = {n: local(n) for n in BIG}
    shapes = {n: given[n].shape for n in WEIGHTS}

    small = {n: given[n] for n in SMALL}
    small_2d = dict(small)
    for n in ("ssm_a_re", "ssm_a_im", "ssm_b_re", "ssm_b_im", "ssm_c_re", "ssm_c_im", "ssm_d"):
        small_2d[n] = small[n][0]
    small_2d["norm_final_g"] = norm_final_g.reshape(1, D_MODEL)

    core = lax.axis_index("c").astype(jnp.int32).reshape(1)
    loss, grad_x, parts, ssm_shares, gs_norm, w_in_reduce = local_step(
        x.reshape(TOKENS, D_MODEL), loss_target.reshape(TOKENS, D_MODEL),
        {n: shard[n] for n in BIG}, small_2d, core)

    (norm_shares,) = run_comm(small_comm([_pack_small(NORM_SMALL, gs_norm, NORM_ROWS, last=loss)]),
                              "gather_norm_grads")
    small_out = [{} for _ in range(4)]
    packs = [_pack_small(NORM_SMALL, {n: given[p + n] for n in NORM_SMALL}, NORM_ROWS) for p in ("", "m_", "v_")]
    for kind, t in enumerate(adam_small("adam_norm_gains", norm_shares, *packs)):
        vals, after = _unpack_small(NORM_SMALL, t, shapes)
        small_out[kind].update(vals)
        if kind == 0:
            total_loss = after
    ssm_in = [[_ssm_2d(n, given[p + n]) for n in SSM_SMALL] for p in ("", "m_", "v_")]
    for n, res in zip(SSM_SMALL, adam_ssm(ssm_shares, *ssm_in)):
        for kind, t in enumerate(res):
            small_out[kind][n] = _ssm_back(n, t)

    big_out, updated = {}, {}
    for n in BIG[1:] + BIG[:1]:
        if n == "w_in":
            sems, chip_sum, land = w_in_reduce
            behind = [updated[k][1] for k in BIG[1:]] + [norm_shares]
            chip_sum, land = split_wait(sems, chip_sum, land, behind, "w_in_reduce_wait", per_chip=True)
            land = handover(land, "w_in_reduce_handover", sums=chip_sum)
            me = 2 * lax.axis_index("x") + lax.axis_index("y")
            parts[n] = lax.dynamic_update_slice(land, lax.dynamic_slice_in_dim(chip_sum, me, 1, 0),
                                                (me, lax.axis_index("c") * chip_sum.shape[1], 0))
        updated[n] = adam_big(n, parts[n], shard[n], local(n, "m_"), local(n, "v_"))
        big_out[n] = [(t.T if n in TRANSPOSED else t)[None] for t in updated[n]]

    outs = [total_loss, grad_x.reshape(LOCAL_BATCH, SEQ, D_MODEL)]
    for kind in range(4):
        for n in WEIGHTS:
            outs.append(big_out[n][kind] if n in BIG else small_out[kind][n])
    return tuple(outs)
```

```python
import functools
import math

import jax
import jax.numpy as jnp
import numpy as np
from jax import lax
from jax.experimental import pallas as pl
from jax.experimental.pallas import tpu as pltpu

F32 = jnp.float32
BF16 = jnp.bfloat16
MESH = pl.DeviceIdType.MESH

D_MODEL = 1024
SEQ = 2048
LOCAL_BATCH = 2
TOKENS = LOCAL_BATCH * SEQ
HEAD_DIM = 64
HEADS_PER_GROUP = 4
GROUP_W = HEADS_PER_GROUP * HEAD_DIM
N_GROUPS = 3
DILATIONS = (1, 4, 16)
ATTN_BLOCK = 128
ROPE_DIM = 16
ROPE_THETA = 500000.0
QKV_W = 3 * N_GROUPS * GROUP_W
SSM_W = 512
SSM_STATE_W = 2048
SSM_LANE_BLOCKS = 4
GATE_W = 2 * D_MODEL
D_FF = 2816
RMS_EPS = 1e-6
NEG_INF = -1e30
ADAM_LR, ADAM_B1, ADAM_B2, ADAM_EPS, ADAM_WD, ADAM_STEP = 0.001, 0.9, 0.999, 1e-08, 0.01, 10
N_CHIPS = 4
N_DEV = 8

VMEM_LIMIT = 56 * 1024 * 1024
LANES = 128


def _params(sem=None):
    return pltpu.CompilerParams(dimension_semantics=sem, vmem_limit_bytes=VMEM_LIMIT)


def _pick(n, cap, align=LANES):
    best = None
    for d in range(align, min(n, cap) + 1, align):
        if n % d == 0:
            best = d
    return n if best is None or n <= cap else best


_DIMS = {"nn": (((1,), (0,)), ((), ())), "nt": (((1,), (1,)), ((), ())), "tn": (((0,), (0,)), ((), ()))}


def _dot(a, b, mode):
    return lax.dot_general(a, b, _DIMS[mode], preferred_element_type=F32)


def matmul(a, b, mode, out_dtype, name, add=None, comm=None):
    if mode == "nn":
        (m, k), n = a.shape, b.shape[1]
    elif mode == "nt":
        (m, k), n = a.shape, b.shape[0]
    else:
        (k, m), n = a.shape, b.shape[1]
    tn = _pick(n, 1408 if mode != "tn" else 512)
    tk = _pick(k, 2816) if mode != "tn" else k
    tm = _pick(m, 1408)
    out_bytes = jnp.dtype(out_dtype).itemsize

    def need(tm_):
        return 2 * 2 * (tm_ * tk + tk * tn) + tm_ * tn * (4 + 2 * out_bytes + (8 if add is not None else 0))

    while need(tm) > 40 * 1024 * 1024 and tm % 256 == 0:
        tm //= 2
    nk = k // tk
    a_spec = {"nn": pl.BlockSpec((tm, tk), lambda i, j, kk: (i, kk)),
              "nt": pl.BlockSpec((tm, tk), lambda i, j, kk: (i, kk)),
              "tn": pl.BlockSpec((tk, tm), lambda i, j, kk: (kk, i))}[mode]
    b_spec = {"nn": pl.BlockSpec((tk, tn), lambda i, j, kk: (kk, j)),
              "nt": pl.BlockSpec((tn, tk), lambda i, j, kk: (j, kk)),
              "tn": pl.BlockSpec((tk, tn), lambda i, j, kk: (kk, j))}[mode]
    o_spec = pl.BlockSpec((tm, tn), lambda i, j, kk: (i, j))

    def body(a_ref, b_ref, *rest):
        if add is not None:
            add_ref, o_ref, acc_ref = rest
        else:
            o_ref, acc_ref = rest
        part = _dot(a_ref[...], b_ref[...], mode)
        if nk == 1:
            res = part if add is None else part + add_ref[...]
            o_ref[...] = res.astype(out_dtype)
            return
        kk = pl.program_id(2)

        @pl.when(kk == 0)
        def _():
            acc_ref[...] = part

        @pl.when(kk > 0)
        def _():
            acc_ref[...] += part

        @pl.when(kk == nk - 1)
        def _():
            res = acc_ref[...] if add is None else acc_ref[...] + add_ref[...]
            o_ref[...] = res.astype(out_dtype)

    in_specs = [a_spec, b_spec] + ([o_spec] if add is not None else [])
    args = (a, b) + ((add,) if add is not None else ())
    res = hosted_call(
        body, comm, name, (m // tm, n // tn, nk), in_specs, [o_spec], [jax.ShapeDtypeStruct((m, n), out_dtype)],
        [pltpu.VMEM((tm, tn) if nk > 1 else (8, LANES), F32)], args, ("parallel", "parallel", "arbitrary"))
    return res[0] if comm is None else res


def matmul_rows(a, b, name, fn, extra, outs, accs=(), add=None, comm=None, tm=512):
    (m, k), n = a.shape, b.shape[1]
    n_fixed = 2 + (add is not None)
    row_spec = lambda cols: pl.BlockSpec((tm, cols), lambda i: (i, 0))
    in_specs = [row_spec(k), pl.BlockSpec((k, n), lambda i: (0, 0))] + ([row_spec(n)] if add is not None else [])
    in_specs += [pl.BlockSpec(e.shape, lambda i: (0, 0)) if e.shape[0] == 1 else row_spec(e.shape[1]) for e in extra]
    out_specs = [row_spec(c) for c, _ in outs] + [pl.BlockSpec((1, c), lambda i: (0, 0)) for c in accs]
    out_shape = [jax.ShapeDtypeStruct((m, c), dt) for c, dt in outs] + [jax.ShapeDtypeStruct((1, c), F32) for c in accs]

    def body(*refs):
        rows = _dot(refs[0][...], refs[1][...], "nn")
        if add is not None:
            rows = rows + refs[2][...]
        n_in = n_fixed + len(extra)
        res = fn(rows, *[r[...] for r in refs[n_fixed:n_in]])
        for r, v in zip(refs[n_in:n_in + len(outs)], res[:len(outs)]):
            r[...] = v.astype(r.dtype)
        first = pl.program_id(0) == 0
        for r, v in zip(refs[n_in + len(outs):], res[len(outs):]):
            @pl.when(first)
            def _(r=r, v=v):
                r[...] = v

            @pl.when(jnp.logical_not(first))
            def _(r=r, v=v):
                r[...] += v

    args = (a, b) + ((add,) if add is not None else ()) + tuple(extra)
    return hosted_call(body, comm, name, (m // tm,), in_specs, out_specs, out_shape, [], args, ("arbitrary",))


def _merge_specs(tm):
    half = lambda blk: pl.BlockSpec((tm, D_MODEL), functools.partial(lambda i, blk_: (i, blk_), blk_=blk))
    return [half(0), half(1), pl.BlockSpec((tm, GROUP_W), lambda i: (i, 0)),
            pl.BlockSpec((GROUP_W, D_MODEL), lambda i: (0, 0)), pl.BlockSpec((tm, SSM_W), lambda i: (i, 0)),
            pl.BlockSpec((SSM_W, GATE_W), lambda i: (0, 0))]


def _merge_operands(g0, g1, at, wa, yg, wg):
    z = _dot(yg[...], wg[...], "nn")
    return (g0[...].astype(F32), g1[...].astype(F32), _dot(at[...], wa[...], "nn"), z[:, :D_MODEL], z[:, D_MODEL:])


def merge_out_proj(gl, attn_b, w_attn_out, yg, w_glu, w_out, x, g_ffn):
    tm = 512

    def body(g0, g1, at, wa, yg_ref, wg, w_ref, x_ref, g_ref, m_ref, x1_ref, h2_ref):
        merged = _merge_fn(*_merge_operands(g0, g1, at, wa, yg_ref, wg)).astype(BF16)
        m_ref[...] = merged
        x1 = _dot(merged, w_ref[...], "nn") + x_ref[...]
        x1_ref[...] = x1
        h2_ref[...] = _rms(x1, g_ref[...]).astype(BF16)

    rows = pl.BlockSpec((tm, D_MODEL), lambda i: (i, 0))
    whole = pl.BlockSpec((D_MODEL, D_MODEL), lambda i: (0, 0))
    gain = pl.BlockSpec((1, D_MODEL), lambda i: (0, 0))
    tok = lambda dt: jax.ShapeDtypeStruct((TOKENS, D_MODEL), dt)
    return pl.pallas_call(
        body, name="merge_out_proj", grid=(TOKENS // tm,), in_specs=_merge_specs(tm) + [whole, rows, gain],
        out_specs=[rows] * 3, out_shape=[tok(BF16), tok(F32), tok(BF16)], compiler_params=_params(("parallel",)),
    )(gl, gl, attn_b, w_attn_out, yg, w_glu, w_out, x, g_ffn)


def merge_bwd(dx1_b, w_out, gl, attn_b, w_attn_out, yg, w_glu, comm=None):
    tm = 512

    def body(dx_ref, w_ref, g0, g1, at, wa, yg_ref, wg, dgl_ref, dad_ref, dz_ref, dat_ref, dyg_ref):
        dm = _dot(dx_ref[...], w_ref[...], "nt")
        _, vjp = jax.vjp(_merge_fn, *_merge_operands(g0, g1, at, wa, yg_ref, wg))
        dg0, dg1, dad, dza, dzb = vjp(dm)
        dat_ref[...] = _dot(dad.astype(BF16), wa[...], "nt")
        dgl_ref[:, :D_MODEL] = dg0.astype(BF16)
        dgl_ref[:, D_MODEL:] = dg1.astype(BF16)
        dad_ref[...] = dad.astype(BF16)
        dz_ref[:, :D_MODEL] = dza.astype(BF16)
        dz_ref[:, D_MODEL:] = dzb.astype(BF16)
        dyg_ref[...] = _dot(dz_ref[...], wg[...], "nt")

    rows = pl.BlockSpec((tm, D_MODEL), lambda i: (i, 0))
    wide = pl.BlockSpec((tm, GATE_W), lambda i: (i, 0))
    whole = pl.BlockSpec((D_MODEL, D_MODEL), lambda i: (0, 0))
    return hosted_call(
        body, comm, "merge_bwd", (TOKENS // tm,), [rows, whole] + _merge_specs(tm),
        [wide, rows, wide, pl.BlockSpec((tm, GROUP_W), lambda i: (i, 0)), pl.BlockSpec((tm, SSM_W), lambda i: (i, 0))],
        [jax.ShapeDtypeStruct((TOKENS, GATE_W), BF16), jax.ShapeDtypeStruct((TOKENS, D_MODEL), BF16),
         jax.ShapeDtypeStruct((TOKENS, GATE_W), BF16), jax.ShapeDtypeStruct((TOKENS, GROUP_W), F32),
         jax.ShapeDtypeStruct((TOKENS, SSM_W), F32)], [],
        (dx1_b, w_out, gl, gl, attn_b, w_attn_out, yg, w_glu), ("arbitrary",))


FFN_TM, FFN_TN = 512, 1408


def ffn_in(h2, wg_t, wu_t, comm=None):
    def body(h_ref, wg_ref, wu_ref, a_ref, b_ref, act_ref):
        hv = h_ref[...]
        a, b = _dot(hv, wg_ref[...], "nt"), _dot(hv, wu_ref[...], "nt")
        a_ref[...] = a.astype(BF16)
        b_ref[...] = b.astype(BF16)
        act_ref[...] = _swiglu_fn(a, b).astype(BF16)

    rows = pl.BlockSpec((FFN_TM, D_MODEL), lambda i, j: (i, 0))
    wts = pl.BlockSpec((FFN_TN, D_MODEL), lambda i, j: (j, 0))
    out = pl.BlockSpec((FFN_TM, FFN_TN), lambda i, j: (i, j))
    return hosted_call(body, comm, "ffn_in", (TOKENS // FFN_TM, D_FF // FFN_TN), [rows, wts, wts], [out] * 3,
                       [jax.ShapeDtypeStruct((TOKENS, D_FF), BF16)] * 3, [], (h2, wg_t, wu_t),
                       ("parallel", "parallel"))


def ffn_in_bwd(dx2_b, wd, a, b):
    def body(dx_ref, wd_ref, a_ref, b_ref, da_ref, db_ref):
        dx = dx_ref[...]
        for lo in range(0, FFN_TN, 512):
            cols = slice(lo, min(lo + 512, FFN_TN))
            dact = _dot(dx, wd_ref[cols, :], "nt")
            _, vjp = jax.vjp(_swiglu_fn, a_ref[:, cols].astype(F32), b_ref[:, cols].astype(F32))
            da, db = vjp(dact)
            da_ref[:, cols] = da.astype(BF16)
            db_ref[:, cols] = db.astype(BF16)

    rows = pl.BlockSpec((FFN_TM, D_MODEL), lambda i, j: (i, 0))
    wts = pl.BlockSpec((FFN_TN, D_MODEL), lambda i, j: (j, 0))
    out = pl.BlockSpec((FFN_TM, FFN_TN), lambda i, j: (i, j))
    return pl.pallas_call(
        body, name="ffn_in_bwd", grid=(TOKENS // FFN_TM, D_FF // FFN_TN), in_specs=[rows, wts, out, out],
        out_specs=[out] * 2, out_shape=[jax.ShapeDtypeStruct((TOKENS, D_FF), BF16)] * 2,
        compiler_params=_params(("parallel", "parallel")),
    )(dx2_b, wd, a, b)


def mix_in_bwd(grads, weights, partial, x, g, skip, comm=None):
    n = len(grads)
    tm = 512

    def body(*refs):
        a_refs, b_refs = refs[:n], refs[n:2 * n]
        part_ref, x_ref, g_ref, skip_ref, gx_ref, dg_ref = refs[2 * n:]
        dh = part_ref[...]
        for a_ref, b_ref in zip(a_refs, b_refs):
            dh = dh + _dot(a_ref[...], b_ref[...], "nn")
        _, vjp = jax.vjp(_rms, x_ref[...], g_ref[...])
        dx, dg = vjp(dh)
        gx_ref[...] = dx + skip_ref[...]
        first = pl.program_id(0) == 0

        @pl.when(first)
        def _():
            dg_ref[...] = dg

        @pl.when(jnp.logical_not(first))
        def _():
            dg_ref[...] += dg

    rows = pl.BlockSpec((tm, D_MODEL), lambda i: (i, 0))
    gain = pl.BlockSpec((1, D_MODEL), lambda i: (0, 0))
    in_specs = [pl.BlockSpec((tm, a.shape[1]), lambda i: (i, 0)) for a in grads]
    in_specs += [pl.BlockSpec(b.shape, lambda i: (0, 0)) for b in weights]
    return hosted_call(
        body, comm, "mix_in_bwd", (TOKENS // tm,), in_specs + [rows, rows, gain, rows], [rows, gain],
        [jax.ShapeDtypeStruct((TOKENS, D_MODEL), F32), jax.ShapeDtypeStruct((1, D_MODEL), F32)], [],
        (*grads, *weights, partial, x, g, skip), ("arbitrary",))


def rowwise(fn, ins, outs, name, accs=(), tm=256, rows=TOKENS, comm=None):
    in_specs, args = [], []
    for item in ins:
        arr, width, blk = item if isinstance(item, tuple) else (item, None, 0)
        if arr.ndim == 3:
            for k in range(arr.shape[0]):
                in_specs.append(pl.BlockSpec((None, tm, arr.shape[2]), functools.partial(lambda i, k_: (k_, i, 0), k_=k)))
                args.append(arr)
            continue
        if arr.shape[0] == 1:
            in_specs.append(pl.BlockSpec(arr.shape, lambda i: (0, 0)))
        elif width is None:
            in_specs.append(pl.BlockSpec((tm, arr.shape[1]), lambda i: (i, 0)))
        else:
            in_specs.append(pl.BlockSpec((tm, width), functools.partial(lambda i, blk_: (i, blk_), blk_=blk)))
        args.append(arr)
    out_specs = [pl.BlockSpec((tm, c), lambda i: (i, 0)) for c, _ in outs]
    out_specs += [pl.BlockSpec((1, c), lambda i: (0, 0)) for c in accs]
    out_shape = [jax.ShapeDtypeStruct((rows, c), dt) for c, dt in outs]
    out_shape += [jax.ShapeDtypeStruct((1, c), F32) for c in accs]
    n_in, n_out = len(args), len(outs)
    c_ins, c_outs, c_sems = _comm_operands(comm)

    def body(*refs):
        refs, c_refs = _comm_refs(comm, refs, n_in, n_out + len(accs))
        step = pl.program_id(0)
        _comm_begin(comm, c_refs, step, rows // tm)
        res = fn(*[r[...] for r in refs[:n_in]])
        for r, v in zip(refs[n_in:n_in + n_out], res[:n_out]):
            r[...] = v.astype(r.dtype)
        first = step == 0
        for r, v in zip(refs[n_in + n_out:], res[n_out:]):
            @pl.when(first)
            def _(r=r, v=v):
                r[...] = v

            @pl.when(jnp.logical_not(first))
            def _(r=r, v=v):
                r[...] += v
        _comm_end(comm, c_refs, step, rows // tm)

    return pl.pallas_call(
        body, name=name, grid=(rows // tm,), in_specs=in_specs + [ANY] * len(c_ins),
        out_specs=out_specs + [ANY] * len(c_outs), out_shape=out_shape + c_outs, scratch_shapes=c_sems,
        compiler_params=_params(("arbitrary",)),
    )(*args, *c_ins)


def first_norm(x, g, others, comm=None):
    tm, n = 256, len(others)

    def body(x_ref, g_ref, *rest):
        srcs, h_ref, dsts = rest[:n], rest[n], rest[n + 1:]
        h_ref[...] = _rms(x_ref[...], g_ref[...]).astype(BF16)
        for k, (s, d) in enumerate(zip(srcs, dsts)):
            @pl.when(pl.program_id(0) == k)
            def _(s=s, d=d):
                d[...] = s[...].astype(BF16)

    rows = pl.BlockSpec((tm, D_MODEL), lambda i: (i, 0))
    whole = [pl.BlockSpec(a.shape, lambda i: (0, 0)) for a in others]
    return hosted_call(
        body, comm, "norm_mix", (TOKENS // tm,), [rows, pl.BlockSpec((1, D_MODEL), lambda i: (0, 0))] + whole,
        [rows] + whole, [jax.ShapeDtypeStruct((TOKENS, D_MODEL), BF16)]
        + [jax.ShapeDtypeStruct(a.shape, BF16) for a in others], [], (x, g, *others), ("arbitrary",))


def _rms(x, g):
    return x * lax.rsqrt(jnp.mean(x * x, axis=-1, keepdims=True) + RMS_EPS) * g


def _colsum(v):
    return jnp.sum(v, axis=0, keepdims=True)


PAIR_W = 2 * HEAD_DIM
N_PAIRS = HEADS_PER_GROUP // 2


def _qkv_order(w_t, back=False):
    dims = (N_PAIRS, N_GROUPS, 3) if back else (3, N_GROUPS, N_PAIRS)
    return w_t.reshape(dims + (PAIR_W, w_t.shape[1])).transpose(2, 1, 0, 3, 4).reshape(QKV_W, w_t.shape[1])


def _rope_tables():
    half = ROPE_DIM // 2
    inv = np.power(np.float32(ROPE_THETA), -np.arange(half, dtype=np.float32) * np.float32(2.0 / ROPE_DIM))
    ang = (np.arange(SEQ, dtype=np.float32)[:, None] * inv[None, :]).astype(np.float32)
    cos, sin = np.cos(ang), np.sin(ang)
    zeros = np.zeros((SEQ, HEAD_DIM - ROPE_DIM), np.float32)
    zh = np.zeros((SEQ, half), np.float32)
    c = np.concatenate([cos, cos, zeros + 1.0], axis=1)
    sa = np.concatenate([-sin, zh, zeros], axis=1)
    sb = np.concatenate([zh, sin, zeros], axis=1)
    return [jnp.asarray(np.tile(t, (1, 2)), F32) for t in (c, sa, sb)]


def _rope_fwd(x, c, sa, sb):
    return x * c + pltpu.roll(x, PAIR_W - 8, 1) * sa + pltpu.roll(x, 8, 1) * sb


def _rope_bwd(dy, c, sa, sb):
    return dy * c + pltpu.roll(dy * sb, PAIR_W - 8, 1) + pltpu.roll(dy * sa, 8, 1)


def _band_masks():
    row = lax.broadcasted_iota(jnp.int32, (ATTN_BLOCK, ATTN_BLOCK), 0)
    col = lax.broadcasted_iota(jnp.int32, (ATTN_BLOCK, ATTN_BLOCK), 1)
    return col <= row, col >= row


def _stack_rows(t):
    return jnp.concatenate([t, t], axis=0)


def _stack_heads(t, first_head):
    return jnp.concatenate([jnp.where(first_head, t, 0), jnp.where(first_head, 0, t)], axis=0)


def _per_head(fn):
    return jnp.concatenate([fn(slice(h * HEAD_DIM, (h + 1) * HEAD_DIM)) for h in range(2)], axis=1)


def _slab_spec(kind):
    return pl.BlockSpec((None, SEQ, PAIR_W), lambda b, p, g: (b, 0, p * 3 * N_GROUPS + g * 3 + kind))


_TABLE_SPEC = pl.BlockSpec((SEQ, PAIR_W), lambda b, p, g: (0, 0))
_PAIR_SPEC = pl.BlockSpec((None, SEQ, PAIR_W), lambda b, p, g: (b, 0, p))


def _block_rows(dil, r, n):
    return pl.ds(n * (ATTN_BLOCK * dil) + r, ATTN_BLOCK, stride=dil)


def proj_qkv(h, w_qkv_t, tables, comm=None):
    tm = 1024
    pair_w = QKV_W // N_PAIRS
    scale = HEAD_DIM ** -0.5

    def body(h_ref, w_ref, c_ref, sa_ref, sb_ref, o_ref):
        rows = _dot(h_ref[...], w_ref[...], "nt")
        c, sa, sb = c_ref[...], sa_ref[...], sb_ref[...]
        for blk in range(pair_w // PAIR_W):
            cols = slice(blk * PAIR_W, (blk + 1) * PAIR_W)
            x = rows[:, cols]
            if blk % 3 == 0:
                x = _rope_fwd(x, c, sa, sb) * scale
            elif blk % 3 == 1:
                x = _rope_fwd(x, c, sa, sb)
            o_ref[:, cols] = x

    table = pl.BlockSpec((tm, PAIR_W), lambda i, j, : (i % (SEQ // tm), 0))
    res = hosted_call(
        body, comm, "proj_qkv", (TOKENS // tm, N_PAIRS),
        [pl.BlockSpec((tm, D_MODEL), lambda i, j: (i, 0)), pl.BlockSpec((pair_w, D_MODEL), lambda i, j: (j, 0)),
         table, table, table],
        [pl.BlockSpec((tm, pair_w), lambda i, j: (i, j))], [jax.ShapeDtypeStruct((TOKENS, QKV_W), F32)], [],
        (h, w_qkv_t, *tables), ("parallel", "parallel"))
    return res[0] if comm is None else res


def attn_fwd(qkv, comm=None):
    def body(qs, ks, v_ref, attn_b_ref, attn_ref, lse_ref, o0, o1, o2, l0, l1, l2):
        g = pl.program_id(2)
        cur_mask, prev_mask = _band_masks()
        first_head = lax.broadcasted_iota(jnp.int32, (ATTN_BLOCK, PAIR_W), 1) < HEAD_DIM

        def run(dil, o_slab, l_slab):
            nb = SEQ // dil // ATTN_BLOCK

            def block(idx, carry):
                r, n = lax.div(idx, nb), lax.rem(idx, nb)
                cur, prev = _block_rows(dil, r, n), _block_rows(dil, r, jnp.maximum(n - 1, 0))
                q = qs[cur, :].astype(BF16)
                kc, kp = ks[cur, :].astype(BF16), ks[prev, :].astype(BF16)
                vc, vp = v_ref[cur, :].astype(BF16), v_ref[prev, :].astype(BF16)
                q2 = _stack_heads(q, first_head)
                mask = _stack_rows(jnp.concatenate([jnp.logical_and(prev_mask, n > 0), cur_mask], axis=1))
                s2 = jnp.where(mask, _dot(q2, jnp.concatenate([kp, kc], axis=0), "nt"), NEG_INF)
                m = jnp.max(s2, axis=-1, keepdims=True)
                vcat, two = jnp.concatenate([vp, vc], axis=0), _stack_rows(first_head)
                vext = jnp.concatenate([jnp.where(two, vcat, 1), jnp.where(two, 1, vcat)], axis=1)
                r2 = _dot(jnp.exp(s2 - m).astype(BF16), vext, "nn")
                r0, r1 = r2[:ATTN_BLOCK, :PAIR_W], r2[ATTN_BLOCK:, PAIR_W:]
                num = jnp.where(first_head, r0, r1)
                den = pltpu.roll(jnp.where(first_head, r1, r0), HEAD_DIM, 1)
                o_slab[cur, :] = num / den
                l_slab[cur, :] = jnp.where(first_head, m[:ATTN_BLOCK], m[ATTN_BLOCK:]) + jnp.log(den)
                return carry

            lax.fori_loop(0, SEQ // ATTN_BLOCK, block, 0, unroll=4)

        for gi, (o_slab, l_slab) in enumerate(((o0, l0), (o1, l1), (o2, l2))):
            @pl.when(g == gi)
            def _(gi=gi, o_slab=o_slab, l_slab=l_slab):
                run(DILATIONS[gi], o_slab, l_slab)

        @pl.when(g == N_GROUPS - 1)
        def _():
            a, b, cc = l0[...], l1[...], l2[...]
            m = jnp.maximum(jnp.maximum(a, b), cc)
            e0, e1, e2 = jnp.exp(a - m), jnp.exp(b - m), jnp.exp(cc - m)
            tot = e0 + e1 + e2
            attn = (e0 * o0[...] + e1 * o1[...] + e2 * o2[...]) / tot
            attn_ref[...] = attn
            attn_b_ref[...] = attn.astype(BF16)
            lse_ref[...] = m + jnp.log(tot)

    shape = (LOCAL_BATCH, SEQ, GROUP_W)
    slab = pltpu.VMEM((SEQ, PAIR_W), F32)
    return hosted_call(
        body, comm, "attn_fwd", (LOCAL_BATCH, N_PAIRS, N_GROUPS),
        [_slab_spec(0), _slab_spec(1), _slab_spec(2)], [_PAIR_SPEC] * 3,
        [jax.ShapeDtypeStruct(shape, BF16), jax.ShapeDtypeStruct(shape, F32), jax.ShapeDtypeStruct(shape, F32)],
        [slab] * 6, (qkv, qkv, qkv), ("parallel", "parallel", "arbitrary"))


def attn_bwd(qkv, tables, dattn, attn, lse, comm=None):
    scale = HEAD_DIM ** -0.5

    def body(qs, ks, v_ref, c_ref, sa_ref, sb_ref, do_ref, out_ref, lse_ref, dqkv_ref, dl, dq_s, dk_s, dv_s):
        g = pl.program_id(2)
        c, sa, sb = c_ref[...], sa_ref[...], sb_ref[...]

        @pl.when(g == 0)
        def _():
            prod = do_ref[...] * out_ref[...]
            dl[...] = _per_head(
                lambda sl: jnp.broadcast_to(jnp.sum(prod[:, sl], axis=-1, keepdims=True), (SEQ, HEAD_DIM)))

        cur_mask, prev_mask = _band_masks()
        first_head = lax.broadcasted_iota(jnp.int32, (ATTN_BLOCK, PAIR_W), 1) < HEAD_DIM

        def run(dil):
            nb = SEQ // dil // ATTN_BLOCK

            def block(idx, carry):
                r, n = lax.div(idx, nb), lax.rem(idx, nb)
                cur = _block_rows(dil, r, n)
                prev = _block_rows(dil, r, jnp.maximum(n - 1, 0))
                nxt = _block_rows(dil, r, jnp.minimum(n + 1, nb - 1))
                q0, q1 = qs[cur, :].astype(BF16), qs[nxt, :].astype(BF16)
                kp, kc = ks[prev, :].astype(BF16), ks[cur, :].astype(BF16)
                vp, vc = v_ref[prev, :].astype(BF16), v_ref[cur, :].astype(BF16)
                do0, do1 = do_ref[cur, :].astype(BF16), do_ref[nxt, :].astype(BF16)
                lse0, lse1, dl0, dl1 = lse_ref[cur, :], lse_ref[nxt, :], dl[cur, :], dl[nxt, :]
                has_prev = jnp.logical_and(prev_mask, n > 0)
                has_next = jnp.logical_and(prev_mask, n < nb - 1)

                def per_row(t):
                    return jnp.concatenate([t[:, 0:1], t[:, HEAD_DIM:HEAD_DIM + 1]], axis=0)

                q20, q21 = _stack_heads(q0, first_head), _stack_heads(q1, first_head)
                do20, do21 = _stack_heads(do0, first_head), _stack_heads(do1, first_head)
                kcat, vcat = jnp.concatenate([kp, kc], axis=0), jnp.concatenate([vp, vc], axis=0)
                mask0 = _stack_rows(jnp.concatenate([has_prev, cur_mask], axis=1))
                p0 = jnp.where(mask0, jnp.exp(_dot(q20, kcat, "nt") - per_row(lse0)), 0.0)
                ds0 = (p0 * (_dot(do20, vcat, "nt") - per_row(dl0))).astype(BF16)
                p1 = jnp.where(_stack_rows(has_next), jnp.exp(_dot(q21, kc, "nt") - per_row(lse1)), 0.0)
                ds1 = (p1 * (_dot(do21, vc, "nt") - per_row(dl1))).astype(BF16)
                dq2 = _dot(ds0, kcat, "nn")
                dq_s[cur, :] = jnp.where(first_head, dq2[:ATTN_BLOCK], dq2[ATTN_BLOCK:])
                ds_cur = jnp.concatenate([ds0[:, ATTN_BLOCK:], ds1], axis=0)
                p_cur = jnp.concatenate([p0[:, ATTN_BLOCK:], p1], axis=0).astype(BF16)
                dk_s[cur, :] = _dot(ds_cur, jnp.concatenate([q20, q21], axis=0), "tn")
                dv_s[cur, :] = _dot(p_cur, jnp.concatenate([do20, do21], axis=0), "tn")
                return carry

            lax.fori_loop(0, SEQ // ATTN_BLOCK, block, 0, unroll=2)

        for gi in range(N_GROUPS):
            @pl.when(g == gi)
            def _(gi=gi):
                run(DILATIONS[gi])

        dqkv_ref[:, 0:PAIR_W] = _rope_bwd(dq_s[...] * scale, c, sa, sb).astype(BF16)
        dqkv_ref[:, PAIR_W:2 * PAIR_W] = _rope_bwd(dk_s[...], c, sa, sb).astype(BF16)
        dqkv_ref[:, 2 * PAIR_W:] = dv_s[...].astype(BF16)

    slab = pltpu.VMEM((SEQ, PAIR_W), F32)
    return hosted_call(
        body, comm, "attn_bwd", (LOCAL_BATCH, N_PAIRS, N_GROUPS),
        [_slab_spec(0), _slab_spec(1), _slab_spec(2), _TABLE_SPEC, _TABLE_SPEC, _TABLE_SPEC,
         _PAIR_SPEC, _PAIR_SPEC, _PAIR_SPEC],
        [pl.BlockSpec((None, SEQ, 3 * PAIR_W), lambda b, p, g: (b, 0, p * N_GROUPS + g))],
        [jax.ShapeDtypeStruct((LOCAL_BATCH, SEQ, QKV_W), BF16)],
        [slab] * 4, (qkv, qkv, qkv, *tables, dattn, attn, lse), ("parallel", "parallel", "arbitrary"))


def _discretize(lr, li, log_dt, br, bi):
    dt = jnp.exp(log_dt)
    mag = jnp.exp(lr * dt)
    ab_re, ab_im = mag * jnp.cos(li * dt), mag * jnp.sin(li * dt)
    den = lr * lr + li * li
    nr, ni = ab_re - 1.0, ab_im
    f_re = (nr * lr + ni * li) / den
    f_im = (ni * lr - nr * li) / den
    return ab_re, ab_im, f_re[None] * br - f_im[None] * bi, f_re[None] * bi + f_im[None] * br


def ssm_prep(lr, li, log_dt, br, bi):
    def body(lr_ref, li_ref, dt_ref, br_ref, bi_ref, *outs):
        for o, v in zip(outs, _discretize(lr_ref[...], li_ref[...], dt_ref[...], br_ref[...], bi_ref[...])):
            o[...] = v
    shapes = [lr, li, br, bi]
    return pl.pallas_call(body, name="ssm_prep",
                          out_shape=[jax.ShapeDtypeStruct(s.shape, F32) for s in shapes])(lr, li, log_dt, br, bi)


def ssm_prep_bwd(lr, li, log_dt, br, bi, g_ab_re, g_ab_im, g_bb_re, g_bb_im):
    def body(lr_ref, li_ref, dt_ref, br_ref, bi_ref, g0, g1, g2, g3, *outs):
        _, vjp = jax.vjp(_discretize, lr_ref[...], li_ref[...], dt_ref[...], br_ref[...], bi_ref[...])
        for o, v in zip(outs, vjp((g0[...], g1[...], g2[...], g3[...]))):
            o[...] = v
    shapes = [lr, li, log_dt, br, bi]
    return pl.pallas_call(body, name="ssm_prep_bwd",
                          out_shape=[jax.ShapeDtypeStruct(s.shape, F32) for s in shapes])(
        lr, li, log_dt, br, bi, g_ab_re, g_ab_im, g_bb_re, g_bb_im)


def _block_diag(t):
    per = SSM_STATE_W // SSM_LANE_BLOCKS // 64
    g = t.transpose(1, 0, 2).reshape(SSM_LANE_BLOCKS, per, 16, 64)
    eye = jnp.eye(per, dtype=t.dtype)
    return jnp.einsum("jgcn,gh->jgchn", g, eye).reshape(SSM_LANE_BLOCKS, per * 16, per * 64)


def _block_diag_t(m):
    per = SSM_STATE_W // SSM_LANE_BLOCKS // 64
    m5 = m.reshape(SSM_LANE_BLOCKS, per, 16, per, 64)
    d = jnp.einsum("jgchn,gh->jgcn", m5, jnp.eye(per, dtype=m.dtype))
    return d.reshape(SSM_LANE_BLOCKS * per, 16, 64).transpose(1, 0, 2)


def _cmul(ar, ai, br, bi):
    return ar * br - ai * bi, ar * bi + ai * br


def _power_tables(ar, ai, reverse):
    width = ar.shape[1]
    row = lax.broadcasted_iota(jnp.int32, (8, width), 0)
    pows = [(ar, ai)]
    for _ in range(7):
        pows.append(_cmul(pows[-1][0], pows[-1][1], ar, ai))
    steps = []
    for k in (1, 2, 4):
        keep = (row >= k) if not reverse else (row < 8 - k)
        steps.append((jnp.where(keep, pows[k - 1][0], 0.0), jnp.where(keep, pows[k - 1][1], 0.0)))
    cr = jnp.zeros((8, width), F32)
    ci = jnp.zeros((8, width), F32)
    for i in range(8):
        pr, pi = pows[i] if not reverse else pows[7 - i]
        cr = jnp.where(row == i, pr, cr)
        ci = jnp.where(row == i, pi, ci)
    return steps, (cr, ci)


SCAN_CHUNK = 2048
STATE_BLOCK = SSM_STATE_W // SSM_LANE_BLOCKS
CHAN_BLOCK = SSM_W // SSM_LANE_BLOCKS


def ssm_fwd(u, ab_re, ab_im, bb_re, bb_im, cb_re, cb_im, d_skip, comm=None):
    nt = SEQ // SCAN_CHUNK
    chan = pl.BlockSpec((None, SCAN_CHUNK, CHAN_BLOCK), lambda b, j, t: (b, t, j))
    state = pl.BlockSpec((None, SCAN_CHUNK, STATE_BLOCK), lambda b, j, t: (b, t, j))
    mat = pl.BlockSpec((None, CHAN_BLOCK, STATE_BLOCK), lambda b, j, t: (j, 0, 0))
    lane = pl.BlockSpec((1, STATE_BLOCK), lambda b, j, t: (0, j))
    dsp = pl.BlockSpec((1, CHAN_BLOCK), lambda b, j, t: (0, j))

    def body(u_ref, ar_ref, ai_ref, bbr_ref, bbi_ref, cbr_ref, cbi_ref, d_ref, y_ref, yg_ref, xr_ref, xi_ref,
             car_r, car_i):
        @pl.when(pl.program_id(2) == 0)
        def _():
            car_r[...] = jnp.zeros_like(car_r)
            car_i[...] = jnp.zeros_like(car_i)

        steps, (pr, pi) = _power_tables(ar_ref[...], ai_ref[...], reverse=False)
        uf = u_ref[...]
        ub = uf.astype(BF16)
        xr_ref[...] = _dot(ub, bbr_ref[...], "nn")
        xi_ref[...] = _dot(ub, bbi_ref[...], "nn")

        def tile(i, carry):
            cr, ci = carry
            sl = pl.ds(pl.multiple_of(i * 8, 8), 8)
            br, bi = xr_ref[sl, :], xi_ref[sl, :]
            for k, (sr, si) in zip((1, 2, 4), steps):
                tr, ti = _cmul(sr, si, pltpu.roll(br, k, 0), pltpu.roll(bi, k, 0))
                br, bi = br + tr, bi + ti
            tr, ti = _cmul(pr, pi, cr, ci)
            br, bi = br + tr, bi + ti
            xr_ref[sl, :] = br
            xi_ref[sl, :] = bi
            return br[7:8, :], bi[7:8, :]

        cr, ci = lax.fori_loop(0, SCAN_CHUNK // 8, tile, (car_r[0:1, :], car_i[0:1, :]), unroll=4)
        car_r[0:1, :] = cr
        car_i[0:1, :] = ci
        y = (_dot(xr_ref[...].astype(BF16), cbr_ref[...], "nt") - _dot(xi_ref[...].astype(BF16), cbi_ref[...], "nt")
             + d_ref[...] * uf)
        y_ref[...] = y
        yg_ref[...] = jax.nn.gelu(y).astype(BF16)

    return hosted_call(
        body, comm, "ssm_fwd", (LOCAL_BATCH, SSM_LANE_BLOCKS, nt),
        [chan, lane, lane, mat, mat, mat, mat, dsp], [chan, chan, state, state],
        [jax.ShapeDtypeStruct((LOCAL_BATCH, SEQ, SSM_W), F32), jax.ShapeDtypeStruct((LOCAL_BATCH, SEQ, SSM_W), BF16),
         jax.ShapeDtypeStruct((LOCAL_BATCH, SEQ, SSM_STATE_W), F32),
         jax.ShapeDtypeStruct((LOCAL_BATCH, SEQ, SSM_STATE_W), F32)],
        [pltpu.VMEM((8, STATE_BLOCK), F32), pltpu.VMEM((8, STATE_BLOCK), F32)],
        (u, ab_re, ab_im, bb_re, bb_im, cb_re, cb_im, d_skip), ("parallel", "parallel", "arbitrary"))


def ssm_bwd(dyg, y, u, xr, xi, ab_re, ab_im, bb_re, bb_im, cb_re, cb_im, d_skip, comm=None):
    nt = SEQ // SCAN_CHUNK
    ntile = SCAN_CHUNK // 8

    def rev(t):
        return nt - 1 - t

    chan = pl.BlockSpec((None, SCAN_CHUNK, CHAN_BLOCK), lambda j, b, t: (b, rev(t), j))
    state = pl.BlockSpec((None, SCAN_CHUNK, STATE_BLOCK), lambda j, b, t: (b, rev(t), j))
    before = pl.BlockSpec((None, 8, STATE_BLOCK), lambda j, b, t: (b, jnp.maximum(rev(t) * ntile - 1, 0), j))
    mat = pl.BlockSpec((None, CHAN_BLOCK, STATE_BLOCK), lambda j, b, t: (j, 0, 0))
    lane = pl.BlockSpec((1, STATE_BLOCK), lambda j, b, t: (0, j))
    lane8 = pl.BlockSpec((8, STATE_BLOCK), lambda j, b, t: (0, j))
    dsp = pl.BlockSpec((1, CHAN_BLOCK), lambda j, b, t: (0, j))

    def body(dyg_ref, y_ref, u_ref, xr_ref, xi_ref, xrb_ref, xib_ref, ar_ref, ai_ref, bbr_ref, bbi_ref, cbr_ref,
             cbi_ref, d_ref, du_ref, dcbr_ref, dcbi_ref, dbbr_ref, dbbi_ref, dd_ref, dar_ref, dai_ref,
             lam_r, lam_i, car_r, car_i):
        b, t = pl.program_id(1), pl.program_id(2)
        first = jnp.logical_and(b == 0, t == 0)

        @pl.when(t == 0)
        def _():
            car_r[...] = jnp.zeros_like(car_r)
            car_i[...] = jnp.zeros_like(car_i)

        @pl.when(first)
        def _():
            for r in (dcbr_ref, dcbi_ref, dbbr_ref, dbbi_ref, dd_ref, dar_ref, dai_ref):
                r[...] = jnp.zeros_like(r)

        steps, (pr, pi) = _power_tables(ar_ref[...], -ai_ref[...], reverse=True)
        uf = u_ref[...]
        _, gelu_vjp = jax.vjp(jax.nn.gelu, y_ref[...])
        dy = gelu_vjp(dyg_ref[...])[0]
        dyb = dy.astype(BF16)
        dd_ref[...] += _colsum(dy * uf)
        lam_r[...] = _dot(dyb, cbr_ref[...], "nn")
        lam_i[...] = -_dot(dyb, cbi_ref[...], "nn")
        dcbr_ref[...] += _dot(dyb, xr_ref[...].astype(BF16), "tn")
        dcbi_ref[...] -= _dot(dyb, xi_ref[...].astype(BF16), "tn")
        row0 = lax.broadcasted_iota(jnp.int32, (8, STATE_BLOCK), 0) == 0
        has_before = rev(t) > 0
        xrb = jnp.where(has_before, xrb_ref[...], 0.0)
        xib = jnp.where(has_before, xib_ref[...], 0.0)

        def tile(s, carry):
            cr, ci, acc_r, acc_i = carry
            i = ntile - 1 - s
            sl = pl.ds(pl.multiple_of(i * 8, 8), 8)
            gr, gi = lam_r[sl, :], lam_i[sl, :]
            for k, (sr, si) in zip((1, 2, 4), steps):
                tr, ti = _cmul(sr, si, pltpu.roll(gr, 8 - k, 0), pltpu.roll(gi, 8 - k, 0))
                gr, gi = gr + tr, gi + ti
            tr, ti = _cmul(pr, pi, cr, ci)
            gr, gi = gr + tr, gi + ti
            lam_r[sl, :] = gr
            lam_i[sl, :] = gi
            sp = pl.ds(pl.multiple_of(jnp.maximum(i - 1, 0) * 8, 8), 8)
            pvr = jnp.where(i > 0, xr_ref[sp, :], xrb)
            pvi = jnp.where(i > 0, xi_ref[sp, :], xib)
            xsr = jnp.where(row0, pltpu.roll(pvr, 1, 0), pltpu.roll(xr_ref[sl, :], 1, 0))
            xsi = jnp.where(row0, pltpu.roll(pvi, 1, 0), pltpu.roll(xi_ref[sl, :], 1, 0))
            acc_r = acc_r + xsr * gr + xsi * gi
            acc_i = acc_i + xsr * gi - xsi * gr
            return gr[0:1, :], gi[0:1, :], acc_r, acc_i

        zero = jnp.zeros((8, STATE_BLOCK), F32)
        cr, ci, acc_r, acc_i = lax.fori_loop(0, ntile, tile, (car_r[0:1, :], car_i[0:1, :], zero, zero), unroll=2)
        car_r[0:1, :] = cr
        car_i[0:1, :] = ci
        dar_ref[...] += acc_r
        dai_ref[...] += acc_i
        lrb, lib = lam_r[...].astype(BF16), lam_i[...].astype(BF16)
        du = _dot(lrb, bbr_ref[...], "nt") + _dot(lib, bbi_ref[...], "nt") + d_ref[...] * dy
        du_ref[...] = du.astype(BF16)
        ub = uf.astype(BF16)
        dbbr_ref[...] += _dot(ub, lrb, "tn")
        dbbi_ref[...] += _dot(ub, lib, "tn")

    mat_shape = jax.ShapeDtypeStruct((SSM_LANE_BLOCKS, CHAN_BLOCK, STATE_BLOCK), F32)
    return hosted_call(
        body, comm, "ssm_bwd", (SSM_LANE_BLOCKS, LOCAL_BATCH, nt),
        [chan, chan, chan, state, state, before, before, lane, lane, mat, mat, mat, mat, dsp],
        [chan, mat, mat, mat, mat, dsp, lane8, lane8],
        [jax.ShapeDtypeStruct((LOCAL_BATCH, SEQ, SSM_W), BF16), mat_shape, mat_shape, mat_shape, mat_shape,
         jax.ShapeDtypeStruct((1, SSM_W), F32), jax.ShapeDtypeStruct((8, SSM_STATE_W), F32),
         jax.ShapeDtypeStruct((8, SSM_STATE_W), F32)],
        [pltpu.VMEM((SCAN_CHUNK, STATE_BLOCK), F32), pltpu.VMEM((SCAN_CHUNK, STATE_BLOCK), F32),
         pltpu.VMEM((8, STATE_BLOCK), F32), pltpu.VMEM((8, STATE_BLOCK), F32)],
        (dyg, y, u, xr, xi, xr, xi, ab_re, ab_im, bb_re, bb_im, cb_re, cb_im, d_skip),
        ("parallel", "arbitrary", "arbitrary"))


def _merge_fn(g0, g1, attn_d, za, zb):
    return jax.nn.sigmoid(g0) * attn_d + jax.nn.sigmoid(g1) * (za * jax.nn.sigmoid(zb))


def _swiglu_fn(a, b):
    return jax.nn.silu(a) * b


def _own_slot(slots, shard):
    me = 2 * lax.axis_index("x") + lax.axis_index("y")
    mine = lax.broadcasted_iota(jnp.int32, (N_CHIPS, 1, 1), 0) == me
    return jnp.where(mine, shard[None], slots)


def _reduce_start(names, gw, shard_shapes):
    return swap_comm([_to_slots(n, gw[n], shard_shapes[n]) for n in names])


def _reduce_chip(names, swap, got, core):
    return exchange_comm([add_halves(n, g, r, core) for n, g, r in zip(names, swap.ins, got)])


def local_step(x, target, shards, small, core):
    g_mix, g_ffn, g_final = small["norm_mix_g"], small["norm_ffn_g"], small["norm_final_g"]
    tables = _rope_tables()
    seqs = lambda t: t.reshape(LOCAL_BATCH, SEQ, t.shape[-1])
    toks = lambda t: t.reshape(TOKENS, t.shape[-1])
    shard_shapes = {n: s.shape for n, s in shards.items()}
    w = {}

    def gather(names):
        return gather_comm([shards[n] for n in names])

    def arrived(names, slots, own=None):
        for n, s in zip(names, slots):
            w[n] = _from_slots(n, s if own is None else _own_slot(s, own))

    later = [n for n in BIG if n != "w_in"]
    sems, w_in_shard, land, token = split_start(shards["w_in"].astype(BF16), "w_in_gather_start")
    zero = token[0, 0]
    h, *rest = first_norm(x, g_mix + zero, [shards[n] for n in later])
    shards = dict(shards)
    shards.update(zip(later, rest))
    br_t = small["ssm_b_re"].transpose(2, 0, 1)
    bi_t = small["ssm_b_im"].transpose(2, 0, 1)
    log_dt = small["ssm_log_dt"].reshape(32, 1)
    ab_re, ab_im, bb_re_t, bb_im_t = ssm_prep(small["ssm_a_re"] + zero, small["ssm_a_im"], log_dt, br_t, bi_t)
    ab = [ab_re.reshape(1, SSM_STATE_W), ab_im.reshape(1, SSM_STATE_W)]
    bb = [_block_diag(bb_re_t).astype(BF16), _block_diag(bb_im_t).astype(BF16)]
    cb = [_block_diag((small["ssm_c_re"] + zero).transpose(1, 0, 2)).astype(BF16),
          _block_diag((small["ssm_c_im"] + zero).transpose(1, 0, 2)).astype(BF16)]
    d_skip = small["ssm_d"].reshape(1, SSM_W)
    w_in_shard, land = split_wait(sems, w_in_shard, land, [h] + bb + cb, "w_in_gather_wait")
    arrived(["w_in"], [handover(land, "w_in_handover")], own=w_in_shard)
    w_qkv, w_u, w_gate = _qkv_order(w["w_in"][:QKV_W]), w["w_in"][QKV_W:QKV_W + SSM_W], w["w_in"][QKV_W + SSM_W:]
    qkv, *slots = proj_qkv(h, w_qkv, tables, comm=gather(["w_attn_out", "w_glu"]))
    arrived(["w_attn_out", "w_glu"], slots)
    qkv = seqs(qkv)
    u = seqs(matmul(h, w_u, "nt", F32, "proj_u"))
    gl, *slots = matmul(h, w_gate, "nt", BF16, "proj_gate", comm=gather(["w_out"]))
    arrived(["w_out"], slots)
    attn_b, attn, lse, *slots = attn_fwd(qkv, comm=gather(["w_ffn_gate"]))
    arrived(["w_ffn_gate"], slots)
    attn_b = toks(attn_b)
    y, yg, xr, xi, *slots = ssm_fwd(u, *ab, *bb, *cb, d_skip, comm=gather(["w_ffn_up"]))
    arrived(["w_ffn_up"], slots)
    yg2 = toks(yg)
    merged, x1, h2 = merge_out_proj(gl, attn_b, w["w_attn_out"], yg2, w["w_glu"], w["w_out"], x, g_ffn)
    a, b, act, *slots = ffn_in(h2, w["w_ffn_gate"], w["w_ffn_up"], comm=gather(["w_ffn_down"]))
    arrived(["w_ffn_down"], slots)

    def final_fn(xv, g, tgt):
        yv, vjp = jax.vjp(_rms, xv, g)
        err = yv - tgt
        dx, dg = vjp(err * (1.0 / D_MODEL))
        loss = 0.5 * jnp.sum(jnp.mean(err * err, axis=-1, keepdims=True), axis=0, keepdims=True)
        return dx, dx, dg, jnp.broadcast_to(loss, (1, LANES))

    dx2, dx2_b, dg_final, loss = matmul_rows(act, w["w_ffn_down"], "ffn_down_loss", final_fn, [g_final, target],
                                             [(D_MODEL, F32), (D_MODEL, BF16)], accs=(D_MODEL, LANES), add=x1)
    gw, parts = {}, {}
    gw["w_ffn_down"] = matmul(act, dx2_b, "tn", F32, "d_ffn_down")
    da_b, db_b = ffn_in_bwd(dx2_b, w["w_ffn_down"], a, b)
    gw["w_ffn_gate"] = matmul(da_b, h2, "tn", F32, "d_ffn_gate")
    gw["w_ffn_up"] = matmul(db_b, h2, "tn", F32, "d_ffn_up")
    ffn = ["w_ffn_down", "w_ffn_gate", "w_ffn_up"]
    swap = _reduce_start(ffn[:2], gw, shard_shapes)
    dh2, *got = matmul(da_b, w["w_ffn_gate"], "nn", F32, "d_h2_gate", comm=swap)
    ffn_exchange = [_reduce_chip(ffn[:2], swap, got, core)]
    swap = _reduce_start(ffn[2:], gw, shard_shapes)

    def norm_bwd(dh, xv, g, skip):
        _, vjp = jax.vjp(_rms, xv, g)
        dx, dg = vjp(dh)
        dx = dx + skip
        return dx, dx, dg

    dx1, dx1_b, dg_ffn, *got = matmul_rows(db_b, w["w_ffn_up"], "d_h2_up_norm", norm_bwd, [x1, g_ffn, dx2],
                                           [(D_MODEL, F32), (D_MODEL, BF16)], accs=(D_MODEL,), add=dh2, comm=swap)
    ffn_up_exchange = _reduce_chip(ffn[2:], swap, got, core)
    gw["w_out"] = matmul(merged, dx1_b, "tn", F32, "d_out")
    dgl_b, dattn_d_b, dz_b, dattn, dyg, parts["w_ffn_up"] = merge_bwd(
        dx1_b, w["w_out"], gl, attn_b, w["w_attn_out"], yg2, w["w_glu"], comm=ffn_up_exchange)
    dattn, dyg = seqs(dattn), seqs(dyg)
    gw["w_attn_out"] = matmul(attn_b, dattn_d_b, "tn", F32, "d_attn_out")
    gw["w_glu"] = matmul(yg2, dz_b, "tn", F32, "d_glu")
    mixer = ["w_out", "w_attn_out", "w_glu"]
    swap = _reduce_start(mixer, gw, shard_shapes)
    du_b, dcb_re, dcb_im, dbb_re, dbb_im, dd, da_re8, da_im8, *rest = ssm_bwd(
        dyg, y, u, xr, xi, *ab, *bb, *cb, d_skip, comm=join_comms(ffn_exchange + [swap]))
    for n, p in zip(ffn[:2], rest[:2]):
        parts[n] = p
    mixer_exchange = _reduce_chip(mixer, swap, rest[2:], core)
    du_b = toks(du_b)
    g_ab_re = jnp.sum(da_re8, axis=0).reshape(32, 64)
    g_ab_im = jnp.sum(da_im8, axis=0).reshape(32, 64)
    d_lr, d_li, d_ldt, d_br_t, d_bi_t = ssm_prep_bwd(
        small["ssm_a_re"], small["ssm_a_im"], log_dt, br_t, bi_t,
        g_ab_re, g_ab_im, _block_diag_t(dbb_re), _block_diag_t(dbb_im))
    as_gcn = lambda t: t.transpose(1, 0, 2).reshape(SSM_W, 64)
    gs = {
        "ssm_a_re": d_lr, "ssm_a_im": d_li, "ssm_log_dt": d_ldt.reshape(1, 32),
        "ssm_b_re": as_gcn(d_br_t), "ssm_b_im": as_gcn(d_bi_t),
        "ssm_c_re": as_gcn(_block_diag_t(dcb_re)), "ssm_c_im": as_gcn(_block_diag_t(dcb_im)),
        "ssm_d": dd.reshape(32, 16).T,
    }
    ssm_gather = small_comm([gs[n] for n in SSM_SMALL])
    dqkv_b, *rest = attn_bwd(qkv, tables, dattn, attn, lse, comm=join_comms([mixer_exchange, ssm_gather]))
    for n, p in zip(mixer, rest):
        parts[n] = p
    ssm_shares = rest[len(mixer):]
    dqkv_b = toks(dqkv_b)
    d_qkv = matmul(dqkv_b, h, "tn", F32, "d_w_qkv")
    d_u = matmul(du_b, h, "tn", F32, "d_w_u")
    d_gate = matmul(dgl_b, h, "tn", F32, "d_w_gate")
    gw["w_in"] = jnp.concatenate([_qkv_order(d_qkv, back=True), d_u, d_gate], axis=0)
    swap = _reduce_start(["w_in"], gw, shard_shapes)
    dh, *got = matmul(dqkv_b, w_qkv, "nn", F32, "d_h_qkv", comm=swap)
    chip_sum = add_halves("w_in", swap.ins[0], got[0], core)
    sems, chip_sum, land, token = split_start(chip_sum, "w_in_reduce_start", per_chip=True)
    grad_x, dg_mix = mix_in_bwd([du_b, dgl_b], [w_u, w_gate], dh, x, g_mix + token[0, 0], dx1)
    gs_norm = {"norm_mix_g": dg_mix, "norm_ffn_g": dg_ffn, "norm_final_g": dg_final}
    return loss, grad_x, parts, ssm_shares, gs_norm, (sems, chip_sum, land)


ANY = pl.BlockSpec(memory_space=pl.ANY)
BIG = ("w_in", "w_glu", "w_attn_out", "w_out", "w_ffn_gate", "w_ffn_up", "w_ffn_down")
TRANSPOSED = ("w_in", "w_ffn_gate", "w_ffn_up")
ROW_SHARDED = TRANSPOSED + ("w_out", "w_ffn_down")
SMALL = ("norm_mix_g", "ssm_a_re", "ssm_a_im", "ssm_log_dt", "ssm_b_re", "ssm_b_im", "ssm_c_re", "ssm_c_im",
         "ssm_d", "norm_ffn_g", "norm_final_g")
WEIGHTS = ("norm_mix_g", "w_in", "ssm_a_re", "ssm_a_im", "ssm_log_dt", "ssm_b_re", "ssm_b_im", "ssm_c_re",
           "ssm_c_im", "ssm_d", "w_glu", "w_attn_out", "w_out", "norm_ffn_g", "w_ffn_gate", "w_ffn_up",
           "w_ffn_down", "norm_final_g")
SSM_SMALL = SMALL[1:9]
NORM_SMALL = (SMALL[0],) + SMALL[9:]
NORM_ROWS = 32
N_BIG = len(BIG)


def _position():
    return lax.axis_index("x"), lax.axis_index("y"), lax.axis_index("c")


def _other_chips(x, y):
    return [(1 - x, y), (x, 1 - y), (1 - x, 1 - y)]


def _remote(src, dst, send_sem, recv_sem, device):
    return pltpu.make_async_remote_copy(src_ref=src, dst_ref=dst, send_sem=send_sem, recv_sem=recv_sem,
                                        device_id=device, device_id_type=MESH)


_later = functools.partial


def _two_level_phases(copies):
    def first(*refs):
        locals_, sends, _, _, _ = copies(*refs)
        for cp in locals_ + sends:
            cp().start()

    def mid(*refs):
        _, _, arrived, passed, _ = copies(*refs)
        for got, cp in zip(arrived, passed):
            got().wait_recv()
            cp().start()

    def last(*refs):
        locals_, sends, _, passed, from_sibling = copies(*refs)
        for cp in from_sibling:
            cp().wait_recv()
        for cp in sends + passed:
            cp().wait_send()
        for cp in locals_:
            cp().wait()

    return first, mid, last


def _half(ref, chip, which):
    rows = ref.shape[1] // 2
    return ref.at[chip, pl.ds(which * rows, rows), :]


class Comm:
    def __init__(self, ins, out_shapes, sems, first, mid, last):
        self.ins, self.out_shapes, self.sems = list(ins), list(out_shapes), list(sems)
        self.first, self.mid, self.last = first, mid, last


def join_comms(comms):
    def cut(refs_by_kind):
        offs, parts = [0, 0, 0], []
        for cm in comms:
            sizes = (len(cm.ins), len(cm.out_shapes), len(cm.sems))
            parts.append(tuple(refs_by_kind[k][offs[k]:offs[k] + sizes[k]] for k in range(3)))
            offs = [o + s for o, s in zip(offs, sizes)]
        return parts

    def phase(which):
        def run(ins, outs, sems):
            for cm, part in zip(comms, cut((ins, outs, sems))):
                fn = getattr(cm, which)
                if fn is not None:
                    fn(*part)
        return run

    return Comm(sum((cm.ins for cm in comms), []), sum((cm.out_shapes for cm in comms), []),
                sum((cm.sems for cm in comms), []), phase("first"), phase("mid"), phase("last"))


def _comm_operands(comm):
    if comm is None:
        return [], [], []
    return comm.ins, comm.out_shapes, comm.sems


def _comm_begin(comm, refs, step, n_steps):
    if comm is None:
        return
    pl.when(step == 0)(lambda: comm.first(*refs))
    if comm.mid is not None:
        pl.when(step == (n_steps * 3) // 4)(lambda: comm.mid(*refs))


def _comm_end(comm, refs, step, n_steps):
    if comm is not None:
        pl.when(step == n_steps - 1)(lambda: comm.last(*refs))


def _comm_refs(comm, refs, n_in, n_out):
    if comm is None:
        return list(refs), None
    ci, co, cs = len(comm.ins), len(comm.out_shapes), len(comm.sems)
    o0 = n_in + ci
    s0 = o0 + n_out + co
    host = list(refs[:n_in]) + list(refs[o0:o0 + n_out]) + list(refs[s0:len(refs) - cs])
    return host, (list(refs[n_in:o0]), list(refs[o0 + n_out:s0]), list(refs[len(refs) - cs:]))


def run_comm(comm, name):
    n_in, n_out = len(comm.ins), len(comm.out_shapes)

    def body(*refs):
        parts = (list(refs[:n_in]), list(refs[n_in:n_in + n_out]), list(refs[n_in + n_out:]))
        comm.first(*parts)
        if comm.mid is not None:
            comm.mid(*parts)
        comm.last(*parts)

    return pl.pallas_call(body, name=name, in_specs=[ANY] * n_in, out_specs=[ANY] * n_out,
                          out_shape=comm.out_shapes, scratch_shapes=comm.sems)(*comm.ins)


def hosted_call(work, comm, name, grid, in_specs, out_specs, out_shape, scratch_shapes, args, semantics):
    c_ins, c_outs, c_sems = _comm_operands(comm)
    n_steps = math.prod(grid)

    def body(*refs):
        host, c_refs = _comm_refs(comm, refs, len(in_specs), len(out_specs))
        step = 0
        for axis, size in enumerate(grid):
            step = step * size + pl.program_id(axis)
        _comm_begin(comm, c_refs, step, n_steps)
        work(*host)
        _comm_end(comm, c_refs, step, n_steps)

    return pl.pallas_call(
        body, name=name, grid=grid, in_specs=list(in_specs) + [ANY] * len(c_ins),
        out_specs=list(out_specs) + [ANY] * len(c_outs), out_shape=list(out_shape) + c_outs,
        scratch_shapes=list(scratch_shapes) + c_sems,
        compiler_params=_params(semantics if comm is None else ("arbitrary",) * len(grid)),
    )(*args, *c_ins)


def gather_comm(shards):
    n = len(shards)

    def copies(srcs, outs, sems):
        send_sems, recv_sems, local_sems = sems
        x, y, c = _position()
        me = 2 * x + y
        sibling = (x, y, 1 - c)
        chips = _other_chips(x, y)
        locals_ = [_later(pltpu.make_async_copy, s, o.at[me], local_sems.at[i])
                   for i, (s, o) in enumerate(zip(srcs, outs))]
        sends, arrived, passed, from_sibling = [], [], [], []
        for j, (px, py) in enumerate(chips):
            for i, (s, o) in enumerate(zip(srcs, outs)):
                rows = s.shape[0] // 2
                sends.append(_later(_remote, s.at[pl.ds(c * rows, rows), :], _half(o, me, c), send_sems.at[i, j],
                                    recv_sems.at[i, j], (px, py, c)))
                got = _half(o, 2 * px + py, c)
                arrived.append(_later(_remote, got, got, send_sems.at[i, j], recv_sems.at[i, j], (px, py, c)))
                passed.append(_later(_remote, got, got, send_sems.at[i, 3 + j], recv_sems.at[i, 3 + j], sibling))
                other = _half(o, 2 * px + py, 1 - c)
                from_sibling.append(_later(_remote, other, other, send_sems.at[i, 3 + j], recv_sems.at[i, 3 + j],
                                           sibling))
        return locals_, sends, arrived, passed, from_sibling

    return Comm(shards, [jax.ShapeDtypeStruct((N_CHIPS,) + s.shape, s.dtype) for s in shards],
                [pltpu.SemaphoreType.DMA((n, 6)), pltpu.SemaphoreType.DMA((n, 6)), pltpu.SemaphoreType.DMA((n,))],
                *_two_level_phases(copies))


HBM = pl.BlockSpec(memory_space=pltpu.HBM)
SEM = pl.BlockSpec(memory_space=pltpu.SEMAPHORE)
N_OTHER = N_CHIPS - 1


def _ici_halves(src_ref, land_ref, sems, per_chip):
    x, y, c = _position()
    me = 2 * x + y
    rows = land_ref.shape[1] // 2
    sends, arrivals = [], []
    for j, (px, py) in enumerate(_other_chips(x, y)):
        piece = src_ref.at[2 * px + py] if per_chip else src_ref.at[pl.ds(c * rows, rows), :]
        sends.append(_later(_remote, piece, _half(land_ref, me, c), sems[j], sems[N_OTHER + j], (px, py, c)))
        got = _half(land_ref, 2 * px + py, c)
        arrivals.append(_later(_remote, got, got, sems[j], sems[N_OTHER + j], (px, py, c)))
    return sends, arrivals


def split_start(src, name, per_chip=False):
    def body(src_ref, land_ref, *rest):
        sems, token = rest[:2 * N_OTHER], rest[-1]
        for cp in _ici_halves(src_ref, land_ref, sems, per_chip)[0]:
            cp().start()
        token[...] = jnp.zeros_like(token)

    rows, cols = (2 * src.shape[1], src.shape[2]) if per_chip else src.shape
    sem = pltpu.SemaphoreType.DMA(())
    land = (N_CHIPS, rows, cols)
    res = pl.pallas_call(
        body, name=name, in_specs=(HBM, HBM),
        out_specs=(SEM,) * (2 * N_OTHER) + (HBM, HBM, pl.BlockSpec(memory_space=pltpu.VMEM)),
        out_shape=(sem,) * (2 * N_OTHER) + (pltpu.HBM(src.shape, src.dtype), pltpu.HBM(land, src.dtype),
                                           jax.ShapeDtypeStruct((8, LANES), F32)),
        input_output_aliases={0: 2 * N_OTHER, 1: 2 * N_OTHER + 1},
        compiler_params=pltpu.CompilerParams(has_side_effects=pltpu.SideEffectType.DATAFLOW_SIDE_EFFECTING),
    )(pltpu.with_memory_space_constraint(src, pltpu.HBM),
      pltpu.with_memory_space_constraint(lax.empty(land, src.dtype), pltpu.HBM))
    return res[:2 * N_OTHER], res[2 * N_OTHER], res[2 * N_OTHER + 1], res[-1]


def split_wait(sems, src, land, after, name, per_chip=False):
    def body(src_ref, land_ref, *rest):
        sends, arrivals = _ici_halves(src_ref, land_ref, rest[:2 * N_OTHER], per_chip)
        for cp in sends:
            cp().wait_send()
        for cp in arrivals:
            cp().wait_recv()

    return pl.pallas_call(
        body, name=name, in_specs=(HBM, HBM) + (SEM,) * (2 * N_OTHER) + (ANY,) * len(after),
        out_specs=(HBM, HBM), out_shape=(pltpu.HBM(src.shape, src.dtype), pltpu.HBM(land.shape, land.dtype)),
        input_output_aliases={0: 0, 1: 1},
        compiler_params=pltpu.CompilerParams(has_side_effects=pltpu.SideEffectType.DATAFLOW_SIDE_EFFECTING),
    )(src, land, *sems, *after)


def handover(land, name, sums=None):
    n = N_OTHER + (sums is not None)

    def body(*refs):
        land_ref, send_sems, recv_sems = refs[0], refs[-2], refs[-1]
        x, y, c = _position()
        me = 2 * x + y
        sibling = (x, y, 1 - c)
        pieces = [(_half(land_ref, 2 * px + py, c), 2 * px + py) for px, py in _other_chips(x, y)]
        if sums is not None:
            pieces.append((refs[1].at[me], me))
        sends = [_remote(piece, _half(land_ref, chip, c), send_sems.at[j], recv_sems.at[j], sibling)
                 for j, (piece, chip) in enumerate(pieces)]
        for cp in sends:
            cp.start()
        for j, (_, chip) in enumerate(pieces):
            other = _half(land_ref, chip, 1 - c)
            _remote(other, other, send_sems.at[j], recv_sems.at[j], sibling).wait_recv()
        for cp in sends:
            cp.wait_send()

    args = (land,) + ((sums,) if sums is not None else ())
    return pl.pallas_call(
        body, name=name, in_specs=[ANY] * len(args), out_specs=ANY,
        out_shape=jax.ShapeDtypeStruct(land.shape, land.dtype), input_output_aliases={0: 0},
        scratch_shapes=[pltpu.SemaphoreType.DMA((n,)), pltpu.SemaphoreType.DMA((n,))],
    )(*args)


def swap_comm(grads):
    n = len(grads)

    def copies(srcs, gots, sems):
        send_sems, recv_sems = sems
        x, y, c = _position()
        out = []
        for i, (s, o) in enumerate(zip(srcs, gots)):
            rows = s.shape[1] // 2
            out.append(_remote(s.at[:, pl.ds((1 - c) * rows, rows), :], o, send_sems.at[i], recv_sems.at[i],
                               (x, y, 1 - c)))
        return out

    def first(srcs, gots, sems):
        for cp in copies(srcs, gots, sems):
            cp.start()

    def last(srcs, gots, sems):
        for cp in copies(srcs, gots, sems):
            cp.wait()

    return Comm(grads, [jax.ShapeDtypeStruct((N_CHIPS, g.shape[1] // 2, g.shape[2]), g.dtype) for g in grads],
                [pltpu.SemaphoreType.DMA((n,)), pltpu.SemaphoreType.DMA((n,))], first, None, last)


def add_halves(name, g, got, core):
    _, half, cols = got.shape
    mine = pl.BlockSpec((None, half, cols), lambda k, c_ref: (k, c_ref[0], 0))
    other = pl.BlockSpec((None, half, cols), lambda k, c_ref: (k, 0, 0))

    def body(c_ref, g_ref, got_ref, o_ref):
        o_ref[...] = (g_ref[...] + got_ref[...]).astype(BF16)

    return pl.pallas_call(
        body, name="add_halves_" + name,
        grid_spec=pltpu.PrefetchScalarGridSpec(num_scalar_prefetch=1, grid=(N_CHIPS,), in_specs=[mine, other],
                                               out_specs=other),
        out_shape=jax.ShapeDtypeStruct(got.shape, BF16),
        compiler_params=_params(("parallel",)),
    )(core, g, got)


def exchange_comm(parts):
    n = len(parts)

    def copies(srcs, outs, sems):
        send_sems, recv_sems, local_sems = sems
        x, y, c = _position()
        me = 2 * x + y
        sibling = (x, y, 1 - c)
        chips = _other_chips(x, y)
        locals_, sends, arrived, passed, from_sibling = [], [], [], [], []
        for i, (s, o) in enumerate(zip(srcs, outs)):
            locals_.append(_later(pltpu.make_async_copy, s.at[me], _half(o, me, c), local_sems.at[i]))
            sends.append(_later(_remote, s.at[me], _half(o, me, c), send_sems.at[i, 3], recv_sems.at[i, 3], sibling))
            other = _half(o, me, 1 - c)
            from_sibling.append(_later(_remote, other, other, send_sems.at[i, 3], recv_sems.at[i, 3], sibling))
        for j, (px, py) in enumerate(chips):
            for i, (s, o) in enumerate(zip(srcs, outs)):
                sends.append(_later(_remote, s.at[2 * px + py], _half(o, me, c), send_sems.at[i, j],
                                    recv_sems.at[i, j], (px, py, c)))
                got = _half(o, 2 * px + py, c)
                arrived.append(_later(_remote, got, got, send_sems.at[i, j], recv_sems.at[i, j], (px, py, c)))
                passed.append(_later(_remote, got, got, send_sems.at[i, 4 + j], recv_sems.at[i, 4 + j], sibling))
                other = _half(o, 2 * px + py, 1 - c)
                from_sibling.append(_later(_remote, other, other, send_sems.at[i, 4 + j], recv_sems.at[i, 4 + j],
                                           sibling))
        return locals_, sends, arrived, passed, from_sibling

    return Comm(parts, [jax.ShapeDtypeStruct((N_CHIPS, 2 * p.shape[1], p.shape[2]), p.dtype) for p in parts],
                [pltpu.SemaphoreType.DMA((n, 7)), pltpu.SemaphoreType.DMA((n, 7)), pltpu.SemaphoreType.DMA((n,))],
                *_two_level_phases(copies))


def small_comm(shares):
    n = len(shares)

    def copies(srcs, outs, sems):
        send_sems, recv_sems, local_sems = sems
        x, y, c = _position()
        me = 4 * x + 2 * y + c
        flips = [(fx, fy, fc) for fx in (0, 1) for fy in (0, 1) for fc in (0, 1)][1:]
        peers = [(1 - x if fx else x, 1 - y if fy else y, 1 - c if fc else c) for fx, fy, fc in flips]
        locals_, sends, arrived = [], [], []
        for i, (src_ref, out_ref) in enumerate(zip(srcs, outs)):
            locals_.append(_later(pltpu.make_async_copy, src_ref, out_ref.at[me], local_sems.at[i]))
            for j, (px, py, pc) in enumerate(peers):
                sends.append(_later(_remote, src_ref, out_ref.at[me], send_sems.at[i, j], recv_sems.at[i, j],
                                    (px, py, pc)))
                got = out_ref.at[4 * px + 2 * py + pc]
                arrived.append(_later(_remote, got, got, send_sems.at[i, j], recv_sems.at[i, j], (px, py, pc)))
        return locals_, sends, arrived

    def first(*refs):
        locals_, sends, _ = copies(*refs)
        for cp in locals_ + sends:
            cp().start()

    def last(*refs):
        locals_, sends, arrived = copies(*refs)
        for cp in arrived:
            cp().wait_recv()
        for cp in sends:
            cp().wait_send()
        for cp in locals_:
            cp().wait()

    return Comm(shares, [jax.ShapeDtypeStruct((N_DEV,) + s.shape, s.dtype) for s in shares],
                [pltpu.SemaphoreType.DMA((n, 7)), pltpu.SemaphoreType.DMA((n, 7)), pltpu.SemaphoreType.DMA((n,))],
                first, None, last)


def _adam_fn(w, g, m, v):
    m = ADAM_B1 * m + (1.0 - ADAM_B1) * g
    v = ADAM_B2 * v + (1.0 - ADAM_B2) * jnp.square(g)
    m_hat = m / (1.0 - ADAM_B1 ** ADAM_STEP)
    v_hat = v / (1.0 - ADAM_B2 ** ADAM_STEP)
    return -ADAM_LR * (m_hat / (jnp.sqrt(v_hat) + ADAM_EPS) + ADAM_WD * w), m, v


def adam_big(name, parts, w, m, v):
    rows, cols = w.shape
    tm = _pick(rows, 384, 16)

    def fn(p0, p1, p2, p3, wv, mv, vv):
        g = ((p0.astype(F32) + p1.astype(F32)) + p2.astype(F32)) + p3.astype(F32)
        return (g,) + _adam_fn(wv, g, mv, vv)

    return rowwise(fn, [parts, w, m, v], [(cols, F32)] * 4, "adam_" + name, tm=tm, rows=rows)


def adam_small(name, gathered, w, m, v):
    def body(g_ref, w_ref, m_ref, v_ref, go_ref, d_ref, mo_ref, vo_ref):
        g = g_ref[0]
        for k in range(1, N_DEV):
            g = g + g_ref[k]
        go_ref[...] = g
        d_ref[...], mo_ref[...], vo_ref[...] = _adam_fn(w_ref[...], g, m_ref[...], v_ref[...])

    return pl.pallas_call(body, name=name, out_shape=[jax.ShapeDtypeStruct(w.shape, F32)] * 4,
                          compiler_params=_params())(gathered, w, m, v)


def _ssm_2d(name, t):
    t = t[0] if t.ndim > 2 else t
    if name in ("ssm_b_re", "ssm_b_im"):
        return t.transpose(0, 2, 1).reshape(SSM_W, 64)
    if name in ("ssm_c_re", "ssm_c_im"):
        return t.reshape(SSM_W, 64)
    return t.T if name == "ssm_d" else t


def _ssm_back(name, t):
    if name in ("ssm_b_re", "ssm_b_im"):
        return t.reshape(32, 16, 64).transpose(0, 2, 1)[None]
    if name in ("ssm_c_re", "ssm_c_im"):
        return t.reshape(1, 32, 16, 64)
    if name == "ssm_d":
        return t.T[None]
    return t if name == "ssm_log_dt" else t[None]


def adam_ssm(shares, w, m, v):
    n = len(w)

    def body(*refs):
        ins, outs = refs[:4 * n], refs[4 * n:]
        for i in range(n):
            g_ref, w_ref, m_ref, v_ref = (ins[k * n + i] for k in range(4))
            g = g_ref[0]
            for k in range(1, N_DEV):
                g = g + g_ref[k]
            outs[4 * i][...] = g
            outs[4 * i + 1][...], outs[4 * i + 2][...], outs[4 * i + 3][...] = _adam_fn(w_ref[...], g, m_ref[...],
                                                                                      v_ref[...])

    out_shape = [jax.ShapeDtypeStruct(t.shape, F32) for t in w for _ in range(4)]
    res = pl.pallas_call(body, name="adam_ssm", out_shape=out_shape, compiler_params=_params())(*shares, *w, *m, *v)
    return [res[4 * i:4 * i + 4] for i in range(n)]


def _pack_small(names, vals, rows, last=None):
    flat = [vals[n].reshape(-1) for n in names]
    if last is not None:
        flat.append(last.reshape(-1))
    flat = jnp.concatenate(flat)
    return jnp.pad(flat, (0, rows * LANES - flat.shape[0])).reshape(rows, LANES)


def _unpack_small(names, pack, shapes):
    flat, out, off = pack.reshape(-1), {}, 0
    for n in names:
        size = math.prod(shapes[n])
        out[n] = flat[off:off + size].reshape(shapes[n])
        off += size
    return out, flat[off]


def _to_slots(name, g, shard_shape):
    rows, cols = shard_shape
    if name in ROW_SHARDED:
        return g.reshape(N_CHIPS, rows, cols)
    return g.reshape(rows, N_CHIPS, cols).transpose(1, 0, 2)


def _from_slots(name, s):
    _, rows, cols = s.shape
    if name in ROW_SHARDED:
        return s.reshape(N_CHIPS * rows, cols)
    return s.transpose(1, 0, 2).reshape(rows, N_CHIPS * cols)


def kernel(x, norm_mix_g, w_in, ssm_a_re, ssm_a_im, ssm_log_dt, ssm_b_re, ssm_b_im, ssm_c_re, ssm_c_im, ssm_d, w_glu, w_attn_out, w_out, norm_ffn_g, w_ffn_gate, w_ffn_up, w_ffn_down, norm_final_g, loss_target, m_norm_mix_g, m_w_in, m_ssm_a_re, m_ssm_a_im, m_ssm_log_dt, m_ssm_b_re, m_ssm_b_im, m_ssm_c_re, m_ssm_c_im, m_ssm_d, m_w_glu, m_w_attn_out, m_w_out, m_norm_ffn_g, m_w_ffn_gate, m_w_ffn_up, m_w_ffn_down, m_norm_final_g, v_norm_mix_g, v_w_in, v_ssm_a_re, v_ssm_a_im, v_ssm_log_dt, v_ssm_b_re, v_ssm_b_im, v_ssm_c_re, v_ssm_c_im, v_ssm_d, v_w_glu, v_w_attn_out, v_w_out, v_norm_ffn_g, v_w_ffn_gate, v_w_ffn_up, v_w_ffn_down, v_norm_final_g):
    given = dict(locals())
    def local(name, prefix=""):
        t = given[prefix + name][0]
        return t.T if name in TRANSPOSED else t

    shard = {n: local(n) for n in BIG}
    shapes = {n: given[n].shape for n in WEIGHTS}

    small = {n: given[n] for n in SMALL}
    small_2d = dict(small)
    for n in ("ssm_a_re", "ssm_a_im", "ssm_b_re", "ssm_b_im", "ssm_c_re", "ssm_c_im", "ssm_d"):
        small_2d[n] = small[n][0]
    small_2d["norm_final_g"] = norm_final_g.reshape(1, D_MODEL)

    core = lax.axis_index("c").astype(jnp.int32).reshape(1)
    loss, grad_x, parts, ssm_shares, gs_norm, w_in_reduce = local_step(
        x.reshape(TOKENS, D_MODEL), loss_target.reshape(TOKENS, D_MODEL),
        {n: shard[n] for n in BIG}, small_2d, core)

    (norm_shares,) = run_comm(small_comm([_pack_small(NORM_SMALL, gs_norm, NORM_ROWS, last=loss)]),
                              "gather_norm_grads")
    small_out = [{} for _ in range(4)]
    packs = [_pack_small(NORM_SMALL, {n: given[p + n] for n in NORM_SMALL}, NORM_ROWS) for p in ("", "m_", "v_")]
    for kind, t in enumerate(adam_small("adam_norm_gains", norm_shares, *packs)):
        vals, after = _unpack_small(NORM_SMALL, t, shapes)
        small_out[kind].update(vals)
        if kind == 0:
            total_loss = after
    ssm_in = [[_ssm_2d(n, given[p + n]) for n in SSM_SMALL] for p in ("", "m_", "v_")]
    for n, res in zip(SSM_SMALL, adam_ssm(ssm_shares, *ssm_in)):
        for kind, t in enumerate(res):
            small_out[kind][n] = _ssm_back(n, t)

    big_out, updated = {}, {}
    for n in BIG[1:] + BIG[:1]:
        if n == "w_in":
            sems, chip_sum, land = w_in_reduce
            behind = [updated[k][1] for k in BIG[1:]] + [norm_shares]
            chip_sum, land = split_wait(sems, chip_sum, land, behind, "w_in_reduce_wait", per_chip=True)
            land = handover(land, "w_in_reduce_handover", sums=chip_sum)
            me = 2 * lax.axis_index("x") + lax.axis_index("y")
            parts[n] = lax.dynamic_update_slice(land, lax.dynamic_slice_in_dim(chip_sum, me, 1, 0),
                                                (me, lax.axis_index("c") * chip_sum.shape[1], 0))
        updated[n] = adam_big(n, parts[n], shard[n], local(n, "m_"), local(n, "v_"))
        big_out[n] = [(t.T if n in TRANSPOSED else t)[None] for t in updated[n]]

    outs = [total_loss, grad_x.reshape(LOCAL_BATCH, SEQ, D_MODEL)]
    for kind in range(4):
        for n in WEIGHTS:
            outs.append(big_out[n][kind] if n in BIG else small_out[kind][n])
    return tuple(outs)
```

```python
import functools
import math

import jax
import jax.numpy as jnp
import numpy as np
from jax import lax
from jax.experimental import pallas as pl
from jax.experimental.pallas import tpu as pltpu

F32 = jnp.float32
BF16 = jnp.bfloat16
MESH = pl.DeviceIdType.MESH

D_MODEL = 1024
SEQ = 2048
LOCAL_BATCH = 2
TOKENS = LOCAL_BATCH * SEQ
HEAD_DIM = 64
HEADS_PER_GROUP = 4
GROUP_W = HEADS_PER_GROUP * HEAD_DIM
N_GROUPS = 3
DILATIONS = (1, 4, 16)
ATTN_BLOCK = 128
ROPE_DIM = 16
ROPE_THETA = 500000.0
QKV_W = 3 * N_GROUPS * GROUP_W
SSM_W = 512
SSM_STATE_W = 2048
SSM_LANE_BLOCKS = 4
GATE_W = 2 * D_MODEL
D_FF = 2816
RMS_EPS = 1e-6
NEG_INF = -1e30
ADAM_LR, ADAM_B1, ADAM_B2, ADAM_EPS, ADAM_WD, ADAM_STEP = 0.001, 0.9, 0.999, 1e-08, 0.01, 10
N_CHIPS = 4
N_DEV = 8

VMEM_LIMIT = 56 * 1024 * 1024
LANES = 128


def _params(sem=None):
    return pltpu.CompilerParams(dimension_semantics=sem, vmem_limit_bytes=VMEM_LIMIT)


def _pick(n, cap, align=LANES):
    best = None
    for d in range(align, min(n, cap) + 1, align):
        if n % d == 0:
            best = d
    return n if best is None or n <= cap else best


_DIMS = {"nn": (((1,), (0,)), ((), ())), "nt": (((1,), (1,)), ((), ())), "tn": (((0,), (0,)), ((), ()))}


def _dot(a, b, mode):
    return lax.dot_general(a, b, _DIMS[mode], preferred_element_type=F32)


def matmul(a, b, mode, out_dtype, name, add=None, comm=None):
    if mode == "nn":
        (m, k), n = a.shape, b.shape[1]
    elif mode == "nt":
        (m, k), n = a.shape, b.shape[0]
    else:
        (k, m), n = a.shape, b.shape[1]
    tn = _pick(n, 1408 if mode != "tn" else 512)
    tk = _pick(k, 2816) if mode != "tn" else k
    tm = _pick(m, 1408)
    out_bytes = jnp.dtype(out_dtype).itemsize

    def need(tm_):
        return 2 * 2 * (tm_ * tk + tk * tn) + tm_ * tn * (4 + 2 * out_bytes + (8 if add is not None else 0))

    while need(tm) > 40 * 1024 * 1024 and tm % 256 == 0:
        tm //= 2
    nk = k // tk
    a_spec = {"nn": pl.BlockSpec((tm, tk), lambda i, j, kk: (i, kk)),
              "nt": pl.BlockSpec((tm, tk), lambda i, j, kk: (i, kk)),
              "tn": pl.BlockSpec((tk, tm), lambda i, j, kk: (kk, i))}[mode]
    b_spec = {"nn": pl.BlockSpec((tk, tn), lambda i, j, kk: (kk, j)),
              "nt": pl.BlockSpec((tn, tk), lambda i, j, kk: (j, kk)),
              "tn": pl.BlockSpec((tk, tn), lambda i, j, kk: (kk, j))}[mode]
    o_spec = pl.BlockSpec((tm, tn), lambda i, j, kk: (i, j))

    def body(a_ref, b_ref, *rest):
        if add is not None:
            add_ref, o_ref, acc_ref = rest
        else:
            o_ref, acc_ref = rest
        part = _dot(a_ref[...], b_ref[...], mode)
        if nk == 1:
            res = part if add is None else part + add_ref[...]
            o_ref[...] = res.astype(out_dtype)
            return
        kk = pl.program_id(2)

        @pl.when(kk == 0)
        def _():
            acc_ref[...] = part

        @pl.when(kk > 0)
        def _():
            acc_ref[...] += part

        @pl.when(kk == nk - 1)
        def _():
            res = acc_ref[...] if add is None else acc_ref[...] + add_ref[...]
            o_ref[...] = res.astype(out_dtype)

    in_specs = [a_spec, b_spec] + ([o_spec] if add is not None else [])
    args = (a, b) + ((add,) if add is not None else ())
    res = hosted_call(
        body, comm, name, (m // tm, n // tn, nk), in_specs, [o_spec], [jax.ShapeDtypeStruct((m, n), out_dtype)],
        [pltpu.VMEM((tm, tn) if nk > 1 else (8, LANES), F32)], args, ("parallel", "parallel", "arbitrary"))
    return res[0] if comm is None else res


def matmul_rows(a, b, name, fn, extra, outs, accs=(), add=None, comm=None, tm=512):
    (m, k), n = a.shape, b.shape[1]
    n_fixed = 2 + (add is not None)
    row_spec = lambda cols: pl.BlockSpec((tm, cols), lambda i: (i, 0))
    in_specs = [row_spec(k), pl.BlockSpec((k, n), lambda i: (0, 0))] + ([row_spec(n)] if add is not None else [])
    in_specs += [pl.BlockSpec(e.shape, lambda i: (0, 0)) if e.shape[0] == 1 else row_spec(e.shape[1]) for e in extra]
    out_specs = [row_spec(c) for c, _ in outs] + [pl.BlockSpec((1, c), lambda i: (0, 0)) for c in accs]
    out_shape = [jax.ShapeDtypeStruct((m, c), dt) for c, dt in outs] + [jax.ShapeDtypeStruct((1, c), F32) for c in accs]

    def body(*refs):
        rows = _dot(refs[0][...], refs[1][...], "nn")
        if add is not None:
            rows = rows + refs[2][...]
        n_in = n_fixed + len(extra)
        res = fn(rows, *[r[...] for r in refs[n_fixed:n_in]])
        for r, v in zip(refs[n_in:n_in + len(outs)], res[:len(outs)]):
            r[...] = v.astype(r.dtype)
        first = pl.program_id(0) == 0
        for r, v in zip(refs[n_in + len(outs):], res[len(outs):]):
            @pl.when(first)
            def _(r=r, v=v):
                r[...] = v

            @pl.when(jnp.logical_not(first))
            def _(r=r, v=v):
                r[...] += v

    args = (a, b) + ((add,) if add is not None else ()) + tuple(extra)
    return hosted_call(body, comm, name, (m // tm,), in_specs, out_specs, out_shape, [], args, ("arbitrary",))


def _merge_specs(tm):
    half = lambda blk: pl.BlockSpec((tm, D_MODEL), functools.partial(lambda i, blk_: (i, blk_), blk_=blk))
    return [half(0), half(1), pl.BlockSpec((tm, GROUP_W), lambda i: (i, 0)),
            pl.BlockSpec((GROUP_W, D_MODEL), lambda i: (0, 0)), pl.BlockSpec((tm, SSM_W), lambda i: (i, 0)),
            pl.BlockSpec((SSM_W, GATE_W), lambda i: (0, 0))]


def _merge_operands(g0, g1, at, wa, yg, wg):
    z = _dot(yg[...], wg[...], "nn")
    return (g0[...].astype(F32), g1[...].astype(F32), _dot(at[...], wa[...], "nn"), z[:, :D_MODEL], z[:, D_MODEL:])


def merge_out_proj(gl, attn_b, w_attn_out, yg, w_glu, w_out, x, g_ffn):
    tm = 512

    def body(g0, g1, at, wa, yg_ref, wg, w_ref, x_ref, g_ref, m_ref, x1_ref, h2_ref):
        merged = _merge_fn(*_merge_operands(g0, g1, at, wa, yg_ref, wg)).astype(BF16)
        m_ref[...] = merged
        x1 = _dot(merged, w_ref[...], "nn") + x_ref[...]
        x1_ref[...] = x1
        h2_ref[...] = _rms(x1, g_ref[...]).astype(BF16)

    rows = pl.BlockSpec((tm, D_MODEL), lambda i: (i, 0))
    whole = pl.BlockSpec((D_MODEL, D_MODEL), lambda i: (0, 0))
    gain = pl.BlockSpec((1, D_MODEL), lambda i: (0, 0))
    tok = lambda dt: jax.ShapeDtypeStruct((TOKENS, D_MODEL), dt)
    return pl.pallas_call(
        body, name="merge_out_proj", grid=(TOKENS // tm,), in_specs=_merge_specs(tm) + [whole, rows, gain],
        out_specs=[rows] * 3, out_shape=[tok(BF16), tok(F32), tok(BF16)], compiler_params=_params(("parallel",)),
    )(gl, gl, attn_b, w_attn_out, yg, w_glu, w_out, x, g_ffn)


def merge_bwd(dx1_b, w_out, gl, attn_b, w_attn_out, yg, w_glu, comm=None):
    tm = 512

    def body(dx_ref, w_ref, g0, g1, at, wa, yg_ref, wg, dgl_ref, dad_ref, dz_ref, dat_ref, dyg_ref):
        dm = _dot(dx_ref[...], w_ref[...], "nt")
        _, vjp = jax.vjp(_merge_fn, *_merge_operands(g0, g1, at, wa, yg_ref, wg))
        dg0, dg1, dad, dza, dzb = vjp(dm)
        dat_ref[...] = _dot(dad.astype(BF16), wa[...], "nt")
        dgl_ref[:, :D_MODEL] = dg0.astype(BF16)
        dgl_ref[:, D_MODEL:] = dg1.astype(BF16)
        dad_ref[...] = dad.astype(BF16)
        dz_ref[:, :D_MODEL] = dza.astype(BF16)
        dz_ref[:, D_MODEL:] = dzb.astype(BF16)
        dyg_ref[...] = _dot(dz_ref[...], wg[...], "nt")

    rows = pl.BlockSpec((tm, D_MODEL), lambda i: (i, 0))
    wide = pl.BlockSpec((tm, GATE_W), lambda i: (i, 0))
    whole = pl.BlockSpec((D_MODEL, D_MODEL), lambda i: (0, 0))
    return hosted_call(
        body, comm, "merge_bwd", (TOKENS // tm,), [rows, whole] + _merge_specs(tm),
        [wide, rows, wide, pl.BlockSpec((tm, GROUP_W), lambda i: (i, 0)), pl.BlockSpec((tm, SSM_W), lambda i: (i, 0))],
        [jax.ShapeDtypeStruct((TOKENS, GATE_W), BF16), jax.ShapeDtypeStruct((TOKENS, D_MODEL), BF16),
         jax.ShapeDtypeStruct((TOKENS, GATE_W), BF16), jax.ShapeDtypeStruct((TOKENS, GROUP_W), F32),
         jax.ShapeDtypeStruct((TOKENS, SSM_W), F32)], [],
        (dx1_b, w_out, gl, gl, attn_b, w_attn_out, yg, w_glu), ("arbitrary",))


FFN_TM, FFN_TN = 512, 1408


def ffn_in(h2, wg_t, wu_t, comm=None):
    def body(h_ref, wg_ref, wu_ref, a_ref, b_ref, act_ref):
        hv = h_ref[...]
        a, b = _dot(hv, wg_ref[...], "nt"), _dot(hv, wu_ref[...], "nt")
        a_ref[...] = a.astype(BF16)
        b_ref[...] = b.astype(BF16)
        act_ref[...] = _swiglu_fn(a, b).astype(BF16)

    rows = pl.BlockSpec((FFN_TM, D_MODEL), lambda i, j: (i, 0))
    wts = pl.BlockSpec((FFN_TN, D_MODEL), lambda i, j: (j, 0))
    out = pl.BlockSpec((FFN_TM, FFN_TN), lambda i, j: (i, j))
    return hosted_call(body, comm, "ffn_in", (TOKENS // FFN_TM, D_FF // FFN_TN), [rows, wts, wts], [out] * 3,
                       [jax.ShapeDtypeStruct((TOKENS, D_FF), BF16)] * 3, [], (h2, wg_t, wu_t),
                       ("parallel", "parallel"))


def ffn_in_bwd(dx2_b, wd, a, b):
    def body(dx_ref, wd_ref, a_ref, b_ref, da_ref, db_ref):
        dact = _dot(dx_ref[...], wd_ref[...], "nt")
        av, bv = a_ref[...].astype(F32), b_ref[...].astype(F32)
        sig = jax.nn.sigmoid(av)
        act = av * sig
        da_ref[...] = (dact * bv * (sig * (1.0 + av - act))).astype(BF16)
        db_ref[...] = (dact * act).astype(BF16)

    rows = pl.BlockSpec((FFN_TM, D_MODEL), lambda i, j: (i, 0))
    wts = pl.BlockSpec((FFN_TN, D_MODEL), lambda i, j: (j, 0))
    out = pl.BlockSpec((FFN_TM, FFN_TN), lambda i, j: (i, j))
    return pl.pallas_call(
        body, name="ffn_in_bwd", grid=(TOKENS // FFN_TM, D_FF // FFN_TN), in_specs=[rows, wts, out, out],
        out_specs=[out] * 2, out_shape=[jax.ShapeDtypeStruct((TOKENS, D_FF), BF16)] * 2,
        compiler_params=_params(("parallel", "parallel")),
    )(dx2_b, wd, a, b)


def mix_in_bwd(grads, weights, partial, x, g, skip, comm=None):
    n = len(grads)
    tm = 512

    def body(*refs):
        a_refs, b_refs = refs[:n], refs[n:2 * n]
        part_ref, x_ref, g_ref, skip_ref, gx_ref, dg_ref = refs[2 * n:]
        dh = part_ref[...]
        for a_ref, b_ref in zip(a_refs, b_refs):
            dh = dh + _dot(a_ref[...], b_ref[...], "nn")
        _, vjp = jax.vjp(_rms, x_ref[...], g_ref[...])
        dx, dg = vjp(dh)
        gx_ref[...] = dx + skip_ref[...]
        first = pl.program_id(0) == 0

        @pl.when(first)
        def _():
            dg_ref[...] = dg

        @pl.when(jnp.logical_not(first))
        def _():
            dg_ref[...] += dg

    rows = pl.BlockSpec((tm, D_MODEL), lambda i: (i, 0))
    gain = pl.BlockSpec((1, D_MODEL), lambda i: (0, 0))
    in_specs = [pl.BlockSpec((tm, a.shape[1]), lambda i: (i, 0)) for a in grads]
    in_specs += [pl.BlockSpec(b.shape, lambda i: (0, 0)) for b in weights]
    return hosted_call(
        body, comm, "mix_in_bwd", (TOKENS // tm,), in_specs + [rows, rows, gain, rows], [rows, gain],
        [jax.ShapeDtypeStruct((TOKENS, D_MODEL), F32), jax.ShapeDtypeStruct((1, D_MODEL), F32)], [],
        (*grads, *weights, partial, x, g, skip), ("arbitrary",))


def rowwise(fn, ins, outs, name, accs=(), tm=256, rows=TOKENS, comm=None):
    in_specs, args = [], []
    for item in ins:
        arr, width, blk = item if isinstance(item, tuple) else (item, None, 0)
        if arr.ndim == 3:
            for k in range(arr.shape[0]):
                in_specs.append(pl.BlockSpec((None, tm, arr.shape[2]), functools.partial(lambda i, k_: (k_, i, 0), k_=k)))
                args.append(arr)
            continue
        if arr.shape[0] == 1:
            in_specs.append(pl.BlockSpec(arr.shape, lambda i: (0, 0)))
        elif width is None:
            in_specs.append(pl.BlockSpec((tm, arr.shape[1]), lambda i: (i, 0)))
        else:
            in_specs.append(pl.BlockSpec((tm, width), functools.partial(lambda i, blk_: (i, blk_), blk_=blk)))
        args.append(arr)
    out_specs = [pl.BlockSpec((tm, c), lambda i: (i, 0)) for c, _ in outs]
    out_specs += [pl.BlockSpec((1, c), lambda i: (0, 0)) for c in accs]
    out_shape = [jax.ShapeDtypeStruct((rows, c), dt) for c, dt in outs]
    out_shape += [jax.ShapeDtypeStruct((1, c), F32) for c in accs]
    n_in, n_out = len(args), len(outs)
    c_ins, c_outs, c_sems = _comm_operands(comm)

    def body(*refs):
        refs, c_refs = _comm_refs(comm, refs, n_in, n_out + len(accs))
        step = pl.program_id(0)
        _comm_begin(comm, c_refs, step, rows // tm)
        res = fn(*[r[...] for r in refs[:n_in]])
        for r, v in zip(refs[n_in:n_in + n_out], res[:n_out]):
            r[...] = v.astype(r.dtype)
        first = step == 0
        for r, v in zip(refs[n_in + n_out:], res[n_out:]):
            @pl.when(first)
            def _(r=r, v=v):
                r[...] = v

            @pl.when(jnp.logical_not(first))
            def _(r=r, v=v):
                r[...] += v
        _comm_end(comm, c_refs, step, rows // tm)

    return pl.pallas_call(
        body, name=name, grid=(rows // tm,), in_specs=in_specs + [ANY] * len(c_ins),
        out_specs=out_specs + [ANY] * len(c_outs), out_shape=out_shape + c_outs, scratch_shapes=c_sems,
        compiler_params=_params(("arbitrary",)),
    )(*args, *c_ins)


def first_norm(x, g, others, comm=None):
    tm, n = 256, len(others)

    def body(x_ref, g_ref, *rest):
        srcs, h_ref, dsts = rest[:n], rest[n], rest[n + 1:]
        h_ref[...] = _rms(x_ref[...], g_ref[...]).astype(BF16)
        for k, (s, d) in enumerate(zip(srcs, dsts)):
            @pl.when(pl.program_id(0) == k)
            def _(s=s, d=d):
                d[...] = s[...].astype(BF16)

    rows = pl.BlockSpec((tm, D_MODEL), lambda i: (i, 0))
    whole = [pl.BlockSpec(a.shape, lambda i: (0, 0)) for a in others]
    return hosted_call(
        body, comm, "norm_mix", (TOKENS // tm,), [rows, pl.BlockSpec((1, D_MODEL), lambda i: (0, 0))] + whole,
        [rows] + whole, [jax.ShapeDtypeStruct((TOKENS, D_MODEL), BF16)]
        + [jax.ShapeDtypeStruct(a.shape, BF16) for a in others], [], (x, g, *others), ("arbitrary",))


def _rms(x, g):
    return x * lax.rsqrt(jnp.mean(x * x, axis=-1, keepdims=True) + RMS_EPS) * g


def _colsum(v):
    return jnp.sum(v, axis=0, keepdims=True)


PAIR_W = 2 * HEAD_DIM
N_PAIRS = HEADS_PER_GROUP // 2


def _qkv_order(w_t, back=False):
    dims = (N_PAIRS, N_GROUPS, 3) if back else (3, N_GROUPS, N_PAIRS)
    return w_t.reshape(dims + (PAIR_W, w_t.shape[1])).transpose(2, 1, 0, 3, 4).reshape(QKV_W, w_t.shape[1])


def _rope_tables():
    half = ROPE_DIM // 2
    inv = np.power(np.float32(ROPE_THETA), -np.arange(half, dtype=np.float32) * np.float32(2.0 / ROPE_DIM))
    ang = (np.arange(SEQ, dtype=np.float32)[:, None] * inv[None, :]).astype(np.float32)
    cos, sin = np.cos(ang), np.sin(ang)
    zeros = np.zeros((SEQ, HEAD_DIM - ROPE_DIM), np.float32)
    zh = np.zeros((SEQ, half), np.float32)
    c = np.concatenate([cos, cos, zeros + 1.0], axis=1)
    sa = np.concatenate([-sin, zh, zeros], axis=1)
    sb = np.concatenate([zh, sin, zeros], axis=1)
    return [jnp.asarray(np.tile(t, (1, 2)), F32) for t in (c, sa, sb)]


def _rope_fwd(x, c, sa, sb):
    return x * c + pltpu.roll(x, PAIR_W - 8, 1) * sa + pltpu.roll(x, 8, 1) * sb


def _rope_bwd(dy, c, sa, sb):
    return dy * c + pltpu.roll(dy * sb, PAIR_W - 8, 1) + pltpu.roll(dy * sa, 8, 1)


def _band_masks():
    row = lax.broadcasted_iota(jnp.int32, (ATTN_BLOCK, ATTN_BLOCK), 0)
    col = lax.broadcasted_iota(jnp.int32, (ATTN_BLOCK, ATTN_BLOCK), 1)
    return col <= row, col >= row


def _stack_rows(t):
    return jnp.concatenate([t, t], axis=0)


def _stack_heads(t, first_head):
    return jnp.concatenate([jnp.where(first_head, t, 0), jnp.where(first_head, 0, t)], axis=0)


def _per_head(fn):
    return jnp.concatenate([fn(slice(h * HEAD_DIM, (h + 1) * HEAD_DIM)) for h in range(2)], axis=1)


def _slab_spec(kind):
    return pl.BlockSpec((None, SEQ, PAIR_W), lambda b, p, g: (b, 0, p * 3 * N_GROUPS + g * 3 + kind))


_TABLE_SPEC = pl.BlockSpec((SEQ, PAIR_W), lambda b, p, g: (0, 0))
_PAIR_SPEC = pl.BlockSpec((None, SEQ, PAIR_W), lambda b, p, g: (b, 0, p))


def _block_rows(dil, r, n):
    return pl.ds(n * (ATTN_BLOCK * dil) + r, ATTN_BLOCK, stride=dil)


def proj_qkv(h, w_qkv_t, tables, comm=None):
    tm = 1024
    pair_w = QKV_W // N_PAIRS
    scale = HEAD_DIM ** -0.5

    def body(h_ref, w_ref, c_ref, sa_ref, sb_ref, o_ref):
        rows = _dot(h_ref[...], w_ref[...], "nt")
        c, sa, sb = c_ref[...], sa_ref[...], sb_ref[...]
        for blk in range(pair_w // PAIR_W):
            cols = slice(blk * PAIR_W, (blk + 1) * PAIR_W)
            x = rows[:, cols]
            if blk % 3 == 0:
                x = _rope_fwd(x, c, sa, sb) * scale
            elif blk % 3 == 1:
                x = _rope_fwd(x, c, sa, sb)
            o_ref[:, cols] = x

    table = pl.BlockSpec((tm, PAIR_W), lambda i, j, : (i % (SEQ // tm), 0))
    res = hosted_call(
        body, comm, "proj_qkv", (TOKENS // tm, N_PAIRS),
        [pl.BlockSpec((tm, D_MODEL), lambda i, j: (i, 0)), pl.BlockSpec((pair_w, D_MODEL), lambda i, j: (j, 0)),
         table, table, table],
        [pl.BlockSpec((tm, pair_w), lambda i, j: (i, j))], [jax.ShapeDtypeStruct((TOKENS, QKV_W), F32)], [],
        (h, w_qkv_t, *tables), ("parallel", "parallel"))
    return res[0] if comm is None else res


def attn_fwd(qkv, comm=None):
    def body(qs, ks, v_ref, attn_b_ref, attn_ref, lse_ref, o0, o1, o2, l0, l1, l2):
        g = pl.program_id(2)
        cur_mask, prev_mask = _band_masks()
        first_head = lax.broadcasted_iota(jnp.int32, (ATTN_BLOCK, PAIR_W), 1) < HEAD_DIM

        def run(dil, o_slab, l_slab):
            nb = SEQ // dil // ATTN_BLOCK

            def block(idx, carry):
                r, n = lax.div(idx, nb), lax.rem(idx, nb)
                cur, prev = _block_rows(dil, r, n), _block_rows(dil, r, jnp.maximum(n - 1, 0))
                q = qs[cur, :].astype(BF16)
                kc, kp = ks[cur, :].astype(BF16), ks[prev, :].astype(BF16)
                vc, vp = v_ref[cur, :].astype(BF16), v_ref[prev, :].astype(BF16)
                q2 = _stack_heads(q, first_head)
                mask = _stack_rows(jnp.concatenate([jnp.logical_and(prev_mask, n > 0), cur_mask], axis=1))
                s2 = jnp.where(mask, _dot(q2, jnp.concatenate([kp, kc], axis=0), "nt"), NEG_INF)
                m = jnp.max(s2, axis=-1, keepdims=True)
                vcat, two = jnp.concatenate([vp, vc], axis=0), _stack_rows(first_head)
                vext = jnp.concatenate([jnp.where(two, vcat, 1), jnp.where(two, 1, vcat)], axis=1)
                r2 = _dot(jnp.exp(s2 - m).astype(BF16), vext, "nn")
                r0, r1 = r2[:ATTN_BLOCK, :PAIR_W], r2[ATTN_BLOCK:, PAIR_W:]
                num = jnp.where(first_head, r0, r1)
                den = pltpu.roll(jnp.where(first_head, r1, r0), HEAD_DIM, 1)
                o_slab[cur, :] = num / den
                l_slab[cur, :] = jnp.where(first_head, m[:ATTN_BLOCK], m[ATTN_BLOCK:]) + jnp.log(den)
                return carry

            lax.fori_loop(0, SEQ // ATTN_BLOCK, block, 0, unroll=4)

        for gi, (o_slab, l_slab) in enumerate(((o0, l0), (o1, l1), (o2, l2))):
            @pl.when(g == gi)
            def _(gi=gi, o_slab=o_slab, l_slab=l_slab):
                run(DILATIONS[gi], o_slab, l_slab)

        @pl.when(g == N_GROUPS - 1)
        def _():
            a, b, cc = l0[...], l1[...], l2[...]
            m = jnp.maximum(jnp.maximum(a, b), cc)
            e0, e1, e2 = jnp.exp(a - m), jnp.exp(b - m), jnp.exp(cc - m)
            tot = e0 + e1 + e2
            attn = (e0 * o0[...] + e1 * o1[...] + e2 * o2[...]) / tot
            attn_ref[...] = attn
            attn_b_ref[...] = attn.astype(BF16)
            lse_ref[...] = m + jnp.log(tot)

    shape = (LOCAL_BATCH, SEQ, GROUP_W)
    slab = pltpu.VMEM((SEQ, PAIR_W), F32)
    return hosted_call(
        body, comm, "attn_fwd", (LOCAL_BATCH, N_PAIRS, N_GROUPS),
        [_slab_spec(0), _slab_spec(1), _slab_spec(2)], [_PAIR_SPEC] * 3,
        [jax.ShapeDtypeStruct(shape, BF16), jax.ShapeDtypeStruct(shape, F32), jax.ShapeDtypeStruct(shape, F32)],
        [slab] * 6, (qkv, qkv, qkv), ("parallel", "parallel", "arbitrary"))


def attn_bwd(qkv, tables, dattn, attn, lse, comm=None):
    scale = HEAD_DIM ** -0.5

    def body(qs, ks, v_ref, c_ref, sa_ref, sb_ref, do_ref, out_ref, lse_ref, dqkv_ref, dl, dq_s, dk_s, dv_s):
        g = pl.program_id(2)
        c, sa, sb = c_ref[...], sa_ref[...], sb_ref[...]

        @pl.when(g == 0)
        def _():
            prod = do_ref[...] * out_ref[...]
            dl[...] = _per_head(
                lambda sl: jnp.broadcast_to(jnp.sum(prod[:, sl], axis=-1, keepdims=True), (SEQ, HEAD_DIM)))

        cur_mask, prev_mask = _band_masks()
        first_head = lax.broadcasted_iota(jnp.int32, (ATTN_BLOCK, PAIR_W), 1) < HEAD_DIM

        def run(dil):
            nb = SEQ // dil // ATTN_BLOCK

            def block(idx, carry):
                r, n = lax.div(idx, nb), lax.rem(idx, nb)
                cur = _block_rows(dil, r, n)
                prev = _block_rows(dil, r, jnp.maximum(n - 1, 0))
                nxt = _block_rows(dil, r, jnp.minimum(n + 1, nb - 1))
                q0, q1 = qs[cur, :].astype(BF16), qs[nxt, :].astype(BF16)
                kp, kc = ks[prev, :].astype(BF16), ks[cur, :].astype(BF16)
                vp, vc = v_ref[prev, :].astype(BF16), v_ref[cur, :].astype(BF16)
                do0, do1 = do_ref[cur, :].astype(BF16), do_ref[nxt, :].astype(BF16)
                lse0, lse1, dl0, dl1 = lse_ref[cur, :], lse_ref[nxt, :], dl[cur, :], dl[nxt, :]
                has_prev = jnp.logical_and(prev_mask, n > 0)
                has_next = jnp.logical_and(prev_mask, n < nb - 1)

                def per_row(t):
                    return jnp.concatenate([t[:, 0:1], t[:, HEAD_DIM:HEAD_DIM + 1]], axis=0)

                q20, q21 = _stack_heads(q0, first_head), _stack_heads(q1, first_head)
                do20, do21 = _stack_heads(do0, first_head), _stack_heads(do1, first_head)
                kcat, vcat = jnp.concatenate([kp, kc], axis=0), jnp.concatenate([vp, vc], axis=0)
                mask0 = _stack_rows(jnp.concatenate([has_prev, cur_mask], axis=1))
                p0 = jnp.where(mask0, jnp.exp(_dot(q20, kcat, "nt") - per_row(lse0)), 0.0)
                ds0 = (p0 * (_dot(do20, vcat, "nt") - per_row(dl0))).astype(BF16)
                p1 = jnp.where(_stack_rows(has_next), jnp.exp(_dot(q21, kc, "nt") - per_row(lse1)), 0.0)
                ds1 = (p1 * (_dot(do21, vc, "nt") - per_row(dl1))).astype(BF16)
                dq2 = _dot(ds0, kcat, "nn")
                dq_s[cur, :] = jnp.where(first_head, dq2[:ATTN_BLOCK], dq2[ATTN_BLOCK:])
                ds_cur = jnp.concatenate([ds0[:, ATTN_BLOCK:], ds1], axis=0)
                p_cur = jnp.concatenate([p0[:, ATTN_BLOCK:], p1], axis=0).astype(BF16)
                dk_s[cur, :] = _dot(ds_cur, jnp.concatenate([q20, q21], axis=0), "tn")
                dv_s[cur, :] = _dot(p_cur, jnp.concatenate([do20, do21], axis=0), "tn")
                return carry

            lax.fori_loop(0, SEQ // ATTN_BLOCK, block, 0, unroll=2)

        for gi in range(N_GROUPS):
            @pl.when(g == gi)
            def _(gi=gi):
                run(DILATIONS[gi])

        dqkv_ref[:, 0:PAIR_W] = _rope_bwd(dq_s[...] * scale, c, sa, sb).astype(BF16)
        dqkv_ref[:, PAIR_W:2 * PAIR_W] = _rope_bwd(dk_s[...], c, sa, sb).astype(BF16)
        dqkv_ref[:, 2 * PAIR_W:] = dv_s[...].astype(BF16)

    slab = pltpu.VMEM((SEQ, PAIR_W), F32)
    return hosted_call(
        body, comm, "attn_bwd", (LOCAL_BATCH, N_PAIRS, N_GROUPS),
        [_slab_spec(0), _slab_spec(1), _slab_spec(2), _TABLE_SPEC, _TABLE_SPEC, _TABLE_SPEC,
         _PAIR_SPEC, _PAIR_SPEC, _PAIR_SPEC],
        [pl.BlockSpec((None, SEQ, 3 * PAIR_W), lambda b, p, g: (b, 0, p * N_GROUPS + g))],
        [jax.ShapeDtypeStruct((LOCAL_BATCH, SEQ, QKV_W), BF16)],
        [slab] * 4, (qkv, qkv, qkv, *tables, dattn, attn, lse), ("parallel", "parallel", "arbitrary"))


def _discretize(lr, li, log_dt, br, bi):
    dt = jnp.exp(log_dt)
    mag = jnp.exp(lr * dt)
    ab_re, ab_im = mag * jnp.cos(li * dt), mag * jnp.sin(li * dt)
    den = lr * lr + li * li
    nr, ni = ab_re - 1.0, ab_im
    f_re = (nr * lr + ni * li) / den
    f_im = (ni * lr - nr * li) / den
    return ab_re, ab_im, f_re[None] * br - f_im[None] * bi, f_re[None] * bi + f_im[None] * br


def ssm_prep(lr, li, log_dt, br, bi):
    def body(lr_ref, li_ref, dt_ref, br_ref, bi_ref, *outs):
        for o, v in zip(outs, _discretize(lr_ref[...], li_ref[...], dt_ref[...], br_ref[...], bi_ref[...])):
            o[...] = v
    shapes = [lr, li, br, bi]
    return pl.pallas_call(body, name="ssm_prep",
                          out_shape=[jax.ShapeDtypeStruct(s.shape, F32) for s in shapes])(lr, li, log_dt, br, bi)


def ssm_prep_bwd(lr, li, log_dt, br, bi, g_ab_re, g_ab_im, g_bb_re, g_bb_im):
    def body(lr_ref, li_ref, dt_ref, br_ref, bi_ref, g0, g1, g2, g3, *outs):
        _, vjp = jax.vjp(_discretize, lr_ref[...], li_ref[...], dt_ref[...], br_ref[...], bi_ref[...])
        for o, v in zip(outs, vjp((g0[...], g1[...], g2[...], g3[...]))):
            o[...] = v
    shapes = [lr, li, log_dt, br, bi]
    return pl.pallas_call(body, name="ssm_prep_bwd",
                          out_shape=[jax.ShapeDtypeStruct(s.shape, F32) for s in shapes])(
        lr, li, log_dt, br, bi, g_ab_re, g_ab_im, g_bb_re, g_bb_im)


def _block_diag(t):
    per = SSM_STATE_W // SSM_LANE_BLOCKS // 64
    g = t.transpose(1, 0, 2).reshape(SSM_LANE_BLOCKS, per, 16, 64)
    eye = jnp.eye(per, dtype=t.dtype)
    return jnp.einsum("jgcn,gh->jgchn", g, eye).reshape(SSM_LANE_BLOCKS, per * 16, per * 64)


def _block_diag_t(m):
    per = SSM_STATE_W // SSM_LANE_BLOCKS // 64
    m5 = m.reshape(SSM_LANE_BLOCKS, per, 16, per, 64)
    d = jnp.einsum("jgchn,gh->jgcn", m5, jnp.eye(per, dtype=m.dtype))
    return d.reshape(SSM_LANE_BLOCKS * per, 16, 64).transpose(1, 0, 2)


def _cmul(ar, ai, br, bi):
    return ar * br - ai * bi, ar * bi + ai * br


def _power_tables(ar, ai, reverse):
    width = ar.shape[1]
    row = lax.broadcasted_iota(jnp.int32, (8, width), 0)
    pows = [(ar, ai)]
    for _ in range(7):
        pows.append(_cmul(pows[-1][0], pows[-1][1], ar, ai))
    steps = []
    for k in (1, 2, 4):
        keep = (row >= k) if not reverse else (row < 8 - k)
        steps.append((jnp.where(keep, pows[k - 1][0], 0.0), jnp.where(keep, pows[k - 1][1], 0.0)))
    cr = jnp.zeros((8, width), F32)
    ci = jnp.zeros((8, width), F32)
    for i in range(8):
        pr, pi = pows[i] if not reverse else pows[7 - i]
        cr = jnp.where(row == i, pr, cr)
        ci = jnp.where(row == i, pi, ci)
    return steps, (cr, ci)


SCAN_CHUNK = 2048
STATE_BLOCK = SSM_STATE_W // SSM_LANE_BLOCKS
CHAN_BLOCK = SSM_W // SSM_LANE_BLOCKS


def ssm_fwd(u, ab_re, ab_im, bb_re, bb_im, cb_re, cb_im, d_skip, comm=None):
    nt = SEQ // SCAN_CHUNK
    chan = pl.BlockSpec((None, SCAN_CHUNK, CHAN_BLOCK), lambda b, j, t: (b, t, j))
    state = pl.BlockSpec((None, SCAN_CHUNK, STATE_BLOCK), lambda b, j, t: (b, t, j))
    mat = pl.BlockSpec((None, CHAN_BLOCK, STATE_BLOCK), lambda b, j, t: (j, 0, 0))
    lane = pl.BlockSpec((1, STATE_BLOCK), lambda b, j, t: (0, j))
    dsp = pl.BlockSpec((1, CHAN_BLOCK), lambda b, j, t: (0, j))

    def body(u_ref, ar_ref, ai_ref, bbr_ref, bbi_ref, cbr_ref, cbi_ref, d_ref, y_ref, yg_ref, xr_ref, xi_ref,
             car_r, car_i):
        @pl.when(pl.program_id(2) == 0)
        def _():
            car_r[...] = jnp.zeros_like(car_r)
            car_i[...] = jnp.zeros_like(car_i)

        steps, (pr, pi) = _power_tables(ar_ref[...], ai_ref[...], reverse=False)
        uf = u_ref[...]
        ub = uf.astype(BF16)
        xr_ref[...] = _dot(ub, bbr_ref[...], "nn")
        xi_ref[...] = _dot(ub, bbi_ref[...], "nn")

        def tile(i, carry):
            cr, ci = carry
            sl = pl.ds(pl.multiple_of(i * 8, 8), 8)
            br, bi = xr_ref[sl, :], xi_ref[sl, :]
            for k, (sr, si) in zip((1, 2, 4), steps):
                tr, ti = _cmul(sr, si, pltpu.roll(br, k, 0), pltpu.roll(bi, k, 0))
                br, bi = br + tr, bi + ti
            tr, ti = _cmul(pr, pi, cr, ci)
            br, bi = br + tr, bi + ti
            xr_ref[sl, :] = br
            xi_ref[sl, :] = bi
            return br[7:8, :], bi[7:8, :]

        cr, ci = lax.fori_loop(0, SCAN_CHUNK // 8, tile, (car_r[0:1, :], car_i[0:1, :]), unroll=4)
        car_r[0:1, :] = cr
        car_i[0:1, :] = ci
        y = (_dot(xr_ref[...].astype(BF16), cbr_ref[...], "nt") - _dot(xi_ref[...].astype(BF16), cbi_ref[...], "nt")
             + d_ref[...] * uf)
        y_ref[...] = y
        yg_ref[...] = jax.nn.gelu(y).astype(BF16)

    return hosted_call(
        body, comm, "ssm_fwd", (LOCAL_BATCH, SSM_LANE_BLOCKS, nt),
        [chan, lane, lane, mat, mat, mat, mat, dsp], [chan, chan, state, state],
        [jax.ShapeDtypeStruct((LOCAL_BATCH, SEQ, SSM_W), F32), jax.ShapeDtypeStruct((LOCAL_BATCH, SEQ, SSM_W), BF16),
         jax.ShapeDtypeStruct((LOCAL_BATCH, SEQ, SSM_STATE_W), F32),
         jax.ShapeDtypeStruct((LOCAL_BATCH, SEQ, SSM_STATE_W), F32)],
        [pltpu.VMEM((8, STATE_BLOCK), F32), pltpu.VMEM((8, STATE_BLOCK), F32)],
        (u, ab_re, ab_im, bb_re, bb_im, cb_re, cb_im, d_skip), ("parallel", "parallel", "arbitrary"))


def ssm_bwd(dyg, y, u, xr, xi, ab_re, ab_im, bb_re, bb_im, cb_re, cb_im, d_skip, comm=None):
    nt = SEQ // SCAN_CHUNK
    ntile = SCAN_CHUNK // 8

    def rev(t):
        return nt - 1 - t

    chan = pl.BlockSpec((None, SCAN_CHUNK, CHAN_BLOCK), lambda j, b, t: (b, rev(t), j))
    state = pl.BlockSpec((None, SCAN_CHUNK, STATE_BLOCK), lambda j, b, t: (b, rev(t), j))
    before = pl.BlockSpec((None, 8, STATE_BLOCK), lambda j, b, t: (b, jnp.maximum(rev(t) * ntile - 1, 0), j))
    mat = pl.BlockSpec((None, CHAN_BLOCK, STATE_BLOCK), lambda j, b, t: (j, 0, 0))
    lane = pl.BlockSpec((1, STATE_BLOCK), lambda j, b, t: (0, j))
    lane8 = pl.BlockSpec((8, STATE_BLOCK), lambda j, b, t: (0, j))
    dsp = pl.BlockSpec((1, CHAN_BLOCK), lambda j, b, t: (0, j))

    def body(dyg_ref, y_ref, u_ref, xr_ref, xi_ref, xrb_ref, xib_ref, ar_ref, ai_ref, bbr_ref, bbi_ref, cbr_ref,
             cbi_ref, d_ref, du_ref, dcbr_ref, dcbi_ref, dbbr_ref, dbbi_ref, dd_ref, dar_ref, dai_ref,
             lam_r, lam_i, car_r, car_i):
        b, t = pl.program_id(1), pl.program_id(2)
        first = jnp.logical_and(b == 0, t == 0)

        @pl.when(t == 0)
        def _():
            car_r[...] = jnp.zeros_like(car_r)
            car_i[...] = jnp.zeros_like(car_i)

        @pl.when(first)
        def _():
            for r in (dcbr_ref, dcbi_ref, dbbr_ref, dbbi_ref, dd_ref, dar_ref, dai_ref):
                r[...] = jnp.zeros_like(r)

        steps, (pr, pi) = _power_tables(ar_ref[...], -ai_ref[...], reverse=True)
        uf = u_ref[...]
        _, gelu_vjp = jax.vjp(jax.nn.gelu, y_ref[...])
        dy = gelu_vjp(dyg_ref[...])[0]
        dyb = dy.astype(BF16)
        dd_ref[...] += _colsum(dy * uf)
        lam_r[...] = _dot(dyb, cbr_ref[...], "nn")
        lam_i[...] = -_dot(dyb, cbi_ref[...], "nn")
        dcbr_ref[...] += _dot(dyb, xr_ref[...].astype(BF16), "tn")
        dcbi_ref[...] -= _dot(dyb, xi_ref[...].astype(BF16), "tn")
        row0 = lax.broadcasted_iota(jnp.int32, (8, STATE_BLOCK), 0) == 0
        has_before = rev(t) > 0
        xrb = jnp.where(has_before, xrb_ref[...], 0.0)
        xib = jnp.where(has_before, xib_ref[...], 0.0)

        def tile(s, carry):
            cr, ci, acc_r, acc_i = carry
            i = ntile - 1 - s
            sl = pl.ds(pl.multiple_of(i * 8, 8), 8)
            gr, gi = lam_r[sl, :], lam_i[sl, :]
            for k, (sr, si) in zip((1, 2, 4), steps):
                tr, ti = _cmul(sr, si, pltpu.roll(gr, 8 - k, 0), pltpu.roll(gi, 8 - k, 0))
                gr, gi = gr + tr, gi + ti
            tr, ti = _cmul(pr, pi, cr, ci)
            gr, gi = gr + tr, gi + ti
            lam_r[sl, :] = gr
            lam_i[sl, :] = gi
            sp = pl.ds(pl.multiple_of(jnp.maximum(i - 1, 0) * 8, 8), 8)
            pvr = jnp.where(i > 0, xr_ref[sp, :], xrb)
            pvi = jnp.where(i > 0, xi_ref[sp, :], xib)
            xsr = jnp.where(row0, pltpu.roll(pvr, 1, 0), pltpu.roll(xr_ref[sl, :], 1, 0))
            xsi = jnp.where(row0, pltpu.roll(pvi, 1, 0), pltpu.roll(xi_ref[sl, :], 1, 0))
            acc_r = acc_r + xsr * gr + xsi * gi
            acc_i = acc_i + xsr * gi - xsi * gr
            return gr[0:1, :], gi[0:1, :], acc_r, acc_i

        zero = jnp.zeros((8, STATE_BLOCK), F32)
        cr, ci, acc_r, acc_i = lax.fori_loop(0, ntile, tile, (car_r[0:1, :], car_i[0:1, :], zero, zero), unroll=2)
        car_r[0:1, :] = cr
        car_i[0:1, :] = ci
        dar_ref[...] += acc_r
        dai_ref[...] += acc_i
        lrb, lib = lam_r[...].astype(BF16), lam_i[...].astype(BF16)
        du = _dot(lrb, bbr_ref[...], "nt") + _dot(lib, bbi_ref[...], "nt") + d_ref[...] * dy
        du_ref[...] = du.astype(BF16)
        ub = uf.astype(BF16)
        dbbr_ref[...] += _dot(ub, lrb, "tn")
        dbbi_ref[...] += _dot(ub, lib, "tn")

    mat_shape = jax.ShapeDtypeStruct((SSM_LANE_BLOCKS, CHAN_BLOCK, STATE_BLOCK), F32)
    return hosted_call(
        body, comm, "ssm_bwd", (SSM_LANE_BLOCKS, LOCAL_BATCH, nt),
        [chan, chan, chan, state, state, before, before, lane, lane, mat, mat, mat, mat, dsp],
        [chan, mat, mat, mat, mat, dsp, lane8, lane8],
        [jax.ShapeDtypeStruct((LOCAL_BATCH, SEQ, SSM_W), BF16), mat_shape, mat_shape, mat_shape, mat_shape,
         jax.ShapeDtypeStruct((1, SSM_W), F32), jax.ShapeDtypeStruct((8, SSM_STATE_W), F32),
         jax.ShapeDtypeStruct((8, SSM_STATE_W), F32)],
        [pltpu.VMEM((SCAN_CHUNK, STATE_BLOCK), F32), pltpu.VMEM((SCAN_CHUNK, STATE_BLOCK), F32),
         pltpu.VMEM((8, STATE_BLOCK), F32), pltpu.VMEM((8, STATE_BLOCK), F32)],
        (dyg, y, u, xr, xi, xr, xi, ab_re, ab_im, bb_re, bb_im, cb_re, cb_im, d_skip),
        ("parallel", "arbitrary", "arbitrary"))


def _merge_fn(g0, g1, attn_d, za, zb):
    return jax.nn.sigmoid(g0) * attn_d + jax.nn.sigmoid(g1) * (za * jax.nn.sigmoid(zb))


def _swiglu_fn(a, b):
    return jax.nn.silu(a) * b


def _own_slot(slots, shard):
    me = 2 * lax.axis_index("x") + lax.axis_index("y")
    mine = lax.broadcasted_iota(jnp.int32, (N_CHIPS, 1, 1), 0) == me
    return jnp.where(mine, shard[None], slots)


def _reduce_start(names, gw, shard_shapes):
    return swap_comm([_to_slots(n, gw[n], shard_shapes[n]) for n in names])


def _reduce_chip(names, swap, got, core):
    return exchange_comm([add_halves(n, g, r, core) for n, g, r in zip(names, swap.ins, got)])


def local_step(x, target, shards, small, core):
    g_mix, g_ffn, g_final = small["norm_mix_g"], small["norm_ffn_g"], small["norm_final_g"]
    tables = _rope_tables()
    seqs = lambda t: t.reshape(LOCAL_BATCH, SEQ, t.shape[-1])
    toks = lambda t: t.reshape(TOKENS, t.shape[-1])
    shard_shapes = {n: s.shape for n, s in shards.items()}
    w = {}

    def gather(names):
        return gather_comm([shards[n] for n in names])

    def arrived(names, slots, own=None):
        for n, s in zip(names, slots):
            w[n] = _from_slots(n, s if own is None else _own_slot(s, own))

    later = [n for n in BIG if n != "w_in"]
    sems, w_in_shard, land, token = split_start(shards["w_in"].astype(BF16), "w_in_gather_start")
    zero = token[0, 0]
    h, *rest = first_norm(x, g_mix + zero, [shards[n] for n in later])
    shards = dict(shards)
    shards.update(zip(later, rest))
    br_t = small["ssm_b_re"].transpose(2, 0, 1)
    bi_t = small["ssm_b_im"].transpose(2, 0, 1)
    log_dt = small["ssm_log_dt"].reshape(32, 1)
    ab_re, ab_im, bb_re_t, bb_im_t = ssm_prep(small["ssm_a_re"] + zero, small["ssm_a_im"], log_dt, br_t, bi_t)
    ab = [ab_re.reshape(1, SSM_STATE_W), ab_im.reshape(1, SSM_STATE_W)]
    bb = [_block_diag(bb_re_t).astype(BF16), _block_diag(bb_im_t).astype(BF16)]
    cb = [_block_diag((small["ssm_c_re"] + zero).transpose(1, 0, 2)).astype(BF16),
          _block_diag((small["ssm_c_im"] + zero).transpose(1, 0, 2)).astype(BF16)]
    d_skip = small["ssm_d"].reshape(1, SSM_W)
    w_in_shard, land = split_wait(sems, w_in_shard, land, [h] + bb + cb, "w_in_gather_wait")
    arrived(["w_in"], [handover(land, "w_in_handover")], own=w_in_shard)
    w_qkv, w_u, w_gate = _qkv_order(w["w_in"][:QKV_W]), w["w_in"][QKV_W:QKV_W + SSM_W], w["w_in"][QKV_W + SSM_W:]
    qkv, *slots = proj_qkv(h, w_qkv, tables, comm=gather(["w_attn_out", "w_glu"]))
    arrived(["w_attn_out", "w_glu"], slots)
    qkv = seqs(qkv)
    u = seqs(matmul(h, w_u, "nt", F32, "proj_u"))
    gl, *slots = matmul(h, w_gate, "nt", BF16, "proj_gate", comm=gather(["w_out"]))
    arrived(["w_out"], slots)
    attn_b, attn, lse, *slots = attn_fwd(qkv, comm=gather(["w_ffn_gate"]))
    arrived(["w_ffn_gate"], slots)
    attn_b = toks(attn_b)
    y, yg, xr, xi, *slots = ssm_fwd(u, *ab, *bb, *cb, d_skip, comm=gather(["w_ffn_up"]))
    arrived(["w_ffn_up"], slots)
    yg2 = toks(yg)
    merged, x1, h2 = merge_out_proj(gl, attn_b, w["w_attn_out"], yg2, w["w_glu"], w["w_out"], x, g_ffn)
    a, b, act, *slots = ffn_in(h2, w["w_ffn_gate"], w["w_ffn_up"], comm=gather(["w_ffn_down"]))
    arrived(["w_ffn_down"], slots)

    def final_fn(xv, g, tgt):
        yv, vjp = jax.vjp(_rms, xv, g)
        err = yv - tgt
        dx, dg = vjp(err * (1.0 / D_MODEL))
        loss = 0.5 * jnp.sum(jnp.mean(err * err, axis=-1, keepdims=True), axis=0, keepdims=True)
        return dx, dx, dg, jnp.broadcast_to(loss, (1, LANES))

    dx2, dx2_b, dg_final, loss = matmul_rows(act, w["w_ffn_down"], "ffn_down_loss", final_fn, [g_final, target],
                                             [(D_MODEL, F32), (D_MODEL, BF16)], accs=(D_MODEL, LANES), add=x1)
    gw, parts = {}, {}
    gw["w_ffn_down"] = matmul(act, dx2_b, "tn", F32, "d_ffn_down")
    da_b, db_b = ffn_in_bwd(dx2_b, w["w_ffn_down"], a, b)
    gw["w_ffn_gate"] = matmul(da_b, h2, "tn", F32, "d_ffn_gate")
    gw["w_ffn_up"] = matmul(db_b, h2, "tn", F32, "d_ffn_up")
    ffn = ["w_ffn_down", "w_ffn_gate", "w_ffn_up"]
    swap = _reduce_start(ffn[:2], gw, shard_shapes)
    dh2, *got = matmul(da_b, w["w_ffn_gate"], "nn", F32, "d_h2_gate", comm=swap)
    ffn_exchange = [_reduce_chip(ffn[:2], swap, got, core)]
    swap = _reduce_start(ffn[2:], gw, shard_shapes)

    def norm_bwd(dh, xv, g, skip):
        _, vjp = jax.vjp(_rms, xv, g)
        dx, dg = vjp(dh)
        dx = dx + skip
        return dx, dx, dg

    dx1, dx1_b, dg_ffn, *got = matmul_rows(db_b, w["w_ffn_up"], "d_h2_up_norm", norm_bwd, [x1, g_ffn, dx2],
                                           [(D_MODEL, F32), (D_MODEL, BF16)], accs=(D_MODEL,), add=dh2, comm=swap)
    ffn_up_exchange = _reduce_chip(ffn[2:], swap, got, core)
    gw["w_out"] = matmul(merged, dx1_b, "tn", F32, "d_out")
    dgl_b, dattn_d_b, dz_b, dattn, dyg, parts["w_ffn_up"] = merge_bwd(
        dx1_b, w["w_out"], gl, attn_b, w["w_attn_out"], yg2, w["w_glu"], comm=ffn_up_exchange)
    dattn, dyg = seqs(dattn), seqs(dyg)
    gw["w_attn_out"] = matmul(attn_b, dattn_d_b, "tn", F32, "d_attn_out")
    gw["w_glu"] = matmul(yg2, dz_b, "tn", F32, "d_glu")
    mixer = ["w_out", "w_attn_out", "w_glu"]
    swap = _reduce_start(mixer, gw, shard_shapes)
    du_b, dcb_re, dcb_im, dbb_re, dbb_im, dd, da_re8, da_im8, *rest = ssm_bwd(
        dyg, y, u, xr, xi, *ab, *bb, *cb, d_skip, comm=join_comms(ffn_exchange + [swap]))
    for n, p in zip(ffn[:2], rest[:2]):
        parts[n] = p
    mixer_exchange = _reduce_chip(mixer, swap, rest[2:], core)
    du_b = toks(du_b)
    g_ab_re = jnp.sum(da_re8, axis=0).reshape(32, 64)
    g_ab_im = jnp.sum(da_im8, axis=0).reshape(32, 64)
    d_lr, d_li, d_ldt, d_br_t, d_bi_t = ssm_prep_bwd(
        small["ssm_a_re"], small["ssm_a_im"], log_dt, br_t, bi_t,
        g_ab_re, g_ab_im, _block_diag_t(dbb_re), _block_diag_t(dbb_im))
    as_gcn = lambda t: t.transpose(1, 0, 2).reshape(SSM_W, 64)
    gs = {
        "ssm_a_re": d_lr, "ssm_a_im": d_li, "ssm_log_dt": d_ldt.reshape(1, 32),
        "ssm_b_re": as_gcn(d_br_t), "ssm_b_im": as_gcn(d_bi_t),
        "ssm_c_re": as_gcn(_block_diag_t(dcb_re)), "ssm_c_im": as_gcn(_block_diag_t(dcb_im)),
        "ssm_d": dd.reshape(32, 16).T,
    }
    ssm_gather = small_comm([gs[n] for n in SSM_SMALL])
    dqkv_b, *rest = attn_bwd(qkv, tables, dattn, attn, lse, comm=join_comms([mixer_exchange, ssm_gather]))
    for n, p in zip(mixer, rest):
        parts[n] = p
    ssm_shares = rest[len(mixer):]
    dqkv_b = toks(dqkv_b)
    d_qkv = matmul(dqkv_b, h, "tn", F32, "d_w_qkv")
    d_u = matmul(du_b, h, "tn", F32, "d_w_u")
    d_gate = matmul(dgl_b, h, "tn", F32, "d_w_gate")
    gw["w_in"] = jnp.concatenate([_qkv_order(d_qkv, back=True), d_u, d_gate], axis=0)
    swap = _reduce_start(["w_in"], gw, shard_shapes)
    dh, *got = matmul(dqkv_b, w_qkv, "nn", F32, "d_h_qkv", comm=swap)
    chip_sum = add_halves("w_in", swap.ins[0], got[0], core)
    sems, chip_sum, land, token = split_start(chip_sum, "w_in_reduce_start", per_chip=True)
    grad_x, dg_mix = mix_in_bwd([du_b, dgl_b], [w_u, w_gate], dh, x, g_mix + token[0, 0], dx1)
    gs_norm = {"norm_mix_g": dg_mix, "norm_ffn_g": dg_ffn, "norm_final_g": dg_final}
    return loss, grad_x, parts, ssm_shares, gs_norm, (sems, chip_sum, land)


ANY = pl.BlockSpec(memory_space=pl.ANY)
BIG = ("w_in", "w_glu", "w_attn_out", "w_out", "w_ffn_gate", "w_ffn_up", "w_ffn_down")
TRANSPOSED = ("w_in", "w_ffn_gate", "w_ffn_up")
ROW_SHARDED = TRANSPOSED + ("w_out", "w_ffn_down")
SMALL = ("norm_mix_g", "ssm_a_re", "ssm_a_im", "ssm_log_dt", "ssm_b_re", "ssm_b_im", "ssm_c_re", "ssm_c_im",
         "ssm_d", "norm_ffn_g", "norm_final_g")
WEIGHTS = ("norm_mix_g", "w_in", "ssm_a_re", "ssm_a_im", "ssm_log_dt", "ssm_b_re", "ssm_b_im", "ssm_c_re",
           "ssm_c_im", "ssm_d", "w_glu", "w_attn_out", "w_out", "norm_ffn_g", "w_ffn_gate", "w_ffn_up",
           "w_ffn_down", "norm_final_g")
SSM_SMALL = SMALL[1:9]
NORM_SMALL = (SMALL[0],) + SMALL[9:]
NORM_ROWS = 32
N_BIG = len(BIG)


def _position():
    return lax.axis_index("x"), lax.axis_index("y"), lax.axis_index("c")


def _other_chips(x, y):
    return [(1 - x, y), (x, 1 - y), (1 - x, 1 - y)]


def _remote(src, dst, send_sem, recv_sem, device):
    return pltpu.make_async_remote_copy(src_ref=src, dst_ref=dst, send_sem=send_sem, recv_sem=recv_sem,
                                        device_id=device, device_id_type=MESH)


_later = functools.partial


def _two_level_phases(copies):
    def first(*refs):
        locals_, sends, _, _, _ = copies(*refs)
        for cp in locals_ + sends:
            cp().start()

    def mid(*refs):
        _, _, arrived, passed, _ = copies(*refs)
        for got, cp in zip(arrived, passed):
            got().wait_recv()
            cp().start()

    def last(*refs):
        locals_, sends, _, passed, from_sibling = copies(*refs)
        for cp in from_sibling:
            cp().wait_recv()
        for cp in sends + passed:
            cp().wait_send()
        for cp in locals_:
            cp().wait()

    return first, mid, last


def _half(ref, chip, which):
    rows = ref.shape[1] // 2
    return ref.at[chip, pl.ds(which * rows, rows), :]


class Comm:
    def __init__(self, ins, out_shapes, sems, first, mid, last):
        self.ins, self.out_shapes, self.sems = list(ins), list(out_shapes), list(sems)
        self.first, self.mid, self.last = first, mid, last


def join_comms(comms):
    def cut(refs_by_kind):
        offs, parts = [0, 0, 0], []
        for cm in comms:
            sizes = (len(cm.ins), len(cm.out_shapes), len(cm.sems))
            parts.append(tuple(refs_by_kind[k][offs[k]:offs[k] + sizes[k]] for k in range(3)))
            offs = [o + s for o, s in zip(offs, sizes)]
        return parts

    def phase(which):
        def run(ins, outs, sems):
            for cm, part in zip(comms, cut((ins, outs, sems))):
                fn = getattr(cm, which)
                if fn is not None:
                    fn(*part)
        return run

    return Comm(sum((cm.ins for cm in comms), []), sum((cm.out_shapes for cm in comms), []),
                sum((cm.sems for cm in comms), []), phase("first"), phase("mid"), phase("last"))


def _comm_operands(comm):
    if comm is None:
        return [], [], []
    return comm.ins, comm.out_shapes, comm.sems


def _comm_begin(comm, refs, step, n_steps):
    if comm is None:
        return
    pl.when(step == 0)(lambda: comm.first(*refs))
    if comm.mid is not None:
        pl.when(step == (n_steps * 3) // 4)(lambda: comm.mid(*refs))


def _comm_end(comm, refs, step, n_steps):
    if comm is not None:
        pl.when(step == n_steps - 1)(lambda: comm.last(*refs))


def _comm_refs(comm, refs, n_in, n_out):
    if comm is None:
        return list(refs), None
    ci, co, cs = len(comm.ins), len(comm.out_shapes), len(comm.sems)
    o0 = n_in + ci
    s0 = o0 + n_out + co
    host = list(refs[:n_in]) + list(refs[o0:o0 + n_out]) + list(refs[s0:len(refs) - cs])
    return host, (list(refs[n_in:o0]), list(refs[o0 + n_out:s0]), list(refs[len(refs) - cs:]))


def run_comm(comm, name):
    n_in, n_out = len(comm.ins), len(comm.out_shapes)

    def body(*refs):
        parts = (list(refs[:n_in]), list(refs[n_in:n_in + n_out]), list(refs[n_in + n_out:]))
        comm.first(*parts)
        if comm.mid is not None:
            comm.mid(*parts)
        comm.last(*parts)

    return pl.pallas_call(body, name=name, in_specs=[ANY] * n_in, out_specs=[ANY] * n_out,
                          out_shape=comm.out_shapes, scratch_shapes=comm.sems)(*comm.ins)


def hosted_call(work, comm, name, grid, in_specs, out_specs, out_shape, scratch_shapes, args, semantics):
    c_ins, c_outs, c_sems = _comm_operands(comm)
    n_steps = math.prod(grid)

    def body(*refs):
        host, c_refs = _comm_refs(comm, refs, len(in_specs), len(out_specs))
        step = 0
        for axis, size in enumerate(grid):
            step = step * size + pl.program_id(axis)
        _comm_begin(comm, c_refs, step, n_steps)
        work(*host)
        _comm_end(comm, c_refs, step, n_steps)

    return pl.pallas_call(
        body, name=name, grid=grid, in_specs=list(in_specs) + [ANY] * len(c_ins),
        out_specs=list(out_specs) + [ANY] * len(c_outs), out_shape=list(out_shape) + c_outs,
        scratch_shapes=list(scratch_shapes) + c_sems,
        compiler_params=_params(semantics if comm is None else ("arbitrary",) * len(grid)),
    )(*args, *c_ins)


def gather_comm(shards):
    n = len(shards)

    def copies(srcs, outs, sems):
        send_sems, recv_sems, local_sems = sems
        x, y, c = _position()
        me = 2 * x + y
        sibling = (x, y, 1 - c)
        chips = _other_chips(x, y)
        locals_ = [_later(pltpu.make_async_copy, s, o.at[me], local_sems.at[i])
                   for i, (s, o) in enumerate(zip(srcs, outs))]
        sends, arrived, passed, from_sibling = [], [], [], []
        for j, (px, py) in enumerate(chips):
            for i, (s, o) in enumerate(zip(srcs, outs)):
                rows = s.shape[0] // 2
                sends.append(_later(_remote, s.at[pl.ds(c * rows, rows), :], _half(o, me, c), send_sems.at[i, j],
                                    recv_sems.at[i, j], (px, py, c)))
                got = _half(o, 2 * px + py, c)
                arrived.append(_later(_remote, got, got, send_sems.at[i, j], recv_sems.at[i, j], (px, py, c)))
                passed.append(_later(_remote, got, got, send_sems.at[i, 3 + j], recv_sems.at[i, 3 + j], sibling))
                other = _half(o, 2 * px + py, 1 - c)
                from_sibling.append(_later(_remote, other, other, send_sems.at[i, 3 + j], recv_sems.at[i, 3 + j],
                                           sibling))
        return locals_, sends, arrived, passed, from_sibling

    return Comm(shards, [jax.ShapeDtypeStruct((N_CHIPS,) + s.shape, s.dtype) for s in shards],
                [pltpu.SemaphoreType.DMA((n, 6)), pltpu.SemaphoreType.DMA((n, 6)), pltpu.SemaphoreType.DMA((n,))],
                *_two_level_phases(copies))


HBM = pl.BlockSpec(memory_space=pltpu.HBM)
SEM = pl.BlockSpec(memory_space=pltpu.SEMAPHORE)
N_OTHER = N_CHIPS - 1


def _ici_halves(src_ref, land_ref, sems, per_chip):
    x, y, c = _position()
    me = 2 * x + y
    rows = land_ref.shape[1] // 2
    sends, arrivals = [], []
    for j, (px, py) in enumerate(_other_chips(x, y)):
        piece = src_ref.at[2 * px + py] if per_chip else src_ref.at[pl.ds(c * rows, rows), :]
        sends.append(_later(_remote, piece, _half(land_ref, me, c), sems[j], sems[N_OTHER + j], (px, py, c)))
        got = _half(land_ref, 2 * px + py, c)
        arrivals.append(_later(_remote, got, got, sems[j], sems[N_OTHER + j], (px, py, c)))
    return sends, arrivals


def split_start(src, name, per_chip=False):
    def body(src_ref, land_ref, *rest):
        sems, token = rest[:2 * N_OTHER], rest[-1]
        for cp in _ici_halves(src_ref, land_ref, sems, per_chip)[0]:
            cp().start()
        token[...] = jnp.zeros_like(token)

    rows, cols = (2 * src.shape[1], src.shape[2]) if per_chip else src.shape
    sem = pltpu.SemaphoreType.DMA(())
    land = (N_CHIPS, rows, cols)
    res = pl.pallas_call(
        body, name=name, in_specs=(HBM, HBM),
        out_specs=(SEM,) * (2 * N_OTHER) + (HBM, HBM, pl.BlockSpec(memory_space=pltpu.VMEM)),
        out_shape=(sem,) * (2 * N_OTHER) + (pltpu.HBM(src.shape, src.dtype), pltpu.HBM(land, src.dtype),
                                           jax.ShapeDtypeStruct((8, LANES), F32)),
        input_output_aliases={0: 2 * N_OTHER, 1: 2 * N_OTHER + 1},
        compiler_params=pltpu.CompilerParams(has_side_effects=pltpu.SideEffectType.DATAFLOW_SIDE_EFFECTING),
    )(pltpu.with_memory_space_constraint(src, pltpu.HBM),
      pltpu.with_memory_space_constraint(lax.empty(land, src.dtype), pltpu.HBM))
    return res[:2 * N_OTHER], res[2 * N_OTHER], res[2 * N_OTHER + 1], res[-1]


def split_wait(sems, src, land, after, name, per_chip=False):
    def body(src_ref, land_ref, *rest):
        sends, arrivals = _ici_halves(src_ref, land_ref, rest[:2 * N_OTHER], per_chip)
        for cp in sends:
            cp().wait_send()
        for cp in arrivals:
            cp().wait_recv()

    return pl.pallas_call(
        body, name=name, in_specs=(HBM, HBM) + (SEM,) * (2 * N_OTHER) + (ANY,) * len(after),
        out_specs=(HBM, HBM), out_shape=(pltpu.HBM(src.shape, src.dtype), pltpu.HBM(land.shape, land.dtype)),
        input_output_aliases={0: 0, 1: 1},
        compiler_params=pltpu.CompilerParams(has_side_effects=pltpu.SideEffectType.DATAFLOW_SIDE_EFFECTING),
    )(src, land, *sems, *after)


def handover(land, name, sums=None):
    n = N_OTHER + (sums is not None)

    def body(*refs):
        land_ref, send_sems, recv_sems = refs[0], refs[-2], refs[-1]
        x, y, c = _position()
        me = 2 * x + y
        sibling = (x, y, 1 - c)
        pieces = [(_half(land_ref, 2 * px + py, c), 2 * px + py) for px, py in _other_chips(x, y)]
        if sums is not None:
            pieces.append((refs[1].at[me], me))
        sends = [_remote(piece, _half(land_ref, chip, c), send_sems.at[j], recv_sems.at[j], sibling)
                 for j, (piece, chip) in enumerate(pieces)]
        for cp in sends:
            cp.start()
        for j, (_, chip) in enumerate(pieces):
            other = _half(land_ref, chip, 1 - c)
            _remote(other, other, send_sems.at[j], recv_sems.at[j], sibling).wait_recv()
        for cp in sends:
            cp.wait_send()

    args = (land,) + ((sums,) if sums is not None else ())
    return pl.pallas_call(
        body, name=name, in_specs=[ANY] * len(args), out_specs=ANY,
        out_shape=jax.ShapeDtypeStruct(land.shape, land.dtype), input_output_aliases={0: 0},
        scratch_shapes=[pltpu.SemaphoreType.DMA((n,)), pltpu.SemaphoreType.DMA((n,))],
    )(*args)


def swap_comm(grads):
    n = len(grads)

    def copies(srcs, gots, sems):
        send_sems, recv_sems = sems
        x, y, c = _position()
        out = []
        for i, (s, o) in enumerate(zip(srcs, gots)):
            rows = s.shape[1] // 2
            out.append(_remote(s.at[:, pl.ds((1 - c) * rows, rows), :], o, send_sems.at[i], recv_sems.at[i],
                               (x, y, 1 - c)))
        return out

    def first(srcs, gots, sems):
        for cp in copies(srcs, gots, sems):
            cp.start()

    def last(srcs, gots, sems):
        for cp in copies(srcs, gots, sems):
            cp.wait()

    return Comm(grads, [jax.ShapeDtypeStruct((N_CHIPS, g.shape[1] // 2, g.shape[2]), g.dtype) for g in grads],
                [pltpu.SemaphoreType.DMA((n,)), pltpu.SemaphoreType.DMA((n,))], first, None, last)


def add_halves(name, g, got, core):
    _, half, cols = got.shape
    mine = pl.BlockSpec((None, half, cols), lambda k, c_ref: (k, c_ref[0], 0))
    other = pl.BlockSpec((None, half, cols), lambda k, c_ref: (k, 0, 0))

    def body(c_ref, g_ref, got_ref, o_ref):
        o_ref[...] = (g_ref[...] + got_ref[...]).astype(BF16)

    return pl.pallas_call(
        body, name="add_halves_" + name,
        grid_spec=pltpu.PrefetchScalarGridSpec(num_scalar_prefetch=1, grid=(N_CHIPS,), in_specs=[mine, other],
                                               out_specs=other),
        out_shape=jax.ShapeDtypeStruct(got.shape, BF16),
        compiler_params=_params(("parallel",)),
    )(core, g, got)


def exchange_comm(parts):
    n = len(parts)

    def copies(srcs, outs, sems):
        send_sems, recv_sems, local_sems = sems
        x, y, c = _position()
        me = 2 * x + y
        sibling = (x, y, 1 - c)
        chips = _other_chips(x, y)
        locals_, sends, arrived, passed, from_sibling = [], [], [], [], []
        for i, (s, o) in enumerate(zip(srcs, outs)):
            locals_.append(_later(pltpu.make_async_copy, s.at[me], _half(o, me, c), local_sems.at[i]))
            sends.append(_later(_remote, s.at[me], _half(o, me, c), send_sems.at[i, 3], recv_sems.at[i, 3], sibling))
            other = _half(o, me, 1 - c)
            from_sibling.append(_later(_remote, other, other, send_sems.at[i, 3], recv_sems.at[i, 3], sibling))
        for j, (px, py) in enumerate(chips):
            for i, (s, o) in enumerate(zip(srcs, outs)):
                sends.append(_later(_remote, s.at[2 * px + py], _half(o, me, c), send_sems.at[i, j],
                                    recv_sems.at[i, j], (px, py, c)))
                got = _half(o, 2 * px + py, c)
                arrived.append(_later(_remote, got, got, send_sems.at[i, j], recv_sems.at[i, j], (px, py, c)))
                passed.append(_later(_remote, got, got, send_sems.at[i, 4 + j], recv_sems.at[i, 4 + j], sibling))
                other = _half(o, 2 * px + py, 1 - c)
                from_sibling.append(_later(_remote, other, other, send_sems.at[i, 4 + j], recv_sems.at[i, 4 + j],
                                           sibling))
        return locals_, sends, arrived, passed, from_sibling

    return Comm(parts, [jax.ShapeDtypeStruct((N_CHIPS, 2 * p.shape[1], p.shape[2]), p.dtype) for p in parts],
                [pltpu.SemaphoreType.DMA((n, 7)), pltpu.SemaphoreType.DMA((n, 7)), pltpu.SemaphoreType.DMA((n,))],
                *_two_level_phases(copies))


def small_comm(shares):
    n = len(shares)

    def copies(srcs, outs, sems):
        send_sems, recv_sems, local_sems = sems
        x, y, c = _position()
        me = 4 * x + 2 * y + c
        flips = [(fx, fy, fc) for fx in (0, 1) for fy in (0, 1) for fc in (0, 1)][1:]
        peers = [(1 - x if fx else x, 1 - y if fy else y, 1 - c if fc else c) for fx, fy, fc in flips]
        locals_, sends, arrived = [], [], []
        for i, (src_ref, out_ref) in enumerate(zip(srcs, outs)):
            locals_.append(_later(pltpu.make_async_copy, src_ref, out_ref.at[me], local_sems.at[i]))
            for j, (px, py, pc) in enumerate(peers):
                sends.append(_later(_remote, src_ref, out_ref.at[me], send_sems.at[i, j], recv_sems.at[i, j],
                                    (px, py, pc)))
                got = out_ref.at[4 * px + 2 * py + pc]
                arrived.append(_later(_remote, got, got, send_sems.at[i, j], recv_sems.at[i, j], (px, py, pc)))
        return locals_, sends, arrived

    def first(*refs):
        locals_, sends, _ = copies(*refs)
        for cp in locals_ + sends:
            cp().start()

    def last(*refs):
        locals_, sends, arrived = copies(*refs)
        for cp in arrived:
            cp().wait_recv()
        for cp in sends:
            cp().wait_send()
        for cp in locals_:
            cp().wait()

    return Comm(shares, [jax.ShapeDtypeStruct((N_DEV,) + s.shape, s.dtype) for s in shares],
                [pltpu.SemaphoreType.DMA((n, 7)), pltpu.SemaphoreType.DMA((n, 7)), pltpu.SemaphoreType.DMA((n,))],
                first, None, last)


def _adam_fn(w, g, m, v):
    m = ADAM_B1 * m + (1.0 - ADAM_B1) * g
    v = ADAM_B2 * v + (1.0 - ADAM_B2) * jnp.square(g)
    m_hat = m / (1.0 - ADAM_B1 ** ADAM_STEP)
    v_hat = v / (1.0 - ADAM_B2 ** ADAM_STEP)
    return -ADAM_LR * (m_hat / (jnp.sqrt(v_hat) + ADAM_EPS) + ADAM_WD * w), m, v


def adam_big(name, parts, w, m, v):
    rows, cols = w.shape
    tm = _pick(rows, 384, 16)

    def fn(p0, p1, p2, p3, wv, mv, vv):
        g = ((p0.astype(F32) + p1.astype(F32)) + p2.astype(F32)) + p3.astype(F32)
        return (g,) + _adam_fn(wv, g, mv, vv)

    return rowwise(fn, [parts, w, m, v], [(cols, F32)] * 4, "adam_" + name, tm=tm, rows=rows)


def adam_small(name, gathered, w, m, v):
    def body(g_ref, w_ref, m_ref, v_ref, go_ref, d_ref, mo_ref, vo_ref):
        g = g_ref[0]
        for k in range(1, N_DEV):
            g = g + g_ref[k]
        go_ref[...] = g
        d_ref[...], mo_ref[...], vo_ref[...] = _adam_fn(w_ref[...], g, m_ref[...], v_ref[...])

    return pl.pallas_call(body, name=name, out_shape=[jax.ShapeDtypeStruct(w.shape, F32)] * 4,
                          compiler_params=_params())(gathered, w, m, v)


def _ssm_2d(name, t):
    t = t[0] if t.ndim > 2 else t
    if name in ("ssm_b_re", "ssm_b_im"):
        return t.transpose(0, 2, 1).reshape(SSM_W, 64)
    if name in ("ssm_c_re", "ssm_c_im"):
        return t.reshape(SSM_W, 64)
    return t.T if name == "ssm_d" else t


def _ssm_back(name, t):
    if name in ("ssm_b_re", "ssm_b_im"):
        return t.reshape(32, 16, 64).transpose(0, 2, 1)[None]
    if name in ("ssm_c_re", "ssm_c_im"):
        return t.reshape(1, 32, 16, 64)
    if name == "ssm_d":
        return t.T[None]
    return t if name == "ssm_log_dt" else t[None]


def adam_ssm(shares, w, m, v):
    n = len(w)

    def body(*refs):
        ins, outs = refs[:4 * n], refs[4 * n:]
        for i in range(n):
            g_ref, w_ref, m_ref, v_ref = (ins[k * n + i] for k in range(4))
            g = g_ref[0]
            for k in range(1, N_DEV):
                g = g + g_ref[k]
            outs[4 * i][...] = g
            outs[4 * i + 1][...], outs[4 * i + 2][...], outs[4 * i + 3][...] = _adam_fn(w_ref[...], g, m_ref[...],
                                                                                      v_ref[...])

    out_shape = [jax.ShapeDtypeStruct(t.shape, F32) for t in w for _ in range(4)]
    res = pl.pallas_call(body, name="adam_ssm", out_shape=out_shape, compiler_params=_params())(*shares, *w, *m, *v)
    return [res[4 * i:4 * i + 4] for i in range(n)]


def _pack_small(names, vals, rows, last=None):
    flat = [vals[n].reshape(-1) for n in names]
    if last is not None:
        flat.append(last.reshape(-1))
    flat = jnp.concatenate(flat)
    return jnp.pad(flat, (0, rows * LANES - flat.shape[0])).reshape(rows, LANES)


def _unpack_small(names, pack, shapes):
    flat, out, off = pack.reshape(-1), {}, 0
    for n in names:
        size = math.prod(shapes[n])
        out[n] = flat[off:off + size].reshape(shapes[n])
        off += size
    return out, flat[off]


def _to_slots(name, g, shard_shape):
    rows, cols = shard_shape
    if name in ROW_SHARDED:
        return g.reshape(N_CHIPS, rows, cols)
    return g.reshape(rows, N_CHIPS, cols).transpose(1, 0, 2)


def _from_slots(name, s):
    _, rows, cols = s.shape
    if name in ROW_SHARDED:
        return s.reshape(N_CHIPS * rows, cols)
    return s.transpose(1, 0, 2).reshape(rows, N_CHIPS * cols)


def kernel(x, norm_mix_g, w_in, ssm_a_re, ssm_a_im, ssm_log_dt, ssm_b_re, ssm_b_im, ssm_c_re, ssm_c_im, ssm_d, w_glu, w_attn_out, w_out, norm_ffn_g, w_ffn_gate, w_ffn_up, w_ffn_down, norm_final_g, loss_target, m_norm_mix_g, m_w_in, m_ssm_a_re, m_ssm_a_im, m_ssm_log_dt, m_ssm_b_re, m_ssm_b_im, m_ssm_c_re, m_ssm_c_im, m_ssm_d, m_w_glu, m_w_attn_out, m_w_out, m_norm_ffn_g, m_w_ffn_gate, m_w_ffn_up, m_w_ffn_down, m_norm_final_g, v_norm_mix_g, v_w_in, v_ssm_a_re, v_ssm_a_im, v_ssm_log_dt, v_ssm_b_re, v_ssm_b_im, v_ssm_c_re, v_ssm_c_im, v_ssm_d, v_w_glu, v_w_attn_out, v_w_out, v_norm_ffn_g, v_w_ffn_gate, v_w_ffn_up, v_w_ffn_down, v_norm_final_g):
    given = dict(locals())
    def local(name, prefix=""):
        t = given[prefix + name][0]
        return t.T if name in TRANSPOSED else t

    shard = {n: local(n) for n in BIG}
    shapes = {n: given[n].shape for n in WEIGHTS}

    small = {n: given[n] for n in SMALL}
    small_2d = dict(small)
    for n in ("ssm_a_re", "ssm_a_im", "ssm_b_re", "ssm_b_im", "ssm_c_re", "ssm_c_im", "ssm_d"):
        small_2d[n] = small[n][0]
    small_2d["norm_final_g"] = norm_final_g.reshape(1, D_MODEL)

    core = lax.axis_index("c").astype(jnp.int32).reshape(1)
    loss, grad_x, parts, ssm_shares, gs_norm, w_in_reduce = local_step(
        x.reshape(TOKENS, D_MODEL), loss_target.reshape(TOKENS, D_MODEL),
        {n: shard[n] for n in BIG}, small_2d, core)

    (norm_shares,) = run_comm(small_comm([_pack_small(NORM_SMALL, gs_norm, NORM_ROWS, last=loss)]),
                              "gather_norm_grads")
    small_out = [{} for _ in range(4)]
    packs = [_pack_small(NORM_SMALL, {n: given[p + n] for n in NORM_SMALL}, NORM_ROWS) for p in ("", "m_", "v_")]
    for kind, t in enumerate(adam_small("adam_norm_gains", norm_shares, *packs)):
        vals, after = _unpack_small(NORM_SMALL, t, shapes)
        small_out[kind].update(vals)
        if kind == 0:
            total_loss = after
    ssm_in = [[_ssm_2d(n, given[p + n]) for n in SSM_SMALL] for p in ("", "m_", "v_")]
    for n, res in zip(SSM_SMALL, adam_ssm(ssm_shares, *ssm_in)):
        for kind, t in enumerate(res):
            small_out[kind][n] = _ssm_back(n, t)

    big_out, updated = {}, {}
    for n in BIG[1:] + BIG[:1]:
        if n == "w_in":
            sems, chip_sum, land = w_in_reduce
            behind = [updated[k][1] for k in BIG[1:]] + [norm_shares]
            chip_sum, land = split_wait(sems, chip_sum, land, behind, "w_in_reduce_wait", per_chip=True)
            land = handover(land, "w_in_reduce_handover", sums=chip_sum)
            me = 2 * lax.axis_index("x") + lax.axis_index("y")
            parts[n] = lax.dynamic_update_slice(land, lax.dynamic_slice_in_dim(chip_sum, me, 1, 0),
                                                (me, lax.axis_index("c") * chip_sum.shape[1], 0))
        updated[n] = adam_big(n, parts[n], shard[n], local(n, "m_"), local(n, "v_"))
        big_out[n] = [(t.T if n in TRANSPOSED else t)[None] for t in updated[n]]

    outs = [total_loss, grad_x.reshape(LOCAL_BATCH, SEQ, D_MODEL)]
    for kind in range(4):
        for n in WEIGHTS:
            outs.append(big_out[n][kind] if n in BIG else small_out[kind][n])
    return tuple(outs)
```

```python
import functools
import math

import jax
import jax.numpy as jnp
import numpy as np
from jax import lax
from jax.experimental import pallas as pl
from jax.experimental.pallas import tpu as pltpu

F32 = jnp.float32
BF16 = jnp.bfloat16
MESH = pl.DeviceIdType.MESH

D_MODEL = 1024
SEQ = 2048
LOCAL_BATCH = 2
TOKENS = LOCAL_BATCH * SEQ
HEAD_DIM = 64
HEADS_PER_GROUP = 4
GROUP_W = HEADS_PER_GROUP * HEAD_DIM
N_GROUPS = 3
DILATIONS = (1, 4, 16)
ATTN_BLOCK = 128
ROPE_DIM = 16
ROPE_THETA = 500000.0
QKV_W = 3 * N_GROUPS * GROUP_W
SSM_W = 512
SSM_STATE_W = 2048
SSM_LANE_BLOCKS = 4
GATE_W = 2 * D_MODEL
D_FF = 2816
RMS_EPS = 1e-6
NEG_INF = -1e30
ADAM_LR, ADAM_B1, ADAM_B2, ADAM_EPS, ADAM_WD, ADAM_STEP = 0.001, 0.9, 0.999, 1e-08, 0.01, 10
N_CHIPS = 4
N_DEV = 8

VMEM_LIMIT = 56 * 1024 * 1024
LANES = 128


def _params(sem=None):
    return pltpu.CompilerParams(dimension_semantics=sem, vmem_limit_bytes=VMEM_LIMIT)


def _pick(n, cap, align=LANES):
    best = None
    for d in range(align, min(n, cap) + 1, align):
        if n % d == 0:
            best = d
    return n if best is None or n <= cap else best


_DIMS = {"nn": (((1,), (0,)), ((), ())), "nt": (((1,), (1,)), ((), ())), "tn": (((0,), (0,)), ((), ()))}


def _dot(a, b, mode):
    return lax.dot_general(a, b, _DIMS[mode], preferred_element_type=F32)


def matmul(a, b, mode, out_dtype, name, add=None, comm=None):
    if mode == "nn":
        (m, k), n = a.shape, b.shape[1]
    elif mode == "nt":
        (m, k), n = a.shape, b.shape[0]
    else:
        (k, m), n = a.shape, b.shape[1]
    tn = _pick(n, 1408 if mode != "tn" else 512)
    tk = _pick(k, 2816) if mode != "tn" else k
    tm = _pick(m, 1408)
    out_bytes = jnp.dtype(out_dtype).itemsize

    def need(tm_):
        return 2 * 2 * (tm_ * tk + tk * tn) + tm_ * tn * (4 + 2 * out_bytes + (8 if add is not None else 0))

    while need(tm) > 40 * 1024 * 1024 and tm % 256 == 0:
        tm //= 2
    nk = k // tk
    a_spec = {"nn": pl.BlockSpec((tm, tk), lambda i, j, kk: (i, kk)),
              "nt": pl.BlockSpec((tm, tk), lambda i, j, kk: (i, kk)),
              "tn": pl.BlockSpec((tk, tm), lambda i, j, kk: (kk, i))}[mode]
    b_spec = {"nn": pl.BlockSpec((tk, tn), lambda i, j, kk: (kk, j)),
              "nt": pl.BlockSpec((tn, tk), lambda i, j, kk: (j, kk)),
              "tn": pl.BlockSpec((tk, tn), lambda i, j, kk: (kk, j))}[mode]
    o_spec = pl.BlockSpec((tm, tn), lambda i, j, kk: (i, j))

    def body(a_ref, b_ref, *rest):
        if add is not None:
            add_ref, o_ref, acc_ref = rest
        else:
            o_ref, acc_ref = rest
        part = _dot(a_ref[...], b_ref[...], mode)
        if nk == 1:
            res = part if add is None else part + add_ref[...]
            o_ref[...] = res.astype(out_dtype)
            return
        kk = pl.program_id(2)

        @pl.when(kk == 0)
        def _():
            acc_ref[...] = part

        @pl.when(kk > 0)
        def _():
            acc_ref[...] += part

        @pl.when(kk == nk - 1)
        def _():
            res = acc_ref[...] if add is None else acc_ref[...] + add_ref[...]
            o_ref[...] = res.astype(out_dtype)

    in_specs = [a_spec, b_spec] + ([o_spec] if add is not None else [])
    args = (a, b) + ((add,) if add is not None else ())
    res = hosted_call(
        body, comm, name, (m // tm, n // tn, nk), in_specs, [o_spec], [jax.ShapeDtypeStruct((m, n), out_dtype)],
        [pltpu.VMEM((tm, tn) if nk > 1 else (8, LANES), F32)], args, ("parallel", "parallel", "arbitrary"))
    return res[0] if comm is None else res


def matmul_rows(a, b, name, fn, extra, outs, accs=(), add=None, comm=None, tm=512):
    (m, k), n = a.shape, b.shape[1]
    n_fixed = 2 + (add is not None)
    row_spec = lambda cols: pl.BlockSpec((tm, cols), lambda i: (i, 0))
    in_specs = [row_spec(k), pl.BlockSpec((k, n), lambda i: (0, 0))] + ([row_spec(n)] if add is not None else [])
    in_specs += [pl.BlockSpec(e.shape, lambda i: (0, 0)) if e.shape[0] == 1 else row_spec(e.shape[1]) for e in extra]
    out_specs = [row_spec(c) for c, _ in outs] + [pl.BlockSpec((1, c), lambda i: (0, 0)) for c in accs]
    out_shape = [jax.ShapeDtypeStruct((m, c), dt) for c, dt in outs] + [jax.ShapeDtypeStruct((1, c), F32) for c in accs]

    def body(*refs):
        rows = _dot(refs[0][...], refs[1][...], "nn")
        if add is not None:
            rows = rows + refs[2][...]
        n_in = n_fixed + len(extra)
        res = fn(rows, *[r[...] for r in refs[n_fixed:n_in]])
        for r, v in zip(refs[n_in:n_in + len(outs)], res[:len(outs)]):
            r[...] = v.astype(r.dtype)
        first = pl.program_id(0) == 0
        for r, v in zip(refs[n_in + len(outs):], res[len(outs):]):
            @pl.when(first)
            def _(r=r, v=v):
                r[...] = v

            @pl.when(jnp.logical_not(first))
            def _(r=r, v=v):
                r[...] += v

    args = (a, b) + ((add,) if add is not None else ()) + tuple(extra)
    return hosted_call(body, comm, name, (m // tm,), in_specs, out_specs, out_shape, [], args, ("arbitrary",))


def _merge_specs(tm):
    half = lambda blk: pl.BlockSpec((tm, D_MODEL), functools.partial(lambda i, blk_: (i, blk_), blk_=blk))
    return [half(0), half(1), pl.BlockSpec((tm, GROUP_W), lambda i: (i, 0)),
            pl.BlockSpec((GROUP_W, D_MODEL), lambda i: (0, 0)), pl.BlockSpec((tm, SSM_W), lambda i: (i, 0)),
            pl.BlockSpec((SSM_W, GATE_W), lambda i: (0, 0))]


def _merge_operands(g0, g1, at, wa, yg, wg):
    z = _dot(yg[...], wg[...], "nn")
    return (g0[...].astype(F32), g1[...].astype(F32), _dot(at[...], wa[...], "nn"), z[:, :D_MODEL], z[:, D_MODEL:])


def merge_out_proj(gl, attn_b, w_attn_out, yg, w_glu, w_out, x, g_ffn):
    tm = 512

    def body(g0, g1, at, wa, yg_ref, wg, w_ref, x_ref, g_ref, m_ref, x1_ref, h2_ref):
        merged = _merge_fn(*_merge_operands(g0, g1, at, wa, yg_ref, wg)).astype(BF16)
        m_ref[...] = merged
        x1 = _dot(merged, w_ref[...], "nn") + x_ref[...]
        x1_ref[...] = x1
        h2_ref[...] = _rms(x1, g_ref[...]).astype(BF16)

    rows = pl.BlockSpec((tm, D_MODEL), lambda i: (i, 0))
    whole = pl.BlockSpec((D_MODEL, D_MODEL), lambda i: (0, 0))
    gain = pl.BlockSpec((1, D_MODEL), lambda i: (0, 0))
    tok = lambda dt: jax.ShapeDtypeStruct((TOKENS, D_MODEL), dt)
    return pl.pallas_call(
        body, name="merge_out_proj", grid=(TOKENS // tm,), in_specs=_merge_specs(tm) + [whole, rows, gain],
        out_specs=[rows] * 3, out_shape=[tok(BF16), tok(F32), tok(BF16)], compiler_params=_params(("parallel",)),
    )(gl, gl, attn_b, w_attn_out, yg, w_glu, w_out, x, g_ffn)


def merge_bwd(dx1_b, w_out, gl, attn_b, w_attn_out, yg, w_glu, comm=None):
    tm = 512

    def body(dx_ref, w_ref, g0, g1, at, wa, yg_ref, wg, dgl_ref, dad_ref, dz_ref, dat_ref, dyg_ref):
        dm = _dot(dx_ref[...], w_ref[...], "nt")
        _, vjp = jax.vjp(_merge_fn, *_merge_operands(g0, g1, at, wa, yg_ref, wg))
        dg0, dg1, dad, dza, dzb = vjp(dm)
        dat_ref[...] = _dot(dad.astype(BF16), wa[...], "nt")
        dgl_ref[:, :D_MODEL] = dg0.astype(BF16)
        dgl_ref[:, D_MODEL:] = dg1.astype(BF16)
        dad_ref[...] = dad.astype(BF16)
        dz_ref[:, :D_MODEL] = dza.astype(BF16)
        dz_ref[:, D_MODEL:] = dzb.astype(BF16)
        dyg_ref[...] = _dot(dz_ref[...], wg[...], "nt")

    rows = pl.BlockSpec((tm, D_MODEL), lambda i: (i, 0))
    wide = pl.BlockSpec((tm, GATE_W), lambda i: (i, 0))
    whole = pl.BlockSpec((D_MODEL, D_MODEL), lambda i: (0, 0))
    return hosted_call(
        body, comm, "merge_bwd", (TOKENS // tm,), [rows, whole] + _merge_specs(tm),
        [wide, rows, wide, pl.BlockSpec((tm, GROUP_W), lambda i: (i, 0)), pl.BlockSpec((tm, SSM_W), lambda i: (i, 0))],
        [jax.ShapeDtypeStruct((TOKENS, GATE_W), BF16), jax.ShapeDtypeStruct((TOKENS, D_MODEL), BF16),
         jax.ShapeDtypeStruct((TOKENS, GATE_W), BF16), jax.ShapeDtypeStruct((TOKENS, GROUP_W), F32),
         jax.ShapeDtypeStruct((TOKENS, SSM_W), F32)], [],
        (dx1_b, w_out, gl, gl, attn_b, w_attn_out, yg, w_glu), ("arbitrary",))


FFN_TM, FFN_TN = 512, 1408


def ffn_in(h2, wg_t, wu_t, comm=None):
    def body(h_ref, wg_ref, wu_ref, a_ref, b_ref, act_ref):
        hv = h_ref[...]
        a, b = _dot(hv, wg_ref[...], "nt"), _dot(hv, wu_ref[...], "nt")
        a_ref[...] = a.astype(BF16)
        b_ref[...] = b.astype(BF16)
        act_ref[...] = _swiglu_fn(a, b).astype(BF16)

    rows = pl.BlockSpec((FFN_TM, D_MODEL), lambda i, j: (i, 0))
    wts = pl.BlockSpec((FFN_TN, D_MODEL), lambda i, j: (j, 0))
    out = pl.BlockSpec((FFN_TM, FFN_TN), lambda i, j: (i, j))
    return hosted_call(body, comm, "ffn_in", (TOKENS // FFN_TM, D_FF // FFN_TN), [rows, wts, wts], [out] * 3,
                       [jax.ShapeDtypeStruct((TOKENS, D_FF), BF16)] * 3, [], (h2, wg_t, wu_t),
                       ("parallel", "parallel"))


def ffn_in_bwd(dx2_b, wd, a, b):
    def body(dx_ref, wd_ref, a_ref, b_ref, da_ref, db_ref):
        dact = _dot(dx_ref[...], wd_ref[...], "nt")
        av, bv = a_ref[...].astype(F32), b_ref[...].astype(F32)
        sig = jax.nn.sigmoid(av)
        act = av * sig
        da_ref[...] = (dact * bv * (sig * (1.0 + av - act))).astype(BF16)
        db_ref[...] = (dact * act).astype(BF16)

    rows = pl.BlockSpec((FFN_TM, D_MODEL), lambda i, j: (i, 0))
    wts = pl.BlockSpec((FFN_TN, D_MODEL), lambda i, j: (j, 0))
    out = pl.BlockSpec((FFN_TM, FFN_TN), lambda i, j: (i, j))
    return pl.pallas_call(
        body, name="ffn_in_bwd", grid=(TOKENS // FFN_TM, D_FF // FFN_TN), in_specs=[rows, wts, out, out],
        out_specs=[out] * 2, out_shape=[jax.ShapeDtypeStruct((TOKENS, D_FF), BF16)] * 2,
        compiler_params=_params(("parallel", "parallel")),
    )(dx2_b, wd, a, b)


def mix_in_bwd(grads, weights, partial, x, g, skip, comm=None):
    n = len(grads)
    tm = 512

    def body(*refs):
        a_refs, b_refs = refs[:n], refs[n:2 * n]
        part_ref, x_ref, g_ref, skip_ref, gx_ref, dg_ref = refs[2 * n:]
        dh = part_ref[...]
        for a_ref, b_ref in zip(a_refs, b_refs):
            dh = dh + _dot(a_ref[...], b_ref[...], "nn")
        _, vjp = jax.vjp(_rms, x_ref[...], g_ref[...])
        dx, dg = vjp(dh)
        gx_ref[...] = dx + skip_ref[...]
        first = pl.program_id(0) == 0

        @pl.when(first)
        def _():
            dg_ref[...] = dg

        @pl.when(jnp.logical_not(first))
        def _():
            dg_ref[...] += dg

    rows = pl.BlockSpec((tm, D_MODEL), lambda i: (i, 0))
    gain = pl.BlockSpec((1, D_MODEL), lambda i: (0, 0))
    in_specs = [pl.BlockSpec((tm, a.shape[1]), lambda i: (i, 0)) for a in grads]
    in_specs += [pl.BlockSpec(b.shape, lambda i: (0, 0)) for b in weights]
    return hosted_call(
        body, comm, "mix_in_bwd", (TOKENS // tm,), in_specs + [rows, rows, gain, rows], [rows, gain],
        [jax.ShapeDtypeStruct((TOKENS, D_MODEL), F32), jax.ShapeDtypeStruct((1, D_MODEL), F32)], [],
        (*grads, *weights, partial, x, g, skip), ("arbitrary",))


def rowwise(fn, ins, outs, name, accs=(), tm=256, rows=TOKENS, comm=None):
    in_specs, args = [], []
    for item in ins:
        arr, width, blk = item if isinstance(item, tuple) else (item, None, 0)
        if arr.ndim == 3:
            for k in range(arr.shape[0]):
                in_specs.append(pl.BlockSpec((None, tm, arr.shape[2]), functools.partial(lambda i, k_: (k_, i, 0), k_=k)))
                args.append(arr)
            continue
        if arr.shape[0] == 1:
            in_specs.append(pl.BlockSpec(arr.shape, lambda i: (0, 0)))
        elif width is None:
            in_specs.append(pl.BlockSpec((tm, arr.shape[1]), lambda i: (i, 0)))
        else:
            in_specs.append(pl.BlockSpec((tm, width), functools.partial(lambda i, blk_: (i, blk_), blk_=blk)))
        args.append(arr)
    out_specs = [pl.BlockSpec((tm, c), lambda i: (i, 0)) for c, _ in outs]
    out_specs += [pl.BlockSpec((1, c), lambda i: (0, 0)) for c in accs]
    out_shape = [jax.ShapeDtypeStruct((rows, c), dt) for c, dt in outs]
    out_shape += [jax.ShapeDtypeStruct((1, c), F32) for c in accs]
    n_in, n_out = len(args), len(outs)
    c_ins, c_outs, c_sems = _comm_operands(comm)

    def body(*refs):
        refs, c_refs = _comm_refs(comm, refs, n_in, n_out + len(accs))
        step = pl.program_id(0)
        _comm_begin(comm, c_refs, step, rows // tm)
        res = fn(*[r[...] for r in refs[:n_in]])
        for r, v in zip(refs[n_in:n_in + n_out], res[:n_out]):
            r[...] = v.astype(r.dtype)
        first = step == 0
        for r, v in zip(refs[n_in + n_out:], res[n_out:]):
            @pl.when(first)
            def _(r=r, v=v):
                r[...] = v

            @pl.when(jnp.logical_not(first))
            def _(r=r, v=v):
                r[...] += v
        _comm_end(comm, c_refs, step, rows // tm)

    return pl.pallas_call(
        body, name=name, grid=(rows // tm,), in_specs=in_specs + [ANY] * len(c_ins),
        out_specs=out_specs + [ANY] * len(c_outs), out_shape=out_shape + c_outs, scratch_shapes=c_sems,
        compiler_params=_params(("arbitrary",)),
    )(*args, *c_ins)


def first_norm(x, g, others, comm=None):
    tm, n = 256, len(others)

    def body(x_ref, g_ref, *rest):
        srcs, h_ref, dsts = rest[:n], rest[n], rest[n + 1:]
        h_ref[...] = _rms(x_ref[...], g_ref[...]).astype(BF16)
        for k, (s, d) in enumerate(zip(srcs, dsts)):
            @pl.when(pl.program_id(0) == k)
            def _(s=s, d=d):
                d[...] = s[...].astype(BF16)

    rows = pl.BlockSpec((tm, D_MODEL), lambda i: (i, 0))
    whole = [pl.BlockSpec(a.shape, lambda i: (0, 0)) for a in others]
    return hosted_call(
        body, comm, "norm_mix", (TOKENS // tm,), [rows, pl.BlockSpec((1, D_MODEL), lambda i: (0, 0))] + whole,
        [rows] + whole, [jax.ShapeDtypeStruct((TOKENS, D_MODEL), BF16)]
        + [jax.ShapeDtypeStruct(a.shape, BF16) for a in others], [], (x, g, *others), ("arbitrary",))


def _rms(x, g):
    return x * lax.rsqrt(jnp.mean(x * x, axis=-1, keepdims=True) + RMS_EPS) * g


def _colsum(v):
    return jnp.sum(v, axis=0, keepdims=True)


PAIR_W = 2 * HEAD_DIM
N_PAIRS = HEADS_PER_GROUP // 2


def _qkv_order(w_t, back=False):
    dims = (N_PAIRS, N_GROUPS, 3) if back else (3, N_GROUPS, N_PAIRS)
    return w_t.reshape(dims + (PAIR_W, w_t.shape[1])).transpose(2, 1, 0, 3, 4).reshape(QKV_W, w_t.shape[1])


def _rope_tables():
    half = ROPE_DIM // 2
    inv = np.power(np.float32(ROPE_THETA), -np.arange(half, dtype=np.float32) * np.float32(2.0 / ROPE_DIM))
    ang = (np.arange(SEQ, dtype=np.float32)[:, None] * inv[None, :]).astype(np.float32)
    cos, sin = np.cos(ang), np.sin(ang)
    zeros = np.zeros((SEQ, HEAD_DIM - ROPE_DIM), np.float32)
    zh = np.zeros((SEQ, half), np.float32)
    c = np.concatenate([cos, cos, zeros + 1.0], axis=1)
    sa = np.concatenate([-sin, zh, zeros], axis=1)
    sb = np.concatenate([zh, sin, zeros], axis=1)
    return [jnp.asarray(np.tile(t, (1, 2)), F32) for t in (c, sa, sb)]


def _rope_fwd(x, c, sa, sb):
    return x * c + pltpu.roll(x, PAIR_W - 8, 1) * sa + pltpu.roll(x, 8, 1) * sb


def _rope_bwd(dy, c, sa, sb):
    return dy * c + pltpu.roll(dy * sb, PAIR_W - 8, 1) + pltpu.roll(dy * sa, 8, 1)


def _band_masks():
    row = lax.broadcasted_iota(jnp.int32, (ATTN_BLOCK, ATTN_BLOCK), 0)
    col = lax.broadcasted_iota(jnp.int32, (ATTN_BLOCK, ATTN_BLOCK), 1)
    return col <= row, col >= row


def _stack_rows(t):
    return jnp.concatenate([t, t], axis=0)


def _stack_heads(t, first_head):
    return jnp.concatenate([jnp.where(first_head, t, 0), jnp.where(first_head, 0, t)], axis=0)


def _per_head(fn):
    return jnp.concatenate([fn(slice(h * HEAD_DIM, (h + 1) * HEAD_DIM)) for h in range(2)], axis=1)


def _slab_spec(kind):
    return pl.BlockSpec((None, SEQ, PAIR_W), lambda b, p, g: (b, 0, p * 3 * N_GROUPS + g * 3 + kind))


_TABLE_SPEC = pl.BlockSpec((SEQ, PAIR_W), lambda b, p, g: (0, 0))
_PAIR_SPEC = pl.BlockSpec((None, SEQ, PAIR_W), lambda b, p, g: (b, 0, p))


def _block_rows(dil, r, n):
    return pl.ds(n * (ATTN_BLOCK * dil) + r, ATTN_BLOCK, stride=dil)


def proj_qkv(h, w_qkv_t, tables, comm=None):
    tm = 1024
    pair_w = QKV_W // N_PAIRS
    scale = HEAD_DIM ** -0.5

    def body(h_ref, w_ref, c_ref, sa_ref, sb_ref, o_ref):
        rows = _dot(h_ref[...], w_ref[...], "nt")
        c, sa, sb = c_ref[...], sa_ref[...], sb_ref[...]
        for blk in range(pair_w // PAIR_W):
            cols = slice(blk * PAIR_W, (blk + 1) * PAIR_W)
            x = rows[:, cols]
            if blk % 3 == 0:
                x = _rope_fwd(x, c, sa, sb) * scale
            elif blk % 3 == 1:
                x = _rope_fwd(x, c, sa, sb)
            o_ref[:, cols] = x

    table = pl.BlockSpec((tm, PAIR_W), lambda i, j, : (i % (SEQ // tm), 0))
    res = hosted_call(
        body, comm, "proj_qkv", (TOKENS // tm, N_PAIRS),
        [pl.BlockSpec((tm, D_MODEL), lambda i, j: (i, 0)), pl.BlockSpec((pair_w, D_MODEL), lambda i, j: (j, 0)),
         table, table, table],
        [pl.BlockSpec((tm, pair_w), lambda i, j: (i, j))], [jax.ShapeDtypeStruct((TOKENS, QKV_W), F32)], [],
        (h, w_qkv_t, *tables), ("parallel", "parallel"))
    return res[0] if comm is None else res


def attn_fwd(qkv, comm=None):
    def body(qs, ks, v_ref, attn_b_ref, attn_ref, lse_ref, o0, o1, o2, l0, l1, l2):
        g = pl.program_id(2)
        cur_mask, prev_mask = _band_masks()
        first_head = lax.broadcasted_iota(jnp.int32, (ATTN_BLOCK, PAIR_W), 1) < HEAD_DIM

        def run(dil, o_slab, l_slab):
            nb = SEQ // dil // ATTN_BLOCK

            def block(idx, carry):
                r, n = lax.div(idx, nb), lax.rem(idx, nb)
                cur, prev = _block_rows(dil, r, n), _block_rows(dil, r, jnp.maximum(n - 1, 0))
                q = qs[cur, :].astype(BF16)
                kc, kp = ks[cur, :].astype(BF16), ks[prev, :].astype(BF16)
                vc, vp = v_ref[cur, :].astype(BF16), v_ref[prev, :].astype(BF16)
                q2 = _stack_heads(q, first_head)
                mask = _stack_rows(jnp.concatenate([jnp.logical_and(prev_mask, n > 0), cur_mask], axis=1))
                s2 = jnp.where(mask, _dot(q2, jnp.concatenate([kp, kc], axis=0), "nt"), NEG_INF)
                m = jnp.max(s2, axis=-1, keepdims=True)
                vcat, two = jnp.concatenate([vp, vc], axis=0), _stack_rows(first_head)
                vext = jnp.concatenate([jnp.where(two, vcat, 1), jnp.where(two, 1, vcat)], axis=1)
                r2 = _dot(jnp.exp(s2 - m).astype(BF16), vext, "nn")
                r0, r1 = r2[:ATTN_BLOCK, :PAIR_W], r2[ATTN_BLOCK:, PAIR_W:]
                num = jnp.where(first_head, r0, r1)
                den = pltpu.roll(jnp.where(first_head, r1, r0), HEAD_DIM, 1)
                o_slab[cur, :] = num / den
                l_slab[cur, :] = jnp.where(first_head, m[:ATTN_BLOCK], m[ATTN_BLOCK:]) + jnp.log(den)
                return carry

            lax.fori_loop(0, SEQ // ATTN_BLOCK, block, 0, unroll=4)

        for gi, (o_slab, l_slab) in enumerate(((o0, l0), (o1, l1), (o2, l2))):
            @pl.when(g == gi)
            def _(gi=gi, o_slab=o_slab, l_slab=l_slab):
                run(DILATIONS[gi], o_slab, l_slab)

        @pl.when(g == N_GROUPS - 1)
        def _():
            a, b, cc = l0[...], l1[...], l2[...]
            m = jnp.maximum(jnp.maximum(a, b), cc)
            e0, e1, e2 = jnp.exp(a - m), jnp.exp(b - m), jnp.exp(cc - m)
            tot = e0 + e1 + e2
            attn = (e0 * o0[...] + e1 * o1[...] + e2 * o2[...]) / tot
            attn_ref[...] = attn
            attn_b_ref[...] = attn.astype(BF16)
            lse_ref[...] = m + jnp.log(tot)

    shape = (LOCAL_BATCH, SEQ, GROUP_W)
    slab = pltpu.VMEM((SEQ, PAIR_W), F32)
    return hosted_call(
        body, comm, "attn_fwd", (LOCAL_BATCH, N_PAIRS, N_GROUPS),
        [_slab_spec(0), _slab_spec(1), _slab_spec(2)], [_PAIR_SPEC] * 3,
        [jax.ShapeDtypeStruct(shape, BF16), jax.ShapeDtypeStruct(shape, F32), jax.ShapeDtypeStruct(shape, F32)],
        [slab] * 6, (qkv, qkv, qkv), ("parallel", "parallel", "arbitrary"))


def attn_bwd(qkv, tables, dattn, attn, lse, comm=None):
    scale = HEAD_DIM ** -0.5

    def body(qs, ks, v_ref, c_ref, sa_ref, sb_ref, do_ref, out_ref, lse_ref, dqkv_ref, dl, dq_s, dk_s, dv_s):
        g = pl.program_id(2)
        c, sa, sb = c_ref[...], sa_ref[...], sb_ref[...]

        @pl.when(g == 0)
        def _():
            prod = do_ref[...] * out_ref[...]
            dl[...] = _per_head(
                lambda sl: jnp.broadcast_to(jnp.sum(prod[:, sl], axis=-1, keepdims=True), (SEQ, HEAD_DIM)))

        cur_mask, prev_mask = _band_masks()
        first_head = lax.broadcasted_iota(jnp.int32, (ATTN_BLOCK, PAIR_W), 1) < HEAD_DIM

        def run(dil):
            nb = SEQ // dil // ATTN_BLOCK

            def block(idx, carry):
                r, n = lax.div(idx, nb), lax.rem(idx, nb)
                cur = _block_rows(dil, r, n)
                prev = _block_rows(dil, r, jnp.maximum(n - 1, 0))
                nxt = _block_rows(dil, r, jnp.minimum(n + 1, nb - 1))
                q0, q1 = qs[cur, :].astype(BF16), qs[nxt, :].astype(BF16)
                kp, kc = ks[prev, :].astype(BF16), ks[cur, :].astype(BF16)
                vp, vc = v_ref[prev, :].astype(BF16), v_ref[cur, :].astype(BF16)
                do0, do1 = do_ref[cur, :].astype(BF16), do_ref[nxt, :].astype(BF16)
                lse0, lse1, dl0, dl1 = lse_ref[cur, :], lse_ref[nxt, :], dl[cur, :], dl[nxt, :]
                has_prev = jnp.logical_and(prev_mask, n > 0)
                has_next = jnp.logical_and(prev_mask, n < nb - 1)

                def per_row(t):
                    return jnp.concatenate([t[:, 0:1], t[:, HEAD_DIM:HEAD_DIM + 1]], axis=0)

                q20, q21 = _stack_heads(q0, first_head), _stack_heads(q1, first_head)
                do20, do21 = _stack_heads(do0, first_head), _stack_heads(do1, first_head)
                kcat, vcat = jnp.concatenate([kp, kc], axis=0), jnp.concatenate([vp, vc], axis=0)
                mask0 = _stack_rows(jnp.concatenate([has_prev, cur_mask], axis=1))
                p0 = jnp.where(mask0, jnp.exp(_dot(q20, kcat, "nt") - per_row(lse0)), 0.0)
                ds0 = (p0 * (_dot(do20, vcat, "nt") - per_row(dl0))).astype(BF16)
                p1 = jnp.where(_stack_rows(has_next), jnp.exp(_dot(q21, kc, "nt") - per_row(lse1)), 0.0)
                ds1 = (p1 * (_dot(do21, vc, "nt") - per_row(dl1))).astype(BF16)
                dq2 = _dot(ds0, kcat, "nn")
                dq_s[cur, :] = jnp.where(first_head, dq2[:ATTN_BLOCK], dq2[ATTN_BLOCK:])
                ds_cur = jnp.concatenate([ds0[:, ATTN_BLOCK:], ds1], axis=0)
                p_cur = jnp.concatenate([p0[:, ATTN_BLOCK:], p1], axis=0).astype(BF16)
                dk_s[cur, :] = _dot(ds_cur, jnp.concatenate([q20, q21], axis=0), "tn")
                dv_s[cur, :] = _dot(p_cur, jnp.concatenate([do20, do21], axis=0), "tn")
                return carry

            lax.fori_loop(0, SEQ // ATTN_BLOCK, block, 0, unroll=2)

        for gi in range(N_GROUPS):
            @pl.when(g == gi)
            def _(gi=gi):
                run(DILATIONS[gi])

        dqkv_ref[:, 0:PAIR_W] = _rope_bwd(dq_s[...] * scale, c, sa, sb).astype(BF16)
        dqkv_ref[:, PAIR_W:2 * PAIR_W] = _rope_bwd(dk_s[...], c, sa, sb).astype(BF16)
        dqkv_ref[:, 2 * PAIR_W:] = dv_s[...].astype(BF16)

    slab = pltpu.VMEM((SEQ, PAIR_W), F32)
    return hosted_call(
        body, comm, "attn_bwd", (LOCAL_BATCH, N_PAIRS, N_GROUPS),
        [_slab_spec(0), _slab_spec(1), _slab_spec(2), _TABLE_SPEC, _TABLE_SPEC, _TABLE_SPEC,
         _PAIR_SPEC, _PAIR_SPEC, _PAIR_SPEC],
        [pl.BlockSpec((None, SEQ, 3 * PAIR_W), lambda b, p, g: (b, 0, p * N_GROUPS + g))],
        [jax.ShapeDtypeStruct((LOCAL_BATCH, SEQ, QKV_W), BF16)],
        [slab] * 4, (qkv, qkv, qkv, *tables, dattn, attn, lse), ("parallel", "parallel", "arbitrary"))


def _discretize(lr, li, log_dt, br, bi):
    dt = jnp.exp(log_dt)
    mag = jnp.exp(lr * dt)
    ab_re, ab_im = mag * jnp.cos(li * dt), mag * jnp.sin(li * dt)
    den = lr * lr + li * li
    nr, ni = ab_re - 1.0, ab_im
    f_re = (nr * lr + ni * li) / den
    f_im = (ni * lr - nr * li) / den
    return ab_re, ab_im, f_re[None] * br - f_im[None] * bi, f_re[None] * bi + f_im[None] * br


def ssm_prep(lr, li, log_dt, br, bi):
    def body(lr_ref, li_ref, dt_ref, br_ref, bi_ref, *outs):
        for o, v in zip(outs, _discretize(lr_ref[...], li_ref[...], dt_ref[...], br_ref[...], bi_ref[...])):
            o[...] = v
    shapes = [lr, li, br, bi]
    return pl.pallas_call(body, name="ssm_prep",
                          out_shape=[jax.ShapeDtypeStruct(s.shape, F32) for s in shapes])(lr, li, log_dt, br, bi)


def ssm_prep_bwd(lr, li, log_dt, br, bi, g_ab_re, g_ab_im, g_bb_re, g_bb_im):
    def body(lr_ref, li_ref, dt_ref, br_ref, bi_ref, g0, g1, g2, g3, *outs):
        _, vjp = jax.vjp(_discretize, lr_ref[...], li_ref[...], dt_ref[...], br_ref[...], bi_ref[...])
        for o, v in zip(outs, vjp((g0[...], g1[...], g2[...], g3[...]))):
            o[...] = v
    shapes = [lr, li, log_dt, br, bi]
    return pl.pallas_call(body, name="ssm_prep_bwd",
                          out_shape=[jax.ShapeDtypeStruct(s.shape, F32) for s in shapes])(
        lr, li, log_dt, br, bi, g_ab_re, g_ab_im, g_bb_re, g_bb_im)


def _block_diag(t):
    per = SSM_STATE_W // SSM_LANE_BLOCKS // 64
    g = t.transpose(1, 0, 2).reshape(SSM_LANE_BLOCKS, per, 16, 64)
    eye = jnp.eye(per, dtype=t.dtype)
    return jnp.einsum("jgcn,gh->jgchn", g, eye).reshape(SSM_LANE_BLOCKS, per * 16, per * 64)


def _block_diag_t(m):
    per = SSM_STATE_W // SSM_LANE_BLOCKS // 64
    m5 = m.reshape(SSM_LANE_BLOCKS, per, 16, per, 64)
    d = jnp.einsum("jgchn,gh->jgcn", m5, jnp.eye(per, dtype=m.dtype))
    return d.reshape(SSM_LANE_BLOCKS * per, 16, 64).transpose(1, 0, 2)


def _cmul(ar, ai, br, bi):
    return ar * br - ai * bi, ar * bi + ai * br


def _power_tables(ar, ai, reverse):
    width = ar.shape[1]
    row = lax.broadcasted_iota(jnp.int32, (8, width), 0)
    pows = [(ar, ai)]
    for _ in range(7):
        pows.append(_cmul(pows[-1][0], pows[-1][1], ar, ai))
    steps = []
    for k in (1, 2, 4):
        keep = (row >= k) if not reverse else (row < 8 - k)
        steps.append((jnp.where(keep, pows[k - 1][0], 0.0), jnp.where(keep, pows[k - 1][1], 0.0)))
    cr = jnp.zeros((8, width), F32)
    ci = jnp.zeros((8, width), F32)
    for i in range(8):
        pr, pi = pows[i] if not reverse else pows[7 - i]
        cr = jnp.where(row == i, pr, cr)
        ci = jnp.where(row == i, pi, ci)
    return steps, (cr, ci)


SCAN_CHUNK = 2048
STATE_BLOCK = SSM_STATE_W // SSM_LANE_BLOCKS
CHAN_BLOCK = SSM_W // SSM_LANE_BLOCKS


def ssm_fwd(u, ab_re, ab_im, bb_re, bb_im, cb_re, cb_im, d_skip, comm=None):
    nt = SEQ // SCAN_CHUNK
    chan = pl.BlockSpec((None, SCAN_CHUNK, CHAN_BLOCK), lambda b, j, t: (b, t, j))
    state = pl.BlockSpec((None, SCAN_CHUNK, STATE_BLOCK), lambda b, j, t: (b, t, j))
    mat = pl.BlockSpec((None, CHAN_BLOCK, STATE_BLOCK), lambda b, j, t: (j, 0, 0))
    lane = pl.BlockSpec((1, STATE_BLOCK), lambda b, j, t: (0, j))
    dsp = pl.BlockSpec((1, CHAN_BLOCK), lambda b, j, t: (0, j))

    def body(u_ref, ar_ref, ai_ref, bbr_ref, bbi_ref, cbr_ref, cbi_ref, d_ref, y_ref, yg_ref, xr_ref, xi_ref,
             car_r, car_i):
        @pl.when(pl.program_id(2) == 0)
        def _():
            car_r[...] = jnp.zeros_like(car_r)
            car_i[...] = jnp.zeros_like(car_i)

        steps, (pr, pi) = _power_tables(ar_ref[...], ai_ref[...], reverse=False)
        uf = u_ref[...]
        ub = uf.astype(BF16)
        xr_ref[...] = _dot(ub, bbr_ref[...], "nn")
        xi_ref[...] = _dot(ub, bbi_ref[...], "nn")

        def tile(i, carry):
            cr, ci = carry
            sl = pl.ds(pl.multiple_of(i * 8, 8), 8)
            br, bi = xr_ref[sl, :], xi_ref[sl, :]
            for k, (sr, si) in zip((1, 2, 4), steps):
                tr, ti = _cmul(sr, si, pltpu.roll(br, k, 0), pltpu.roll(bi, k, 0))
                br, bi = br + tr, bi + ti
            tr, ti = _cmul(pr, pi, cr, ci)
            br, bi = br + tr, bi + ti
            xr_ref[sl, :] = br
            xi_ref[sl, :] = bi
            return br[7:8, :], bi[7:8, :]

        cr, ci = lax.fori_loop(0, SCAN_CHUNK // 8, tile, (car_r[0:1, :], car_i[0:1, :]), unroll=4)
        car_r[0:1, :] = cr
        car_i[0:1, :] = ci
        y = (_dot(xr_ref[...].astype(BF16), cbr_ref[...], "nt") - _dot(xi_ref[...].astype(BF16), cbi_ref[...], "nt")
             + d_ref[...] * uf)
        y_ref[...] = y
        yg_ref[...] = jax.nn.gelu(y).astype(BF16)

    return hosted_call(
        body, comm, "ssm_fwd", (LOCAL_BATCH, SSM_LANE_BLOCKS, nt),
        [chan, lane, lane, mat, mat, mat, mat, dsp], [chan, chan, state, state],
        [jax.ShapeDtypeStruct((LOCAL_BATCH, SEQ, SSM_W), F32), jax.ShapeDtypeStruct((LOCAL_BATCH, SEQ, SSM_W), BF16),
         jax.ShapeDtypeStruct((LOCAL_BATCH, SEQ, SSM_STATE_W), F32),
         jax.ShapeDtypeStruct((LOCAL_BATCH, SEQ, SSM_STATE_W), F32)],
        [pltpu.VMEM((8, STATE_BLOCK), F32), pltpu.VMEM((8, STATE_BLOCK), F32)],
        (u, ab_re, ab_im, bb_re, bb_im, cb_re, cb_im, d_skip), ("parallel", "parallel", "arbitrary"))


def ssm_bwd(dyg, y, u, xr, xi, ab_re, ab_im, bb_re, bb_im, cb_re, cb_im, d_skip, comm=None):
    nt = SEQ // SCAN_CHUNK
    ntile = SCAN_CHUNK // 8

    def rev(t):
        return nt - 1 - t

    chan = pl.BlockSpec((None, SCAN_CHUNK, CHAN_BLOCK), lambda j, b, t: (b, rev(t), j))
    state = pl.BlockSpec((None, SCAN_CHUNK, STATE_BLOCK), lambda j, b, t: (b, rev(t), j))
    before = pl.BlockSpec((None, 8, STATE_BLOCK), lambda j, b, t: (b, jnp.maximum(rev(t) * ntile - 1, 0), j))
    mat = pl.BlockSpec((None, CHAN_BLOCK, STATE_BLOCK), lambda j, b, t: (j, 0, 0))
    lane = pl.BlockSpec((1, STATE_BLOCK), lambda j, b, t: (0, j))
    lane8 = pl.BlockSpec((8, STATE_BLOCK), lambda j, b, t: (0, j))
    dsp = pl.BlockSpec((1, CHAN_BLOCK), lambda j, b, t: (0, j))

    def body(dyg_ref, y_ref, u_ref, xr_ref, xi_ref, xrb_ref, xib_ref, ar_ref, ai_ref, bbr_ref, bbi_ref, cbr_ref,
             cbi_ref, d_ref, du_ref, dcbr_ref, dcbi_ref, dbbr_ref, dbbi_ref, dd_ref, dar_ref, dai_ref,
             lam_r, lam_i, car_r, car_i):
        b, t = pl.program_id(1), pl.program_id(2)
        first = jnp.logical_and(b == 0, t == 0)

        @pl.when(t == 0)
        def _():
            car_r[...] = jnp.zeros_like(car_r)
            car_i[...] = jnp.zeros_like(car_i)

        @pl.when(first)
        def _():
            for r in (dcbr_ref, dcbi_ref, dbbr_ref, dbbi_ref, dd_ref, dar_ref, dai_ref):
                r[...] = jnp.zeros_like(r)

        steps, (pr, pi) = _power_tables(ar_ref[...], -ai_ref[...], reverse=True)
        uf = u_ref[...]
        _, gelu_vjp = jax.vjp(jax.nn.gelu, y_ref[...])
        dy = gelu_vjp(dyg_ref[...])[0]
        dyb = dy.astype(BF16)
        dd_ref[...] += _colsum(dy * uf)
        lam_r[...] = _dot(dyb, cbr_ref[...], "nn")
        lam_i[...] = -_dot(dyb, cbi_ref[...], "nn")
        dcbr_ref[...] += _dot(dyb, xr_ref[...].astype(BF16), "tn")
        dcbi_ref[...] -= _dot(dyb, xi_ref[...].astype(BF16), "tn")
        row0 = lax.broadcasted_iota(jnp.int32, (8, STATE_BLOCK), 0) == 0
        has_before = rev(t) > 0
        xrb = jnp.where(has_before, xrb_ref[...], 0.0)
        xib = jnp.where(has_before, xib_ref[...], 0.0)

        def tile(s, carry):
            cr, ci, acc_r, acc_i = carry
            i = ntile - 1 - s
            sl = pl.ds(pl.multiple_of(i * 8, 8), 8)
            gr, gi = lam_r[sl, :], lam_i[sl, :]
            for k, (sr, si) in zip((1, 2, 4), steps):
                tr, ti = _cmul(sr, si, pltpu.roll(gr, 8 - k, 0), pltpu.roll(gi, 8 - k, 0))
                gr, gi = gr + tr, gi + ti
            tr, ti = _cmul(pr, pi, cr, ci)
            gr, gi = gr + tr, gi + ti
            lam_r[sl, :] = gr
            lam_i[sl, :] = gi
            sp = pl.ds(pl.multiple_of(jnp.maximum(i - 1, 0) * 8, 8), 8)
            pvr = jnp.where(i > 0, xr_ref[sp, :], xrb)
            pvi = jnp.where(i > 0, xi_ref[sp, :], xib)
            xsr = jnp.where(row0, pltpu.roll(pvr, 1, 0), pltpu.roll(xr_ref[sl, :], 1, 0))
            xsi = jnp.where(row0, pltpu.roll(pvi, 1, 0), pltpu.roll(xi_ref[sl, :], 1, 0))
            acc_r = acc_r + xsr * gr + xsi * gi
            acc_i = acc_i + xsr * gi - xsi * gr
            return gr[0:1, :], gi[0:1, :], acc_r, acc_i

        zero = jnp.zeros((8, STATE_BLOCK), F32)
        cr, ci, acc_r, acc_i = lax.fori_loop(0, ntile, tile, (car_r[0:1, :], car_i[0:1, :], zero, zero), unroll=2)
        car_r[0:1, :] = cr
        car_i[0:1, :] = ci
        dar_ref[...] += acc_r
        dai_ref[...] += acc_i
        lrb, lib = lam_r[...].astype(BF16), lam_i[...].astype(BF16)
        du = _dot(lrb, bbr_ref[...], "nt") + _dot(lib, bbi_ref[...], "nt") + d_ref[...] * dy
        du_ref[...] = du.astype(BF16)
        ub = uf.astype(BF16)
        dbbr_ref[...] += _dot(ub, lrb, "tn")
        dbbi_ref[...] += _dot(ub, lib, "tn")

    mat_shape = jax.ShapeDtypeStruct((SSM_LANE_BLOCKS, CHAN_BLOCK, STATE_BLOCK), F32)
    return hosted_call(
        body, comm, "ssm_bwd", (SSM_LANE_BLOCKS, LOCAL_BATCH, nt),
        [chan, chan, chan, state, state, before, before, lane, lane, mat, mat, mat, mat, dsp],
        [chan, mat, mat, mat, mat, dsp, lane8, lane8],
        [jax.ShapeDtypeStruct((LOCAL_BATCH, SEQ, SSM_W), BF16), mat_shape, mat_shape, mat_shape, mat_shape,
         jax.ShapeDtypeStruct((1, SSM_W), F32), jax.ShapeDtypeStruct((8, SSM_STATE_W), F32),
         jax.ShapeDtypeStruct((8, SSM_STATE_W), F32)],
        [pltpu.VMEM((SCAN_CHUNK, STATE_BLOCK), F32), pltpu.VMEM((SCAN_CHUNK, STATE_BLOCK), F32),
         pltpu.VMEM((8, STATE_BLOCK), F32), pltpu.VMEM((8, STATE_BLOCK), F32)],
        (dyg, y, u, xr, xi, xr, xi, ab_re, ab_im, bb_re, bb_im, cb_re, cb_im, d_skip),
        ("parallel", "arbitrary", "arbitrary"))


def _merge_fn(g0, g1, attn_d, za, zb):
    return jax.nn.sigmoid(g0) * attn_d + jax.nn.sigmoid(g1) * (za * jax.nn.sigmoid(zb))


def _swiglu_fn(a, b):
    return jax.nn.silu(a) * b


def _own_slot(slots, shard):
    me = 2 * lax.axis_index("x") + lax.axis_index("y")
    mine = lax.broadcasted_iota(jnp.int32, (N_CHIPS, 1, 1), 0) == me
    return jnp.where(mine, shard[None], slots)


def _reduce_start(names, gw, shard_shapes):
    return swap_comm([_to_slots(n, gw[n], shard_shapes[n]) for n in names])


def _reduce_chip(names, swap, got, core):
    return exchange_comm([add_halves(n, g, r, core) for n, g, r in zip(names, swap.ins, got)])


def local_step(x, target, shards, small, core):
    g_mix, g_ffn, g_final = small["norm_mix_g"], small["norm_ffn_g"], small["norm_final_g"]
    tables = _rope_tables()
    seqs = lambda t: t.reshape(LOCAL_BATCH, SEQ, t.shape[-1])
    toks = lambda t: t.reshape(TOKENS, t.shape[-1])
    shard_shapes = {n: s.shape for n, s in shards.items()}
    w = {}

    def gather(names):
        return gather_comm([shards[n] for n in names])

    def arrived(names, slots, own=None):
        for n, s in zip(names, slots):
            w[n] = _from_slots(n, s if own is None else _own_slot(s, own))

    later = [n for n in BIG if n != "w_in"]
    sems, w_in_shard, land, token = split_start(shards["w_in"].astype(BF16), "w_in_gather_start")
    zero = token[0, 0]
    h, *rest = first_norm(x, g_mix + zero, [shards[n] for n in later])
    shards = dict(shards)
    shards.update(zip(later, rest))
    br_t = small["ssm_b_re"].transpose(2, 0, 1)
    bi_t = small["ssm_b_im"].transpose(2, 0, 1)
    log_dt = small["ssm_log_dt"].reshape(32, 1)
    ab_re, ab_im, bb_re_t, bb_im_t = ssm_prep(small["ssm_a_re"] + zero, small["ssm_a_im"], log_dt, br_t, bi_t)
    ab = [ab_re.reshape(1, SSM_STATE_W), ab_im.reshape(1, SSM_STATE_W)]
    bb = [_block_diag(bb_re_t).astype(BF16), _block_diag(bb_im_t).astype(BF16)]
    cb = [_block_diag((small["ssm_c_re"] + zero).transpose(1, 0, 2)).astype(BF16),
          _block_diag((small["ssm_c_im"] + zero).transpose(1, 0, 2)).astype(BF16)]
    d_skip = small["ssm_d"].reshape(1, SSM_W)
    w_in_shard, land = split_wait(sems, w_in_shard, land, [h] + bb + cb, "w_in_gather_wait")
    arrived(["w_in"], [handover(land, "w_in_handover")], own=w_in_shard)
    w_qkv, w_u, w_gate = _qkv_order(w["w_in"][:QKV_W]), w["w_in"][QKV_W:QKV_W + SSM_W], w["w_in"][QKV_W + SSM_W:]
    qkv, *slots = proj_qkv(h, w_qkv, tables, comm=gather(["w_attn_out", "w_glu"]))
    arrived(["w_attn_out", "w_glu"], slots)
    qkv = seqs(qkv)
    u = seqs(matmul(h, w_u, "nt", F32, "proj_u"))
    gl, *slots = matmul(h, w_gate, "nt", BF16, "proj_gate", comm=gather(["w_out"]))
    arrived(["w_out"], slots)
    attn_b, attn, lse, *slots = attn_fwd(qkv, comm=gather(["w_ffn_gate"]))
    arrived(["w_ffn_gate"], slots)
    attn_b = toks(attn_b)
    y, yg, xr, xi, *slots = ssm_fwd(u, *ab, *bb, *cb, d_skip, comm=gather(["w_ffn_up"]))
    arrived(["w_ffn_up"], slots)
    yg2 = toks(yg)
    merged, x1, h2 = merge_out_proj(gl, attn_b, w["w_attn_out"], yg2, w["w_glu"], w["w_out"], x, g_ffn)
    a, b, act, *slots = ffn_in(h2, w["w_ffn_gate"], w["w_ffn_up"], comm=gather(["w_ffn_down"]))
    arrived(["w_ffn_down"], slots)

    def final_fn(xv, g, tgt):
        yv, vjp = jax.vjp(_rms, xv, g)
        err = yv - tgt
        dx, dg = vjp(err * (1.0 / D_MODEL))
        loss = 0.5 * jnp.sum(jnp.mean(err * err, axis=-1, keepdims=True), axis=0, keepdims=True)
        return dx, dx, dg, jnp.broadcast_to(loss, (1, LANES))

    dx2, dx2_b, dg_final, loss = matmul_rows(act, w["w_ffn_down"], "ffn_down_loss", final_fn, [g_final, target],
                                             [(D_MODEL, F32), (D_MODEL, BF16)], accs=(D_MODEL, LANES), add=x1)
    gw, parts = {}, {}
    gw["w_ffn_down"] = matmul(act, dx2_b, "tn", F32, "d_ffn_down")
    da_b, db_b = ffn_in_bwd(dx2_b, w["w_ffn_down"], a, b)
    gw["w_ffn_gate"] = matmul(da_b, h2, "tn", F32, "d_ffn_gate")
    gw["w_ffn_up"] = matmul(db_b, h2, "tn", F32, "d_ffn_up")
    ffn = ["w_ffn_down", "w_ffn_gate", "w_ffn_up"]
    swap = _reduce_start(ffn[:2], gw, shard_shapes)
    dh2, *got = matmul(da_b, w["w_ffn_gate"], "nn", F32, "d_h2_gate", comm=swap)
    ffn_exchange = [_reduce_chip(ffn[:2], swap, got, core)]
    swap = _reduce_start(ffn[2:], gw, shard_shapes)

    def norm_bwd(dh, xv, g, skip):
        _, vjp = jax.vjp(_rms, xv, g)
        dx, dg = vjp(dh)
        dx = dx + skip
        return dx, dx, dg

    dx1, dx1_b, dg_ffn, *got = matmul_rows(db_b, w["w_ffn_up"], "d_h2_up_norm", norm_bwd, [x1, g_ffn, dx2],
                                           [(D_MODEL, F32), (D_MODEL, BF16)], accs=(D_MODEL,), add=dh2, comm=swap)
    ffn_up_exchange = _reduce_chip(ffn[2:], swap, got, core)
    gw["w_out"] = matmul(merged, dx1_b, "tn", F32, "d_out")
    dgl_b, dattn_d_b, dz_b, dattn, dyg, parts["w_ffn_up"] = merge_bwd(
        dx1_b, w["w_out"], gl, attn_b, w["w_attn_out"], yg2, w["w_glu"], comm=ffn_up_exchange)
    dattn, dyg = seqs(dattn), seqs(dyg)
    gw["w_attn_out"] = matmul(attn_b, dattn_d_b, "tn", F32, "d_attn_out")
    gw["w_glu"] = matmul(yg2, dz_b, "tn", F32, "d_glu")
    mixer = ["w_out", "w_attn_out", "w_glu"]
    swap = _reduce_start(mixer, gw, shard_shapes)
    du_b, dcb_re, dcb_im, dbb_re, dbb_im, dd, da_re8, da_im8, *rest = ssm_bwd(
        dyg, y, u, xr, xi, *ab, *bb, *cb, d_skip, comm=join_comms(ffn_exchange + [swap]))
    for n, p in zip(ffn[:2], rest[:2]):
        parts[n] = p
    mixer_exchange = _reduce_chip(mixer, swap, rest[2:], core)
    du_b = toks(du_b)
    g_ab_re = jnp.sum(da_re8, axis=0).reshape(32, 64)
    g_ab_im = jnp.sum(da_im8, axis=0).reshape(32, 64)
    d_lr, d_li, d_ldt, d_br_t, d_bi_t = ssm_prep_bwd(
        small["ssm_a_re"], small["ssm_a_im"], log_dt, br_t, bi_t,
        g_ab_re, g_ab_im, _block_diag_t(dbb_re), _block_diag_t(dbb_im))
    as_gcn = lambda t: t.transpose(1, 0, 2).reshape(SSM_W, 64)
    gs = {
        "ssm_a_re": d_lr, "ssm_a_im": d_li, "ssm_log_dt": d_ldt.reshape(1, 32),
        "ssm_b_re": as_gcn(d_br_t), "ssm_b_im": as_gcn(d_bi_t),
        "ssm_c_re": as_gcn(_block_diag_t(dcb_re)), "ssm_c_im": as_gcn(_block_diag_t(dcb_im)),
        "ssm_d": dd.reshape(32, 16).T,
    }
    ssm_gather = small_comm([gs[n] for n in SSM_SMALL])
    dqkv_b, *rest = attn_bwd(qkv, tables, dattn, attn, lse, comm=join_comms([mixer_exchange, ssm_gather]))
    for n, p in zip(mixer, rest):
        parts[n] = p
    ssm_shares = rest[len(mixer):]
    dqkv_b = toks(dqkv_b)
    d_qkv = matmul(dqkv_b, h, "tn", F32, "d_w_qkv")
    d_u = matmul(du_b, h, "tn", F32, "d_w_u")
    d_gate = matmul(dgl_b, h, "tn", F32, "d_w_gate")
    gw["w_in"] = jnp.concatenate([_qkv_order(d_qkv, back=True), d_u, d_gate], axis=0)
    swap = _reduce_start(["w_in"], gw, shard_shapes)
    dh, *got = matmul(dqkv_b, w_qkv, "nn", F32, "d_h_qkv", comm=swap)
    chip_sum = add_halves("w_in", swap.ins[0], got[0], core)
    sems, chip_sum, land, token = split_start(chip_sum, "w_in_reduce_start", per_chip=True)
    grad_x, dg_mix = mix_in_bwd([du_b, dgl_b], [w_u, w_gate], dh, x, g_mix + token[0, 0], dx1)
    gs_norm = {"norm_mix_g": dg_mix, "norm_ffn_g": dg_ffn, "norm_final_g": dg_final}
    return loss, grad_x, parts, ssm_shares, gs_norm, (sems, chip_sum, land)


ANY = pl.BlockSpec(memory_space=pl.ANY)
BIG = ("w_in", "w_glu", "w_attn_out", "w_out", "w_ffn_gate", "w_ffn_up", "w_ffn_down")
TRANSPOSED = ("w_in", "w_ffn_gate", "w_ffn_up")
ROW_SHARDED = TRANSPOSED + ("w_out", "w_ffn_down")
SMALL = ("norm_mix_g", "ssm_a_re", "ssm_a_im", "ssm_log_dt", "ssm_b_re", "ssm_b_im", "ssm_c_re", "ssm_c_im",
         "ssm_d", "norm_ffn_g", "norm_final_g")
WEIGHTS = ("norm_mix_g", "w_in", "ssm_a_re", "ssm_a_im", "ssm_log_dt", "ssm_b_re", "ssm_b_im", "ssm_c_re",
           "ssm_c_im", "ssm_d", "w_glu", "w_attn_out", "w_out", "norm_ffn_g", "w_ffn_gate", "w_ffn_up",
           "w_ffn_down", "norm_final_g")
SSM_SMALL = SMALL[1:9]
NORM_SMALL = (SMALL[0],) + SMALL[9:]
NORM_ROWS = 32
N_BIG = len(BIG)


def _position():
    return lax.axis_index("x"), lax.axis_index("y"), lax.axis_index("c")


def _other_chips(x, y):
    return [(1 - x, y), (x, 1 - y), (1 - x, 1 - y)]


def _remote(src, dst, send_sem, recv_sem, device):
    return pltpu.make_async_remote_copy(src_ref=src, dst_ref=dst, send_sem=send_sem, recv_sem=recv_sem,
                                        device_id=device, device_id_type=MESH)


_later = functools.partial


def _two_level_phases(copies):
    def first(*refs):
        locals_, sends, _, _, _ = copies(*refs)
        for cp in locals_ + sends:
            cp().start()

    def mid(*refs):
        _, _, arrived, passed, _ = copies(*refs)
        for got, cp in zip(arrived, passed):
            got().wait_recv()
            cp().start()

    def last(*refs):
        locals_, sends, _, passed, from_sibling = copies(*refs)
        for cp in from_sibling:
            cp().wait_recv()
        for cp in sends + passed:
            cp().wait_send()
        for cp in locals_:
            cp().wait()

    return first, mid, last


def _half(ref, chip, which):
    rows = ref.shape[1] // 2
    return ref.at[chip, pl.ds(which * rows, rows), :]


class Comm:
    def __init__(self, ins, out_shapes, sems, first, mid, last):
        self.ins, self.out_shapes, self.sems = list(ins), list(out_shapes), list(sems)
        self.first, self.mid, self.last = first, mid, last


def join_comms(comms):
    def cut(refs_by_kind):
        offs, parts = [0, 0, 0], []
        for cm in comms:
            sizes = (len(cm.ins), len(cm.out_shapes), len(cm.sems))
            parts.append(tuple(refs_by_kind[k][offs[k]:offs[k] + sizes[k]] for k in range(3)))
            offs = [o + s for o, s in zip(offs, sizes)]
        return parts

    def phase(which):
        def run(ins, outs, sems):
            for cm, part in zip(comms, cut((ins, outs, sems))):
                fn = getattr(cm, which)
                if fn is not None:
                    fn(*part)
        return run

    return Comm(sum((cm.ins for cm in comms), []), sum((cm.out_shapes for cm in comms), []),
                sum((cm.sems for cm in comms), []), phase("first"), phase("mid"), phase("last"))


def _comm_operands(comm):
    if comm is None:
        return [], [], []
    return comm.ins, comm.out_shapes, comm.sems


def _comm_begin(comm, refs, step, n_steps):
    if comm is None:
        return
    pl.when(step == 0)(lambda: comm.first(*refs))
    if comm.mid is not None:
        pl.when(step == (n_steps * 3) // 4)(lambda: comm.mid(*refs))


def _comm_end(comm, refs, step, n_steps):
    if comm is not None:
        pl.when(step == n_steps - 1)(lambda: comm.last(*refs))


def _comm_refs(comm, refs, n_in, n_out):
    if comm is None:
        return list(refs), None
    ci, co, cs = len(comm.ins), len(comm.out_shapes), len(comm.sems)
    o0 = n_in + ci
    s0 = o0 + n_out + co
    host = list(refs[:n_in]) + list(refs[o0:o0 + n_out]) + list(refs[s0:len(refs) - cs])
    return host, (list(refs[n_in:o0]), list(refs[o0 + n_out:s0]), list(refs[len(refs) - cs:]))


def run_comm(comm, name):
    n_in, n_out = len(comm.ins), len(comm.out_shapes)

    def body(*refs):
        parts = (list(refs[:n_in]), list(refs[n_in:n_in + n_out]), list(refs[n_in + n_out:]))
        comm.first(*parts)
        if comm.mid is not None:
            comm.mid(*parts)
        comm.last(*parts)

    return pl.pallas_call(body, name=name, in_specs=[ANY] * n_in, out_specs=[ANY] * n_out,
                          out_shape=comm.out_shapes, scratch_shapes=comm.sems)(*comm.ins)


def hosted_call(work, comm, name, grid, in_specs, out_specs, out_shape, scratch_shapes, args, semantics):
    c_ins, c_outs, c_sems = _comm_operands(comm)
    n_steps = math.prod(grid)

    def body(*refs):
        host, c_refs = _comm_refs(comm, refs, len(in_specs), len(out_specs))
        step = 0
        for axis, size in enumerate(grid):
            step = step * size + pl.program_id(axis)
        _comm_begin(comm, c_refs, step, n_steps)
        work(*host)
        _comm_end(comm, c_refs, step, n_steps)

    return pl.pallas_call(
        body, name=name, grid=grid, in_specs=list(in_specs) + [ANY] * len(c_ins),
        out_specs=list(out_specs) + [ANY] * len(c_outs), out_shape=list(out_shape) + c_outs,
        scratch_shapes=list(scratch_shapes) + c_sems,
        compiler_params=_params(semantics if comm is None else ("arbitrary",) * len(grid)),
    )(*args, *c_ins)


def gather_comm(shards):
    n = len(shards)

    def copies(srcs, outs, sems):
        send_sems, recv_sems, local_sems = sems
        x, y, c = _position()
        me = 2 * x + y
        sibling = (x, y, 1 - c)
        chips = _other_chips(x, y)
        locals_ = [_later(pltpu.make_async_copy, s, o.at[me], local_sems.at[i])
                   for i, (s, o) in enumerate(zip(srcs, outs))]
        sends, arrived, passed, from_sibling = [], [], [], []
        for j, (px, py) in enumerate(chips):
            for i, (s, o) in enumerate(zip(srcs, outs)):
                rows = s.shape[0] // 2
                sends.append(_later(_remote, s.at[pl.ds(c * rows, rows), :], _half(o, me, c), send_sems.at[i, j],
                                    recv_sems.at[i, j], (px, py, c)))
                got = _half(o, 2 * px + py, c)
                arrived.append(_later(_remote, got, got, send_sems.at[i, j], recv_sems.at[i, j], (px, py, c)))
                passed.append(_later(_remote, got, got, send_sems.at[i, 3 + j], recv_sems.at[i, 3 + j], sibling))
                other = _half(o, 2 * px + py, 1 - c)
                from_sibling.append(_later(_remote, other, other, send_sems.at[i, 3 + j], recv_sems.at[i, 3 + j],
                                           sibling))
        return locals_, sends, arrived, passed, from_sibling

    return Comm(shards, [jax.ShapeDtypeStruct((N_CHIPS,) + s.shape, s.dtype) for s in shards],
                [pltpu.SemaphoreType.DMA((n, 6)), pltpu.SemaphoreType.DMA((n, 6)), pltpu.SemaphoreType.DMA((n,))],
                *_two_level_phases(copies))


HBM = pl.BlockSpec(memory_space=pltpu.HBM)
SEM = pl.BlockSpec(memory_space=pltpu.SEMAPHORE)
N_OTHER = N_CHIPS - 1


def _ici_halves(src_ref, land_ref, sems, per_chip):
    x, y, c = _position()
    me = 2 * x + y
    rows = land_ref.shape[1] // 2
    sends, arrivals = [], []
    for j, (px, py) in enumerate(_other_chips(x, y)):
        piece = src_ref.at[2 * px + py] if per_chip else src_ref.at[pl.ds(c * rows, rows), :]
        sends.append(_later(_remote, piece, _half(land_ref, me, c), sems[j], sems[N_OTHER + j], (px, py, c)))
        got = _half(land_ref, 2 * px + py, c)
        arrivals.append(_later(_remote, got, got, sems[j], sems[N_OTHER + j], (px, py, c)))
    return sends, arrivals


def split_start(src, name, per_chip=False):
    def body(src_ref, land_ref, *rest):
        sems, token = rest[:2 * N_OTHER], rest[-1]
        for cp in _ici_halves(src_ref, land_ref, sems, per_chip)[0]:
            cp().start()
        token[...] = jnp.zeros_like(token)

    rows, cols = (2 * src.shape[1], src.shape[2]) if per_chip else src.shape
    sem = pltpu.SemaphoreType.DMA(())
    land = (N_CHIPS, rows, cols)
    res = pl.pallas_call(
        body, name=name, in_specs=(HBM, HBM),
        out_specs=(SEM,) * (2 * N_OTHER) + (HBM, HBM, pl.BlockSpec(memory_space=pltpu.VMEM)),
        out_shape=(sem,) * (2 * N_OTHER) + (pltpu.HBM(src.shape, src.dtype), pltpu.HBM(land, src.dtype),
                                           jax.ShapeDtypeStruct((8, LANES), F32)),
        input_output_aliases={0: 2 * N_OTHER, 1: 2 * N_OTHER + 1},
        compiler_params=pltpu.CompilerParams(has_side_effects=pltpu.SideEffectType.DATAFLOW_SIDE_EFFECTING),
    )(pltpu.with_memory_space_constraint(src, pltpu.HBM),
      pltpu.with_memory_space_constraint(lax.empty(land, src.dtype), pltpu.HBM))
    return res[:2 * N_OTHER], res[2 * N_OTHER], res[2 * N_OTHER + 1], res[-1]


def split_wait(sems, src, land, after, name, per_chip=False):
    def body(src_ref, land_ref, *rest):
        sends, arrivals = _ici_halves(src_ref, land_ref, rest[:2 * N_OTHER], per_chip)
        for cp in sends:
            cp().wait_send()
        for cp in arrivals:
            cp().wait_recv()

    return pl.pallas_call(
        body, name=name, in_specs=(HBM, HBM) + (SEM,) * (2 * N_OTHER) + (ANY,) * len(after),
        out_specs=(HBM, HBM), out_shape=(pltpu.HBM(src.shape, src.dtype), pltpu.HBM(land.shape, land.dtype)),
        input_output_aliases={0: 0, 1: 1},
        compiler_params=pltpu.CompilerParams(has_side_effects=pltpu.SideEffectType.DATAFLOW_SIDE_EFFECTING),
    )(src, land, *sems, *after)


def handover(land, name, sums=None):
    n = N_OTHER + (sums is not None)

    def body(*refs):
        land_ref, send_sems, recv_sems = refs[0], refs[-2], refs[-1]
        x, y, c = _position()
        me = 2 * x + y
        sibling = (x, y, 1 - c)
        pieces = [(_half(land_ref, 2 * px + py, c), 2 * px + py) for px, py in _other_chips(x, y)]
        if sums is not None:
            pieces.append((refs[1].at[me], me))
        sends = [_remote(piece, _half(land_ref, chip, c), send_sems.at[j], recv_sems.at[j], sibling)
                 for j, (piece, chip) in enumerate(pieces)]
        for cp in sends:
            cp.start()
        for j, (_, chip) in enumerate(pieces):
            other = _half(land_ref, chip, 1 - c)
            _remote(other, other, send_sems.at[j], recv_sems.at[j], sibling).wait_recv()
        for cp in sends:
            cp.wait_send()

    args = (land,) + ((sums,) if sums is not None else ())
    return pl.pallas_call(
        body, name=name, in_specs=[ANY] * len(args), out_specs=ANY,
        out_shape=jax.ShapeDtypeStruct(land.shape, land.dtype), input_output_aliases={0: 0},
        scratch_shapes=[pltpu.SemaphoreType.DMA((n,)), pltpu.SemaphoreType.DMA((n,))],
    )(*args)


def swap_comm(grads):
    n = len(grads)

    def copies(srcs, gots, sems):
        send_sems, recv_sems = sems
        x, y, c = _position()
        out = []
        for i, (s, o) in enumerate(zip(srcs, gots)):
            rows = s.shape[1] // 2
            out.append(_remote(s.at[:, pl.ds((1 - c) * rows, rows), :], o, send_sems.at[i], recv_sems.at[i],
                               (x, y, 1 - c)))
        return out

    def first(srcs, gots, sems):
        for cp in copies(srcs, gots, sems):
            cp.start()

    def last(srcs, gots, sems):
        for cp in copies(srcs, gots, sems):
            cp.wait()

    return Comm(grads, [jax.ShapeDtypeStruct((N_CHIPS, g.shape[1] // 2, g.shape[2]), g.dtype) for g in grads],
                [pltpu.SemaphoreType.DMA((n,)), pltpu.SemaphoreType.DMA((n,))], first, None, last)


def add_halves(name, g, got, core):
    _, half, cols = got.shape
    mine = pl.BlockSpec((None, half, cols), lambda k, c_ref: (k, c_ref[0], 0))
    other = pl.BlockSpec((None, half, cols), lambda k, c_ref: (k, 0, 0))

    def body(c_ref, g_ref, got_ref, o_ref):
        o_ref[...] = (g_ref[...] + got_ref[...]).astype(BF16)

    return pl.pallas_call(
        body, name="add_halves_" + name,
        grid_spec=pltpu.PrefetchScalarGridSpec(num_scalar_prefetch=1, grid=(N_CHIPS,), in_specs=[mine, other],
                                               out_specs=other),
        out_shape=jax.ShapeDtypeStruct(got.shape, BF16),
        compiler_params=_params(("parallel",)),
    )(core, g, got)


def exchange_comm(parts):
    n = len(parts)

    def copies(srcs, outs, sems):
        send_sems, recv_sems, local_sems = sems
        x, y, c = _position()
        me = 2 * x + y
        sibling = (x, y, 1 - c)
        chips = _other_chips(x, y)
        locals_, sends, arrived, passed, from_sibling = [], [], [], [], []
        for i, (s, o) in enumerate(zip(srcs, outs)):
            locals_.append(_later(pltpu.make_async_copy, s.at[me], _half(o, me, c), local_sems.at[i]))
            sends.append(_later(_remote, s.at[me], _half(o, me, c), send_sems.at[i, 3], recv_sems.at[i, 3], sibling))
            other = _half(o, me, 1 - c)
            from_sibling.append(_later(_remote, other, other, send_sems.at[i, 3], recv_sems.at[i, 3], sibling))
        for j, (px, py) in enumerate(chips):
            for i, (s, o) in enumerate(zip(srcs, outs)):
                sends.append(_later(_remote, s.at[2 * px + py], _half(o, me, c), send_sems.at[i, j],
                                    recv_sems.at[i, j], (px, py, c)))
                got = _half(o, 2 * px + py, c)
                arrived.append(_later(_remote, got, got, send_sems.at[i, j], recv_sems.at[i, j], (px, py, c)))
                passed.append(_later(_remote, got, got, send_sems.at[i, 4 + j], recv_sems.at[i, 4 + j], sibling))
                other = _half(o, 2 * px + py, 1 - c)
                from_sibling.append(_later(_remote, other, other, send_sems.at[i, 4 + j], recv_sems.at[i, 4 + j],
                                           sibling))
        return locals_, sends, arrived, passed, from_sibling

    return Comm(parts, [jax.ShapeDtypeStruct((N_CHIPS, 2 * p.shape[1], p.shape[2]), p.dtype) for p in parts],
                [pltpu.SemaphoreType.DMA((n, 7)), pltpu.SemaphoreType.DMA((n, 7)), pltpu.SemaphoreType.DMA((n,))],
                *_two_level_phases(copies))


def small_comm(shares, after=()):
    n = len(shares)

    def copies(srcs, outs, sems):
        send_sems, recv_sems, local_sems = sems
        x, y, c = _position()
        me = 4 * x + 2 * y + c
        flips = [(fx, fy, fc) for fx in (0, 1) for fy in (0, 1) for fc in (0, 1)][1:]
        peers = [(1 - x if fx else x, 1 - y if fy else y, 1 - c if fc else c) for fx, fy, fc in flips]
        locals_, sends, arrived = [], [], []
        for i, (src_ref, out_ref) in enumerate(zip(srcs, outs)):
            locals_.append(_later(pltpu.make_async_copy, src_ref, out_ref.at[me], local_sems.at[i]))
            for j, (px, py, pc) in enumerate(peers):
                sends.append(_later(_remote, src_ref, out_ref.at[me], send_sems.at[i, j], recv_sems.at[i, j],
                                    (px, py, pc)))
                got = out_ref.at[4 * px + 2 * py + pc]
                arrived.append(_later(_remote, got, got, send_sems.at[i, j], recv_sems.at[i, j], (px, py, pc)))
        return locals_, sends, arrived

    def first(*refs):
        locals_, sends, _ = copies(*refs)
        for cp in locals_ + sends:
            cp().start()

    def last(*refs):
        locals_, sends, arrived = copies(*refs)
        for cp in arrived:
            cp().wait_recv()
        for cp in sends:
            cp().wait_send()
        for cp in locals_:
            cp().wait()

    return Comm(list(shares) + list(after), [jax.ShapeDtypeStruct((N_DEV,) + s.shape, s.dtype) for s in shares],
                [pltpu.SemaphoreType.DMA((n, 7)), pltpu.SemaphoreType.DMA((n, 7)), pltpu.SemaphoreType.DMA((n,))],
                first, None, last)


def _adam_fn(w, g, m, v):
    m = ADAM_B1 * m + (1.0 - ADAM_B1) * g
    v = ADAM_B2 * v + (1.0 - ADAM_B2) * jnp.square(g)
    m_hat = m / (1.0 - ADAM_B1 ** ADAM_STEP)
    v_hat = v / (1.0 - ADAM_B2 ** ADAM_STEP)
    return -ADAM_LR * (m_hat / (jnp.sqrt(v_hat) + ADAM_EPS) + ADAM_WD * w), m, v


def adam_big(name, parts, w, m, v):
    rows, cols = w.shape
    tm = _pick(rows, 384, 16)

    def fn(p0, p1, p2, p3, wv, mv, vv):
        g = ((p0.astype(F32) + p1.astype(F32)) + p2.astype(F32)) + p3.astype(F32)
        return (g,) + _adam_fn(wv, g, mv, vv)

    return rowwise(fn, [parts, w, m, v], [(cols, F32)] * 4, "adam_" + name, tm=tm, rows=rows)


def adam_small(name, gathered, w, m, v):
    def body(g_ref, w_ref, m_ref, v_ref, go_ref, d_ref, mo_ref, vo_ref):
        g = g_ref[0]
        for k in range(1, N_DEV):
            g = g + g_ref[k]
        go_ref[...] = g
        d_ref[...], mo_ref[...], vo_ref[...] = _adam_fn(w_ref[...], g, m_ref[...], v_ref[...])

    return pl.pallas_call(body, name=name, out_shape=[jax.ShapeDtypeStruct(w.shape, F32)] * 4,
                          compiler_params=_params())(gathered, w, m, v)


def _ssm_2d(name, t):
    t = t[0] if t.ndim > 2 else t
    if name in ("ssm_b_re", "ssm_b_im"):
        return t.transpose(0, 2, 1).reshape(SSM_W, 64)
    if name in ("ssm_c_re", "ssm_c_im"):
        return t.reshape(SSM_W, 64)
    return t.T if name == "ssm_d" else t


def _ssm_back(name, t):
    if name in ("ssm_b_re", "ssm_b_im"):
        return t.reshape(32, 16, 64).transpose(0, 2, 1)[None]
    if name in ("ssm_c_re", "ssm_c_im"):
        return t.reshape(1, 32, 16, 64)
    if name == "ssm_d":
        return t.T[None]
    return t if name == "ssm_log_dt" else t[None]


def adam_ssm(shares, w, m, v):
    n = len(w)

    def body(*refs):
        ins, outs = refs[:4 * n], refs[4 * n:]
        for i in range(n):
            g_ref, w_ref, m_ref, v_ref = (ins[k * n + i] for k in range(4))
            g = g_ref[0]
            for k in range(1, N_DEV):
                g = g + g_ref[k]
            outs[4 * i][...] = g
            outs[4 * i + 1][...], outs[4 * i + 2][...], outs[4 * i + 3][...] = _adam_fn(w_ref[...], g, m_ref[...],
                                                                                      v_ref[...])

    out_shape = [jax.ShapeDtypeStruct(t.shape, F32) for t in w for _ in range(4)]
    res = pl.pallas_call(body, name="adam_ssm", out_shape=out_shape, compiler_params=_params())(*shares, *w, *m, *v)
    return [res[4 * i:4 * i + 4] for i in range(n)]


def _pack_small(names, vals, rows, last=None):
    flat = [vals[n].reshape(-1) for n in names]
    if last is not None:
        flat.append(last.reshape(-1))
    flat = jnp.concatenate(flat)
    return jnp.pad(flat, (0, rows * LANES - flat.shape[0])).reshape(rows, LANES)


def _unpack_small(names, pack, shapes):
    flat, out, off = pack.reshape(-1), {}, 0
    for n in names:
        size = math.prod(shapes[n])
        out[n] = flat[off:off + size].reshape(shapes[n])
        off += size
    return out, flat[off]


def _to_slots(name, g, shard_shape):
    rows, cols = shard_shape
    if name in ROW_SHARDED:
        return g.reshape(N_CHIPS, rows, cols)
    return g.reshape(rows, N_CHIPS, cols).transpose(1, 0, 2)


def _from_slots(name, s):
    _, rows, cols = s.shape
    if name in ROW_SHARDED:
        return s.reshape(N_CHIPS * rows, cols)
    return s.transpose(1, 0, 2).reshape(rows, N_CHIPS * cols)


def kernel(x, norm_mix_g, w_in, ssm_a_re, ssm_a_im, ssm_log_dt, ssm_b_re, ssm_b_im, ssm_c_re, ssm_c_im, ssm_d, w_glu, w_attn_out, w_out, norm_ffn_g, w_ffn_gate, w_ffn_up, w_ffn_down, norm_final_g, loss_target, m_norm_mix_g, m_w_in, m_ssm_a_re, m_ssm_a_im, m_ssm_log_dt, m_ssm_b_re, m_ssm_b_im, m_ssm_c_re, m_ssm_c_im, m_ssm_d, m_w_glu, m_w_attn_out, m_w_out, m_norm_ffn_g, m_w_ffn_gate, m_w_ffn_up, m_w_ffn_down, m_norm_final_g, v_norm_mix_g, v_w_in, v_ssm_a_re, v_ssm_a_im, v_ssm_log_dt, v_ssm_b_re, v_ssm_b_im, v_ssm_c_re, v_ssm_c_im, v_ssm_d, v_w_glu, v_w_attn_out, v_w_out, v_norm_ffn_g, v_w_ffn_gate, v_w_ffn_up, v_w_ffn_down, v_norm_final_g):
    given = dict(locals())
    def local(name, prefix=""):
        t = given[prefix + name][0]
        return t.T if name in TRANSPOSED else t

    shard = {n: local(n) for n in BIG}
    shapes = {n: given[n].shape for n in WEIGHTS}

    small = {n: given[n] for n in SMALL}
    small_2d = dict(small)
    for n in ("ssm_a_re", "ssm_a_im", "ssm_b_re", "ssm_b_im", "ssm_c_re", "ssm_c_im", "ssm_d"):
        small_2d[n] = small[n][0]
    small_2d["norm_final_g"] = norm_final_g.reshape(1, D_MODEL)

    core = lax.axis_index("c").astype(jnp.int32).reshape(1)
    loss, grad_x, parts, ssm_shares, gs_norm, w_in_reduce = local_step(
        x.reshape(TOKENS, D_MODEL), loss_target.reshape(TOKENS, D_MODEL),
        {n: shard[n] for n in BIG}, small_2d, core)

    small_out = [{} for _ in range(4)]
    ssm_in = [[_ssm_2d(n, given[p + n]) for n in SSM_SMALL] for p in ("", "m_", "v_")]
    for n, res in zip(SSM_SMALL, adam_ssm(ssm_shares, *ssm_in)):
        for kind, t in enumerate(res):
            small_out[kind][n] = _ssm_back(n, t)

    big_out, updated = {}, {}
    for n in BIG[1:] + BIG[:1]:
        if n == "w_in":
            sems, chip_sum, land = w_in_reduce
            behind = [updated[k][1] for k in BIG[1:]]
            chip_sum, land = split_wait(sems, chip_sum, land, behind, "w_in_reduce_wait", per_chip=True)
            land = handover(land, "w_in_reduce_handover", sums=chip_sum)
            me = 2 * lax.axis_index("x") + lax.axis_index("y")
            parts[n] = lax.dynamic_update_slice(land, lax.dynamic_slice_in_dim(chip_sum, me, 1, 0),
                                                (me, lax.axis_index("c") * chip_sum.shape[1], 0))
        updated[n] = adam_big(n, parts[n], shard[n], local(n, "m_"), local(n, "v_"))
        big_out[n] = [(t.T if n in TRANSPOSED else t)[None] for t in updated[n]]

    (norm_shares,) = run_comm(small_comm([_pack_small(NORM_SMALL, gs_norm, NORM_ROWS, last=loss)], after=[land]),
                              "gather_norm_grads")
    packs = [_pack_small(NORM_SMALL, {n: given[p + n] for n in NORM_SMALL}, NORM_ROWS) for p in ("", "m_", "v_")]
    for kind, t in enumerate(adam_small("adam_norm_gains", norm_shares, *packs)):
        vals, after = _unpack_small(NORM_SMALL, t, shapes)
        small_out[kind].update(vals)
        if kind == 0:
            total_loss = after

    outs = [total_loss, grad_x.reshape(LOCAL_BATCH, SEQ, D_MODEL)]
    for kind in range(4):
        for n in WEIGHTS:
            outs.append(big_out[n][kind] if n in BIG else small_out[kind][n])
    return tuple(outs)
```

```python
import functools
import math

import jax
import jax.numpy as jnp
import numpy as np
from jax import lax
from jax.experimental import pallas as pl
from jax.experimental.pallas import tpu as pltpu

F32 = jnp.float32
BF16 = jnp.bfloat16
MESH = pl.DeviceIdType.MESH

D_MODEL = 1024
SEQ = 2048
LOCAL_BATCH = 2
TOKENS = LOCAL_BATCH * SEQ
HEAD_DIM = 64
HEADS_PER_GROUP = 4
GROUP_W = HEADS_PER_GROUP * HEAD_DIM
N_GROUPS = 3
DILATIONS = (1, 4, 16)
ATTN_BLOCK = 128
ROPE_DIM = 16
ROPE_THETA = 500000.0
QKV_W = 3 * N_GROUPS * GROUP_W
SSM_W = 512
SSM_STATE_W = 2048
SSM_LANE_BLOCKS = 4
GATE_W = 2 * D_MODEL
D_FF = 2816
RMS_EPS = 1e-6
NEG_INF = -1e30
ADAM_LR, ADAM_B1, ADAM_B2, ADAM_EPS, ADAM_WD, ADAM_STEP = 0.001, 0.9, 0.999, 1e-08, 0.01, 10
N_CHIPS = 4
N_DEV = 8

VMEM_LIMIT = 56 * 1024 * 1024
LANES = 128


def _params(sem=None):
    return pltpu.CompilerParams(dimension_semantics=sem, vmem_limit_bytes=VMEM_LIMIT)


def _pick(n, cap, align=LANES):
    best = None
    for d in range(align, min(n, cap) + 1, align):
        if n % d == 0:
            best = d
    return n if best is None or n <= cap else best


_DIMS = {"nn": (((1,), (0,)), ((), ())), "nt": (((1,), (1,)), ((), ())), "tn": (((0,), (0,)), ((), ()))}


def _dot(a, b, mode):
    return lax.dot_general(a, b, _DIMS[mode], preferred_element_type=F32)


def matmul(a, b, mode, out_dtype, name, add=None, comm=None):
    if mode == "nn":
        (m, k), n = a.shape, b.shape[1]
    elif mode == "nt":
        (m, k), n = a.shape, b.shape[0]
    else:
        (k, m), n = a.shape, b.shape[1]
    tn = _pick(n, 1408 if mode != "tn" else 512)
    tk = _pick(k, 2816) if mode != "tn" else k
    tm = _pick(m, 1408)
    out_bytes = jnp.dtype(out_dtype).itemsize

    def need(tm_):
        return 2 * 2 * (tm_ * tk + tk * tn) + tm_ * tn * (4 + 2 * out_bytes + (8 if add is not None else 0))

    while need(tm) > 40 * 1024 * 1024 and tm % 256 == 0:
        tm //= 2
    nk = k // tk
    a_spec = {"nn": pl.BlockSpec((tm, tk), lambda i, j, kk: (i, kk)),
              "nt": pl.BlockSpec((tm, tk), lambda i, j, kk: (i, kk)),
              "tn": pl.BlockSpec((tk, tm), lambda i, j, kk: (kk, i))}[mode]
    b_spec = {"nn": pl.BlockSpec((tk, tn), lambda i, j, kk: (kk, j)),
              "nt": pl.BlockSpec((tn, tk), lambda i, j, kk: (j, kk)),
              "tn": pl.BlockSpec((tk, tn), lambda i, j, kk: (kk, j))}[mode]
    o_spec = pl.BlockSpec((tm, tn), lambda i, j, kk: (i, j))

    def body(a_ref, b_ref, *rest):
        if add is not None:
            add_ref, o_ref, acc_ref = rest
        else:
            o_ref, acc_ref = rest
        part = _dot(a_ref[...], b_ref[...], mode)
        if nk == 1:
            res = part if add is None else part + add_ref[...]
            o_ref[...] = res.astype(out_dtype)
            return
        kk = pl.program_id(2)

        @pl.when(kk == 0)
        def _():
            acc_ref[...] = part

        @pl.when(kk > 0)
        def _():
            acc_ref[...] += part

        @pl.when(kk == nk - 1)
        def _():
            res = acc_ref[...] if add is None else acc_ref[...] + add_ref[...]
            o_ref[...] = res.astype(out_dtype)

    in_specs = [a_spec, b_spec] + ([o_spec] if add is not None else [])
    args = (a, b) + ((add,) if add is not None else ())
    res = hosted_call(
        body, comm, name, (m // tm, n // tn, nk), in_specs, [o_spec], [jax.ShapeDtypeStruct((m, n), out_dtype)],
        [pltpu.VMEM((tm, tn) if nk > 1 else (8, LANES), F32)], args, ("parallel", "parallel", "arbitrary"))
    return res[0] if comm is None else res


def matmul_rows(a, b, name, fn, extra, outs, accs=(), add=None, comm=None, tm=512):
    (m, k), n = a.shape, b.shape[1]
    n_fixed = 2 + (add is not None)
    row_spec = lambda cols: pl.BlockSpec((tm, cols), lambda i: (i, 0))
    in_specs = [row_spec(k), pl.BlockSpec((k, n), lambda i: (0, 0))] + ([row_spec(n)] if add is not None else [])
    in_specs += [pl.BlockSpec(e.shape, lambda i: (0, 0)) if e.shape[0] == 1 else row_spec(e.shape[1]) for e in extra]
    out_specs = [row_spec(c) for c, _ in outs] + [pl.BlockSpec((1, c), lambda i: (0, 0)) for c in accs]
    out_shape = [jax.ShapeDtypeStruct((m, c), dt) for c, dt in outs] + [jax.ShapeDtypeStruct((1, c), F32) for c in accs]

    def body(*refs):
        rows = _dot(refs[0][...], refs[1][...], "nn")
        if add is not None:
            rows = rows + refs[2][...]
        n_in = n_fixed + len(extra)
        res = fn(rows, *[r[...] for r in refs[n_fixed:n_in]])
        for r, v in zip(refs[n_in:n_in + len(outs)], res[:len(outs)]):
            r[...] = v.astype(r.dtype)
        first = pl.program_id(0) == 0
        for r, v in zip(refs[n_in + len(outs):], res[len(outs):]):
            @pl.when(first)
            def _(r=r, v=v):
                r[...] = v

            @pl.when(jnp.logical_not(first))
            def _(r=r, v=v):
                r[...] += v

    args = (a, b) + ((add,) if add is not None else ()) + tuple(extra)
    return hosted_call(body, comm, name, (m // tm,), in_specs, out_specs, out_shape, [], args, ("arbitrary",))


def _merge_specs(tm):
    half = lambda blk: pl.BlockSpec((tm, D_MODEL), functools.partial(lambda i, blk_: (i, blk_), blk_=blk))
    return [half(0), half(1), pl.BlockSpec((tm, GROUP_W), lambda i: (i, 0)),
            pl.BlockSpec((GROUP_W, D_MODEL), lambda i: (0, 0)), pl.BlockSpec((tm, SSM_W), lambda i: (i, 0)),
            pl.BlockSpec((SSM_W, GATE_W), lambda i: (0, 0))]


def _merge_operands(g0, g1, at, wa, yg, wg):
    z = _dot(yg[...], wg[...], "nn")
    return (g0[...].astype(F32), g1[...].astype(F32), _dot(at[...], wa[...], "nn"), z[:, :D_MODEL], z[:, D_MODEL:])


def merge_out_proj(gl, attn_b, w_attn_out, yg, w_glu, w_out, x, g_ffn):
    tm = 512

    def body(g0, g1, at, wa, yg_ref, wg, w_ref, x_ref, g_ref, m_ref, x1_ref, h2_ref):
        merged = _merge_fn(*_merge_operands(g0, g1, at, wa, yg_ref, wg)).astype(BF16)
        m_ref[...] = merged
        x1 = _dot(merged, w_ref[...], "nn") + x_ref[...]
        x1_ref[...] = x1
        h2_ref[...] = _rms(x1, g_ref[...]).astype(BF16)

    rows = pl.BlockSpec((tm, D_MODEL), lambda i: (i, 0))
    whole = pl.BlockSpec((D_MODEL, D_MODEL), lambda i: (0, 0))
    gain = pl.BlockSpec((1, D_MODEL), lambda i: (0, 0))
    tok = lambda dt: jax.ShapeDtypeStruct((TOKENS, D_MODEL), dt)
    return pl.pallas_call(
        body, name="merge_out_proj", grid=(TOKENS // tm,), in_specs=_merge_specs(tm) + [whole, rows, gain],
        out_specs=[rows] * 3, out_shape=[tok(BF16), tok(F32), tok(BF16)], compiler_params=_params(("parallel",)),
    )(gl, gl, attn_b, w_attn_out, yg, w_glu, w_out, x, g_ffn)


def merge_bwd(dx1_b, w_out, gl, attn_b, w_attn_out, yg, w_glu, comm=None):
    tm = 512

    def body(dx_ref, w_ref, g0, g1, at, wa, yg_ref, wg, dgl_ref, dad_ref, dz_ref, dat_ref, dyg_ref):
        dm = _dot(dx_ref[...], w_ref[...], "nt")
        _, vjp = jax.vjp(_merge_fn, *_merge_operands(g0, g1, at, wa, yg_ref, wg))
        dg0, dg1, dad, dza, dzb = vjp(dm)
        dat_ref[...] = _dot(dad.astype(BF16), wa[...], "nt")
        dgl_ref[:, :D_MODEL] = dg0.astype(BF16)
        dgl_ref[:, D_MODEL:] = dg1.astype(BF16)
        dad_ref[...] = dad.astype(BF16)
        dz_ref[:, :D_MODEL] = dza.astype(BF16)
        dz_ref[:, D_MODEL:] = dzb.astype(BF16)
        dyg_ref[...] = _dot(dz_ref[...], wg[...], "nt")

    rows = pl.BlockSpec((tm, D_MODEL), lambda i: (i, 0))
    wide = pl.BlockSpec((tm, GATE_W), lambda i: (i, 0))
    whole = pl.BlockSpec((D_MODEL, D_MODEL), lambda i: (0, 0))
    return hosted_call(
        body, comm, "merge_bwd", (TOKENS // tm,), [rows, whole] + _merge_specs(tm),
        [wide, rows, wide, pl.BlockSpec((tm, GROUP_W), lambda i: (i, 0)), pl.BlockSpec((tm, SSM_W), lambda i: (i, 0))],
        [jax.ShapeDtypeStruct((TOKENS, GATE_W), BF16), jax.ShapeDtypeStruct((TOKENS, D_MODEL), BF16),
         jax.ShapeDtypeStruct((TOKENS, GATE_W), BF16), jax.ShapeDtypeStruct((TOKENS, GROUP_W), F32),
         jax.ShapeDtypeStruct((TOKENS, SSM_W), F32)], [],
        (dx1_b, w_out, gl, gl, attn_b, w_attn_out, yg, w_glu), ("arbitrary",))


FFN_TM, FFN_TN = 512, 1408


def ffn_in(h2, wg_t, wu_t, comm=None):
    def body(h_ref, wg_ref, wu_ref, a_ref, b_ref, act_ref):
        hv = h_ref[...]
        a, b = _dot(hv, wg_ref[...], "nt"), _dot(hv, wu_ref[...], "nt")
        a_ref[...] = a.astype(BF16)
        b_ref[...] = b.astype(BF16)
        act_ref[...] = _swiglu_fn(a, b).astype(BF16)

    rows = pl.BlockSpec((FFN_TM, D_MODEL), lambda i, j: (i, 0))
    wts = pl.BlockSpec((FFN_TN, D_MODEL), lambda i, j: (j, 0))
    out = pl.BlockSpec((FFN_TM, FFN_TN), lambda i, j: (i, j))
    return hosted_call(body, comm, "ffn_in", (TOKENS // FFN_TM, D_FF // FFN_TN), [rows, wts, wts], [out] * 3,
                       [jax.ShapeDtypeStruct((TOKENS, D_FF), BF16)] * 3, [], (h2, wg_t, wu_t),
                       ("parallel", "parallel"))


def ffn_in_bwd(dx2_b, wd, a, b):
    def body(dx_ref, wd_ref, a_ref, b_ref, da_ref, db_ref):
        dact = _dot(dx_ref[...], wd_ref[...], "nt")
        av, bv = a_ref[...].astype(F32), b_ref[...].astype(F32)
        sig = jax.nn.sigmoid(av)
        act = av * sig
        da_ref[...] = (dact * bv * (sig * (1.0 + av - act))).astype(BF16)
        db_ref[...] = (dact * act).astype(BF16)

    rows = pl.BlockSpec((FFN_TM, D_MODEL), lambda i, j: (i, 0))
    wts = pl.BlockSpec((FFN_TN, D_MODEL), lambda i, j: (j, 0))
    out = pl.BlockSpec((FFN_TM, FFN_TN), lambda i, j: (i, j))
    return pl.pallas_call(
        body, name="ffn_in_bwd", grid=(TOKENS // FFN_TM, D_FF // FFN_TN), in_specs=[rows, wts, out, out],
        out_specs=[out] * 2, out_shape=[jax.ShapeDtypeStruct((TOKENS, D_FF), BF16)] * 2,
        compiler_params=_params(("parallel", "parallel")),
    )(dx2_b, wd, a, b)


def mix_in_bwd(grads, weights, partial, x, g, skip, comm=None):
    n = len(grads)
    tm = 512

    def body(*refs):
        a_refs, b_refs = refs[:n], refs[n:2 * n]
        part_ref, x_ref, g_ref, skip_ref, gx_ref, dg_ref = refs[2 * n:]
        dh = part_ref[...]
        for a_ref, b_ref in zip(a_refs, b_refs):
            dh = dh + _dot(a_ref[...], b_ref[...], "nn")
        _, vjp = jax.vjp(_rms, x_ref[...], g_ref[...])
        dx, dg = vjp(dh)
        gx_ref[...] = dx + skip_ref[...]
        first = pl.program_id(0) == 0

        @pl.when(first)
        def _():
            dg_ref[...] = dg

        @pl.when(jnp.logical_not(first))
        def _():
            dg_ref[...] += dg

    rows = pl.BlockSpec((tm, D_MODEL), lambda i: (i, 0))
    gain = pl.BlockSpec((1, D_MODEL), lambda i: (0, 0))
    in_specs = [pl.BlockSpec((tm, a.shape[1]), lambda i: (i, 0)) for a in grads]
    in_specs += [pl.BlockSpec(b.shape, lambda i: (0, 0)) for b in weights]
    return hosted_call(
        body, comm, "mix_in_bwd", (TOKENS // tm,), in_specs + [rows, rows, gain, rows], [rows, gain],
        [jax.ShapeDtypeStruct((TOKENS, D_MODEL), F32), jax.ShapeDtypeStruct((1, D_MODEL), F32)], [],
        (*grads, *weights, partial, x, g, skip), ("arbitrary",))


def rowwise(fn, ins, outs, name, accs=(), tm=256, rows=TOKENS, comm=None):
    in_specs, args = [], []
    for item in ins:
        arr, width, blk = item if isinstance(item, tuple) else (item, None, 0)
        if arr.ndim == 3:
            for k in range(arr.shape[0]):
                in_specs.append(pl.BlockSpec((None, tm, arr.shape[2]), functools.partial(lambda i, k_: (k_, i, 0), k_=k)))
                args.append(arr)
            continue
        if arr.shape[0] == 1:
            in_specs.append(pl.BlockSpec(arr.shape, lambda i: (0, 0)))
        elif width is None:
            in_specs.append(pl.BlockSpec((tm, arr.shape[1]), lambda i: (i, 0)))
        else:
            in_specs.append(pl.BlockSpec((tm, width), functools.partial(lambda i, blk_: (i, blk_), blk_=blk)))
        args.append(arr)
    out_specs = [pl.BlockSpec((tm, c), lambda i: (i, 0)) for c, _ in outs]
    out_specs += [pl.BlockSpec((1, c), lambda i: (0, 0)) for c in accs]
    out_shape = [jax.ShapeDtypeStruct((rows, c), dt) for c, dt in outs]
    out_shape += [jax.ShapeDtypeStruct((1, c), F32) for c in accs]
    n_in, n_out = len(args), len(outs)
    c_ins, c_outs, c_sems = _comm_operands(comm)

    def body(*refs):
        refs, c_refs = _comm_refs(comm, refs, n_in, n_out + len(accs))
        step = pl.program_id(0)
        _comm_begin(comm, c_refs, step, rows // tm)
        res = fn(*[r[...] for r in refs[:n_in]])
        for r, v in zip(refs[n_in:n_in + n_out], res[:n_out]):
            r[...] = v.astype(r.dtype)
        first = step == 0
        for r, v in zip(refs[n_in + n_out:], res[n_out:]):
            @pl.when(first)
            def _(r=r, v=v):
                r[...] = v

            @pl.when(jnp.logical_not(first))
            def _(r=r, v=v):
                r[...] += v
        _comm_end(comm, c_refs, step, rows // tm)

    return pl.pallas_call(
        body, name=name, grid=(rows // tm,), in_specs=in_specs + [ANY] * len(c_ins),
        out_specs=out_specs + [ANY] * len(c_outs), out_shape=out_shape + c_outs, scratch_shapes=c_sems,
        compiler_params=_params(("arbitrary",)),
    )(*args, *c_ins)


def first_norm(x, g, others, comm=None):
    tm, n = 256, len(others)

    def body(x_ref, g_ref, *rest):
        srcs, h_ref, dsts = rest[:n], rest[n], rest[n + 1:]
        h_ref[...] = _rms(x_ref[...], g_ref[...]).astype(BF16)
        for k, (s, d) in enumerate(zip(srcs, dsts)):
            @pl.when(pl.program_id(0) == k)
            def _(s=s, d=d):
                d[...] = s[...].astype(BF16)

    rows = pl.BlockSpec((tm, D_MODEL), lambda i: (i, 0))
    whole = [pl.BlockSpec(a.shape, lambda i: (0, 0)) for a in others]
    return hosted_call(
        body, comm, "norm_mix", (TOKENS // tm,), [rows, pl.BlockSpec((1, D_MODEL), lambda i: (0, 0))] + whole,
        [rows] + whole, [jax.ShapeDtypeStruct((TOKENS, D_MODEL), BF16)]
        + [jax.ShapeDtypeStruct(a.shape, BF16) for a in others], [], (x, g, *others), ("arbitrary",))


def _rms(x, g):
    return x * lax.rsqrt(jnp.mean(x * x, axis=-1, keepdims=True) + RMS_EPS) * g


def _colsum(v):
    return jnp.sum(v, axis=0, keepdims=True)


PAIR_W = 2 * HEAD_DIM
N_PAIRS = HEADS_PER_GROUP // 2


def _qkv_order(w_t, back=False):
    dims = (N_PAIRS, N_GROUPS, 3) if back else (3, N_GROUPS, N_PAIRS)
    return w_t.reshape(dims + (PAIR_W, w_t.shape[1])).transpose(2, 1, 0, 3, 4).reshape(QKV_W, w_t.shape[1])


def _rope_tables():
    half = ROPE_DIM // 2
    inv = np.power(np.float32(ROPE_THETA), -np.arange(half, dtype=np.float32) * np.float32(2.0 / ROPE_DIM))
    ang = (np.arange(SEQ, dtype=np.float32)[:, None] * inv[None, :]).astype(np.float32)
    cos, sin = np.cos(ang), np.sin(ang)
    zeros = np.zeros((SEQ, HEAD_DIM - ROPE_DIM), np.float32)
    zh = np.zeros((SEQ, half), np.float32)
    c = np.concatenate([cos, cos, zeros + 1.0], axis=1)
    sa = np.concatenate([-sin, zh, zeros], axis=1)
    sb = np.concatenate([zh, sin, zeros], axis=1)
    return [jnp.asarray(np.tile(t, (1, 2)), F32) for t in (c, sa, sb)]


def _rope_fwd(x, c, sa, sb):
    return x * c + pltpu.roll(x, PAIR_W - 8, 1) * sa + pltpu.roll(x, 8, 1) * sb


def _rope_bwd(dy, c, sa, sb):
    return dy * c + pltpu.roll(dy * sb, PAIR_W - 8, 1) + pltpu.roll(dy * sa, 8, 1)


def _band_masks():
    row = lax.broadcasted_iota(jnp.int32, (ATTN_BLOCK, ATTN_BLOCK), 0)
    col = lax.broadcasted_iota(jnp.int32, (ATTN_BLOCK, ATTN_BLOCK), 1)
    return col <= row, col >= row


def _stack_rows(t):
    return jnp.concatenate([t, t], axis=0)


def _stack_heads(t, first_head):
    return jnp.concatenate([jnp.where(first_head, t, 0), jnp.where(first_head, 0, t)], axis=0)


def _per_head(fn):
    return jnp.concatenate([fn(slice(h * HEAD_DIM, (h + 1) * HEAD_DIM)) for h in range(2)], axis=1)


def _slab_spec(kind):
    return pl.BlockSpec((None, SEQ, PAIR_W), lambda b, p, g: (b, 0, p * 3 * N_GROUPS + g * 3 + kind))


_TABLE_SPEC = pl.BlockSpec((SEQ, PAIR_W), lambda b, p, g: (0, 0))
_PAIR_SPEC = pl.BlockSpec((None, SEQ, PAIR_W), lambda b, p, g: (b, 0, p))


def _block_rows(dil, r, n):
    return pl.ds(n * (ATTN_BLOCK * dil) + r, ATTN_BLOCK, stride=dil)


def proj_qkv(h, w_qkv_t, tables, comm=None):
    tm = 1024
    pair_w = QKV_W // N_PAIRS
    scale = HEAD_DIM ** -0.5

    def body(h_ref, w_ref, c_ref, sa_ref, sb_ref, o_ref):
        rows = _dot(h_ref[...], w_ref[...], "nt")
        c, sa, sb = c_ref[...], sa_ref[...], sb_ref[...]
        for blk in range(pair_w // PAIR_W):
            cols = slice(blk * PAIR_W, (blk + 1) * PAIR_W)
            x = rows[:, cols]
            if blk % 3 == 0:
                x = _rope_fwd(x, c, sa, sb) * scale
            elif blk % 3 == 1:
                x = _rope_fwd(x, c, sa, sb)
            o_ref[:, cols] = x

    table = pl.BlockSpec((tm, PAIR_W), lambda i, j, : (i % (SEQ // tm), 0))
    res = hosted_call(
        body, comm, "proj_qkv", (TOKENS // tm, N_PAIRS),
        [pl.BlockSpec((tm, D_MODEL), lambda i, j: (i, 0)), pl.BlockSpec((pair_w, D_MODEL), lambda i, j: (j, 0)),
         table, table, table],
        [pl.BlockSpec((tm, pair_w), lambda i, j: (i, j))], [jax.ShapeDtypeStruct((TOKENS, QKV_W), F32)], [],
        (h, w_qkv_t, *tables), ("parallel", "parallel"))
    return res[0] if comm is None else res


def attn_fwd(qkv, comm=None):
    def body(qs, ks, v_ref, attn_b_ref, attn_ref, lse_ref, o0, o1, o2, l0, l1, l2):
        g = pl.program_id(2)
        cur_mask, prev_mask = _band_masks()
        first_head = lax.broadcasted_iota(jnp.int32, (ATTN_BLOCK, PAIR_W), 1) < HEAD_DIM

        def run(dil, o_slab, l_slab):
            nb = SEQ // dil // ATTN_BLOCK

            def block(idx, carry):
                r, n = lax.div(idx, nb), lax.rem(idx, nb)
                cur, prev = _block_rows(dil, r, n), _block_rows(dil, r, jnp.maximum(n - 1, 0))
                q = qs[cur, :].astype(BF16)
                kc, kp = ks[cur, :].astype(BF16), ks[prev, :].astype(BF16)
                vc, vp = v_ref[cur, :].astype(BF16), v_ref[prev, :].astype(BF16)
                q2 = _stack_heads(q, first_head)
                mask = _stack_rows(jnp.concatenate([jnp.logical_and(prev_mask, n > 0), cur_mask], axis=1))
                s2 = jnp.where(mask, _dot(q2, jnp.concatenate([kp, kc], axis=0), "nt"), NEG_INF)
                m = jnp.max(s2, axis=-1, keepdims=True)
                vcat, two = jnp.concatenate([vp, vc], axis=0), _stack_rows(first_head)
                vext = jnp.concatenate([jnp.where(two, vcat, 1), jnp.where(two, 1, vcat)], axis=1)
                r2 = _dot(jnp.exp(s2 - m).astype(BF16), vext, "nn")
                r0, r1 = r2[:ATTN_BLOCK, :PAIR_W], r2[ATTN_BLOCK:, PAIR_W:]
                num = jnp.where(first_head, r0, r1)
                den = pltpu.roll(jnp.where(first_head, r1, r0), HEAD_DIM, 1)
                o_slab[cur, :] = num / den
                l_slab[cur, :] = jnp.where(first_head, m[:ATTN_BLOCK], m[ATTN_BLOCK:]) + jnp.log(den)
                return carry

            lax.fori_loop(0, SEQ // ATTN_BLOCK, block, 0, unroll=4)

        for gi, (o_slab, l_slab) in enumerate(((o0, l0), (o1, l1), (o2, l2))):
            @pl.when(g == gi)
            def _(gi=gi, o_slab=o_slab, l_slab=l_slab):
                run(DILATIONS[gi], o_slab, l_slab)

        @pl.when(g == N_GROUPS - 1)
        def _():
            a, b, cc = l0[...], l1[...], l2[...]
            m = jnp.maximum(jnp.maximum(a, b), cc)
            e0, e1, e2 = jnp.exp(a - m), jnp.exp(b - m), jnp.exp(cc - m)
            tot = e0 + e1 + e2
            attn = (e0 * o0[...] + e1 * o1[...] + e2 * o2[...]) / tot
            attn_ref[...] = attn
            attn_b_ref[...] = attn.astype(BF16)
            lse_ref[...] = m + jnp.log(tot)

    shape = (LOCAL_BATCH, SEQ, GROUP_W)
    slab = pltpu.VMEM((SEQ, PAIR_W), F32)
    return hosted_call(
        body, comm, "attn_fwd", (LOCAL_BATCH, N_PAIRS, N_GROUPS),
        [_slab_spec(0), _slab_spec(1), _slab_spec(2)], [_PAIR_SPEC] * 3,
        [jax.ShapeDtypeStruct(shape, BF16), jax.ShapeDtypeStruct(shape, F32), jax.ShapeDtypeStruct(shape, F32)],
        [slab] * 6, (qkv, qkv, qkv), ("parallel", "parallel", "arbitrary"))


def attn_bwd(qkv, tables, dattn, attn, lse, comm=None):
    scale = HEAD_DIM ** -0.5

    def body(qs, ks, v_ref, c_ref, sa_ref, sb_ref, do_ref, out_ref, lse_ref, dqkv_ref, dl, dq_s, dk_s, dv_s):
        g = pl.program_id(2)
        c, sa, sb = c_ref[...], sa_ref[...], sb_ref[...]

        @pl.when(g == 0)
        def _():
            prod = do_ref[...] * out_ref[...]
            dl[...] = _per_head(
                lambda sl: jnp.broadcast_to(jnp.sum(prod[:, sl], axis=-1, keepdims=True), (SEQ, HEAD_DIM)))

        cur_mask, prev_mask = _band_masks()
        first_head = lax.broadcasted_iota(jnp.int32, (ATTN_BLOCK, PAIR_W), 1) < HEAD_DIM

        def run(dil):
            nb = SEQ // dil // ATTN_BLOCK

            def block(idx, carry):
                r, n = lax.div(idx, nb), lax.rem(idx, nb)
                cur = _block_rows(dil, r, n)
                prev = _block_rows(dil, r, jnp.maximum(n - 1, 0))
                nxt = _block_rows(dil, r, jnp.minimum(n + 1, nb - 1))
                q0, q1 = qs[cur, :].astype(BF16), qs[nxt, :].astype(BF16)
                kp, kc = ks[prev, :].astype(BF16), ks[cur, :].astype(BF16)
                vp, vc = v_ref[prev, :].astype(BF16), v_ref[cur, :].astype(BF16)
                do0, do1 = do_ref[cur, :].astype(BF16), do_ref[nxt, :].astype(BF16)
                lse0, lse1, dl0, dl1 = lse_ref[cur, :], lse_ref[nxt, :], dl[cur, :], dl[nxt, :]
                has_prev = jnp.logical_and(prev_mask, n > 0)
                has_next = jnp.logical_and(prev_mask, n < nb - 1)

                def per_row(t):
                    return jnp.concatenate([t[:, 0:1], t[:, HEAD_DIM:HEAD_DIM + 1]], axis=0)

                q20, q21 = _stack_heads(q0, first_head), _stack_heads(q1, first_head)
                do20, do21 = _stack_heads(do0, first_head), _stack_heads(do1, first_head)
                kcat, vcat = jnp.concatenate([kp, kc], axis=0), jnp.concatenate([vp, vc], axis=0)
                mask0 = _stack_rows(jnp.concatenate([has_prev, cur_mask], axis=1))
                p0 = jnp.where(mask0, jnp.exp(_dot(q20, kcat, "nt") - per_row(lse0)), 0.0)
                ds0 = (p0 * (_dot(do20, vcat, "nt") - per_row(dl0))).astype(BF16)
                p1 = jnp.where(_stack_rows(has_next), jnp.exp(_dot(q21, kc, "nt") - per_row(lse1)), 0.0)
                ds1 = (p1 * (_dot(do21, vc, "nt") - per_row(dl1))).astype(BF16)
                dq2 = _dot(ds0, kcat, "nn")
                dq_s[cur, :] = jnp.where(first_head, dq2[:ATTN_BLOCK], dq2[ATTN_BLOCK:])
                ds_cur = jnp.concatenate([ds0[:, ATTN_BLOCK:], ds1], axis=0)
                p_cur = jnp.concatenate([p0[:, ATTN_BLOCK:], p1], axis=0).astype(BF16)
                dk_s[cur, :] = _dot(ds_cur, jnp.concatenate([q20, q21], axis=0), "tn")
                dv_s[cur, :] = _dot(p_cur, jnp.concatenate([do20, do21], axis=0), "tn")
                return carry

            lax.fori_loop(0, SEQ // ATTN_BLOCK, block, 0, unroll=2)

        for gi in range(N_GROUPS):
            @pl.when(g == gi)
            def _(gi=gi):
                run(DILATIONS[gi])

        dqkv_ref[:, 0:PAIR_W] = _rope_bwd(dq_s[...] * scale, c, sa, sb).astype(BF16)
        dqkv_ref[:, PAIR_W:2 * PAIR_W] = _rope_bwd(dk_s[...], c, sa, sb).astype(BF16)
        dqkv_ref[:, 2 * PAIR_W:] = dv_s[...].astype(BF16)

    slab = pltpu.VMEM((SEQ, PAIR_W), F32)
    return hosted_call(
        body, comm, "attn_bwd", (LOCAL_BATCH, N_PAIRS, N_GROUPS),
        [_slab_spec(0), _slab_spec(1), _slab_spec(2), _TABLE_SPEC, _TABLE_SPEC, _TABLE_SPEC,
         _PAIR_SPEC, _PAIR_SPEC, _PAIR_SPEC],
        [pl.BlockSpec((None, SEQ, 3 * PAIR_W), lambda b, p, g: (b, 0, p * N_GROUPS + g))],
        [jax.ShapeDtypeStruct((LOCAL_BATCH, SEQ, QKV_W), BF16)],
        [slab] * 4, (qkv, qkv, qkv, *tables, dattn, attn, lse), ("parallel", "parallel", "arbitrary"))


def _discretize(lr, li, log_dt, br, bi):
    dt = jnp.exp(log_dt)
    mag = jnp.exp(lr * dt)
    ab_re, ab_im = mag * jnp.cos(li * dt), mag * jnp.sin(li * dt)
    den = lr * lr + li * li
    nr, ni = ab_re - 1.0, ab_im
    f_re = (nr * lr + ni * li) / den
    f_im = (ni * lr - nr * li) / den
    return ab_re, ab_im, f_re[None] * br - f_im[None] * bi, f_re[None] * bi + f_im[None] * br


def ssm_prep(lr, li, log_dt, br, bi):
    def body(lr_ref, li_ref, dt_ref, br_ref, bi_ref, *outs):
        for o, v in zip(outs, _discretize(lr_ref[...], li_ref[...], dt_ref[...], br_ref[...], bi_ref[...])):
            o[...] = v
    shapes = [lr, li, br, bi]
    return pl.pallas_call(body, name="ssm_prep",
                          out_shape=[jax.ShapeDtypeStruct(s.shape, F32) for s in shapes])(lr, li, log_dt, br, bi)


def ssm_prep_bwd(lr, li, log_dt, br, bi, g_ab_re, g_ab_im, g_bb_re, g_bb_im):
    def body(lr_ref, li_ref, dt_ref, br_ref, bi_ref, g0, g1, g2, g3, *outs):
        _, vjp = jax.vjp(_discretize, lr_ref[...], li_ref[...], dt_ref[...], br_ref[...], bi_ref[...])
        for o, v in zip(outs, vjp((g0[...], g1[...], g2[...], g3[...]))):
            o[...] = v
    shapes = [lr, li, log_dt, br, bi]
    return pl.pallas_call(body, name="ssm_prep_bwd",
                          out_shape=[jax.ShapeDtypeStruct(s.shape, F32) for s in shapes])(
        lr, li, log_dt, br, bi, g_ab_re, g_ab_im, g_bb_re, g_bb_im)


def _block_diag(t):
    per = SSM_STATE_W // SSM_LANE_BLOCKS // 64
    g = t.transpose(1, 0, 2).reshape(SSM_LANE_BLOCKS, per, 16, 64)
    eye = jnp.eye(per, dtype=t.dtype)
    return jnp.einsum("jgcn,gh->jgchn", g, eye).reshape(SSM_LANE_BLOCKS, per * 16, per * 64)


def _block_diag_t(m):
    per = SSM_STATE_W // SSM_LANE_BLOCKS // 64
    m5 = m.reshape(SSM_LANE_BLOCKS, per, 16, per, 64)
    d = jnp.einsum("jgchn,gh->jgcn", m5, jnp.eye(per, dtype=m.dtype))
    return d.reshape(SSM_LANE_BLOCKS * per, 16, 64).transpose(1, 0, 2)


def _cmul(ar, ai, br, bi):
    return ar * br - ai * bi, ar * bi + ai * br


def _power_tables(ar, ai, reverse):
    width = ar.shape[1]
    row = lax.broadcasted_iota(jnp.int32, (8, width), 0)
    pows = [(ar, ai)]
    for _ in range(7):
        pows.append(_cmul(pows[-1][0], pows[-1][1], ar, ai))
    steps = []
    for k in (1, 2, 4):
        keep = (row >= k) if not reverse else (row < 8 - k)
        steps.append((jnp.where(keep, pows[k - 1][0], 0.0), jnp.where(keep, pows[k - 1][1], 0.0)))
    cr = jnp.zeros((8, width), F32)
    ci = jnp.zeros((8, width), F32)
    for i in range(8):
        pr, pi = pows[i] if not reverse else pows[7 - i]
        cr = jnp.where(row == i, pr, cr)
        ci = jnp.where(row == i, pi, ci)
    return steps, (cr, ci)


SCAN_CHUNK = 2048
STATE_BLOCK = SSM_STATE_W // SSM_LANE_BLOCKS
CHAN_BLOCK = SSM_W // SSM_LANE_BLOCKS


def ssm_fwd(u, ab_re, ab_im, bb_re, bb_im, cb_re, cb_im, d_skip, comm=None):
    nt = SEQ // SCAN_CHUNK
    chan = pl.BlockSpec((None, SCAN_CHUNK, CHAN_BLOCK), lambda b, j, t: (b, t, j))
    state = pl.BlockSpec((None, SCAN_CHUNK, STATE_BLOCK), lambda b, j, t: (b, t, j))
    mat = pl.BlockSpec((None, CHAN_BLOCK, STATE_BLOCK), lambda b, j, t: (j, 0, 0))
    lane = pl.BlockSpec((1, STATE_BLOCK), lambda b, j, t: (0, j))
    dsp = pl.BlockSpec((1, CHAN_BLOCK), lambda b, j, t: (0, j))

    def body(u_ref, ar_ref, ai_ref, bbr_ref, bbi_ref, cbr_ref, cbi_ref, d_ref, y_ref, yg_ref, xr_ref, xi_ref,
             car_r, car_i):
        @pl.when(pl.program_id(2) == 0)
        def _():
            car_r[...] = jnp.zeros_like(car_r)
            car_i[...] = jnp.zeros_like(car_i)

        steps, (pr, pi) = _power_tables(ar_ref[...], ai_ref[...], reverse=False)
        uf = u_ref[...]
        ub = uf.astype(BF16)
        xr_ref[...] = _dot(ub, bbr_ref[...], "nn")
        xi_ref[...] = _dot(ub, bbi_ref[...], "nn")

        def tile(i, carry):
            cr, ci = carry
            sl = pl.ds(pl.multiple_of(i * 8, 8), 8)
            br, bi = xr_ref[sl, :], xi_ref[sl, :]
            for k, (sr, si) in zip((1, 2, 4), steps):
                tr, ti = _cmul(sr, si, pltpu.roll(br, k, 0), pltpu.roll(bi, k, 0))
                br, bi = br + tr, bi + ti
            tr, ti = _cmul(pr, pi, cr, ci)
            br, bi = br + tr, bi + ti
            xr_ref[sl, :] = br
            xi_ref[sl, :] = bi
            return br[7:8, :], bi[7:8, :]

        cr, ci = lax.fori_loop(0, SCAN_CHUNK // 8, tile, (car_r[0:1, :], car_i[0:1, :]), unroll=4)
        car_r[0:1, :] = cr
        car_i[0:1, :] = ci
        y = (_dot(xr_ref[...].astype(BF16), cbr_ref[...], "nt") - _dot(xi_ref[...].astype(BF16), cbi_ref[...], "nt")
             + d_ref[...] * uf)
        y_ref[...] = y
        yg_ref[...] = jax.nn.gelu(y).astype(BF16)

    return hosted_call(
        body, comm, "ssm_fwd", (LOCAL_BATCH, SSM_LANE_BLOCKS, nt),
        [chan, lane, lane, mat, mat, mat, mat, dsp], [chan, chan, state, state],
        [jax.ShapeDtypeStruct((LOCAL_BATCH, SEQ, SSM_W), F32), jax.ShapeDtypeStruct((LOCAL_BATCH, SEQ, SSM_W), BF16),
         jax.ShapeDtypeStruct((LOCAL_BATCH, SEQ, SSM_STATE_W), F32),
         jax.ShapeDtypeStruct((LOCAL_BATCH, SEQ, SSM_STATE_W), F32)],
        [pltpu.VMEM((8, STATE_BLOCK), F32), pltpu.VMEM((8, STATE_BLOCK), F32)],
        (u, ab_re, ab_im, bb_re, bb_im, cb_re, cb_im, d_skip), ("parallel", "parallel", "arbitrary"))


def ssm_bwd(dyg, y, u, xr, xi, ab_re, ab_im, bb_re, bb_im, cb_re, cb_im, d_skip, comm=None):
    nt = SEQ // SCAN_CHUNK
    ntile = SCAN_CHUNK // 8

    def rev(t):
        return nt - 1 - t

    chan = pl.BlockSpec((None, SCAN_CHUNK, CHAN_BLOCK), lambda j, b, t: (b, rev(t), j))
    state = pl.BlockSpec((None, SCAN_CHUNK, STATE_BLOCK), lambda j, b, t: (b, rev(t), j))
    before = pl.BlockSpec((None, 8, STATE_BLOCK), lambda j, b, t: (b, jnp.maximum(rev(t) * ntile - 1, 0), j))
    mat = pl.BlockSpec((None, CHAN_BLOCK, STATE_BLOCK), lambda j, b, t: (j, 0, 0))
    lane = pl.BlockSpec((1, STATE_BLOCK), lambda j, b, t: (0, j))
    lane8 = pl.BlockSpec((8, STATE_BLOCK), lambda j, b, t: (0, j))
    dsp = pl.BlockSpec((1, CHAN_BLOCK), lambda j, b, t: (0, j))

    def body(dyg_ref, y_ref, u_ref, xr_ref, xi_ref, xrb_ref, xib_ref, ar_ref, ai_ref, bbr_ref, bbi_ref, cbr_ref,
             cbi_ref, d_ref, du_ref, dcbr_ref, dcbi_ref, dbbr_ref, dbbi_ref, dd_ref, dar_ref, dai_ref,
             lam_r, lam_i, car_r, car_i):
        b, t = pl.program_id(1), pl.program_id(2)
        first = jnp.logical_and(b == 0, t == 0)

        @pl.when(t == 0)
        def _():
            car_r[...] = jnp.zeros_like(car_r)
            car_i[...] = jnp.zeros_like(car_i)

        @pl.when(first)
        def _():
            for r in (dcbr_ref, dcbi_ref, dbbr_ref, dbbi_ref, dd_ref, dar_ref, dai_ref):
                r[...] = jnp.zeros_like(r)

        steps, (pr, pi) = _power_tables(ar_ref[...], -ai_ref[...], reverse=True)
        uf = u_ref[...]
        _, gelu_vjp = jax.vjp(jax.nn.gelu, y_ref[...])
        dy = gelu_vjp(dyg_ref[...])[0]
        dyb = dy.astype(BF16)
        dd_ref[...] += _colsum(dy * uf)
        lam_r[...] = _dot(dyb, cbr_ref[...], "nn")
        lam_i[...] = -_dot(dyb, cbi_ref[...], "nn")
        dcbr_ref[...] += _dot(dyb, xr_ref[...].astype(BF16), "tn")
        dcbi_ref[...] -= _dot(dyb, xi_ref[...].astype(BF16), "tn")
        row0 = lax.broadcasted_iota(jnp.int32, (8, STATE_BLOCK), 0) == 0
        has_before = rev(t) > 0
        xrb = jnp.where(has_before, xrb_ref[...], 0.0)
        xib = jnp.where(has_before, xib_ref[...], 0.0)

        def tile(s, carry):
            cr, ci, acc_r, acc_i = carry
            i = ntile - 1 - s
            sl = pl.ds(pl.multiple_of(i * 8, 8), 8)
            gr, gi = lam_r[sl, :], lam_i[sl, :]
            for k, (sr, si) in zip((1, 2, 4), steps):
                tr, ti = _cmul(sr, si, pltpu.roll(gr, 8 - k, 0), pltpu.roll(gi, 8 - k, 0))
                gr, gi = gr + tr, gi + ti
            tr, ti = _cmul(pr, pi, cr, ci)
            gr, gi = gr + tr, gi + ti
            lam_r[sl, :] = gr
            lam_i[sl, :] = gi
            sp = pl.ds(pl.multiple_of(jnp.maximum(i - 1, 0) * 8, 8), 8)
            pvr = jnp.where(i > 0, xr_ref[sp, :], xrb)
            pvi = jnp.where(i > 0, xi_ref[sp, :], xib)
            xsr = jnp.where(row0, pltpu.roll(pvr, 1, 0), pltpu.roll(xr_ref[sl, :], 1, 0))
            xsi = jnp.where(row0, pltpu.roll(pvi, 1, 0), pltpu.roll(xi_ref[sl, :], 1, 0))
            acc_r = acc_r + xsr * gr + xsi * gi
            acc_i = acc_i + xsr * gi - xsi * gr
            return gr[0:1, :], gi[0:1, :], acc_r, acc_i

        zero = jnp.zeros((8, STATE_BLOCK), F32)
        cr, ci, acc_r, acc_i = lax.fori_loop(0, ntile, tile, (car_r[0:1, :], car_i[0:1, :], zero, zero), unroll=2)
        car_r[0:1, :] = cr
        car_i[0:1, :] = ci
        dar_ref[...] += acc_r
        dai_ref[...] += acc_i
        lrb, lib = lam_r[...].astype(BF16), lam_i[...].astype(BF16)
        du = _dot(lrb, bbr_ref[...], "nt") + _dot(lib, bbi_ref[...], "nt") + d_ref[...] * dy
        du_ref[...] = du.astype(BF16)
        ub = uf.astype(BF16)
        dbbr_ref[...] += _dot(ub, lrb, "tn")
        dbbi_ref[...] += _dot(ub, lib, "tn")

    mat_shape = jax.ShapeDtypeStruct((SSM_LANE_BLOCKS, CHAN_BLOCK, STATE_BLOCK), F32)
    return hosted_call(
        body, comm, "ssm_bwd", (SSM_LANE_BLOCKS, LOCAL_BATCH, nt),
        [chan, chan, chan, state, state, before, before, lane, lane, mat, mat, mat, mat, dsp],
        [chan, mat, mat, mat, mat, dsp, lane8, lane8],
        [jax.ShapeDtypeStruct((LOCAL_BATCH, SEQ, SSM_W), BF16), mat_shape, mat_shape, mat_shape, mat_shape,
         jax.ShapeDtypeStruct((1, SSM_W), F32), jax.ShapeDtypeStruct((8, SSM_STATE_W), F32),
         jax.ShapeDtypeStruct((8, SSM_STATE_W), F32)],
        [pltpu.VMEM((SCAN_CHUNK, STATE_BLOCK), F32), pltpu.VMEM((SCAN_CHUNK, STATE_BLOCK), F32),
         pltpu.VMEM((8, STATE_BLOCK), F32), pltpu.VMEM((8, STATE_BLOCK), F32)],
        (dyg, y, u, xr, xi, xr, xi, ab_re, ab_im, bb_re, bb_im, cb_re, cb_im, d_skip),
        ("parallel", "arbitrary", "arbitrary"))


def _merge_fn(g0, g1, attn_d, za, zb):
    return jax.nn.sigmoid(g0) * attn_d + jax.nn.sigmoid(g1) * (za * jax.nn.sigmoid(zb))


def _swiglu_fn(a, b):
    return jax.nn.silu(a) * b


def _own_slot(slots, shard):
    me = 2 * lax.axis_index("x") + lax.axis_index("y")
    mine = lax.broadcasted_iota(jnp.int32, (N_CHIPS, 1, 1), 0) == me
    return jnp.where(mine, shard[None], slots)


def _reduce_start(names, gw, shard_shapes):
    return swap_comm([_to_slots(n, gw[n], shard_shapes[n]) for n in names])


def _reduce_chip(names, swap, got, core):
    return exchange_comm([add_halves(n, g, r, core) for n, g, r in zip(names, swap.ins, got)])


def local_step(x, target, shards, small, core):
    g_mix, g_ffn, g_final = small["norm_mix_g"], small["norm_ffn_g"], small["norm_final_g"]
    tables = _rope_tables()
    seqs = lambda t: t.reshape(LOCAL_BATCH, SEQ, t.shape[-1])
    toks = lambda t: t.reshape(TOKENS, t.shape[-1])
    shard_shapes = {n: s.shape for n, s in shards.items()}
    w = {}

    def gather(names):
        return gather_comm([shards[n] for n in names])

    def arrived(names, slots, own=None):
        for n, s in zip(names, slots):
            w[n] = _from_slots(n, s if own is None else _own_slot(s, own))

    later = [n for n in BIG if n != "w_in"]
    sems, w_in_shard, land, token = split_start(shards["w_in"].astype(BF16), "w_in_gather_start")
    zero = token[0, 0]
    h, *rest = first_norm(x, g_mix + zero, [shards[n] for n in later])
    shards = dict(shards)
    shards.update(zip(later, rest))
    br_t = small["ssm_b_re"].transpose(2, 0, 1)
    bi_t = small["ssm_b_im"].transpose(2, 0, 1)
    log_dt = small["ssm_log_dt"].reshape(32, 1)
    ab_re, ab_im, bb_re_t, bb_im_t = ssm_prep(small["ssm_a_re"] + zero, small["ssm_a_im"], log_dt, br_t, bi_t)
    ab = [ab_re.reshape(1, SSM_STATE_W), ab_im.reshape(1, SSM_STATE_W)]
    bb = [_block_diag(bb_re_t).astype(BF16), _block_diag(bb_im_t).astype(BF16)]
    cb = [_block_diag((small["ssm_c_re"] + zero).transpose(1, 0, 2)).astype(BF16),
          _block_diag((small["ssm_c_im"] + zero).transpose(1, 0, 2)).astype(BF16)]
    d_skip = small["ssm_d"].reshape(1, SSM_W)
    w_in_shard, land = split_wait(sems, w_in_shard, land, [h] + bb + cb, "w_in_gather_wait")
    arrived(["w_in"], [handover(land, "w_in_handover")], own=w_in_shard)
    w_qkv, w_u, w_gate = _qkv_order(w["w_in"][:QKV_W]), w["w_in"][QKV_W:QKV_W + SSM_W], w["w_in"][QKV_W + SSM_W:]
    qkv, *slots = proj_qkv(h, w_qkv, tables, comm=gather(["w_attn_out", "w_glu"]))
    arrived(["w_attn_out", "w_glu"], slots)
    qkv = seqs(qkv)
    u = seqs(matmul(h, w_u, "nt", F32, "proj_u"))
    gl, *slots = matmul(h, w_gate, "nt", BF16, "proj_gate", comm=gather(["w_out"]))
    arrived(["w_out"], slots)
    attn_b, attn, lse, *slots = attn_fwd(qkv, comm=gather(["w_ffn_gate"]))
    arrived(["w_ffn_gate"], slots)
    attn_b = toks(attn_b)
    y, yg, xr, xi, *slots = ssm_fwd(u, *ab, *bb, *cb, d_skip, comm=gather(["w_ffn_up"]))
    arrived(["w_ffn_up"], slots)
    yg2 = toks(yg)
    merged, x1, h2 = merge_out_proj(gl, attn_b, w["w_attn_out"], yg2, w["w_glu"], w["w_out"], x, g_ffn)
    a, b, act, *slots = ffn_in(h2, w["w_ffn_gate"], w["w_ffn_up"], comm=gather(["w_ffn_down"]))
    arrived(["w_ffn_down"], slots)

    def final_fn(xv, g, tgt):
        yv, vjp = jax.vjp(_rms, xv, g)
        err = yv - tgt
        dx, dg = vjp(err * (1.0 / D_MODEL))
        loss = 0.5 * jnp.sum(jnp.mean(err * err, axis=-1, keepdims=True), axis=0, keepdims=True)
        return dx, dx, dg, jnp.broadcast_to(loss, (1, LANES))

    dx2, dx2_b, dg_final, loss = matmul_rows(act, w["w_ffn_down"], "ffn_down_loss", final_fn, [g_final, target],
                                             [(D_MODEL, F32), (D_MODEL, BF16)], accs=(D_MODEL, LANES), add=x1)
    gw, parts = {}, {}
    gw["w_ffn_down"] = matmul(act, dx2_b, "tn", F32, "d_ffn_down")
    da_b, db_b = ffn_in_bwd(dx2_b, w["w_ffn_down"], a, b)
    gw["w_ffn_gate"] = matmul(da_b, h2, "tn", F32, "d_ffn_gate")
    gw["w_ffn_up"] = matmul(db_b, h2, "tn", F32, "d_ffn_up")
    ffn = ["w_ffn_down", "w_ffn_gate", "w_ffn_up"]
    swap = _reduce_start(ffn[:2], gw, shard_shapes)
    dh2, *got = matmul(da_b, w["w_ffn_gate"], "nn", F32, "d_h2_gate", comm=swap)
    ffn_exchange = [_reduce_chip(ffn[:2], swap, got, core)]
    swap = _reduce_start(ffn[2:], gw, shard_shapes)

    def norm_bwd(dh, xv, g, skip):
        _, vjp = jax.vjp(_rms, xv, g)
        dx, dg = vjp(dh)
        dx = dx + skip
        return dx, dx, dg

    dx1, dx1_b, dg_ffn, *got = matmul_rows(db_b, w["w_ffn_up"], "d_h2_up_norm", norm_bwd, [x1, g_ffn, dx2],
                                           [(D_MODEL, F32), (D_MODEL, BF16)], accs=(D_MODEL,), add=dh2, comm=swap)
    ffn_up_exchange = _reduce_chip(ffn[2:], swap, got, core)
    gw["w_out"] = matmul(merged, dx1_b, "tn", F32, "d_out")
    dgl_b, dattn_d_b, dz_b, dattn, dyg, parts["w_ffn_up"] = merge_bwd(
        dx1_b, w["w_out"], gl, attn_b, w["w_attn_out"], yg2, w["w_glu"], comm=ffn_up_exchange)
    dattn, dyg = seqs(dattn), seqs(dyg)
    gw["w_attn_out"] = matmul(attn_b, dattn_d_b, "tn", F32, "d_attn_out")
    gw["w_glu"] = matmul(yg2, dz_b, "tn", F32, "d_glu")
    mixer = ["w_out", "w_attn_out", "w_glu"]
    swap = _reduce_start(mixer, gw, shard_shapes)
    du_b, dcb_re, dcb_im, dbb_re, dbb_im, dd, da_re8, da_im8, *rest = ssm_bwd(
        dyg, y, u, xr, xi, *ab, *bb, *cb, d_skip, comm=join_comms(ffn_exchange + [swap]))
    for n, p in zip(ffn[:2], rest[:2]):
        parts[n] = p
    mixer_exchange = _reduce_chip(mixer, swap, rest[2:], core)
    du_b = toks(du_b)
    g_ab_re = jnp.sum(da_re8, axis=0).reshape(32, 64)
    g_ab_im = jnp.sum(da_im8, axis=0).reshape(32, 64)
    d_lr, d_li, d_ldt, d_br_t, d_bi_t = ssm_prep_bwd(
        small["ssm_a_re"], small["ssm_a_im"], log_dt, br_t, bi_t,
        g_ab_re, g_ab_im, _block_diag_t(dbb_re), _block_diag_t(dbb_im))
    as_gcn = lambda t: t.transpose(1, 0, 2).reshape(SSM_W, 64)
    gs = {
        "ssm_a_re": d_lr, "ssm_a_im": d_li, "ssm_log_dt": d_ldt.reshape(1, 32),
        "ssm_b_re": as_gcn(d_br_t), "ssm_b_im": as_gcn(d_bi_t),
        "ssm_c_re": as_gcn(_block_diag_t(dcb_re)), "ssm_c_im": as_gcn(_block_diag_t(dcb_im)),
        "ssm_d": dd.reshape(32, 16).T,
    }
    ssm_gather = small_comm([gs[n] for n in SSM_SMALL])
    dqkv_b, *rest = attn_bwd(qkv, tables, dattn, attn, lse, comm=join_comms([mixer_exchange, ssm_gather]))
    for n, p in zip(mixer, rest):
        parts[n] = p
    ssm_shares = rest[len(mixer):]
    dqkv_b = toks(dqkv_b)
    d_qkv = matmul(dqkv_b, h, "tn", F32, "d_w_qkv")
    d_u = matmul(du_b, h, "tn", F32, "d_w_u")
    d_gate = matmul(dgl_b, h, "tn", F32, "d_w_gate")
    gw["w_in"] = jnp.concatenate([_qkv_order(d_qkv, back=True), d_u, d_gate], axis=0)
    swap = _reduce_start(["w_in"], gw, shard_shapes)
    dh, *got = matmul(dqkv_b, w_qkv, "nn", F32, "d_h_qkv", comm=swap)
    chip_sum = add_halves("w_in", swap.ins[0], got[0], core)
    sems, chip_sum, land, token = split_start(chip_sum, "w_in_reduce_start", per_chip=True)
    grad_x, dg_mix = mix_in_bwd([du_b, dgl_b], [w_u, w_gate], dh, x, g_mix + token[0, 0], dx1)
    gs_norm = {"norm_mix_g": dg_mix, "norm_ffn_g": dg_ffn, "norm_final_g": dg_final}
    return loss, grad_x, parts, ssm_shares, gs_norm, (sems, chip_sum, land)


ANY = pl.BlockSpec(memory_space=pl.ANY)
BIG = ("w_in", "w_glu", "w_attn_out", "w_out", "w_ffn_gate", "w_ffn_up", "w_ffn_down")
TRANSPOSED = ("w_in", "w_ffn_gate", "w_ffn_up")
ROW_SHARDED = TRANSPOSED + ("w_out", "w_ffn_down")
SMALL = ("norm_mix_g", "ssm_a_re", "ssm_a_im", "ssm_log_dt", "ssm_b_re", "ssm_b_im", "ssm_c_re", "ssm_c_im",
         "ssm_d", "norm_ffn_g", "norm_final_g")
WEIGHTS = ("norm_mix_g", "w_in", "ssm_a_re", "ssm_a_im", "ssm_log_dt", "ssm_b_re", "ssm_b_im", "ssm_c_re",
           "ssm_c_im", "ssm_d", "w_glu", "w_attn_out", "w_out", "norm_ffn_g", "w_ffn_gate", "w_ffn_up",
           "w_ffn_down", "norm_final_g")
SSM_SMALL = SMALL[1:9]
NORM_SMALL = (SMALL[0],) + SMALL[9:]
NORM_ROWS = 32
N_BIG = len(BIG)


def _position():
    return lax.axis_index("x"), lax.axis_index("y"), lax.axis_index("c")


def _other_chips(x, y):
    return [(1 - x, y), (x, 1 - y), (1 - x, 1 - y)]


def _remote(src, dst, send_sem, recv_sem, device):
    return pltpu.make_async_remote_copy(src_ref=src, dst_ref=dst, send_sem=send_sem, recv_sem=recv_sem,
                                        device_id=device, device_id_type=MESH)


_later = functools.partial


def _two_level_phases(copies):
    def first(*refs):
        locals_, sends, _, _, _ = copies(*refs)
        for cp in locals_ + sends:
            cp().start()

    def mid(*refs):
        _, _, arrived, passed, _ = copies(*refs)
        for got, cp in zip(arrived, passed):
            got().wait_recv()
            cp().start()

    def last(*refs):
        locals_, sends, _, passed, from_sibling = copies(*refs)
        for cp in from_sibling:
            cp().wait_recv()
        for cp in sends + passed:
            cp().wait_send()
        for cp in locals_:
            cp().wait()

    return first, mid, last


def _half(ref, chip, which):
    rows = ref.shape[1] // 2
    return ref.at[chip, pl.ds(which * rows, rows), :]


class Comm:
    def __init__(self, ins, out_shapes, sems, first, mid, last):
        self.ins, self.out_shapes, self.sems = list(ins), list(out_shapes), list(sems)
        self.first, self.mid, self.last = first, mid, last


def join_comms(comms):
    def cut(refs_by_kind):
        offs, parts = [0, 0, 0], []
        for cm in comms:
            sizes = (len(cm.ins), len(cm.out_shapes), len(cm.sems))
            parts.append(tuple(refs_by_kind[k][offs[k]:offs[k] + sizes[k]] for k in range(3)))
            offs = [o + s for o, s in zip(offs, sizes)]
        return parts

    def phase(which):
        def run(ins, outs, sems):
            for cm, part in zip(comms, cut((ins, outs, sems))):
                fn = getattr(cm, which)
                if fn is not None:
                    fn(*part)
        return run

    return Comm(sum((cm.ins for cm in comms), []), sum((cm.out_shapes for cm in comms), []),
                sum((cm.sems for cm in comms), []), phase("first"), phase("mid"), phase("last"))


def _comm_operands(comm):
    if comm is None:
        return [], [], []
    return comm.ins, comm.out_shapes, comm.sems


def _comm_begin(comm, refs, step, n_steps):
    if comm is None:
        return
    pl.when(step == 0)(lambda: comm.first(*refs))
    if comm.mid is not None:
        pl.when(step == (n_steps * 3) // 4)(lambda: comm.mid(*refs))


def _comm_end(comm, refs, step, n_steps):
    if comm is not None:
        pl.when(step == n_steps - 1)(lambda: comm.last(*refs))


def _comm_refs(comm, refs, n_in, n_out):
    if comm is None:
        return list(refs), None
    ci, co, cs = len(comm.ins), len(comm.out_shapes), len(comm.sems)
    o0 = n_in + ci
    s0 = o0 + n_out + co
    host = list(refs[:n_in]) + list(refs[o0:o0 + n_out]) + list(refs[s0:len(refs) - cs])
    return host, (list(refs[n_in:o0]), list(refs[o0 + n_out:s0]), list(refs[len(refs) - cs:]))


def run_comm(comm, name):
    n_in, n_out = len(comm.ins), len(comm.out_shapes)

    def body(*refs):
        parts = (list(refs[:n_in]), list(refs[n_in:n_in + n_out]), list(refs[n_in + n_out:]))
        comm.first(*parts)
        if comm.mid is not None:
            comm.mid(*parts)
        comm.last(*parts)

    return pl.pallas_call(body, name=name, in_specs=[ANY] * n_in, out_specs=[ANY] * n_out,
                          out_shape=comm.out_shapes, scratch_shapes=comm.sems)(*comm.ins)


def hosted_call(work, comm, name, grid, in_specs, out_specs, out_shape, scratch_shapes, args, semantics):
    c_ins, c_outs, c_sems = _comm_operands(comm)
    n_steps = math.prod(grid)

    def body(*refs):
        host, c_refs = _comm_refs(comm, refs, len(in_specs), len(out_specs))
        step = 0
        for axis, size in enumerate(grid):
            step = step * size + pl.program_id(axis)
        _comm_begin(comm, c_refs, step, n_steps)
        work(*host)
        _comm_end(comm, c_refs, step, n_steps)

    return pl.pallas_call(
        body, name=name, grid=grid, in_specs=list(in_specs) + [ANY] * len(c_ins),
        out_specs=list(out_specs) + [ANY] * len(c_outs), out_shape=list(out_shape) + c_outs,
        scratch_shapes=list(scratch_shapes) + c_sems,
        compiler_params=_params(semantics if comm is None else ("arbitrary",) * len(grid)),
    )(*args, *c_ins)


def gather_comm(shards):
    n = len(shards)

    def copies(srcs, outs, sems):
        send_sems, recv_sems, local_sems = sems
        x, y, c = _position()
        me = 2 * x + y
        sibling = (x, y, 1 - c)
        chips = _other_chips(x, y)
        locals_ = [_later(pltpu.make_async_copy, s, o.at[me], local_sems.at[i])
                   for i, (s, o) in enumerate(zip(srcs, outs))]
        sends, arrived, passed, from_sibling = [], [], [], []
        for j, (px, py) in enumerate(chips):
            for i, (s, o) in enumerate(zip(srcs, outs)):
                rows = s.shape[0] // 2
                sends.append(_later(_remote, s.at[pl.ds(c * rows, rows), :], _half(o, me, c), send_sems.at[i, j],
                                    recv_sems.at[i, j], (px, py, c)))
                got = _half(o, 2 * px + py, c)
                arrived.append(_later(_remote, got, got, send_sems.at[i, j], recv_sems.at[i, j], (px, py, c)))
                passed.append(_later(_remote, got, got, send_sems.at[i, 3 + j], recv_sems.at[i, 3 + j], sibling))
                other = _half(o, 2 * px + py, 1 - c)
                from_sibling.append(_later(_remote, other, other, send_sems.at[i, 3 + j], recv_sems.at[i, 3 + j],
                                           sibling))
        return locals_, sends, arrived, passed, from_sibling

    return Comm(shards, [jax.ShapeDtypeStruct((N_CHIPS,) + s.shape, s.dtype) for s in shards],
                [pltpu.SemaphoreType.DMA((n, 6)), pltpu.SemaphoreType.DMA((n, 6)), pltpu.SemaphoreType.DMA((n,))],
                *_two_level_phases(copies))


HBM = pl.BlockSpec(memory_space=pltpu.HBM)
SEM = pl.BlockSpec(memory_space=pltpu.SEMAPHORE)
N_OTHER = N_CHIPS - 1


def _ici_halves(src_ref, land_ref, sems, per_chip):
    x, y, c = _position()
    me = 2 * x + y
    rows = land_ref.shape[1] // 2
    sends, arrivals = [], []
    for j, (px, py) in enumerate(_other_chips(x, y)):
        piece = src_ref.at[2 * px + py] if per_chip else src_ref.at[pl.ds(c * rows, rows), :]
        sends.append(_later(_remote, piece, _half(land_ref, me, c), sems[j], sems[N_OTHER + j], (px, py, c)))
        got = _half(land_ref, 2 * px + py, c)
        arrivals.append(_later(_remote, got, got, sems[j], sems[N_OTHER + j], (px, py, c)))
    return sends, arrivals


def split_start(src, name, per_chip=False):
    def body(src_ref, land_ref, *rest):
        sems, token = rest[:2 * N_OTHER], rest[-1]
        for cp in _ici_halves(src_ref, land_ref, sems, per_chip)[0]:
            cp().start()
        token[...] = jnp.zeros_like(token)

    rows, cols = (2 * src.shape[1], src.shape[2]) if per_chip else src.shape
    sem = pltpu.SemaphoreType.DMA(())
    land = (N_CHIPS, rows, cols)
    res = pl.pallas_call(
        body, name=name, in_specs=(HBM, HBM),
        out_specs=(SEM,) * (2 * N_OTHER) + (HBM, HBM, pl.BlockSpec(memory_space=pltpu.VMEM)),
        out_shape=(sem,) * (2 * N_OTHER) + (pltpu.HBM(src.shape, src.dtype), pltpu.HBM(land, src.dtype),
                                           jax.ShapeDtypeStruct((8, LANES), F32)),
        input_output_aliases={0: 2 * N_OTHER, 1: 2 * N_OTHER + 1},
        compiler_params=pltpu.CompilerParams(has_side_effects=pltpu.SideEffectType.DATAFLOW_SIDE_EFFECTING),
    )(pltpu.with_memory_space_constraint(src, pltpu.HBM),
      pltpu.with_memory_space_constraint(lax.empty(land, src.dtype), pltpu.HBM))
    return res[:2 * N_OTHER], res[2 * N_OTHER], res[2 * N_OTHER + 1], res[-1]


def split_wait(sems, src, land, after, name, per_chip=False):
    def body(src_ref, land_ref, *rest):
        sends, arrivals = _ici_halves(src_ref, land_ref, rest[:2 * N_OTHER], per_chip)
        for cp in sends:
            cp().wait_send()
        for cp in arrivals:
            cp().wait_recv()

    return pl.pallas_call(
        body, name=name, in_specs=(HBM, HBM) + (SEM,) * (2 * N_OTHER) + (ANY,) * len(after),
        out_specs=(HBM, HBM), out_shape=(pltpu.HBM(src.shape, src.dtype), pltpu.HBM(land.shape, land.dtype)),
        input_output_aliases={0: 0, 1: 1},
        compiler_params=pltpu.CompilerParams(has_side_effects=pltpu.SideEffectType.DATAFLOW_SIDE_EFFECTING),
    )(src, land, *sems, *after)


def handover(land, name, sums=None):
    n = N_OTHER + (sums is not None)

    def body(*refs):
        land_ref, send_sems, recv_sems = refs[0], refs[-2], refs[-1]
        x, y, c = _position()
        me = 2 * x + y
        sibling = (x, y, 1 - c)
        pieces = [(_half(land_ref, 2 * px + py, c), 2 * px + py) for px, py in _other_chips(x, y)]
        if sums is not None:
            pieces.append((refs[1].at[me], me))
        sends = [_remote(piece, _half(land_ref, chip, c), send_sems.at[j], recv_sems.at[j], sibling)
                 for j, (piece, chip) in enumerate(pieces)]
        for cp in sends:
            cp.start()
        for j, (_, chip) in enumerate(pieces):
            other = _half(land_ref, chip, 1 - c)
            _remote(other, other, send_sems.at[j], recv_sems.at[j], sibling).wait_recv()
        for cp in sends:
            cp.wait_send()

    args = (land,) + ((sums,) if sums is not None else ())
    return pl.pallas_call(
        body, name=name, in_specs=[ANY] * len(args), out_specs=ANY,
        out_shape=jax.ShapeDtypeStruct(land.shape, land.dtype), input_output_aliases={0: 0},
        scratch_shapes=[pltpu.SemaphoreType.DMA((n,)), pltpu.SemaphoreType.DMA((n,))],
    )(*args)


def swap_comm(grads):
    n = len(grads)

    def copies(srcs, gots, sems):
        send_sems, recv_sems = sems
        x, y, c = _position()
        out = []
        for i, (s, o) in enumerate(zip(srcs, gots)):
            rows = s.shape[1] // 2
            out.append(_remote(s.at[:, pl.ds((1 - c) * rows, rows), :], o, send_sems.at[i], recv_sems.at[i],
                               (x, y, 1 - c)))
        return out

    def first(srcs, gots, sems):
        for cp in copies(srcs, gots, sems):
            cp.start()

    def last(srcs, gots, sems):
        for cp in copies(srcs, gots, sems):
            cp.wait()

    return Comm(grads, [jax.ShapeDtypeStruct((N_CHIPS, g.shape[1] // 2, g.shape[2]), g.dtype) for g in grads],
                [pltpu.SemaphoreType.DMA((n,)), pltpu.SemaphoreType.DMA((n,))], first, None, last)


def add_halves(name, g, got, core):
    _, half, cols = got.shape
    mine = pl.BlockSpec((None, half, cols), lambda k, c_ref: (k, c_ref[0], 0))
    other = pl.BlockSpec((None, half, cols), lambda k, c_ref: (k, 0, 0))

    def body(c_ref, g_ref, got_ref, o_ref):
        o_ref[...] = (g_ref[...] + got_ref[...]).astype(BF16)

    return pl.pallas_call(
        body, name="add_halves_" + name,
        grid_spec=pltpu.PrefetchScalarGridSpec(num_scalar_prefetch=1, grid=(N_CHIPS,), in_specs=[mine, other],
                                               out_specs=other),
        out_shape=jax.ShapeDtypeStruct(got.shape, BF16),
        compiler_params=_params(("parallel",)),
    )(core, g, got)


def exchange_comm(parts):
    n = len(parts)

    def copies(srcs, outs, sems):
        send_sems, recv_sems, local_sems = sems
        x, y, c = _position()
        me = 2 * x + y
        sibling = (x, y, 1 - c)
        chips = _other_chips(x, y)
        locals_, sends, arrived, passed, from_sibling = [], [], [], [], []
        for i, (s, o) in enumerate(zip(srcs, outs)):
            locals_.append(_later(pltpu.make_async_copy, s.at[me], _half(o, me, c), local_sems.at[i]))
            sends.append(_later(_remote, s.at[me], _half(o, me, c), send_sems.at[i, 3], recv_sems.at[i, 3], sibling))
            other = _half(o, me, 1 - c)
            from_sibling.append(_later(_remote, other, other, send_sems.at[i, 3], recv_sems.at[i, 3], sibling))
        for j, (px, py) in enumerate(chips):
            for i, (s, o) in enumerate(zip(srcs, outs)):
                sends.append(_later(_remote, s.at[2 * px + py], _half(o, me, c), send_sems.at[i, j],
                                    recv_sems.at[i, j], (px, py, c)))
                got = _half(o, 2 * px + py, c)
                arrived.append(_later(_remote, got, got, send_sems.at[i, j], recv_sems.at[i, j], (px, py, c)))
                passed.append(_later(_remote, got, got, send_sems.at[i, 4 + j], recv_sems.at[i, 4 + j], sibling))
                other = _half(o, 2 * px + py, 1 - c)
                from_sibling.append(_later(_remote, other, other, send_sems.at[i, 4 + j], recv_sems.at[i, 4 + j],
                                           sibling))
        return locals_, sends, arrived, passed, from_sibling

    return Comm(parts, [jax.ShapeDtypeStruct((N_CHIPS, 2 * p.shape[1], p.shape[2]), p.dtype) for p in parts],
                [pltpu.SemaphoreType.DMA((n, 7)), pltpu.SemaphoreType.DMA((n, 7)), pltpu.SemaphoreType.DMA((n,))],
                *_two_level_phases(copies))


def small_comm(shares, after=()):
    n = len(shares)

    def copies(srcs, outs, sems):
        send_sems, recv_sems, local_sems = sems
        x, y, c = _position()
        me = 4 * x + 2 * y + c
        flips = [(fx, fy, fc) for fx in (0, 1) for fy in (0, 1) for fc in (0, 1)][1:]
        peers = [(1 - x if fx else x, 1 - y if fy else y, 1 - c if fc else c) for fx, fy, fc in flips]
        locals_, sends, arrived = [], [], []
        for i, (src_ref, out_ref) in enumerate(zip(srcs, outs)):
            locals_.append(_later(pltpu.make_async_copy, src_ref, out_ref.at[me], local_sems.at[i]))
            for j, (px, py, pc) in enumerate(peers):
                sends.append(_later(_remote, src_ref, out_ref.at[me], send_sems.at[i, j], recv_sems.at[i, j],
                                    (px, py, pc)))
                got = out_ref.at[4 * px + 2 * py + pc]
                arrived.append(_later(_remote, got, got, send_sems.at[i, j], recv_sems.at[i, j], (px, py, pc)))
        return locals_, sends, arrived

    def first(*refs):
        locals_, sends, _ = copies(*refs)
        for cp in locals_ + sends:
            cp().start()

    def last(*refs):
        locals_, sends, arrived = copies(*refs)
        for cp in arrived:
            cp().wait_recv()
        for cp in sends:
            cp().wait_send()
        for cp in locals_:
            cp().wait()

    return Comm(list(shares) + list(after), [jax.ShapeDtypeStruct((N_DEV,) + s.shape, s.dtype) for s in shares],
                [pltpu.SemaphoreType.DMA((n, 7)), pltpu.SemaphoreType.DMA((n, 7)), pltpu.SemaphoreType.DMA((n,))],
                first, None, last)


def _adam_fn(w, g, m, v):
    m = ADAM_B1 * m + (1.0 - ADAM_B1) * g
    v = ADAM_B2 * v + (1.0 - ADAM_B2) * jnp.square(g)
    m_hat = m / (1.0 - ADAM_B1 ** ADAM_STEP)
    v_hat = v / (1.0 - ADAM_B2 ** ADAM_STEP)
    return -ADAM_LR * (m_hat / (jnp.sqrt(v_hat) + ADAM_EPS) + ADAM_WD * w), m, v


def adam_big(name, parts, w, m, v):
    rows, cols = w.shape
    tm = _pick(rows, 384, 16)

    def fn(p0, p1, p2, p3, wv, mv, vv):
        g = ((p0.astype(F32) + p1.astype(F32)) + p2.astype(F32)) + p3.astype(F32)
        return (g,) + _adam_fn(wv, g, mv, vv)

    return rowwise(fn, [parts, w, m, v], [(cols, F32)] * 4, "adam_" + name, tm=tm, rows=rows)


def adam_small(name, gathered, w, m, v):
    def body(g_ref, w_ref, m_ref, v_ref, go_ref, d_ref, mo_ref, vo_ref):
        g = g_ref[0]
        for k in range(1, N_DEV):
            g = g + g_ref[k]
        go_ref[...] = g
        d_ref[...], mo_ref[...], vo_ref[...] = _adam_fn(w_ref[...], g, m_ref[...], v_ref[...])

    return pl.pallas_call(body, name=name, out_shape=[jax.ShapeDtypeStruct(w.shape, F32)] * 4,
                          compiler_params=_params())(gathered, w, m, v)


def _ssm_2d(name, t):
    t = t[0] if t.ndim > 2 else t
    if name in ("ssm_b_re", "ssm_b_im"):
        return t.transpose(0, 2, 1).reshape(SSM_W, 64)
    if name in ("ssm_c_re", "ssm_c_im"):
        return t.reshape(SSM_W, 64)
    return t.T if name == "ssm_d" else t


def _ssm_back(name, t):
    if name in ("ssm_b_re", "ssm_b_im"):
        return t.reshape(32, 16, 64).transpose(0, 2, 1)[None]
    if name in ("ssm_c_re", "ssm_c_im"):
        return t.reshape(1, 32, 16, 64)
    if name == "ssm_d":
        return t.T[None]
    return t if name == "ssm_log_dt" else t[None]


def adam_ssm(shares, w, m, v):
    n = len(w)

    def body(*refs):
        ins, outs = refs[:4 * n], refs[4 * n:]
        for i in range(n):
            g_ref, w_ref, m_ref, v_ref = (ins[k * n + i] for k in range(4))
            g = g_ref[0]
            for k in range(1, N_DEV):
                g = g + g_ref[k]
            outs[4 * i][...] = g
            outs[4 * i + 1][...], outs[4 * i + 2][...], outs[4 * i + 3][...] = _adam_fn(w_ref[...], g, m_ref[...],
                                                                                      v_ref[...])

    out_shape = [jax.ShapeDtypeStruct(t.shape, F32) for t in w for _ in range(4)]
    res = pl.pallas_call(body, name="adam_ssm", out_shape=out_shape, compiler_params=_params())(*shares, *w, *m, *v)
    return [res[4 * i:4 * i + 4] for i in range(n)]


def _pack_small(names, vals, rows, last=None):
    flat = [vals[n].reshape(-1) for n in names]
    if last is not None:
        flat.append(last.reshape(-1))
    flat = jnp.concatenate(flat)
    return jnp.pad(flat, (0, rows * LANES - flat.shape[0])).reshape(rows, LANES)


def _unpack_small(names, pack, shapes):
    flat, out, off = pack.reshape(-1), {}, 0
    for n in names:
        size = math.prod(shapes[n])
        out[n] = flat[off:off + size].reshape(shapes[n])
        off += size
    return out, flat[off]


def _to_slots(name, g, shard_shape):
    rows, cols = shard_shape
    if name in ROW_SHARDED:
        return g.reshape(N_CHIPS, rows, cols)
    return g.reshape(rows, N_CHIPS, cols).transpose(1, 0, 2)


def _from_slots(name, s):
    _, rows, cols = s.shape
    if name in ROW_SHARDED:
        return s.reshape(N_CHIPS * rows, cols)
    return s.transpose(1, 0, 2).reshape(rows, N_CHIPS * cols)


def kernel(x, norm_mix_g, w_in, ssm_a_re, ssm_a_im, ssm_log_dt, ssm_b_re, ssm_b_im, ssm_c_re, ssm_c_im, ssm_d, w_glu, w_attn_out, w_out, norm_ffn_g, w_ffn_gate, w_ffn_up, w_ffn_down, norm_final_g, loss_target, m_norm_mix_g, m_w_in, m_ssm_a_re, m_ssm_a_im, m_ssm_log_dt, m_ssm_b_re, m_ssm_b_im, m_ssm_c_re, m_ssm_c_im, m_ssm_d, m_w_glu, m_w_attn_out, m_w_out, m_norm_ffn_g, m_w_ffn_gate, m_w_ffn_up, m_w_ffn_down, m_norm_final_g, v_norm_mix_g, v_w_in, v_ssm_a_re, v_ssm_a_im, v_ssm_log_dt, v_ssm_b_re, v_ssm_b_im, v_ssm_c_re, v_ssm_c_im, v_ssm_d, v_w_glu, v_w_attn_out, v_w_out, v_norm_ffn_g, v_w_ffn_gate, v_w_ffn_up, v_w_ffn_down, v_norm_final_g):
    given = dict(locals())
    def local(name, prefix=""):
        t = given[prefix + name][0]
        return t.T if name in TRANSPOSED else t

    shard = {n: local(n) for n in BIG}
    shapes = {n: given[n].shape for n in WEIGHTS}

    small = {n: given[n] for n in SMALL}
    small_2d = dict(small)
    for n in ("ssm_a_re", "ssm_a_im", "ssm_b_re", "ssm_b_im", "ssm_c_re", "ssm_c_im", "ssm_d"):
        small_2d[n] = small[n][0]
    small_2d["norm_final_g"] = norm_final_g.reshape(1, D_MODEL)

    core = lax.axis_index("c").astype(jnp.int32).reshape(1)
    loss, grad_x, parts, ssm_shares, gs_norm, w_in_reduce = local_step(
        x.reshape(TOKENS, D_MODEL), loss_target.reshape(TOKENS, D_MODEL),
        {n: shard[n] for n in BIG}, small_2d, core)

    small_out = [{} for _ in range(4)]
    ssm_in = [[_ssm_2d(n, given[p + n]) for n in SSM_SMALL] for p in ("", "m_", "v_")]
    for n, res in zip(SSM_SMALL, adam_ssm(ssm_shares, *ssm_in)):
        for kind, t in enumerate(res):
            small_out[kind][n] = _ssm_back(n, t)

    big_out, updated = {}, {}
    for n in BIG[1:] + BIG[:1]:
        if n == "w_in":
            sems, chip_sum, land = w_in_reduce
            behind = [updated[k][1] for k in BIG[1:]] + [gs_norm["norm_mix_g"]]
            chip_sum, land = split_wait(sems, chip_sum, land, behind, "w_in_reduce_wait", per_chip=True)
            land = handover(land, "w_in_reduce_handover", sums=chip_sum)
            me = 2 * lax.axis_index("x") + lax.axis_index("y")
            parts[n] = lax.dynamic_update_slice(land, lax.dynamic_slice_in_dim(chip_sum, me, 1, 0),
                                                (me, lax.axis_index("c") * chip_sum.shape[1], 0))
        updated[n] = adam_big(n, parts[n], shard[n], local(n, "m_"), local(n, "v_"))
        big_out[n] = [(t.T if n in TRANSPOSED else t)[None] for t in updated[n]]

    (norm_shares,) = run_comm(small_comm([_pack_small(NORM_SMALL, gs_norm, NORM_ROWS, last=loss)], after=[land]),
                              "gather_norm_grads")
    packs = [_pack_small(NORM_SMALL, {n: given[p + n] for n in NORM_SMALL}, NORM_ROWS) for p in ("", "m_", "v_")]
    for kind, t in enumerate(adam_small("adam_norm_gains", norm_shares, *packs)):
        vals, after = _unpack_small(NORM_SMALL, t, shapes)
        small_out[kind].update(vals)
        if kind == 0:
            total_loss = after

    outs = [total_loss, grad_x.reshape(LOCAL_BATCH, SEQ, D_MODEL)]
    for kind in range(4):
        for n in WEIGHTS:
            outs.append(big_out[n][kind] if n in BIG else small_out[kind][n])
    return tuple(outs)
```

```python
import functools
import math

import jax
import jax.numpy as jnp
import numpy as np
from jax import lax
from jax.experimental import pallas as pl
from jax.experimental.pallas import tpu as pltpu

F32 = jnp.float32
BF16 = jnp.bfloat16
MESH = pl.DeviceIdType.MESH

D_MODEL = 1024
SEQ = 2048
LOCAL_BATCH = 2
TOKENS = LOCAL_BATCH * SEQ
HEAD_DIM = 64
HEADS_PER_GROUP = 4
GROUP_W = HEADS_PER_GROUP * HEAD_DIM
N_GROUPS = 3
DILATIONS = (1, 4, 16)
ATTN_BLOCK = 128
ROPE_DIM = 16
ROPE_THETA = 500000.0
QKV_W = 3 * N_GROUPS * GROUP_W
SSM_W = 512
SSM_STATE_W = 2048
SSM_LANE_BLOCKS = 4
GATE_W = 2 * D_MODEL
D_FF = 2816
RMS_EPS = 1e-6
NEG_INF = -1e30
ADAM_LR, ADAM_B1, ADAM_B2, ADAM_EPS, ADAM_WD, ADAM_STEP = 0.001, 0.9, 0.999, 1e-08, 0.01, 10
N_CHIPS = 4
N_DEV = 8

VMEM_LIMIT = 56 * 1024 * 1024
LANES = 128


def _params(sem=None):
    return pltpu.CompilerParams(dimension_semantics=sem, vmem_limit_bytes=VMEM_LIMIT)


def _pick(n, cap, align=LANES):
    best = None
    for d in range(align, min(n, cap) + 1, align):
        if n % d == 0:
            best = d
    return n if best is None or n <= cap else best


_DIMS = {"nn": (((1,), (0,)), ((), ())), "nt": (((1,), (1,)), ((), ())), "tn": (((0,), (0,)), ((), ()))}


def _dot(a, b, mode):
    return lax.dot_general(a, b, _DIMS[mode], preferred_element_type=F32)


def matmul(a, b, mode, out_dtype, name, add=None, comm=None):
    if mode == "nn":
        (m, k), n = a.shape, b.shape[1]
    elif mode == "nt":
        (m, k), n = a.shape, b.shape[0]
    else:
        (k, m), n = a.shape, b.shape[1]
    tn = _pick(n, 1408 if mode != "tn" else 512)
    tk = _pick(k, 2816) if mode != "tn" else k
    tm = _pick(m, 1408)
    out_bytes = jnp.dtype(out_dtype).itemsize

    def need(tm_):
        return 2 * 2 * (tm_ * tk + tk * tn) + tm_ * tn * (4 + 2 * out_bytes + (8 if add is not None else 0))

    while need(tm) > 40 * 1024 * 1024 and tm % 256 == 0:
        tm //= 2
    nk = k // tk
    a_spec = {"nn": pl.BlockSpec((tm, tk), lambda i, j, kk: (i, kk)),
              "nt": pl.BlockSpec((tm, tk), lambda i, j, kk: (i, kk)),
              "tn": pl.BlockSpec((tk, tm), lambda i, j, kk: (kk, i))}[mode]
    b_spec = {"nn": pl.BlockSpec((tk, tn), lambda i, j, kk: (kk, j)),
              "nt": pl.BlockSpec((tn, tk), lambda i, j, kk: (j, kk)),
              "tn": pl.BlockSpec((tk, tn), lambda i, j, kk: (kk, j))}[mode]
    o_spec = pl.BlockSpec((tm, tn), lambda i, j, kk: (i, j))

    def body(a_ref, b_ref, *rest):
        if add is not None:
            add_ref, o_ref, acc_ref = rest
        else:
            o_ref, acc_ref = rest
        part = _dot(a_ref[...], b_ref[...], mode)
        if nk == 1:
            res = part if add is None else part + add_ref[...]
            o_ref[...] = res.astype(out_dtype)
            return
        kk = pl.program_id(2)

        @pl.when(kk == 0)
        def _():
            acc_ref[...] = part

        @pl.when(kk > 0)
        def _():
            acc_ref[...] += part

        @pl.when(kk == nk - 1)
        def _():
            res = acc_ref[...] if add is None else acc_ref[...] + add_ref[...]
            o_ref[...] = res.astype(out_dtype)

    in_specs = [a_spec, b_spec] + ([o_spec] if add is not None else [])
    args = (a, b) + ((add,) if add is not None else ())
    res = hosted_call(
        body, comm, name, (m // tm, n // tn, nk), in_specs, [o_spec], [jax.ShapeDtypeStruct((m, n), out_dtype)],
        [pltpu.VMEM((tm, tn) if nk > 1 else (8, LANES), F32)], args, ("parallel", "parallel", "arbitrary"))
    return res[0] if comm is None else res


def matmul_rows(a, b, name, fn, extra, outs, accs=(), add=None, comm=None, tm=512):
    a_list, b_list = (list(a), list(b)) if isinstance(a, (list, tuple)) else ([a], [b])
    m, n = a_list[0].shape[0], b_list[0].shape[1]
    n_mm = len(a_list)
    n_fixed = 2 * n_mm + (add is not None)
    row_spec = lambda cols: pl.BlockSpec((tm, cols), lambda i: (i, 0))
    in_specs = [row_spec(t.shape[1]) for t in a_list] + [pl.BlockSpec(t.shape, lambda i: (0, 0)) for t in b_list]
    in_specs += [row_spec(n)] if add is not None else []
    in_specs += [pl.BlockSpec(e.shape, lambda i: (0, 0)) if e.shape[0] == 1 else row_spec(e.shape[1]) for e in extra]
    out_specs = [row_spec(c) for c, _ in outs] + [pl.BlockSpec((1, c), lambda i: (0, 0)) for c in accs]
    out_shape = [jax.ShapeDtypeStruct((m, c), dt) for c, dt in outs] + [jax.ShapeDtypeStruct((1, c), F32) for c in accs]

    def body(*refs):
        rows = _dot(refs[0][...], refs[n_mm][...], "nn")
        for i in range(1, n_mm):
            rows = rows + _dot(refs[i][...], refs[n_mm + i][...], "nn")
        if add is not None:
            rows = rows + refs[2 * n_mm][...]
        n_in = n_fixed + len(extra)
        res = fn(rows, *[r[...] for r in refs[n_fixed:n_in]])
        for r, v in zip(refs[n_in:n_in + len(outs)], res[:len(outs)]):
            r[...] = v.astype(r.dtype)
        first = pl.program_id(0) == 0
        for r, v in zip(refs[n_in + len(outs):], res[len(outs):]):
            @pl.when(first)
            def _(r=r, v=v):
                r[...] = v

            @pl.when(jnp.logical_not(first))
            def _(r=r, v=v):
                r[...] += v

    args = tuple(a_list) + tuple(b_list) + ((add,) if add is not None else ()) + tuple(extra)
    return hosted_call(body, comm, name, (m // tm,), in_specs, out_specs, out_shape, [], args, ("arbitrary",))


def _merge_specs(tm):
    half = lambda blk: pl.BlockSpec((tm, D_MODEL), functools.partial(lambda i, blk_: (i, blk_), blk_=blk))
    return [half(0), half(1), pl.BlockSpec((tm, GROUP_W), lambda i: (i, 0)),
            pl.BlockSpec((GROUP_W, D_MODEL), lambda i: (0, 0)), pl.BlockSpec((tm, SSM_W), lambda i: (i, 0)),
            pl.BlockSpec((SSM_W, GATE_W), lambda i: (0, 0))]


def _merge_operands(g0, g1, at, wa, yg, wg):
    z = _dot(yg[...], wg[...], "nn")
    return (g0[...].astype(F32), g1[...].astype(F32), _dot(at[...], wa[...], "nn"), z[:, :D_MODEL], z[:, D_MODEL:])


def merge_out_proj(gl, attn_b, w_attn_out, yg, w_glu, w_out, x, g_ffn):
    tm = 512

    def body(g0, g1, at, wa, yg_ref, wg, w_ref, x_ref, g_ref, m_ref, x1_ref, h2_ref):
        merged = _merge_fn(*_merge_operands(g0, g1, at, wa, yg_ref, wg)).astype(BF16)
        m_ref[...] = merged
        x1 = _dot(merged, w_ref[...], "nn") + x_ref[...]
        x1_ref[...] = x1
        h2_ref[...] = _rms(x1, g_ref[...]).astype(BF16)

    rows = pl.BlockSpec((tm, D_MODEL), lambda i: (i, 0))
    whole = pl.BlockSpec((D_MODEL, D_MODEL), lambda i: (0, 0))
    gain = pl.BlockSpec((1, D_MODEL), lambda i: (0, 0))
    tok = lambda dt: jax.ShapeDtypeStruct((TOKENS, D_MODEL), dt)
    return pl.pallas_call(
        body, name="merge_out_proj", grid=(TOKENS // tm,), in_specs=_merge_specs(tm) + [whole, rows, gain],
        out_specs=[rows] * 3, out_shape=[tok(BF16), tok(F32), tok(BF16)], compiler_params=_params(("parallel",)),
    )(gl, gl, attn_b, w_attn_out, yg, w_glu, w_out, x, g_ffn)


def merge_bwd(dx1_b, w_out, gl, attn_b, w_attn_out, yg, w_glu, comm=None):
    tm = 512

    def body(dx_ref, w_ref, g0, g1, at, wa, yg_ref, wg, dgl_ref, dad_ref, dz_ref, dat_ref, dyg_ref):
        dm = _dot(dx_ref[...], w_ref[...], "nt")
        _, vjp = jax.vjp(_merge_fn, *_merge_operands(g0, g1, at, wa, yg_ref, wg))
        dg0, dg1, dad, dza, dzb = vjp(dm)
        dat_ref[...] = _dot(dad.astype(BF16), wa[...], "nt")
        dgl_ref[:, :D_MODEL] = dg0.astype(BF16)
        dgl_ref[:, D_MODEL:] = dg1.astype(BF16)
        dad_ref[...] = dad.astype(BF16)
        dz_ref[:, :D_MODEL] = dza.astype(BF16)
        dz_ref[:, D_MODEL:] = dzb.astype(BF16)
        dyg_ref[...] = _dot(dz_ref[...], wg[...], "nt")

    rows = pl.BlockSpec((tm, D_MODEL), lambda i: (i, 0))
    wide = pl.BlockSpec((tm, GATE_W), lambda i: (i, 0))
    whole = pl.BlockSpec((D_MODEL, D_MODEL), lambda i: (0, 0))
    return hosted_call(
        body, comm, "merge_bwd", (TOKENS // tm,), [rows, whole] + _merge_specs(tm),
        [wide, rows, wide, pl.BlockSpec((tm, GROUP_W), lambda i: (i, 0)), pl.BlockSpec((tm, SSM_W), lambda i: (i, 0))],
        [jax.ShapeDtypeStruct((TOKENS, GATE_W), BF16), jax.ShapeDtypeStruct((TOKENS, D_MODEL), BF16),
         jax.ShapeDtypeStruct((TOKENS, GATE_W), BF16), jax.ShapeDtypeStruct((TOKENS, GROUP_W), F32),
         jax.ShapeDtypeStruct((TOKENS, SSM_W), F32)], [],
        (dx1_b, w_out, gl, gl, attn_b, w_attn_out, yg, w_glu), ("arbitrary",))


FFN_TM, FFN_TN = 512, 1408


def ffn_in(h2, wg_t, wu_t, comm=None):
    def body(h_ref, wg_ref, wu_ref, a_ref, b_ref, act_ref):
        hv = h_ref[...]
        a, b = _dot(hv, wg_ref[...], "nt"), _dot(hv, wu_ref[...], "nt")
        a_ref[...] = a.astype(BF16)
        b_ref[...] = b.astype(BF16)
        act_ref[...] = _swiglu_fn(a, b).astype(BF16)

    rows = pl.BlockSpec((FFN_TM, D_MODEL), lambda i, j: (i, 0))
    wts = pl.BlockSpec((FFN_TN, D_MODEL), lambda i, j: (j, 0))
    out = pl.BlockSpec((FFN_TM, FFN_TN), lambda i, j: (i, j))
    return hosted_call(body, comm, "ffn_in", (TOKENS // FFN_TM, D_FF // FFN_TN), [rows, wts, wts], [out] * 3,
                       [jax.ShapeDtypeStruct((TOKENS, D_FF), BF16)] * 3, [], (h2, wg_t, wu_t),
                       ("parallel", "parallel"))


def ffn_in_bwd(dx2_b, wd, a, b):
    def body(dx_ref, wd_ref, a_ref, b_ref, da_ref, db_ref):
        dact = _dot(dx_ref[...], wd_ref[...], "nt")
        av, bv = a_ref[...].astype(F32), b_ref[...].astype(F32)
        sig = jax.nn.sigmoid(av)
        act = av * sig
        da_ref[...] = (dact * bv * (sig * (1.0 + av - act))).astype(BF16)
        db_ref[...] = (dact * act).astype(BF16)

    rows = pl.BlockSpec((FFN_TM, D_MODEL), lambda i, j: (i, 0))
    wts = pl.BlockSpec((FFN_TN, D_MODEL), lambda i, j: (j, 0))
    out = pl.BlockSpec((FFN_TM, FFN_TN), lambda i, j: (i, j))
    return pl.pallas_call(
        body, name="ffn_in_bwd", grid=(TOKENS // FFN_TM, D_FF // FFN_TN), in_specs=[rows, wts, out, out],
        out_specs=[out] * 2, out_shape=[jax.ShapeDtypeStruct((TOKENS, D_FF), BF16)] * 2,
        compiler_params=_params(("parallel", "parallel")),
    )(dx2_b, wd, a, b)


def mix_in_bwd(grads, weights, partial, x, g, skip, comm=None):
    n = len(grads)
    tm = 512

    def body(*refs):
        a_refs, b_refs = refs[:n], refs[n:2 * n]
        part_ref, x_ref, g_ref, skip_ref, gx_ref, dg_ref = refs[2 * n:]
        dh = part_ref[...]
        for a_ref, b_ref in zip(a_refs, b_refs):
            dh = dh + _dot(a_ref[...], b_ref[...], "nn")
        _, vjp = jax.vjp(_rms, x_ref[...], g_ref[...])
        dx, dg = vjp(dh)
        gx_ref[...] = dx + skip_ref[...]
        first = pl.program_id(0) == 0

        @pl.when(first)
        def _():
            dg_ref[...] = dg

        @pl.when(jnp.logical_not(first))
        def _():
            dg_ref[...] += dg

    rows = pl.BlockSpec((tm, D_MODEL), lambda i: (i, 0))
    gain = pl.BlockSpec((1, D_MODEL), lambda i: (0, 0))
    in_specs = [pl.BlockSpec((tm, a.shape[1]), lambda i: (i, 0)) for a in grads]
    in_specs += [pl.BlockSpec(b.shape, lambda i: (0, 0)) for b in weights]
    return hosted_call(
        body, comm, "mix_in_bwd", (TOKENS // tm,), in_specs + [rows, rows, gain, rows], [rows, gain],
        [jax.ShapeDtypeStruct((TOKENS, D_MODEL), F32), jax.ShapeDtypeStruct((1, D_MODEL), F32)], [],
        (*grads, *weights, partial, x, g, skip), ("arbitrary",))


def rowwise(fn, ins, outs, name, accs=(), tm=256, rows=TOKENS, comm=None):
    in_specs, args = [], []
    for item in ins:
        arr, width, blk = item if isinstance(item, tuple) else (item, None, 0)
        if arr.ndim == 3:
            for k in range(arr.shape[0]):
                in_specs.append(pl.BlockSpec((None, tm, arr.shape[2]), functools.partial(lambda i, k_: (k_, i, 0), k_=k)))
                args.append(arr)
            continue
        if arr.shape[0] == 1:
            in_specs.append(pl.BlockSpec(arr.shape, lambda i: (0, 0)))
        elif width is None:
            in_specs.append(pl.BlockSpec((tm, arr.shape[1]), lambda i: (i, 0)))
        else:
            in_specs.append(pl.BlockSpec((tm, width), functools.partial(lambda i, blk_: (i, blk_), blk_=blk)))
        args.append(arr)
    out_specs = [pl.BlockSpec((tm, c), lambda i: (i, 0)) for c, _ in outs]
    out_specs += [pl.BlockSpec((1, c), lambda i: (0, 0)) for c in accs]
    out_shape = [jax.ShapeDtypeStruct((rows, c), dt) for c, dt in outs]
    out_shape += [jax.ShapeDtypeStruct((1, c), F32) for c in accs]
    n_in, n_out = len(args), len(outs)
    c_ins, c_outs, c_sems = _comm_operands(comm)

    def body(*refs):
        refs, c_refs = _comm_refs(comm, refs, n_in, n_out + len(accs))
        step = pl.program_id(0)
        _comm_begin(comm, c_refs, step, rows // tm)
        res = fn(*[r[...] for r in refs[:n_in]])
        for r, v in zip(refs[n_in:n_in + n_out], res[:n_out]):
            r[...] = v.astype(r.dtype)
        first = step == 0
        for r, v in zip(refs[n_in + n_out:], res[n_out:]):
            @pl.when(first)
            def _(r=r, v=v):
                r[...] = v

            @pl.when(jnp.logical_not(first))
            def _(r=r, v=v):
                r[...] += v
        _comm_end(comm, c_refs, step, rows // tm)

    return pl.pallas_call(
        body, name=name, grid=(rows // tm,), in_specs=in_specs + [ANY] * len(c_ins),
        out_specs=out_specs + [ANY] * len(c_outs), out_shape=out_shape + c_outs, scratch_shapes=c_sems,
        compiler_params=_params(("arbitrary",)),
    )(*args, *c_ins)


def first_norm(x, g, others, comm=None):
    tm, n = 256, len(others)

    def body(x_ref, g_ref, *rest):
        srcs, h_ref, dsts = rest[:n], rest[n], rest[n + 1:]
        h_ref[...] = _rms(x_ref[...], g_ref[...]).astype(BF16)
        for k, (s, d) in enumerate(zip(srcs, dsts)):
            @pl.when(pl.program_id(0) == k)
            def _(s=s, d=d):
                d[...] = s[...].astype(BF16)

    rows = pl.BlockSpec((tm, D_MODEL), lambda i: (i, 0))
    whole = [pl.BlockSpec(a.shape, lambda i: (0, 0)) for a in others]
    return hosted_call(
        body, comm, "norm_mix", (TOKENS // tm,), [rows, pl.BlockSpec((1, D_MODEL), lambda i: (0, 0))] + whole,
        [rows] + whole, [jax.ShapeDtypeStruct((TOKENS, D_MODEL), BF16)]
        + [jax.ShapeDtypeStruct(a.shape, BF16) for a in others], [], (x, g, *others), ("arbitrary",))


def _rms(x, g):
    return x * lax.rsqrt(jnp.mean(x * x, axis=-1, keepdims=True) + RMS_EPS) * g


def _colsum(v):
    return jnp.sum(v, axis=0, keepdims=True)


PAIR_W = 2 * HEAD_DIM
N_PAIRS = HEADS_PER_GROUP // 2


def _qkv_order(w_t, back=False):
    dims = (N_PAIRS, N_GROUPS, 3) if back else (3, N_GROUPS, N_PAIRS)
    return w_t.reshape(dims + (PAIR_W, w_t.shape[1])).transpose(2, 1, 0, 3, 4).reshape(QKV_W, w_t.shape[1])


def _rope_tables():
    half = ROPE_DIM // 2
    inv = np.power(np.float32(ROPE_THETA), -np.arange(half, dtype=np.float32) * np.float32(2.0 / ROPE_DIM))
    ang = (np.arange(SEQ, dtype=np.float32)[:, None] * inv[None, :]).astype(np.float32)
    cos, sin = np.cos(ang), np.sin(ang)
    zeros = np.zeros((SEQ, HEAD_DIM - ROPE_DIM), np.float32)
    zh = np.zeros((SEQ, half), np.float32)
    c = np.concatenate([cos, cos, zeros + 1.0], axis=1)
    sa = np.concatenate([-sin, zh, zeros], axis=1)
    sb = np.concatenate([zh, sin, zeros], axis=1)
    return [jnp.asarray(np.tile(t, (1, 2)), F32) for t in (c, sa, sb)]


def _rope_fwd(x, c, sa, sb):
    return x * c + pltpu.roll(x, PAIR_W - 8, 1) * sa + pltpu.roll(x, 8, 1) * sb


def _rope_bwd(dy, c, sa, sb):
    return dy * c + pltpu.roll(dy * sb, PAIR_W - 8, 1) + pltpu.roll(dy * sa, 8, 1)


def _band_masks():
    row = lax.broadcasted_iota(jnp.int32, (ATTN_BLOCK, ATTN_BLOCK), 0)
    col = lax.broadcasted_iota(jnp.int32, (ATTN_BLOCK, ATTN_BLOCK), 1)
    return col <= row, col >= row


def _stack_rows(t):
    return jnp.concatenate([t, t], axis=0)


def _stack_heads(t, first_head):
    return jnp.concatenate([jnp.where(first_head, t, 0), jnp.where(first_head, 0, t)], axis=0)


def _per_head(fn):
    return jnp.concatenate([fn(slice(h * HEAD_DIM, (h + 1) * HEAD_DIM)) for h in range(2)], axis=1)


def _slab_spec(kind):
    return pl.BlockSpec((None, SEQ, PAIR_W), lambda b, p, g: (b, 0, p * 3 * N_GROUPS + g * 3 + kind))


_TABLE_SPEC = pl.BlockSpec((SEQ, PAIR_W), lambda b, p, g: (0, 0))
_PAIR_SPEC = pl.BlockSpec((None, SEQ, PAIR_W), lambda b, p, g: (b, 0, p))


def _block_rows(dil, r, n):
    return pl.ds(n * (ATTN_BLOCK * dil) + r, ATTN_BLOCK, stride=dil)


def proj_qkv(h, w_qkv_t, tables, comm=None):
    tm = 1024
    pair_w = QKV_W // N_PAIRS
    scale = HEAD_DIM ** -0.5

    def body(h_ref, w_ref, c_ref, sa_ref, sb_ref, o_ref):
        rows = _dot(h_ref[...], w_ref[...], "nt")
        c, sa, sb = c_ref[...], sa_ref[...], sb_ref[...]
        for blk in range(pair_w // PAIR_W):
            cols = slice(blk * PAIR_W, (blk + 1) * PAIR_W)
            x = rows[:, cols]
            if blk % 3 == 0:
                x = _rope_fwd(x, c, sa, sb) * scale
            elif blk % 3 == 1:
                x = _rope_fwd(x, c, sa, sb)
            o_ref[:, cols] = x

    table = pl.BlockSpec((tm, PAIR_W), lambda i, j, : (i % (SEQ // tm), 0))
    res = hosted_call(
        body, comm, "proj_qkv", (TOKENS // tm, N_PAIRS),
        [pl.BlockSpec((tm, D_MODEL), lambda i, j: (i, 0)), pl.BlockSpec((pair_w, D_MODEL), lambda i, j: (j, 0)),
         table, table, table],
        [pl.BlockSpec((tm, pair_w), lambda i, j: (i, j))], [jax.ShapeDtypeStruct((TOKENS, QKV_W), F32)], [],
        (h, w_qkv_t, *tables), ("parallel", "parallel"))
    return res[0] if comm is None else res


def attn_fwd(qkv, comm=None):
    def body(qs, ks, v_ref, attn_b_ref, attn_ref, lse_ref, o0, o1, o2, l0, l1, l2):
        g = pl.program_id(2)
        cur_mask, prev_mask = _band_masks()
        first_head = lax.broadcasted_iota(jnp.int32, (ATTN_BLOCK, PAIR_W), 1) < HEAD_DIM

        def run(dil, o_slab, l_slab):
            nb = SEQ // dil // ATTN_BLOCK

            def block(idx, carry):
                r, n = lax.div(idx, nb), lax.rem(idx, nb)
                cur, prev = _block_rows(dil, r, n), _block_rows(dil, r, jnp.maximum(n - 1, 0))
                q = qs[cur, :].astype(BF16)
                kc, kp = ks[cur, :].astype(BF16), ks[prev, :].astype(BF16)
                vc, vp = v_ref[cur, :].astype(BF16), v_ref[prev, :].astype(BF16)
                q2 = _stack_heads(q, first_head)
                mask = _stack_rows(jnp.concatenate([jnp.logical_and(prev_mask, n > 0), cur_mask], axis=1))
                s2 = jnp.where(mask, _dot(q2, jnp.concatenate([kp, kc], axis=0), "nt"), NEG_INF)
                m = jnp.max(s2, axis=-1, keepdims=True)
                vcat, two = jnp.concatenate([vp, vc], axis=0), _stack_rows(first_head)
                vext = jnp.concatenate([jnp.where(two, vcat, 1), jnp.where(two, 1, vcat)], axis=1)
                r2 = _dot(jnp.exp(s2 - m).astype(BF16), vext, "nn")
                r0, r1 = r2[:ATTN_BLOCK, :PAIR_W], r2[ATTN_BLOCK:, PAIR_W:]
                num = jnp.where(first_head, r0, r1)
                den = pltpu.roll(jnp.where(first_head, r1, r0), HEAD_DIM, 1)
                o_slab[cur, :] = num / den
                l_slab[cur, :] = jnp.where(first_head, m[:ATTN_BLOCK], m[ATTN_BLOCK:]) + jnp.log(den)
                return carry

            lax.fori_loop(0, SEQ // ATTN_BLOCK, block, 0, unroll=4)

        for gi, (o_slab, l_slab) in enumerate(((o0, l0), (o1, l1), (o2, l2))):
            @pl.when(g == gi)
            def _(gi=gi, o_slab=o_slab, l_slab=l_slab):
                run(DILATIONS[gi], o_slab, l_slab)

        @pl.when(g == N_GROUPS - 1)
        def _():
            a, b, cc = l0[...], l1[...], l2[...]
            m = jnp.maximum(jnp.maximum(a, b), cc)
            e0, e1, e2 = jnp.exp(a - m), jnp.exp(b - m), jnp.exp(cc - m)
            tot = e0 + e1 + e2
            attn = (e0 * o0[...] + e1 * o1[...] + e2 * o2[...]) / tot
            attn_ref[...] = attn
            attn_b_ref[...] = attn.astype(BF16)
            lse_ref[...] = m + jnp.log(tot)

    shape = (LOCAL_BATCH, SEQ, GROUP_W)
    slab = pltpu.VMEM((SEQ, PAIR_W), F32)
    return hosted_call(
        body, comm, "attn_fwd", (LOCAL_BATCH, N_PAIRS, N_GROUPS),
        [_slab_spec(0), _slab_spec(1), _slab_spec(2)], [_PAIR_SPEC] * 3,
        [jax.ShapeDtypeStruct(shape, BF16), jax.ShapeDtypeStruct(shape, F32), jax.ShapeDtypeStruct(shape, F32)],
        [slab] * 6, (qkv, qkv, qkv), ("parallel", "parallel", "arbitrary"))


def attn_bwd(qkv, tables, dattn, attn, lse, comm=None):
    scale = HEAD_DIM ** -0.5

    def body(qs, ks, v_ref, c_ref, sa_ref, sb_ref, do_ref, out_ref, lse_ref, dqkv_ref, dl, dq_s, dk_s, dv_s):
        g = pl.program_id(2)
        c, sa, sb = c_ref[...], sa_ref[...], sb_ref[...]

        @pl.when(g == 0)
        def _():
            prod = do_ref[...] * out_ref[...]
            dl[...] = _per_head(
                lambda sl: jnp.broadcast_to(jnp.sum(prod[:, sl], axis=-1, keepdims=True), (SEQ, HEAD_DIM)))

        cur_mask, prev_mask = _band_masks()
        first_head = lax.broadcasted_iota(jnp.int32, (ATTN_BLOCK, PAIR_W), 1) < HEAD_DIM

        def run(dil):
            nb = SEQ // dil // ATTN_BLOCK

            def block(idx, carry):
                r, n = lax.div(idx, nb), lax.rem(idx, nb)
                cur = _block_rows(dil, r, n)
                prev = _block_rows(dil, r, jnp.maximum(n - 1, 0))
                nxt = _block_rows(dil, r, jnp.minimum(n + 1, nb - 1))
                q0, q1 = qs[cur, :].astype(BF16), qs[nxt, :].astype(BF16)
                kp, kc = ks[prev, :].astype(BF16), ks[cur, :].astype(BF16)
                vp, vc = v_ref[prev, :].astype(BF16), v_ref[cur, :].astype(BF16)
                do0, do1 = do_ref[cur, :].astype(BF16), do_ref[nxt, :].astype(BF16)
                lse0, lse1, dl0, dl1 = lse_ref[cur, :], lse_ref[nxt, :], dl[cur, :], dl[nxt, :]
                has_prev = jnp.logical_and(prev_mask, n > 0)
                has_next = jnp.logical_and(prev_mask, n < nb - 1)

                def per_row(t):
                    return jnp.concatenate([t[:, 0:1], t[:, HEAD_DIM:HEAD_DIM + 1]], axis=0)

                q20, q21 = _stack_heads(q0, first_head), _stack_heads(q1, first_head)
                do20, do21 = _stack_heads(do0, first_head), _stack_heads(do1, first_head)
                kcat, vcat = jnp.concatenate([kp, kc], axis=0), jnp.concatenate([vp, vc], axis=0)
                mask0 = _stack_rows(jnp.concatenate([has_prev, cur_mask], axis=1))
                p0 = jnp.where(mask0, jnp.exp(_dot(q20, kcat, "nt") - per_row(lse0)), 0.0)
                ds0 = (p0 * (_dot(do20, vcat, "nt") - per_row(dl0))).astype(BF16)
                p1 = jnp.where(_stack_rows(has_next), jnp.exp(_dot(q21, kc, "nt") - per_row(lse1)), 0.0)
                ds1 = (p1 * (_dot(do21, vc, "nt") - per_row(dl1))).astype(BF16)
                dq2 = _dot(ds0, kcat, "nn")
                dq_s[cur, :] = jnp.where(first_head, dq2[:ATTN_BLOCK], dq2[ATTN_BLOCK:])
                ds_cur = jnp.concatenate([ds0[:, ATTN_BLOCK:], ds1], axis=0)
                p_cur = jnp.concatenate([p0[:, ATTN_BLOCK:], p1], axis=0).astype(BF16)
                dk_s[cur, :] = _dot(ds_cur, jnp.concatenate([q20, q21], axis=0), "tn")
                dv_s[cur, :] = _dot(p_cur, jnp.concatenate([do20, do21], axis=0), "tn")
                return carry

            lax.fori_loop(0, SEQ // ATTN_BLOCK, block, 0, unroll=2)

        for gi in range(N_GROUPS):
            @pl.when(g == gi)
            def _(gi=gi):
                run(DILATIONS[gi])

        dqkv_ref[:, 0:PAIR_W] = _rope_bwd(dq_s[...] * scale, c, sa, sb).astype(BF16)
        dqkv_ref[:, PAIR_W:2 * PAIR_W] = _rope_bwd(dk_s[...], c, sa, sb).astype(BF16)
        dqkv_ref[:, 2 * PAIR_W:] = dv_s[...].astype(BF16)

    slab = pltpu.VMEM((SEQ, PAIR_W), F32)
    return hosted_call(
        body, comm, "attn_bwd", (LOCAL_BATCH, N_PAIRS, N_GROUPS),
        [_slab_spec(0), _slab_spec(1), _slab_spec(2), _TABLE_SPEC, _TABLE_SPEC, _TABLE_SPEC,
         _PAIR_SPEC, _PAIR_SPEC, _PAIR_SPEC],
        [pl.BlockSpec((None, SEQ, 3 * PAIR_W), lambda b, p, g: (b, 0, p * N_GROUPS + g))],
        [jax.ShapeDtypeStruct((LOCAL_BATCH, SEQ, QKV_W), BF16)],
        [slab] * 4, (qkv, qkv, qkv, *tables, dattn, attn, lse), ("parallel", "parallel", "arbitrary"))


def _discretize(lr, li, log_dt, br, bi):
    dt = jnp.exp(log_dt)
    mag = jnp.exp(lr * dt)
    ab_re, ab_im = mag * jnp.cos(li * dt), mag * jnp.sin(li * dt)
    den = lr * lr + li * li
    nr, ni = ab_re - 1.0, ab_im
    f_re = (nr * lr + ni * li) / den
    f_im = (ni * lr - nr * li) / den
    return ab_re, ab_im, f_re[None] * br - f_im[None] * bi, f_re[None] * bi + f_im[None] * br


def ssm_prep(lr, li, log_dt, br, bi):
    def body(lr_ref, li_ref, dt_ref, br_ref, bi_ref, *outs):
        for o, v in zip(outs, _discretize(lr_ref[...], li_ref[...], dt_ref[...], br_ref[...], bi_ref[...])):
            o[...] = v
    shapes = [lr, li, br, bi]
    return pl.pallas_call(body, name="ssm_prep",
                          out_shape=[jax.ShapeDtypeStruct(s.shape, F32) for s in shapes])(lr, li, log_dt, br, bi)


def ssm_prep_bwd(lr, li, log_dt, br, bi, g_ab_re, g_ab_im, g_bb_re, g_bb_im):
    def body(lr_ref, li_ref, dt_ref, br_ref, bi_ref, g0, g1, g2, g3, *outs):
        _, vjp = jax.vjp(_discretize, lr_ref[...], li_ref[...], dt_ref[...], br_ref[...], bi_ref[...])
        for o, v in zip(outs, vjp((g0[...], g1[...], g2[...], g3[...]))):
            o[...] = v
    shapes = [lr, li, log_dt, br, bi]
    return pl.pallas_call(body, name="ssm_prep_bwd",
                          out_shape=[jax.ShapeDtypeStruct(s.shape, F32) for s in shapes])(
        lr, li, log_dt, br, bi, g_ab_re, g_ab_im, g_bb_re, g_bb_im)


def _block_diag(t):
    per = SSM_STATE_W // SSM_LANE_BLOCKS // 64
    g = t.transpose(1, 0, 2).reshape(SSM_LANE_BLOCKS, per, 16, 64)
    eye = jnp.eye(per, dtype=t.dtype)
    return jnp.einsum("jgcn,gh->jgchn", g, eye).reshape(SSM_LANE_BLOCKS, per * 16, per * 64)


def _block_diag_t(m):
    per = SSM_STATE_W // SSM_LANE_BLOCKS // 64
    m5 = m.reshape(SSM_LANE_BLOCKS, per, 16, per, 64)
    d = jnp.einsum("jgchn,gh->jgcn", m5, jnp.eye(per, dtype=m.dtype))
    return d.reshape(SSM_LANE_BLOCKS * per, 16, 64).transpose(1, 0, 2)


def _cmul(ar, ai, br, bi):
    return ar * br - ai * bi, ar * bi + ai * br


def _power_tables(ar, ai, reverse):
    width = ar.shape[1]
    row = lax.broadcasted_iota(jnp.int32, (8, width), 0)
    pows = [(ar, ai)]
    for _ in range(7):
        pows.append(_cmul(pows[-1][0], pows[-1][1], ar, ai))
    steps = []
    for k in (1, 2, 4):
        keep = (row >= k) if not reverse else (row < 8 - k)
        steps.append((jnp.where(keep, pows[k - 1][0], 0.0), jnp.where(keep, pows[k - 1][1], 0.0)))
    cr = jnp.zeros((8, width), F32)
    ci = jnp.zeros((8, width), F32)
    for i in range(8):
        pr, pi = pows[i] if not reverse else pows[7 - i]
        cr = jnp.where(row == i, pr, cr)
        ci = jnp.where(row == i, pi, ci)
    return steps, (cr, ci)


SCAN_CHUNK = 2048
STATE_BLOCK = SSM_STATE_W // SSM_LANE_BLOCKS
CHAN_BLOCK = SSM_W // SSM_LANE_BLOCKS


def ssm_fwd(u, ab_re, ab_im, bb_re, bb_im, cb_re, cb_im, d_skip, comm=None):
    nt = SEQ // SCAN_CHUNK
    chan = pl.BlockSpec((None, SCAN_CHUNK, CHAN_BLOCK), lambda b, j, t: (b, t, j))
    state = pl.BlockSpec((None, SCAN_CHUNK, STATE_BLOCK), lambda b, j, t: (b, t, j))
    mat = pl.BlockSpec((None, CHAN_BLOCK, STATE_BLOCK), lambda b, j, t: (j, 0, 0))
    lane = pl.BlockSpec((1, STATE_BLOCK), lambda b, j, t: (0, j))
    dsp = pl.BlockSpec((1, CHAN_BLOCK), lambda b, j, t: (0, j))

    def body(u_ref, ar_ref, ai_ref, bbr_ref, bbi_ref, cbr_ref, cbi_ref, d_ref, y_ref, yg_ref, xr_ref, xi_ref,
             car_r, car_i):
        @pl.when(pl.program_id(2) == 0)
        def _():
            car_r[...] = jnp.zeros_like(car_r)
            car_i[...] = jnp.zeros_like(car_i)

        steps, (pr, pi) = _power_tables(ar_ref[...], ai_ref[...], reverse=False)
        uf = u_ref[...]
        ub = uf.astype(BF16)
        xr_ref[...] = _dot(ub, bbr_ref[...], "nn")
        xi_ref[...] = _dot(ub, bbi_ref[...], "nn")

        def tile(i, carry):
            cr, ci = carry
            sl = pl.ds(pl.multiple_of(i * 8, 8), 8)
            br, bi = xr_ref[sl, :], xi_ref[sl, :]
            for k, (sr, si) in zip((1, 2, 4), steps):
                tr, ti = _cmul(sr, si, pltpu.roll(br, k, 0), pltpu.roll(bi, k, 0))
                br, bi = br + tr, bi + ti
            tr, ti = _cmul(pr, pi, cr, ci)
            br, bi = br + tr, bi + ti
            xr_ref[sl, :] = br
            xi_ref[sl, :] = bi
            return br[7:8, :], bi[7:8, :]

        cr, ci = lax.fori_loop(0, SCAN_CHUNK // 8, tile, (car_r[0:1, :], car_i[0:1, :]), unroll=4)
        car_r[0:1, :] = cr
        car_i[0:1, :] = ci
        y = (_dot(xr_ref[...].astype(BF16), cbr_ref[...], "nt") - _dot(xi_ref[...].astype(BF16), cbi_ref[...], "nt")
             + d_ref[...] * uf)
        y_ref[...] = y
        yg_ref[...] = jax.nn.gelu(y).astype(BF16)

    return hosted_call(
        body, comm, "ssm_fwd", (LOCAL_BATCH, SSM_LANE_BLOCKS, nt),
        [chan, lane, lane, mat, mat, mat, mat, dsp], [chan, chan, state, state],
        [jax.ShapeDtypeStruct((LOCAL_BATCH, SEQ, SSM_W), F32), jax.ShapeDtypeStruct((LOCAL_BATCH, SEQ, SSM_W), BF16),
         jax.ShapeDtypeStruct((LOCAL_BATCH, SEQ, SSM_STATE_W), F32),
         jax.ShapeDtypeStruct((LOCAL_BATCH, SEQ, SSM_STATE_W), F32)],
        [pltpu.VMEM((8, STATE_BLOCK), F32), pltpu.VMEM((8, STATE_BLOCK), F32)],
        (u, ab_re, ab_im, bb_re, bb_im, cb_re, cb_im, d_skip), ("parallel", "parallel", "arbitrary"))


def ssm_bwd(dyg, y, u, xr, xi, ab_re, ab_im, bb_re, bb_im, cb_re, cb_im, d_skip, comm=None):
    nt = SEQ // SCAN_CHUNK
    ntile = SCAN_CHUNK // 8

    def rev(t):
        return nt - 1 - t

    chan = pl.BlockSpec((None, SCAN_CHUNK, CHAN_BLOCK), lambda j, b, t: (b, rev(t), j))
    state = pl.BlockSpec((None, SCAN_CHUNK, STATE_BLOCK), lambda j, b, t: (b, rev(t), j))
    before = pl.BlockSpec((None, 8, STATE_BLOCK), lambda j, b, t: (b, jnp.maximum(rev(t) * ntile - 1, 0), j))
    mat = pl.BlockSpec((None, CHAN_BLOCK, STATE_BLOCK), lambda j, b, t: (j, 0, 0))
    lane = pl.BlockSpec((1, STATE_BLOCK), lambda j, b, t: (0, j))
    lane8 = pl.BlockSpec((8, STATE_BLOCK), lambda j, b, t: (0, j))
    dsp = pl.BlockSpec((1, CHAN_BLOCK), lambda j, b, t: (0, j))

    def body(dyg_ref, y_ref, u_ref, xr_ref, xi_ref, xrb_ref, xib_ref, ar_ref, ai_ref, bbr_ref, bbi_ref, cbr_ref,
             cbi_ref, d_ref, du_ref, dcbr_ref, dcbi_ref, dbbr_ref, dbbi_ref, dd_ref, dar_ref, dai_ref,
             lam_r, lam_i, car_r, car_i):
        b, t = pl.program_id(1), pl.program_id(2)
        first = jnp.logical_and(b == 0, t == 0)

        @pl.when(t == 0)
        def _():
            car_r[...] = jnp.zeros_like(car_r)
            car_i[...] = jnp.zeros_like(car_i)

        @pl.when(first)
        def _():
            for r in (dcbr_ref, dcbi_ref, dbbr_ref, dbbi_ref, dd_ref, dar_ref, dai_ref):
                r[...] = jnp.zeros_like(r)

        steps, (pr, pi) = _power_tables(ar_ref[...], -ai_ref[...], reverse=True)
        uf = u_ref[...]
        _, gelu_vjp = jax.vjp(jax.nn.gelu, y_ref[...])
        dy = gelu_vjp(dyg_ref[...])[0]
        dyb = dy.astype(BF16)
        dd_ref[...] += _colsum(dy * uf)
        lam_r[...] = _dot(dyb, cbr_ref[...], "nn")
        lam_i[...] = -_dot(dyb, cbi_ref[...], "nn")
        dcbr_ref[...] += _dot(dyb, xr_ref[...].astype(BF16), "tn")
        dcbi_ref[...] -= _dot(dyb, xi_ref[...].astype(BF16), "tn")
        row0 = lax.broadcasted_iota(jnp.int32, (8, STATE_BLOCK), 0) == 0
        has_before = rev(t) > 0
        xrb = jnp.where(has_before, xrb_ref[...], 0.0)
        xib = jnp.where(has_before, xib_ref[...], 0.0)

        def tile(s, carry):
            cr, ci, acc_r, acc_i = carry
            i = ntile - 1 - s
            sl = pl.ds(pl.multiple_of(i * 8, 8), 8)
            gr, gi = lam_r[sl, :], lam_i[sl, :]
            for k, (sr, si) in zip((1, 2, 4), steps):
                tr, ti = _cmul(sr, si, pltpu.roll(gr, 8 - k, 0), pltpu.roll(gi, 8 - k, 0))
                gr, gi = gr + tr, gi + ti
            tr, ti = _cmul(pr, pi, cr, ci)
            gr, gi = gr + tr, gi + ti
            lam_r[sl, :] = gr
            lam_i[sl, :] = gi
            sp = pl.ds(pl.multiple_of(jnp.maximum(i - 1, 0) * 8, 8), 8)
            pvr = jnp.where(i > 0, xr_ref[sp, :], xrb)
            pvi = jnp.where(i > 0, xi_ref[sp, :], xib)
            xsr = jnp.where(row0, pltpu.roll(pvr, 1, 0), pltpu.roll(xr_ref[sl, :], 1, 0))
            xsi = jnp.where(row0, pltpu.roll(pvi, 1, 0), pltpu.roll(xi_ref[sl, :], 1, 0))
            acc_r = acc_r + xsr * gr + xsi * gi
            acc_i = acc_i + xsr * gi - xsi * gr
            return gr[0:1, :], gi[0:1, :], acc_r, acc_i

        zero = jnp.zeros((8, STATE_BLOCK), F32)
        cr, ci, acc_r, acc_i = lax.fori_loop(0, ntile, tile, (car_r[0:1, :], car_i[0:1, :], zero, zero), unroll=2)
        car_r[0:1, :] = cr
        car_i[0:1, :] = ci
        dar_ref[...] += acc_r
        dai_ref[...] += acc_i
        lrb, lib = lam_r[...].astype(BF16), lam_i[...].astype(BF16)
        du = _dot(lrb, bbr_ref[...], "nt") + _dot(lib, bbi_ref[...], "nt") + d_ref[...] * dy
        du_ref[...] = du.astype(BF16)
        ub = uf.astype(BF16)
        dbbr_ref[...] += _dot(ub, lrb, "tn")
        dbbi_ref[...] += _dot(ub, lib, "tn")

    mat_shape = jax.ShapeDtypeStruct((SSM_LANE_BLOCKS, CHAN_BLOCK, STATE_BLOCK), F32)
    return hosted_call(
        body, comm, "ssm_bwd", (SSM_LANE_BLOCKS, LOCAL_BATCH, nt),
        [chan, chan, chan, state, state, before, before, lane, lane, mat, mat, mat, mat, dsp],
        [chan, mat, mat, mat, mat, dsp, lane8, lane8],
        [jax.ShapeDtypeStruct((LOCAL_BATCH, SEQ, SSM_W), BF16), mat_shape, mat_shape, mat_shape, mat_shape,
         jax.ShapeDtypeStruct((1, SSM_W), F32), jax.ShapeDtypeStruct((8, SSM_STATE_W), F32),
         jax.ShapeDtypeStruct((8, SSM_STATE_W), F32)],
        [pltpu.VMEM((SCAN_CHUNK, STATE_BLOCK), F32), pltpu.VMEM((SCAN_CHUNK, STATE_BLOCK), F32),
         pltpu.VMEM((8, STATE_BLOCK), F32), pltpu.VMEM((8, STATE_BLOCK), F32)],
        (dyg, y, u, xr, xi, xr, xi, ab_re, ab_im, bb_re, bb_im, cb_re, cb_im, d_skip),
        ("parallel", "arbitrary", "arbitrary"))


def _merge_fn(g0, g1, attn_d, za, zb):
    return jax.nn.sigmoid(g0) * attn_d + jax.nn.sigmoid(g1) * (za * jax.nn.sigmoid(zb))


def _swiglu_fn(a, b):
    return jax.nn.silu(a) * b


def _own_slot(slots, shard):
    me = 2 * lax.axis_index("x") + lax.axis_index("y")
    mine = lax.broadcasted_iota(jnp.int32, (N_CHIPS, 1, 1), 0) == me
    return jnp.where(mine, shard[None], slots)


def _reduce_start(names, gw, shard_shapes):
    return swap_comm([_to_slots(n, gw[n], shard_shapes[n]) for n in names])


def _reduce_chip(names, slots, got, core):
    return exchange_comm([add_halves(n, g, r, core) for n, g, r in zip(names, slots, got)])


def local_step(x, target, shards, small, core):
    g_mix, g_ffn, g_final = small["norm_mix_g"], small["norm_ffn_g"], small["norm_final_g"]
    tables = _rope_tables()
    seqs = lambda t: t.reshape(LOCAL_BATCH, SEQ, t.shape[-1])
    toks = lambda t: t.reshape(TOKENS, t.shape[-1])
    shard_shapes = {n: s.shape for n, s in shards.items()}
    w = {}

    def gather(names):
        return gather_comm([shards[n] for n in names])

    def arrived(names, slots, own=None):
        for n, s in zip(names, slots):
            w[n] = _from_slots(n, s if own is None else _own_slot(s, own))

    later = [n for n in BIG if n != "w_in"]
    sems, w_in_shard, land, token = split_start(shards["w_in"].astype(BF16), "w_in_gather_start")
    zero = token[0, 0]
    h, *rest = first_norm(x, g_mix + zero, [shards[n] for n in later])
    shards = dict(shards)
    shards.update(zip(later, rest))
    br_t = small["ssm_b_re"].transpose(2, 0, 1)
    bi_t = small["ssm_b_im"].transpose(2, 0, 1)
    log_dt = small["ssm_log_dt"].reshape(32, 1)
    ab_re, ab_im, bb_re_t, bb_im_t = ssm_prep(small["ssm_a_re"] + zero, small["ssm_a_im"], log_dt, br_t, bi_t)
    ab = [ab_re.reshape(1, SSM_STATE_W), ab_im.reshape(1, SSM_STATE_W)]
    bb = [_block_diag(bb_re_t).astype(BF16), _block_diag(bb_im_t).astype(BF16)]
    cb = [_block_diag((small["ssm_c_re"] + zero).transpose(1, 0, 2)).astype(BF16),
          _block_diag((small["ssm_c_im"] + zero).transpose(1, 0, 2)).astype(BF16)]
    d_skip = small["ssm_d"].reshape(1, SSM_W)
    w_in_shard, land = split_wait(sems, w_in_shard, land, [h] + bb + cb, "w_in_gather_wait")
    arrived(["w_in"], [handover(land, "w_in_handover")], own=w_in_shard)
    w_qkv, w_u, w_gate = _qkv_order(w["w_in"][:QKV_W]), w["w_in"][QKV_W:QKV_W + SSM_W], w["w_in"][QKV_W + SSM_W:]
    qkv, *slots = proj_qkv(h, w_qkv, tables, comm=gather(["w_attn_out", "w_glu"]))
    arrived(["w_attn_out", "w_glu"], slots)
    qkv = seqs(qkv)
    u = seqs(matmul(h, w_u, "nt", F32, "proj_u"))
    gl, *slots = matmul(h, w_gate, "nt", BF16, "proj_gate", comm=gather(["w_out"]))
    arrived(["w_out"], slots)
    attn_b, attn, lse, *slots = attn_fwd(qkv, comm=gather(["w_ffn_gate"]))
    arrived(["w_ffn_gate"], slots)
    attn_b = toks(attn_b)
    y, yg, xr, xi, *slots = ssm_fwd(u, *ab, *bb, *cb, d_skip, comm=gather(["w_ffn_up"]))
    arrived(["w_ffn_up"], slots)
    yg2 = toks(yg)
    merged, x1, h2 = merge_out_proj(gl, attn_b, w["w_attn_out"], yg2, w["w_glu"], w["w_out"], x, g_ffn)
    a, b, act, *slots = ffn_in(h2, w["w_ffn_gate"], w["w_ffn_up"], comm=gather(["w_ffn_down"]))
    arrived(["w_ffn_down"], slots)

    def final_fn(xv, g, tgt):
        yv, vjp = jax.vjp(_rms, xv, g)
        err = yv - tgt
        dx, dg = vjp(err * (1.0 / D_MODEL))
        loss = 0.5 * jnp.sum(jnp.mean(err * err, axis=-1, keepdims=True), axis=0, keepdims=True)
        return dx, dx, dg, jnp.broadcast_to(loss, (1, LANES))

    dx2, dx2_b, dg_final, loss = matmul_rows(act, w["w_ffn_down"], "ffn_down_loss", final_fn, [g_final, target],
                                             [(D_MODEL, F32), (D_MODEL, BF16)], accs=(D_MODEL, LANES), add=x1)
    gw, parts = {}, {}
    gw["w_ffn_down"] = matmul(act, dx2_b, "tn", F32, "d_ffn_down")
    da_b, db_b = ffn_in_bwd(dx2_b, w["w_ffn_down"], a, b)
    gw["w_ffn_gate"] = matmul(da_b, h2, "tn", F32, "d_ffn_gate")
    gw["w_ffn_up"] = matmul(db_b, h2, "tn", F32, "d_ffn_up")
    ffn = ["w_ffn_down", "w_ffn_gate", "w_ffn_up"]
    swap = _reduce_start(ffn, gw, shard_shapes)

    def norm_bwd(dh, xv, g, skip):
        _, vjp = jax.vjp(_rms, xv, g)
        dx, dg = vjp(dh)
        dx = dx + skip
        return dx, dx, dg

    dx1, dx1_b, dg_ffn, *got = matmul_rows(
        [da_b, db_b], [w["w_ffn_gate"], w["w_ffn_up"]], "d_h2_norm", norm_bwd, [x1, g_ffn, dx2],
        [(D_MODEL, F32), (D_MODEL, BF16)], accs=(D_MODEL,), comm=swap, tm=256)
    ffn_exchange = [_reduce_chip(ffn[:2], swap.ins[:2], got[:2], core)]
    ffn_up_exchange = _reduce_chip(ffn[2:], swap.ins[2:], got[2:], core)
    gw["w_out"] = matmul(merged, dx1_b, "tn", F32, "d_out")
    dgl_b, dattn_d_b, dz_b, dattn, dyg, parts["w_ffn_up"] = merge_bwd(
        dx1_b, w["w_out"], gl, attn_b, w["w_attn_out"], yg2, w["w_glu"], comm=ffn_up_exchange)
    dattn, dyg = seqs(dattn), seqs(dyg)
    gw["w_attn_out"] = matmul(attn_b, dattn_d_b, "tn", F32, "d_attn_out")
    gw["w_glu"] = matmul(yg2, dz_b, "tn", F32, "d_glu")
    mixer = ["w_out", "w_attn_out", "w_glu"]
    swap = _reduce_start(mixer, gw, shard_shapes)
    du_b, dcb_re, dcb_im, dbb_re, dbb_im, dd, da_re8, da_im8, *rest = ssm_bwd(
        dyg, y, u, xr, xi, *ab, *bb, *cb, d_skip, comm=join_comms(ffn_exchange + [swap]))
    for n, p in zip(ffn[:2], rest[:2]):
        parts[n] = p
    mixer_exchange = _reduce_chip(mixer, swap.ins, rest[2:], core)
    du_b = toks(du_b)
    g_ab_re = jnp.sum(da_re8, axis=0).reshape(32, 64)
    g_ab_im = jnp.sum(da_im8, axis=0).reshape(32, 64)
    d_lr, d_li, d_ldt, d_br_t, d_bi_t = ssm_prep_bwd(
        small["ssm_a_re"], small["ssm_a_im"], log_dt, br_t, bi_t,
        g_ab_re, g_ab_im, _block_diag_t(dbb_re), _block_diag_t(dbb_im))
    as_gcn = lambda t: t.transpose(1, 0, 2).reshape(SSM_W, 64)
    gs = {
        "ssm_a_re": d_lr, "ssm_a_im": d_li, "ssm_log_dt": d_ldt.reshape(1, 32),
        "ssm_b_re": as_gcn(d_br_t), "ssm_b_im": as_gcn(d_bi_t),
        "ssm_c_re": as_gcn(_block_diag_t(dcb_re)), "ssm_c_im": as_gcn(_block_diag_t(dcb_im)),
        "ssm_d": dd.reshape(32, 16).T,
    }
    ssm_gather = small_comm([gs[n] for n in SSM_SMALL])
    dqkv_b, *rest = attn_bwd(qkv, tables, dattn, attn, lse, comm=join_comms([mixer_exchange, ssm_gather]))
    for n, p in zip(mixer, rest):
        parts[n] = p
    ssm_shares = rest[len(mixer):]
    dqkv_b = toks(dqkv_b)
    d_qkv = matmul(dqkv_b, h, "tn", F32, "d_w_qkv")
    d_u = matmul(du_b, h, "tn", F32, "d_w_u")
    d_gate = matmul(dgl_b, h, "tn", F32, "d_w_gate")
    gw["w_in"] = jnp.concatenate([_qkv_order(d_qkv, back=True), d_u, d_gate], axis=0)
    swap = _reduce_start(["w_in"], gw, shard_shapes)
    dh, *got = matmul(dqkv_b, w_qkv, "nn", F32, "d_h_qkv", comm=swap)
    chip_sum = add_halves("w_in", swap.ins[0], got[0], core)
    sems, chip_sum, land, token = split_start(chip_sum, "w_in_reduce_start", per_chip=True)
    grad_x, dg_mix = mix_in_bwd([du_b, dgl_b], [w_u, w_gate], dh, x, g_mix + token[0, 0], dx1)
    gs_norm = {"norm_mix_g": dg_mix, "norm_ffn_g": dg_ffn, "norm_final_g": dg_final}
    return loss, grad_x, parts, ssm_shares, gs_norm, (sems, chip_sum, land)


ANY = pl.BlockSpec(memory_space=pl.ANY)
BIG = ("w_in", "w_glu", "w_attn_out", "w_out", "w_ffn_gate", "w_ffn_up", "w_ffn_down")
TRANSPOSED = ("w_in", "w_ffn_gate", "w_ffn_up")
ROW_SHARDED = TRANSPOSED + ("w_out", "w_ffn_down")
SMALL = ("norm_mix_g", "ssm_a_re", "ssm_a_im", "ssm_log_dt", "ssm_b_re", "ssm_b_im", "ssm_c_re", "ssm_c_im",
         "ssm_d", "norm_ffn_g", "norm_final_g")
WEIGHTS = ("norm_mix_g", "w_in", "ssm_a_re", "ssm_a_im", "ssm_log_dt", "ssm_b_re", "ssm_b_im", "ssm_c_re",
           "ssm_c_im", "ssm_d", "w_glu", "w_attn_out", "w_out", "norm_ffn_g", "w_ffn_gate", "w_ffn_up",
           "w_ffn_down", "norm_final_g")
SSM_SMALL = SMALL[1:9]
NORM_SMALL = (SMALL[0],) + SMALL[9:]
NORM_ROWS = 32
N_BIG = len(BIG)


def _position():
    return lax.axis_index("x"), lax.axis_index("y"), lax.axis_index("c")


def _other_chips(x, y):
    return [(1 - x, y), (x, 1 - y), (1 - x, 1 - y)]


def _remote(src, dst, send_sem, recv_sem, device):
    return pltpu.make_async_remote_copy(src_ref=src, dst_ref=dst, send_sem=send_sem, recv_sem=recv_sem,
                                        device_id=device, device_id_type=MESH)


_later = functools.partial


def _two_level_phases(copies):
    def first(*refs):
        locals_, sends, _, _, _ = copies(*refs)
        for cp in locals_ + sends:
            cp().start()

    def mid(*refs):
        _, _, arrived, passed, _ = copies(*refs)
        for got, cp in zip(arrived, passed):
            got().wait_recv()
            cp().start()

    def last(*refs):
        locals_, sends, _, passed, from_sibling = copies(*refs)
        for cp in from_sibling:
            cp().wait_recv()
        for cp in sends + passed:
            cp().wait_send()
        for cp in locals_:
            cp().wait()

    return first, mid, last


def _half(ref, chip, which):
    rows = ref.shape[1] // 2
    return ref.at[chip, pl.ds(which * rows, rows), :]


class Comm:
    def __init__(self, ins, out_shapes, sems, first, mid, last):
        self.ins, self.out_shapes, self.sems = list(ins), list(out_shapes), list(sems)
        self.first, self.mid, self.last = first, mid, last


def join_comms(comms):
    def cut(refs_by_kind):
        offs, parts = [0, 0, 0], []
        for cm in comms:
            sizes = (len(cm.ins), len(cm.out_shapes), len(cm.sems))
            parts.append(tuple(refs_by_kind[k][offs[k]:offs[k] + sizes[k]] for k in range(3)))
            offs = [o + s for o, s in zip(offs, sizes)]
        return parts

    def phase(which):
        def run(ins, outs, sems):
            for cm, part in zip(comms, cut((ins, outs, sems))):
                fn = getattr(cm, which)
                if fn is not None:
                    fn(*part)
        return run

    return Comm(sum((cm.ins for cm in comms), []), sum((cm.out_shapes for cm in comms), []),
                sum((cm.sems for cm in comms), []), phase("first"), phase("mid"), phase("last"))


def _comm_operands(comm):
    if comm is None:
        return [], [], []
    return comm.ins, comm.out_shapes, comm.sems


def _comm_begin(comm, refs, step, n_steps):
    if comm is None:
        return
    pl.when(step == 0)(lambda: comm.first(*refs))
    if comm.mid is not None:
        pl.when(step == (n_steps * 3) // 4)(lambda: comm.mid(*refs))


def _comm_end(comm, refs, step, n_steps):
    if comm is not None:
        pl.when(step == n_steps - 1)(lambda: comm.last(*refs))


def _comm_refs(comm, refs, n_in, n_out):
    if comm is None:
        return list(refs), None
    ci, co, cs = len(comm.ins), len(comm.out_shapes), len(comm.sems)
    o0 = n_in + ci
    s0 = o0 + n_out + co
    host = list(refs[:n_in]) + list(refs[o0:o0 + n_out]) + list(refs[s0:len(refs) - cs])
    return host, (list(refs[n_in:o0]), list(refs[o0 + n_out:s0]), list(refs[len(refs) - cs:]))


def run_comm(comm, name):
    n_in, n_out = len(comm.ins), len(comm.out_shapes)

    def body(*refs):
        parts = (list(refs[:n_in]), list(refs[n_in:n_in + n_out]), list(refs[n_in + n_out:]))
        comm.first(*parts)
        if comm.mid is not None:
            comm.mid(*parts)
        comm.last(*parts)

    return pl.pallas_call(body, name=name, in_specs=[ANY] * n_in, out_specs=[ANY] * n_out,
                          out_shape=comm.out_shapes, scratch_shapes=comm.sems)(*comm.ins)


def hosted_call(work, comm, name, grid, in_specs, out_specs, out_shape, scratch_shapes, args, semantics):
    c_ins, c_outs, c_sems = _comm_operands(comm)
    n_steps = math.prod(grid)

    def body(*refs):
        host, c_refs = _comm_refs(comm, refs, len(in_specs), len(out_specs))
        step = 0
        for axis, size in enumerate(grid):
            step = step * size + pl.program_id(axis)
        _comm_begin(comm, c_refs, step, n_steps)
        work(*host)
        _comm_end(comm, c_refs, step, n_steps)

    return pl.pallas_call(
        body, name=name, grid=grid, in_specs=list(in_specs) + [ANY] * len(c_ins),
        out_specs=list(out_specs) + [ANY] * len(c_outs), out_shape=list(out_shape) + c_outs,
        scratch_shapes=list(scratch_shapes) + c_sems,
        compiler_params=_params(semantics if comm is None else ("arbitrary",) * len(grid)),
    )(*args, *c_ins)


def gather_comm(shards):
    n = len(shards)

    def copies(srcs, outs, sems):
        send_sems, recv_sems, local_sems = sems
        x, y, c = _position()
        me = 2 * x + y
        sibling = (x, y, 1 - c)
        chips = _other_chips(x, y)
        locals_ = [_later(pltpu.make_async_copy, s, o.at[me], local_sems.at[i])
                   for i, (s, o) in enumerate(zip(srcs, outs))]
        sends, arrived, passed, from_sibling = [], [], [], []
        for j, (px, py) in enumerate(chips):
            for i, (s, o) in enumerate(zip(srcs, outs)):
                rows = s.shape[0] // 2
                sends.append(_later(_remote, s.at[pl.ds(c * rows, rows), :], _half(o, me, c), send_sems.at[i, j],
                                    recv_sems.at[i, j], (px, py, c)))
                got = _half(o, 2 * px + py, c)
                arrived.append(_later(_remote, got, got, send_sems.at[i, j], recv_sems.at[i, j], (px, py, c)))
                passed.append(_later(_remote, got, got, send_sems.at[i, 3 + j], recv_sems.at[i, 3 + j], sibling))
                other = _half(o, 2 * px + py, 1 - c)
                from_sibling.append(_later(_remote, other, other, send_sems.at[i, 3 + j], recv_sems.at[i, 3 + j],
                                           sibling))
        return locals_, sends, arrived, passed, from_sibling

    return Comm(shards, [jax.ShapeDtypeStruct((N_CHIPS,) + s.shape, s.dtype) for s in shards],
                [pltpu.SemaphoreType.DMA((n, 6)), pltpu.SemaphoreType.DMA((n, 6)), pltpu.SemaphoreType.DMA((n,))],
                *_two_level_phases(copies))


HBM = pl.BlockSpec(memory_space=pltpu.HBM)
SEM = pl.BlockSpec(memory_space=pltpu.SEMAPHORE)
N_OTHER = N_CHIPS - 1


def _ici_halves(src_ref, land_ref, sems, per_chip):
    x, y, c = _position()
    me = 2 * x + y
    rows = land_ref.shape[1] // 2
    sends, arrivals = [], []
    for j, (px, py) in enumerate(_other_chips(x, y)):
        piece = src_ref.at[2 * px + py] if per_chip else src_ref.at[pl.ds(c * rows, rows), :]
        sends.append(_later(_remote, piece, _half(land_ref, me, c), sems[j], sems[N_OTHER + j], (px, py, c)))
        got = _half(land_ref, 2 * px + py, c)
        arrivals.append(_later(_remote, got, got, sems[j], sems[N_OTHER + j], (px, py, c)))
    return sends, arrivals


def split_start(src, name, per_chip=False):
    def body(src_ref, land_ref, *rest):
        sems, token = rest[:2 * N_OTHER], rest[-1]
        for cp in _ici_halves(src_ref, land_ref, sems, per_chip)[0]:
            cp().start()
        token[...] = jnp.zeros_like(token)

    rows, cols = (2 * src.shape[1], src.shape[2]) if per_chip else src.shape
    sem = pltpu.SemaphoreType.DMA(())
    land = (N_CHIPS, rows, cols)
    res = pl.pallas_call(
        body, name=name, in_specs=(HBM, HBM),
        out_specs=(SEM,) * (2 * N_OTHER) + (HBM, HBM, pl.BlockSpec(memory_space=pltpu.VMEM)),
        out_shape=(sem,) * (2 * N_OTHER) + (pltpu.HBM(src.shape, src.dtype), pltpu.HBM(land, src.dtype),
                                           jax.ShapeDtypeStruct((8, LANES), F32)),
        input_output_aliases={0: 2 * N_OTHER, 1: 2 * N_OTHER + 1},
        compiler_params=pltpu.CompilerParams(has_side_effects=pltpu.SideEffectType.DATAFLOW_SIDE_EFFECTING),
    )(pltpu.with_memory_space_constraint(src, pltpu.HBM),
      pltpu.with_memory_space_constraint(lax.empty(land, src.dtype), pltpu.HBM))
    return res[:2 * N_OTHER], res[2 * N_OTHER], res[2 * N_OTHER + 1], res[-1]


def split_wait(sems, src, land, after, name, per_chip=False):
    def body(src_ref, land_ref, *rest):
        sends, arrivals = _ici_halves(src_ref, land_ref, rest[:2 * N_OTHER], per_chip)
        for cp in sends:
            cp().wait_send()
        for cp in arrivals:
            cp().wait_recv()

    return pl.pallas_call(
        body, name=name, in_specs=(HBM, HBM) + (SEM,) * (2 * N_OTHER) + (ANY,) * len(after),
        out_specs=(HBM, HBM), out_shape=(pltpu.HBM(src.shape, src.dtype), pltpu.HBM(land.shape, land.dtype)),
        input_output_aliases={0: 0, 1: 1},
        compiler_params=pltpu.CompilerParams(has_side_effects=pltpu.SideEffectType.DATAFLOW_SIDE_EFFECTING),
    )(src, land, *sems, *after)


def handover(land, name, sums=None):
    n = N_OTHER + (sums is not None)

    def body(*refs):
        land_ref, send_sems, recv_sems = refs[0], refs[-2], refs[-1]
        x, y, c = _position()
        me = 2 * x + y
        sibling = (x, y, 1 - c)
        pieces = [(_half(land_ref, 2 * px + py, c), 2 * px + py) for px, py in _other_chips(x, y)]
        if sums is not None:
            pieces.append((refs[1].at[me], me))
        sends = [_remote(piece, _half(land_ref, chip, c), send_sems.at[j], recv_sems.at[j], sibling)
                 for j, (piece, chip) in enumerate(pieces)]
        for cp in sends:
            cp.start()
        for j, (_, chip) in enumerate(pieces):
            other = _half(land_ref, chip, 1 - c)
            _remote(other, other, send_sems.at[j], recv_sems.at[j], sibling).wait_recv()
        for cp in sends:
            cp.wait_send()

    args = (land,) + ((sums,) if sums is not None else ())
    return pl.pallas_call(
        body, name=name, in_specs=[ANY] * len(args), out_specs=ANY,
        out_shape=jax.ShapeDtypeStruct(land.shape, land.dtype), input_output_aliases={0: 0},
        scratch_shapes=[pltpu.SemaphoreType.DMA((n,)), pltpu.SemaphoreType.DMA((n,))],
    )(*args)


def swap_comm(grads):
    n = len(grads)

    def copies(srcs, gots, sems):
        send_sems, recv_sems = sems
        x, y, c = _position()
        out = []
        for i, (s, o) in enumerate(zip(srcs, gots)):
            rows = s.shape[1] // 2
            out.append(_remote(s.at[:, pl.ds((1 - c) * rows, rows), :], o, send_sems.at[i], recv_sems.at[i],
                               (x, y, 1 - c)))
        return out

    def first(srcs, gots, sems):
        for cp in copies(srcs, gots, sems):
            cp.start()

    def last(srcs, gots, sems):
        for cp in copies(srcs, gots, sems):
            cp.wait()

    return Comm(grads, [jax.ShapeDtypeStruct((N_CHIPS, g.shape[1] // 2, g.shape[2]), g.dtype) for g in grads],
                [pltpu.SemaphoreType.DMA((n,)), pltpu.SemaphoreType.DMA((n,))], first, None, last)


def add_halves(name, g, got, core):
    _, half, cols = got.shape
    mine = pl.BlockSpec((None, half, cols), lambda k, c_ref: (k, c_ref[0], 0))
    other = pl.BlockSpec((None, half, cols), lambda k, c_ref: (k, 0, 0))

    def body(c_ref, g_ref, got_ref, o_ref):
        o_ref[...] = (g_ref[...] + got_ref[...]).astype(BF16)

    return pl.pallas_call(
        body, name="add_halves_" + name,
        grid_spec=pltpu.PrefetchScalarGridSpec(num_scalar_prefetch=1, grid=(N_CHIPS,), in_specs=[mine, other],
                                               out_specs=other),
        out_shape=jax.ShapeDtypeStruct(got.shape, BF16),
        compiler_params=_params(("parallel",)),
    )(core, g, got)


def exchange_comm(parts):
    n = len(parts)

    def copies(srcs, outs, sems):
        send_sems, recv_sems, local_sems = sems
        x, y, c = _position()
        me = 2 * x + y
        sibling = (x, y, 1 - c)
        chips = _other_chips(x, y)
        locals_, sends, arrived, passed, from_sibling = [], [], [], [], []
        for i, (s, o) in enumerate(zip(srcs, outs)):
            locals_.append(_later(pltpu.make_async_copy, s.at[me], _half(o, me, c), local_sems.at[i]))
            sends.append(_later(_remote, s.at[me], _half(o, me, c), send_sems.at[i, 3], recv_sems.at[i, 3], sibling))
            other = _half(o, me, 1 - c)
            from_sibling.append(_later(_remote, other, other, send_sems.at[i, 3], recv_sems.at[i, 3], sibling))
        for j, (px, py) in enumerate(chips):
            for i, (s, o) in enumerate(zip(srcs, outs)):
                sends.append(_later(_remote, s.at[2 * px + py], _half(o, me, c), send_sems.at[i, j],
                                    recv_sems.at[i, j], (px, py, c)))
                got = _half(o, 2 * px + py, c)
                arrived.append(_later(_remote, got, got, send_sems.at[i, j], recv_sems.at[i, j], (px, py, c)))
                passed.append(_later(_remote, got, got, send_sems.at[i, 4 + j], recv_sems.at[i, 4 + j], sibling))
                other = _half(o, 2 * px + py, 1 - c)
                from_sibling.append(_later(_remote, other, other, send_sems.at[i, 4 + j], recv_sems.at[i, 4 + j],
                                           sibling))
        return locals_, sends, arrived, passed, from_sibling

    return Comm(parts, [jax.ShapeDtypeStruct((N_CHIPS, 2 * p.shape[1], p.shape[2]), p.dtype) for p in parts],
                [pltpu.SemaphoreType.DMA((n, 7)), pltpu.SemaphoreType.DMA((n, 7)), pltpu.SemaphoreType.DMA((n,))],
                *_two_level_phases(copies))


def small_comm(shares):
    n = len(shares)

    def copies(srcs, outs, sems):
        send_sems, recv_sems, local_sems = sems
        x, y, c = _position()
        me = 4 * x + 2 * y + c
        flips = [(fx, fy, fc) for fx in (0, 1) for fy in (0, 1) for fc in (0, 1)][1:]
        peers = [(1 - x if fx else x, 1 - y if fy else y, 1 - c if fc else c) for fx, fy, fc in flips]
        locals_, sends, arrived = [], [], []
        for i, (src_ref, out_ref) in enumerate(zip(srcs, outs)):
            locals_.append(_later(pltpu.make_async_copy, src_ref, out_ref.at[me], local_sems.at[i]))
            for j, (px, py, pc) in enumerate(peers):
                sends.append(_later(_remote, src_ref, out_ref.at[me], send_sems.at[i, j], recv_sems.at[i, j],
                                    (px, py, pc)))
                got = out_ref.at[4 * px + 2 * py + pc]
                arrived.append(_later(_remote, got, got, send_sems.at[i, j], recv_sems.at[i, j], (px, py, pc)))
        return locals_, sends, arrived

    def first(*refs):
        locals_, sends, _ = copies(*refs)
        for cp in locals_ + sends:
            cp().start()

    def last(*refs):
        locals_, sends, arrived = copies(*refs)
        for cp in arrived:
            cp().wait_recv()
        for cp in sends:
            cp().wait_send()
        for cp in locals_:
            cp().wait()

    return Comm(shares, [jax.ShapeDtypeStruct((N_DEV,) + s.shape, s.dtype) for s in shares],
                [pltpu.SemaphoreType.DMA((n, 7)), pltpu.SemaphoreType.DMA((n, 7)), pltpu.SemaphoreType.DMA((n,))],
                first, None, last)


def _adam_fn(w, g, m, v):
    m = ADAM_B1 * m + (1.0 - ADAM_B1) * g
    v = ADAM_B2 * v + (1.0 - ADAM_B2) * jnp.square(g)
    m_hat = m / (1.0 - ADAM_B1 ** ADAM_STEP)
    v_hat = v / (1.0 - ADAM_B2 ** ADAM_STEP)
    return -ADAM_LR * (m_hat / (jnp.sqrt(v_hat) + ADAM_EPS) + ADAM_WD * w), m, v


def adam_big(name, parts, w, m, v):
    rows, cols = w.shape
    tm = _pick(rows, 384, 16)

    def fn(p0, p1, p2, p3, wv, mv, vv):
        g = ((p0.astype(F32) + p1.astype(F32)) + p2.astype(F32)) + p3.astype(F32)
        return (g,) + _adam_fn(wv, g, mv, vv)

    return rowwise(fn, [parts, w, m, v], [(cols, F32)] * 4, "adam_" + name, tm=tm, rows=rows)


def adam_small(name, gathered, w, m, v):
    def body(g_ref, w_ref, m_ref, v_ref, go_ref, d_ref, mo_ref, vo_ref):
        g = g_ref[0]
        for k in range(1, N_DEV):
            g = g + g_ref[k]
        go_ref[...] = g
        d_ref[...], mo_ref[...], vo_ref[...] = _adam_fn(w_ref[...], g, m_ref[...], v_ref[...])

    return pl.pallas_call(body, name=name, out_shape=[jax.ShapeDtypeStruct(w.shape, F32)] * 4,
                          compiler_params=_params())(gathered, w, m, v)


def _ssm_2d(name, t):
    t = t[0] if t.ndim > 2 else t
    if name in ("ssm_b_re", "ssm_b_im"):
        return t.transpose(0, 2, 1).reshape(SSM_W, 64)
    if name in ("ssm_c_re", "ssm_c_im"):
        return t.reshape(SSM_W, 64)
    return t.T if name == "ssm_d" else t


def _ssm_back(name, t):
    if name in ("ssm_b_re", "ssm_b_im"):
        return t.reshape(32, 16, 64).transpose(0, 2, 1)[None]
    if name in ("ssm_c_re", "ssm_c_im"):
        return t.reshape(1, 32, 16, 64)
    if name == "ssm_d":
        return t.T[None]
    return t if name == "ssm_log_dt" else t[None]


def adam_ssm(shares, w, m, v):
    n = len(w)

    def body(*refs):
        ins, outs = refs[:4 * n], refs[4 * n:]
        for i in range(n):
            g_ref, w_ref, m_ref, v_ref = (ins[k * n + i] for k in range(4))
            g = g_ref[0]
            for k in range(1, N_DEV):
                g = g + g_ref[k]
            outs[4 * i][...] = g
            outs[4 * i + 1][...], outs[4 * i + 2][...], outs[4 * i + 3][...] = _adam_fn(w_ref[...], g, m_ref[...],
                                                                                      v_ref[...])

    out_shape = [jax.ShapeDtypeStruct(t.shape, F32) for t in w for _ in range(4)]
    res = pl.pallas_call(body, name="adam_ssm", out_shape=out_shape, compiler_params=_params())(*shares, *w, *m, *v)
    return [res[4 * i:4 * i + 4] for i in range(n)]


def _pack_small(names, vals, rows, last=None):
    flat = [vals[n].reshape(-1) for n in names]
    if last is not None:
        flat.append(last.reshape(-1))
    flat = jnp.concatenate(flat)
    return jnp.pad(flat, (0, rows * LANES - flat.shape[0])).reshape(rows, LANES)


def _unpack_small(names, pack, shapes):
    flat, out, off = pack.reshape(-1), {}, 0
    for n in names:
        size = math.prod(shapes[n])
        out[n] = flat[off:off + size].reshape(shapes[n])
        off += size
    return out, flat[off]


def _to_slots(name, g, shard_shape):
    rows, cols = shard_shape
    if name in ROW_SHARDED:
        return g.reshape(N_CHIPS, rows, cols)
    return g.reshape(rows, N_CHIPS, cols).transpose(1, 0, 2)


def _from_slots(name, s):
    _, rows, cols = s.shape
    if name in ROW_SHARDED:
        return s.reshape(N_CHIPS * rows, cols)
    return s.transpose(1, 0, 2).reshape(rows, N_CHIPS * cols)


def kernel(x, norm_mix_g, w_in, ssm_a_re, ssm_a_im, ssm_log_dt, ssm_b_re, ssm_b_im, ssm_c_re, ssm_c_im, ssm_d, w_glu, w_attn_out, w_out, norm_ffn_g, w_ffn_gate, w_ffn_up, w_ffn_down, norm_final_g, loss_target, m_norm_mix_g, m_w_in, m_ssm_a_re, m_ssm_a_im, m_ssm_log_dt, m_ssm_b_re, m_ssm_b_im, m_ssm_c_re, m_ssm_c_im, m_ssm_d, m_w_glu, m_w_attn_out, m_w_out, m_norm_ffn_g, m_w_ffn_gate, m_w_ffn_up, m_w_ffn_down, m_norm_final_g, v_norm_mix_g, v_w_in, v_ssm_a_re, v_ssm_a_im, v_ssm_log_dt, v_ssm_b_re, v_ssm_b_im, v_ssm_c_re, v_ssm_c_im, v_ssm_d, v_w_glu, v_w_attn_out, v_w_out, v_norm_ffn_g, v_w_ffn_gate, v_w_ffn_up, v_w_ffn_down, v_norm_final_g):
    given = dict(locals())
    def local(name, prefix=""):
        t = given[prefix + name][0]
        return t.T if name in TRANSPOSED else t

    shard = {n: local(n) for n in BIG}
    shapes = {n: given[n].shape for n in WEIGHTS}

    small = {n: given[n] for n in SMALL}
    small_2d = dict(small)
    for n in ("ssm_a_re", "ssm_a_im", "ssm_b_re", "ssm_b_im", "ssm_c_re", "ssm_c_im", "ssm_d"):
        small_2d[n] = small[n][0]
    small_2d["norm_final_g"] = norm_final_g.reshape(1, D_MODEL)

    core = lax.axis_index("c").astype(jnp.int32).reshape(1)
    loss, grad_x, parts, ssm_shares, gs_norm, w_in_reduce = local_step(
        x.reshape(TOKENS, D_MODEL), loss_target.reshape(TOKENS, D_MODEL),
        {n: shard[n] for n in BIG}, small_2d, core)

    (norm_shares,) = run_comm(small_comm([_pack_small(NORM_SMALL, gs_norm, NORM_ROWS, last=loss)]),
                              "gather_norm_grads")
    small_out = [{} for _ in range(4)]
    packs = [_pack_small(NORM_SMALL, {n: given[p + n] for n in NORM_SMALL}, NORM_ROWS) for p in ("", "m_", "v_")]
    for kind, t in enumerate(adam_small("adam_norm_gains", norm_shares, *packs)):
        vals, after = _unpack_small(NORM_SMALL, t, shapes)
        small_out[kind].update(vals)
        if kind == 0:
            total_loss = after
    ssm_in = [[_ssm_2d(n, given[p + n]) for n in SSM_SMALL] for p in ("", "m_", "v_")]
    for n, res in zip(SSM_SMALL, adam_ssm(ssm_shares, *ssm_in)):
        for kind, t in enumerate(res):
            small_out[kind][n] = _ssm_back(n, t)

    big_out, updated = {}, {}
    for n in BIG[1:] + BIG[:1]:
        if n == "w_in":
            sems, chip_sum, land = w_in_reduce
            behind = [updated[k][1] for k in BIG[1:]] + [norm_shares]
            chip_sum, land = split_wait(sems, chip_sum, land, behind, "w_in_reduce_wait", per_chip=True)
            land = handover(land, "w_in_reduce_handover", sums=chip_sum)
            me = 2 * lax.axis_index("x") + lax.axis_index("y")
            parts[n] = lax.dynamic_update_slice(land, lax.dynamic_slice_in_dim(chip_sum, me, 1, 0),
                                                (me, lax.axis_index("c") * chip_sum.shape[1], 0))
        updated[n] = adam_big(n, parts[n], shard[n], local(n, "m_"), local(n, "v_"))
        big_out[n] = [(t.T if n in TRANSPOSED else t)[None] for t in updated[n]]

    outs = [total_loss, grad_x.reshape(LOCAL_BATCH, SEQ, D_MODEL)]
    for kind in range(4):
        for n in WEIGHTS:
            outs.append(big_out[n][kind] if n in BIG else small_out[kind][n])
    return tuple(outs)
```

```python
import functools
import math

import jax
import jax.numpy as jnp
import numpy as np
from jax import lax
from jax.experimental import pallas as pl
from jax.experimental.pallas import tpu as pltpu

F32 = jnp.float32
BF16 = jnp.bfloat16
MESH = pl.DeviceIdType.MESH

D_MODEL = 1024
SEQ = 2048
LOCAL_BATCH = 2
TOKENS = LOCAL_BATCH * SEQ
HEAD_DIM = 64
HEADS_PER_GROUP = 4
GROUP_W = HEADS_PER_GROUP * HEAD_DIM
N_GROUPS = 3
DILATIONS = (1, 4, 16)
ATTN_BLOCK = 128
ROPE_DIM = 16
ROPE_THETA = 500000.0
QKV_W = 3 * N_GROUPS * GROUP_W
SSM_W = 512
SSM_STATE_W = 2048
SSM_LANE_BLOCKS = 4
GATE_W = 2 * D_MODEL
D_FF = 2816
RMS_EPS = 1e-6
NEG_INF = -1e30
ADAM_LR, ADAM_B1, ADAM_B2, ADAM_EPS, ADAM_WD, ADAM_STEP = 0.001, 0.9, 0.999, 1e-08, 0.01, 10
N_CHIPS = 4
N_DEV = 8

VMEM_LIMIT = 56 * 1024 * 1024
LANES = 128


def _params(sem=None):
    return pltpu.CompilerParams(dimension_semantics=sem, vmem_limit_bytes=VMEM_LIMIT)


def _pick(n, cap, align=LANES):
    best = None
    for d in range(align, min(n, cap) + 1, align):
        if n % d == 0:
            best = d
    return n if best is None or n <= cap else best


_DIMS = {"nn": (((1,), (0,)), ((), ())), "nt": (((1,), (1,)), ((), ())), "tn": (((0,), (0,)), ((), ()))}


def _dot(a, b, mode):
    return lax.dot_general(a, b, _DIMS[mode], preferred_element_type=F32)


def matmul(a, b, mode, out_dtype, name, add=None, comm=None):
    if mode == "nn":
        (m, k), n = a.shape, b.shape[1]
    elif mode == "nt":
        (m, k), n = a.shape, b.shape[0]
    else:
        (k, m), n = a.shape, b.shape[1]
    tn = _pick(n, 1408 if mode != "tn" else 512)
    tk = _pick(k, 2816) if mode != "tn" else k
    tm = _pick(m, 1408)
    out_bytes = jnp.dtype(out_dtype).itemsize

    def need(tm_):
        return 2 * 2 * (tm_ * tk + tk * tn) + tm_ * tn * (4 + 2 * out_bytes + (8 if add is not None else 0))

    while need(tm) > 40 * 1024 * 1024 and tm % 256 == 0:
        tm //= 2
    nk = k // tk
    a_spec = {"nn": pl.BlockSpec((tm, tk), lambda i, j, kk: (i, kk)),
              "nt": pl.BlockSpec((tm, tk), lambda i, j, kk: (i, kk)),
              "tn": pl.BlockSpec((tk, tm), lambda i, j, kk: (kk, i))}[mode]
    b_spec = {"nn": pl.BlockSpec((tk, tn), lambda i, j, kk: (kk, j)),
              "nt": pl.BlockSpec((tn, tk), lambda i, j, kk: (j, kk)),
              "tn": pl.BlockSpec((tk, tn), lambda i, j, kk: (kk, j))}[mode]
    o_spec = pl.BlockSpec((tm, tn), lambda i, j, kk: (i, j))

    def body(a_ref, b_ref, *rest):
        if add is not None:
            add_ref, o_ref, acc_ref = rest
        else:
            o_ref, acc_ref = rest
        part = _dot(a_ref[...], b_ref[...], mode)
        if nk == 1:
            res = part if add is None else part + add_ref[...]
            o_ref[...] = res.astype(out_dtype)
            return
        kk = pl.program_id(2)

        @pl.when(kk == 0)
        def _():
            acc_ref[...] = part

        @pl.when(kk > 0)
        def _():
            acc_ref[...] += part

        @pl.when(kk == nk - 1)
        def _():
            res = acc_ref[...] if add is None else acc_ref[...] + add_ref[...]
            o_ref[...] = res.astype(out_dtype)

    in_specs = [a_spec, b_spec] + ([o_spec] if add is not None else [])
    args = (a, b) + ((add,) if add is not None else ())
    res = hosted_call(
        body, comm, name, (m // tm, n // tn, nk), in_specs, [o_spec], [jax.ShapeDtypeStruct((m, n), out_dtype)],
        [pltpu.VMEM((tm, tn) if nk > 1 else (8, LANES), F32)], args, ("parallel", "parallel", "arbitrary"))
    return res[0] if comm is None else res


def matmul_rows(a, b, name, fn, extra, outs, accs=(), add=None, comm=None, tm=512):
    a_list, b_list = (list(a), list(b)) if isinstance(a, (list, tuple)) else ([a], [b])
    m, n = a_list[0].shape[0], b_list[0].shape[1]
    n_mm = len(a_list)
    n_fixed = 2 * n_mm + (add is not None)
    row_spec = lambda cols: pl.BlockSpec((tm, cols), lambda i: (i, 0))
    in_specs = [row_spec(t.shape[1]) for t in a_list] + [pl.BlockSpec(t.shape, lambda i: (0, 0)) for t in b_list]
    in_specs += [row_spec(n)] if add is not None else []
    in_specs += [pl.BlockSpec(e.shape, lambda i: (0, 0)) if e.shape[0] == 1 else row_spec(e.shape[1]) for e in extra]
    out_specs = [row_spec(c) for c, _ in outs] + [pl.BlockSpec((1, c), lambda i: (0, 0)) for c in accs]
    out_shape = [jax.ShapeDtypeStruct((m, c), dt) for c, dt in outs] + [jax.ShapeDtypeStruct((1, c), F32) for c in accs]

    def body(*refs):
        rows = _dot(refs[0][...], refs[n_mm][...], "nn")
        for i in range(1, n_mm):
            rows = rows + _dot(refs[i][...], refs[n_mm + i][...], "nn")
        if add is not None:
            rows = rows + refs[2 * n_mm][...]
        n_in = n_fixed + len(extra)
        res = fn(rows, *[r[...] for r in refs[n_fixed:n_in]])
        for r, v in zip(refs[n_in:n_in + len(outs)], res[:len(outs)]):
            r[...] = v.astype(r.dtype)
        first = pl.program_id(0) == 0
        for r, v in zip(refs[n_in + len(outs):], res[len(outs):]):
            @pl.when(first)
            def _(r=r, v=v):
                r[...] = v

            @pl.when(jnp.logical_not(first))
            def _(r=r, v=v):
                r[...] += v

    args = tuple(a_list) + tuple(b_list) + ((add,) if add is not None else ()) + tuple(extra)
    return hosted_call(body, comm, name, (m // tm,), in_specs, out_specs, out_shape, [], args, ("arbitrary",))


def _merge_specs(tm):
    half = lambda blk: pl.BlockSpec((tm, D_MODEL), functools.partial(lambda i, blk_: (i, blk_), blk_=blk))
    return [half(0), half(1), pl.BlockSpec((tm, GROUP_W), lambda i: (i, 0)),
            pl.BlockSpec((GROUP_W, D_MODEL), lambda i: (0, 0)), pl.BlockSpec((tm, SSM_W), lambda i: (i, 0)),
            pl.BlockSpec((SSM_W, GATE_W), lambda i: (0, 0))]


def _merge_operands(g0, g1, at, wa, yg, wg):
    z = _dot(yg[...], wg[...], "nn")
    return (g0[...].astype(F32), g1[...].astype(F32), _dot(at[...], wa[...], "nn"), z[:, :D_MODEL], z[:, D_MODEL:])


def merge_out_proj(gl, attn_b, w_attn_out, yg, w_glu, w_out, x, g_ffn):
    tm = 512

    def body(g0, g1, at, wa, yg_ref, wg, w_ref, x_ref, g_ref, m_ref, x1_ref, h2_ref):
        merged = _merge_fn(*_merge_operands(g0, g1, at, wa, yg_ref, wg)).astype(BF16)
        m_ref[...] = merged
        x1 = _dot(merged, w_ref[...], "nn") + x_ref[...]
        x1_ref[...] = x1
        h2_ref[...] = _rms(x1, g_ref[...]).astype(BF16)

    rows = pl.BlockSpec((tm, D_MODEL), lambda i: (i, 0))
    whole = pl.BlockSpec((D_MODEL, D_MODEL), lambda i: (0, 0))
    gain = pl.BlockSpec((1, D_MODEL), lambda i: (0, 0))
    tok = lambda dt: jax.ShapeDtypeStruct((TOKENS, D_MODEL), dt)
    return pl.pallas_call(
        body, name="merge_out_proj", grid=(TOKENS // tm,), in_specs=_merge_specs(tm) + [whole, rows, gain],
        out_specs=[rows] * 3, out_shape=[tok(BF16), tok(F32), tok(BF16)], compiler_params=_params(("parallel",)),
    )(gl, gl, attn_b, w_attn_out, yg, w_glu, w_out, x, g_ffn)


def merge_bwd(dx1_b, w_out, gl, attn_b, w_attn_out, yg, w_glu, comm=None):
    tm = 512

    def body(dx_ref, w_ref, g0, g1, at, wa, yg_ref, wg, dgl_ref, dad_ref, dz_ref, dat_ref, dyg_ref):
        dm = _dot(dx_ref[...], w_ref[...], "nt")
        _, vjp = jax.vjp(_merge_fn, *_merge_operands(g0, g1, at, wa, yg_ref, wg))
        dg0, dg1, dad, dza, dzb = vjp(dm)
        dat_ref[...] = _dot(dad.astype(BF16), wa[...], "nt")
        dgl_ref[:, :D_MODEL] = dg0.astype(BF16)
        dgl_ref[:, D_MODEL:] = dg1.astype(BF16)
        dad_ref[...] = dad.astype(BF16)
        dz_ref[:, :D_MODEL] = dza.astype(BF16)
        dz_ref[:, D_MODEL:] = dzb.astype(BF16)
        dyg_ref[...] = _dot(dz_ref[...], wg[...], "nt")

    rows = pl.BlockSpec((tm, D_MODEL), lambda i: (i, 0))
    wide = pl.BlockSpec((tm, GATE_W), lambda i: (i, 0))
    whole = pl.BlockSpec((D_MODEL, D_MODEL), lambda i: (0, 0))
    return hosted_call(
        body, comm, "merge_bwd", (TOKENS // tm,), [rows, whole] + _merge_specs(tm),
        [wide, rows, wide, pl.BlockSpec((tm, GROUP_W), lambda i: (i, 0)), pl.BlockSpec((tm, SSM_W), lambda i: (i, 0))],
        [jax.ShapeDtypeStruct((TOKENS, GATE_W), BF16), jax.ShapeDtypeStruct((TOKENS, D_MODEL), BF16),
         jax.ShapeDtypeStruct((TOKENS, GATE_W), BF16), jax.ShapeDtypeStruct((TOKENS, GROUP_W), F32),
         jax.ShapeDtypeStruct((TOKENS, SSM_W), F32)], [],
        (dx1_b, w_out, gl, gl, attn_b, w_attn_out, yg, w_glu), ("arbitrary",))


FFN_TM, FFN_TN = 512, 1408


def ffn_in(h2, wg_t, wu_t, comm=None):
    def body(h_ref, wg_ref, wu_ref, a_ref, b_ref, act_ref):
        hv = h_ref[...]
        a, b = _dot(hv, wg_ref[...], "nt"), _dot(hv, wu_ref[...], "nt")
        a_ref[...] = a.astype(BF16)
        b_ref[...] = b.astype(BF16)
        act_ref[...] = _swiglu_fn(a, b).astype(BF16)

    rows = pl.BlockSpec((FFN_TM, D_MODEL), lambda i, j: (i, 0))
    wts = pl.BlockSpec((FFN_TN, D_MODEL), lambda i, j: (j, 0))
    out = pl.BlockSpec((FFN_TM, FFN_TN), lambda i, j: (i, j))
    return hosted_call(body, comm, "ffn_in", (TOKENS // FFN_TM, D_FF // FFN_TN), [rows, wts, wts], [out] * 3,
                       [jax.ShapeDtypeStruct((TOKENS, D_FF), BF16)] * 3, [], (h2, wg_t, wu_t),
                       ("parallel", "parallel"))


def ffn_in_bwd(dx2_b, wd, a, b):
    def body(dx_ref, wd_ref, a_ref, b_ref, da_ref, db_ref):
        dact = _dot(dx_ref[...], wd_ref[...], "nt")
        av, bv = a_ref[...].astype(F32), b_ref[...].astype(F32)
        sig = jax.nn.sigmoid(av)
        act = av * sig
        da_ref[...] = (dact * bv * (sig * (1.0 + av - act))).astype(BF16)
        db_ref[...] = (dact * act).astype(BF16)

    rows = pl.BlockSpec((FFN_TM, D_MODEL), lambda i, j: (i, 0))
    wts = pl.BlockSpec((FFN_TN, D_MODEL), lambda i, j: (j, 0))
    out = pl.BlockSpec((FFN_TM, FFN_TN), lambda i, j: (i, j))
    return pl.pallas_call(
        body, name="ffn_in_bwd", grid=(TOKENS // FFN_TM, D_FF // FFN_TN), in_specs=[rows, wts, out, out],
        out_specs=[out] * 2, out_shape=[jax.ShapeDtypeStruct((TOKENS, D_FF), BF16)] * 2,
        compiler_params=_params(("parallel", "parallel")),
    )(dx2_b, wd, a, b)


def mix_in_bwd(grads, weights, partial, x, g, skip, comm=None):
    n = len(grads)
    tm = 512

    def body(*refs):
        a_refs, b_refs = refs[:n], refs[n:2 * n]
        part_ref, x_ref, g_ref, skip_ref, gx_ref, dg_ref = refs[2 * n:]
        dh = part_ref[...]
        for a_ref, b_ref in zip(a_refs, b_refs):
            dh = dh + _dot(a_ref[...], b_ref[...], "nn")
        _, vjp = jax.vjp(_rms, x_ref[...], g_ref[...])
        dx, dg = vjp(dh)
        gx_ref[...] = dx + skip_ref[...]
        first = pl.program_id(0) == 0

        @pl.when(first)
        def _():
            dg_ref[...] = dg

        @pl.when(jnp.logical_not(first))
        def _():
            dg_ref[...] += dg

    rows = pl.BlockSpec((tm, D_MODEL), lambda i: (i, 0))
    gain = pl.BlockSpec((1, D_MODEL), lambda i: (0, 0))
    in_specs = [pl.BlockSpec((tm, a.shape[1]), lambda i: (i, 0)) for a in grads]
    in_specs += [pl.BlockSpec(b.shape, lambda i: (0, 0)) for b in weights]
    return hosted_call(
        body, comm, "mix_in_bwd", (TOKENS // tm,), in_specs + [rows, rows, gain, rows], [rows, gain],
        [jax.ShapeDtypeStruct((TOKENS, D_MODEL), F32), jax.ShapeDtypeStruct((1, D_MODEL), F32)], [],
        (*grads, *weights, partial, x, g, skip), ("arbitrary",))


def rowwise(fn, ins, outs, name, accs=(), tm=256, rows=TOKENS, comm=None):
    in_specs, args = [], []
    for item in ins:
        arr, width, blk = item if isinstance(item, tuple) else (item, None, 0)
        if arr.ndim == 3:
            for k in range(arr.shape[0]):
                in_specs.append(pl.BlockSpec((None, tm, arr.shape[2]), functools.partial(lambda i, k_: (k_, i, 0), k_=k)))
                args.append(arr)
            continue
        if arr.shape[0] == 1:
            in_specs.append(pl.BlockSpec(arr.shape, lambda i: (0, 0)))
        elif width is None:
            in_specs.append(pl.BlockSpec((tm, arr.shape[1]), lambda i: (i, 0)))
        else:
            in_specs.append(pl.BlockSpec((tm, width), functools.partial(lambda i, blk_: (i, blk_), blk_=blk)))
        args.append(arr)
    out_specs = [pl.BlockSpec((tm, c), lambda i: (i, 0)) for c, _ in outs]
    out_specs += [pl.BlockSpec((1, c), lambda i: (0, 0)) for c in accs]
    out_shape = [jax.ShapeDtypeStruct((rows, c), dt) for c, dt in outs]
    out_shape += [jax.ShapeDtypeStruct((1, c), F32) for c in accs]
    n_in, n_out = len(args), len(outs)
    c_ins, c_outs, c_sems = _comm_operands(comm)

    def body(*refs):
        refs, c_refs = _comm_refs(comm, refs, n_in, n_out + len(accs))
        step = pl.program_id(0)
        _comm_begin(comm, c_refs, step, rows // tm)
        res = fn(*[r[...] for r in refs[:n_in]])
        for r, v in zip(refs[n_in:n_in + n_out], res[:n_out]):
            r[...] = v.astype(r.dtype)
        first = step == 0
        for r, v in zip(refs[n_in + n_out:], res[n_out:]):
            @pl.when(first)
            def _(r=r, v=v):
                r[...] = v

            @pl.when(jnp.logical_not(first))
            def _(r=r, v=v):
                r[...] += v
        _comm_end(comm, c_refs, step, rows // tm)

    return pl.pallas_call(
        body, name=name, grid=(rows // tm,), in_specs=in_specs + [ANY] * len(c_ins),
        out_specs=out_specs + [ANY] * len(c_outs), out_shape=out_shape + c_outs, scratch_shapes=c_sems,
        compiler_params=_params(("arbitrary",)),
    )(*args, *c_ins)


def first_norm(x, g, others, comm=None):
    tm, n = 256, len(others)

    def body(x_ref, g_ref, *rest):
        srcs, h_ref, dsts = rest[:n], rest[n], rest[n + 1:]
        h_ref[...] = _rms(x_ref[...], g_ref[...]).astype(BF16)
        for k, (s, d) in enumerate(zip(srcs, dsts)):
            @pl.when(pl.program_id(0) == k)
            def _(s=s, d=d):
                d[...] = s[...].astype(BF16)

    rows = pl.BlockSpec((tm, D_MODEL), lambda i: (i, 0))
    whole = [pl.BlockSpec(a.shape, lambda i: (0, 0)) for a in others]
    return hosted_call(
        body, comm, "norm_mix", (TOKENS // tm,), [rows, pl.BlockSpec((1, D_MODEL), lambda i: (0, 0))] + whole,
        [rows] + whole, [jax.ShapeDtypeStruct((TOKENS, D_MODEL), BF16)]
        + [jax.ShapeDtypeStruct(a.shape, BF16) for a in others], [], (x, g, *others), ("arbitrary",))


def _rms(x, g):
    return x * lax.rsqrt(jnp.mean(x * x, axis=-1, keepdims=True) + RMS_EPS) * g


def _colsum(v):
    return jnp.sum(v, axis=0, keepdims=True)


PAIR_W = 2 * HEAD_DIM
N_PAIRS = HEADS_PER_GROUP // 2


def _qkv_order(w_t, back=False):
    dims = (N_PAIRS, N_GROUPS, 3) if back else (3, N_GROUPS, N_PAIRS)
    return w_t.reshape(dims + (PAIR_W, w_t.shape[1])).transpose(2, 1, 0, 3, 4).reshape(QKV_W, w_t.shape[1])


def _rope_tables():
    half = ROPE_DIM // 2
    inv = np.power(np.float32(ROPE_THETA), -np.arange(half, dtype=np.float32) * np.float32(2.0 / ROPE_DIM))
    ang = (np.arange(SEQ, dtype=np.float32)[:, None] * inv[None, :]).astype(np.float32)
    cos, sin = np.cos(ang), np.sin(ang)
    zeros = np.zeros((SEQ, HEAD_DIM - ROPE_DIM), np.float32)
    zh = np.zeros((SEQ, half), np.float32)
    c = np.concatenate([cos, cos, zeros + 1.0], axis=1)
    sa = np.concatenate([-sin, zh, zeros], axis=1)
    sb = np.concatenate([zh, sin, zeros], axis=1)
    return [jnp.asarray(np.tile(t, (1, 2)), F32) for t in (c, sa, sb)]


def _rope_fwd(x, c, sa, sb):
    return x * c + pltpu.roll(x, PAIR_W - 8, 1) * sa + pltpu.roll(x, 8, 1) * sb


def _rope_bwd(dy, c, sa, sb):
    return dy * c + pltpu.roll(dy * sb, PAIR_W - 8, 1) + pltpu.roll(dy * sa, 8, 1)


def _band_masks():
    row = lax.broadcasted_iota(jnp.int32, (ATTN_BLOCK, ATTN_BLOCK), 0)
    col = lax.broadcasted_iota(jnp.int32, (ATTN_BLOCK, ATTN_BLOCK), 1)
    return col <= row, col >= row


def _stack_rows(t):
    return jnp.concatenate([t, t], axis=0)


def _stack_heads(t, first_head):
    return jnp.concatenate([jnp.where(first_head, t, 0), jnp.where(first_head, 0, t)], axis=0)


def _per_head(fn):
    return jnp.concatenate([fn(slice(h * HEAD_DIM, (h + 1) * HEAD_DIM)) for h in range(2)], axis=1)


def _slab_spec(kind):
    return pl.BlockSpec((None, SEQ, PAIR_W), lambda b, p, g: (b, 0, p * 3 * N_GROUPS + g * 3 + kind))


_TABLE_SPEC = pl.BlockSpec((SEQ, PAIR_W), lambda b, p, g: (0, 0))
_PAIR_SPEC = pl.BlockSpec((None, SEQ, PAIR_W), lambda b, p, g: (b, 0, p))


def _block_rows(dil, r, n):
    return pl.ds(n * (ATTN_BLOCK * dil) + r, ATTN_BLOCK, stride=dil)


def proj_qkv(h, w_qkv_t, tables, comm=None):
    tm = 1024
    pair_w = QKV_W // N_PAIRS
    scale = HEAD_DIM ** -0.5

    def body(h_ref, w_ref, c_ref, sa_ref, sb_ref, o_ref):
        rows = _dot(h_ref[...], w_ref[...], "nt")
        c, sa, sb = c_ref[...], sa_ref[...], sb_ref[...]
        for blk in range(pair_w // PAIR_W):
            cols = slice(blk * PAIR_W, (blk + 1) * PAIR_W)
            x = rows[:, cols]
            if blk % 3 == 0:
                x = _rope_fwd(x, c, sa, sb) * scale
            elif blk % 3 == 1:
                x = _rope_fwd(x, c, sa, sb)
            o_ref[:, cols] = x

    table = pl.BlockSpec((tm, PAIR_W), lambda i, j, : (i % (SEQ // tm), 0))
    res = hosted_call(
        body, comm, "proj_qkv", (TOKENS // tm, N_PAIRS),
        [pl.BlockSpec((tm, D_MODEL), lambda i, j: (i, 0)), pl.BlockSpec((pair_w, D_MODEL), lambda i, j: (j, 0)),
         table, table, table],
        [pl.BlockSpec((tm, pair_w), lambda i, j: (i, j))], [jax.ShapeDtypeStruct((TOKENS, QKV_W), F32)], [],
        (h, w_qkv_t, *tables), ("parallel", "parallel"))
    return res[0] if comm is None else res


def attn_fwd(qkv, comm=None):
    def body(qs, ks, v_ref, attn_b_ref, attn_ref, lse_ref, o0, o1, o2, l0, l1, l2):
        g = pl.program_id(2)
        cur_mask, prev_mask = _band_masks()
        first_head = lax.broadcasted_iota(jnp.int32, (ATTN_BLOCK, PAIR_W), 1) < HEAD_DIM

        def run(dil, o_slab, l_slab):
            nb = SEQ // dil // ATTN_BLOCK

            def block(idx, carry):
                r, n = lax.div(idx, nb), lax.rem(idx, nb)
                cur, prev = _block_rows(dil, r, n), _block_rows(dil, r, jnp.maximum(n - 1, 0))
                q = qs[cur, :].astype(BF16)
                kc, kp = ks[cur, :].astype(BF16), ks[prev, :].astype(BF16)
                vc, vp = v_ref[cur, :].astype(BF16), v_ref[prev, :].astype(BF16)
                q2 = _stack_heads(q, first_head)
                mask = _stack_rows(jnp.concatenate([jnp.logical_and(prev_mask, n > 0), cur_mask], axis=1))
                s2 = jnp.where(mask, _dot(q2, jnp.concatenate([kp, kc], axis=0), "nt"), NEG_INF)
                m = jnp.max(s2, axis=-1, keepdims=True)
                vcat, two = jnp.concatenate([vp, vc], axis=0), _stack_rows(first_head)
                vext = jnp.concatenate([jnp.where(two, vcat, 1), jnp.where(two, 1, vcat)], axis=1)
                r2 = _dot(jnp.exp(s2 - m).astype(BF16), vext, "nn")
                r0, r1 = r2[:ATTN_BLOCK, :PAIR_W], r2[ATTN_BLOCK:, PAIR_W:]
                num = jnp.where(first_head, r0, r1)
                den = pltpu.roll(jnp.where(first_head, r1, r0), HEAD_DIM, 1)
                o_slab[cur, :] = num / den
                l_slab[cur, :] = jnp.where(first_head, m[:ATTN_BLOCK], m[ATTN_BLOCK:]) + jnp.log(den)
                return carry

            lax.fori_loop(0, SEQ // ATTN_BLOCK, block, 0, unroll=4)

        for gi, (o_slab, l_slab) in enumerate(((o0, l0), (o1, l1), (o2, l2))):
            @pl.when(g == gi)
            def _(gi=gi, o_slab=o_slab, l_slab=l_slab):
                run(DILATIONS[gi], o_slab, l_slab)

        @pl.when(g == N_GROUPS - 1)
        def _():
            a, b, cc = l0[...], l1[...], l2[...]
            m = jnp.maximum(jnp.maximum(a, b), cc)
            e0, e1, e2 = jnp.exp(a - m), jnp.exp(b - m), jnp.exp(cc - m)
            tot = e0 + e1 + e2
            attn = (e0 * o0[...] + e1 * o1[...] + e2 * o2[...]) / tot
            attn_ref[...] = attn
            attn_b_ref[...] = attn.astype(BF16)
            lse_ref[...] = m + jnp.log(tot)

    shape = (LOCAL_BATCH, SEQ, GROUP_W)
    slab = pltpu.VMEM((SEQ, PAIR_W), F32)
    return hosted_call(
        body, comm, "attn_fwd", (LOCAL_BATCH, N_PAIRS, N_GROUPS),
        [_slab_spec(0), _slab_spec(1), _slab_spec(2)], [_PAIR_SPEC] * 3,
        [jax.ShapeDtypeStruct(shape, BF16), jax.ShapeDtypeStruct(shape, F32), jax.ShapeDtypeStruct(shape, F32)],
        [slab] * 6, (qkv, qkv, qkv), ("parallel", "parallel", "arbitrary"))


def attn_bwd(qkv, tables, dattn, attn, lse, comm=None):
    scale = HEAD_DIM ** -0.5

    def body(qs, ks, v_ref, c_ref, sa_ref, sb_ref, do_ref, out_ref, lse_ref, dqkv_ref, dl, dq_s, dk_s, dv_s):
        g = pl.program_id(2)
        c, sa, sb = c_ref[...], sa_ref[...], sb_ref[...]

        @pl.when(g == 0)
        def _():
            prod = do_ref[...] * out_ref[...]
            dl[...] = _per_head(
                lambda sl: jnp.broadcast_to(jnp.sum(prod[:, sl], axis=-1, keepdims=True), (SEQ, HEAD_DIM)))

        cur_mask, prev_mask = _band_masks()
        first_head = lax.broadcasted_iota(jnp.int32, (ATTN_BLOCK, PAIR_W), 1) < HEAD_DIM

        def run(dil):
            nb = SEQ // dil // ATTN_BLOCK

            def block(idx, carry):
                r, n = lax.div(idx, nb), lax.rem(idx, nb)
                cur = _block_rows(dil, r, n)
                prev = _block_rows(dil, r, jnp.maximum(n - 1, 0))
                nxt = _block_rows(dil, r, jnp.minimum(n + 1, nb - 1))
                q0, q1 = qs[cur, :].astype(BF16), qs[nxt, :].astype(BF16)
                kp, kc = ks[prev, :].astype(BF16), ks[cur, :].astype(BF16)
                vp, vc = v_ref[prev, :].astype(BF16), v_ref[cur, :].astype(BF16)
                do0, do1 = do_ref[cur, :].astype(BF16), do_ref[nxt, :].astype(BF16)
                lse0, lse1, dl0, dl1 = lse_ref[cur, :], lse_ref[nxt, :], dl[cur, :], dl[nxt, :]
                has_prev = jnp.logical_and(prev_mask, n > 0)
                has_next = jnp.logical_and(prev_mask, n < nb - 1)

                def per_row(t):
                    return jnp.concatenate([t[:, 0:1], t[:, HEAD_DIM:HEAD_DIM + 1]], axis=0)

                q20, q21 = _stack_heads(q0, first_head), _stack_heads(q1, first_head)
                do20, do21 = _stack_heads(do0, first_head), _stack_heads(do1, first_head)
                kcat, vcat = jnp.concatenate([kp, kc], axis=0), jnp.concatenate([vp, vc], axis=0)
                mask0 = _stack_rows(jnp.concatenate([has_prev, cur_mask], axis=1))
                p0 = jnp.where(mask0, jnp.exp(_dot(q20, kcat, "nt") - per_row(lse0)), 0.0)
                ds0 = (p0 * (_dot(do20, vcat, "nt") - per_row(dl0))).astype(BF16)
                p1 = jnp.where(_stack_rows(has_next), jnp.exp(_dot(q21, kc, "nt") - per_row(lse1)), 0.0)
                ds1 = (p1 * (_dot(do21, vc, "nt") - per_row(dl1))).astype(BF16)
                dq2 = _dot(ds0, kcat, "nn")
                dq_s[cur, :] = jnp.where(first_head, dq2[:ATTN_BLOCK], dq2[ATTN_BLOCK:])
                ds_cur = jnp.concatenate([ds0[:, ATTN_BLOCK:], ds1], axis=0)
                p_cur = jnp.concatenate([p0[:, ATTN_BLOCK:], p1], axis=0).astype(BF16)
                dk_s[cur, :] = _dot(ds_cur, jnp.concatenate([q20, q21], axis=0), "tn")
                dv_s[cur, :] = _dot(p_cur, jnp.concatenate([do20, do21], axis=0), "tn")
                return carry

            lax.fori_loop(0, SEQ // ATTN_BLOCK, block, 0, unroll=2)

        for gi in range(N_GROUPS):
            @pl.when(g == gi)
            def _(gi=gi):
                run(DILATIONS[gi])

        dqkv_ref[:, 0:PAIR_W] = _rope_bwd(dq_s[...] * scale, c, sa, sb).astype(BF16)
        dqkv_ref[:, PAIR_W:2 * PAIR_W] = _rope_bwd(dk_s[...], c, sa, sb).astype(BF16)
        dqkv_ref[:, 2 * PAIR_W:] = dv_s[...].astype(BF16)

    slab = pltpu.VMEM((SEQ, PAIR_W), F32)
    return hosted_call(
        body, comm, "attn_bwd", (LOCAL_BATCH, N_PAIRS, N_GROUPS),
        [_slab_spec(0), _slab_spec(1), _slab_spec(2), _TABLE_SPEC, _TABLE_SPEC, _TABLE_SPEC,
         _PAIR_SPEC, _PAIR_SPEC, _PAIR_SPEC],
        [pl.BlockSpec((None, SEQ, 3 * PAIR_W), lambda b, p, g: (b, 0, p * N_GROUPS + g))],
        [jax.ShapeDtypeStruct((LOCAL_BATCH, SEQ, QKV_W), BF16)],
        [slab] * 4, (qkv, qkv, qkv, *tables, dattn, attn, lse), ("parallel", "parallel", "arbitrary"))


def _discretize(lr, li, log_dt, br, bi):
    dt = jnp.exp(log_dt)
    mag = jnp.exp(lr * dt)
    ab_re, ab_im = mag * jnp.cos(li * dt), mag * jnp.sin(li * dt)
    den = lr * lr + li * li
    nr, ni = ab_re - 1.0, ab_im
    f_re = (nr * lr + ni * li) / den
    f_im = (ni * lr - nr * li) / den
    return ab_re, ab_im, f_re[None] * br - f_im[None] * bi, f_re[None] * bi + f_im[None] * br


def ssm_prep(lr, li, log_dt, br, bi):
    def body(lr_ref, li_ref, dt_ref, br_ref, bi_ref, *outs):
        for o, v in zip(outs, _discretize(lr_ref[...], li_ref[...], dt_ref[...], br_ref[...], bi_ref[...])):
            o[...] = v
    shapes = [lr, li, br, bi]
    return pl.pallas_call(body, name="ssm_prep",
                          out_shape=[jax.ShapeDtypeStruct(s.shape, F32) for s in shapes])(lr, li, log_dt, br, bi)


def ssm_prep_bwd(lr, li, log_dt, br, bi, g_ab_re, g_ab_im, g_bb_re, g_bb_im):
    def body(lr_ref, li_ref, dt_ref, br_ref, bi_ref, g0, g1, g2, g3, *outs):
        _, vjp = jax.vjp(_discretize, lr_ref[...], li_ref[...], dt_ref[...], br_ref[...], bi_ref[...])
        for o, v in zip(outs, vjp((g0[...], g1[...], g2[...], g3[...]))):
            o[...] = v
    shapes = [lr, li, log_dt, br, bi]
    return pl.pallas_call(body, name="ssm_prep_bwd",
                          out_shape=[jax.ShapeDtypeStruct(s.shape, F32) for s in shapes])(
        lr, li, log_dt, br, bi, g_ab_re, g_ab_im, g_bb_re, g_bb_im)


def _block_diag(t):
    per = SSM_STATE_W // SSM_LANE_BLOCKS // 64
    g = t.transpose(1, 0, 2).reshape(SSM_LANE_BLOCKS, per, 16, 64)
    eye = jnp.eye(per, dtype=t.dtype)
    return jnp.einsum("jgcn,gh->jgchn", g, eye).reshape(SSM_LANE_BLOCKS, per * 16, per * 64)


def _block_diag_t(m):
    per = SSM_STATE_W // SSM_LANE_BLOCKS // 64
    m5 = m.reshape(SSM_LANE_BLOCKS, per, 16, per, 64)
    d = jnp.einsum("jgchn,gh->jgcn", m5, jnp.eye(per, dtype=m.dtype))
    return d.reshape(SSM_LANE_BLOCKS * per, 16, 64).transpose(1, 0, 2)


def _cmul(ar, ai, br, bi):
    return ar * br - ai * bi, ar * bi + ai * br


def _power_tables(ar, ai, reverse):
    width = ar.shape[1]
    row = lax.broadcasted_iota(jnp.int32, (8, width), 0)
    pows = [(ar, ai)]
    for _ in range(7):
        pows.append(_cmul(pows[-1][0], pows[-1][1], ar, ai))
    steps = []
    for k in (1, 2, 4):
        keep = (row >= k) if not reverse else (row < 8 - k)
        steps.append((jnp.where(keep, pows[k - 1][0], 0.0), jnp.where(keep, pows[k - 1][1], 0.0)))
    cr = jnp.zeros((8, width), F32)
    ci = jnp.zeros((8, width), F32)
    for i in range(8):
        pr, pi = pows[i] if not reverse else pows[7 - i]
        cr = jnp.where(row == i, pr, cr)
        ci = jnp.where(row == i, pi, ci)
    return steps, (cr, ci)


SCAN_CHUNK = 2048
STATE_BLOCK = SSM_STATE_W // SSM_LANE_BLOCKS
CHAN_BLOCK = SSM_W // SSM_LANE_BLOCKS


def ssm_fwd(u, ab_re, ab_im, bb_re, bb_im, cb_re, cb_im, d_skip, comm=None):
    nt = SEQ // SCAN_CHUNK
    chan = pl.BlockSpec((None, SCAN_CHUNK, CHAN_BLOCK), lambda b, j, t: (b, t, j))
    state = pl.BlockSpec((None, SCAN_CHUNK, STATE_BLOCK), lambda b, j, t: (b, t, j))
    mat = pl.BlockSpec((None, CHAN_BLOCK, STATE_BLOCK), lambda b, j, t: (j, 0, 0))
    lane = pl.BlockSpec((1, STATE_BLOCK), lambda b, j, t: (0, j))
    dsp = pl.BlockSpec((1, CHAN_BLOCK), lambda b, j, t: (0, j))

    def body(u_ref, ar_ref, ai_ref, bbr_ref, bbi_ref, cbr_ref, cbi_ref, d_ref, y_ref, yg_ref, xr_ref, xi_ref,
             car_r, car_i):
        @pl.when(pl.program_id(2) == 0)
        def _():
            car_r[...] = jnp.zeros_like(car_r)
            car_i[...] = jnp.zeros_like(car_i)

        steps, (pr, pi) = _power_tables(ar_ref[...], ai_ref[...], reverse=False)
        uf = u_ref[...]
        ub = uf.astype(BF16)
        xr_ref[...] = _dot(ub, bbr_ref[...], "nn")
        xi_ref[...] = _dot(ub, bbi_ref[...], "nn")

        def tile(i, carry):
            cr, ci = carry
            sl = pl.ds(pl.multiple_of(i * 8, 8), 8)
            br, bi = xr_ref[sl, :], xi_ref[sl, :]
            for k, (sr, si) in zip((1, 2, 4), steps):
                tr, ti = _cmul(sr, si, pltpu.roll(br, k, 0), pltpu.roll(bi, k, 0))
                br, bi = br + tr, bi + ti
            tr, ti = _cmul(pr, pi, cr, ci)
            br, bi = br + tr, bi + ti
            xr_ref[sl, :] = br
            xi_ref[sl, :] = bi
            return br[7:8, :], bi[7:8, :]

        cr, ci = lax.fori_loop(0, SCAN_CHUNK // 8, tile, (car_r[0:1, :], car_i[0:1, :]), unroll=4)
        car_r[0:1, :] = cr
        car_i[0:1, :] = ci
        y = (_dot(xr_ref[...].astype(BF16), cbr_ref[...], "nt") - _dot(xi_ref[...].astype(BF16), cbi_ref[...], "nt")
             + d_ref[...] * uf)
        y_ref[...] = y
        yg_ref[...] = jax.nn.gelu(y).astype(BF16)

    return hosted_call(
        body, comm, "ssm_fwd", (LOCAL_BATCH, SSM_LANE_BLOCKS, nt),
        [chan, lane, lane, mat, mat, mat, mat, dsp], [chan, chan, state, state],
        [jax.ShapeDtypeStruct((LOCAL_BATCH, SEQ, SSM_W), F32), jax.ShapeDtypeStruct((LOCAL_BATCH, SEQ, SSM_W), BF16),
         jax.ShapeDtypeStruct((LOCAL_BATCH, SEQ, SSM_STATE_W), F32),
         jax.ShapeDtypeStruct((LOCAL_BATCH, SEQ, SSM_STATE_W), F32)],
        [pltpu.VMEM((8, STATE_BLOCK), F32), pltpu.VMEM((8, STATE_BLOCK), F32)],
        (u, ab_re, ab_im, bb_re, bb_im, cb_re, cb_im, d_skip), ("parallel", "parallel", "arbitrary"))


def ssm_bwd(dyg, y, u, xr, xi, ab_re, ab_im, bb_re, bb_im, cb_re, cb_im, d_skip, comm=None):
    nt = SEQ // SCAN_CHUNK
    ntile = SCAN_CHUNK // 8

    def rev(t):
        return nt - 1 - t

    chan = pl.BlockSpec((None, SCAN_CHUNK, CHAN_BLOCK), lambda j, b, t: (b, rev(t), j))
    state = pl.BlockSpec((None, SCAN_CHUNK, STATE_BLOCK), lambda j, b, t: (b, rev(t), j))
    before = pl.BlockSpec((None, 8, STATE_BLOCK), lambda j, b, t: (b, jnp.maximum(rev(t) * ntile - 1, 0), j))
    mat = pl.BlockSpec((None, CHAN_BLOCK, STATE_BLOCK), lambda j, b, t: (j, 0, 0))
    lane = pl.BlockSpec((1, STATE_BLOCK), lambda j, b, t: (0, j))
    lane8 = pl.BlockSpec((8, STATE_BLOCK), lambda j, b, t: (0, j))
    dsp = pl.BlockSpec((1, CHAN_BLOCK), lambda j, b, t: (0, j))

    def body(dyg_ref, y_ref, u_ref, xr_ref, xi_ref, xrb_ref, xib_ref, ar_ref, ai_ref, bbr_ref, bbi_ref, cbr_ref,
             cbi_ref, d_ref, du_ref, dcbr_ref, dcbi_ref, dbbr_ref, dbbi_ref, dd_ref, dar_ref, dai_ref,
             lam_r, lam_i, car_r, car_i):
        b, t = pl.program_id(1), pl.program_id(2)
        first = jnp.logical_and(b == 0, t == 0)

        @pl.when(t == 0)
        def _():
            car_r[...] = jnp.zeros_like(car_r)
            car_i[...] = jnp.zeros_like(car_i)

        @pl.when(first)
        def _():
            for r in (dcbr_ref, dcbi_ref, dbbr_ref, dbbi_ref, dd_ref, dar_ref, dai_ref):
                r[...] = jnp.zeros_like(r)

        steps, (pr, pi) = _power_tables(ar_ref[...], -ai_ref[...], reverse=True)
        uf = u_ref[...]
        _, gelu_vjp = jax.vjp(jax.nn.gelu, y_ref[...])
        dy = gelu_vjp(dyg_ref[...])[0]
        dyb = dy.astype(BF16)
        dd_ref[...] += _colsum(dy * uf)
        lam_r[...] = _dot(dyb, cbr_ref[...], "nn")
        lam_i[...] = -_dot(dyb, cbi_ref[...], "nn")
        dcbr_ref[...] += _dot(dyb, xr_ref[...].astype(BF16), "tn")
        dcbi_ref[...] -= _dot(dyb, xi_ref[...].astype(BF16), "tn")
        row0 = lax.broadcasted_iota(jnp.int32, (8, STATE_BLOCK), 0) == 0
        has_before = rev(t) > 0
        xrb = jnp.where(has_before, xrb_ref[...], 0.0)
        xib = jnp.where(has_before, xib_ref[...], 0.0)

        def tile(s, carry):
            cr, ci, acc_r, acc_i = carry
            i = ntile - 1 - s
            sl = pl.ds(pl.multiple_of(i * 8, 8), 8)
            gr, gi = lam_r[sl, :], lam_i[sl, :]
            for k, (sr, si) in zip((1, 2, 4), steps):
                tr, ti = _cmul(sr, si, pltpu.roll(gr, 8 - k, 0), pltpu.roll(gi, 8 - k, 0))
                gr, gi = gr + tr, gi + ti
            tr, ti = _cmul(pr, pi, cr, ci)
            gr, gi = gr + tr, gi + ti
            lam_r[sl, :] = gr
            lam_i[sl, :] = gi
            sp = pl.ds(pl.multiple_of(jnp.maximum(i - 1, 0) * 8, 8), 8)
            pvr = jnp.where(i > 0, xr_ref[sp, :], xrb)
            pvi = jnp.where(i > 0, xi_ref[sp, :], xib)
            xsr = jnp.where(row0, pltpu.roll(pvr, 1, 0), pltpu.roll(xr_ref[sl, :], 1, 0))
            xsi = jnp.where(row0, pltpu.roll(pvi, 1, 0), pltpu.roll(xi_ref[sl, :], 1, 0))
            acc_r = acc_r + xsr * gr + xsi * gi
            acc_i = acc_i + xsr * gi - xsi * gr
            return gr[0:1, :], gi[0:1, :], acc_r, acc_i

        zero = jnp.zeros((8, STATE_BLOCK), F32)
        cr, ci, acc_r, acc_i = lax.fori_loop(0, ntile, tile, (car_r[0:1, :], car_i[0:1, :], zero, zero), unroll=2)
        car_r[0:1, :] = cr
        car_i[0:1, :] = ci
        dar_ref[...] += acc_r
        dai_ref[...] += acc_i
        lrb, lib = lam_r[...].astype(BF16), lam_i[...].astype(BF16)
        du = _dot(lrb, bbr_ref[...], "nt") + _dot(lib, bbi_ref[...], "nt") + d_ref[...] * dy
        du_ref[...] = du.astype(BF16)
        ub = uf.astype(BF16)
        dbbr_ref[...] += _dot(ub, lrb, "tn")
        dbbi_ref[...] += _dot(ub, lib, "tn")

    mat_shape = jax.ShapeDtypeStruct((SSM_LANE_BLOCKS, CHAN_BLOCK, STATE_BLOCK), F32)
    return hosted_call(
        body, comm, "ssm_bwd", (SSM_LANE_BLOCKS, LOCAL_BATCH, nt),
        [chan, chan, chan, state, state, before, before, lane, lane, mat, mat, mat, mat, dsp],
        [chan, mat, mat, mat, mat, dsp, lane8, lane8],
        [jax.ShapeDtypeStruct((LOCAL_BATCH, SEQ, SSM_W), BF16), mat_shape, mat_shape, mat_shape, mat_shape,
         jax.ShapeDtypeStruct((1, SSM_W), F32), jax.ShapeDtypeStruct((8, SSM_STATE_W), F32),
         jax.ShapeDtypeStruct((8, SSM_STATE_W), F32)],
        [pltpu.VMEM((SCAN_CHUNK, STATE_BLOCK), F32), pltpu.VMEM((SCAN_CHUNK, STATE_BLOCK), F32),
         pltpu.VMEM((8, STATE_BLOCK), F32), pltpu.VMEM((8, STATE_BLOCK), F32)],
        (dyg, y, u, xr, xi, xr, xi, ab_re, ab_im, bb_re, bb_im, cb_re, cb_im, d_skip),
        ("parallel", "arbitrary", "arbitrary"))


def _merge_fn(g0, g1, attn_d, za, zb):
    return jax.nn.sigmoid(g0) * attn_d + jax.nn.sigmoid(g1) * (za * jax.nn.sigmoid(zb))


def _swiglu_fn(a, b):
    return jax.nn.silu(a) * b


def _own_slot(slots, shard):
    me = 2 * lax.axis_index("x") + lax.axis_index("y")
    mine = lax.broadcasted_iota(jnp.int32, (N_CHIPS, 1, 1), 0) == me
    return jnp.where(mine, shard[None], slots)


def _reduce_start(names, gw, shard_shapes):
    return swap_comm([_to_slots(n, gw[n], shard_shapes[n]) for n in names])


def _reduce_chip(names, slots, got, core):
    return exchange_comm([add_halves(n, g, r, core) for n, g, r in zip(names, slots, got)])


def local_step(x, target, shards, small, core):
    g_mix, g_ffn, g_final = small["norm_mix_g"], small["norm_ffn_g"], small["norm_final_g"]
    tables = _rope_tables()
    seqs = lambda t: t.reshape(LOCAL_BATCH, SEQ, t.shape[-1])
    toks = lambda t: t.reshape(TOKENS, t.shape[-1])
    shard_shapes = {n: s.shape for n, s in shards.items()}
    w = {}

    def gather(names):
        return gather_comm([shards[n] for n in names])

    def arrived(names, slots, own=None):
        for n, s in zip(names, slots):
            w[n] = _from_slots(n, s if own is None else _own_slot(s, own))

    later = [n for n in BIG if n != "w_in"]
    sems, w_in_shard, land, token = split_start(shards["w_in"].astype(BF16), "w_in_gather_start")
    zero = token[0, 0]
    h, *rest = first_norm(x, g_mix + zero, [shards[n] for n in later])
    shards = dict(shards)
    shards.update(zip(later, rest))
    br_t = small["ssm_b_re"].transpose(2, 0, 1)
    bi_t = small["ssm_b_im"].transpose(2, 0, 1)
    log_dt = small["ssm_log_dt"].reshape(32, 1)
    ab_re, ab_im, bb_re_t, bb_im_t = ssm_prep(small["ssm_a_re"] + zero, small["ssm_a_im"], log_dt, br_t, bi_t)
    ab = [ab_re.reshape(1, SSM_STATE_W), ab_im.reshape(1, SSM_STATE_W)]
    bb = [_block_diag(bb_re_t).astype(BF16), _block_diag(bb_im_t).astype(BF16)]
    cb = [_block_diag((small["ssm_c_re"] + zero).transpose(1, 0, 2)).astype(BF16),
          _block_diag((small["ssm_c_im"] + zero).transpose(1, 0, 2)).astype(BF16)]
    d_skip = small["ssm_d"].reshape(1, SSM_W)
    w_in_shard, land = split_wait(sems, w_in_shard, land, [h] + bb + cb, "w_in_gather_wait")
    arrived(["w_in"], [handover(land, "w_in_handover")], own=w_in_shard)
    w_qkv, w_u, w_gate = _qkv_order(w["w_in"][:QKV_W]), w["w_in"][QKV_W:QKV_W + SSM_W], w["w_in"][QKV_W + SSM_W:]
    qkv, *slots = proj_qkv(h, w_qkv, tables, comm=gather(["w_attn_out", "w_glu"]))
    arrived(["w_attn_out", "w_glu"], slots)
    qkv = seqs(qkv)
    u = seqs(matmul(h, w_u, "nt", F32, "proj_u"))
    gl, *slots = matmul(h, w_gate, "nt", BF16, "proj_gate", comm=gather(["w_out"]))
    arrived(["w_out"], slots)
    attn_b, attn, lse, *slots = attn_fwd(qkv, comm=gather(["w_ffn_gate"]))
    arrived(["w_ffn_gate"], slots)
    attn_b = toks(attn_b)
    y, yg, xr, xi, *slots = ssm_fwd(u, *ab, *bb, *cb, d_skip, comm=gather(["w_ffn_up"]))
    arrived(["w_ffn_up"], slots)
    yg2 = toks(yg)
    merged, x1, h2 = merge_out_proj(gl, attn_b, w["w_attn_out"], yg2, w["w_glu"], w["w_out"], x, g_ffn)
    a, b, act, *slots = ffn_in(h2, w["w_ffn_gate"], w["w_ffn_up"], comm=gather(["w_ffn_down"]))
    arrived(["w_ffn_down"], slots)

    def final_fn(xv, g, tgt):
        yv, vjp = jax.vjp(_rms, xv, g)
        err = yv - tgt
        dx, dg = vjp(err * (1.0 / D_MODEL))
        loss = 0.5 * jnp.sum(jnp.mean(err * err, axis=-1, keepdims=True), axis=0, keepdims=True)
        return dx, dx, dg, jnp.broadcast_to(loss, (1, LANES))

    dx2, dx2_b, dg_final, loss = matmul_rows(act, w["w_ffn_down"], "ffn_down_loss", final_fn, [g_final, target],
                                             [(D_MODEL, F32), (D_MODEL, BF16)], accs=(D_MODEL, LANES), add=x1)
    gw, parts = {}, {}
    gw["w_ffn_down"] = matmul(act, dx2_b, "tn", F32, "d_ffn_down")
    da_b, db_b = ffn_in_bwd(dx2_b, w["w_ffn_down"], a, b)
    gw["w_ffn_gate"] = matmul(da_b, h2, "tn", F32, "d_ffn_gate")
    gw["w_ffn_up"] = matmul(db_b, h2, "tn", F32, "d_ffn_up")
    ffn = ["w_ffn_down", "w_ffn_gate", "w_ffn_up"]
    swap = _reduce_start(ffn, gw, shard_shapes)

    def norm_bwd(dh, xv, g, skip):
        _, vjp = jax.vjp(_rms, xv, g)
        dx, dg = vjp(dh)
        dx = dx + skip
        return dx, dx, dg

    dx1, dx1_b, dg_ffn, *got = matmul_rows(
        [da_b, db_b], [w["w_ffn_gate"], w["w_ffn_up"]], "d_h2_norm", norm_bwd, [x1, g_ffn, dx2],
        [(D_MODEL, F32), (D_MODEL, BF16)], accs=(D_MODEL,), comm=swap, tm=256)
    ffn_exchange = [_reduce_chip(ffn[:2], swap.ins[:2], got[:2], core)]
    ffn_up_exchange = _reduce_chip(ffn[2:], swap.ins[2:], got[2:], core)
    gw["w_out"] = matmul(merged, dx1_b, "tn", F32, "d_out")
    dgl_b, dattn_d_b, dz_b, dattn, dyg, parts["w_ffn_up"] = merge_bwd(
        dx1_b, w["w_out"], gl, attn_b, w["w_attn_out"], yg2, w["w_glu"], comm=ffn_up_exchange)
    dattn, dyg = seqs(dattn), seqs(dyg)
    gw["w_attn_out"] = matmul(attn_b, dattn_d_b, "tn", F32, "d_attn_out")
    gw["w_glu"] = matmul(yg2, dz_b, "tn", F32, "d_glu")
    mixer = ["w_out", "w_attn_out", "w_glu"]
    swap = _reduce_start(mixer, gw, shard_shapes)
    du_b, dcb_re, dcb_im, dbb_re, dbb_im, dd, da_re8, da_im8, *rest = ssm_bwd(
        dyg, y, u, xr, xi, *ab, *bb, *cb, d_skip, comm=join_comms(ffn_exchange + [swap]))
    for n, p in zip(ffn[:2], rest[:2]):
        parts[n] = p
    mixer_exchange = _reduce_chip(mixer, swap.ins, rest[2:], core)
    du_b = toks(du_b)
    g_ab_re = jnp.sum(da_re8, axis=0).reshape(32, 64)
    g_ab_im = jnp.sum(da_im8, axis=0).reshape(32, 64)
    d_lr, d_li, d_ldt, d_br_t, d_bi_t = ssm_prep_bwd(
        small["ssm_a_re"], small["ssm_a_im"], log_dt, br_t, bi_t,
        g_ab_re, g_ab_im, _block_diag_t(dbb_re), _block_diag_t(dbb_im))
    as_gcn = lambda t: t.transpose(1, 0, 2).reshape(SSM_W, 64)
    gs = {
        "ssm_a_re": d_lr, "ssm_a_im": d_li, "ssm_log_dt": d_ldt.reshape(1, 32),
        "ssm_b_re": as_gcn(d_br_t), "ssm_b_im": as_gcn(d_bi_t),
        "ssm_c_re": as_gcn(_block_diag_t(dcb_re)), "ssm_c_im": as_gcn(_block_diag_t(dcb_im)),
        "ssm_d": dd.reshape(32, 16).T,
    }
    ssm_gather = small_comm([gs[n] for n in SSM_SMALL])
    dqkv_b, *rest = attn_bwd(qkv, tables, dattn, attn, lse, comm=join_comms([mixer_exchange, ssm_gather]))
    for n, p in zip(mixer, rest):
        parts[n] = p
    ssm_shares = rest[len(mixer):]
    dqkv_b = toks(dqkv_b)
    d_qkv = matmul(dqkv_b, h, "tn", F32, "d_w_qkv")
    d_u = matmul(du_b, h, "tn", F32, "d_w_u")
    d_gate = matmul(dgl_b, h, "tn", F32, "d_w_gate")
    gw["w_in"] = jnp.concatenate([_qkv_order(d_qkv, back=True), d_u, d_gate], axis=0)
    swap = _reduce_start(["w_in"], gw, shard_shapes)
    dh, *got = matmul(dqkv_b, w_qkv, "nn", F32, "d_h_qkv", comm=swap)
    chip_sum = add_halves("w_in", swap.ins[0], got[0], core)
    sems, chip_sum, land, token = split_start(chip_sum, "w_in_reduce_start", per_chip=True)
    grad_x, dg_mix = mix_in_bwd([du_b, dgl_b], [w_u, w_gate], dh, x, g_mix + token[0, 0], dx1)
    gs_norm = {"norm_mix_g": dg_mix, "norm_ffn_g": dg_ffn, "norm_final_g": dg_final}
    return loss, grad_x, parts, ssm_shares, gs_norm, (sems, chip_sum, land)


ANY = pl.BlockSpec(memory_space=pl.ANY)
BIG = ("w_in", "w_glu", "w_attn_out", "w_out", "w_ffn_gate", "w_ffn_up", "w_ffn_down")
TRANSPOSED = ("w_in", "w_ffn_gate", "w_ffn_up")
ROW_SHARDED = TRANSPOSED + ("w_out", "w_ffn_down")
SMALL = ("norm_mix_g", "ssm_a_re", "ssm_a_im", "ssm_log_dt", "ssm_b_re", "ssm_b_im", "ssm_c_re", "ssm_c_im",
         "ssm_d", "norm_ffn_g", "norm_final_g")
WEIGHTS = ("norm_mix_g", "w_in", "ssm_a_re", "ssm_a_im", "ssm_log_dt", "ssm_b_re", "ssm_b_im", "ssm_c_re",
           "ssm_c_im", "ssm_d", "w_glu", "w_attn_out", "w_out", "norm_ffn_g", "w_ffn_gate", "w_ffn_up",
           "w_ffn_down", "norm_final_g")
SSM_SMALL = SMALL[1:9]
NORM_SMALL = (SMALL[0],) + SMALL[9:]
NORM_ROWS = 32
NORM_HOST = "w_ffn_gate"
N_BIG = len(BIG)


def _position():
    return lax.axis_index("x"), lax.axis_index("y"), lax.axis_index("c")


def _other_chips(x, y):
    return [(1 - x, y), (x, 1 - y), (1 - x, 1 - y)]


def _remote(src, dst, send_sem, recv_sem, device):
    return pltpu.make_async_remote_copy(src_ref=src, dst_ref=dst, send_sem=send_sem, recv_sem=recv_sem,
                                        device_id=device, device_id_type=MESH)


_later = functools.partial


def _two_level_phases(copies):
    def first(*refs):
        locals_, sends, _, _, _ = copies(*refs)
        for cp in locals_ + sends:
            cp().start()

    def mid(*refs):
        _, _, arrived, passed, _ = copies(*refs)
        for got, cp in zip(arrived, passed):
            got().wait_recv()
            cp().start()

    def last(*refs):
        locals_, sends, _, passed, from_sibling = copies(*refs)
        for cp in from_sibling:
            cp().wait_recv()
        for cp in sends + passed:
            cp().wait_send()
        for cp in locals_:
            cp().wait()

    return first, mid, last


def _half(ref, chip, which):
    rows = ref.shape[1] // 2
    return ref.at[chip, pl.ds(which * rows, rows), :]


class Comm:
    def __init__(self, ins, out_shapes, sems, first, mid, last):
        self.ins, self.out_shapes, self.sems = list(ins), list(out_shapes), list(sems)
        self.first, self.mid, self.last = first, mid, last


def join_comms(comms):
    def cut(refs_by_kind):
        offs, parts = [0, 0, 0], []
        for cm in comms:
            sizes = (len(cm.ins), len(cm.out_shapes), len(cm.sems))
            parts.append(tuple(refs_by_kind[k][offs[k]:offs[k] + sizes[k]] for k in range(3)))
            offs = [o + s for o, s in zip(offs, sizes)]
        return parts

    def phase(which):
        def run(ins, outs, sems):
            for cm, part in zip(comms, cut((ins, outs, sems))):
                fn = getattr(cm, which)
                if fn is not None:
                    fn(*part)
        return run

    return Comm(sum((cm.ins for cm in comms), []), sum((cm.out_shapes for cm in comms), []),
                sum((cm.sems for cm in comms), []), phase("first"), phase("mid"), phase("last"))


def _comm_operands(comm):
    if comm is None:
        return [], [], []
    return comm.ins, comm.out_shapes, comm.sems


def _comm_begin(comm, refs, step, n_steps):
    if comm is None:
        return
    pl.when(step == 0)(lambda: comm.first(*refs))
    if comm.mid is not None:
        pl.when(step == (n_steps * 3) // 4)(lambda: comm.mid(*refs))


def _comm_end(comm, refs, step, n_steps):
    if comm is not None:
        pl.when(step == n_steps - 1)(lambda: comm.last(*refs))


def _comm_refs(comm, refs, n_in, n_out):
    if comm is None:
        return list(refs), None
    ci, co, cs = len(comm.ins), len(comm.out_shapes), len(comm.sems)
    o0 = n_in + ci
    s0 = o0 + n_out + co
    host = list(refs[:n_in]) + list(refs[o0:o0 + n_out]) + list(refs[s0:len(refs) - cs])
    return host, (list(refs[n_in:o0]), list(refs[o0 + n_out:s0]), list(refs[len(refs) - cs:]))


def run_comm(comm, name):
    n_in, n_out = len(comm.ins), len(comm.out_shapes)

    def body(*refs):
        parts = (list(refs[:n_in]), list(refs[n_in:n_in + n_out]), list(refs[n_in + n_out:]))
        comm.first(*parts)
        if comm.mid is not None:
            comm.mid(*parts)
        comm.last(*parts)

    return pl.pallas_call(body, name=name, in_specs=[ANY] * n_in, out_specs=[ANY] * n_out,
                          out_shape=comm.out_shapes, scratch_shapes=comm.sems)(*comm.ins)


def hosted_call(work, comm, name, grid, in_specs, out_specs, out_shape, scratch_shapes, args, semantics):
    c_ins, c_outs, c_sems = _comm_operands(comm)
    n_steps = math.prod(grid)

    def body(*refs):
        host, c_refs = _comm_refs(comm, refs, len(in_specs), len(out_specs))
        step = 0
        for axis, size in enumerate(grid):
            step = step * size + pl.program_id(axis)
        _comm_begin(comm, c_refs, step, n_steps)
        work(*host)
        _comm_end(comm, c_refs, step, n_steps)

    return pl.pallas_call(
        body, name=name, grid=grid, in_specs=list(in_specs) + [ANY] * len(c_ins),
        out_specs=list(out_specs) + [ANY] * len(c_outs), out_shape=list(out_shape) + c_outs,
        scratch_shapes=list(scratch_shapes) + c_sems,
        compiler_params=_params(semantics if comm is None else ("arbitrary",) * len(grid)),
    )(*args, *c_ins)


def gather_comm(shards):
    n = len(shards)

    def copies(srcs, outs, sems):
        send_sems, recv_sems, local_sems = sems
        x, y, c = _position()
        me = 2 * x + y
        sibling = (x, y, 1 - c)
        chips = _other_chips(x, y)
        locals_ = [_later(pltpu.make_async_copy, s, o.at[me], local_sems.at[i])
                   for i, (s, o) in enumerate(zip(srcs, outs))]
        sends, arrived, passed, from_sibling = [], [], [], []
        for j, (px, py) in enumerate(chips):
            for i, (s, o) in enumerate(zip(srcs, outs)):
                rows = s.shape[0] // 2
                sends.append(_later(_remote, s.at[pl.ds(c * rows, rows), :], _half(o, me, c), send_sems.at[i, j],
                                    recv_sems.at[i, j], (px, py, c)))
                got = _half(o, 2 * px + py, c)
                arrived.append(_later(_remote, got, got, send_sems.at[i, j], recv_sems.at[i, j], (px, py, c)))
                passed.append(_later(_remote, got, got, send_sems.at[i, 3 + j], recv_sems.at[i, 3 + j], sibling))
                other = _half(o, 2 * px + py, 1 - c)
                from_sibling.append(_later(_remote, other, other, send_sems.at[i, 3 + j], recv_sems.at[i, 3 + j],
                                           sibling))
        return locals_, sends, arrived, passed, from_sibling

    return Comm(shards, [jax.ShapeDtypeStruct((N_CHIPS,) + s.shape, s.dtype) for s in shards],
                [pltpu.SemaphoreType.DMA((n, 6)), pltpu.SemaphoreType.DMA((n, 6)), pltpu.SemaphoreType.DMA((n,))],
                *_two_level_phases(copies))


HBM = pl.BlockSpec(memory_space=pltpu.HBM)
SEM = pl.BlockSpec(memory_space=pltpu.SEMAPHORE)
N_OTHER = N_CHIPS - 1


def _ici_halves(src_ref, land_ref, sems, per_chip):
    x, y, c = _position()
    me = 2 * x + y
    rows = land_ref.shape[1] // 2
    sends, arrivals = [], []
    for j, (px, py) in enumerate(_other_chips(x, y)):
        piece = src_ref.at[2 * px + py] if per_chip else src_ref.at[pl.ds(c * rows, rows), :]
        sends.append(_later(_remote, piece, _half(land_ref, me, c), sems[j], sems[N_OTHER + j], (px, py, c)))
        got = _half(land_ref, 2 * px + py, c)
        arrivals.append(_later(_remote, got, got, sems[j], sems[N_OTHER + j], (px, py, c)))
    return sends, arrivals


def split_start(src, name, per_chip=False):
    def body(src_ref, land_ref, *rest):
        sems, token = rest[:2 * N_OTHER], rest[-1]
        for cp in _ici_halves(src_ref, land_ref, sems, per_chip)[0]:
            cp().start()
        token[...] = jnp.zeros_like(token)

    rows, cols = (2 * src.shape[1], src.shape[2]) if per_chip else src.shape
    sem = pltpu.SemaphoreType.DMA(())
    land = (N_CHIPS, rows, cols)
    res = pl.pallas_call(
        body, name=name, in_specs=(HBM, HBM),
        out_specs=(SEM,) * (2 * N_OTHER) + (HBM, HBM, pl.BlockSpec(memory_space=pltpu.VMEM)),
        out_shape=(sem,) * (2 * N_OTHER) + (pltpu.HBM(src.shape, src.dtype), pltpu.HBM(land, src.dtype),
                                           jax.ShapeDtypeStruct((8, LANES), F32)),
        input_output_aliases={0: 2 * N_OTHER, 1: 2 * N_OTHER + 1},
        compiler_params=pltpu.CompilerParams(has_side_effects=pltpu.SideEffectType.DATAFLOW_SIDE_EFFECTING),
    )(pltpu.with_memory_space_constraint(src, pltpu.HBM),
      pltpu.with_memory_space_constraint(lax.empty(land, src.dtype), pltpu.HBM))
    return res[:2 * N_OTHER], res[2 * N_OTHER], res[2 * N_OTHER + 1], res[-1]


def split_wait(sems, src, land, after, name, per_chip=False):
    def body(src_ref, land_ref, *rest):
        sends, arrivals = _ici_halves(src_ref, land_ref, rest[:2 * N_OTHER], per_chip)
        for cp in sends:
            cp().wait_send()
        for cp in arrivals:
            cp().wait_recv()

    return pl.pallas_call(
        body, name=name, in_specs=(HBM, HBM) + (SEM,) * (2 * N_OTHER) + (ANY,) * len(after),
        out_specs=(HBM, HBM), out_shape=(pltpu.HBM(src.shape, src.dtype), pltpu.HBM(land.shape, land.dtype)),
        input_output_aliases={0: 0, 1: 1},
        compiler_params=pltpu.CompilerParams(has_side_effects=pltpu.SideEffectType.DATAFLOW_SIDE_EFFECTING),
    )(src, land, *sems, *after)


def handover(land, name, sums=None):
    n = N_OTHER + (sums is not None)

    def body(*refs):
        land_ref, send_sems, recv_sems = refs[0], refs[-2], refs[-1]
        x, y, c = _position()
        me = 2 * x + y
        sibling = (x, y, 1 - c)
        pieces = [(_half(land_ref, 2 * px + py, c), 2 * px + py) for px, py in _other_chips(x, y)]
        if sums is not None:
            pieces.append((refs[1].at[me], me))
        sends = [_remote(piece, _half(land_ref, chip, c), send_sems.at[j], recv_sems.at[j], sibling)
                 for j, (piece, chip) in enumerate(pieces)]
        for cp in sends:
            cp.start()
        for j, (_, chip) in enumerate(pieces):
            other = _half(land_ref, chip, 1 - c)
            _remote(other, other, send_sems.at[j], recv_sems.at[j], sibling).wait_recv()
        for cp in sends:
            cp.wait_send()

    args = (land,) + ((sums,) if sums is not None else ())
    return pl.pallas_call(
        body, name=name, in_specs=[ANY] * len(args), out_specs=ANY,
        out_shape=jax.ShapeDtypeStruct(land.shape, land.dtype), input_output_aliases={0: 0},
        scratch_shapes=[pltpu.SemaphoreType.DMA((n,)), pltpu.SemaphoreType.DMA((n,))],
    )(*args)


def swap_comm(grads):
    n = len(grads)

    def copies(srcs, gots, sems):
        send_sems, recv_sems = sems
        x, y, c = _position()
        out = []
        for i, (s, o) in enumerate(zip(srcs, gots)):
            rows = s.shape[1] // 2
            out.append(_remote(s.at[:, pl.ds((1 - c) * rows, rows), :], o, send_sems.at[i], recv_sems.at[i],
                               (x, y, 1 - c)))
        return out

    def first(srcs, gots, sems):
        for cp in copies(srcs, gots, sems):
            cp.start()

    def last(srcs, gots, sems):
        for cp in copies(srcs, gots, sems):
            cp.wait()

    return Comm(grads, [jax.ShapeDtypeStruct((N_CHIPS, g.shape[1] // 2, g.shape[2]), g.dtype) for g in grads],
                [pltpu.SemaphoreType.DMA((n,)), pltpu.SemaphoreType.DMA((n,))], first, None, last)


def add_halves(name, g, got, core):
    _, half, cols = got.shape
    mine = pl.BlockSpec((None, half, cols), lambda k, c_ref: (k, c_ref[0], 0))
    other = pl.BlockSpec((None, half, cols), lambda k, c_ref: (k, 0, 0))

    def body(c_ref, g_ref, got_ref, o_ref):
        o_ref[...] = (g_ref[...] + got_ref[...]).astype(BF16)

    return pl.pallas_call(
        body, name="add_halves_" + name,
        grid_spec=pltpu.PrefetchScalarGridSpec(num_scalar_prefetch=1, grid=(N_CHIPS,), in_specs=[mine, other],
                                               out_specs=other),
        out_shape=jax.ShapeDtypeStruct(got.shape, BF16),
        compiler_params=_params(("parallel",)),
    )(core, g, got)


def exchange_comm(parts):
    n = len(parts)

    def copies(srcs, outs, sems):
        send_sems, recv_sems, local_sems = sems
        x, y, c = _position()
        me = 2 * x + y
        sibling = (x, y, 1 - c)
        chips = _other_chips(x, y)
        locals_, sends, arrived, passed, from_sibling = [], [], [], [], []
        for i, (s, o) in enumerate(zip(srcs, outs)):
            locals_.append(_later(pltpu.make_async_copy, s.at[me], _half(o, me, c), local_sems.at[i]))
            sends.append(_later(_remote, s.at[me], _half(o, me, c), send_sems.at[i, 3], recv_sems.at[i, 3], sibling))
            other = _half(o, me, 1 - c)
            from_sibling.append(_later(_remote, other, other, send_sems.at[i, 3], recv_sems.at[i, 3], sibling))
        for j, (px, py) in enumerate(chips):
            for i, (s, o) in enumerate(zip(srcs, outs)):
                sends.append(_later(_remote, s.at[2 * px + py], _half(o, me, c), send_sems.at[i, j],
                                    recv_sems.at[i, j], (px, py, c)))
                got = _half(o, 2 * px + py, c)
                arrived.append(_later(_remote, got, got, send_sems.at[i, j], recv_sems.at[i, j], (px, py, c)))
                passed.append(_later(_remote, got, got, send_sems.at[i, 4 + j], recv_sems.at[i, 4 + j], sibling))
                other = _half(o, 2 * px + py, 1 - c)
                from_sibling.append(_later(_remote, other, other, send_sems.at[i, 4 + j], recv_sems.at[i, 4 + j],
                                           sibling))
        return locals_, sends, arrived, passed, from_sibling

    return Comm(parts, [jax.ShapeDtypeStruct((N_CHIPS, 2 * p.shape[1], p.shape[2]), p.dtype) for p in parts],
                [pltpu.SemaphoreType.DMA((n, 7)), pltpu.SemaphoreType.DMA((n, 7)), pltpu.SemaphoreType.DMA((n,))],
                *_two_level_phases(copies))


def small_comm(shares):
    n = len(shares)

    def copies(srcs, outs, sems):
        send_sems, recv_sems, local_sems = sems
        x, y, c = _position()
        me = 4 * x + 2 * y + c
        flips = [(fx, fy, fc) for fx in (0, 1) for fy in (0, 1) for fc in (0, 1)][1:]
        peers = [(1 - x if fx else x, 1 - y if fy else y, 1 - c if fc else c) for fx, fy, fc in flips]
        locals_, sends, arrived = [], [], []
        for i, (src_ref, out_ref) in enumerate(zip(srcs, outs)):
            locals_.append(_later(pltpu.make_async_copy, src_ref, out_ref.at[me], local_sems.at[i]))
            for j, (px, py, pc) in enumerate(peers):
                sends.append(_later(_remote, src_ref, out_ref.at[me], send_sems.at[i, j], recv_sems.at[i, j],
                                    (px, py, pc)))
                got = out_ref.at[4 * px + 2 * py + pc]
                arrived.append(_later(_remote, got, got, send_sems.at[i, j], recv_sems.at[i, j], (px, py, pc)))
        return locals_, sends, arrived

    def first(*refs):
        locals_, sends, _ = copies(*refs)
        for cp in locals_ + sends:
            cp().start()

    def last(*refs):
        locals_, sends, arrived = copies(*refs)
        for cp in arrived:
            cp().wait_recv()
        for cp in sends:
            cp().wait_send()
        for cp in locals_:
            cp().wait()

    return Comm(shares, [jax.ShapeDtypeStruct((N_DEV,) + s.shape, s.dtype) for s in shares],
                [pltpu.SemaphoreType.DMA((n, 7)), pltpu.SemaphoreType.DMA((n, 7)), pltpu.SemaphoreType.DMA((n,))],
                first, None, last)


def _adam_fn(w, g, m, v):
    m = ADAM_B1 * m + (1.0 - ADAM_B1) * g
    v = ADAM_B2 * v + (1.0 - ADAM_B2) * jnp.square(g)
    m_hat = m / (1.0 - ADAM_B1 ** ADAM_STEP)
    v_hat = v / (1.0 - ADAM_B2 ** ADAM_STEP)
    return -ADAM_LR * (m_hat / (jnp.sqrt(v_hat) + ADAM_EPS) + ADAM_WD * w), m, v


def adam_big(name, parts, w, m, v, comm=None):
    rows, cols = w.shape
    tm = _pick(rows, 384, 16)

    def fn(p0, p1, p2, p3, wv, mv, vv):
        g = ((p0.astype(F32) + p1.astype(F32)) + p2.astype(F32)) + p3.astype(F32)
        return (g,) + _adam_fn(wv, g, mv, vv)

    return rowwise(fn, [parts, w, m, v], [(cols, F32)] * 4, "adam_" + name, tm=tm, rows=rows, comm=comm)


def adam_small(name, gathered, w, m, v):
    def body(g_ref, w_ref, m_ref, v_ref, go_ref, d_ref, mo_ref, vo_ref):
        g = g_ref[0]
        for k in range(1, N_DEV):
            g = g + g_ref[k]
        go_ref[...] = g
        d_ref[...], mo_ref[...], vo_ref[...] = _adam_fn(w_ref[...], g, m_ref[...], v_ref[...])

    return pl.pallas_call(body, name=name, out_shape=[jax.ShapeDtypeStruct(w.shape, F32)] * 4,
                          compiler_params=_params())(gathered, w, m, v)


def _ssm_2d(name, t):
    t = t[0] if t.ndim > 2 else t
    if name in ("ssm_b_re", "ssm_b_im"):
        return t.transpose(0, 2, 1).reshape(SSM_W, 64)
    if name in ("ssm_c_re", "ssm_c_im"):
        return t.reshape(SSM_W, 64)
    return t.T if name == "ssm_d" else t


def _ssm_back(name, t):
    if name in ("ssm_b_re", "ssm_b_im"):
        return t.reshape(32, 16, 64).transpose(0, 2, 1)[None]
    if name in ("ssm_c_re", "ssm_c_im"):
        return t.reshape(1, 32, 16, 64)
    if name == "ssm_d":
        return t.T[None]
    return t if name == "ssm_log_dt" else t[None]


def adam_ssm(shares, w, m, v):
    n = len(w)

    def body(*refs):
        ins, outs = refs[:4 * n], refs[4 * n:]
        for i in range(n):
            g_ref, w_ref, m_ref, v_ref = (ins[k * n + i] for k in range(4))
            g = g_ref[0]
            for k in range(1, N_DEV):
                g = g + g_ref[k]
            outs[4 * i][...] = g
            outs[4 * i + 1][...], outs[4 * i + 2][...], outs[4 * i + 3][...] = _adam_fn(w_ref[...], g, m_ref[...],
                                                                                      v_ref[...])

    out_shape = [jax.ShapeDtypeStruct(t.shape, F32) for t in w for _ in range(4)]
    res = pl.pallas_call(body, name="adam_ssm", out_shape=out_shape, compiler_params=_params())(*shares, *w, *m, *v)
    return [res[4 * i:4 * i + 4] for i in range(n)]


def _pack_small(names, vals, rows, last=None):
    flat = [vals[n].reshape(-1) for n in names]
    if last is not None:
        flat.append(last.reshape(-1))
    flat = jnp.concatenate(flat)
    return jnp.pad(flat, (0, rows * LANES - flat.shape[0])).reshape(rows, LANES)


def _unpack_small(names, pack, shapes):
    flat, out, off = pack.reshape(-1), {}, 0
    for n in names:
        size = math.prod(shapes[n])
        out[n] = flat[off:off + size].reshape(shapes[n])
        off += size
    return out, flat[off]


def _to_slots(name, g, shard_shape):
    rows, cols = shard_shape
    if name in ROW_SHARDED:
        return g.reshape(N_CHIPS, rows, cols)
    return g.reshape(rows, N_CHIPS, cols).transpose(1, 0, 2)


def _from_slots(name, s):
    _, rows, cols = s.shape
    if name in ROW_SHARDED:
        return s.reshape(N_CHIPS * rows, cols)
    return s.transpose(1, 0, 2).reshape(rows, N_CHIPS * cols)


def kernel(x, norm_mix_g, w_in, ssm_a_re, ssm_a_im, ssm_log_dt, ssm_b_re, ssm_b_im, ssm_c_re, ssm_c_im, ssm_d, w_glu, w_attn_out, w_out, norm_ffn_g, w_ffn_gate, w_ffn_up, w_ffn_down, norm_final_g, loss_target, m_norm_mix_g, m_w_in, m_ssm_a_re, m_ssm_a_im, m_ssm_log_dt, m_ssm_b_re, m_ssm_b_im, m_ssm_c_re, m_ssm_c_im, m_ssm_d, m_w_glu, m_w_attn_out, m_w_out, m_norm_ffn_g, m_w_ffn_gate, m_w_ffn_up, m_w_ffn_down, m_norm_final_g, v_norm_mix_g, v_w_in, v_ssm_a_re, v_ssm_a_im, v_ssm_log_dt, v_ssm_b_re, v_ssm_b_im, v_ssm_c_re, v_ssm_c_im, v_ssm_d, v_w_glu, v_w_attn_out, v_w_out, v_norm_ffn_g, v_w_ffn_gate, v_w_ffn_up, v_w_ffn_down, v_norm_final_g):
    given = dict(locals())
    def local(name, prefix=""):
        t = given[prefix + name][0]
        return t.T if name in TRANSPOSED else t

    shard = {n: local(n) for n in BIG}
    shapes = {n: given[n].shape for n in WEIGHTS}

    small = {n: given[n] for n in SMALL}
    small_2d = dict(small)
    for n in ("ssm_a_re", "ssm_a_im", "ssm_b_re", "ssm_b_im", "ssm_c_re", "ssm_c_im", "ssm_d"):
        small_2d[n] = small[n][0]
    small_2d["norm_final_g"] = norm_final_g.reshape(1, D_MODEL)

    core = lax.axis_index("c").astype(jnp.int32).reshape(1)
    loss, grad_x, parts, ssm_shares, gs_norm, w_in_reduce = local_step(
        x.reshape(TOKENS, D_MODEL), loss_target.reshape(TOKENS, D_MODEL),
        {n: shard[n] for n in BIG}, small_2d, core)

    norm_gather = small_comm([_pack_small(NORM_SMALL, gs_norm, NORM_ROWS, last=loss)])
    small_out = [{} for _ in range(4)]
    ssm_in = [[_ssm_2d(n, given[p + n]) for n in SSM_SMALL] for p in ("", "m_", "v_")]
    for n, res in zip(SSM_SMALL, adam_ssm(ssm_shares, *ssm_in)):
        for kind, t in enumerate(res):
            small_out[kind][n] = _ssm_back(n, t)

    big_out, updated = {}, {}
    for n in BIG[1:] + BIG[:1]:
        if n == "w_in":
            sems, chip_sum, land = w_in_reduce
            behind = [updated[k][1] for k in BIG[1:]]
            chip_sum, land = split_wait(sems, chip_sum, land, behind, "w_in_reduce_wait", per_chip=True)
            land = handover(land, "w_in_reduce_handover", sums=chip_sum)
            me = 2 * lax.axis_index("x") + lax.axis_index("y")
            parts[n] = lax.dynamic_update_slice(land, lax.dynamic_slice_in_dim(chip_sum, me, 1, 0),
                                                (me, lax.axis_index("c") * chip_sum.shape[1], 0))
        res = adam_big(n, parts[n], shard[n], local(n, "m_"), local(n, "v_"),
                       comm=norm_gather if n == NORM_HOST else None)
        updated[n] = res[:4]
        if n == NORM_HOST:
            norm_shares = res[4]
            packs = [_pack_small(NORM_SMALL, {k: given[p + k] for k in NORM_SMALL}, NORM_ROWS)
                     for p in ("", "m_", "v_")]
            for kind, t in enumerate(adam_small("adam_norm_gains", norm_shares, *packs)):
                vals, after = _unpack_small(NORM_SMALL, t, shapes)
                small_out[kind].update(vals)
                if kind == 0:
                    total_loss = after
        big_out[n] = [(t.T if n in TRANSPOSED else t)[None] for t in updated[n]]

    outs = [total_loss, grad_x.reshape(LOCAL_BATCH, SEQ, D_MODEL)]
    for kind in range(4):
        for n in WEIGHTS:
            outs.append(big_out[n][kind] if n in BIG else small_out[kind][n])
    return tuple(outs)
```

```python
import functools
import math

import jax
import jax.numpy as jnp
import numpy as np
from jax import lax
from jax.experimental import pallas as pl
from jax.experimental.pallas import tpu as pltpu

F32 = jnp.float32
BF16 = jnp.bfloat16
MESH = pl.DeviceIdType.MESH

D_MODEL = 1024
SEQ = 2048
LOCAL_BATCH = 2
TOKENS = LOCAL_BATCH * SEQ
HEAD_DIM = 64
HEADS_PER_GROUP = 4
GROUP_W = HEADS_PER_GROUP * HEAD_DIM
N_GROUPS = 3
DILATIONS = (1, 4, 16)
ATTN_BLOCK = 128
ROPE_DIM = 16
ROPE_THETA = 500000.0
QKV_W = 3 * N_GROUPS * GROUP_W
SSM_W = 512
SSM_STATE_W = 2048
SSM_LANE_BLOCKS = 4
GATE_W = 2 * D_MODEL
D_FF = 2816
RMS_EPS = 1e-6
NEG_INF = -1e30
ADAM_LR, ADAM_B1, ADAM_B2, ADAM_EPS, ADAM_WD, ADAM_STEP = 0.001, 0.9, 0.999, 1e-08, 0.01, 10
N_CHIPS = 4
N_DEV = 8

VMEM_LIMIT = 56 * 1024 * 1024
LANES = 128


def _params(sem=None):
    return pltpu.CompilerParams(dimension_semantics=sem, vmem_limit_bytes=VMEM_LIMIT)


def _pick(n, cap, align=LANES):
    best = None
    for d in range(align, min(n, cap) + 1, align):
        if n % d == 0:
            best = d
    return n if best is None or n <= cap else best


_DIMS = {"nn": (((1,), (0,)), ((), ())), "nt": (((1,), (1,)), ((), ())), "tn": (((0,), (0,)), ((), ()))}


def _dot(a, b, mode):
    return lax.dot_general(a, b, _DIMS[mode], preferred_element_type=F32)


def matmul(a, b, mode, out_dtype, name, add=None, comm=None):
    if mode == "nn":
        (m, k), n = a.shape, b.shape[1]
    elif mode == "nt":
        (m, k), n = a.shape, b.shape[0]
    else:
        (k, m), n = a.shape, b.shape[1]
    tn = _pick(n, 1408 if mode != "tn" else 512)
    tk = _pick(k, 2816) if mode != "tn" else k
    tm = _pick(m, 1408)
    out_bytes = jnp.dtype(out_dtype).itemsize

    def need(tm_):
        return 2 * 2 * (tm_ * tk + tk * tn) + tm_ * tn * (4 + 2 * out_bytes + (8 if add is not None else 0))

    while need(tm) > 40 * 1024 * 1024 and tm % 256 == 0:
        tm //= 2
    nk = k // tk
    a_spec = {"nn": pl.BlockSpec((tm, tk), lambda i, j, kk: (i, kk)),
              "nt": pl.BlockSpec((tm, tk), lambda i, j, kk: (i, kk)),
              "tn": pl.BlockSpec((tk, tm), lambda i, j, kk: (kk, i))}[mode]
    b_spec = {"nn": pl.BlockSpec((tk, tn), lambda i, j, kk: (kk, j)),
              "nt": pl.BlockSpec((tn, tk), lambda i, j, kk: (j, kk)),
              "tn": pl.BlockSpec((tk, tn), lambda i, j, kk: (kk, j))}[mode]
    o_spec = pl.BlockSpec((tm, tn), lambda i, j, kk: (i, j))

    def body(a_ref, b_ref, *rest):
        if add is not None:
            add_ref, o_ref, acc_ref = rest
        else:
            o_ref, acc_ref = rest
        part = _dot(a_ref[...], b_ref[...], mode)
        if nk == 1:
            res = part if add is None else part + add_ref[...]
            o_ref[...] = res.astype(out_dtype)
            return
        kk = pl.program_id(2)

        @pl.when(kk == 0)
        def _():
            acc_ref[...] = part

        @pl.when(kk > 0)
        def _():
            acc_ref[...] += part

        @pl.when(kk == nk - 1)
        def _():
            res = acc_ref[...] if add is None else acc_ref[...] + add_ref[...]
            o_ref[...] = res.astype(out_dtype)

    in_specs = [a_spec, b_spec] + ([o_spec] if add is not None else [])
    args = (a, b) + ((add,) if add is not None else ())
    res = hosted_call(
        body, comm, name, (m // tm, n // tn, nk), in_specs, [o_spec], [jax.ShapeDtypeStruct((m, n), out_dtype)],
        [pltpu.VMEM((tm, tn) if nk > 1 else (8, LANES), F32)], args, ("parallel", "parallel", "arbitrary"))
    return res[0] if comm is None else res


def matmul_rows(a, b, name, fn, extra, outs, accs=(), add=None, comm=None, tm=512):
    a_list, b_list = (list(a), list(b)) if isinstance(a, (list, tuple)) else ([a], [b])
    m, n = a_list[0].shape[0], b_list[0].shape[1]
    n_mm = len(a_list)
    n_fixed = 2 * n_mm + (add is not None)
    row_spec = lambda cols: pl.BlockSpec((tm, cols), lambda i: (i, 0))
    in_specs = [row_spec(t.shape[1]) for t in a_list] + [pl.BlockSpec(t.shape, lambda i: (0, 0)) for t in b_list]
    in_specs += [row_spec(n)] if add is not None else []
    in_specs += [pl.BlockSpec(e.shape, lambda i: (0, 0)) if e.shape[0] == 1 else row_spec(e.shape[1]) for e in extra]
    out_specs = [row_spec(c) for c, _ in outs] + [pl.BlockSpec((1, c), lambda i: (0, 0)) for c in accs]
    out_shape = [jax.ShapeDtypeStruct((m, c), dt) for c, dt in outs] + [jax.ShapeDtypeStruct((1, c), F32) for c in accs]

    def body(*refs):
        rows = _dot(refs[0][...], refs[n_mm][...], "nn")
        for i in range(1, n_mm):
            rows = rows + _dot(refs[i][...], refs[n_mm + i][...], "nn")
        if add is not None:
            rows = rows + refs[2 * n_mm][...]
        n_in = n_fixed + len(extra)
        res = fn(rows, *[r[...] for r in refs[n_fixed:n_in]])
        for r, v in zip(refs[n_in:n_in + len(outs)], res[:len(outs)]):
            r[...] = v.astype(r.dtype)
        first = pl.program_id(0) == 0
        for r, v in zip(refs[n_in + len(outs):], res[len(outs):]):
            @pl.when(first)
            def _(r=r, v=v):
                r[...] = v

            @pl.when(jnp.logical_not(first))
            def _(r=r, v=v):
                r[...] += v

    args = tuple(a_list) + tuple(b_list) + ((add,) if add is not None else ()) + tuple(extra)
    return hosted_call(body, comm, name, (m // tm,), in_specs, out_specs, out_shape, [], args, ("arbitrary",))


def _merge_specs(tm):
    half = lambda blk: pl.BlockSpec((tm, D_MODEL), functools.partial(lambda i, blk_: (i, blk_), blk_=blk))
    return [half(0), half(1), pl.BlockSpec((tm, GROUP_W), lambda i: (i, 0)),
            pl.BlockSpec((GROUP_W, D_MODEL), lambda i: (0, 0)), pl.BlockSpec((tm, SSM_W), lambda i: (i, 0)),
            pl.BlockSpec((SSM_W, GATE_W), lambda i: (0, 0))]


def _merge_operands(g0, g1, at, wa, yg, wg):
    z = _dot(yg[...], wg[...], "nn")
    return (g0[...].astype(F32), g1[...].astype(F32), _dot(at[...], wa[...], "nn"), z[:, :D_MODEL], z[:, D_MODEL:])


def merge_out_proj(gl, attn_b, w_attn_out, yg, w_glu, w_out, x, g_ffn):
    tm = 512

    def body(g0, g1, at, wa, yg_ref, wg, w_ref, x_ref, g_ref, m_ref, x1_ref, h2_ref):
        merged = _merge_fn(*_merge_operands(g0, g1, at, wa, yg_ref, wg)).astype(BF16)
        m_ref[...] = merged
        x1 = _dot(merged, w_ref[...], "nn") + x_ref[...]
        x1_ref[...] = x1
        h2_ref[...] = _rms(x1, g_ref[...]).astype(BF16)

    rows = pl.BlockSpec((tm, D_MODEL), lambda i: (i, 0))
    whole = pl.BlockSpec((D_MODEL, D_MODEL), lambda i: (0, 0))
    gain = pl.BlockSpec((1, D_MODEL), lambda i: (0, 0))
    tok = lambda dt: jax.ShapeDtypeStruct((TOKENS, D_MODEL), dt)
    return pl.pallas_call(
        body, name="merge_out_proj", grid=(TOKENS // tm,), in_specs=_merge_specs(tm) + [whole, rows, gain],
        out_specs=[rows] * 3, out_shape=[tok(BF16), tok(F32), tok(BF16)], compiler_params=_params(("parallel",)),
    )(gl, gl, attn_b, w_attn_out, yg, w_glu, w_out, x, g_ffn)


def merge_bwd(dx1_b, w_out, gl, attn_b, w_attn_out, yg, w_glu, comm=None):
    tm = 512

    def body(dx_ref, w_ref, g0, g1, at, wa, yg_ref, wg, dgl_ref, dad_ref, dz_ref, dat_ref, dyg_ref):
        dm = _dot(dx_ref[...], w_ref[...], "nt")
        _, vjp = jax.vjp(_merge_fn, *_merge_operands(g0, g1, at, wa, yg_ref, wg))
        dg0, dg1, dad, dza, dzb = vjp(dm)
        dat_ref[...] = _dot(dad.astype(BF16), wa[...], "nt")
        dgl_ref[:, :D_MODEL] = dg0.astype(BF16)
        dgl_ref[:, D_MODEL:] = dg1.astype(BF16)
        dad_ref[...] = dad.astype(BF16)
        dz_ref[:, :D_MODEL] = dza.astype(BF16)
        dz_ref[:, D_MODEL:] = dzb.astype(BF16)
        dyg_ref[...] = _dot(dz_ref[...], wg[...], "nt")

    rows = pl.BlockSpec((tm, D_MODEL), lambda i: (i, 0))
    wide = pl.BlockSpec((tm, GATE_W), lambda i: (i, 0))
    whole = pl.BlockSpec((D_MODEL, D_MODEL), lambda i: (0, 0))
    return hosted_call(
        body, comm, "merge_bwd", (TOKENS // tm,), [rows, whole] + _merge_specs(tm),
        [wide, rows, wide, pl.BlockSpec((tm, GROUP_W), lambda i: (i, 0)), pl.BlockSpec((tm, SSM_W), lambda i: (i, 0))],
        [jax.ShapeDtypeStruct((TOKENS, GATE_W), BF16), jax.ShapeDtypeStruct((TOKENS, D_MODEL), BF16),
         jax.ShapeDtypeStruct((TOKENS, GATE_W), BF16), jax.ShapeDtypeStruct((TOKENS, GROUP_W), F32),
         jax.ShapeDtypeStruct((TOKENS, SSM_W), F32)], [],
        (dx1_b, w_out, gl, gl, attn_b, w_attn_out, yg, w_glu), ("arbitrary",))


FFN_TM, FFN_TN = 512, 1408


def ffn_in(h2, wg_t, wu_t, comm=None):
    def body(h_ref, wg_ref, wu_ref, a_ref, b_ref, act_ref):
        hv = h_ref[...]
        a, b = _dot(hv, wg_ref[...], "nt"), _dot(hv, wu_ref[...], "nt")
        a_ref[...] = a.astype(BF16)
        b_ref[...] = b.astype(BF16)
        act_ref[...] = _swiglu_fn(a, b).astype(BF16)

    rows = pl.BlockSpec((FFN_TM, D_MODEL), lambda i, j: (i, 0))
    wts = pl.BlockSpec((FFN_TN, D_MODEL), lambda i, j: (j, 0))
    out = pl.BlockSpec((FFN_TM, FFN_TN), lambda i, j: (i, j))
    return hosted_call(body, comm, "ffn_in", (TOKENS // FFN_TM, D_FF // FFN_TN), [rows, wts, wts], [out] * 3,
                       [jax.ShapeDtypeStruct((TOKENS, D_FF), BF16)] * 3, [], (h2, wg_t, wu_t),
                       ("parallel", "parallel"))


def ffn_in_bwd(dx2_b, wd, a, b):
    def body(dx_ref, wd_ref, a_ref, b_ref, da_ref, db_ref):
        dact = _dot(dx_ref[...], wd_ref[...], "nt")
        av, bv = a_ref[...].astype(F32), b_ref[...].astype(F32)
        sig = jax.nn.sigmoid(av)
        act = av * sig
        da_ref[...] = (dact * bv * (sig * (1.0 + av - act))).astype(BF16)
        db_ref[...] = (dact * act).astype(BF16)

    rows = pl.BlockSpec((FFN_TM, D_MODEL), lambda i, j: (i, 0))
    wts = pl.BlockSpec((FFN_TN, D_MODEL), lambda i, j: (j, 0))
    out = pl.BlockSpec((FFN_TM, FFN_TN), lambda i, j: (i, j))
    return pl.pallas_call(
        body, name="ffn_in_bwd", grid=(TOKENS // FFN_TM, D_FF // FFN_TN), in_specs=[rows, wts, out, out],
        out_specs=[out] * 2, out_shape=[jax.ShapeDtypeStruct((TOKENS, D_FF), BF16)] * 2,
        compiler_params=_params(("parallel", "parallel")),
    )(dx2_b, wd, a, b)


def mix_in_bwd(grads, weights, partial, x, g, skip, comm=None):
    n = len(grads)
    tm = 512

    def body(*refs):
        a_refs, b_refs = refs[:n], refs[n:2 * n]
        part_ref, x_ref, g_ref, skip_ref, gx_ref, dg_ref = refs[2 * n:]
        dh = part_ref[...]
        for a_ref, b_ref in zip(a_refs, b_refs):
            dh = dh + _dot(a_ref[...], b_ref[...], "nn")
        _, vjp = jax.vjp(_rms, x_ref[...], g_ref[...])
        dx, dg = vjp(dh)
        gx_ref[...] = dx + skip_ref[...]
        first = pl.program_id(0) == 0

        @pl.when(first)
        def _():
            dg_ref[...] = dg

        @pl.when(jnp.logical_not(first))
        def _():
            dg_ref[...] += dg

    rows = pl.BlockSpec((tm, D_MODEL), lambda i: (i, 0))
    gain = pl.BlockSpec((1, D_MODEL), lambda i: (0, 0))
    in_specs = [pl.BlockSpec((tm, a.shape[1]), lambda i: (i, 0)) for a in grads]
    in_specs += [pl.BlockSpec(b.shape, lambda i: (0, 0)) for b in weights]
    return hosted_call(
        body, comm, "mix_in_bwd", (TOKENS // tm,), in_specs + [rows, rows, gain, rows], [rows, gain],
        [jax.ShapeDtypeStruct((TOKENS, D_MODEL), F32), jax.ShapeDtypeStruct((1, D_MODEL), F32)], [],
        (*grads, *weights, partial, x, g, skip), ("arbitrary",))


def rowwise(fn, ins, outs, name, accs=(), tm=256, rows=TOKENS, comm=None):
    in_specs, args = [], []
    for item in ins:
        arr, width, blk = item if isinstance(item, tuple) else (item, None, 0)
        if arr.ndim == 3:
            for k in range(arr.shape[0]):
                in_specs.append(pl.BlockSpec((None, tm, arr.shape[2]), functools.partial(lambda i, k_: (k_, i, 0), k_=k)))
                args.append(arr)
            continue
        if arr.shape[0] == 1:
            in_specs.append(pl.BlockSpec(arr.shape, lambda i: (0, 0)))
        elif width is None:
            in_specs.append(pl.BlockSpec((tm, arr.shape[1]), lambda i: (i, 0)))
        else:
            in_specs.append(pl.BlockSpec((tm, width), functools.partial(lambda i, blk_: (i, blk_), blk_=blk)))
        args.append(arr)
    out_specs = [pl.BlockSpec((tm, c), lambda i: (i, 0)) for c, _ in outs]
    out_specs += [pl.BlockSpec((1, c), lambda i: (0, 0)) for c in accs]
    out_shape = [jax.ShapeDtypeStruct((rows, c), dt) for c, dt in outs]
    out_shape += [jax.ShapeDtypeStruct((1, c), F32) for c in accs]
    n_in, n_out = len(args), len(outs)
    c_ins, c_outs, c_sems = _comm_operands(comm)

    def body(*refs):
        refs, c_refs = _comm_refs(comm, refs, n_in, n_out + len(accs))
        step = pl.program_id(0)
        _comm_begin(comm, c_refs, step, rows // tm)
        res = fn(*[r[...] for r in refs[:n_in]])
        for r, v in zip(refs[n_in:n_in + n_out], res[:n_out]):
            r[...] = v.astype(r.dtype)
        first = step == 0
        for r, v in zip(refs[n_in + n_out:], res[n_out:]):
            @pl.when(first)
            def _(r=r, v=v):
                r[...] = v

            @pl.when(jnp.logical_not(first))
            def _(r=r, v=v):
                r[...] += v
        _comm_end(comm, c_refs, step, rows // tm)

    return pl.pallas_call(
        body, name=name, grid=(rows // tm,), in_specs=in_specs + [ANY] * len(c_ins),
        out_specs=out_specs + [ANY] * len(c_outs), out_shape=out_shape + c_outs, scratch_shapes=c_sems,
        compiler_params=_params(("arbitrary",)),
    )(*args, *c_ins)


def first_norm(x, g, others, comm=None):
    tm, n = 256, len(others)

    def body(x_ref, g_ref, *rest):
        srcs, h_ref, dsts = rest[:n], rest[n], rest[n + 1:]
        h_ref[...] = _rms(x_ref[...], g_ref[...]).astype(BF16)
        for k, (s, d) in enumerate(zip(srcs, dsts)):
            @pl.when(pl.program_id(0) == k)
            def _(s=s, d=d):
                d[...] = s[...].astype(BF16)

    rows = pl.BlockSpec((tm, D_MODEL), lambda i: (i, 0))
    whole = [pl.BlockSpec(a.shape, lambda i: (0, 0)) for a in others]
    return hosted_call(
        body, comm, "norm_mix", (TOKENS // tm,), [rows, pl.BlockSpec((1, D_MODEL), lambda i: (0, 0))] + whole,
        [rows] + whole, [jax.ShapeDtypeStruct((TOKENS, D_MODEL), BF16)]
        + [jax.ShapeDtypeStruct(a.shape, BF16) for a in others], [], (x, g, *others), ("arbitrary",))


def _rms(x, g):
    return x * lax.rsqrt(jnp.mean(x * x, axis=-1, keepdims=True) + RMS_EPS) * g


def _colsum(v):
    return jnp.sum(v, axis=0, keepdims=True)


PAIR_W = 2 * HEAD_DIM
N_PAIRS = HEADS_PER_GROUP // 2


def _qkv_order(w_t, back=False):
    dims = (N_PAIRS, N_GROUPS, 3) if back else (3, N_GROUPS, N_PAIRS)
    return w_t.reshape(dims + (PAIR_W, w_t.shape[1])).transpose(2, 1, 0, 3, 4).reshape(QKV_W, w_t.shape[1])


def _rope_tables():
    half = ROPE_DIM // 2
    inv = np.power(np.float32(ROPE_THETA), -np.arange(half, dtype=np.float32) * np.float32(2.0 / ROPE_DIM))
    ang = (np.arange(SEQ, dtype=np.float32)[:, None] * inv[None, :]).astype(np.float32)
    cos, sin = np.cos(ang), np.sin(ang)
    zeros = np.zeros((SEQ, HEAD_DIM - ROPE_DIM), np.float32)
    zh = np.zeros((SEQ, half), np.float32)
    c = np.concatenate([cos, cos, zeros + 1.0], axis=1)
    sa = np.concatenate([-sin, zh, zeros], axis=1)
    sb = np.concatenate([zh, sin, zeros], axis=1)
    return [jnp.asarray(np.tile(t, (1, 2)), F32) for t in (c, sa, sb)]


def _rope_fwd(x, c, sa, sb):
    return x * c + pltpu.roll(x, PAIR_W - 8, 1) * sa + pltpu.roll(x, 8, 1) * sb


def _rope_bwd(dy, c, sa, sb):
    return dy * c + pltpu.roll(dy * sb, PAIR_W - 8, 1) + pltpu.roll(dy * sa, 8, 1)


def _band_masks():
    row = lax.broadcasted_iota(jnp.int32, (ATTN_BLOCK, ATTN_BLOCK), 0)
    col = lax.broadcasted_iota(jnp.int32, (ATTN_BLOCK, ATTN_BLOCK), 1)
    return col <= row, col >= row


def _stack_rows(t):
    return jnp.concatenate([t, t], axis=0)


def _stack_heads(t, first_head):
    return jnp.concatenate([jnp.where(first_head, t, 0), jnp.where(first_head, 0, t)], axis=0)


def _per_head(fn):
    return jnp.concatenate([fn(slice(h * HEAD_DIM, (h + 1) * HEAD_DIM)) for h in range(2)], axis=1)


def _slab_spec(kind):
    return pl.BlockSpec((None, SEQ, PAIR_W), lambda b, p, g: (b, 0, p * 3 * N_GROUPS + g * 3 + kind))


_TABLE_SPEC = pl.BlockSpec((SEQ, PAIR_W), lambda b, p, g: (0, 0))
_PAIR_SPEC = pl.BlockSpec((None, SEQ, PAIR_W), lambda b, p, g: (b, 0, p))


def _block_rows(dil, r, n):
    return pl.ds(n * (ATTN_BLOCK * dil) + r, ATTN_BLOCK, stride=dil)


def proj_qkv(h, w_qkv_t, tables, comm=None):
    tm = 1024
    pair_w = QKV_W // N_PAIRS
    scale = HEAD_DIM ** -0.5

    def body(h_ref, w_ref, c_ref, sa_ref, sb_ref, o_ref):
        rows = _dot(h_ref[...], w_ref[...], "nt")
        c, sa, sb = c_ref[...], sa_ref[...], sb_ref[...]
        for blk in range(pair_w // PAIR_W):
            cols = slice(blk * PAIR_W, (blk + 1) * PAIR_W)
            x = rows[:, cols]
            if blk % 3 == 0:
                x = _rope_fwd(x, c, sa, sb) * scale
            elif blk % 3 == 1:
                x = _rope_fwd(x, c, sa, sb)
            o_ref[:, cols] = x

    table = pl.BlockSpec((tm, PAIR_W), lambda i, j, : (i % (SEQ // tm), 0))
    res = hosted_call(
        body, comm, "proj_qkv", (TOKENS // tm, N_PAIRS),
        [pl.BlockSpec((tm, D_MODEL), lambda i, j: (i, 0)), pl.BlockSpec((pair_w, D_MODEL), lambda i, j: (j, 0)),
         table, table, table],
        [pl.BlockSpec((tm, pair_w), lambda i, j: (i, j))], [jax.ShapeDtypeStruct((TOKENS, QKV_W), F32)], [],
        (h, w_qkv_t, *tables), ("parallel", "parallel"))
    return res[0] if comm is None else res


def attn_fwd(qkv, comm=None):
    def body(qs, ks, v_ref, attn_b_ref, attn_ref, lse_ref, o0, o1, o2, l0, l1, l2):
        g = pl.program_id(2)
        cur_mask, prev_mask = _band_masks()
        first_head = lax.broadcasted_iota(jnp.int32, (ATTN_BLOCK, PAIR_W), 1) < HEAD_DIM

        def run(dil, o_slab, l_slab):
            nb = SEQ // dil // ATTN_BLOCK

            def block(idx, carry):
                r, n = lax.div(idx, nb), lax.rem(idx, nb)
                cur, prev = _block_rows(dil, r, n), _block_rows(dil, r, jnp.maximum(n - 1, 0))
                q = qs[cur, :].astype(BF16)
                kc, kp = ks[cur, :].astype(BF16), ks[prev, :].astype(BF16)
                vc, vp = v_ref[cur, :].astype(BF16), v_ref[prev, :].astype(BF16)
                q2 = _stack_heads(q, first_head)
                mask = _stack_rows(jnp.concatenate([jnp.logical_and(prev_mask, n > 0), cur_mask], axis=1))
                s2 = jnp.where(mask, _dot(q2, jnp.concatenate([kp, kc], axis=0), "nt"), NEG_INF)
                m = jnp.max(s2, axis=-1, keepdims=True)
                vcat, two = jnp.concatenate([vp, vc], axis=0), _stack_rows(first_head)
                vext = jnp.concatenate([jnp.where(two, vcat, 1), jnp.where(two, 1, vcat)], axis=1)
                r2 = _dot(jnp.exp(s2 - m).astype(BF16), vext, "nn")
                r0, r1 = r2[:ATTN_BLOCK, :PAIR_W], r2[ATTN_BLOCK:, PAIR_W:]
                num = jnp.where(first_head, r0, r1)
                den = pltpu.roll(jnp.where(first_head, r1, r0), HEAD_DIM, 1)
                o_slab[cur, :] = num / den
                l_slab[cur, :] = jnp.where(first_head, m[:ATTN_BLOCK], m[ATTN_BLOCK:]) + jnp.log(den)
                return carry

            lax.fori_loop(0, SEQ // ATTN_BLOCK, block, 0, unroll=4)

        for gi, (o_slab, l_slab) in enumerate(((o0, l0), (o1, l1), (o2, l2))):
            @pl.when(g == gi)
            def _(gi=gi, o_slab=o_slab, l_slab=l_slab):
                run(DILATIONS[gi], o_slab, l_slab)

        @pl.when(g == N_GROUPS - 1)
        def _():
            a, b, cc = l0[...], l1[...], l2[...]
            m = jnp.maximum(jnp.maximum(a, b), cc)
            e0, e1, e2 = jnp.exp(a - m), jnp.exp(b - m), jnp.exp(cc - m)
            tot = e0 + e1 + e2
            attn = (e0 * o0[...] + e1 * o1[...] + e2 * o2[...]) / tot
            attn_ref[...] = attn
            attn_b_ref[...] = attn.astype(BF16)
            lse_ref[...] = m + jnp.log(tot)

    shape = (LOCAL_BATCH, SEQ, GROUP_W)
    slab = pltpu.VMEM((SEQ, PAIR_W), F32)
    return hosted_call(
        body, comm, "attn_fwd", (LOCAL_BATCH, N_PAIRS, N_GROUPS),
        [_slab_spec(0), _slab_spec(1), _slab_spec(2)], [_PAIR_SPEC] * 3,
        [jax.ShapeDtypeStruct(shape, BF16), jax.ShapeDtypeStruct(shape, F32), jax.ShapeDtypeStruct(shape, F32)],
        [slab] * 6, (qkv, qkv, qkv), ("parallel", "parallel", "arbitrary"))


def attn_bwd(qkv, tables, dattn, attn, lse, comm=None):
    scale = HEAD_DIM ** -0.5

    def body(qs, ks, v_ref, c_ref, sa_ref, sb_ref, do_ref, out_ref, lse_ref, dqkv_ref, dl, dq_s, dk_s, dv_s):
        g = pl.program_id(2)
        c, sa, sb = c_ref[...], sa_ref[...], sb_ref[...]

        @pl.when(g == 0)
        def _():
            prod = do_ref[...] * out_ref[...]
            dl[...] = _per_head(
                lambda sl: jnp.broadcast_to(jnp.sum(prod[:, sl], axis=-1, keepdims=True), (SEQ, HEAD_DIM)))

        cur_mask, prev_mask = _band_masks()
        first_head = lax.broadcasted_iota(jnp.int32, (ATTN_BLOCK, PAIR_W), 1) < HEAD_DIM

        def run(dil):
            nb = SEQ // dil // ATTN_BLOCK

            def block(idx, carry):
                r, n = lax.div(idx, nb), lax.rem(idx, nb)
                cur = _block_rows(dil, r, n)
                prev = _block_rows(dil, r, jnp.maximum(n - 1, 0))
                nxt = _block_rows(dil, r, jnp.minimum(n + 1, nb - 1))
                q0, q1 = qs[cur, :].astype(BF16), qs[nxt, :].astype(BF16)
                kp, kc = ks[prev, :].astype(BF16), ks[cur, :].astype(BF16)
                vp, vc = v_ref[prev, :].astype(BF16), v_ref[cur, :].astype(BF16)
                do0, do1 = do_ref[cur, :].astype(BF16), do_ref[nxt, :].astype(BF16)
                lse0, lse1, dl0, dl1 = lse_ref[cur, :], lse_ref[nxt, :], dl[cur, :], dl[nxt, :]
                has_prev = jnp.logical_and(prev_mask, n > 0)
                has_next = jnp.logical_and(prev_mask, n < nb - 1)

                def per_row(t):
                    return jnp.concatenate([t[:, 0:1], t[:, HEAD_DIM:HEAD_DIM + 1]], axis=0)

                q20, q21 = _stack_heads(q0, first_head), _stack_heads(q1, first_head)
                do20, do21 = _stack_heads(do0, first_head), _stack_heads(do1, first_head)
                kcat, vcat = jnp.concatenate([kp, kc], axis=0), jnp.concatenate([vp, vc], axis=0)
                mask0 = _stack_rows(jnp.concatenate([has_prev, cur_mask], axis=1))
                p0 = jnp.where(mask0, jnp.exp(_dot(q20, kcat, "nt") - per_row(lse0)), 0.0)
                ds0 = (p0 * (_dot(do20, vcat, "nt") - per_row(dl0))).astype(BF16)
                p1 = jnp.where(_stack_rows(has_next), jnp.exp(_dot(q21, kc, "nt") - per_row(lse1)), 0.0)
                ds1 = (p1 * (_dot(do21, vc, "nt") - per_row(dl1))).astype(BF16)
                dq2 = _dot(ds0, kcat, "nn")
                dq_s[cur, :] = jnp.where(first_head, dq2[:ATTN_BLOCK], dq2[ATTN_BLOCK:])
                ds_cur = jnp.concatenate([ds0[:, ATTN_BLOCK:], ds1], axis=0)
                p_cur = jnp.concatenate([p0[:, ATTN_BLOCK:], p1], axis=0).astype(BF16)
                dk_s[cur, :] = _dot(ds_cur, jnp.concatenate([q20, q21], axis=0), "tn")
                dv_s[cur, :] = _dot(p_cur, jnp.concatenate([do20, do21], axis=0), "tn")
                return carry

            lax.fori_loop(0, SEQ // ATTN_BLOCK, block, 0, unroll=2)

        for gi in range(N_GROUPS):
            @pl.when(g == gi)
            def _(gi=gi):
                run(DILATIONS[gi])

        dqkv_ref[:, 0:PAIR_W] = _rope_bwd(dq_s[...] * scale, c, sa, sb).astype(BF16)
        dqkv_ref[:, PAIR_W:2 * PAIR_W] = _rope_bwd(dk_s[...], c, sa, sb).astype(BF16)
        dqkv_ref[:, 2 * PAIR_W:] = dv_s[...].astype(BF16)

    slab = pltpu.VMEM((SEQ, PAIR_W), F32)
    return hosted_call(
        body, comm, "attn_bwd", (LOCAL_BATCH, N_PAIRS, N_GROUPS),
        [_slab_spec(0), _slab_spec(1), _slab_spec(2), _TABLE_SPEC, _TABLE_SPEC, _TABLE_SPEC,
         _PAIR_SPEC, _PAIR_SPEC, _PAIR_SPEC],
        [pl.BlockSpec((None, SEQ, 3 * PAIR_W), lambda b, p, g: (b, 0, p * N_GROUPS + g))],
        [jax.ShapeDtypeStruct((LOCAL_BATCH, SEQ, QKV_W), BF16)],
        [slab] * 4, (qkv, qkv, qkv, *tables, dattn, attn, lse), ("parallel", "parallel", "arbitrary"))


def _discretize(lr, li, log_dt, br, bi):
    dt = jnp.exp(log_dt)
    mag = jnp.exp(lr * dt)
    ab_re, ab_im = mag * jnp.cos(li * dt), mag * jnp.sin(li * dt)
    den = lr * lr + li * li
    nr, ni = ab_re - 1.0, ab_im
    f_re = (nr * lr + ni * li) / den
    f_im = (ni * lr - nr * li) / den
    return ab_re, ab_im, f_re[None] * br - f_im[None] * bi, f_re[None] * bi + f_im[None] * br


def ssm_prep(lr, li, log_dt, br, bi):
    def body(lr_ref, li_ref, dt_ref, br_ref, bi_ref, *outs):
        for o, v in zip(outs, _discretize(lr_ref[...], li_ref[...], dt_ref[...], br_ref[...], bi_ref[...])):
            o[...] = v
    shapes = [lr, li, br, bi]
    return pl.pallas_call(body, name="ssm_prep",
                          out_shape=[jax.ShapeDtypeStruct(s.shape, F32) for s in shapes])(lr, li, log_dt, br, bi)


def ssm_prep_bwd(lr, li, log_dt, br, bi, g_ab_re, g_ab_im, g_bb_re, g_bb_im):
    def body(lr_ref, li_ref, dt_ref, br_ref, bi_ref, g0, g1, g2, g3, *outs):
        _, vjp = jax.vjp(_discretize, lr_ref[...], li_ref[...], dt_ref[...], br_ref[...], bi_ref[...])
        for o, v in zip(outs, vjp((g0[...], g1[...], g2[...], g3[...]))):
            o[...] = v
    shapes = [lr, li, log_dt, br, bi]
    return pl.pallas_call(body, name="ssm_prep_bwd",
                          out_shape=[jax.ShapeDtypeStruct(s.shape, F32) for s in shapes])(
        lr, li, log_dt, br, bi, g_ab_re, g_ab_im, g_bb_re, g_bb_im)


def _block_diag(t):
    per = SSM_STATE_W // SSM_LANE_BLOCKS // 64
    g = t.transpose(1, 0, 2).reshape(SSM_LANE_BLOCKS, per, 16, 64)
    eye = jnp.eye(per, dtype=t.dtype)
    return jnp.einsum("jgcn,gh->jgchn", g, eye).reshape(SSM_LANE_BLOCKS, per * 16, per * 64)


def _block_diag_t(m):
    per = SSM_STATE_W // SSM_LANE_BLOCKS // 64
    m5 = m.reshape(SSM_LANE_BLOCKS, per, 16, per, 64)
    d = jnp.einsum("jgchn,gh->jgcn", m5, jnp.eye(per, dtype=m.dtype))
    return d.reshape(SSM_LANE_BLOCKS * per, 16, 64).transpose(1, 0, 2)


def _cmul(ar, ai, br, bi):
    return ar * br - ai * bi, ar * bi + ai * br


def _power_tables(ar, ai, reverse):
    width = ar.shape[1]
    row = lax.broadcasted_iota(jnp.int32, (8, width), 0)
    pows = [(ar, ai)]
    for _ in range(7):
        pows.append(_cmul(pows[-1][0], pows[-1][1], ar, ai))
    steps = []
    for k in (1, 2, 4):
        keep = (row >= k) if not reverse else (row < 8 - k)
        steps.append((jnp.where(keep, pows[k - 1][0], 0.0), jnp.where(keep, pows[k - 1][1], 0.0)))
    cr = jnp.zeros((8, width), F32)
    ci = jnp.zeros((8, width), F32)
    for i in range(8):
        pr, pi = pows[i] if not reverse else pows[7 - i]
        cr = jnp.where(row == i, pr, cr)
        ci = jnp.where(row == i, pi, ci)
    return steps, (cr, ci)


SCAN_CHUNK = 2048
STATE_BLOCK = SSM_STATE_W // SSM_LANE_BLOCKS
CHAN_BLOCK = SSM_W // SSM_LANE_BLOCKS


def ssm_fwd(u, ab_re, ab_im, bb_re, bb_im, cb_re, cb_im, d_skip, comm=None):
    nt = SEQ // SCAN_CHUNK
    chan = pl.BlockSpec((None, SCAN_CHUNK, CHAN_BLOCK), lambda b, j, t: (b, t, j))
    state = pl.BlockSpec((None, SCAN_CHUNK, STATE_BLOCK), lambda b, j, t: (b, t, j))
    mat = pl.BlockSpec((None, CHAN_BLOCK, STATE_BLOCK), lambda b, j, t: (j, 0, 0))
    lane = pl.BlockSpec((1, STATE_BLOCK), lambda b, j, t: (0, j))
    dsp = pl.BlockSpec((1, CHAN_BLOCK), lambda b, j, t: (0, j))

    def body(u_ref, ar_ref, ai_ref, bbr_ref, bbi_ref, cbr_ref, cbi_ref, d_ref, y_ref, yg_ref, xr_ref, xi_ref,
             car_r, car_i):
        @pl.when(pl.program_id(2) == 0)
        def _():
            car_r[...] = jnp.zeros_like(car_r)
            car_i[...] = jnp.zeros_like(car_i)

        steps, (pr, pi) = _power_tables(ar_ref[...], ai_ref[...], reverse=False)
        uf = u_ref[...]
        ub = uf.astype(BF16)
        xr_ref[...] = _dot(ub, bbr_ref[...], "nn")
        xi_ref[...] = _dot(ub, bbi_ref[...], "nn")

        def tile(i, carry):
            cr, ci = carry
            sl = pl.ds(pl.multiple_of(i * 8, 8), 8)
            br, bi = xr_ref[sl, :], xi_ref[sl, :]
            for k, (sr, si) in zip((1, 2, 4), steps):
                tr, ti = _cmul(sr, si, pltpu.roll(br, k, 0), pltpu.roll(bi, k, 0))
                br, bi = br + tr, bi + ti
            tr, ti = _cmul(pr, pi, cr, ci)
            br, bi = br + tr, bi + ti
            xr_ref[sl, :] = br
            xi_ref[sl, :] = bi
            return br[7:8, :], bi[7:8, :]

        cr, ci = lax.fori_loop(0, SCAN_CHUNK // 8, tile, (car_r[0:1, :], car_i[0:1, :]), unroll=4)
        car_r[0:1, :] = cr
        car_i[0:1, :] = ci
        y = (_dot(xr_ref[...].astype(BF16), cbr_ref[...], "nt") - _dot(xi_ref[...].astype(BF16), cbi_ref[...], "nt")
             + d_ref[...] * uf)
        y_ref[...] = y
        yg_ref[...] = jax.nn.gelu(y).astype(BF16)

    return hosted_call(
        body, comm, "ssm_fwd", (LOCAL_BATCH, SSM_LANE_BLOCKS, nt),
        [chan, lane, lane, mat, mat, mat, mat, dsp], [chan, chan, state, state],
        [jax.ShapeDtypeStruct((LOCAL_BATCH, SEQ, SSM_W), F32), jax.ShapeDtypeStruct((LOCAL_BATCH, SEQ, SSM_W), BF16),
         jax.ShapeDtypeStruct((LOCAL_BATCH, SEQ, SSM_STATE_W), F32),
         jax.ShapeDtypeStruct((LOCAL_BATCH, SEQ, SSM_STATE_W), F32)],
        [pltpu.VMEM((8, STATE_BLOCK), F32), pltpu.VMEM((8, STATE_BLOCK), F32)],
        (u, ab_re, ab_im, bb_re, bb_im, cb_re, cb_im, d_skip), ("parallel", "parallel", "arbitrary"))


def ssm_bwd(dyg, y, u, xr, xi, ab_re, ab_im, bb_re, bb_im, cb_re, cb_im, d_skip, comm=None):
    nt = SEQ // SCAN_CHUNK
    ntile = SCAN_CHUNK // 8

    def rev(t):
        return nt - 1 - t

    chan = pl.BlockSpec((None, SCAN_CHUNK, CHAN_BLOCK), lambda j, b, t: (b, rev(t), j))
    state = pl.BlockSpec((None, SCAN_CHUNK, STATE_BLOCK), lambda j, b, t: (b, rev(t), j))
    before = pl.BlockSpec((None, 8, STATE_BLOCK), lambda j, b, t: (b, jnp.maximum(rev(t) * ntile - 1, 0), j))
    mat = pl.BlockSpec((None, CHAN_BLOCK, STATE_BLOCK), lambda j, b, t: (j, 0, 0))
    lane = pl.BlockSpec((1, STATE_BLOCK), lambda j, b, t: (0, j))
    lane8 = pl.BlockSpec((8, STATE_BLOCK), lambda j, b, t: (0, j))
    dsp = pl.BlockSpec((1, CHAN_BLOCK), lambda j, b, t: (0, j))

    def body(dyg_ref, y_ref, u_ref, xr_ref, xi_ref, xrb_ref, xib_ref, ar_ref, ai_ref, bbr_ref, bbi_ref, cbr_ref,
             cbi_ref, d_ref, du_ref, dcbr_ref, dcbi_ref, dbbr_ref, dbbi_ref, dd_ref, dar_ref, dai_ref,
             lam_r, lam_i, car_r, car_i):
        b, t = pl.program_id(1), pl.program_id(2)
        first = jnp.logical_and(b == 0, t == 0)

        @pl.when(t == 0)
        def _():
            car_r[...] = jnp.zeros_like(car_r)
            car_i[...] = jnp.zeros_like(car_i)

        @pl.when(first)
        def _():
            for r in (dcbr_ref, dcbi_ref, dbbr_ref, dbbi_ref, dd_ref, dar_ref, dai_ref):
                r[...] = jnp.zeros_like(r)

        steps, (pr, pi) = _power_tables(ar_ref[...], -ai_ref[...], reverse=True)
        uf = u_ref[...]
        _, gelu_vjp = jax.vjp(jax.nn.gelu, y_ref[...])
        dy = gelu_vjp(dyg_ref[...])[0]
        dyb = dy.astype(BF16)
        dd_ref[...] += _colsum(dy * uf)
        lam_r[...] = _dot(dyb, cbr_ref[...], "nn")
        lam_i[...] = -_dot(dyb, cbi_ref[...], "nn")
        dcbr_ref[...] += _dot(dyb, xr_ref[...].astype(BF16), "tn")
        dcbi_ref[...] -= _dot(dyb, xi_ref[...].astype(BF16), "tn")
        row0 = lax.broadcasted_iota(jnp.int32, (8, STATE_BLOCK), 0) == 0
        has_before = rev(t) > 0
        xrb = jnp.where(has_before, xrb_ref[...], 0.0)
        xib = jnp.where(has_before, xib_ref[...], 0.0)

        def tile(s, carry):
            cr, ci, acc_r, acc_i = carry
            i = ntile - 1 - s
            sl = pl.ds(pl.multiple_of(i * 8, 8), 8)
            gr, gi = lam_r[sl, :], lam_i[sl, :]
            for k, (sr, si) in zip((1, 2, 4), steps):
                tr, ti = _cmul(sr, si, pltpu.roll(gr, 8 - k, 0), pltpu.roll(gi, 8 - k, 0))
                gr, gi = gr + tr, gi + ti
            tr, ti = _cmul(pr, pi, cr, ci)
            gr, gi = gr + tr, gi + ti
            lam_r[sl, :] = gr
            lam_i[sl, :] = gi
            sp = pl.ds(pl.multiple_of(jnp.maximum(i - 1, 0) * 8, 8), 8)
            pvr = jnp.where(i > 0, xr_ref[sp, :], xrb)
            pvi = jnp.where(i > 0, xi_ref[sp, :], xib)
            xsr = jnp.where(row0, pltpu.roll(pvr, 1, 0), pltpu.roll(xr_ref[sl, :], 1, 0))
            xsi = jnp.where(row0, pltpu.roll(pvi, 1, 0), pltpu.roll(xi_ref[sl, :], 1, 0))
            acc_r = acc_r + xsr * gr + xsi * gi
            acc_i = acc_i + xsr * gi - xsi * gr
            return gr[0:1, :], gi[0:1, :], acc_r, acc_i

        zero = jnp.zeros((8, STATE_BLOCK), F32)
        cr, ci, acc_r, acc_i = lax.fori_loop(0, ntile, tile, (car_r[0:1, :], car_i[0:1, :], zero, zero), unroll=2)
        car_r[0:1, :] = cr
        car_i[0:1, :] = ci
        dar_ref[...] += acc_r
        dai_ref[...] += acc_i
        lrb, lib = lam_r[...].astype(BF16), lam_i[...].astype(BF16)
        du = _dot(lrb, bbr_ref[...], "nt") + _dot(lib, bbi_ref[...], "nt") + d_ref[...] * dy
        du_ref[...] = du.astype(BF16)
        ub = uf.astype(BF16)
        dbbr_ref[...] += _dot(ub, lrb, "tn")
        dbbi_ref[...] += _dot(ub, lib, "tn")

    mat_shape = jax.ShapeDtypeStruct((SSM_LANE_BLOCKS, CHAN_BLOCK, STATE_BLOCK), F32)
    return hosted_call(
        body, comm, "ssm_bwd", (SSM_LANE_BLOCKS, LOCAL_BATCH, nt),
        [chan, chan, chan, state, state, before, before, lane, lane, mat, mat, mat, mat, dsp],
        [chan, mat, mat, mat, mat, dsp, lane8, lane8],
        [jax.ShapeDtypeStruct((LOCAL_BATCH, SEQ, SSM_W), BF16), mat_shape, mat_shape, mat_shape, mat_shape,
         jax.ShapeDtypeStruct((1, SSM_W), F32), jax.ShapeDtypeStruct((8, SSM_STATE_W), F32),
         jax.ShapeDtypeStruct((8, SSM_STATE_W), F32)],
        [pltpu.VMEM((SCAN_CHUNK, STATE_BLOCK), F32), pltpu.VMEM((SCAN_CHUNK, STATE_BLOCK), F32),
         pltpu.VMEM((8, STATE_BLOCK), F32), pltpu.VMEM((8, STATE_BLOCK), F32)],
        (dyg, y, u, xr, xi, xr, xi, ab_re, ab_im, bb_re, bb_im, cb_re, cb_im, d_skip),
        ("parallel", "arbitrary", "arbitrary"))


def _merge_fn(g0, g1, attn_d, za, zb):
    return jax.nn.sigmoid(g0) * attn_d + jax.nn.sigmoid(g1) * (za * jax.nn.sigmoid(zb))


def _swiglu_fn(a, b):
    return jax.nn.silu(a) * b


def _own_slot(slots, shard):
    me = 2 * lax.axis_index("x") + lax.axis_index("y")
    mine = lax.broadcasted_iota(jnp.int32, (N_CHIPS, 1, 1), 0) == me
    return jnp.where(mine, shard[None], slots)


def _reduce_start(names, gw, shard_shapes):
    return swap_comm([_to_slots(n, gw[n], shard_shapes[n]) for n in names])


def _reduce_chip(names, slots, got, core):
    return exchange_comm([add_halves(n, g, r, core) for n, g, r in zip(names, slots, got)])


def local_step(x, target, shards, small, core):
    g_mix, g_ffn, g_final = small["norm_mix_g"], small["norm_ffn_g"], small["norm_final_g"]
    tables = _rope_tables()
    seqs = lambda t: t.reshape(LOCAL_BATCH, SEQ, t.shape[-1])
    toks = lambda t: t.reshape(TOKENS, t.shape[-1])
    shard_shapes = {n: s.shape for n, s in shards.items()}
    w = {}

    def gather(names):
        return gather_comm([shards[n] for n in names])

    def arrived(names, slots, own=None):
        for n, s in zip(names, slots):
            w[n] = _from_slots(n, s if own is None else _own_slot(s, own))

    later = [n for n in BIG if n != "w_in"]
    sems, w_in_shard, land, token = split_start(shards["w_in"].astype(BF16), "w_in_gather_start")
    zero = token[0, 0]
    h, *rest = first_norm(x, g_mix + zero, [shards[n] for n in later])
    shards = dict(shards)
    shards.update(zip(later, rest))
    br_t = small["ssm_b_re"].transpose(2, 0, 1)
    bi_t = small["ssm_b_im"].transpose(2, 0, 1)
    log_dt = small["ssm_log_dt"].reshape(32, 1)
    ab_re, ab_im, bb_re_t, bb_im_t = ssm_prep(small["ssm_a_re"] + zero, small["ssm_a_im"], log_dt, br_t, bi_t)
    ab = [ab_re.reshape(1, SSM_STATE_W), ab_im.reshape(1, SSM_STATE_W)]
    bb = [_block_diag(bb_re_t).astype(BF16), _block_diag(bb_im_t).astype(BF16)]
    cb = [_block_diag((small["ssm_c_re"] + zero).transpose(1, 0, 2)).astype(BF16),
          _block_diag((small["ssm_c_im"] + zero).transpose(1, 0, 2)).astype(BF16)]
    d_skip = small["ssm_d"].reshape(1, SSM_W)
    w_in_shard, land = split_wait(sems, w_in_shard, land, [h] + bb + cb, "w_in_gather_wait")
    arrived(["w_in"], [handover(land, "w_in_handover")], own=w_in_shard)
    w_qkv, w_u, w_gate = _qkv_order(w["w_in"][:QKV_W]), w["w_in"][QKV_W:QKV_W + SSM_W], w["w_in"][QKV_W + SSM_W:]
    qkv, *slots = proj_qkv(h, w_qkv, tables, comm=gather(["w_attn_out", "w_glu"]))
    arrived(["w_attn_out", "w_glu"], slots)
    qkv = seqs(qkv)
    u = seqs(matmul(h, w_u, "nt", F32, "proj_u"))
    gl, *slots = matmul(h, w_gate, "nt", BF16, "proj_gate", comm=gather(["w_out"]))
    arrived(["w_out"], slots)
    attn_b, attn, lse, *slots = attn_fwd(qkv, comm=gather(["w_ffn_gate"]))
    arrived(["w_ffn_gate"], slots)
    attn_b = toks(attn_b)
    y, yg, xr, xi, *slots = ssm_fwd(u, *ab, *bb, *cb, d_skip, comm=gather(["w_ffn_up"]))
    arrived(["w_ffn_up"], slots)
    yg2 = toks(yg)
    merged, x1, h2 = merge_out_proj(gl, attn_b, w["w_attn_out"], yg2, w["w_glu"], w["w_out"], x, g_ffn)
    a, b, act, *slots = ffn_in(h2, w["w_ffn_gate"], w["w_ffn_up"], comm=gather(["w_ffn_down"]))
    arrived(["w_ffn_down"], slots)

    def final_fn(xv, g, tgt):
        yv, vjp = jax.vjp(_rms, xv, g)
        err = yv - tgt
        dx, dg = vjp(err * (1.0 / D_MODEL))
        loss = 0.5 * jnp.sum(jnp.mean(err * err, axis=-1, keepdims=True), axis=0, keepdims=True)
        return dx, dx, dg, jnp.broadcast_to(loss, (1, LANES))

    dx2, dx2_b, dg_final, loss = matmul_rows(act, w["w_ffn_down"], "ffn_down_loss", final_fn, [g_final, target],
                                             [(D_MODEL, F32), (D_MODEL, BF16)], accs=(D_MODEL, LANES), add=x1)
    gw, parts = {}, {}
    gw["w_ffn_down"] = matmul(act, dx2_b, "tn", F32, "d_ffn_down")
    da_b, db_b = ffn_in_bwd(dx2_b, w["w_ffn_down"], a, b)
    gw["w_ffn_gate"] = matmul(da_b, h2, "tn", F32, "d_ffn_gate")
    gw["w_ffn_up"] = matmul(db_b, h2, "tn", F32, "d_ffn_up")
    ffn = ["w_ffn_down", "w_ffn_gate", "w_ffn_up"]
    swap = _reduce_start(ffn, gw, shard_shapes)

    def norm_bwd(dh, xv, g, skip):
        _, vjp = jax.vjp(_rms, xv, g)
        dx, dg = vjp(dh)
        dx = dx + skip
        return dx, dx, dg

    dx1, dx1_b, dg_ffn, *got = matmul_rows(
        [da_b, db_b], [w["w_ffn_gate"], w["w_ffn_up"]], "d_h2_norm", norm_bwd, [x1, g_ffn, dx2],
        [(D_MODEL, F32), (D_MODEL, BF16)], accs=(D_MODEL,), comm=swap, tm=256)
    ffn_exchange = [_reduce_chip(ffn[:2], swap.ins[:2], got[:2], core)]
    ffn_up_exchange = _reduce_chip(ffn[2:], swap.ins[2:], got[2:], core)
    gw["w_out"] = matmul(merged, dx1_b, "tn", F32, "d_out")
    dgl_b, dattn_d_b, dz_b, dattn, dyg, parts["w_ffn_up"] = merge_bwd(
        dx1_b, w["w_out"], gl, attn_b, w["w_attn_out"], yg2, w["w_glu"], comm=ffn_up_exchange)
    dattn, dyg = seqs(dattn), seqs(dyg)
    gw["w_attn_out"] = matmul(attn_b, dattn_d_b, "tn", F32, "d_attn_out")
    gw["w_glu"] = matmul(yg2, dz_b, "tn", F32, "d_glu")
    mixer = ["w_out", "w_attn_out", "w_glu"]
    swap = _reduce_start(mixer, gw, shard_shapes)
    du_b, dcb_re, dcb_im, dbb_re, dbb_im, dd, da_re8, da_im8, *rest = ssm_bwd(
        dyg, y, u, xr, xi, *ab, *bb, *cb, d_skip, comm=join_comms(ffn_exchange + [swap]))
    for n, p in zip(ffn[:2], rest[:2]):
        parts[n] = p
    mixer_exchange = _reduce_chip(mixer, swap.ins, rest[2:], core)
    du_b = toks(du_b)
    g_ab_re = jnp.sum(da_re8, axis=0).reshape(32, 64)
    g_ab_im = jnp.sum(da_im8, axis=0).reshape(32, 64)
    d_lr, d_li, d_ldt, d_br_t, d_bi_t = ssm_prep_bwd(
        small["ssm_a_re"], small["ssm_a_im"], log_dt, br_t, bi_t,
        g_ab_re, g_ab_im, _block_diag_t(dbb_re), _block_diag_t(dbb_im))
    as_gcn = lambda t: t.transpose(1, 0, 2).reshape(SSM_W, 64)
    gs = {
        "ssm_a_re": d_lr, "ssm_a_im": d_li, "ssm_log_dt": d_ldt.reshape(1, 32),
        "ssm_b_re": as_gcn(d_br_t), "ssm_b_im": as_gcn(d_bi_t),
        "ssm_c_re": as_gcn(_block_diag_t(dcb_re)), "ssm_c_im": as_gcn(_block_diag_t(dcb_im)),
        "ssm_d": dd.reshape(32, 16).T,
    }
    ssm_gather = small_comm([gs[n] for n in SSM_SMALL])
    dqkv_b, *rest = attn_bwd(qkv, tables, dattn, attn, lse, comm=join_comms([mixer_exchange, ssm_gather]))
    for n, p in zip(mixer, rest):
        parts[n] = p
    ssm_shares = rest[len(mixer):]
    dqkv_b = toks(dqkv_b)
    d_qkv = matmul(dqkv_b, h, "tn", F32, "d_w_qkv")
    d_u = matmul(du_b, h, "tn", F32, "d_w_u")
    d_gate = matmul(dgl_b, h, "tn", F32, "d_w_gate")
    gw["w_in"] = jnp.concatenate([_qkv_order(d_qkv, back=True), d_u, d_gate], axis=0)
    swap = _reduce_start(["w_in"], gw, shard_shapes)
    dh, *got = matmul(dqkv_b, w_qkv, "nn", F32, "d_h_qkv", comm=swap)
    chip_sum = add_halves("w_in", swap.ins[0], got[0], core)
    sems, chip_sum, land, token = split_start(chip_sum, "w_in_reduce_start", per_chip=True)
    grad_x, dg_mix = mix_in_bwd([du_b, dgl_b], [w_u, w_gate], dh, x, g_mix + token[0, 0], dx1)
    gs_norm = {"norm_mix_g": dg_mix, "norm_ffn_g": dg_ffn, "norm_final_g": dg_final}
    return loss, grad_x, parts, ssm_shares, gs_norm, (sems, chip_sum, land)


ANY = pl.BlockSpec(memory_space=pl.ANY)
BIG = ("w_in", "w_glu", "w_attn_out", "w_out", "w_ffn_gate", "w_ffn_up", "w_ffn_down")
TRANSPOSED = ("w_in", "w_ffn_gate", "w_ffn_up")
ROW_SHARDED = TRANSPOSED + ("w_out", "w_ffn_down")
SMALL = ("norm_mix_g", "ssm_a_re", "ssm_a_im", "ssm_log_dt", "ssm_b_re", "ssm_b_im", "ssm_c_re", "ssm_c_im",
         "ssm_d", "norm_ffn_g", "norm_final_g")
WEIGHTS = ("norm_mix_g", "w_in", "ssm_a_re", "ssm_a_im", "ssm_log_dt", "ssm_b_re", "ssm_b_im", "ssm_c_re",
           "ssm_c_im", "ssm_d", "w_glu", "w_attn_out", "w_out", "norm_ffn_g", "w_ffn_gate", "w_ffn_up",
           "w_ffn_down", "norm_final_g")
SSM_SMALL = SMALL[1:9]
NORM_SMALL = (SMALL[0],) + SMALL[9:]
NORM_ROWS = 32
NORM_HOST = "w_ffn_down"
assert NORM_HOST in BIG[1:] and NORM_HOST not in TRANSPOSED
N_BIG = len(BIG)


def _position():
    return lax.axis_index("x"), lax.axis_index("y"), lax.axis_index("c")


def _other_chips(x, y):
    return [(1 - x, y), (x, 1 - y), (1 - x, 1 - y)]


def _remote(src, dst, send_sem, recv_sem, device):
    return pltpu.make_async_remote_copy(src_ref=src, dst_ref=dst, send_sem=send_sem, recv_sem=recv_sem,
                                        device_id=device, device_id_type=MESH)


_later = functools.partial


def _two_level_phases(copies):
    def first(*refs):
        locals_, sends, _, _, _ = copies(*refs)
        for cp in locals_ + sends:
            cp().start()

    def mid(*refs):
        _, _, arrived, passed, _ = copies(*refs)
        for got, cp in zip(arrived, passed):
            got().wait_recv()
            cp().start()

    def last(*refs):
        locals_, sends, _, passed, from_sibling = copies(*refs)
        for cp in from_sibling:
            cp().wait_recv()
        for cp in sends + passed:
            cp().wait_send()
        for cp in locals_:
            cp().wait()

    return first, mid, last


def _half(ref, chip, which):
    rows = ref.shape[1] // 2
    return ref.at[chip, pl.ds(which * rows, rows), :]


class Comm:
    def __init__(self, ins, out_shapes, sems, first, mid, last):
        self.ins, self.out_shapes, self.sems = list(ins), list(out_shapes), list(sems)
        self.first, self.mid, self.last = first, mid, last


def join_comms(comms):
    def cut(refs_by_kind):
        offs, parts = [0, 0, 0], []
        for cm in comms:
            sizes = (len(cm.ins), len(cm.out_shapes), len(cm.sems))
            parts.append(tuple(refs_by_kind[k][offs[k]:offs[k] + sizes[k]] for k in range(3)))
            offs = [o + s for o, s in zip(offs, sizes)]
        return parts

    def phase(which):
        def run(ins, outs, sems):
            for cm, part in zip(comms, cut((ins, outs, sems))):
                fn = getattr(cm, which)
                if fn is not None:
                    fn(*part)
        return run

    return Comm(sum((cm.ins for cm in comms), []), sum((cm.out_shapes for cm in comms), []),
                sum((cm.sems for cm in comms), []), phase("first"), phase("mid"), phase("last"))


def _comm_operands(comm):
    if comm is None:
        return [], [], []
    return comm.ins, comm.out_shapes, comm.sems


def _comm_begin(comm, refs, step, n_steps):
    if comm is None:
        return
    pl.when(step == 0)(lambda: comm.first(*refs))
    if comm.mid is not None:
        pl.when(step == (n_steps * 3) // 4)(lambda: comm.mid(*refs))


def _comm_end(comm, refs, step, n_steps):
    if comm is not None:
        pl.when(step == n_steps - 1)(lambda: comm.last(*refs))


def _comm_refs(comm, refs, n_in, n_out):
    if comm is None:
        return list(refs), None
    ci, co, cs = len(comm.ins), len(comm.out_shapes), len(comm.sems)
    o0 = n_in + ci
    s0 = o0 + n_out + co
    host = list(refs[:n_in]) + list(refs[o0:o0 + n_out]) + list(refs[s0:len(refs) - cs])
    return host, (list(refs[n_in:o0]), list(refs[o0 + n_out:s0]), list(refs[len(refs) - cs:]))


def run_comm(comm, name):
    n_in, n_out = len(comm.ins), len(comm.out_shapes)

    def body(*refs):
        parts = (list(refs[:n_in]), list(refs[n_in:n_in + n_out]), list(refs[n_in + n_out:]))
        comm.first(*parts)
        if comm.mid is not None:
            comm.mid(*parts)
        comm.last(*parts)

    return pl.pallas_call(body, name=name, in_specs=[ANY] * n_in, out_specs=[ANY] * n_out,
                          out_shape=comm.out_shapes, scratch_shapes=comm.sems)(*comm.ins)


def hosted_call(work, comm, name, grid, in_specs, out_specs, out_shape, scratch_shapes, args, semantics):
    c_ins, c_outs, c_sems = _comm_operands(comm)
    n_steps = math.prod(grid)

    def body(*refs):
        host, c_refs = _comm_refs(comm, refs, len(in_specs), len(out_specs))
        step = 0
        for axis, size in enumerate(grid):
            step = step * size + pl.program_id(axis)
        _comm_begin(comm, c_refs, step, n_steps)
        work(*host)
        _comm_end(comm, c_refs, step, n_steps)

    return pl.pallas_call(
        body, name=name, grid=grid, in_specs=list(in_specs) + [ANY] * len(c_ins),
        out_specs=list(out_specs) + [ANY] * len(c_outs), out_shape=list(out_shape) + c_outs,
        scratch_shapes=list(scratch_shapes) + c_sems,
        compiler_params=_params(semantics if comm is None else ("arbitrary",) * len(grid)),
    )(*args, *c_ins)


def gather_comm(shards):
    n = len(shards)

    def copies(srcs, outs, sems):
        send_sems, recv_sems, local_sems = sems
        x, y, c = _position()
        me = 2 * x + y
        sibling = (x, y, 1 - c)
        chips = _other_chips(x, y)
        locals_ = [_later(pltpu.make_async_copy, s, o.at[me], local_sems.at[i])
                   for i, (s, o) in enumerate(zip(srcs, outs))]
        sends, arrived, passed, from_sibling = [], [], [], []
        for j, (px, py) in enumerate(chips):
            for i, (s, o) in enumerate(zip(srcs, outs)):
                rows = s.shape[0] // 2
                sends.append(_later(_remote, s.at[pl.ds(c * rows, rows), :], _half(o, me, c), send_sems.at[i, j],
                                    recv_sems.at[i, j], (px, py, c)))
                got = _half(o, 2 * px + py, c)
                arrived.append(_later(_remote, got, got, send_sems.at[i, j], recv_sems.at[i, j], (px, py, c)))
                passed.append(_later(_remote, got, got, send_sems.at[i, 3 + j], recv_sems.at[i, 3 + j], sibling))
                other = _half(o, 2 * px + py, 1 - c)
                from_sibling.append(_later(_remote, other, other, send_sems.at[i, 3 + j], recv_sems.at[i, 3 + j],
                                           sibling))
        return locals_, sends, arrived, passed, from_sibling

    return Comm(shards, [jax.ShapeDtypeStruct((N_CHIPS,) + s.shape, s.dtype) for s in shards],
                [pltpu.SemaphoreType.DMA((n, 6)), pltpu.SemaphoreType.DMA((n, 6)), pltpu.SemaphoreType.DMA((n,))],
                *_two_level_phases(copies))


HBM = pl.BlockSpec(memory_space=pltpu.HBM)
SEM = pl.BlockSpec(memory_space=pltpu.SEMAPHORE)
N_OTHER = N_CHIPS - 1


def _ici_halves(src_ref, land_ref, sems, per_chip):
    x, y, c = _position()
    me = 2 * x + y
    rows = land_ref.shape[1] // 2
    sends, arrivals = [], []
    for j, (px, py) in enumerate(_other_chips(x, y)):
        piece = src_ref.at[2 * px + py] if per_chip else src_ref.at[pl.ds(c * rows, rows), :]
        sends.append(_later(_remote, piece, _half(land_ref, me, c), sems[j], sems[N_OTHER + j], (px, py, c)))
        got = _half(land_ref, 2 * px + py, c)
        arrivals.append(_later(_remote, got, got, sems[j], sems[N_OTHER + j], (px, py, c)))
    return sends, arrivals


def split_start(src, name, per_chip=False):
    def body(src_ref, land_ref, *rest):
        sems, token = rest[:2 * N_OTHER], rest[-1]
        for cp in _ici_halves(src_ref, land_ref, sems, per_chip)[0]:
            cp().start()
        token[...] = jnp.zeros_like(token)

    rows, cols = (2 * src.shape[1], src.shape[2]) if per_chip else src.shape
    sem = pltpu.SemaphoreType.DMA(())
    land = (N_CHIPS, rows, cols)
    res = pl.pallas_call(
        body, name=name, in_specs=(HBM, HBM),
        out_specs=(SEM,) * (2 * N_OTHER) + (HBM, HBM, pl.BlockSpec(memory_space=pltpu.VMEM)),
        out_shape=(sem,) * (2 * N_OTHER) + (pltpu.HBM(src.shape, src.dtype), pltpu.HBM(land, src.dtype),
                                           jax.ShapeDtypeStruct((8, LANES), F32)),
        input_output_aliases={0: 2 * N_OTHER, 1: 2 * N_OTHER + 1},
        compiler_params=pltpu.CompilerParams(has_side_effects=pltpu.SideEffectType.DATAFLOW_SIDE_EFFECTING),
    )(pltpu.with_memory_space_constraint(src, pltpu.HBM),
      pltpu.with_memory_space_constraint(lax.empty(land, src.dtype), pltpu.HBM))
    return res[:2 * N_OTHER], res[2 * N_OTHER], res[2 * N_OTHER + 1], res[-1]


def split_wait(sems, src, land, after, name, per_chip=False):
    def body(src_ref, land_ref, *rest):
        sends, arrivals = _ici_halves(src_ref, land_ref, rest[:2 * N_OTHER], per_chip)
        for cp in sends:
            cp().wait_send()
        for cp in arrivals:
            cp().wait_recv()

    return pl.pallas_call(
        body, name=name, in_specs=(HBM, HBM) + (SEM,) * (2 * N_OTHER) + (ANY,) * len(after),
        out_specs=(HBM, HBM), out_shape=(pltpu.HBM(src.shape, src.dtype), pltpu.HBM(land.shape, land.dtype)),
        input_output_aliases={0: 0, 1: 1},
        compiler_params=pltpu.CompilerParams(has_side_effects=pltpu.SideEffectType.DATAFLOW_SIDE_EFFECTING),
    )(src, land, *sems, *after)


def handover(land, name, sums=None):
    n = N_OTHER + (sums is not None)

    def body(*refs):
        land_ref, send_sems, recv_sems = refs[0], refs[-2], refs[-1]
        x, y, c = _position()
        me = 2 * x + y
        sibling = (x, y, 1 - c)
        pieces = [(_half(land_ref, 2 * px + py, c), 2 * px + py) for px, py in _other_chips(x, y)]
        if sums is not None:
            pieces.append((refs[1].at[me], me))
        sends = [_remote(piece, _half(land_ref, chip, c), send_sems.at[j], recv_sems.at[j], sibling)
                 for j, (piece, chip) in enumerate(pieces)]
        for cp in sends:
            cp.start()
        for j, (_, chip) in enumerate(pieces):
            other = _half(land_ref, chip, 1 - c)
            _remote(other, other, send_sems.at[j], recv_sems.at[j], sibling).wait_recv()
        for cp in sends:
            cp.wait_send()

    args = (land,) + ((sums,) if sums is not None else ())
    return pl.pallas_call(
        body, name=name, in_specs=[ANY] * len(args), out_specs=ANY,
        out_shape=jax.ShapeDtypeStruct(land.shape, land.dtype), input_output_aliases={0: 0},
        scratch_shapes=[pltpu.SemaphoreType.DMA((n,)), pltpu.SemaphoreType.DMA((n,))],
    )(*args)


def swap_comm(grads):
    n = len(grads)

    def copies(srcs, gots, sems):
        send_sems, recv_sems = sems
        x, y, c = _position()
        out = []
        for i, (s, o) in enumerate(zip(srcs, gots)):
            rows = s.shape[1] // 2
            out.append(_remote(s.at[:, pl.ds((1 - c) * rows, rows), :], o, send_sems.at[i], recv_sems.at[i],
                               (x, y, 1 - c)))
        return out

    def first(srcs, gots, sems):
        for cp in copies(srcs, gots, sems):
            cp.start()

    def last(srcs, gots, sems):
        for cp in copies(srcs, gots, sems):
            cp.wait()

    return Comm(grads, [jax.ShapeDtypeStruct((N_CHIPS, g.shape[1] // 2, g.shape[2]), g.dtype) for g in grads],
                [pltpu.SemaphoreType.DMA((n,)), pltpu.SemaphoreType.DMA((n,))], first, None, last)


def add_halves(name, g, got, core):
    _, half, cols = got.shape
    mine = pl.BlockSpec((None, half, cols), lambda k, c_ref: (k, c_ref[0], 0))
    other = pl.BlockSpec((None, half, cols), lambda k, c_ref: (k, 0, 0))

    def body(c_ref, g_ref, got_ref, o_ref):
        o_ref[...] = (g_ref[...] + got_ref[...]).astype(BF16)

    return pl.pallas_call(
        body, name="add_halves_" + name,
        grid_spec=pltpu.PrefetchScalarGridSpec(num_scalar_prefetch=1, grid=(N_CHIPS,), in_specs=[mine, other],
                                               out_specs=other),
        out_shape=jax.ShapeDtypeStruct(got.shape, BF16),
        compiler_params=_params(("parallel",)),
    )(core, g, got)


def exchange_comm(parts):
    n = len(parts)

    def copies(srcs, outs, sems):
        send_sems, recv_sems, local_sems = sems
        x, y, c = _position()
        me = 2 * x + y
        sibling = (x, y, 1 - c)
        chips = _other_chips(x, y)
        locals_, sends, arrived, passed, from_sibling = [], [], [], [], []
        for i, (s, o) in enumerate(zip(srcs, outs)):
            locals_.append(_later(pltpu.make_async_copy, s.at[me], _half(o, me, c), local_sems.at[i]))
            sends.append(_later(_remote, s.at[me], _half(o, me, c), send_sems.at[i, 3], recv_sems.at[i, 3], sibling))
            other = _half(o, me, 1 - c)
            from_sibling.append(_later(_remote, other, other, send_sems.at[i, 3], recv_sems.at[i, 3], sibling))
        for j, (px, py) in enumerate(chips):
            for i, (s, o) in enumerate(zip(srcs, outs)):
                sends.append(_later(_remote, s.at[2 * px + py], _half(o, me, c), send_sems.at[i, j],
                                    recv_sems.at[i, j], (px, py, c)))
                got = _half(o, 2 * px + py, c)
                arrived.append(_later(_remote, got, got, send_sems.at[i, j], recv_sems.at[i, j], (px, py, c)))
                passed.append(_later(_remote, got, got, send_sems.at[i, 4 + j], recv_sems.at[i, 4 + j], sibling))
                other = _half(o, 2 * px + py, 1 - c)
                from_sibling.append(_later(_remote, other, other, send_sems.at[i, 4 + j], recv_sems.at[i, 4 + j],
                                           sibling))
        return locals_, sends, arrived, passed, from_sibling

    return Comm(parts, [jax.ShapeDtypeStruct((N_CHIPS, 2 * p.shape[1], p.shape[2]), p.dtype) for p in parts],
                [pltpu.SemaphoreType.DMA((n, 7)), pltpu.SemaphoreType.DMA((n, 7)), pltpu.SemaphoreType.DMA((n,))],
                *_two_level_phases(copies))


def small_comm(shares):
    n = len(shares)

    def copies(srcs, outs, sems):
        send_sems, recv_sems, local_sems = sems
        x, y, c = _position()
        me = 4 * x + 2 * y + c
        flips = [(fx, fy, fc) for fx in (0, 1) for fy in (0, 1) for fc in (0, 1)][1:]
        peers = [(1 - x if fx else x, 1 - y if fy else y, 1 - c if fc else c) for fx, fy, fc in flips]
        locals_, sends, arrived = [], [], []
        for i, (src_ref, out_ref) in enumerate(zip(srcs, outs)):
            locals_.append(_later(pltpu.make_async_copy, src_ref, out_ref.at[me], local_sems.at[i]))
            for j, (px, py, pc) in enumerate(peers):
                sends.append(_later(_remote, src_ref, out_ref.at[me], send_sems.at[i, j], recv_sems.at[i, j],
                                    (px, py, pc)))
                got = out_ref.at[4 * px + 2 * py + pc]
                arrived.append(_later(_remote, got, got, send_sems.at[i, j], recv_sems.at[i, j], (px, py, pc)))
        return locals_, sends, arrived

    def first(*refs):
        locals_, sends, _ = copies(*refs)
        for cp in locals_ + sends:
            cp().start()

    def last(*refs):
        locals_, sends, arrived = copies(*refs)
        for cp in arrived:
            cp().wait_recv()
        for cp in sends:
            cp().wait_send()
        for cp in locals_:
            cp().wait()

    return Comm(shares, [jax.ShapeDtypeStruct((N_DEV,) + s.shape, s.dtype) for s in shares],
                [pltpu.SemaphoreType.DMA((n, 7)), pltpu.SemaphoreType.DMA((n, 7)), pltpu.SemaphoreType.DMA((n,))],
                first, None, last)


def _adam_fn(w, g, m, v):
    m = ADAM_B1 * m + (1.0 - ADAM_B1) * g
    v = ADAM_B2 * v + (1.0 - ADAM_B2) * jnp.square(g)
    m_hat = m / (1.0 - ADAM_B1 ** ADAM_STEP)
    v_hat = v / (1.0 - ADAM_B2 ** ADAM_STEP)
    return -ADAM_LR * (m_hat / (jnp.sqrt(v_hat) + ADAM_EPS) + ADAM_WD * w), m, v


def adam_big(name, parts, w, m, v, comm=None):
    rows, cols = w.shape
    tm = _pick(rows, 384, 16)

    def fn(p0, p1, p2, p3, wv, mv, vv):
        g = ((p0.astype(F32) + p1.astype(F32)) + p2.astype(F32)) + p3.astype(F32)
        return (g,) + _adam_fn(wv, g, mv, vv)

    return rowwise(fn, [parts, w, m, v], [(cols, F32)] * 4, "adam_" + name, tm=tm, rows=rows, comm=comm)


def adam_small(name, gathered, w, m, v):
    def body(g_ref, w_ref, m_ref, v_ref, go_ref, d_ref, mo_ref, vo_ref):
        g = g_ref[0]
        for k in range(1, N_DEV):
            g = g + g_ref[k]
        go_ref[...] = g
        d_ref[...], mo_ref[...], vo_ref[...] = _adam_fn(w_ref[...], g, m_ref[...], v_ref[...])

    return pl.pallas_call(body, name=name, out_shape=[jax.ShapeDtypeStruct(w.shape, F32)] * 4,
                          compiler_params=_params())(gathered, w, m, v)


def _ssm_2d(name, t):
    t = t[0] if t.ndim > 2 else t
    if name in ("ssm_b_re", "ssm_b_im"):
        return t.transpose(0, 2, 1).reshape(SSM_W, 64)
    if name in ("ssm_c_re", "ssm_c_im"):
        return t.reshape(SSM_W, 64)
    return t.T if name == "ssm_d" else t


def _ssm_back(name, t):
    if name in ("ssm_b_re", "ssm_b_im"):
        return t.reshape(32, 16, 64).transpose(0, 2, 1)[None]
    if name in ("ssm_c_re", "ssm_c_im"):
        return t.reshape(1, 32, 16, 64)
    if name == "ssm_d":
        return t.T[None]
    return t if name == "ssm_log_dt" else t[None]


def adam_ssm(shares, w, m, v):
    n = len(w)

    def body(*refs):
        ins, outs = refs[:4 * n], refs[4 * n:]
        for i in range(n):
            g_ref, w_ref, m_ref, v_ref = (ins[k * n + i] for k in range(4))
            g = g_ref[0]
            for k in range(1, N_DEV):
                g = g + g_ref[k]
            outs[4 * i][...] = g
            outs[4 * i + 1][...], outs[4 * i + 2][...], outs[4 * i + 3][...] = _adam_fn(w_ref[...], g, m_ref[...],
                                                                                      v_ref[...])

    out_shape = [jax.ShapeDtypeStruct(t.shape, F32) for t in w for _ in range(4)]
    res = pl.pallas_call(body, name="adam_ssm", out_shape=out_shape, compiler_params=_params())(*shares, *w, *m, *v)
    return [res[4 * i:4 * i + 4] for i in range(n)]


def _pack_small(names, vals, rows, last=None):
    flat = [vals[n].reshape(-1) for n in names]
    if last is not None:
        flat.append(last.reshape(-1))
    flat = jnp.concatenate(flat)
    return jnp.pad(flat, (0, rows * LANES - flat.shape[0])).reshape(rows, LANES)


def _unpack_small(names, pack, shapes):
    flat, out, off = pack.reshape(-1), {}, 0
    for n in names:
        size = math.prod(shapes[n])
        out[n] = flat[off:off + size].reshape(shapes[n])
        off += size
    return out, flat[off]


def _to_slots(name, g, shard_shape):
    rows, cols = shard_shape
    if name in ROW_SHARDED:
        return g.reshape(N_CHIPS, rows, cols)
    return g.reshape(rows, N_CHIPS, cols).transpose(1, 0, 2)


def _from_slots(name, s):
    _, rows, cols = s.shape
    if name in ROW_SHARDED:
        return s.reshape(N_CHIPS * rows, cols)
    return s.transpose(1, 0, 2).reshape(rows, N_CHIPS * cols)


def kernel(x, norm_mix_g, w_in, ssm_a_re, ssm_a_im, ssm_log_dt, ssm_b_re, ssm_b_im, ssm_c_re, ssm_c_im, ssm_d, w_glu, w_attn_out, w_out, norm_ffn_g, w_ffn_gate, w_ffn_up, w_ffn_down, norm_final_g, loss_target, m_norm_mix_g, m_w_in, m_ssm_a_re, m_ssm_a_im, m_ssm_log_dt, m_ssm_b_re, m_ssm_b_im, m_ssm_c_re, m_ssm_c_im, m_ssm_d, m_w_glu, m_w_attn_out, m_w_out, m_norm_ffn_g, m_w_ffn_gate, m_w_ffn_up, m_w_ffn_down, m_norm_final_g, v_norm_mix_g, v_w_in, v_ssm_a_re, v_ssm_a_im, v_ssm_log_dt, v_ssm_b_re, v_ssm_b_im, v_ssm_c_re, v_ssm_c_im, v_ssm_d, v_w_glu, v_w_attn_out, v_w_out, v_norm_ffn_g, v_w_ffn_gate, v_w_ffn_up, v_w_ffn_down, v_norm_final_g):
    given = dict(locals())
    def local(name, prefix=""):
        t = given[prefix + name][0]
        return t.T if name in TRANSPOSED else t

    shard = {n: local(n) for n in BIG}
    shapes = {n: given[n].shape for n in WEIGHTS}

    small = {n: given[n] for n in SMALL}
    small_2d = dict(small)
    for n in ("ssm_a_re", "ssm_a_im", "ssm_b_re", "ssm_b_im", "ssm_c_re", "ssm_c_im", "ssm_d"):
        small_2d[n] = small[n][0]
    small_2d["norm_final_g"] = norm_final_g.reshape(1, D_MODEL)

    core = lax.axis_index("c").astype(jnp.int32).reshape(1)
    loss, grad_x, parts, ssm_shares, gs_norm, w_in_reduce = local_step(
        x.reshape(TOKENS, D_MODEL), loss_target.reshape(TOKENS, D_MODEL),
        {n: shard[n] for n in BIG}, small_2d, core)

    norm_gather = small_comm([_pack_small(NORM_SMALL, gs_norm, NORM_ROWS, last=loss)])
    small_out = [{} for _ in range(4)]
    ssm_in = [[_ssm_2d(n, given[p + n]) for n in SSM_SMALL] for p in ("", "m_", "v_")]
    for n, res in zip(SSM_SMALL, adam_ssm(ssm_shares, *ssm_in)):
        for kind, t in enumerate(res):
            small_out[kind][n] = _ssm_back(n, t)

    big_out, updated = {}, {}
    for n in BIG[1:] + BIG[:1]:
        if n == "w_in":
            sems, chip_sum, land = w_in_reduce
            behind = [updated[k][1] for k in BIG[1:]]
            chip_sum, land = split_wait(sems, chip_sum, land, behind, "w_in_reduce_wait", per_chip=True)
            land = handover(land, "w_in_reduce_handover", sums=chip_sum)
            me = 2 * lax.axis_index("x") + lax.axis_index("y")
            parts[n] = lax.dynamic_update_slice(land, lax.dynamic_slice_in_dim(chip_sum, me, 1, 0),
                                                (me, lax.axis_index("c") * chip_sum.shape[1], 0))
        res = adam_big(n, parts[n], shard[n], local(n, "m_"), local(n, "v_"),
                       comm=norm_gather if n == NORM_HOST else None)
        updated[n] = res[:4]
        if n == NORM_HOST:
            norm_shares = res[4]
            packs = [_pack_small(NORM_SMALL, {k: given[p + k] for k in NORM_SMALL}, NORM_ROWS)
                     for p in ("", "m_", "v_")]
            for kind, t in enumerate(adam_small("adam_norm_gains", norm_shares, *packs)):
                vals, after = _unpack_small(NORM_SMALL, t, shapes)
                small_out[kind].update(vals)
                if kind == 0:
                    total_loss = after
        big_out[n] = [(t.T if n in TRANSPOSED else t)[None] for t in updated[n]]

    outs = [total_loss, grad_x.reshape(LOCAL_BATCH, SEQ, D_MODEL)]
    for kind in range(4):
        for n in WEIGHTS:
            outs.append(big_out[n][kind] if n in BIG else small_out[kind][n])
    return tuple(outs)
```

```python
import functools
import math

import jax
import jax.numpy as jnp
import numpy as np
from jax import lax
from jax.experimental import pallas as pl
from jax.experimental.pallas import tpu as pltpu

F32 = jnp.float32
BF16 = jnp.bfloat16
MESH = pl.DeviceIdType.MESH

D_MODEL = 1024
SEQ = 2048
LOCAL_BATCH = 2
TOKENS = LOCAL_BATCH * SEQ
HEAD_DIM = 64
HEADS_PER_GROUP = 4
GROUP_W = HEADS_PER_GROUP * HEAD_DIM
N_GROUPS = 3
DILATIONS = (1, 4, 16)
ATTN_BLOCK = 128
ROPE_DIM = 16
ROPE_THETA = 500000.0
QKV_W = 3 * N_GROUPS * GROUP_W
SSM_W = 512
SSM_STATE_W = 2048
SSM_LANE_BLOCKS = 4
GATE_W = 2 * D_MODEL
D_FF = 2816
RMS_EPS = 1e-6
NEG_INF = -1e30
ADAM_LR, ADAM_B1, ADAM_B2, ADAM_EPS, ADAM_WD, ADAM_STEP = 0.001, 0.9, 0.999, 1e-08, 0.01, 10
N_CHIPS = 4
N_DEV = 8

VMEM_LIMIT = 56 * 1024 * 1024
LANES = 128


def _params(sem=None):
    return pltpu.CompilerParams(dimension_semantics=sem, vmem_limit_bytes=VMEM_LIMIT)


def _pick(n, cap, align=LANES):
    best = None
    for d in range(align, min(n, cap) + 1, align):
        if n % d == 0:
            best = d
    return n if best is None or n <= cap else best


_DIMS = {"nn": (((1,), (0,)), ((), ())), "nt": (((1,), (1,)), ((), ())), "tn": (((0,), (0,)), ((), ()))}


def _dot(a, b, mode):
    return lax.dot_general(a, b, _DIMS[mode], preferred_element_type=F32)


def matmul(a, b, mode, out_dtype, name, add=None, comm=None):
    if mode == "nn":
        (m, k), n = a.shape, b.shape[1]
    elif mode == "nt":
        (m, k), n = a.shape, b.shape[0]
    else:
        (k, m), n = a.shape, b.shape[1]
    tn = _pick(n, 1408 if mode != "tn" else 512)
    tk = _pick(k, 2816) if mode != "tn" else k
    tm = _pick(m, 1408)
    out_bytes = jnp.dtype(out_dtype).itemsize

    def need(tm_):
        return 2 * 2 * (tm_ * tk + tk * tn) + tm_ * tn * (4 + 2 * out_bytes + (8 if add is not None else 0))

    while need(tm) > 40 * 1024 * 1024 and tm % 256 == 0:
        tm //= 2
    nk = k // tk
    a_spec = {"nn": pl.BlockSpec((tm, tk), lambda i, j, kk: (i, kk)),
              "nt": pl.BlockSpec((tm, tk), lambda i, j, kk: (i, kk)),
              "tn": pl.BlockSpec((tk, tm), lambda i, j, kk: (kk, i))}[mode]
    b_spec = {"nn": pl.BlockSpec((tk, tn), lambda i, j, kk: (kk, j)),
              "nt": pl.BlockSpec((tn, tk), lambda i, j, kk: (j, kk)),
              "tn": pl.BlockSpec((tk, tn), lambda i, j, kk: (kk, j))}[mode]
    o_spec = pl.BlockSpec((tm, tn), lambda i, j, kk: (i, j))

    def body(a_ref, b_ref, *rest):
        if add is not None:
            add_ref, o_ref, acc_ref = rest
        else:
            o_ref, acc_ref = rest
        part = _dot(a_ref[...], b_ref[...], mode)
        if nk == 1:
            res = part if add is None else part + add_ref[...]
            o_ref[...] = res.astype(out_dtype)
            return
        kk = pl.program_id(2)

        @pl.when(kk == 0)
        def _():
            acc_ref[...] = part

        @pl.when(kk > 0)
        def _():
            acc_ref[...] += part

        @pl.when(kk == nk - 1)
        def _():
            res = acc_ref[...] if add is None else acc_ref[...] + add_ref[...]
            o_ref[...] = res.astype(out_dtype)

    in_specs = [a_spec, b_spec] + ([o_spec] if add is not None else [])
    args = (a, b) + ((add,) if add is not None else ())
    res = hosted_call(
        body, comm, name, (m // tm, n // tn, nk), in_specs, [o_spec], [jax.ShapeDtypeStruct((m, n), out_dtype)],
        [pltpu.VMEM((tm, tn) if nk > 1 else (8, LANES), F32)], args, ("parallel", "parallel", "arbitrary"))
    return res[0] if comm is None else res


def matmul_rows(a, b, name, fn, extra, outs, accs=(), add=None, comm=None, tm=512):
    a_list, b_list = (list(a), list(b)) if isinstance(a, (list, tuple)) else ([a], [b])
    m, n = a_list[0].shape[0], b_list[0].shape[1]
    n_mm = len(a_list)
    n_fixed = 2 * n_mm + (add is not None)
    row_spec = lambda cols: pl.BlockSpec((tm, cols), lambda i: (i, 0))
    in_specs = [row_spec(t.shape[1]) for t in a_list] + [pl.BlockSpec(t.shape, lambda i: (0, 0)) for t in b_list]
    in_specs += [row_spec(n)] if add is not None else []
    in_specs += [pl.BlockSpec(e.shape, lambda i: (0, 0)) if e.shape[0] == 1 else row_spec(e.shape[1]) for e in extra]
    out_specs = [row_spec(c) for c, _ in outs] + [pl.BlockSpec((1, c), lambda i: (0, 0)) for c in accs]
    out_shape = [jax.ShapeDtypeStruct((m, c), dt) for c, dt in outs] + [jax.ShapeDtypeStruct((1, c), F32) for c in accs]

    def body(*refs):
        rows = _dot(refs[0][...], refs[n_mm][...], "nn")
        for i in range(1, n_mm):
            rows = rows + _dot(refs[i][...], refs[n_mm + i][...], "nn")
        if add is not None:
            rows = rows + refs[2 * n_mm][...]
        n_in = n_fixed + len(extra)
        res = fn(rows, *[r[...] for r in refs[n_fixed:n_in]])
        for r, v in zip(refs[n_in:n_in + len(outs)], res[:len(outs)]):
            r[...] = v.astype(r.dtype)
        first = pl.program_id(0) == 0
        for r, v in zip(refs[n_in + len(outs):], res[len(outs):]):
            @pl.when(first)
            def _(r=r, v=v):
                r[...] = v

            @pl.when(jnp.logical_not(first))
            def _(r=r, v=v):
                r[...] += v

    args = tuple(a_list) + tuple(b_list) + ((add,) if add is not None else ()) + tuple(extra)
    return hosted_call(body, comm, name, (m // tm,), in_specs, out_specs, out_shape, [], args, ("arbitrary",))


def _merge_specs(tm):
    half = lambda blk: pl.BlockSpec((tm, D_MODEL), functools.partial(lambda i, blk_: (i, blk_), blk_=blk))
    return [half(0), half(1), pl.BlockSpec((tm, GROUP_W), lambda i: (i, 0)),
            pl.BlockSpec((GROUP_W, D_MODEL), lambda i: (0, 0)), pl.BlockSpec((tm, SSM_W), lambda i: (i, 0)),
            pl.BlockSpec((SSM_W, GATE_W), lambda i: (0, 0))]


def _merge_operands(g0, g1, at, wa, yg, wg):
    z = _dot(yg[...], wg[...], "nn")
    return (g0[...].astype(F32), g1[...].astype(F32), _dot(at[...], wa[...], "nn"), z[:, :D_MODEL], z[:, D_MODEL:])


def merge_out_proj(gl, attn_b, w_attn_out, yg, w_glu, w_out, x, g_ffn):
    tm = 512

    def body(g0, g1, at, wa, yg_ref, wg, w_ref, x_ref, g_ref, m_ref, x1_ref, h2_ref):
        merged = _merge_fn(*_merge_operands(g0, g1, at, wa, yg_ref, wg)).astype(BF16)
        m_ref[...] = merged
        x1 = _dot(merged, w_ref[...], "nn") + x_ref[...]
        x1_ref[...] = x1
        h2_ref[...] = _rms(x1, g_ref[...]).astype(BF16)

    rows = pl.BlockSpec((tm, D_MODEL), lambda i: (i, 0))
    whole = pl.BlockSpec((D_MODEL, D_MODEL), lambda i: (0, 0))
    gain = pl.BlockSpec((1, D_MODEL), lambda i: (0, 0))
    tok = lambda dt: jax.ShapeDtypeStruct((TOKENS, D_MODEL), dt)
    return pl.pallas_call(
        body, name="merge_out_proj", grid=(TOKENS // tm,), in_specs=_merge_specs(tm) + [whole, rows, gain],
        out_specs=[rows] * 3, out_shape=[tok(BF16), tok(F32), tok(BF16)], compiler_params=_params(("parallel",)),
    )(gl, gl, attn_b, w_attn_out, yg, w_glu, w_out, x, g_ffn)


def merge_bwd(dx1_b, w_out, gl, attn_b, w_attn_out, yg, w_glu, comm=None):
    tm = 512

    def body(dx_ref, w_ref, g0, g1, at, wa, yg_ref, wg, dgl_ref, dad_ref, dz_ref, dat_ref, dyg_ref):
        dm = _dot(dx_ref[...], w_ref[...], "nt")
        _, vjp = jax.vjp(_merge_fn, *_merge_operands(g0, g1, at, wa, yg_ref, wg))
        dg0, dg1, dad, dza, dzb = vjp(dm)
        dat_ref[...] = _dot(dad.astype(BF16), wa[...], "nt")
        dgl_ref[:, :D_MODEL] = dg0.astype(BF16)
        dgl_ref[:, D_MODEL:] = dg1.astype(BF16)
        dad_ref[...] = dad.astype(BF16)
        dz_ref[:, :D_MODEL] = dza.astype(BF16)
        dz_ref[:, D_MODEL:] = dzb.astype(BF16)
        dyg_ref[...] = _dot(dz_ref[...], wg[...], "nt")

    rows = pl.BlockSpec((tm, D_MODEL), lambda i: (i, 0))
    wide = pl.BlockSpec((tm, GATE_W), lambda i: (i, 0))
    whole = pl.BlockSpec((D_MODEL, D_MODEL), lambda i: (0, 0))
    return hosted_call(
        body, comm, "merge_bwd", (TOKENS // tm,), [rows, whole] + _merge_specs(tm),
        [wide, rows, wide, pl.BlockSpec((tm, GROUP_W), lambda i: (i, 0)), pl.BlockSpec((tm, SSM_W), lambda i: (i, 0))],
        [jax.ShapeDtypeStruct((TOKENS, GATE_W), BF16), jax.ShapeDtypeStruct((TOKENS, D_MODEL), BF16),
         jax.ShapeDtypeStruct((TOKENS, GATE_W), BF16), jax.ShapeDtypeStruct((TOKENS, GROUP_W), F32),
         jax.ShapeDtypeStruct((TOKENS, SSM_W), F32)], [],
        (dx1_b, w_out, gl, gl, attn_b, w_attn_out, yg, w_glu), ("arbitrary",))


FFN_TM, FFN_TN = 512, 1408


def ffn_in(h2, wg_t, wu_t, comm=None):
    def body(h_ref, wg_ref, wu_ref, a_ref, b_ref, act_ref):
        hv = h_ref[...]
        a, b = _dot(hv, wg_ref[...], "nt"), _dot(hv, wu_ref[...], "nt")
        a_ref[...] = a.astype(BF16)
        b_ref[...] = b.astype(BF16)
        act_ref[...] = _swiglu_fn(a, b).astype(BF16)

    rows = pl.BlockSpec((FFN_TM, D_MODEL), lambda i, j: (i, 0))
    wts = pl.BlockSpec((FFN_TN, D_MODEL), lambda i, j: (j, 0))
    out = pl.BlockSpec((FFN_TM, FFN_TN), lambda i, j: (i, j))
    return hosted_call(body, comm, "ffn_in", (TOKENS // FFN_TM, D_FF // FFN_TN), [rows, wts, wts], [out] * 3,
                       [jax.ShapeDtypeStruct((TOKENS, D_FF), BF16)] * 3, [], (h2, wg_t, wu_t),
                       ("parallel", "parallel"))


def ffn_in_bwd(dx2_b, wd, a, b):
    def body(dx_ref, wd_ref, a_ref, b_ref, da_ref, db_ref):
        dact = _dot(dx_ref[...], wd_ref[...], "nt")
        av, bv = a_ref[...].astype(F32), b_ref[...].astype(F32)
        sig = jax.nn.sigmoid(av)
        act = av * sig
        da_ref[...] = (dact * bv * (sig * (1.0 + av - act))).astype(BF16)
        db_ref[...] = (dact * act).astype(BF16)

    rows = pl.BlockSpec((FFN_TM, D_MODEL), lambda i, j: (i, 0))
    wts = pl.BlockSpec((FFN_TN, D_MODEL), lambda i, j: (j, 0))
    out = pl.BlockSpec((FFN_TM, FFN_TN), lambda i, j: (i, j))
    return pl.pallas_call(
        body, name="ffn_in_bwd", grid=(TOKENS // FFN_TM, D_FF // FFN_TN), in_specs=[rows, wts, out, out],
        out_specs=[out] * 2, out_shape=[jax.ShapeDtypeStruct((TOKENS, D_FF), BF16)] * 2,
        compiler_params=_params(("parallel", "parallel")),
    )(dx2_b, wd, a, b)


def mix_in_bwd(grads, weights, partial, x, g, skip, comm=None):
    n = len(grads)
    tm = 512

    def body(*refs):
        a_refs, b_refs = refs[:n], refs[n:2 * n]
        part_ref, x_ref, g_ref, skip_ref, gx_ref, dg_ref = refs[2 * n:]
        dh = part_ref[...]
        for a_ref, b_ref in zip(a_refs, b_refs):
            dh = dh + _dot(a_ref[...], b_ref[...], "nn")
        _, vjp = jax.vjp(_rms, x_ref[...], g_ref[...])
        dx, dg = vjp(dh)
        gx_ref[...] = dx + skip_ref[...]
        first = pl.program_id(0) == 0

        @pl.when(first)
        def _():
            dg_ref[...] = dg

        @pl.when(jnp.logical_not(first))
        def _():
            dg_ref[...] += dg

    rows = pl.BlockSpec((tm, D_MODEL), lambda i: (i, 0))
    gain = pl.BlockSpec((1, D_MODEL), lambda i: (0, 0))
    in_specs = [pl.BlockSpec((tm, a.shape[1]), lambda i: (i, 0)) for a in grads]
    in_specs += [pl.BlockSpec(b.shape, lambda i: (0, 0)) for b in weights]
    return hosted_call(
        body, comm, "mix_in_bwd", (TOKENS // tm,), in_specs + [rows, rows, gain, rows], [rows, gain],
        [jax.ShapeDtypeStruct((TOKENS, D_MODEL), F32), jax.ShapeDtypeStruct((1, D_MODEL), F32)], [],
        (*grads, *weights, partial, x, g, skip), ("arbitrary",))


def rowwise(fn, ins, outs, name, accs=(), tm=256, rows=TOKENS, comm=None):
    in_specs, args = [], []
    for item in ins:
        arr, width, blk = item if isinstance(item, tuple) else (item, None, 0)
        if arr.ndim == 3:
            for k in range(arr.shape[0]):
                in_specs.append(pl.BlockSpec((None, tm, arr.shape[2]), functools.partial(lambda i, k_: (k_, i, 0), k_=k)))
                args.append(arr)
            continue
        if arr.shape[0] == 1:
            in_specs.append(pl.BlockSpec(arr.shape, lambda i: (0, 0)))
        elif width is None:
            in_specs.append(pl.BlockSpec((tm, arr.shape[1]), lambda i: (i, 0)))
        else:
            in_specs.append(pl.BlockSpec((tm, width), functools.partial(lambda i, blk_: (i, blk_), blk_=blk)))
        args.append(arr)
    out_specs = [pl.BlockSpec((tm, c), lambda i: (i, 0)) for c, _ in outs]
    out_specs += [pl.BlockSpec((1, c), lambda i: (0, 0)) for c in accs]
    out_shape = [jax.ShapeDtypeStruct((rows, c), dt) for c, dt in outs]
    out_shape += [jax.ShapeDtypeStruct((1, c), F32) for c in accs]
    n_in, n_out = len(args), len(outs)
    c_ins, c_outs, c_sems = _comm_operands(comm)

    def body(*refs):
        refs, c_refs = _comm_refs(comm, refs, n_in, n_out + len(accs))
        step = pl.program_id(0)
        _comm_begin(comm, c_refs, step, rows // tm)
        res = fn(*[r[...] for r in refs[:n_in]])
        for r, v in zip(refs[n_in:n_in + n_out], res[:n_out]):
            r[...] = v.astype(r.dtype)
        first = step == 0
        for r, v in zip(refs[n_in + n_out:], res[n_out:]):
            @pl.when(first)
            def _(r=r, v=v):
                r[...] = v

            @pl.when(jnp.logical_not(first))
            def _(r=r, v=v):
                r[...] += v
        _comm_end(comm, c_refs, step, rows // tm)

    return pl.pallas_call(
        body, name=name, grid=(rows // tm,), in_specs=in_specs + [ANY] * len(c_ins),
        out_specs=out_specs + [ANY] * len(c_outs), out_shape=out_shape + c_outs, scratch_shapes=c_sems,
        compiler_params=_params(("arbitrary",)),
    )(*args, *c_ins)


def first_norm(x, g, others, comm=None):
    tm, n = 256, len(others)

    def body(x_ref, g_ref, *rest):
        srcs, h_ref, dsts = rest[:n], rest[n], rest[n + 1:]
        h_ref[...] = _rms(x_ref[...], g_ref[...]).astype(BF16)
        for k, (s, d) in enumerate(zip(srcs, dsts)):
            @pl.when(pl.program_id(0) == k)
            def _(s=s, d=d):
                d[...] = s[...].astype(BF16)

    rows = pl.BlockSpec((tm, D_MODEL), lambda i: (i, 0))
    whole = [pl.BlockSpec(a.shape, lambda i: (0, 0)) for a in others]
    return hosted_call(
        body, comm, "norm_mix", (TOKENS // tm,), [rows, pl.BlockSpec((1, D_MODEL), lambda i: (0, 0))] + whole,
        [rows] + whole, [jax.ShapeDtypeStruct((TOKENS, D_MODEL), BF16)]
        + [jax.ShapeDtypeStruct(a.shape, BF16) for a in others], [], (x, g, *others), ("arbitrary",))


def _rms(x, g):
    return x * lax.rsqrt(jnp.mean(x * x, axis=-1, keepdims=True) + RMS_EPS) * g


def _colsum(v):
    return jnp.sum(v, axis=0, keepdims=True)


PAIR_W = 2 * HEAD_DIM
N_PAIRS = HEADS_PER_GROUP // 2


def _qkv_order(w_t, back=False):
    dims = (N_PAIRS, N_GROUPS, 3) if back else (3, N_GROUPS, N_PAIRS)
    return w_t.reshape(dims + (PAIR_W, w_t.shape[1])).transpose(2, 1, 0, 3, 4).reshape(QKV_W, w_t.shape[1])


def _rope_tables():
    half = ROPE_DIM // 2
    inv = np.power(np.float32(ROPE_THETA), -np.arange(half, dtype=np.float32) * np.float32(2.0 / ROPE_DIM))
    ang = (np.arange(SEQ, dtype=np.float32)[:, None] * inv[None, :]).astype(np.float32)
    cos, sin = np.cos(ang), np.sin(ang)
    zeros = np.zeros((SEQ, HEAD_DIM - ROPE_DIM), np.float32)
    zh = np.zeros((SEQ, half), np.float32)
    c = np.concatenate([cos, cos, zeros + 1.0], axis=1)
    sa = np.concatenate([-sin, zh, zeros], axis=1)
    sb = np.concatenate([zh, sin, zeros], axis=1)
    return [jnp.asarray(np.tile(t, (1, 2)), F32) for t in (c, sa, sb)]


def _rope_fwd(x, c, sa, sb):
    return x * c + pltpu.roll(x, PAIR_W - 8, 1) * sa + pltpu.roll(x, 8, 1) * sb


def _rope_bwd(dy, c, sa, sb):
    return dy * c + pltpu.roll(dy * sb, PAIR_W - 8, 1) + pltpu.roll(dy * sa, 8, 1)


def _band_masks():
    row = lax.broadcasted_iota(jnp.int32, (ATTN_BLOCK, ATTN_BLOCK), 0)
    col = lax.broadcasted_iota(jnp.int32, (ATTN_BLOCK, ATTN_BLOCK), 1)
    return col <= row, col >= row


def _stack_rows(t):
    return jnp.concatenate([t, t], axis=0)


def _stack_heads(t, first_head):
    return jnp.concatenate([jnp.where(first_head, t, 0), jnp.where(first_head, 0, t)], axis=0)


def _per_head(fn):
    return jnp.concatenate([fn(slice(h * HEAD_DIM, (h + 1) * HEAD_DIM)) for h in range(2)], axis=1)


def _slab_spec(kind):
    return pl.BlockSpec((None, SEQ, PAIR_W), lambda b, p, g: (b, 0, p * 3 * N_GROUPS + g * 3 + kind))


_TABLE_SPEC = pl.BlockSpec((SEQ, PAIR_W), lambda b, p, g: (0, 0))
_PAIR_SPEC = pl.BlockSpec((None, SEQ, PAIR_W), lambda b, p, g: (b, 0, p))


def _block_rows(dil, r, n):
    return pl.ds(n * (ATTN_BLOCK * dil) + r, ATTN_BLOCK, stride=dil)


def proj_qkv(h, w_qkv_t, tables, comm=None):
    tm = 1024
    pair_w = QKV_W // N_PAIRS
    scale = HEAD_DIM ** -0.5

    def body(h_ref, w_ref, c_ref, sa_ref, sb_ref, o_ref):
        rows = _dot(h_ref[...], w_ref[...], "nt")
        c, sa, sb = c_ref[...], sa_ref[...], sb_ref[...]
        for blk in range(pair_w // PAIR_W):
            cols = slice(blk * PAIR_W, (blk + 1) * PAIR_W)
            x = rows[:, cols]
            if blk % 3 == 0:
                x = _rope_fwd(x, c, sa, sb) * scale
            elif blk % 3 == 1:
                x = _rope_fwd(x, c, sa, sb)
            o_ref[:, cols] = x

    table = pl.BlockSpec((tm, PAIR_W), lambda i, j, : (i % (SEQ // tm), 0))
    res = hosted_call(
        body, comm, "proj_qkv", (TOKENS // tm, N_PAIRS),
        [pl.BlockSpec((tm, D_MODEL), lambda i, j: (i, 0)), pl.BlockSpec((pair_w, D_MODEL), lambda i, j: (j, 0)),
         table, table, table],
        [pl.BlockSpec((tm, pair_w), lambda i, j: (i, j))], [jax.ShapeDtypeStruct((TOKENS, QKV_W), F32)], [],
        (h, w_qkv_t, *tables), ("parallel", "parallel"))
    return res[0] if comm is None else res


def attn_fwd(qkv, comm=None):
    def body(qs, ks, v_ref, attn_b_ref, attn_ref, lse_ref, o0, o1, o2, l0, l1, l2):
        g = pl.program_id(2)
        cur_mask, prev_mask = _band_masks()
        first_head = lax.broadcasted_iota(jnp.int32, (ATTN_BLOCK, PAIR_W), 1) < HEAD_DIM

        def run(dil, o_slab, l_slab):
            nb = SEQ // dil // ATTN_BLOCK

            def block(idx, carry):
                r, n = lax.div(idx, nb), lax.rem(idx, nb)
                cur, prev = _block_rows(dil, r, n), _block_rows(dil, r, jnp.maximum(n - 1, 0))
                q = qs[cur, :].astype(BF16)
                kc, kp = ks[cur, :].astype(BF16), ks[prev, :].astype(BF16)
                vc, vp = v_ref[cur, :].astype(BF16), v_ref[prev, :].astype(BF16)
                q2 = _stack_heads(q, first_head)
                mask = _stack_rows(jnp.concatenate([jnp.logical_and(prev_mask, n > 0), cur_mask], axis=1))
                s2 = jnp.where(mask, _dot(q2, jnp.concatenate([kp, kc], axis=0), "nt"), NEG_INF)
                m = jnp.max(s2, axis=-1, keepdims=True)
                vcat, two = jnp.concatenate([vp, vc], axis=0), _stack_rows(first_head)
                vext = jnp.concatenate([jnp.where(two, vcat, 1), jnp.where(two, 1, vcat)], axis=1)
                r2 = _dot(jnp.exp(s2 - m).astype(BF16), vext, "nn")
                r0, r1 = r2[:ATTN_BLOCK, :PAIR_W], r2[ATTN_BLOCK:, PAIR_W:]
                num = jnp.where(first_head, r0, r1)
                den = pltpu.roll(jnp.where(first_head, r1, r0), HEAD_DIM, 1)
                o_slab[cur, :] = num / den
                l_slab[cur, :] = jnp.where(first_head, m[:ATTN_BLOCK], m[ATTN_BLOCK:]) + jnp.log(den)
                return carry

            lax.fori_loop(0, SEQ // ATTN_BLOCK, block, 0, unroll=4)

        for gi, (o_slab, l_slab) in enumerate(((o0, l0), (o1, l1), (o2, l2))):
            @pl.when(g == gi)
            def _(gi=gi, o_slab=o_slab, l_slab=l_slab):
                run(DILATIONS[gi], o_slab, l_slab)

        @pl.when(g == N_GROUPS - 1)
        def _():
            a, b, cc = l0[...], l1[...], l2[...]
            m = jnp.maximum(jnp.maximum(a, b), cc)
            e0, e1, e2 = jnp.exp(a - m), jnp.exp(b - m), jnp.exp(cc - m)
            tot = e0 + e1 + e2
            attn = (e0 * o0[...] + e1 * o1[...] + e2 * o2[...]) / tot
            attn_ref[...] = attn
            attn_b_ref[...] = attn.astype(BF16)
            lse_ref[...] = m + jnp.log(tot)

    shape = (LOCAL_BATCH, SEQ, GROUP_W)
    slab = pltpu.VMEM((SEQ, PAIR_W), F32)
    return hosted_call(
        body, comm, "attn_fwd", (LOCAL_BATCH, N_PAIRS, N_GROUPS),
        [_slab_spec(0), _slab_spec(1), _slab_spec(2)], [_PAIR_SPEC] * 3,
        [jax.ShapeDtypeStruct(shape, BF16), jax.ShapeDtypeStruct(shape, F32), jax.ShapeDtypeStruct(shape, F32)],
        [slab] * 6, (qkv, qkv, qkv), ("parallel", "parallel", "arbitrary"))


def attn_bwd(qkv, tables, dattn, attn, lse, comm=None):
    scale = HEAD_DIM ** -0.5

    def body(qs, ks, v_ref, c_ref, sa_ref, sb_ref, do_ref, out_ref, lse_ref, dqkv_ref, dl, dq_s, dk_s, dv_s):
        g = pl.program_id(2)
        c, sa, sb = c_ref[...], sa_ref[...], sb_ref[...]

        @pl.when(g == 0)
        def _():
            prod = do_ref[...] * out_ref[...]
            dl[...] = _per_head(
                lambda sl: jnp.broadcast_to(jnp.sum(prod[:, sl], axis=-1, keepdims=True), (SEQ, HEAD_DIM)))

        cur_mask, prev_mask = _band_masks()
        first_head = lax.broadcasted_iota(jnp.int32, (ATTN_BLOCK, PAIR_W), 1) < HEAD_DIM

        def run(dil):
            nb = SEQ // dil // ATTN_BLOCK

            def block(idx, carry):
                r, n = lax.div(idx, nb), lax.rem(idx, nb)
                cur = _block_rows(dil, r, n)
                prev = _block_rows(dil, r, jnp.maximum(n - 1, 0))
                nxt = _block_rows(dil, r, jnp.minimum(n + 1, nb - 1))
                q0, q1 = qs[cur, :].astype(BF16), qs[nxt, :].astype(BF16)
                kp, kc = ks[prev, :].astype(BF16), ks[cur, :].astype(BF16)
                vp, vc = v_ref[prev, :].astype(BF16), v_ref[cur, :].astype(BF16)
                do0, do1 = do_ref[cur, :].astype(BF16), do_ref[nxt, :].astype(BF16)
                lse0, lse1, dl0, dl1 = lse_ref[cur, :], lse_ref[nxt, :], dl[cur, :], dl[nxt, :]
                has_prev = jnp.logical_and(prev_mask, n > 0)
                has_next = jnp.logical_and(prev_mask, n < nb - 1)

                def per_row(t):
                    return jnp.concatenate([t[:, 0:1], t[:, HEAD_DIM:HEAD_DIM + 1]], axis=0)

                q20, q21 = _stack_heads(q0, first_head), _stack_heads(q1, first_head)
                do20, do21 = _stack_heads(do0, first_head), _stack_heads(do1, first_head)
                kcat, vcat = jnp.concatenate([kp, kc], axis=0), jnp.concatenate([vp, vc], axis=0)
                mask0 = _stack_rows(jnp.concatenate([has_prev, cur_mask], axis=1))
                p0 = jnp.where(mask0, jnp.exp(_dot(q20, kcat, "nt") - per_row(lse0)), 0.0)
                ds0 = (p0 * (_dot(do20, vcat, "nt") - per_row(dl0))).astype(BF16)
                p1 = jnp.where(_stack_rows(has_next), jnp.exp(_dot(q21, kc, "nt") - per_row(lse1)), 0.0)
                ds1 = (p1 * (_dot(do21, vc, "nt") - per_row(dl1))).astype(BF16)
                dq2 = _dot(ds0, kcat, "nn")
                dq_s[cur, :] = jnp.where(first_head, dq2[:ATTN_BLOCK], dq2[ATTN_BLOCK:])
                ds_cur = jnp.concatenate([ds0[:, ATTN_BLOCK:], ds1], axis=0)
                p_cur = jnp.concatenate([p0[:, ATTN_BLOCK:], p1], axis=0).astype(BF16)
                dk_s[cur, :] = _dot(ds_cur, jnp.concatenate([q20, q21], axis=0), "tn")
                dv_s[cur, :] = _dot(p_cur, jnp.concatenate([do20, do21], axis=0), "tn")
                return carry

            lax.fori_loop(0, SEQ // ATTN_BLOCK, block, 0, unroll=2)

        for gi in range(N_GROUPS):
            @pl.when(g == gi)
            def _(gi=gi):
                run(DILATIONS[gi])

        dqkv_ref[:, 0:PAIR_W] = _rope_bwd(dq_s[...] * scale, c, sa, sb).astype(BF16)
        dqkv_ref[:, PAIR_W:2 * PAIR_W] = _rope_bwd(dk_s[...], c, sa, sb).astype(BF16)
        dqkv_ref[:, 2 * PAIR_W:] = dv_s[...].astype(BF16)

    slab = pltpu.VMEM((SEQ, PAIR_W), F32)
    return hosted_call(
        body, comm, "attn_bwd", (LOCAL_BATCH, N_PAIRS, N_GROUPS),
        [_slab_spec(0), _slab_spec(1), _slab_spec(2), _TABLE_SPEC, _TABLE_SPEC, _TABLE_SPEC,
         _PAIR_SPEC, _PAIR_SPEC, _PAIR_SPEC],
        [pl.BlockSpec((None, SEQ, 3 * PAIR_W), lambda b, p, g: (b, 0, p * N_GROUPS + g))],
        [jax.ShapeDtypeStruct((LOCAL_BATCH, SEQ, QKV_W), BF16)],
        [slab] * 4, (qkv, qkv, qkv, *tables, dattn, attn, lse), ("parallel", "parallel", "arbitrary"))


def _discretize(lr, li, log_dt, br, bi):
    dt = jnp.exp(log_dt)
    mag = jnp.exp(lr * dt)
    ab_re, ab_im = mag * jnp.cos(li * dt), mag * jnp.sin(li * dt)
    den = lr * lr + li * li
    nr, ni = ab_re - 1.0, ab_im
    f_re = (nr * lr + ni * li) / den
    f_im = (ni * lr - nr * li) / den
    return ab_re, ab_im, f_re[None] * br - f_im[None] * bi, f_re[None] * bi + f_im[None] * br


def ssm_prep(lr, li, log_dt, br, bi):
    def body(lr_ref, li_ref, dt_ref, br_ref, bi_ref, *outs):
        for o, v in zip(outs, _discretize(lr_ref[...], li_ref[...], dt_ref[...], br_ref[...], bi_ref[...])):
            o[...] = v
    shapes = [lr, li, br, bi]
    return pl.pallas_call(body, name="ssm_prep",
                          out_shape=[jax.ShapeDtypeStruct(s.shape, F32) for s in shapes])(lr, li, log_dt, br, bi)


def ssm_prep_bwd(lr, li, log_dt, br, bi, g_ab_re, g_ab_im, g_bb_re, g_bb_im):
    def body(lr_ref, li_ref, dt_ref, br_ref, bi_ref, g0, g1, g2, g3, *outs):
        _, vjp = jax.vjp(_discretize, lr_ref[...], li_ref[...], dt_ref[...], br_ref[...], bi_ref[...])
        for o, v in zip(outs, vjp((g0[...], g1[...], g2[...], g3[...]))):
            o[...] = v
    shapes = [lr, li, log_dt, br, bi]
    return pl.pallas_call(body, name="ssm_prep_bwd",
                          out_shape=[jax.ShapeDtypeStruct(s.shape, F32) for s in shapes])(
        lr, li, log_dt, br, bi, g_ab_re, g_ab_im, g_bb_re, g_bb_im)


def _block_diag(t):
    per = SSM_STATE_W // SSM_LANE_BLOCKS // 64
    g = t.transpose(1, 0, 2).reshape(SSM_LANE_BLOCKS, per, 16, 64)
    eye = jnp.eye(per, dtype=t.dtype)
    return jnp.einsum("jgcn,gh->jgchn", g, eye).reshape(SSM_LANE_BLOCKS, per * 16, per * 64)


def _block_diag_t(m):
    per = SSM_STATE_W // SSM_LANE_BLOCKS // 64
    m5 = m.reshape(SSM_LANE_BLOCKS, per, 16, per, 64)
    d = jnp.einsum("jgchn,gh->jgcn", m5, jnp.eye(per, dtype=m.dtype))
    return d.reshape(SSM_LANE_BLOCKS * per, 16, 64).transpose(1, 0, 2)


def _cmul(ar, ai, br, bi):
    return ar * br - ai * bi, ar * bi + ai * br


def _power_tables(ar, ai, reverse):
    width = ar.shape[1]
    row = lax.broadcasted_iota(jnp.int32, (8, width), 0)
    pows = [(ar, ai)]
    for _ in range(7):
        pows.append(_cmul(pows[-1][0], pows[-1][1], ar, ai))
    steps = []
    for k in (1, 2, 4):
        keep = (row >= k) if not reverse else (row < 8 - k)
        steps.append((jnp.where(keep, pows[k - 1][0], 0.0), jnp.where(keep, pows[k - 1][1], 0.0)))
    cr = jnp.zeros((8, width), F32)
    ci = jnp.zeros((8, width), F32)
    for i in range(8):
        pr, pi = pows[i] if not reverse else pows[7 - i]
        cr = jnp.where(row == i, pr, cr)
        ci = jnp.where(row == i, pi, ci)
    return steps, (cr, ci)


SCAN_CHUNK = 2048
STATE_BLOCK = SSM_STATE_W // SSM_LANE_BLOCKS
CHAN_BLOCK = SSM_W // SSM_LANE_BLOCKS


def ssm_fwd(u, ab_re, ab_im, bb_re, bb_im, cb_re, cb_im, d_skip, comm=None):
    nt = SEQ // SCAN_CHUNK
    chan = pl.BlockSpec((None, SCAN_CHUNK, CHAN_BLOCK), lambda b, j, t: (b, t, j))
    state = pl.BlockSpec((None, SCAN_CHUNK, STATE_BLOCK), lambda b, j, t: (b, t, j))
    mat = pl.BlockSpec((None, CHAN_BLOCK, STATE_BLOCK), lambda b, j, t: (j, 0, 0))
    lane = pl.BlockSpec((1, STATE_BLOCK), lambda b, j, t: (0, j))
    dsp = pl.BlockSpec((1, CHAN_BLOCK), lambda b, j, t: (0, j))

    def body(u_ref, ar_ref, ai_ref, bbr_ref, bbi_ref, cbr_ref, cbi_ref, d_ref, y_ref, yg_ref, xr_ref, xi_ref,
             car_r, car_i):
        @pl.when(pl.program_id(2) == 0)
        def _():
            car_r[...] = jnp.zeros_like(car_r)
            car_i[...] = jnp.zeros_like(car_i)

        steps, (pr, pi) = _power_tables(ar_ref[...], ai_ref[...], reverse=False)
        uf = u_ref[...]
        ub = uf.astype(BF16)
        xr_ref[...] = _dot(ub, bbr_ref[...], "nn")
        xi_ref[...] = _dot(ub, bbi_ref[...], "nn")

        def tile(i, carry):
            cr, ci = carry
            sl = pl.ds(pl.multiple_of(i * 8, 8), 8)
            br, bi = xr_ref[sl, :], xi_ref[sl, :]
            for k, (sr, si) in zip((1, 2, 4), steps):
                tr, ti = _cmul(sr, si, pltpu.roll(br, k, 0), pltpu.roll(bi, k, 0))
                br, bi = br + tr, bi + ti
            tr, ti = _cmul(pr, pi, cr, ci)
            br, bi = br + tr, bi + ti
            xr_ref[sl, :] = br
            xi_ref[sl, :] = bi
            return br[7:8, :], bi[7:8, :]

        cr, ci = lax.fori_loop(0, SCAN_CHUNK // 8, tile, (car_r[0:1, :], car_i[0:1, :]), unroll=4)
        car_r[0:1, :] = cr
        car_i[0:1, :] = ci
        y = (_dot(xr_ref[...].astype(BF16), cbr_ref[...], "nt") - _dot(xi_ref[...].astype(BF16), cbi_ref[...], "nt")
             + d_ref[...] * uf)
        y_ref[...] = y
        yg_ref[...] = jax.nn.gelu(y).astype(BF16)

    return hosted_call(
        body, comm, "ssm_fwd", (LOCAL_BATCH, SSM_LANE_BLOCKS, nt),
        [chan, lane, lane, mat, mat, mat, mat, dsp], [chan, chan, state, state],
        [jax.ShapeDtypeStruct((LOCAL_BATCH, SEQ, SSM_W), F32), jax.ShapeDtypeStruct((LOCAL_BATCH, SEQ, SSM_W), BF16),
         jax.ShapeDtypeStruct((LOCAL_BATCH, SEQ, SSM_STATE_W), F32),
         jax.ShapeDtypeStruct((LOCAL_BATCH, SEQ, SSM_STATE_W), F32)],
        [pltpu.VMEM((8, STATE_BLOCK), F32), pltpu.VMEM((8, STATE_BLOCK), F32)],
        (u, ab_re, ab_im, bb_re, bb_im, cb_re, cb_im, d_skip), ("parallel", "parallel", "arbitrary"))


def ssm_bwd(dyg, y, u, xr, xi, ab_re, ab_im, bb_re, bb_im, cb_re, cb_im, d_skip, comm=None):
    nt = SEQ // SCAN_CHUNK
    ntile = SCAN_CHUNK // 8

    def rev(t):
        return nt - 1 - t

    chan = pl.BlockSpec((None, SCAN_CHUNK, CHAN_BLOCK), lambda j, b, t: (b, rev(t), j))
    state = pl.BlockSpec((None, SCAN_CHUNK, STATE_BLOCK), lambda j, b, t: (b, rev(t), j))
    before = pl.BlockSpec((None, 8, STATE_BLOCK), lambda j, b, t: (b, jnp.maximum(rev(t) * ntile - 1, 0), j))
    mat = pl.BlockSpec((None, CHAN_BLOCK, STATE_BLOCK), lambda j, b, t: (j, 0, 0))
    lane = pl.BlockSpec((1, STATE_BLOCK), lambda j, b, t: (0, j))
    lane8 = pl.BlockSpec((8, STATE_BLOCK), lambda j, b, t: (0, j))
    dsp = pl.BlockSpec((1, CHAN_BLOCK), lambda j, b, t: (0, j))

    def body(dyg_ref, y_ref, u_ref, xr_ref, xi_ref, xrb_ref, xib_ref, ar_ref, ai_ref, bbr_ref, bbi_ref, cbr_ref,
             cbi_ref, d_ref, du_ref, dcbr_ref, dcbi_ref, dbbr_ref, dbbi_ref, dd_ref, dar_ref, dai_ref,
             lam_r, lam_i, car_r, car_i):
        b, t = pl.program_id(1), pl.program_id(2)
        first = jnp.logical_and(b == 0, t == 0)

        @pl.when(t == 0)
        def _():
            car_r[...] = jnp.zeros_like(car_r)
            car_i[...] = jnp.zeros_like(car_i)

        @pl.when(first)
        def _():
            for r in (dcbr_ref, dcbi_ref, dbbr_ref, dbbi_ref, dd_ref, dar_ref, dai_ref):
                r[...] = jnp.zeros_like(r)

        steps, (pr, pi) = _power_tables(ar_ref[...], -ai_ref[...], reverse=True)
        uf = u_ref[...]
        _, gelu_vjp = jax.vjp(jax.nn.gelu, y_ref[...])
        dy = gelu_vjp(dyg_ref[...])[0]
        dyb = dy.astype(BF16)
        dd_ref[...] += _colsum(dy * uf)
        lam_r[...] = _dot(dyb, cbr_ref[...], "nn")
        lam_i[...] = -_dot(dyb, cbi_ref[...], "nn")
        dcbr_ref[...] += _dot(dyb, xr_ref[...].astype(BF16), "tn")
        dcbi_ref[...] -= _dot(dyb, xi_ref[...].astype(BF16), "tn")
        row0 = lax.broadcasted_iota(jnp.int32, (8, STATE_BLOCK), 0) == 0
        has_before = rev(t) > 0
        xrb = jnp.where(has_before, xrb_ref[...], 0.0)
        xib = jnp.where(has_before, xib_ref[...], 0.0)

        def tile(s, carry):
            cr, ci, acc_r, acc_i = carry
            i = ntile - 1 - s
            sl = pl.ds(pl.multiple_of(i * 8, 8), 8)
            gr, gi = lam_r[sl, :], lam_i[sl, :]
            for k, (sr, si) in zip((1, 2, 4), steps):
                tr, ti = _cmul(sr, si, pltpu.roll(gr, 8 - k, 0), pltpu.roll(gi, 8 - k, 0))
                gr, gi = gr + tr, gi + ti
            tr, ti = _cmul(pr, pi, cr, ci)
            gr, gi = gr + tr, gi + ti
            lam_r[sl, :] = gr
            lam_i[sl, :] = gi
            sp = pl.ds(pl.multiple_of(jnp.maximum(i - 1, 0) * 8, 8), 8)
            pvr = jnp.where(i > 0, xr_ref[sp, :], xrb)
            pvi = jnp.where(i > 0, xi_ref[sp, :], xib)
            xsr = jnp.where(row0, pltpu.roll(pvr, 1, 0), pltpu.roll(xr_ref[sl, :], 1, 0))
            xsi = jnp.where(row0, pltpu.roll(pvi, 1, 0), pltpu.roll(xi_ref[sl, :], 1, 0))
            acc_r = acc_r + xsr * gr + xsi * gi
            acc_i = acc_i + xsr * gi - xsi * gr
            return gr[0:1, :], gi[0:1, :], acc_r, acc_i

        zero = jnp.zeros((8, STATE_BLOCK), F32)
        cr, ci, acc_r, acc_i = lax.fori_loop(0, ntile, tile, (car_r[0:1, :], car_i[0:1, :], zero, zero), unroll=2)
        car_r[0:1, :] = cr
        car_i[0:1, :] = ci
        dar_ref[...] += acc_r
        dai_ref[...] += acc_i
        lrb, lib = lam_r[...].astype(BF16), lam_i[...].astype(BF16)
        du = _dot(lrb, bbr_ref[...], "nt") + _dot(lib, bbi_ref[...], "nt") + d_ref[...] * dy
        du_ref[...] = du.astype(BF16)
        ub = uf.astype(BF16)
        dbbr_ref[...] += _dot(ub, lrb, "tn")
        dbbi_ref[...] += _dot(ub, lib, "tn")

    mat_shape = jax.ShapeDtypeStruct((SSM_LANE_BLOCKS, CHAN_BLOCK, STATE_BLOCK), F32)
    return hosted_call(
        body, comm, "ssm_bwd", (SSM_LANE_BLOCKS, LOCAL_BATCH, nt),
        [chan, chan, chan, state, state, before, before, lane, lane, mat, mat, mat, mat, dsp],
        [chan, mat, mat, mat, mat, dsp, lane8, lane8],
        [jax.ShapeDtypeStruct((LOCAL_BATCH, SEQ, SSM_W), BF16), mat_shape, mat_shape, mat_shape, mat_shape,
         jax.ShapeDtypeStruct((1, SSM_W), F32), jax.ShapeDtypeStruct((8, SSM_STATE_W), F32),
         jax.ShapeDtypeStruct((8, SSM_STATE_W), F32)],
        [pltpu.VMEM((SCAN_CHUNK, STATE_BLOCK), F32), pltpu.VMEM((SCAN_CHUNK, STATE_BLOCK), F32),
         pltpu.VMEM((8, STATE_BLOCK), F32), pltpu.VMEM((8, STATE_BLOCK), F32)],
        (dyg, y, u, xr, xi, xr, xi, ab_re, ab_im, bb_re, bb_im, cb_re, cb_im, d_skip),
        ("parallel", "arbitrary", "arbitrary"))


def _merge_fn(g0, g1, attn_d, za, zb):
    return jax.nn.sigmoid(g0) * attn_d + jax.nn.sigmoid(g1) * (za * jax.nn.sigmoid(zb))


def _swiglu_fn(a, b):
    return jax.nn.silu(a) * b


def _own_slot(slots, shard):
    me = 2 * lax.axis_index("x") + lax.axis_index("y")
    mine = lax.broadcasted_iota(jnp.int32, (N_CHIPS, 1, 1), 0) == me
    return jnp.where(mine, shard[None], slots)


def _reduce_start(names, gw, shard_shapes):
    return swap_comm([_to_slots(n, gw[n], shard_shapes[n]) for n in names])


def _reduce_chip(names, slots, got, core):
    return exchange_comm([add_halves(n, g, r, core) for n, g, r in zip(names, slots, got)])


def local_step(x, target, shards, small, core):
    g_mix, g_ffn, g_final = small["norm_mix_g"], small["norm_ffn_g"], small["norm_final_g"]
    tables = _rope_tables()
    seqs = lambda t: t.reshape(LOCAL_BATCH, SEQ, t.shape[-1])
    toks = lambda t: t.reshape(TOKENS, t.shape[-1])
    shard_shapes = {n: s.shape for n, s in shards.items()}
    w = {}

    def gather(names):
        return gather_comm([shards[n] for n in names])

    def arrived(names, slots, own=None):
        for n, s in zip(names, slots):
            w[n] = _from_slots(n, s if own is None else _own_slot(s, own))

    later = [n for n in BIG if n != "w_in"]
    sems, w_in_shard, land, token = split_start(shards["w_in"].astype(BF16), "w_in_gather_start")
    zero = token[0, 0]
    h, *rest = first_norm(x, g_mix + zero, [shards[n] for n in later])
    shards = dict(shards)
    shards.update(zip(later, rest))
    br_t = small["ssm_b_re"].transpose(2, 0, 1)
    bi_t = small["ssm_b_im"].transpose(2, 0, 1)
    log_dt = small["ssm_log_dt"].reshape(32, 1)
    ab_re, ab_im, bb_re_t, bb_im_t = ssm_prep(small["ssm_a_re"] + zero, small["ssm_a_im"], log_dt, br_t, bi_t)
    ab = [ab_re.reshape(1, SSM_STATE_W), ab_im.reshape(1, SSM_STATE_W)]
    bb = [_block_diag(bb_re_t).astype(BF16), _block_diag(bb_im_t).astype(BF16)]
    cb = [_block_diag((small["ssm_c_re"] + zero).transpose(1, 0, 2)).astype(BF16),
          _block_diag((small["ssm_c_im"] + zero).transpose(1, 0, 2)).astype(BF16)]
    d_skip = small["ssm_d"].reshape(1, SSM_W)
    w_in_shard, land = split_wait(sems, w_in_shard, land, [h] + bb + cb, "w_in_gather_wait")
    arrived(["w_in"], [handover(land, "w_in_handover")], own=w_in_shard)
    w_qkv, w_u, w_gate = _qkv_order(w["w_in"][:QKV_W]), w["w_in"][QKV_W:QKV_W + SSM_W], w["w_in"][QKV_W + SSM_W:]
    qkv, *slots = proj_qkv(h, w_qkv, tables, comm=gather(["w_attn_out", "w_glu"]))
    arrived(["w_attn_out", "w_glu"], slots)
    qkv = seqs(qkv)
    u = seqs(matmul(h, w_u, "nt", F32, "proj_u"))
    gl, *slots = matmul(h, w_gate, "nt", BF16, "proj_gate", comm=gather(["w_out"]))
    arrived(["w_out"], slots)
    attn_b, attn, lse, *slots = attn_fwd(qkv, comm=gather(["w_ffn_gate"]))
    arrived(["w_ffn_gate"], slots)
    attn_b = toks(attn_b)
    y, yg, xr, xi, *slots = ssm_fwd(u, *ab, *bb, *cb, d_skip, comm=gather(["w_ffn_up"]))
    arrived(["w_ffn_up"], slots)
    yg2 = toks(yg)
    merged, x1, h2 = merge_out_proj(gl, attn_b, w["w_attn_out"], yg2, w["w_glu"], w["w_out"], x, g_ffn)
    a, b, act, *slots = ffn_in(h2, w["w_ffn_gate"], w["w_ffn_up"], comm=gather(["w_ffn_down"]))
    arrived(["w_ffn_down"], slots)

    def final_fn(xv, g, tgt):
        yv, vjp = jax.vjp(_rms, xv, g)
        err = yv - tgt
        dx, dg = vjp(err * (1.0 / D_MODEL))
        loss = 0.5 * jnp.sum(jnp.mean(err * err, axis=-1, keepdims=True), axis=0, keepdims=True)
        return dx, dx, dg, jnp.broadcast_to(loss, (1, LANES))

    dx2, dx2_b, dg_final, loss = matmul_rows(act, w["w_ffn_down"], "ffn_down_loss", final_fn, [g_final, target],
                                             [(D_MODEL, F32), (D_MODEL, BF16)], accs=(D_MODEL, LANES), add=x1)
    gw, parts = {}, {}
    gw["w_ffn_down"] = matmul(act, dx2_b, "tn", F32, "d_ffn_down")
    da_b, db_b = ffn_in_bwd(dx2_b, w["w_ffn_down"], a, b)
    gw["w_ffn_gate"] = matmul(da_b, h2, "tn", F32, "d_ffn_gate")
    gw["w_ffn_up"] = matmul(db_b, h2, "tn", F32, "d_ffn_up")
    ffn = ["w_ffn_down", "w_ffn_gate", "w_ffn_up"]
    swap = _reduce_start(ffn, gw, shard_shapes)

    def norm_bwd(dh, xv, g, skip):
        _, vjp = jax.vjp(_rms, xv, g)
        dx, dg = vjp(dh)
        dx = dx + skip
        return dx, dx, dg

    dx1, dx1_b, dg_ffn, *got = matmul_rows(
        [da_b, db_b], [w["w_ffn_gate"], w["w_ffn_up"]], "d_h2_norm", norm_bwd, [x1, g_ffn, dx2],
        [(D_MODEL, F32), (D_MODEL, BF16)], accs=(D_MODEL,), comm=swap, tm=256)
    ffn_exchange = [_reduce_chip(ffn[:2], swap.ins[:2], got[:2], core)]
    ffn_up_exchange = _reduce_chip(ffn[2:], swap.ins[2:], got[2:], core)
    gw["w_out"] = matmul(merged, dx1_b, "tn", F32, "d_out")
    dgl_b, dattn_d_b, dz_b, dattn, dyg, parts["w_ffn_up"] = merge_bwd(
        dx1_b, w["w_out"], gl, attn_b, w["w_attn_out"], yg2, w["w_glu"], comm=ffn_up_exchange)
    dattn, dyg = seqs(dattn), seqs(dyg)
    gw["w_attn_out"] = matmul(attn_b, dattn_d_b, "tn", F32, "d_attn_out")
    gw["w_glu"] = matmul(yg2, dz_b, "tn", F32, "d_glu")
    mixer = ["w_out", "w_attn_out", "w_glu"]
    swap = _reduce_start(mixer, gw, shard_shapes)
    du_b, dcb_re, dcb_im, dbb_re, dbb_im, dd, da_re8, da_im8, *rest = ssm_bwd(
        dyg, y, u, xr, xi, *ab, *bb, *cb, d_skip, comm=join_comms(ffn_exchange + [swap]))
    for n, p in zip(ffn[:2], rest[:2]):
        parts[n] = p
    mixer_exchange = _reduce_chip(mixer, swap.ins, rest[2:], core)
    du_b = toks(du_b)
    g_ab_re = jnp.sum(da_re8, axis=0).reshape(32, 64)
    g_ab_im = jnp.sum(da_im8, axis=0).reshape(32, 64)
    d_lr, d_li, d_ldt, d_br_t, d_bi_t = ssm_prep_bwd(
        small["ssm_a_re"], small["ssm_a_im"], log_dt, br_t, bi_t,
        g_ab_re, g_ab_im, _block_diag_t(dbb_re), _block_diag_t(dbb_im))
    as_gcn = lambda t: t.transpose(1, 0, 2).reshape(SSM_W, 64)
    gs = {
        "ssm_a_re": d_lr, "ssm_a_im": d_li, "ssm_log_dt": d_ldt.reshape(1, 32),
        "ssm_b_re": as_gcn(d_br_t), "ssm_b_im": as_gcn(d_bi_t),
        "ssm_c_re": as_gcn(_block_diag_t(dcb_re)), "ssm_c_im": as_gcn(_block_diag_t(dcb_im)),
        "ssm_d": dd.reshape(32, 16).T,
    }
    ssm_gather = small_comm([gs[n] for n in SSM_SMALL])
    dqkv_b, *rest = attn_bwd(qkv, tables, dattn, attn, lse, comm=join_comms([mixer_exchange, ssm_gather]))
    for n, p in zip(mixer, rest):
        parts[n] = p
    ssm_shares = rest[len(mixer):]
    dqkv_b = toks(dqkv_b)
    d_qkv = matmul(dqkv_b, h, "tn", F32, "d_w_qkv")
    d_u = matmul(du_b, h, "tn", F32, "d_w_u")
    d_gate = matmul(dgl_b, h, "tn", F32, "d_w_gate")
    gw["w_in"] = jnp.concatenate([_qkv_order(d_qkv, back=True), d_u, d_gate], axis=0)
    swap = _reduce_start(["w_in"], gw, shard_shapes)
    dh, *got = matmul(dqkv_b, w_qkv, "nn", F32, "d_h_qkv", comm=swap)
    chip_sum = add_halves("w_in", swap.ins[0], got[0], core)
    sems, chip_sum, land, token = split_start(chip_sum, "w_in_reduce_start", per_chip=True)
    grad_x, dg_mix = mix_in_bwd([du_b, dgl_b], [w_u, w_gate], dh, x, g_mix + token[0, 0], dx1)
    gs_norm = {"norm_mix_g": dg_mix, "norm_ffn_g": dg_ffn, "norm_final_g": dg_final}
    return loss, grad_x, parts, ssm_shares, gs_norm, (sems, chip_sum, land)


ANY = pl.BlockSpec(memory_space=pl.ANY)
BIG = ("w_in", "w_glu", "w_attn_out", "w_out", "w_ffn_gate", "w_ffn_up", "w_ffn_down")
TRANSPOSED = ("w_in", "w_ffn_gate", "w_ffn_up")
ROW_SHARDED = TRANSPOSED + ("w_out", "w_ffn_down")
SMALL = ("norm_mix_g", "ssm_a_re", "ssm_a_im", "ssm_log_dt", "ssm_b_re", "ssm_b_im", "ssm_c_re", "ssm_c_im",
         "ssm_d", "norm_ffn_g", "norm_final_g")
WEIGHTS = ("norm_mix_g", "w_in", "ssm_a_re", "ssm_a_im", "ssm_log_dt", "ssm_b_re", "ssm_b_im", "ssm_c_re",
           "ssm_c_im", "ssm_d", "w_glu", "w_attn_out", "w_out", "norm_ffn_g", "w_ffn_gate", "w_ffn_up",
           "w_ffn_down", "norm_final_g")
SSM_SMALL = SMALL[1:9]
NORM_SMALL = (SMALL[0],) + SMALL[9:]
NORM_ROWS = 32
NORM_HOST = "w_attn_out"
assert NORM_HOST in BIG[1:]
N_BIG = len(BIG)


def _position():
    return lax.axis_index("x"), lax.axis_index("y"), lax.axis_index("c")


def _other_chips(x, y):
    return [(1 - x, y), (x, 1 - y), (1 - x, 1 - y)]


def _remote(src, dst, send_sem, recv_sem, device):
    return pltpu.make_async_remote_copy(src_ref=src, dst_ref=dst, send_sem=send_sem, recv_sem=recv_sem,
                                        device_id=device, device_id_type=MESH)


_later = functools.partial


def _two_level_phases(copies):
    def first(*refs):
        locals_, sends, _, _, _ = copies(*refs)
        for cp in locals_ + sends:
            cp().start()

    def mid(*refs):
        _, _, arrived, passed, _ = copies(*refs)
        for got, cp in zip(arrived, passed):
            got().wait_recv()
            cp().start()

    def last(*refs):
        locals_, sends, _, passed, from_sibling = copies(*refs)
        for cp in from_sibling:
            cp().wait_recv()
        for cp in sends + passed:
            cp().wait_send()
        for cp in locals_:
            cp().wait()

    return first, mid, last


def _half(ref, chip, which):
    rows = ref.shape[1] // 2
    return ref.at[chip, pl.ds(which * rows, rows), :]


class Comm:
    def __init__(self, ins, out_shapes, sems, first, mid, last):
        self.ins, self.out_shapes, self.sems = list(ins), list(out_shapes), list(sems)
        self.first, self.mid, self.last = first, mid, last


def join_comms(comms):
    def cut(refs_by_kind):
        offs, parts = [0, 0, 0], []
        for cm in comms:
            sizes = (len(cm.ins), len(cm.out_shapes), len(cm.sems))
            parts.append(tuple(refs_by_kind[k][offs[k]:offs[k] + sizes[k]] for k in range(3)))
            offs = [o + s for o, s in zip(offs, sizes)]
        return parts

    def phase(which):
        def run(ins, outs, sems):
            for cm, part in zip(comms, cut((ins, outs, sems))):
                fn = getattr(cm, which)
                if fn is not None:
                    fn(*part)
        return run

    return Comm(sum((cm.ins for cm in comms), []), sum((cm.out_shapes for cm in comms), []),
                sum((cm.sems for cm in comms), []), phase("first"), phase("mid"), phase("last"))


def _comm_operands(comm):
    if comm is None:
        return [], [], []
    return comm.ins, comm.out_shapes, comm.sems


def _comm_begin(comm, refs, step, n_steps):
    if comm is None:
        return
    pl.when(step == 0)(lambda: comm.first(*refs))
    if comm.mid is not None:
        pl.when(step == (n_steps * 3) // 4)(lambda: comm.mid(*refs))


def _comm_end(comm, refs, step, n_steps):
    if comm is not None:
        pl.when(step == n_steps - 1)(lambda: comm.last(*refs))


def _comm_refs(comm, refs, n_in, n_out):
    if comm is None:
        return list(refs), None
    ci, co, cs = len(comm.ins), len(comm.out_shapes), len(comm.sems)
    o0 = n_in + ci
    s0 = o0 + n_out + co
    host = list(refs[:n_in]) + list(refs[o0:o0 + n_out]) + list(refs[s0:len(refs) - cs])
    return host, (list(refs[n_in:o0]), list(refs[o0 + n_out:s0]), list(refs[len(refs) - cs:]))


def run_comm(comm, name):
    n_in, n_out = len(comm.ins), len(comm.out_shapes)

    def body(*refs):
        parts = (list(refs[:n_in]), list(refs[n_in:n_in + n_out]), list(refs[n_in + n_out:]))
        comm.first(*parts)
        if comm.mid is not None:
            comm.mid(*parts)
        comm.last(*parts)

    return pl.pallas_call(body, name=name, in_specs=[ANY] * n_in, out_specs=[ANY] * n_out,
                          out_shape=comm.out_shapes, scratch_shapes=comm.sems)(*comm.ins)


def hosted_call(work, comm, name, grid, in_specs, out_specs, out_shape, scratch_shapes, args, semantics):
    c_ins, c_outs, c_sems = _comm_operands(comm)
    n_steps = math.prod(grid)

    def body(*refs):
        host, c_refs = _comm_refs(comm, refs, len(in_specs), len(out_specs))
        step = 0
        for axis, size in enumerate(grid):
            step = step * size + pl.program_id(axis)
        _comm_begin(comm, c_refs, step, n_steps)
        work(*host)
        _comm_end(comm, c_refs, step, n_steps)

    return pl.pallas_call(
        body, name=name, grid=grid, in_specs=list(in_specs) + [ANY] * len(c_ins),
        out_specs=list(out_specs) + [ANY] * len(c_outs), out_shape=list(out_shape) + c_outs,
        scratch_shapes=list(scratch_shapes) + c_sems,
        compiler_params=_params(semantics if comm is None else ("arbitrary",) * len(grid)),
    )(*args, *c_ins)


def gather_comm(shards):
    n = len(shards)

    def copies(srcs, outs, sems):
        send_sems, recv_sems, local_sems = sems
        x, y, c = _position()
        me = 2 * x + y
        sibling = (x, y, 1 - c)
        chips = _other_chips(x, y)
        locals_ = [_later(pltpu.make_async_copy, s, o.at[me], local_sems.at[i])
                   for i, (s, o) in enumerate(zip(srcs, outs))]
        sends, arrived, passed, from_sibling = [], [], [], []
        for j, (px, py) in enumerate(chips):
            for i, (s, o) in enumerate(zip(srcs, outs)):
                rows = s.shape[0] // 2
                sends.append(_later(_remote, s.at[pl.ds(c * rows, rows), :], _half(o, me, c), send_sems.at[i, j],
                                    recv_sems.at[i, j], (px, py, c)))
                got = _half(o, 2 * px + py, c)
                arrived.append(_later(_remote, got, got, send_sems.at[i, j], recv_sems.at[i, j], (px, py, c)))
                passed.append(_later(_remote, got, got, send_sems.at[i, 3 + j], recv_sems.at[i, 3 + j], sibling))
                other = _half(o, 2 * px + py, 1 - c)
                from_sibling.append(_later(_remote, other, other, send_sems.at[i, 3 + j], recv_sems.at[i, 3 + j],
                                           sibling))
        return locals_, sends, arrived, passed, from_sibling

    return Comm(shards, [jax.ShapeDtypeStruct((N_CHIPS,) + s.shape, s.dtype) for s in shards],
                [pltpu.SemaphoreType.DMA((n, 6)), pltpu.SemaphoreType.DMA((n, 6)), pltpu.SemaphoreType.DMA((n,))],
                *_two_level_phases(copies))


HBM = pl.BlockSpec(memory_space=pltpu.HBM)
SEM = pl.BlockSpec(memory_space=pltpu.SEMAPHORE)
N_OTHER = N_CHIPS - 1


def _ici_halves(src_ref, land_ref, sems, per_chip):
    x, y, c = _position()
    me = 2 * x + y
    rows = land_ref.shape[1] // 2
    sends, arrivals = [], []
    for j, (px, py) in enumerate(_other_chips(x, y)):
        piece = src_ref.at[2 * px + py] if per_chip else src_ref.at[pl.ds(c * rows, rows), :]
        sends.append(_later(_remote, piece, _half(land_ref, me, c), sems[j], sems[N_OTHER + j], (px, py, c)))
        got = _half(land_ref, 2 * px + py, c)
        arrivals.append(_later(_remote, got, got, sems[j], sems[N_OTHER + j], (px, py, c)))
    return sends, arrivals


def split_start(src, name, per_chip=False):
    def body(src_ref, land_ref, *rest):
        sems, token = rest[:2 * N_OTHER], rest[-1]
        for cp in _ici_halves(src_ref, land_ref, sems, per_chip)[0]:
            cp().start()
        token[...] = jnp.zeros_like(token)

    rows, cols = (2 * src.shape[1], src.shape[2]) if per_chip else src.shape
    sem = pltpu.SemaphoreType.DMA(())
    land = (N_CHIPS, rows, cols)
    res = pl.pallas_call(
        body, name=name, in_specs=(HBM, HBM),
        out_specs=(SEM,) * (2 * N_OTHER) + (HBM, HBM, pl.BlockSpec(memory_space=pltpu.VMEM)),
        out_shape=(sem,) * (2 * N_OTHER) + (pltpu.HBM(src.shape, src.dtype), pltpu.HBM(land, src.dtype),
                                           jax.ShapeDtypeStruct((8, LANES), F32)),
        input_output_aliases={0: 2 * N_OTHER, 1: 2 * N_OTHER + 1},
        compiler_params=pltpu.CompilerParams(has_side_effects=pltpu.SideEffectType.DATAFLOW_SIDE_EFFECTING),
    )(pltpu.with_memory_space_constraint(src, pltpu.HBM),
      pltpu.with_memory_space_constraint(lax.empty(land, src.dtype), pltpu.HBM))
    return res[:2 * N_OTHER], res[2 * N_OTHER], res[2 * N_OTHER + 1], res[-1]


def split_wait(sems, src, land, after, name, per_chip=False):
    def body(src_ref, land_ref, *rest):
        sends, arrivals = _ici_halves(src_ref, land_ref, rest[:2 * N_OTHER], per_chip)
        for cp in sends:
            cp().wait_send()
        for cp in arrivals:
            cp().wait_recv()

    return pl.pallas_call(
        body, name=name, in_specs=(HBM, HBM) + (SEM,) * (2 * N_OTHER) + (ANY,) * len(after),
        out_specs=(HBM, HBM), out_shape=(pltpu.HBM(src.shape, src.dtype), pltpu.HBM(land.shape, land.dtype)),
        input_output_aliases={0: 0, 1: 1},
        compiler_params=pltpu.CompilerParams(has_side_effects=pltpu.SideEffectType.DATAFLOW_SIDE_EFFECTING),
    )(src, land, *sems, *after)


def handover(land, name, sums=None):
    n = N_OTHER + (sums is not None)

    def body(*refs):
        land_ref, send_sems, recv_sems = refs[0], refs[-2], refs[-1]
        x, y, c = _position()
        me = 2 * x + y
        sibling = (x, y, 1 - c)
        pieces = [(_half(land_ref, 2 * px + py, c), 2 * px + py) for px, py in _other_chips(x, y)]
        if sums is not None:
            pieces.append((refs[1].at[me], me))
        sends = [_remote(piece, _half(land_ref, chip, c), send_sems.at[j], recv_sems.at[j], sibling)
                 for j, (piece, chip) in enumerate(pieces)]
        for cp in sends:
            cp.start()
        for j, (_, chip) in enumerate(pieces):
            other = _half(land_ref, chip, 1 - c)
            _remote(other, other, send_sems.at[j], recv_sems.at[j], sibling).wait_recv()
        for cp in sends:
            cp.wait_send()

    args = (land,) + ((sums,) if sums is not None else ())
    return pl.pallas_call(
        body, name=name, in_specs=[ANY] * len(args), out_specs=ANY,
        out_shape=jax.ShapeDtypeStruct(land.shape, land.dtype), input_output_aliases={0: 0},
        scratch_shapes=[pltpu.SemaphoreType.DMA((n,)), pltpu.SemaphoreType.DMA((n,))],
    )(*args)


def swap_comm(grads):
    n = len(grads)

    def copies(srcs, gots, sems):
        send_sems, recv_sems = sems
        x, y, c = _position()
        out = []
        for i, (s, o) in enumerate(zip(srcs, gots)):
            rows = s.shape[1] // 2
            out.append(_remote(s.at[:, pl.ds((1 - c) * rows, rows), :], o, send_sems.at[i], recv_sems.at[i],
                               (x, y, 1 - c)))
        return out

    def first(srcs, gots, sems):
        for cp in copies(srcs, gots, sems):
            cp.start()

    def last(srcs, gots, sems):
        for cp in copies(srcs, gots, sems):
            cp.wait()

    return Comm(grads, [jax.ShapeDtypeStruct((N_CHIPS, g.shape[1] // 2, g.shape[2]), g.dtype) for g in grads],
                [pltpu.SemaphoreType.DMA((n,)), pltpu.SemaphoreType.DMA((n,))], first, None, last)


def add_halves(name, g, got, core):
    _, half, cols = got.shape
    mine = pl.BlockSpec((None, half, cols), lambda k, c_ref: (k, c_ref[0], 0))
    other = pl.BlockSpec((None, half, cols), lambda k, c_ref: (k, 0, 0))

    def body(c_ref, g_ref, got_ref, o_ref):
        o_ref[...] = (g_ref[...] + got_ref[...]).astype(BF16)

    return pl.pallas_call(
        body, name="add_halves_" + name,
        grid_spec=pltpu.PrefetchScalarGridSpec(num_scalar_prefetch=1, grid=(N_CHIPS,), in_specs=[mine, other],
                                               out_specs=other),
        out_shape=jax.ShapeDtypeStruct(got.shape, BF16),
        compiler_params=_params(("parallel",)),
    )(core, g, got)


def exchange_comm(parts):
    n = len(parts)

    def copies(srcs, outs, sems):
        send_sems, recv_sems, local_sems = sems
        x, y, c = _position()
        me = 2 * x + y
        sibling = (x, y, 1 - c)
        chips = _other_chips(x, y)
        locals_, sends, arrived, passed, from_sibling = [], [], [], [], []
        for i, (s, o) in enumerate(zip(srcs, outs)):
            locals_.append(_later(pltpu.make_async_copy, s.at[me], _half(o, me, c), local_sems.at[i]))
            sends.append(_later(_remote, s.at[me], _half(o, me, c), send_sems.at[i, 3], recv_sems.at[i, 3], sibling))
            other = _half(o, me, 1 - c)
            from_sibling.append(_later(_remote, other, other, send_sems.at[i, 3], recv_sems.at[i, 3], sibling))
        for j, (px, py) in enumerate(chips):
            for i, (s, o) in enumerate(zip(srcs, outs)):
                sends.append(_later(_remote, s.at[2 * px + py], _half(o, me, c), send_sems.at[i, j],
                                    recv_sems.at[i, j], (px, py, c)))
                got = _half(o, 2 * px + py, c)
                arrived.append(_later(_remote, got, got, send_sems.at[i, j], recv_sems.at[i, j], (px, py, c)))
                passed.append(_later(_remote, got, got, send_sems.at[i, 4 + j], recv_sems.at[i, 4 + j], sibling))
                other = _half(o, 2 * px + py, 1 - c)
                from_sibling.append(_later(_remote, other, other, send_sems.at[i, 4 + j], recv_sems.at[i, 4 + j],
                                           sibling))
        return locals_, sends, arrived, passed, from_sibling

    return Comm(parts, [jax.ShapeDtypeStruct((N_CHIPS, 2 * p.shape[1], p.shape[2]), p.dtype) for p in parts],
                [pltpu.SemaphoreType.DMA((n, 7)), pltpu.SemaphoreType.DMA((n, 7)), pltpu.SemaphoreType.DMA((n,))],
                *_two_level_phases(copies))


def small_comm(shares):
    n = len(shares)

    def copies(srcs, outs, sems):
        send_sems, recv_sems, local_sems = sems
        x, y, c = _position()
        me = 4 * x + 2 * y + c
        flips = [(fx, fy, fc) for fx in (0, 1) for fy in (0, 1) for fc in (0, 1)][1:]
        peers = [(1 - x if fx else x, 1 - y if fy else y, 1 - c if fc else c) for fx, fy, fc in flips]
        locals_, sends, arrived = [], [], []
        for i, (src_ref, out_ref) in enumerate(zip(srcs, outs)):
            locals_.append(_later(pltpu.make_async_copy, src_ref, out_ref.at[me], local_sems.at[i]))
            for j, (px, py, pc) in enumerate(peers):
                sends.append(_later(_remote, src_ref, out_ref.at[me], send_sems.at[i, j], recv_sems.at[i, j],
                                    (px, py, pc)))
                got = out_ref.at[4 * px + 2 * py + pc]
                arrived.append(_later(_remote, got, got, send_sems.at[i, j], recv_sems.at[i, j], (px, py, pc)))
        return locals_, sends, arrived

    def first(*refs):
        locals_, sends, _ = copies(*refs)
        for cp in locals_ + sends:
            cp().start()

    def last(*refs):
        locals_, sends, arrived = copies(*refs)
        for cp in arrived:
            cp().wait_recv()
        for cp in sends:
            cp().wait_send()
        for cp in locals_:
            cp().wait()

    return Comm(shares, [jax.ShapeDtypeStruct((N_DEV,) + s.shape, s.dtype) for s in shares],
                [pltpu.SemaphoreType.DMA((n, 7)), pltpu.SemaphoreType.DMA((n, 7)), pltpu.SemaphoreType.DMA((n,))],
                first, None, last)


def _adam_fn(w, g, m, v):
    m = ADAM_B1 * m + (1.0 - ADAM_B1) * g
    v = ADAM_B2 * v + (1.0 - ADAM_B2) * jnp.square(g)
    m_hat = m / (1.0 - ADAM_B1 ** ADAM_STEP)
    v_hat = v / (1.0 - ADAM_B2 ** ADAM_STEP)
    return -ADAM_LR * (m_hat / (jnp.sqrt(v_hat) + ADAM_EPS) + ADAM_WD * w), m, v


def adam_big(name, parts, w, m, v, comm=None):
    rows, cols = w.shape
    tm = _pick(rows, 384, 16)

    def fn(p0, p1, p2, p3, wv, mv, vv):
        g = ((p0.astype(F32) + p1.astype(F32)) + p2.astype(F32)) + p3.astype(F32)
        return (g,) + _adam_fn(wv, g, mv, vv)

    return rowwise(fn, [parts, w, m, v], [(cols, F32)] * 4, "adam_" + name, tm=tm, rows=rows, comm=comm)


def adam_small(name, gathered, w, m, v):
    def body(g_ref, w_ref, m_ref, v_ref, go_ref, d_ref, mo_ref, vo_ref):
        g = g_ref[0]
        for k in range(1, N_DEV):
            g = g + g_ref[k]
        go_ref[...] = g
        d_ref[...], mo_ref[...], vo_ref[...] = _adam_fn(w_ref[...], g, m_ref[...], v_ref[...])

    return pl.pallas_call(body, name=name, out_shape=[jax.ShapeDtypeStruct(w.shape, F32)] * 4,
                          compiler_params=_params())(gathered, w, m, v)


def _ssm_2d(name, t):
    t = t[0] if t.ndim > 2 else t
    if name in ("ssm_b_re", "ssm_b_im"):
        return t.transpose(0, 2, 1).reshape(SSM_W, 64)
    if name in ("ssm_c_re", "ssm_c_im"):
        return t.reshape(SSM_W, 64)
    return t.T if name == "ssm_d" else t


def _ssm_back(name, t):
    if name in ("ssm_b_re", "ssm_b_im"):
        return t.reshape(32, 16, 64).transpose(0, 2, 1)[None]
    if name in ("ssm_c_re", "ssm_c_im"):
        return t.reshape(1, 32, 16, 64)
    if name == "ssm_d":
        return t.T[None]
    return t if name == "ssm_log_dt" else t[None]


def adam_ssm(shares, w, m, v):
    n = len(w)

    def body(*refs):
        ins, outs = refs[:4 * n], refs[4 * n:]
        for i in range(n):
            g_ref, w_ref, m_ref, v_ref = (ins[k * n + i] for k in range(4))
            g = g_ref[0]
            for k in range(1, N_DEV):
                g = g + g_ref[k]
            outs[4 * i][...] = g
            outs[4 * i + 1][...], outs[4 * i + 2][...], outs[4 * i + 3][...] = _adam_fn(w_ref[...], g, m_ref[...],
                                                                                      v_ref[...])

    out_shape = [jax.ShapeDtypeStruct(t.shape, F32) for t in w for _ in range(4)]
    res = pl.pallas_call(body, name="adam_ssm", out_shape=out_shape, compiler_params=_params())(*shares, *w, *m, *v)
    return [res[4 * i:4 * i + 4] for i in range(n)]


def _pack_small(names, vals, rows, last=None):
    flat = [vals[n].reshape(-1) for n in names]
    if last is not None:
        flat.append(last.reshape(-1))
    flat = jnp.concatenate(flat)
    return jnp.pad(flat, (0, rows * LANES - flat.shape[0])).reshape(rows, LANES)


def _unpack_small(names, pack, shapes):
    flat, out, off = pack.reshape(-1), {}, 0
    for n in names:
        size = math.prod(shapes[n])
        out[n] = flat[off:off + size].reshape(shapes[n])
        off += size
    return out, flat[off]


def _to_slots(name, g, shard_shape):
    rows, cols = shard_shape
    if name in ROW_SHARDED:
        return g.reshape(N_CHIPS, rows, cols)
    return g.reshape(rows, N_CHIPS, cols).transpose(1, 0, 2)


def _from_slots(name, s):
    _, rows, cols = s.shape
    if name in ROW_SHARDED:
        return s.reshape(N_CHIPS * rows, cols)
    return s.transpose(1, 0, 2).reshape(rows, N_CHIPS * cols)


def kernel(x, norm_mix_g, w_in, ssm_a_re, ssm_a_im, ssm_log_dt, ssm_b_re, ssm_b_im, ssm_c_re, ssm_c_im, ssm_d, w_glu, w_attn_out, w_out, norm_ffn_g, w_ffn_gate, w_ffn_up, w_ffn_down, norm_final_g, loss_target, m_norm_mix_g, m_w_in, m_ssm_a_re, m_ssm_a_im, m_ssm_log_dt, m_ssm_b_re, m_ssm_b_im, m_ssm_c_re, m_ssm_c_im, m_ssm_d, m_w_glu, m_w_attn_out, m_w_out, m_norm_ffn_g, m_w_ffn_gate, m_w_ffn_up, m_w_ffn_down, m_norm_final_g, v_norm_mix_g, v_w_in, v_ssm_a_re, v_ssm_a_im, v_ssm_log_dt, v_ssm_b_re, v_ssm_b_im, v_ssm_c_re, v_ssm_c_im, v_ssm_d, v_w_glu, v_w_attn_out, v_w_out, v_norm_ffn_g, v_w_ffn_gate, v_w_ffn_up, v_w_ffn_down, v_norm_final_g):
    given = dict(locals())
    def local(name, prefix=""):
        t = given[prefix + name][0]
        return t.T if name in TRANSPOSED else t

    shard = {n: local(n) for n in BIG}
    shapes = {n: given[n].shape for n in WEIGHTS}

    small = {n: given[n] for n in SMALL}
    small_2d = dict(small)
    for n in ("ssm_a_re", "ssm_a_im", "ssm_b_re", "ssm_b_im", "ssm_c_re", "ssm_c_im", "ssm_d"):
        small_2d[n] = small[n][0]
    small_2d["norm_final_g"] = norm_final_g.reshape(1, D_MODEL)

    core = lax.axis_index("c").astype(jnp.int32).reshape(1)
    loss, grad_x, parts, ssm_shares, gs_norm, w_in_reduce = local_step(
        x.reshape(TOKENS, D_MODEL), loss_target.reshape(TOKENS, D_MODEL),
        {n: shard[n] for n in BIG}, small_2d, core)

    norm_gather = small_comm([_pack_small(NORM_SMALL, gs_norm, NORM_ROWS, last=loss)])
    small_out = [{} for _ in range(4)]
    ssm_in = [[_ssm_2d(n, given[p + n]) for n in SSM_SMALL] for p in ("", "m_", "v_")]
    for n, res in zip(SSM_SMALL, adam_ssm(ssm_shares, *ssm_in)):
        for kind, t in enumerate(res):
            small_out[kind][n] = _ssm_back(n, t)

    big_out, updated = {}, {}
    for n in BIG[1:] + BIG[:1]:
        if n == "w_in":
            sems, chip_sum, land = w_in_reduce
            behind = [updated[k][1] for k in BIG[1:]]
            chip_sum, land = split_wait(sems, chip_sum, land, behind, "w_in_reduce_wait", per_chip=True)
            land = handover(land, "w_in_reduce_handover", sums=chip_sum)
            me = 2 * lax.axis_index("x") + lax.axis_index("y")
            parts[n] = lax.dynamic_update_slice(land, lax.dynamic_slice_in_dim(chip_sum, me, 1, 0),
                                                (me, lax.axis_index("c") * chip_sum.shape[1], 0))
        res = adam_big(n, parts[n], shard[n], local(n, "m_"), local(n, "v_"),
                       comm=norm_gather if n == NORM_HOST else None)
        updated[n] = res[:4]
        if n == NORM_HOST:
            norm_shares = res[4]
            packs = [_pack_small(NORM_SMALL, {k: given[p + k] for k in NORM_SMALL}, NORM_ROWS)
                     for p in ("", "m_", "v_")]
            for kind, t in enumerate(adam_small("adam_norm_gains", norm_shares, *packs)):
                vals, after = _unpack_small(NORM_SMALL, t, shapes)
                small_out[kind].update(vals)
                if kind == 0:
                    total_loss = after
        big_out[n] = [(t.T if n in TRANSPOSED else t)[None] for t in updated[n]]

    outs = [total_loss, grad_x.reshape(LOCAL_BATCH, SEQ, D_MODEL)]
    for kind in range(4):
        for n in WEIGHTS:
            outs.append(big_out[n][kind] if n in BIG else small_out[kind][n])
    return tuple(outs)
```

```python
import functools
import math

import jax
import jax.numpy as jnp
import numpy as np
from jax import lax
from jax.experimental import pallas as pl
from jax.experimental.pallas import tpu as pltpu

F32 = jnp.float32
BF16 = jnp.bfloat16
MESH = pl.DeviceIdType.MESH

D_MODEL = 1024
SEQ = 2048
LOCAL_BATCH = 2
TOKENS = LOCAL_BATCH * SEQ
HEAD_DIM = 64
HEADS_PER_GROUP = 4
GROUP_W = HEADS_PER_GROUP * HEAD_DIM
N_GROUPS = 3
DILATIONS = (1, 4, 16)
ATTN_BLOCK = 128
ROPE_DIM = 16
ROPE_THETA = 500000.0
QKV_W = 3 * N_GROUPS * GROUP_W
SSM_W = 512
SSM_STATE_W = 2048
SSM_LANE_BLOCKS = 4
GATE_W = 2 * D_MODEL
D_FF = 2816
RMS_EPS = 1e-6
NEG_INF = -1e30
ADAM_LR, ADAM_B1, ADAM_B2, ADAM_EPS, ADAM_WD, ADAM_STEP = 0.001, 0.9, 0.999, 1e-08, 0.01, 10
N_CHIPS = 4
N_DEV = 8

VMEM_LIMIT = 56 * 1024 * 1024
LANES = 128


def _params(sem=None):
    return pltpu.CompilerParams(dimension_semantics=sem, vmem_limit_bytes=VMEM_LIMIT)


def _pick(n, cap, align=LANES):
    best = None
    for d in range(align, min(n, cap) + 1, align):
        if n % d == 0:
            best = d
    return n if best is None or n <= cap else best


_DIMS = {"nn": (((1,), (0,)), ((), ())), "nt": (((1,), (1,)), ((), ())), "tn": (((0,), (0,)), ((), ()))}


def _dot(a, b, mode):
    return lax.dot_general(a, b, _DIMS[mode], preferred_element_type=F32)


def matmul(a, b, mode, out_dtype, name, add=None, comm=None, col_slots=1):
    if mode == "nn":
        (m, k), n = a.shape, b.shape[1]
    elif mode == "nt":
        (m, k), n = a.shape, b.shape[0]
    else:
        (k, m), n = a.shape, b.shape[1]
    assert n % col_slots == 0 and (col_slots == 1 or add is None)
    tn = _pick(n // col_slots, 1408 if mode != "tn" else 512)
    tk = _pick(k, 2816) if mode != "tn" else k
    tm = _pick(m, 1408)
    out_bytes = jnp.dtype(out_dtype).itemsize

    def need(tm_):
        return 2 * 2 * (tm_ * tk + tk * tn) + tm_ * tn * (4 + 2 * out_bytes + (8 if add is not None else 0))

    while need(tm) > 40 * 1024 * 1024 and tm % 256 == 0:
        tm //= 2
    nk = k // tk
    a_spec = {"nn": pl.BlockSpec((tm, tk), lambda i, j, kk: (i, kk)),
              "nt": pl.BlockSpec((tm, tk), lambda i, j, kk: (i, kk)),
              "tn": pl.BlockSpec((tk, tm), lambda i, j, kk: (kk, i))}[mode]
    b_spec = {"nn": pl.BlockSpec((tk, tn), lambda i, j, kk: (kk, j)),
              "nt": pl.BlockSpec((tn, tk), lambda i, j, kk: (j, kk)),
              "tn": pl.BlockSpec((tk, tn), lambda i, j, kk: (kk, j))}[mode]
    o_spec, o_shape = pl.BlockSpec((tm, tn), lambda i, j, kk: (i, j)), (m, n)
    if col_slots > 1:
        per_slot = n // col_slots // tn
        o_spec = pl.BlockSpec((None, tm, tn), lambda i, j, kk: (lax.div(j, per_slot), i, lax.rem(j, per_slot)))
        o_shape = (col_slots, m, n // col_slots)

    def body(a_ref, b_ref, *rest):
        if add is not None:
            add_ref, o_ref, acc_ref = rest
        else:
            o_ref, acc_ref = rest
        part = _dot(a_ref[...], b_ref[...], mode)
        if nk == 1:
            res = part if add is None else part + add_ref[...]
            o_ref[...] = res.astype(out_dtype)
            return
        kk = pl.program_id(2)

        @pl.when(kk == 0)
        def _():
            acc_ref[...] = part

        @pl.when(kk > 0)
        def _():
            acc_ref[...] += part

        @pl.when(kk == nk - 1)
        def _():
            res = acc_ref[...] if add is None else acc_ref[...] + add_ref[...]
            o_ref[...] = res.astype(out_dtype)

    in_specs = [a_spec, b_spec] + ([o_spec] if add is not None else [])
    args = (a, b) + ((add,) if add is not None else ())
    res = hosted_call(
        body, comm, name, (m // tm, n // tn, nk), in_specs, [o_spec], [jax.ShapeDtypeStruct(o_shape, out_dtype)],
        [pltpu.VMEM((tm, tn) if nk > 1 else (8, LANES), F32)], args, ("parallel", "parallel", "arbitrary"))
    return res[0] if comm is None else res


def matmul_rows(a, b, name, fn, extra, outs, accs=(), add=None, comm=None, tm=512):
    a_list, b_list = (list(a), list(b)) if isinstance(a, (list, tuple)) else ([a], [b])
    m, n = a_list[0].shape[0], b_list[0].shape[1]
    n_mm = len(a_list)
    n_fixed = 2 * n_mm + (add is not None)
    row_spec = lambda cols: pl.BlockSpec((tm, cols), lambda i: (i, 0))
    in_specs = [row_spec(t.shape[1]) for t in a_list] + [pl.BlockSpec(t.shape, lambda i: (0, 0)) for t in b_list]
    in_specs += [row_spec(n)] if add is not None else []
    in_specs += [pl.BlockSpec(e.shape, lambda i: (0, 0)) if e.shape[0] == 1 else row_spec(e.shape[1]) for e in extra]
    out_specs = [row_spec(c) for c, _ in outs] + [pl.BlockSpec((1, c), lambda i: (0, 0)) for c in accs]
    out_shape = [jax.ShapeDtypeStruct((m, c), dt) for c, dt in outs] + [jax.ShapeDtypeStruct((1, c), F32) for c in accs]

    def body(*refs):
        rows = _dot(refs[0][...], refs[n_mm][...], "nn")
        for i in range(1, n_mm):
            rows = rows + _dot(refs[i][...], refs[n_mm + i][...], "nn")
        if add is not None:
            rows = rows + refs[2 * n_mm][...]
        n_in = n_fixed + len(extra)
        res = fn(rows, *[r[...] for r in refs[n_fixed:n_in]])
        for r, v in zip(refs[n_in:n_in + len(outs)], res[:len(outs)]):
            r[...] = v.astype(r.dtype)
        first = pl.program_id(0) == 0
        for r, v in zip(refs[n_in + len(outs):], res[len(outs):]):
            @pl.when(first)
            def _(r=r, v=v):
                r[...] = v

            @pl.when(jnp.logical_not(first))
            def _(r=r, v=v):
                r[...] += v

    args = tuple(a_list) + tuple(b_list) + ((add,) if add is not None else ()) + tuple(extra)
    return hosted_call(body, comm, name, (m // tm,), in_specs, out_specs, out_shape, [], args, ("arbitrary",))


def _merge_specs(tm):
    half = lambda blk: pl.BlockSpec((tm, D_MODEL), functools.partial(lambda i, blk_: (i, blk_), blk_=blk))
    return [half(0), half(1), pl.BlockSpec((tm, GROUP_W), lambda i: (i, 0)),
            pl.BlockSpec((GROUP_W, D_MODEL), lambda i: (0, 0)), pl.BlockSpec((tm, SSM_W), lambda i: (i, 0)),
            pl.BlockSpec((SSM_W, GATE_W), lambda i: (0, 0))]


def _merge_operands(g0, g1, at, wa, yg, wg):
    z = _dot(yg[...], wg[...], "nn")
    return (g0[...].astype(F32), g1[...].astype(F32), _dot(at[...], wa[...], "nn"), z[:, :D_MODEL], z[:, D_MODEL:])


def merge_out_proj(gl, attn_b, w_attn_out, yg, w_glu, w_out, x, g_ffn):
    tm = 512

    def body(g0, g1, at, wa, yg_ref, wg, w_ref, x_ref, g_ref, m_ref, x1_ref, h2_ref):
        merged = _merge_fn(*_merge_operands(g0, g1, at, wa, yg_ref, wg)).astype(BF16)
        m_ref[...] = merged
        x1 = _dot(merged, w_ref[...], "nn") + x_ref[...]
        x1_ref[...] = x1
        h2_ref[...] = _rms(x1, g_ref[...]).astype(BF16)

    rows = pl.BlockSpec((tm, D_MODEL), lambda i: (i, 0))
    whole = pl.BlockSpec((D_MODEL, D_MODEL), lambda i: (0, 0))
    gain = pl.BlockSpec((1, D_MODEL), lambda i: (0, 0))
    tok = lambda dt: jax.ShapeDtypeStruct((TOKENS, D_MODEL), dt)
    return pl.pallas_call(
        body, name="merge_out_proj", grid=(TOKENS // tm,), in_specs=_merge_specs(tm) + [whole, rows, gain],
        out_specs=[rows] * 3, out_shape=[tok(BF16), tok(F32), tok(BF16)], compiler_params=_params(("parallel",)),
    )(gl, gl, attn_b, w_attn_out, yg, w_glu, w_out, x, g_ffn)


def merge_bwd(dx1_b, w_out, gl, attn_b, w_attn_out, yg, w_glu, comm=None):
    tm = 512

    def body(dx_ref, w_ref, g0, g1, at, wa, yg_ref, wg, dgl_ref, dad_ref, dz_ref, dat_ref, dyg_ref):
        dm = _dot(dx_ref[...], w_ref[...], "nt")
        _, vjp = jax.vjp(_merge_fn, *_merge_operands(g0, g1, at, wa, yg_ref, wg))
        dg0, dg1, dad, dza, dzb = vjp(dm)
        dat_ref[...] = _dot(dad.astype(BF16), wa[...], "nt")
        dgl_ref[:, :D_MODEL] = dg0.astype(BF16)
        dgl_ref[:, D_MODEL:] = dg1.astype(BF16)
        dad_ref[...] = dad.astype(BF16)
        dz_ref[:, :D_MODEL] = dza.astype(BF16)
        dz_ref[:, D_MODEL:] = dzb.astype(BF16)
        dyg_ref[...] = _dot(dz_ref[...], wg[...], "nt")

    rows = pl.BlockSpec((tm, D_MODEL), lambda i: (i, 0))
    wide = pl.BlockSpec((tm, GATE_W), lambda i: (i, 0))
    whole = pl.BlockSpec((D_MODEL, D_MODEL), lambda i: (0, 0))
    return hosted_call(
        body, comm, "merge_bwd", (TOKENS // tm,), [rows, whole] + _merge_specs(tm),
        [wide, rows, wide, pl.BlockSpec((tm, GROUP_W), lambda i: (i, 0)), pl.BlockSpec((tm, SSM_W), lambda i: (i, 0))],
        [jax.ShapeDtypeStruct((TOKENS, GATE_W), BF16), jax.ShapeDtypeStruct((TOKENS, D_MODEL), BF16),
         jax.ShapeDtypeStruct((TOKENS, GATE_W), BF16), jax.ShapeDtypeStruct((TOKENS, GROUP_W), F32),
         jax.ShapeDtypeStruct((TOKENS, SSM_W), F32)], [],
        (dx1_b, w_out, gl, gl, attn_b, w_attn_out, yg, w_glu), ("arbitrary",))


FFN_TM, FFN_TN = 512, 1408


def ffn_in(h2, wg_t, wu_t, comm=None):
    def body(h_ref, wg_ref, wu_ref, a_ref, b_ref, act_ref):
        hv = h_ref[...]
        a, b = _dot(hv, wg_ref[...], "nt"), _dot(hv, wu_ref[...], "nt")
        a_ref[...] = a.astype(BF16)
        b_ref[...] = b.astype(BF16)
        act_ref[...] = _swiglu_fn(a, b).astype(BF16)

    rows = pl.BlockSpec((FFN_TM, D_MODEL), lambda i, j: (i, 0))
    wts = pl.BlockSpec((FFN_TN, D_MODEL), lambda i, j: (j, 0))
    out = pl.BlockSpec((FFN_TM, FFN_TN), lambda i, j: (i, j))
    return hosted_call(body, comm, "ffn_in", (TOKENS // FFN_TM, D_FF // FFN_TN), [rows, wts, wts], [out] * 3,
                       [jax.ShapeDtypeStruct((TOKENS, D_FF), BF16)] * 3, [], (h2, wg_t, wu_t),
                       ("parallel", "parallel"))


def ffn_in_bwd(dx2_b, wd, a, b):
    def body(dx_ref, wd_ref, a_ref, b_ref, da_ref, db_ref):
        dact = _dot(dx_ref[...], wd_ref[...], "nt")
        av, bv = a_ref[...].astype(F32), b_ref[...].astype(F32)
        sig = jax.nn.sigmoid(av)
        act = av * sig
        da_ref[...] = (dact * bv * (sig * (1.0 + av - act))).astype(BF16)
        db_ref[...] = (dact * act).astype(BF16)

    rows = pl.BlockSpec((FFN_TM, D_MODEL), lambda i, j: (i, 0))
    wts = pl.BlockSpec((FFN_TN, D_MODEL), lambda i, j: (j, 0))
    out = pl.BlockSpec((FFN_TM, FFN_TN), lambda i, j: (i, j))
    return pl.pallas_call(
        body, name="ffn_in_bwd", grid=(TOKENS // FFN_TM, D_FF // FFN_TN), in_specs=[rows, wts, out, out],
        out_specs=[out] * 2, out_shape=[jax.ShapeDtypeStruct((TOKENS, D_FF), BF16)] * 2,
        compiler_params=_params(("parallel", "parallel")),
    )(dx2_b, wd, a, b)


def mix_in_bwd(grads, weights, partial, x, g, skip, comm=None):
    n = len(grads)
    tm = 512

    def body(*refs):
        a_refs, b_refs = refs[:n], refs[n:2 * n]
        part_ref, x_ref, g_ref, skip_ref, gx_ref, dg_ref = refs[2 * n:]
        dh = part_ref[...]
        for a_ref, b_ref in zip(a_refs, b_refs):
            dh = dh + _dot(a_ref[...], b_ref[...], "nn")
        _, vjp = jax.vjp(_rms, x_ref[...], g_ref[...])
        dx, dg = vjp(dh)
        gx_ref[...] = dx + skip_ref[...]
        first = pl.program_id(0) == 0

        @pl.when(first)
        def _():
            dg_ref[...] = dg

        @pl.when(jnp.logical_not(first))
        def _():
            dg_ref[...] += dg

    rows = pl.BlockSpec((tm, D_MODEL), lambda i: (i, 0))
    gain = pl.BlockSpec((1, D_MODEL), lambda i: (0, 0))
    in_specs = [pl.BlockSpec((tm, a.shape[1]), lambda i: (i, 0)) for a in grads]
    in_specs += [pl.BlockSpec(b.shape, lambda i: (0, 0)) for b in weights]
    return hosted_call(
        body, comm, "mix_in_bwd", (TOKENS // tm,), in_specs + [rows, rows, gain, rows], [rows, gain],
        [jax.ShapeDtypeStruct((TOKENS, D_MODEL), F32), jax.ShapeDtypeStruct((1, D_MODEL), F32)], [],
        (*grads, *weights, partial, x, g, skip), ("arbitrary",))


def rowwise(fn, ins, outs, name, accs=(), tm=256, rows=TOKENS, comm=None):
    in_specs, args = [], []
    for item in ins:
        arr, width, blk = item if isinstance(item, tuple) else (item, None, 0)
        if arr.ndim == 3:
            for k in range(arr.shape[0]):
                in_specs.append(pl.BlockSpec((None, tm, arr.shape[2]), functools.partial(lambda i, k_: (k_, i, 0), k_=k)))
                args.append(arr)
            continue
        if arr.shape[0] == 1:
            in_specs.append(pl.BlockSpec(arr.shape, lambda i: (0, 0)))
        elif width is None:
            in_specs.append(pl.BlockSpec((tm, arr.shape[1]), lambda i: (i, 0)))
        else:
            in_specs.append(pl.BlockSpec((tm, width), functools.partial(lambda i, blk_: (i, blk_), blk_=blk)))
        args.append(arr)
    out_specs = [pl.BlockSpec((tm, c), lambda i: (i, 0)) for c, _ in outs]
    out_specs += [pl.BlockSpec((1, c), lambda i: (0, 0)) for c in accs]
    out_shape = [jax.ShapeDtypeStruct((rows, c), dt) for c, dt in outs]
    out_shape += [jax.ShapeDtypeStruct((1, c), F32) for c in accs]
    n_in, n_out = len(args), len(outs)
    c_ins, c_outs, c_sems = _comm_operands(comm)

    def body(*refs):
        refs, c_refs = _comm_refs(comm, refs, n_in, n_out + len(accs))
        step = pl.program_id(0)
        _comm_begin(comm, c_refs, step, rows // tm)
        res = fn(*[r[...] for r in refs[:n_in]])
        for r, v in zip(refs[n_in:n_in + n_out], res[:n_out]):
            r[...] = v.astype(r.dtype)
        first = step == 0
        for r, v in zip(refs[n_in + n_out:], res[n_out:]):
            @pl.when(first)
            def _(r=r, v=v):
                r[...] = v

            @pl.when(jnp.logical_not(first))
            def _(r=r, v=v):
                r[...] += v
        _comm_end(comm, c_refs, step, rows // tm)

    return pl.pallas_call(
        body, name=name, grid=(rows // tm,), in_specs=in_specs + [ANY] * len(c_ins),
        out_specs=out_specs + [ANY] * len(c_outs), out_shape=out_shape + c_outs, scratch_shapes=c_sems,
        compiler_params=_params(("arbitrary",)),
    )(*args, *c_ins)


def first_norm(x, g, others, comm=None):
    tm, n = 256, len(others)

    def body(x_ref, g_ref, *rest):
        srcs, h_ref, dsts = rest[:n], rest[n], rest[n + 1:]
        h_ref[...] = _rms(x_ref[...], g_ref[...]).astype(BF16)
        for k, (s, d) in enumerate(zip(srcs, dsts)):
            @pl.when(pl.program_id(0) == k)
            def _(s=s, d=d):
                d[...] = s[...].astype(BF16)

    rows = pl.BlockSpec((tm, D_MODEL), lambda i: (i, 0))
    whole = [pl.BlockSpec(a.shape, lambda i: (0, 0)) for a in others]
    return hosted_call(
        body, comm, "norm_mix", (TOKENS // tm,), [rows, pl.BlockSpec((1, D_MODEL), lambda i: (0, 0))] + whole,
        [rows] + whole, [jax.ShapeDtypeStruct((TOKENS, D_MODEL), BF16)]
        + [jax.ShapeDtypeStruct(a.shape, BF16) for a in others], [], (x, g, *others), ("arbitrary",))


def _rms(x, g):
    return x * lax.rsqrt(jnp.mean(x * x, axis=-1, keepdims=True) + RMS_EPS) * g


def _colsum(v):
    return jnp.sum(v, axis=0, keepdims=True)


PAIR_W = 2 * HEAD_DIM
N_PAIRS = HEADS_PER_GROUP // 2


def _qkv_order(w_t, back=False):
    dims = (N_PAIRS, N_GROUPS, 3) if back else (3, N_GROUPS, N_PAIRS)
    return w_t.reshape(dims + (PAIR_W, w_t.shape[1])).transpose(2, 1, 0, 3, 4).reshape(QKV_W, w_t.shape[1])


def _rope_tables():
    half = ROPE_DIM // 2
    inv = np.power(np.float32(ROPE_THETA), -np.arange(half, dtype=np.float32) * np.float32(2.0 / ROPE_DIM))
    ang = (np.arange(SEQ, dtype=np.float32)[:, None] * inv[None, :]).astype(np.float32)
    cos, sin = np.cos(ang), np.sin(ang)
    zeros = np.zeros((SEQ, HEAD_DIM - ROPE_DIM), np.float32)
    zh = np.zeros((SEQ, half), np.float32)
    c = np.concatenate([cos, cos, zeros + 1.0], axis=1)
    sa = np.concatenate([-sin, zh, zeros], axis=1)
    sb = np.concatenate([zh, sin, zeros], axis=1)
    return [jnp.asarray(np.tile(t, (1, 2)), F32) for t in (c, sa, sb)]


def _rope_fwd(x, c, sa, sb):
    return x * c + pltpu.roll(x, PAIR_W - 8, 1) * sa + pltpu.roll(x, 8, 1) * sb


def _rope_bwd(dy, c, sa, sb):
    return dy * c + pltpu.roll(dy * sb, PAIR_W - 8, 1) + pltpu.roll(dy * sa, 8, 1)


def _band_masks():
    row = lax.broadcasted_iota(jnp.int32, (ATTN_BLOCK, ATTN_BLOCK), 0)
    col = lax.broadcasted_iota(jnp.int32, (ATTN_BLOCK, ATTN_BLOCK), 1)
    return col <= row, col >= row


def _stack_rows(t):
    return jnp.concatenate([t, t], axis=0)


def _stack_heads(t, first_head):
    return jnp.concatenate([jnp.where(first_head, t, 0), jnp.where(first_head, 0, t)], axis=0)


def _per_head(fn):
    return jnp.concatenate([fn(slice(h * HEAD_DIM, (h + 1) * HEAD_DIM)) for h in range(2)], axis=1)


def _slab_spec(kind):
    return pl.BlockSpec((None, SEQ, PAIR_W), lambda b, p, g: (b, 0, p * 3 * N_GROUPS + g * 3 + kind))


_TABLE_SPEC = pl.BlockSpec((SEQ, PAIR_W), lambda b, p, g: (0, 0))
_PAIR_SPEC = pl.BlockSpec((None, SEQ, PAIR_W), lambda b, p, g: (b, 0, p))


def _block_rows(dil, r, n):
    return pl.ds(n * (ATTN_BLOCK * dil) + r, ATTN_BLOCK, stride=dil)


def proj_qkv(h, w_qkv_t, tables, comm=None):
    tm = 1024
    pair_w = QKV_W // N_PAIRS
    scale = HEAD_DIM ** -0.5

    def body(h_ref, w_ref, c_ref, sa_ref, sb_ref, o_ref):
        rows = _dot(h_ref[...], w_ref[...], "nt")
        c, sa, sb = c_ref[...], sa_ref[...], sb_ref[...]
        for blk in range(pair_w // PAIR_W):
            cols = slice(blk * PAIR_W, (blk + 1) * PAIR_W)
            x = rows[:, cols]
            if blk % 3 == 0:
                x = _rope_fwd(x, c, sa, sb) * scale
            elif blk % 3 == 1:
                x = _rope_fwd(x, c, sa, sb)
            o_ref[:, cols] = x

    table = pl.BlockSpec((tm, PAIR_W), lambda i, j, : (i % (SEQ // tm), 0))
    res = hosted_call(
        body, comm, "proj_qkv", (TOKENS // tm, N_PAIRS),
        [pl.BlockSpec((tm, D_MODEL), lambda i, j: (i, 0)), pl.BlockSpec((pair_w, D_MODEL), lambda i, j: (j, 0)),
         table, table, table],
        [pl.BlockSpec((tm, pair_w), lambda i, j: (i, j))], [jax.ShapeDtypeStruct((TOKENS, QKV_W), F32)], [],
        (h, w_qkv_t, *tables), ("parallel", "parallel"))
    return res[0] if comm is None else res


def attn_fwd(qkv, comm=None):
    def body(qs, ks, v_ref, attn_b_ref, attn_ref, lse_ref, o0, o1, o2, l0, l1, l2):
        g = pl.program_id(2)
        cur_mask, prev_mask = _band_masks()
        first_head = lax.broadcasted_iota(jnp.int32, (ATTN_BLOCK, PAIR_W), 1) < HEAD_DIM

        def run(dil, o_slab, l_slab):
            nb = SEQ // dil // ATTN_BLOCK

            def block(idx, carry):
                r, n = lax.div(idx, nb), lax.rem(idx, nb)
                cur, prev = _block_rows(dil, r, n), _block_rows(dil, r, jnp.maximum(n - 1, 0))
                q = qs[cur, :].astype(BF16)
                kc, kp = ks[cur, :].astype(BF16), ks[prev, :].astype(BF16)
                vc, vp = v_ref[cur, :].astype(BF16), v_ref[prev, :].astype(BF16)
                q2 = _stack_heads(q, first_head)
                mask = _stack_rows(jnp.concatenate([jnp.logical_and(prev_mask, n > 0), cur_mask], axis=1))
                s2 = jnp.where(mask, _dot(q2, jnp.concatenate([kp, kc], axis=0), "nt"), NEG_INF)
                m = jnp.max(s2, axis=-1, keepdims=True)
                vcat, two = jnp.concatenate([vp, vc], axis=0), _stack_rows(first_head)
                vext = jnp.concatenate([jnp.where(two, vcat, 1), jnp.where(two, 1, vcat)], axis=1)
                r2 = _dot(jnp.exp(s2 - m).astype(BF16), vext, "nn")
                r0, r1 = r2[:ATTN_BLOCK, :PAIR_W], r2[ATTN_BLOCK:, PAIR_W:]
                num = jnp.where(first_head, r0, r1)
                den = pltpu.roll(jnp.where(first_head, r1, r0), HEAD_DIM, 1)
                o_slab[cur, :] = num / den
                l_slab[cur, :] = jnp.where(first_head, m[:ATTN_BLOCK], m[ATTN_BLOCK:]) + jnp.log(den)
                return carry

            lax.fori_loop(0, SEQ // ATTN_BLOCK, block, 0, unroll=4)

        for gi, (o_slab, l_slab) in enumerate(((o0, l0), (o1, l1), (o2, l2))):
            @pl.when(g == gi)
            def _(gi=gi, o_slab=o_slab, l_slab=l_slab):
                run(DILATIONS[gi], o_slab, l_slab)

        @pl.when(g == N_GROUPS - 1)
        def _():
            a, b, cc = l0[...], l1[...], l2[...]
            m = jnp.maximum(jnp.maximum(a, b), cc)
            e0, e1, e2 = jnp.exp(a - m), jnp.exp(b - m), jnp.exp(cc - m)
            tot = e0 + e1 + e2
            attn = (e0 * o0[...] + e1 * o1[...] + e2 * o2[...]) / tot
            attn_ref[...] = attn
            attn_b_ref[...] = attn.astype(BF16)
            lse_ref[...] = m + jnp.log(tot)

    shape = (LOCAL_BATCH, SEQ, GROUP_W)
    slab = pltpu.VMEM((SEQ, PAIR_W), F32)
    return hosted_call(
        body, comm, "attn_fwd", (LOCAL_BATCH, N_PAIRS, N_GROUPS),
        [_slab_spec(0), _slab_spec(1), _slab_spec(2)], [_PAIR_SPEC] * 3,
        [jax.ShapeDtypeStruct(shape, BF16), jax.ShapeDtypeStruct(shape, F32), jax.ShapeDtypeStruct(shape, F32)],
        [slab] * 6, (qkv, qkv, qkv), ("parallel", "parallel", "arbitrary"))


def attn_bwd(qkv, tables, dattn, attn, lse, comm=None):
    scale = HEAD_DIM ** -0.5

    def body(qs, ks, v_ref, c_ref, sa_ref, sb_ref, do_ref, out_ref, lse_ref, dqkv_ref, dl, dq_s, dk_s, dv_s):
        g = pl.program_id(2)
        c, sa, sb = c_ref[...], sa_ref[...], sb_ref[...]

        @pl.when(g == 0)
        def _():
            prod = do_ref[...] * out_ref[...]
            dl[...] = _per_head(
                lambda sl: jnp.broadcast_to(jnp.sum(prod[:, sl], axis=-1, keepdims=True), (SEQ, HEAD_DIM)))

        cur_mask, prev_mask = _band_masks()
        first_head = lax.broadcasted_iota(jnp.int32, (ATTN_BLOCK, PAIR_W), 1) < HEAD_DIM

        def run(dil):
            nb = SEQ // dil // ATTN_BLOCK

            def block(idx, carry):
                r, n = lax.div(idx, nb), lax.rem(idx, nb)
                cur = _block_rows(dil, r, n)
                prev = _block_rows(dil, r, jnp.maximum(n - 1, 0))
                nxt = _block_rows(dil, r, jnp.minimum(n + 1, nb - 1))
                q0, q1 = qs[cur, :].astype(BF16), qs[nxt, :].astype(BF16)
                kp, kc = ks[prev, :].astype(BF16), ks[cur, :].astype(BF16)
                vp, vc = v_ref[prev, :].astype(BF16), v_ref[cur, :].astype(BF16)
                do0, do1 = do_ref[cur, :].astype(BF16), do_ref[nxt, :].astype(BF16)
                lse0, lse1, dl0, dl1 = lse_ref[cur, :], lse_ref[nxt, :], dl[cur, :], dl[nxt, :]
                has_prev = jnp.logical_and(prev_mask, n > 0)
                has_next = jnp.logical_and(prev_mask, n < nb - 1)

                def per_row(t):
                    return jnp.concatenate([t[:, 0:1], t[:, HEAD_DIM:HEAD_DIM + 1]], axis=0)

                q20, q21 = _stack_heads(q0, first_head), _stack_heads(q1, first_head)
                do20, do21 = _stack_heads(do0, first_head), _stack_heads(do1, first_head)
                kcat, vcat = jnp.concatenate([kp, kc], axis=0), jnp.concatenate([vp, vc], axis=0)
                mask0 = _stack_rows(jnp.concatenate([has_prev, cur_mask], axis=1))
                p0 = jnp.where(mask0, jnp.exp(_dot(q20, kcat, "nt") - per_row(lse0)), 0.0)
                ds0 = (p0 * (_dot(do20, vcat, "nt") - per_row(dl0))).astype(BF16)
                p1 = jnp.where(_stack_rows(has_next), jnp.exp(_dot(q21, kc, "nt") - per_row(lse1)), 0.0)
                ds1 = (p1 * (_dot(do21, vc, "nt") - per_row(dl1))).astype(BF16)
                dq2 = _dot(ds0, kcat, "nn")
                dq_s[cur, :] = jnp.where(first_head, dq2[:ATTN_BLOCK], dq2[ATTN_BLOCK:])
                ds_cur = jnp.concatenate([ds0[:, ATTN_BLOCK:], ds1], axis=0)
                p_cur = jnp.concatenate([p0[:, ATTN_BLOCK:], p1], axis=0).astype(BF16)
                dk_s[cur, :] = _dot(ds_cur, jnp.concatenate([q20, q21], axis=0), "tn")
                dv_s[cur, :] = _dot(p_cur, jnp.concatenate([do20, do21], axis=0), "tn")
                return carry

            lax.fori_loop(0, SEQ // ATTN_BLOCK, block, 0, unroll=2)

        for gi in range(N_GROUPS):
            @pl.when(g == gi)
            def _(gi=gi):
                run(DILATIONS[gi])

        dqkv_ref[:, 0:PAIR_W] = _rope_bwd(dq_s[...] * scale, c, sa, sb).astype(BF16)
        dqkv_ref[:, PAIR_W:2 * PAIR_W] = _rope_bwd(dk_s[...], c, sa, sb).astype(BF16)
        dqkv_ref[:, 2 * PAIR_W:] = dv_s[...].astype(BF16)

    slab = pltpu.VMEM((SEQ, PAIR_W), F32)
    return hosted_call(
        body, comm, "attn_bwd", (LOCAL_BATCH, N_PAIRS, N_GROUPS),
        [_slab_spec(0), _slab_spec(1), _slab_spec(2), _TABLE_SPEC, _TABLE_SPEC, _TABLE_SPEC,
         _PAIR_SPEC, _PAIR_SPEC, _PAIR_SPEC],
        [pl.BlockSpec((None, SEQ, 3 * PAIR_W), lambda b, p, g: (b, 0, p * N_GROUPS + g))],
        [jax.ShapeDtypeStruct((LOCAL_BATCH, SEQ, QKV_W), BF16)],
        [slab] * 4, (qkv, qkv, qkv, *tables, dattn, attn, lse), ("parallel", "parallel", "arbitrary"))


def _discretize(lr, li, log_dt, br, bi):
    dt = jnp.exp(log_dt)
    mag = jnp.exp(lr * dt)
    ab_re, ab_im = mag * jnp.cos(li * dt), mag * jnp.sin(li * dt)
    den = lr * lr + li * li
    nr, ni = ab_re - 1.0, ab_im
    f_re = (nr * lr + ni * li) / den
    f_im = (ni * lr - nr * li) / den
    return ab_re, ab_im, f_re[None] * br - f_im[None] * bi, f_re[None] * bi + f_im[None] * br


def ssm_prep(lr, li, log_dt, br, bi):
    def body(lr_ref, li_ref, dt_ref, br_ref, bi_ref, *outs):
        for o, v in zip(outs, _discretize(lr_ref[...], li_ref[...], dt_ref[...], br_ref[...], bi_ref[...])):
            o[...] = v
    shapes = [lr, li, br, bi]
    return pl.pallas_call(body, name="ssm_prep",
                          out_shape=[jax.ShapeDtypeStruct(s.shape, F32) for s in shapes])(lr, li, log_dt, br, bi)


def ssm_prep_bwd(lr, li, log_dt, br, bi, g_ab_re, g_ab_im, g_bb_re, g_bb_im):
    def body(lr_ref, li_ref, dt_ref, br_ref, bi_ref, g0, g1, g2, g3, *outs):
        _, vjp = jax.vjp(_discretize, lr_ref[...], li_ref[...], dt_ref[...], br_ref[...], bi_ref[...])
        for o, v in zip(outs, vjp((g0[...], g1[...], g2[...], g3[...]))):
            o[...] = v
    shapes = [lr, li, log_dt, br, bi]
    return pl.pallas_call(body, name="ssm_prep_bwd",
                          out_shape=[jax.ShapeDtypeStruct(s.shape, F32) for s in shapes])(
        lr, li, log_dt, br, bi, g_ab_re, g_ab_im, g_bb_re, g_bb_im)


def _block_diag(t):
    per = SSM_STATE_W // SSM_LANE_BLOCKS // 64
    g = t.transpose(1, 0, 2).reshape(SSM_LANE_BLOCKS, per, 16, 64)
    eye = jnp.eye(per, dtype=t.dtype)
    return jnp.einsum("jgcn,gh->jgchn", g, eye).reshape(SSM_LANE_BLOCKS, per * 16, per * 64)


def _block_diag_t(m):
    per = SSM_STATE_W // SSM_LANE_BLOCKS // 64
    m5 = m.reshape(SSM_LANE_BLOCKS, per, 16, per, 64)
    d = jnp.einsum("jgchn,gh->jgcn", m5, jnp.eye(per, dtype=m.dtype))
    return d.reshape(SSM_LANE_BLOCKS * per, 16, 64).transpose(1, 0, 2)


def _cmul(ar, ai, br, bi):
    return ar * br - ai * bi, ar * bi + ai * br


def _power_tables(ar, ai, reverse):
    width = ar.shape[1]
    row = lax.broadcasted_iota(jnp.int32, (8, width), 0)
    pows = [(ar, ai)]
    for _ in range(7):
        pows.append(_cmul(pows[-1][0], pows[-1][1], ar, ai))
    steps = []
    for k in (1, 2, 4):
        keep = (row >= k) if not reverse else (row < 8 - k)
        steps.append((jnp.where(keep, pows[k - 1][0], 0.0), jnp.where(keep, pows[k - 1][1], 0.0)))
    cr = jnp.zeros((8, width), F32)
    ci = jnp.zeros((8, width), F32)
    for i in range(8):
        pr, pi = pows[i] if not reverse else pows[7 - i]
        cr = jnp.where(row == i, pr, cr)
        ci = jnp.where(row == i, pi, ci)
    return steps, (cr, ci)


SCAN_CHUNK = 2048
STATE_BLOCK = SSM_STATE_W // SSM_LANE_BLOCKS
CHAN_BLOCK = SSM_W // SSM_LANE_BLOCKS


def ssm_fwd(u, ab_re, ab_im, bb_re, bb_im, cb_re, cb_im, d_skip, comm=None):
    nt = SEQ // SCAN_CHUNK
    chan = pl.BlockSpec((None, SCAN_CHUNK, CHAN_BLOCK), lambda b, j, t: (b, t, j))
    state = pl.BlockSpec((None, SCAN_CHUNK, STATE_BLOCK), lambda b, j, t: (b, t, j))
    mat = pl.BlockSpec((None, CHAN_BLOCK, STATE_BLOCK), lambda b, j, t: (j, 0, 0))
    lane = pl.BlockSpec((1, STATE_BLOCK), lambda b, j, t: (0, j))
    dsp = pl.BlockSpec((1, CHAN_BLOCK), lambda b, j, t: (0, j))

    def body(u_ref, ar_ref, ai_ref, bbr_ref, bbi_ref, cbr_ref, cbi_ref, d_ref, y_ref, yg_ref, xr_ref, xi_ref,
             car_r, car_i):
        @pl.when(pl.program_id(2) == 0)
        def _():
            car_r[...] = jnp.zeros_like(car_r)
            car_i[...] = jnp.zeros_like(car_i)

        steps, (pr, pi) = _power_tables(ar_ref[...], ai_ref[...], reverse=False)
        uf = u_ref[...]
        ub = uf.astype(BF16)
        xr_ref[...] = _dot(ub, bbr_ref[...], "nn")
        xi_ref[...] = _dot(ub, bbi_ref[...], "nn")

        def tile(i, carry):
            cr, ci = carry
            sl = pl.ds(pl.multiple_of(i * 8, 8), 8)
            br, bi = xr_ref[sl, :], xi_ref[sl, :]
            for k, (sr, si) in zip((1, 2, 4), steps):
                tr, ti = _cmul(sr, si, pltpu.roll(br, k, 0), pltpu.roll(bi, k, 0))
                br, bi = br + tr, bi + ti
            tr, ti = _cmul(pr, pi, cr, ci)
            br, bi = br + tr, bi + ti
            xr_ref[sl, :] = br
            xi_ref[sl, :] = bi
            return br[7:8, :], bi[7:8, :]

        cr, ci = lax.fori_loop(0, SCAN_CHUNK // 8, tile, (car_r[0:1, :], car_i[0:1, :]), unroll=4)
        car_r[0:1, :] = cr
        car_i[0:1, :] = ci
        y = (_dot(xr_ref[...].astype(BF16), cbr_ref[...], "nt") - _dot(xi_ref[...].astype(BF16), cbi_ref[...], "nt")
             + d_ref[...] * uf)
        y_ref[...] = y
        yg_ref[...] = jax.nn.gelu(y).astype(BF16)

    return hosted_call(
        body, comm, "ssm_fwd", (LOCAL_BATCH, SSM_LANE_BLOCKS, nt),
        [chan, lane, lane, mat, mat, mat, mat, dsp], [chan, chan, state, state],
        [jax.ShapeDtypeStruct((LOCAL_BATCH, SEQ, SSM_W), F32), jax.ShapeDtypeStruct((LOCAL_BATCH, SEQ, SSM_W), BF16),
         jax.ShapeDtypeStruct((LOCAL_BATCH, SEQ, SSM_STATE_W), F32),
         jax.ShapeDtypeStruct((LOCAL_BATCH, SEQ, SSM_STATE_W), F32)],
        [pltpu.VMEM((8, STATE_BLOCK), F32), pltpu.VMEM((8, STATE_BLOCK), F32)],
        (u, ab_re, ab_im, bb_re, bb_im, cb_re, cb_im, d_skip), ("parallel", "parallel", "arbitrary"))


def ssm_bwd(dyg, y, u, xr, xi, ab_re, ab_im, bb_re, bb_im, cb_re, cb_im, d_skip, comm=None):
    nt = SEQ // SCAN_CHUNK
    ntile = SCAN_CHUNK // 8

    def rev(t):
        return nt - 1 - t

    chan = pl.BlockSpec((None, SCAN_CHUNK, CHAN_BLOCK), lambda j, b, t: (b, rev(t), j))
    state = pl.BlockSpec((None, SCAN_CHUNK, STATE_BLOCK), lambda j, b, t: (b, rev(t), j))
    before = pl.BlockSpec((None, 8, STATE_BLOCK), lambda j, b, t: (b, jnp.maximum(rev(t) * ntile - 1, 0), j))
    mat = pl.BlockSpec((None, CHAN_BLOCK, STATE_BLOCK), lambda j, b, t: (j, 0, 0))
    lane = pl.BlockSpec((1, STATE_BLOCK), lambda j, b, t: (0, j))
    lane8 = pl.BlockSpec((8, STATE_BLOCK), lambda j, b, t: (0, j))
    dsp = pl.BlockSpec((1, CHAN_BLOCK), lambda j, b, t: (0, j))

    def body(dyg_ref, y_ref, u_ref, xr_ref, xi_ref, xrb_ref, xib_ref, ar_ref, ai_ref, bbr_ref, bbi_ref, cbr_ref,
             cbi_ref, d_ref, du_ref, dcbr_ref, dcbi_ref, dbbr_ref, dbbi_ref, dd_ref, dar_ref, dai_ref,
             lam_r, lam_i, car_r, car_i):
        b, t = pl.program_id(1), pl.program_id(2)
        first = jnp.logical_and(b == 0, t == 0)

        @pl.when(t == 0)
        def _():
            car_r[...] = jnp.zeros_like(car_r)
            car_i[...] = jnp.zeros_like(car_i)

        @pl.when(first)
        def _():
            for r in (dcbr_ref, dcbi_ref, dbbr_ref, dbbi_ref, dd_ref, dar_ref, dai_ref):
                r[...] = jnp.zeros_like(r)

        steps, (pr, pi) = _power_tables(ar_ref[...], -ai_ref[...], reverse=True)
        uf = u_ref[...]
        _, gelu_vjp = jax.vjp(jax.nn.gelu, y_ref[...])
        dy = gelu_vjp(dyg_ref[...])[0]
        dyb = dy.astype(BF16)
        dd_ref[...] += _colsum(dy * uf)
        lam_r[...] = _dot(dyb, cbr_ref[...], "nn")
        lam_i[...] = -_dot(dyb, cbi_ref[...], "nn")
        dcbr_ref[...] += _dot(dyb, xr_ref[...].astype(BF16), "tn")
        dcbi_ref[...] -= _dot(dyb, xi_ref[...].astype(BF16), "tn")
        row0 = lax.broadcasted_iota(jnp.int32, (8, STATE_BLOCK), 0) == 0
        has_before = rev(t) > 0
        xrb = jnp.where(has_before, xrb_ref[...], 0.0)
        xib = jnp.where(has_before, xib_ref[...], 0.0)

        def tile(s, carry):
            cr, ci, acc_r, acc_i = carry
            i = ntile - 1 - s
            sl = pl.ds(pl.multiple_of(i * 8, 8), 8)
            gr, gi = lam_r[sl, :], lam_i[sl, :]
            for k, (sr, si) in zip((1, 2, 4), steps):
                tr, ti = _cmul(sr, si, pltpu.roll(gr, 8 - k, 0), pltpu.roll(gi, 8 - k, 0))
                gr, gi = gr + tr, gi + ti
            tr, ti = _cmul(pr, pi, cr, ci)
            gr, gi = gr + tr, gi + ti
            lam_r[sl, :] = gr
            lam_i[sl, :] = gi
            sp = pl.ds(pl.multiple_of(jnp.maximum(i - 1, 0) * 8, 8), 8)
            pvr = jnp.where(i > 0, xr_ref[sp, :], xrb)
            pvi = jnp.where(i > 0, xi_ref[sp, :], xib)
            xsr = jnp.where(row0, pltpu.roll(pvr, 1, 0), pltpu.roll(xr_ref[sl, :], 1, 0))
            xsi = jnp.where(row0, pltpu.roll(pvi, 1, 0), pltpu.roll(xi_ref[sl, :], 1, 0))
            acc_r = acc_r + xsr * gr + xsi * gi
            acc_i = acc_i + xsr * gi - xsi * gr
            return gr[0:1, :], gi[0:1, :], acc_r, acc_i

        zero = jnp.zeros((8, STATE_BLOCK), F32)
        cr, ci, acc_r, acc_i = lax.fori_loop(0, ntile, tile, (car_r[0:1, :], car_i[0:1, :], zero, zero), unroll=2)
        car_r[0:1, :] = cr
        car_i[0:1, :] = ci
        dar_ref[...] += acc_r
        dai_ref[...] += acc_i
        lrb, lib = lam_r[...].astype(BF16), lam_i[...].astype(BF16)
        du = _dot(lrb, bbr_ref[...], "nt") + _dot(lib, bbi_ref[...], "nt") + d_ref[...] * dy
        du_ref[...] = du.astype(BF16)
        ub = uf.astype(BF16)
        dbbr_ref[...] += _dot(ub, lrb, "tn")
        dbbi_ref[...] += _dot(ub, lib, "tn")

    mat_shape = jax.ShapeDtypeStruct((SSM_LANE_BLOCKS, CHAN_BLOCK, STATE_BLOCK), F32)
    return hosted_call(
        body, comm, "ssm_bwd", (SSM_LANE_BLOCKS, LOCAL_BATCH, nt),
        [chan, chan, chan, state, state, before, before, lane, lane, mat, mat, mat, mat, dsp],
        [chan, mat, mat, mat, mat, dsp, lane8, lane8],
        [jax.ShapeDtypeStruct((LOCAL_BATCH, SEQ, SSM_W), BF16), mat_shape, mat_shape, mat_shape, mat_shape,
         jax.ShapeDtypeStruct((1, SSM_W), F32), jax.ShapeDtypeStruct((8, SSM_STATE_W), F32),
         jax.ShapeDtypeStruct((8, SSM_STATE_W), F32)],
        [pltpu.VMEM((SCAN_CHUNK, STATE_BLOCK), F32), pltpu.VMEM((SCAN_CHUNK, STATE_BLOCK), F32),
         pltpu.VMEM((8, STATE_BLOCK), F32), pltpu.VMEM((8, STATE_BLOCK), F32)],
        (dyg, y, u, xr, xi, xr, xi, ab_re, ab_im, bb_re, bb_im, cb_re, cb_im, d_skip),
        ("parallel", "arbitrary", "arbitrary"))


def _merge_fn(g0, g1, attn_d, za, zb):
    return jax.nn.sigmoid(g0) * attn_d + jax.nn.sigmoid(g1) * (za * jax.nn.sigmoid(zb))


def _swiglu_fn(a, b):
    return jax.nn.silu(a) * b


def _own_slot(slots, shard):
    me = 2 * lax.axis_index("x") + lax.axis_index("y")
    mine = lax.broadcasted_iota(jnp.int32, (N_CHIPS, 1, 1), 0) == me
    return jnp.where(mine, shard[None], slots)


def _reduce_start(names, gw, shard_shapes):
    return swap_comm([_to_slots(n, gw[n], shard_shapes[n]) for n in names])


def _reduce_chip(names, slots, got, core):
    return exchange_comm([add_halves(n, g, r, core) for n, g, r in zip(names, slots, got)])


def local_step(x, target, shards, small, core):
    g_mix, g_ffn, g_final = small["norm_mix_g"], small["norm_ffn_g"], small["norm_final_g"]
    tables = _rope_tables()
    seqs = lambda t: t.reshape(LOCAL_BATCH, SEQ, t.shape[-1])
    toks = lambda t: t.reshape(TOKENS, t.shape[-1])
    shard_shapes = {n: s.shape for n, s in shards.items()}
    w = {}

    def gather(names):
        return gather_comm([shards[n] for n in names])

    def arrived(names, slots, own=None):
        for n, s in zip(names, slots):
            w[n] = _from_slots(n, s if own is None else _own_slot(s, own))

    later = [n for n in BIG if n != "w_in"]
    sems, w_in_shard, land, token = split_start(shards["w_in"].astype(BF16), "w_in_gather_start")
    zero = token[0, 0]
    h, *rest = first_norm(x, g_mix + zero, [shards[n] for n in later])
    shards = dict(shards)
    shards.update(zip(later, rest))
    br_t = small["ssm_b_re"].transpose(2, 0, 1)
    bi_t = small["ssm_b_im"].transpose(2, 0, 1)
    log_dt = small["ssm_log_dt"].reshape(32, 1)
    ab_re, ab_im, bb_re_t, bb_im_t = ssm_prep(small["ssm_a_re"] + zero, small["ssm_a_im"], log_dt, br_t, bi_t)
    ab = [ab_re.reshape(1, SSM_STATE_W), ab_im.reshape(1, SSM_STATE_W)]
    bb = [_block_diag(bb_re_t).astype(BF16), _block_diag(bb_im_t).astype(BF16)]
    cb = [_block_diag((small["ssm_c_re"] + zero).transpose(1, 0, 2)).astype(BF16),
          _block_diag((small["ssm_c_im"] + zero).transpose(1, 0, 2)).astype(BF16)]
    d_skip = small["ssm_d"].reshape(1, SSM_W)
    w_in_shard, land = split_wait(sems, w_in_shard, land, [h] + bb + cb, "w_in_gather_wait")
    arrived(["w_in"], [handover(land, "w_in_handover")], own=w_in_shard)
    w_qkv, w_u, w_gate = _qkv_order(w["w_in"][:QKV_W]), w["w_in"][QKV_W:QKV_W + SSM_W], w["w_in"][QKV_W + SSM_W:]
    qkv, *slots = proj_qkv(h, w_qkv, tables, comm=gather(["w_attn_out", "w_glu"]))
    arrived(["w_attn_out", "w_glu"], slots)
    qkv = seqs(qkv)
    u = seqs(matmul(h, w_u, "nt", F32, "proj_u"))
    gl, *slots = matmul(h, w_gate, "nt", BF16, "proj_gate", comm=gather(["w_out"]))
    arrived(["w_out"], slots)
    attn_b, attn, lse, *slots = attn_fwd(qkv, comm=gather(["w_ffn_gate"]))
    arrived(["w_ffn_gate"], slots)
    attn_b = toks(attn_b)
    y, yg, xr, xi, *slots = ssm_fwd(u, *ab, *bb, *cb, d_skip, comm=gather(["w_ffn_up"]))
    arrived(["w_ffn_up"], slots)
    yg2 = toks(yg)
    merged, x1, h2 = merge_out_proj(gl, attn_b, w["w_attn_out"], yg2, w["w_glu"], w["w_out"], x, g_ffn)
    a, b, act, *slots = ffn_in(h2, w["w_ffn_gate"], w["w_ffn_up"], comm=gather(["w_ffn_down"]))
    arrived(["w_ffn_down"], slots)

    def final_fn(xv, g, tgt):
        yv, vjp = jax.vjp(_rms, xv, g)
        err = yv - tgt
        dx, dg = vjp(err * (1.0 / D_MODEL))
        loss = 0.5 * jnp.sum(jnp.mean(err * err, axis=-1, keepdims=True), axis=0, keepdims=True)
        return dx, dx, dg, jnp.broadcast_to(loss, (1, LANES))

    dx2, dx2_b, dg_final, loss = matmul_rows(act, w["w_ffn_down"], "ffn_down_loss", final_fn, [g_final, target],
                                             [(D_MODEL, F32), (D_MODEL, BF16)], accs=(D_MODEL, LANES), add=x1)
    gw, parts = {}, {}
    gw["w_ffn_down"] = matmul(act, dx2_b, "tn", F32, "d_ffn_down")
    da_b, db_b = ffn_in_bwd(dx2_b, w["w_ffn_down"], a, b)
    gw["w_ffn_gate"] = matmul(da_b, h2, "tn", F32, "d_ffn_gate")
    gw["w_ffn_up"] = matmul(db_b, h2, "tn", F32, "d_ffn_up")
    ffn = ["w_ffn_down", "w_ffn_gate", "w_ffn_up"]
    swap = _reduce_start(ffn, gw, shard_shapes)

    def norm_bwd(dh, xv, g, skip):
        _, vjp = jax.vjp(_rms, xv, g)
        dx, dg = vjp(dh)
        dx = dx + skip
        return dx, dx, dg

    dx1, dx1_b, dg_ffn, *got = matmul_rows(
        [da_b, db_b], [w["w_ffn_gate"], w["w_ffn_up"]], "d_h2_norm", norm_bwd, [x1, g_ffn, dx2],
        [(D_MODEL, F32), (D_MODEL, BF16)], accs=(D_MODEL,), comm=swap, tm=256)
    ffn_exchange = [_reduce_chip(ffn[:2], swap.ins[:2], got[:2], core)]
    ffn_up_exchange = _reduce_chip(ffn[2:], swap.ins[2:], got[2:], core)
    gw["w_out"] = matmul(merged, dx1_b, "tn", F32, "d_out")
    dgl_b, dattn_d_b, dz_b, dattn, dyg, parts["w_ffn_up"] = merge_bwd(
        dx1_b, w["w_out"], gl, attn_b, w["w_attn_out"], yg2, w["w_glu"], comm=ffn_up_exchange)
    dattn, dyg = seqs(dattn), seqs(dyg)
    gw["w_attn_out"] = matmul(attn_b, dattn_d_b, "tn", F32, "d_attn_out", col_slots=N_CHIPS)
    gw["w_glu"] = matmul(yg2, dz_b, "tn", F32, "d_glu", col_slots=N_CHIPS)
    mixer = ["w_out", "w_attn_out", "w_glu"]
    swap = _reduce_start(mixer, gw, shard_shapes)
    du_b, dcb_re, dcb_im, dbb_re, dbb_im, dd, da_re8, da_im8, *rest = ssm_bwd(
        dyg, y, u, xr, xi, *ab, *bb, *cb, d_skip, comm=join_comms(ffn_exchange + [swap]))
    for n, p in zip(ffn[:2], rest[:2]):
        parts[n] = p
    mixer_exchange = _reduce_chip(mixer, swap.ins, rest[2:], core)
    du_b = toks(du_b)
    g_ab_re = jnp.sum(da_re8, axis=0).reshape(32, 64)
    g_ab_im = jnp.sum(da_im8, axis=0).reshape(32, 64)
    d_lr, d_li, d_ldt, d_br_t, d_bi_t = ssm_prep_bwd(
        small["ssm_a_re"], small["ssm_a_im"], log_dt, br_t, bi_t,
        g_ab_re, g_ab_im, _block_diag_t(dbb_re), _block_diag_t(dbb_im))
    as_gcn = lambda t: t.transpose(1, 0, 2).reshape(SSM_W, 64)
    gs = {
        "ssm_a_re": d_lr, "ssm_a_im": d_li, "ssm_log_dt": d_ldt.reshape(1, 32),
        "ssm_b_re": as_gcn(d_br_t), "ssm_b_im": as_gcn(d_bi_t),
        "ssm_c_re": as_gcn(_block_diag_t(dcb_re)), "ssm_c_im": as_gcn(_block_diag_t(dcb_im)),
        "ssm_d": dd.reshape(32, 16).T,
    }
    ssm_gather = small_comm([gs[n] for n in SSM_SMALL])
    dqkv_b, *rest = attn_bwd(qkv, tables, dattn, attn, lse, comm=join_comms([mixer_exchange, ssm_gather]))
    for n, p in zip(mixer, rest):
        parts[n] = p
    ssm_shares = rest[len(mixer):]
    dqkv_b = toks(dqkv_b)
    d_qkv = matmul(dqkv_b, h, "tn", F32, "d_w_qkv")
    d_u = matmul(du_b, h, "tn", F32, "d_w_u")
    d_gate = matmul(dgl_b, h, "tn", F32, "d_w_gate")
    gw["w_in"] = jnp.concatenate([_qkv_order(d_qkv, back=True), d_u, d_gate], axis=0)
    swap = _reduce_start(["w_in"], gw, shard_shapes)
    dh, *got = matmul(dqkv_b, w_qkv, "nn", F32, "d_h_qkv", comm=swap)
    chip_sum = add_halves("w_in", swap.ins[0], got[0], core)
    sems, chip_sum, land, token = split_start(chip_sum, "w_in_reduce_start", per_chip=True)
    grad_x, dg_mix = mix_in_bwd([du_b, dgl_b], [w_u, w_gate], dh, x, g_mix + token[0, 0], dx1)
    gs_norm = {"norm_mix_g": dg_mix, "norm_ffn_g": dg_ffn, "norm_final_g": dg_final}
    return loss, grad_x, parts, ssm_shares, gs_norm, (sems, chip_sum, land)


ANY = pl.BlockSpec(memory_space=pl.ANY)
BIG = ("w_in", "w_glu", "w_attn_out", "w_out", "w_ffn_gate", "w_ffn_up", "w_ffn_down")
TRANSPOSED = ("w_in", "w_ffn_gate", "w_ffn_up")
ROW_SHARDED = TRANSPOSED + ("w_out", "w_ffn_down")
SMALL = ("norm_mix_g", "ssm_a_re", "ssm_a_im", "ssm_log_dt", "ssm_b_re", "ssm_b_im", "ssm_c_re", "ssm_c_im",
         "ssm_d", "norm_ffn_g", "norm_final_g")
WEIGHTS = ("norm_mix_g", "w_in", "ssm_a_re", "ssm_a_im", "ssm_log_dt", "ssm_b_re", "ssm_b_im", "ssm_c_re",
           "ssm_c_im", "ssm_d", "w_glu", "w_attn_out", "w_out", "norm_ffn_g", "w_ffn_gate", "w_ffn_up",
           "w_ffn_down", "norm_final_g")
SSM_SMALL = SMALL[1:9]
NORM_SMALL = (SMALL[0],) + SMALL[9:]
NORM_ROWS = 32
N_BIG = len(BIG)


def _position():
    return lax.axis_index("x"), lax.axis_index("y"), lax.axis_index("c")


def _other_chips(x, y):
    return [(1 - x, y), (x, 1 - y), (1 - x, 1 - y)]


def _remote(src, dst, send_sem, recv_sem, device):
    return pltpu.make_async_remote_copy(src_ref=src, dst_ref=dst, send_sem=send_sem, recv_sem=recv_sem,
                                        device_id=device, device_id_type=MESH)


_later = functools.partial


def _two_level_phases(copies):
    def first(*refs):
        locals_, sends, _, _, _ = copies(*refs)
        for cp in locals_ + sends:
            cp().start()

    def mid(*refs):
        _, _, arrived, passed, _ = copies(*refs)
        for got, cp in zip(arrived, passed):
            got().wait_recv()
            cp().start()

    def last(*refs):
        locals_, sends, _, passed, from_sibling = copies(*refs)
        for cp in from_sibling:
            cp().wait_recv()
        for cp in sends + passed:
            cp().wait_send()
        for cp in locals_:
            cp().wait()

    return first, mid, last


def _half(ref, chip, which):
    rows = ref.shape[1] // 2
    return ref.at[chip, pl.ds(which * rows, rows), :]


class Comm:
    def __init__(self, ins, out_shapes, sems, first, mid, last):
        self.ins, self.out_shapes, self.sems = list(ins), list(out_shapes), list(sems)
        self.first, self.mid, self.last = first, mid, last


def join_comms(comms):
    def cut(refs_by_kind):
        offs, parts = [0, 0, 0], []
        for cm in comms:
            sizes = (len(cm.ins), len(cm.out_shapes), len(cm.sems))
            parts.append(tuple(refs_by_kind[k][offs[k]:offs[k] + sizes[k]] for k in range(3)))
            offs = [o + s for o, s in zip(offs, sizes)]
        return parts

    def phase(which):
        def run(ins, outs, sems):
            for cm, part in zip(comms, cut((ins, outs, sems))):
                fn = getattr(cm, which)
                if fn is not None:
                    fn(*part)
        return run

    return Comm(sum((cm.ins for cm in comms), []), sum((cm.out_shapes for cm in comms), []),
                sum((cm.sems for cm in comms), []), phase("first"), phase("mid"), phase("last"))


def _comm_operands(comm):
    if comm is None:
        return [], [], []
    return comm.ins, comm.out_shapes, comm.sems


def _comm_begin(comm, refs, step, n_steps):
    if comm is None:
        return
    pl.when(step == 0)(lambda: comm.first(*refs))
    if comm.mid is not None:
        pl.when(step == (n_steps * 3) // 4)(lambda: comm.mid(*refs))


def _comm_end(comm, refs, step, n_steps):
    if comm is not None:
        pl.when(step == n_steps - 1)(lambda: comm.last(*refs))


def _comm_refs(comm, refs, n_in, n_out):
    if comm is None:
        return list(refs), None
    ci, co, cs = len(comm.ins), len(comm.out_shapes), len(comm.sems)
    o0 = n_in + ci
    s0 = o0 + n_out + co
    host = list(refs[:n_in]) + list(refs[o0:o0 + n_out]) + list(refs[s0:len(refs) - cs])
    return host, (list(refs[n_in:o0]), list(refs[o0 + n_out:s0]), list(refs[len(refs) - cs:]))


def run_comm(comm, name):
    n_in, n_out = len(comm.ins), len(comm.out_shapes)

    def body(*refs):
        parts = (list(refs[:n_in]), list(refs[n_in:n_in + n_out]), list(refs[n_in + n_out:]))
        comm.first(*parts)
        if comm.mid is not None:
            comm.mid(*parts)
        comm.last(*parts)

    return pl.pallas_call(body, name=name, in_specs=[ANY] * n_in, out_specs=[ANY] * n_out,
                          out_shape=comm.out_shapes, scratch_shapes=comm.sems)(*comm.ins)


def hosted_call(work, comm, name, grid, in_specs, out_specs, out_shape, scratch_shapes, args, semantics):
    c_ins, c_outs, c_sems = _comm_operands(comm)
    n_steps = math.prod(grid)

    def body(*refs):
        host, c_refs = _comm_refs(comm, refs, len(in_specs), len(out_specs))
        step = 0
        for axis, size in enumerate(grid):
            step = step * size + pl.program_id(axis)
        _comm_begin(comm, c_refs, step, n_steps)
        work(*host)
        _comm_end(comm, c_refs, step, n_steps)

    return pl.pallas_call(
        body, name=name, grid=grid, in_specs=list(in_specs) + [ANY] * len(c_ins),
        out_specs=list(out_specs) + [ANY] * len(c_outs), out_shape=list(out_shape) + c_outs,
        scratch_shapes=list(scratch_shapes) + c_sems,
        compiler_params=_params(semantics if comm is None else ("arbitrary",) * len(grid)),
    )(*args, *c_ins)


def gather_comm(shards):
    n = len(shards)

    def copies(srcs, outs, sems):
        send_sems, recv_sems, local_sems = sems
        x, y, c = _position()
        me = 2 * x + y
        sibling = (x, y, 1 - c)
        chips = _other_chips(x, y)
        locals_ = [_later(pltpu.make_async_copy, s, o.at[me], local_sems.at[i])
                   for i, (s, o) in enumerate(zip(srcs, outs))]
        sends, arrived, passed, from_sibling = [], [], [], []
        for j, (px, py) in enumerate(chips):
            for i, (s, o) in enumerate(zip(srcs, outs)):
                rows = s.shape[0] // 2
                sends.append(_later(_remote, s.at[pl.ds(c * rows, rows), :], _half(o, me, c), send_sems.at[i, j],
                                    recv_sems.at[i, j], (px, py, c)))
                got = _half(o, 2 * px + py, c)
                arrived.append(_later(_remote, got, got, send_sems.at[i, j], recv_sems.at[i, j], (px, py, c)))
                passed.append(_later(_remote, got, got, send_sems.at[i, 3 + j], recv_sems.at[i, 3 + j], sibling))
                other = _half(o, 2 * px + py, 1 - c)
                from_sibling.append(_later(_remote, other, other, send_sems.at[i, 3 + j], recv_sems.at[i, 3 + j],
                                           sibling))
        return locals_, sends, arrived, passed, from_sibling

    return Comm(shards, [jax.ShapeDtypeStruct((N_CHIPS,) + s.shape, s.dtype) for s in shards],
                [pltpu.SemaphoreType.DMA((n, 6)), pltpu.SemaphoreType.DMA((n, 6)), pltpu.SemaphoreType.DMA((n,))],
                *_two_level_phases(copies))


HBM = pl.BlockSpec(memory_space=pltpu.HBM)
SEM = pl.BlockSpec(memory_space=pltpu.SEMAPHORE)
N_OTHER = N_CHIPS - 1


def _ici_halves(src_ref, land_ref, sems, per_chip):
    x, y, c = _position()
    me = 2 * x + y
    rows = land_ref.shape[1] // 2
    sends, arrivals = [], []
    for j, (px, py) in enumerate(_other_chips(x, y)):
        piece = src_ref.at[2 * px + py] if per_chip else src_ref.at[pl.ds(c * rows, rows), :]
        sends.append(_later(_remote, piece, _half(land_ref, me, c), sems[j], sems[N_OTHER + j], (px, py, c)))
        got = _half(land_ref, 2 * px + py, c)
        arrivals.append(_later(_remote, got, got, sems[j], sems[N_OTHER + j], (px, py, c)))
    return sends, arrivals


def split_start(src, name, per_chip=False):
    def body(src_ref, land_ref, *rest):
        sems, token = rest[:2 * N_OTHER], rest[-1]
        for cp in _ici_halves(src_ref, land_ref, sems, per_chip)[0]:
            cp().start()
        token[...] = jnp.zeros_like(token)

    rows, cols = (2 * src.shape[1], src.shape[2]) if per_chip else src.shape
    sem = pltpu.SemaphoreType.DMA(())
    land = (N_CHIPS, rows, cols)
    res = pl.pallas_call(
        body, name=name, in_specs=(HBM, HBM),
        out_specs=(SEM,) * (2 * N_OTHER) + (HBM, HBM, pl.BlockSpec(memory_space=pltpu.VMEM)),
        out_shape=(sem,) * (2 * N_OTHER) + (pltpu.HBM(src.shape, src.dtype), pltpu.HBM(land, src.dtype),
                                           jax.ShapeDtypeStruct((8, LANES), F32)),
        input_output_aliases={0: 2 * N_OTHER, 1: 2 * N_OTHER + 1},
        compiler_params=pltpu.CompilerParams(has_side_effects=pltpu.SideEffectType.DATAFLOW_SIDE_EFFECTING),
    )(pltpu.with_memory_space_constraint(src, pltpu.HBM),
      pltpu.with_memory_space_constraint(lax.empty(land, src.dtype), pltpu.HBM))
    return res[:2 * N_OTHER], res[2 * N_OTHER], res[2 * N_OTHER + 1], res[-1]


def split_wait(sems, src, land, after, name, per_chip=False):
    def body(src_ref, land_ref, *rest):
        sends, arrivals = _ici_halves(src_ref, land_ref, rest[:2 * N_OTHER], per_chip)
        for cp in sends:
            cp().wait_send()
        for cp in arrivals:
            cp().wait_recv()

    return pl.pallas_call(
        body, name=name, in_specs=(HBM, HBM) + (SEM,) * (2 * N_OTHER) + (ANY,) * len(after),
        out_specs=(HBM, HBM), out_shape=(pltpu.HBM(src.shape, src.dtype), pltpu.HBM(land.shape, land.dtype)),
        input_output_aliases={0: 0, 1: 1},
        compiler_params=pltpu.CompilerParams(has_side_effects=pltpu.SideEffectType.DATAFLOW_SIDE_EFFECTING),
    )(src, land, *sems, *after)


def handover(land, name, sums=None):
    n = N_OTHER + (sums is not None)

    def body(*refs):
        land_ref, send_sems, recv_sems = refs[0], refs[-2], refs[-1]
        x, y, c = _position()
        me = 2 * x + y
        sibling = (x, y, 1 - c)
        pieces = [(_half(land_ref, 2 * px + py, c), 2 * px + py) for px, py in _other_chips(x, y)]
        if sums is not None:
            pieces.append((refs[1].at[me], me))
        sends = [_remote(piece, _half(land_ref, chip, c), send_sems.at[j], recv_sems.at[j], sibling)
                 for j, (piece, chip) in enumerate(pieces)]
        for cp in sends:
            cp.start()
        for j, (_, chip) in enumerate(pieces):
            other = _half(land_ref, chip, 1 - c)
            _remote(other, other, send_sems.at[j], recv_sems.at[j], sibling).wait_recv()
        for cp in sends:
            cp.wait_send()

    args = (land,) + ((sums,) if sums is not None else ())
    return pl.pallas_call(
        body, name=name, in_specs=[ANY] * len(args), out_specs=ANY,
        out_shape=jax.ShapeDtypeStruct(land.shape, land.dtype), input_output_aliases={0: 0},
        scratch_shapes=[pltpu.SemaphoreType.DMA((n,)), pltpu.SemaphoreType.DMA((n,))],
    )(*args)


def swap_comm(grads):
    n = len(grads)

    def copies(srcs, gots, sems):
        send_sems, recv_sems = sems
        x, y, c = _position()
        out = []
        for i, (s, o) in enumerate(zip(srcs, gots)):
            rows = s.shape[1] // 2
            out.append(_remote(s.at[:, pl.ds((1 - c) * rows, rows), :], o, send_sems.at[i], recv_sems.at[i],
                               (x, y, 1 - c)))
        return out

    def first(srcs, gots, sems):
        for cp in copies(srcs, gots, sems):
            cp.start()

    def last(srcs, gots, sems):
        for cp in copies(srcs, gots, sems):
            cp.wait()

    return Comm(grads, [jax.ShapeDtypeStruct((N_CHIPS, g.shape[1] // 2, g.shape[2]), g.dtype) for g in grads],
                [pltpu.SemaphoreType.DMA((n,)), pltpu.SemaphoreType.DMA((n,))], first, None, last)


def add_halves(name, g, got, core):
    _, half, cols = got.shape
    mine = pl.BlockSpec((None, half, cols), lambda k, c_ref: (k, c_ref[0], 0))
    other = pl.BlockSpec((None, half, cols), lambda k, c_ref: (k, 0, 0))

    def body(c_ref, g_ref, got_ref, o_ref):
        o_ref[...] = (g_ref[...] + got_ref[...]).astype(BF16)

    return pl.pallas_call(
        body, name="add_halves_" + name,
        grid_spec=pltpu.PrefetchScalarGridSpec(num_scalar_prefetch=1, grid=(N_CHIPS,), in_specs=[mine, other],
                                               out_specs=other),
        out_shape=jax.ShapeDtypeStruct(got.shape, BF16),
        compiler_params=_params(("parallel",)),
    )(core, g, got)


def exchange_comm(parts):
    n = len(parts)

    def copies(srcs, outs, sems):
        send_sems, recv_sems, local_sems = sems
        x, y, c = _position()
        me = 2 * x + y
        sibling = (x, y, 1 - c)
        chips = _other_chips(x, y)
        locals_, sends, arrived, passed, from_sibling = [], [], [], [], []
        for i, (s, o) in enumerate(zip(srcs, outs)):
            locals_.append(_later(pltpu.make_async_copy, s.at[me], _half(o, me, c), local_sems.at[i]))
            sends.append(_later(_remote, s.at[me], _half(o, me, c), send_sems.at[i, 3], recv_sems.at[i, 3], sibling))
            other = _half(o, me, 1 - c)
            from_sibling.append(_later(_remote, other, other, send_sems.at[i, 3], recv_sems.at[i, 3], sibling))
        for j, (px, py) in enumerate(chips):
            for i, (s, o) in enumerate(zip(srcs, outs)):
                sends.append(_later(_remote, s.at[2 * px + py], _half(o, me, c), send_sems.at[i, j],
                                    recv_sems.at[i, j], (px, py, c)))
                got = _half(o, 2 * px + py, c)
                arrived.append(_later(_remote, got, got, send_sems.at[i, j], recv_sems.at[i, j], (px, py, c)))
                passed.append(_later(_remote, got, got, send_sems.at[i, 4 + j], recv_sems.at[i, 4 + j], sibling))
                other = _half(o, 2 * px + py, 1 - c)
                from_sibling.append(_later(_remote, other, other, send_sems.at[i, 4 + j], recv_sems.at[i, 4 + j],
                                           sibling))
        return locals_, sends, arrived, passed, from_sibling

    return Comm(parts, [jax.ShapeDtypeStruct((N_CHIPS, 2 * p.shape[1], p.shape[2]), p.dtype) for p in parts],
                [pltpu.SemaphoreType.DMA((n, 7)), pltpu.SemaphoreType.DMA((n, 7)), pltpu.SemaphoreType.DMA((n,))],
                *_two_level_phases(copies))


def small_comm(shares):
    n = len(shares)

    def copies(srcs, outs, sems):
        send_sems, recv_sems, local_sems = sems
        x, y, c = _position()
        me = 4 * x + 2 * y + c
        flips = [(fx, fy, fc) for fx in (0, 1) for fy in (0, 1) for fc in (0, 1)][1:]
        peers = [(1 - x if fx else x, 1 - y if fy else y, 1 - c if fc else c) for fx, fy, fc in flips]
        locals_, sends, arrived = [], [], []
        for i, (src_ref, out_ref) in enumerate(zip(srcs, outs)):
            locals_.append(_later(pltpu.make_async_copy, src_ref, out_ref.at[me], local_sems.at[i]))
            for j, (px, py, pc) in enumerate(peers):
                sends.append(_later(_remote, src_ref, out_ref.at[me], send_sems.at[i, j], recv_sems.at[i, j],
                                    (px, py, pc)))
                got = out_ref.at[4 * px + 2 * py + pc]
                arrived.append(_later(_remote, got, got, send_sems.at[i, j], recv_sems.at[i, j], (px, py, pc)))
        return locals_, sends, arrived

    def first(*refs):
        locals_, sends, _ = copies(*refs)
        for cp in locals_ + sends:
            cp().start()

    def last(*refs):
        locals_, sends, arrived = copies(*refs)
        for cp in arrived:
            cp().wait_recv()
        for cp in sends:
            cp().wait_send()
        for cp in locals_:
            cp().wait()

    return Comm(shares, [jax.ShapeDtypeStruct((N_DEV,) + s.shape, s.dtype) for s in shares],
                [pltpu.SemaphoreType.DMA((n, 7)), pltpu.SemaphoreType.DMA((n, 7)), pltpu.SemaphoreType.DMA((n,))],
                first, None, last)


def _adam_fn(w, g, m, v):
    m = ADAM_B1 * m + (1.0 - ADAM_B1) * g
    v = ADAM_B2 * v + (1.0 - ADAM_B2) * jnp.square(g)
    m_hat = m / (1.0 - ADAM_B1 ** ADAM_STEP)
    v_hat = v / (1.0 - ADAM_B2 ** ADAM_STEP)
    return -ADAM_LR * (m_hat / (jnp.sqrt(v_hat) + ADAM_EPS) + ADAM_WD * w), m, v


def adam_big(name, parts, w, m, v):
    rows, cols = w.shape
    tm = _pick(rows, 384, 16)

    def fn(p0, p1, p2, p3, wv, mv, vv):
        g = ((p0.astype(F32) + p1.astype(F32)) + p2.astype(F32)) + p3.astype(F32)
        return (g,) + _adam_fn(wv, g, mv, vv)

    return rowwise(fn, [parts, w, m, v], [(cols, F32)] * 4, "adam_" + name, tm=tm, rows=rows)


def adam_small(name, gathered, w, m, v):
    def body(g_ref, w_ref, m_ref, v_ref, go_ref, d_ref, mo_ref, vo_ref):
        g = g_ref[0]
        for k in range(1, N_DEV):
            g = g + g_ref[k]
        go_ref[...] = g
        d_ref[...], mo_ref[...], vo_ref[...] = _adam_fn(w_ref[...], g, m_ref[...], v_ref[...])

    return pl.pallas_call(body, name=name, out_shape=[jax.ShapeDtypeStruct(w.shape, F32)] * 4,
                          compiler_params=_params())(gathered, w, m, v)


def _ssm_2d(name, t):
    t = t[0] if t.ndim > 2 else t
    if name in ("ssm_b_re", "ssm_b_im"):
        return t.transpose(0, 2, 1).reshape(SSM_W, 64)
    if name in ("ssm_c_re", "ssm_c_im"):
        return t.reshape(SSM_W, 64)
    return t.T if name == "ssm_d" else t


def _ssm_back(name, t):
    if name in ("ssm_b_re", "ssm_b_im"):
        return t.reshape(32, 16, 64).transpose(0, 2, 1)[None]
    if name in ("ssm_c_re", "ssm_c_im"):
        return t.reshape(1, 32, 16, 64)
    if name == "ssm_d":
        return t.T[None]
    return t if name == "ssm_log_dt" else t[None]


def adam_ssm(shares, w, m, v):
    n = len(w)

    def body(*refs):
        ins, outs = refs[:4 * n], refs[4 * n:]
        for i in range(n):
            g_ref, w_ref, m_ref, v_ref = (ins[k * n + i] for k in range(4))
            g = g_ref[0]
            for k in range(1, N_DEV):
                g = g + g_ref[k]
            outs[4 * i][...] = g
            outs[4 * i + 1][...], outs[4 * i + 2][...], outs[4 * i + 3][...] = _adam_fn(w_ref[...], g, m_ref[...],
                                                                                      v_ref[...])

    out_shape = [jax.ShapeDtypeStruct(t.shape, F32) for t in w for _ in range(4)]
    res = pl.pallas_call(body, name="adam_ssm", out_shape=out_shape, compiler_params=_params())(*shares, *w, *m, *v)
    return [res[4 * i:4 * i + 4] for i in range(n)]


def _pack_small(names, vals, rows, last=None):
    flat = [vals[n].reshape(-1) for n in names]
    if last is not None:
        flat.append(last.reshape(-1))
    flat = jnp.concatenate(flat)
    return jnp.pad(flat, (0, rows * LANES - flat.shape[0])).reshape(rows, LANES)


def _unpack_small(names, pack, shapes):
    flat, out, off = pack.reshape(-1), {}, 0
    for n in names:
        size = math.prod(shapes[n])
        out[n] = flat[off:off + size].reshape(shapes[n])
        off += size
    return out, flat[off]


def _to_slots(name, g, shard_shape):
    rows, cols = shard_shape
    if g.shape == (N_CHIPS, rows, cols):
        return g
    if name in ROW_SHARDED:
        return g.reshape(N_CHIPS, rows, cols)
    return g.reshape(rows, N_CHIPS, cols).transpose(1, 0, 2)


def _from_slots(name, s):
    _, rows, cols = s.shape
    if name in ROW_SHARDED:
        return s.reshape(N_CHIPS * rows, cols)
    return s.transpose(1, 0, 2).reshape(rows, N_CHIPS * cols)


def kernel(x, norm_mix_g, w_in, ssm_a_re, ssm_a_im, ssm_log_dt, ssm_b_re, ssm_b_im, ssm_c_re, ssm_c_im, ssm_d, w_glu, w_attn_out, w_out, norm_ffn_g, w_ffn_gate, w_ffn_up, w_ffn_down, norm_final_g, loss_target, m_norm_mix_g, m_w_in, m_ssm_a_re, m_ssm_a_im, m_ssm_log_dt, m_ssm_b_re, m_ssm_b_im, m_ssm_c_re, m_ssm_c_im, m_ssm_d, m_w_glu, m_w_attn_out, m_w_out, m_norm_ffn_g, m_w_ffn_gate, m_w_ffn_up, m_w_ffn_down, m_norm_final_g, v_norm_mix_g, v_w_in, v_ssm_a_re, v_ssm_a_im, v_ssm_log_dt, v_ssm_b_re, v_ssm_b_im, v_ssm_c_re, v_ssm_c_im, v_ssm_d, v_w_glu, v_w_attn_out, v_w_out, v_norm_ffn_g, v_w_ffn_gate, v_w_ffn_up, v_w_ffn_down, v_norm_final_g):
    given = dict(locals())
    def local(name, prefix=""):
        t = given[prefix + name][0]
        return t.T if name in TRANSPOSED else t

    shard = {n: local(n) for n in BIG}
    shapes = {n: given[n].shape for n in WEIGHTS}

    small = {n: given[n] for n in SMALL}
    small_2d = dict(small)
    for n in ("ssm_a_re", "ssm_a_im", "ssm_b_re", "ssm_b_im", "ssm_c_re", "ssm_c_im", "ssm_d"):
        small_2d[n] = small[n][0]
    small_2d["norm_final_g"] = norm_final_g.reshape(1, D_MODEL)

    core = lax.axis_index("c").astype(jnp.int32).reshape(1)
    loss, grad_x, parts, ssm_shares, gs_norm, w_in_reduce = local_step(
        x.reshape(TOKENS, D_MODEL), loss_target.reshape(TOKENS, D_MODEL),
        {n: shard[n] for n in BIG}, small_2d, core)

    (norm_shares,) = run_comm(small_comm([_pack_small(NORM_SMALL, gs_norm, NORM_ROWS, last=loss)]),
                              "gather_norm_grads")
    small_out = [{} for _ in range(4)]
    packs = [_pack_small(NORM_SMALL, {n: given[p + n] for n in NORM_SMALL}, NORM_ROWS) for p in ("", "m_", "v_")]
    for kind, t in enumerate(adam_small("adam_norm_gains", norm_shares, *packs)):
        vals, after = _unpack_small(NORM_SMALL, t, shapes)
        small_out[kind].update(vals)
        if kind == 0:
            total_loss = after
    ssm_in = [[_ssm_2d(n, given[p + n]) for n in SSM_SMALL] for p in ("", "m_", "v_")]
    for n, res in zip(SSM_SMALL, adam_ssm(ssm_shares, *ssm_in)):
        for kind, t in enumerate(res):
            small_out[kind][n] = _ssm_back(n, t)

    big_out, updated = {}, {}
    for n in BIG[1:] + BIG[:1]:
        if n == "w_in":
            sems, chip_sum, land = w_in_reduce
            behind = [updated[k][1] for k in BIG[1:]] + [norm_shares]
            chip_sum, land = split_wait(sems, chip_sum, land, behind, "w_in_reduce_wait", per_chip=True)
            land = handover(land, "w_in_reduce_handover", sums=chip_sum)
            me = 2 * lax.axis_index("x") + lax.axis_index("y")
            parts[n] = lax.dynamic_update_slice(land, lax.dynamic_slice_in_dim(chip_sum, me, 1, 0),
                                                (me, lax.axis_index("c") * chip_sum.shape[1], 0))
        updated[n] = adam_big(n, parts[n], shard[n], local(n, "m_"), local(n, "v_"))
        big_out[n] = [(t.T if n in TRANSPOSED else t)[None] for t in updated[n]]

    outs = [total_loss, grad_x.reshape(LOCAL_BATCH, SEQ, D_MODEL)]
    for kind in range(4):
        for n in WEIGHTS:
            outs.append(big_out[n][kind] if n in BIG else small_out[kind][n])
    return tuple(outs)
```

```python
import functools
import math

import jax
import jax.numpy as jnp
import numpy as np
from jax import lax
from jax.experimental import pallas as pl
from jax.experimental.pallas import tpu as pltpu

F32 = jnp.float32
BF16 = jnp.bfloat16
MESH = pl.DeviceIdType.MESH

D_MODEL = 1024
SEQ = 2048
LOCAL_BATCH = 2
TOKENS = LOCAL_BATCH * SEQ
HEAD_DIM = 64
HEADS_PER_GROUP = 4
GROUP_W = HEADS_PER_GROUP * HEAD_DIM
N_GROUPS = 3
DILATIONS = (1, 4, 16)
ATTN_BLOCK = 128
ROPE_DIM = 16
ROPE_THETA = 500000.0
QKV_W = 3 * N_GROUPS * GROUP_W
SSM_W = 512
SSM_STATE_W = 2048
SSM_LANE_BLOCKS = 4
GATE_W = 2 * D_MODEL
D_FF = 2816
RMS_EPS = 1e-6
NEG_INF = -1e30
ADAM_LR, ADAM_B1, ADAM_B2, ADAM_EPS, ADAM_WD, ADAM_STEP = 0.001, 0.9, 0.999, 1e-08, 0.01, 10
N_CHIPS = 4
N_DEV = 8

VMEM_LIMIT = 56 * 1024 * 1024
LANES = 128


def _params(sem=None):
    return pltpu.CompilerParams(dimension_semantics=sem, vmem_limit_bytes=VMEM_LIMIT)


def _pick(n, cap, align=LANES):
    best = None
    for d in range(align, min(n, cap) + 1, align):
        if n % d == 0:
            best = d
    return n if best is None or n <= cap else best


_DIMS = {"nn": (((1,), (0,)), ((), ())), "nt": (((1,), (1,)), ((), ())), "tn": (((0,), (0,)), ((), ()))}


def _dot(a, b, mode):
    return lax.dot_general(a, b, _DIMS[mode], preferred_element_type=F32)


def matmul(a, b, mode, out_dtype, name, add=None, comm=None, col_slots=1):
    if mode == "nn":
        (m, k), n = a.shape, b.shape[1]
    elif mode == "nt":
        (m, k), n = a.shape, b.shape[0]
    else:
        (k, m), n = a.shape, b.shape[1]
    assert n % col_slots == 0 and (col_slots == 1 or add is None)
    tn = _pick(n // col_slots, 1408 if mode != "tn" else 512)
    tk = _pick(k, 2816) if mode != "tn" else k
    tm = _pick(m, 1408)
    out_bytes = jnp.dtype(out_dtype).itemsize

    def need(tm_):
        return 2 * 2 * (tm_ * tk + tk * tn) + tm_ * tn * (4 + 2 * out_bytes + (8 if add is not None else 0))

    while need(tm) > 40 * 1024 * 1024 and tm % 256 == 0:
        tm //= 2
    nk = k // tk
    a_spec = {"nn": pl.BlockSpec((tm, tk), lambda i, j, kk: (i, kk)),
              "nt": pl.BlockSpec((tm, tk), lambda i, j, kk: (i, kk)),
              "tn": pl.BlockSpec((tk, tm), lambda i, j, kk: (kk, i))}[mode]
    b_spec = {"nn": pl.BlockSpec((tk, tn), lambda i, j, kk: (kk, j)),
              "nt": pl.BlockSpec((tn, tk), lambda i, j, kk: (j, kk)),
              "tn": pl.BlockSpec((tk, tn), lambda i, j, kk: (kk, j))}[mode]
    o_spec, o_shape = pl.BlockSpec((tm, tn), lambda i, j, kk: (i, j)), (m, n)
    if col_slots > 1:
        per_slot = n // col_slots // tn
        o_spec = pl.BlockSpec((None, tm, tn), lambda i, j, kk: (lax.div(j, per_slot), i, lax.rem(j, per_slot)))
        o_shape = (col_slots, m, n // col_slots)

    def body(a_ref, b_ref, *rest):
        if add is not None:
            add_ref, o_ref, acc_ref = rest
        else:
            o_ref, acc_ref = rest
        part = _dot(a_ref[...], b_ref[...], mode)
        if nk == 1:
            res = part if add is None else part + add_ref[...]
            o_ref[...] = res.astype(out_dtype)
            return
        kk = pl.program_id(2)

        @pl.when(kk == 0)
        def _():
            acc_ref[...] = part

        @pl.when(kk > 0)
        def _():
            acc_ref[...] += part

        @pl.when(kk == nk - 1)
        def _():
            res = acc_ref[...] if add is None else acc_ref[...] + add_ref[...]
            o_ref[...] = res.astype(out_dtype)

    in_specs = [a_spec, b_spec] + ([o_spec] if add is not None else [])
    args = (a, b) + ((add,) if add is not None else ())
    res = hosted_call(
        body, comm, name, (m // tm, n // tn, nk), in_specs, [o_spec], [jax.ShapeDtypeStruct(o_shape, out_dtype)],
        [pltpu.VMEM((tm, tn) if nk > 1 else (8, LANES), F32)], args, ("parallel", "parallel", "arbitrary"))
    return res[0] if comm is None else res


def matmul_rows(a, b, name, fn, extra, outs, accs=(), add=None, comm=None, tm=512):
    a_list, b_list = (list(a), list(b)) if isinstance(a, (list, tuple)) else ([a], [b])
    m, n = a_list[0].shape[0], b_list[0].shape[1]
    n_mm = len(a_list)
    n_fixed = 2 * n_mm + (add is not None)
    row_spec = lambda cols: pl.BlockSpec((tm, cols), lambda i: (i, 0))
    in_specs = [row_spec(t.shape[1]) for t in a_list] + [pl.BlockSpec(t.shape, lambda i: (0, 0)) for t in b_list]
    in_specs += [row_spec(n)] if add is not None else []
    in_specs += [pl.BlockSpec(e.shape, lambda i: (0, 0)) if e.shape[0] == 1 else row_spec(e.shape[1]) for e in extra]
    out_specs = [row_spec(c) for c, _ in outs] + [pl.BlockSpec((1, c), lambda i: (0, 0)) for c in accs]
    out_shape = [jax.ShapeDtypeStruct((m, c), dt) for c, dt in outs] + [jax.ShapeDtypeStruct((1, c), F32) for c in accs]

    def body(*refs):
        rows = _dot(refs[0][...], refs[n_mm][...], "nn")
        for i in range(1, n_mm):
            rows = rows + _dot(refs[i][...], refs[n_mm + i][...], "nn")
        if add is not None:
            rows = rows + refs[2 * n_mm][...]
        n_in = n_fixed + len(extra)
        res = fn(rows, *[r[...] for r in refs[n_fixed:n_in]])
        for r, v in zip(refs[n_in:n_in + len(outs)], res[:len(outs)]):
            r[...] = v.astype(r.dtype)
        first = pl.program_id(0) == 0
        for r, v in zip(refs[n_in + len(outs):], res[len(outs):]):
            @pl.when(first)
            def _(r=r, v=v):
                r[...] = v

            @pl.when(jnp.logical_not(first))
            def _(r=r, v=v):
                r[...] += v

    args = tuple(a_list) + tuple(b_list) + ((add,) if add is not None else ()) + tuple(extra)
    return hosted_call(body, comm, name, (m // tm,), in_specs, out_specs, out_shape, [], args, ("arbitrary",))


def _merge_specs(tm):
    half = lambda blk: pl.BlockSpec((tm, D_MODEL), functools.partial(lambda i, blk_: (i, blk_), blk_=blk))
    return [half(0), half(1), pl.BlockSpec((tm, GROUP_W), lambda i: (i, 0)),
            pl.BlockSpec((GROUP_W, D_MODEL), lambda i: (0, 0)), pl.BlockSpec((tm, SSM_W), lambda i: (i, 0)),
            pl.BlockSpec((SSM_W, GATE_W), lambda i: (0, 0))]


def _merge_operands(g0, g1, at, wa, yg, wg):
    z = _dot(yg[...], wg[...], "nn")
    return (g0[...].astype(F32), g1[...].astype(F32), _dot(at[...], wa[...], "nn"), z[:, :D_MODEL], z[:, D_MODEL:])


def merge_out_proj(gl, attn_b, w_attn_out, yg, w_glu, w_out, x, g_ffn):
    tm = 512

    def body(g0, g1, at, wa, yg_ref, wg, w_ref, x_ref, g_ref, m_ref, x1_ref, h2_ref):
        merged = _merge_fn(*_merge_operands(g0, g1, at, wa, yg_ref, wg)).astype(BF16)
        m_ref[...] = merged
        x1 = _dot(merged, w_ref[...], "nn") + x_ref[...]
        x1_ref[...] = x1
        h2_ref[...] = _rms(x1, g_ref[...]).astype(BF16)

    rows = pl.BlockSpec((tm, D_MODEL), lambda i: (i, 0))
    whole = pl.BlockSpec((D_MODEL, D_MODEL), lambda i: (0, 0))
    gain = pl.BlockSpec((1, D_MODEL), lambda i: (0, 0))
    tok = lambda dt: jax.ShapeDtypeStruct((TOKENS, D_MODEL), dt)
    return pl.pallas_call(
        body, name="merge_out_proj", grid=(TOKENS // tm,), in_specs=_merge_specs(tm) + [whole, rows, gain],
        out_specs=[rows] * 3, out_shape=[tok(BF16), tok(F32), tok(BF16)], compiler_params=_params(("parallel",)),
    )(gl, gl, attn_b, w_attn_out, yg, w_glu, w_out, x, g_ffn)


def merge_bwd(dx1_b, w_out, gl, attn_b, w_attn_out, yg, w_glu, comm=None):
    tm = 512

    def body(dx_ref, w_ref, g0, g1, at, wa, yg_ref, wg, dgl_ref, dad_ref, dz_ref, dat_ref, dyg_ref):
        dm = _dot(dx_ref[...], w_ref[...], "nt")
        _, vjp = jax.vjp(_merge_fn, *_merge_operands(g0, g1, at, wa, yg_ref, wg))
        dg0, dg1, dad, dza, dzb = vjp(dm)
        dat_ref[...] = _dot(dad.astype(BF16), wa[...], "nt")
        dgl_ref[:, :D_MODEL] = dg0.astype(BF16)
        dgl_ref[:, D_MODEL:] = dg1.astype(BF16)
        dad_ref[...] = dad.astype(BF16)
        dz_ref[:, :D_MODEL] = dza.astype(BF16)
        dz_ref[:, D_MODEL:] = dzb.astype(BF16)
        dyg_ref[...] = _dot(dz_ref[...], wg[...], "nt")

    rows = pl.BlockSpec((tm, D_MODEL), lambda i: (i, 0))
    wide = pl.BlockSpec((tm, GATE_W), lambda i: (i, 0))
    whole = pl.BlockSpec((D_MODEL, D_MODEL), lambda i: (0, 0))
    return hosted_call(
        body, comm, "merge_bwd", (TOKENS // tm,), [rows, whole] + _merge_specs(tm),
        [wide, rows, wide, pl.BlockSpec((tm, GROUP_W), lambda i: (i, 0)), pl.BlockSpec((tm, SSM_W), lambda i: (i, 0))],
        [jax.ShapeDtypeStruct((TOKENS, GATE_W), BF16), jax.ShapeDtypeStruct((TOKENS, D_MODEL), BF16),
         jax.ShapeDtypeStruct((TOKENS, GATE_W), BF16), jax.ShapeDtypeStruct((TOKENS, GROUP_W), F32),
         jax.ShapeDtypeStruct((TOKENS, SSM_W), F32)], [],
        (dx1_b, w_out, gl, gl, attn_b, w_attn_out, yg, w_glu), ("arbitrary",))


FFN_TM, FFN_TN = 512, 1408


def ffn_in(h2, wg_t, wu_t, comm=None):
    def body(h_ref, wg_ref, wu_ref, a_ref, b_ref, act_ref):
        hv = h_ref[...]
        a, b = _dot(hv, wg_ref[...], "nt"), _dot(hv, wu_ref[...], "nt")
        a_ref[...] = a.astype(BF16)
        b_ref[...] = b.astype(BF16)
        act_ref[...] = _swiglu_fn(a, b).astype(BF16)

    rows = pl.BlockSpec((FFN_TM, D_MODEL), lambda i, j: (i, 0))
    wts = pl.BlockSpec((FFN_TN, D_MODEL), lambda i, j: (j, 0))
    out = pl.BlockSpec((FFN_TM, FFN_TN), lambda i, j: (i, j))
    return hosted_call(body, comm, "ffn_in", (TOKENS // FFN_TM, D_FF // FFN_TN), [rows, wts, wts], [out] * 3,
                       [jax.ShapeDtypeStruct((TOKENS, D_FF), BF16)] * 3, [], (h2, wg_t, wu_t),
                       ("parallel", "parallel"))


def ffn_in_bwd(dx2_b, wd, a, b):
    def body(dx_ref, wd_ref, a_ref, b_ref, da_ref, db_ref):
        dact = _dot(dx_ref[...], wd_ref[...], "nt")
        av, bv = a_ref[...].astype(F32), b_ref[...].astype(F32)
        sig = jax.nn.sigmoid(av)
        act = av * sig
        da_ref[...] = (dact * bv * (sig * (1.0 + av - act))).astype(BF16)
        db_ref[...] = (dact * act).astype(BF16)

    rows = pl.BlockSpec((FFN_TM, D_MODEL), lambda i, j: (i, 0))
    wts = pl.BlockSpec((FFN_TN, D_MODEL), lambda i, j: (j, 0))
    out = pl.BlockSpec((FFN_TM, FFN_TN), lambda i, j: (i, j))
    return pl.pallas_call(
        body, name="ffn_in_bwd", grid=(TOKENS // FFN_TM, D_FF // FFN_TN), in_specs=[rows, wts, out, out],
        out_specs=[out] * 2, out_shape=[jax.ShapeDtypeStruct((TOKENS, D_FF), BF16)] * 2,
        compiler_params=_params(("parallel", "parallel")),
    )(dx2_b, wd, a, b)


def mix_in_bwd(grads, weights, partial, x, g, skip, comm=None):
    n = len(grads)
    tm = 512

    def body(*refs):
        a_refs, b_refs = refs[:n], refs[n:2 * n]
        part_ref, x_ref, g_ref, skip_ref, gx_ref, dg_ref = refs[2 * n:]
        dh = part_ref[...]
        for a_ref, b_ref in zip(a_refs, b_refs):
            dh = dh + _dot(a_ref[...], b_ref[...], "nn")
        _, vjp = jax.vjp(_rms, x_ref[...], g_ref[...])
        dx, dg = vjp(dh)
        gx_ref[...] = dx + skip_ref[...]
        first = pl.program_id(0) == 0

        @pl.when(first)
        def _():
            dg_ref[...] = dg

        @pl.when(jnp.logical_not(first))
        def _():
            dg_ref[...] += dg

    rows = pl.BlockSpec((tm, D_MODEL), lambda i: (i, 0))
    gain = pl.BlockSpec((1, D_MODEL), lambda i: (0, 0))
    in_specs = [pl.BlockSpec((tm, a.shape[1]), lambda i: (i, 0)) for a in grads]
    in_specs += [pl.BlockSpec(b.shape, lambda i: (0, 0)) for b in weights]
    return hosted_call(
        body, comm, "mix_in_bwd", (TOKENS // tm,), in_specs + [rows, rows, gain, rows], [rows, gain],
        [jax.ShapeDtypeStruct((TOKENS, D_MODEL), F32), jax.ShapeDtypeStruct((1, D_MODEL), F32)], [],
        (*grads, *weights, partial, x, g, skip), ("arbitrary",))


def rowwise(fn, ins, outs, name, accs=(), tm=256, rows=TOKENS, comm=None):
    in_specs, args = [], []
    for item in ins:
        arr, width, blk = item if isinstance(item, tuple) else (item, None, 0)
        if arr.ndim == 3:
            for k in range(arr.shape[0]):
                in_specs.append(pl.BlockSpec((None, tm, arr.shape[2]), functools.partial(lambda i, k_: (k_, i, 0), k_=k)))
                args.append(arr)
            continue
        if arr.shape[0] == 1:
            in_specs.append(pl.BlockSpec(arr.shape, lambda i: (0, 0)))
        elif width is None:
            in_specs.append(pl.BlockSpec((tm, arr.shape[1]), lambda i: (i, 0)))
        else:
            in_specs.append(pl.BlockSpec((tm, width), functools.partial(lambda i, blk_: (i, blk_), blk_=blk)))
        args.append(arr)
    out_specs = [pl.BlockSpec((tm, c), lambda i: (i, 0)) for c, _ in outs]
    out_specs += [pl.BlockSpec((1, c), lambda i: (0, 0)) for c in accs]
    out_shape = [jax.ShapeDtypeStruct((rows, c), dt) for c, dt in outs]
    out_shape += [jax.ShapeDtypeStruct((1, c), F32) for c in accs]
    n_in, n_out = len(args), len(outs)
    c_ins, c_outs, c_sems = _comm_operands(comm)

    def body(*refs):
        refs, c_refs = _comm_refs(comm, refs, n_in, n_out + len(accs))
        step = pl.program_id(0)
        _comm_begin(comm, c_refs, step, rows // tm)
        res = fn(*[r[...] for r in refs[:n_in]])
        for r, v in zip(refs[n_in:n_in + n_out], res[:n_out]):
            r[...] = v.astype(r.dtype)
        first = step == 0
        for r, v in zip(refs[n_in + n_out:], res[n_out:]):
            @pl.when(first)
            def _(r=r, v=v):
                r[...] = v

            @pl.when(jnp.logical_not(first))
            def _(r=r, v=v):
                r[...] += v
        _comm_end(comm, c_refs, step, rows // tm)

    return pl.pallas_call(
        body, name=name, grid=(rows // tm,), in_specs=in_specs + [ANY] * len(c_ins),
        out_specs=out_specs + [ANY] * len(c_outs), out_shape=out_shape + c_outs, scratch_shapes=c_sems,
        compiler_params=_params(("arbitrary",)),
    )(*args, *c_ins)


def first_norm(x, g, others, comm=None):
    tm, n = 256, len(others)

    def body(x_ref, g_ref, *rest):
        srcs, h_ref, dsts = rest[:n], rest[n], rest[n + 1:]
        h_ref[...] = _rms(x_ref[...], g_ref[...]).astype(BF16)
        for k, (s, d) in enumerate(zip(srcs, dsts)):
            @pl.when(pl.program_id(0) == k)
            def _(s=s, d=d):
                d[...] = s[...].astype(BF16)

    rows = pl.BlockSpec((tm, D_MODEL), lambda i: (i, 0))
    whole = [pl.BlockSpec(a.shape, lambda i: (0, 0)) for a in others]
    return hosted_call(
        body, comm, "norm_mix", (TOKENS // tm,), [rows, pl.BlockSpec((1, D_MODEL), lambda i: (0, 0))] + whole,
        [rows] + whole, [jax.ShapeDtypeStruct((TOKENS, D_MODEL), BF16)]
        + [jax.ShapeDtypeStruct(a.shape, BF16) for a in others], [], (x, g, *others), ("arbitrary",))


def _rms(x, g):
    return x * lax.rsqrt(jnp.mean(x * x, axis=-1, keepdims=True) + RMS_EPS) * g


def _colsum(v):
    return jnp.sum(v, axis=0, keepdims=True)


PAIR_W = 2 * HEAD_DIM
N_PAIRS = HEADS_PER_GROUP // 2


def _qkv_order(w_t, back=False):
    dims = (N_PAIRS, N_GROUPS, 3) if back else (3, N_GROUPS, N_PAIRS)
    return w_t.reshape(dims + (PAIR_W, w_t.shape[1])).transpose(2, 1, 0, 3, 4).reshape(QKV_W, w_t.shape[1])


def _rope_tables():
    half = ROPE_DIM // 2
    inv = np.power(np.float32(ROPE_THETA), -np.arange(half, dtype=np.float32) * np.float32(2.0 / ROPE_DIM))
    ang = (np.arange(SEQ, dtype=np.float32)[:, None] * inv[None, :]).astype(np.float32)
    cos, sin = np.cos(ang), np.sin(ang)
    zeros = np.zeros((SEQ, HEAD_DIM - ROPE_DIM), np.float32)
    zh = np.zeros((SEQ, half), np.float32)
    c = np.concatenate([cos, cos, zeros + 1.0], axis=1)
    sa = np.concatenate([-sin, zh, zeros], axis=1)
    sb = np.concatenate([zh, sin, zeros], axis=1)
    return [jnp.asarray(np.tile(t, (1, 2)), F32) for t in (c, sa, sb)]


def _rope_fwd(x, c, sa, sb):
    return x * c + pltpu.roll(x, PAIR_W - 8, 1) * sa + pltpu.roll(x, 8, 1) * sb


def _rope_bwd(dy, c, sa, sb):
    return dy * c + pltpu.roll(dy * sb, PAIR_W - 8, 1) + pltpu.roll(dy * sa, 8, 1)


def _band_masks():
    row = lax.broadcasted_iota(jnp.int32, (ATTN_BLOCK, ATTN_BLOCK), 0)
    col = lax.broadcasted_iota(jnp.int32, (ATTN_BLOCK, ATTN_BLOCK), 1)
    return col <= row, col >= row


def _stack_rows(t):
    return jnp.concatenate([t, t], axis=0)


def _stack_heads(t, first_head):
    return jnp.concatenate([jnp.where(first_head, t, 0), jnp.where(first_head, 0, t)], axis=0)


def _per_head(fn):
    return jnp.concatenate([fn(slice(h * HEAD_DIM, (h + 1) * HEAD_DIM)) for h in range(2)], axis=1)


def _slab_spec(kind):
    return pl.BlockSpec((None, SEQ, PAIR_W), lambda b, p, g: (b, 0, p * 3 * N_GROUPS + g * 3 + kind))


_TABLE_SPEC = pl.BlockSpec((SEQ, PAIR_W), lambda b, p, g: (0, 0))
_PAIR_SPEC = pl.BlockSpec((None, SEQ, PAIR_W), lambda b, p, g: (b, 0, p))


def _block_rows(dil, r, n):
    return pl.ds(n * (ATTN_BLOCK * dil) + r, ATTN_BLOCK, stride=dil)


def proj_qkv(h, w_qkv_t, tables, comm=None):
    tm = 1024
    pair_w = QKV_W // N_PAIRS
    scale = HEAD_DIM ** -0.5

    def body(h_ref, w_ref, c_ref, sa_ref, sb_ref, o_ref):
        rows = _dot(h_ref[...], w_ref[...], "nt")
        c, sa, sb = c_ref[...], sa_ref[...], sb_ref[...]
        for blk in range(pair_w // PAIR_W):
            cols = slice(blk * PAIR_W, (blk + 1) * PAIR_W)
            x = rows[:, cols]
            if blk % 3 == 0:
                x = _rope_fwd(x, c, sa, sb) * scale
            elif blk % 3 == 1:
                x = _rope_fwd(x, c, sa, sb)
            o_ref[:, cols] = x

    table = pl.BlockSpec((tm, PAIR_W), lambda i, j, : (i % (SEQ // tm), 0))
    res = hosted_call(
        body, comm, "proj_qkv", (TOKENS // tm, N_PAIRS),
        [pl.BlockSpec((tm, D_MODEL), lambda i, j: (i, 0)), pl.BlockSpec((pair_w, D_MODEL), lambda i, j: (j, 0)),
         table, table, table],
        [pl.BlockSpec((tm, pair_w), lambda i, j: (i, j))], [jax.ShapeDtypeStruct((TOKENS, QKV_W), F32)], [],
        (h, w_qkv_t, *tables), ("parallel", "parallel"))
    return res[0] if comm is None else res


def attn_fwd(qkv, comm=None):
    def body(qs, ks, v_ref, attn_b_ref, attn_ref, lse_ref, o0, o1, o2, l0, l1, l2):
        g = pl.program_id(2)
        cur_mask, prev_mask = _band_masks()
        first_head = lax.broadcasted_iota(jnp.int32, (ATTN_BLOCK, PAIR_W), 1) < HEAD_DIM

        def run(dil, o_slab, l_slab):
            nb = SEQ // dil // ATTN_BLOCK

            def block(idx, carry):
                r, n = lax.div(idx, nb), lax.rem(idx, nb)
                cur, prev = _block_rows(dil, r, n), _block_rows(dil, r, jnp.maximum(n - 1, 0))
                q = qs[cur, :].astype(BF16)
                kc, kp = ks[cur, :].astype(BF16), ks[prev, :].astype(BF16)
                vc, vp = v_ref[cur, :].astype(BF16), v_ref[prev, :].astype(BF16)
                q2 = _stack_heads(q, first_head)
                mask = _stack_rows(jnp.concatenate([jnp.logical_and(prev_mask, n > 0), cur_mask], axis=1))
                s2 = jnp.where(mask, _dot(q2, jnp.concatenate([kp, kc], axis=0), "nt"), NEG_INF)
                m = jnp.max(s2, axis=-1, keepdims=True)
                vcat, two = jnp.concatenate([vp, vc], axis=0), _stack_rows(first_head)
                vext = jnp.concatenate([jnp.where(two, vcat, 1), jnp.where(two, 1, vcat)], axis=1)
                r2 = _dot(jnp.exp(s2 - m).astype(BF16), vext, "nn")
                r0, r1 = r2[:ATTN_BLOCK, :PAIR_W], r2[ATTN_BLOCK:, PAIR_W:]
                num = jnp.where(first_head, r0, r1)
                den = pltpu.roll(jnp.where(first_head, r1, r0), HEAD_DIM, 1)
                o_slab[cur, :] = num / den
                l_slab[cur, :] = jnp.where(first_head, m[:ATTN_BLOCK], m[ATTN_BLOCK:]) + jnp.log(den)
                return carry

            lax.fori_loop(0, SEQ // ATTN_BLOCK, block, 0, unroll=4)

        for gi, (o_slab, l_slab) in enumerate(((o0, l0), (o1, l1), (o2, l2))):
            @pl.when(g == gi)
            def _(gi=gi, o_slab=o_slab, l_slab=l_slab):
                run(DILATIONS[gi], o_slab, l_slab)

        @pl.when(g == N_GROUPS - 1)
        def _():
            a, b, cc = l0[...], l1[...], l2[...]
            m = jnp.maximum(jnp.maximum(a, b), cc)
            e0, e1, e2 = jnp.exp(a - m), jnp.exp(b - m), jnp.exp(cc - m)
            tot = e0 + e1 + e2
            attn = (e0 * o0[...] + e1 * o1[...] + e2 * o2[...]) / tot
            attn_ref[...] = attn
            attn_b_ref[...] = attn.astype(BF16)
            lse_ref[...] = m + jnp.log(tot)

    shape = (LOCAL_BATCH, SEQ, GROUP_W)
    slab = pltpu.VMEM((SEQ, PAIR_W), F32)
    return hosted_call(
        body, comm, "attn_fwd", (LOCAL_BATCH, N_PAIRS, N_GROUPS),
        [_slab_spec(0), _slab_spec(1), _slab_spec(2)], [_PAIR_SPEC] * 3,
        [jax.ShapeDtypeStruct(shape, BF16), jax.ShapeDtypeStruct(shape, F32), jax.ShapeDtypeStruct(shape, F32)],
        [slab] * 6, (qkv, qkv, qkv), ("parallel", "parallel", "arbitrary"))


def attn_bwd(qkv, tables, dattn, attn, lse, comm=None):
    scale = HEAD_DIM ** -0.5

    def body(qs, ks, v_ref, c_ref, sa_ref, sb_ref, do_ref, out_ref, lse_ref, dqkv_ref, dl, dq_s, dk_s, dv_s):
        g = pl.program_id(2)
        c, sa, sb = c_ref[...], sa_ref[...], sb_ref[...]

        @pl.when(g == 0)
        def _():
            prod = do_ref[...] * out_ref[...]
            dl[...] = _per_head(
                lambda sl: jnp.broadcast_to(jnp.sum(prod[:, sl], axis=-1, keepdims=True), (SEQ, HEAD_DIM)))

        cur_mask, prev_mask = _band_masks()
        first_head = lax.broadcasted_iota(jnp.int32, (ATTN_BLOCK, PAIR_W), 1) < HEAD_DIM

        def run(dil):
            nb = SEQ // dil // ATTN_BLOCK

            def block(idx, carry):
                r, n = lax.div(idx, nb), lax.rem(idx, nb)
                cur = _block_rows(dil, r, n)
                prev = _block_rows(dil, r, jnp.maximum(n - 1, 0))
                nxt = _block_rows(dil, r, jnp.minimum(n + 1, nb - 1))
                q0, q1 = qs[cur, :].astype(BF16), qs[nxt, :].astype(BF16)
                kp, kc = ks[prev, :].astype(BF16), ks[cur, :].astype(BF16)
                vp, vc = v_ref[prev, :].astype(BF16), v_ref[cur, :].astype(BF16)
                do0, do1 = do_ref[cur, :].astype(BF16), do_ref[nxt, :].astype(BF16)
                lse0, lse1, dl0, dl1 = lse_ref[cur, :], lse_ref[nxt, :], dl[cur, :], dl[nxt, :]
                has_prev = jnp.logical_and(prev_mask, n > 0)
                has_next = jnp.logical_and(prev_mask, n < nb - 1)

                def per_row(t):
                    return jnp.concatenate([t[:, 0:1], t[:, HEAD_DIM:HEAD_DIM + 1]], axis=0)

                q20, q21 = _stack_heads(q0, first_head), _stack_heads(q1, first_head)
                do20, do21 = _stack_heads(do0, first_head), _stack_heads(do1, first_head)
                kcat, vcat = jnp.concatenate([kp, kc], axis=0), jnp.concatenate([vp, vc], axis=0)
                mask0 = _stack_rows(jnp.concatenate([has_prev, cur_mask], axis=1))
                p0 = jnp.where(mask0, jnp.exp(_dot(q20, kcat, "nt") - per_row(lse0)), 0.0)
                ds0 = (p0 * (_dot(do20, vcat, "nt") - per_row(dl0))).astype(BF16)
                p1 = jnp.where(_stack_rows(has_next), jnp.exp(_dot(q21, kc, "nt") - per_row(lse1)), 0.0)
                ds1 = (p1 * (_dot(do21, vc, "nt") - per_row(dl1))).astype(BF16)
                dq2 = _dot(ds0, kcat, "nn")
                dq_s[cur, :] = jnp.where(first_head, dq2[:ATTN_BLOCK], dq2[ATTN_BLOCK:])
                ds_cur = jnp.concatenate([ds0[:, ATTN_BLOCK:], ds1], axis=0)
                p_cur = jnp.concatenate([p0[:, ATTN_BLOCK:], p1], axis=0).astype(BF16)
                dk_s[cur, :] = _dot(ds_cur, jnp.concatenate([q20, q21], axis=0), "tn")
                dv_s[cur, :] = _dot(p_cur, jnp.concatenate([do20, do21], axis=0), "tn")
                return carry

            lax.fori_loop(0, SEQ // ATTN_BLOCK, block, 0, unroll=2)

        for gi in range(N_GROUPS):
            @pl.when(g == gi)
            def _(gi=gi):
                run(DILATIONS[gi])

        dqkv_ref[:, 0:PAIR_W] = _rope_bwd(dq_s[...] * scale, c, sa, sb).astype(BF16)
        dqkv_ref[:, PAIR_W:2 * PAIR_W] = _rope_bwd(dk_s[...], c, sa, sb).astype(BF16)
        dqkv_ref[:, 2 * PAIR_W:] = dv_s[...].astype(BF16)

    slab = pltpu.VMEM((SEQ, PAIR_W), F32)
    return hosted_call(
        body, comm, "attn_bwd", (LOCAL_BATCH, N_PAIRS, N_GROUPS),
        [_slab_spec(0), _slab_spec(1), _slab_spec(2), _TABLE_SPEC, _TABLE_SPEC, _TABLE_SPEC,
         _PAIR_SPEC, _PAIR_SPEC, _PAIR_SPEC],
        [pl.BlockSpec((None, SEQ, 3 * PAIR_W), lambda b, p, g: (b, 0, p * N_GROUPS + g))],
        [jax.ShapeDtypeStruct((LOCAL_BATCH, SEQ, QKV_W), BF16)],
        [slab] * 4, (qkv, qkv, qkv, *tables, dattn, attn, lse), ("parallel", "parallel", "arbitrary"))


def _discretize(lr, li, log_dt, br, bi):
    dt = jnp.exp(log_dt)
    mag = jnp.exp(lr * dt)
    ab_re, ab_im = mag * jnp.cos(li * dt), mag * jnp.sin(li * dt)
    den = lr * lr + li * li
    nr, ni = ab_re - 1.0, ab_im
    f_re = (nr * lr + ni * li) / den
    f_im = (ni * lr - nr * li) / den
    return ab_re, ab_im, f_re[None] * br - f_im[None] * bi, f_re[None] * bi + f_im[None] * br


def ssm_prep(lr, li, log_dt, br, bi):
    def body(lr_ref, li_ref, dt_ref, br_ref, bi_ref, *outs):
        for o, v in zip(outs, _discretize(lr_ref[...], li_ref[...], dt_ref[...], br_ref[...], bi_ref[...])):
            o[...] = v
    shapes = [lr, li, br, bi]
    return pl.pallas_call(body, name="ssm_prep",
                          out_shape=[jax.ShapeDtypeStruct(s.shape, F32) for s in shapes])(lr, li, log_dt, br, bi)


def ssm_prep_bwd(lr, li, log_dt, br, bi, g_ab_re, g_ab_im, g_bb_re, g_bb_im):
    def body(lr_ref, li_ref, dt_ref, br_ref, bi_ref, g0, g1, g2, g3, *outs):
        _, vjp = jax.vjp(_discretize, lr_ref[...], li_ref[...], dt_ref[...], br_ref[...], bi_ref[...])
        for o, v in zip(outs, vjp((g0[...], g1[...], g2[...], g3[...]))):
            o[...] = v
    shapes = [lr, li, log_dt, br, bi]
    return pl.pallas_call(body, name="ssm_prep_bwd",
                          out_shape=[jax.ShapeDtypeStruct(s.shape, F32) for s in shapes])(
        lr, li, log_dt, br, bi, g_ab_re, g_ab_im, g_bb_re, g_bb_im)


def _block_diag(t):
    per = SSM_STATE_W // SSM_LANE_BLOCKS // 64
    g = t.transpose(1, 0, 2).reshape(SSM_LANE_BLOCKS, per, 16, 64)
    eye = jnp.eye(per, dtype=t.dtype)
    return jnp.einsum("jgcn,gh->jgchn", g, eye).reshape(SSM_LANE_BLOCKS, per * 16, per * 64)


def _block_diag_t(m):
    per = SSM_STATE_W // SSM_LANE_BLOCKS // 64
    m5 = m.reshape(SSM_LANE_BLOCKS, per, 16, per, 64)
    d = jnp.einsum("jgchn,gh->jgcn", m5, jnp.eye(per, dtype=m.dtype))
    return d.reshape(SSM_LANE_BLOCKS * per, 16, 64).transpose(1, 0, 2)


def _cmul(ar, ai, br, bi):
    return ar * br - ai * bi, ar * bi + ai * br


def _power_tables(ar, ai, reverse):
    width = ar.shape[1]
    row = lax.broadcasted_iota(jnp.int32, (8, width), 0)
    pows = [(ar, ai)]
    for _ in range(7):
        pows.append(_cmul(pows[-1][0], pows[-1][1], ar, ai))
    steps = []
    for k in (1, 2, 4):
        keep = (row >= k) if not reverse else (row < 8 - k)
        steps.append((jnp.where(keep, pows[k - 1][0], 0.0), jnp.where(keep, pows[k - 1][1], 0.0)))
    cr = jnp.zeros((8, width), F32)
    ci = jnp.zeros((8, width), F32)
    for i in range(8):
        pr, pi = pows[i] if not reverse else pows[7 - i]
        cr = jnp.where(row == i, pr, cr)
        ci = jnp.where(row == i, pi, ci)
    return steps, (cr, ci)


SCAN_CHUNK = 2048
STATE_BLOCK = SSM_STATE_W // SSM_LANE_BLOCKS
CHAN_BLOCK = SSM_W // SSM_LANE_BLOCKS


def ssm_fwd(u, ab_re, ab_im, bb_re, bb_im, cb_re, cb_im, d_skip, comm=None):
    nt = SEQ // SCAN_CHUNK
    chan = pl.BlockSpec((None, SCAN_CHUNK, CHAN_BLOCK), lambda b, j, t: (b, t, j))
    state = pl.BlockSpec((None, SCAN_CHUNK, STATE_BLOCK), lambda b, j, t: (b, t, j))
    mat = pl.BlockSpec((None, CHAN_BLOCK, STATE_BLOCK), lambda b, j, t: (j, 0, 0))
    lane = pl.BlockSpec((1, STATE_BLOCK), lambda b, j, t: (0, j))
    dsp = pl.BlockSpec((1, CHAN_BLOCK), lambda b, j, t: (0, j))

    def body(u_ref, ar_ref, ai_ref, bbr_ref, bbi_ref, cbr_ref, cbi_ref, d_ref, y_ref, yg_ref, xr_ref, xi_ref,
             car_r, car_i):
        @pl.when(pl.program_id(2) == 0)
        def _():
            car_r[...] = jnp.zeros_like(car_r)
            car_i[...] = jnp.zeros_like(car_i)

        steps, (pr, pi) = _power_tables(ar_ref[...], ai_ref[...], reverse=False)
        uf = u_ref[...]
        ub = uf.astype(BF16)
        xr_ref[...] = _dot(ub, bbr_ref[...], "nn")
        xi_ref[...] = _dot(ub, bbi_ref[...], "nn")

        def tile(i, carry):
            cr, ci = carry
            sl = pl.ds(pl.multiple_of(i * 8, 8), 8)
            br, bi = xr_ref[sl, :], xi_ref[sl, :]
            for k, (sr, si) in zip((1, 2, 4), steps):
                tr, ti = _cmul(sr, si, pltpu.roll(br, k, 0), pltpu.roll(bi, k, 0))
                br, bi = br + tr, bi + ti
            tr, ti = _cmul(pr, pi, cr, ci)
            br, bi = br + tr, bi + ti
            xr_ref[sl, :] = br
            xi_ref[sl, :] = bi
            return br[7:8, :], bi[7:8, :]

        cr, ci = lax.fori_loop(0, SCAN_CHUNK // 8, tile, (car_r[0:1, :], car_i[0:1, :]), unroll=4)
        car_r[0:1, :] = cr
        car_i[0:1, :] = ci
        y = (_dot(xr_ref[...].astype(BF16), cbr_ref[...], "nt") - _dot(xi_ref[...].astype(BF16), cbi_ref[...], "nt")
             + d_ref[...] * uf)
        y_ref[...] = y
        yg_ref[...] = jax.nn.gelu(y).astype(BF16)

    return hosted_call(
        body, comm, "ssm_fwd", (LOCAL_BATCH, SSM_LANE_BLOCKS, nt),
        [chan, lane, lane, mat, mat, mat, mat, dsp], [chan, chan, state, state],
        [jax.ShapeDtypeStruct((LOCAL_BATCH, SEQ, SSM_W), F32), jax.ShapeDtypeStruct((LOCAL_BATCH, SEQ, SSM_W), BF16),
         jax.ShapeDtypeStruct((LOCAL_BATCH, SEQ, SSM_STATE_W), F32),
         jax.ShapeDtypeStruct((LOCAL_BATCH, SEQ, SSM_STATE_W), F32)],
        [pltpu.VMEM((8, STATE_BLOCK), F32), pltpu.VMEM((8, STATE_BLOCK), F32)],
        (u, ab_re, ab_im, bb_re, bb_im, cb_re, cb_im, d_skip), ("parallel", "parallel", "arbitrary"))


def ssm_bwd(dyg, y, u, xr, xi, ab_re, ab_im, bb_re, bb_im, cb_re, cb_im, d_skip, comm=None):
    nt = SEQ // SCAN_CHUNK
    ntile = SCAN_CHUNK // 8

    def rev(t):
        return nt - 1 - t

    chan = pl.BlockSpec((None, SCAN_CHUNK, CHAN_BLOCK), lambda j, b, t: (b, rev(t), j))
    state = pl.BlockSpec((None, SCAN_CHUNK, STATE_BLOCK), lambda j, b, t: (b, rev(t), j))
    before = pl.BlockSpec((None, 8, STATE_BLOCK), lambda j, b, t: (b, jnp.maximum(rev(t) * ntile - 1, 0), j))
    mat = pl.BlockSpec((None, CHAN_BLOCK, STATE_BLOCK), lambda j, b, t: (j, 0, 0))
    lane = pl.BlockSpec((1, STATE_BLOCK), lambda j, b, t: (0, j))
    lane8 = pl.BlockSpec((8, STATE_BLOCK), lambda j, b, t: (0, j))
    dsp = pl.BlockSpec((1, CHAN_BLOCK), lambda j, b, t: (0, j))

    def body(dyg_ref, y_ref, u_ref, xr_ref, xi_ref, xrb_ref, xib_ref, ar_ref, ai_ref, bbr_ref, bbi_ref, cbr_ref,
             cbi_ref, d_ref, du_ref, dcbr_ref, dcbi_ref, dbbr_ref, dbbi_ref, dd_ref, dar_ref, dai_ref,
             lam_r, lam_i, car_r, car_i):
        b, t = pl.program_id(1), pl.program_id(2)
        first = jnp.logical_and(b == 0, t == 0)

        @pl.when(t == 0)
        def _():
            car_r[...] = jnp.zeros_like(car_r)
            car_i[...] = jnp.zeros_like(car_i)

        @pl.when(first)
        def _():
            for r in (dcbr_ref, dcbi_ref, dbbr_ref, dbbi_ref, dd_ref, dar_ref, dai_ref):
                r[...] = jnp.zeros_like(r)

        steps, (pr, pi) = _power_tables(ar_ref[...], -ai_ref[...], reverse=True)
        uf = u_ref[...]
        _, gelu_vjp = jax.vjp(jax.nn.gelu, y_ref[...])
        dy = gelu_vjp(dyg_ref[...])[0]
        dyb = dy.astype(BF16)
        dd_ref[...] += _colsum(dy * uf)
        lam_r[...] = _dot(dyb, cbr_ref[...], "nn")
        lam_i[...] = -_dot(dyb, cbi_ref[...], "nn")
        dcbr_ref[...] += _dot(dyb, xr_ref[...].astype(BF16), "tn")
        dcbi_ref[...] -= _dot(dyb, xi_ref[...].astype(BF16), "tn")
        row0 = lax.broadcasted_iota(jnp.int32, (8, STATE_BLOCK), 0) == 0
        has_before = rev(t) > 0
        xrb = jnp.where(has_before, xrb_ref[...], 0.0)
        xib = jnp.where(has_before, xib_ref[...], 0.0)

        def tile(s, carry):
            cr, ci, acc_r, acc_i = carry
            i = ntile - 1 - s
            sl = pl.ds(pl.multiple_of(i * 8, 8), 8)
            gr, gi = lam_r[sl, :], lam_i[sl, :]
            for k, (sr, si) in zip((1, 2, 4), steps):
                tr, ti = _cmul(sr, si, pltpu.roll(gr, 8 - k, 0), pltpu.roll(gi, 8 - k, 0))
                gr, gi = gr + tr, gi + ti
            tr, ti = _cmul(pr, pi, cr, ci)
            gr, gi = gr + tr, gi + ti
            lam_r[sl, :] = gr
            lam_i[sl, :] = gi
            sp = pl.ds(pl.multiple_of(jnp.maximum(i - 1, 0) * 8, 8), 8)
            pvr = jnp.where(i > 0, xr_ref[sp, :], xrb)
            pvi = jnp.where(i > 0, xi_ref[sp, :], xib)
            xsr = jnp.where(row0, pltpu.roll(pvr, 1, 0), pltpu.roll(xr_ref[sl, :], 1, 0))
            xsi = jnp.where(row0, pltpu.roll(pvi, 1, 0), pltpu.roll(xi_ref[sl, :], 1, 0))
            acc_r = acc_r + xsr * gr + xsi * gi
            acc_i = acc_i + xsr * gi - xsi * gr
            return gr[0:1, :], gi[0:1, :], acc_r, acc_i

        zero = jnp.zeros((8, STATE_BLOCK), F32)
        cr, ci, acc_r, acc_i = lax.fori_loop(0, ntile, tile, (car_r[0:1, :], car_i[0:1, :], zero, zero), unroll=2)
        car_r[0:1, :] = cr
        car_i[0:1, :] = ci
        dar_ref[...] += acc_r
        dai_ref[...] += acc_i
        lrb, lib = lam_r[...].astype(BF16), lam_i[...].astype(BF16)
        du = _dot(lrb, bbr_ref[...], "nt") + _dot(lib, bbi_ref[...], "nt") + d_ref[...] * dy
        du_ref[...] = du.astype(BF16)
        ub = uf.astype(BF16)
        dbbr_ref[...] += _dot(ub, lrb, "tn")
        dbbi_ref[...] += _dot(ub, lib, "tn")

    mat_shape = jax.ShapeDtypeStruct((SSM_LANE_BLOCKS, CHAN_BLOCK, STATE_BLOCK), F32)
    return hosted_call(
        body, comm, "ssm_bwd", (SSM_LANE_BLOCKS, LOCAL_BATCH, nt),
        [chan, chan, chan, state, state, before, before, lane, lane, mat, mat, mat, mat, dsp],
        [chan, mat, mat, mat, mat, dsp, lane8, lane8],
        [jax.ShapeDtypeStruct((LOCAL_BATCH, SEQ, SSM_W), BF16), mat_shape, mat_shape, mat_shape, mat_shape,
         jax.ShapeDtypeStruct((1, SSM_W), F32), jax.ShapeDtypeStruct((8, SSM_STATE_W), F32),
         jax.ShapeDtypeStruct((8, SSM_STATE_W), F32)],
        [pltpu.VMEM((SCAN_CHUNK, STATE_BLOCK), F32), pltpu.VMEM((SCAN_CHUNK, STATE_BLOCK), F32),
         pltpu.VMEM((8, STATE_BLOCK), F32), pltpu.VMEM((8, STATE_BLOCK), F32)],
        (dyg, y, u, xr, xi, xr, xi, ab_re, ab_im, bb_re, bb_im, cb_re, cb_im, d_skip),
        ("parallel", "arbitrary", "arbitrary"))


def _merge_fn(g0, g1, attn_d, za, zb):
    return jax.nn.sigmoid(g0) * attn_d + jax.nn.sigmoid(g1) * (za * jax.nn.sigmoid(zb))


def _swiglu_fn(a, b):
    return jax.nn.silu(a) * b


def _own_slot(slots, shard):
    me = 2 * lax.axis_index("x") + lax.axis_index("y")
    mine = lax.broadcasted_iota(jnp.int32, (N_CHIPS, 1, 1), 0) == me
    return jnp.where(mine, shard[None], slots)


def _reduce_start(names, gw, shard_shapes):
    return swap_comm([_to_slots(n, gw[n], shard_shapes[n]) for n in names])


def _reduce_chip(names, slots, got, core):
    return exchange_comm([add_halves(n, g, r, core) for n, g, r in zip(names, slots, got)])


def local_step(x, target, shards, small, core):
    g_mix, g_ffn, g_final = small["norm_mix_g"], small["norm_ffn_g"], small["norm_final_g"]
    tables = _rope_tables()
    seqs = lambda t: t.reshape(LOCAL_BATCH, SEQ, t.shape[-1])
    toks = lambda t: t.reshape(TOKENS, t.shape[-1])
    shard_shapes = {n: s.shape for n, s in shards.items()}
    w = {}

    def gather(names):
        return gather_comm([shards[n] for n in names])

    def arrived(names, slots, own=None):
        for n, s in zip(names, slots):
            w[n] = _from_slots(n, s if own is None else _own_slot(s, own))

    later = [n for n in BIG if n != "w_in"]
    sems, w_in_shard, land, token = split_start(shards["w_in"].astype(BF16), "w_in_gather_start")
    zero = token[0, 0]
    h, *rest = first_norm(x, g_mix + zero, [shards[n] for n in later])
    shards = dict(shards)
    shards.update(zip(later, rest))
    br_t = small["ssm_b_re"].transpose(2, 0, 1)
    bi_t = small["ssm_b_im"].transpose(2, 0, 1)
    log_dt = small["ssm_log_dt"].reshape(32, 1)
    ab_re, ab_im, bb_re_t, bb_im_t = ssm_prep(small["ssm_a_re"] + zero, small["ssm_a_im"], log_dt, br_t, bi_t)
    ab = [ab_re.reshape(1, SSM_STATE_W), ab_im.reshape(1, SSM_STATE_W)]
    bb = [_block_diag(bb_re_t).astype(BF16), _block_diag(bb_im_t).astype(BF16)]
    cb = [_block_diag((small["ssm_c_re"] + zero).transpose(1, 0, 2)).astype(BF16),
          _block_diag((small["ssm_c_im"] + zero).transpose(1, 0, 2)).astype(BF16)]
    d_skip = small["ssm_d"].reshape(1, SSM_W)
    w_in_shard, land = split_wait(sems, w_in_shard, land, [h] + bb + cb, "w_in_gather_wait")
    arrived(["w_in"], [handover(land, "w_in_handover")], own=w_in_shard)
    w_qkv, w_u, w_gate = _qkv_order(w["w_in"][:QKV_W]), w["w_in"][QKV_W:QKV_W + SSM_W], w["w_in"][QKV_W + SSM_W:]
    qkv, *slots = proj_qkv(h, w_qkv, tables, comm=gather(["w_attn_out", "w_glu"]))
    arrived(["w_attn_out", "w_glu"], slots)
    qkv = seqs(qkv)
    u = seqs(matmul(h, w_u, "nt", F32, "proj_u"))
    gl, *slots = matmul(h, w_gate, "nt", BF16, "proj_gate", comm=gather(["w_out"]))
    arrived(["w_out"], slots)
    attn_b, attn, lse, *slots = attn_fwd(qkv, comm=gather(["w_ffn_gate"]))
    arrived(["w_ffn_gate"], slots)
    attn_b = toks(attn_b)
    y, yg, xr, xi, *slots = ssm_fwd(u, *ab, *bb, *cb, d_skip, comm=gather(["w_ffn_up"]))
    arrived(["w_ffn_up"], slots)
    yg2 = toks(yg)
    merged, x1, h2 = merge_out_proj(gl, attn_b, w["w_attn_out"], yg2, w["w_glu"], w["w_out"], x, g_ffn)
    a, b, act, *slots = ffn_in(h2, w["w_ffn_gate"], w["w_ffn_up"], comm=gather(["w_ffn_down"]))
    arrived(["w_ffn_down"], slots)

    def final_fn(xv, g, tgt):
        yv, vjp = jax.vjp(_rms, xv, g)
        err = yv - tgt
        dx, dg = vjp(err * (1.0 / D_MODEL))
        loss = 0.5 * jnp.sum(jnp.mean(err * err, axis=-1, keepdims=True), axis=0, keepdims=True)
        return dx, dx, dg, jnp.broadcast_to(loss, (1, LANES))

    dx2, dx2_b, dg_final, loss = matmul_rows(act, w["w_ffn_down"], "ffn_down_loss", final_fn, [g_final, target],
                                             [(D_MODEL, F32), (D_MODEL, BF16)], accs=(D_MODEL, LANES), add=x1)
    gw, parts = {}, {}
    gw["w_ffn_down"] = matmul(act, dx2_b, "tn", F32, "d_ffn_down")
    da_b, db_b = ffn_in_bwd(dx2_b, w["w_ffn_down"], a, b)
    gw["w_ffn_gate"] = matmul(da_b, h2, "tn", F32, "d_ffn_gate")
    gw["w_ffn_up"] = matmul(db_b, h2, "tn", F32, "d_ffn_up")
    ffn = ["w_ffn_down", "w_ffn_gate", "w_ffn_up"]
    swap = _reduce_start(ffn, gw, shard_shapes)

    def norm_bwd(dh, xv, g, skip):
        _, vjp = jax.vjp(_rms, xv, g)
        dx, dg = vjp(dh)
        dx = dx + skip
        return dx, dx, dg

    dx1, dx1_b, dg_ffn, *got = matmul_rows(
        [da_b, db_b], [w["w_ffn_gate"], w["w_ffn_up"]], "d_h2_norm", norm_bwd, [x1, g_ffn, dx2],
        [(D_MODEL, F32), (D_MODEL, BF16)], accs=(D_MODEL,), comm=swap, tm=256)
    ffn_exchange = [_reduce_chip(ffn[:2], swap.ins[:2], got[:2], core)]
    ffn_up_exchange = _reduce_chip(ffn[2:], swap.ins[2:], got[2:], core)
    gw["w_out"] = matmul(merged, dx1_b, "tn", F32, "d_out")
    dgl_b, dattn_d_b, dz_b, dattn, dyg, parts["w_ffn_up"] = merge_bwd(
        dx1_b, w["w_out"], gl, attn_b, w["w_attn_out"], yg2, w["w_glu"], comm=ffn_up_exchange)
    dattn, dyg = seqs(dattn), seqs(dyg)
    gw["w_attn_out"] = matmul(attn_b, dattn_d_b, "tn", F32, "d_attn_out", col_slots=N_CHIPS)
    gw["w_glu"] = matmul(yg2, dz_b, "tn", F32, "d_glu", col_slots=N_CHIPS)
    mixer = ["w_out", "w_attn_out", "w_glu"]
    swap = _reduce_start(mixer, gw, shard_shapes)
    du_b, dcb_re, dcb_im, dbb_re, dbb_im, dd, da_re8, da_im8, *rest = ssm_bwd(
        dyg, y, u, xr, xi, *ab, *bb, *cb, d_skip, comm=join_comms(ffn_exchange + [swap]))
    for n, p in zip(ffn[:2], rest[:2]):
        parts[n] = p
    mixer_exchange = _reduce_chip(mixer, swap.ins, rest[2:], core)
    du_b = toks(du_b)
    g_ab_re = jnp.sum(da_re8, axis=0).reshape(32, 64)
    g_ab_im = jnp.sum(da_im8, axis=0).reshape(32, 64)
    d_lr, d_li, d_ldt, d_br_t, d_bi_t = ssm_prep_bwd(
        small["ssm_a_re"], small["ssm_a_im"], log_dt, br_t, bi_t,
        g_ab_re, g_ab_im, _block_diag_t(dbb_re), _block_diag_t(dbb_im))
    as_gcn = lambda t: t.transpose(1, 0, 2).reshape(SSM_W, 64)
    gs = {
        "ssm_a_re": d_lr, "ssm_a_im": d_li, "ssm_log_dt": d_ldt.reshape(1, 32),
        "ssm_b_re": as_gcn(d_br_t), "ssm_b_im": as_gcn(d_bi_t),
        "ssm_c_re": as_gcn(_block_diag_t(dcb_re)), "ssm_c_im": as_gcn(_block_diag_t(dcb_im)),
        "ssm_d": dd.reshape(32, 16).T,
    }
    ssm_gather = small_comm([gs[n] for n in SSM_SMALL])
    dqkv_b, *rest = attn_bwd(qkv, tables, dattn, attn, lse, comm=join_comms([mixer_exchange, ssm_gather]))
    for n, p in zip(mixer, rest):
        parts[n] = p
    ssm_shares = rest[len(mixer):]
    dqkv_b = toks(dqkv_b)
    d_qkv = matmul(dqkv_b, h, "tn", F32, "d_w_qkv")
    d_u = matmul(du_b, h, "tn", F32, "d_w_u")
    d_gate = matmul(dgl_b, h, "tn", F32, "d_w_gate")
    gw["w_in"] = jnp.concatenate([_qkv_order(d_qkv, back=True), d_u, d_gate], axis=0)
    swap = _reduce_start(["w_in"], gw, shard_shapes)
    dh, *got = matmul(dqkv_b, w_qkv, "nn", F32, "d_h_qkv", comm=swap)
    chip_sum = add_halves("w_in", swap.ins[0], got[0], core)
    sems, chip_sum, land, token = split_start(chip_sum, "w_in_reduce_start", per_chip=True)
    grad_x, dg_mix = mix_in_bwd([du_b, dgl_b], [w_u, w_gate], dh, x, g_mix + token[0, 0], dx1)
    gs_norm = {"norm_mix_g": dg_mix, "norm_ffn_g": dg_ffn, "norm_final_g": dg_final}
    return loss, grad_x, parts, ssm_shares, gs_norm, (sems, chip_sum, land)


ANY = pl.BlockSpec(memory_space=pl.ANY)
BIG = ("w_in", "w_glu", "w_attn_out", "w_out", "w_ffn_gate", "w_ffn_up", "w_ffn_down")
TRANSPOSED = ("w_in", "w_ffn_gate", "w_ffn_up")
ROW_SHARDED = TRANSPOSED + ("w_out", "w_ffn_down")
SMALL = ("norm_mix_g", "ssm_a_re", "ssm_a_im", "ssm_log_dt", "ssm_b_re", "ssm_b_im", "ssm_c_re", "ssm_c_im",
         "ssm_d", "norm_ffn_g", "norm_final_g")
WEIGHTS = ("norm_mix_g", "w_in", "ssm_a_re", "ssm_a_im", "ssm_log_dt", "ssm_b_re", "ssm_b_im", "ssm_c_re",
           "ssm_c_im", "ssm_d", "w_glu", "w_attn_out", "w_out", "norm_ffn_g", "w_ffn_gate", "w_ffn_up",
           "w_ffn_down", "norm_final_g")
SSM_SMALL = SMALL[1:9]
NORM_SMALL = (SMALL[0],) + SMALL[9:]
NORM_ROWS = 32
N_BIG = len(BIG)


def _position():
    return lax.axis_index("x"), lax.axis_index("y"), lax.axis_index("c")


def _other_chips(x, y):
    return [(1 - x, y), (x, 1 - y), (1 - x, 1 - y)]


def _remote(src, dst, send_sem, recv_sem, device):
    return pltpu.make_async_remote_copy(src_ref=src, dst_ref=dst, send_sem=send_sem, recv_sem=recv_sem,
                                        device_id=device, device_id_type=MESH)


_later = functools.partial


def _two_level_phases(copies):
    def first(*refs):
        locals_, sends, _, _, _ = copies(*refs)
        for cp in locals_ + sends:
            cp().start()

    def mid(*refs):
        _, _, arrived, passed, _ = copies(*refs)
        for got, cp in zip(arrived, passed):
            got().wait_recv()
            cp().start()

    def last(*refs):
        locals_, sends, _, passed, from_sibling = copies(*refs)
        for cp in from_sibling:
            cp().wait_recv()
        for cp in sends + passed:
            cp().wait_send()
        for cp in locals_:
            cp().wait()

    return first, mid, last


def _half(ref, chip, which):
    rows = ref.shape[1] // 2
    return ref.at[chip, pl.ds(which * rows, rows), :]


class Comm:
    def __init__(self, ins, out_shapes, sems, first, mid, last):
        self.ins, self.out_shapes, self.sems = list(ins), list(out_shapes), list(sems)
        self.first, self.mid, self.last = first, mid, last


def join_comms(comms):
    def cut(refs_by_kind):
        offs, parts = [0, 0, 0], []
        for cm in comms:
            sizes = (len(cm.ins), len(cm.out_shapes), len(cm.sems))
            parts.append(tuple(refs_by_kind[k][offs[k]:offs[k] + sizes[k]] for k in range(3)))
            offs = [o + s for o, s in zip(offs, sizes)]
        return parts

    def phase(which):
        def run(ins, outs, sems):
            for cm, part in zip(comms, cut((ins, outs, sems))):
                fn = getattr(cm, which)
                if fn is not None:
                    fn(*part)
        return run

    return Comm(sum((cm.ins for cm in comms), []), sum((cm.out_shapes for cm in comms), []),
                sum((cm.sems for cm in comms), []), phase("first"), phase("mid"), phase("last"))


def _comm_operands(comm):
    if comm is None:
        return [], [], []
    return comm.ins, comm.out_shapes, comm.sems


def _comm_begin(comm, refs, step, n_steps):
    if comm is None:
        return
    pl.when(step == 0)(lambda: comm.first(*refs))
    if comm.mid is not None:
        pl.when(step == (n_steps * 3) // 4)(lambda: comm.mid(*refs))


def _comm_end(comm, refs, step, n_steps):
    if comm is not None:
        pl.when(step == n_steps - 1)(lambda: comm.last(*refs))


def _comm_refs(comm, refs, n_in, n_out):
    if comm is None:
        return list(refs), None
    ci, co, cs = len(comm.ins), len(comm.out_shapes), len(comm.sems)
    o0 = n_in + ci
    s0 = o0 + n_out + co
    host = list(refs[:n_in]) + list(refs[o0:o0 + n_out]) + list(refs[s0:len(refs) - cs])
    return host, (list(refs[n_in:o0]), list(refs[o0 + n_out:s0]), list(refs[len(refs) - cs:]))


def run_comm(comm, name):
    n_in, n_out = len(comm.ins), len(comm.out_shapes)

    def body(*refs):
        parts = (list(refs[:n_in]), list(refs[n_in:n_in + n_out]), list(refs[n_in + n_out:]))
        comm.first(*parts)
        if comm.mid is not None:
            comm.mid(*parts)
        comm.last(*parts)

    return pl.pallas_call(body, name=name, in_specs=[ANY] * n_in, out_specs=[ANY] * n_out,
                          out_shape=comm.out_shapes, scratch_shapes=comm.sems)(*comm.ins)


def hosted_call(work, comm, name, grid, in_specs, out_specs, out_shape, scratch_shapes, args, semantics):
    c_ins, c_outs, c_sems = _comm_operands(comm)
    n_steps = math.prod(grid)

    def body(*refs):
        host, c_refs = _comm_refs(comm, refs, len(in_specs), len(out_specs))
        step = 0
        for axis, size in enumerate(grid):
            step = step * size + pl.program_id(axis)
        if n_steps == 1:
            _comm_begin(comm, c_refs, step, n_steps)
        work(*host)
        if n_steps > 1:
            _comm_begin(comm, c_refs, step, n_steps)
        _comm_end(comm, c_refs, step, n_steps)

    return pl.pallas_call(
        body, name=name, grid=grid, in_specs=list(in_specs) + [ANY] * len(c_ins),
        out_specs=list(out_specs) + [ANY] * len(c_outs), out_shape=list(out_shape) + c_outs,
        scratch_shapes=list(scratch_shapes) + c_sems,
        compiler_params=_params(semantics if comm is None else ("arbitrary",) * len(grid)),
    )(*args, *c_ins)


def gather_comm(shards):
    n = len(shards)

    def copies(srcs, outs, sems):
        send_sems, recv_sems, local_sems = sems
        x, y, c = _position()
        me = 2 * x + y
        sibling = (x, y, 1 - c)
        chips = _other_chips(x, y)
        locals_ = [_later(pltpu.make_async_copy, s, o.at[me], local_sems.at[i])
                   for i, (s, o) in enumerate(zip(srcs, outs))]
        sends, arrived, passed, from_sibling = [], [], [], []
        for j, (px, py) in enumerate(chips):
            for i, (s, o) in enumerate(zip(srcs, outs)):
                rows = s.shape[0] // 2
                sends.append(_later(_remote, s.at[pl.ds(c * rows, rows), :], _half(o, me, c), send_sems.at[i, j],
                                    recv_sems.at[i, j], (px, py, c)))
                got = _half(o, 2 * px + py, c)
                arrived.append(_later(_remote, got, got, send_sems.at[i, j], recv_sems.at[i, j], (px, py, c)))
                passed.append(_later(_remote, got, got, send_sems.at[i, 3 + j], recv_sems.at[i, 3 + j], sibling))
                other = _half(o, 2 * px + py, 1 - c)
                from_sibling.append(_later(_remote, other, other, send_sems.at[i, 3 + j], recv_sems.at[i, 3 + j],
                                           sibling))
        return locals_, sends, arrived, passed, from_sibling

    return Comm(shards, [jax.ShapeDtypeStruct((N_CHIPS,) + s.shape, s.dtype) for s in shards],
                [pltpu.SemaphoreType.DMA((n, 6)), pltpu.SemaphoreType.DMA((n, 6)), pltpu.SemaphoreType.DMA((n,))],
                *_two_level_phases(copies))


HBM = pl.BlockSpec(memory_space=pltpu.HBM)
SEM = pl.BlockSpec(memory_space=pltpu.SEMAPHORE)
N_OTHER = N_CHIPS - 1


def _ici_halves(src_ref, land_ref, sems, per_chip):
    x, y, c = _position()
    me = 2 * x + y
    rows = land_ref.shape[1] // 2
    sends, arrivals = [], []
    for j, (px, py) in enumerate(_other_chips(x, y)):
        piece = src_ref.at[2 * px + py] if per_chip else src_ref.at[pl.ds(c * rows, rows), :]
        sends.append(_later(_remote, piece, _half(land_ref, me, c), sems[j], sems[N_OTHER + j], (px, py, c)))
        got = _half(land_ref, 2 * px + py, c)
        arrivals.append(_later(_remote, got, got, sems[j], sems[N_OTHER + j], (px, py, c)))
    return sends, arrivals


def split_start(src, name, per_chip=False):
    def body(src_ref, land_ref, *rest):
        sems, token = rest[:2 * N_OTHER], rest[-1]
        for cp in _ici_halves(src_ref, land_ref, sems, per_chip)[0]:
            cp().start()
        token[...] = jnp.zeros_like(token)

    rows, cols = (2 * src.shape[1], src.shape[2]) if per_chip else src.shape
    sem = pltpu.SemaphoreType.DMA(())
    land = (N_CHIPS, rows, cols)
    res = pl.pallas_call(
        body, name=name, in_specs=(HBM, HBM),
        out_specs=(SEM,) * (2 * N_OTHER) + (HBM, HBM, pl.BlockSpec(memory_space=pltpu.VMEM)),
        out_shape=(sem,) * (2 * N_OTHER) + (pltpu.HBM(src.shape, src.dtype), pltpu.HBM(land, src.dtype),
                                           jax.ShapeDtypeStruct((8, LANES), F32)),
        input_output_aliases={0: 2 * N_OTHER, 1: 2 * N_OTHER + 1},
        compiler_params=pltpu.CompilerParams(has_side_effects=pltpu.SideEffectType.DATAFLOW_SIDE_EFFECTING),
    )(pltpu.with_memory_space_constraint(src, pltpu.HBM),
      pltpu.with_memory_space_constraint(lax.empty(land, src.dtype), pltpu.HBM))
    return res[:2 * N_OTHER], res[2 * N_OTHER], res[2 * N_OTHER + 1], res[-1]


def split_wait(sems, src, land, after, name, per_chip=False):
    def body(src_ref, land_ref, *rest):
        sends, arrivals = _ici_halves(src_ref, land_ref, rest[:2 * N_OTHER], per_chip)
        for cp in sends:
            cp().wait_send()
        for cp in arrivals:
            cp().wait_recv()

    return pl.pallas_call(
        body, name=name, in_specs=(HBM, HBM) + (SEM,) * (2 * N_OTHER) + (ANY,) * len(after),
        out_specs=(HBM, HBM), out_shape=(pltpu.HBM(src.shape, src.dtype), pltpu.HBM(land.shape, land.dtype)),
        input_output_aliases={0: 0, 1: 1},
        compiler_params=pltpu.CompilerParams(has_side_effects=pltpu.SideEffectType.DATAFLOW_SIDE_EFFECTING),
    )(src, land, *sems, *after)


def handover(land, name, sums=None):
    n = N_OTHER + (sums is not None)

    def body(*refs):
        land_ref, send_sems, recv_sems = refs[0], refs[-2], refs[-1]
        x, y, c = _position()
        me = 2 * x + y
        sibling = (x, y, 1 - c)
        pieces = [(_half(land_ref, 2 * px + py, c), 2 * px + py) for px, py in _other_chips(x, y)]
        if sums is not None:
            pieces.append((refs[1].at[me], me))
        sends = [_remote(piece, _half(land_ref, chip, c), send_sems.at[j], recv_sems.at[j], sibling)
                 for j, (piece, chip) in enumerate(pieces)]
        for cp in sends:
            cp.start()
        for j, (_, chip) in enumerate(pieces):
            other = _half(land_ref, chip, 1 - c)
            _remote(other, other, send_sems.at[j], recv_sems.at[j], sibling).wait_recv()
        for cp in sends:
            cp.wait_send()

    args = (land,) + ((sums,) if sums is not None else ())
    return pl.pallas_call(
        body, name=name, in_specs=[ANY] * len(args), out_specs=ANY,
        out_shape=jax.ShapeDtypeStruct(land.shape, land.dtype), input_output_aliases={0: 0},
        scratch_shapes=[pltpu.SemaphoreType.DMA((n,)), pltpu.SemaphoreType.DMA((n,))],
    )(*args)


def swap_comm(grads):
    n = len(grads)

    def copies(srcs, gots, sems):
        send_sems, recv_sems = sems
        x, y, c = _position()
        out = []
        for i, (s, o) in enumerate(zip(srcs, gots)):
            rows = s.shape[1] // 2
            out.append(_remote(s.at[:, pl.ds((1 - c) * rows, rows), :], o, send_sems.at[i], recv_sems.at[i],
                               (x, y, 1 - c)))
        return out

    def first(srcs, gots, sems):
        for cp in copies(srcs, gots, sems):
            cp.start()

    def last(srcs, gots, sems):
        for cp in copies(srcs, gots, sems):
            cp.wait()

    return Comm(grads, [jax.ShapeDtypeStruct((N_CHIPS, g.shape[1] // 2, g.shape[2]), g.dtype) for g in grads],
                [pltpu.SemaphoreType.DMA((n,)), pltpu.SemaphoreType.DMA((n,))], first, None, last)


def add_halves(name, g, got, core):
    _, half, cols = got.shape
    mine = pl.BlockSpec((None, half, cols), lambda k, c_ref: (k, c_ref[0], 0))
    other = pl.BlockSpec((None, half, cols), lambda k, c_ref: (k, 0, 0))

    def body(c_ref, g_ref, got_ref, o_ref):
        o_ref[...] = (g_ref[...] + got_ref[...]).astype(BF16)

    return pl.pallas_call(
        body, name="add_halves_" + name,
        grid_spec=pltpu.PrefetchScalarGridSpec(num_scalar_prefetch=1, grid=(N_CHIPS,), in_specs=[mine, other],
                                               out_specs=other),
        out_shape=jax.ShapeDtypeStruct(got.shape, BF16),
        compiler_params=_params(("parallel",)),
    )(core, g, got)


def exchange_comm(parts):
    n = len(parts)

    def copies(srcs, outs, sems):
        send_sems, recv_sems, local_sems = sems
        x, y, c = _position()
        me = 2 * x + y
        sibling = (x, y, 1 - c)
        chips = _other_chips(x, y)
        locals_, sends, arrived, passed, from_sibling = [], [], [], [], []
        for i, (s, o) in enumerate(zip(srcs, outs)):
            locals_.append(_later(pltpu.make_async_copy, s.at[me], _half(o, me, c), local_sems.at[i]))
            sends.append(_later(_remote, s.at[me], _half(o, me, c), send_sems.at[i, 3], recv_sems.at[i, 3], sibling))
            other = _half(o, me, 1 - c)
            from_sibling.append(_later(_remote, other, other, send_sems.at[i, 3], recv_sems.at[i, 3], sibling))
        for j, (px, py) in enumerate(chips):
            for i, (s, o) in enumerate(zip(srcs, outs)):
                sends.append(_later(_remote, s.at[2 * px + py], _half(o, me, c), send_sems.at[i, j],
                                    recv_sems.at[i, j], (px, py, c)))
                got = _half(o, 2 * px + py, c)
                arrived.append(_later(_remote, got, got, send_sems.at[i, j], recv_sems.at[i, j], (px, py, c)))
                passed.append(_later(_remote, got, got, send_sems.at[i, 4 + j], recv_sems.at[i, 4 + j], sibling))
                other = _half(o, 2 * px + py, 1 - c)
                from_sibling.append(_later(_remote, other, other, send_sems.at[i, 4 + j], recv_sems.at[i, 4 + j],
                                           sibling))
        return locals_, sends, arrived, passed, from_sibling

    return Comm(parts, [jax.ShapeDtypeStruct((N_CHIPS, 2 * p.shape[1], p.shape[2]), p.dtype) for p in parts],
                [pltpu.SemaphoreType.DMA((n, 7)), pltpu.SemaphoreType.DMA((n, 7)), pltpu.SemaphoreType.DMA((n,))],
                *_two_level_phases(copies))


def small_comm(shares):
    n = len(shares)

    def copies(srcs, outs, sems):
        send_sems, recv_sems, local_sems = sems
        x, y, c = _position()
        me = 4 * x + 2 * y + c
        flips = [(fx, fy, fc) for fx in (0, 1) for fy in (0, 1) for fc in (0, 1)][1:]
        peers = [(1 - x if fx else x, 1 - y if fy else y, 1 - c if fc else c) for fx, fy, fc in flips]
        locals_, sends, arrived = [], [], []
        for i, (src_ref, out_ref) in enumerate(zip(srcs, outs)):
            locals_.append(_later(pltpu.make_async_copy, src_ref, out_ref.at[me], local_sems.at[i]))
            for j, (px, py, pc) in enumerate(peers):
                sends.append(_later(_remote, src_ref, out_ref.at[me], send_sems.at[i, j], recv_sems.at[i, j],
                                    (px, py, pc)))
                got = out_ref.at[4 * px + 2 * py + pc]
                arrived.append(_later(_remote, got, got, send_sems.at[i, j], recv_sems.at[i, j], (px, py, pc)))
        return locals_, sends, arrived

    def first(*refs):
        locals_, sends, _ = copies(*refs)
        for cp in locals_ + sends:
            cp().start()

    def last(*refs):
        locals_, sends, arrived = copies(*refs)
        for cp in arrived:
            cp().wait_recv()
        for cp in sends:
            cp().wait_send()
        for cp in locals_:
            cp().wait()

    return Comm(shares, [jax.ShapeDtypeStruct((N_DEV,) + s.shape, s.dtype) for s in shares],
                [pltpu.SemaphoreType.DMA((n, 7)), pltpu.SemaphoreType.DMA((n, 7)), pltpu.SemaphoreType.DMA((n,))],
                first, None, last)


def _adam_fn(w, g, m, v):
    m = ADAM_B1 * m + (1.0 - ADAM_B1) * g
    v = ADAM_B2 * v + (1.0 - ADAM_B2) * jnp.square(g)
    m_hat = m / (1.0 - ADAM_B1 ** ADAM_STEP)
    v_hat = v / (1.0 - ADAM_B2 ** ADAM_STEP)
    return -ADAM_LR * (m_hat / (jnp.sqrt(v_hat) + ADAM_EPS) + ADAM_WD * w), m, v


def adam_big(name, parts, w, m, v):
    rows, cols = w.shape
    tm = _pick(rows, 384, 16)

    def fn(p0, p1, p2, p3, wv, mv, vv):
        g = ((p0.astype(F32) + p1.astype(F32)) + p2.astype(F32)) + p3.astype(F32)
        return (g,) + _adam_fn(wv, g, mv, vv)

    return rowwise(fn, [parts, w, m, v], [(cols, F32)] * 4, "adam_" + name, tm=tm, rows=rows)


def adam_small(name, gathered, w, m, v):
    def body(g_ref, w_ref, m_ref, v_ref, go_ref, d_ref, mo_ref, vo_ref):
        g = g_ref[0]
        for k in range(1, N_DEV):
            g = g + g_ref[k]
        go_ref[...] = g
        d_ref[...], mo_ref[...], vo_ref[...] = _adam_fn(w_ref[...], g, m_ref[...], v_ref[...])

    return pl.pallas_call(body, name=name, out_shape=[jax.ShapeDtypeStruct(w.shape, F32)] * 4,
                          compiler_params=_params())(gathered, w, m, v)


def _ssm_2d(name, t):
    t = t[0] if t.ndim > 2 else t
    if name in ("ssm_b_re", "ssm_b_im"):
        return t.transpose(0, 2, 1).reshape(SSM_W, 64)
    if name in ("ssm_c_re", "ssm_c_im"):
        return t.reshape(SSM_W, 64)
    return t.T if name == "ssm_d" else t


def _ssm_back(name, t):
    if name in ("ssm_b_re", "ssm_b_im"):
        return t.reshape(32, 16, 64).transpose(0, 2, 1)[None]
    if name in ("ssm_c_re", "ssm_c_im"):
        return t.reshape(1, 32, 16, 64)
    if name == "ssm_d":
        return t.T[None]
    return t if name == "ssm_log_dt" else t[None]


def adam_ssm(shares, w, m, v):
    n = len(w)

    def body(*refs):
        ins, outs = refs[:4 * n], refs[4 * n:]
        for i in range(n):
            g_ref, w_ref, m_ref, v_ref = (ins[k * n + i] for k in range(4))
            g = g_ref[0]
            for k in range(1, N_DEV):
                g = g + g_ref[k]
            outs[4 * i][...] = g
            outs[4 * i + 1][...], outs[4 * i + 2][...], outs[4 * i + 3][...] = _adam_fn(w_ref[...], g, m_ref[...],
                                                                                      v_ref[...])

    out_shape = [jax.ShapeDtypeStruct(t.shape, F32) for t in w for _ in range(4)]
    res = pl.pallas_call(body, name="adam_ssm", out_shape=out_shape, compiler_params=_params())(*shares, *w, *m, *v)
    return [res[4 * i:4 * i + 4] for i in range(n)]


def _pack_small(names, vals, rows, last=None):
    flat = [vals[n].reshape(-1) for n in names]
    if last is not None:
        flat.append(last.reshape(-1))
    flat = jnp.concatenate(flat)
    return jnp.pad(flat, (0, rows * LANES - flat.shape[0])).reshape(rows, LANES)


def _unpack_small(names, pack, shapes):
    flat, out, off = pack.reshape(-1), {}, 0
    for n in names:
        size = math.prod(shapes[n])
        out[n] = flat[off:off + size].reshape(shapes[n])
        off += size
    return out, flat[off]


def _to_slots(name, g, shard_shape):
    rows, cols = shard_shape
    if g.shape == (N_CHIPS, rows, cols):
        return g
    if name in ROW_SHARDED:
        return g.reshape(N_CHIPS, rows, cols)
    return g.reshape(rows, N_CHIPS, cols).transpose(1, 0, 2)


def _from_slots(name, s):
    _, rows, cols = s.shape
    if name in ROW_SHARDED:
        return s.reshape(N_CHIPS * rows, cols)
    return s.transpose(1, 0, 2).reshape(rows, N_CHIPS * cols)


def kernel(x, norm_mix_g, w_in, ssm_a_re, ssm_a_im, ssm_log_dt, ssm_b_re, ssm_b_im, ssm_c_re, ssm_c_im, ssm_d, w_glu, w_attn_out, w_out, norm_ffn_g, w_ffn_gate, w_ffn_up, w_ffn_down, norm_final_g, loss_target, m_norm_mix_g, m_w_in, m_ssm_a_re, m_ssm_a_im, m_ssm_log_dt, m_ssm_b_re, m_ssm_b_im, m_ssm_c_re, m_ssm_c_im, m_ssm_d, m_w_glu, m_w_attn_out, m_w_out, m_norm_ffn_g, m_w_ffn_gate, m_w_ffn_up, m_w_ffn_down, m_norm_final_g, v_norm_mix_g, v_w_in, v_ssm_a_re, v_ssm_a_im, v_ssm_log_dt, v_ssm_b_re, v_ssm_b_im, v_ssm_c_re, v_ssm_c_im, v_ssm_d, v_w_glu, v_w_attn_out, v_w_out, v_norm_ffn_g, v_w_ffn_gate, v_w_ffn_up, v_w_ffn_down, v_norm_final_g):
    given = dict(locals())
    def local(name, prefix=""):
        t = given[prefix + name][0]
        return t.T if name in TRANSPOSED else t

    shard = {n: local(n) for n in BIG}
    shapes = {n: given[n].shape for n in WEIGHTS}

    small = {n: given[n] for n in SMALL}
    small_2d = dict(small)
    for n in ("ssm_a_re", "ssm_a_im", "ssm_b_re", "ssm_b_im", "ssm_c_re", "ssm_c_im", "ssm_d"):
        small_2d[n] = small[n][0]
    small_2d["norm_final_g"] = norm_final_g.reshape(1, D_MODEL)

    core = lax.axis_index("c").astype(jnp.int32).reshape(1)
    loss, grad_x, parts, ssm_shares, gs_norm, w_in_reduce = local_step(
        x.reshape(TOKENS, D_MODEL), loss_target.reshape(TOKENS, D_MODEL),
        {n: shard[n] for n in BIG}, small_2d, core)

    (norm_shares,) = run_comm(small_comm([_pack_small(NORM_SMALL, gs_norm, NORM_ROWS, last=loss)]),
                              "gather_norm_grads")
    small_out = [{} for _ in range(4)]
    packs = [_pack_small(NORM_SMALL, {n: given[p + n] for n in NORM_SMALL}, NORM_ROWS) for p in ("", "m_", "v_")]
    for kind, t in enumerate(adam_small("adam_norm_gains", norm_shares, *packs)):
        vals, after = _unpack_small(NORM_SMALL, t, shapes)
        small_out[kind].update(vals)
        if kind == 0:
            total_loss = after
    ssm_in = [[_ssm_2d(n, given[p + n]) for n in SSM_SMALL] for p in ("", "m_", "v_")]
    for n, res in zip(SSM_SMALL, adam_ssm(ssm_shares, *ssm_in)):
        for kind, t in enumerate(res):
            small_out[kind][n] = _ssm_back(n, t)

    big_out, updated = {}, {}
    for n in BIG[1:] + BIG[:1]:
        if n == "w_in":
            sems, chip_sum, land = w_in_reduce
            behind = [updated[k][1] for k in BIG[1:]] + [norm_shares]
            chip_sum, land = split_wait(sems, chip_sum, land, behind, "w_in_reduce_wait", per_chip=True)
            land = handover(land, "w_in_reduce_handover", sums=chip_sum)
            me = 2 * lax.axis_index("x") + lax.axis_index("y")
            parts[n] = lax.dynamic_update_slice(land, lax.dynamic_slice_in_dim(chip_sum, me, 1, 0),
                                                (me, lax.axis_index("c") * chip_sum.shape[1], 0))
        updated[n] = adam_big(n, parts[n], shard[n], local(n, "m_"), local(n, "v_"))
        big_out[n] = [(t.T if n in TRANSPOSED else t)[None] for t in updated[n]]

    outs = [total_loss, grad_x.reshape(LOCAL_BATCH, SEQ, D_MODEL)]
    for kind in range(4):
        for n in WEIGHTS:
            outs.append(big_out[n][kind] if n in BIG else small_out[kind][n])
    return tuple(outs)
```

```python
import functools
import math

import jax
import jax.numpy as jnp
import numpy as np
from jax import lax
from jax.experimental import pallas as pl
from jax.experimental.pallas import tpu as pltpu

F32 = jnp.float32
BF16 = jnp.bfloat16
MESH = pl.DeviceIdType.MESH

D_MODEL = 1024
SEQ = 2048
LOCAL_BATCH = 2
TOKENS = LOCAL_BATCH * SEQ
HEAD_DIM = 64
HEADS_PER_GROUP = 4
GROUP_W = HEADS_PER_GROUP * HEAD_DIM
N_GROUPS = 3
DILATIONS = (1, 4, 16)
ATTN_BLOCK = 128
ROPE_DIM = 16
ROPE_THETA = 500000.0
QKV_W = 3 * N_GROUPS * GROUP_W
SSM_W = 512
SSM_STATE_W = 2048
SSM_LANE_BLOCKS = 4
GATE_W = 2 * D_MODEL
D_FF = 2816
RMS_EPS = 1e-6
NEG_INF = -1e30
ADAM_LR, ADAM_B1, ADAM_B2, ADAM_EPS, ADAM_WD, ADAM_STEP = 0.001, 0.9, 0.999, 1e-08, 0.01, 10
N_CHIPS = 4
N_DEV = 8

VMEM_LIMIT = 56 * 1024 * 1024
LANES = 128


def _params(sem=None):
    return pltpu.CompilerParams(dimension_semantics=sem, vmem_limit_bytes=VMEM_LIMIT)


def _pick(n, cap, align=LANES):
    best = None
    for d in range(align, min(n, cap) + 1, align):
        if n % d == 0:
            best = d
    return n if best is None or n <= cap else best


_DIMS = {"nn": (((1,), (0,)), ((), ())), "nt": (((1,), (1,)), ((), ())), "tn": (((0,), (0,)), ((), ()))}


def _dot(a, b, mode):
    return lax.dot_general(a, b, _DIMS[mode], preferred_element_type=F32)


def matmul(a, b, mode, out_dtype, name, add=None, comm=None, col_slots=1):
    if mode == "nn":
        (m, k), n = a.shape, b.shape[1]
    elif mode == "nt":
        (m, k), n = a.shape, b.shape[0]
    else:
        (k, m), n = a.shape, b.shape[1]
    assert n % col_slots == 0 and (col_slots == 1 or add is None)
    tn = _pick(n // col_slots, 1408 if mode != "tn" else 512)
    tk = _pick(k, 2816) if mode != "tn" else k
    tm = _pick(m, 1408)
    out_bytes = jnp.dtype(out_dtype).itemsize

    def need(tm_):
        return 2 * 2 * (tm_ * tk + tk * tn) + tm_ * tn * (4 + 2 * out_bytes + (8 if add is not None else 0))

    while need(tm) > 40 * 1024 * 1024 and tm % 256 == 0:
        tm //= 2
    nk = k // tk
    a_spec = {"nn": pl.BlockSpec((tm, tk), lambda i, j, kk: (i, kk)),
              "nt": pl.BlockSpec((tm, tk), lambda i, j, kk: (i, kk)),
              "tn": pl.BlockSpec((tk, tm), lambda i, j, kk: (kk, i))}[mode]
    b_spec = {"nn": pl.BlockSpec((tk, tn), lambda i, j, kk: (kk, j)),
              "nt": pl.BlockSpec((tn, tk), lambda i, j, kk: (j, kk)),
              "tn": pl.BlockSpec((tk, tn), lambda i, j, kk: (kk, j))}[mode]
    o_spec, o_shape = pl.BlockSpec((tm, tn), lambda i, j, kk: (i, j)), (m, n)
    if col_slots > 1:
        per_slot = n // col_slots // tn
        o_spec = pl.BlockSpec((None, tm, tn), lambda i, j, kk: (lax.div(j, per_slot), i, lax.rem(j, per_slot)))
        o_shape = (col_slots, m, n // col_slots)

    def body(a_ref, b_ref, *rest):
        if add is not None:
            add_ref, o_ref, acc_ref = rest
        else:
            o_ref, acc_ref = rest
        part = _dot(a_ref[...], b_ref[...], mode)
        if nk == 1:
            res = part if add is None else part + add_ref[...]
            o_ref[...] = res.astype(out_dtype)
            return
        kk = pl.program_id(2)

        @pl.when(kk == 0)
        def _():
            acc_ref[...] = part

        @pl.when(kk > 0)
        def _():
            acc_ref[...] += part

        @pl.when(kk == nk - 1)
        def _():
            res = acc_ref[...] if add is None else acc_ref[...] + add_ref[...]
            o_ref[...] = res.astype(out_dtype)

    in_specs = [a_spec, b_spec] + ([o_spec] if add is not None else [])
    args = (a, b) + ((add,) if add is not None else ())
    res = hosted_call(
        body, comm, name, (m // tm, n // tn, nk), in_specs, [o_spec], [jax.ShapeDtypeStruct(o_shape, out_dtype)],
        [pltpu.VMEM((tm, tn) if nk > 1 else (8, LANES), F32)], args, ("parallel", "parallel", "arbitrary"))
    return res[0] if comm is None else res


def matmul_rows(a, b, name, fn, extra, outs, accs=(), add=None, comm=None, tm=512):
    a_list, b_list = (list(a), list(b)) if isinstance(a, (list, tuple)) else ([a], [b])
    m, n = a_list[0].shape[0], b_list[0].shape[1]
    n_mm = len(a_list)
    n_fixed = 2 * n_mm + (add is not None)
    row_spec = lambda cols: pl.BlockSpec((tm, cols), lambda i: (i, 0))
    in_specs = [row_spec(t.shape[1]) for t in a_list] + [pl.BlockSpec(t.shape, lambda i: (0, 0)) for t in b_list]
    in_specs += [row_spec(n)] if add is not None else []
    in_specs += [pl.BlockSpec(e.shape, lambda i: (0, 0)) if e.shape[0] == 1 else row_spec(e.shape[1]) for e in extra]
    out_specs = [row_spec(c) for c, _ in outs] + [pl.BlockSpec((1, c), lambda i: (0, 0)) for c in accs]
    out_shape = [jax.ShapeDtypeStruct((m, c), dt) for c, dt in outs] + [jax.ShapeDtypeStruct((1, c), F32) for c in accs]

    def body(*refs):
        rows = _dot(refs[0][...], refs[n_mm][...], "nn")
        for i in range(1, n_mm):
            rows = rows + _dot(refs[i][...], refs[n_mm + i][...], "nn")
        if add is not None:
            rows = rows + refs[2 * n_mm][...]
        n_in = n_fixed + len(extra)
        res = fn(rows, *[r[...] for r in refs[n_fixed:n_in]])
        for r, v in zip(refs[n_in:n_in + len(outs)], res[:len(outs)]):
            r[...] = v.astype(r.dtype)
        first = pl.program_id(0) == 0
        for r, v in zip(refs[n_in + len(outs):], res[len(outs):]):
            @pl.when(first)
            def _(r=r, v=v):
                r[...] = v

            @pl.when(jnp.logical_not(first))
            def _(r=r, v=v):
                r[...] += v

    args = tuple(a_list) + tuple(b_list) + ((add,) if add is not None else ()) + tuple(extra)
    return hosted_call(body, comm, name, (m // tm,), in_specs, out_specs, out_shape, [], args, ("arbitrary",))


def _merge_specs(tm):
    half = lambda blk: pl.BlockSpec((tm, D_MODEL), functools.partial(lambda i, blk_: (i, blk_), blk_=blk))
    return [half(0), half(1), pl.BlockSpec((tm, GROUP_W), lambda i: (i, 0)),
            pl.BlockSpec((GROUP_W, D_MODEL), lambda i: (0, 0)), pl.BlockSpec((tm, SSM_W), lambda i: (i, 0)),
            pl.BlockSpec((SSM_W, GATE_W), lambda i: (0, 0))]


def _merge_operands(g0, g1, at, wa, yg, wg):
    z = _dot(yg[...], wg[...], "nn")
    return (g0[...].astype(F32), g1[...].astype(F32), _dot(at[...], wa[...], "nn"), z[:, :D_MODEL], z[:, D_MODEL:])


def merge_out_proj(gl, attn_b, w_attn_out, yg, w_glu, w_out, x, g_ffn):
    tm = 512

    def body(g0, g1, at, wa, yg_ref, wg, w_ref, x_ref, g_ref, m_ref, x1_ref, h2_ref):
        merged = _merge_fn(*_merge_operands(g0, g1, at, wa, yg_ref, wg)).astype(BF16)
        m_ref[...] = merged
        x1 = _dot(merged, w_ref[...], "nn") + x_ref[...]
        x1_ref[...] = x1
        h2_ref[...] = _rms(x1, g_ref[...]).astype(BF16)

    rows = pl.BlockSpec((tm, D_MODEL), lambda i: (i, 0))
    whole = pl.BlockSpec((D_MODEL, D_MODEL), lambda i: (0, 0))
    gain = pl.BlockSpec((1, D_MODEL), lambda i: (0, 0))
    tok = lambda dt: jax.ShapeDtypeStruct((TOKENS, D_MODEL), dt)
    return pl.pallas_call(
        body, name="merge_out_proj", grid=(TOKENS // tm,), in_specs=_merge_specs(tm) + [whole, rows, gain],
        out_specs=[rows] * 3, out_shape=[tok(BF16), tok(F32), tok(BF16)], compiler_params=_params(("parallel",)),
    )(gl, gl, attn_b, w_attn_out, yg, w_glu, w_out, x, g_ffn)


def merge_bwd(dx1_b, w_out, gl, attn_b, w_attn_out, yg, w_glu, comm=None):
    tm = 512

    def body(dx_ref, w_ref, g0, g1, at, wa, yg_ref, wg, dgl_ref, dad_ref, dz_ref, dat_ref, dyg_ref):
        dm = _dot(dx_ref[...], w_ref[...], "nt")
        _, vjp = jax.vjp(_merge_fn, *_merge_operands(g0, g1, at, wa, yg_ref, wg))
        dg0, dg1, dad, dza, dzb = vjp(dm)
        dat_ref[...] = _dot(dad.astype(BF16), wa[...], "nt")
        dgl_ref[:, :D_MODEL] = dg0.astype(BF16)
        dgl_ref[:, D_MODEL:] = dg1.astype(BF16)
        dad_ref[...] = dad.astype(BF16)
        dz_ref[:, :D_MODEL] = dza.astype(BF16)
        dz_ref[:, D_MODEL:] = dzb.astype(BF16)
        dyg_ref[...] = _dot(dz_ref[...], wg[...], "nt")

    rows = pl.BlockSpec((tm, D_MODEL), lambda i: (i, 0))
    wide = pl.BlockSpec((tm, GATE_W), lambda i: (i, 0))
    whole = pl.BlockSpec((D_MODEL, D_MODEL), lambda i: (0, 0))
    return hosted_call(
        body, comm, "merge_bwd", (TOKENS // tm,), [rows, whole] + _merge_specs(tm),
        [wide, rows, wide, pl.BlockSpec((tm, GROUP_W), lambda i: (i, 0)), pl.BlockSpec((tm, SSM_W), lambda i: (i, 0))],
        [jax.ShapeDtypeStruct((TOKENS, GATE_W), BF16), jax.ShapeDtypeStruct((TOKENS, D_MODEL), BF16),
         jax.ShapeDtypeStruct((TOKENS, GATE_W), BF16), jax.ShapeDtypeStruct((TOKENS, GROUP_W), F32),
         jax.ShapeDtypeStruct((TOKENS, SSM_W), F32)], [],
        (dx1_b, w_out, gl, gl, attn_b, w_attn_out, yg, w_glu), ("arbitrary",))


FFN_TM, FFN_TN = 512, 1408


def ffn_in(h2, wg_t, wu_t, comm=None):
    def body(h_ref, wg_ref, wu_ref, a_ref, b_ref, act_ref):
        hv = h_ref[...]
        a, b = _dot(hv, wg_ref[...], "nt"), _dot(hv, wu_ref[...], "nt")
        a_ref[...] = a.astype(BF16)
        b_ref[...] = b.astype(BF16)
        act_ref[...] = _swiglu_fn(a, b).astype(BF16)

    rows = pl.BlockSpec((FFN_TM, D_MODEL), lambda i, j: (i, 0))
    wts = pl.BlockSpec((FFN_TN, D_MODEL), lambda i, j: (j, 0))
    out = pl.BlockSpec((FFN_TM, FFN_TN), lambda i, j: (i, j))
    return hosted_call(body, comm, "ffn_in", (TOKENS // FFN_TM, D_FF // FFN_TN), [rows, wts, wts], [out] * 3,
                       [jax.ShapeDtypeStruct((TOKENS, D_FF), BF16)] * 3, [], (h2, wg_t, wu_t),
                       ("parallel", "parallel"))


def ffn_in_bwd(dx2_b, wd, a, b):
    def body(dx_ref, wd_ref, a_ref, b_ref, da_ref, db_ref):
        dact = _dot(dx_ref[...], wd_ref[...], "nt")
        av, bv = a_ref[...].astype(F32), b_ref[...].astype(F32)
        sig = jax.nn.sigmoid(av)
        act = av * sig
        da_ref[...] = (dact * bv * (sig * (1.0 + av - act))).astype(BF16)
        db_ref[...] = (dact * act).astype(BF16)

    rows = pl.BlockSpec((FFN_TM, D_MODEL), lambda i, j: (i, 0))
    wts = pl.BlockSpec((FFN_TN, D_MODEL), lambda i, j: (j, 0))
    out = pl.BlockSpec((FFN_TM, FFN_TN), lambda i, j: (i, j))
    return pl.pallas_call(
        body, name="ffn_in_bwd", grid=(TOKENS // FFN_TM, D_FF // FFN_TN), in_specs=[rows, wts, out, out],
        out_specs=[out] * 2, out_shape=[jax.ShapeDtypeStruct((TOKENS, D_FF), BF16)] * 2,
        compiler_params=_params(("parallel", "parallel")),
    )(dx2_b, wd, a, b)


def mix_in_bwd(grads, weights, partial, x, g, skip, comm=None):
    n = len(grads)
    tm = 512

    def body(*refs):
        a_refs, b_refs = refs[:n], refs[n:2 * n]
        part_ref, x_ref, g_ref, skip_ref, gx_ref, dg_ref = refs[2 * n:]
        dh = part_ref[...]
        for a_ref, b_ref in zip(a_refs, b_refs):
            dh = dh + _dot(a_ref[...], b_ref[...], "nn")
        _, vjp = jax.vjp(_rms, x_ref[...], g_ref[...])
        dx, dg = vjp(dh)
        gx_ref[...] = dx + skip_ref[...]
        first = pl.program_id(0) == 0

        @pl.when(first)
        def _():
            dg_ref[...] = dg

        @pl.when(jnp.logical_not(first))
        def _():
            dg_ref[...] += dg

    rows = pl.BlockSpec((tm, D_MODEL), lambda i: (i, 0))
    gain = pl.BlockSpec((1, D_MODEL), lambda i: (0, 0))
    in_specs = [pl.BlockSpec((tm, a.shape[1]), lambda i: (i, 0)) for a in grads]
    in_specs += [pl.BlockSpec(b.shape, lambda i: (0, 0)) for b in weights]
    return hosted_call(
        body, comm, "mix_in_bwd", (TOKENS // tm,), in_specs + [rows, rows, gain, rows], [rows, gain],
        [jax.ShapeDtypeStruct((TOKENS, D_MODEL), F32), jax.ShapeDtypeStruct((1, D_MODEL), F32)], [],
        (*grads, *weights, partial, x, g, skip), ("arbitrary",))


def rowwise(fn, ins, outs, name, accs=(), tm=256, rows=TOKENS, comm=None):
    in_specs, args = [], []
    for item in ins:
        arr, width, blk = item if isinstance(item, tuple) else (item, None, 0)
        if arr.ndim == 3:
            for k in range(arr.shape[0]):
                in_specs.append(pl.BlockSpec((None, tm, arr.shape[2]), functools.partial(lambda i, k_: (k_, i, 0), k_=k)))
                args.append(arr)
            continue
        if arr.shape[0] == 1:
            in_specs.append(pl.BlockSpec(arr.shape, lambda i: (0, 0)))
        elif width is None:
            in_specs.append(pl.BlockSpec((tm, arr.shape[1]), lambda i: (i, 0)))
        else:
            in_specs.append(pl.BlockSpec((tm, width), functools.partial(lambda i, blk_: (i, blk_), blk_=blk)))
        args.append(arr)
    out_specs = [pl.BlockSpec((tm, c), lambda i: (i, 0)) for c, _ in outs]
    out_specs += [pl.BlockSpec((1, c), lambda i: (0, 0)) for c in accs]
    out_shape = [jax.ShapeDtypeStruct((rows, c), dt) for c, dt in outs]
    out_shape += [jax.ShapeDtypeStruct((1, c), F32) for c in accs]
    n_in, n_out = len(args), len(outs)
    c_ins, c_outs, c_sems = _comm_operands(comm)

    def body(*refs):
        refs, c_refs = _comm_refs(comm, refs, n_in, n_out + len(accs))
        step = pl.program_id(0)
        _comm_begin(comm, c_refs, step, rows // tm)
        res = fn(*[r[...] for r in refs[:n_in]])
        for r, v in zip(refs[n_in:n_in + n_out], res[:n_out]):
            r[...] = v.astype(r.dtype)
        first = step == 0
        for r, v in zip(refs[n_in + n_out:], res[n_out:]):
            @pl.when(first)
            def _(r=r, v=v):
                r[...] = v

            @pl.when(jnp.logical_not(first))
            def _(r=r, v=v):
                r[...] += v
        _comm_end(comm, c_refs, step, rows // tm)

    return pl.pallas_call(
        body, name=name, grid=(rows // tm,), in_specs=in_specs + [ANY] * len(c_ins),
        out_specs=out_specs + [ANY] * len(c_outs), out_shape=out_shape + c_outs, scratch_shapes=c_sems,
        compiler_params=_params(("arbitrary",)),
    )(*args, *c_ins)


def first_norm(x, g, others, comm=None):
    tm, n = 256, len(others)

    def body(x_ref, g_ref, *rest):
        srcs, h_ref, dsts = rest[:n], rest[n], rest[n + 1:]
        h_ref[...] = _rms(x_ref[...], g_ref[...]).astype(BF16)
        for k, (s, d) in enumerate(zip(srcs, dsts)):
            @pl.when(pl.program_id(0) == k)
            def _(s=s, d=d):
                d[...] = s[...].astype(BF16)

    rows = pl.BlockSpec((tm, D_MODEL), lambda i: (i, 0))
    whole = [pl.BlockSpec(a.shape, lambda i: (0, 0)) for a in others]
    return hosted_call(
        body, comm, "norm_mix", (TOKENS // tm,), [rows, pl.BlockSpec((1, D_MODEL), lambda i: (0, 0))] + whole,
        [rows] + whole, [jax.ShapeDtypeStruct((TOKENS, D_MODEL), BF16)]
        + [jax.ShapeDtypeStruct(a.shape, BF16) for a in others], [], (x, g, *others), ("arbitrary",))


def _rms(x, g):
    return x * lax.rsqrt(jnp.mean(x * x, axis=-1, keepdims=True) + RMS_EPS) * g


def _colsum(v):
    return jnp.sum(v, axis=0, keepdims=True)


PAIR_W = 2 * HEAD_DIM
N_PAIRS = HEADS_PER_GROUP // 2


def _qkv_order(w_t, back=False):
    dims = (N_PAIRS, N_GROUPS, 3) if back else (3, N_GROUPS, N_PAIRS)
    return w_t.reshape(dims + (PAIR_W, w_t.shape[1])).transpose(2, 1, 0, 3, 4).reshape(QKV_W, w_t.shape[1])


def _rope_tables():
    half = ROPE_DIM // 2
    inv = np.power(np.float32(ROPE_THETA), -np.arange(half, dtype=np.float32) * np.float32(2.0 / ROPE_DIM))
    ang = (np.arange(SEQ, dtype=np.float32)[:, None] * inv[None, :]).astype(np.float32)
    cos, sin = np.cos(ang), np.sin(ang)
    zeros = np.zeros((SEQ, HEAD_DIM - ROPE_DIM), np.float32)
    zh = np.zeros((SEQ, half), np.float32)
    c = np.concatenate([cos, cos, zeros + 1.0], axis=1)
    sa = np.concatenate([-sin, zh, zeros], axis=1)
    sb = np.concatenate([zh, sin, zeros], axis=1)
    return [jnp.asarray(np.tile(t, (1, 2)), F32) for t in (c, sa, sb)]


def _rope_fwd(x, c, sa, sb):
    return x * c + pltpu.roll(x, PAIR_W - 8, 1) * sa + pltpu.roll(x, 8, 1) * sb


def _rope_bwd(dy, c, sa, sb):
    return dy * c + pltpu.roll(dy * sb, PAIR_W - 8, 1) + pltpu.roll(dy * sa, 8, 1)


def _band_masks():
    row = lax.broadcasted_iota(jnp.int32, (ATTN_BLOCK, ATTN_BLOCK), 0)
    col = lax.broadcasted_iota(jnp.int32, (ATTN_BLOCK, ATTN_BLOCK), 1)
    return col <= row, col >= row


def _stack_rows(t):
    return jnp.concatenate([t, t], axis=0)


def _stack_heads(t, first_head):
    return jnp.concatenate([jnp.where(first_head, t, 0), jnp.where(first_head, 0, t)], axis=0)


def _per_head(fn):
    return jnp.concatenate([fn(slice(h * HEAD_DIM, (h + 1) * HEAD_DIM)) for h in range(2)], axis=1)


def _slab_spec(kind):
    return pl.BlockSpec((None, SEQ, PAIR_W), lambda b, p, g: (b, 0, p * 3 * N_GROUPS + g * 3 + kind))


_TABLE_SPEC = pl.BlockSpec((SEQ, PAIR_W), lambda b, p, g: (0, 0))
_PAIR_SPEC = pl.BlockSpec((None, SEQ, PAIR_W), lambda b, p, g: (b, 0, p))


def _block_rows(dil, r, n):
    return pl.ds(n * (ATTN_BLOCK * dil) + r, ATTN_BLOCK, stride=dil)


def proj_qkv(h, w_qkv_t, tables, comm=None):
    tm = 1024
    pair_w = QKV_W // N_PAIRS
    scale = HEAD_DIM ** -0.5

    def body(h_ref, w_ref, c_ref, sa_ref, sb_ref, o_ref):
        rows = _dot(h_ref[...], w_ref[...], "nt")
        c, sa, sb = c_ref[...], sa_ref[...], sb_ref[...]
        for blk in range(pair_w // PAIR_W):
            cols = slice(blk * PAIR_W, (blk + 1) * PAIR_W)
            x = rows[:, cols]
            if blk % 3 == 0:
                x = _rope_fwd(x, c, sa, sb) * scale
            elif blk % 3 == 1:
                x = _rope_fwd(x, c, sa, sb)
            o_ref[:, cols] = x

    table = pl.BlockSpec((tm, PAIR_W), lambda i, j, : (i % (SEQ // tm), 0))
    res = hosted_call(
        body, comm, "proj_qkv", (TOKENS // tm, N_PAIRS),
        [pl.BlockSpec((tm, D_MODEL), lambda i, j: (i, 0)), pl.BlockSpec((pair_w, D_MODEL), lambda i, j: (j, 0)),
         table, table, table],
        [pl.BlockSpec((tm, pair_w), lambda i, j: (i, j))], [jax.ShapeDtypeStruct((TOKENS, QKV_W), F32)], [],
        (h, w_qkv_t, *tables), ("parallel", "parallel"))
    return res[0] if comm is None else res


def attn_fwd(qkv, comm=None):
    def body(qs, ks, v_ref, attn_b_ref, attn_ref, lse_ref, o0, o1, o2, l0, l1, l2):
        g = pl.program_id(2)
        cur_mask, prev_mask = _band_masks()
        first_head = lax.broadcasted_iota(jnp.int32, (ATTN_BLOCK, PAIR_W), 1) < HEAD_DIM

        def run(dil, o_slab, l_slab):
            nb = SEQ // dil // ATTN_BLOCK

            def block(idx, carry):
                r, n = lax.div(idx, nb), lax.rem(idx, nb)
                cur, prev = _block_rows(dil, r, n), _block_rows(dil, r, jnp.maximum(n - 1, 0))
                q = qs[cur, :].astype(BF16)
                kc, kp = ks[cur, :].astype(BF16), ks[prev, :].astype(BF16)
                vc, vp = v_ref[cur, :].astype(BF16), v_ref[prev, :].astype(BF16)
                q2 = _stack_heads(q, first_head)
                mask = _stack_rows(jnp.concatenate([jnp.logical_and(prev_mask, n > 0), cur_mask], axis=1))
                s2 = jnp.where(mask, _dot(q2, jnp.concatenate([kp, kc], axis=0), "nt"), NEG_INF)
                m = jnp.max(s2, axis=-1, keepdims=True)
                vcat, two = jnp.concatenate([vp, vc], axis=0), _stack_rows(first_head)
                vext = jnp.concatenate([jnp.where(two, vcat, 1), jnp.where(two, 1, vcat)], axis=1)
                r2 = _dot(jnp.exp(s2 - m).astype(BF16), vext, "nn")
                r0, r1 = r2[:ATTN_BLOCK, :PAIR_W], r2[ATTN_BLOCK:, PAIR_W:]
                num = jnp.where(first_head, r0, r1)
                den = pltpu.roll(jnp.where(first_head, r1, r0), HEAD_DIM, 1)
                o_slab[cur, :] = num / den
                l_slab[cur, :] = jnp.where(first_head, m[:ATTN_BLOCK], m[ATTN_BLOCK:]) + jnp.log(den)
                return carry

            lax.fori_loop(0, SEQ // ATTN_BLOCK, block, 0, unroll=4)

        for gi, (o_slab, l_slab) in enumerate(((o0, l0), (o1, l1), (o2, l2))):
            @pl.when(g == gi)
            def _(gi=gi, o_slab=o_slab, l_slab=l_slab):
                run(DILATIONS[gi], o_slab, l_slab)

        @pl.when(g == N_GROUPS - 1)
        def _():
            a, b, cc = l0[...], l1[...], l2[...]
            m = jnp.maximum(jnp.maximum(a, b), cc)
            e0, e1, e2 = jnp.exp(a - m), jnp.exp(b - m), jnp.exp(cc - m)
            tot = e0 + e1 + e2
            attn = (e0 * o0[...] + e1 * o1[...] + e2 * o2[...]) / tot
            attn_ref[...] = attn
            attn_b_ref[...] = attn.astype(BF16)
            lse_ref[...] = m + jnp.log(tot)

    shape = (LOCAL_BATCH, SEQ, GROUP_W)
    slab = pltpu.VMEM((SEQ, PAIR_W), F32)
    return hosted_call(
        body, comm, "attn_fwd", (LOCAL_BATCH, N_PAIRS, N_GROUPS),
        [_slab_spec(0), _slab_spec(1), _slab_spec(2)], [_PAIR_SPEC] * 3,
        [jax.ShapeDtypeStruct(shape, BF16), jax.ShapeDtypeStruct(shape, F32), jax.ShapeDtypeStruct(shape, F32)],
        [slab] * 6, (qkv, qkv, qkv), ("parallel", "parallel", "arbitrary"))


def attn_bwd(qkv, tables, dattn, attn, lse, comm=None):
    scale = HEAD_DIM ** -0.5

    def body(qs, ks, v_ref, c_ref, sa_ref, sb_ref, do_ref, out_ref, lse_ref, dqkv_ref, dl, dq_s, dk_s, dv_s):
        g = pl.program_id(2)
        c, sa, sb = c_ref[...], sa_ref[...], sb_ref[...]

        @pl.when(g == 0)
        def _():
            prod = do_ref[...] * out_ref[...]
            dl[...] = _per_head(
                lambda sl: jnp.broadcast_to(jnp.sum(prod[:, sl], axis=-1, keepdims=True), (SEQ, HEAD_DIM)))

        cur_mask, prev_mask = _band_masks()
        first_head = lax.broadcasted_iota(jnp.int32, (ATTN_BLOCK, PAIR_W), 1) < HEAD_DIM

        def run(dil):
            nb = SEQ // dil // ATTN_BLOCK

            def block(idx, carry):
                r, n = lax.div(idx, nb), lax.rem(idx, nb)
                cur = _block_rows(dil, r, n)
                prev = _block_rows(dil, r, jnp.maximum(n - 1, 0))
                nxt = _block_rows(dil, r, jnp.minimum(n + 1, nb - 1))
                q0, q1 = qs[cur, :].astype(BF16), qs[nxt, :].astype(BF16)
                kp, kc = ks[prev, :].astype(BF16), ks[cur, :].astype(BF16)
                vp, vc = v_ref[prev, :].astype(BF16), v_ref[cur, :].astype(BF16)
                do0, do1 = do_ref[cur, :].astype(BF16), do_ref[nxt, :].astype(BF16)
                lse0, lse1, dl0, dl1 = lse_ref[cur, :], lse_ref[nxt, :], dl[cur, :], dl[nxt, :]
                has_prev = jnp.logical_and(prev_mask, n > 0)
                has_next = jnp.logical_and(prev_mask, n < nb - 1)

                def per_row(t):
                    return jnp.concatenate([t[:, 0:1], t[:, HEAD_DIM:HEAD_DIM + 1]], axis=0)

                q20, q21 = _stack_heads(q0, first_head), _stack_heads(q1, first_head)
                do20, do21 = _stack_heads(do0, first_head), _stack_heads(do1, first_head)
                kcat, vcat = jnp.concatenate([kp, kc], axis=0), jnp.concatenate([vp, vc], axis=0)
                mask0 = _stack_rows(jnp.concatenate([has_prev, cur_mask], axis=1))
                p0 = jnp.where(mask0, jnp.exp(_dot(q20, kcat, "nt") - per_row(lse0)), 0.0)
                ds0 = (p0 * (_dot(do20, vcat, "nt") - per_row(dl0))).astype(BF16)
                p1 = jnp.where(_stack_rows(has_next), jnp.exp(_dot(q21, kc, "nt") - per_row(lse1)), 0.0)
                ds1 = (p1 * (_dot(do21, vc, "nt") - per_row(dl1))).astype(BF16)
                dq2 = _dot(ds0, kcat, "nn")
                dq_s[cur, :] = jnp.where(first_head, dq2[:ATTN_BLOCK], dq2[ATTN_BLOCK:])
                ds_cur = jnp.concatenate([ds0[:, ATTN_BLOCK:], ds1], axis=0)
                p_cur = jnp.concatenate([p0[:, ATTN_BLOCK:], p1], axis=0).astype(BF16)
                dk_s[cur, :] = _dot(ds_cur, jnp.concatenate([q20, q21], axis=0), "tn")
                dv_s[cur, :] = _dot(p_cur, jnp.concatenate([do20, do21], axis=0), "tn")
                return carry

            lax.fori_loop(0, SEQ // ATTN_BLOCK, block, 0, unroll=2)

        for gi in range(N_GROUPS):
            @pl.when(g == gi)
            def _(gi=gi):
                run(DILATIONS[gi])

        dqkv_ref[:, 0:PAIR_W] = _rope_bwd(dq_s[...] * scale, c, sa, sb).astype(BF16)
        dqkv_ref[:, PAIR_W:2 * PAIR_W] = _rope_bwd(dk_s[...], c, sa, sb).astype(BF16)
        dqkv_ref[:, 2 * PAIR_W:] = dv_s[...].astype(BF16)

    slab = pltpu.VMEM((SEQ, PAIR_W), F32)
    return hosted_call(
        body, comm, "attn_bwd", (LOCAL_BATCH, N_PAIRS, N_GROUPS),
        [_slab_spec(0), _slab_spec(1), _slab_spec(2), _TABLE_SPEC, _TABLE_SPEC, _TABLE_SPEC,
         _PAIR_SPEC, _PAIR_SPEC, _PAIR_SPEC],
        [pl.BlockSpec((None, SEQ, 3 * PAIR_W), lambda b, p, g: (b, 0, p * N_GROUPS + g))],
        [jax.ShapeDtypeStruct((LOCAL_BATCH, SEQ, QKV_W), BF16)],
        [slab] * 4, (qkv, qkv, qkv, *tables, dattn, attn, lse), ("parallel", "parallel", "arbitrary"))


def _discretize(lr, li, log_dt, br, bi):
    dt = jnp.exp(log_dt)
    mag = jnp.exp(lr * dt)
    ab_re, ab_im = mag * jnp.cos(li * dt), mag * jnp.sin(li * dt)
    den = lr * lr + li * li
    nr, ni = ab_re - 1.0, ab_im
    f_re = (nr * lr + ni * li) / den
    f_im = (ni * lr - nr * li) / den
    return ab_re, ab_im, f_re[None] * br - f_im[None] * bi, f_re[None] * bi + f_im[None] * br


def ssm_prep(lr, li, log_dt, br, bi):
    def body(lr_ref, li_ref, dt_ref, br_ref, bi_ref, *outs):
        for o, v in zip(outs, _discretize(lr_ref[...], li_ref[...], dt_ref[...], br_ref[...], bi_ref[...])):
            o[...] = v
    shapes = [lr, li, br, bi]
    return pl.pallas_call(body, name="ssm_prep",
                          out_shape=[jax.ShapeDtypeStruct(s.shape, F32) for s in shapes])(lr, li, log_dt, br, bi)


def ssm_prep_bwd(lr, li, log_dt, br, bi, g_ab_re, g_ab_im, g_bb_re, g_bb_im):
    def body(lr_ref, li_ref, dt_ref, br_ref, bi_ref, g0, g1, g2, g3, *outs):
        _, vjp = jax.vjp(_discretize, lr_ref[...], li_ref[...], dt_ref[...], br_ref[...], bi_ref[...])
        for o, v in zip(outs, vjp((g0[...], g1[...], g2[...], g3[...]))):
            o[...] = v
    shapes = [lr, li, log_dt, br, bi]
    return pl.pallas_call(body, name="ssm_prep_bwd",
                          out_shape=[jax.ShapeDtypeStruct(s.shape, F32) for s in shapes])(
        lr, li, log_dt, br, bi, g_ab_re, g_ab_im, g_bb_re, g_bb_im)


def _block_diag(t):
    per = SSM_STATE_W // SSM_LANE_BLOCKS // 64
    g = t.transpose(1, 0, 2).reshape(SSM_LANE_BLOCKS, per, 16, 64)
    eye = jnp.eye(per, dtype=t.dtype)
    return jnp.einsum("jgcn,gh->jgchn", g, eye).reshape(SSM_LANE_BLOCKS, per * 16, per * 64)


def _block_diag_t(m):
    per = SSM_STATE_W // SSM_LANE_BLOCKS // 64
    m5 = m.reshape(SSM_LANE_BLOCKS, per, 16, per, 64)
    d = jnp.einsum("jgchn,gh->jgcn", m5, jnp.eye(per, dtype=m.dtype))
    return d.reshape(SSM_LANE_BLOCKS * per, 16, 64).transpose(1, 0, 2)


def _cmul(ar, ai, br, bi):
    return ar * br - ai * bi, ar * bi + ai * br


def _power_tables(ar, ai, reverse):
    width = ar.shape[1]
    row = lax.broadcasted_iota(jnp.int32, (8, width), 0)
    pows = [(ar, ai)]
    for _ in range(7):
        pows.append(_cmul(pows[-1][0], pows[-1][1], ar, ai))
    steps = []
    for k in (1, 2, 4):
        keep = (row >= k) if not reverse else (row < 8 - k)
        steps.append((jnp.where(keep, pows[k - 1][0], 0.0), jnp.where(keep, pows[k - 1][1], 0.0)))
    cr = jnp.zeros((8, width), F32)
    ci = jnp.zeros((8, width), F32)
    for i in range(8):
        pr, pi = pows[i] if not reverse else pows[7 - i]
        cr = jnp.where(row == i, pr, cr)
        ci = jnp.where(row == i, pi, ci)
    return steps, (cr, ci)


SCAN_CHUNK = 2048
STATE_BLOCK = SSM_STATE_W // SSM_LANE_BLOCKS
CHAN_BLOCK = SSM_W // SSM_LANE_BLOCKS


def ssm_fwd(u, ab_re, ab_im, bb_re, bb_im, cb_re, cb_im, d_skip, comm=None):
    nt = SEQ // SCAN_CHUNK
    chan = pl.BlockSpec((None, SCAN_CHUNK, CHAN_BLOCK), lambda b, j, t: (b, t, j))
    state = pl.BlockSpec((None, SCAN_CHUNK, STATE_BLOCK), lambda b, j, t: (b, t, j))
    mat = pl.BlockSpec((None, CHAN_BLOCK, STATE_BLOCK), lambda b, j, t: (j, 0, 0))
    lane = pl.BlockSpec((1, STATE_BLOCK), lambda b, j, t: (0, j))
    dsp = pl.BlockSpec((1, CHAN_BLOCK), lambda b, j, t: (0, j))

    def body(u_ref, ar_ref, ai_ref, bbr_ref, bbi_ref, cbr_ref, cbi_ref, d_ref, y_ref, yg_ref, xr_ref, xi_ref,
             car_r, car_i):
        @pl.when(pl.program_id(2) == 0)
        def _():
            car_r[...] = jnp.zeros_like(car_r)
            car_i[...] = jnp.zeros_like(car_i)

        steps, (pr, pi) = _power_tables(ar_ref[...], ai_ref[...], reverse=False)
        uf = u_ref[...]
        ub = uf.astype(BF16)
        xr_ref[...] = _dot(ub, bbr_ref[...], "nn")
        xi_ref[...] = _dot(ub, bbi_ref[...], "nn")

        def tile(i, carry):
            cr, ci = carry
            sl = pl.ds(pl.multiple_of(i * 8, 8), 8)
            br, bi = xr_ref[sl, :], xi_ref[sl, :]
            for k, (sr, si) in zip((1, 2, 4), steps):
                tr, ti = _cmul(sr, si, pltpu.roll(br, k, 0), pltpu.roll(bi, k, 0))
                br, bi = br + tr, bi + ti
            tr, ti = _cmul(pr, pi, cr, ci)
            br, bi = br + tr, bi + ti
            xr_ref[sl, :] = br
            xi_ref[sl, :] = bi
            return br[7:8, :], bi[7:8, :]

        cr, ci = lax.fori_loop(0, SCAN_CHUNK // 8, tile, (car_r[0:1, :], car_i[0:1, :]), unroll=4)
        car_r[0:1, :] = cr
        car_i[0:1, :] = ci
        y = (_dot(xr_ref[...].astype(BF16), cbr_ref[...], "nt") - _dot(xi_ref[...].astype(BF16), cbi_ref[...], "nt")
             + d_ref[...] * uf)
        y_ref[...] = y
        yg_ref[...] = jax.nn.gelu(y).astype(BF16)

    return hosted_call(
        body, comm, "ssm_fwd", (LOCAL_BATCH, SSM_LANE_BLOCKS, nt),
        [chan, lane, lane, mat, mat, mat, mat, dsp], [chan, chan, state, state],
        [jax.ShapeDtypeStruct((LOCAL_BATCH, SEQ, SSM_W), F32), jax.ShapeDtypeStruct((LOCAL_BATCH, SEQ, SSM_W), BF16),
         jax.ShapeDtypeStruct((LOCAL_BATCH, SEQ, SSM_STATE_W), F32),
         jax.ShapeDtypeStruct((LOCAL_BATCH, SEQ, SSM_STATE_W), F32)],
        [pltpu.VMEM((8, STATE_BLOCK), F32), pltpu.VMEM((8, STATE_BLOCK), F32)],
        (u, ab_re, ab_im, bb_re, bb_im, cb_re, cb_im, d_skip), ("parallel", "parallel", "arbitrary"))


def ssm_bwd(dyg, y, u, xr, xi, ab_re, ab_im, bb_re, bb_im, cb_re, cb_im, d_skip, comm=None):
    nt = SEQ // SCAN_CHUNK
    ntile = SCAN_CHUNK // 8

    def rev(t):
        return nt - 1 - t

    chan = pl.BlockSpec((None, SCAN_CHUNK, CHAN_BLOCK), lambda j, b, t: (b, rev(t), j))
    state = pl.BlockSpec((None, SCAN_CHUNK, STATE_BLOCK), lambda j, b, t: (b, rev(t), j))
    before = pl.BlockSpec((None, 8, STATE_BLOCK), lambda j, b, t: (b, jnp.maximum(rev(t) * ntile - 1, 0), j))
    mat = pl.BlockSpec((None, CHAN_BLOCK, STATE_BLOCK), lambda j, b, t: (j, 0, 0))
    lane = pl.BlockSpec((1, STATE_BLOCK), lambda j, b, t: (0, j))
    lane8 = pl.BlockSpec((8, STATE_BLOCK), lambda j, b, t: (0, j))
    dsp = pl.BlockSpec((1, CHAN_BLOCK), lambda j, b, t: (0, j))

    def body(dyg_ref, y_ref, u_ref, xr_ref, xi_ref, xrb_ref, xib_ref, ar_ref, ai_ref, bbr_ref, bbi_ref, cbr_ref,
             cbi_ref, d_ref, du_ref, dcbr_ref, dcbi_ref, dbbr_ref, dbbi_ref, dd_ref, dar_ref, dai_ref,
             lam_r, lam_i, car_r, car_i):
        b, t = pl.program_id(1), pl.program_id(2)
        first = jnp.logical_and(b == 0, t == 0)

        @pl.when(t == 0)
        def _():
            car_r[...] = jnp.zeros_like(car_r)
            car_i[...] = jnp.zeros_like(car_i)

        @pl.when(first)
        def _():
            for r in (dcbr_ref, dcbi_ref, dbbr_ref, dbbi_ref, dd_ref, dar_ref, dai_ref):
                r[...] = jnp.zeros_like(r)

        steps, (pr, pi) = _power_tables(ar_ref[...], -ai_ref[...], reverse=True)
        uf = u_ref[...]
        _, gelu_vjp = jax.vjp(jax.nn.gelu, y_ref[...])
        dy = gelu_vjp(dyg_ref[...])[0]
        dyb = dy.astype(BF16)
        dd_ref[...] += _colsum(dy * uf)
        lam_r[...] = _dot(dyb, cbr_ref[...], "nn")
        lam_i[...] = -_dot(dyb, cbi_ref[...], "nn")
        dcbr_ref[...] += _dot(dyb, xr_ref[...].astype(BF16), "tn")
        dcbi_ref[...] -= _dot(dyb, xi_ref[...].astype(BF16), "tn")
        row0 = lax.broadcasted_iota(jnp.int32, (8, STATE_BLOCK), 0) == 0
        has_before = rev(t) > 0
        xrb = jnp.where(has_before, xrb_ref[...], 0.0)
        xib = jnp.where(has_before, xib_ref[...], 0.0)

        def tile(s, carry):
            cr, ci, acc_r, acc_i = carry
            i = ntile - 1 - s
            sl = pl.ds(pl.multiple_of(i * 8, 8), 8)
            gr, gi = lam_r[sl, :], lam_i[sl, :]
            for k, (sr, si) in zip((1, 2, 4), steps):
                tr, ti = _cmul(sr, si, pltpu.roll(gr, 8 - k, 0), pltpu.roll(gi, 8 - k, 0))
                gr, gi = gr + tr, gi + ti
            tr, ti = _cmul(pr, pi, cr, ci)
            gr, gi = gr + tr, gi + ti
            lam_r[sl, :] = gr
            lam_i[sl, :] = gi
            sp = pl.ds(pl.multiple_of(jnp.maximum(i - 1, 0) * 8, 8), 8)
            pvr = jnp.where(i > 0, xr_ref[sp, :], xrb)
            pvi = jnp.where(i > 0, xi_ref[sp, :], xib)
            xsr = jnp.where(row0, pltpu.roll(pvr, 1, 0), pltpu.roll(xr_ref[sl, :], 1, 0))
            xsi = jnp.where(row0, pltpu.roll(pvi, 1, 0), pltpu.roll(xi_ref[sl, :], 1, 0))
            acc_r = acc_r + xsr * gr + xsi * gi
            acc_i = acc_i + xsr * gi - xsi * gr
            return gr[0:1, :], gi[0:1, :], acc_r, acc_i

        zero = jnp.zeros((8, STATE_BLOCK), F32)
        cr, ci, acc_r, acc_i = lax.fori_loop(0, ntile, tile, (car_r[0:1, :], car_i[0:1, :], zero, zero), unroll=2)
        car_r[0:1, :] = cr
        car_i[0:1, :] = ci
        dar_ref[...] += acc_r
        dai_ref[...] += acc_i
        lrb, lib = lam_r[...].astype(BF16), lam_i[...].astype(BF16)
        du = _dot(lrb, bbr_ref[...], "nt") + _dot(lib, bbi_ref[...], "nt") + d_ref[...] * dy
        du_ref[...] = du.astype(BF16)
        ub = uf.astype(BF16)
        dbbr_ref[...] += _dot(ub, lrb, "tn")
        dbbi_ref[...] += _dot(ub, lib, "tn")

    mat_shape = jax.ShapeDtypeStruct((SSM_LANE_BLOCKS, CHAN_BLOCK, STATE_BLOCK), F32)
    return hosted_call(
        body, comm, "ssm_bwd", (SSM_LANE_BLOCKS, LOCAL_BATCH, nt),
        [chan, chan, chan, state, state, before, before, lane, lane, mat, mat, mat, mat, dsp],
        [chan, mat, mat, mat, mat, dsp, lane8, lane8],
        [jax.ShapeDtypeStruct((LOCAL_BATCH, SEQ, SSM_W), BF16), mat_shape, mat_shape, mat_shape, mat_shape,
         jax.ShapeDtypeStruct((1, SSM_W), F32), jax.ShapeDtypeStruct((8, SSM_STATE_W), F32),
         jax.ShapeDtypeStruct((8, SSM_STATE_W), F32)],
        [pltpu.VMEM((SCAN_CHUNK, STATE_BLOCK), F32), pltpu.VMEM((SCAN_CHUNK, STATE_BLOCK), F32),
         pltpu.VMEM((8, STATE_BLOCK), F32), pltpu.VMEM((8, STATE_BLOCK), F32)],
        (dyg, y, u, xr, xi, xr, xi, ab_re, ab_im, bb_re, bb_im, cb_re, cb_im, d_skip),
        ("parallel", "arbitrary", "arbitrary"))


def _merge_fn(g0, g1, attn_d, za, zb):
    return jax.nn.sigmoid(g0) * attn_d + jax.nn.sigmoid(g1) * (za * jax.nn.sigmoid(zb))


def _swiglu_fn(a, b):
    return jax.nn.silu(a) * b


def _own_slot(slots, shard):
    me = 2 * lax.axis_index("x") + lax.axis_index("y")
    mine = lax.broadcasted_iota(jnp.int32, (N_CHIPS, 1, 1), 0) == me
    return jnp.where(mine, shard[None], slots)


def _reduce_start(names, gw, shard_shapes):
    return swap_comm([_to_slots(n, gw[n], shard_shapes[n]) for n in names])


def _reduce_chip(names, slots, got, core):
    return exchange_comm([add_halves(n, g, r, core) for n, g, r in zip(names, slots, got)])


def local_step(x, target, shards, small, core):
    g_mix, g_ffn, g_final = small["norm_mix_g"], small["norm_ffn_g"], small["norm_final_g"]
    tables = _rope_tables()
    seqs = lambda t: t.reshape(LOCAL_BATCH, SEQ, t.shape[-1])
    toks = lambda t: t.reshape(TOKENS, t.shape[-1])
    shard_shapes = {n: s.shape for n, s in shards.items()}
    w = {}

    def gather(names):
        return gather_comm([shards[n] for n in names])

    def arrived(names, slots, own=None):
        for n, s in zip(names, slots):
            w[n] = _from_slots(n, s if own is None else _own_slot(s, own))

    later = [n for n in BIG if n != "w_in"]
    sems, w_in_shard, land, token = split_start(shards["w_in"].astype(BF16), "w_in_gather_start")
    zero = token[0, 0]
    h, *rest = first_norm(x, g_mix + zero, [shards[n] for n in later])
    shards = dict(shards)
    shards.update(zip(later, rest))
    br_t = small["ssm_b_re"].transpose(2, 0, 1)
    bi_t = small["ssm_b_im"].transpose(2, 0, 1)
    log_dt = small["ssm_log_dt"].reshape(32, 1)
    ab_re, ab_im, bb_re_t, bb_im_t = ssm_prep(small["ssm_a_re"] + zero, small["ssm_a_im"], log_dt, br_t, bi_t)
    ab = [ab_re.reshape(1, SSM_STATE_W), ab_im.reshape(1, SSM_STATE_W)]
    bb = [_block_diag(bb_re_t).astype(BF16), _block_diag(bb_im_t).astype(BF16)]
    cb = [_block_diag((small["ssm_c_re"] + zero).transpose(1, 0, 2)).astype(BF16),
          _block_diag((small["ssm_c_im"] + zero).transpose(1, 0, 2)).astype(BF16)]
    d_skip = small["ssm_d"].reshape(1, SSM_W)
    w_in_shard, land = split_wait(sems, w_in_shard, land, [h] + bb + cb, "w_in_gather_wait")
    arrived(["w_in"], [handover(land, "w_in_handover")], own=w_in_shard)
    w_qkv, w_u, w_gate = _qkv_order(w["w_in"][:QKV_W]), w["w_in"][QKV_W:QKV_W + SSM_W], w["w_in"][QKV_W + SSM_W:]
    qkv, *slots = proj_qkv(h, w_qkv, tables, comm=gather(["w_attn_out", "w_glu"]))
    arrived(["w_attn_out", "w_glu"], slots)
    qkv = seqs(qkv)
    u = seqs(matmul(h, w_u, "nt", F32, "proj_u"))
    gl, *slots = matmul(h, w_gate, "nt", BF16, "proj_gate", comm=gather(["w_out"]))
    arrived(["w_out"], slots)
    attn_b, attn, lse, *slots = attn_fwd(qkv, comm=gather(["w_ffn_gate"]))
    arrived(["w_ffn_gate"], slots)
    attn_b = toks(attn_b)
    y, yg, xr, xi, *slots = ssm_fwd(u, *ab, *bb, *cb, d_skip, comm=gather(["w_ffn_up"]))
    arrived(["w_ffn_up"], slots)
    yg2 = toks(yg)
    merged, x1, h2 = merge_out_proj(gl, attn_b, w["w_attn_out"], yg2, w["w_glu"], w["w_out"], x, g_ffn)
    a, b, act, *slots = ffn_in(h2, w["w_ffn_gate"], w["w_ffn_up"], comm=gather(["w_ffn_down"]))
    arrived(["w_ffn_down"], slots)

    def final_fn(xv, g, tgt):
        yv, vjp = jax.vjp(_rms, xv, g)
        err = yv - tgt
        dx, dg = vjp(err * (1.0 / D_MODEL))
        loss = 0.5 * jnp.sum(jnp.mean(err * err, axis=-1, keepdims=True), axis=0, keepdims=True)
        return dx, dx, dg, jnp.broadcast_to(loss, (1, LANES))

    dx2, dx2_b, dg_final, loss = matmul_rows(act, w["w_ffn_down"], "ffn_down_loss", final_fn, [g_final, target],
                                             [(D_MODEL, F32), (D_MODEL, BF16)], accs=(D_MODEL, LANES), add=x1)
    gw, parts = {}, {}
    gw["w_ffn_down"] = matmul(act, dx2_b, "tn", F32, "d_ffn_down")
    da_b, db_b = ffn_in_bwd(dx2_b, w["w_ffn_down"], a, b)
    gw["w_ffn_gate"] = matmul(da_b, h2, "tn", F32, "d_ffn_gate")
    gw["w_ffn_up"] = matmul(db_b, h2, "tn", F32, "d_ffn_up")
    ffn = ["w_ffn_down", "w_ffn_gate", "w_ffn_up"]
    swap = _reduce_start(ffn, gw, shard_shapes)

    def norm_bwd(dh, xv, g, skip):
        _, vjp = jax.vjp(_rms, xv, g)
        dx, dg = vjp(dh)
        dx = dx + skip
        return dx, dx, dg

    dx1, dx1_b, dg_ffn, *got = matmul_rows(
        [da_b, db_b], [w["w_ffn_gate"], w["w_ffn_up"]], "d_h2_norm", norm_bwd, [x1, g_ffn, dx2],
        [(D_MODEL, F32), (D_MODEL, BF16)], accs=(D_MODEL,), comm=swap, tm=256)
    ffn_exchange = [_reduce_chip(ffn[:2], swap.ins[:2], got[:2], core)]
    ffn_up_exchange = _reduce_chip(ffn[2:], swap.ins[2:], got[2:], core)
    gw["w_out"] = matmul(merged, dx1_b, "tn", F32, "d_out")
    dgl_b, dattn_d_b, dz_b, dattn, dyg, parts["w_ffn_up"] = merge_bwd(
        dx1_b, w["w_out"], gl, attn_b, w["w_attn_out"], yg2, w["w_glu"], comm=ffn_up_exchange)
    dattn, dyg = seqs(dattn), seqs(dyg)
    gw["w_attn_out"] = matmul(attn_b, dattn_d_b, "tn", F32, "d_attn_out", col_slots=N_CHIPS)
    gw["w_glu"] = matmul(yg2, dz_b, "tn", F32, "d_glu", col_slots=N_CHIPS)
    mixer = ["w_out", "w_attn_out", "w_glu"]
    swap = _reduce_start(mixer, gw, shard_shapes)
    du_b, dcb_re, dcb_im, dbb_re, dbb_im, dd, da_re8, da_im8, *rest = ssm_bwd(
        dyg, y, u, xr, xi, *ab, *bb, *cb, d_skip, comm=join_comms(ffn_exchange + [swap]))
    for n, p in zip(ffn[:2], rest[:2]):
        parts[n] = p
    mixer_exchange = _reduce_chip(mixer, swap.ins, rest[2:], core)
    du_b = toks(du_b)
    g_ab_re = jnp.sum(da_re8, axis=0).reshape(32, 64)
    g_ab_im = jnp.sum(da_im8, axis=0).reshape(32, 64)
    d_lr, d_li, d_ldt, d_br_t, d_bi_t = ssm_prep_bwd(
        small["ssm_a_re"], small["ssm_a_im"], log_dt, br_t, bi_t,
        g_ab_re, g_ab_im, _block_diag_t(dbb_re), _block_diag_t(dbb_im))
    as_gcn = lambda t: t.transpose(1, 0, 2).reshape(SSM_W, 64)
    gs = {
        "ssm_a_re": d_lr, "ssm_a_im": d_li, "ssm_log_dt": d_ldt.reshape(1, 32),
        "ssm_b_re": as_gcn(d_br_t), "ssm_b_im": as_gcn(d_bi_t),
        "ssm_c_re": as_gcn(_block_diag_t(dcb_re)), "ssm_c_im": as_gcn(_block_diag_t(dcb_im)),
        "ssm_d": dd.reshape(32, 16).T,
    }
    ssm_gather = small_comm([gs[n] for n in SSM_SMALL])
    dqkv_b, *rest = attn_bwd(qkv, tables, dattn, attn, lse, comm=mixer_exchange)
    for n, p in zip(mixer, rest):
        parts[n] = p
    dqkv_b = toks(dqkv_b)
    d_qkv = matmul(dqkv_b, h, "tn", F32, "d_w_qkv")
    d_u = matmul(du_b, h, "tn", F32, "d_w_u")
    d_gate = matmul(dgl_b, h, "tn", F32, "d_w_gate")
    gw["w_in"] = jnp.concatenate([_qkv_order(d_qkv, back=True), d_u, d_gate], axis=0)
    swap = _reduce_start(["w_in"], gw, shard_shapes)
    dh, *got = matmul(dqkv_b, w_qkv, "nn", F32, "d_h_qkv", comm=join_comms([swap, ssm_gather]))
    ssm_shares = got[1:]
    chip_sum = add_halves("w_in", swap.ins[0], got[0], core)
    sems, chip_sum, land, token = split_start(chip_sum, "w_in_reduce_start", per_chip=True)
    grad_x, dg_mix = mix_in_bwd([du_b, dgl_b], [w_u, w_gate], dh, x, g_mix + token[0, 0], dx1)
    gs_norm = {"norm_mix_g": dg_mix, "norm_ffn_g": dg_ffn, "norm_final_g": dg_final}
    return loss, grad_x, parts, ssm_shares, gs_norm, (sems, chip_sum, land)


ANY = pl.BlockSpec(memory_space=pl.ANY)
BIG = ("w_in", "w_glu", "w_attn_out", "w_out", "w_ffn_gate", "w_ffn_up", "w_ffn_down")
TRANSPOSED = ("w_in", "w_ffn_gate", "w_ffn_up")
ROW_SHARDED = TRANSPOSED + ("w_out", "w_ffn_down")
SMALL = ("norm_mix_g", "ssm_a_re", "ssm_a_im", "ssm_log_dt", "ssm_b_re", "ssm_b_im", "ssm_c_re", "ssm_c_im",
         "ssm_d", "norm_ffn_g", "norm_final_g")
WEIGHTS = ("norm_mix_g", "w_in", "ssm_a_re", "ssm_a_im", "ssm_log_dt", "ssm_b_re", "ssm_b_im", "ssm_c_re",
           "ssm_c_im", "ssm_d", "w_glu", "w_attn_out", "w_out", "norm_ffn_g", "w_ffn_gate", "w_ffn_up",
           "w_ffn_down", "norm_final_g")
SSM_SMALL = SMALL[1:9]
NORM_SMALL = (SMALL[0],) + SMALL[9:]
NORM_ROWS = 32
N_BIG = len(BIG)


def _position():
    return lax.axis_index("x"), lax.axis_index("y"), lax.axis_index("c")


def _other_chips(x, y):
    return [(1 - x, y), (x, 1 - y), (1 - x, 1 - y)]


def _remote(src, dst, send_sem, recv_sem, device):
    return pltpu.make_async_remote_copy(src_ref=src, dst_ref=dst, send_sem=send_sem, recv_sem=recv_sem,
                                        device_id=device, device_id_type=MESH)


_later = functools.partial


def _two_level_phases(copies):
    def first(*refs):
        locals_, sends, _, _, _ = copies(*refs)
        for cp in locals_ + sends:
            cp().start()

    def mid(*refs):
        _, _, arrived, passed, _ = copies(*refs)
        for got, cp in zip(arrived, passed):
            got().wait_recv()
            cp().start()

    def last(*refs):
        locals_, sends, _, passed, from_sibling = copies(*refs)
        for cp in from_sibling:
            cp().wait_recv()
        for cp in sends + passed:
            cp().wait_send()
        for cp in locals_:
            cp().wait()

    return first, mid, last


def _half(ref, chip, which):
    rows = ref.shape[1] // 2
    return ref.at[chip, pl.ds(which * rows, rows), :]


class Comm:
    def __init__(self, ins, out_shapes, sems, first, mid, last):
        self.ins, self.out_shapes, self.sems = list(ins), list(out_shapes), list(sems)
        self.first, self.mid, self.last = first, mid, last


def join_comms(comms):
    def cut(refs_by_kind):
        offs, parts = [0, 0, 0], []
        for cm in comms:
            sizes = (len(cm.ins), len(cm.out_shapes), len(cm.sems))
            parts.append(tuple(refs_by_kind[k][offs[k]:offs[k] + sizes[k]] for k in range(3)))
            offs = [o + s for o, s in zip(offs, sizes)]
        return parts

    def phase(which):
        def run(ins, outs, sems):
            for cm, part in zip(comms, cut((ins, outs, sems))):
                fn = getattr(cm, which)
                if fn is not None:
                    fn(*part)
        return run

    return Comm(sum((cm.ins for cm in comms), []), sum((cm.out_shapes for cm in comms), []),
                sum((cm.sems for cm in comms), []), phase("first"), phase("mid"), phase("last"))


def _comm_operands(comm):
    if comm is None:
        return [], [], []
    return comm.ins, comm.out_shapes, comm.sems


def _comm_begin(comm, refs, step, n_steps):
    if comm is None:
        return
    pl.when(step == 0)(lambda: comm.first(*refs))
    if comm.mid is not None:
        pl.when(step == (n_steps * 3) // 4)(lambda: comm.mid(*refs))


def _comm_end(comm, refs, step, n_steps):
    if comm is not None:
        pl.when(step == n_steps - 1)(lambda: comm.last(*refs))


def _comm_refs(comm, refs, n_in, n_out):
    if comm is None:
        return list(refs), None
    ci, co, cs = len(comm.ins), len(comm.out_shapes), len(comm.sems)
    o0 = n_in + ci
    s0 = o0 + n_out + co
    host = list(refs[:n_in]) + list(refs[o0:o0 + n_out]) + list(refs[s0:len(refs) - cs])
    return host, (list(refs[n_in:o0]), list(refs[o0 + n_out:s0]), list(refs[len(refs) - cs:]))


def run_comm(comm, name):
    n_in, n_out = len(comm.ins), len(comm.out_shapes)

    def body(*refs):
        parts = (list(refs[:n_in]), list(refs[n_in:n_in + n_out]), list(refs[n_in + n_out:]))
        comm.first(*parts)
        if comm.mid is not None:
            comm.mid(*parts)
        comm.last(*parts)

    return pl.pallas_call(body, name=name, in_specs=[ANY] * n_in, out_specs=[ANY] * n_out,
                          out_shape=comm.out_shapes, scratch_shapes=comm.sems)(*comm.ins)


def hosted_call(work, comm, name, grid, in_specs, out_specs, out_shape, scratch_shapes, args, semantics):
    c_ins, c_outs, c_sems = _comm_operands(comm)
    n_steps = math.prod(grid)

    def body(*refs):
        host, c_refs = _comm_refs(comm, refs, len(in_specs), len(out_specs))
        step = 0
        for axis, size in enumerate(grid):
            step = step * size + pl.program_id(axis)
        _comm_begin(comm, c_refs, step, n_steps)
        work(*host)
        _comm_end(comm, c_refs, step, n_steps)

    return pl.pallas_call(
        body, name=name, grid=grid, in_specs=list(in_specs) + [ANY] * len(c_ins),
        out_specs=list(out_specs) + [ANY] * len(c_outs), out_shape=list(out_shape) + c_outs,
        scratch_shapes=list(scratch_shapes) + c_sems,
        compiler_params=_params(semantics if comm is None else ("arbitrary",) * len(grid)),
    )(*args, *c_ins)


def gather_comm(shards):
    n = len(shards)

    def copies(srcs, outs, sems):
        send_sems, recv_sems, local_sems = sems
        x, y, c = _position()
        me = 2 * x + y
        sibling = (x, y, 1 - c)
        chips = _other_chips(x, y)
        locals_ = [_later(pltpu.make_async_copy, s, o.at[me], local_sems.at[i])
                   for i, (s, o) in enumerate(zip(srcs, outs))]
        sends, arrived, passed, from_sibling = [], [], [], []
        for j, (px, py) in enumerate(chips):
            for i, (s, o) in enumerate(zip(srcs, outs)):
                rows = s.shape[0] // 2
                sends.append(_later(_remote, s.at[pl.ds(c * rows, rows), :], _half(o, me, c), send_sems.at[i, j],
                                    recv_sems.at[i, j], (px, py, c)))
                got = _half(o, 2 * px + py, c)
                arrived.append(_later(_remote, got, got, send_sems.at[i, j], recv_sems.at[i, j], (px, py, c)))
                passed.append(_later(_remote, got, got, send_sems.at[i, 3 + j], recv_sems.at[i, 3 + j], sibling))
                other = _half(o, 2 * px + py, 1 - c)
                from_sibling.append(_later(_remote, other, other, send_sems.at[i, 3 + j], recv_sems.at[i, 3 + j],
                                           sibling))
        return locals_, sends, arrived, passed, from_sibling

    return Comm(shards, [jax.ShapeDtypeStruct((N_CHIPS,) + s.shape, s.dtype) for s in shards],
                [pltpu.SemaphoreType.DMA((n, 6)), pltpu.SemaphoreType.DMA((n, 6)), pltpu.SemaphoreType.DMA((n,))],
                *_two_level_phases(copies))


HBM = pl.BlockSpec(memory_space=pltpu.HBM)
SEM = pl.BlockSpec(memory_space=pltpu.SEMAPHORE)
N_OTHER = N_CHIPS - 1


def _ici_halves(src_ref, land_ref, sems, per_chip):
    x, y, c = _position()
    me = 2 * x + y
    rows = land_ref.shape[1] // 2
    sends, arrivals = [], []
    for j, (px, py) in enumerate(_other_chips(x, y)):
        piece = src_ref.at[2 * px + py] if per_chip else src_ref.at[pl.ds(c * rows, rows), :]
        sends.append(_later(_remote, piece, _half(land_ref, me, c), sems[j], sems[N_OTHER + j], (px, py, c)))
        got = _half(land_ref, 2 * px + py, c)
        arrivals.append(_later(_remote, got, got, sems[j], sems[N_OTHER + j], (px, py, c)))
    return sends, arrivals


def split_start(src, name, per_chip=False):
    def body(src_ref, land_ref, *rest):
        sems, token = rest[:2 * N_OTHER], rest[-1]
        for cp in _ici_halves(src_ref, land_ref, sems, per_chip)[0]:
            cp().start()
        token[...] = jnp.zeros_like(token)

    rows, cols = (2 * src.shape[1], src.shape[2]) if per_chip else src.shape
    sem = pltpu.SemaphoreType.DMA(())
    land = (N_CHIPS, rows, cols)
    res = pl.pallas_call(
        body, name=name, in_specs=(HBM, HBM),
        out_specs=(SEM,) * (2 * N_OTHER) + (HBM, HBM, pl.BlockSpec(memory_space=pltpu.VMEM)),
        out_shape=(sem,) * (2 * N_OTHER) + (pltpu.HBM(src.shape, src.dtype), pltpu.HBM(land, src.dtype),
                                           jax.ShapeDtypeStruct((8, LANES), F32)),
        input_output_aliases={0: 2 * N_OTHER, 1: 2 * N_OTHER + 1},
        compiler_params=pltpu.CompilerParams(has_side_effects=pltpu.SideEffectType.DATAFLOW_SIDE_EFFECTING),
    )(pltpu.with_memory_space_constraint(src, pltpu.HBM),
      pltpu.with_memory_space_constraint(lax.empty(land, src.dtype), pltpu.HBM))
    return res[:2 * N_OTHER], res[2 * N_OTHER], res[2 * N_OTHER + 1], res[-1]


def split_wait(sems, src, land, after, name, per_chip=False):
    def body(src_ref, land_ref, *rest):
        sends, arrivals = _ici_halves(src_ref, land_ref, rest[:2 * N_OTHER], per_chip)
        for cp in sends:
            cp().wait_send()
        for cp in arrivals:
            cp().wait_recv()

    return pl.pallas_call(
        body, name=name, in_specs=(HBM, HBM) + (SEM,) * (2 * N_OTHER) + (ANY,) * len(after),
        out_specs=(HBM, HBM), out_shape=(pltpu.HBM(src.shape, src.dtype), pltpu.HBM(land.shape, land.dtype)),
        input_output_aliases={0: 0, 1: 1},
        compiler_params=pltpu.CompilerParams(has_side_effects=pltpu.SideEffectType.DATAFLOW_SIDE_EFFECTING),
    )(src, land, *sems, *after)


def handover(land, name, sums=None):
    n = N_OTHER + (sums is not None)

    def body(*refs):
        land_ref, send_sems, recv_sems = refs[0], refs[-2], refs[-1]
        x, y, c = _position()
        me = 2 * x + y
        sibling = (x, y, 1 - c)
        pieces = [(_half(land_ref, 2 * px + py, c), 2 * px + py) for px, py in _other_chips(x, y)]
        if sums is not None:
            pieces.append((refs[1].at[me], me))
        sends = [_remote(piece, _half(land_ref, chip, c), send_sems.at[j], recv_sems.at[j], sibling)
                 for j, (piece, chip) in enumerate(pieces)]
        for cp in sends:
            cp.start()
        for j, (_, chip) in enumerate(pieces):
            other = _half(land_ref, chip, 1 - c)
            _remote(other, other, send_sems.at[j], recv_sems.at[j], sibling).wait_recv()
        for cp in sends:
            cp.wait_send()

    args = (land,) + ((sums,) if sums is not None else ())
    return pl.pallas_call(
        body, name=name, in_specs=[ANY] * len(args), out_specs=ANY,
        out_shape=jax.ShapeDtypeStruct(land.shape, land.dtype), input_output_aliases={0: 0},
        scratch_shapes=[pltpu.SemaphoreType.DMA((n,)), pltpu.SemaphoreType.DMA((n,))],
    )(*args)


def swap_comm(grads):
    n = len(grads)

    def copies(srcs, gots, sems):
        send_sems, recv_sems = sems
        x, y, c = _position()
        out = []
        for i, (s, o) in enumerate(zip(srcs, gots)):
            rows = s.shape[1] // 2
            out.append(_remote(s.at[:, pl.ds((1 - c) * rows, rows), :], o, send_sems.at[i], recv_sems.at[i],
                               (x, y, 1 - c)))
        return out

    def first(srcs, gots, sems):
        for cp in copies(srcs, gots, sems):
            cp.start()

    def last(srcs, gots, sems):
        for cp in copies(srcs, gots, sems):
            cp.wait()

    return Comm(grads, [jax.ShapeDtypeStruct((N_CHIPS, g.shape[1] // 2, g.shape[2]), g.dtype) for g in grads],
                [pltpu.SemaphoreType.DMA((n,)), pltpu.SemaphoreType.DMA((n,))], first, None, last)


def add_halves(name, g, got, core):
    _, half, cols = got.shape
    mine = pl.BlockSpec((None, half, cols), lambda k, c_ref: (k, c_ref[0], 0))
    other = pl.BlockSpec((None, half, cols), lambda k, c_ref: (k, 0, 0))

    def body(c_ref, g_ref, got_ref, o_ref):
        o_ref[...] = (g_ref[...] + got_ref[...]).astype(BF16)

    return pl.pallas_call(
        body, name="add_halves_" + name,
        grid_spec=pltpu.PrefetchScalarGridSpec(num_scalar_prefetch=1, grid=(N_CHIPS,), in_specs=[mine, other],
                                               out_specs=other),
        out_shape=jax.ShapeDtypeStruct(got.shape, BF16),
        compiler_params=_params(("parallel",)),
    )(core, g, got)


def exchange_comm(parts):
    n = len(parts)

    def copies(srcs, outs, sems):
        send_sems, recv_sems, local_sems = sems
        x, y, c = _position()
        me = 2 * x + y
        sibling = (x, y, 1 - c)
        chips = _other_chips(x, y)
        locals_, sends, arrived, passed, from_sibling = [], [], [], [], []
        for i, (s, o) in enumerate(zip(srcs, outs)):
            locals_.append(_later(pltpu.make_async_copy, s.at[me], _half(o, me, c), local_sems.at[i]))
            sends.append(_later(_remote, s.at[me], _half(o, me, c), send_sems.at[i, 3], recv_sems.at[i, 3], sibling))
            other = _half(o, me, 1 - c)
            from_sibling.append(_later(_remote, other, other, send_sems.at[i, 3], recv_sems.at[i, 3], sibling))
        for j, (px, py) in enumerate(chips):
            for i, (s, o) in enumerate(zip(srcs, outs)):
                sends.append(_later(_remote, s.at[2 * px + py], _half(o, me, c), send_sems.at[i, j],
                                    recv_sems.at[i, j], (px, py, c)))
                got = _half(o, 2 * px + py, c)
                arrived.append(_later(_remote, got, got, send_sems.at[i, j], recv_sems.at[i, j], (px, py, c)))
                passed.append(_later(_remote, got, got, send_sems.at[i, 4 + j], recv_sems.at[i, 4 + j], sibling))
                other = _half(o, 2 * px + py, 1 - c)
                from_sibling.append(_later(_remote, other, other, send_sems.at[i, 4 + j], recv_sems.at[i, 4 + j],
                                           sibling))
        return locals_, sends, arrived, passed, from_sibling

    return Comm(parts, [jax.ShapeDtypeStruct((N_CHIPS, 2 * p.shape[1], p.shape[2]), p.dtype) for p in parts],
                [pltpu.SemaphoreType.DMA((n, 7)), pltpu.SemaphoreType.DMA((n, 7)), pltpu.SemaphoreType.DMA((n,))],
                *_two_level_phases(copies))


def small_comm(shares):
    n = len(shares)

    def copies(srcs, outs, sems):
        send_sems, recv_sems, local_sems = sems
        x, y, c = _position()
        me = 4 * x + 2 * y + c
        flips = [(fx, fy, fc) for fx in (0, 1) for fy in (0, 1) for fc in (0, 1)][1:]
        peers = [(1 - x if fx else x, 1 - y if fy else y, 1 - c if fc else c) for fx, fy, fc in flips]
        locals_, sends, arrived = [], [], []
        for i, (src_ref, out_ref) in enumerate(zip(srcs, outs)):
            locals_.append(_later(pltpu.make_async_copy, src_ref, out_ref.at[me], local_sems.at[i]))
            for j, (px, py, pc) in enumerate(peers):
                sends.append(_later(_remote, src_ref, out_ref.at[me], send_sems.at[i, j], recv_sems.at[i, j],
                                    (px, py, pc)))
                got = out_ref.at[4 * px + 2 * py + pc]
                arrived.append(_later(_remote, got, got, send_sems.at[i, j], recv_sems.at[i, j], (px, py, pc)))
        return locals_, sends, arrived

    def first(*refs):
        locals_, sends, _ = copies(*refs)
        for cp in locals_ + sends:
            cp().start()

    def last(*refs):
        locals_, sends, arrived = copies(*refs)
        for cp in arrived:
            cp().wait_recv()
        for cp in sends:
            cp().wait_send()
        for cp in locals_:
            cp().wait()

    return Comm(shares, [jax.ShapeDtypeStruct((N_DEV,) + s.shape, s.dtype) for s in shares],
                [pltpu.SemaphoreType.DMA((n, 7)), pltpu.SemaphoreType.DMA((n, 7)), pltpu.SemaphoreType.DMA((n,))],
                first, None, last)


def _adam_fn(w, g, m, v):
    m = ADAM_B1 * m + (1.0 - ADAM_B1) * g
    v = ADAM_B2 * v + (1.0 - ADAM_B2) * jnp.square(g)
    m_hat = m / (1.0 - ADAM_B1 ** ADAM_STEP)
    v_hat = v / (1.0 - ADAM_B2 ** ADAM_STEP)
    return -ADAM_LR * (m_hat / (jnp.sqrt(v_hat) + ADAM_EPS) + ADAM_WD * w), m, v


def adam_big(name, parts, w, m, v):
    rows, cols = w.shape
    tm = _pick(rows, 384, 16)

    def fn(p0, p1, p2, p3, wv, mv, vv):
        g = ((p0.astype(F32) + p1.astype(F32)) + p2.astype(F32)) + p3.astype(F32)
        return (g,) + _adam_fn(wv, g, mv, vv)

    return rowwise(fn, [parts, w, m, v], [(cols, F32)] * 4, "adam_" + name, tm=tm, rows=rows)


def adam_small(name, gathered, w, m, v):
    def body(g_ref, w_ref, m_ref, v_ref, go_ref, d_ref, mo_ref, vo_ref):
        g = g_ref[0]
        for k in range(1, N_DEV):
            g = g + g_ref[k]
        go_ref[...] = g
        d_ref[...], mo_ref[...], vo_ref[...] = _adam_fn(w_ref[...], g, m_ref[...], v_ref[...])

    return pl.pallas_call(body, name=name, out_shape=[jax.ShapeDtypeStruct(w.shape, F32)] * 4,
                          compiler_params=_params())(gathered, w, m, v)


def _ssm_2d(name, t):
    t = t[0] if t.ndim > 2 else t
    if name in ("ssm_b_re", "ssm_b_im"):
        return t.transpose(0, 2, 1).reshape(SSM_W, 64)
    if name in ("ssm_c_re", "ssm_c_im"):
        return t.reshape(SSM_W, 64)
    return t.T if name == "ssm_d" else t


def _ssm_back(name, t):
    if name in ("ssm_b_re", "ssm_b_im"):
        return t.reshape(32, 16, 64).transpose(0, 2, 1)[None]
    if name in ("ssm_c_re", "ssm_c_im"):
        return t.reshape(1, 32, 16, 64)
    if name == "ssm_d":
        return t.T[None]
    return t if name == "ssm_log_dt" else t[None]


def adam_ssm(shares, w, m, v):
    n = len(w)

    def body(*refs):
        ins, outs = refs[:4 * n], refs[4 * n:]
        for i in range(n):
            g_ref, w_ref, m_ref, v_ref = (ins[k * n + i] for k in range(4))
            g = g_ref[0]
            for k in range(1, N_DEV):
                g = g + g_ref[k]
            outs[4 * i][...] = g
            outs[4 * i + 1][...], outs[4 * i + 2][...], outs[4 * i + 3][...] = _adam_fn(w_ref[...], g, m_ref[...],
                                                                                      v_ref[...])

    out_shape = [jax.ShapeDtypeStruct(t.shape, F32) for t in w for _ in range(4)]
    res = pl.pallas_call(body, name="adam_ssm", out_shape=out_shape, compiler_params=_params())(*shares, *w, *m, *v)
    return [res[4 * i:4 * i + 4] for i in range(n)]


def _pack_small(names, vals, rows, last=None):
    flat = [vals[n].reshape(-1) for n in names]
    if last is not None:
        flat.append(last.reshape(-1))
    flat = jnp.concatenate(flat)
    return jnp.pad(flat, (0, rows * LANES - flat.shape[0])).reshape(rows, LANES)


def _unpack_small(names, pack, shapes):
    flat, out, off = pack.reshape(-1), {}, 0
    for n in names:
        size = math.prod(shapes[n])
        out[n] = flat[off:off + size].reshape(shapes[n])
        off += size
    return out, flat[off]


def _to_slots(name, g, shard_shape):
    rows, cols = shard_shape
    if g.shape == (N_CHIPS, rows, cols):
        return g
    if name in ROW_SHARDED:
        return g.reshape(N_CHIPS, rows, cols)
    return g.reshape(rows, N_CHIPS, cols).transpose(1, 0, 2)


def _from_slots(name, s):
    _, rows, cols = s.shape
    if name in ROW_SHARDED:
        return s.reshape(N_CHIPS * rows, cols)
    return s.transpose(1, 0, 2).reshape(rows, N_CHIPS * cols)


def kernel(x, norm_mix_g, w_in, ssm_a_re, ssm_a_im, ssm_log_dt, ssm_b_re, ssm_b_im, ssm_c_re, ssm_c_im, ssm_d, w_glu, w_attn_out, w_out, norm_ffn_g, w_ffn_gate, w_ffn_up, w_ffn_down, norm_final_g, loss_target, m_norm_mix_g, m_w_in, m_ssm_a_re, m_ssm_a_im, m_ssm_log_dt, m_ssm_b_re, m_ssm_b_im, m_ssm_c_re, m_ssm_c_im, m_ssm_d, m_w_glu, m_w_attn_out, m_w_out, m_norm_ffn_g, m_w_ffn_gate, m_w_ffn_up, m_w_ffn_down, m_norm_final_g, v_norm_mix_g, v_w_in, v_ssm_a_re, v_ssm_a_im, v_ssm_log_dt, v_ssm_b_re, v_ssm_b_im, v_ssm_c_re, v_ssm_c_im, v_ssm_d, v_w_glu, v_w_attn_out, v_w_out, v_norm_ffn_g, v_w_ffn_gate, v_w_ffn_up, v_w_ffn_down, v_norm_final_g):
    given = dict(locals())
    def local(name, prefix=""):
        t = given[prefix + name][0]
        return t.T if name in TRANSPOSED else t

    shard = {n: local(n) for n in BIG}
    shapes = {n: given[n].shape for n in WEIGHTS}

    small = {n: given[n] for n in SMALL}
    small_2d = dict(small)
    for n in ("ssm_a_re", "ssm_a_im", "ssm_b_re", "ssm_b_im", "ssm_c_re", "ssm_c_im", "ssm_d"):
        small_2d[n] = small[n][0]
    small_2d["norm_final_g"] = norm_final_g.reshape(1, D_MODEL)

    core = lax.axis_index("c").astype(jnp.int32).reshape(1)
    loss, grad_x, parts, ssm_shares, gs_norm, w_in_reduce = local_step(
        x.reshape(TOKENS, D_MODEL), loss_target.reshape(TOKENS, D_MODEL),
        {n: shard[n] for n in BIG}, small_2d, core)

    (norm_shares,) = run_comm(small_comm([_pack_small(NORM_SMALL, gs_norm, NORM_ROWS, last=loss)]),
                              "gather_norm_grads")
    small_out = [{} for _ in range(4)]
    packs = [_pack_small(NORM_SMALL, {n: given[p + n] for n in NORM_SMALL}, NORM_ROWS) for p in ("", "m_", "v_")]
    for kind, t in enumerate(adam_small("adam_norm_gains", norm_shares, *packs)):
        vals, after = _unpack_small(NORM_SMALL, t, shapes)
        small_out[kind].update(vals)
        if kind == 0:
            total_loss = after
    ssm_in = [[_ssm_2d(n, given[p + n]) for n in SSM_SMALL] for p in ("", "m_", "v_")]
    for n, res in zip(SSM_SMALL, adam_ssm(ssm_shares, *ssm_in)):
        for kind, t in enumerate(res):
            small_out[kind][n] = _ssm_back(n, t)

    big_out, updated = {}, {}
    for n in BIG[1:] + BIG[:1]:
        if n == "w_in":
            sems, chip_sum, land = w_in_reduce
            behind = [updated[k][1] for k in BIG[1:]] + [norm_shares]
            chip_sum, land = split_wait(sems, chip_sum, land, behind, "w_in_reduce_wait", per_chip=True)
            land = handover(land, "w_in_reduce_handover", sums=chip_sum)
            me = 2 * lax.axis_index("x") + lax.axis_index("y")
            parts[n] = lax.dynamic_update_slice(land, lax.dynamic_slice_in_dim(chip_sum, me, 1, 0),
                                                (me, lax.axis_index("c") * chip_sum.shape[1], 0))
        updated[n] = adam_big(n, parts[n], shard[n], local(n, "m_"), local(n, "v_"))
        big_out[n] = [(t.T if n in TRANSPOSED else t)[None] for t in updated[n]]

    outs = [total_loss, grad_x.reshape(LOCAL_BATCH, SEQ, D_MODEL)]
    for kind in range(4):
        for n in WEIGHTS:
            outs.append(big_out[n][kind] if n in BIG else small_out[kind][n])
    return tuple(outs)
```
